```python
import math
import jax, jax.numpy as jnp
from jax import lax
import numpy as np

D_MODEL = 1024
BATCH = 8
SEQ = 4096
DEPTH = 1

GMLP_WIDTH = D_MODEL
GMLP_GROUPS = 8
GMLP_GROUP_DIM = GMLP_WIDTH // GMLP_GROUPS
GMLP_CHUNK = 128
HGRN_HEADS = 8
HGRN_DK = 128
HGRN_DV = D_MODEL // HGRN_HEADS
HGRN_KEY_WIDTH = HGRN_HEADS * HGRN_DK
HGRN_VAL_WIDTH = HGRN_HEADS * HGRN_DV
HGRN_CHUNK = 64
HGRN_SCALE = HGRN_DK ** -0.5
N_BRANCHES = 2
D_FF = -(-(8 * D_MODEL) // (3 * 256)) * 256
IN_SIZES = (GMLP_WIDTH, GMLP_WIDTH, HGRN_KEY_WIDTH, HGRN_KEY_WIDTH,
            HGRN_VAL_WIDTH, HGRN_VAL_WIDTH, D_MODEL, D_MODEL)
IN_WIDTH = sum(IN_SIZES)
NORM_EPS = 1e-6

kernel_name = "gmlp_hgrn2_gated_hybrid_block"


def _split_points(sizes):
    pts, acc = [], 0
    for s in sizes[:-1]:
        acc += s
        pts.append(acc)
    return pts


def rms_norm(x, gain):
    xf = x.astype(jnp.float32)
    y = xf * lax.rsqrt(jnp.mean(xf * xf, axis=-1, keepdims=True) + NORM_EPS)
    return (y * gain.astype(jnp.float32)).astype(x.dtype)


def layer_norm(x, gain, bias):
    xf = x.astype(jnp.float32)
    mu = jnp.mean(xf, axis=-1, keepdims=True)
    var = jnp.mean(jnp.square(xf - mu), axis=-1, keepdims=True)
    y = (xf - mu) * lax.rsqrt(var + NORM_EPS)
    return (y * gain.astype(jnp.float32) + bias.astype(jnp.float32)).astype(x.dtype)


def gmlp_spatial_gating(u, v, ln_g, ln_b, w_s, b_s):
    B, S, _ = v.shape
    n_chunks = S // GMLP_CHUNK
    v = layer_norm(v, ln_g, ln_b)
    vc = v.reshape(B, n_chunks, GMLP_CHUNK, GMLP_GROUPS, GMLP_GROUP_DIM)
    causal = jnp.tril(jnp.ones((GMLP_CHUNK, GMLP_CHUNK), dtype=bool))
    w = jnp.where(causal, w_s, jnp.zeros((), w_s.dtype)).astype(v.dtype)
    mixed = jnp.einsum('gts,bnsgd->bntgd', w, vc) + b_s.T.astype(v.dtype)[:, :, None]
    return u * mixed.reshape(B, S, GMLP_WIDTH)


def hgrn2_chunkwise(q, f_logit, v, lb):
    B, S, _ = q.shape
    C = HGRN_CHUNK
    n_chunks = S // C
    f32 = jnp.float32
    lb = lb.astype(f32)
    f = lb + (1.0 - lb) * jax.nn.sigmoid(f_logit.astype(f32))
    k = 1.0 - f
    log_f = jnp.log(f)

    def to_chunks(t, d):
        return t.reshape(B, n_chunks, C, HGRN_HEADS, d).transpose(0, 3, 1, 2, 4)

    qc = to_chunks(q.astype(f32), HGRN_DK) * HGRN_SCALE
    kc = to_chunks(k, HGRN_DK)
    vc = to_chunks(v.astype(f32), HGRN_DV)
    A = jnp.cumsum(to_chunks(log_f, HGRN_DK), axis=3)
    A_mid = A[:, :, :, C // 2 - 1:C // 2, :]
    A_last = A[:, :, :, C - 1:C, :]

    q_in = qc * jnp.exp(A - A_mid)
    k_in = kc * jnp.exp(A_mid - A)
    causal = jnp.tril(jnp.ones((C, C), dtype=bool))
    scores = jnp.where(causal, jnp.einsum('bhctk,bhcsk->bhcts', q_in, k_in), 0.0)
    o_intra = jnp.einsum('bhcts,bhcsv->bhctv', scores, vc)

    dS = jnp.einsum('bhcsk,bhcsv->bhckv', kc * jnp.exp(A_last - A), vc)
    decay = jnp.exp(A_last[:, :, :, 0, :])

    def step(S_prev, xs):
        d, ds = xs
        return d[..., None] * S_prev + ds, S_prev

    S0 = jnp.zeros((B, HGRN_HEADS, HGRN_DK, HGRN_DV), f32)
    _, S_before = lax.scan(step, S0, (jnp.moveaxis(decay, 2, 0), jnp.moveaxis(dS, 2, 0)))
    S_before = jnp.moveaxis(S_before, 0, 2)
    o_inter = jnp.einsum('bhctk,bhckv->bhctv', qc * jnp.exp(A), S_before)

    o = o_intra + o_inter
    return o.transpose(0, 2, 3, 1, 4).reshape(B, S, HGRN_HEADS, HGRN_DV)


def _fwd_setup_inputs(seed: int = 0) -> dict:
    key = jax.random.key(seed)
    ks = jax.random.split(key, 16)
    f32 = jnp.float32

    def normal(k, shape, scale):
        return jax.random.normal(k, shape, f32) * scale

    return {
        "x": normal(ks[0], (BATCH, SEQ, D_MODEL), 1.0),
        "norm_mix_g": 1.0 + normal(ks[1], (DEPTH, D_MODEL), 0.05),
        "w_in": normal(ks[2], (DEPTH, D_MODEL, IN_WIDTH), D_MODEL ** -0.5),
        "gmlp_ln_g": 1.0 + normal(ks[3], (DEPTH, GMLP_WIDTH), 0.05),
        "gmlp_ln_b": normal(ks[4], (DEPTH, GMLP_WIDTH), 0.02),
        "gmlp_w_s": normal(ks[5], (DEPTH, GMLP_GROUPS, GMLP_CHUNK, GMLP_CHUNK), GMLP_CHUNK ** -0.5),
        "gmlp_b_s": 1.0 + normal(ks[6], (DEPTH, GMLP_GROUPS, GMLP_CHUNK), 0.1),
        "hgrn_lb_table": normal(ks[7], (DEPTH + 1, HGRN_KEY_WIDTH), 0.5),
        "hgrn_norm_g": 1.0 + normal(ks[8], (DEPTH, HGRN_DV * HGRN_HEADS), 0.05),
        "w_branch_a": normal(ks[9], (DEPTH, GMLP_WIDTH, D_MODEL), GMLP_WIDTH ** -0.5),
        "w_branch_b": normal(ks[10], (DEPTH, HGRN_VAL_WIDTH, D_MODEL), HGRN_VAL_WIDTH ** -0.5),
        "w_out": normal(ks[11], (DEPTH, D_MODEL, D_MODEL), D_MODEL ** -0.5),
        "norm_ffn_g": 1.0 + normal(ks[12], (DEPTH, D_MODEL), 0.05),
        "w_gate_up": normal(ks[13], (DEPTH, D_MODEL, 2 * D_FF), D_MODEL ** -0.5),
        "w_down": normal(ks[14], (DEPTH, D_FF, D_MODEL), D_FF ** -0.5),
        "norm_final_g": 1.0 + normal(ks[15], (D_MODEL,), 0.05),
    }


def _fwd_reference(x, norm_mix_g, w_in, gmlp_ln_g, gmlp_ln_b, gmlp_w_s, gmlp_b_s,
              hgrn_lb_table, hgrn_norm_g, w_branch_a, w_branch_b, w_out,
              norm_ffn_g, w_gate_up, w_down, norm_final_g):
    B, S, _ = x.shape
    split_pts = _split_points(IN_SIZES)
    lb_all = jnp.cumsum(jax.nn.softmax(hgrn_lb_table.astype(jnp.float32), axis=0), axis=0)

    for l in range(DEPTH):
        h = rms_norm(x, norm_mix_g[l])
        proj = jnp.einsum('bsd,de->bse', h, w_in[l])
        u, v, q, f_logit, i_val, g_out, gate_a, gate_b = jnp.split(proj, split_pts, axis=-1)

        a = gmlp_spatial_gating(jax.nn.gelu(u), jax.nn.gelu(v), gmlp_ln_g[l], gmlp_ln_b[l],
                                gmlp_w_s[l], gmlp_b_s[l])
        y_a = jnp.einsum('bse,ed->bsd', a, w_branch_a[l])

        o = hgrn2_chunkwise(q, f_logit, i_val, lb_all[l])
        o = o * lax.rsqrt(jnp.mean(o * o, axis=-1, keepdims=True) + NORM_EPS)
        o = (o.reshape(B, S, HGRN_VAL_WIDTH) * hgrn_norm_g[l].astype(jnp.float32)).astype(x.dtype)
        o = o * jax.nn.silu(g_out)
        y_b = jnp.einsum('bse,ed->bsd', o, w_branch_b[l])

        merged = jax.nn.sigmoid(gate_a) * y_a + jax.nn.sigmoid(gate_b) * y_b
        x = x + jnp.einsum('bsd,de->bse', merged, w_out[l])

        h = rms_norm(x, norm_ffn_g[l])
        gu = jnp.einsum('bsd,df->bsf', h, w_gate_up[l])
        gate, up = jnp.split(gu, [D_FF], axis=-1)
        x = x + jnp.einsum('bsf,fd->bsd', jax.nn.silu(gate) * up, w_down[l])

    return rms_norm(x, norm_final_g)


import jax as _jax
import jax.numpy as _jnp

TWIN_FORMAT = 'train_step'
FWD_PARAMS = ['x', 'norm_mix_g', 'w_in', 'gmlp_ln_g', 'gmlp_ln_b', 'gmlp_w_s', 'gmlp_b_s', 'hgrn_lb_table', 'hgrn_norm_g', 'w_branch_a', 'w_branch_b', 'w_out', 'norm_ffn_g', 'w_gate_up', 'w_down', 'norm_final_g']
TWIN_WEIGHTS = ['norm_mix_g', 'w_in', 'gmlp_ln_g', 'gmlp_ln_b', 'gmlp_w_s', 'gmlp_b_s', 'hgrn_lb_table', 'hgrn_norm_g', 'w_branch_a', 'w_branch_b', 'w_out', 'norm_ffn_g', 'w_gate_up', 'w_down', 'norm_final_g']
TWIN_DIFF_INPUT = 'x'
TWIN_INPUTS = ['x', 'norm_mix_g', 'w_in', 'gmlp_ln_g', 'gmlp_ln_b', 'gmlp_w_s', 'gmlp_b_s', 'hgrn_lb_table', 'hgrn_norm_g', 'w_branch_a', 'w_branch_b', 'w_out', 'norm_ffn_g', 'w_gate_up', 'w_down', 'norm_final_g', 'loss_target', 'm_norm_mix_g', 'm_w_in', 'm_gmlp_ln_g', 'm_gmlp_ln_b', 'm_gmlp_w_s', 'm_gmlp_b_s', 'm_hgrn_lb_table', 'm_hgrn_norm_g', 'm_w_branch_a', 'm_w_branch_b', 'm_w_out', 'm_norm_ffn_g', 'm_w_gate_up', 'm_w_down', 'm_norm_final_g', 'v_norm_mix_g', 'v_w_in', 'v_gmlp_ln_g', 'v_gmlp_ln_b', 'v_gmlp_w_s', 'v_gmlp_b_s', 'v_hgrn_lb_table', 'v_hgrn_norm_g', 'v_w_branch_a', 'v_w_branch_b', 'v_w_out', 'v_norm_ffn_g', 'v_w_gate_up', 'v_w_down', 'v_norm_final_g']
TWIN_OUTPUTS = ['loss', 'grad_x', 'grad_norm_mix_g', 'grad_w_in', 'grad_gmlp_ln_g', 'grad_gmlp_ln_b', 'grad_gmlp_w_s', 'grad_gmlp_b_s', 'grad_hgrn_lb_table', 'grad_hgrn_norm_g', 'grad_w_branch_a', 'grad_w_branch_b', 'grad_w_out', 'grad_norm_ffn_g', 'grad_w_gate_up', 'grad_w_down', 'grad_norm_final_g', 'delta_norm_mix_g', 'delta_w_in', 'delta_gmlp_ln_g', 'delta_gmlp_ln_b', 'delta_gmlp_w_s', 'delta_gmlp_b_s', 'delta_hgrn_lb_table', 'delta_hgrn_norm_g', 'delta_w_branch_a', 'delta_w_branch_b', 'delta_w_out', 'delta_norm_ffn_g', 'delta_w_gate_up', 'delta_w_down', 'delta_norm_final_g', 'new_m_norm_mix_g', 'new_m_w_in', 'new_m_gmlp_ln_g', 'new_m_gmlp_ln_b', 'new_m_gmlp_w_s', 'new_m_gmlp_b_s', 'new_m_hgrn_lb_table', 'new_m_hgrn_norm_g', 'new_m_w_branch_a', 'new_m_w_branch_b', 'new_m_w_out', 'new_m_norm_ffn_g', 'new_m_w_gate_up', 'new_m_w_down', 'new_m_norm_final_g', 'new_v_norm_mix_g', 'new_v_w_in', 'new_v_gmlp_ln_g', 'new_v_gmlp_ln_b', 'new_v_gmlp_w_s', 'new_v_gmlp_b_s', 'new_v_hgrn_lb_table', 'new_v_hgrn_norm_g', 'new_v_w_branch_a', 'new_v_w_branch_b', 'new_v_w_out', 'new_v_norm_ffn_g', 'new_v_w_gate_up', 'new_v_w_down', 'new_v_norm_final_g']
TWIN_LEAF_KINDS = {'loss': 'loss', 'grad_x': 'grad_x', 'grad_norm_mix_g': 'grad_w', 'grad_w_in': 'grad_w', 'grad_gmlp_ln_g': 'grad_w', 'grad_gmlp_ln_b': 'grad_w', 'grad_gmlp_w_s': 'grad_w', 'grad_gmlp_b_s': 'grad_w', 'grad_hgrn_lb_table': 'grad_w', 'grad_hgrn_norm_g': 'grad_w', 'grad_w_branch_a': 'grad_w', 'grad_w_branch_b': 'grad_w', 'grad_w_out': 'grad_w', 'grad_norm_ffn_g': 'grad_w', 'grad_w_gate_up': 'grad_w', 'grad_w_down': 'grad_w', 'grad_norm_final_g': 'grad_w', 'delta_norm_mix_g': 'delta_w', 'delta_w_in': 'delta_w', 'delta_gmlp_ln_g': 'delta_w', 'delta_gmlp_ln_b': 'delta_w', 'delta_gmlp_w_s': 'delta_w', 'delta_gmlp_b_s': 'delta_w', 'delta_hgrn_lb_table': 'delta_w', 'delta_hgrn_norm_g': 'delta_w', 'delta_w_branch_a': 'delta_w', 'delta_w_branch_b': 'delta_w', 'delta_w_out': 'delta_w', 'delta_norm_ffn_g': 'delta_w', 'delta_w_gate_up': 'delta_w', 'delta_w_down': 'delta_w', 'delta_norm_final_g': 'delta_w', 'new_m_norm_mix_g': 'new_m', 'new_m_w_in': 'new_m', 'new_m_gmlp_ln_g': 'new_m', 'new_m_gmlp_ln_b': 'new_m', 'new_m_gmlp_w_s': 'new_m', 'new_m_gmlp_b_s': 'new_m', 'new_m_hgrn_lb_table': 'new_m', 'new_m_hgrn_norm_g': 'new_m', 'new_m_w_branch_a': 'new_m', 'new_m_w_branch_b': 'new_m', 'new_m_w_out': 'new_m', 'new_m_norm_ffn_g': 'new_m', 'new_m_w_gate_up': 'new_m', 'new_m_w_down': 'new_m', 'new_m_norm_final_g': 'new_m', 'new_v_norm_mix_g': 'new_v', 'new_v_w_in': 'new_v', 'new_v_gmlp_ln_g': 'new_v', 'new_v_gmlp_ln_b': 'new_v', 'new_v_gmlp_w_s': 'new_v', 'new_v_gmlp_b_s': 'new_v', 'new_v_hgrn_lb_table': 'new_v', 'new_v_hgrn_norm_g': 'new_v', 'new_v_w_branch_a': 'new_v', 'new_v_w_branch_b': 'new_v', 'new_v_w_out': 'new_v', 'new_v_norm_ffn_g': 'new_v', 'new_v_w_gate_up': 'new_v', 'new_v_w_down': 'new_v', 'new_v_norm_final_g': 'new_v'}


def _forward(args):
    return _fwd_reference(*[args[k] for k in FWD_PARAMS])


def _output_shape():
    out = _jax.eval_shape(lambda: _forward(_fwd_setup_inputs(0)))
    return out.shape, out.dtype

N_MICROBATCH = 1
ADAM_LR = 0.001
ADAM_B1 = 0.9
ADAM_B2 = 0.999
ADAM_EPS = 1e-08
ADAM_WD = 0.01
ADAM_STEP = 10
PER_EXAMPLE_BATCH_AXIS = {'x': 0, 'loss_target': 0}
SHARED_INPUTS = []
_WEIGHT_DTYPES = {'norm_mix_g': _jnp.float32, 'w_in': _jnp.float32, 'gmlp_ln_g': _jnp.float32, 'gmlp_ln_b': _jnp.float32, 'gmlp_w_s': _jnp.float32, 'gmlp_b_s': _jnp.float32, 'hgrn_lb_table': _jnp.float32, 'hgrn_norm_g': _jnp.float32, 'w_branch_a': _jnp.float32, 'w_branch_b': _jnp.float32, 'w_out': _jnp.float32, 'norm_ffn_g': _jnp.float32, 'w_gate_up': _jnp.float32, 'w_down': _jnp.float32, 'norm_final_g': _jnp.float32}
MOMENT_SCALE = {'norm_mix_g': 1.687266e-01, 'w_in': 5.576355e-02, 'gmlp_ln_g': 4.576127e-02, 'gmlp_ln_b': 4.212769e-02, 'gmlp_w_s': 4.377184e-02, 'gmlp_b_s': 6.414666e-02, 'hgrn_lb_table': 3.311495e-02, 'hgrn_norm_g': 5.880355e-02, 'w_branch_a': 8.016152e-02, 'w_branch_b': 5.641502e-02, 'w_out': 9.828740e-02, 'norm_ffn_g': 1.388463e-01, 'w_gate_up': 5.129314e-02, 'w_down': 8.426746e-02, 'norm_final_g': 3.195608e+01}


def _to_microbatches(a, axis):
    t = _jnp.moveaxis(a, axis, 0)
    t = t.reshape((N_MICROBATCH, t.shape[0] // N_MICROBATCH) + t.shape[1:])
    return _jnp.moveaxis(t, 1, axis + 1)


def setup_inputs(seed: int = 0) -> dict:
    inp = _fwd_setup_inputs(seed)
    key = _jax.random.fold_in(_jax.random.key(seed), 7919)
    shape, _ = _output_shape()
    out = dict(inp)
    out["loss_target"] = _jax.random.normal(_jax.random.fold_in(key, 0), shape, _jnp.float32)
    for i, name in enumerate(TWIN_WEIGHTS):
        w = inp[name].astype(_jnp.float32)
        if MOMENT_SCALE is None:
            s = _jnp.sqrt(_jnp.mean(_jnp.square(w)) + 1e-30)
        else:
            s = MOMENT_SCALE[name]
        km, kv = _jax.random.split(_jax.random.fold_in(key, i + 1))
        out[name] = w
        out["m_" + name] = s * _jax.random.normal(km, w.shape, _jnp.float32)
        out["v_" + name] = (s * s) * _jax.random.uniform(kv, w.shape, _jnp.float32, 0.5, 1.5)
    if N_MICROBATCH > 1:
        for name, axis in PER_EXAMPLE_BATCH_AXIS.items():
            out[name] = _to_microbatches(out[name], axis)
    return {'x': out['x'], 'norm_mix_g': out['norm_mix_g'], 'w_in': out['w_in'], 'gmlp_ln_g': out['gmlp_ln_g'], 'gmlp_ln_b': out['gmlp_ln_b'], 'gmlp_w_s': out['gmlp_w_s'], 'gmlp_b_s': out['gmlp_b_s'], 'hgrn_lb_table': out['hgrn_lb_table'], 'hgrn_norm_g': out['hgrn_norm_g'], 'w_branch_a': out['w_branch_a'], 'w_branch_b': out['w_branch_b'], 'w_out': out['w_out'], 'norm_ffn_g': out['norm_ffn_g'], 'w_gate_up': out['w_gate_up'], 'w_down': out['w_down'], 'norm_final_g': out['norm_final_g'], 'loss_target': out['loss_target'], 'm_norm_mix_g': out['m_norm_mix_g'], 'm_w_in': out['m_w_in'], 'm_gmlp_ln_g': out['m_gmlp_ln_g'], 'm_gmlp_ln_b': out['m_gmlp_ln_b'], 'm_gmlp_w_s': out['m_gmlp_w_s'], 'm_gmlp_b_s': out['m_gmlp_b_s'], 'm_hgrn_lb_table': out['m_hgrn_lb_table'], 'm_hgrn_norm_g': out['m_hgrn_norm_g'], 'm_w_branch_a': out['m_w_branch_a'], 'm_w_branch_b': out['m_w_branch_b'], 'm_w_out': out['m_w_out'], 'm_norm_ffn_g': out['m_norm_ffn_g'], 'm_w_gate_up': out['m_w_gate_up'], 'm_w_down': out['m_w_down'], 'm_norm_final_g': out['m_norm_final_g'], 'v_norm_mix_g': out['v_norm_mix_g'], 'v_w_in': out['v_w_in'], 'v_gmlp_ln_g': out['v_gmlp_ln_g'], 'v_gmlp_ln_b': out['v_gmlp_ln_b'], 'v_gmlp_w_s': out['v_gmlp_w_s'], 'v_gmlp_b_s': out['v_gmlp_b_s'], 'v_hgrn_lb_table': out['v_hgrn_lb_table'], 'v_hgrn_norm_g': out['v_hgrn_norm_g'], 'v_w_branch_a': out['v_w_branch_a'], 'v_w_branch_b': out['v_w_branch_b'], 'v_w_out': out['v_w_out'], 'v_norm_ffn_g': out['v_norm_ffn_g'], 'v_w_gate_up': out['v_w_gate_up'], 'v_w_down': out['v_w_down'], 'v_norm_final_g': out['v_norm_final_g']}


def _loss(weights, diff, rest, loss_target):
    with _jax.named_scope("forward"):
        args = {**rest, TWIN_DIFF_INPUT: diff, **{k: w.astype(_WEIGHT_DTYPES[k]) for k, w in weights.items()}}
        y = _forward(args)
    with _jax.named_scope("loss_head"):
        err = _jnp.square(y.astype(_jnp.float32) - loss_target)
        return 0.5 * _jnp.sum(_jnp.mean(err, axis=-1)) if err.ndim else 0.5 * err


def _adamw(w, g, m, v):
    m = ADAM_B1 * m + (1.0 - ADAM_B1) * g
    v = ADAM_B2 * v + (1.0 - ADAM_B2) * _jnp.square(g)
    m_hat = m / (1.0 - ADAM_B1 ** ADAM_STEP)
    v_hat = v / (1.0 - ADAM_B2 ** ADAM_STEP)
    delta = -ADAM_LR * (m_hat / (_jnp.sqrt(v_hat) + ADAM_EPS) + ADAM_WD * w)
    return delta, m, v


def reference(x, norm_mix_g, w_in, gmlp_ln_g, gmlp_ln_b, gmlp_w_s, gmlp_b_s, hgrn_lb_table, hgrn_norm_g, w_branch_a, w_branch_b, w_out, norm_ffn_g, w_gate_up, w_down, norm_final_g, loss_target, m_norm_mix_g, m_w_in, m_gmlp_ln_g, m_gmlp_ln_b, m_gmlp_w_s, m_gmlp_b_s, m_hgrn_lb_table, m_hgrn_norm_g, m_w_branch_a, m_w_branch_b, m_w_out, m_norm_ffn_g, m_w_gate_up, m_w_down, m_norm_final_g, v_norm_mix_g, v_w_in, v_gmlp_ln_g, v_gmlp_ln_b, v_gmlp_w_s, v_gmlp_b_s, v_hgrn_lb_table, v_hgrn_norm_g, v_w_branch_a, v_w_branch_b, v_w_out, v_norm_ffn_g, v_w_gate_up, v_w_down, v_norm_final_g):
    given = dict(x=x, norm_mix_g=norm_mix_g, w_in=w_in, gmlp_ln_g=gmlp_ln_g, gmlp_ln_b=gmlp_ln_b, gmlp_w_s=gmlp_w_s, gmlp_b_s=gmlp_b_s, hgrn_lb_table=hgrn_lb_table, hgrn_norm_g=hgrn_norm_g, w_branch_a=w_branch_a, w_branch_b=w_branch_b, w_out=w_out, norm_ffn_g=norm_ffn_g, w_gate_up=w_gate_up, w_down=w_down, norm_final_g=norm_final_g, loss_target=loss_target, m_norm_mix_g=m_norm_mix_g, m_w_in=m_w_in, m_gmlp_ln_g=m_gmlp_ln_g, m_gmlp_ln_b=m_gmlp_ln_b, m_gmlp_w_s=m_gmlp_w_s, m_gmlp_b_s=m_gmlp_b_s, m_hgrn_lb_table=m_hgrn_lb_table, m_hgrn_norm_g=m_hgrn_norm_g, m_w_branch_a=m_w_branch_a, m_w_branch_b=m_w_branch_b, m_w_out=m_w_out, m_norm_ffn_g=m_norm_ffn_g, m_w_gate_up=m_w_gate_up, m_w_down=m_w_down, m_norm_final_g=m_norm_final_g, v_norm_mix_g=v_norm_mix_g, v_w_in=v_w_in, v_gmlp_ln_g=v_gmlp_ln_g, v_gmlp_ln_b=v_gmlp_ln_b, v_gmlp_w_s=v_gmlp_w_s, v_gmlp_b_s=v_gmlp_b_s, v_hgrn_lb_table=v_hgrn_lb_table, v_hgrn_norm_g=v_hgrn_norm_g, v_w_branch_a=v_w_branch_a, v_w_branch_b=v_w_branch_b, v_w_out=v_w_out, v_norm_ffn_g=v_norm_ffn_g, v_w_gate_up=v_w_gate_up, v_w_down=v_w_down, v_norm_final_g=v_norm_final_g)
    weights = {n: given[n] for n in TWIN_WEIGHTS}
    shared = {n: given[n] for n in SHARED_INPUTS}
    per_example = {n: given[n] for n in ['x']}
    grad_fn = _jax.value_and_grad(_loss, argnums=(0, 1))

    def one_microbatch(ex, loss_target):
        ex = dict(ex)
        diff = ex.pop(TWIN_DIFF_INPUT)
        return grad_fn(weights, diff, {**shared, **ex}, loss_target)

    if N_MICROBATCH == 1:
        loss, (grad_w, grad_x) = one_microbatch(per_example, given["loss_target"])
    else:
        def body(carry, xs):
            loss_sum, grad_sum = carry
            l_k, (gw_k, gx_k) = one_microbatch(xs[0], xs[1])
            with _jax.named_scope("update"):
                return (loss_sum + l_k, _jax.tree.map(_jnp.add, grad_sum, gw_k)), gx_k

        init = (_jnp.zeros((), _jnp.float32), _jax.tree.map(_jnp.zeros_like, weights))
        (loss, grad_w), grad_x = _jax.lax.scan(body, init, (per_example, given["loss_target"]))
    with _jax.named_scope("update"):
        delta_w, new_m, new_v = {}, {}, {}
        for n in TWIN_WEIGHTS:
            delta_w[n], new_m[n], new_v[n] = _adamw(weights[n], grad_w[n], given["m_" + n], given["v_" + n])
    return (loss, grad_x, *[grad_w[n] for n in TWIN_WEIGHTS], *[delta_w[n] for n in TWIN_WEIGHTS],
            *[new_m[n] for n in TWIN_WEIGHTS], *[new_v[n] for n in TWIN_WEIGHTS])
```

```python
import functools

import jax
import jax.numpy as jnp
from jax import lax
from jax.experimental import pallas as pl
from jax.experimental.pallas import tpu as pltpu

F32, BF16 = jnp.float32, jnp.bfloat16
D_MODEL = 1024
N_DEV = 8
HEADS = 8
HEAD_DIM = 128
GROUPS = 8
GMLP_CHUNK = 128
HGRN_CHUNK = 64
HGRN_SCALE = HEAD_DIM ** -0.5
D_FF = 2816
FF_BLOCK = D_FF // 4
DOWN_ROWS = D_FF // N_DEV
BRANCH_ROWS = D_MODEL // N_DEV
NORM_EPS = 1e-6
ADAM_LR, ADAM_B1, ADAM_B2, ADAM_EPS, ADAM_WD, ADAM_STEP = 0.001, 0.9, 0.999, 1e-08, 0.01, 10
SMALL_ROWS = 192
V7X_VMEM_BYTES = 64 * 1024 * 1024
VMEM_CAP = V7X_VMEM_BYTES - 6 * 1024 * 1024
MESH_ID = pl.DeviceIdType.MESH
ANY = pl.BlockSpec(memory_space=pl.ANY)
Q_POS, U_POS, GATE_POS = 0, 4, 6


def _pos_of_dev(j):
    return jnp.where(j < 2, j + 4, jnp.where(j < 6, j - 2, j))


def _dev_of_pos(p):
    return jnp.where(p < 4, p + 2, jnp.where(p < 6, p - 4, p))


def _nbytes(shape, dtype):
    n = 1
    for s in shape:
        n *= s
    return n * jnp.dtype(dtype).itemsize


def _params(blocks, scratch=(), temps=0, sem=None):
    need = 2 * sum(_nbytes(s, d) for s, d in blocks) + sum(_nbytes(s, d) for s, d in scratch) + temps
    need = min(max(need + (4 << 20), 16 << 20), VMEM_CAP)
    return pltpu.CompilerParams(dimension_semantics=sem, vmem_limit_bytes=int(need))


def _tile(n, pref):
    return pref if n % pref == 0 else n


def _dot(a, b):
    return jnp.dot(a, b, preferred_element_type=F32)


def _dot_nt(a, b):
    return lax.dot_general(a, b, (((1,), (1,)), ((), ())), preferred_element_type=F32)


def _dot_tn(a, b):
    return lax.dot_general(a, b, (((0,), (0,)), ((), ())), preferred_element_type=F32)


def _sigmoid(x):
    return 1.0 / (1.0 + jnp.exp(-x))


_GELU_C = 0.7978845608028654


def _gelu(x):
    return x * (0.5 * (1.0 + jnp.tanh(_GELU_C * (x + 0.044715 * (x * x * x)))))


def _gelu_grad(x):
    t = jnp.tanh(_GELU_C * (x + 0.044715 * (x * x * x)))
    return 0.5 * (1.0 + t) + 0.5 * x * (1.0 - t * t) * (_GELU_C * (1.0 + 3.0 * 0.044715 * x * x))


def _rms_stats(x):
    r = lax.rsqrt(jnp.mean(x * x, axis=-1, keepdims=True) + NORM_EPS)
    return r, x * r


def _rms_bwd(dy, x, g):
    r, xh = _rms_stats(x)
    dg = jnp.sum(dy * xh, axis=0, keepdims=True)
    dxh = dy * g
    dx = r * (dxh - xh * jnp.mean(dxh * xh, axis=-1, keepdims=True))
    return dx, dg


def _split3(x):
    hi = x.astype(BF16)
    r = x - hi.astype(F32)
    mid = r.astype(BF16)
    lo = (r - mid.astype(F32)).astype(BF16)
    return hi, mid, lo


def _mask_mm(mask_bf16, x):
    hi, mid, lo = _split3(x)
    return _dot(mask_bf16, hi) + _dot(mask_bf16, mid) + _dot(mask_bf16, lo)


def _place():
    return lax.axis_index("x"), lax.axis_index("y"), lax.axis_index("c")


def _all_gather(name, srcs, out_shapes, slicers):
    n = len(srcs)

    def body(*refs):
        src, out = refs[:n], refs[n:2 * n]
        send, recv, loc = refs[2 * n:]
        x, y, c = _place()
        me, sib = (x, y, c), (x, y, 1 - c)
        chips = [(1 - x, y), (x, 1 - y), (1 - x, 1 - y)]

        def dev(p):
            return 4 * p[0] + 2 * p[1] + p[2]

        def rc(i, k, block, to, from_src=False):
            dst = slicers[i](out[i], dev(block))
            return pltpu.make_async_remote_copy(
                src_ref=src[i] if from_src else dst, dst_ref=dst, send_sem=send.at[7 * i + k],
                recv_sem=recv.at[7 * i + k], device_id=to, device_id_type=MESH_ID)

        mine = [pltpu.make_async_copy(src[i], slicers[i](out[i], dev(me)), loc.at[i]) for i in range(n)]
        for cp in mine:
            cp.start()
        first = []
        for i in range(n):
            first.append(rc(i, 0, me, sib, True))
            for j, chip in enumerate(chips):
                first.append(rc(i, 1 + j, me, (*chip, c), True))
        for cp in first:
            cp.start()
        passed = []
        for j, chip in enumerate(chips):
            for i in range(n):
                rc(i, 1 + j, (*chip, c), me).wait_recv()
                cp = rc(i, 4 + j, (*chip, c), sib)
                cp.start()
                passed.append(cp)
        for i in range(n):
            rc(i, 0, sib, me).wait_recv()
            for j, chip in enumerate(chips):
                rc(i, 4 + j, (*chip, 1 - c), me).wait_recv()
        for cp in first + passed:
            cp.wait_send()
        for cp in mine:
            cp.wait()

    return pl.pallas_call(
        body, name=name, out_shape=[jax.ShapeDtypeStruct(s, d) for s, d in out_shapes],
        in_specs=[ANY] * n, out_specs=[ANY] * n,
        scratch_shapes=[pltpu.SemaphoreType.DMA((7 * n,)), pltpu.SemaphoreType.DMA((7 * n,)),
                        pltpu.SemaphoreType.DMA((n,))],
    )(*srcs)


def _exchange_sibling(name, grads, shard_fns, shard_shapes):
    n = len(grads)

    def body(*refs):
        g, own, land = refs[:n], refs[n:2 * n], refs[2 * n:3 * n]
        send, recv, loc = refs[3 * n:]
        x, y, c = _place()
        remote, local = [], []
        for i in range(n):
            for q in range(4):
                cp = pltpu.make_async_remote_copy(
                    src_ref=shard_fns[i](g[i], 2 * q + (1 - c)), dst_ref=land[i].at[q], send_sem=send.at[4 * i + q],
                    recv_sem=recv.at[4 * i + q], device_id=(x, y, 1 - c), device_id_type=MESH_ID)
                cp.start()
                remote.append(cp)
                lc = pltpu.make_async_copy(shard_fns[i](g[i], 2 * q + c), own[i].at[q], loc.at[4 * i + q])
                lc.start()
                local.append(lc)
        for cp in remote:
            cp.wait()
        for cp in local:
            cp.wait()

    shapes = [jax.ShapeDtypeStruct((4, *s), F32) for s in shard_shapes]
    outs = pl.pallas_call(
        body, name=name, out_shape=shapes + shapes, in_specs=[ANY] * n, out_specs=[ANY] * (2 * n),
        scratch_shapes=[pltpu.SemaphoreType.DMA((4 * n,)), pltpu.SemaphoreType.DMA((4 * n,)),
                        pltpu.SemaphoreType.DMA((4 * n,))],
    )(*grads)
    return outs[:n], outs[n:]


def _exchange_chips(name, parts):
    n = len(parts)

    def body(*refs):
        part, out = refs[:n], refs[n:2 * n]
        send, recv, loc = refs[2 * n:]
        x, y, c = _place()
        remote, local = [], []
        for i in range(n):
            lc = pltpu.make_async_copy(part[i].at[2 * x + y], out[i].at[3], loc.at[i])
            lc.start()
            local.append(lc)
            for s in range(3):
                qx = 1 - x if (s + 1) // 2 else x
                qy = 1 - y if (s + 1) % 2 else y
                cp = pltpu.make_async_remote_copy(
                    src_ref=part[i].at[2 * qx + qy], dst_ref=out[i].at[s], send_sem=send.at[3 * i + s],
                    recv_sem=recv.at[3 * i + s], device_id=(qx, qy, c), device_id_type=MESH_ID)
                cp.start()
                remote.append(cp)
        for cp in remote:
            cp.wait()
        for cp in local:
            cp.wait()

    return pl.pallas_call(
        body, name=name, out_shape=[jax.ShapeDtypeStruct(p.shape, p.dtype) for p in parts],
        in_specs=[ANY] * n, out_specs=[ANY] * n,
        scratch_shapes=[pltpu.SemaphoreType.DMA((3 * n,)), pltpu.SemaphoreType.DMA((3 * n,)),
                        pltpu.SemaphoreType.DMA((n,))],
    )(*parts)


def _chip_partial(name, own, land):
    _, rows, cols = own.shape
    tr = _tile(rows, 256) if rows % 256 == 0 else _tile(rows, 176)

    def body(a_ref, b_ref, o_ref):
        o_ref[...] = (a_ref[...] + b_ref[...]).astype(BF16)

    spec = pl.BlockSpec((None, tr, cols), lambda q, r: (q, r, 0))
    return pl.pallas_call(
        body, name=name, out_shape=jax.ShapeDtypeStruct(own.shape, BF16), grid=(4, rows // tr),
        in_specs=[spec, spec], out_specs=spec,
        compiler_params=_params([((tr, cols), F32)] * 2 + [((tr, cols), BF16)], sem=("arbitrary", "arbitrary")),
    )(own, land)


def _adamw_math(w, g, m, v):
    m = ADAM_B1 * m + (1.0 - ADAM_B1) * g
    v = ADAM_B2 * v + (1.0 - ADAM_B2) * (g * g)
    m_hat = m / (1.0 - ADAM_B1 ** ADAM_STEP)
    v_hat = v / (1.0 - ADAM_B2 ** ADAM_STEP)
    delta = -ADAM_LR * (m_hat / (jnp.sqrt(v_hat) + ADAM_EPS) + ADAM_WD * w)
    return delta, m, v


def _adamw(name, parts, w, m, v):
    n_parts, rows, cols = parts.shape
    tr = _tile(rows, 256) if rows % 256 == 0 else _tile(rows, 176)
    if rows % tr:
        tr = rows

    def body(p_ref, w_ref, m_ref, v_ref, g_out, d_out, m_out, v_out):
        g = p_ref[n_parts - 1].astype(F32)
        for s in range(n_parts - 1):
            g = g + p_ref[s].astype(F32)
        delta, m_new, v_new = _adamw_math(w_ref[...], g, m_ref[...], v_ref[...])
        g_out[...] = g
        d_out[...] = delta
        m_out[...] = m_new
        v_out[...] = v_new

    spec = pl.BlockSpec((tr, cols), lambda r: (r, 0))
    return pl.pallas_call(
        body, name=name, out_shape=[jax.ShapeDtypeStruct((rows, cols), F32)] * 4, grid=(rows // tr,),
        in_specs=[pl.BlockSpec((n_parts, tr, cols), lambda r: (0, r, 0)), spec, spec, spec], out_specs=[spec] * 4,
        compiler_params=_params([((n_parts, tr, cols), parts.dtype)] + [((tr, cols), F32)] * 7, sem=("arbitrary",)),
    )(parts, w, m, v)


def _rms_forward(x, gain):
    t = x.shape[0]
    tm = _tile(t, 512)

    def body(x_ref, g_ref, h_ref):
        _, xh = _rms_stats(x_ref[...])
        h_ref[...] = (xh * g_ref[...]).astype(BF16)

    return pl.pallas_call(
        body, name="rms_mix_fwd", out_shape=jax.ShapeDtypeStruct((t, D_MODEL), BF16), grid=(t // tm,),
        in_specs=[pl.BlockSpec((tm, D_MODEL), lambda m: (m, 0)), pl.BlockSpec((1, D_MODEL), lambda m: (0, 0))],
        out_specs=pl.BlockSpec((tm, D_MODEL), lambda m: (m, 0)),
        compiler_params=_params([((tm, D_MODEL), F32), ((tm, D_MODEL), BF16)], temps=8 << 20, sem=("arbitrary",)),
    )(x, gain)


def _proj_forward(h, w_in_g):
    t = h.shape[0]
    tm = _tile(t, 1024)

    def body(h_ref, w_ref, o_ref):
        o_ref[...] = _dot(h_ref[...], w_ref[...])

    return pl.pallas_call(
        body, name="proj_fwd", out_shape=jax.ShapeDtypeStruct((N_DEV, t, D_MODEL), F32), grid=(N_DEV, t // tm),
        in_specs=[pl.BlockSpec((tm, D_MODEL), lambda p, m: (m, 0)),
                  pl.BlockSpec((None, D_MODEL, D_MODEL), lambda p, m: (p, 0, 0))],
        out_specs=pl.BlockSpec((None, tm, D_MODEL), lambda p, m: (p, m, 0)),
        compiler_params=_params([((tm, D_MODEL), BF16), ((D_MODEL, D_MODEL), BF16), ((tm, D_MODEL), F32)],
                                sem=("arbitrary", "arbitrary")),
    )(h, w_in_g)


def _masked_ws(ws_ref, g):
    row = lax.broadcasted_iota(jnp.int32, (GMLP_CHUNK, GMLP_CHUNK), 0)
    col = lax.broadcasted_iota(jnp.int32, (GMLP_CHUNK, GMLP_CHUNK), 1)
    return jnp.where(row >= col, ws_ref[g], 0.0).astype(BF16)


def _gmlp_forward(proj, ln_g, ln_b, w_s, bias_b):
    t = proj.shape[1]
    tm = _tile(t, 256)
    chunks = tm // GMLP_CHUNK

    def body(u_ref, v_ref, lng_ref, lnb_ref, ws_ref, bias_ref, a_ref, vn_scr):
        vv = _gelu(v_ref[...])
        mu = jnp.mean(vv, axis=-1, keepdims=True)
        cen = vv - mu
        var = jnp.mean(cen * cen, axis=-1, keepdims=True)
        vn_scr[...] = ((cen * lax.rsqrt(var + NORM_EPS)) * lng_ref[...] + lnb_ref[...]).astype(BF16)
        for g in range(GROUPS):
            wm = _masked_ws(ws_ref, g)
            cols = slice(g * HEAD_DIM, (g + 1) * HEAD_DIM)
            for c in range(chunks):
                rows = slice(c * GMLP_CHUNK, (c + 1) * GMLP_CHUNK)
                mixed = _dot(wm, vn_scr[rows, cols]) + bias_ref[g]
                a_ref[rows, cols] = (_gelu(u_ref[rows, cols]) * mixed).astype(BF16)

    small = pl.BlockSpec((GROUPS, GMLP_CHUNK, GMLP_CHUNK), lambda m: (0, 0, 0))
    vec = pl.BlockSpec((1, D_MODEL), lambda m: (0, 0))
    return pl.pallas_call(
        body, name="gmlp_fwd", out_shape=jax.ShapeDtypeStruct((t, D_MODEL), BF16), grid=(t // tm,),
        in_specs=[pl.BlockSpec((None, tm, D_MODEL), lambda m: (U_POS, m, 0)),
                  pl.BlockSpec((None, tm, D_MODEL), lambda m: (U_POS + 1, m, 0)), vec, vec, small, small],
        out_specs=pl.BlockSpec((tm, D_MODEL), lambda m: (m, 0)),
        scratch_shapes=[pltpu.VMEM((tm, D_MODEL), BF16)],
        compiler_params=_params([((tm, D_MODEL), F32)] * 2 + [((tm, D_MODEL), BF16)] + [((8, 128, 128), F32)] * 2,
                                scratch=[((tm, D_MODEL), BF16)], temps=8 << 20, sem=("arbitrary",)),
    )(proj, proj, ln_g, ln_b, w_s, bias_b)


def _lower_bound(tab_ref):
    t0, t1 = tab_ref[0:1, :], tab_ref[1:2, :]
    mx = jnp.maximum(t0, t1)
    e0, e1 = jnp.exp(t0 - mx), jnp.exp(t1 - mx)
    return e0 / (e0 + e1)


def _tri_masks():
    row = lax.broadcasted_iota(jnp.int32, (HGRN_CHUNK, HGRN_CHUNK), 0)
    col = lax.broadcasted_iota(jnp.int32, (HGRN_CHUNK, HGRN_CHUNK), 1)
    return row >= col, row <= col


def _hgrn_chunk(q, fl, lb):
    lower, _ = _tri_masks()
    s = _sigmoid(fl)
    f = lb + (1.0 - lb) * s
    k = 1.0 - f
    a = _mask_mm(lower.astype(BF16), jnp.log(f))
    a_mid = a[HGRN_CHUNK // 2 - 1:HGRN_CHUNK // 2, :]
    a_last = a[HGRN_CHUNK - 1:HGRN_CHUNK, :]
    qs = q * HGRN_SCALE
    e_in, e_out, e_end, e_all = jnp.exp(a - a_mid), jnp.exp(a_mid - a), jnp.exp(a_last - a), jnp.exp(a)
    return dict(s=s, f=f, k=k, a_last=a_last, e_in=e_in, e_out=e_out, e_end=e_end, e_all=e_all,
                qi=qs * e_in, ki=k * e_out, kd=k * e_end, qe=qs * e_all)


def _hgrn_forward(proj, lb_table, norm_g):
    t = proj.shape[1]
    tb = _tile(t, 256)
    nc = tb // HGRN_CHUNK
    n_chunks = t // HGRN_CHUNK

    def body(q_ref, f_ref, i_ref, g_ref, tab_ref, ng_ref, og_ref, o_ref, st_ref, state):
        @pl.when(pl.program_id(1) == 0)
        def _():
            state[...] = jnp.zeros_like(state)

        lb = _lower_bound(tab_ref)
        lower, _ = _tri_masks()
        for c in range(nc):
            rows = slice(c * HGRN_CHUNK, (c + 1) * HGRN_CHUNK)
            ch = _hgrn_chunk(q_ref[rows, :], f_ref[rows, :], lb)
            vb = i_ref[rows, :].astype(BF16)
            st = state[...]
            st_ref[c] = st
            p = jnp.where(lower, _dot_nt(ch["qi"].astype(BF16), ch["ki"].astype(BF16)), 0.0)
            o = _dot(p.astype(BF16), vb) + _dot_nt(ch["qe"].astype(BF16), st.astype(BF16))
            state[...] = st * jnp.exp(ch["a_last"]) + _dot_tn(vb, ch["kd"].astype(BF16))
            o_ref[rows, :] = o
            r, oh = _rms_stats(o)
            gz = g_ref[rows, :]
            og_ref[rows, :] = ((oh * ng_ref[...]) * (gz * _sigmoid(gz))).astype(BF16)

    def blk(p):
        return pl.BlockSpec((None, tb, HEAD_DIM), lambda h, n: (p, n, h))

    out_blk = pl.BlockSpec((tb, HEAD_DIM), lambda h, n: (n, h))
    return pl.pallas_call(
        body, name="hgrn_fwd",
        out_shape=[jax.ShapeDtypeStruct((t, D_MODEL), BF16), jax.ShapeDtypeStruct((t, D_MODEL), F32),
                   jax.ShapeDtypeStruct((HEADS, n_chunks, HEAD_DIM, HEAD_DIM), F32)],
        grid=(HEADS, t // tb),
        in_specs=[blk(Q_POS), blk(Q_POS + 1), blk(Q_POS + 2), blk(Q_POS + 3),
                  pl.BlockSpec((2, HEAD_DIM), lambda h, n: (0, h)), pl.BlockSpec((1, HEAD_DIM), lambda h, n: (0, h))],
        out_specs=[out_blk, out_blk, pl.BlockSpec((None, nc, HEAD_DIM, HEAD_DIM), lambda h, n: (h, n, 0, 0))],
        scratch_shapes=[pltpu.VMEM((HEAD_DIM, HEAD_DIM), F32)],
        compiler_params=_params([((tb, HEAD_DIM), F32)] * 6 + [((nc, HEAD_DIM, HEAD_DIM), F32)], temps=8 << 20,
                                sem=("arbitrary", "arbitrary")),
    )(proj, proj, proj, proj, lb_table, norm_g)


def _branch_out_forward(a, og, proj, x, w_a, w_b, w_out, ffn_g):
    t = x.shape[0]
    tm = _tile(t, 256)

    def body(a_ref, og_ref, ga_ref, gb_ref, x_ref, wa_ref, wb_ref, wo_ref, g_ref, ya_ref, yb_ref, mg_ref, x1_ref, h2_ref):
        ya = _dot(a_ref[...], wa_ref[...])
        yb = _dot(og_ref[...], wb_ref[...])
        ya_ref[...] = ya
        yb_ref[...] = yb
        merged = (_sigmoid(ga_ref[...]) * ya + _sigmoid(gb_ref[...]) * yb).astype(BF16)
        mg_ref[...] = merged
        x1 = x_ref[...] + _dot(merged, wo_ref[...])
        x1_ref[...] = x1
        _, xh = _rms_stats(x1)
        h2_ref[...] = (xh * g_ref[...]).astype(BF16)

    tok = pl.BlockSpec((tm, D_MODEL), lambda m: (m, 0))
    wsp = pl.BlockSpec((D_MODEL, D_MODEL), lambda m: (0, 0))
    return pl.pallas_call(
        body, name="branch_out_fwd",
        out_shape=[jax.ShapeDtypeStruct((t, D_MODEL), F32), jax.ShapeDtypeStruct((t, D_MODEL), F32),
                   jax.ShapeDtypeStruct((t, D_MODEL), BF16), jax.ShapeDtypeStruct((t, D_MODEL), F32),
                   jax.ShapeDtypeStruct((t, D_MODEL), BF16)],
        grid=(t // tm,),
        in_specs=[tok, tok, pl.BlockSpec((None, tm, D_MODEL), lambda m: (GATE_POS, m, 0)),
                  pl.BlockSpec((None, tm, D_MODEL), lambda m: (GATE_POS + 1, m, 0)), tok, wsp, wsp, wsp,
                  pl.BlockSpec((1, D_MODEL), lambda m: (0, 0))],
        out_specs=[tok] * 5,
        compiler_params=_params([((tm, D_MODEL), BF16)] * 4 + [((tm, D_MODEL), F32)] * 6 + [((D_MODEL, D_MODEL), BF16)] * 3,
                                temps=8 << 20, sem=("arbitrary",)),
    )(a, og, proj, proj, x, w_a, w_b, w_out, ffn_g)


def _ffn_forward(h2, x1, w_gu, w_down):
    t = x1.shape[0]
    tm = _tile(t, 512)

    def body(h_ref, wg_ref, wu_ref, wd_ref, x1_ref, gu_ref, act_ref, x2_ref, acc):
        j = pl.program_id(1)
        h = h_ref[...]
        gate = _dot(h, wg_ref[...])
        up = _dot(h, wu_ref[...])
        gu_ref[0] = gate
        gu_ref[1] = up
        act = ((gate * _sigmoid(gate)) * up).astype(BF16)
        act_ref[...] = act
        part = _dot(act, wd_ref[...])

        @pl.when(j == 0)
        def _():
            acc[...] = part

        @pl.when(j > 0)
        def _():
            acc[...] += part

        @pl.when(j == 3)
        def _():
            x2_ref[...] = x1_ref[...] + acc[...]

    tok = pl.BlockSpec((tm, D_MODEL), lambda m, j: (m, 0))
    return pl.pallas_call(
        body, name="ffn_fwd",
        out_shape=[jax.ShapeDtypeStruct((4, 2, t, FF_BLOCK), F32), jax.ShapeDtypeStruct((4, t, FF_BLOCK), BF16),
                   jax.ShapeDtypeStruct((t, D_MODEL), F32)],
        grid=(t // tm, 4),
        in_specs=[tok, pl.BlockSpec((None, D_MODEL, FF_BLOCK), lambda m, j: (j, 0, 0)),
                  pl.BlockSpec((None, D_MODEL, FF_BLOCK), lambda m, j: (j + 4, 0, 0)),
                  pl.BlockSpec((FF_BLOCK, D_MODEL), lambda m, j: (j, 0)), tok],
        out_specs=[pl.BlockSpec((None, 2, tm, FF_BLOCK), lambda m, j: (j, 0, m, 0)),
                   pl.BlockSpec((None, tm, FF_BLOCK), lambda m, j: (j, m, 0)), tok],
        scratch_shapes=[pltpu.VMEM((tm, D_MODEL), F32)],
        compiler_params=_params([((tm, D_MODEL), BF16), ((D_MODEL, 768), BF16), ((D_MODEL, 768), BF16),
                                 ((FF_BLOCK, D_MODEL), BF16), ((tm, D_MODEL), F32), ((2, tm, 768), F32),
                                 ((tm, 768), BF16), ((tm, D_MODEL), F32)],
                                scratch=[((tm, D_MODEL), F32)], temps=8 << 20, sem=("arbitrary", "arbitrary")),
    )(h2, w_gu, w_gu, w_down, x1)


def _loss_and_final_backward(x2, target, final_g):
    t = x2.shape[0]
    tm = _tile(t, 256)

    def body(x_ref, t_ref, g_ref, loss_ref, dg_ref, dx_ref, dxb_ref):
        @pl.when(pl.program_id(0) == 0)
        def _():
            loss_ref[...] = jnp.zeros_like(loss_ref)
            dg_ref[...] = jnp.zeros_like(dg_ref)

        x = x_ref[...]
        g = g_ref[...]
        r, xh = _rms_stats(x)
        err = xh * g - t_ref[...]
        loss_ref[...] += 0.5 * jnp.sum(jnp.mean(err * err, axis=-1, keepdims=True), axis=0, keepdims=True)
        dy = err * (1.0 / D_MODEL)
        dg_ref[...] += jnp.sum(dy * xh, axis=0, keepdims=True)
        dxh = dy * g
        dx = r * (dxh - xh * jnp.mean(dxh * xh, axis=-1, keepdims=True))
        dx_ref[...] = dx
        dxb_ref[...] = dx.astype(BF16)

    tok = pl.BlockSpec((tm, D_MODEL), lambda m: (m, 0))
    vec = pl.BlockSpec((1, D_MODEL), lambda m: (0, 0))
    return pl.pallas_call(
        body, name="loss_final_bwd",
        out_shape=[jax.ShapeDtypeStruct((8, 128), F32), jax.ShapeDtypeStruct((1, D_MODEL), F32),
                   jax.ShapeDtypeStruct((t, D_MODEL), F32), jax.ShapeDtypeStruct((t, D_MODEL), BF16)],
        grid=(t // tm,), in_specs=[tok, tok, vec],
        out_specs=[pl.BlockSpec((8, 128), lambda m: (0, 0)), vec, tok, tok],
        compiler_params=_params([((tm, D_MODEL), F32)] * 4, temps=8 << 20, sem=("arbitrary",)),
    )(x2, target, final_g)


def _ffn_backward(dx2b, dx2, gu, x1, w_gu, w_down, ffn_g):
    t = x1.shape[0]
    tm = _tile(t, 512)

    def body(dxb_ref, dx2_ref, gu_ref, x1_ref, wg_ref, wu_ref, wd_ref, g_ref, dgu_ref, dx1_ref, dx1b_ref, dg_ref, acc):
        m, j = pl.program_id(0), pl.program_id(1)

        @pl.when((m == 0) & (j == 0))
        def _():
            dg_ref[...] = jnp.zeros_like(dg_ref)

        dact = _dot_nt(dxb_ref[...], wd_ref[...])
        gate, up = gu_ref[0], gu_ref[1]
        sg = _sigmoid(gate)
        dgate = (dact * up * (sg * (1.0 + gate * (1.0 - sg)))).astype(BF16)
        dup = (dact * (gate * sg)).astype(BF16)
        dgu_ref[0] = dgate
        dgu_ref[1] = dup
        part = _dot_nt(dgate, wg_ref[...]) + _dot_nt(dup, wu_ref[...])

        @pl.when(j == 0)
        def _():
            acc[...] = part

        @pl.when(j > 0)
        def _():
            acc[...] += part

        @pl.when(j == 3)
        def _():
            dx, dg = _rms_bwd(acc[...], x1_ref[...], g_ref[...])
            dx1 = dx2_ref[...] + dx
            dx1_ref[...] = dx1
            dx1b_ref[...] = dx1.astype(BF16)
            dg_ref[...] += dg

    tok = pl.BlockSpec((tm, D_MODEL), lambda m, j: (m, 0))
    vec = pl.BlockSpec((1, D_MODEL), lambda m, j: (0, 0))
    gu_spec = pl.BlockSpec((None, 2, tm, FF_BLOCK), lambda m, j: (j, 0, m, 0))
    return pl.pallas_call(
        body, name="ffn_bwd",
        out_shape=[jax.ShapeDtypeStruct((4, 2, t, FF_BLOCK), BF16), jax.ShapeDtypeStruct((t, D_MODEL), F32),
                   jax.ShapeDtypeStruct((t, D_MODEL), BF16), jax.ShapeDtypeStruct((1, D_MODEL), F32)],
        grid=(t // tm, 4),
        in_specs=[tok, tok, gu_spec, tok, pl.BlockSpec((None, D_MODEL, FF_BLOCK), lambda m, j: (j, 0, 0)),
                  pl.BlockSpec((None, D_MODEL, FF_BLOCK), lambda m, j: (j + 4, 0, 0)),
                  pl.BlockSpec((FF_BLOCK, D_MODEL), lambda m, j: (j, 0)), vec],
        out_specs=[gu_spec, tok, tok, vec],
        scratch_shapes=[pltpu.VMEM((tm, D_MODEL), F32)],
        compiler_params=_params([((tm, D_MODEL), BF16), ((tm, D_MODEL), F32), ((2, tm, 768), F32), ((tm, D_MODEL), F32),
                                 ((D_MODEL, 768), BF16), ((D_MODEL, 768), BF16), ((FF_BLOCK, D_MODEL), BF16),
                                 ((2, tm, 768), BF16), ((tm, D_MODEL), F32), ((tm, D_MODEL), BF16)],
                                scratch=[((tm, D_MODEL), F32)], temps=8 << 20, sem=("arbitrary", "arbitrary")),
    )(dx2b, dx2, gu, x1, w_gu, w_gu, w_down, ffn_g)


def _branch_out_backward(dx1b, ya, yb, proj, w_a, w_b, w_out):
    t = ya.shape[0]
    tm = _tile(t, 256)

    def body(dx_ref, ya_ref, yb_ref, ga_ref, gb_ref, wa_ref, wb_ref, wo_ref, dya_ref, dyb_ref, dgate_ref, da_ref, dog_ref):
        dm = _dot_nt(dx_ref[...], wo_ref[...])
        sa, sb = _sigmoid(ga_ref[...]), _sigmoid(gb_ref[...])
        dya = (dm * sa).astype(BF16)
        dyb = (dm * sb).astype(BF16)
        dya_ref[...] = dya
        dyb_ref[...] = dyb
        dgate_ref[0] = (dm * ya_ref[...] * (sa * (1.0 - sa))).astype(BF16)
        dgate_ref[1] = (dm * yb_ref[...] * (sb * (1.0 - sb))).astype(BF16)
        da_ref[...] = _dot_nt(dya, wa_ref[...])
        dog_ref[...] = _dot_nt(dyb, wb_ref[...])

    tok = pl.BlockSpec((tm, D_MODEL), lambda m: (m, 0))
    wsp = pl.BlockSpec((D_MODEL, D_MODEL), lambda m: (0, 0))
    return pl.pallas_call(
        body, name="branch_out_bwd",
        out_shape=[jax.ShapeDtypeStruct((t, D_MODEL), BF16), jax.ShapeDtypeStruct((t, D_MODEL), BF16),
                   jax.ShapeDtypeStruct((2, t, D_MODEL), BF16), jax.ShapeDtypeStruct((t, D_MODEL), F32),
                   jax.ShapeDtypeStruct((t, D_MODEL), F32)],
        grid=(t // tm,),
        in_specs=[tok, tok, tok, pl.BlockSpec((None, tm, D_MODEL), lambda m: (GATE_POS, m, 0)),
                  pl.BlockSpec((None, tm, D_MODEL), lambda m: (GATE_POS + 1, m, 0)), wsp, wsp, wsp],
        out_specs=[tok, tok, pl.BlockSpec((2, tm, D_MODEL), lambda m: (0, m, 0)), tok, tok],
        compiler_params=_params([((tm, D_MODEL), BF16)] * 5 + [((tm, D_MODEL), F32)] * 6 + [((D_MODEL, D_MODEL), BF16)] * 3,
                                temps=8 << 20, sem=("arbitrary",)),
    )(dx1b, ya, yb, proj, proj, w_a, w_b, w_out)


def _hgrn_backward(dog, o_saved, states, proj, lb_table, norm_g):
    t = proj.shape[1]
    tb = _tile(t, 256)
    nc = tb // HGRN_CHUNK
    nb = t // tb

    def body(dog_ref, o_ref, st_ref, q_ref, f_ref, i_ref, g_ref, tab_ref, ng_ref, dp_ref, dng_ref, dtab_ref, gstate):
        @pl.when(pl.program_id(1) == 0)
        def _():
            gstate[...] = jnp.zeros_like(gstate)
            dng_ref[...] = jnp.zeros_like(dng_ref)
            dtab_ref[...] = jnp.zeros_like(dtab_ref)

        lb = _lower_bound(tab_ref)
        ng = ng_ref[...]
        lower, upper = _tri_masks()
        row = lax.broadcasted_iota(jnp.int32, (HGRN_CHUNK, HEAD_DIM), 0)
        dng = jnp.zeros((1, HEAD_DIM), F32)
        dlb = jnp.zeros((1, HEAD_DIM), F32)
        for c in reversed(range(nc)):
            rows = slice(c * HGRN_CHUNK, (c + 1) * HGRN_CHUNK)
            ch = _hgrn_chunk(q_ref[rows, :], f_ref[rows, :], lb)
            v = i_ref[rows, :]
            vb = v.astype(BF16)
            st = st_ref[c]
            stb = st.astype(BF16)
            qib, kib, kdb, qeb = (ch[n].astype(BF16) for n in ("qi", "ki", "kd", "qe"))
            p = jnp.where(lower, _dot_nt(qib, kib), 0.0)
            o, gz, d_og = o_ref[rows, :], g_ref[rows, :], dog_ref[rows, :]
            r, oh = _rms_stats(o)
            sg = _sigmoid(gz)
            d_on = d_og * (gz * sg)
            dgz = d_og * (oh * ng) * (sg * (1.0 + gz * (1.0 - sg)))
            dng = dng + jnp.sum(d_on * oh, axis=0, keepdims=True)
            doh = d_on * ng
            do = r * (doh - oh * jnp.mean(doh * oh, axis=-1, keepdims=True))
            dob = do.astype(BF16)
            gt = gstate[...]
            gtb = gt.astype(BF16)
            dv = _dot_tn(p.astype(BF16), dob) + _dot_nt(kdb, gtb)
            dp = jnp.where(lower, _dot_nt(dob, vb), 0.0).astype(BF16)
            dqi = _dot(dp, kib)
            dki = _dot_tn(dp, qib)
            dqe = _dot(dob, stb)
            dkd = _dot(vb, gtb)
            ddecay = jnp.sum(gt * st, axis=0, keepdims=True)
            decay = jnp.exp(ch["a_last"])
            gstate[...] = gt * decay + _dot_tn(dob, qeb)
            dqs = dqi * ch["e_in"] + dqe * ch["e_all"]
            dk = dki * ch["e_out"] + dkd * ch["e_end"]
            t_in, t_out, t_end = dqi * ch["qi"], dki * ch["ki"], dkd * ch["kd"]
            da = t_in - t_out + dqe * ch["qe"] - t_end
            da_mid = jnp.sum(t_out - t_in, axis=0, keepdims=True)
            da_last = jnp.sum(t_end, axis=0, keepdims=True) + ddecay * decay
            da = da + jnp.where(row == HGRN_CHUNK // 2 - 1, da_mid, 0.0) + jnp.where(row == HGRN_CHUNK - 1, da_last, 0.0)
            dlogf = _mask_mm(upper.astype(BF16), da)
            df = dlogf / ch["f"] - dk
            s = ch["s"]
            dlb = dlb + jnp.sum(df * (1.0 - s), axis=0, keepdims=True)
            dp_ref[0, rows, :] = (dqs * HGRN_SCALE).astype(BF16)
            dp_ref[1, rows, :] = (df * (1.0 - lb) * (s * (1.0 - s))).astype(BF16)
            dp_ref[2, rows, :] = dv.astype(BF16)
            dp_ref[3, rows, :] = dgz.astype(BF16)
        dng_ref[...] += dng
        dt0 = dlb * (lb * (1.0 - lb))
        dtab_ref[0:1, :] += dt0
        dtab_ref[1:2, :] -= dt0

    def blk(p):
        return pl.BlockSpec((None, tb, HEAD_DIM), lambda h, n: (p, nb - 1 - n, h))

    tok = pl.BlockSpec((tb, HEAD_DIM), lambda h, n: (nb - 1 - n, h))
    return pl.pallas_call(
        body, name="hgrn_bwd",
        out_shape=[jax.ShapeDtypeStruct((N_DEV, t, D_MODEL), BF16), jax.ShapeDtypeStruct((1, D_MODEL), F32),
                   jax.ShapeDtypeStruct((2, D_MODEL), F32)],
        grid=(HEADS, nb),
        in_specs=[tok, tok, pl.BlockSpec((None, nc, HEAD_DIM, HEAD_DIM), lambda h, n: (h, nb - 1 - n, 0, 0)),
                  blk(Q_POS), blk(Q_POS + 1), blk(Q_POS + 2), blk(Q_POS + 3),
                  pl.BlockSpec((2, HEAD_DIM), lambda h, n: (0, h)), pl.BlockSpec((1, HEAD_DIM), lambda h, n: (0, h))],
        out_specs=[pl.BlockSpec((4, tb, HEAD_DIM), lambda h, n: (0, nb - 1 - n, h)),
                   pl.BlockSpec((1, HEAD_DIM), lambda h, n: (0, h)), pl.BlockSpec((2, HEAD_DIM), lambda h, n: (0, h))],
        scratch_shapes=[pltpu.VMEM((HEAD_DIM, HEAD_DIM), F32)],
        compiler_params=_params([((tb, HEAD_DIM), F32)] * 6 + [((nc, HEAD_DIM, HEAD_DIM), F32)] + [((4, tb, HEAD_DIM), BF16)],
                                temps=8 << 20, sem=("arbitrary", "arbitrary")),
    )(dog, o_saved, states, proj, proj, proj, proj, lb_table, norm_g)


def _gmlp_backward(dproj, da, proj, ln_g, ln_b, w_s, bias_b):
    t = proj.shape[1]
    tm = _tile(t, 256)
    chunks = tm // GMLP_CHUNK

    def body(_, da_ref, u_ref, v_ref, lng_ref, lnb_ref, ws_ref, bias_ref, dp_ref, dlng_ref, dlnb_ref, dws_ref, dbs_ref,
             vn_scr, dvn_scr):
        @pl.when(pl.program_id(0) == 0)
        def _():
            dlng_ref[...] = jnp.zeros_like(dlng_ref)
            dlnb_ref[...] = jnp.zeros_like(dlnb_ref)
            dws_ref[...] = jnp.zeros_like(dws_ref)
            dbs_ref[...] = jnp.zeros_like(dbs_ref)

        v = v_ref[...]
        vv = _gelu(v)
        mu = jnp.mean(vv, axis=-1, keepdims=True)
        cen = vv - mu
        rstd = lax.rsqrt(jnp.mean(cen * cen, axis=-1, keepdims=True) + NORM_EPS)
        vhat = cen * rstd
        lng = lng_ref[...]
        vn_scr[...] = (vhat * lng + lnb_ref[...]).astype(BF16)
        row = lax.broadcasted_iota(jnp.int32, (GMLP_CHUNK, GMLP_CHUNK), 0)
        col = lax.broadcasted_iota(jnp.int32, (GMLP_CHUNK, GMLP_CHUNK), 1)
        for g in range(GROUPS):
            wm = _masked_ws(ws_ref, g)
            cols = slice(g * HEAD_DIM, (g + 1) * HEAD_DIM)
            dws = jnp.zeros((GMLP_CHUNK, GMLP_CHUNK), F32)
            dbs = jnp.zeros((GMLP_CHUNK, GMLP_CHUNK), F32)
            for c in range(chunks):
                rows = slice(c * GMLP_CHUNK, (c + 1) * GMLP_CHUNK)
                vn = vn_scr[rows, cols]
                mixed = _dot(wm, vn) + bias_ref[g]
                u = u_ref[rows, cols]
                d_a = da_ref[rows, cols]
                dp_ref[0, rows, cols] = (d_a * mixed * _gelu_grad(u)).astype(BF16)
                dmix = d_a * _gelu(u)
                dmb = dmix.astype(BF16)
                dbs = dbs + dmix
                dws = dws + _dot_nt(dmb, vn)
                dvn_scr[rows, cols] = _dot_tn(wm, dmb)
            dws_ref[g] += jnp.where(row >= col, dws, 0.0)
            dbs_ref[g] += jnp.broadcast_to(jnp.sum(dbs, axis=-1, keepdims=True), (GMLP_CHUNK, GMLP_CHUNK))
        dvn = dvn_scr[...]
        dlng_ref[...] += jnp.sum(dvn * vhat, axis=0, keepdims=True)
        dlnb_ref[...] += jnp.sum(dvn, axis=0, keepdims=True)
        dvh = dvn * lng
        dvv = rstd * (dvh - jnp.mean(dvh, axis=-1, keepdims=True) - vhat * jnp.mean(dvh * vhat, axis=-1, keepdims=True))
        dp_ref[1] = (dvv * _gelu_grad(v)).astype(BF16)

    tok = pl.BlockSpec((tm, D_MODEL), lambda m: (m, 0))
    small = pl.BlockSpec((GROUPS, GMLP_CHUNK, GMLP_CHUNK), lambda m: (0, 0, 0))
    vec = pl.BlockSpec((1, D_MODEL), lambda m: (0, 0))
    return pl.pallas_call(
        body, name="gmlp_bwd",
        out_shape=[jax.ShapeDtypeStruct(dproj.shape, BF16), jax.ShapeDtypeStruct((1, D_MODEL), F32),
                   jax.ShapeDtypeStruct((1, D_MODEL), F32), jax.ShapeDtypeStruct((GROUPS, GMLP_CHUNK, GMLP_CHUNK), F32),
                   jax.ShapeDtypeStruct((GROUPS, GMLP_CHUNK, GMLP_CHUNK), F32)],
        grid=(t // tm,),
        in_specs=[ANY, tok, pl.BlockSpec((None, tm, D_MODEL), lambda m: (U_POS, m, 0)),
                  pl.BlockSpec((None, tm, D_MODEL), lambda m: (U_POS + 1, m, 0)), vec, vec, small, small],
        out_specs=[pl.BlockSpec((2, tm, D_MODEL), lambda m: (U_POS // 2, m, 0)), vec, vec, small, small],
        scratch_shapes=[pltpu.VMEM((tm, D_MODEL), BF16), pltpu.VMEM((tm, D_MODEL), F32)],
        input_output_aliases={0: 0},
        compiler_params=_params([((tm, D_MODEL), F32)] * 3 + [((2, tm, D_MODEL), BF16)] + [((8, 128, 128), F32)] * 4,
                                scratch=[((tm, D_MODEL), BF16), ((tm, D_MODEL), F32)], temps=12 << 20, sem=("arbitrary",)),
    )(dproj, da, proj, proj, ln_g, ln_b, w_s, bias_b)


def _place_gate_grads(dproj, dgates):
    t = dgates.shape[1]
    tm = _tile(t, 512)

    def body(_, src_ref, dst_ref):
        dst_ref[...] = src_ref[...]

    return pl.pallas_call(
        body, name="place_gate_grads", out_shape=jax.ShapeDtypeStruct(dproj.shape, BF16), grid=(t // tm,),
        in_specs=[ANY, pl.BlockSpec((2, tm, D_MODEL), lambda m: (0, m, 0))],
        out_specs=pl.BlockSpec((2, tm, D_MODEL), lambda m: (GATE_POS // 2, m, 0)),
        input_output_aliases={0: 0},
        compiler_params=_params([((2, tm, D_MODEL), BF16)] * 2, sem=("arbitrary",)),
    )(dproj, dgates)


def _input_backward(dproj, w_in_g, x, dx1, mix_g):
    t = x.shape[0]
    tm = _tile(t, 512)

    def body(dp_ref, w_ref, x_ref, dx1_ref, g_ref, dx_ref, dg_ref, acc):
        m, p = pl.program_id(0), pl.program_id(1)

        @pl.when((m == 0) & (p == 0))
        def _():
            dg_ref[...] = jnp.zeros_like(dg_ref)

        part = _dot_nt(dp_ref[...], w_ref[...])

        @pl.when(p == 0)
        def _():
            acc[...] = part

        @pl.when(p > 0)
        def _():
            acc[...] += part

        @pl.when(p == N_DEV - 1)
        def _():
            dx, dg = _rms_bwd(acc[...], x_ref[...], g_ref[...])
            dx_ref[...] = dx1_ref[...] + dx
            dg_ref[...] += dg

    tok = pl.BlockSpec((tm, D_MODEL), lambda m, p: (m, 0))
    vec = pl.BlockSpec((1, D_MODEL), lambda m, p: (0, 0))
    return pl.pallas_call(
        body, name="input_bwd",
        out_shape=[jax.ShapeDtypeStruct((t, D_MODEL), F32), jax.ShapeDtypeStruct((1, D_MODEL), F32)],
        grid=(t // tm, N_DEV),
        in_specs=[pl.BlockSpec((None, tm, D_MODEL), lambda m, p: (p, m, 0)),
                  pl.BlockSpec((None, D_MODEL, D_MODEL), lambda m, p: (p, 0, 0)), tok, tok, vec],
        out_specs=[tok, vec],
        scratch_shapes=[pltpu.VMEM((tm, D_MODEL), F32)],
        compiler_params=_params([((tm, D_MODEL), BF16), ((D_MODEL, D_MODEL), BF16)] + [((tm, D_MODEL), F32)] * 3,
                                scratch=[((tm, D_MODEL), F32)], temps=8 << 20, sem=("arbitrary", "arbitrary")),
    )(dproj, w_in_g, x, dx1, mix_g)


def _weight_grad(name, a, b, a_spec, b_spec, out_shape, out_spec, grid, blocks):
    def body(a_ref, b_ref, o_ref):
        part = _dot_tn(a_ref[...], b_ref[...])
        m = pl.program_id(len(grid) - 1)

        @pl.when(m == 0)
        def _():
            o_ref[...] = part

        @pl.when(m > 0)
        def _():
            o_ref[...] += part

    return pl.pallas_call(
        body, name=name, out_shape=jax.ShapeDtypeStruct(out_shape, F32), grid=grid, in_specs=[a_spec, b_spec],
        out_specs=out_spec, compiler_params=_params(blocks, temps=8 << 20, sem=("arbitrary",) * len(grid)),
    )(a, b)


def _pack_small(mix_g, ln_g, ln_b, w_s, b_s, lb_table, hg_norm, ffn_g, final_g):
    def part(a):
        a = a.reshape(-1, D_MODEL)
        return jnp.pad(a, ((0, 8 - a.shape[0]), (0, 0)))

    return jnp.concatenate([part(mix_g), part(ln_g), part(ln_b), part(hg_norm), part(ffn_g), part(final_g),
                            part(lb_table), part(b_s), w_s.reshape(GMLP_CHUNK, D_MODEL)], axis=0)


def _unpack_small(pack):
    return dict(norm_mix_g=pack[0:1], gmlp_ln_g=pack[8:9], gmlp_ln_b=pack[16:17], hgrn_norm_g=pack[24:25],
                norm_ffn_g=pack[32:33], norm_final_g=pack[40], hgrn_lb_table=pack[48:50],
                gmlp_b_s=pack[56:57].reshape(1, GROUPS, GMLP_CHUNK),
                gmlp_w_s=pack[64:192].reshape(1, GROUPS, GMLP_CHUNK, GMLP_CHUNK))


def _adamw_small(gathered, w, m, v):
    rows = w.shape[0]

    def body(p_ref, w_ref, m_ref, v_ref, g_out, d_out, m_out, v_out):
        g = p_ref[0]
        for j in range(1, N_DEV):
            g = g + p_ref[j]
        delta, m_new, v_new = _adamw_math(w_ref[...], g, m_ref[...], v_ref[...])
        g_out[...] = g
        d_out[...] = delta
        m_out[...] = m_new
        v_out[...] = v_new

    tr = 64
    spec = pl.BlockSpec((tr, D_MODEL), lambda r: (r, 0))
    return pl.pallas_call(
        body, name="adamw_small", out_shape=[jax.ShapeDtypeStruct((rows, D_MODEL), F32)] * 4, grid=(rows // tr,),
        in_specs=[pl.BlockSpec((N_DEV, tr, D_MODEL), lambda r: (0, r, 0)), spec, spec, spec], out_specs=[spec] * 4,
        compiler_params=_params([((N_DEV, tr, D_MODEL), F32)] + [((tr, D_MODEL), F32)] * 7, sem=("arbitrary",)),
    )(gathered, w, m, v)


def kernel(x, norm_mix_g, w_in, gmlp_ln_g, gmlp_ln_b, gmlp_w_s, gmlp_b_s, hgrn_lb_table, hgrn_norm_g, w_branch_a, w_branch_b, w_out, norm_ffn_g, w_gate_up, w_down, norm_final_g, loss_target, m_norm_mix_g, m_w_in, m_gmlp_ln_g, m_gmlp_ln_b, m_gmlp_w_s, m_gmlp_b_s, m_hgrn_lb_table, m_hgrn_norm_g, m_w_branch_a, m_w_branch_b, m_w_out, m_norm_ffn_g, m_w_gate_up, m_w_down, m_norm_final_g, v_norm_mix_g, v_w_in, v_gmlp_ln_g, v_gmlp_ln_b, v_gmlp_w_s, v_gmlp_b_s, v_hgrn_lb_table, v_hgrn_norm_g, v_w_branch_a, v_w_branch_b, v_w_out, v_norm_ffn_g, v_w_gate_up, v_w_down, v_norm_final_g):
    t = x.shape[1]
    x2d = x.reshape(t, D_MODEL)
    target = loss_target.reshape(t, D_MODEL)
    final_g = norm_final_g.reshape(1, D_MODEL)

    shards = [w_in[0].astype(BF16), w_branch_a[0].astype(BF16), w_branch_b[0].astype(BF16), w_out[0].astype(BF16),
              w_gate_up[0].astype(BF16), w_down[0].astype(BF16)]

    def rows_of(n):
        return lambda ref, j: ref.at[pl.ds(pl.multiple_of(j * n, 8), n)]

    w_in_g, w_a, w_b, w_o, w_gu, w_dn = _all_gather(
        "weights_all_gather", shards,
        [((N_DEV, D_MODEL, D_MODEL), BF16), ((D_MODEL, D_MODEL), BF16), ((D_MODEL, D_MODEL), BF16),
         ((D_MODEL, D_MODEL), BF16), ((N_DEV, D_MODEL, FF_BLOCK), BF16), ((D_FF, D_MODEL), BF16)],
        [lambda ref, j: ref.at[_pos_of_dev(j)], rows_of(BRANCH_ROWS), rows_of(BRANCH_ROWS), rows_of(BRANCH_ROWS),
         lambda ref, j: ref.at[j], rows_of(DOWN_ROWS)])

    h = _rms_forward(x2d, norm_mix_g)
    proj = _proj_forward(h, w_in_g)
    bias_b = jnp.broadcast_to(gmlp_b_s[0][:, :, None], (GROUPS, GMLP_CHUNK, GMLP_CHUNK))
    a = _gmlp_forward(proj, gmlp_ln_g, gmlp_ln_b, gmlp_w_s[0], bias_b)
    og, o_saved, states = _hgrn_forward(proj, hgrn_lb_table, hgrn_norm_g)
    ya, yb, merged, x1, h2 = _branch_out_forward(a, og, proj, x2d, w_a, w_b, w_o, norm_ffn_g)
    gu, act, x2 = _ffn_forward(h2, x1, w_gu, w_dn)
    loss_tile, d_final_g, dx2, dx2b = _loss_and_final_backward(x2, target, final_g)

    dgu, dx1, dx1b, d_ffn_g = _ffn_backward(dx2b, dx2, gu, x1, w_gu, w_dn, norm_ffn_g)
    dya, dyb, dgates, da, dog = _branch_out_backward(dx1b, ya, yb, proj, w_a, w_b, w_o)
    dproj, d_hg_norm, d_lb = _hgrn_backward(dog, o_saved, states, proj, hgrn_lb_table, hgrn_norm_g)
    dproj, d_ln_g, d_ln_b, d_ws, d_bs = _gmlp_backward(dproj, da, proj, gmlp_ln_g, gmlp_ln_b, gmlp_w_s[0], bias_b)
    dproj = _place_gate_grads(dproj, dgates)
    grad_x, d_mix_g = _input_backward(dproj, w_in_g, x2d, dx1, norm_mix_g)

    tm = _tile(t, 512)
    nm = t // tm
    tok_a = pl.BlockSpec((tm, D_MODEL), lambda m: (m, 0))
    full_o = pl.BlockSpec((D_MODEL, D_MODEL), lambda m: (0, 0))
    sq_blocks = [((tm, D_MODEL), BF16)] * 2 + [((D_MODEL, D_MODEL), F32)]
    g_in = _weight_grad(
        "grad_w_in", h, dproj, pl.BlockSpec((tm, D_MODEL), lambda p, m: (m, 0)),
        pl.BlockSpec((None, tm, D_MODEL), lambda p, m: (p, m, 0)), (N_DEV, D_MODEL, D_MODEL),
        pl.BlockSpec((None, D_MODEL, D_MODEL), lambda p, m: (p, 0, 0)), (N_DEV, nm), sq_blocks)
    g_a = _weight_grad("grad_w_a", a, dya, tok_a, tok_a, (D_MODEL, D_MODEL), full_o, (nm,), sq_blocks)
    g_b = _weight_grad("grad_w_b", og, dyb, tok_a, tok_a, (D_MODEL, D_MODEL), full_o, (nm,), sq_blocks)
    g_o = _weight_grad("grad_w_out", merged, dx1b, tok_a, tok_a, (D_MODEL, D_MODEL), full_o, (nm,), sq_blocks)
    g_gu = _weight_grad(
        "grad_w_gate_up", h2, dgu, pl.BlockSpec((tm, D_MODEL), lambda j, m: (m, 0)),
        pl.BlockSpec((None, None, tm, FF_BLOCK), lambda j, m: (j % 4, j // 4, m, 0)), (N_DEV, D_MODEL, FF_BLOCK),
        pl.BlockSpec((None, D_MODEL, FF_BLOCK), lambda j, m: (j, 0, 0)), (N_DEV, nm),
        [((tm, D_MODEL), BF16), ((tm, 768), BF16), ((D_MODEL, 768), F32)])
    g_dn = _weight_grad(
        "grad_w_down", act, dx2b, pl.BlockSpec((None, tm, FF_BLOCK), lambda j, m: (j, m, 0)),
        pl.BlockSpec((tm, D_MODEL), lambda j, m: (m, 0)), (D_FF, D_MODEL),
        pl.BlockSpec((FF_BLOCK, D_MODEL), lambda j, m: (j, 0)), (4, nm),
        [((tm, 768), BF16), ((tm, D_MODEL), BF16), ((FF_BLOCK, D_MODEL), F32)])

    shard_fns = [lambda ref, j: ref.at[_pos_of_dev(j)], rows_of(BRANCH_ROWS), rows_of(BRANCH_ROWS), rows_of(BRANCH_ROWS),
                 lambda ref, j: ref.at[j], rows_of(DOWN_ROWS)]
    shard_shapes = [(D_MODEL, D_MODEL), (BRANCH_ROWS, D_MODEL), (BRANCH_ROWS, D_MODEL), (BRANCH_ROWS, D_MODEL),
                    (D_MODEL, FF_BLOCK), (DOWN_ROWS, D_MODEL)]
    names = ["w_in", "w_branch_a", "w_branch_b", "w_out", "w_gate_up", "w_down"]
    own, land = _exchange_sibling("grads_to_sibling", [g_in, g_a, g_b, g_o, g_gu, g_dn], shard_fns, shard_shapes)
    partials = [_chip_partial("chip_partial_" + nme, o_, l_) for nme, o_, l_ in zip(names, own, land)]
    reduced = _exchange_chips("grads_to_chips", partials)
    big = {}
    for nme, parts, w, m, v in zip(
            names, reduced, [w_in, w_branch_a, w_branch_b, w_out, w_gate_up, w_down],
            [m_w_in, m_w_branch_a, m_w_branch_b, m_w_out, m_w_gate_up, m_w_down],
            [v_w_in, v_w_branch_a, v_w_branch_b, v_w_out, v_w_gate_up, v_w_down]):
        outs = _adamw("adamw_" + nme, parts, w[0], m[0], v[0])
        big[nme] = [o_[None] for o_ in outs]

    d_bs_row = d_bs[:, :, 0]
    small_partial = _pack_small(d_mix_g, d_ln_g, d_ln_b, d_ws, d_bs_row, d_lb, d_hg_norm, d_ffn_g, d_final_g)
    (small_all,) = _all_gather("small_grads_all_gather", [small_partial], [((N_DEV, SMALL_ROWS, D_MODEL), F32)],
                               [lambda ref, j: ref.at[j]])

    def packed(prefix_vals):
        return _pack_small(*prefix_vals)

    w_pack = packed([norm_mix_g, gmlp_ln_g, gmlp_ln_b, gmlp_w_s, gmlp_b_s, hgrn_lb_table, hgrn_norm_g, norm_ffn_g, norm_final_g])
    m_pack = packed([m_norm_mix_g, m_gmlp_ln_g, m_gmlp_ln_b, m_gmlp_w_s, m_gmlp_b_s, m_hgrn_lb_table, m_hgrn_norm_g, m_norm_ffn_g, m_norm_final_g])
    v_pack = packed([v_norm_mix_g, v_gmlp_ln_g, v_gmlp_ln_b, v_gmlp_w_s, v_gmlp_b_s, v_hgrn_lb_table, v_hgrn_norm_g, v_norm_ffn_g, v_norm_final_g])
    small = [_unpack_small(p) for p in _adamw_small(small_all, w_pack, m_pack, v_pack)]

    loss = lax.psum(loss_tile[0, 0], ("x", "y", "c"))
    order = ["norm_mix_g", "w_in", "gmlp_ln_g", "gmlp_ln_b", "gmlp_w_s", "gmlp_b_s", "hgrn_lb_table", "hgrn_norm_g",
             "w_branch_a", "w_branch_b", "w_out", "norm_ffn_g", "w_gate_up", "w_down", "norm_final_g"]
    outs = [loss, grad_x.reshape(1, t, D_MODEL)]
    for kind in range(4):
        for nme in order:
            outs.append(big[nme][kind] if nme in big else small[kind][nme])
    return tuple(outs)
```

```python
import functools

import jax
import jax.numpy as jnp
from jax import lax
from jax.experimental import pallas as pl
from jax.experimental.pallas import tpu as pltpu

F32, BF16 = jnp.float32, jnp.bfloat16
D_MODEL = 1024
N_DEV = 8
HEADS = 8
HEAD_DIM = 128
GROUPS = 8
GMLP_CHUNK = 128
HGRN_CHUNK = 64
HGRN_SCALE = HEAD_DIM ** -0.5
D_FF = 2816
FF_BLOCK = D_FF // 4
DOWN_ROWS = D_FF // N_DEV
BRANCH_ROWS = D_MODEL // N_DEV
NORM_EPS = 1e-6
ADAM_LR, ADAM_B1, ADAM_B2, ADAM_EPS, ADAM_WD, ADAM_STEP = 0.001, 0.9, 0.999, 1e-08, 0.01, 10
SMALL_ROWS = 192
V7X_VMEM_BYTES = 64 * 1024 * 1024
VMEM_CAP = V7X_VMEM_BYTES - 6 * 1024 * 1024
MESH_ID = pl.DeviceIdType.MESH
ANY = pl.BlockSpec(memory_space=pl.ANY)
Q_POS, U_POS, GATE_POS = 0, 4, 6


def _pos_of_dev(j):
    return jnp.where(j < 2, j + 4, jnp.where(j < 6, j - 2, j))


def _dev_of_pos(p):
    return jnp.where(p < 4, p + 2, jnp.where(p < 6, p - 4, p))


def _nbytes(shape, dtype):
    n = 1
    for s in shape:
        n *= s
    return n * jnp.dtype(dtype).itemsize


def _params(blocks, scratch=(), temps=0, sem=None):
    need = 2 * sum(_nbytes(s, d) for s, d in blocks) + sum(_nbytes(s, d) for s, d in scratch) + temps
    need = min(max(need + (4 << 20), 16 << 20), VMEM_CAP)
    return pltpu.CompilerParams(dimension_semantics=sem, vmem_limit_bytes=int(need))


def _tile(n, pref):
    return pref if n % pref == 0 else n


def _dot(a, b):
    return jnp.dot(a, b, preferred_element_type=F32)


def _dot_nt(a, b):
    return lax.dot_general(a, b, (((1,), (1,)), ((), ())), preferred_element_type=F32)


def _dot_tn(a, b):
    return lax.dot_general(a, b, (((0,), (0,)), ((), ())), preferred_element_type=F32)


def _sigmoid(x):
    return 1.0 / (1.0 + jnp.exp(-x))


_GELU_C = 0.7978845608028654


def _gelu(x):
    return x * (0.5 * (1.0 + jnp.tanh(_GELU_C * (x + 0.044715 * (x * x * x)))))


def _gelu_grad(x):
    t = jnp.tanh(_GELU_C * (x + 0.044715 * (x * x * x)))
    return 0.5 * (1.0 + t) + 0.5 * x * (1.0 - t * t) * (_GELU_C * (1.0 + 3.0 * 0.044715 * x * x))


def _rms_stats(x):
    r = lax.rsqrt(jnp.mean(x * x, axis=-1, keepdims=True) + NORM_EPS)
    return r, x * r


def _rms_bwd(dy, x, g):
    r, xh = _rms_stats(x)
    dg = jnp.sum(dy * xh, axis=0, keepdims=True)
    dxh = dy * g
    dx = r * (dxh - xh * jnp.mean(dxh * xh, axis=-1, keepdims=True))
    return dx, dg


def _split3(x):
    hi = x.astype(BF16)
    r = x - hi.astype(F32)
    mid = r.astype(BF16)
    lo = (r - mid.astype(F32)).astype(BF16)
    return hi, mid, lo


def _mask_mm(mask_bf16, x):
    hi, mid, lo = _split3(x)
    return _dot(mask_bf16, hi) + _dot(mask_bf16, mid) + _dot(mask_bf16, lo)


def _place():
    return lax.axis_index("x"), lax.axis_index("y"), lax.axis_index("c")


def _all_gather(name, srcs, out_shapes, slicers):
    n = len(srcs)

    def body(*refs):
        src, out = refs[:n], refs[n:2 * n]
        send, recv, loc = refs[2 * n:]
        x, y, c = _place()
        me, sib = (x, y, c), (x, y, 1 - c)
        chips = [(1 - x, y), (x, 1 - y), (1 - x, 1 - y)]

        def dev(p):
            return 4 * p[0] + 2 * p[1] + p[2]

        def rc(i, k, block, to, from_src=False):
            dst = slicers[i](out[i], dev(block))
            return pltpu.make_async_remote_copy(
                src_ref=src[i] if from_src else dst, dst_ref=dst, send_sem=send.at[7 * i + k],
                recv_sem=recv.at[7 * i + k], device_id=to, device_id_type=MESH_ID)

        mine = [pltpu.make_async_copy(src[i], slicers[i](out[i], dev(me)), loc.at[i]) for i in range(n)]
        for cp in mine:
            cp.start()
        first = []
        for i in range(n):
            first.append(rc(i, 0, me, sib, True))
            for j, chip in enumerate(chips):
                first.append(rc(i, 1 + j, me, (*chip, c), True))
        for cp in first:
            cp.start()
        passed = []
        for j, chip in enumerate(chips):
            for i in range(n):
                rc(i, 1 + j, (*chip, c), me).wait_recv()
                cp = rc(i, 4 + j, (*chip, c), sib)
                cp.start()
                passed.append(cp)
        for i in range(n):
            rc(i, 0, sib, me).wait_recv()
            for j, chip in enumerate(chips):
                rc(i, 4 + j, (*chip, 1 - c), me).wait_recv()
        for cp in first + passed:
            cp.wait_send()
        for cp in mine:
            cp.wait()

    return pl.pallas_call(
        body, name=name, out_shape=[jax.ShapeDtypeStruct(s, d) for s, d in out_shapes],
        in_specs=[ANY] * n, out_specs=[ANY] * n,
        scratch_shapes=[pltpu.SemaphoreType.DMA((7 * n,)), pltpu.SemaphoreType.DMA((7 * n,)),
                        pltpu.SemaphoreType.DMA((n,))],
    )(*srcs)


def _exchange_sibling(name, grads, shard_fns, shard_shapes):
    n = len(grads)

    def body(*refs):
        g, land = refs[:n], refs[n:2 * n]
        send, recv = refs[2 * n:]
        x, y, c = _place()
        remote = []
        for i in range(n):
            for q in range(4):
                cp = pltpu.make_async_remote_copy(
                    src_ref=shard_fns[i](g[i], 2 * q + (1 - c)), dst_ref=land[i].at[q], send_sem=send.at[4 * i + q],
                    recv_sem=recv.at[4 * i + q], device_id=(x, y, 1 - c), device_id_type=MESH_ID)
                cp.start()
                remote.append(cp)
        for cp in remote:
            cp.wait()

    return pl.pallas_call(
        body, name=name, out_shape=[jax.ShapeDtypeStruct((4, *s), F32) for s in shard_shapes],
        in_specs=[ANY] * n, out_specs=[ANY] * n,
        scratch_shapes=[pltpu.SemaphoreType.DMA((4 * n,)), pltpu.SemaphoreType.DMA((4 * n,))],
    )(*grads)


def _exchange_chips(name, parts):
    n = len(parts)

    def body(*refs):
        part, out = refs[:n], refs[n:2 * n]
        send, recv = refs[2 * n:]
        x, y, c = _place()
        remote = []
        for i in range(n):
            for s in range(3):
                qx = 1 - x if (s + 1) // 2 else x
                qy = 1 - y if (s + 1) % 2 else y
                cp = pltpu.make_async_remote_copy(
                    src_ref=part[i].at[2 * qx + qy], dst_ref=out[i].at[s], send_sem=send.at[3 * i + s],
                    recv_sem=recv.at[3 * i + s], device_id=(qx, qy, c), device_id_type=MESH_ID)
                cp.start()
                remote.append(cp)
        for cp in remote:
            cp.wait()

    return pl.pallas_call(
        body, name=name, out_shape=[jax.ShapeDtypeStruct((3, *p.shape[1:]), p.dtype) for p in parts],
        in_specs=[ANY] * n, out_specs=[ANY] * n,
        scratch_shapes=[pltpu.SemaphoreType.DMA((3 * n,)), pltpu.SemaphoreType.DMA((3 * n,))],
    )(*parts)


def _chip_partial(name, core, grad, own_block, own_index, land):
    _, rows, cols = land.shape
    tr = own_block[-2]

    def body(core_ref, a_ref, b_ref, o_ref):
        o_ref[...] = (a_ref[...] + b_ref[...]).astype(BF16)

    spec = pl.BlockSpec((None, tr, cols), lambda q, r, c: (q, r, 0))
    return pl.pallas_call(
        body, name=name, out_shape=jax.ShapeDtypeStruct(land.shape, BF16),
        grid_spec=pltpu.PrefetchScalarGridSpec(
            num_scalar_prefetch=1, grid=(4, rows // tr),
            in_specs=[pl.BlockSpec(own_block, lambda q, r, c: own_index(q, r, c[0])), spec], out_specs=spec),
        compiler_params=_params([((tr, cols), F32)] * 2 + [((tr, cols), BF16)], sem=("arbitrary", "arbitrary")),
    )(core, grad, land)


def _adamw_math(w, g, m, v):
    m = ADAM_B1 * m + (1.0 - ADAM_B1) * g
    v = ADAM_B2 * v + (1.0 - ADAM_B2) * (g * g)
    m_hat = m / (1.0 - ADAM_B1 ** ADAM_STEP)
    v_hat = v / (1.0 - ADAM_B2 ** ADAM_STEP)
    delta = -ADAM_LR * (m_hat / (jnp.sqrt(v_hat) + ADAM_EPS) + ADAM_WD * w)
    return delta, m, v


def _adamw(name, chip, own, landed, w, m, v):
    _, rows, cols = own.shape
    tr = _tile(rows, 256) if rows % 256 == 0 else _tile(rows, 176)

    def body(chip_ref, own_ref, l_ref, w_ref, m_ref, v_ref, g_out, d_out, m_out, v_out):
        g = own_ref[...].astype(F32)
        for s in range(3):
            g = g + l_ref[s].astype(F32)
        delta, m_new, v_new = _adamw_math(w_ref[...], g, m_ref[...], v_ref[...])
        g_out[...] = g
        d_out[...] = delta
        m_out[...] = m_new
        v_out[...] = v_new

    spec = pl.BlockSpec((tr, cols), lambda r, c: (r, 0))
    return pl.pallas_call(
        body, name=name, out_shape=[jax.ShapeDtypeStruct((rows, cols), F32)] * 4,
        grid_spec=pltpu.PrefetchScalarGridSpec(
            num_scalar_prefetch=1, grid=(rows // tr,),
            in_specs=[pl.BlockSpec((None, tr, cols), lambda r, c: (c[0], r, 0)),
                      pl.BlockSpec((3, tr, cols), lambda r, c: (0, r, 0)), spec, spec, spec],
            out_specs=[spec] * 4),
        compiler_params=_params([((4, tr, cols), own.dtype)] + [((tr, cols), F32)] * 7, sem=("arbitrary",)),
    )(chip, own, landed, w, m, v)


def _rms_forward(x, gain):
    t = x.shape[0]
    tm = _tile(t, 512)

    def body(x_ref, g_ref, h_ref):
        _, xh = _rms_stats(x_ref[...])
        h_ref[...] = (xh * g_ref[...]).astype(BF16)

    return pl.pallas_call(
        body, name="rms_mix_fwd", out_shape=jax.ShapeDtypeStruct((t, D_MODEL), BF16), grid=(t // tm,),
        in_specs=[pl.BlockSpec((tm, D_MODEL), lambda m: (m, 0)), pl.BlockSpec((1, D_MODEL), lambda m: (0, 0))],
        out_specs=pl.BlockSpec((tm, D_MODEL), lambda m: (m, 0)),
        compiler_params=_params([((tm, D_MODEL), F32), ((tm, D_MODEL), BF16)], temps=8 << 20, sem=("arbitrary",)),
    )(x, gain)


def _proj_forward(h, w_in_g):
    t = h.shape[0]
    tm = _tile(t, 1024)

    def body(h_ref, w_ref, o_ref):
        o_ref[...] = _dot(h_ref[...], w_ref[...])

    return pl.pallas_call(
        body, name="proj_fwd", out_shape=jax.ShapeDtypeStruct((N_DEV, t, D_MODEL), F32), grid=(N_DEV, t // tm),
        in_specs=[pl.BlockSpec((tm, D_MODEL), lambda p, m: (m, 0)),
                  pl.BlockSpec((None, D_MODEL, D_MODEL), lambda p, m: (p, 0, 0))],
        out_specs=pl.BlockSpec((None, tm, D_MODEL), lambda p, m: (p, m, 0)),
        compiler_params=_params([((tm, D_MODEL), BF16), ((D_MODEL, D_MODEL), BF16), ((tm, D_MODEL), F32)],
                                sem=("arbitrary", "arbitrary")),
    )(h, w_in_g)


def _masked_ws(ws_ref, g):
    row = lax.broadcasted_iota(jnp.int32, (GMLP_CHUNK, GMLP_CHUNK), 0)
    col = lax.broadcasted_iota(jnp.int32, (GMLP_CHUNK, GMLP_CHUNK), 1)
    return jnp.where(row >= col, ws_ref[g], 0.0).astype(BF16)


def _gmlp_forward(proj, ln_g, ln_b, w_s, bias_b):
    t = proj.shape[1]
    tm = _tile(t, 256)
    chunks = tm // GMLP_CHUNK

    def body(u_ref, v_ref, lng_ref, lnb_ref, ws_ref, bias_ref, a_ref, vn_scr):
        vv = _gelu(v_ref[...])
        mu = jnp.mean(vv, axis=-1, keepdims=True)
        cen = vv - mu
        var = jnp.mean(cen * cen, axis=-1, keepdims=True)
        vn_scr[...] = ((cen * lax.rsqrt(var + NORM_EPS)) * lng_ref[...] + lnb_ref[...]).astype(BF16)
        for g in range(GROUPS):
            wm = _masked_ws(ws_ref, g)
            cols = slice(g * HEAD_DIM, (g + 1) * HEAD_DIM)
            for c in range(chunks):
                rows = slice(c * GMLP_CHUNK, (c + 1) * GMLP_CHUNK)
                mixed = _dot(wm, vn_scr[rows, cols]) + bias_ref[g]
                a_ref[rows, cols] = (_gelu(u_ref[rows, cols]) * mixed).astype(BF16)

    small = pl.BlockSpec((GROUPS, GMLP_CHUNK, GMLP_CHUNK), lambda m: (0, 0, 0))
    vec = pl.BlockSpec((1, D_MODEL), lambda m: (0, 0))
    return pl.pallas_call(
        body, name="gmlp_fwd", out_shape=jax.ShapeDtypeStruct((t, D_MODEL), BF16), grid=(t // tm,),
        in_specs=[pl.BlockSpec((None, tm, D_MODEL), lambda m: (U_POS, m, 0)),
                  pl.BlockSpec((None, tm, D_MODEL), lambda m: (U_POS + 1, m, 0)), vec, vec, small, small],
        out_specs=pl.BlockSpec((tm, D_MODEL), lambda m: (m, 0)),
        scratch_shapes=[pltpu.VMEM((tm, D_MODEL), BF16)],
        compiler_params=_params([((tm, D_MODEL), F32)] * 2 + [((tm, D_MODEL), BF16)] + [((8, 128, 128), F32)] * 2,
                                scratch=[((tm, D_MODEL), BF16)], temps=8 << 20, sem=("arbitrary",)),
    )(proj, proj, ln_g, ln_b, w_s, bias_b)


def _lower_bound(tab_ref):
    t0, t1 = tab_ref[0:1, :], tab_ref[1:2, :]
    mx = jnp.maximum(t0, t1)
    e0, e1 = jnp.exp(t0 - mx), jnp.exp(t1 - mx)
    return e0 / (e0 + e1)


def _tri_masks():
    row = lax.broadcasted_iota(jnp.int32, (HGRN_CHUNK, HGRN_CHUNK), 0)
    col = lax.broadcasted_iota(jnp.int32, (HGRN_CHUNK, HGRN_CHUNK), 1)
    return row >= col, row <= col


def _hgrn_chunk(q, fl, lb):
    lower, _ = _tri_masks()
    s = _sigmoid(fl)
    f = lb + (1.0 - lb) * s
    k = 1.0 - f
    a = _mask_mm(lower.astype(BF16), jnp.log(f))
    a_mid = a[HGRN_CHUNK // 2 - 1:HGRN_CHUNK // 2, :]
    a_last = a[HGRN_CHUNK - 1:HGRN_CHUNK, :]
    qs = q * HGRN_SCALE
    e_in, e_out, e_end, e_all = jnp.exp(a - a_mid), jnp.exp(a_mid - a), jnp.exp(a_last - a), jnp.exp(a)
    return dict(s=s, f=f, k=k, a_last=a_last, e_in=e_in, e_out=e_out, e_end=e_end, e_all=e_all,
                qi=qs * e_in, ki=k * e_out, kd=k * e_end, qe=qs * e_all)


def _hgrn_forward(proj, lb_table, norm_g):
    t = proj.shape[1]
    tb = _tile(t, 256)
    nc = tb // HGRN_CHUNK
    n_chunks = t // HGRN_CHUNK

    def body(q_ref, f_ref, i_ref, g_ref, tab_ref, ng_ref, og_ref, o_ref, st_ref, state):
        @pl.when(pl.program_id(1) == 0)
        def _():
            state[...] = jnp.zeros_like(state)

        lb = _lower_bound(tab_ref)
        lower, _ = _tri_masks()
        for c in range(nc):
            rows = slice(c * HGRN_CHUNK, (c + 1) * HGRN_CHUNK)
            ch = _hgrn_chunk(q_ref[rows, :], f_ref[rows, :], lb)
            vb = i_ref[rows, :].astype(BF16)
            st = state[...]
            st_ref[c] = st
            p = jnp.where(lower, _dot_nt(ch["qi"].astype(BF16), ch["ki"].astype(BF16)), 0.0)
            o = _dot(p.astype(BF16), vb) + _dot_nt(ch["qe"].astype(BF16), st.astype(BF16))
            state[...] = st * jnp.exp(ch["a_last"]) + _dot_tn(vb, ch["kd"].astype(BF16))
            o_ref[rows, :] = o
            r, oh = _rms_stats(o)
            gz = g_ref[rows, :]
            og_ref[rows, :] = ((oh * ng_ref[...]) * (gz * _sigmoid(gz))).astype(BF16)

    def blk(p):
        return pl.BlockSpec((None, tb, HEAD_DIM), lambda h, n: (p, n, h))

    out_blk = pl.BlockSpec((tb, HEAD_DIM), lambda h, n: (n, h))
    return pl.pallas_call(
        body, name="hgrn_fwd",
        out_shape=[jax.ShapeDtypeStruct((t, D_MODEL), BF16), jax.ShapeDtypeStruct((t, D_MODEL), F32),
                   jax.ShapeDtypeStruct((HEADS, n_chunks, HEAD_DIM, HEAD_DIM), F32)],
        grid=(HEADS, t // tb),
        in_specs=[blk(Q_POS), blk(Q_POS + 1), blk(Q_POS + 2), blk(Q_POS + 3),
                  pl.BlockSpec((2, HEAD_DIM), lambda h, n: (0, h)), pl.BlockSpec((1, HEAD_DIM), lambda h, n: (0, h))],
        out_specs=[out_blk, out_blk, pl.BlockSpec((None, nc, HEAD_DIM, HEAD_DIM), lambda h, n: (h, n, 0, 0))],
        scratch_shapes=[pltpu.VMEM((HEAD_DIM, HEAD_DIM), F32)],
        compiler_params=_params([((tb, HEAD_DIM), F32)] * 6 + [((nc, HEAD_DIM, HEAD_DIM), F32)], temps=8 << 20,
                                sem=("arbitrary", "arbitrary")),
    )(proj, proj, proj, proj, lb_table, norm_g)


def _branch_out_forward(a, og, proj, x, w_a, w_b, w_out, ffn_g):
    t = x.shape[0]
    tm = _tile(t, 256)

    def body(a_ref, og_ref, ga_ref, gb_ref, x_ref, wa_ref, wb_ref, wo_ref, g_ref, ya_ref, yb_ref, mg_ref, x1_ref, h2_ref):
        ya = _dot(a_ref[...], wa_ref[...])
        yb = _dot(og_ref[...], wb_ref[...])
        ya_ref[...] = ya
        yb_ref[...] = yb
        merged = (_sigmoid(ga_ref[...]) * ya + _sigmoid(gb_ref[...]) * yb).astype(BF16)
        mg_ref[...] = merged
        x1 = x_ref[...] + _dot(merged, wo_ref[...])
        x1_ref[...] = x1
        _, xh = _rms_stats(x1)
        h2_ref[...] = (xh * g_ref[...]).astype(BF16)

    tok = pl.BlockSpec((tm, D_MODEL), lambda m: (m, 0))
    wsp = pl.BlockSpec((D_MODEL, D_MODEL), lambda m: (0, 0))
    return pl.pallas_call(
        body, name="branch_out_fwd",
        out_shape=[jax.ShapeDtypeStruct((t, D_MODEL), F32), jax.ShapeDtypeStruct((t, D_MODEL), F32),
                   jax.ShapeDtypeStruct((t, D_MODEL), BF16), jax.ShapeDtypeStruct((t, D_MODEL), F32),
                   jax.ShapeDtypeStruct((t, D_MODEL), BF16)],
        grid=(t // tm,),
        in_specs=[tok, tok, pl.BlockSpec((None, tm, D_MODEL), lambda m: (GATE_POS, m, 0)),
                  pl.BlockSpec((None, tm, D_MODEL), lambda m: (GATE_POS + 1, m, 0)), tok, wsp, wsp, wsp,
                  pl.BlockSpec((1, D_MODEL), lambda m: (0, 0))],
        out_specs=[tok] * 5,
        compiler_params=_params([((tm, D_MODEL), BF16)] * 4 + [((tm, D_MODEL), F32)] * 6 + [((D_MODEL, D_MODEL), BF16)] * 3,
                                temps=8 << 20, sem=("arbitrary",)),
    )(a, og, proj, proj, x, w_a, w_b, w_out, ffn_g)


def _ffn_forward(h2, x1, w_gu, w_down):
    t = x1.shape[0]
    tm = _tile(t, 512)

    def body(h_ref, wg_ref, wu_ref, wd_ref, x1_ref, gu_ref, act_ref, x2_ref, acc):
        j = pl.program_id(1)
        h = h_ref[...]
        gate = _dot(h, wg_ref[...])
        up = _dot(h, wu_ref[...])
        gu_ref[0] = gate
        gu_ref[1] = up
        act = ((gate * _sigmoid(gate)) * up).astype(BF16)
        act_ref[...] = act
        part = _dot(act, wd_ref[...])

        @pl.when(j == 0)
        def _():
            acc[...] = part

        @pl.when(j > 0)
        def _():
            acc[...] += part

        @pl.when(j == 3)
        def _():
            x2_ref[...] = x1_ref[...] + acc[...]

    tok = pl.BlockSpec((tm, D_MODEL), lambda m, j: (m, 0))
    return pl.pallas_call(
        body, name="ffn_fwd",
        out_shape=[jax.ShapeDtypeStruct((4, 2, t, FF_BLOCK), F32), jax.ShapeDtypeStruct((4, t, FF_BLOCK), BF16),
                   jax.ShapeDtypeStruct((t, D_MODEL), F32)],
        grid=(t // tm, 4),
        in_specs=[tok, pl.BlockSpec((None, D_MODEL, FF_BLOCK), lambda m, j: (j, 0, 0)),
                  pl.BlockSpec((None, D_MODEL, FF_BLOCK), lambda m, j: (j + 4, 0, 0)),
                  pl.BlockSpec((FF_BLOCK, D_MODEL), lambda m, j: (j, 0)), tok],
        out_specs=[pl.BlockSpec((None, 2, tm, FF_BLOCK), lambda m, j: (j, 0, m, 0)),
                   pl.BlockSpec((None, tm, FF_BLOCK), lambda m, j: (j, m, 0)), tok],
        scratch_shapes=[pltpu.VMEM((tm, D_MODEL), F32)],
        compiler_params=_params([((tm, D_MODEL), BF16), ((D_MODEL, 768), BF16), ((D_MODEL, 768), BF16),
                                 ((FF_BLOCK, D_MODEL), BF16), ((tm, D_MODEL), F32), ((2, tm, 768), F32),
                                 ((tm, 768), BF16), ((tm, D_MODEL), F32)],
                                scratch=[((tm, D_MODEL), F32)], temps=8 << 20, sem=("arbitrary", "arbitrary")),
    )(h2, w_gu, w_gu, w_down, x1)


def _loss_and_final_backward(x2, target, final_g):
    t = x2.shape[0]
    tm = _tile(t, 256)

    def body(x_ref, t_ref, g_ref, loss_ref, dg_ref, dx_ref, dxb_ref):
        @pl.when(pl.program_id(0) == 0)
        def _():
            loss_ref[...] = jnp.zeros_like(loss_ref)
            dg_ref[...] = jnp.zeros_like(dg_ref)

        x = x_ref[...]
        g = g_ref[...]
        r, xh = _rms_stats(x)
        err = xh * g - t_ref[...]
        loss_ref[...] += 0.5 * jnp.sum(jnp.mean(err * err, axis=-1, keepdims=True), axis=0, keepdims=True)
        dy = err * (1.0 / D_MODEL)
        dg_ref[...] += jnp.sum(dy * xh, axis=0, keepdims=True)
        dxh = dy * g
        dx = r * (dxh - xh * jnp.mean(dxh * xh, axis=-1, keepdims=True))
        dx_ref[...] = dx
        dxb_ref[...] = dx.astype(BF16)

    tok = pl.BlockSpec((tm, D_MODEL), lambda m: (m, 0))
    vec = pl.BlockSpec((1, D_MODEL), lambda m: (0, 0))
    return pl.pallas_call(
        body, name="loss_final_bwd",
        out_shape=[jax.ShapeDtypeStruct((8, 128), F32), jax.ShapeDtypeStruct((1, D_MODEL), F32),
                   jax.ShapeDtypeStruct((t, D_MODEL), F32), jax.ShapeDtypeStruct((t, D_MODEL), BF16)],
        grid=(t // tm,), in_specs=[tok, tok, vec],
        out_specs=[pl.BlockSpec((8, 128), lambda m: (0, 0)), vec, tok, tok],
        compiler_params=_params([((tm, D_MODEL), F32)] * 4, temps=8 << 20, sem=("arbitrary",)),
    )(x2, target, final_g)


def _ffn_backward(dx2b, dx2, gu, x1, w_gu, w_down, ffn_g):
    t = x1.shape[0]
    tm = _tile(t, 512)

    def body(dxb_ref, dx2_ref, gu_ref, x1_ref, wg_ref, wu_ref, wd_ref, g_ref, dgu_ref, dx1_ref, dx1b_ref, dg_ref, acc):
        m, j = pl.program_id(0), pl.program_id(1)

        @pl.when((m == 0) & (j == 0))
        def _():
            dg_ref[...] = jnp.zeros_like(dg_ref)

        dact = _dot_nt(dxb_ref[...], wd_ref[...])
        gate, up = gu_ref[0], gu_ref[1]
        sg = _sigmoid(gate)
        dgate = (dact * up * (sg * (1.0 + gate * (1.0 - sg)))).astype(BF16)
        dup = (dact * (gate * sg)).astype(BF16)
        dgu_ref[0] = dgate
        dgu_ref[1] = dup
        part = _dot_nt(dgate, wg_ref[...]) + _dot_nt(dup, wu_ref[...])

        @pl.when(j == 0)
        def _():
            acc[...] = part

        @pl.when(j > 0)
        def _():
            acc[...] += part

        @pl.when(j == 3)
        def _():
            dx, dg = _rms_bwd(acc[...], x1_ref[...], g_ref[...])
            dx1 = dx2_ref[...] + dx
            dx1_ref[...] = dx1
            dx1b_ref[...] = dx1.astype(BF16)
            dg_ref[...] += dg

    tok = pl.BlockSpec((tm, D_MODEL), lambda m, j: (m, 0))
    vec = pl.BlockSpec((1, D_MODEL), lambda m, j: (0, 0))
    gu_spec = pl.BlockSpec((None, 2, tm, FF_BLOCK), lambda m, j: (j, 0, m, 0))
    return pl.pallas_call(
        body, name="ffn_bwd",
        out_shape=[jax.ShapeDtypeStruct((4, 2, t, FF_BLOCK), BF16), jax.ShapeDtypeStruct((t, D_MODEL), F32),
                   jax.ShapeDtypeStruct((t, D_MODEL), BF16), jax.ShapeDtypeStruct((1, D_MODEL), F32)],
        grid=(t // tm, 4),
        in_specs=[tok, tok, gu_spec, tok, pl.BlockSpec((None, D_MODEL, FF_BLOCK), lambda m, j: (j, 0, 0)),
                  pl.BlockSpec((None, D_MODEL, FF_BLOCK), lambda m, j: (j + 4, 0, 0)),
                  pl.BlockSpec((FF_BLOCK, D_MODEL), lambda m, j: (j, 0)), vec],
        out_specs=[gu_spec, tok, tok, vec],
        scratch_shapes=[pltpu.VMEM((tm, D_MODEL), F32)],
        compiler_params=_params([((tm, D_MODEL), BF16), ((tm, D_MODEL), F32), ((2, tm, 768), F32), ((tm, D_MODEL), F32),
                                 ((D_MODEL, 768), BF16), ((D_MODEL, 768), BF16), ((FF_BLOCK, D_MODEL), BF16),
                                 ((2, tm, 768), BF16), ((tm, D_MODEL), F32), ((tm, D_MODEL), BF16)],
                                scratch=[((tm, D_MODEL), F32)], temps=8 << 20, sem=("arbitrary", "arbitrary")),
    )(dx2b, dx2, gu, x1, w_gu, w_gu, w_down, ffn_g)


def _branch_out_backward(dx1b, ya, yb, proj, w_a, w_b, w_out):
    t = ya.shape[0]
    tm = _tile(t, 256)

    def body(dx_ref, ya_ref, yb_ref, ga_ref, gb_ref, wa_ref, wb_ref, wo_ref, dya_ref, dyb_ref, dgate_ref, da_ref, dog_ref):
        dm = _dot_nt(dx_ref[...], wo_ref[...])
        sa, sb = _sigmoid(ga_ref[...]), _sigmoid(gb_ref[...])
        dya = (dm * sa).astype(BF16)
        dyb = (dm * sb).astype(BF16)
        dya_ref[...] = dya
        dyb_ref[...] = dyb
        dgate_ref[0] = (dm * ya_ref[...] * (sa * (1.0 - sa))).astype(BF16)
        dgate_ref[1] = (dm * yb_ref[...] * (sb * (1.0 - sb))).astype(BF16)
        da_ref[...] = _dot_nt(dya, wa_ref[...])
        dog_ref[...] = _dot_nt(dyb, wb_ref[...])

    tok = pl.BlockSpec((tm, D_MODEL), lambda m: (m, 0))
    wsp = pl.BlockSpec((D_MODEL, D_MODEL), lambda m: (0, 0))
    return pl.pallas_call(
        body, name="branch_out_bwd",
        out_shape=[jax.ShapeDtypeStruct((t, D_MODEL), BF16), jax.ShapeDtypeStruct((t, D_MODEL), BF16),
                   jax.ShapeDtypeStruct((2, t, D_MODEL), BF16), jax.ShapeDtypeStruct((t, D_MODEL), F32),
                   jax.ShapeDtypeStruct((t, D_MODEL), F32)],
        grid=(t // tm,),
        in_specs=[tok, tok, tok, pl.BlockSpec((None, tm, D_MODEL), lambda m: (GATE_POS, m, 0)),
                  pl.BlockSpec((None, tm, D_MODEL), lambda m: (GATE_POS + 1, m, 0)), wsp, wsp, wsp],
        out_specs=[tok, tok, pl.BlockSpec((2, tm, D_MODEL), lambda m: (0, m, 0)), tok, tok],
        compiler_params=_params([((tm, D_MODEL), BF16)] * 5 + [((tm, D_MODEL), F32)] * 6 + [((D_MODEL, D_MODEL), BF16)] * 3,
                                temps=8 << 20, sem=("arbitrary",)),
    )(dx1b, ya, yb, proj, proj, w_a, w_b, w_out)


def _hgrn_backward(dog, o_saved, states, proj, lb_table, norm_g):
    t = proj.shape[1]
    tb = _tile(t, 256)
    nc = tb // HGRN_CHUNK
    nb = t // tb

    def body(dog_ref, o_ref, st_ref, q_ref, f_ref, i_ref, g_ref, tab_ref, ng_ref, dp_ref, dng_ref, dtab_ref, gstate):
        @pl.when(pl.program_id(1) == 0)
        def _():
            gstate[...] = jnp.zeros_like(gstate)
            dng_ref[...] = jnp.zeros_like(dng_ref)
            dtab_ref[...] = jnp.zeros_like(dtab_ref)

        lb = _lower_bound(tab_ref)
        ng = ng_ref[...]
        lower, upper = _tri_masks()
        row = lax.broadcasted_iota(jnp.int32, (HGRN_CHUNK, HEAD_DIM), 0)
        dng = jnp.zeros((1, HEAD_DIM), F32)
        dlb = jnp.zeros((1, HEAD_DIM), F32)
        for c in reversed(range(nc)):
            rows = slice(c * HGRN_CHUNK, (c + 1) * HGRN_CHUNK)
            ch = _hgrn_chunk(q_ref[rows, :], f_ref[rows, :], lb)
            v = i_ref[rows, :]
            vb = v.astype(BF16)
            st = st_ref[c]
            stb = st.astype(BF16)
            qib, kib, kdb, qeb = (ch[n].astype(BF16) for n in ("qi", "ki", "kd", "qe"))
            p = jnp.where(lower, _dot_nt(qib, kib), 0.0)
            o, gz, d_og = o_ref[rows, :], g_ref[rows, :], dog_ref[rows, :]
            r, oh = _rms_stats(o)
            sg = _sigmoid(gz)
            d_on = d_og * (gz * sg)
            dgz = d_og * (oh * ng) * (sg * (1.0 + gz * (1.0 - sg)))
            dng = dng + jnp.sum(d_on * oh, axis=0, keepdims=True)
            doh = d_on * ng
            do = r * (doh - oh * jnp.mean(doh * oh, axis=-1, keepdims=True))
            dob = do.astype(BF16)
            gt = gstate[...]
            gtb = gt.astype(BF16)
            dv = _dot_tn(p.astype(BF16), dob) + _dot_nt(kdb, gtb)
            dp = jnp.where(lower, _dot_nt(dob, vb), 0.0).astype(BF16)
            dqi = _dot(dp, kib)
            dki = _dot_tn(dp, qib)
            dqe = _dot(dob, stb)
            dkd = _dot(vb, gtb)
            ddecay = jnp.sum(gt * st, axis=0, keepdims=True)
            decay = jnp.exp(ch["a_last"])
            gstate[...] = gt * decay + _dot_tn(dob, qeb)
            dqs = dqi * ch["e_in"] + dqe * ch["e_all"]
            dk = dki * ch["e_out"] + dkd * ch["e_end"]
            t_in, t_out, t_end = dqi * ch["qi"], dki * ch["ki"], dkd * ch["kd"]
            da = t_in - t_out + dqe * ch["qe"] - t_end
            da_mid = jnp.sum(t_out - t_in, axis=0, keepdims=True)
            da_last = jnp.sum(t_end, axis=0, keepdims=True) + ddecay * decay
            da = da + jnp.where(row == HGRN_CHUNK // 2 - 1, da_mid, 0.0) + jnp.where(row == HGRN_CHUNK - 1, da_last, 0.0)
            dlogf = _mask_mm(upper.astype(BF16), da)
            df = dlogf / ch["f"] - dk
            s = ch["s"]
            dlb = dlb + jnp.sum(df * (1.0 - s), axis=0, keepdims=True)
            dp_ref[0, rows, :] = (dqs * HGRN_SCALE).astype(BF16)
            dp_ref[1, rows, :] = (df * (1.0 - lb) * (s * (1.0 - s))).astype(BF16)
            dp_ref[2, rows, :] = dv.astype(BF16)
            dp_ref[3, rows, :] = dgz.astype(BF16)
        dng_ref[...] += dng
        dt0 = dlb * (lb * (1.0 - lb))
        dtab_ref[0:1, :] += dt0
        dtab_ref[1:2, :] -= dt0

    def blk(p):
        return pl.BlockSpec((None, tb, HEAD_DIM), lambda h, n: (p, nb - 1 - n, h))

    tok = pl.BlockSpec((tb, HEAD_DIM), lambda h, n: (nb - 1 - n, h))
    return pl.pallas_call(
        body, name="hgrn_bwd",
        out_shape=[jax.ShapeDtypeStruct((N_DEV, t, D_MODEL), BF16), jax.ShapeDtypeStruct((1, D_MODEL), F32),
                   jax.ShapeDtypeStruct((2, D_MODEL), F32)],
        grid=(HEADS, nb),
        in_specs=[tok, tok, pl.BlockSpec((None, nc, HEAD_DIM, HEAD_DIM), lambda h, n: (h, nb - 1 - n, 0, 0)),
                  blk(Q_POS), blk(Q_POS + 1), blk(Q_POS + 2), blk(Q_POS + 3),
                  pl.BlockSpec((2, HEAD_DIM), lambda h, n: (0, h)), pl.BlockSpec((1, HEAD_DIM), lambda h, n: (0, h))],
        out_specs=[pl.BlockSpec((4, tb, HEAD_DIM), lambda h, n: (0, nb - 1 - n, h)),
                   pl.BlockSpec((1, HEAD_DIM), lambda h, n: (0, h)), pl.BlockSpec((2, HEAD_DIM), lambda h, n: (0, h))],
        scratch_shapes=[pltpu.VMEM((HEAD_DIM, HEAD_DIM), F32)],
        compiler_params=_params([((tb, HEAD_DIM), F32)] * 6 + [((nc, HEAD_DIM, HEAD_DIM), F32)] + [((4, tb, HEAD_DIM), BF16)],
                                temps=8 << 20, sem=("arbitrary", "arbitrary")),
    )(dog, o_saved, states, proj, proj, proj, proj, lb_table, norm_g)


def _gmlp_backward(dproj, da, proj, ln_g, ln_b, w_s, bias_b):
    t = proj.shape[1]
    tm = _tile(t, 256)
    chunks = tm // GMLP_CHUNK

    def body(_, da_ref, u_ref, v_ref, lng_ref, lnb_ref, ws_ref, bias_ref, dp_ref, dlng_ref, dlnb_ref, dws_ref, dbs_ref,
             vn_scr, dvn_scr):
        @pl.when(pl.program_id(0) == 0)
        def _():
            dlng_ref[...] = jnp.zeros_like(dlng_ref)
            dlnb_ref[...] = jnp.zeros_like(dlnb_ref)
            dws_ref[...] = jnp.zeros_like(dws_ref)
            dbs_ref[...] = jnp.zeros_like(dbs_ref)

        v = v_ref[...]
        vv = _gelu(v)
        mu = jnp.mean(vv, axis=-1, keepdims=True)
        cen = vv - mu
        rstd = lax.rsqrt(jnp.mean(cen * cen, axis=-1, keepdims=True) + NORM_EPS)
        vhat = cen * rstd
        lng = lng_ref[...]
        vn_scr[...] = (vhat * lng + lnb_ref[...]).astype(BF16)
        row = lax.broadcasted_iota(jnp.int32, (GMLP_CHUNK, GMLP_CHUNK), 0)
        col = lax.broadcasted_iota(jnp.int32, (GMLP_CHUNK, GMLP_CHUNK), 1)
        for g in range(GROUPS):
            wm = _masked_ws(ws_ref, g)
            cols = slice(g * HEAD_DIM, (g + 1) * HEAD_DIM)
            dws = jnp.zeros((GMLP_CHUNK, GMLP_CHUNK), F32)
            dbs = jnp.zeros((GMLP_CHUNK, GMLP_CHUNK), F32)
            for c in range(chunks):
                rows = slice(c * GMLP_CHUNK, (c + 1) * GMLP_CHUNK)
                vn = vn_scr[rows, cols]
                mixed = _dot(wm, vn) + bias_ref[g]
                u = u_ref[rows, cols]
                d_a = da_ref[rows, cols]
                dp_ref[0, rows, cols] = (d_a * mixed * _gelu_grad(u)).astype(BF16)
                dmix = d_a * _gelu(u)
                dmb = dmix.astype(BF16)
                dbs = dbs + dmix
                dws = dws + _dot_nt(dmb, vn)
                dvn_scr[rows, cols] = _dot_tn(wm, dmb)
            dws_ref[g] += jnp.where(row >= col, dws, 0.0)
            dbs_ref[g] += jnp.broadcast_to(jnp.sum(dbs, axis=-1, keepdims=True), (GMLP_CHUNK, GMLP_CHUNK))
        dvn = dvn_scr[...]
        dlng_ref[...] += jnp.sum(dvn * vhat, axis=0, keepdims=True)
        dlnb_ref[...] += jnp.sum(dvn, axis=0, keepdims=True)
        dvh = dvn * lng
        dvv = rstd * (dvh - jnp.mean(dvh, axis=-1, keepdims=True) - vhat * jnp.mean(dvh * vhat, axis=-1, keepdims=True))
        dp_ref[1] = (dvv * _gelu_grad(v)).astype(BF16)

    tok = pl.BlockSpec((tm, D_MODEL), lambda m: (m, 0))
    small = pl.BlockSpec((GROUPS, GMLP_CHUNK, GMLP_CHUNK), lambda m: (0, 0, 0))
    vec = pl.BlockSpec((1, D_MODEL), lambda m: (0, 0))
    return pl.pallas_call(
        body, name="gmlp_bwd",
        out_shape=[jax.ShapeDtypeStruct(dproj.shape, BF16), jax.ShapeDtypeStruct((1, D_MODEL), F32),
                   jax.ShapeDtypeStruct((1, D_MODEL), F32), jax.ShapeDtypeStruct((GROUPS, GMLP_CHUNK, GMLP_CHUNK), F32),
                   jax.ShapeDtypeStruct((GROUPS, GMLP_CHUNK, GMLP_CHUNK), F32)],
        grid=(t // tm,),
        in_specs=[ANY, tok, pl.BlockSpec((None, tm, D_MODEL), lambda m: (U_POS, m, 0)),
                  pl.BlockSpec((None, tm, D_MODEL), lambda m: (U_POS + 1, m, 0)), vec, vec, small, small],
        out_specs=[pl.BlockSpec((2, tm, D_MODEL), lambda m: (U_POS // 2, m, 0)), vec, vec, small, small],
        scratch_shapes=[pltpu.VMEM((tm, D_MODEL), BF16), pltpu.VMEM((tm, D_MODEL), F32)],
        input_output_aliases={0: 0},
        compiler_params=_params([((tm, D_MODEL), F32)] * 3 + [((2, tm, D_MODEL), BF16)] + [((8, 128, 128), F32)] * 4,
                                scratch=[((tm, D_MODEL), BF16), ((tm, D_MODEL), F32)], temps=12 << 20, sem=("arbitrary",)),
    )(dproj, da, proj, proj, ln_g, ln_b, w_s, bias_b)


def _place_gate_grads(dproj, dgates):
    t = dgates.shape[1]
    tm = _tile(t, 512)

    def body(_, src_ref, dst_ref):
        dst_ref[...] = src_ref[...]

    return pl.pallas_call(
        body, name="place_gate_grads", out_shape=jax.ShapeDtypeStruct(dproj.shape, BF16), grid=(t // tm,),
        in_specs=[ANY, pl.BlockSpec((2, tm, D_MODEL), lambda m: (0, m, 0))],
        out_specs=pl.BlockSpec((2, tm, D_MODEL), lambda m: (GATE_POS // 2, m, 0)),
        input_output_aliases={0: 0},
        compiler_params=_params([((2, tm, D_MODEL), BF16)] * 2, sem=("arbitrary",)),
    )(dproj, dgates)


def _input_backward(dproj, w_in_g, x, dx1, mix_g):
    t = x.shape[0]
    tm = _tile(t, 512)

    def body(dp_ref, w_ref, x_ref, dx1_ref, g_ref, dx_ref, dg_ref, acc):
        m, p = pl.program_id(0), pl.program_id(1)

        @pl.when((m == 0) & (p == 0))
        def _():
            dg_ref[...] = jnp.zeros_like(dg_ref)

        part = _dot_nt(dp_ref[...], w_ref[...])

        @pl.when(p == 0)
        def _():
            acc[...] = part

        @pl.when(p > 0)
        def _():
            acc[...] += part

        @pl.when(p == N_DEV - 1)
        def _():
            dx, dg = _rms_bwd(acc[...], x_ref[...], g_ref[...])
            dx_ref[...] = dx1_ref[...] + dx
            dg_ref[...] += dg

    tok = pl.BlockSpec((tm, D_MODEL), lambda m, p: (m, 0))
    vec = pl.BlockSpec((1, D_MODEL), lambda m, p: (0, 0))
    return pl.pallas_call(
        body, name="input_bwd",
        out_shape=[jax.ShapeDtypeStruct((t, D_MODEL), F32), jax.ShapeDtypeStruct((1, D_MODEL), F32)],
        grid=(t // tm, N_DEV),
        in_specs=[pl.BlockSpec((None, tm, D_MODEL), lambda m, p: (p, m, 0)),
                  pl.BlockSpec((None, D_MODEL, D_MODEL), lambda m, p: (p, 0, 0)), tok, tok, vec],
        out_specs=[tok, vec],
        scratch_shapes=[pltpu.VMEM((tm, D_MODEL), F32)],
        compiler_params=_params([((tm, D_MODEL), BF16), ((D_MODEL, D_MODEL), BF16)] + [((tm, D_MODEL), F32)] * 3,
                                scratch=[((tm, D_MODEL), F32)], temps=8 << 20, sem=("arbitrary", "arbitrary")),
    )(dproj, w_in_g, x, dx1, mix_g)


def _weight_grad(name, a, b, a_spec, b_spec, out_shape, out_spec, grid, blocks):
    def body(a_ref, b_ref, o_ref):
        part = _dot_tn(a_ref[...], b_ref[...])
        m = pl.program_id(len(grid) - 1)

        @pl.when(m == 0)
        def _():
            o_ref[...] = part

        @pl.when(m > 0)
        def _():
            o_ref[...] += part

    return pl.pallas_call(
        body, name=name, out_shape=jax.ShapeDtypeStruct(out_shape, F32), grid=grid, in_specs=[a_spec, b_spec],
        out_specs=out_spec, compiler_params=_params(blocks, temps=8 << 20, sem=("arbitrary",) * len(grid)),
    )(a, b)


def _pack_small(mix_g, ln_g, ln_b, w_s, b_s, lb_table, hg_norm, ffn_g, final_g):
    def part(a):
        a = a.reshape(-1, D_MODEL)
        return jnp.pad(a, ((0, 8 - a.shape[0]), (0, 0)))

    return jnp.concatenate([part(mix_g), part(ln_g), part(ln_b), part(hg_norm), part(ffn_g), part(final_g),
                            part(lb_table), part(b_s), w_s.reshape(GMLP_CHUNK, D_MODEL)], axis=0)


def _unpack_small(pack):
    return dict(norm_mix_g=pack[0:1], gmlp_ln_g=pack[8:9], gmlp_ln_b=pack[16:17], hgrn_norm_g=pack[24:25],
                norm_ffn_g=pack[32:33], norm_final_g=pack[40], hgrn_lb_table=pack[48:50],
                gmlp_b_s=pack[56:57].reshape(1, GROUPS, GMLP_CHUNK),
                gmlp_w_s=pack[64:192].reshape(1, GROUPS, GMLP_CHUNK, GMLP_CHUNK))


def _adamw_small(gathered, w, m, v):
    rows = w.shape[0]

    def body(p_ref, w_ref, m_ref, v_ref, g_out, d_out, m_out, v_out):
        g = p_ref[0]
        for j in range(1, N_DEV):
            g = g + p_ref[j]
        delta, m_new, v_new = _adamw_math(w_ref[...], g, m_ref[...], v_ref[...])
        g_out[...] = g
        d_out[...] = delta
        m_out[...] = m_new
        v_out[...] = v_new

    tr = 64
    spec = pl.BlockSpec((tr, D_MODEL), lambda r: (r, 0))
    return pl.pallas_call(
        body, name="adamw_small", out_shape=[jax.ShapeDtypeStruct((rows, D_MODEL), F32)] * 4, grid=(rows // tr,),
        in_specs=[pl.BlockSpec((N_DEV, tr, D_MODEL), lambda r: (0, r, 0)), spec, spec, spec], out_specs=[spec] * 4,
        compiler_params=_params([((N_DEV, tr, D_MODEL), F32)] + [((tr, D_MODEL), F32)] * 7, sem=("arbitrary",)),
    )(gathered, w, m, v)


def kernel(x, norm_mix_g, w_in, gmlp_ln_g, gmlp_ln_b, gmlp_w_s, gmlp_b_s, hgrn_lb_table, hgrn_norm_g, w_branch_a, w_branch_b, w_out, norm_ffn_g, w_gate_up, w_down, norm_final_g, loss_target, m_norm_mix_g, m_w_in, m_gmlp_ln_g, m_gmlp_ln_b, m_gmlp_w_s, m_gmlp_b_s, m_hgrn_lb_table, m_hgrn_norm_g, m_w_branch_a, m_w_branch_b, m_w_out, m_norm_ffn_g, m_w_gate_up, m_w_down, m_norm_final_g, v_norm_mix_g, v_w_in, v_gmlp_ln_g, v_gmlp_ln_b, v_gmlp_w_s, v_gmlp_b_s, v_hgrn_lb_table, v_hgrn_norm_g, v_w_branch_a, v_w_branch_b, v_w_out, v_norm_ffn_g, v_w_gate_up, v_w_down, v_norm_final_g):
    t = x.shape[1]
    x2d = x.reshape(t, D_MODEL)
    target = loss_target.reshape(t, D_MODEL)
    final_g = norm_final_g.reshape(1, D_MODEL)

    shards = [w_in[0].astype(BF16), w_branch_a[0].astype(BF16), w_branch_b[0].astype(BF16), w_out[0].astype(BF16),
              w_gate_up[0].astype(BF16), w_down[0].astype(BF16)]

    def rows_of(n):
        return lambda ref, j: ref.at[pl.ds(pl.multiple_of(j * n, 8), n)]

    w_in_g, w_a, w_b, w_o, w_gu, w_dn = _all_gather(
        "weights_all_gather", shards,
        [((N_DEV, D_MODEL, D_MODEL), BF16), ((D_MODEL, D_MODEL), BF16), ((D_MODEL, D_MODEL), BF16),
         ((D_MODEL, D_MODEL), BF16), ((N_DEV, D_MODEL, FF_BLOCK), BF16), ((D_FF, D_MODEL), BF16)],
        [lambda ref, j: ref.at[_pos_of_dev(j)], rows_of(BRANCH_ROWS), rows_of(BRANCH_ROWS), rows_of(BRANCH_ROWS),
         lambda ref, j: ref.at[j], rows_of(DOWN_ROWS)])

    h = _rms_forward(x2d, norm_mix_g)
    proj = _proj_forward(h, w_in_g)
    bias_b = jnp.broadcast_to(gmlp_b_s[0][:, :, None], (GROUPS, GMLP_CHUNK, GMLP_CHUNK))
    a = _gmlp_forward(proj, gmlp_ln_g, gmlp_ln_b, gmlp_w_s[0], bias_b)
    og, o_saved, states = _hgrn_forward(proj, hgrn_lb_table, hgrn_norm_g)
    ya, yb, merged, x1, h2 = _branch_out_forward(a, og, proj, x2d, w_a, w_b, w_o, norm_ffn_g)
    gu, act, x2 = _ffn_forward(h2, x1, w_gu, w_dn)
    loss_tile, d_final_g, dx2, dx2b = _loss_and_final_backward(x2, target, final_g)

    dgu, dx1, dx1b, d_ffn_g = _ffn_backward(dx2b, dx2, gu, x1, w_gu, w_dn, norm_ffn_g)
    dya, dyb, dgates, da, dog = _branch_out_backward(dx1b, ya, yb, proj, w_a, w_b, w_o)
    dproj, d_hg_norm, d_lb = _hgrn_backward(dog, o_saved, states, proj, hgrn_lb_table, hgrn_norm_g)
    dproj, d_ln_g, d_ln_b, d_ws, d_bs = _gmlp_backward(dproj, da, proj, gmlp_ln_g, gmlp_ln_b, gmlp_w_s[0], bias_b)
    dproj = _place_gate_grads(dproj, dgates)
    grad_x, d_mix_g = _input_backward(dproj, w_in_g, x2d, dx1, norm_mix_g)

    tm = _tile(t, 512)
    nm = t // tm
    tok_a = pl.BlockSpec((tm, D_MODEL), lambda m: (m, 0))
    full_o = pl.BlockSpec((D_MODEL, D_MODEL), lambda m: (0, 0))
    sq_blocks = [((tm, D_MODEL), BF16)] * 2 + [((D_MODEL, D_MODEL), F32)]
    g_in = _weight_grad(
        "grad_w_in", h, dproj, pl.BlockSpec((tm, D_MODEL), lambda p, m: (m, 0)),
        pl.BlockSpec((None, tm, D_MODEL), lambda p, m: (p, m, 0)), (N_DEV, D_MODEL, D_MODEL),
        pl.BlockSpec((None, D_MODEL, D_MODEL), lambda p, m: (p, 0, 0)), (N_DEV, nm), sq_blocks)
    g_a = _weight_grad("grad_w_a", a, dya, tok_a, tok_a, (D_MODEL, D_MODEL), full_o, (nm,), sq_blocks)
    g_b = _weight_grad("grad_w_b", og, dyb, tok_a, tok_a, (D_MODEL, D_MODEL), full_o, (nm,), sq_blocks)
    g_o = _weight_grad("grad_w_out", merged, dx1b, tok_a, tok_a, (D_MODEL, D_MODEL), full_o, (nm,), sq_blocks)
    g_gu = _weight_grad(
        "grad_w_gate_up", h2, dgu, pl.BlockSpec((tm, D_MODEL), lambda j, m: (m, 0)),
        pl.BlockSpec((None, None, tm, FF_BLOCK), lambda j, m: (j % 4, j // 4, m, 0)), (N_DEV, D_MODEL, FF_BLOCK),
        pl.BlockSpec((None, D_MODEL, FF_BLOCK), lambda j, m: (j, 0, 0)), (N_DEV, nm),
        [((tm, D_MODEL), BF16), ((tm, 768), BF16), ((D_MODEL, 768), F32)])
    g_dn = _weight_grad(
        "grad_w_down", act, dx2b, pl.BlockSpec((None, tm, FF_BLOCK), lambda j, m: (j, m, 0)),
        pl.BlockSpec((tm, D_MODEL), lambda j, m: (m, 0)), (D_FF, D_MODEL),
        pl.BlockSpec((FF_BLOCK, D_MODEL), lambda j, m: (j, 0)), (4, nm),
        [((tm, 768), BF16), ((tm, D_MODEL), BF16), ((FF_BLOCK, D_MODEL), F32)])

    shard_fns = [lambda ref, j: ref.at[_pos_of_dev(j)], rows_of(BRANCH_ROWS), rows_of(BRANCH_ROWS), rows_of(BRANCH_ROWS),
                 lambda ref, j: ref.at[j], rows_of(DOWN_ROWS)]
    shard_shapes = [(D_MODEL, D_MODEL), (BRANCH_ROWS, D_MODEL), (BRANCH_ROWS, D_MODEL), (BRANCH_ROWS, D_MODEL),
                    (D_MODEL, FF_BLOCK), (DOWN_ROWS, D_MODEL)]
    names = ["w_in", "w_branch_a", "w_branch_b", "w_out", "w_gate_up", "w_down"]
    grads = [g_in, g_a, g_b, g_o, g_gu, g_dn]
    land = _exchange_sibling("grads_to_sibling", grads, shard_fns, shard_shapes)
    core = lax.axis_index("c").astype(jnp.int32).reshape(1)
    chip = (2 * lax.axis_index("x") + lax.axis_index("y")).astype(jnp.int32).reshape(1)
    branch = ((BRANCH_ROWS, D_MODEL), lambda q, r, c: (2 * q + c, 0))
    own_blocks = [((None, 256, D_MODEL), lambda q, r, c: (_pos_of_dev(2 * q + c), r, 0)), branch, branch, branch,
                  ((None, 256, FF_BLOCK), lambda q, r, c: (2 * q + c, r, 0)),
                  ((DOWN_ROWS // 2, D_MODEL), lambda q, r, c: (2 * (2 * q + c) + r, 0))]
    partials = [_chip_partial("chip_partial_" + nme, core, g_, blk, idx, l_)
                for nme, g_, (blk, idx), l_ in zip(names, grads, own_blocks, land)]
    landed = _exchange_chips("grads_to_chips", partials)
    big = {}
    for nme, own, lnd, w, m, v in zip(
            names, partials, landed, [w_in, w_branch_a, w_branch_b, w_out, w_gate_up, w_down],
            [m_w_in, m_w_branch_a, m_w_branch_b, m_w_out, m_w_gate_up, m_w_down],
            [v_w_in, v_w_branch_a, v_w_branch_b, v_w_out, v_w_gate_up, v_w_down]):
        outs = _adamw("adamw_" + nme, chip, own, lnd, w[0], m[0], v[0])
        big[nme] = [o_[None] for o_ in outs]

    d_bs_row = d_bs[:, :, 0]
    small_partial = _pack_small(d_mix_g, d_ln_g, d_ln_b, d_ws, d_bs_row, d_lb, d_hg_norm, d_ffn_g, d_final_g)
    (small_all,) = _all_gather("small_grads_all_gather", [small_partial], [((N_DEV, SMALL_ROWS, D_MODEL), F32)],
                               [lambda ref, j: ref.at[j]])

    def packed(prefix_vals):
        return _pack_small(*prefix_vals)

    w_pack = packed([norm_mix_g, gmlp_ln_g, gmlp_ln_b, gmlp_w_s, gmlp_b_s, hgrn_lb_table, hgrn_norm_g, norm_ffn_g, norm_final_g])
    m_pack = packed([m_norm_mix_g, m_gmlp_ln_g, m_gmlp_ln_b, m_gmlp_w_s, m_gmlp_b_s, m_hgrn_lb_table, m_hgrn_norm_g, m_norm_ffn_g, m_norm_final_g])
    v_pack = packed([v_norm_mix_g, v_gmlp_ln_g, v_gmlp_ln_b, v_gmlp_w_s, v_gmlp_b_s, v_hgrn_lb_table, v_hgrn_norm_g, v_norm_ffn_g, v_norm_final_g])
    small = [_unpack_small(p) for p in _adamw_small(small_all, w_pack, m_pack, v_pack)]

    loss = lax.psum(loss_tile[0, 0], ("x", "y", "c"))
    order = ["norm_mix_g", "w_in", "gmlp_ln_g", "gmlp_ln_b", "gmlp_w_s", "gmlp_b_s", "hgrn_lb_table", "hgrn_norm_g",
             "w_branch_a", "w_branch_b", "w_out", "norm_ffn_g", "w_gate_up", "w_down", "norm_final_g"]
    outs = [loss, grad_x.reshape(1, t, D_MODEL)]
    for kind in range(4):
        for nme in order:
            outs.append(big[nme][kind] if nme in big else small[kind][nme])
    return tuple(outs)
```

```python
import functools

import jax
import jax.numpy as jnp
from jax import lax
from jax.experimental import pallas as pl
from jax.experimental.pallas import tpu as pltpu

F32, BF16 = jnp.float32, jnp.bfloat16
D_MODEL = 1024
N_DEV = 8
HEADS = 8
HEAD_DIM = 128
GROUPS = 8
GMLP_CHUNK = 128
HGRN_CHUNK = 64
HGRN_SCALE = HEAD_DIM ** -0.5
D_FF = 2816
FF_BLOCK = D_FF // 4
DOWN_ROWS = D_FF // N_DEV
BRANCH_ROWS = D_MODEL // N_DEV
NORM_EPS = 1e-6
ADAM_LR, ADAM_B1, ADAM_B2, ADAM_EPS, ADAM_WD, ADAM_STEP = 0.001, 0.9, 0.999, 1e-08, 0.01, 10
SMALL_ROWS = 192
V7X_VMEM_BYTES = 64 * 1024 * 1024
VMEM_CAP = V7X_VMEM_BYTES - 6 * 1024 * 1024
MESH_ID = pl.DeviceIdType.MESH
ANY = pl.BlockSpec(memory_space=pl.ANY)
Q_POS, U_POS, GATE_POS = 0, 4, 6


def _pos_of_dev(j):
    return jnp.where(j < 2, j + 4, jnp.where(j < 6, j - 2, j))


def _dev_of_pos(p):
    return jnp.where(p < 4, p + 2, jnp.where(p < 6, p - 4, p))


def _nbytes(shape, dtype):
    n = 1
    for s in shape:
        n *= s
    return n * jnp.dtype(dtype).itemsize


def _params(blocks, scratch=(), temps=0, sem=None):
    need = 2 * sum(_nbytes(s, d) for s, d in blocks) + sum(_nbytes(s, d) for s, d in scratch) + temps
    assert need + (4 << 20) <= VMEM_CAP, need
    return pltpu.CompilerParams(dimension_semantics=sem, vmem_limit_bytes=VMEM_CAP)


def _tile(n, pref):
    return pref if n % pref == 0 else n


def _dot(a, b):
    return jnp.dot(a, b, preferred_element_type=F32)


def _dot_nt(a, b):
    return lax.dot_general(a, b, (((1,), (1,)), ((), ())), preferred_element_type=F32)


def _dot_tn(a, b):
    return lax.dot_general(a, b, (((0,), (0,)), ((), ())), preferred_element_type=F32)


def _sigmoid(x):
    return 1.0 / (1.0 + jnp.exp(-x))


_GELU_C = 0.7978845608028654


def _gelu(x):
    return x * (0.5 * (1.0 + jnp.tanh(_GELU_C * (x + 0.044715 * (x * x * x)))))


def _gelu_grad(x):
    t = jnp.tanh(_GELU_C * (x + 0.044715 * (x * x * x)))
    return 0.5 * (1.0 + t) + 0.5 * x * (1.0 - t * t) * (_GELU_C * (1.0 + 3.0 * 0.044715 * x * x))


def _rms_stats(x):
    r = lax.rsqrt(jnp.mean(x * x, axis=-1, keepdims=True) + NORM_EPS)
    return r, x * r


def _rms_bwd(dy, x, g):
    r, xh = _rms_stats(x)
    dg = jnp.sum(dy * xh, axis=0, keepdims=True)
    dxh = dy * g
    dx = r * (dxh - xh * jnp.mean(dxh * xh, axis=-1, keepdims=True))
    return dx, dg


def _split3(x):
    hi = x.astype(BF16)
    r = x - hi.astype(F32)
    mid = r.astype(BF16)
    lo = (r - mid.astype(F32)).astype(BF16)
    return hi, mid, lo


def _mask_mm(mask_bf16, x):
    hi, mid, lo = _split3(x)
    return _dot(mask_bf16, hi) + _dot(mask_bf16, mid) + _dot(mask_bf16, lo)


def _place():
    return lax.axis_index("x"), lax.axis_index("y"), lax.axis_index("c")


def _all_gather(name, srcs, out_shapes, slicers):
    n = len(srcs)

    def body(*refs):
        src, out = refs[:n], refs[n:2 * n]
        send, recv, loc = refs[2 * n:]
        x, y, c = _place()
        me, sib = (x, y, c), (x, y, 1 - c)
        chips = [(1 - x, y), (x, 1 - y), (1 - x, 1 - y)]

        def dev(p):
            return 4 * p[0] + 2 * p[1] + p[2]

        def rc(i, k, block, to, from_src=False):
            dst = slicers[i](out[i], dev(block))
            return pltpu.make_async_remote_copy(
                src_ref=src[i] if from_src else dst, dst_ref=dst, send_sem=send.at[7 * i + k],
                recv_sem=recv.at[7 * i + k], device_id=to, device_id_type=MESH_ID)

        mine = [pltpu.make_async_copy(src[i], slicers[i](out[i], dev(me)), loc.at[i]) for i in range(n)]
        for cp in mine:
            cp.start()
        first = []
        for i in range(n):
            first.append(rc(i, 0, me, sib, True))
            for j, chip in enumerate(chips):
                first.append(rc(i, 1 + j, me, (*chip, c), True))
        for cp in first:
            cp.start()
        passed = []
        for j, chip in enumerate(chips):
            for i in range(n):
                rc(i, 1 + j, (*chip, c), me).wait_recv()
                cp = rc(i, 4 + j, (*chip, c), sib)
                cp.start()
                passed.append(cp)
        for i in range(n):
            rc(i, 0, sib, me).wait_recv()
            for j, chip in enumerate(chips):
                rc(i, 4 + j, (*chip, 1 - c), me).wait_recv()
        for cp in first + passed:
            cp.wait_send()
        for cp in mine:
            cp.wait()

    return pl.pallas_call(
        body, name=name, out_shape=[jax.ShapeDtypeStruct(s, d) for s, d in out_shapes],
        in_specs=[ANY] * n, out_specs=[ANY] * n,
        scratch_shapes=[pltpu.SemaphoreType.DMA((7 * n,)), pltpu.SemaphoreType.DMA((7 * n,)),
                        pltpu.SemaphoreType.DMA((n,))],
    )(*srcs)


def _exchange_sibling(name, grads, shard_fns, shard_shapes):
    n = len(grads)

    def body(*refs):
        g, land = refs[:n], refs[n:2 * n]
        send, recv = refs[2 * n:]
        x, y, c = _place()
        remote = []
        for i in range(n):
            for q in range(4):
                cp = pltpu.make_async_remote_copy(
                    src_ref=shard_fns[i](g[i], 2 * q + (1 - c)), dst_ref=land[i].at[q], send_sem=send.at[4 * i + q],
                    recv_sem=recv.at[4 * i + q], device_id=(x, y, 1 - c), device_id_type=MESH_ID)
                cp.start()
                remote.append(cp)
        for cp in remote:
            cp.wait()

    return pl.pallas_call(
        body, name=name, out_shape=[jax.ShapeDtypeStruct((4, *s), F32) for s in shard_shapes],
        in_specs=[ANY] * n, out_specs=[ANY] * n,
        scratch_shapes=[pltpu.SemaphoreType.DMA((4 * n,)), pltpu.SemaphoreType.DMA((4 * n,))],
    )(*grads)


def _exchange_chips(name, parts):
    n = len(parts)

    def body(*refs):
        part, out = refs[:n], refs[n:2 * n]
        send, recv = refs[2 * n:]
        x, y, c = _place()
        remote = []
        for i in range(n):
            for s in range(3):
                qx = 1 - x if (s + 1) // 2 else x
                qy = 1 - y if (s + 1) % 2 else y
                cp = pltpu.make_async_remote_copy(
                    src_ref=part[i].at[2 * qx + qy], dst_ref=out[i].at[s], send_sem=send.at[3 * i + s],
                    recv_sem=recv.at[3 * i + s], device_id=(qx, qy, c), device_id_type=MESH_ID)
                cp.start()
                remote.append(cp)
        for cp in remote:
            cp.wait()

    return pl.pallas_call(
        body, name=name, out_shape=[jax.ShapeDtypeStruct((3, *p.shape[1:]), p.dtype) for p in parts],
        in_specs=[ANY] * n, out_specs=[ANY] * n,
        scratch_shapes=[pltpu.SemaphoreType.DMA((3 * n,)), pltpu.SemaphoreType.DMA((3 * n,))],
    )(*parts)


def _chip_partial(name, core, grad, own_block, own_index, land):
    _, rows, cols = land.shape
    tr = own_block[-2]

    def body(core_ref, a_ref, b_ref, o_ref):
        o_ref[...] = (a_ref[...] + b_ref[...]).astype(BF16)

    spec = pl.BlockSpec((None, tr, cols), lambda q, r, c: (q, r, 0))
    return pl.pallas_call(
        body, name=name, out_shape=jax.ShapeDtypeStruct(land.shape, BF16),
        grid_spec=pltpu.PrefetchScalarGridSpec(
            num_scalar_prefetch=1, grid=(4, rows // tr),
            in_specs=[pl.BlockSpec(own_block, lambda q, r, c: own_index(q, r, c[0])), spec], out_specs=spec),
        compiler_params=_params([((tr, cols), F32)] * 2 + [((tr, cols), BF16)], sem=("arbitrary", "arbitrary")),
    )(core, grad, land)


def _adamw_math(w, g, m, v):
    m = ADAM_B1 * m + (1.0 - ADAM_B1) * g
    v = ADAM_B2 * v + (1.0 - ADAM_B2) * (g * g)
    m_hat = m / (1.0 - ADAM_B1 ** ADAM_STEP)
    v_hat = v / (1.0 - ADAM_B2 ** ADAM_STEP)
    delta = -ADAM_LR * (m_hat / (jnp.sqrt(v_hat) + ADAM_EPS) + ADAM_WD * w)
    return delta, m, v


def _adamw(name, chip, own, landed, w, m, v):
    _, rows, cols = own.shape
    tr = _tile(rows, 256) if rows % 256 == 0 else _tile(rows, 176)

    def body(chip_ref, own_ref, l_ref, w_ref, m_ref, v_ref, g_out, d_out, m_out, v_out):
        g = own_ref[...].astype(F32)
        for s in range(3):
            g = g + l_ref[s].astype(F32)
        delta, m_new, v_new = _adamw_math(w_ref[...], g, m_ref[...], v_ref[...])
        g_out[...] = g
        d_out[...] = delta
        m_out[...] = m_new
        v_out[...] = v_new

    spec = pl.BlockSpec((tr, cols), lambda r, c: (r, 0))
    return pl.pallas_call(
        body, name=name, out_shape=[jax.ShapeDtypeStruct((rows, cols), F32)] * 4,
        grid_spec=pltpu.PrefetchScalarGridSpec(
            num_scalar_prefetch=1, grid=(rows // tr,),
            in_specs=[pl.BlockSpec((None, tr, cols), lambda r, c: (c[0], r, 0)),
                      pl.BlockSpec((3, tr, cols), lambda r, c: (0, r, 0)), spec, spec, spec],
            out_specs=[spec] * 4),
        compiler_params=_params([((4, tr, cols), own.dtype)] + [((tr, cols), F32)] * 7, sem=("arbitrary",)),
    )(chip, own, landed, w, m, v)


def _rms_forward(x, gain):
    t = x.shape[0]
    tm = _tile(t, 512)

    def body(x_ref, g_ref, h_ref):
        _, xh = _rms_stats(x_ref[...])
        h_ref[...] = (xh * g_ref[...]).astype(BF16)

    return pl.pallas_call(
        body, name="rms_mix_fwd", out_shape=jax.ShapeDtypeStruct((t, D_MODEL), BF16), grid=(t // tm,),
        in_specs=[pl.BlockSpec((tm, D_MODEL), lambda m: (m, 0)), pl.BlockSpec((1, D_MODEL), lambda m: (0, 0))],
        out_specs=pl.BlockSpec((tm, D_MODEL), lambda m: (m, 0)),
        compiler_params=_params([((tm, D_MODEL), F32), ((tm, D_MODEL), BF16)], temps=8 << 20, sem=("arbitrary",)),
    )(x, gain)


def _proj_forward(h, w_in_g):
    t = h.shape[0]
    tm = _tile(t, 1024)

    def body(h_ref, w_ref, o_ref):
        o_ref[...] = _dot(h_ref[...], w_ref[...])

    return pl.pallas_call(
        body, name="proj_fwd", out_shape=jax.ShapeDtypeStruct((N_DEV, t, D_MODEL), F32), grid=(N_DEV, t // tm),
        in_specs=[pl.BlockSpec((tm, D_MODEL), lambda p, m: (m, 0)),
                  pl.BlockSpec((None, D_MODEL, D_MODEL), lambda p, m: (p, 0, 0))],
        out_specs=pl.BlockSpec((None, tm, D_MODEL), lambda p, m: (p, m, 0)),
        compiler_params=_params([((tm, D_MODEL), BF16), ((D_MODEL, D_MODEL), BF16), ((tm, D_MODEL), F32)],
                                sem=("arbitrary", "arbitrary")),
    )(h, w_in_g)


def _masked_ws(ws_ref, g):
    row = lax.broadcasted_iota(jnp.int32, (GMLP_CHUNK, GMLP_CHUNK), 0)
    col = lax.broadcasted_iota(jnp.int32, (GMLP_CHUNK, GMLP_CHUNK), 1)
    return jnp.where(row >= col, ws_ref[g], 0.0).astype(BF16)


def _gmlp_forward(proj, ln_g, ln_b, w_s, bias_b):
    t = proj.shape[1]
    tm = _tile(t, 256)
    chunks = tm // GMLP_CHUNK

    def body(u_ref, v_ref, lng_ref, lnb_ref, ws_ref, bias_ref, a_ref, vn_scr):
        vv = _gelu(v_ref[...])
        mu = jnp.mean(vv, axis=-1, keepdims=True)
        cen = vv - mu
        var = jnp.mean(cen * cen, axis=-1, keepdims=True)
        vn_scr[...] = ((cen * lax.rsqrt(var + NORM_EPS)) * lng_ref[...] + lnb_ref[...]).astype(BF16)
        for g in range(GROUPS):
            wm = _masked_ws(ws_ref, g)
            cols = slice(g * HEAD_DIM, (g + 1) * HEAD_DIM)
            for c in range(chunks):
                rows = slice(c * GMLP_CHUNK, (c + 1) * GMLP_CHUNK)
                mixed = _dot(wm, vn_scr[rows, cols]) + bias_ref[g]
                a_ref[rows, cols] = (_gelu(u_ref[rows, cols]) * mixed).astype(BF16)

    small = pl.BlockSpec((GROUPS, GMLP_CHUNK, GMLP_CHUNK), lambda m: (0, 0, 0))
    vec = pl.BlockSpec((1, D_MODEL), lambda m: (0, 0))
    return pl.pallas_call(
        body, name="gmlp_fwd", out_shape=jax.ShapeDtypeStruct((t, D_MODEL), BF16), grid=(t // tm,),
        in_specs=[pl.BlockSpec((None, tm, D_MODEL), lambda m: (U_POS, m, 0)),
                  pl.BlockSpec((None, tm, D_MODEL), lambda m: (U_POS + 1, m, 0)), vec, vec, small, small],
        out_specs=pl.BlockSpec((tm, D_MODEL), lambda m: (m, 0)),
        scratch_shapes=[pltpu.VMEM((tm, D_MODEL), BF16)],
        compiler_params=_params([((tm, D_MODEL), F32)] * 2 + [((tm, D_MODEL), BF16)] + [((8, 128, 128), F32)] * 2,
                                scratch=[((tm, D_MODEL), BF16)], temps=8 << 20, sem=("arbitrary",)),
    )(proj, proj, ln_g, ln_b, w_s, bias_b)


def _lower_bound(tab_ref):
    t0, t1 = tab_ref[0:1, :], tab_ref[1:2, :]
    mx = jnp.maximum(t0, t1)
    e0, e1 = jnp.exp(t0 - mx), jnp.exp(t1 - mx)
    return e0 / (e0 + e1)


def _tri_masks():
    row = lax.broadcasted_iota(jnp.int32, (HGRN_CHUNK, HGRN_CHUNK), 0)
    col = lax.broadcasted_iota(jnp.int32, (HGRN_CHUNK, HGRN_CHUNK), 1)
    return row >= col, row <= col


def _chunk_rows(c):
    return slice(c * HGRN_CHUNK, (c + 1) * HGRN_CHUNK)


def _per_chunk(x, nc, fn):
    return jnp.concatenate([fn(x[_chunk_rows(c)]) for c in range(nc)], axis=0)


def _chunk_row_bcast(x, nc, i):
    return _per_chunk(x, nc, lambda xc: jnp.broadcast_to(xc[i:i + 1, :], (HGRN_CHUNK, HEAD_DIM)))


def _hgrn_gates(q, fl, lb, nc):
    lower, _ = _tri_masks()
    lower = lower.astype(BF16)
    s = _sigmoid(fl)
    f = lb + (1.0 - lb) * s
    k = 1.0 - f
    hi, mid, lo = _split3(jnp.log(f))
    a = jnp.concatenate([_dot(lower, hi[_chunk_rows(c)]) + _dot(lower, mid[_chunk_rows(c)]) + _dot(lower, lo[_chunk_rows(c)])
                         for c in range(nc)], axis=0)
    a_mid = _chunk_row_bcast(a, nc, HGRN_CHUNK // 2 - 1)
    a_last = _chunk_row_bcast(a, nc, HGRN_CHUNK - 1)
    qs = q * HGRN_SCALE
    e_in, e_out, e_end, e_all = jnp.exp(a - a_mid), jnp.exp(a_mid - a), jnp.exp(a_last - a), jnp.exp(a)
    decay = [jnp.exp(a[c * HGRN_CHUNK + HGRN_CHUNK - 1:(c + 1) * HGRN_CHUNK, :]) for c in range(nc)]
    return dict(s=s, f=f, k=k, decay=decay, e_in=e_in, e_out=e_out, e_end=e_end, e_all=e_all,
                qi=qs * e_in, ki=k * e_out, kd=k * e_end, qe=qs * e_all)


def _hgrn_forward(proj, lb_table, norm_g):
    t = proj.shape[1]
    tb = _tile(t, 512)
    nc = tb // HGRN_CHUNK
    n_chunks = t // HGRN_CHUNK

    def body(q_ref, f_ref, i_ref, g_ref, tab_ref, ng_ref, og_ref, o_ref, st_ref, state):
        @pl.when(pl.program_id(1) == 0)
        def _():
            state[...] = jnp.zeros_like(state)

        lower, _ = _tri_masks()
        gt = _hgrn_gates(q_ref[...], f_ref[...], _lower_bound(tab_ref), nc)
        qi, ki, kd, qe = (gt[n].astype(BF16) for n in ("qi", "ki", "kd", "qe"))
        vb = i_ref[...].astype(BF16)
        o_intra, d_state = [], []
        for c in range(nc):
            rows = _chunk_rows(c)
            p = jnp.where(lower, _dot_nt(qi[rows], ki[rows]), 0.0).astype(BF16)
            o_intra.append(_dot(p, vb[rows]))
            d_state.append(_dot_tn(vb[rows], kd[rows]))
        st = state[...]
        outs = []
        for c in range(nc):
            st_ref[c] = st
            outs.append(o_intra[c] + _dot_nt(qe[_chunk_rows(c)], st.astype(BF16)))
            st = st * gt["decay"][c] + d_state[c]
        state[...] = st
        o = jnp.concatenate(outs, axis=0)
        o_ref[...] = o
        _, oh = _rms_stats(o)
        gz = g_ref[...]
        og_ref[...] = ((oh * ng_ref[...]) * (gz * _sigmoid(gz))).astype(BF16)

    def blk(p):
        return pl.BlockSpec((None, tb, HEAD_DIM), lambda h, n: (p, n, h))

    out_blk = pl.BlockSpec((tb, HEAD_DIM), lambda h, n: (n, h))
    return pl.pallas_call(
        body, name="hgrn_fwd",
        out_shape=[jax.ShapeDtypeStruct((t, D_MODEL), BF16), jax.ShapeDtypeStruct((t, D_MODEL), F32),
                   jax.ShapeDtypeStruct((HEADS, n_chunks, HEAD_DIM, HEAD_DIM), F32)],
        grid=(HEADS, t // tb),
        in_specs=[blk(Q_POS), blk(Q_POS + 1), blk(Q_POS + 2), blk(Q_POS + 3),
                  pl.BlockSpec((2, HEAD_DIM), lambda h, n: (0, h)), pl.BlockSpec((1, HEAD_DIM), lambda h, n: (0, h))],
        out_specs=[out_blk, out_blk, pl.BlockSpec((None, nc, HEAD_DIM, HEAD_DIM), lambda h, n: (h, n, 0, 0))],
        scratch_shapes=[pltpu.VMEM((HEAD_DIM, HEAD_DIM), F32)],
        compiler_params=_params([((tb, HEAD_DIM), F32)] * 6 + [((nc, HEAD_DIM, HEAD_DIM), F32)], temps=8 << 20,
                                sem=("arbitrary", "arbitrary")),
    )(proj, proj, proj, proj, lb_table, norm_g)


def _branch_out_forward(a, og, proj, x, w_a, w_b, w_out, ffn_g):
    t = x.shape[0]
    tm = _tile(t, 256)

    def body(a_ref, og_ref, ga_ref, gb_ref, x_ref, wa_ref, wb_ref, wo_ref, g_ref, ya_ref, yb_ref, mg_ref, x1_ref, h2_ref):
        ya = _dot(a_ref[...], wa_ref[...])
        yb = _dot(og_ref[...], wb_ref[...])
        ya_ref[...] = ya
        yb_ref[...] = yb
        merged = (_sigmoid(ga_ref[...]) * ya + _sigmoid(gb_ref[...]) * yb).astype(BF16)
        mg_ref[...] = merged
        x1 = x_ref[...] + _dot(merged, wo_ref[...])
        x1_ref[...] = x1
        _, xh = _rms_stats(x1)
        h2_ref[...] = (xh * g_ref[...]).astype(BF16)

    tok = pl.BlockSpec((tm, D_MODEL), lambda m: (m, 0))
    wsp = pl.BlockSpec((D_MODEL, D_MODEL), lambda m: (0, 0))
    return pl.pallas_call(
        body, name="branch_out_fwd",
        out_shape=[jax.ShapeDtypeStruct((t, D_MODEL), F32), jax.ShapeDtypeStruct((t, D_MODEL), F32),
                   jax.ShapeDtypeStruct((t, D_MODEL), BF16), jax.ShapeDtypeStruct((t, D_MODEL), F32),
                   jax.ShapeDtypeStruct((t, D_MODEL), BF16)],
        grid=(t // tm,),
        in_specs=[tok, tok, pl.BlockSpec((None, tm, D_MODEL), lambda m: (GATE_POS, m, 0)),
                  pl.BlockSpec((None, tm, D_MODEL), lambda m: (GATE_POS + 1, m, 0)), tok, wsp, wsp, wsp,
                  pl.BlockSpec((1, D_MODEL), lambda m: (0, 0))],
        out_specs=[tok] * 5,
        compiler_params=_params([((tm, D_MODEL), BF16)] * 4 + [((tm, D_MODEL), F32)] * 6 + [((D_MODEL, D_MODEL), BF16)] * 3,
                                temps=8 << 20, sem=("arbitrary",)),
    )(a, og, proj, proj, x, w_a, w_b, w_out, ffn_g)


def _ffn_forward(h2, x1, w_gu, w_down):
    t = x1.shape[0]
    tm = _tile(t, 512)

    def body(h_ref, wg_ref, wu_ref, wd_ref, x1_ref, gu_ref, act_ref, x2_ref, acc):
        j = pl.program_id(1)
        h = h_ref[...]
        gate = _dot(h, wg_ref[...])
        up = _dot(h, wu_ref[...])
        gu_ref[0] = gate
        gu_ref[1] = up
        act = ((gate * _sigmoid(gate)) * up).astype(BF16)
        act_ref[...] = act
        part = _dot(act, wd_ref[...])

        @pl.when(j == 0)
        def _():
            acc[...] = part

        @pl.when(j > 0)
        def _():
            acc[...] += part

        @pl.when(j == 3)
        def _():
            x2_ref[...] = x1_ref[...] + acc[...]

    tok = pl.BlockSpec((tm, D_MODEL), lambda m, j: (m, 0))
    return pl.pallas_call(
        body, name="ffn_fwd",
        out_shape=[jax.ShapeDtypeStruct((4, 2, t, FF_BLOCK), F32), jax.ShapeDtypeStruct((4, t, FF_BLOCK), BF16),
                   jax.ShapeDtypeStruct((t, D_MODEL), F32)],
        grid=(t // tm, 4),
        in_specs=[tok, pl.BlockSpec((None, D_MODEL, FF_BLOCK), lambda m, j: (j, 0, 0)),
                  pl.BlockSpec((None, D_MODEL, FF_BLOCK), lambda m, j: (j + 4, 0, 0)),
                  pl.BlockSpec((FF_BLOCK, D_MODEL), lambda m, j: (j, 0)), tok],
        out_specs=[pl.BlockSpec((None, 2, tm, FF_BLOCK), lambda m, j: (j, 0, m, 0)),
                   pl.BlockSpec((None, tm, FF_BLOCK), lambda m, j: (j, m, 0)), tok],
        scratch_shapes=[pltpu.VMEM((tm, D_MODEL), F32)],
        compiler_params=_params([((tm, D_MODEL), BF16), ((D_MODEL, 768), BF16), ((D_MODEL, 768), BF16),
                                 ((FF_BLOCK, D_MODEL), BF16), ((tm, D_MODEL), F32), ((2, tm, 768), F32),
                                 ((tm, 768), BF16), ((tm, D_MODEL), F32)],
                                scratch=[((tm, D_MODEL), F32)], temps=8 << 20, sem=("arbitrary", "arbitrary")),
    )(h2, w_gu, w_gu, w_down, x1)


def _loss_and_final_backward(x2, target, final_g):
    t = x2.shape[0]
    tm = _tile(t, 256)

    def body(x_ref, t_ref, g_ref, loss_ref, dg_ref, dx_ref, dxb_ref):
        @pl.when(pl.program_id(0) == 0)
        def _():
            loss_ref[...] = jnp.zeros_like(loss_ref)
            dg_ref[...] = jnp.zeros_like(dg_ref)

        x = x_ref[...]
        g = g_ref[...]
        r, xh = _rms_stats(x)
        err = xh * g - t_ref[...]
        loss_ref[...] += 0.5 * jnp.sum(jnp.mean(err * err, axis=-1, keepdims=True), axis=0, keepdims=True)
        dy = err * (1.0 / D_MODEL)
        dg_ref[...] += jnp.sum(dy * xh, axis=0, keepdims=True)
        dxh = dy * g
        dx = r * (dxh - xh * jnp.mean(dxh * xh, axis=-1, keepdims=True))
        dx_ref[...] = dx
        dxb_ref[...] = dx.astype(BF16)

    tok = pl.BlockSpec((tm, D_MODEL), lambda m: (m, 0))
    vec = pl.BlockSpec((1, D_MODEL), lambda m: (0, 0))
    return pl.pallas_call(
        body, name="loss_final_bwd",
        out_shape=[jax.ShapeDtypeStruct((8, 128), F32), jax.ShapeDtypeStruct((1, D_MODEL), F32),
                   jax.ShapeDtypeStruct((t, D_MODEL), F32), jax.ShapeDtypeStruct((t, D_MODEL), BF16)],
        grid=(t // tm,), in_specs=[tok, tok, vec],
        out_specs=[pl.BlockSpec((8, 128), lambda m: (0, 0)), vec, tok, tok],
        compiler_params=_params([((tm, D_MODEL), F32)] * 4, temps=8 << 20, sem=("arbitrary",)),
    )(x2, target, final_g)


def _ffn_backward(dx2b, dx2, gu, x1, w_gu, w_down, ffn_g):
    t = x1.shape[0]
    tm = _tile(t, 512)

    def body(dxb_ref, dx2_ref, gu_ref, x1_ref, wg_ref, wu_ref, wd_ref, g_ref, dgu_ref, dx1_ref, dx1b_ref, dg_ref, acc):
        m, j = pl.program_id(0), pl.program_id(1)

        @pl.when((m == 0) & (j == 0))
        def _():
            dg_ref[...] = jnp.zeros_like(dg_ref)

        dact = _dot_nt(dxb_ref[...], wd_ref[...])
        gate, up = gu_ref[0], gu_ref[1]
        sg = _sigmoid(gate)
        dgate = (dact * up * (sg * (1.0 + gate * (1.0 - sg)))).astype(BF16)
        dup = (dact * (gate * sg)).astype(BF16)
        dgu_ref[0] = dgate
        dgu_ref[1] = dup
        part = _dot_nt(dgate, wg_ref[...]) + _dot_nt(dup, wu_ref[...])

        @pl.when(j == 0)
        def _():
            acc[...] = part

        @pl.when(j > 0)
        def _():
            acc[...] += part

        @pl.when(j == 3)
        def _():
            dx, dg = _rms_bwd(acc[...], x1_ref[...], g_ref[...])
            dx1 = dx2_ref[...] + dx
            dx1_ref[...] = dx1
            dx1b_ref[...] = dx1.astype(BF16)
            dg_ref[...] += dg

    tok = pl.BlockSpec((tm, D_MODEL), lambda m, j: (m, 0))
    vec = pl.BlockSpec((1, D_MODEL), lambda m, j: (0, 0))
    gu_spec = pl.BlockSpec((None, 2, tm, FF_BLOCK), lambda m, j: (j, 0, m, 0))
    return pl.pallas_call(
        body, name="ffn_bwd",
        out_shape=[jax.ShapeDtypeStruct((4, 2, t, FF_BLOCK), BF16), jax.ShapeDtypeStruct((t, D_MODEL), F32),
                   jax.ShapeDtypeStruct((t, D_MODEL), BF16), jax.ShapeDtypeStruct((1, D_MODEL), F32)],
        grid=(t // tm, 4),
        in_specs=[tok, tok, gu_spec, tok, pl.BlockSpec((None, D_MODEL, FF_BLOCK), lambda m, j: (j, 0, 0)),
                  pl.BlockSpec((None, D_MODEL, FF_BLOCK), lambda m, j: (j + 4, 0, 0)),
                  pl.BlockSpec((FF_BLOCK, D_MODEL), lambda m, j: (j, 0)), vec],
        out_specs=[gu_spec, tok, tok, vec],
        scratch_shapes=[pltpu.VMEM((tm, D_MODEL), F32)],
        compiler_params=_params([((tm, D_MODEL), BF16), ((tm, D_MODEL), F32), ((2, tm, 768), F32), ((tm, D_MODEL), F32),
                                 ((D_MODEL, 768), BF16), ((D_MODEL, 768), BF16), ((FF_BLOCK, D_MODEL), BF16),
                                 ((2, tm, 768), BF16), ((tm, D_MODEL), F32), ((tm, D_MODEL), BF16)],
                                scratch=[((tm, D_MODEL), F32)], temps=8 << 20, sem=("arbitrary", "arbitrary")),
    )(dx2b, dx2, gu, x1, w_gu, w_gu, w_down, ffn_g)


def _branch_out_backward(dx1b, ya, yb, proj, w_a, w_b, w_out):
    t = ya.shape[0]
    tm = _tile(t, 256)

    def body(dx_ref, ya_ref, yb_ref, ga_ref, gb_ref, wa_ref, wb_ref, wo_ref, dya_ref, dyb_ref, dgate_ref, da_ref, dog_ref):
        dm = _dot_nt(dx_ref[...], wo_ref[...])
        sa, sb = _sigmoid(ga_ref[...]), _sigmoid(gb_ref[...])
        dya = (dm * sa).astype(BF16)
        dyb = (dm * sb).astype(BF16)
        dya_ref[...] = dya
        dyb_ref[...] = dyb
        dgate_ref[0] = (dm * ya_ref[...] * (sa * (1.0 - sa))).astype(BF16)
        dgate_ref[1] = (dm * yb_ref[...] * (sb * (1.0 - sb))).astype(BF16)
        da_ref[...] = _dot_nt(dya, wa_ref[...])
        dog_ref[...] = _dot_nt(dyb, wb_ref[...])

    tok = pl.BlockSpec((tm, D_MODEL), lambda m: (m, 0))
    wsp = pl.BlockSpec((D_MODEL, D_MODEL), lambda m: (0, 0))
    return pl.pallas_call(
        body, name="branch_out_bwd",
        out_shape=[jax.ShapeDtypeStruct((t, D_MODEL), BF16), jax.ShapeDtypeStruct((t, D_MODEL), BF16),
                   jax.ShapeDtypeStruct((2, t, D_MODEL), BF16), jax.ShapeDtypeStruct((t, D_MODEL), F32),
                   jax.ShapeDtypeStruct((t, D_MODEL), F32)],
        grid=(t // tm,),
        in_specs=[tok, tok, tok, pl.BlockSpec((None, tm, D_MODEL), lambda m: (GATE_POS, m, 0)),
                  pl.BlockSpec((None, tm, D_MODEL), lambda m: (GATE_POS + 1, m, 0)), wsp, wsp, wsp],
        out_specs=[tok, tok, pl.BlockSpec((2, tm, D_MODEL), lambda m: (0, m, 0)), tok, tok],
        compiler_params=_params([((tm, D_MODEL), BF16)] * 5 + [((tm, D_MODEL), F32)] * 6 + [((D_MODEL, D_MODEL), BF16)] * 3,
                                temps=8 << 20, sem=("arbitrary",)),
    )(dx1b, ya, yb, proj, proj, w_a, w_b, w_out)


def _hgrn_backward(dog, o_saved, states, proj, lb_table, norm_g):
    t = proj.shape[1]
    tb = _tile(t, 256)
    nc = tb // HGRN_CHUNK
    nb = t // tb

    def body(dog_ref, o_ref, st_ref, q_ref, f_ref, i_ref, g_ref, tab_ref, ng_ref, dp_ref, dng_ref, dtab_ref, gstate):
        @pl.when(pl.program_id(1) == 0)
        def _():
            gstate[...] = jnp.zeros_like(gstate)
            dng_ref[...] = jnp.zeros_like(dng_ref)
            dtab_ref[...] = jnp.zeros_like(dtab_ref)

        lb = _lower_bound(tab_ref)
        ng = ng_ref[...]
        lower, upper = _tri_masks()
        gt = _hgrn_gates(q_ref[...], f_ref[...], lb, nc)
        qi, ki, kd, qe = (gt[n].astype(BF16) for n in ("qi", "ki", "kd", "qe"))
        vb = i_ref[...].astype(BF16)
        o, gz, d_og = o_ref[...], g_ref[...], dog_ref[...]
        r, oh = _rms_stats(o)
        sg = _sigmoid(gz)
        d_on = d_og * (gz * sg)
        dgz = d_og * (oh * ng) * (sg * (1.0 + gz * (1.0 - sg)))
        dng_ref[...] += jnp.sum(d_on * oh, axis=0, keepdims=True)
        doh = d_on * ng
        dob = (r * (doh - oh * jnp.mean(doh * oh, axis=-1, keepdims=True))).astype(BF16)
        dv_intra, dqi, dki, dqe, g_upd = [], [], [], [], []
        for c in range(nc):
            rows = _chunk_rows(c)
            p = jnp.where(lower, _dot_nt(qi[rows], ki[rows]), 0.0).astype(BF16)
            dv_intra.append(_dot_tn(p, dob[rows]))
            dp = jnp.where(lower, _dot_nt(dob[rows], vb[rows]), 0.0).astype(BF16)
            dqi.append(_dot(dp, ki[rows]))
            dki.append(_dot_tn(dp, qi[rows]))
            dqe.append(_dot(dob[rows], st_ref[c].astype(BF16)))
            g_upd.append(_dot_tn(dob[rows], qe[rows]))
        g_after = [None] * nc
        g = gstate[...]
        for c in reversed(range(nc)):
            g_after[c] = g
            g = g * gt["decay"][c] + g_upd[c]
        gstate[...] = g
        dkd, dv, da_last = [], [], []
        for c in range(nc):
            rows = _chunk_rows(c)
            gb = g_after[c].astype(BF16)
            dkd.append(_dot(vb[rows], gb))
            dv.append(dv_intra[c] + _dot_nt(kd[rows], gb))
            da_last.append(jnp.sum(g_after[c] * st_ref[c], axis=0, keepdims=True) * gt["decay"][c])
        dqi, dki, dqe, dkd, dv = (jnp.concatenate(z, axis=0) for z in (dqi, dki, dqe, dkd, dv))
        dqs = dqi * gt["e_in"] + dqe * gt["e_all"]
        dk = dki * gt["e_out"] + dkd * gt["e_end"]
        t_in, t_out, t_end = dqi * gt["qi"], dki * gt["ki"], dkd * gt["kd"]
        da = t_in - t_out + dqe * gt["qe"] - t_end
        row = lax.broadcasted_iota(jnp.int32, (HGRN_CHUNK, HEAD_DIM), 0)
        d_mid = t_out - t_in
        pieces = []
        for c in range(nc):
            rows = _chunk_rows(c)
            da_mid = jnp.sum(d_mid[rows], axis=0, keepdims=True)
            da_end = jnp.sum(t_end[rows], axis=0, keepdims=True) + da_last[c]
            da_c = da[rows] + jnp.where(row == HGRN_CHUNK // 2 - 1, da_mid, 0.0) + jnp.where(row == HGRN_CHUNK - 1, da_end, 0.0)
            pieces.append(_mask_mm(upper.astype(BF16), da_c))
        df = jnp.concatenate(pieces, axis=0) / gt["f"] - dk
        s = gt["s"]
        dlb = jnp.sum(df * (1.0 - s), axis=0, keepdims=True)
        dp_ref[0] = (dqs * HGRN_SCALE).astype(BF16)
        dp_ref[1] = (df * (1.0 - lb) * (s * (1.0 - s))).astype(BF16)
        dp_ref[2] = dv.astype(BF16)
        dp_ref[3] = dgz.astype(BF16)
        dt0 = dlb * (lb * (1.0 - lb))
        dtab_ref[0:1, :] += dt0
        dtab_ref[1:2, :] -= dt0

    def blk(p):
        return pl.BlockSpec((None, tb, HEAD_DIM), lambda h, n: (p, nb - 1 - n, h))

    tok = pl.BlockSpec((tb, HEAD_DIM), lambda h, n: (nb - 1 - n, h))
    return pl.pallas_call(
        body, name="hgrn_bwd",
        out_shape=[jax.ShapeDtypeStruct((N_DEV, t, D_MODEL), BF16), jax.ShapeDtypeStruct((1, D_MODEL), F32),
                   jax.ShapeDtypeStruct((2, D_MODEL), F32)],
        grid=(HEADS, nb),
        in_specs=[tok, tok, pl.BlockSpec((None, nc, HEAD_DIM, HEAD_DIM), lambda h, n: (h, nb - 1 - n, 0, 0)),
                  blk(Q_POS), blk(Q_POS + 1), blk(Q_POS + 2), blk(Q_POS + 3),
                  pl.BlockSpec((2, HEAD_DIM), lambda h, n: (0, h)), pl.BlockSpec((1, HEAD_DIM), lambda h, n: (0, h))],
        out_specs=[pl.BlockSpec((4, tb, HEAD_DIM), lambda h, n: (0, nb - 1 - n, h)),
                   pl.BlockSpec((1, HEAD_DIM), lambda h, n: (0, h)), pl.BlockSpec((2, HEAD_DIM), lambda h, n: (0, h))],
        scratch_shapes=[pltpu.VMEM((HEAD_DIM, HEAD_DIM), F32)],
        compiler_params=_params([((tb, HEAD_DIM), F32)] * 6 + [((nc, HEAD_DIM, HEAD_DIM), F32)] + [((4, tb, HEAD_DIM), BF16)],
                                temps=8 << 20, sem=("arbitrary", "arbitrary")),
    )(dog, o_saved, states, proj, proj, proj, proj, lb_table, norm_g)


def _gmlp_backward(dproj, da, proj, ln_g, ln_b, w_s, bias_b):
    t = proj.shape[1]
    tm = _tile(t, 256)
    chunks = tm // GMLP_CHUNK

    def body(_, da_ref, u_ref, v_ref, lng_ref, lnb_ref, ws_ref, bias_ref, dp_ref, dlng_ref, dlnb_ref, dws_ref, dbs_ref,
             vn_scr, dvn_scr):
        @pl.when(pl.program_id(0) == 0)
        def _():
            dlng_ref[...] = jnp.zeros_like(dlng_ref)
            dlnb_ref[...] = jnp.zeros_like(dlnb_ref)
            dws_ref[...] = jnp.zeros_like(dws_ref)
            dbs_ref[...] = jnp.zeros_like(dbs_ref)

        v = v_ref[...]
        vv = _gelu(v)
        mu = jnp.mean(vv, axis=-1, keepdims=True)
        cen = vv - mu
        rstd = lax.rsqrt(jnp.mean(cen * cen, axis=-1, keepdims=True) + NORM_EPS)
        vhat = cen * rstd
        lng = lng_ref[...]
        vn_scr[...] = (vhat * lng + lnb_ref[...]).astype(BF16)
        row = lax.broadcasted_iota(jnp.int32, (GMLP_CHUNK, GMLP_CHUNK), 0)
        col = lax.broadcasted_iota(jnp.int32, (GMLP_CHUNK, GMLP_CHUNK), 1)
        for g in range(GROUPS):
            wm = _masked_ws(ws_ref, g)
            cols = slice(g * HEAD_DIM, (g + 1) * HEAD_DIM)
            dws = jnp.zeros((GMLP_CHUNK, GMLP_CHUNK), F32)
            dbs = jnp.zeros((GMLP_CHUNK, GMLP_CHUNK), F32)
            for c in range(chunks):
                rows = slice(c * GMLP_CHUNK, (c + 1) * GMLP_CHUNK)
                vn = vn_scr[rows, cols]
                mixed = _dot(wm, vn) + bias_ref[g]
                u = u_ref[rows, cols]
                d_a = da_ref[rows, cols]
                dp_ref[0, rows, cols] = (d_a * mixed * _gelu_grad(u)).astype(BF16)
                dmix = d_a * _gelu(u)
                dmb = dmix.astype(BF16)
                dbs = dbs + dmix
                dws = dws + _dot_nt(dmb, vn)
                dvn_scr[rows, cols] = _dot_tn(wm, dmb)
            dws_ref[g] += jnp.where(row >= col, dws, 0.0)
            dbs_ref[g] += jnp.broadcast_to(jnp.sum(dbs, axis=-1, keepdims=True), (GMLP_CHUNK, GMLP_CHUNK))
        dvn = dvn_scr[...]
        dlng_ref[...] += jnp.sum(dvn * vhat, axis=0, keepdims=True)
        dlnb_ref[...] += jnp.sum(dvn, axis=0, keepdims=True)
        dvh = dvn * lng
        dvv = rstd * (dvh - jnp.mean(dvh, axis=-1, keepdims=True) - vhat * jnp.mean(dvh * vhat, axis=-1, keepdims=True))
        dp_ref[1] = (dvv * _gelu_grad(v)).astype(BF16)

    tok = pl.BlockSpec((tm, D_MODEL), lambda m: (m, 0))
    small = pl.BlockSpec((GROUPS, GMLP_CHUNK, GMLP_CHUNK), lambda m: (0, 0, 0))
    vec = pl.BlockSpec((1, D_MODEL), lambda m: (0, 0))
    return pl.pallas_call(
        body, name="gmlp_bwd",
        out_shape=[jax.ShapeDtypeStruct(dproj.shape, BF16), jax.ShapeDtypeStruct((1, D_MODEL), F32),
                   jax.ShapeDtypeStruct((1, D_MODEL), F32), jax.ShapeDtypeStruct((GROUPS, GMLP_CHUNK, GMLP_CHUNK), F32),
                   jax.ShapeDtypeStruct((GROUPS, GMLP_CHUNK, GMLP_CHUNK), F32)],
        grid=(t // tm,),
        in_specs=[ANY, tok, pl.BlockSpec((None, tm, D_MODEL), lambda m: (U_POS, m, 0)),
                  pl.BlockSpec((None, tm, D_MODEL), lambda m: (U_POS + 1, m, 0)), vec, vec, small, small],
        out_specs=[pl.BlockSpec((2, tm, D_MODEL), lambda m: (U_POS // 2, m, 0)), vec, vec, small, small],
        scratch_shapes=[pltpu.VMEM((tm, D_MODEL), BF16), pltpu.VMEM((tm, D_MODEL), F32)],
        input_output_aliases={0: 0},
        compiler_params=_params([((tm, D_MODEL), F32)] * 3 + [((2, tm, D_MODEL), BF16)] + [((8, 128, 128), F32)] * 4,
                                scratch=[((tm, D_MODEL), BF16), ((tm, D_MODEL), F32)], temps=12 << 20, sem=("arbitrary",)),
    )(dproj, da, proj, proj, ln_g, ln_b, w_s, bias_b)


def _place_gate_grads(dproj, dgates):
    t = dgates.shape[1]
    tm = _tile(t, 512)

    def body(_, src_ref, dst_ref):
        dst_ref[...] = src_ref[...]

    return pl.pallas_call(
        body, name="place_gate_grads", out_shape=jax.ShapeDtypeStruct(dproj.shape, BF16), grid=(t // tm,),
        in_specs=[ANY, pl.BlockSpec((2, tm, D_MODEL), lambda m: (0, m, 0))],
        out_specs=pl.BlockSpec((2, tm, D_MODEL), lambda m: (GATE_POS // 2, m, 0)),
        input_output_aliases={0: 0},
        compiler_params=_params([((2, tm, D_MODEL), BF16)] * 2, sem=("arbitrary",)),
    )(dproj, dgates)


def _input_backward(dproj, w_in_g, x, dx1, mix_g):
    t = x.shape[0]
    tm = _tile(t, 512)

    def body(dp_ref, w_ref, x_ref, dx1_ref, g_ref, dx_ref, dg_ref, acc):
        m, p = pl.program_id(0), pl.program_id(1)

        @pl.when((m == 0) & (p == 0))
        def _():
            dg_ref[...] = jnp.zeros_like(dg_ref)

        part = _dot_nt(dp_ref[...], w_ref[...])

        @pl.when(p == 0)
        def _():
            acc[...] = part

        @pl.when(p > 0)
        def _():
            acc[...] += part

        @pl.when(p == N_DEV - 1)
        def _():
            dx, dg = _rms_bwd(acc[...], x_ref[...], g_ref[...])
            dx_ref[...] = dx1_ref[...] + dx
            dg_ref[...] += dg

    tok = pl.BlockSpec((tm, D_MODEL), lambda m, p: (m, 0))
    vec = pl.BlockSpec((1, D_MODEL), lambda m, p: (0, 0))
    return pl.pallas_call(
        body, name="input_bwd",
        out_shape=[jax.ShapeDtypeStruct((t, D_MODEL), F32), jax.ShapeDtypeStruct((1, D_MODEL), F32)],
        grid=(t // tm, N_DEV),
        in_specs=[pl.BlockSpec((None, tm, D_MODEL), lambda m, p: (p, m, 0)),
                  pl.BlockSpec((None, D_MODEL, D_MODEL), lambda m, p: (p, 0, 0)), tok, tok, vec],
        out_specs=[tok, vec],
        scratch_shapes=[pltpu.VMEM((tm, D_MODEL), F32)],
        compiler_params=_params([((tm, D_MODEL), BF16), ((D_MODEL, D_MODEL), BF16)] + [((tm, D_MODEL), F32)] * 3,
                                scratch=[((tm, D_MODEL), F32)], temps=8 << 20, sem=("arbitrary", "arbitrary")),
    )(dproj, w_in_g, x, dx1, mix_g)


def _weight_grad(name, a, b, a_spec, b_spec, out_shape, out_spec, grid, blocks):
    def body(a_ref, b_ref, o_ref):
        part = _dot_tn(a_ref[...], b_ref[...])
        m = pl.program_id(len(grid) - 1)

        @pl.when(m == 0)
        def _():
            o_ref[...] = part

        @pl.when(m > 0)
        def _():
            o_ref[...] += part

    return pl.pallas_call(
        body, name=name, out_shape=jax.ShapeDtypeStruct(out_shape, F32), grid=grid, in_specs=[a_spec, b_spec],
        out_specs=out_spec, compiler_params=_params(blocks, temps=8 << 20, sem=("arbitrary",) * len(grid)),
    )(a, b)


def _pack_small(mix_g, ln_g, ln_b, w_s, b_s, lb_table, hg_norm, ffn_g, final_g):
    def part(a):
        a = a.reshape(-1, D_MODEL)
        return jnp.pad(a, ((0, 8 - a.shape[0]), (0, 0)))

    return jnp.concatenate([part(mix_g), part(ln_g), part(ln_b), part(hg_norm), part(ffn_g), part(final_g),
                            part(lb_table), part(b_s), w_s.reshape(GMLP_CHUNK, D_MODEL)], axis=0)


def _unpack_small(pack):
    return dict(norm_mix_g=pack[0:1], gmlp_ln_g=pack[8:9], gmlp_ln_b=pack[16:17], hgrn_norm_g=pack[24:25],
                norm_ffn_g=pack[32:33], norm_final_g=pack[40], hgrn_lb_table=pack[48:50],
                gmlp_b_s=pack[56:57].reshape(1, GROUPS, GMLP_CHUNK),
                gmlp_w_s=pack[64:192].reshape(1, GROUPS, GMLP_CHUNK, GMLP_CHUNK))


def _adamw_small(gathered, w, m, v):
    rows = w.shape[0]

    def body(p_ref, w_ref, m_ref, v_ref, g_out, d_out, m_out, v_out):
        g = p_ref[0]
        for j in range(1, N_DEV):
            g = g + p_ref[j]
        delta, m_new, v_new = _adamw_math(w_ref[...], g, m_ref[...], v_ref[...])
        g_out[...] = g
        d_out[...] = delta
        m_out[...] = m_new
        v_out[...] = v_new

    tr = 64
    spec = pl.BlockSpec((tr, D_MODEL), lambda r: (r, 0))
    return pl.pallas_call(
        body, name="adamw_small", out_shape=[jax.ShapeDtypeStruct((rows, D_MODEL), F32)] * 4, grid=(rows // tr,),
        in_specs=[pl.BlockSpec((N_DEV, tr, D_MODEL), lambda r: (0, r, 0)), spec, spec, spec], out_specs=[spec] * 4,
        compiler_params=_params([((N_DEV, tr, D_MODEL), F32)] + [((tr, D_MODEL), F32)] * 7, sem=("arbitrary",)),
    )(gathered, w, m, v)


def kernel(x, norm_mix_g, w_in, gmlp_ln_g, gmlp_ln_b, gmlp_w_s, gmlp_b_s, hgrn_lb_table, hgrn_norm_g, w_branch_a, w_branch_b, w_out, norm_ffn_g, w_gate_up, w_down, norm_final_g, loss_target, m_norm_mix_g, m_w_in, m_gmlp_ln_g, m_gmlp_ln_b, m_gmlp_w_s, m_gmlp_b_s, m_hgrn_lb_table, m_hgrn_norm_g, m_w_branch_a, m_w_branch_b, m_w_out, m_norm_ffn_g, m_w_gate_up, m_w_down, m_norm_final_g, v_norm_mix_g, v_w_in, v_gmlp_ln_g, v_gmlp_ln_b, v_gmlp_w_s, v_gmlp_b_s, v_hgrn_lb_table, v_hgrn_norm_g, v_w_branch_a, v_w_branch_b, v_w_out, v_norm_ffn_g, v_w_gate_up, v_w_down, v_norm_final_g):
    t = x.shape[1]
    x2d = x.reshape(t, D_MODEL)
    target = loss_target.reshape(t, D_MODEL)
    final_g = norm_final_g.reshape(1, D_MODEL)

    shards = [w_in[0].astype(BF16), w_branch_a[0].astype(BF16), w_branch_b[0].astype(BF16), w_out[0].astype(BF16),
              w_gate_up[0].astype(BF16), w_down[0].astype(BF16)]

    def rows_of(n):
        return lambda ref, j: ref.at[pl.ds(pl.multiple_of(j * n, 8), n)]

    w_in_g, w_a, w_b, w_o, w_gu, w_dn = _all_gather(
        "weights_all_gather", shards,
        [((N_DEV, D_MODEL, D_MODEL), BF16), ((D_MODEL, D_MODEL), BF16), ((D_MODEL, D_MODEL), BF16),
         ((D_MODEL, D_MODEL), BF16), ((N_DEV, D_MODEL, FF_BLOCK), BF16), ((D_FF, D_MODEL), BF16)],
        [lambda ref, j: ref.at[_pos_of_dev(j)], rows_of(BRANCH_ROWS), rows_of(BRANCH_ROWS), rows_of(BRANCH_ROWS),
         lambda ref, j: ref.at[j], rows_of(DOWN_ROWS)])

    h = _rms_forward(x2d, norm_mix_g)
    proj = _proj_forward(h, w_in_g)
    bias_b = jnp.broadcast_to(gmlp_b_s[0][:, :, None], (GROUPS, GMLP_CHUNK, GMLP_CHUNK))
    a = _gmlp_forward(proj, gmlp_ln_g, gmlp_ln_b, gmlp_w_s[0], bias_b)
    og, o_saved, states = _hgrn_forward(proj, hgrn_lb_table, hgrn_norm_g)
    ya, yb, merged, x1, h2 = _branch_out_forward(a, og, proj, x2d, w_a, w_b, w_o, norm_ffn_g)
    gu, act, x2 = _ffn_forward(h2, x1, w_gu, w_dn)
    loss_tile, d_final_g, dx2, dx2b = _loss_and_final_backward(x2, target, final_g)

    dgu, dx1, dx1b, d_ffn_g = _ffn_backward(dx2b, dx2, gu, x1, w_gu, w_dn, norm_ffn_g)
    dya, dyb, dgates, da, dog = _branch_out_backward(dx1b, ya, yb, proj, w_a, w_b, w_o)
    dproj, d_hg_norm, d_lb = _hgrn_backward(dog, o_saved, states, proj, hgrn_lb_table, hgrn_norm_g)
    dproj, d_ln_g, d_ln_b, d_ws, d_bs = _gmlp_backward(dproj, da, proj, gmlp_ln_g, gmlp_ln_b, gmlp_w_s[0], bias_b)
    dproj = _place_gate_grads(dproj, dgates)
    grad_x, d_mix_g = _input_backward(dproj, w_in_g, x2d, dx1, norm_mix_g)

    tm = _tile(t, 512)
    nm = t // tm
    tok_a = pl.BlockSpec((tm, D_MODEL), lambda m: (m, 0))
    full_o = pl.BlockSpec((D_MODEL, D_MODEL), lambda m: (0, 0))
    sq_blocks = [((tm, D_MODEL), BF16)] * 2 + [((D_MODEL, D_MODEL), F32)]
    g_in = _weight_grad(
        "grad_w_in", h, dproj, pl.BlockSpec((tm, D_MODEL), lambda p, m: (m, 0)),
        pl.BlockSpec((None, tm, D_MODEL), lambda p, m: (p, m, 0)), (N_DEV, D_MODEL, D_MODEL),
        pl.BlockSpec((None, D_MODEL, D_MODEL), lambda p, m: (p, 0, 0)), (N_DEV, nm), sq_blocks)
    g_a = _weight_grad("grad_w_a", a, dya, tok_a, tok_a, (D_MODEL, D_MODEL), full_o, (nm,), sq_blocks)
    g_b = _weight_grad("grad_w_b", og, dyb, tok_a, tok_a, (D_MODEL, D_MODEL), full_o, (nm,), sq_blocks)
    g_o = _weight_grad("grad_w_out", merged, dx1b, tok_a, tok_a, (D_MODEL, D_MODEL), full_o, (nm,), sq_blocks)
    g_gu = _weight_grad(
        "grad_w_gate_up", h2, dgu, pl.BlockSpec((tm, D_MODEL), lambda j, m: (m, 0)),
        pl.BlockSpec((None, None, tm, FF_BLOCK), lambda j, m: (j % 4, j // 4, m, 0)), (N_DEV, D_MODEL, FF_BLOCK),
        pl.BlockSpec((None, D_MODEL, FF_BLOCK), lambda j, m: (j, 0, 0)), (N_DEV, nm),
        [((tm, D_MODEL), BF16), ((tm, 768), BF16), ((D_MODEL, 768), F32)])
    g_dn = _weight_grad(
        "grad_w_down", act, dx2b, pl.BlockSpec((None, tm, FF_BLOCK), lambda j, m: (j, m, 0)),
        pl.BlockSpec((tm, D_MODEL), lambda j, m: (m, 0)), (D_FF, D_MODEL),
        pl.BlockSpec((FF_BLOCK, D_MODEL), lambda j, m: (j, 0)), (4, nm),
        [((tm, 768), BF16), ((tm, D_MODEL), BF16), ((FF_BLOCK, D_MODEL), F32)])

    shard_fns = [lambda ref, j: ref.at[_pos_of_dev(j)], rows_of(BRANCH_ROWS), rows_of(BRANCH_ROWS), rows_of(BRANCH_ROWS),
                 lambda ref, j: ref.at[j], rows_of(DOWN_ROWS)]
    shard_shapes = [(D_MODEL, D_MODEL), (BRANCH_ROWS, D_MODEL), (BRANCH_ROWS, D_MODEL), (BRANCH_ROWS, D_MODEL),
                    (D_MODEL, FF_BLOCK), (DOWN_ROWS, D_MODEL)]
    names = ["w_in", "w_branch_a", "w_branch_b", "w_out", "w_gate_up", "w_down"]
    grads = [g_in, g_a, g_b, g_o, g_gu, g_dn]
    land = _exchange_sibling("grads_to_sibling", grads, shard_fns, shard_shapes)
    core = lax.axis_index("c").astype(jnp.int32).reshape(1)
    chip = (2 * lax.axis_index("x") + lax.axis_index("y")).astype(jnp.int32).reshape(1)
    branch = ((BRANCH_ROWS, D_MODEL), lambda q, r, c: (2 * q + c, 0))
    own_blocks = [((None, 256, D_MODEL), lambda q, r, c: (_pos_of_dev(2 * q + c), r, 0)), branch, branch, branch,
                  ((None, 256, FF_BLOCK), lambda q, r, c: (2 * q + c, r, 0)),
                  ((DOWN_ROWS // 2, D_MODEL), lambda q, r, c: (2 * (2 * q + c) + r, 0))]
    partials = [_chip_partial("chip_partial_" + nme, core, g_, blk, idx, l_)
                for nme, g_, (blk, idx), l_ in zip(names, grads, own_blocks, land)]
    landed = _exchange_chips("grads_to_chips", partials)
    big = {}
    for nme, own, lnd, w, m, v in zip(
            names, partials, landed, [w_in, w_branch_a, w_branch_b, w_out, w_gate_up, w_down],
            [m_w_in, m_w_branch_a, m_w_branch_b, m_w_out, m_w_gate_up, m_w_down],
            [v_w_in, v_w_branch_a, v_w_branch_b, v_w_out, v_w_gate_up, v_w_down]):
        outs = _adamw("adamw_" + nme, chip, own, lnd, w[0], m[0], v[0])
        big[nme] = [o_[None] for o_ in outs]

    d_bs_row = d_bs[:, :, 0]
    small_partial = _pack_small(d_mix_g, d_ln_g, d_ln_b, d_ws, d_bs_row, d_lb, d_hg_norm, d_ffn_g, d_final_g)
    (small_all,) = _all_gather("small_grads_all_gather", [small_partial], [((N_DEV, SMALL_ROWS, D_MODEL), F32)],
                               [lambda ref, j: ref.at[j]])

    def packed(prefix_vals):
        return _pack_small(*prefix_vals)

    w_pack = packed([norm_mix_g, gmlp_ln_g, gmlp_ln_b, gmlp_w_s, gmlp_b_s, hgrn_lb_table, hgrn_norm_g, norm_ffn_g, norm_final_g])
    m_pack = packed([m_norm_mix_g, m_gmlp_ln_g, m_gmlp_ln_b, m_gmlp_w_s, m_gmlp_b_s, m_hgrn_lb_table, m_hgrn_norm_g, m_norm_ffn_g, m_norm_final_g])
    v_pack = packed([v_norm_mix_g, v_gmlp_ln_g, v_gmlp_ln_b, v_gmlp_w_s, v_gmlp_b_s, v_hgrn_lb_table, v_hgrn_norm_g, v_norm_ffn_g, v_norm_final_g])
    small = [_unpack_small(p) for p in _adamw_small(small_all, w_pack, m_pack, v_pack)]

    loss = lax.psum(loss_tile[0, 0], ("x", "y", "c"))
    order = ["norm_mix_g", "w_in", "gmlp_ln_g", "gmlp_ln_b", "gmlp_w_s", "gmlp_b_s", "hgrn_lb_table", "hgrn_norm_g",
             "w_branch_a", "w_branch_b", "w_out", "norm_ffn_g", "w_gate_up", "w_down", "norm_final_g"]
    outs = [loss, grad_x.reshape(1, t, D_MODEL)]
    for kind in range(4):
        for nme in order:
            outs.append(big[nme][kind] if nme in big else small[kind][nme])
    return tuple(outs)
```

```python
import functools

import jax
import jax.numpy as jnp
from jax import lax
from jax.experimental import pallas as pl
from jax.experimental.pallas import tpu as pltpu
from jax.experimental.pallas import tpu_sc as plsc

F32, BF16 = jnp.float32, jnp.bfloat16
D_MODEL = 1024
N_DEV = 8
HEADS = 8
HEAD_DIM = 128
GROUPS = 8
GMLP_CHUNK = 128
HGRN_CHUNK = 64
HGRN_SCALE = HEAD_DIM ** -0.5
D_FF = 2816
FF_BLOCK = D_FF // 4
DOWN_ROWS = D_FF // N_DEV
BRANCH_ROWS = D_MODEL // N_DEV
NORM_EPS = 1e-6
ADAM_LR, ADAM_B1, ADAM_B2, ADAM_EPS, ADAM_WD, ADAM_STEP = 0.001, 0.9, 0.999, 1e-08, 0.01, 10
SMALL_ROWS = 192
V7X_VMEM_BYTES = 64 * 1024 * 1024
VMEM_CAP = V7X_VMEM_BYTES - 6 * 1024 * 1024
MESH_ID = pl.DeviceIdType.MESH
ANY = pl.BlockSpec(memory_space=pl.ANY)
Q_POS, U_POS, GATE_POS = 0, 4, 6


def _pos_of_dev(j):
    return jnp.where(j < 2, j + 4, jnp.where(j < 6, j - 2, j))


def _dev_of_pos(p):
    return jnp.where(p < 4, p + 2, jnp.where(p < 6, p - 4, p))


def _nbytes(shape, dtype):
    n = 1
    for s in shape:
        n *= s
    return n * jnp.dtype(dtype).itemsize


def _params(blocks, scratch=(), temps=0, sem=None):
    need = 2 * sum(_nbytes(s, d) for s, d in blocks) + sum(_nbytes(s, d) for s, d in scratch) + temps
    assert need + (4 << 20) <= VMEM_CAP, need
    return pltpu.CompilerParams(dimension_semantics=sem, vmem_limit_bytes=VMEM_CAP)


def _tile(n, pref):
    return pref if n % pref == 0 else n


def _dot(a, b):
    return jnp.dot(a, b, preferred_element_type=F32)


def _dot_nt(a, b):
    return lax.dot_general(a, b, (((1,), (1,)), ((), ())), preferred_element_type=F32)


def _dot_tn(a, b):
    return lax.dot_general(a, b, (((0,), (0,)), ((), ())), preferred_element_type=F32)


def _sigmoid(x):
    return 1.0 / (1.0 + jnp.exp(-x))


_GELU_C = 0.7978845608028654


def _gelu(x):
    return x * (0.5 * (1.0 + jnp.tanh(_GELU_C * (x + 0.044715 * (x * x * x)))))


def _gelu_grad(x):
    t = jnp.tanh(_GELU_C * (x + 0.044715 * (x * x * x)))
    return 0.5 * (1.0 + t) + 0.5 * x * (1.0 - t * t) * (_GELU_C * (1.0 + 3.0 * 0.044715 * x * x))


def _rms_stats(x):
    r = lax.rsqrt(jnp.mean(x * x, axis=-1, keepdims=True) + NORM_EPS)
    return r, x * r


def _rms_bwd(dy, x, g):
    r, xh = _rms_stats(x)
    dg = jnp.sum(dy * xh, axis=0, keepdims=True)
    dxh = dy * g
    dx = r * (dxh - xh * jnp.mean(dxh * xh, axis=-1, keepdims=True))
    return dx, dg


def _split3(x):
    hi = x.astype(BF16)
    r = x - hi.astype(F32)
    mid = r.astype(BF16)
    lo = (r - mid.astype(F32)).astype(BF16)
    return hi, mid, lo


def _mask_mm(mask_bf16, x):
    hi, mid, lo = _split3(x)
    return _dot(mask_bf16, hi) + _dot(mask_bf16, mid) + _dot(mask_bf16, lo)


def _place():
    return lax.axis_index("x"), lax.axis_index("y"), lax.axis_index("c")


def _gather_copies(src, out, send, recv, loc, slicers):
    n = len(src)
    x, y, c = _place()
    me, sib = (x, y, c), (x, y, 1 - c)
    chips = [(1 - x, y), (x, 1 - y), (1 - x, 1 - y)]

    def dev(p):
        return 4 * p[0] + 2 * p[1] + p[2]

    def rc(i, k, block, to, from_src=False):
        dst = slicers[i](out[i], dev(block))
        return pltpu.make_async_remote_copy(
            src_ref=src[i] if from_src else dst, dst_ref=dst, send_sem=send.at[7 * i + k],
            recv_sem=recv.at[7 * i + k], device_id=to, device_id_type=MESH_ID)

    mine = [pltpu.make_async_copy(src[i], slicers[i](out[i], dev(me)), loc.at[i]) for i in range(n)]
    for cp in mine:
        cp.start()
    first = []
    for i in range(n):
        first.append(rc(i, 0, me, sib, True))
        for j, chip in enumerate(chips):
            first.append(rc(i, 1 + j, me, (*chip, c), True))
    for cp in first:
        cp.start()
    passed = []
    for j, chip in enumerate(chips):
        for i in range(n):
            rc(i, 1 + j, (*chip, c), me).wait_recv()
            cp = rc(i, 4 + j, (*chip, c), sib)
            cp.start()
            passed.append(cp)
    for i in range(n):
        rc(i, 0, sib, me).wait_recv()
        for j, chip in enumerate(chips):
            rc(i, 4 + j, (*chip, 1 - c), me).wait_recv()
    for cp in first + passed:
        cp.wait_send()
    for cp in mine:
        cp.wait()


def _gather_scratch(n):
    return [pltpu.SemaphoreType.DMA((7 * n,)), pltpu.SemaphoreType.DMA((7 * n,)), pltpu.SemaphoreType.DMA((n,))]


def _all_gather(name, srcs, out_shapes, slicers):
    n = len(srcs)

    def body(*refs):
        _gather_copies(refs[:n], refs[n:2 * n], *refs[2 * n:], slicers)

    return pl.pallas_call(
        body, name=name, out_shape=[jax.ShapeDtypeStruct(s, d) for s, d in out_shapes],
        in_specs=[ANY] * n, out_specs=[ANY] * n, scratch_shapes=_gather_scratch(n),
    )(*srcs)


def _handshake(peers):
    barrier = pltpu.get_barrier_semaphore()
    for peer in peers:
        pl.semaphore_signal(barrier, inc=1, device_id=peer, device_id_type=MESH_ID)
    pl.semaphore_wait(barrier, len(peers))


def _all_gather_async(name, collective_id, srcs, out_shapes, slicers):
    n = len(srcs)

    def body(*refs):
        x, y, c = _place()
        _handshake([(1 - x if dx else x, 1 - y if dy else y, 1 - c if dc else c)
                    for dx in (0, 1) for dy in (0, 1) for dc in (0, 1) if dx or dy or dc])
        _gather_copies(refs[:n], refs[n:2 * n], *refs[2 * n:], slicers)

    return pl.kernel(
        body, out_type=[jax.ShapeDtypeStruct(s, d) for s, d in out_shapes],
        mesh=plsc.ScalarSubcoreMesh(axis_name="sequencer", num_cores=1), name=name, scratch_types=_gather_scratch(n),
        compiler_params=pltpu.CompilerParams(collective_id=collective_id),
    )(*srcs)


def _exchange_sibling(name, grads, shard_fns, shard_shapes):
    n = len(grads)

    def body(*refs):
        g, land = refs[:n], refs[n:2 * n]
        send, recv = refs[2 * n:]
        x, y, c = _place()
        remote = []
        for i in range(n):
            for q in range(4):
                cp = pltpu.make_async_remote_copy(
                    src_ref=shard_fns[i](g[i], 2 * q + (1 - c)), dst_ref=land[i].at[q], send_sem=send.at[4 * i + q],
                    recv_sem=recv.at[4 * i + q], device_id=(x, y, 1 - c), device_id_type=MESH_ID)
                cp.start()
                remote.append(cp)
        for cp in remote:
            cp.wait()

    return pl.pallas_call(
        body, name=name, out_shape=[jax.ShapeDtypeStruct((4, *s), F32) for s in shard_shapes],
        in_specs=[ANY] * n, out_specs=[ANY] * n,
        scratch_shapes=[pltpu.SemaphoreType.DMA((4 * n,)), pltpu.SemaphoreType.DMA((4 * n,))],
    )(*grads)


def _exchange_chips(name, parts):
    n = len(parts)

    def body(*refs):
        part, out = refs[:n], refs[n:2 * n]
        send, recv = refs[2 * n:]
        x, y, c = _place()
        remote = []
        for i in range(n):
            for s in range(3):
                qx = 1 - x if (s + 1) // 2 else x
                qy = 1 - y if (s + 1) % 2 else y
                cp = pltpu.make_async_remote_copy(
                    src_ref=part[i].at[2 * qx + qy], dst_ref=out[i].at[s], send_sem=send.at[3 * i + s],
                    recv_sem=recv.at[3 * i + s], device_id=(qx, qy, c), device_id_type=MESH_ID)
                cp.start()
                remote.append(cp)
        for cp in remote:
            cp.wait()

    return pl.pallas_call(
        body, name=name, out_shape=[jax.ShapeDtypeStruct((3, *p.shape[1:]), p.dtype) for p in parts],
        in_specs=[ANY] * n, out_specs=[ANY] * n,
        scratch_shapes=[pltpu.SemaphoreType.DMA((3 * n,)), pltpu.SemaphoreType.DMA((3 * n,))],
    )(*parts)


def _chip_partial(name, core, grad, own_block, own_index, land):
    _, rows, cols = land.shape
    tr = own_block[-2]

    def body(core_ref, a_ref, b_ref, o_ref):
        o_ref[...] = (a_ref[...] + b_ref[...]).astype(BF16)

    spec = pl.BlockSpec((None, tr, cols), lambda q, r, c: (q, r, 0))
    return pl.pallas_call(
        body, name=name, out_shape=jax.ShapeDtypeStruct(land.shape, BF16),
        grid_spec=pltpu.PrefetchScalarGridSpec(
            num_scalar_prefetch=1, grid=(4, rows // tr),
            in_specs=[pl.BlockSpec(own_block, lambda q, r, c: own_index(q, r, c[0])), spec], out_specs=spec),
        compiler_params=_params([((tr, cols), F32)] * 2 + [((tr, cols), BF16)], sem=("arbitrary", "arbitrary")),
    )(core, grad, land)


def _adamw_math(w, g, m, v):
    m = ADAM_B1 * m + (1.0 - ADAM_B1) * g
    v = ADAM_B2 * v + (1.0 - ADAM_B2) * (g * g)
    m_hat = m / (1.0 - ADAM_B1 ** ADAM_STEP)
    v_hat = v / (1.0 - ADAM_B2 ** ADAM_STEP)
    delta = -ADAM_LR * (m_hat / (jnp.sqrt(v_hat) + ADAM_EPS) + ADAM_WD * w)
    return delta, m, v


def _adamw(name, chip, own, landed, w, m, v):
    _, rows, cols = own.shape
    tr = _tile(rows, 256) if rows % 256 == 0 else _tile(rows, 176)

    def body(chip_ref, own_ref, l_ref, w_ref, m_ref, v_ref, g_out, d_out, m_out, v_out):
        g = own_ref[...].astype(F32)
        for s in range(3):
            g = g + l_ref[s].astype(F32)
        delta, m_new, v_new = _adamw_math(w_ref[...], g, m_ref[...], v_ref[...])
        g_out[...] = g
        d_out[...] = delta
        m_out[...] = m_new
        v_out[...] = v_new

    spec = pl.BlockSpec((tr, cols), lambda r, c: (r, 0))
    return pl.pallas_call(
        body, name=name, out_shape=[jax.ShapeDtypeStruct((rows, cols), F32)] * 4,
        grid_spec=pltpu.PrefetchScalarGridSpec(
            num_scalar_prefetch=1, grid=(rows // tr,),
            in_specs=[pl.BlockSpec((None, tr, cols), lambda r, c: (c[0], r, 0)),
                      pl.BlockSpec((3, tr, cols), lambda r, c: (0, r, 0)), spec, spec, spec],
            out_specs=[spec] * 4),
        compiler_params=_params([((4, tr, cols), own.dtype)] + [((tr, cols), F32)] * 7, sem=("arbitrary",)),
    )(chip, own, landed, w, m, v)


def _rms_forward(x, gain):
    t = x.shape[0]
    tm = _tile(t, 512)

    def body(x_ref, g_ref, h_ref):
        _, xh = _rms_stats(x_ref[...])
        h_ref[...] = (xh * g_ref[...]).astype(BF16)

    return pl.pallas_call(
        body, name="rms_mix_fwd", out_shape=jax.ShapeDtypeStruct((t, D_MODEL), BF16), grid=(t // tm,),
        in_specs=[pl.BlockSpec((tm, D_MODEL), lambda m: (m, 0)), pl.BlockSpec((1, D_MODEL), lambda m: (0, 0))],
        out_specs=pl.BlockSpec((tm, D_MODEL), lambda m: (m, 0)),
        compiler_params=_params([((tm, D_MODEL), F32), ((tm, D_MODEL), BF16)], temps=8 << 20, sem=("arbitrary",)),
    )(x, gain)


def _proj_forward(h, w_in_g):
    t = h.shape[0]
    tm = _tile(t, 1024)

    def body(h_ref, w_ref, o_ref):
        o_ref[...] = _dot(h_ref[...], w_ref[...])

    return pl.pallas_call(
        body, name="proj_fwd", out_shape=jax.ShapeDtypeStruct((N_DEV, t, D_MODEL), F32), grid=(N_DEV, t // tm),
        in_specs=[pl.BlockSpec((tm, D_MODEL), lambda p, m: (m, 0)),
                  pl.BlockSpec((None, D_MODEL, D_MODEL), lambda p, m: (p, 0, 0))],
        out_specs=pl.BlockSpec((None, tm, D_MODEL), lambda p, m: (p, m, 0)),
        compiler_params=_params([((tm, D_MODEL), BF16), ((D_MODEL, D_MODEL), BF16), ((tm, D_MODEL), F32)],
                                sem=("arbitrary", "arbitrary")),
    )(h, w_in_g)


def _masked_ws(ws_ref, g):
    row = lax.broadcasted_iota(jnp.int32, (GMLP_CHUNK, GMLP_CHUNK), 0)
    col = lax.broadcasted_iota(jnp.int32, (GMLP_CHUNK, GMLP_CHUNK), 1)
    return jnp.where(row >= col, ws_ref[g], 0.0).astype(BF16)


def _gmlp_forward(proj, ln_g, ln_b, w_s, bias_b):
    t = proj.shape[1]
    tm = _tile(t, 256)
    chunks = tm // GMLP_CHUNK

    def body(u_ref, v_ref, lng_ref, lnb_ref, ws_ref, bias_ref, a_ref, vn_scr):
        vv = _gelu(v_ref[...])
        mu = jnp.mean(vv, axis=-1, keepdims=True)
        cen = vv - mu
        var = jnp.mean(cen * cen, axis=-1, keepdims=True)
        vn_scr[...] = ((cen * lax.rsqrt(var + NORM_EPS)) * lng_ref[...] + lnb_ref[...]).astype(BF16)
        for g in range(GROUPS):
            wm = _masked_ws(ws_ref, g)
            cols = slice(g * HEAD_DIM, (g + 1) * HEAD_DIM)
            for c in range(chunks):
                rows = slice(c * GMLP_CHUNK, (c + 1) * GMLP_CHUNK)
                mixed = _dot(wm, vn_scr[rows, cols]) + bias_ref[g]
                a_ref[rows, cols] = (_gelu(u_ref[rows, cols]) * mixed).astype(BF16)

    small = pl.BlockSpec((GROUPS, GMLP_CHUNK, GMLP_CHUNK), lambda m: (0, 0, 0))
    vec = pl.BlockSpec((1, D_MODEL), lambda m: (0, 0))
    return pl.pallas_call(
        body, name="gmlp_fwd", out_shape=jax.ShapeDtypeStruct((t, D_MODEL), BF16), grid=(t // tm,),
        in_specs=[pl.BlockSpec((None, tm, D_MODEL), lambda m: (U_POS, m, 0)),
                  pl.BlockSpec((None, tm, D_MODEL), lambda m: (U_POS + 1, m, 0)), vec, vec, small, small],
        out_specs=pl.BlockSpec((tm, D_MODEL), lambda m: (m, 0)),
        scratch_shapes=[pltpu.VMEM((tm, D_MODEL), BF16)],
        compiler_params=_params([((tm, D_MODEL), F32)] * 2 + [((tm, D_MODEL), BF16)] + [((8, 128, 128), F32)] * 2,
                                scratch=[((tm, D_MODEL), BF16)], temps=8 << 20, sem=("arbitrary",)),
    )(proj, proj, ln_g, ln_b, w_s, bias_b)


def _lower_bound(tab_ref):
    t0, t1 = tab_ref[0:1, :], tab_ref[1:2, :]
    mx = jnp.maximum(t0, t1)
    e0, e1 = jnp.exp(t0 - mx), jnp.exp(t1 - mx)
    return e0 / (e0 + e1)


def _tri_masks():
    row = lax.broadcasted_iota(jnp.int32, (HGRN_CHUNK, HGRN_CHUNK), 0)
    col = lax.broadcasted_iota(jnp.int32, (HGRN_CHUNK, HGRN_CHUNK), 1)
    return row >= col, row <= col


def _chunk_rows(c):
    return slice(c * HGRN_CHUNK, (c + 1) * HGRN_CHUNK)


def _per_chunk(x, nc, fn):
    return jnp.concatenate([fn(x[_chunk_rows(c)]) for c in range(nc)], axis=0)


def _chunk_row_bcast(x, nc, i):
    return _per_chunk(x, nc, lambda xc: jnp.broadcast_to(xc[i:i + 1, :], (HGRN_CHUNK, HEAD_DIM)))


def _hgrn_gates(q, fl, lb, nc):
    lower, _ = _tri_masks()
    lower = lower.astype(BF16)
    s = _sigmoid(fl)
    f = lb + (1.0 - lb) * s
    k = 1.0 - f
    hi, mid, lo = _split3(jnp.log(f))
    a = jnp.concatenate([_dot(lower, hi[_chunk_rows(c)]) + _dot(lower, mid[_chunk_rows(c)]) + _dot(lower, lo[_chunk_rows(c)])
                         for c in range(nc)], axis=0)
    a_mid = _chunk_row_bcast(a, nc, HGRN_CHUNK // 2 - 1)
    a_last = _chunk_row_bcast(a, nc, HGRN_CHUNK - 1)
    qs = q * HGRN_SCALE
    e_in, e_out, e_end, e_all = jnp.exp(a - a_mid), jnp.exp(a_mid - a), jnp.exp(a_last - a), jnp.exp(a)
    decay = [jnp.exp(a[c * HGRN_CHUNK + HGRN_CHUNK - 1:(c + 1) * HGRN_CHUNK, :]) for c in range(nc)]
    return dict(s=s, f=f, k=k, decay=decay, e_in=e_in, e_out=e_out, e_end=e_end, e_all=e_all,
                qi=qs * e_in, ki=k * e_out, kd=k * e_end, qe=qs * e_all)


def _hgrn_forward(proj, lb_table, norm_g):
    t = proj.shape[1]
    tb = _tile(t, 512)
    nc = tb // HGRN_CHUNK
    n_chunks = t // HGRN_CHUNK

    def body(q_ref, f_ref, i_ref, g_ref, tab_ref, ng_ref, og_ref, o_ref, st_ref, state):
        @pl.when(pl.program_id(1) == 0)
        def _():
            state[...] = jnp.zeros_like(state)

        lower, _ = _tri_masks()
        gt = _hgrn_gates(q_ref[...], f_ref[...], _lower_bound(tab_ref), nc)
        qi, ki, kd, qe = (gt[n].astype(BF16) for n in ("qi", "ki", "kd", "qe"))
        vb = i_ref[...].astype(BF16)
        o_intra, d_state = [], []
        for c in range(nc):
            rows = _chunk_rows(c)
            p = jnp.where(lower, _dot_nt(qi[rows], ki[rows]), 0.0).astype(BF16)
            o_intra.append(_dot(p, vb[rows]))
            d_state.append(_dot_tn(vb[rows], kd[rows]))
        st = state[...]
        outs = []
        for c in range(nc):
            st_ref[c] = st
            outs.append(o_intra[c] + _dot_nt(qe[_chunk_rows(c)], st.astype(BF16)))
            st = st * gt["decay"][c] + d_state[c]
        state[...] = st
        o = jnp.concatenate(outs, axis=0)
        o_ref[...] = o
        _, oh = _rms_stats(o)
        gz = g_ref[...]
        og_ref[...] = ((oh * ng_ref[...]) * (gz * _sigmoid(gz))).astype(BF16)

    def blk(p):
        return pl.BlockSpec((None, tb, HEAD_DIM), lambda h, n: (p, n, h))

    out_blk = pl.BlockSpec((tb, HEAD_DIM), lambda h, n: (n, h))
    return pl.pallas_call(
        body, name="hgrn_fwd",
        out_shape=[jax.ShapeDtypeStruct((t, D_MODEL), BF16), jax.ShapeDtypeStruct((t, D_MODEL), F32),
                   jax.ShapeDtypeStruct((HEADS, n_chunks, HEAD_DIM, HEAD_DIM), F32)],
        grid=(HEADS, t // tb),
        in_specs=[blk(Q_POS), blk(Q_POS + 1), blk(Q_POS + 2), blk(Q_POS + 3),
                  pl.BlockSpec((2, HEAD_DIM), lambda h, n: (0, h)), pl.BlockSpec((1, HEAD_DIM), lambda h, n: (0, h))],
        out_specs=[out_blk, out_blk, pl.BlockSpec((None, nc, HEAD_DIM, HEAD_DIM), lambda h, n: (h, n, 0, 0))],
        scratch_shapes=[pltpu.VMEM((HEAD_DIM, HEAD_DIM), F32)],
        compiler_params=_params([((tb, HEAD_DIM), F32)] * 6 + [((nc, HEAD_DIM, HEAD_DIM), F32)], temps=8 << 20,
                                sem=("arbitrary", "arbitrary")),
    )(proj, proj, proj, proj, lb_table, norm_g)


def _branch_out_forward(a, og, proj, x, w_a, w_b, w_out, ffn_g):
    t = x.shape[0]
    tm = _tile(t, 256)

    def body(a_ref, og_ref, ga_ref, gb_ref, x_ref, wa_ref, wb_ref, wo_ref, g_ref, ya_ref, yb_ref, mg_ref, x1_ref, h2_ref):
        ya = _dot(a_ref[...], wa_ref[...])
        yb = _dot(og_ref[...], wb_ref[...])
        ya_ref[...] = ya
        yb_ref[...] = yb
        merged = (_sigmoid(ga_ref[...]) * ya + _sigmoid(gb_ref[...]) * yb).astype(BF16)
        mg_ref[...] = merged
        x1 = x_ref[...] + _dot(merged, wo_ref[...])
        x1_ref[...] = x1
        _, xh = _rms_stats(x1)
        h2_ref[...] = (xh * g_ref[...]).astype(BF16)

    tok = pl.BlockSpec((tm, D_MODEL), lambda m: (m, 0))
    wsp = pl.BlockSpec((D_MODEL, D_MODEL), lambda m: (0, 0))
    return pl.pallas_call(
        body, name="branch_out_fwd",
        out_shape=[jax.ShapeDtypeStruct((t, D_MODEL), F32), jax.ShapeDtypeStruct((t, D_MODEL), F32),
                   jax.ShapeDtypeStruct((t, D_MODEL), BF16), jax.ShapeDtypeStruct((t, D_MODEL), F32),
                   jax.ShapeDtypeStruct((t, D_MODEL), BF16)],
        grid=(t // tm,),
        in_specs=[tok, tok, pl.BlockSpec((None, tm, D_MODEL), lambda m: (GATE_POS, m, 0)),
                  pl.BlockSpec((None, tm, D_MODEL), lambda m: (GATE_POS + 1, m, 0)), tok, wsp, wsp, wsp,
                  pl.BlockSpec((1, D_MODEL), lambda m: (0, 0))],
        out_specs=[tok] * 5,
        compiler_params=_params([((tm, D_MODEL), BF16)] * 4 + [((tm, D_MODEL), F32)] * 6 + [((D_MODEL, D_MODEL), BF16)] * 3,
                                temps=8 << 20, sem=("arbitrary",)),
    )(a, og, proj, proj, x, w_a, w_b, w_out, ffn_g)


def _ffn_forward(h2, x1, w_gu, w_down):
    t = x1.shape[0]
    tm = _tile(t, 512)

    def body(h_ref, wg_ref, wu_ref, wd_ref, x1_ref, gu_ref, act_ref, x2_ref, acc):
        j = pl.program_id(1)
        h = h_ref[...]
        gate = _dot(h, wg_ref[...])
        up = _dot(h, wu_ref[...])
        gu_ref[0] = gate
        gu_ref[1] = up
        act = ((gate * _sigmoid(gate)) * up).astype(BF16)
        act_ref[...] = act
        part = _dot(act, wd_ref[...])

        @pl.when(j == 0)
        def _():
            acc[...] = part

        @pl.when(j > 0)
        def _():
            acc[...] += part

        @pl.when(j == 3)
        def _():
            x2_ref[...] = x1_ref[...] + acc[...]

    tok = pl.BlockSpec((tm, D_MODEL), lambda m, j: (m, 0))
    return pl.pallas_call(
        body, name="ffn_fwd",
        out_shape=[jax.ShapeDtypeStruct((4, 2, t, FF_BLOCK), F32), jax.ShapeDtypeStruct((4, t, FF_BLOCK), BF16),
                   jax.ShapeDtypeStruct((t, D_MODEL), F32)],
        grid=(t // tm, 4),
        in_specs=[tok, pl.BlockSpec((None, D_MODEL, FF_BLOCK), lambda m, j: (j, 0, 0)),
                  pl.BlockSpec((None, D_MODEL, FF_BLOCK), lambda m, j: (j + 4, 0, 0)),
                  pl.BlockSpec((FF_BLOCK, D_MODEL), lambda m, j: (j, 0)), tok],
        out_specs=[pl.BlockSpec((None, 2, tm, FF_BLOCK), lambda m, j: (j, 0, m, 0)),
                   pl.BlockSpec((None, tm, FF_BLOCK), lambda m, j: (j, m, 0)), tok],
        scratch_shapes=[pltpu.VMEM((tm, D_MODEL), F32)],
        compiler_params=_params([((tm, D_MODEL), BF16), ((D_MODEL, 768), BF16), ((D_MODEL, 768), BF16),
                                 ((FF_BLOCK, D_MODEL), BF16), ((tm, D_MODEL), F32), ((2, tm, 768), F32),
                                 ((tm, 768), BF16), ((tm, D_MODEL), F32)],
                                scratch=[((tm, D_MODEL), F32)], temps=8 << 20, sem=("arbitrary", "arbitrary")),
    )(h2, w_gu, w_gu, w_down, x1)


def _loss_and_final_backward(x2, target, final_g):
    t = x2.shape[0]
    tm = _tile(t, 256)

    def body(x_ref, t_ref, g_ref, loss_ref, dg_ref, dx_ref, dxb_ref):
        @pl.when(pl.program_id(0) == 0)
        def _():
            loss_ref[...] = jnp.zeros_like(loss_ref)
            dg_ref[...] = jnp.zeros_like(dg_ref)

        x = x_ref[...]
        g = g_ref[...]
        r, xh = _rms_stats(x)
        err = xh * g - t_ref[...]
        loss_ref[...] += 0.5 * jnp.sum(jnp.mean(err * err, axis=-1, keepdims=True), axis=0, keepdims=True)
        dy = err * (1.0 / D_MODEL)
        dg_ref[...] += jnp.sum(dy * xh, axis=0, keepdims=True)
        dxh = dy * g
        dx = r * (dxh - xh * jnp.mean(dxh * xh, axis=-1, keepdims=True))
        dx_ref[...] = dx
        dxb_ref[...] = dx.astype(BF16)

    tok = pl.BlockSpec((tm, D_MODEL), lambda m: (m, 0))
    vec = pl.BlockSpec((1, D_MODEL), lambda m: (0, 0))
    return pl.pallas_call(
        body, name="loss_final_bwd",
        out_shape=[jax.ShapeDtypeStruct((8, 128), F32), jax.ShapeDtypeStruct((1, D_MODEL), F32),
                   jax.ShapeDtypeStruct((t, D_MODEL), F32), jax.ShapeDtypeStruct((t, D_MODEL), BF16)],
        grid=(t // tm,), in_specs=[tok, tok, vec],
        out_specs=[pl.BlockSpec((8, 128), lambda m: (0, 0)), vec, tok, tok],
        compiler_params=_params([((tm, D_MODEL), F32)] * 4, temps=8 << 20, sem=("arbitrary",)),
    )(x2, target, final_g)


def _ffn_backward(dx2b, dx2, gu, x1, w_gu, w_down, ffn_g):
    t = x1.shape[0]
    tm = _tile(t, 512)

    def body(dxb_ref, dx2_ref, gu_ref, x1_ref, wg_ref, wu_ref, wd_ref, g_ref, dgu_ref, dx1_ref, dx1b_ref, dg_ref, acc):
        m, j = pl.program_id(0), pl.program_id(1)

        @pl.when((m == 0) & (j == 0))
        def _():
            dg_ref[...] = jnp.zeros_like(dg_ref)

        dact = _dot_nt(dxb_ref[...], wd_ref[...])
        gate, up = gu_ref[0], gu_ref[1]
        sg = _sigmoid(gate)
        dgate = (dact * up * (sg * (1.0 + gate * (1.0 - sg)))).astype(BF16)
        dup = (dact * (gate * sg)).astype(BF16)
        dgu_ref[0] = dgate
        dgu_ref[1] = dup
        part = _dot_nt(dgate, wg_ref[...]) + _dot_nt(dup, wu_ref[...])

        @pl.when(j == 0)
        def _():
            acc[...] = part

        @pl.when(j > 0)
        def _():
            acc[...] += part

        @pl.when(j == 3)
        def _():
            dx, dg = _rms_bwd(acc[...], x1_ref[...], g_ref[...])
            dx1 = dx2_ref[...] + dx
            dx1_ref[...] = dx1
            dx1b_ref[...] = dx1.astype(BF16)
            dg_ref[...] += dg

    tok = pl.BlockSpec((tm, D_MODEL), lambda m, j: (m, 0))
    vec = pl.BlockSpec((1, D_MODEL), lambda m, j: (0, 0))
    gu_spec = pl.BlockSpec((None, 2, tm, FF_BLOCK), lambda m, j: (j, 0, m, 0))
    return pl.pallas_call(
        body, name="ffn_bwd",
        out_shape=[jax.ShapeDtypeStruct((4, 2, t, FF_BLOCK), BF16), jax.ShapeDtypeStruct((t, D_MODEL), F32),
                   jax.ShapeDtypeStruct((t, D_MODEL), BF16), jax.ShapeDtypeStruct((1, D_MODEL), F32)],
        grid=(t // tm, 4),
        in_specs=[tok, tok, gu_spec, tok, pl.BlockSpec((None, D_MODEL, FF_BLOCK), lambda m, j: (j, 0, 0)),
                  pl.BlockSpec((None, D_MODEL, FF_BLOCK), lambda m, j: (j + 4, 0, 0)),
                  pl.BlockSpec((FF_BLOCK, D_MODEL), lambda m, j: (j, 0)), vec],
        out_specs=[gu_spec, tok, tok, vec],
        scratch_shapes=[pltpu.VMEM((tm, D_MODEL), F32)],
        compiler_params=_params([((tm, D_MODEL), BF16), ((tm, D_MODEL), F32), ((2, tm, 768), F32), ((tm, D_MODEL), F32),
                                 ((D_MODEL, 768), BF16), ((D_MODEL, 768), BF16), ((FF_BLOCK, D_MODEL), BF16),
                                 ((2, tm, 768), BF16), ((tm, D_MODEL), F32), ((tm, D_MODEL), BF16)],
                                scratch=[((tm, D_MODEL), F32)], temps=8 << 20, sem=("arbitrary", "arbitrary")),
    )(dx2b, dx2, gu, x1, w_gu, w_gu, w_down, ffn_g)


def _branch_out_backward(dx1b, ya, yb, proj, w_a, w_b, w_out):
    t = ya.shape[0]
    tm = _tile(t, 256)

    def body(dx_ref, ya_ref, yb_ref, ga_ref, gb_ref, wa_ref, wb_ref, wo_ref, dya_ref, dyb_ref, dgate_ref, da_ref, dog_ref):
        dm = _dot_nt(dx_ref[...], wo_ref[...])
        sa, sb = _sigmoid(ga_ref[...]), _sigmoid(gb_ref[...])
        dya = (dm * sa).astype(BF16)
        dyb = (dm * sb).astype(BF16)
        dya_ref[...] = dya
        dyb_ref[...] = dyb
        dgate_ref[0] = (dm * ya_ref[...] * (sa * (1.0 - sa))).astype(BF16)
        dgate_ref[1] = (dm * yb_ref[...] * (sb * (1.0 - sb))).astype(BF16)
        da_ref[...] = _dot_nt(dya, wa_ref[...])
        dog_ref[...] = _dot_nt(dyb, wb_ref[...])

    tok = pl.BlockSpec((tm, D_MODEL), lambda m: (m, 0))
    wsp = pl.BlockSpec((D_MODEL, D_MODEL), lambda m: (0, 0))
    return pl.pallas_call(
        body, name="branch_out_bwd",
        out_shape=[jax.ShapeDtypeStruct((t, D_MODEL), BF16), jax.ShapeDtypeStruct((t, D_MODEL), BF16),
                   jax.ShapeDtypeStruct((2, t, D_MODEL), BF16), jax.ShapeDtypeStruct((t, D_MODEL), F32),
                   jax.ShapeDtypeStruct((t, D_MODEL), F32)],
        grid=(t // tm,),
        in_specs=[tok, tok, tok, pl.BlockSpec((None, tm, D_MODEL), lambda m: (GATE_POS, m, 0)),
                  pl.BlockSpec((None, tm, D_MODEL), lambda m: (GATE_POS + 1, m, 0)), wsp, wsp, wsp],
        out_specs=[tok, tok, pl.BlockSpec((2, tm, D_MODEL), lambda m: (0, m, 0)), tok, tok],
        compiler_params=_params([((tm, D_MODEL), BF16)] * 5 + [((tm, D_MODEL), F32)] * 6 + [((D_MODEL, D_MODEL), BF16)] * 3,
                                temps=8 << 20, sem=("arbitrary",)),
    )(dx1b, ya, yb, proj, proj, w_a, w_b, w_out)


def _hgrn_backward(dog, o_saved, states, proj, lb_table, norm_g):
    t = proj.shape[1]
    tb = _tile(t, 256)
    nc = tb // HGRN_CHUNK
    nb = t // tb

    def body(dog_ref, o_ref, st_ref, q_ref, f_ref, i_ref, g_ref, tab_ref, ng_ref, dp_ref, dng_ref, dtab_ref, gstate):
        @pl.when(pl.program_id(1) == 0)
        def _():
            gstate[...] = jnp.zeros_like(gstate)
            dng_ref[...] = jnp.zeros_like(dng_ref)
            dtab_ref[...] = jnp.zeros_like(dtab_ref)

        lb = _lower_bound(tab_ref)
        ng = ng_ref[...]
        lower, upper = _tri_masks()
        gt = _hgrn_gates(q_ref[...], f_ref[...], lb, nc)
        qi, ki, kd, qe = (gt[n].astype(BF16) for n in ("qi", "ki", "kd", "qe"))
        vb = i_ref[...].astype(BF16)
        o, gz, d_og = o_ref[...], g_ref[...], dog_ref[...]
        r, oh = _rms_stats(o)
        sg = _sigmoid(gz)
        d_on = d_og * (gz * sg)
        dgz = d_og * (oh * ng) * (sg * (1.0 + gz * (1.0 - sg)))
        dng_ref[...] += jnp.sum(d_on * oh, axis=0, keepdims=True)
        doh = d_on * ng
        dob = (r * (doh - oh * jnp.mean(doh * oh, axis=-1, keepdims=True))).astype(BF16)
        dv_intra, dqi, dki, dqe, g_upd = [], [], [], [], []
        for c in range(nc):
            rows = _chunk_rows(c)
            p = jnp.where(lower, _dot_nt(qi[rows], ki[rows]), 0.0).astype(BF16)
            dv_intra.append(_dot_tn(p, dob[rows]))
            dp = jnp.where(lower, _dot_nt(dob[rows], vb[rows]), 0.0).astype(BF16)
            dqi.append(_dot(dp, ki[rows]))
            dki.append(_dot_tn(dp, qi[rows]))
            dqe.append(_dot(dob[rows], st_ref[c].astype(BF16)))
            g_upd.append(_dot_tn(dob[rows], qe[rows]))
        g_after = [None] * nc
        g = gstate[...]
        for c in reversed(range(nc)):
            g_after[c] = g
            g = g * gt["decay"][c] + g_upd[c]
        gstate[...] = g
        dkd, dv, da_last = [], [], []
        for c in range(nc):
            rows = _chunk_rows(c)
            gb = g_after[c].astype(BF16)
            dkd.append(_dot(vb[rows], gb))
            dv.append(dv_intra[c] + _dot_nt(kd[rows], gb))
            da_last.append(jnp.sum(g_after[c] * st_ref[c], axis=0, keepdims=True) * gt["decay"][c])
        dqi, dki, dqe, dkd, dv = (jnp.concatenate(z, axis=0) for z in (dqi, dki, dqe, dkd, dv))
        dqs = dqi * gt["e_in"] + dqe * gt["e_all"]
        dk = dki * gt["e_out"] + dkd * gt["e_end"]
        t_in, t_out, t_end = dqi * gt["qi"], dki * gt["ki"], dkd * gt["kd"]
        da = t_in - t_out + dqe * gt["qe"] - t_end
        row = lax.broadcasted_iota(jnp.int32, (HGRN_CHUNK, HEAD_DIM), 0)
        d_mid = t_out - t_in
        pieces = []
        for c in range(nc):
            rows = _chunk_rows(c)
            da_mid = jnp.sum(d_mid[rows], axis=0, keepdims=True)
            da_end = jnp.sum(t_end[rows], axis=0, keepdims=True) + da_last[c]
            da_c = da[rows] + jnp.where(row == HGRN_CHUNK // 2 - 1, da_mid, 0.0) + jnp.where(row == HGRN_CHUNK - 1, da_end, 0.0)
            pieces.append(_mask_mm(upper.astype(BF16), da_c))
        df = jnp.concatenate(pieces, axis=0) / gt["f"] - dk
        s = gt["s"]
        dlb = jnp.sum(df * (1.0 - s), axis=0, keepdims=True)
        dp_ref[0] = (dqs * HGRN_SCALE).astype(BF16)
        dp_ref[1] = (df * (1.0 - lb) * (s * (1.0 - s))).astype(BF16)
        dp_ref[2] = dv.astype(BF16)
        dp_ref[3] = dgz.astype(BF16)
        dt0 = dlb * (lb * (1.0 - lb))
        dtab_ref[0:1, :] += dt0
        dtab_ref[1:2, :] -= dt0

    def blk(p):
        return pl.BlockSpec((None, tb, HEAD_DIM), lambda h, n: (p, nb - 1 - n, h))

    tok = pl.BlockSpec((tb, HEAD_DIM), lambda h, n: (nb - 1 - n, h))
    return pl.pallas_call(
        body, name="hgrn_bwd",
        out_shape=[jax.ShapeDtypeStruct((N_DEV, t, D_MODEL), BF16), jax.ShapeDtypeStruct((1, D_MODEL), F32),
                   jax.ShapeDtypeStruct((2, D_MODEL), F32)],
        grid=(HEADS, nb),
        in_specs=[tok, tok, pl.BlockSpec((None, nc, HEAD_DIM, HEAD_DIM), lambda h, n: (h, nb - 1 - n, 0, 0)),
                  blk(Q_POS), blk(Q_POS + 1), blk(Q_POS + 2), blk(Q_POS + 3),
                  pl.BlockSpec((2, HEAD_DIM), lambda h, n: (0, h)), pl.BlockSpec((1, HEAD_DIM), lambda h, n: (0, h))],
        out_specs=[pl.BlockSpec((4, tb, HEAD_DIM), lambda h, n: (0, nb - 1 - n, h)),
                   pl.BlockSpec((1, HEAD_DIM), lambda h, n: (0, h)), pl.BlockSpec((2, HEAD_DIM), lambda h, n: (0, h))],
        scratch_shapes=[pltpu.VMEM((HEAD_DIM, HEAD_DIM), F32)],
        compiler_params=_params([((tb, HEAD_DIM), F32)] * 6 + [((nc, HEAD_DIM, HEAD_DIM), F32)] + [((4, tb, HEAD_DIM), BF16)],
                                temps=8 << 20, sem=("arbitrary", "arbitrary")),
    )(dog, o_saved, states, proj, proj, proj, proj, lb_table, norm_g)


def _gmlp_backward(dproj, da, proj, ln_g, ln_b, w_s, bias_b):
    t = proj.shape[1]
    tm = _tile(t, 256)
    chunks = tm // GMLP_CHUNK

    def body(_, da_ref, u_ref, v_ref, lng_ref, lnb_ref, ws_ref, bias_ref, dp_ref, dlng_ref, dlnb_ref, dws_ref, dbs_ref,
             vn_scr, dvn_scr):
        @pl.when(pl.program_id(0) == 0)
        def _():
            dlng_ref[...] = jnp.zeros_like(dlng_ref)
            dlnb_ref[...] = jnp.zeros_like(dlnb_ref)
            dws_ref[...] = jnp.zeros_like(dws_ref)
            dbs_ref[...] = jnp.zeros_like(dbs_ref)

        v = v_ref[...]
        vv = _gelu(v)
        mu = jnp.mean(vv, axis=-1, keepdims=True)
        cen = vv - mu
        rstd = lax.rsqrt(jnp.mean(cen * cen, axis=-1, keepdims=True) + NORM_EPS)
        vhat = cen * rstd
        lng = lng_ref[...]
        vn_scr[...] = (vhat * lng + lnb_ref[...]).astype(BF16)
        row = lax.broadcasted_iota(jnp.int32, (GMLP_CHUNK, GMLP_CHUNK), 0)
        col = lax.broadcasted_iota(jnp.int32, (GMLP_CHUNK, GMLP_CHUNK), 1)
        for g in range(GROUPS):
            wm = _masked_ws(ws_ref, g)
            cols = slice(g * HEAD_DIM, (g + 1) * HEAD_DIM)
            dws = jnp.zeros((GMLP_CHUNK, GMLP_CHUNK), F32)
            dbs = jnp.zeros((GMLP_CHUNK, GMLP_CHUNK), F32)
            for c in range(chunks):
                rows = slice(c * GMLP_CHUNK, (c + 1) * GMLP_CHUNK)
                vn = vn_scr[rows, cols]
                mixed = _dot(wm, vn) + bias_ref[g]
                u = u_ref[rows, cols]
                d_a = da_ref[rows, cols]
                dp_ref[0, rows, cols] = (d_a * mixed * _gelu_grad(u)).astype(BF16)
                dmix = d_a * _gelu(u)
                dmb = dmix.astype(BF16)
                dbs = dbs + dmix
                dws = dws + _dot_nt(dmb, vn)
                dvn_scr[rows, cols] = _dot_tn(wm, dmb)
            dws_ref[g] += jnp.where(row >= col, dws, 0.0)
            dbs_ref[g] += jnp.broadcast_to(jnp.sum(dbs, axis=-1, keepdims=True), (GMLP_CHUNK, GMLP_CHUNK))
        dvn = dvn_scr[...]
        dlng_ref[...] += jnp.sum(dvn * vhat, axis=0, keepdims=True)
        dlnb_ref[...] += jnp.sum(dvn, axis=0, keepdims=True)
        dvh = dvn * lng
        dvv = rstd * (dvh - jnp.mean(dvh, axis=-1, keepdims=True) - vhat * jnp.mean(dvh * vhat, axis=-1, keepdims=True))
        dp_ref[1] = (dvv * _gelu_grad(v)).astype(BF16)

    tok = pl.BlockSpec((tm, D_MODEL), lambda m: (m, 0))
    small = pl.BlockSpec((GROUPS, GMLP_CHUNK, GMLP_CHUNK), lambda m: (0, 0, 0))
    vec = pl.BlockSpec((1, D_MODEL), lambda m: (0, 0))
    return pl.pallas_call(
        body, name="gmlp_bwd",
        out_shape=[jax.ShapeDtypeStruct(dproj.shape, BF16), jax.ShapeDtypeStruct((1, D_MODEL), F32),
                   jax.ShapeDtypeStruct((1, D_MODEL), F32), jax.ShapeDtypeStruct((GROUPS, GMLP_CHUNK, GMLP_CHUNK), F32),
                   jax.ShapeDtypeStruct((GROUPS, GMLP_CHUNK, GMLP_CHUNK), F32)],
        grid=(t // tm,),
        in_specs=[ANY, tok, pl.BlockSpec((None, tm, D_MODEL), lambda m: (U_POS, m, 0)),
                  pl.BlockSpec((None, tm, D_MODEL), lambda m: (U_POS + 1, m, 0)), vec, vec, small, small],
        out_specs=[pl.BlockSpec((2, tm, D_MODEL), lambda m: (U_POS // 2, m, 0)), vec, vec, small, small],
        scratch_shapes=[pltpu.VMEM((tm, D_MODEL), BF16), pltpu.VMEM((tm, D_MODEL), F32)],
        input_output_aliases={0: 0},
        compiler_params=_params([((tm, D_MODEL), F32)] * 3 + [((2, tm, D_MODEL), BF16)] + [((8, 128, 128), F32)] * 4,
                                scratch=[((tm, D_MODEL), BF16), ((tm, D_MODEL), F32)], temps=12 << 20, sem=("arbitrary",)),
    )(dproj, da, proj, proj, ln_g, ln_b, w_s, bias_b)


def _place_gate_grads(dproj, dgates):
    t = dgates.shape[1]
    tm = _tile(t, 512)

    def body(_, src_ref, dst_ref):
        dst_ref[...] = src_ref[...]

    return pl.pallas_call(
        body, name="place_gate_grads", out_shape=jax.ShapeDtypeStruct(dproj.shape, BF16), grid=(t // tm,),
        in_specs=[ANY, pl.BlockSpec((2, tm, D_MODEL), lambda m: (0, m, 0))],
        out_specs=pl.BlockSpec((2, tm, D_MODEL), lambda m: (GATE_POS // 2, m, 0)),
        input_output_aliases={0: 0},
        compiler_params=_params([((2, tm, D_MODEL), BF16)] * 2, sem=("arbitrary",)),
    )(dproj, dgates)


def _input_backward(dproj, w_in_g, x, dx1, mix_g):
    t = x.shape[0]
    tm = _tile(t, 512)

    def body(dp_ref, w_ref, x_ref, dx1_ref, g_ref, dx_ref, dg_ref, acc):
        m, p = pl.program_id(0), pl.program_id(1)

        @pl.when((m == 0) & (p == 0))
        def _():
            dg_ref[...] = jnp.zeros_like(dg_ref)

        part = _dot_nt(dp_ref[...], w_ref[...])

        @pl.when(p == 0)
        def _():
            acc[...] = part

        @pl.when(p > 0)
        def _():
            acc[...] += part

        @pl.when(p == N_DEV - 1)
        def _():
            dx, dg = _rms_bwd(acc[...], x_ref[...], g_ref[...])
            dx_ref[...] = dx1_ref[...] + dx
            dg_ref[...] += dg

    tok = pl.BlockSpec((tm, D_MODEL), lambda m, p: (m, 0))
    vec = pl.BlockSpec((1, D_MODEL), lambda m, p: (0, 0))
    return pl.pallas_call(
        body, name="input_bwd",
        out_shape=[jax.ShapeDtypeStruct((t, D_MODEL), F32), jax.ShapeDtypeStruct((1, D_MODEL), F32)],
        grid=(t // tm, N_DEV),
        in_specs=[pl.BlockSpec((None, tm, D_MODEL), lambda m, p: (p, m, 0)),
                  pl.BlockSpec((None, D_MODEL, D_MODEL), lambda m, p: (p, 0, 0)), tok, tok, vec],
        out_specs=[tok, vec],
        scratch_shapes=[pltpu.VMEM((tm, D_MODEL), F32)],
        compiler_params=_params([((tm, D_MODEL), BF16), ((D_MODEL, D_MODEL), BF16)] + [((tm, D_MODEL), F32)] * 3,
                                scratch=[((tm, D_MODEL), F32)], temps=8 << 20, sem=("arbitrary", "arbitrary")),
    )(dproj, w_in_g, x, dx1, mix_g)


def _weight_grad(name, a, b, a_spec, b_spec, out_shape, out_spec, grid, blocks):
    def body(a_ref, b_ref, o_ref):
        part = _dot_tn(a_ref[...], b_ref[...])
        m = pl.program_id(len(grid) - 1)

        @pl.when(m == 0)
        def _():
            o_ref[...] = part

        @pl.when(m > 0)
        def _():
            o_ref[...] += part

    return pl.pallas_call(
        body, name=name, out_shape=jax.ShapeDtypeStruct(out_shape, F32), grid=grid, in_specs=[a_spec, b_spec],
        out_specs=out_spec, compiler_params=_params(blocks, temps=8 << 20, sem=("arbitrary",) * len(grid)),
    )(a, b)


def _pack_small(mix_g, ln_g, ln_b, w_s, b_s, lb_table, hg_norm, ffn_g, final_g):
    def part(a):
        a = a.reshape(-1, D_MODEL)
        return jnp.pad(a, ((0, 8 - a.shape[0]), (0, 0)))

    return jnp.concatenate([part(mix_g), part(ln_g), part(ln_b), part(hg_norm), part(ffn_g), part(final_g),
                            part(lb_table), part(b_s), w_s.reshape(GMLP_CHUNK, D_MODEL)], axis=0)


def _unpack_small(pack):
    return dict(norm_mix_g=pack[0:1], gmlp_ln_g=pack[8:9], gmlp_ln_b=pack[16:17], hgrn_norm_g=pack[24:25],
                norm_ffn_g=pack[32:33], norm_final_g=pack[40], hgrn_lb_table=pack[48:50],
                gmlp_b_s=pack[56:57].reshape(1, GROUPS, GMLP_CHUNK),
                gmlp_w_s=pack[64:192].reshape(1, GROUPS, GMLP_CHUNK, GMLP_CHUNK))


def _adamw_small(gathered, w, m, v):
    rows = w.shape[0]

    def body(p_ref, w_ref, m_ref, v_ref, g_out, d_out, m_out, v_out):
        g = p_ref[0]
        for j in range(1, N_DEV):
            g = g + p_ref[j]
        delta, m_new, v_new = _adamw_math(w_ref[...], g, m_ref[...], v_ref[...])
        g_out[...] = g
        d_out[...] = delta
        m_out[...] = m_new
        v_out[...] = v_new

    tr = 64
    spec = pl.BlockSpec((tr, D_MODEL), lambda r: (r, 0))
    return pl.pallas_call(
        body, name="adamw_small", out_shape=[jax.ShapeDtypeStruct((rows, D_MODEL), F32)] * 4, grid=(rows // tr,),
        in_specs=[pl.BlockSpec((N_DEV, tr, D_MODEL), lambda r: (0, r, 0)), spec, spec, spec], out_specs=[spec] * 4,
        compiler_params=_params([((N_DEV, tr, D_MODEL), F32)] + [((tr, D_MODEL), F32)] * 7, sem=("arbitrary",)),
    )(gathered, w, m, v)


def kernel(x, norm_mix_g, w_in, gmlp_ln_g, gmlp_ln_b, gmlp_w_s, gmlp_b_s, hgrn_lb_table, hgrn_norm_g, w_branch_a, w_branch_b, w_out, norm_ffn_g, w_gate_up, w_down, norm_final_g, loss_target, m_norm_mix_g, m_w_in, m_gmlp_ln_g, m_gmlp_ln_b, m_gmlp_w_s, m_gmlp_b_s, m_hgrn_lb_table, m_hgrn_norm_g, m_w_branch_a, m_w_branch_b, m_w_out, m_norm_ffn_g, m_w_gate_up, m_w_down, m_norm_final_g, v_norm_mix_g, v_w_in, v_gmlp_ln_g, v_gmlp_ln_b, v_gmlp_w_s, v_gmlp_b_s, v_hgrn_lb_table, v_hgrn_norm_g, v_w_branch_a, v_w_branch_b, v_w_out, v_norm_ffn_g, v_w_gate_up, v_w_down, v_norm_final_g):
    t = x.shape[1]
    x2d = x.reshape(t, D_MODEL)
    target = loss_target.reshape(t, D_MODEL)
    final_g = norm_final_g.reshape(1, D_MODEL)

    shards = [w_in[0].astype(BF16), w_branch_a[0].astype(BF16), w_branch_b[0].astype(BF16), w_out[0].astype(BF16),
              w_gate_up[0].astype(BF16), w_down[0].astype(BF16)]

    def rows_of(n):
        return lambda ref, j: ref.at[pl.ds(pl.multiple_of(j * n, 8), n)]

    gathered = [((N_DEV, D_MODEL, D_MODEL), BF16), ((D_MODEL, D_MODEL), BF16), ((D_MODEL, D_MODEL), BF16),
                ((D_MODEL, D_MODEL), BF16), ((N_DEV, D_MODEL, FF_BLOCK), BF16), ((D_FF, D_MODEL), BF16)]
    places = [lambda ref, j: ref.at[_pos_of_dev(j)], rows_of(BRANCH_ROWS), rows_of(BRANCH_ROWS), rows_of(BRANCH_ROWS),
              lambda ref, j: ref.at[j], rows_of(DOWN_ROWS)]
    (w_in_g,) = _all_gather("w_in_all_gather", shards[:1], gathered[:1], places[:1])
    _, later = lax.optimization_barrier((w_in_g, shards[1:]))
    w_a, w_b, w_o, w_gu, w_dn = _all_gather_async("weights_all_gather", 0, later, gathered[1:], places[1:])

    h = _rms_forward(x2d, norm_mix_g)
    proj = _proj_forward(h, w_in_g)
    bias_b = jnp.broadcast_to(gmlp_b_s[0][:, :, None], (GROUPS, GMLP_CHUNK, GMLP_CHUNK))
    a = _gmlp_forward(proj, gmlp_ln_g, gmlp_ln_b, gmlp_w_s[0], bias_b)
    og, o_saved, states = _hgrn_forward(proj, hgrn_lb_table, hgrn_norm_g)
    ya, yb, merged, x1, h2 = _branch_out_forward(a, og, proj, x2d, w_a, w_b, w_o, norm_ffn_g)
    gu, act, x2 = _ffn_forward(h2, x1, w_gu, w_dn)
    loss_tile, d_final_g, dx2, dx2b = _loss_and_final_backward(x2, target, final_g)

    dgu, dx1, dx1b, d_ffn_g = _ffn_backward(dx2b, dx2, gu, x1, w_gu, w_dn, norm_ffn_g)
    dya, dyb, dgates, da, dog = _branch_out_backward(dx1b, ya, yb, proj, w_a, w_b, w_o)
    dproj, d_hg_norm, d_lb = _hgrn_backward(dog, o_saved, states, proj, hgrn_lb_table, hgrn_norm_g)
    dproj, d_ln_g, d_ln_b, d_ws, d_bs = _gmlp_backward(dproj, da, proj, gmlp_ln_g, gmlp_ln_b, gmlp_w_s[0], bias_b)
    dproj = _place_gate_grads(dproj, dgates)
    grad_x, d_mix_g = _input_backward(dproj, w_in_g, x2d, dx1, norm_mix_g)

    tm = _tile(t, 512)
    nm = t // tm
    tok_a = pl.BlockSpec((tm, D_MODEL), lambda m: (m, 0))
    full_o = pl.BlockSpec((D_MODEL, D_MODEL), lambda m: (0, 0))
    sq_blocks = [((tm, D_MODEL), BF16)] * 2 + [((D_MODEL, D_MODEL), F32)]
    g_in = _weight_grad(
        "grad_w_in", h, dproj, pl.BlockSpec((tm, D_MODEL), lambda p, m: (m, 0)),
        pl.BlockSpec((None, tm, D_MODEL), lambda p, m: (p, m, 0)), (N_DEV, D_MODEL, D_MODEL),
        pl.BlockSpec((None, D_MODEL, D_MODEL), lambda p, m: (p, 0, 0)), (N_DEV, nm), sq_blocks)
    g_a = _weight_grad("grad_w_a", a, dya, tok_a, tok_a, (D_MODEL, D_MODEL), full_o, (nm,), sq_blocks)
    g_b = _weight_grad("grad_w_b", og, dyb, tok_a, tok_a, (D_MODEL, D_MODEL), full_o, (nm,), sq_blocks)
    g_o = _weight_grad("grad_w_out", merged, dx1b, tok_a, tok_a, (D_MODEL, D_MODEL), full_o, (nm,), sq_blocks)
    g_gu = _weight_grad(
        "grad_w_gate_up", h2, dgu, pl.BlockSpec((tm, D_MODEL), lambda j, m: (m, 0)),
        pl.BlockSpec((None, None, tm, FF_BLOCK), lambda j, m: (j % 4, j // 4, m, 0)), (N_DEV, D_MODEL, FF_BLOCK),
        pl.BlockSpec((None, D_MODEL, FF_BLOCK), lambda j, m: (j, 0, 0)), (N_DEV, nm),
        [((tm, D_MODEL), BF16), ((tm, 768), BF16), ((D_MODEL, 768), F32)])
    g_dn = _weight_grad(
        "grad_w_down", act, dx2b, pl.BlockSpec((None, tm, FF_BLOCK), lambda j, m: (j, m, 0)),
        pl.BlockSpec((tm, D_MODEL), lambda j, m: (m, 0)), (D_FF, D_MODEL),
        pl.BlockSpec((FF_BLOCK, D_MODEL), lambda j, m: (j, 0)), (4, nm),
        [((tm, 768), BF16), ((tm, D_MODEL), BF16), ((FF_BLOCK, D_MODEL), F32)])

    shard_fns = [lambda ref, j: ref.at[_pos_of_dev(j)], rows_of(BRANCH_ROWS), rows_of(BRANCH_ROWS), rows_of(BRANCH_ROWS),
                 lambda ref, j: ref.at[j], rows_of(DOWN_ROWS)]
    shard_shapes = [(D_MODEL, D_MODEL), (BRANCH_ROWS, D_MODEL), (BRANCH_ROWS, D_MODEL), (BRANCH_ROWS, D_MODEL),
                    (D_MODEL, FF_BLOCK), (DOWN_ROWS, D_MODEL)]
    names = ["w_in", "w_branch_a", "w_branch_b", "w_out", "w_gate_up", "w_down"]
    grads = [g_in, g_a, g_b, g_o, g_gu, g_dn]
    land = _exchange_sibling("grads_to_sibling", grads, shard_fns, shard_shapes)
    core = lax.axis_index("c").astype(jnp.int32).reshape(1)
    chip = (2 * lax.axis_index("x") + lax.axis_index("y")).astype(jnp.int32).reshape(1)
    branch = ((BRANCH_ROWS, D_MODEL), lambda q, r, c: (2 * q + c, 0))
    own_blocks = [((None, 256, D_MODEL), lambda q, r, c: (_pos_of_dev(2 * q + c), r, 0)), branch, branch, branch,
                  ((None, 256, FF_BLOCK), lambda q, r, c: (2 * q + c, r, 0)),
                  ((DOWN_ROWS // 2, D_MODEL), lambda q, r, c: (2 * (2 * q + c) + r, 0))]
    partials = [_chip_partial("chip_partial_" + nme, core, g_, blk, idx, l_)
                for nme, g_, (blk, idx), l_ in zip(names, grads, own_blocks, land)]
    landed = _exchange_chips("grads_to_chips", partials)
    big = {}
    for nme, own, lnd, w, m, v in zip(
            names, partials, landed, [w_in, w_branch_a, w_branch_b, w_out, w_gate_up, w_down],
            [m_w_in, m_w_branch_a, m_w_branch_b, m_w_out, m_w_gate_up, m_w_down],
            [v_w_in, v_w_branch_a, v_w_branch_b, v_w_out, v_w_gate_up, v_w_down]):
        outs = _adamw("adamw_" + nme, chip, own, lnd, w[0], m[0], v[0])
        big[nme] = [o_[None] for o_ in outs]

    d_bs_row = d_bs[:, :, 0]
    small_partial = _pack_small(d_mix_g, d_ln_g, d_ln_b, d_ws, d_bs_row, d_lb, d_hg_norm, d_ffn_g, d_final_g)
    (small_all,) = _all_gather_async("small_grads_all_gather", 1, [small_partial],
                                     [((N_DEV, SMALL_ROWS, D_MODEL), F32)], [lambda ref, j: ref.at[j]])

    def packed(prefix_vals):
        return _pack_small(*prefix_vals)

    w_pack = packed([norm_mix_g, gmlp_ln_g, gmlp_ln_b, gmlp_w_s, gmlp_b_s, hgrn_lb_table, hgrn_norm_g, norm_ffn_g, norm_final_g])
    m_pack = packed([m_norm_mix_g, m_gmlp_ln_g, m_gmlp_ln_b, m_gmlp_w_s, m_gmlp_b_s, m_hgrn_lb_table, m_hgrn_norm_g, m_norm_ffn_g, m_norm_final_g])
    v_pack = packed([v_norm_mix_g, v_gmlp_ln_g, v_gmlp_ln_b, v_gmlp_w_s, v_gmlp_b_s, v_hgrn_lb_table, v_hgrn_norm_g, v_norm_ffn_g, v_norm_final_g])
    small = [_unpack_small(p) for p in _adamw_small(small_all, w_pack, m_pack, v_pack)]

    loss = lax.psum(loss_tile[0, 0], ("x", "y", "c"))
    order = ["norm_mix_g", "w_in", "gmlp_ln_g", "gmlp_ln_b", "gmlp_w_s", "gmlp_b_s", "hgrn_lb_table", "hgrn_norm_g",
             "w_branch_a", "w_branch_b", "w_out", "norm_ffn_g", "w_gate_up", "w_down", "norm_final_g"]
    outs = [loss, grad_x.reshape(1, t, D_MODEL)]
    for kind in range(4):
        for nme in order:
            outs.append(big[nme][kind] if nme in big else small[kind][nme])
    return tuple(outs)
```

```python
import functools

import jax
import jax.numpy as jnp
from jax import lax
from jax.experimental import pallas as pl
from jax.experimental.pallas import tpu as pltpu
from jax.experimental.pallas import tpu_sc as plsc

F32, BF16 = jnp.float32, jnp.bfloat16
D_MODEL = 1024
N_DEV = 8
HEADS = 8
HEAD_DIM = 128
GROUPS = 8
GMLP_CHUNK = 128
HGRN_CHUNK = 64
HGRN_SCALE = HEAD_DIM ** -0.5
D_FF = 2816
FF_BLOCK = D_FF // 4
DOWN_ROWS = D_FF // N_DEV
BRANCH_ROWS = D_MODEL // N_DEV
NORM_EPS = 1e-6
ADAM_LR, ADAM_B1, ADAM_B2, ADAM_EPS, ADAM_WD, ADAM_STEP = 0.001, 0.9, 0.999, 1e-08, 0.01, 10
SMALL_ROWS = 192
V7X_VMEM_BYTES = 64 * 1024 * 1024
VMEM_CAP = V7X_VMEM_BYTES - 6 * 1024 * 1024
MESH_ID = pl.DeviceIdType.MESH
ANY = pl.BlockSpec(memory_space=pl.ANY)
Q_POS, U_POS, GATE_POS = 0, 4, 6


def _pos_of_dev(j):
    return jnp.where(j < 2, j + 4, jnp.where(j < 6, j - 2, j))


def _dev_of_pos(p):
    return jnp.where(p < 4, p + 2, jnp.where(p < 6, p - 4, p))


def _nbytes(shape, dtype):
    n = 1
    for s in shape:
        n *= s
    return n * jnp.dtype(dtype).itemsize


def _params(blocks, scratch=(), temps=0, sem=None):
    need = 2 * sum(_nbytes(s, d) for s, d in blocks) + sum(_nbytes(s, d) for s, d in scratch) + temps
    assert need + (4 << 20) <= VMEM_CAP, need
    return pltpu.CompilerParams(dimension_semantics=sem, vmem_limit_bytes=VMEM_CAP)


def _tile(n, pref):
    return pref if n % pref == 0 else n


def _dot(a, b):
    return jnp.dot(a, b, preferred_element_type=F32)


def _dot_nt(a, b):
    return lax.dot_general(a, b, (((1,), (1,)), ((), ())), preferred_element_type=F32)


def _dot_tn(a, b):
    return lax.dot_general(a, b, (((0,), (0,)), ((), ())), preferred_element_type=F32)


def _sigmoid(x):
    return 1.0 / (1.0 + jnp.exp(-x))


_GELU_C = 0.7978845608028654


def _gelu(x):
    return x * (0.5 * (1.0 + jnp.tanh(_GELU_C * (x + 0.044715 * (x * x * x)))))


def _gelu_grad(x):
    t = jnp.tanh(_GELU_C * (x + 0.044715 * (x * x * x)))
    return 0.5 * (1.0 + t) + 0.5 * x * (1.0 - t * t) * (_GELU_C * (1.0 + 3.0 * 0.044715 * x * x))


def _rms_stats(x):
    r = lax.rsqrt(jnp.mean(x * x, axis=-1, keepdims=True) + NORM_EPS)
    return r, x * r


def _rms_bwd(dy, x, g):
    r, xh = _rms_stats(x)
    dg = jnp.sum(dy * xh, axis=0, keepdims=True)
    dxh = dy * g
    dx = r * (dxh - xh * jnp.mean(dxh * xh, axis=-1, keepdims=True))
    return dx, dg


def _split3(x):
    hi = x.astype(BF16)
    r = x - hi.astype(F32)
    mid = r.astype(BF16)
    lo = (r - mid.astype(F32)).astype(BF16)
    return hi, mid, lo


def _mask_mm(mask_bf16, x):
    hi, mid, lo = _split3(x)
    return _dot(mask_bf16, hi) + _dot(mask_bf16, mid) + _dot(mask_bf16, lo)


def _place():
    return lax.axis_index("x"), lax.axis_index("y"), lax.axis_index("c")


def _gather_copies(src, out, send, recv, loc, slicers):
    n = len(src)
    x, y, c = _place()
    me, sib = (x, y, c), (x, y, 1 - c)
    chips = [(1 - x, y), (x, 1 - y), (1 - x, 1 - y)]

    def dev(p):
        return 4 * p[0] + 2 * p[1] + p[2]

    def rc(i, k, block, to, from_src=False):
        dst = slicers[i](out[i], dev(block))
        return pltpu.make_async_remote_copy(
            src_ref=src[i] if from_src else dst, dst_ref=dst, send_sem=send.at[7 * i + k],
            recv_sem=recv.at[7 * i + k], device_id=to, device_id_type=MESH_ID)

    mine = [pltpu.make_async_copy(src[i], slicers[i](out[i], dev(me)), loc.at[i]) for i in range(n)]
    for cp in mine:
        cp.start()
    first = []
    for i in range(n):
        first.append(rc(i, 0, me, sib, True))
        for j, chip in enumerate(chips):
            first.append(rc(i, 1 + j, me, (*chip, c), True))
    for cp in first:
        cp.start()
    passed = []
    for j, chip in enumerate(chips):
        for i in range(n):
            rc(i, 1 + j, (*chip, c), me).wait_recv()
            cp = rc(i, 4 + j, (*chip, c), sib)
            cp.start()
            passed.append(cp)
    for i in range(n):
        rc(i, 0, sib, me).wait_recv()
        for j, chip in enumerate(chips):
            rc(i, 4 + j, (*chip, 1 - c), me).wait_recv()
    for cp in first + passed:
        cp.wait_send()
    for cp in mine:
        cp.wait()


def _gather_scratch(n):
    return [pltpu.SemaphoreType.DMA((7 * n,)), pltpu.SemaphoreType.DMA((7 * n,)), pltpu.SemaphoreType.DMA((n,))]


def _all_gather(name, srcs, out_shapes, slicers):
    n = len(srcs)

    def body(*refs):
        _gather_copies(refs[:n], refs[n:2 * n], *refs[2 * n:], slicers)

    return pl.pallas_call(
        body, name=name, out_shape=[jax.ShapeDtypeStruct(s, d) for s, d in out_shapes],
        in_specs=[ANY] * n, out_specs=[ANY] * n, scratch_shapes=_gather_scratch(n),
    )(*srcs)


def _handshake(peers):
    barrier = pltpu.get_barrier_semaphore()
    for peer in peers:
        pl.semaphore_signal(barrier, inc=1, device_id=peer, device_id_type=MESH_ID)
    pl.semaphore_wait(barrier, len(peers))


def _all_gather_async(name, collective_id, srcs, out_shapes, slicers):
    n = len(srcs)

    def body(*refs):
        x, y, c = _place()
        _handshake([(1 - x if dx else x, 1 - y if dy else y, 1 - c if dc else c)
                    for dx in (0, 1) for dy in (0, 1) for dc in (0, 1) if dx or dy or dc])
        _gather_copies(refs[:n], refs[n:2 * n], *refs[2 * n:], slicers)

    return _sequencer_call(name, collective_id, body, srcs, [jax.ShapeDtypeStruct(s, d) for s, d in out_shapes],
                           _gather_scratch(n))


def _sequencer_call(name, collective_id, body, operands, out_types, scratch):
    return pl.kernel(
        body, out_type=out_types, mesh=plsc.ScalarSubcoreMesh(axis_name="sequencer", num_cores=1), name=name,
        scratch_types=scratch, compiler_params=pltpu.CompilerParams(collective_id=collective_id),
    )(*operands)


def _exchange_sibling(name, collective_id, grads, shard_fns, shard_shapes):
    n = len(grads)

    def body(*refs):
        g, land = refs[:n], refs[n:2 * n]
        send, recv = refs[2 * n:]
        x, y, c = _place()
        _handshake([(x, y, 1 - c)])
        remote = []
        for i in range(n):
            for q in range(4):
                cp = pltpu.make_async_remote_copy(
                    src_ref=shard_fns[i](g[i], 2 * q + (1 - c)), dst_ref=land[i].at[q], send_sem=send.at[4 * i + q],
                    recv_sem=recv.at[4 * i + q], device_id=(x, y, 1 - c), device_id_type=MESH_ID)
                cp.start()
                remote.append(cp)
        for cp in remote:
            cp.wait()

    return _sequencer_call(name, collective_id, body, grads, [jax.ShapeDtypeStruct((4, *s), F32) for s in shard_shapes],
                           [pltpu.SemaphoreType.DMA((4 * n,)), pltpu.SemaphoreType.DMA((4 * n,))])


def _exchange_chips(name, collective_id, parts):
    n = len(parts)

    def body(*refs):
        part, out = refs[:n], refs[n:2 * n]
        send, recv = refs[2 * n:]
        x, y, c = _place()
        _handshake([(1 - x, y, c), (x, 1 - y, c), (1 - x, 1 - y, c)])
        remote = []
        for i in range(n):
            for s in range(3):
                qx = 1 - x if (s + 1) // 2 else x
                qy = 1 - y if (s + 1) % 2 else y
                cp = pltpu.make_async_remote_copy(
                    src_ref=part[i].at[2 * qx + qy], dst_ref=out[i].at[s], send_sem=send.at[3 * i + s],
                    recv_sem=recv.at[3 * i + s], device_id=(qx, qy, c), device_id_type=MESH_ID)
                cp.start()
                remote.append(cp)
        for cp in remote:
            cp.wait()

    return _sequencer_call(name, collective_id, body, parts,
                           [jax.ShapeDtypeStruct((3, *p.shape[1:]), p.dtype) for p in parts],
                           [pltpu.SemaphoreType.DMA((3 * n,)), pltpu.SemaphoreType.DMA((3 * n,))])


def _chip_partial(name, core, grad, own_block, own_index, land):
    _, rows, cols = land.shape
    tr = own_block[-2]

    def body(core_ref, a_ref, b_ref, o_ref):
        o_ref[...] = (a_ref[...] + b_ref[...]).astype(BF16)

    spec = pl.BlockSpec((None, tr, cols), lambda q, r, c: (q, r, 0))
    return pl.pallas_call(
        body, name=name, out_shape=jax.ShapeDtypeStruct(land.shape, BF16),
        grid_spec=pltpu.PrefetchScalarGridSpec(
            num_scalar_prefetch=1, grid=(4, rows // tr),
            in_specs=[pl.BlockSpec(own_block, lambda q, r, c: own_index(q, r, c[0])), spec], out_specs=spec),
        compiler_params=_params([((tr, cols), F32)] * 2 + [((tr, cols), BF16)], sem=("arbitrary", "arbitrary")),
    )(core, grad, land)


def _adamw_math(w, g, m, v):
    m = ADAM_B1 * m + (1.0 - ADAM_B1) * g
    v = ADAM_B2 * v + (1.0 - ADAM_B2) * (g * g)
    m_hat = m / (1.0 - ADAM_B1 ** ADAM_STEP)
    v_hat = v / (1.0 - ADAM_B2 ** ADAM_STEP)
    delta = -ADAM_LR * (m_hat / (jnp.sqrt(v_hat) + ADAM_EPS) + ADAM_WD * w)
    return delta, m, v


def _adamw(name, chip, own, landed, w, m, v):
    _, rows, cols = own.shape
    tr = _tile(rows, 256) if rows % 256 == 0 else _tile(rows, 176)

    def body(chip_ref, own_ref, l_ref, w_ref, m_ref, v_ref, g_out, d_out, m_out, v_out):
        g = own_ref[...].astype(F32)
        for s in range(3):
            g = g + l_ref[s].astype(F32)
        delta, m_new, v_new = _adamw_math(w_ref[...], g, m_ref[...], v_ref[...])
        g_out[...] = g
        d_out[...] = delta
        m_out[...] = m_new
        v_out[...] = v_new

    spec = pl.BlockSpec((tr, cols), lambda r, c: (r, 0))
    return pl.pallas_call(
        body, name=name, out_shape=[jax.ShapeDtypeStruct((rows, cols), F32)] * 4,
        grid_spec=pltpu.PrefetchScalarGridSpec(
            num_scalar_prefetch=1, grid=(rows // tr,),
            in_specs=[pl.BlockSpec((None, tr, cols), lambda r, c: (c[0], r, 0)),
                      pl.BlockSpec((3, tr, cols), lambda r, c: (0, r, 0)), spec, spec, spec],
            out_specs=[spec] * 4),
        compiler_params=_params([((4, tr, cols), own.dtype)] + [((tr, cols), F32)] * 7, sem=("arbitrary",)),
    )(chip, own, landed, w, m, v)


def _rms_forward(x, gain):
    t = x.shape[0]
    tm = _tile(t, 512)

    def body(x_ref, g_ref, h_ref):
        _, xh = _rms_stats(x_ref[...])
        h_ref[...] = (xh * g_ref[...]).astype(BF16)

    return pl.pallas_call(
        body, name="rms_mix_fwd", out_shape=jax.ShapeDtypeStruct((t, D_MODEL), BF16), grid=(t // tm,),
        in_specs=[pl.BlockSpec((tm, D_MODEL), lambda m: (m, 0)), pl.BlockSpec((1, D_MODEL), lambda m: (0, 0))],
        out_specs=pl.BlockSpec((tm, D_MODEL), lambda m: (m, 0)),
        compiler_params=_params([((tm, D_MODEL), F32), ((tm, D_MODEL), BF16)], temps=8 << 20, sem=("arbitrary",)),
    )(x, gain)


def _proj_forward(h, w_in_g):
    t = h.shape[0]
    tm = _tile(t, 1024)

    def body(h_ref, w_ref, o_ref):
        o_ref[...] = _dot(h_ref[...], w_ref[...])

    return pl.pallas_call(
        body, name="proj_fwd", out_shape=jax.ShapeDtypeStruct((N_DEV, t, D_MODEL), F32), grid=(N_DEV, t // tm),
        in_specs=[pl.BlockSpec((tm, D_MODEL), lambda p, m: (m, 0)),
                  pl.BlockSpec((None, D_MODEL, D_MODEL), lambda p, m: (p, 0, 0))],
        out_specs=pl.BlockSpec((None, tm, D_MODEL), lambda p, m: (p, m, 0)),
        compiler_params=_params([((tm, D_MODEL), BF16), ((D_MODEL, D_MODEL), BF16), ((tm, D_MODEL), F32)],
                                sem=("arbitrary", "arbitrary")),
    )(h, w_in_g)


def _masked_ws(ws_ref, g):
    row = lax.broadcasted_iota(jnp.int32, (GMLP_CHUNK, GMLP_CHUNK), 0)
    col = lax.broadcasted_iota(jnp.int32, (GMLP_CHUNK, GMLP_CHUNK), 1)
    return jnp.where(row >= col, ws_ref[g], 0.0).astype(BF16)


def _gmlp_forward(proj, ln_g, ln_b, w_s, bias_b):
    t = proj.shape[1]
    tm = _tile(t, 256)
    chunks = tm // GMLP_CHUNK

    def body(u_ref, v_ref, lng_ref, lnb_ref, ws_ref, bias_ref, a_ref, vn_scr):
        vv = _gelu(v_ref[...])
        mu = jnp.mean(vv, axis=-1, keepdims=True)
        cen = vv - mu
        var = jnp.mean(cen * cen, axis=-1, keepdims=True)
        vn_scr[...] = ((cen * lax.rsqrt(var + NORM_EPS)) * lng_ref[...] + lnb_ref[...]).astype(BF16)
        for g in range(GROUPS):
            wm = _masked_ws(ws_ref, g)
            cols = slice(g * HEAD_DIM, (g + 1) * HEAD_DIM)
            for c in range(chunks):
                rows = slice(c * GMLP_CHUNK, (c + 1) * GMLP_CHUNK)
                mixed = _dot(wm, vn_scr[rows, cols]) + bias_ref[g]
                a_ref[rows, cols] = (_gelu(u_ref[rows, cols]) * mixed).astype(BF16)

    small = pl.BlockSpec((GROUPS, GMLP_CHUNK, GMLP_CHUNK), lambda m: (0, 0, 0))
    vec = pl.BlockSpec((1, D_MODEL), lambda m: (0, 0))
    return pl.pallas_call(
        body, name="gmlp_fwd", out_shape=jax.ShapeDtypeStruct((t, D_MODEL), BF16), grid=(t // tm,),
        in_specs=[pl.BlockSpec((None, tm, D_MODEL), lambda m: (U_POS, m, 0)),
                  pl.BlockSpec((None, tm, D_MODEL), lambda m: (U_POS + 1, m, 0)), vec, vec, small, small],
        out_specs=pl.BlockSpec((tm, D_MODEL), lambda m: (m, 0)),
        scratch_shapes=[pltpu.VMEM((tm, D_MODEL), BF16)],
        compiler_params=_params([((tm, D_MODEL), F32)] * 2 + [((tm, D_MODEL), BF16)] + [((8, 128, 128), F32)] * 2,
                                scratch=[((tm, D_MODEL), BF16)], temps=8 << 20, sem=("arbitrary",)),
    )(proj, proj, ln_g, ln_b, w_s, bias_b)


def _lower_bound(tab_ref):
    t0, t1 = tab_ref[0:1, :], tab_ref[1:2, :]
    mx = jnp.maximum(t0, t1)
    e0, e1 = jnp.exp(t0 - mx), jnp.exp(t1 - mx)
    return e0 / (e0 + e1)


def _tri_masks():
    row = lax.broadcasted_iota(jnp.int32, (HGRN_CHUNK, HGRN_CHUNK), 0)
    col = lax.broadcasted_iota(jnp.int32, (HGRN_CHUNK, HGRN_CHUNK), 1)
    return row >= col, row <= col


def _chunk_rows(c):
    return slice(c * HGRN_CHUNK, (c + 1) * HGRN_CHUNK)


def _per_chunk(x, nc, fn):
    return jnp.concatenate([fn(x[_chunk_rows(c)]) for c in range(nc)], axis=0)


def _chunk_row_bcast(x, nc, i):
    return _per_chunk(x, nc, lambda xc: jnp.broadcast_to(xc[i:i + 1, :], (HGRN_CHUNK, HEAD_DIM)))


def _hgrn_gates(q, fl, lb, nc):
    lower, _ = _tri_masks()
    lower = lower.astype(BF16)
    s = _sigmoid(fl)
    f = lb + (1.0 - lb) * s
    k = 1.0 - f
    hi, mid, lo = _split3(jnp.log(f))
    a = jnp.concatenate([_dot(lower, hi[_chunk_rows(c)]) + _dot(lower, mid[_chunk_rows(c)]) + _dot(lower, lo[_chunk_rows(c)])
                         for c in range(nc)], axis=0)
    a_mid = _chunk_row_bcast(a, nc, HGRN_CHUNK // 2 - 1)
    a_last = _chunk_row_bcast(a, nc, HGRN_CHUNK - 1)
    qs = q * HGRN_SCALE
    e_in, e_out, e_end, e_all = jnp.exp(a - a_mid), jnp.exp(a_mid - a), jnp.exp(a_last - a), jnp.exp(a)
    decay = [jnp.exp(a[c * HGRN_CHUNK + HGRN_CHUNK - 1:(c + 1) * HGRN_CHUNK, :]) for c in range(nc)]
    return dict(s=s, f=f, k=k, decay=decay, e_in=e_in, e_out=e_out, e_end=e_end, e_all=e_all,
                qi=qs * e_in, ki=k * e_out, kd=k * e_end, qe=qs * e_all)


def _hgrn_forward(proj, lb_table, norm_g):
    t = proj.shape[1]
    tb = _tile(t, 512)
    nc = tb // HGRN_CHUNK
    n_chunks = t // HGRN_CHUNK

    def body(q_ref, f_ref, i_ref, g_ref, tab_ref, ng_ref, og_ref, o_ref, st_ref, state):
        @pl.when(pl.program_id(1) == 0)
        def _():
            state[...] = jnp.zeros_like(state)

        lower, _ = _tri_masks()
        gt = _hgrn_gates(q_ref[...], f_ref[...], _lower_bound(tab_ref), nc)
        qi, ki, kd, qe = (gt[n].astype(BF16) for n in ("qi", "ki", "kd", "qe"))
        vb = i_ref[...].astype(BF16)
        o_intra, d_state = [], []
        for c in range(nc):
            rows = _chunk_rows(c)
            p = jnp.where(lower, _dot_nt(qi[rows], ki[rows]), 0.0).astype(BF16)
            o_intra.append(_dot(p, vb[rows]))
            d_state.append(_dot_tn(vb[rows], kd[rows]))
        st = state[...]
        outs = []
        for c in range(nc):
            st_ref[c] = st
            outs.append(o_intra[c] + _dot_nt(qe[_chunk_rows(c)], st.astype(BF16)))
            st = st * gt["decay"][c] + d_state[c]
        state[...] = st
        o = jnp.concatenate(outs, axis=0)
        o_ref[...] = o
        _, oh = _rms_stats(o)
        gz = g_ref[...]
        og_ref[...] = ((oh * ng_ref[...]) * (gz * _sigmoid(gz))).astype(BF16)

    def blk(p):
        return pl.BlockSpec((None, tb, HEAD_DIM), lambda h, n: (p, n, h))

    out_blk = pl.BlockSpec((tb, HEAD_DIM), lambda h, n: (n, h))
    return pl.pallas_call(
        body, name="hgrn_fwd",
        out_shape=[jax.ShapeDtypeStruct((t, D_MODEL), BF16), jax.ShapeDtypeStruct((t, D_MODEL), F32),
                   jax.ShapeDtypeStruct((HEADS, n_chunks, HEAD_DIM, HEAD_DIM), F32)],
        grid=(HEADS, t // tb),
        in_specs=[blk(Q_POS), blk(Q_POS + 1), blk(Q_POS + 2), blk(Q_POS + 3),
                  pl.BlockSpec((2, HEAD_DIM), lambda h, n: (0, h)), pl.BlockSpec((1, HEAD_DIM), lambda h, n: (0, h))],
        out_specs=[out_blk, out_blk, pl.BlockSpec((None, nc, HEAD_DIM, HEAD_DIM), lambda h, n: (h, n, 0, 0))],
        scratch_shapes=[pltpu.VMEM((HEAD_DIM, HEAD_DIM), F32)],
        compiler_params=_params([((tb, HEAD_DIM), F32)] * 6 + [((nc, HEAD_DIM, HEAD_DIM), F32)], temps=8 << 20,
                                sem=("arbitrary", "arbitrary")),
    )(proj, proj, proj, proj, lb_table, norm_g)


def _branch_out_forward(a, og, proj, x, w_a, w_b, w_out, ffn_g):
    t = x.shape[0]
    tm = _tile(t, 256)

    def body(a_ref, og_ref, ga_ref, gb_ref, x_ref, wa_ref, wb_ref, wo_ref, g_ref, ya_ref, yb_ref, mg_ref, x1_ref, h2_ref):
        ya = _dot(a_ref[...], wa_ref[...])
        yb = _dot(og_ref[...], wb_ref[...])
        ya_ref[...] = ya
        yb_ref[...] = yb
        merged = (_sigmoid(ga_ref[...]) * ya + _sigmoid(gb_ref[...]) * yb).astype(BF16)
        mg_ref[...] = merged
        x1 = x_ref[...] + _dot(merged, wo_ref[...])
        x1_ref[...] = x1
        _, xh = _rms_stats(x1)
        h2_ref[...] = (xh * g_ref[...]).astype(BF16)

    tok = pl.BlockSpec((tm, D_MODEL), lambda m: (m, 0))
    wsp = pl.BlockSpec((D_MODEL, D_MODEL), lambda m: (0, 0))
    return pl.pallas_call(
        body, name="branch_out_fwd",
        out_shape=[jax.ShapeDtypeStruct((t, D_MODEL), F32), jax.ShapeDtypeStruct((t, D_MODEL), F32),
                   jax.ShapeDtypeStruct((t, D_MODEL), BF16), jax.ShapeDtypeStruct((t, D_MODEL), F32),
                   jax.ShapeDtypeStruct((t, D_MODEL), BF16)],
        grid=(t // tm,),
        in_specs=[tok, tok, pl.BlockSpec((None, tm, D_MODEL), lambda m: (GATE_POS, m, 0)),
                  pl.BlockSpec((None, tm, D_MODEL), lambda m: (GATE_POS + 1, m, 0)), tok, wsp, wsp, wsp,
                  pl.BlockSpec((1, D_MODEL), lambda m: (0, 0))],
        out_specs=[tok] * 5,
        compiler_params=_params([((tm, D_MODEL), BF16)] * 4 + [((tm, D_MODEL), F32)] * 6 + [((D_MODEL, D_MODEL), BF16)] * 3,
                                temps=8 << 20, sem=("arbitrary",)),
    )(a, og, proj, proj, x, w_a, w_b, w_out, ffn_g)


def _ffn_forward(h2, x1, w_gu, w_down):
    t = x1.shape[0]
    tm = _tile(t, 512)

    def body(h_ref, wg_ref, wu_ref, wd_ref, x1_ref, gu_ref, act_ref, x2_ref, acc):
        j = pl.program_id(1)
        h = h_ref[...]
        gate = _dot(h, wg_ref[...])
        up = _dot(h, wu_ref[...])
        gu_ref[0] = gate
        gu_ref[1] = up
        act = ((gate * _sigmoid(gate)) * up).astype(BF16)
        act_ref[...] = act
        part = _dot(act, wd_ref[...])

        @pl.when(j == 0)
        def _():
            acc[...] = part

        @pl.when(j > 0)
        def _():
            acc[...] += part

        @pl.when(j == 3)
        def _():
            x2_ref[...] = x1_ref[...] + acc[...]

    tok = pl.BlockSpec((tm, D_MODEL), lambda m, j: (m, 0))
    return pl.pallas_call(
        body, name="ffn_fwd",
        out_shape=[jax.ShapeDtypeStruct((4, 2, t, FF_BLOCK), F32), jax.ShapeDtypeStruct((4, t, FF_BLOCK), BF16),
                   jax.ShapeDtypeStruct((t, D_MODEL), F32)],
        grid=(t // tm, 4),
        in_specs=[tok, pl.BlockSpec((None, D_MODEL, FF_BLOCK), lambda m, j: (j, 0, 0)),
                  pl.BlockSpec((None, D_MODEL, FF_BLOCK), lambda m, j: (j + 4, 0, 0)),
                  pl.BlockSpec((FF_BLOCK, D_MODEL), lambda m, j: (j, 0)), tok],
        out_specs=[pl.BlockSpec((None, 2, tm, FF_BLOCK), lambda m, j: (j, 0, m, 0)),
                   pl.BlockSpec((None, tm, FF_BLOCK), lambda m, j: (j, m, 0)), tok],
        scratch_shapes=[pltpu.VMEM((tm, D_MODEL), F32)],
        compiler_params=_params([((tm, D_MODEL), BF16), ((D_MODEL, 768), BF16), ((D_MODEL, 768), BF16),
                                 ((FF_BLOCK, D_MODEL), BF16), ((tm, D_MODEL), F32), ((2, tm, 768), F32),
                                 ((tm, 768), BF16), ((tm, D_MODEL), F32)],
                                scratch=[((tm, D_MODEL), F32)], temps=8 << 20, sem=("arbitrary", "arbitrary")),
    )(h2, w_gu, w_gu, w_down, x1)


def _loss_and_final_backward(x2, target, final_g):
    t = x2.shape[0]
    tm = _tile(t, 256)

    def body(x_ref, t_ref, g_ref, loss_ref, dg_ref, dx_ref, dxb_ref):
        @pl.when(pl.program_id(0) == 0)
        def _():
            loss_ref[...] = jnp.zeros_like(loss_ref)
            dg_ref[...] = jnp.zeros_like(dg_ref)

        x = x_ref[...]
        g = g_ref[...]
        r, xh = _rms_stats(x)
        err = xh * g - t_ref[...]
        loss_ref[...] += 0.5 * jnp.sum(jnp.mean(err * err, axis=-1, keepdims=True), axis=0, keepdims=True)
        dy = err * (1.0 / D_MODEL)
        dg_ref[...] += jnp.sum(dy * xh, axis=0, keepdims=True)
        dxh = dy * g
        dx = r * (dxh - xh * jnp.mean(dxh * xh, axis=-1, keepdims=True))
        dx_ref[...] = dx
        dxb_ref[...] = dx.astype(BF16)

    tok = pl.BlockSpec((tm, D_MODEL), lambda m: (m, 0))
    vec = pl.BlockSpec((1, D_MODEL), lambda m: (0, 0))
    return pl.pallas_call(
        body, name="loss_final_bwd",
        out_shape=[jax.ShapeDtypeStruct((8, 128), F32), jax.ShapeDtypeStruct((1, D_MODEL), F32),
                   jax.ShapeDtypeStruct((t, D_MODEL), F32), jax.ShapeDtypeStruct((t, D_MODEL), BF16)],
        grid=(t // tm,), in_specs=[tok, tok, vec],
        out_specs=[pl.BlockSpec((8, 128), lambda m: (0, 0)), vec, tok, tok],
        compiler_params=_params([((tm, D_MODEL), F32)] * 4, temps=8 << 20, sem=("arbitrary",)),
    )(x2, target, final_g)


def _ffn_backward(dx2b, dx2, gu, x1, w_gu, w_down, ffn_g):
    t = x1.shape[0]
    tm = _tile(t, 512)

    def body(dxb_ref, dx2_ref, gu_ref, x1_ref, wg_ref, wu_ref, wd_ref, g_ref, dgu_ref, dx1_ref, dx1b_ref, dg_ref, acc):
        m, j = pl.program_id(0), pl.program_id(1)

        @pl.when((m == 0) & (j == 0))
        def _():
            dg_ref[...] = jnp.zeros_like(dg_ref)

        dact = _dot_nt(dxb_ref[...], wd_ref[...])
        gate, up = gu_ref[0], gu_ref[1]
        sg = _sigmoid(gate)
        dgate = (dact * up * (sg * (1.0 + gate * (1.0 - sg)))).astype(BF16)
        dup = (dact * (gate * sg)).astype(BF16)
        dgu_ref[0] = dgate
        dgu_ref[1] = dup
        part = _dot_nt(dgate, wg_ref[...]) + _dot_nt(dup, wu_ref[...])

        @pl.when(j == 0)
        def _():
            acc[...] = part

        @pl.when(j > 0)
        def _():
            acc[...] += part

        @pl.when(j == 3)
        def _():
            dx, dg = _rms_bwd(acc[...], x1_ref[...], g_ref[...])
            dx1 = dx2_ref[...] + dx
            dx1_ref[...] = dx1
            dx1b_ref[...] = dx1.astype(BF16)
            dg_ref[...] += dg

    tok = pl.BlockSpec((tm, D_MODEL), lambda m, j: (m, 0))
    vec = pl.BlockSpec((1, D_MODEL), lambda m, j: (0, 0))
    gu_spec = pl.BlockSpec((None, 2, tm, FF_BLOCK), lambda m, j: (j, 0, m, 0))
    return pl.pallas_call(
        body, name="ffn_bwd",
        out_shape=[jax.ShapeDtypeStruct((4, 2, t, FF_BLOCK), BF16), jax.ShapeDtypeStruct((t, D_MODEL), F32),
                   jax.ShapeDtypeStruct((t, D_MODEL), BF16), jax.ShapeDtypeStruct((1, D_MODEL), F32)],
        grid=(t // tm, 4),
        in_specs=[tok, tok, gu_spec, tok, pl.BlockSpec((None, D_MODEL, FF_BLOCK), lambda m, j: (j, 0, 0)),
                  pl.BlockSpec((None, D_MODEL, FF_BLOCK), lambda m, j: (j + 4, 0, 0)),
                  pl.BlockSpec((FF_BLOCK, D_MODEL), lambda m, j: (j, 0)), vec],
        out_specs=[gu_spec, tok, tok, vec],
        scratch_shapes=[pltpu.VMEM((tm, D_MODEL), F32)],
        compiler_params=_params([((tm, D_MODEL), BF16), ((tm, D_MODEL), F32), ((2, tm, 768), F32), ((tm, D_MODEL), F32),
                                 ((D_MODEL, 768), BF16), ((D_MODEL, 768), BF16), ((FF_BLOCK, D_MODEL), BF16),
                                 ((2, tm, 768), BF16), ((tm, D_MODEL), F32), ((tm, D_MODEL), BF16)],
                                scratch=[((tm, D_MODEL), F32)], temps=8 << 20, sem=("arbitrary", "arbitrary")),
    )(dx2b, dx2, gu, x1, w_gu, w_gu, w_down, ffn_g)


def _branch_out_backward(dx1b, ya, yb, proj, w_a, w_b, w_out):
    t = ya.shape[0]
    tm = _tile(t, 256)

    def body(dx_ref, ya_ref, yb_ref, ga_ref, gb_ref, wa_ref, wb_ref, wo_ref, dya_ref, dyb_ref, dgate_ref, da_ref, dog_ref):
        dm = _dot_nt(dx_ref[...], wo_ref[...])
        sa, sb = _sigmoid(ga_ref[...]), _sigmoid(gb_ref[...])
        dya = (dm * sa).astype(BF16)
        dyb = (dm * sb).astype(BF16)
        dya_ref[...] = dya
        dyb_ref[...] = dyb
        dgate_ref[0] = (dm * ya_ref[...] * (sa * (1.0 - sa))).astype(BF16)
        dgate_ref[1] = (dm * yb_ref[...] * (sb * (1.0 - sb))).astype(BF16)
        da_ref[...] = _dot_nt(dya, wa_ref[...])
        dog_ref[...] = _dot_nt(dyb, wb_ref[...])

    tok = pl.BlockSpec((tm, D_MODEL), lambda m: (m, 0))
    wsp = pl.BlockSpec((D_MODEL, D_MODEL), lambda m: (0, 0))
    return pl.pallas_call(
        body, name="branch_out_bwd",
        out_shape=[jax.ShapeDtypeStruct((t, D_MODEL), BF16), jax.ShapeDtypeStruct((t, D_MODEL), BF16),
                   jax.ShapeDtypeStruct((2, t, D_MODEL), BF16), jax.ShapeDtypeStruct((t, D_MODEL), F32),
                   jax.ShapeDtypeStruct((t, D_MODEL), F32)],
        grid=(t // tm,),
        in_specs=[tok, tok, tok, pl.BlockSpec((None, tm, D_MODEL), lambda m: (GATE_POS, m, 0)),
                  pl.BlockSpec((None, tm, D_MODEL), lambda m: (GATE_POS + 1, m, 0)), wsp, wsp, wsp],
        out_specs=[tok, tok, pl.BlockSpec((2, tm, D_MODEL), lambda m: (0, m, 0)), tok, tok],
        compiler_params=_params([((tm, D_MODEL), BF16)] * 5 + [((tm, D_MODEL), F32)] * 6 + [((D_MODEL, D_MODEL), BF16)] * 3,
                                temps=8 << 20, sem=("arbitrary",)),
    )(dx1b, ya, yb, proj, proj, w_a, w_b, w_out)


def _hgrn_backward(dog, o_saved, states, proj, lb_table, norm_g):
    t = proj.shape[1]
    tb = _tile(t, 256)
    nc = tb // HGRN_CHUNK
    nb = t // tb

    def body(dog_ref, o_ref, st_ref, q_ref, f_ref, i_ref, g_ref, tab_ref, ng_ref, dp_ref, dng_ref, dtab_ref, gstate):
        @pl.when(pl.program_id(1) == 0)
        def _():
            gstate[...] = jnp.zeros_like(gstate)
            dng_ref[...] = jnp.zeros_like(dng_ref)
            dtab_ref[...] = jnp.zeros_like(dtab_ref)

        lb = _lower_bound(tab_ref)
        ng = ng_ref[...]
        lower, upper = _tri_masks()
        gt = _hgrn_gates(q_ref[...], f_ref[...], lb, nc)
        qi, ki, kd, qe = (gt[n].astype(BF16) for n in ("qi", "ki", "kd", "qe"))
        vb = i_ref[...].astype(BF16)
        o, gz, d_og = o_ref[...], g_ref[...], dog_ref[...]
        r, oh = _rms_stats(o)
        sg = _sigmoid(gz)
        d_on = d_og * (gz * sg)
        dgz = d_og * (oh * ng) * (sg * (1.0 + gz * (1.0 - sg)))
        dng_ref[...] += jnp.sum(d_on * oh, axis=0, keepdims=True)
        doh = d_on * ng
        dob = (r * (doh - oh * jnp.mean(doh * oh, axis=-1, keepdims=True))).astype(BF16)
        dv_intra, dqi, dki, dqe, g_upd = [], [], [], [], []
        for c in range(nc):
            rows = _chunk_rows(c)
            p = jnp.where(lower, _dot_nt(qi[rows], ki[rows]), 0.0).astype(BF16)
            dv_intra.append(_dot_tn(p, dob[rows]))
            dp = jnp.where(lower, _dot_nt(dob[rows], vb[rows]), 0.0).astype(BF16)
            dqi.append(_dot(dp, ki[rows]))
            dki.append(_dot_tn(dp, qi[rows]))
            dqe.append(_dot(dob[rows], st_ref[c].astype(BF16)))
            g_upd.append(_dot_tn(dob[rows], qe[rows]))
        g_after = [None] * nc
        g = gstate[...]
        for c in reversed(range(nc)):
            g_after[c] = g
            g = g * gt["decay"][c] + g_upd[c]
        gstate[...] = g
        dkd, dv, da_last = [], [], []
        for c in range(nc):
            rows = _chunk_rows(c)
            gb = g_after[c].astype(BF16)
            dkd.append(_dot(vb[rows], gb))
            dv.append(dv_intra[c] + _dot_nt(kd[rows], gb))
            da_last.append(jnp.sum(g_after[c] * st_ref[c], axis=0, keepdims=True) * gt["decay"][c])
        dqi, dki, dqe, dkd, dv = (jnp.concatenate(z, axis=0) for z in (dqi, dki, dqe, dkd, dv))
        dqs = dqi * gt["e_in"] + dqe * gt["e_all"]
        dk = dki * gt["e_out"] + dkd * gt["e_end"]
        t_in, t_out, t_end = dqi * gt["qi"], dki * gt["ki"], dkd * gt["kd"]
        da = t_in - t_out + dqe * gt["qe"] - t_end
        row = lax.broadcasted_iota(jnp.int32, (HGRN_CHUNK, HEAD_DIM), 0)
        d_mid = t_out - t_in
        pieces = []
        for c in range(nc):
            rows = _chunk_rows(c)
            da_mid = jnp.sum(d_mid[rows], axis=0, keepdims=True)
            da_end = jnp.sum(t_end[rows], axis=0, keepdims=True) + da_last[c]
            da_c = da[rows] + jnp.where(row == HGRN_CHUNK // 2 - 1, da_mid, 0.0) + jnp.where(row == HGRN_CHUNK - 1, da_end, 0.0)
            pieces.append(_mask_mm(upper.astype(BF16), da_c))
        df = jnp.concatenate(pieces, axis=0) / gt["f"] - dk
        s = gt["s"]
        dlb = jnp.sum(df * (1.0 - s), axis=0, keepdims=True)
        dp_ref[0] = (dqs * HGRN_SCALE).astype(BF16)
        dp_ref[1] = (df * (1.0 - lb) * (s * (1.0 - s))).astype(BF16)
        dp_ref[2] = dv.astype(BF16)
        dp_ref[3] = dgz.astype(BF16)
        dt0 = dlb * (lb * (1.0 - lb))
        dtab_ref[0:1, :] += dt0
        dtab_ref[1:2, :] -= dt0

    def blk(p):
        return pl.BlockSpec((None, tb, HEAD_DIM), lambda h, n: (p, nb - 1 - n, h))

    tok = pl.BlockSpec((tb, HEAD_DIM), lambda h, n: (nb - 1 - n, h))
    return pl.pallas_call(
        body, name="hgrn_bwd",
        out_shape=[jax.ShapeDtypeStruct((N_DEV, t, D_MODEL), BF16), jax.ShapeDtypeStruct((1, D_MODEL), F32),
                   jax.ShapeDtypeStruct((2, D_MODEL), F32)],
        grid=(HEADS, nb),
        in_specs=[tok, tok, pl.BlockSpec((None, nc, HEAD_DIM, HEAD_DIM), lambda h, n: (h, nb - 1 - n, 0, 0)),
                  blk(Q_POS), blk(Q_POS + 1), blk(Q_POS + 2), blk(Q_POS + 3),
                  pl.BlockSpec((2, HEAD_DIM), lambda h, n: (0, h)), pl.BlockSpec((1, HEAD_DIM), lambda h, n: (0, h))],
        out_specs=[pl.BlockSpec((4, tb, HEAD_DIM), lambda h, n: (0, nb - 1 - n, h)),
                   pl.BlockSpec((1, HEAD_DIM), lambda h, n: (0, h)), pl.BlockSpec((2, HEAD_DIM), lambda h, n: (0, h))],
        scratch_shapes=[pltpu.VMEM((HEAD_DIM, HEAD_DIM), F32)],
        compiler_params=_params([((tb, HEAD_DIM), F32)] * 6 + [((nc, HEAD_DIM, HEAD_DIM), F32)] + [((4, tb, HEAD_DIM), BF16)],
                                temps=8 << 20, sem=("arbitrary", "arbitrary")),
    )(dog, o_saved, states, proj, proj, proj, proj, lb_table, norm_g)


def _gmlp_backward(dproj, da, proj, ln_g, ln_b, w_s, bias_b):
    t = proj.shape[1]
    tm = _tile(t, 256)
    chunks = tm // GMLP_CHUNK

    def body(_, da_ref, u_ref, v_ref, lng_ref, lnb_ref, ws_ref, bias_ref, dp_ref, dlng_ref, dlnb_ref, dws_ref, dbs_ref,
             vn_scr, dvn_scr):
        @pl.when(pl.program_id(0) == 0)
        def _():
            dlng_ref[...] = jnp.zeros_like(dlng_ref)
            dlnb_ref[...] = jnp.zeros_like(dlnb_ref)
            dws_ref[...] = jnp.zeros_like(dws_ref)
            dbs_ref[...] = jnp.zeros_like(dbs_ref)

        v = v_ref[...]
        vv = _gelu(v)
        mu = jnp.mean(vv, axis=-1, keepdims=True)
        cen = vv - mu
        rstd = lax.rsqrt(jnp.mean(cen * cen, axis=-1, keepdims=True) + NORM_EPS)
        vhat = cen * rstd
        lng = lng_ref[...]
        vn_scr[...] = (vhat * lng + lnb_ref[...]).astype(BF16)
        row = lax.broadcasted_iota(jnp.int32, (GMLP_CHUNK, GMLP_CHUNK), 0)
        col = lax.broadcasted_iota(jnp.int32, (GMLP_CHUNK, GMLP_CHUNK), 1)
        for g in range(GROUPS):
            wm = _masked_ws(ws_ref, g)
            cols = slice(g * HEAD_DIM, (g + 1) * HEAD_DIM)
            dws = jnp.zeros((GMLP_CHUNK, GMLP_CHUNK), F32)
            dbs = jnp.zeros((GMLP_CHUNK, GMLP_CHUNK), F32)
            for c in range(chunks):
                rows = slice(c * GMLP_CHUNK, (c + 1) * GMLP_CHUNK)
                vn = vn_scr[rows, cols]
                mixed = _dot(wm, vn) + bias_ref[g]
                u = u_ref[rows, cols]
                d_a = da_ref[rows, cols]
                dp_ref[0, rows, cols] = (d_a * mixed * _gelu_grad(u)).astype(BF16)
                dmix = d_a * _gelu(u)
                dmb = dmix.astype(BF16)
                dbs = dbs + dmix
                dws = dws + _dot_nt(dmb, vn)
                dvn_scr[rows, cols] = _dot_tn(wm, dmb)
            dws_ref[g] += jnp.where(row >= col, dws, 0.0)
            dbs_ref[g] += jnp.broadcast_to(jnp.sum(dbs, axis=-1, keepdims=True), (GMLP_CHUNK, GMLP_CHUNK))
        dvn = dvn_scr[...]
        dlng_ref[...] += jnp.sum(dvn * vhat, axis=0, keepdims=True)
        dlnb_ref[...] += jnp.sum(dvn, axis=0, keepdims=True)
        dvh = dvn * lng
        dvv = rstd * (dvh - jnp.mean(dvh, axis=-1, keepdims=True) - vhat * jnp.mean(dvh * vhat, axis=-1, keepdims=True))
        dp_ref[1] = (dvv * _gelu_grad(v)).astype(BF16)

    tok = pl.BlockSpec((tm, D_MODEL), lambda m: (m, 0))
    small = pl.BlockSpec((GROUPS, GMLP_CHUNK, GMLP_CHUNK), lambda m: (0, 0, 0))
    vec = pl.BlockSpec((1, D_MODEL), lambda m: (0, 0))
    return pl.pallas_call(
        body, name="gmlp_bwd",
        out_shape=[jax.ShapeDtypeStruct(dproj.shape, BF16), jax.ShapeDtypeStruct((1, D_MODEL), F32),
                   jax.ShapeDtypeStruct((1, D_MODEL), F32), jax.ShapeDtypeStruct((GROUPS, GMLP_CHUNK, GMLP_CHUNK), F32),
                   jax.ShapeDtypeStruct((GROUPS, GMLP_CHUNK, GMLP_CHUNK), F32)],
        grid=(t // tm,),
        in_specs=[ANY, tok, pl.BlockSpec((None, tm, D_MODEL), lambda m: (U_POS, m, 0)),
                  pl.BlockSpec((None, tm, D_MODEL), lambda m: (U_POS + 1, m, 0)), vec, vec, small, small],
        out_specs=[pl.BlockSpec((2, tm, D_MODEL), lambda m: (U_POS // 2, m, 0)), vec, vec, small, small],
        scratch_shapes=[pltpu.VMEM((tm, D_MODEL), BF16), pltpu.VMEM((tm, D_MODEL), F32)],
        input_output_aliases={0: 0},
        compiler_params=_params([((tm, D_MODEL), F32)] * 3 + [((2, tm, D_MODEL), BF16)] + [((8, 128, 128), F32)] * 4,
                                scratch=[((tm, D_MODEL), BF16), ((tm, D_MODEL), F32)], temps=12 << 20, sem=("arbitrary",)),
    )(dproj, da, proj, proj, ln_g, ln_b, w_s, bias_b)


def _place_gate_grads(dproj, dgates):
    t = dgates.shape[1]
    tm = _tile(t, 512)

    def body(_, src_ref, dst_ref):
        dst_ref[...] = src_ref[...]

    return pl.pallas_call(
        body, name="place_gate_grads", out_shape=jax.ShapeDtypeStruct(dproj.shape, BF16), grid=(t // tm,),
        in_specs=[ANY, pl.BlockSpec((2, tm, D_MODEL), lambda m: (0, m, 0))],
        out_specs=pl.BlockSpec((2, tm, D_MODEL), lambda m: (GATE_POS // 2, m, 0)),
        input_output_aliases={0: 0},
        compiler_params=_params([((2, tm, D_MODEL), BF16)] * 2, sem=("arbitrary",)),
    )(dproj, dgates)


def _input_backward(dproj, w_in_g, x, dx1, mix_g):
    t = x.shape[0]
    tm = _tile(t, 512)

    def body(dp_ref, w_ref, x_ref, dx1_ref, g_ref, dx_ref, dg_ref, acc):
        m, p = pl.program_id(0), pl.program_id(1)

        @pl.when((m == 0) & (p == 0))
        def _():
            dg_ref[...] = jnp.zeros_like(dg_ref)

        part = _dot_nt(dp_ref[...], w_ref[...])

        @pl.when(p == 0)
        def _():
            acc[...] = part

        @pl.when(p > 0)
        def _():
            acc[...] += part

        @pl.when(p == N_DEV - 1)
        def _():
            dx, dg = _rms_bwd(acc[...], x_ref[...], g_ref[...])
            dx_ref[...] = dx1_ref[...] + dx
            dg_ref[...] += dg

    tok = pl.BlockSpec((tm, D_MODEL), lambda m, p: (m, 0))
    vec = pl.BlockSpec((1, D_MODEL), lambda m, p: (0, 0))
    return pl.pallas_call(
        body, name="input_bwd",
        out_shape=[jax.ShapeDtypeStruct((t, D_MODEL), F32), jax.ShapeDtypeStruct((1, D_MODEL), F32)],
        grid=(t // tm, N_DEV),
        in_specs=[pl.BlockSpec((None, tm, D_MODEL), lambda m, p: (p, m, 0)),
                  pl.BlockSpec((None, D_MODEL, D_MODEL), lambda m, p: (p, 0, 0)), tok, tok, vec],
        out_specs=[tok, vec],
        scratch_shapes=[pltpu.VMEM((tm, D_MODEL), F32)],
        compiler_params=_params([((tm, D_MODEL), BF16), ((D_MODEL, D_MODEL), BF16)] + [((tm, D_MODEL), F32)] * 3,
                                scratch=[((tm, D_MODEL), F32)], temps=8 << 20, sem=("arbitrary", "arbitrary")),
    )(dproj, w_in_g, x, dx1, mix_g)


def _weight_grad(name, a, b, a_spec, b_spec, out_shape, out_spec, grid, blocks):
    def body(a_ref, b_ref, o_ref):
        part = _dot_tn(a_ref[...], b_ref[...])
        m = pl.program_id(len(grid) - 1)

        @pl.when(m == 0)
        def _():
            o_ref[...] = part

        @pl.when(m > 0)
        def _():
            o_ref[...] += part

    return pl.pallas_call(
        body, name=name, out_shape=jax.ShapeDtypeStruct(out_shape, F32), grid=grid, in_specs=[a_spec, b_spec],
        out_specs=out_spec, compiler_params=_params(blocks, temps=8 << 20, sem=("arbitrary",) * len(grid)),
    )(a, b)


def _pack_small(mix_g, ln_g, ln_b, w_s, b_s, lb_table, hg_norm, ffn_g, final_g):
    def part(a):
        a = a.reshape(-1, D_MODEL)
        return jnp.pad(a, ((0, 8 - a.shape[0]), (0, 0)))

    return jnp.concatenate([part(mix_g), part(ln_g), part(ln_b), part(hg_norm), part(ffn_g), part(final_g),
                            part(lb_table), part(b_s), w_s.reshape(GMLP_CHUNK, D_MODEL)], axis=0)


def _unpack_small(pack):
    return dict(norm_mix_g=pack[0:1], gmlp_ln_g=pack[8:9], gmlp_ln_b=pack[16:17], hgrn_norm_g=pack[24:25],
                norm_ffn_g=pack[32:33], norm_final_g=pack[40], hgrn_lb_table=pack[48:50],
                gmlp_b_s=pack[56:57].reshape(1, GROUPS, GMLP_CHUNK),
                gmlp_w_s=pack[64:192].reshape(1, GROUPS, GMLP_CHUNK, GMLP_CHUNK))


def _adamw_small(gathered, w, m, v):
    rows = w.shape[0]

    def body(p_ref, w_ref, m_ref, v_ref, g_out, d_out, m_out, v_out):
        g = p_ref[0]
        for j in range(1, N_DEV):
            g = g + p_ref[j]
        delta, m_new, v_new = _adamw_math(w_ref[...], g, m_ref[...], v_ref[...])
        g_out[...] = g
        d_out[...] = delta
        m_out[...] = m_new
        v_out[...] = v_new

    tr = 64
    spec = pl.BlockSpec((tr, D_MODEL), lambda r: (r, 0))
    return pl.pallas_call(
        body, name="adamw_small", out_shape=[jax.ShapeDtypeStruct((rows, D_MODEL), F32)] * 4, grid=(rows // tr,),
        in_specs=[pl.BlockSpec((N_DEV, tr, D_MODEL), lambda r: (0, r, 0)), spec, spec, spec], out_specs=[spec] * 4,
        compiler_params=_params([((N_DEV, tr, D_MODEL), F32)] + [((tr, D_MODEL), F32)] * 7, sem=("arbitrary",)),
    )(gathered, w, m, v)


def kernel(x, norm_mix_g, w_in, gmlp_ln_g, gmlp_ln_b, gmlp_w_s, gmlp_b_s, hgrn_lb_table, hgrn_norm_g, w_branch_a, w_branch_b, w_out, norm_ffn_g, w_gate_up, w_down, norm_final_g, loss_target, m_norm_mix_g, m_w_in, m_gmlp_ln_g, m_gmlp_ln_b, m_gmlp_w_s, m_gmlp_b_s, m_hgrn_lb_table, m_hgrn_norm_g, m_w_branch_a, m_w_branch_b, m_w_out, m_norm_ffn_g, m_w_gate_up, m_w_down, m_norm_final_g, v_norm_mix_g, v_w_in, v_gmlp_ln_g, v_gmlp_ln_b, v_gmlp_w_s, v_gmlp_b_s, v_hgrn_lb_table, v_hgrn_norm_g, v_w_branch_a, v_w_branch_b, v_w_out, v_norm_ffn_g, v_w_gate_up, v_w_down, v_norm_final_g):
    t = x.shape[1]
    x2d = x.reshape(t, D_MODEL)
    target = loss_target.reshape(t, D_MODEL)
    final_g = norm_final_g.reshape(1, D_MODEL)

    shards = [w_in[0].astype(BF16), w_branch_a[0].astype(BF16), w_branch_b[0].astype(BF16), w_out[0].astype(BF16),
              w_gate_up[0].astype(BF16), w_down[0].astype(BF16)]

    def rows_of(n):
        return lambda ref, j: ref.at[pl.ds(pl.multiple_of(j * n, 8), n)]

    gathered = [((N_DEV, D_MODEL, D_MODEL), BF16), ((D_MODEL, D_MODEL), BF16), ((D_MODEL, D_MODEL), BF16),
                ((D_MODEL, D_MODEL), BF16), ((N_DEV, D_MODEL, FF_BLOCK), BF16), ((D_FF, D_MODEL), BF16)]
    places = [lambda ref, j: ref.at[_pos_of_dev(j)], rows_of(BRANCH_ROWS), rows_of(BRANCH_ROWS), rows_of(BRANCH_ROWS),
              lambda ref, j: ref.at[j], rows_of(DOWN_ROWS)]
    (w_in_g,) = _all_gather("w_in_all_gather", shards[:1], gathered[:1], places[:1])
    _, later = lax.optimization_barrier((w_in_g, shards[1:]))
    w_a, w_b, w_o, w_gu, w_dn = _all_gather_async("weights_all_gather", 0, later, gathered[1:], places[1:])

    h = _rms_forward(x2d, norm_mix_g)
    proj = _proj_forward(h, w_in_g)
    bias_b = jnp.broadcast_to(gmlp_b_s[0][:, :, None], (GROUPS, GMLP_CHUNK, GMLP_CHUNK))
    a = _gmlp_forward(proj, gmlp_ln_g, gmlp_ln_b, gmlp_w_s[0], bias_b)
    og, o_saved, states = _hgrn_forward(proj, hgrn_lb_table, hgrn_norm_g)
    ya, yb, merged, x1, h2 = _branch_out_forward(a, og, proj, x2d, w_a, w_b, w_o, norm_ffn_g)
    gu, act, x2 = _ffn_forward(h2, x1, w_gu, w_dn)
    loss_tile, d_final_g, dx2, dx2b = _loss_and_final_backward(x2, target, final_g)

    core = lax.axis_index("c").astype(jnp.int32).reshape(1)
    chip = (2 * lax.axis_index("x") + lax.axis_index("y")).astype(jnp.int32).reshape(1)
    branch_rows, branch_shape = rows_of(BRANCH_ROWS), (BRANCH_ROWS, D_MODEL)
    branch_block = ((BRANCH_ROWS, D_MODEL), lambda q, r, c: (2 * q + c, 0))

    def chip_partials(names, grads, land, own_blocks):
        return [_chip_partial("chip_partial_" + nme, core, g_, blk, idx, l_)
                for nme, g_, (blk, idx), l_ in zip(names, grads, own_blocks, land)]

    tm = _tile(t, 512)
    nm = t // tm
    tok_a = pl.BlockSpec((tm, D_MODEL), lambda m: (m, 0))
    full_o = pl.BlockSpec((D_MODEL, D_MODEL), lambda m: (0, 0))
    sq_blocks = [((tm, D_MODEL), BF16)] * 2 + [((D_MODEL, D_MODEL), F32)]

    dgu, dx1, dx1b, d_ffn_g = _ffn_backward(dx2b, dx2, gu, x1, w_gu, w_dn, norm_ffn_g)
    g_gu = _weight_grad(
        "grad_w_gate_up", h2, dgu, pl.BlockSpec((tm, D_MODEL), lambda j, m: (m, 0)),
        pl.BlockSpec((None, None, tm, FF_BLOCK), lambda j, m: (j % 4, j // 4, m, 0)), (N_DEV, D_MODEL, FF_BLOCK),
        pl.BlockSpec((None, D_MODEL, FF_BLOCK), lambda j, m: (j, 0, 0)), (N_DEV, nm),
        [((tm, D_MODEL), BF16), ((tm, 768), BF16), ((D_MODEL, 768), F32)])
    g_dn = _weight_grad(
        "grad_w_down", act, dx2b, pl.BlockSpec((None, tm, FF_BLOCK), lambda j, m: (j, m, 0)),
        pl.BlockSpec((tm, D_MODEL), lambda j, m: (m, 0)), (D_FF, D_MODEL),
        pl.BlockSpec((FF_BLOCK, D_MODEL), lambda j, m: (j, 0)), (4, nm),
        [((tm, 768), BF16), ((tm, D_MODEL), BF16), ((FF_BLOCK, D_MODEL), F32)])
    names_f, grads_f = ["w_gate_up", "w_down"], [g_gu, g_dn]
    land_f = _exchange_sibling("ffn_grads_to_sibling", 2, grads_f, [lambda ref, j: ref.at[j], rows_of(DOWN_ROWS)],
                               [(D_MODEL, FF_BLOCK), (DOWN_ROWS, D_MODEL)])

    dya, dyb, dgates, da, dog = _branch_out_backward(dx1b, ya, yb, proj, w_a, w_b, w_o)
    g_a = _weight_grad("grad_w_a", a, dya, tok_a, tok_a, (D_MODEL, D_MODEL), full_o, (nm,), sq_blocks)
    g_b = _weight_grad("grad_w_b", og, dyb, tok_a, tok_a, (D_MODEL, D_MODEL), full_o, (nm,), sq_blocks)
    g_o = _weight_grad("grad_w_out", merged, dx1b, tok_a, tok_a, (D_MODEL, D_MODEL), full_o, (nm,), sq_blocks)
    names_b, grads_b = ["w_branch_a", "w_branch_b", "w_out"], [g_a, g_b, g_o]
    land_b = _exchange_sibling("branch_grads_to_sibling", 3, grads_b, [branch_rows] * 3, [branch_shape] * 3)

    part_f = chip_partials(names_f, grads_f, land_f,
                           [((None, 256, FF_BLOCK), lambda q, r, c: (2 * q + c, r, 0)),
                            ((DOWN_ROWS // 2, D_MODEL), lambda q, r, c: (2 * (2 * q + c) + r, 0))])
    landed_f = _exchange_chips("ffn_grads_to_chips", 5, part_f)

    dproj, d_hg_norm, d_lb = _hgrn_backward(dog, o_saved, states, proj, hgrn_lb_table, hgrn_norm_g)

    part_b = chip_partials(names_b, grads_b, land_b, [branch_block] * 3)
    landed_b = _exchange_chips("branch_grads_to_chips", 6, part_b)

    dproj, d_ln_g, d_ln_b, d_ws, d_bs = _gmlp_backward(dproj, da, proj, gmlp_ln_g, gmlp_ln_b, gmlp_w_s[0], bias_b)
    dproj = _place_gate_grads(dproj, dgates)
    g_in = _weight_grad(
        "grad_w_in", h, dproj, pl.BlockSpec((tm, D_MODEL), lambda p, m: (m, 0)),
        pl.BlockSpec((None, tm, D_MODEL), lambda p, m: (p, m, 0)), (N_DEV, D_MODEL, D_MODEL),
        pl.BlockSpec((None, D_MODEL, D_MODEL), lambda p, m: (p, 0, 0)), (N_DEV, nm), sq_blocks)
    land_i = _exchange_sibling("w_in_grads_to_sibling", 4, [g_in], [lambda ref, j: ref.at[_pos_of_dev(j)]],
                               [(D_MODEL, D_MODEL)])
    grad_x, d_mix_g = _input_backward(dproj, w_in_g, x2d, dx1, norm_mix_g)
    part_i = chip_partials(["w_in"], [g_in], land_i,
                           [((None, 256, D_MODEL), lambda q, r, c: (_pos_of_dev(2 * q + c), r, 0))])
    landed_i = _exchange_chips("w_in_grads_to_chips", 7, part_i)

    big = {}
    for nme, own, lnd, w, m, v in zip(
            names_f + names_b + ["w_in"], part_f + part_b + part_i, landed_f + landed_b + landed_i,
            [w_gate_up, w_down, w_branch_a, w_branch_b, w_out, w_in],
            [m_w_gate_up, m_w_down, m_w_branch_a, m_w_branch_b, m_w_out, m_w_in],
            [v_w_gate_up, v_w_down, v_w_branch_a, v_w_branch_b, v_w_out, v_w_in]):
        outs = _adamw("adamw_" + nme, chip, own, lnd, w[0], m[0], v[0])
        big[nme] = [o_[None] for o_ in outs]

    d_bs_row = d_bs[:, :, 0]
    small_partial = _pack_small(d_mix_g, d_ln_g, d_ln_b, d_ws, d_bs_row, d_lb, d_hg_norm, d_ffn_g, d_final_g)
    (small_all,) = _all_gather_async("small_grads_all_gather", 1, [small_partial],
                                     [((N_DEV, SMALL_ROWS, D_MODEL), F32)], [lambda ref, j: ref.at[j]])

    def packed(prefix_vals):
        return _pack_small(*prefix_vals)

    w_pack = packed([norm_mix_g, gmlp_ln_g, gmlp_ln_b, gmlp_w_s, gmlp_b_s, hgrn_lb_table, hgrn_norm_g, norm_ffn_g, norm_final_g])
    m_pack = packed([m_norm_mix_g, m_gmlp_ln_g, m_gmlp_ln_b, m_gmlp_w_s, m_gmlp_b_s, m_hgrn_lb_table, m_hgrn_norm_g, m_norm_ffn_g, m_norm_final_g])
    v_pack = packed([v_norm_mix_g, v_gmlp_ln_g, v_gmlp_ln_b, v_gmlp_w_s, v_gmlp_b_s, v_hgrn_lb_table, v_hgrn_norm_g, v_norm_ffn_g, v_norm_final_g])
    small = [_unpack_small(p) for p in _adamw_small(small_all, w_pack, m_pack, v_pack)]

    loss = lax.psum(loss_tile[0, 0], ("x", "y", "c"))
    order = ["norm_mix_g", "w_in", "gmlp_ln_g", "gmlp_ln_b", "gmlp_w_s", "gmlp_b_s", "hgrn_lb_table", "hgrn_norm_g",
             "w_branch_a", "w_branch_b", "w_out", "norm_ffn_g", "w_gate_up", "w_down", "norm_final_g"]
    outs = [loss, grad_x.reshape(1, t, D_MODEL)]
    for kind in range(4):
        for nme in order:
            outs.append(big[nme][kind] if nme in big else small[kind][nme])
    return tuple(outs)
```

```python
import functools

import jax
import jax.numpy as jnp
from jax import lax
from jax.experimental import pallas as pl
from jax.experimental.pallas import tpu as pltpu
from jax.experimental.pallas import tpu_sc as plsc

F32, BF16 = jnp.float32, jnp.bfloat16
D_MODEL = 1024
N_DEV = 8
HEADS = 8
HEAD_DIM = 128
GROUPS = 8
GMLP_CHUNK = 128
HGRN_CHUNK = 64
HGRN_SCALE = HEAD_DIM ** -0.5
D_FF = 2816
FF_BLOCK = D_FF // 4
DOWN_ROWS = D_FF // N_DEV
BRANCH_ROWS = D_MODEL // N_DEV
NORM_EPS = 1e-6
ADAM_LR, ADAM_B1, ADAM_B2, ADAM_EPS, ADAM_WD, ADAM_STEP = 0.001, 0.9, 0.999, 1e-08, 0.01, 10
SMALL_ROWS = 192
V7X_VMEM_BYTES = 64 * 1024 * 1024
VMEM_CAP = V7X_VMEM_BYTES - 6 * 1024 * 1024
MESH_ID = pl.DeviceIdType.MESH
ANY = pl.BlockSpec(memory_space=pl.ANY)
Q_POS, U_POS, GATE_POS = 0, 4, 6


def _pos_of_dev(j):
    return jnp.where(j < 2, j + 4, jnp.where(j < 6, j - 2, j))


def _dev_of_pos(p):
    return jnp.where(p < 4, p + 2, jnp.where(p < 6, p - 4, p))


def _nbytes(shape, dtype):
    n = 1
    for s in shape:
        n *= s
    return n * jnp.dtype(dtype).itemsize


def _params(blocks, scratch=(), temps=0, sem=None):
    need = 2 * sum(_nbytes(s, d) for s, d in blocks) + sum(_nbytes(s, d) for s, d in scratch) + temps
    assert need + (4 << 20) <= VMEM_CAP, need
    return pltpu.CompilerParams(dimension_semantics=sem, vmem_limit_bytes=VMEM_CAP)


def _tile(n, pref):
    return pref if n % pref == 0 else n


def _dot(a, b):
    return jnp.dot(a, b, preferred_element_type=F32)


def _dot_nt(a, b):
    return lax.dot_general(a, b, (((1,), (1,)), ((), ())), preferred_element_type=F32)


def _dot_tn(a, b):
    return lax.dot_general(a, b, (((0,), (0,)), ((), ())), preferred_element_type=F32)


def _sigmoid(x):
    return 1.0 / (1.0 + jnp.exp(-x))


_GELU_C = 0.7978845608028654


def _gelu(x):
    return x * (0.5 * (1.0 + jnp.tanh(_GELU_C * (x + 0.044715 * (x * x * x)))))


def _gelu_grad(x):
    t = jnp.tanh(_GELU_C * (x + 0.044715 * (x * x * x)))
    return 0.5 * (1.0 + t) + 0.5 * x * (1.0 - t * t) * (_GELU_C * (1.0 + 3.0 * 0.044715 * x * x))


def _rms_stats(x):
    r = lax.rsqrt(jnp.mean(x * x, axis=-1, keepdims=True) + NORM_EPS)
    return r, x * r


def _rms_bwd(dy, x, g):
    r, xh = _rms_stats(x)
    dg = jnp.sum(dy * xh, axis=0, keepdims=True)
    dxh = dy * g
    dx = r * (dxh - xh * jnp.mean(dxh * xh, axis=-1, keepdims=True))
    return dx, dg


def _split3(x):
    hi = x.astype(BF16)
    r = x - hi.astype(F32)
    mid = r.astype(BF16)
    lo = (r - mid.astype(F32)).astype(BF16)
    return hi, mid, lo


def _mask_mm(mask_bf16, x):
    hi, mid, lo = _split3(x)
    return _dot(mask_bf16, hi) + _dot(mask_bf16, mid) + _dot(mask_bf16, lo)


def _place():
    return lax.axis_index("x"), lax.axis_index("y"), lax.axis_index("c")


def _gather_copies(src, out, send, recv, loc, slicers):
    n = len(src)
    x, y, c = _place()
    me, sib = (x, y, c), (x, y, 1 - c)
    chips = [(1 - x, y), (x, 1 - y), (1 - x, 1 - y)]

    def dev(p):
        return 4 * p[0] + 2 * p[1] + p[2]

    def rc(i, k, block, to, from_src=False):
        dst = slicers[i](out[i], dev(block))
        return pltpu.make_async_remote_copy(
            src_ref=src[i] if from_src else dst, dst_ref=dst, send_sem=send.at[7 * i + k],
            recv_sem=recv.at[7 * i + k], device_id=to, device_id_type=MESH_ID)

    mine = [pltpu.make_async_copy(src[i], slicers[i](out[i], dev(me)), loc.at[i]) for i in range(n)]
    for cp in mine:
        cp.start()
    first = []
    for i in range(n):
        first.append(rc(i, 0, me, sib, True))
        for j, chip in enumerate(chips):
            first.append(rc(i, 1 + j, me, (*chip, c), True))
    for cp in first:
        cp.start()
    passed = []
    for j, chip in enumerate(chips):
        for i in range(n):
            rc(i, 1 + j, (*chip, c), me).wait_recv()
            cp = rc(i, 4 + j, (*chip, c), sib)
            cp.start()
            passed.append(cp)
    for i in range(n):
        rc(i, 0, sib, me).wait_recv()
        for j, chip in enumerate(chips):
            rc(i, 4 + j, (*chip, 1 - c), me).wait_recv()
    for cp in first + passed:
        cp.wait_send()
    for cp in mine:
        cp.wait()


def _gather_scratch(n):
    return [pltpu.SemaphoreType.DMA((7 * n,)), pltpu.SemaphoreType.DMA((7 * n,)), pltpu.SemaphoreType.DMA((n,))]


def _all_gather(name, srcs, out_shapes, slicers):
    n = len(srcs)

    def body(*refs):
        _gather_copies(refs[:n], refs[n:2 * n], *refs[2 * n:], slicers)

    return pl.pallas_call(
        body, name=name, out_shape=[jax.ShapeDtypeStruct(s, d) for s, d in out_shapes],
        in_specs=[ANY] * n, out_specs=[ANY] * n, scratch_shapes=_gather_scratch(n),
    )(*srcs)


def _handshake(peers):
    barrier = pltpu.get_barrier_semaphore()
    for peer in peers:
        pl.semaphore_signal(barrier, inc=1, device_id=peer, device_id_type=MESH_ID)
    pl.semaphore_wait(barrier, len(peers))


def _all_gather_async(name, collective_id, srcs, out_shapes, slicers):
    n = len(srcs)

    def body(*refs):
        x, y, c = _place()
        _handshake([(1 - x if dx else x, 1 - y if dy else y, 1 - c if dc else c)
                    for dx in (0, 1) for dy in (0, 1) for dc in (0, 1) if dx or dy or dc])
        _gather_copies(refs[:n], refs[n:2 * n], *refs[2 * n:], slicers)

    return _sequencer_call(name, collective_id, body, srcs, [jax.ShapeDtypeStruct(s, d) for s, d in out_shapes],
                           _gather_scratch(n))


def _sequencer_call(name, collective_id, body, operands, out_types, scratch):
    return pl.kernel(
        body, out_type=out_types, mesh=plsc.ScalarSubcoreMesh(axis_name="sequencer", num_cores=1), name=name,
        scratch_types=scratch, compiler_params=pltpu.CompilerParams(collective_id=collective_id),
    )(*operands)


def _exchange_sibling(name, collective_id, grads, shard_fns, shard_shapes):
    n = len(grads)

    def body(*refs):
        g, land = refs[:n], refs[n:2 * n]
        send, recv = refs[2 * n:]
        x, y, c = _place()
        _handshake([(x, y, 1 - c)])
        remote = []
        for i in range(n):
            for q in range(4):
                cp = pltpu.make_async_remote_copy(
                    src_ref=shard_fns[i](g[i], 2 * q + (1 - c)), dst_ref=land[i].at[q], send_sem=send.at[4 * i + q],
                    recv_sem=recv.at[4 * i + q], device_id=(x, y, 1 - c), device_id_type=MESH_ID)
                cp.start()
                remote.append(cp)
        for cp in remote:
            cp.wait()

    return _sequencer_call(name, collective_id, body, grads, [jax.ShapeDtypeStruct((4, *s), F32) for s in shard_shapes],
                           [pltpu.SemaphoreType.DMA((4 * n,)), pltpu.SemaphoreType.DMA((4 * n,))])


def _exchange_chips(name, collective_id, parts):
    n = len(parts)

    def body(*refs):
        part, out = refs[:n], refs[n:2 * n]
        send, recv = refs[2 * n:]
        x, y, c = _place()
        _handshake([(1 - x, y, c), (x, 1 - y, c), (1 - x, 1 - y, c)])
        remote = []
        for i in range(n):
            for s in range(3):
                qx = 1 - x if (s + 1) // 2 else x
                qy = 1 - y if (s + 1) % 2 else y
                cp = pltpu.make_async_remote_copy(
                    src_ref=part[i].at[2 * qx + qy], dst_ref=out[i].at[s], send_sem=send.at[3 * i + s],
                    recv_sem=recv.at[3 * i + s], device_id=(qx, qy, c), device_id_type=MESH_ID)
                cp.start()
                remote.append(cp)
        for cp in remote:
            cp.wait()

    return _sequencer_call(name, collective_id, body, parts,
                           [jax.ShapeDtypeStruct((3, *p.shape[1:]), p.dtype) for p in parts],
                           [pltpu.SemaphoreType.DMA((3 * n,)), pltpu.SemaphoreType.DMA((3 * n,))])


def _chip_partial(name, core, grad, own_block, own_index, land):
    _, rows, cols = land.shape
    tr = own_block[-2]

    def body(core_ref, a_ref, b_ref, o_ref):
        o_ref[...] = (a_ref[...] + b_ref[...]).astype(BF16)

    spec = pl.BlockSpec((None, tr, cols), lambda q, r, c: (q, r, 0))
    return pl.pallas_call(
        body, name=name, out_shape=jax.ShapeDtypeStruct(land.shape, BF16),
        grid_spec=pltpu.PrefetchScalarGridSpec(
            num_scalar_prefetch=1, grid=(4, rows // tr),
            in_specs=[pl.BlockSpec(own_block, lambda q, r, c: own_index(q, r, c[0])), spec], out_specs=spec),
        compiler_params=_params([((tr, cols), F32)] * 2 + [((tr, cols), BF16)], sem=("arbitrary", "arbitrary")),
    )(core, grad, land)


def _adamw_math(w, g, m, v):
    m = ADAM_B1 * m + (1.0 - ADAM_B1) * g
    v = ADAM_B2 * v + (1.0 - ADAM_B2) * (g * g)
    m_hat = m / (1.0 - ADAM_B1 ** ADAM_STEP)
    v_hat = v / (1.0 - ADAM_B2 ** ADAM_STEP)
    delta = -ADAM_LR * (m_hat / (jnp.sqrt(v_hat) + ADAM_EPS) + ADAM_WD * w)
    return delta, m, v


def _adamw(name, chip, own, landed, w, m, v):
    _, rows, cols = own.shape
    tr = _tile(rows, 256) if rows % 256 == 0 else _tile(rows, 176)

    def body(chip_ref, own_ref, l_ref, w_ref, m_ref, v_ref, g_out, d_out, m_out, v_out):
        g = own_ref[...].astype(F32)
        for s in range(3):
            g = g + l_ref[s].astype(F32)
        delta, m_new, v_new = _adamw_math(w_ref[...], g, m_ref[...], v_ref[...])
        g_out[...] = g
        d_out[...] = delta
        m_out[...] = m_new
        v_out[...] = v_new

    spec = pl.BlockSpec((tr, cols), lambda r, c: (r, 0))
    return pl.pallas_call(
        body, name=name, out_shape=[jax.ShapeDtypeStruct((rows, cols), F32)] * 4,
        grid_spec=pltpu.PrefetchScalarGridSpec(
            num_scalar_prefetch=1, grid=(rows // tr,),
            in_specs=[pl.BlockSpec((None, tr, cols), lambda r, c: (c[0], r, 0)),
                      pl.BlockSpec((3, tr, cols), lambda r, c: (0, r, 0)), spec, spec, spec],
            out_specs=[spec] * 4),
        compiler_params=_params([((4, tr, cols), own.dtype)] + [((tr, cols), F32)] * 7, sem=("arbitrary",)),
    )(chip, own, landed, w, m, v)


def _rms_forward(x, gain):
    t = x.shape[0]
    tm = _tile(t, 512)

    def body(x_ref, g_ref, h_ref):
        _, xh = _rms_stats(x_ref[...])
        h_ref[...] = (xh * g_ref[...]).astype(BF16)

    return pl.pallas_call(
        body, name="rms_mix_fwd", out_shape=jax.ShapeDtypeStruct((t, D_MODEL), BF16), grid=(t // tm,),
        in_specs=[pl.BlockSpec((tm, D_MODEL), lambda m: (m, 0)), pl.BlockSpec((1, D_MODEL), lambda m: (0, 0))],
        out_specs=pl.BlockSpec((tm, D_MODEL), lambda m: (m, 0)),
        compiler_params=_params([((tm, D_MODEL), F32), ((tm, D_MODEL), BF16)], temps=8 << 20, sem=("arbitrary",)),
    )(x, gain)


def _proj_forward(h, w_in_g):
    t = h.shape[0]
    tm = _tile(t, 1024)

    def body(h_ref, w_ref, o_ref):
        o_ref[...] = _dot(h_ref[...], w_ref[...])

    return pl.pallas_call(
        body, name="proj_fwd", out_shape=jax.ShapeDtypeStruct((N_DEV, t, D_MODEL), F32), grid=(N_DEV, t // tm),
        in_specs=[pl.BlockSpec((tm, D_MODEL), lambda p, m: (m, 0)),
                  pl.BlockSpec((None, D_MODEL, D_MODEL), lambda p, m: (p, 0, 0))],
        out_specs=pl.BlockSpec((None, tm, D_MODEL), lambda p, m: (p, m, 0)),
        compiler_params=_params([((tm, D_MODEL), BF16), ((D_MODEL, D_MODEL), BF16), ((tm, D_MODEL), F32)],
                                sem=("arbitrary", "arbitrary")),
    )(h, w_in_g)


def _masked_ws(ws_ref, g):
    row = lax.broadcasted_iota(jnp.int32, (GMLP_CHUNK, GMLP_CHUNK), 0)
    col = lax.broadcasted_iota(jnp.int32, (GMLP_CHUNK, GMLP_CHUNK), 1)
    return jnp.where(row >= col, ws_ref[g], 0.0).astype(BF16)


def _gmlp_forward(proj, ln_g, ln_b, w_s, bias_b):
    t = proj.shape[1]
    tm = _tile(t, 256)
    chunks = tm // GMLP_CHUNK

    def body(u_ref, v_ref, lng_ref, lnb_ref, ws_ref, bias_ref, a_ref, vn_scr):
        vv = _gelu(v_ref[...])
        mu = jnp.mean(vv, axis=-1, keepdims=True)
        cen = vv - mu
        var = jnp.mean(cen * cen, axis=-1, keepdims=True)
        vn_scr[...] = ((cen * lax.rsqrt(var + NORM_EPS)) * lng_ref[...] + lnb_ref[...]).astype(BF16)
        for g in range(GROUPS):
            wm = _masked_ws(ws_ref, g)
            cols = slice(g * HEAD_DIM, (g + 1) * HEAD_DIM)
            for c in range(chunks):
                rows = slice(c * GMLP_CHUNK, (c + 1) * GMLP_CHUNK)
                mixed = _dot(wm, vn_scr[rows, cols]) + bias_ref[g]
                a_ref[rows, cols] = (_gelu(u_ref[rows, cols]) * mixed).astype(BF16)

    small = pl.BlockSpec((GROUPS, GMLP_CHUNK, GMLP_CHUNK), lambda m: (0, 0, 0))
    vec = pl.BlockSpec((1, D_MODEL), lambda m: (0, 0))
    return pl.pallas_call(
        body, name="gmlp_fwd", out_shape=jax.ShapeDtypeStruct((t, D_MODEL), BF16), grid=(t // tm,),
        in_specs=[pl.BlockSpec((None, tm, D_MODEL), lambda m: (U_POS, m, 0)),
                  pl.BlockSpec((None, tm, D_MODEL), lambda m: (U_POS + 1, m, 0)), vec, vec, small, small],
        out_specs=pl.BlockSpec((tm, D_MODEL), lambda m: (m, 0)),
        scratch_shapes=[pltpu.VMEM((tm, D_MODEL), BF16)],
        compiler_params=_params([((tm, D_MODEL), F32)] * 2 + [((tm, D_MODEL), BF16)] + [((8, 128, 128), F32)] * 2,
                                scratch=[((tm, D_MODEL), BF16)], temps=8 << 20, sem=("arbitrary",)),
    )(proj, proj, ln_g, ln_b, w_s, bias_b)


def _lower_bound(tab_ref):
    t0, t1 = tab_ref[0:1, :], tab_ref[1:2, :]
    mx = jnp.maximum(t0, t1)
    e0, e1 = jnp.exp(t0 - mx), jnp.exp(t1 - mx)
    return e0 / (e0 + e1)


def _tri_masks():
    row = lax.broadcasted_iota(jnp.int32, (HGRN_CHUNK, HGRN_CHUNK), 0)
    col = lax.broadcasted_iota(jnp.int32, (HGRN_CHUNK, HGRN_CHUNK), 1)
    return row >= col, row <= col


def _chunk_rows(c):
    return slice(c * HGRN_CHUNK, (c + 1) * HGRN_CHUNK)


def _per_chunk(x, nc, fn):
    return jnp.concatenate([fn(x[_chunk_rows(c)]) for c in range(nc)], axis=0)


def _chunk_row_bcast(x, nc, i):
    return _per_chunk(x, nc, lambda xc: jnp.broadcast_to(xc[i:i + 1, :], (HGRN_CHUNK, HEAD_DIM)))


def _hgrn_gates(q, fl, lb, nc):
    lower, _ = _tri_masks()
    lower = lower.astype(BF16)
    s = _sigmoid(fl)
    f = lb + (1.0 - lb) * s
    k = 1.0 - f
    hi, mid, lo = _split3(jnp.log(f))
    a = jnp.concatenate([_dot(lower, hi[_chunk_rows(c)]) + _dot(lower, mid[_chunk_rows(c)]) + _dot(lower, lo[_chunk_rows(c)])
                         for c in range(nc)], axis=0)
    a_mid = _chunk_row_bcast(a, nc, HGRN_CHUNK // 2 - 1)
    a_last = _chunk_row_bcast(a, nc, HGRN_CHUNK - 1)
    qs = q * HGRN_SCALE
    e_in, e_out, e_end, e_all = jnp.exp(a - a_mid), jnp.exp(a_mid - a), jnp.exp(a_last - a), jnp.exp(a)
    decay = [jnp.exp(a[c * HGRN_CHUNK + HGRN_CHUNK - 1:(c + 1) * HGRN_CHUNK, :]) for c in range(nc)]
    return dict(s=s, f=f, k=k, decay=decay, e_in=e_in, e_out=e_out, e_end=e_end, e_all=e_all,
                qi=qs * e_in, ki=k * e_out, kd=k * e_end, qe=qs * e_all)


def _hgrn_forward(proj, lb_table, norm_g):
    t = proj.shape[1]
    tb = _tile(t, 512)
    nc = tb // HGRN_CHUNK
    n_chunks = t // HGRN_CHUNK

    def body(q_ref, f_ref, i_ref, g_ref, tab_ref, ng_ref, og_ref, o_ref, st_ref, state):
        @pl.when(pl.program_id(1) == 0)
        def _():
            state[...] = jnp.zeros_like(state)

        lower, _ = _tri_masks()
        gt = _hgrn_gates(q_ref[...], f_ref[...], _lower_bound(tab_ref), nc)
        qi, ki, kd, qe = (gt[n].astype(BF16) for n in ("qi", "ki", "kd", "qe"))
        vb = i_ref[...].astype(BF16)
        o_intra, d_state = [], []
        for c in range(nc):
            rows = _chunk_rows(c)
            p = jnp.where(lower, _dot_nt(qi[rows], ki[rows]), 0.0).astype(BF16)
            o_intra.append(_dot(p, vb[rows]))
            d_state.append(_dot_tn(vb[rows], kd[rows]))
        st = state[...]
        outs = []
        for c in range(nc):
            st_ref[c] = st
            outs.append(o_intra[c] + _dot_nt(qe[_chunk_rows(c)], st.astype(BF16)))
            st = st * gt["decay"][c] + d_state[c]
        state[...] = st
        o = jnp.concatenate(outs, axis=0)
        o_ref[...] = o
        _, oh = _rms_stats(o)
        gz = g_ref[...]
        og_ref[...] = ((oh * ng_ref[...]) * (gz * _sigmoid(gz))).astype(BF16)

    def blk(p):
        return pl.BlockSpec((None, tb, HEAD_DIM), lambda h, n: (p, n, h))

    out_blk = pl.BlockSpec((tb, HEAD_DIM), lambda h, n: (n, h))
    return pl.pallas_call(
        body, name="hgrn_fwd",
        out_shape=[jax.ShapeDtypeStruct((t, D_MODEL), BF16), jax.ShapeDtypeStruct((t, D_MODEL), F32),
                   jax.ShapeDtypeStruct((HEADS, n_chunks, HEAD_DIM, HEAD_DIM), F32)],
        grid=(HEADS, t // tb),
        in_specs=[blk(Q_POS), blk(Q_POS + 1), blk(Q_POS + 2), blk(Q_POS + 3),
                  pl.BlockSpec((2, HEAD_DIM), lambda h, n: (0, h)), pl.BlockSpec((1, HEAD_DIM), lambda h, n: (0, h))],
        out_specs=[out_blk, out_blk, pl.BlockSpec((None, nc, HEAD_DIM, HEAD_DIM), lambda h, n: (h, n, 0, 0))],
        scratch_shapes=[pltpu.VMEM((HEAD_DIM, HEAD_DIM), F32)],
        compiler_params=_params([((tb, HEAD_DIM), F32)] * 6 + [((nc, HEAD_DIM, HEAD_DIM), F32)], temps=8 << 20,
                                sem=("arbitrary", "arbitrary")),
    )(proj, proj, proj, proj, lb_table, norm_g)


def _branch_out_forward(a, og, proj, x, w_a, w_b, w_out, ffn_g):
    t = x.shape[0]
    tm = _tile(t, 256)

    def body(a_ref, og_ref, ga_ref, gb_ref, x_ref, wa_ref, wb_ref, wo_ref, g_ref, ya_ref, yb_ref, mg_ref, x1_ref, h2_ref):
        ya = _dot(a_ref[...], wa_ref[...])
        yb = _dot(og_ref[...], wb_ref[...])
        ya_ref[...] = ya
        yb_ref[...] = yb
        merged = (_sigmoid(ga_ref[...]) * ya + _sigmoid(gb_ref[...]) * yb).astype(BF16)
        mg_ref[...] = merged
        x1 = x_ref[...] + _dot(merged, wo_ref[...])
        x1_ref[...] = x1
        _, xh = _rms_stats(x1)
        h2_ref[...] = (xh * g_ref[...]).astype(BF16)

    tok = pl.BlockSpec((tm, D_MODEL), lambda m: (m, 0))
    wsp = pl.BlockSpec((D_MODEL, D_MODEL), lambda m: (0, 0))
    return pl.pallas_call(
        body, name="branch_out_fwd",
        out_shape=[jax.ShapeDtypeStruct((t, D_MODEL), F32), jax.ShapeDtypeStruct((t, D_MODEL), F32),
                   jax.ShapeDtypeStruct((t, D_MODEL), BF16), jax.ShapeDtypeStruct((t, D_MODEL), F32),
                   jax.ShapeDtypeStruct((t, D_MODEL), BF16)],
        grid=(t // tm,),
        in_specs=[tok, tok, pl.BlockSpec((None, tm, D_MODEL), lambda m: (GATE_POS, m, 0)),
                  pl.BlockSpec((None, tm, D_MODEL), lambda m: (GATE_POS + 1, m, 0)), tok, wsp, wsp, wsp,
                  pl.BlockSpec((1, D_MODEL), lambda m: (0, 0))],
        out_specs=[tok] * 5,
        compiler_params=_params([((tm, D_MODEL), BF16)] * 4 + [((tm, D_MODEL), F32)] * 6 + [((D_MODEL, D_MODEL), BF16)] * 3,
                                temps=8 << 20, sem=("arbitrary",)),
    )(a, og, proj, proj, x, w_a, w_b, w_out, ffn_g)


def _ffn_forward(h2, x1, w_gu, w_down):
    t = x1.shape[0]
    tm = _tile(t, 512)

    def body(h_ref, wg_ref, wu_ref, wd_ref, x1_ref, gu_ref, act_ref, x2_ref, acc):
        j = pl.program_id(1)
        h = h_ref[...]
        gate = _dot(h, wg_ref[...])
        up = _dot(h, wu_ref[...])
        gu_ref[0] = gate
        gu_ref[1] = up
        act = ((gate * _sigmoid(gate)) * up).astype(BF16)
        act_ref[...] = act
        part = _dot(act, wd_ref[...])

        @pl.when(j == 0)
        def _():
            acc[...] = part

        @pl.when(j > 0)
        def _():
            acc[...] += part

        @pl.when(j == 3)
        def _():
            x2_ref[...] = x1_ref[...] + acc[...]

    tok = pl.BlockSpec((tm, D_MODEL), lambda m, j: (m, 0))
    return pl.pallas_call(
        body, name="ffn_fwd",
        out_shape=[jax.ShapeDtypeStruct((4, 2, t, FF_BLOCK), F32), jax.ShapeDtypeStruct((4, t, FF_BLOCK), BF16),
                   jax.ShapeDtypeStruct((t, D_MODEL), F32)],
        grid=(t // tm, 4),
        in_specs=[tok, pl.BlockSpec((None, D_MODEL, FF_BLOCK), lambda m, j: (j, 0, 0)),
                  pl.BlockSpec((None, D_MODEL, FF_BLOCK), lambda m, j: (j + 4, 0, 0)),
                  pl.BlockSpec((FF_BLOCK, D_MODEL), lambda m, j: (j, 0)), tok],
        out_specs=[pl.BlockSpec((None, 2, tm, FF_BLOCK), lambda m, j: (j, 0, m, 0)),
                   pl.BlockSpec((None, tm, FF_BLOCK), lambda m, j: (j, m, 0)), tok],
        scratch_shapes=[pltpu.VMEM((tm, D_MODEL), F32)],
        compiler_params=_params([((tm, D_MODEL), BF16), ((D_MODEL, 768), BF16), ((D_MODEL, 768), BF16),
                                 ((FF_BLOCK, D_MODEL), BF16), ((tm, D_MODEL), F32), ((2, tm, 768), F32),
                                 ((tm, 768), BF16), ((tm, D_MODEL), F32)],
                                scratch=[((tm, D_MODEL), F32)], temps=8 << 20, sem=("arbitrary", "arbitrary")),
    )(h2, w_gu, w_gu, w_down, x1)


def _loss_and_final_backward(x2, target, final_g):
    t = x2.shape[0]
    tm = _tile(t, 256)

    def body(x_ref, t_ref, g_ref, loss_ref, dg_ref, dx_ref, dxb_ref):
        @pl.when(pl.program_id(0) == 0)
        def _():
            loss_ref[...] = jnp.zeros_like(loss_ref)
            dg_ref[...] = jnp.zeros_like(dg_ref)

        x = x_ref[...]
        g = g_ref[...]
        r, xh = _rms_stats(x)
        err = xh * g - t_ref[...]
        loss_ref[...] += 0.5 * jnp.sum(jnp.mean(err * err, axis=-1, keepdims=True), axis=0, keepdims=True)
        dy = err * (1.0 / D_MODEL)
        dg_ref[...] += jnp.sum(dy * xh, axis=0, keepdims=True)
        dxh = dy * g
        dx = r * (dxh - xh * jnp.mean(dxh * xh, axis=-1, keepdims=True))
        dx_ref[...] = dx
        dxb_ref[...] = dx.astype(BF16)

    tok = pl.BlockSpec((tm, D_MODEL), lambda m: (m, 0))
    vec = pl.BlockSpec((1, D_MODEL), lambda m: (0, 0))
    return pl.pallas_call(
        body, name="loss_final_bwd",
        out_shape=[jax.ShapeDtypeStruct((8, 128), F32), jax.ShapeDtypeStruct((1, D_MODEL), F32),
                   jax.ShapeDtypeStruct((t, D_MODEL), F32), jax.ShapeDtypeStruct((t, D_MODEL), BF16)],
        grid=(t // tm,), in_specs=[tok, tok, vec],
        out_specs=[pl.BlockSpec((8, 128), lambda m: (0, 0)), vec, tok, tok],
        compiler_params=_params([((tm, D_MODEL), F32)] * 4, temps=8 << 20, sem=("arbitrary",)),
    )(x2, target, final_g)


def _ffn_backward(dx2b, dx2, gu, x1, w_gu, w_down, ffn_g):
    t = x1.shape[0]
    tm = _tile(t, 512)

    def body(dxb_ref, dx2_ref, gu_ref, x1_ref, wg_ref, wu_ref, wd_ref, g_ref, dgu_ref, dx1_ref, dx1b_ref, dg_ref, acc):
        m, j = pl.program_id(0), pl.program_id(1)

        @pl.when((m == 0) & (j == 0))
        def _():
            dg_ref[...] = jnp.zeros_like(dg_ref)

        dact = _dot_nt(dxb_ref[...], wd_ref[...])
        gate, up = gu_ref[0], gu_ref[1]
        sg = _sigmoid(gate)
        dgate = (dact * up * (sg * (1.0 + gate * (1.0 - sg)))).astype(BF16)
        dup = (dact * (gate * sg)).astype(BF16)
        dgu_ref[0] = dgate
        dgu_ref[1] = dup
        part = _dot_nt(dgate, wg_ref[...]) + _dot_nt(dup, wu_ref[...])

        @pl.when(j == 0)
        def _():
            acc[...] = part

        @pl.when(j > 0)
        def _():
            acc[...] += part

        @pl.when(j == 3)
        def _():
            dx, dg = _rms_bwd(acc[...], x1_ref[...], g_ref[...])
            dx1 = dx2_ref[...] + dx
            dx1_ref[...] = dx1
            dx1b_ref[...] = dx1.astype(BF16)
            dg_ref[...] += dg

    tok = pl.BlockSpec((tm, D_MODEL), lambda m, j: (m, 0))
    vec = pl.BlockSpec((1, D_MODEL), lambda m, j: (0, 0))
    gu_spec = pl.BlockSpec((None, 2, tm, FF_BLOCK), lambda m, j: (j, 0, m, 0))
    return pl.pallas_call(
        body, name="ffn_bwd",
        out_shape=[jax.ShapeDtypeStruct((4, 2, t, FF_BLOCK), BF16), jax.ShapeDtypeStruct((t, D_MODEL), F32),
                   jax.ShapeDtypeStruct((t, D_MODEL), BF16), jax.ShapeDtypeStruct((1, D_MODEL), F32)],
        grid=(t // tm, 4),
        in_specs=[tok, tok, gu_spec, tok, pl.BlockSpec((None, D_MODEL, FF_BLOCK), lambda m, j: (j, 0, 0)),
                  pl.BlockSpec((None, D_MODEL, FF_BLOCK), lambda m, j: (j + 4, 0, 0)),
                  pl.BlockSpec((FF_BLOCK, D_MODEL), lambda m, j: (j, 0)), vec],
        out_specs=[gu_spec, tok, tok, vec],
        scratch_shapes=[pltpu.VMEM((tm, D_MODEL), F32)],
        compiler_params=_params([((tm, D_MODEL), BF16), ((tm, D_MODEL), F32), ((2, tm, 768), F32), ((tm, D_MODEL), F32),
                                 ((D_MODEL, 768), BF16), ((D_MODEL, 768), BF16), ((FF_BLOCK, D_MODEL), BF16),
                                 ((2, tm, 768), BF16), ((tm, D_MODEL), F32), ((tm, D_MODEL), BF16)],
                                scratch=[((tm, D_MODEL), F32)], temps=8 << 20, sem=("arbitrary", "arbitrary")),
    )(dx2b, dx2, gu, x1, w_gu, w_gu, w_down, ffn_g)


def _branch_out_backward(dx1b, ya, yb, proj, w_a, w_b, w_out):
    t = ya.shape[0]
    tm = _tile(t, 256)

    def body(dx_ref, ya_ref, yb_ref, ga_ref, gb_ref, wa_ref, wb_ref, wo_ref, dya_ref, dyb_ref, dgate_ref, da_ref, dog_ref):
        dm = _dot_nt(dx_ref[...], wo_ref[...])
        sa, sb = _sigmoid(ga_ref[...]), _sigmoid(gb_ref[...])
        dya = (dm * sa).astype(BF16)
        dyb = (dm * sb).astype(BF16)
        dya_ref[...] = dya
        dyb_ref[...] = dyb
        dgate_ref[0] = (dm * ya_ref[...] * (sa * (1.0 - sa))).astype(BF16)
        dgate_ref[1] = (dm * yb_ref[...] * (sb * (1.0 - sb))).astype(BF16)
        da_ref[...] = _dot_nt(dya, wa_ref[...])
        dog_ref[...] = _dot_nt(dyb, wb_ref[...])

    tok = pl.BlockSpec((tm, D_MODEL), lambda m: (m, 0))
    wsp = pl.BlockSpec((D_MODEL, D_MODEL), lambda m: (0, 0))
    return pl.pallas_call(
        body, name="branch_out_bwd",
        out_shape=[jax.ShapeDtypeStruct((t, D_MODEL), BF16), jax.ShapeDtypeStruct((t, D_MODEL), BF16),
                   jax.ShapeDtypeStruct((2, t, D_MODEL), BF16), jax.ShapeDtypeStruct((t, D_MODEL), F32),
                   jax.ShapeDtypeStruct((t, D_MODEL), F32)],
        grid=(t // tm,),
        in_specs=[tok, tok, tok, pl.BlockSpec((None, tm, D_MODEL), lambda m: (GATE_POS, m, 0)),
                  pl.BlockSpec((None, tm, D_MODEL), lambda m: (GATE_POS + 1, m, 0)), wsp, wsp, wsp],
        out_specs=[tok, tok, pl.BlockSpec((2, tm, D_MODEL), lambda m: (0, m, 0)), tok, tok],
        compiler_params=_params([((tm, D_MODEL), BF16)] * 5 + [((tm, D_MODEL), F32)] * 6 + [((D_MODEL, D_MODEL), BF16)] * 3,
                                temps=8 << 20, sem=("arbitrary",)),
    )(dx1b, ya, yb, proj, proj, w_a, w_b, w_out)


def _hgrn_backward(dog, o_saved, states, proj, lb_table, norm_g):
    t = proj.shape[1]
    tb = _tile(t, 256)
    nc = tb // HGRN_CHUNK
    nb = t // tb

    def body(dog_ref, o_ref, st_ref, q_ref, f_ref, i_ref, g_ref, tab_ref, ng_ref, dp_ref, dng_ref, dtab_ref, gstate):
        @pl.when(pl.program_id(1) == 0)
        def _():
            gstate[...] = jnp.zeros_like(gstate)
            dng_ref[...] = jnp.zeros_like(dng_ref)
            dtab_ref[...] = jnp.zeros_like(dtab_ref)

        lb = _lower_bound(tab_ref)
        ng = ng_ref[...]
        lower, upper = _tri_masks()
        gt = _hgrn_gates(q_ref[...], f_ref[...], lb, nc)
        qi, ki, kd, qe = (gt[n].astype(BF16) for n in ("qi", "ki", "kd", "qe"))
        vb = i_ref[...].astype(BF16)
        o, gz, d_og = o_ref[...], g_ref[...], dog_ref[...]
        r, oh = _rms_stats(o)
        sg = _sigmoid(gz)
        d_on = d_og * (gz * sg)
        dgz = d_og * (oh * ng) * (sg * (1.0 + gz * (1.0 - sg)))
        dng_ref[...] += jnp.sum(d_on * oh, axis=0, keepdims=True)
        doh = d_on * ng
        dob = (r * (doh - oh * jnp.mean(doh * oh, axis=-1, keepdims=True))).astype(BF16)
        dv_intra, dqi, dki, dqe, g_upd = [], [], [], [], []
        for c in range(nc):
            rows = _chunk_rows(c)
            p = jnp.where(lower, _dot_nt(qi[rows], ki[rows]), 0.0).astype(BF16)
            dv_intra.append(_dot_tn(p, dob[rows]))
            dp = jnp.where(lower, _dot_nt(dob[rows], vb[rows]), 0.0).astype(BF16)
            dqi.append(_dot(dp, ki[rows]))
            dki.append(_dot_tn(dp, qi[rows]))
            dqe.append(_dot(dob[rows], st_ref[c].astype(BF16)))
            g_upd.append(_dot_tn(dob[rows], qe[rows]))
        g_after = [None] * nc
        g = gstate[...]
        for c in reversed(range(nc)):
            g_after[c] = g
            g = g * gt["decay"][c] + g_upd[c]
        gstate[...] = g
        dkd, dv, da_last = [], [], []
        for c in range(nc):
            rows = _chunk_rows(c)
            gb = g_after[c].astype(BF16)
            dkd.append(_dot(vb[rows], gb))
            dv.append(dv_intra[c] + _dot_nt(kd[rows], gb))
            da_last.append(jnp.sum(g_after[c] * st_ref[c], axis=0, keepdims=True) * gt["decay"][c])
        dqi, dki, dqe, dkd, dv = (jnp.concatenate(z, axis=0) for z in (dqi, dki, dqe, dkd, dv))
        dqs = dqi * gt["e_in"] + dqe * gt["e_all"]
        dk = dki * gt["e_out"] + dkd * gt["e_end"]
        t_in, t_out, t_end = dqi * gt["qi"], dki * gt["ki"], dkd * gt["kd"]
        da = t_in - t_out + dqe * gt["qe"] - t_end
        row = lax.broadcasted_iota(jnp.int32, (HGRN_CHUNK, HEAD_DIM), 0)
        d_mid = t_out - t_in
        pieces = []
        for c in range(nc):
            rows = _chunk_rows(c)
            da_mid = jnp.sum(d_mid[rows], axis=0, keepdims=True)
            da_end = jnp.sum(t_end[rows], axis=0, keepdims=True) + da_last[c]
            da_c = da[rows] + jnp.where(row == HGRN_CHUNK // 2 - 1, da_mid, 0.0) + jnp.where(row == HGRN_CHUNK - 1, da_end, 0.0)
            pieces.append(_mask_mm(upper.astype(BF16), da_c))
        df = jnp.concatenate(pieces, axis=0) / gt["f"] - dk
        s = gt["s"]
        dlb = jnp.sum(df * (1.0 - s), axis=0, keepdims=True)
        dp_ref[0] = (dqs * HGRN_SCALE).astype(BF16)
        dp_ref[1] = (df * (1.0 - lb) * (s * (1.0 - s))).astype(BF16)
        dp_ref[2] = dv.astype(BF16)
        dp_ref[3] = dgz.astype(BF16)
        dt0 = dlb * (lb * (1.0 - lb))
        dtab_ref[0:1, :] += dt0
        dtab_ref[1:2, :] -= dt0

    def blk(p):
        return pl.BlockSpec((None, tb, HEAD_DIM), lambda h, n: (p, nb - 1 - n, h))

    tok = pl.BlockSpec((tb, HEAD_DIM), lambda h, n: (nb - 1 - n, h))
    return pl.pallas_call(
        body, name="hgrn_bwd",
        out_shape=[jax.ShapeDtypeStruct((N_DEV, t, D_MODEL), BF16), jax.ShapeDtypeStruct((1, D_MODEL), F32),
                   jax.ShapeDtypeStruct((2, D_MODEL), F32)],
        grid=(HEADS, nb),
        in_specs=[tok, tok, pl.BlockSpec((None, nc, HEAD_DIM, HEAD_DIM), lambda h, n: (h, nb - 1 - n, 0, 0)),
                  blk(Q_POS), blk(Q_POS + 1), blk(Q_POS + 2), blk(Q_POS + 3),
                  pl.BlockSpec((2, HEAD_DIM), lambda h, n: (0, h)), pl.BlockSpec((1, HEAD_DIM), lambda h, n: (0, h))],
        out_specs=[pl.BlockSpec((4, tb, HEAD_DIM), lambda h, n: (0, nb - 1 - n, h)),
                   pl.BlockSpec((1, HEAD_DIM), lambda h, n: (0, h)), pl.BlockSpec((2, HEAD_DIM), lambda h, n: (0, h))],
        scratch_shapes=[pltpu.VMEM((HEAD_DIM, HEAD_DIM), F32)],
        compiler_params=_params([((tb, HEAD_DIM), F32)] * 6 + [((nc, HEAD_DIM, HEAD_DIM), F32)] + [((4, tb, HEAD_DIM), BF16)],
                                temps=8 << 20, sem=("arbitrary", "arbitrary")),
    )(dog, o_saved, states, proj, proj, proj, proj, lb_table, norm_g)


def _gmlp_backward(dproj, da, proj, ln_g, ln_b, w_s, bias_b):
    t = proj.shape[1]
    tm = _tile(t, 256)
    chunks = tm // GMLP_CHUNK

    def body(_, da_ref, u_ref, v_ref, lng_ref, lnb_ref, ws_ref, bias_ref, dp_ref, dlng_ref, dlnb_ref, dws_ref, dbs_ref,
             vn_scr, dvn_scr):
        @pl.when(pl.program_id(0) == 0)
        def _():
            dlng_ref[...] = jnp.zeros_like(dlng_ref)
            dlnb_ref[...] = jnp.zeros_like(dlnb_ref)
            dws_ref[...] = jnp.zeros_like(dws_ref)
            dbs_ref[...] = jnp.zeros_like(dbs_ref)

        v = v_ref[...]
        vv = _gelu(v)
        mu = jnp.mean(vv, axis=-1, keepdims=True)
        cen = vv - mu
        rstd = lax.rsqrt(jnp.mean(cen * cen, axis=-1, keepdims=True) + NORM_EPS)
        vhat = cen * rstd
        lng = lng_ref[...]
        vn_scr[...] = (vhat * lng + lnb_ref[...]).astype(BF16)
        row = lax.broadcasted_iota(jnp.int32, (GMLP_CHUNK, GMLP_CHUNK), 0)
        col = lax.broadcasted_iota(jnp.int32, (GMLP_CHUNK, GMLP_CHUNK), 1)
        for g in range(GROUPS):
            wm = _masked_ws(ws_ref, g)
            cols = slice(g * HEAD_DIM, (g + 1) * HEAD_DIM)
            dws = jnp.zeros((GMLP_CHUNK, GMLP_CHUNK), F32)
            dbs = jnp.zeros((GMLP_CHUNK, GMLP_CHUNK), F32)
            for c in range(chunks):
                rows = slice(c * GMLP_CHUNK, (c + 1) * GMLP_CHUNK)
                vn = vn_scr[rows, cols]
                mixed = _dot(wm, vn) + bias_ref[g]
                u = u_ref[rows, cols]
                d_a = da_ref[rows, cols]
                dp_ref[0, rows, cols] = (d_a * mixed * _gelu_grad(u)).astype(BF16)
                dmix = d_a * _gelu(u)
                dmb = dmix.astype(BF16)
                dbs = dbs + dmix
                dws = dws + _dot_nt(dmb, vn)
                dvn_scr[rows, cols] = _dot_tn(wm, dmb)
            dws_ref[g] += jnp.where(row >= col, dws, 0.0)
            dbs_ref[g] += jnp.broadcast_to(jnp.sum(dbs, axis=-1, keepdims=True), (GMLP_CHUNK, GMLP_CHUNK))
        dvn = dvn_scr[...]
        dlng_ref[...] += jnp.sum(dvn * vhat, axis=0, keepdims=True)
        dlnb_ref[...] += jnp.sum(dvn, axis=0, keepdims=True)
        dvh = dvn * lng
        dvv = rstd * (dvh - jnp.mean(dvh, axis=-1, keepdims=True) - vhat * jnp.mean(dvh * vhat, axis=-1, keepdims=True))
        dp_ref[1] = (dvv * _gelu_grad(v)).astype(BF16)

    tok = pl.BlockSpec((tm, D_MODEL), lambda m: (m, 0))
    small = pl.BlockSpec((GROUPS, GMLP_CHUNK, GMLP_CHUNK), lambda m: (0, 0, 0))
    vec = pl.BlockSpec((1, D_MODEL), lambda m: (0, 0))
    return pl.pallas_call(
        body, name="gmlp_bwd",
        out_shape=[jax.ShapeDtypeStruct(dproj.shape, BF16), jax.ShapeDtypeStruct((1, D_MODEL), F32),
                   jax.ShapeDtypeStruct((1, D_MODEL), F32), jax.ShapeDtypeStruct((GROUPS, GMLP_CHUNK, GMLP_CHUNK), F32),
                   jax.ShapeDtypeStruct((GROUPS, GMLP_CHUNK, GMLP_CHUNK), F32)],
        grid=(t // tm,),
        in_specs=[ANY, tok, pl.BlockSpec((None, tm, D_MODEL), lambda m: (U_POS, m, 0)),
                  pl.BlockSpec((None, tm, D_MODEL), lambda m: (U_POS + 1, m, 0)), vec, vec, small, small],
        out_specs=[pl.BlockSpec((2, tm, D_MODEL), lambda m: (U_POS // 2, m, 0)), vec, vec, small, small],
        scratch_shapes=[pltpu.VMEM((tm, D_MODEL), BF16), pltpu.VMEM((tm, D_MODEL), F32)],
        input_output_aliases={0: 0},
        compiler_params=_params([((tm, D_MODEL), F32)] * 3 + [((2, tm, D_MODEL), BF16)] + [((8, 128, 128), F32)] * 4,
                                scratch=[((tm, D_MODEL), BF16), ((tm, D_MODEL), F32)], temps=12 << 20, sem=("arbitrary",)),
    )(dproj, da, proj, proj, ln_g, ln_b, w_s, bias_b)


def _place_gate_grads(dproj, dgates):
    t = dgates.shape[1]
    tm = _tile(t, 512)

    def body(_, src_ref, dst_ref):
        dst_ref[...] = src_ref[...]

    return pl.pallas_call(
        body, name="place_gate_grads", out_shape=jax.ShapeDtypeStruct(dproj.shape, BF16), grid=(t // tm,),
        in_specs=[ANY, pl.BlockSpec((2, tm, D_MODEL), lambda m: (0, m, 0))],
        out_specs=pl.BlockSpec((2, tm, D_MODEL), lambda m: (GATE_POS // 2, m, 0)),
        input_output_aliases={0: 0},
        compiler_params=_params([((2, tm, D_MODEL), BF16)] * 2, sem=("arbitrary",)),
    )(dproj, dgates)


def _input_backward(dproj, w_in_g, x, dx1, mix_g):
    t = x.shape[0]
    tm = _tile(t, 512)

    def body(dp_ref, w_ref, x_ref, dx1_ref, g_ref, dx_ref, dg_ref, acc):
        m, p = pl.program_id(0), pl.program_id(1)

        @pl.when((m == 0) & (p == 0))
        def _():
            dg_ref[...] = jnp.zeros_like(dg_ref)

        part = _dot_nt(dp_ref[...], w_ref[...])

        @pl.when(p == 0)
        def _():
            acc[...] = part

        @pl.when(p > 0)
        def _():
            acc[...] += part

        @pl.when(p == N_DEV - 1)
        def _():
            dx, dg = _rms_bwd(acc[...], x_ref[...], g_ref[...])
            dx_ref[...] = dx1_ref[...] + dx
            dg_ref[...] += dg

    tok = pl.BlockSpec((tm, D_MODEL), lambda m, p: (m, 0))
    vec = pl.BlockSpec((1, D_MODEL), lambda m, p: (0, 0))
    return pl.pallas_call(
        body, name="input_bwd",
        out_shape=[jax.ShapeDtypeStruct((t, D_MODEL), F32), jax.ShapeDtypeStruct((1, D_MODEL), F32)],
        grid=(t // tm, N_DEV),
        in_specs=[pl.BlockSpec((None, tm, D_MODEL), lambda m, p: (p, m, 0)),
                  pl.BlockSpec((None, D_MODEL, D_MODEL), lambda m, p: (p, 0, 0)), tok, tok, vec],
        out_specs=[tok, vec],
        scratch_shapes=[pltpu.VMEM((tm, D_MODEL), F32)],
        compiler_params=_params([((tm, D_MODEL), BF16), ((D_MODEL, D_MODEL), BF16)] + [((tm, D_MODEL), F32)] * 3,
                                scratch=[((tm, D_MODEL), F32)], temps=8 << 20, sem=("arbitrary", "arbitrary")),
    )(dproj, w_in_g, x, dx1, mix_g)


def _weight_grad(name, a, b, a_spec, b_spec, out_shape, out_spec, grid, blocks):
    def body(a_ref, b_ref, o_ref):
        part = _dot_tn(a_ref[...], b_ref[...])
        m = pl.program_id(len(grid) - 1)

        @pl.when(m == 0)
        def _():
            o_ref[...] = part

        @pl.when(m > 0)
        def _():
            o_ref[...] += part

    return pl.pallas_call(
        body, name=name, out_shape=jax.ShapeDtypeStruct(out_shape, F32), grid=grid, in_specs=[a_spec, b_spec],
        out_specs=out_spec, compiler_params=_params(blocks, temps=8 << 20, sem=("arbitrary",) * len(grid)),
    )(a, b)


def _pack_small(mix_g, ln_g, ln_b, w_s, b_s, lb_table, hg_norm, ffn_g, final_g):
    def part(a):
        a = a.reshape(-1, D_MODEL)
        return jnp.pad(a, ((0, 8 - a.shape[0]), (0, 0)))

    return jnp.concatenate([part(mix_g), part(ln_g), part(ln_b), part(hg_norm), part(ffn_g), part(final_g),
                            part(lb_table), part(b_s), w_s.reshape(GMLP_CHUNK, D_MODEL)], axis=0)


def _unpack_small(pack):
    return dict(norm_mix_g=pack[0:1], gmlp_ln_g=pack[8:9], gmlp_ln_b=pack[16:17], hgrn_norm_g=pack[24:25],
                norm_ffn_g=pack[32:33], norm_final_g=pack[40], hgrn_lb_table=pack[48:50],
                gmlp_b_s=pack[56:57].reshape(1, GROUPS, GMLP_CHUNK),
                gmlp_w_s=pack[64:192].reshape(1, GROUPS, GMLP_CHUNK, GMLP_CHUNK))


def _adamw_small(name, gathered, w, m, v):
    rows = w.shape[0]

    def body(p_ref, w_ref, m_ref, v_ref, g_out, d_out, m_out, v_out):
        g = p_ref[0]
        for j in range(1, N_DEV):
            g = g + p_ref[j]
        delta, m_new, v_new = _adamw_math(w_ref[...], g, m_ref[...], v_ref[...])
        g_out[...] = g
        d_out[...] = delta
        m_out[...] = m_new
        v_out[...] = v_new

    tr = _tile(rows, 64)
    spec = pl.BlockSpec((tr, D_MODEL), lambda r: (r, 0))
    return pl.pallas_call(
        body, name=name, out_shape=[jax.ShapeDtypeStruct((rows, D_MODEL), F32)] * 4, grid=(rows // tr,),
        in_specs=[pl.BlockSpec((N_DEV, tr, D_MODEL), lambda r: (0, r, 0)), spec, spec, spec], out_specs=[spec] * 4,
        compiler_params=_params([((N_DEV, tr, D_MODEL), F32)] + [((tr, D_MODEL), F32)] * 7, sem=("arbitrary",)),
    )(gathered, w, m, v)


def kernel(x, norm_mix_g, w_in, gmlp_ln_g, gmlp_ln_b, gmlp_w_s, gmlp_b_s, hgrn_lb_table, hgrn_norm_g, w_branch_a, w_branch_b, w_out, norm_ffn_g, w_gate_up, w_down, norm_final_g, loss_target, m_norm_mix_g, m_w_in, m_gmlp_ln_g, m_gmlp_ln_b, m_gmlp_w_s, m_gmlp_b_s, m_hgrn_lb_table, m_hgrn_norm_g, m_w_branch_a, m_w_branch_b, m_w_out, m_norm_ffn_g, m_w_gate_up, m_w_down, m_norm_final_g, v_norm_mix_g, v_w_in, v_gmlp_ln_g, v_gmlp_ln_b, v_gmlp_w_s, v_gmlp_b_s, v_hgrn_lb_table, v_hgrn_norm_g, v_w_branch_a, v_w_branch_b, v_w_out, v_norm_ffn_g, v_w_gate_up, v_w_down, v_norm_final_g):
    t = x.shape[1]
    x2d = x.reshape(t, D_MODEL)
    target = loss_target.reshape(t, D_MODEL)
    final_g = norm_final_g.reshape(1, D_MODEL)

    shards = [w_in[0].astype(BF16), w_branch_a[0].astype(BF16), w_branch_b[0].astype(BF16), w_out[0].astype(BF16),
              w_gate_up[0].astype(BF16), w_down[0].astype(BF16)]

    def rows_of(n):
        return lambda ref, j: ref.at[pl.ds(pl.multiple_of(j * n, 8), n)]

    gathered = [((N_DEV, D_MODEL, D_MODEL), BF16), ((D_MODEL, D_MODEL), BF16), ((D_MODEL, D_MODEL), BF16),
                ((D_MODEL, D_MODEL), BF16), ((N_DEV, D_MODEL, FF_BLOCK), BF16), ((D_FF, D_MODEL), BF16)]
    places = [lambda ref, j: ref.at[_pos_of_dev(j)], rows_of(BRANCH_ROWS), rows_of(BRANCH_ROWS), rows_of(BRANCH_ROWS),
              lambda ref, j: ref.at[j], rows_of(DOWN_ROWS)]
    (w_in_g,) = _all_gather("w_in_all_gather", shards[:1], gathered[:1], places[:1])
    _, later = lax.optimization_barrier((w_in_g, shards[1:]))
    w_a, w_b, w_o, w_gu, w_dn = _all_gather_async("weights_all_gather", 0, later, gathered[1:], places[1:])

    h = _rms_forward(x2d, norm_mix_g)
    proj = _proj_forward(h, w_in_g)
    bias_b = jnp.broadcast_to(gmlp_b_s[0][:, :, None], (GROUPS, GMLP_CHUNK, GMLP_CHUNK))
    a = _gmlp_forward(proj, gmlp_ln_g, gmlp_ln_b, gmlp_w_s[0], bias_b)
    og, o_saved, states = _hgrn_forward(proj, hgrn_lb_table, hgrn_norm_g)
    ya, yb, merged, x1, h2 = _branch_out_forward(a, og, proj, x2d, w_a, w_b, w_o, norm_ffn_g)
    gu, act, x2 = _ffn_forward(h2, x1, w_gu, w_dn)
    loss_tile, d_final_g, dx2, dx2b = _loss_and_final_backward(x2, target, final_g)

    core = lax.axis_index("c").astype(jnp.int32).reshape(1)
    chip = (2 * lax.axis_index("x") + lax.axis_index("y")).astype(jnp.int32).reshape(1)
    branch_rows, branch_shape = rows_of(BRANCH_ROWS), (BRANCH_ROWS, D_MODEL)
    branch_block = ((BRANCH_ROWS, D_MODEL), lambda q, r, c: (2 * q + c, 0))

    def chip_partials(names, grads, land, own_blocks):
        return [_chip_partial("chip_partial_" + nme, core, g_, blk, idx, l_)
                for nme, g_, (blk, idx), l_ in zip(names, grads, own_blocks, land)]

    tm = _tile(t, 512)
    nm = t // tm
    tok_a = pl.BlockSpec((tm, D_MODEL), lambda m: (m, 0))
    full_o = pl.BlockSpec((D_MODEL, D_MODEL), lambda m: (0, 0))
    sq_blocks = [((tm, D_MODEL), BF16)] * 2 + [((D_MODEL, D_MODEL), F32)]

    dgu, dx1, dx1b, d_ffn_g = _ffn_backward(dx2b, dx2, gu, x1, w_gu, w_dn, norm_ffn_g)
    g_gu = _weight_grad(
        "grad_w_gate_up", h2, dgu, pl.BlockSpec((tm, D_MODEL), lambda j, m: (m, 0)),
        pl.BlockSpec((None, None, tm, FF_BLOCK), lambda j, m: (j % 4, j // 4, m, 0)), (N_DEV, D_MODEL, FF_BLOCK),
        pl.BlockSpec((None, D_MODEL, FF_BLOCK), lambda j, m: (j, 0, 0)), (N_DEV, nm),
        [((tm, D_MODEL), BF16), ((tm, 768), BF16), ((D_MODEL, 768), F32)])
    g_dn = _weight_grad(
        "grad_w_down", act, dx2b, pl.BlockSpec((None, tm, FF_BLOCK), lambda j, m: (j, m, 0)),
        pl.BlockSpec((tm, D_MODEL), lambda j, m: (m, 0)), (D_FF, D_MODEL),
        pl.BlockSpec((FF_BLOCK, D_MODEL), lambda j, m: (j, 0)), (4, nm),
        [((tm, 768), BF16), ((tm, D_MODEL), BF16), ((FF_BLOCK, D_MODEL), F32)])
    names_f, grads_f = ["w_gate_up", "w_down"], [g_gu, g_dn]
    land_f = _exchange_sibling("ffn_grads_to_sibling", 2, grads_f, [lambda ref, j: ref.at[j], rows_of(DOWN_ROWS)],
                               [(D_MODEL, FF_BLOCK), (DOWN_ROWS, D_MODEL)])

    dya, dyb, dgates, da, dog = _branch_out_backward(dx1b, ya, yb, proj, w_a, w_b, w_o)
    g_a = _weight_grad("grad_w_a", a, dya, tok_a, tok_a, (D_MODEL, D_MODEL), full_o, (nm,), sq_blocks)
    g_b = _weight_grad("grad_w_b", og, dyb, tok_a, tok_a, (D_MODEL, D_MODEL), full_o, (nm,), sq_blocks)
    g_o = _weight_grad("grad_w_out", merged, dx1b, tok_a, tok_a, (D_MODEL, D_MODEL), full_o, (nm,), sq_blocks)
    names_b, grads_b = ["w_branch_a", "w_branch_b", "w_out"], [g_a, g_b, g_o]
    land_b = _exchange_sibling("branch_grads_to_sibling", 3, grads_b, [branch_rows] * 3, [branch_shape] * 3)

    part_f = chip_partials(names_f, grads_f, land_f,
                           [((None, 256, FF_BLOCK), lambda q, r, c: (2 * q + c, r, 0)),
                            ((DOWN_ROWS // 2, D_MODEL), lambda q, r, c: (2 * (2 * q + c) + r, 0))])
    landed_f = _exchange_chips("ffn_grads_to_chips", 5, part_f)

    dog, _ = lax.optimization_barrier((dog, part_f))
    dproj, d_hg_norm, d_lb = _hgrn_backward(dog, o_saved, states, proj, hgrn_lb_table, hgrn_norm_g)

    part_b = chip_partials(names_b, grads_b, land_b, [branch_block] * 3)
    landed_b = _exchange_chips("branch_grads_to_chips", 6, part_b)

    da, _ = lax.optimization_barrier((da, part_b))
    dproj, d_ln_g, d_ln_b, d_ws, d_bs = _gmlp_backward(dproj, da, proj, gmlp_ln_g, gmlp_ln_b, gmlp_w_s[0], bias_b)
    dproj = _place_gate_grads(dproj, dgates)

    def packed(vals):
        return _pack_small(*vals)

    w_pack = packed([norm_mix_g, gmlp_ln_g, gmlp_ln_b, gmlp_w_s, gmlp_b_s, hgrn_lb_table, hgrn_norm_g, norm_ffn_g, norm_final_g])
    m_pack = packed([m_norm_mix_g, m_gmlp_ln_g, m_gmlp_ln_b, m_gmlp_w_s, m_gmlp_b_s, m_hgrn_lb_table, m_hgrn_norm_g, m_norm_ffn_g, m_norm_final_g])
    v_pack = packed([v_norm_mix_g, v_gmlp_ln_g, v_gmlp_ln_b, v_gmlp_w_s, v_gmlp_b_s, v_hgrn_lb_table, v_hgrn_norm_g, v_norm_ffn_g, v_norm_final_g])
    small_partial = _pack_small(jnp.zeros((1, D_MODEL), F32), d_ln_g, d_ln_b, d_ws, d_bs[:, :, 0], d_lb, d_hg_norm, d_ffn_g,
                                d_final_g)
    (small_all,) = _all_gather_async("small_grads_all_gather", 1, [small_partial],
                                     [((N_DEV, SMALL_ROWS, D_MODEL), F32)], [lambda ref, j: ref.at[j]])

    g_in = _weight_grad(
        "grad_w_in", h, dproj, pl.BlockSpec((tm, D_MODEL), lambda p, m: (m, 0)),
        pl.BlockSpec((None, tm, D_MODEL), lambda p, m: (p, m, 0)), (N_DEV, D_MODEL, D_MODEL),
        pl.BlockSpec((None, D_MODEL, D_MODEL), lambda p, m: (p, 0, 0)), (N_DEV, nm), sq_blocks)
    land_i = _exchange_sibling("w_in_grads_to_sibling", 4, [g_in], [lambda ref, j: ref.at[_pos_of_dev(j)]],
                               [(D_MODEL, D_MODEL)])

    big = {}
    for nme, own, lnd, w, m, v in zip(
            names_f + names_b, part_f + part_b, landed_f + landed_b,
            [w_gate_up, w_down, w_branch_a, w_branch_b, w_out], [m_w_gate_up, m_w_down, m_w_branch_a, m_w_branch_b, m_w_out],
            [v_w_gate_up, v_w_down, v_w_branch_a, v_w_branch_b, v_w_out]):
        big[nme] = [o_[None] for o_ in _adamw("adamw_" + nme, chip, own, lnd, w[0], m[0], v[0])]
    small_outs = _adamw_small("adamw_small", small_all, w_pack, m_pack, v_pack)
    land_i, _ = lax.optimization_barrier((land_i, (big, small_outs)))
    part_i = chip_partials(["w_in"], [g_in], land_i,
                           [((None, 256, D_MODEL), lambda q, r, c: (_pos_of_dev(2 * q + c), r, 0))])
    landed_i = _exchange_chips("w_in_grads_to_chips", 7, part_i)

    dx1, _ = lax.optimization_barrier((dx1, part_i))
    grad_x, d_mix_g = _input_backward(dproj, w_in_g, x2d, dx1, norm_mix_g)
    big["w_in"] = [o_[None] for o_ in _adamw("adamw_w_in", chip, part_i[0], landed_i[0], w_in[0], m_w_in[0], v_w_in[0])]

    def row8(a):
        return jnp.pad(a, ((0, 7), (0, 0)))

    (mix_all,) = _all_gather_async("mix_gain_grad_all_gather", 8, [row8(d_mix_g)], [((N_DEV, 8, D_MODEL), F32)],
                                   [lambda ref, j: ref.at[j]])
    mix_outs = _adamw_small("adamw_mix_gain", mix_all, row8(norm_mix_g), row8(m_norm_mix_g), row8(v_norm_mix_g))
    small = [dict(_unpack_small(p), norm_mix_g=q[0:1]) for p, q in zip(small_outs, mix_outs)]

    loss = lax.psum(loss_tile[0, 0], ("x", "y", "c"))
    order = ["norm_mix_g", "w_in", "gmlp_ln_g", "gmlp_ln_b", "gmlp_w_s", "gmlp_b_s", "hgrn_lb_table", "hgrn_norm_g",
             "w_branch_a", "w_branch_b", "w_out", "norm_ffn_g", "w_gate_up", "w_down", "norm_final_g"]
    outs = [loss, grad_x.reshape(1, t, D_MODEL)]
    for kind in range(4):
        for nme in order:
            outs.append(big[nme][kind] if nme in big else small[kind][nme])
    return tuple(outs)
```

```python
import functools

import jax
import jax.numpy as jnp
from jax import lax
from jax.experimental import pallas as pl
from jax.experimental.pallas import tpu as pltpu
from jax.experimental.pallas import tpu_sc as plsc

F32, BF16 = jnp.float32, jnp.bfloat16
D_MODEL = 1024
N_DEV = 8
HEADS = 8
HEAD_DIM = 128
GROUPS = 8
GMLP_CHUNK = 128
HGRN_CHUNK = 64
HGRN_SCALE = HEAD_DIM ** -0.5
D_FF = 2816
FF_BLOCK = D_FF // 4
DOWN_ROWS = D_FF // N_DEV
BRANCH_ROWS = D_MODEL // N_DEV
NORM_EPS = 1e-6
ADAM_LR, ADAM_B1, ADAM_B2, ADAM_EPS, ADAM_WD, ADAM_STEP = 0.001, 0.9, 0.999, 1e-08, 0.01, 10
SMALL_ROWS = 192
V7X_VMEM_BYTES = 64 * 1024 * 1024
VMEM_CAP = V7X_VMEM_BYTES - 6 * 1024 * 1024
MESH_ID = pl.DeviceIdType.MESH
ANY = pl.BlockSpec(memory_space=pl.ANY)
Q_POS, U_POS, GATE_POS = 0, 4, 6


def _pos_of_dev(j):
    return jnp.where(j < 2, j + 4, jnp.where(j < 6, j - 2, j))


def _dev_of_pos(p):
    return jnp.where(p < 4, p + 2, jnp.where(p < 6, p - 4, p))


def _nbytes(shape, dtype):
    n = 1
    for s in shape:
        n *= s
    return n * jnp.dtype(dtype).itemsize


def _params(blocks, scratch=(), temps=0, sem=None):
    need = 2 * sum(_nbytes(s, d) for s, d in blocks) + sum(_nbytes(s, d) for s, d in scratch) + temps
    assert need + (4 << 20) <= VMEM_CAP, need
    return pltpu.CompilerParams(dimension_semantics=sem, vmem_limit_bytes=VMEM_CAP)


def _tile(n, pref):
    return pref if n % pref == 0 else n


def _dot(a, b):
    return jnp.dot(a, b, preferred_element_type=F32)


def _dot_nt(a, b):
    return lax.dot_general(a, b, (((1,), (1,)), ((), ())), preferred_element_type=F32)


def _dot_tn(a, b):
    return lax.dot_general(a, b, (((0,), (0,)), ((), ())), preferred_element_type=F32)


def _sigmoid(x):
    return 1.0 / (1.0 + jnp.exp(-x))


_GELU_C = 0.7978845608028654


def _gelu(x):
    return x * (0.5 * (1.0 + jnp.tanh(_GELU_C * (x + 0.044715 * (x * x * x)))))


def _gelu_grad(x):
    t = jnp.tanh(_GELU_C * (x + 0.044715 * (x * x * x)))
    return 0.5 * (1.0 + t) + 0.5 * x * (1.0 - t * t) * (_GELU_C * (1.0 + 3.0 * 0.044715 * x * x))


def _rms_stats(x):
    r = lax.rsqrt(jnp.mean(x * x, axis=-1, keepdims=True) + NORM_EPS)
    return r, x * r


def _rms_bwd(dy, x, g):
    r, xh = _rms_stats(x)
    dg = jnp.sum(dy * xh, axis=0, keepdims=True)
    dxh = dy * g
    dx = r * (dxh - xh * jnp.mean(dxh * xh, axis=-1, keepdims=True))
    return dx, dg


def _split3(x):
    hi = x.astype(BF16)
    r = x - hi.astype(F32)
    mid = r.astype(BF16)
    lo = (r - mid.astype(F32)).astype(BF16)
    return hi, mid, lo


def _mask_mm(mask_bf16, x):
    hi, mid, lo = _split3(x)
    return _dot(mask_bf16, hi) + _dot(mask_bf16, mid) + _dot(mask_bf16, lo)


def _place():
    return lax.axis_index("x"), lax.axis_index("y"), lax.axis_index("c")


def _gather_copies(src, out, send, recv, loc, slicers):
    n = len(src)
    x, y, c = _place()
    me, sib = (x, y, c), (x, y, 1 - c)
    chips = [(1 - x, y), (x, 1 - y), (1 - x, 1 - y)]

    def dev(p):
        return 4 * p[0] + 2 * p[1] + p[2]

    def rc(i, k, block, to, from_src=False):
        dst = slicers[i](out[i], dev(block))
        return pltpu.make_async_remote_copy(
            src_ref=src[i] if from_src else dst, dst_ref=dst, send_sem=send.at[7 * i + k],
            recv_sem=recv.at[7 * i + k], device_id=to, device_id_type=MESH_ID)

    mine = [pltpu.make_async_copy(src[i], slicers[i](out[i], dev(me)), loc.at[i]) for i in range(n)]
    for cp in mine:
        cp.start()
    first = []
    for i in range(n):
        first.append(rc(i, 0, me, sib, True))
        for j, chip in enumerate(chips):
            first.append(rc(i, 1 + j, me, (*chip, c), True))
    for cp in first:
        cp.start()
    passed = []
    for j, chip in enumerate(chips):
        for i in range(n):
            rc(i, 1 + j, (*chip, c), me).wait_recv()
            cp = rc(i, 4 + j, (*chip, c), sib)
            cp.start()
            passed.append(cp)
    for i in range(n):
        rc(i, 0, sib, me).wait_recv()
        for j, chip in enumerate(chips):
            rc(i, 4 + j, (*chip, 1 - c), me).wait_recv()
    for cp in first + passed:
        cp.wait_send()
    for cp in mine:
        cp.wait()


def _gather_scratch(n):
    return [pltpu.SemaphoreType.DMA((7 * n,)), pltpu.SemaphoreType.DMA((7 * n,)), pltpu.SemaphoreType.DMA((n,))]


def _all_gather(name, srcs, out_shapes, slicers):
    n = len(srcs)

    def body(*refs):
        _gather_copies(refs[:n], refs[n:2 * n], *refs[2 * n:], slicers)

    return pl.pallas_call(
        body, name=name, out_shape=[jax.ShapeDtypeStruct(s, d) for s, d in out_shapes],
        in_specs=[ANY] * n, out_specs=[ANY] * n, scratch_shapes=_gather_scratch(n),
    )(*srcs)


def _handshake(peers):
    barrier = pltpu.get_barrier_semaphore()
    for peer in peers:
        pl.semaphore_signal(barrier, inc=1, device_id=peer, device_id_type=MESH_ID)
    pl.semaphore_wait(barrier, len(peers))


def _all_gather_async(name, collective_id, srcs, out_shapes, slicers):
    n = len(srcs)

    def body(*refs):
        x, y, c = _place()
        _handshake([(1 - x if dx else x, 1 - y if dy else y, 1 - c if dc else c)
                    for dx in (0, 1) for dy in (0, 1) for dc in (0, 1) if dx or dy or dc])
        _gather_copies(refs[:n], refs[n:2 * n], *refs[2 * n:], slicers)

    return _sequencer_call(name, collective_id, body, srcs, [jax.ShapeDtypeStruct(s, d) for s, d in out_shapes],
                           _gather_scratch(n))


def _sequencer_call(name, collective_id, body, operands, out_types, scratch):
    return pl.kernel(
        body, out_type=out_types, mesh=plsc.ScalarSubcoreMesh(axis_name="sequencer", num_cores=1), name=name,
        scratch_types=scratch, compiler_params=pltpu.CompilerParams(collective_id=collective_id),
    )(*operands)


def _exchange_sibling(name, collective_id, grads, shard_fns, shard_shapes):
    n = len(grads)

    def body(*refs):
        g, land = refs[:n], refs[n:2 * n]
        send, recv = refs[2 * n:]
        x, y, c = _place()
        _handshake([(x, y, 1 - c)])
        remote = []
        for i in range(n):
            for q in range(4):
                cp = pltpu.make_async_remote_copy(
                    src_ref=shard_fns[i](g[i], 2 * q + (1 - c)), dst_ref=land[i].at[q], send_sem=send.at[4 * i + q],
                    recv_sem=recv.at[4 * i + q], device_id=(x, y, 1 - c), device_id_type=MESH_ID)
                cp.start()
                remote.append(cp)
        for cp in remote:
            cp.wait()

    return _sequencer_call(name, collective_id, body, grads, [jax.ShapeDtypeStruct((4, *s), F32) for s in shard_shapes],
                           [pltpu.SemaphoreType.DMA((4 * n,)), pltpu.SemaphoreType.DMA((4 * n,))])


def _exchange_chips(name, collective_id, parts):
    n = len(parts)

    def body(*refs):
        part, out = refs[:n], refs[n:2 * n]
        send, recv = refs[2 * n:]
        x, y, c = _place()
        _handshake([(1 - x, y, c), (x, 1 - y, c), (1 - x, 1 - y, c)])
        remote = []
        for i in range(n):
            for s in range(3):
                qx = 1 - x if (s + 1) // 2 else x
                qy = 1 - y if (s + 1) % 2 else y
                cp = pltpu.make_async_remote_copy(
                    src_ref=part[i].at[2 * qx + qy], dst_ref=out[i].at[s], send_sem=send.at[3 * i + s],
                    recv_sem=recv.at[3 * i + s], device_id=(qx, qy, c), device_id_type=MESH_ID)
                cp.start()
                remote.append(cp)
        for cp in remote:
            cp.wait()

    return _sequencer_call(name, collective_id, body, parts,
                           [jax.ShapeDtypeStruct((3, *p.shape[1:]), p.dtype) for p in parts],
                           [pltpu.SemaphoreType.DMA((3 * n,)), pltpu.SemaphoreType.DMA((3 * n,))])


def _chip_partial(name, core, grad, own_block, own_index, land):
    _, rows, cols = land.shape
    tr = own_block[-2]

    def body(core_ref, a_ref, b_ref, o_ref):
        o_ref[...] = (a_ref[...] + b_ref[...]).astype(BF16)

    spec = pl.BlockSpec((None, tr, cols), lambda q, r, c: (q, r, 0))
    return pl.pallas_call(
        body, name=name, out_shape=jax.ShapeDtypeStruct(land.shape, BF16),
        grid_spec=pltpu.PrefetchScalarGridSpec(
            num_scalar_prefetch=1, grid=(4, rows // tr),
            in_specs=[pl.BlockSpec(own_block, lambda q, r, c: own_index(q, r, c[0])), spec], out_specs=spec),
        compiler_params=_params([((tr, cols), F32)] * 2 + [((tr, cols), BF16)], sem=("arbitrary", "arbitrary")),
    )(core, grad, land)


def _adamw_math(w, g, m, v):
    m = ADAM_B1 * m + (1.0 - ADAM_B1) * g
    v = ADAM_B2 * v + (1.0 - ADAM_B2) * (g * g)
    m_hat = m / (1.0 - ADAM_B1 ** ADAM_STEP)
    v_hat = v / (1.0 - ADAM_B2 ** ADAM_STEP)
    delta = -ADAM_LR * (m_hat / (jnp.sqrt(v_hat) + ADAM_EPS) + ADAM_WD * w)
    return delta, m, v


def _adamw(name, chip, own, landed, w, m, v):
    _, rows, cols = own.shape
    tr = _tile(rows, 256) if rows % 256 == 0 else _tile(rows, 176)

    def body(chip_ref, own_ref, l_ref, w_ref, m_ref, v_ref, g_out, d_out, m_out, v_out):
        g = own_ref[...].astype(F32)
        for s in range(3):
            g = g + l_ref[s].astype(F32)
        delta, m_new, v_new = _adamw_math(w_ref[...], g, m_ref[...], v_ref[...])
        g_out[...] = g
        d_out[...] = delta
        m_out[...] = m_new
        v_out[...] = v_new

    spec = pl.BlockSpec((tr, cols), lambda r, c: (r, 0))
    return pl.pallas_call(
        body, name=name, out_shape=[jax.ShapeDtypeStruct((rows, cols), F32)] * 4,
        grid_spec=pltpu.PrefetchScalarGridSpec(
            num_scalar_prefetch=1, grid=(rows // tr,),
            in_specs=[pl.BlockSpec((None, tr, cols), lambda r, c: (c[0], r, 0)),
                      pl.BlockSpec((3, tr, cols), lambda r, c: (0, r, 0)), spec, spec, spec],
            out_specs=[spec] * 4),
        compiler_params=_params([((4, tr, cols), own.dtype)] + [((tr, cols), F32)] * 7, sem=("arbitrary",)),
    )(chip, own, landed, w, m, v)


def _rms_forward(x, gain):
    t = x.shape[0]
    tm = _tile(t, 512)

    def body(x_ref, g_ref, h_ref):
        _, xh = _rms_stats(x_ref[...])
        h_ref[...] = (xh * g_ref[...]).astype(BF16)

    return pl.pallas_call(
        body, name="rms_mix_fwd", out_shape=jax.ShapeDtypeStruct((t, D_MODEL), BF16), grid=(t // tm,),
        in_specs=[pl.BlockSpec((tm, D_MODEL), lambda m: (m, 0)), pl.BlockSpec((1, D_MODEL), lambda m: (0, 0))],
        out_specs=pl.BlockSpec((tm, D_MODEL), lambda m: (m, 0)),
        compiler_params=_params([((tm, D_MODEL), F32), ((tm, D_MODEL), BF16)], temps=8 << 20, sem=("arbitrary",)),
    )(x, gain)


def _proj_forward(h, w_in_g):
    t = h.shape[0]
    tm = _tile(t, 1024)

    def body(h_ref, w_ref, o_ref):
        o_ref[...] = _dot(h_ref[...], w_ref[...])

    return pl.pallas_call(
        body, name="proj_fwd", out_shape=jax.ShapeDtypeStruct((N_DEV, t, D_MODEL), F32), grid=(N_DEV, t // tm),
        in_specs=[pl.BlockSpec((tm, D_MODEL), lambda p, m: (m, 0)),
                  pl.BlockSpec((None, D_MODEL, D_MODEL), lambda p, m: (p, 0, 0))],
        out_specs=pl.BlockSpec((None, tm, D_MODEL), lambda p, m: (p, m, 0)),
        compiler_params=_params([((tm, D_MODEL), BF16), ((D_MODEL, D_MODEL), BF16), ((tm, D_MODEL), F32)],
                                sem=("arbitrary", "arbitrary")),
    )(h, w_in_g)


def _masked_ws(ws_ref, g):
    row = lax.broadcasted_iota(jnp.int32, (GMLP_CHUNK, GMLP_CHUNK), 0)
    col = lax.broadcasted_iota(jnp.int32, (GMLP_CHUNK, GMLP_CHUNK), 1)
    return jnp.where(row >= col, ws_ref[g], 0.0).astype(BF16)


def _gmlp_forward(proj, ln_g, ln_b, w_s, bias_b):
    t = proj.shape[1]
    tm = _tile(t, 256)
    chunks = tm // GMLP_CHUNK

    def body(u_ref, v_ref, lng_ref, lnb_ref, ws_ref, bias_ref, a_ref, vn_scr):
        vv = _gelu(v_ref[...])
        mu = jnp.mean(vv, axis=-1, keepdims=True)
        cen = vv - mu
        var = jnp.mean(cen * cen, axis=-1, keepdims=True)
        vn_scr[...] = ((cen * lax.rsqrt(var + NORM_EPS)) * lng_ref[...] + lnb_ref[...]).astype(BF16)
        for g in range(GROUPS):
            wm = _masked_ws(ws_ref, g)
            cols = slice(g * HEAD_DIM, (g + 1) * HEAD_DIM)
            for c in range(chunks):
                rows = slice(c * GMLP_CHUNK, (c + 1) * GMLP_CHUNK)
                mixed = _dot(wm, vn_scr[rows, cols]) + bias_ref[g]
                a_ref[rows, cols] = (_gelu(u_ref[rows, cols]) * mixed).astype(BF16)

    small = pl.BlockSpec((GROUPS, GMLP_CHUNK, GMLP_CHUNK), lambda m: (0, 0, 0))
    vec = pl.BlockSpec((1, D_MODEL), lambda m: (0, 0))
    return pl.pallas_call(
        body, name="gmlp_fwd", out_shape=jax.ShapeDtypeStruct((t, D_MODEL), BF16), grid=(t // tm,),
        in_specs=[pl.BlockSpec((None, tm, D_MODEL), lambda m: (U_POS, m, 0)),
                  pl.BlockSpec((None, tm, D_MODEL), lambda m: (U_POS + 1, m, 0)), vec, vec, small, small],
        out_specs=pl.BlockSpec((tm, D_MODEL), lambda m: (m, 0)),
        scratch_shapes=[pltpu.VMEM((tm, D_MODEL), BF16)],
        compiler_params=_params([((tm, D_MODEL), F32)] * 2 + [((tm, D_MODEL), BF16)] + [((8, 128, 128), F32)] * 2,
                                scratch=[((tm, D_MODEL), BF16)], temps=8 << 20, sem=("arbitrary",)),
    )(proj, proj, ln_g, ln_b, w_s, bias_b)


def _lower_bound(tab_ref):
    t0, t1 = tab_ref[0:1, :], tab_ref[1:2, :]
    mx = jnp.maximum(t0, t1)
    e0, e1 = jnp.exp(t0 - mx), jnp.exp(t1 - mx)
    return e0 / (e0 + e1)


def _tri_masks():
    row = lax.broadcasted_iota(jnp.int32, (HGRN_CHUNK, HGRN_CHUNK), 0)
    col = lax.broadcasted_iota(jnp.int32, (HGRN_CHUNK, HGRN_CHUNK), 1)
    return row >= col, row <= col


def _chunk_rows(c):
    return slice(c * HGRN_CHUNK, (c + 1) * HGRN_CHUNK)


def _per_chunk(x, nc, fn):
    return jnp.concatenate([fn(x[_chunk_rows(c)]) for c in range(nc)], axis=0)


def _chunk_row_bcast(x, nc, i):
    return _per_chunk(x, nc, lambda xc: jnp.broadcast_to(xc[i:i + 1, :], (HGRN_CHUNK, HEAD_DIM)))


def _hgrn_gates(q, fl, lb, nc):
    lower, _ = _tri_masks()
    lower = lower.astype(BF16)
    s = _sigmoid(fl)
    f = lb + (1.0 - lb) * s
    k = 1.0 - f
    hi, mid, lo = _split3(jnp.log(f))
    a = jnp.concatenate([_dot(lower, hi[_chunk_rows(c)]) + _dot(lower, mid[_chunk_rows(c)]) + _dot(lower, lo[_chunk_rows(c)])
                         for c in range(nc)], axis=0)
    a_mid = _chunk_row_bcast(a, nc, HGRN_CHUNK // 2 - 1)
    a_last = _chunk_row_bcast(a, nc, HGRN_CHUNK - 1)
    qs = q * HGRN_SCALE
    e_in, e_out, e_end, e_all = jnp.exp(a - a_mid), jnp.exp(a_mid - a), jnp.exp(a_last - a), jnp.exp(a)
    decay = [jnp.exp(a[c * HGRN_CHUNK + HGRN_CHUNK - 1:(c + 1) * HGRN_CHUNK, :]) for c in range(nc)]
    return dict(s=s, f=f, k=k, decay=decay, e_in=e_in, e_out=e_out, e_end=e_end, e_all=e_all,
                qi=qs * e_in, ki=k * e_out, kd=k * e_end, qe=qs * e_all)


def _hgrn_forward(proj, lb_table, norm_g):
    t = proj.shape[1]
    tb = _tile(t, 512)
    nc = tb // HGRN_CHUNK
    n_chunks = t // HGRN_CHUNK

    def body(q_ref, f_ref, i_ref, g_ref, tab_ref, ng_ref, og_ref, o_ref, st_ref, state):
        @pl.when(pl.program_id(1) == 0)
        def _():
            state[...] = jnp.zeros_like(state)

        lower, _ = _tri_masks()
        gt = _hgrn_gates(q_ref[...], f_ref[...], _lower_bound(tab_ref), nc)
        qi, ki, kd, qe = (gt[n].astype(BF16) for n in ("qi", "ki", "kd", "qe"))
        vb = i_ref[...].astype(BF16)
        o_intra, d_state = [], []
        for c in range(nc):
            rows = _chunk_rows(c)
            p = jnp.where(lower, _dot_nt(qi[rows], ki[rows]), 0.0).astype(BF16)
            o_intra.append(_dot(p, vb[rows]))
            d_state.append(_dot_tn(vb[rows], kd[rows]))
        st = state[...]
        outs = []
        for c in range(nc):
            st_ref[c] = st
            outs.append(o_intra[c] + _dot_nt(qe[_chunk_rows(c)], st.astype(BF16)))
            st = st * gt["decay"][c] + d_state[c]
        state[...] = st
        o = jnp.concatenate(outs, axis=0)
        o_ref[...] = o
        _, oh = _rms_stats(o)
        gz = g_ref[...]
        og_ref[...] = ((oh * ng_ref[...]) * (gz * _sigmoid(gz))).astype(BF16)

    def blk(p):
        return pl.BlockSpec((None, tb, HEAD_DIM), lambda h, n: (p, n, h))

    out_blk = pl.BlockSpec((tb, HEAD_DIM), lambda h, n: (n, h))
    return pl.pallas_call(
        body, name="hgrn_fwd",
        out_shape=[jax.ShapeDtypeStruct((t, D_MODEL), BF16), jax.ShapeDtypeStruct((t, D_MODEL), F32),
                   jax.ShapeDtypeStruct((HEADS, n_chunks, HEAD_DIM, HEAD_DIM), F32)],
        grid=(HEADS, t // tb),
        in_specs=[blk(Q_POS), blk(Q_POS + 1), blk(Q_POS + 2), blk(Q_POS + 3),
                  pl.BlockSpec((2, HEAD_DIM), lambda h, n: (0, h)), pl.BlockSpec((1, HEAD_DIM), lambda h, n: (0, h))],
        out_specs=[out_blk, out_blk, pl.BlockSpec((None, nc, HEAD_DIM, HEAD_DIM), lambda h, n: (h, n, 0, 0))],
        scratch_shapes=[pltpu.VMEM((HEAD_DIM, HEAD_DIM), F32)],
        compiler_params=_params([((tb, HEAD_DIM), F32)] * 6 + [((nc, HEAD_DIM, HEAD_DIM), F32)], temps=8 << 20,
                                sem=("arbitrary", "arbitrary")),
    )(proj, proj, proj, proj, lb_table, norm_g)


def _branch_out_forward(a, og, proj, x, w_a, w_b, w_out, ffn_g):
    t = x.shape[0]
    tm = _tile(t, 256)

    def body(a_ref, og_ref, ga_ref, gb_ref, x_ref, wa_ref, wb_ref, wo_ref, g_ref, ya_ref, yb_ref, mg_ref, x1_ref, h2_ref):
        ya = _dot(a_ref[...], wa_ref[...])
        yb = _dot(og_ref[...], wb_ref[...])
        ya_ref[...] = ya
        yb_ref[...] = yb
        merged = (_sigmoid(ga_ref[...]) * ya + _sigmoid(gb_ref[...]) * yb).astype(BF16)
        mg_ref[...] = merged
        x1 = x_ref[...] + _dot(merged, wo_ref[...])
        x1_ref[...] = x1
        _, xh = _rms_stats(x1)
        h2_ref[...] = (xh * g_ref[...]).astype(BF16)

    tok = pl.BlockSpec((tm, D_MODEL), lambda m: (m, 0))
    wsp = pl.BlockSpec((D_MODEL, D_MODEL), lambda m: (0, 0))
    return pl.pallas_call(
        body, name="branch_out_fwd",
        out_shape=[jax.ShapeDtypeStruct((t, D_MODEL), F32), jax.ShapeDtypeStruct((t, D_MODEL), F32),
                   jax.ShapeDtypeStruct((t, D_MODEL), BF16), jax.ShapeDtypeStruct((t, D_MODEL), F32),
                   jax.ShapeDtypeStruct((t, D_MODEL), BF16)],
        grid=(t // tm,),
        in_specs=[tok, tok, pl.BlockSpec((None, tm, D_MODEL), lambda m: (GATE_POS, m, 0)),
                  pl.BlockSpec((None, tm, D_MODEL), lambda m: (GATE_POS + 1, m, 0)), tok, wsp, wsp, wsp,
                  pl.BlockSpec((1, D_MODEL), lambda m: (0, 0))],
        out_specs=[tok] * 5,
        compiler_params=_params([((tm, D_MODEL), BF16)] * 4 + [((tm, D_MODEL), F32)] * 6 + [((D_MODEL, D_MODEL), BF16)] * 3,
                                temps=8 << 20, sem=("arbitrary",)),
    )(a, og, proj, proj, x, w_a, w_b, w_out, ffn_g)


def _ffn_forward(h2, x1, w_gu, w_down):
    t = x1.shape[0]
    tm = _tile(t, 512)

    def body(h_ref, wg_ref, wu_ref, wd_ref, x1_ref, gu_ref, act_ref, x2_ref, acc):
        j = pl.program_id(1)
        h = h_ref[...]
        gate = _dot(h, wg_ref[...])
        up = _dot(h, wu_ref[...])
        gu_ref[0] = gate
        gu_ref[1] = up
        act = ((gate * _sigmoid(gate)) * up).astype(BF16)
        act_ref[...] = act
        part = _dot(act, wd_ref[...])

        @pl.when(j == 0)
        def _():
            acc[...] = part

        @pl.when(j > 0)
        def _():
            acc[...] += part

        @pl.when(j == 3)
        def _():
            x2_ref[...] = x1_ref[...] + acc[...]

    tok = pl.BlockSpec((tm, D_MODEL), lambda m, j: (m, 0))
    return pl.pallas_call(
        body, name="ffn_fwd",
        out_shape=[jax.ShapeDtypeStruct((4, 2, t, FF_BLOCK), F32), jax.ShapeDtypeStruct((4, t, FF_BLOCK), BF16),
                   jax.ShapeDtypeStruct((t, D_MODEL), F32)],
        grid=(t // tm, 4),
        in_specs=[tok, pl.BlockSpec((None, D_MODEL, FF_BLOCK), lambda m, j: (j, 0, 0)),
                  pl.BlockSpec((None, D_MODEL, FF_BLOCK), lambda m, j: (j + 4, 0, 0)),
                  pl.BlockSpec((FF_BLOCK, D_MODEL), lambda m, j: (j, 0)), tok],
        out_specs=[pl.BlockSpec((None, 2, tm, FF_BLOCK), lambda m, j: (j, 0, m, 0)),
                   pl.BlockSpec((None, tm, FF_BLOCK), lambda m, j: (j, m, 0)), tok],
        scratch_shapes=[pltpu.VMEM((tm, D_MODEL), F32)],
        compiler_params=_params([((tm, D_MODEL), BF16), ((D_MODEL, 768), BF16), ((D_MODEL, 768), BF16),
                                 ((FF_BLOCK, D_MODEL), BF16), ((tm, D_MODEL), F32), ((2, tm, 768), F32),
                                 ((tm, 768), BF16), ((tm, D_MODEL), F32)],
                                scratch=[((tm, D_MODEL), F32)], temps=8 << 20, sem=("arbitrary", "arbitrary")),
    )(h2, w_gu, w_gu, w_down, x1)


def _loss_and_final_backward(x2, target, final_g):
    t = x2.shape[0]
    tm = _tile(t, 256)

    def body(x_ref, t_ref, g_ref, loss_ref, dg_ref, dx_ref, dxb_ref):
        @pl.when(pl.program_id(0) == 0)
        def _():
            loss_ref[...] = jnp.zeros_like(loss_ref)
            dg_ref[...] = jnp.zeros_like(dg_ref)

        x = x_ref[...]
        g = g_ref[...]
        r, xh = _rms_stats(x)
        err = xh * g - t_ref[...]
        loss_ref[...] += 0.5 * jnp.sum(jnp.mean(err * err, axis=-1, keepdims=True), axis=0, keepdims=True)
        dy = err * (1.0 / D_MODEL)
        dg_ref[...] += jnp.sum(dy * xh, axis=0, keepdims=True)
        dxh = dy * g
        dx = r * (dxh - xh * jnp.mean(dxh * xh, axis=-1, keepdims=True))
        dx_ref[...] = dx
        dxb_ref[...] = dx.astype(BF16)

    tok = pl.BlockSpec((tm, D_MODEL), lambda m: (m, 0))
    vec = pl.BlockSpec((1, D_MODEL), lambda m: (0, 0))
    return pl.pallas_call(
        body, name="loss_final_bwd",
        out_shape=[jax.ShapeDtypeStruct((8, 128), F32), jax.ShapeDtypeStruct((1, D_MODEL), F32),
                   jax.ShapeDtypeStruct((t, D_MODEL), F32), jax.ShapeDtypeStruct((t, D_MODEL), BF16)],
        grid=(t // tm,), in_specs=[tok, tok, vec],
        out_specs=[pl.BlockSpec((8, 128), lambda m: (0, 0)), vec, tok, tok],
        compiler_params=_params([((tm, D_MODEL), F32)] * 4, temps=8 << 20, sem=("arbitrary",)),
    )(x2, target, final_g)


def _ffn_backward(dx2b, dx2, gu, x1, w_gu, w_down, ffn_g):
    t = x1.shape[0]
    tm = _tile(t, 512)

    def body(dxb_ref, dx2_ref, gu_ref, x1_ref, wg_ref, wu_ref, wd_ref, g_ref, dgu_ref, dx1_ref, dx1b_ref, dg_ref, acc):
        m, j = pl.program_id(0), pl.program_id(1)

        @pl.when((m == 0) & (j == 0))
        def _():
            dg_ref[...] = jnp.zeros_like(dg_ref)

        dact = _dot_nt(dxb_ref[...], wd_ref[...])
        gate, up = gu_ref[0], gu_ref[1]
        sg = _sigmoid(gate)
        dgate = (dact * up * (sg * (1.0 + gate * (1.0 - sg)))).astype(BF16)
        dup = (dact * (gate * sg)).astype(BF16)
        dgu_ref[0] = dgate
        dgu_ref[1] = dup
        part = _dot_nt(dgate, wg_ref[...]) + _dot_nt(dup, wu_ref[...])

        @pl.when(j == 0)
        def _():
            acc[...] = part

        @pl.when(j > 0)
        def _():
            acc[...] += part

        @pl.when(j == 3)
        def _():
            dx, dg = _rms_bwd(acc[...], x1_ref[...], g_ref[...])
            dx1 = dx2_ref[...] + dx
            dx1_ref[...] = dx1
            dx1b_ref[...] = dx1.astype(BF16)
            dg_ref[...] += dg

    tok = pl.BlockSpec((tm, D_MODEL), lambda m, j: (m, 0))
    vec = pl.BlockSpec((1, D_MODEL), lambda m, j: (0, 0))
    gu_spec = pl.BlockSpec((None, 2, tm, FF_BLOCK), lambda m, j: (j, 0, m, 0))
    return pl.pallas_call(
        body, name="ffn_bwd",
        out_shape=[jax.ShapeDtypeStruct((4, 2, t, FF_BLOCK), BF16), jax.ShapeDtypeStruct((t, D_MODEL), F32),
                   jax.ShapeDtypeStruct((t, D_MODEL), BF16), jax.ShapeDtypeStruct((1, D_MODEL), F32)],
        grid=(t // tm, 4),
        in_specs=[tok, tok, gu_spec, tok, pl.BlockSpec((None, D_MODEL, FF_BLOCK), lambda m, j: (j, 0, 0)),
                  pl.BlockSpec((None, D_MODEL, FF_BLOCK), lambda m, j: (j + 4, 0, 0)),
                  pl.BlockSpec((FF_BLOCK, D_MODEL), lambda m, j: (j, 0)), vec],
        out_specs=[gu_spec, tok, tok, vec],
        scratch_shapes=[pltpu.VMEM((tm, D_MODEL), F32)],
        compiler_params=_params([((tm, D_MODEL), BF16), ((tm, D_MODEL), F32), ((2, tm, 768), F32), ((tm, D_MODEL), F32),
                                 ((D_MODEL, 768), BF16), ((D_MODEL, 768), BF16), ((FF_BLOCK, D_MODEL), BF16),
                                 ((2, tm, 768), BF16), ((tm, D_MODEL), F32), ((tm, D_MODEL), BF16)],
                                scratch=[((tm, D_MODEL), F32)], temps=8 << 20, sem=("arbitrary", "arbitrary")),
    )(dx2b, dx2, gu, x1, w_gu, w_gu, w_down, ffn_g)


def _branch_out_backward(dx1b, ya, yb, proj, w_a, w_b, w_out):
    t = ya.shape[0]
    tm = _tile(t, 256)

    def body(dx_ref, ya_ref, yb_ref, ga_ref, gb_ref, wa_ref, wb_ref, wo_ref, dya_ref, dyb_ref, dgate_ref, da_ref, dog_ref):
        dm = _dot_nt(dx_ref[...], wo_ref[...])
        sa, sb = _sigmoid(ga_ref[...]), _sigmoid(gb_ref[...])
        dya = (dm * sa).astype(BF16)
        dyb = (dm * sb).astype(BF16)
        dya_ref[...] = dya
        dyb_ref[...] = dyb
        dgate_ref[0] = (dm * ya_ref[...] * (sa * (1.0 - sa))).astype(BF16)
        dgate_ref[1] = (dm * yb_ref[...] * (sb * (1.0 - sb))).astype(BF16)
        da_ref[...] = _dot_nt(dya, wa_ref[...])
        dog_ref[...] = _dot_nt(dyb, wb_ref[...])

    tok = pl.BlockSpec((tm, D_MODEL), lambda m: (m, 0))
    wsp = pl.BlockSpec((D_MODEL, D_MODEL), lambda m: (0, 0))
    return pl.pallas_call(
        body, name="branch_out_bwd",
        out_shape=[jax.ShapeDtypeStruct((t, D_MODEL), BF16), jax.ShapeDtypeStruct((t, D_MODEL), BF16),
                   jax.ShapeDtypeStruct((N_DEV, t, D_MODEL), BF16), jax.ShapeDtypeStruct((t, D_MODEL), F32),
                   jax.ShapeDtypeStruct((t, D_MODEL), F32)],
        grid=(t // tm,),
        in_specs=[tok, tok, tok, pl.BlockSpec((None, tm, D_MODEL), lambda m: (GATE_POS, m, 0)),
                  pl.BlockSpec((None, tm, D_MODEL), lambda m: (GATE_POS + 1, m, 0)), wsp, wsp, wsp],
        out_specs=[tok, tok, pl.BlockSpec((2, tm, D_MODEL), lambda m: (GATE_POS // 2, m, 0)), tok, tok],
        compiler_params=_params([((tm, D_MODEL), BF16)] * 5 + [((tm, D_MODEL), F32)] * 6 + [((D_MODEL, D_MODEL), BF16)] * 3,
                                temps=8 << 20, sem=("arbitrary",)),
    )(dx1b, ya, yb, proj, proj, w_a, w_b, w_out)


def _hgrn_backward(dproj, dog, o_saved, states, proj, lb_table, norm_g):
    t = proj.shape[1]
    tb = _tile(t, 256)
    nc = tb // HGRN_CHUNK
    nb = t // tb

    def body(_, dog_ref, o_ref, st_ref, q_ref, f_ref, i_ref, g_ref, tab_ref, ng_ref, dp_ref, dng_ref, dtab_ref, gstate):
        @pl.when(pl.program_id(1) == 0)
        def _():
            gstate[...] = jnp.zeros_like(gstate)
            dng_ref[...] = jnp.zeros_like(dng_ref)
            dtab_ref[...] = jnp.zeros_like(dtab_ref)

        lb = _lower_bound(tab_ref)
        ng = ng_ref[...]
        lower, upper = _tri_masks()
        gt = _hgrn_gates(q_ref[...], f_ref[...], lb, nc)
        qi, ki, kd, qe = (gt[n].astype(BF16) for n in ("qi", "ki", "kd", "qe"))
        vb = i_ref[...].astype(BF16)
        o, gz, d_og = o_ref[...], g_ref[...], dog_ref[...]
        r, oh = _rms_stats(o)
        sg = _sigmoid(gz)
        d_on = d_og * (gz * sg)
        dgz = d_og * (oh * ng) * (sg * (1.0 + gz * (1.0 - sg)))
        dng_ref[...] += jnp.sum(d_on * oh, axis=0, keepdims=True)
        doh = d_on * ng
        dob = (r * (doh - oh * jnp.mean(doh * oh, axis=-1, keepdims=True))).astype(BF16)
        dv_intra, dqi, dki, dqe, g_upd = [], [], [], [], []
        for c in range(nc):
            rows = _chunk_rows(c)
            p = jnp.where(lower, _dot_nt(qi[rows], ki[rows]), 0.0).astype(BF16)
            dv_intra.append(_dot_tn(p, dob[rows]))
            dp = jnp.where(lower, _dot_nt(dob[rows], vb[rows]), 0.0).astype(BF16)
            dqi.append(_dot(dp, ki[rows]))
            dki.append(_dot_tn(dp, qi[rows]))
            dqe.append(_dot(dob[rows], st_ref[c].astype(BF16)))
            g_upd.append(_dot_tn(dob[rows], qe[rows]))
        g_after = [None] * nc
        g = gstate[...]
        for c in reversed(range(nc)):
            g_after[c] = g
            g = g * gt["decay"][c] + g_upd[c]
        gstate[...] = g
        dkd, dv, da_last = [], [], []
        for c in range(nc):
            rows = _chunk_rows(c)
            gb = g_after[c].astype(BF16)
            dkd.append(_dot(vb[rows], gb))
            dv.append(dv_intra[c] + _dot_nt(kd[rows], gb))
            da_last.append(jnp.sum(g_after[c] * st_ref[c], axis=0, keepdims=True) * gt["decay"][c])
        dqi, dki, dqe, dkd, dv = (jnp.concatenate(z, axis=0) for z in (dqi, dki, dqe, dkd, dv))
        dqs = dqi * gt["e_in"] + dqe * gt["e_all"]
        dk = dki * gt["e_out"] + dkd * gt["e_end"]
        t_in, t_out, t_end = dqi * gt["qi"], dki * gt["ki"], dkd * gt["kd"]
        da = t_in - t_out + dqe * gt["qe"] - t_end
        row = lax.broadcasted_iota(jnp.int32, (HGRN_CHUNK, HEAD_DIM), 0)
        d_mid = t_out - t_in
        pieces = []
        for c in range(nc):
            rows = _chunk_rows(c)
            da_mid = jnp.sum(d_mid[rows], axis=0, keepdims=True)
            da_end = jnp.sum(t_end[rows], axis=0, keepdims=True) + da_last[c]
            da_c = da[rows] + jnp.where(row == HGRN_CHUNK // 2 - 1, da_mid, 0.0) + jnp.where(row == HGRN_CHUNK - 1, da_end, 0.0)
            pieces.append(_mask_mm(upper.astype(BF16), da_c))
        df = jnp.concatenate(pieces, axis=0) / gt["f"] - dk
        s = gt["s"]
        dlb = jnp.sum(df * (1.0 - s), axis=0, keepdims=True)
        dp_ref[0] = (dqs * HGRN_SCALE).astype(BF16)
        dp_ref[1] = (df * (1.0 - lb) * (s * (1.0 - s))).astype(BF16)
        dp_ref[2] = dv.astype(BF16)
        dp_ref[3] = dgz.astype(BF16)
        dt0 = dlb * (lb * (1.0 - lb))
        dtab_ref[0:1, :] += dt0
        dtab_ref[1:2, :] -= dt0

    def blk(p):
        return pl.BlockSpec((None, tb, HEAD_DIM), lambda h, n: (p, nb - 1 - n, h))

    tok = pl.BlockSpec((tb, HEAD_DIM), lambda h, n: (nb - 1 - n, h))
    return pl.pallas_call(
        body, name="hgrn_bwd",
        out_shape=[jax.ShapeDtypeStruct((N_DEV, t, D_MODEL), BF16), jax.ShapeDtypeStruct((1, D_MODEL), F32),
                   jax.ShapeDtypeStruct((2, D_MODEL), F32)],
        grid=(HEADS, nb),
        in_specs=[ANY, tok, tok, pl.BlockSpec((None, nc, HEAD_DIM, HEAD_DIM), lambda h, n: (h, nb - 1 - n, 0, 0)),
                  blk(Q_POS), blk(Q_POS + 1), blk(Q_POS + 2), blk(Q_POS + 3),
                  pl.BlockSpec((2, HEAD_DIM), lambda h, n: (0, h)), pl.BlockSpec((1, HEAD_DIM), lambda h, n: (0, h))],
        out_specs=[pl.BlockSpec((4, tb, HEAD_DIM), lambda h, n: (0, nb - 1 - n, h)),
                   pl.BlockSpec((1, HEAD_DIM), lambda h, n: (0, h)), pl.BlockSpec((2, HEAD_DIM), lambda h, n: (0, h))],
        scratch_shapes=[pltpu.VMEM((HEAD_DIM, HEAD_DIM), F32)],
        input_output_aliases={0: 0},
        compiler_params=_params([((tb, HEAD_DIM), F32)] * 6 + [((nc, HEAD_DIM, HEAD_DIM), F32)] + [((4, tb, HEAD_DIM), BF16)],
                                temps=8 << 20, sem=("arbitrary", "arbitrary")),
    )(dproj, dog, o_saved, states, proj, proj, proj, proj, lb_table, norm_g)


def _gmlp_backward(dproj, da, proj, ln_g, ln_b, w_s, bias_b):
    t = proj.shape[1]
    tm = _tile(t, 256)
    chunks = tm // GMLP_CHUNK

    def body(_, da_ref, u_ref, v_ref, lng_ref, lnb_ref, ws_ref, bias_ref, dp_ref, dlng_ref, dlnb_ref, dws_ref, dbs_ref,
             vn_scr, dvn_scr):
        @pl.when(pl.program_id(0) == 0)
        def _():
            dlng_ref[...] = jnp.zeros_like(dlng_ref)
            dlnb_ref[...] = jnp.zeros_like(dlnb_ref)
            dws_ref[...] = jnp.zeros_like(dws_ref)
            dbs_ref[...] = jnp.zeros_like(dbs_ref)

        v = v_ref[...]
        vv = _gelu(v)
        mu = jnp.mean(vv, axis=-1, keepdims=True)
        cen = vv - mu
        rstd = lax.rsqrt(jnp.mean(cen * cen, axis=-1, keepdims=True) + NORM_EPS)
        vhat = cen * rstd
        lng = lng_ref[...]
        vn_scr[...] = (vhat * lng + lnb_ref[...]).astype(BF16)
        row = lax.broadcasted_iota(jnp.int32, (GMLP_CHUNK, GMLP_CHUNK), 0)
        col = lax.broadcasted_iota(jnp.int32, (GMLP_CHUNK, GMLP_CHUNK), 1)
        for g in range(GROUPS):
            wm = _masked_ws(ws_ref, g)
            cols = slice(g * HEAD_DIM, (g + 1) * HEAD_DIM)
            dws = jnp.zeros((GMLP_CHUNK, GMLP_CHUNK), F32)
            dbs = jnp.zeros((GMLP_CHUNK, GMLP_CHUNK), F32)
            for c in range(chunks):
                rows = slice(c * GMLP_CHUNK, (c + 1) * GMLP_CHUNK)
                vn = vn_scr[rows, cols]
                mixed = _dot(wm, vn) + bias_ref[g]
                u = u_ref[rows, cols]
                d_a = da_ref[rows, cols]
                dp_ref[0, rows, cols] = (d_a * mixed * _gelu_grad(u)).astype(BF16)
                dmix = d_a * _gelu(u)
                dmb = dmix.astype(BF16)
                dbs = dbs + dmix
                dws = dws + _dot_nt(dmb, vn)
                dvn_scr[rows, cols] = _dot_tn(wm, dmb)
            dws_ref[g] += jnp.where(row >= col, dws, 0.0)
            dbs_ref[g] += jnp.broadcast_to(jnp.sum(dbs, axis=-1, keepdims=True), (GMLP_CHUNK, GMLP_CHUNK))
        dvn = dvn_scr[...]
        dlng_ref[...] += jnp.sum(dvn * vhat, axis=0, keepdims=True)
        dlnb_ref[...] += jnp.sum(dvn, axis=0, keepdims=True)
        dvh = dvn * lng
        dvv = rstd * (dvh - jnp.mean(dvh, axis=-1, keepdims=True) - vhat * jnp.mean(dvh * vhat, axis=-1, keepdims=True))
        dp_ref[1] = (dvv * _gelu_grad(v)).astype(BF16)

    tok = pl.BlockSpec((tm, D_MODEL), lambda m: (m, 0))
    small = pl.BlockSpec((GROUPS, GMLP_CHUNK, GMLP_CHUNK), lambda m: (0, 0, 0))
    vec = pl.BlockSpec((1, D_MODEL), lambda m: (0, 0))
    return pl.pallas_call(
        body, name="gmlp_bwd",
        out_shape=[jax.ShapeDtypeStruct(dproj.shape, BF16), jax.ShapeDtypeStruct((1, D_MODEL), F32),
                   jax.ShapeDtypeStruct((1, D_MODEL), F32), jax.ShapeDtypeStruct((GROUPS, GMLP_CHUNK, GMLP_CHUNK), F32),
                   jax.ShapeDtypeStruct((GROUPS, GMLP_CHUNK, GMLP_CHUNK), F32)],
        grid=(t // tm,),
        in_specs=[ANY, tok, pl.BlockSpec((None, tm, D_MODEL), lambda m: (U_POS, m, 0)),
                  pl.BlockSpec((None, tm, D_MODEL), lambda m: (U_POS + 1, m, 0)), vec, vec, small, small],
        out_specs=[pl.BlockSpec((2, tm, D_MODEL), lambda m: (U_POS // 2, m, 0)), vec, vec, small, small],
        scratch_shapes=[pltpu.VMEM((tm, D_MODEL), BF16), pltpu.VMEM((tm, D_MODEL), F32)],
        input_output_aliases={0: 0},
        compiler_params=_params([((tm, D_MODEL), F32)] * 3 + [((2, tm, D_MODEL), BF16)] + [((8, 128, 128), F32)] * 4,
                                scratch=[((tm, D_MODEL), BF16), ((tm, D_MODEL), F32)], temps=12 << 20, sem=("arbitrary",)),
    )(dproj, da, proj, proj, ln_g, ln_b, w_s, bias_b)


def _input_backward(dproj, w_in_g, x, dx1, mix_g):
    t = x.shape[0]
    tm = _tile(t, 512)

    def body(dp_ref, w_ref, x_ref, dx1_ref, g_ref, dx_ref, dg_ref, acc):
        m, p = pl.program_id(0), pl.program_id(1)

        @pl.when((m == 0) & (p == 0))
        def _():
            dg_ref[...] = jnp.zeros_like(dg_ref)

        part = _dot_nt(dp_ref[...], w_ref[...])

        @pl.when(p == 0)
        def _():
            acc[...] = part

        @pl.when(p > 0)
        def _():
            acc[...] += part

        @pl.when(p == N_DEV - 1)
        def _():
            dx, dg = _rms_bwd(acc[...], x_ref[...], g_ref[...])
            dx_ref[...] = dx1_ref[...] + dx
            dg_ref[...] += dg

    tok = pl.BlockSpec((tm, D_MODEL), lambda m, p: (m, 0))
    vec = pl.BlockSpec((1, D_MODEL), lambda m, p: (0, 0))
    return pl.pallas_call(
        body, name="input_bwd",
        out_shape=[jax.ShapeDtypeStruct((t, D_MODEL), F32), jax.ShapeDtypeStruct((1, D_MODEL), F32)],
        grid=(t // tm, N_DEV),
        in_specs=[pl.BlockSpec((None, tm, D_MODEL), lambda m, p: (p, m, 0)),
                  pl.BlockSpec((None, D_MODEL, D_MODEL), lambda m, p: (p, 0, 0)), tok, tok, vec],
        out_specs=[tok, vec],
        scratch_shapes=[pltpu.VMEM((tm, D_MODEL), F32)],
        compiler_params=_params([((tm, D_MODEL), BF16), ((D_MODEL, D_MODEL), BF16)] + [((tm, D_MODEL), F32)] * 3,
                                scratch=[((tm, D_MODEL), F32)], temps=8 << 20, sem=("arbitrary", "arbitrary")),
    )(dproj, w_in_g, x, dx1, mix_g)


def _weight_grad(name, a, b, a_spec, b_spec, out_shape, out_spec, grid, blocks):
    def body(a_ref, b_ref, o_ref):
        part = _dot_tn(a_ref[...], b_ref[...])
        m = pl.program_id(len(grid) - 1)

        @pl.when(m == 0)
        def _():
            o_ref[...] = part

        @pl.when(m > 0)
        def _():
            o_ref[...] += part

    return pl.pallas_call(
        body, name=name, out_shape=jax.ShapeDtypeStruct(out_shape, F32), grid=grid, in_specs=[a_spec, b_spec],
        out_specs=out_spec, compiler_params=_params(blocks, temps=8 << 20, sem=("arbitrary",) * len(grid)),
    )(a, b)


def _pack_small(mix_g, ln_g, ln_b, w_s, b_s, lb_table, hg_norm, ffn_g, final_g):
    def part(a):
        a = a.reshape(-1, D_MODEL)
        return jnp.pad(a, ((0, 8 - a.shape[0]), (0, 0)))

    return jnp.concatenate([part(mix_g), part(ln_g), part(ln_b), part(hg_norm), part(ffn_g), part(final_g),
                            part(lb_table), part(b_s), w_s.reshape(GMLP_CHUNK, D_MODEL)], axis=0)


def _unpack_small(pack):
    return dict(norm_mix_g=pack[0:1], gmlp_ln_g=pack[8:9], gmlp_ln_b=pack[16:17], hgrn_norm_g=pack[24:25],
                norm_ffn_g=pack[32:33], norm_final_g=pack[40], hgrn_lb_table=pack[48:50],
                gmlp_b_s=pack[56:57].reshape(1, GROUPS, GMLP_CHUNK),
                gmlp_w_s=pack[64:192].reshape(1, GROUPS, GMLP_CHUNK, GMLP_CHUNK))


def _adamw_small(name, gathered, w, m, v):
    rows = w.shape[0]

    def body(p_ref, w_ref, m_ref, v_ref, g_out, d_out, m_out, v_out):
        g = p_ref[0]
        for j in range(1, N_DEV):
            g = g + p_ref[j]
        delta, m_new, v_new = _adamw_math(w_ref[...], g, m_ref[...], v_ref[...])
        g_out[...] = g
        d_out[...] = delta
        m_out[...] = m_new
        v_out[...] = v_new

    tr = _tile(rows, 64)
    spec = pl.BlockSpec((tr, D_MODEL), lambda r: (r, 0))
    return pl.pallas_call(
        body, name=name, out_shape=[jax.ShapeDtypeStruct((rows, D_MODEL), F32)] * 4, grid=(rows // tr,),
        in_specs=[pl.BlockSpec((N_DEV, tr, D_MODEL), lambda r: (0, r, 0)), spec, spec, spec], out_specs=[spec] * 4,
        compiler_params=_params([((N_DEV, tr, D_MODEL), F32)] + [((tr, D_MODEL), F32)] * 7, sem=("arbitrary",)),
    )(gathered, w, m, v)


def kernel(x, norm_mix_g, w_in, gmlp_ln_g, gmlp_ln_b, gmlp_w_s, gmlp_b_s, hgrn_lb_table, hgrn_norm_g, w_branch_a, w_branch_b, w_out, norm_ffn_g, w_gate_up, w_down, norm_final_g, loss_target, m_norm_mix_g, m_w_in, m_gmlp_ln_g, m_gmlp_ln_b, m_gmlp_w_s, m_gmlp_b_s, m_hgrn_lb_table, m_hgrn_norm_g, m_w_branch_a, m_w_branch_b, m_w_out, m_norm_ffn_g, m_w_gate_up, m_w_down, m_norm_final_g, v_norm_mix_g, v_w_in, v_gmlp_ln_g, v_gmlp_ln_b, v_gmlp_w_s, v_gmlp_b_s, v_hgrn_lb_table, v_hgrn_norm_g, v_w_branch_a, v_w_branch_b, v_w_out, v_norm_ffn_g, v_w_gate_up, v_w_down, v_norm_final_g):
    t = x.shape[1]
    x2d = x.reshape(t, D_MODEL)
    target = loss_target.reshape(t, D_MODEL)
    final_g = norm_final_g.reshape(1, D_MODEL)

    shards = [w_in[0].astype(BF16), w_branch_a[0].astype(BF16), w_branch_b[0].astype(BF16), w_out[0].astype(BF16),
              w_gate_up[0].astype(BF16), w_down[0].astype(BF16)]

    def rows_of(n):
        return lambda ref, j: ref.at[pl.ds(pl.multiple_of(j * n, 8), n)]

    gathered = [((N_DEV, D_MODEL, D_MODEL), BF16), ((D_MODEL, D_MODEL), BF16), ((D_MODEL, D_MODEL), BF16),
                ((D_MODEL, D_MODEL), BF16), ((N_DEV, D_MODEL, FF_BLOCK), BF16), ((D_FF, D_MODEL), BF16)]
    places = [lambda ref, j: ref.at[_pos_of_dev(j)], rows_of(BRANCH_ROWS), rows_of(BRANCH_ROWS), rows_of(BRANCH_ROWS),
              lambda ref, j: ref.at[j], rows_of(DOWN_ROWS)]
    (w_in_g,) = _all_gather("w_in_all_gather", shards[:1], gathered[:1], places[:1])
    _, later = lax.optimization_barrier((w_in_g, shards[1:]))
    w_a, w_b, w_o, w_gu, w_dn = _all_gather_async("weights_all_gather", 0, later, gathered[1:], places[1:])

    h = _rms_forward(x2d, norm_mix_g)
    proj = _proj_forward(h, w_in_g)
    bias_b = jnp.broadcast_to(gmlp_b_s[0][:, :, None], (GROUPS, GMLP_CHUNK, GMLP_CHUNK))
    a = _gmlp_forward(proj, gmlp_ln_g, gmlp_ln_b, gmlp_w_s[0], bias_b)
    og, o_saved, states = _hgrn_forward(proj, hgrn_lb_table, hgrn_norm_g)
    ya, yb, merged, x1, h2 = _branch_out_forward(a, og, proj, x2d, w_a, w_b, w_o, norm_ffn_g)
    gu, act, x2 = _ffn_forward(h2, x1, w_gu, w_dn)
    loss_tile, d_final_g, dx2, dx2b = _loss_and_final_backward(x2, target, final_g)

    core = lax.axis_index("c").astype(jnp.int32).reshape(1)
    chip = (2 * lax.axis_index("x") + lax.axis_index("y")).astype(jnp.int32).reshape(1)
    branch_rows, branch_shape = rows_of(BRANCH_ROWS), (BRANCH_ROWS, D_MODEL)
    branch_block = ((BRANCH_ROWS, D_MODEL), lambda q, r, c: (2 * q + c, 0))

    def chip_partials(names, grads, land, own_blocks):
        return [_chip_partial("chip_partial_" + nme, core, g_, blk, idx, l_)
                for nme, g_, (blk, idx), l_ in zip(names, grads, own_blocks, land)]

    tm = _tile(t, 512)
    nm = t // tm
    tok_a = pl.BlockSpec((tm, D_MODEL), lambda m: (m, 0))
    full_o = pl.BlockSpec((D_MODEL, D_MODEL), lambda m: (0, 0))
    sq_blocks = [((tm, D_MODEL), BF16)] * 2 + [((D_MODEL, D_MODEL), F32)]

    dgu, dx1, dx1b, d_ffn_g = _ffn_backward(dx2b, dx2, gu, x1, w_gu, w_dn, norm_ffn_g)
    g_gu = _weight_grad(
        "grad_w_gate_up", h2, dgu, pl.BlockSpec((tm, D_MODEL), lambda j, m: (m, 0)),
        pl.BlockSpec((None, None, tm, FF_BLOCK), lambda j, m: (j % 4, j // 4, m, 0)), (N_DEV, D_MODEL, FF_BLOCK),
        pl.BlockSpec((None, D_MODEL, FF_BLOCK), lambda j, m: (j, 0, 0)), (N_DEV, nm),
        [((tm, D_MODEL), BF16), ((tm, 768), BF16), ((D_MODEL, 768), F32)])
    g_dn = _weight_grad(
        "grad_w_down", act, dx2b, pl.BlockSpec((None, tm, FF_BLOCK), lambda j, m: (j, m, 0)),
        pl.BlockSpec((tm, D_MODEL), lambda j, m: (m, 0)), (D_FF, D_MODEL),
        pl.BlockSpec((FF_BLOCK, D_MODEL), lambda j, m: (j, 0)), (4, nm),
        [((tm, 768), BF16), ((tm, D_MODEL), BF16), ((FF_BLOCK, D_MODEL), F32)])
    names_f, grads_f = ["w_gate_up", "w_down"], [g_gu, g_dn]
    land_f = _exchange_sibling("ffn_grads_to_sibling", 2, grads_f, [lambda ref, j: ref.at[j], rows_of(DOWN_ROWS)],
                               [(D_MODEL, FF_BLOCK), (DOWN_ROWS, D_MODEL)])

    dya, dyb, dproj, da, dog = _branch_out_backward(dx1b, ya, yb, proj, w_a, w_b, w_o)
    g_a = _weight_grad("grad_w_a", a, dya, tok_a, tok_a, (D_MODEL, D_MODEL), full_o, (nm,), sq_blocks)
    g_b = _weight_grad("grad_w_b", og, dyb, tok_a, tok_a, (D_MODEL, D_MODEL), full_o, (nm,), sq_blocks)
    g_o = _weight_grad("grad_w_out", merged, dx1b, tok_a, tok_a, (D_MODEL, D_MODEL), full_o, (nm,), sq_blocks)
    names_b, grads_b = ["w_branch_a", "w_branch_b", "w_out"], [g_a, g_b, g_o]
    land_b = _exchange_sibling("branch_grads_to_sibling", 3, grads_b, [branch_rows] * 3, [branch_shape] * 3)

    part_f = chip_partials(names_f, grads_f, land_f,
                           [((None, 256, FF_BLOCK), lambda q, r, c: (2 * q + c, r, 0)),
                            ((DOWN_ROWS // 2, D_MODEL), lambda q, r, c: (2 * (2 * q + c) + r, 0))])
    landed_f = _exchange_chips("ffn_grads_to_chips", 5, part_f)

    dog, _ = lax.optimization_barrier((dog, part_f))
    dproj, d_hg_norm, d_lb = _hgrn_backward(dproj, dog, o_saved, states, proj, hgrn_lb_table, hgrn_norm_g)

    part_b = chip_partials(names_b, grads_b, land_b, [branch_block] * 3)
    landed_b = _exchange_chips("branch_grads_to_chips", 6, part_b)

    da, _ = lax.optimization_barrier((da, part_b))
    dproj, d_ln_g, d_ln_b, d_ws, d_bs = _gmlp_backward(dproj, da, proj, gmlp_ln_g, gmlp_ln_b, gmlp_w_s[0], bias_b)

    def packed(vals):
        return _pack_small(*vals)

    w_pack = packed([norm_mix_g, gmlp_ln_g, gmlp_ln_b, gmlp_w_s, gmlp_b_s, hgrn_lb_table, hgrn_norm_g, norm_ffn_g, norm_final_g])
    m_pack = packed([m_norm_mix_g, m_gmlp_ln_g, m_gmlp_ln_b, m_gmlp_w_s, m_gmlp_b_s, m_hgrn_lb_table, m_hgrn_norm_g, m_norm_ffn_g, m_norm_final_g])
    v_pack = packed([v_norm_mix_g, v_gmlp_ln_g, v_gmlp_ln_b, v_gmlp_w_s, v_gmlp_b_s, v_hgrn_lb_table, v_hgrn_norm_g, v_norm_ffn_g, v_norm_final_g])
    small_partial = _pack_small(jnp.zeros((1, D_MODEL), F32), d_ln_g, d_ln_b, d_ws, d_bs[:, :, 0], d_lb, d_hg_norm, d_ffn_g,
                                d_final_g)
    (small_all,) = _all_gather_async("small_grads_all_gather", 1, [small_partial],
                                     [((N_DEV, SMALL_ROWS, D_MODEL), F32)], [lambda ref, j: ref.at[j]])

    g_in = _weight_grad(
        "grad_w_in", h, dproj, pl.BlockSpec((tm, D_MODEL), lambda p, m: (m, 0)),
        pl.BlockSpec((None, tm, D_MODEL), lambda p, m: (p, m, 0)), (N_DEV, D_MODEL, D_MODEL),
        pl.BlockSpec((None, D_MODEL, D_MODEL), lambda p, m: (p, 0, 0)), (N_DEV, nm), sq_blocks)
    land_i = _exchange_sibling("w_in_grads_to_sibling", 4, [g_in], [lambda ref, j: ref.at[_pos_of_dev(j)]],
                               [(D_MODEL, D_MODEL)])

    big = {}
    for nme, own, lnd, w, m, v in zip(
            names_f + names_b, part_f + part_b, landed_f + landed_b,
            [w_gate_up, w_down, w_branch_a, w_branch_b, w_out], [m_w_gate_up, m_w_down, m_w_branch_a, m_w_branch_b, m_w_out],
            [v_w_gate_up, v_w_down, v_w_branch_a, v_w_branch_b, v_w_out]):
        big[nme] = [o_[None] for o_ in _adamw("adamw_" + nme, chip, own, lnd, w[0], m[0], v[0])]
    small_outs = _adamw_small("adamw_small", small_all, w_pack, m_pack, v_pack)
    land_i, _ = lax.optimization_barrier((land_i, (big, small_outs)))
    part_i = chip_partials(["w_in"], [g_in], land_i,
                           [((None, 256, D_MODEL), lambda q, r, c: (_pos_of_dev(2 * q + c), r, 0))])
    landed_i = _exchange_chips("w_in_grads_to_chips", 7, part_i)

    dx1, _ = lax.optimization_barrier((dx1, part_i))
    grad_x, d_mix_g = _input_backward(dproj, w_in_g, x2d, dx1, norm_mix_g)
    big["w_in"] = [o_[None] for o_ in _adamw("adamw_w_in", chip, part_i[0], landed_i[0], w_in[0], m_w_in[0], v_w_in[0])]

    def row8(a):
        return jnp.pad(a, ((0, 7), (0, 0)))

    d_mix_g, _ = lax.optimization_barrier((d_mix_g, landed_i))
    (mix_all,) = _all_gather_async("mix_gain_grad_all_gather", 8, [row8(d_mix_g)], [((N_DEV, 8, D_MODEL), F32)],
                                   [lambda ref, j: ref.at[j]])
    mix_outs = _adamw_small("adamw_mix_gain", mix_all, row8(norm_mix_g), row8(m_norm_mix_g), row8(v_norm_mix_g))
    small = [dict(_unpack_small(p), norm_mix_g=q[0:1]) for p, q in zip(small_outs, mix_outs)]

    loss = lax.psum(loss_tile[0, 0], ("x", "y", "c"))
    order = ["norm_mix_g", "w_in", "gmlp_ln_g", "gmlp_ln_b", "gmlp_w_s", "gmlp_b_s", "hgrn_lb_table", "hgrn_norm_g",
             "w_branch_a", "w_branch_b", "w_out", "norm_ffn_g", "w_gate_up", "w_down", "norm_final_g"]
    outs = [loss, grad_x.reshape(1, t, D_MODEL)]
    for kind in range(4):
        for nme in order:
            outs.append(big[nme][kind] if nme in big else small[kind][nme])
    return tuple(outs)
```

```python
import functools

import jax
import jax.numpy as jnp
from jax import lax
from jax.experimental import pallas as pl
from jax.experimental.pallas import tpu as pltpu
from jax.experimental.pallas import tpu_sc as plsc

F32, BF16 = jnp.float32, jnp.bfloat16
D_MODEL = 1024
N_DEV = 8
HEADS = 8
HEAD_DIM = 128
GROUPS = 8
GMLP_CHUNK = 128
HGRN_CHUNK = 64
HGRN_SCALE = HEAD_DIM ** -0.5
D_FF = 2816
FF_BLOCK = D_FF // 4
DOWN_ROWS = D_FF // N_DEV
BRANCH_ROWS = D_MODEL // N_DEV
NORM_EPS = 1e-6
ADAM_LR, ADAM_B1, ADAM_B2, ADAM_EPS, ADAM_WD, ADAM_STEP = 0.001, 0.9, 0.999, 1e-08, 0.01, 10
SMALL_ROWS = 192
V7X_VMEM_BYTES = 64 * 1024 * 1024
VMEM_CAP = V7X_VMEM_BYTES - 6 * 1024 * 1024
MESH_ID = pl.DeviceIdType.MESH
ANY = pl.BlockSpec(memory_space=pl.ANY)
Q_POS, U_POS, GATE_POS = 0, 4, 6


def _pos_of_dev(j):
    return jnp.where(j < 2, j + 4, jnp.where(j < 6, j - 2, j))


def _dev_of_pos(p):
    return jnp.where(p < 4, p + 2, jnp.where(p < 6, p - 4, p))


def _nbytes(shape, dtype):
    n = 1
    for s in shape:
        n *= s
    return n * jnp.dtype(dtype).itemsize


def _params(blocks, scratch=(), temps=0, sem=None):
    need = 2 * sum(_nbytes(s, d) for s, d in blocks) + sum(_nbytes(s, d) for s, d in scratch) + temps
    assert need + (4 << 20) <= VMEM_CAP, need
    return pltpu.CompilerParams(dimension_semantics=sem, vmem_limit_bytes=VMEM_CAP)


def _tile(n, pref):
    return pref if n % pref == 0 else n


def _dot(a, b):
    return jnp.dot(a, b, preferred_element_type=F32)


def _dot_nt(a, b):
    return lax.dot_general(a, b, (((1,), (1,)), ((), ())), preferred_element_type=F32)


def _dot_tn(a, b):
    return lax.dot_general(a, b, (((0,), (0,)), ((), ())), preferred_element_type=F32)


def _sigmoid(x):
    return 1.0 / (1.0 + jnp.exp(-x))


_GELU_C = 0.7978845608028654


def _gelu(x):
    return x * (0.5 * (1.0 + jnp.tanh(_GELU_C * (x + 0.044715 * (x * x * x)))))


def _gelu_grad(x):
    t = jnp.tanh(_GELU_C * (x + 0.044715 * (x * x * x)))
    return 0.5 * (1.0 + t) + 0.5 * x * (1.0 - t * t) * (_GELU_C * (1.0 + 3.0 * 0.044715 * x * x))


def _rms_stats(x):
    r = lax.rsqrt(jnp.mean(x * x, axis=-1, keepdims=True) + NORM_EPS)
    return r, x * r


def _rms_bwd(dy, x, g):
    r, xh = _rms_stats(x)
    dg = jnp.sum(dy * xh, axis=0, keepdims=True)
    dxh = dy * g
    dx = r * (dxh - xh * jnp.mean(dxh * xh, axis=-1, keepdims=True))
    return dx, dg


def _split3(x):
    hi = x.astype(BF16)
    r = x - hi.astype(F32)
    mid = r.astype(BF16)
    lo = (r - mid.astype(F32)).astype(BF16)
    return hi, mid, lo


def _mask_mm(mask_bf16, x):
    hi, mid, lo = _split3(x)
    return _dot(mask_bf16, hi) + _dot(mask_bf16, mid) + _dot(mask_bf16, lo)


def _place():
    return lax.axis_index("x"), lax.axis_index("y"), lax.axis_index("c")


def _gather_copies(src, out, send, recv, loc, slicers):
    n = len(src)
    x, y, c = _place()
    me, sib = (x, y, c), (x, y, 1 - c)
    chips = [(1 - x, y), (x, 1 - y), (1 - x, 1 - y)]

    def dev(p):
        return 4 * p[0] + 2 * p[1] + p[2]

    def rc(i, k, block, to, from_src=False):
        dst = slicers[i](out[i], dev(block))
        return pltpu.make_async_remote_copy(
            src_ref=src[i] if from_src else dst, dst_ref=dst, send_sem=send.at[7 * i + k],
            recv_sem=recv.at[7 * i + k], device_id=to, device_id_type=MESH_ID)

    mine = [pltpu.make_async_copy(src[i], slicers[i](out[i], dev(me)), loc.at[i]) for i in range(n)]
    for cp in mine:
        cp.start()
    first = []
    for i in range(n):
        first.append(rc(i, 0, me, sib, True))
        for j, chip in enumerate(chips):
            first.append(rc(i, 1 + j, me, (*chip, c), True))
    for cp in first:
        cp.start()
    passed = []
    for j, chip in enumerate(chips):
        for i in range(n):
            rc(i, 1 + j, (*chip, c), me).wait_recv()
            cp = rc(i, 4 + j, (*chip, c), sib)
            cp.start()
            passed.append(cp)
    for i in range(n):
        rc(i, 0, sib, me).wait_recv()
        for j, chip in enumerate(chips):
            rc(i, 4 + j, (*chip, 1 - c), me).wait_recv()
    for cp in first + passed:
        cp.wait_send()
    for cp in mine:
        cp.wait()


def _gather_scratch(n):
    return [pltpu.SemaphoreType.DMA((7 * n,)), pltpu.SemaphoreType.DMA((7 * n,)), pltpu.SemaphoreType.DMA((n,))]


def _all_gather(name, srcs, out_shapes, slicers):
    n = len(srcs)

    def body(*refs):
        _gather_copies(refs[:n], refs[n:2 * n], *refs[2 * n:], slicers)

    return pl.pallas_call(
        body, name=name, out_shape=[jax.ShapeDtypeStruct(s, d) for s, d in out_shapes],
        in_specs=[ANY] * n, out_specs=[ANY] * n, scratch_shapes=_gather_scratch(n),
    )(*srcs)


def _handshake(peers):
    barrier = pltpu.get_barrier_semaphore()
    for peer in peers:
        pl.semaphore_signal(barrier, inc=1, device_id=peer, device_id_type=MESH_ID)
    pl.semaphore_wait(barrier, len(peers))


def _all_gather_async(name, collective_id, srcs, out_shapes, slicers):
    n = len(srcs)

    def body(*refs):
        x, y, c = _place()
        _handshake([(1 - x if dx else x, 1 - y if dy else y, 1 - c if dc else c)
                    for dx in (0, 1) for dy in (0, 1) for dc in (0, 1) if dx or dy or dc])
        _gather_copies(refs[:n], refs[n:2 * n], *refs[2 * n:], slicers)

    return _sequencer_call(name, collective_id, body, srcs, [jax.ShapeDtypeStruct(s, d) for s, d in out_shapes],
                           _gather_scratch(n))


def _sequencer_call(name, collective_id, body, operands, out_types, scratch):
    return pl.kernel(
        body, out_type=out_types, mesh=plsc.ScalarSubcoreMesh(axis_name="sequencer", num_cores=1), name=name,
        scratch_types=scratch, compiler_params=pltpu.CompilerParams(collective_id=collective_id),
    )(*operands)


def _exchange_sibling(name, collective_id, grads, shard_fns, shard_shapes):
    n = len(grads)

    def body(*refs):
        g, land = refs[:n], refs[n:2 * n]
        send, recv = refs[2 * n:]
        x, y, c = _place()
        _handshake([(x, y, 1 - c)])
        remote = []
        for i in range(n):
            for q in range(4):
                cp = pltpu.make_async_remote_copy(
                    src_ref=shard_fns[i](g[i], 2 * q + (1 - c)), dst_ref=land[i].at[q], send_sem=send.at[4 * i + q],
                    recv_sem=recv.at[4 * i + q], device_id=(x, y, 1 - c), device_id_type=MESH_ID)
                cp.start()
                remote.append(cp)
        for cp in remote:
            cp.wait()

    return _sequencer_call(name, collective_id, body, grads, [jax.ShapeDtypeStruct((4, *s), F32) for s in shard_shapes],
                           [pltpu.SemaphoreType.DMA((4 * n,)), pltpu.SemaphoreType.DMA((4 * n,))])


def _exchange_chips(name, collective_id, parts):
    n = len(parts)

    def body(*refs):
        part, out = refs[:n], refs[n:2 * n]
        send, recv = refs[2 * n:]
        x, y, c = _place()
        _handshake([(1 - x, y, c), (x, 1 - y, c), (1 - x, 1 - y, c)])
        remote = []
        for i in range(n):
            for s in range(3):
                qx = 1 - x if (s + 1) // 2 else x
                qy = 1 - y if (s + 1) % 2 else y
                cp = pltpu.make_async_remote_copy(
                    src_ref=part[i].at[2 * qx + qy], dst_ref=out[i].at[s], send_sem=send.at[3 * i + s],
                    recv_sem=recv.at[3 * i + s], device_id=(qx, qy, c), device_id_type=MESH_ID)
                cp.start()
                remote.append(cp)
        for cp in remote:
            cp.wait()

    return _sequencer_call(name, collective_id, body, parts,
                           [jax.ShapeDtypeStruct((3, *p.shape[1:]), p.dtype) for p in parts],
                           [pltpu.SemaphoreType.DMA((3 * n,)), pltpu.SemaphoreType.DMA((3 * n,))])


def _chip_partial(name, core, grad, own_block, own_index, land):
    _, rows, cols = land.shape
    tr = own_block[-2]

    def body(core_ref, a_ref, b_ref, o_ref):
        o_ref[...] = (a_ref[...] + b_ref[...]).astype(BF16)

    spec = pl.BlockSpec((None, tr, cols), lambda q, r, c: (q, r, 0))
    return pl.pallas_call(
        body, name=name, out_shape=jax.ShapeDtypeStruct(land.shape, BF16),
        grid_spec=pltpu.PrefetchScalarGridSpec(
            num_scalar_prefetch=1, grid=(4, rows // tr),
            in_specs=[pl.BlockSpec(own_block, lambda q, r, c: own_index(q, r, c[0])), spec], out_specs=spec),
        compiler_params=_params([((tr, cols), F32)] * 2 + [((tr, cols), BF16)], sem=("arbitrary", "arbitrary")),
    )(core, grad, land)


def _adamw_math(w, g, m, v):
    m = ADAM_B1 * m + (1.0 - ADAM_B1) * g
    v = ADAM_B2 * v + (1.0 - ADAM_B2) * (g * g)
    m_hat = m / (1.0 - ADAM_B1 ** ADAM_STEP)
    v_hat = v / (1.0 - ADAM_B2 ** ADAM_STEP)
    delta = -ADAM_LR * (m_hat / (jnp.sqrt(v_hat) + ADAM_EPS) + ADAM_WD * w)
    return delta, m, v


def _adamw(name, chip, own, landed, w, m, v):
    _, rows, cols = own.shape
    tr = _tile(rows, 256) if rows % 256 == 0 else _tile(rows, 176)

    def body(chip_ref, own_ref, l_ref, w_ref, m_ref, v_ref, g_out, d_out, m_out, v_out):
        g = own_ref[...].astype(F32)
        for s in range(3):
            g = g + l_ref[s].astype(F32)
        delta, m_new, v_new = _adamw_math(w_ref[...], g, m_ref[...], v_ref[...])
        g_out[...] = g
        d_out[...] = delta
        m_out[...] = m_new
        v_out[...] = v_new

    spec = pl.BlockSpec((tr, cols), lambda r, c: (r, 0))
    return pl.pallas_call(
        body, name=name, out_shape=[jax.ShapeDtypeStruct((rows, cols), F32)] * 4,
        grid_spec=pltpu.PrefetchScalarGridSpec(
            num_scalar_prefetch=1, grid=(rows // tr,),
            in_specs=[pl.BlockSpec((None, tr, cols), lambda r, c: (c[0], r, 0)),
                      pl.BlockSpec((3, tr, cols), lambda r, c: (0, r, 0)), spec, spec, spec],
            out_specs=[spec] * 4),
        compiler_params=_params([((4, tr, cols), own.dtype)] + [((tr, cols), F32)] * 7, sem=("arbitrary",)),
    )(chip, own, landed, w, m, v)


def _rms_forward(x, gain):
    t = x.shape[0]
    tm = _tile(t, 512)

    def body(x_ref, g_ref, h_ref, ht_ref):
        _, xh = _rms_stats(x_ref[...])
        h = (xh * g_ref[...]).astype(BF16)
        h_ref[...] = h
        ht_ref[...] = h.T

    return pl.pallas_call(
        body, name="rms_mix_fwd",
        out_shape=[jax.ShapeDtypeStruct((t, D_MODEL), BF16), jax.ShapeDtypeStruct((D_MODEL, t), BF16)], grid=(t // tm,),
        in_specs=[pl.BlockSpec((tm, D_MODEL), lambda m: (m, 0)), pl.BlockSpec((1, D_MODEL), lambda m: (0, 0))],
        out_specs=[pl.BlockSpec((tm, D_MODEL), lambda m: (m, 0)), pl.BlockSpec((D_MODEL, tm), lambda m: (0, m))],
        compiler_params=_params([((tm, D_MODEL), F32), ((tm, D_MODEL), BF16), ((tm, D_MODEL), BF16)], temps=8 << 20,
                                sem=("arbitrary",)),
    )(x, gain)


def _proj_forward(h, w_in_g):
    t = h.shape[0]
    tm = _tile(t, 1024)

    def body(h_ref, w_ref, o_ref):
        o_ref[...] = _dot(h_ref[...], w_ref[...])

    return pl.pallas_call(
        body, name="proj_fwd", out_shape=jax.ShapeDtypeStruct((N_DEV, t, D_MODEL), F32), grid=(N_DEV, t // tm),
        in_specs=[pl.BlockSpec((tm, D_MODEL), lambda p, m: (m, 0)),
                  pl.BlockSpec((None, D_MODEL, D_MODEL), lambda p, m: (p, 0, 0))],
        out_specs=pl.BlockSpec((None, tm, D_MODEL), lambda p, m: (p, m, 0)),
        compiler_params=_params([((tm, D_MODEL), BF16), ((D_MODEL, D_MODEL), BF16), ((tm, D_MODEL), F32)],
                                sem=("arbitrary", "arbitrary")),
    )(h, w_in_g)


def _masked_ws(ws_ref, g):
    row = lax.broadcasted_iota(jnp.int32, (GMLP_CHUNK, GMLP_CHUNK), 0)
    col = lax.broadcasted_iota(jnp.int32, (GMLP_CHUNK, GMLP_CHUNK), 1)
    return jnp.where(row >= col, ws_ref[g], 0.0).astype(BF16)


def _gmlp_forward(proj, ln_g, ln_b, w_s, bias_b):
    t = proj.shape[1]
    tm = _tile(t, 256)
    chunks = tm // GMLP_CHUNK

    def body(u_ref, v_ref, lng_ref, lnb_ref, ws_ref, bias_ref, a_ref, vn_scr):
        vv = _gelu(v_ref[...])
        mu = jnp.mean(vv, axis=-1, keepdims=True)
        cen = vv - mu
        var = jnp.mean(cen * cen, axis=-1, keepdims=True)
        vn_scr[...] = ((cen * lax.rsqrt(var + NORM_EPS)) * lng_ref[...] + lnb_ref[...]).astype(BF16)
        for g in range(GROUPS):
            wm = _masked_ws(ws_ref, g)
            cols = slice(g * HEAD_DIM, (g + 1) * HEAD_DIM)
            for c in range(chunks):
                rows = slice(c * GMLP_CHUNK, (c + 1) * GMLP_CHUNK)
                mixed = _dot(wm, vn_scr[rows, cols]) + bias_ref[g]
                a_ref[rows, cols] = (_gelu(u_ref[rows, cols]) * mixed).astype(BF16)

    small = pl.BlockSpec((GROUPS, GMLP_CHUNK, GMLP_CHUNK), lambda m: (0, 0, 0))
    vec = pl.BlockSpec((1, D_MODEL), lambda m: (0, 0))
    return pl.pallas_call(
        body, name="gmlp_fwd", out_shape=jax.ShapeDtypeStruct((t, D_MODEL), BF16), grid=(t // tm,),
        in_specs=[pl.BlockSpec((None, tm, D_MODEL), lambda m: (U_POS, m, 0)),
                  pl.BlockSpec((None, tm, D_MODEL), lambda m: (U_POS + 1, m, 0)), vec, vec, small, small],
        out_specs=pl.BlockSpec((tm, D_MODEL), lambda m: (m, 0)),
        scratch_shapes=[pltpu.VMEM((tm, D_MODEL), BF16)],
        compiler_params=_params([((tm, D_MODEL), F32)] * 2 + [((tm, D_MODEL), BF16)] + [((8, 128, 128), F32)] * 2,
                                scratch=[((tm, D_MODEL), BF16)], temps=8 << 20, sem=("arbitrary",)),
    )(proj, proj, ln_g, ln_b, w_s, bias_b)


def _lower_bound(tab_ref):
    t0, t1 = tab_ref[0:1, :], tab_ref[1:2, :]
    mx = jnp.maximum(t0, t1)
    e0, e1 = jnp.exp(t0 - mx), jnp.exp(t1 - mx)
    return e0 / (e0 + e1)


def _tri_masks():
    row = lax.broadcasted_iota(jnp.int32, (HGRN_CHUNK, HGRN_CHUNK), 0)
    col = lax.broadcasted_iota(jnp.int32, (HGRN_CHUNK, HGRN_CHUNK), 1)
    return row >= col, row <= col


def _chunk_rows(c):
    return slice(c * HGRN_CHUNK, (c + 1) * HGRN_CHUNK)


def _per_chunk(x, nc, fn):
    return jnp.concatenate([fn(x[_chunk_rows(c)]) for c in range(nc)], axis=0)


def _chunk_row_bcast(x, nc, i):
    return _per_chunk(x, nc, lambda xc: jnp.broadcast_to(xc[i:i + 1, :], (HGRN_CHUNK, HEAD_DIM)))


def _hgrn_gates(q, fl, lb, nc):
    lower, _ = _tri_masks()
    lower = lower.astype(BF16)
    s = _sigmoid(fl)
    f = lb + (1.0 - lb) * s
    k = 1.0 - f
    hi, mid, lo = _split3(jnp.log(f))
    a = jnp.concatenate([_dot(lower, hi[_chunk_rows(c)]) + _dot(lower, mid[_chunk_rows(c)]) + _dot(lower, lo[_chunk_rows(c)])
                         for c in range(nc)], axis=0)
    a_mid = _chunk_row_bcast(a, nc, HGRN_CHUNK // 2 - 1)
    a_last = _chunk_row_bcast(a, nc, HGRN_CHUNK - 1)
    qs = q * HGRN_SCALE
    e_in, e_out, e_end, e_all = jnp.exp(a - a_mid), jnp.exp(a_mid - a), jnp.exp(a_last - a), jnp.exp(a)
    decay = [jnp.exp(a[c * HGRN_CHUNK + HGRN_CHUNK - 1:(c + 1) * HGRN_CHUNK, :]) for c in range(nc)]
    return dict(s=s, f=f, k=k, decay=decay, e_in=e_in, e_out=e_out, e_end=e_end, e_all=e_all,
                qi=qs * e_in, ki=k * e_out, kd=k * e_end, qe=qs * e_all)


def _hgrn_forward(proj, lb_table, norm_g):
    t = proj.shape[1]
    tb = _tile(t, 512)
    nc = tb // HGRN_CHUNK
    n_chunks = t // HGRN_CHUNK

    def body(q_ref, f_ref, i_ref, g_ref, tab_ref, ng_ref, og_ref, o_ref, st_ref, state):
        @pl.when(pl.program_id(1) == 0)
        def _():
            state[...] = jnp.zeros_like(state)

        lower, _ = _tri_masks()
        gt = _hgrn_gates(q_ref[...], f_ref[...], _lower_bound(tab_ref), nc)
        qi, ki, kd, qe = (gt[n].astype(BF16) for n in ("qi", "ki", "kd", "qe"))
        vb = i_ref[...].astype(BF16)
        o_intra, d_state = [], []
        for c in range(nc):
            rows = _chunk_rows(c)
            p = jnp.where(lower, _dot_nt(qi[rows], ki[rows]), 0.0).astype(BF16)
            o_intra.append(_dot(p, vb[rows]))
            d_state.append(_dot_tn(vb[rows], kd[rows]))
        st = state[...]
        outs = []
        for c in range(nc):
            st_ref[c] = st
            outs.append(o_intra[c] + _dot_nt(qe[_chunk_rows(c)], st.astype(BF16)))
            st = st * gt["decay"][c] + d_state[c]
        state[...] = st
        o = jnp.concatenate(outs, axis=0)
        o_ref[...] = o
        _, oh = _rms_stats(o)
        gz = g_ref[...]
        og_ref[...] = ((oh * ng_ref[...]) * (gz * _sigmoid(gz))).astype(BF16)

    def blk(p):
        return pl.BlockSpec((None, tb, HEAD_DIM), lambda h, n: (p, n, h))

    out_blk = pl.BlockSpec((tb, HEAD_DIM), lambda h, n: (n, h))
    return pl.pallas_call(
        body, name="hgrn_fwd",
        out_shape=[jax.ShapeDtypeStruct((t, D_MODEL), BF16), jax.ShapeDtypeStruct((t, D_MODEL), F32),
                   jax.ShapeDtypeStruct((HEADS, n_chunks, HEAD_DIM, HEAD_DIM), F32)],
        grid=(HEADS, t // tb),
        in_specs=[blk(Q_POS), blk(Q_POS + 1), blk(Q_POS + 2), blk(Q_POS + 3),
                  pl.BlockSpec((2, HEAD_DIM), lambda h, n: (0, h)), pl.BlockSpec((1, HEAD_DIM), lambda h, n: (0, h))],
        out_specs=[out_blk, out_blk, pl.BlockSpec((None, nc, HEAD_DIM, HEAD_DIM), lambda h, n: (h, n, 0, 0))],
        scratch_shapes=[pltpu.VMEM((HEAD_DIM, HEAD_DIM), F32)],
        compiler_params=_params([((tb, HEAD_DIM), F32)] * 6 + [((nc, HEAD_DIM, HEAD_DIM), F32)], temps=8 << 20,
                                sem=("arbitrary", "arbitrary")),
    )(proj, proj, proj, proj, lb_table, norm_g)


def _branch_out_forward(a, og, proj, x, w_a, w_b, w_out, ffn_g):
    t = x.shape[0]
    tm = _tile(t, 256)

    def body(a_ref, og_ref, ga_ref, gb_ref, x_ref, wa_ref, wb_ref, wo_ref, g_ref, ya_ref, yb_ref, mg_ref, x1_ref, h2_ref,
             h2t_ref):
        ya = _dot(a_ref[...], wa_ref[...])
        yb = _dot(og_ref[...], wb_ref[...])
        ya_ref[...] = ya
        yb_ref[...] = yb
        merged = (_sigmoid(ga_ref[...]) * ya + _sigmoid(gb_ref[...]) * yb).astype(BF16)
        mg_ref[...] = merged
        x1 = x_ref[...] + _dot(merged, wo_ref[...])
        x1_ref[...] = x1
        _, xh = _rms_stats(x1)
        h2 = (xh * g_ref[...]).astype(BF16)
        h2_ref[...] = h2
        h2t_ref[...] = h2.T

    tok = pl.BlockSpec((tm, D_MODEL), lambda m: (m, 0))
    wsp = pl.BlockSpec((D_MODEL, D_MODEL), lambda m: (0, 0))
    return pl.pallas_call(
        body, name="branch_out_fwd",
        out_shape=[jax.ShapeDtypeStruct((t, D_MODEL), F32), jax.ShapeDtypeStruct((t, D_MODEL), F32),
                   jax.ShapeDtypeStruct((t, D_MODEL), BF16), jax.ShapeDtypeStruct((t, D_MODEL), F32),
                   jax.ShapeDtypeStruct((t, D_MODEL), BF16), jax.ShapeDtypeStruct((D_MODEL, t), BF16)],
        grid=(t // tm,),
        in_specs=[tok, tok, pl.BlockSpec((None, tm, D_MODEL), lambda m: (GATE_POS, m, 0)),
                  pl.BlockSpec((None, tm, D_MODEL), lambda m: (GATE_POS + 1, m, 0)), tok, wsp, wsp, wsp,
                  pl.BlockSpec((1, D_MODEL), lambda m: (0, 0))],
        out_specs=[tok] * 5 + [pl.BlockSpec((D_MODEL, tm), lambda m: (0, m))],
        compiler_params=_params([((tm, D_MODEL), BF16)] * 5 + [((tm, D_MODEL), F32)] * 6 + [((D_MODEL, D_MODEL), BF16)] * 3,
                                temps=8 << 20, sem=("arbitrary",)),
    )(a, og, proj, proj, x, w_a, w_b, w_out, ffn_g)


def _ffn_forward(h2, x1, w_gu, w_down):
    t = x1.shape[0]
    tm = _tile(t, 512)

    def body(h_ref, wg_ref, wu_ref, wd_ref, x1_ref, gu_ref, act_ref, x2_ref, acc):
        j = pl.program_id(1)
        h = h_ref[...]
        gate = _dot(h, wg_ref[...])
        up = _dot(h, wu_ref[...])
        gu_ref[0] = gate
        gu_ref[1] = up
        act = ((gate * _sigmoid(gate)) * up).astype(BF16)
        act_ref[...] = act
        part = _dot(act, wd_ref[...])

        @pl.when(j == 0)
        def _():
            acc[...] = part

        @pl.when(j > 0)
        def _():
            acc[...] += part

        @pl.when(j == 3)
        def _():
            x2_ref[...] = x1_ref[...] + acc[...]

    tok = pl.BlockSpec((tm, D_MODEL), lambda m, j: (m, 0))
    return pl.pallas_call(
        body, name="ffn_fwd",
        out_shape=[jax.ShapeDtypeStruct((4, 2, t, FF_BLOCK), F32), jax.ShapeDtypeStruct((4, t, FF_BLOCK), BF16),
                   jax.ShapeDtypeStruct((t, D_MODEL), F32)],
        grid=(t // tm, 4),
        in_specs=[tok, pl.BlockSpec((None, D_MODEL, FF_BLOCK), lambda m, j: (j, 0, 0)),
                  pl.BlockSpec((None, D_MODEL, FF_BLOCK), lambda m, j: (j + 4, 0, 0)),
                  pl.BlockSpec((FF_BLOCK, D_MODEL), lambda m, j: (j, 0)), tok],
        out_specs=[pl.BlockSpec((None, 2, tm, FF_BLOCK), lambda m, j: (j, 0, m, 0)),
                   pl.BlockSpec((None, tm, FF_BLOCK), lambda m, j: (j, m, 0)), tok],
        scratch_shapes=[pltpu.VMEM((tm, D_MODEL), F32)],
        compiler_params=_params([((tm, D_MODEL), BF16), ((D_MODEL, 768), BF16), ((D_MODEL, 768), BF16),
                                 ((FF_BLOCK, D_MODEL), BF16), ((tm, D_MODEL), F32), ((2, tm, 768), F32),
                                 ((tm, 768), BF16), ((tm, D_MODEL), F32)],
                                scratch=[((tm, D_MODEL), F32)], temps=8 << 20, sem=("arbitrary", "arbitrary")),
    )(h2, w_gu, w_gu, w_down, x1)


def _loss_and_final_backward(x2, target, final_g):
    t = x2.shape[0]
    tm = _tile(t, 256)

    def body(x_ref, t_ref, g_ref, loss_ref, dg_ref, dx_ref, dxb_ref):
        @pl.when(pl.program_id(0) == 0)
        def _():
            loss_ref[...] = jnp.zeros_like(loss_ref)
            dg_ref[...] = jnp.zeros_like(dg_ref)

        x = x_ref[...]
        g = g_ref[...]
        r, xh = _rms_stats(x)
        err = xh * g - t_ref[...]
        loss_ref[...] += 0.5 * jnp.sum(jnp.mean(err * err, axis=-1, keepdims=True), axis=0, keepdims=True)
        dy = err * (1.0 / D_MODEL)
        dg_ref[...] += jnp.sum(dy * xh, axis=0, keepdims=True)
        dxh = dy * g
        dx = r * (dxh - xh * jnp.mean(dxh * xh, axis=-1, keepdims=True))
        dx_ref[...] = dx
        dxb_ref[...] = dx.astype(BF16)

    tok = pl.BlockSpec((tm, D_MODEL), lambda m: (m, 0))
    vec = pl.BlockSpec((1, D_MODEL), lambda m: (0, 0))
    return pl.pallas_call(
        body, name="loss_final_bwd",
        out_shape=[jax.ShapeDtypeStruct((8, 128), F32), jax.ShapeDtypeStruct((1, D_MODEL), F32),
                   jax.ShapeDtypeStruct((t, D_MODEL), F32), jax.ShapeDtypeStruct((t, D_MODEL), BF16)],
        grid=(t // tm,), in_specs=[tok, tok, vec],
        out_specs=[pl.BlockSpec((8, 128), lambda m: (0, 0)), vec, tok, tok],
        compiler_params=_params([((tm, D_MODEL), F32)] * 4, temps=8 << 20, sem=("arbitrary",)),
    )(x2, target, final_g)


def _ffn_backward(dx2b, dx2, gu, x1, w_gu, w_down, ffn_g):
    t = x1.shape[0]
    tm = _tile(t, 512)

    def body(dxb_ref, dx2_ref, gu_ref, x1_ref, wg_ref, wu_ref, wd_ref, g_ref, dgu_ref, dx1_ref, dx1b_ref, dg_ref, acc):
        m, j = pl.program_id(0), pl.program_id(1)

        @pl.when((m == 0) & (j == 0))
        def _():
            dg_ref[...] = jnp.zeros_like(dg_ref)

        dact = _dot_nt(dxb_ref[...], wd_ref[...])
        gate, up = gu_ref[0], gu_ref[1]
        sg = _sigmoid(gate)
        dgate = (dact * up * (sg * (1.0 + gate * (1.0 - sg)))).astype(BF16)
        dup = (dact * (gate * sg)).astype(BF16)
        dgu_ref[0] = dgate
        dgu_ref[1] = dup
        part = _dot_nt(dgate, wg_ref[...]) + _dot_nt(dup, wu_ref[...])

        @pl.when(j == 0)
        def _():
            acc[...] = part

        @pl.when(j > 0)
        def _():
            acc[...] += part

        @pl.when(j == 3)
        def _():
            dx, dg = _rms_bwd(acc[...], x1_ref[...], g_ref[...])
            dx1 = dx2_ref[...] + dx
            dx1_ref[...] = dx1
            dx1b_ref[...] = dx1.astype(BF16)
            dg_ref[...] += dg

    tok = pl.BlockSpec((tm, D_MODEL), lambda m, j: (m, 0))
    vec = pl.BlockSpec((1, D_MODEL), lambda m, j: (0, 0))
    gu_spec = pl.BlockSpec((None, 2, tm, FF_BLOCK), lambda m, j: (j, 0, m, 0))
    return pl.pallas_call(
        body, name="ffn_bwd",
        out_shape=[jax.ShapeDtypeStruct((4, 2, t, FF_BLOCK), BF16), jax.ShapeDtypeStruct((t, D_MODEL), F32),
                   jax.ShapeDtypeStruct((t, D_MODEL), BF16), jax.ShapeDtypeStruct((1, D_MODEL), F32)],
        grid=(t // tm, 4),
        in_specs=[tok, tok, gu_spec, tok, pl.BlockSpec((None, D_MODEL, FF_BLOCK), lambda m, j: (j, 0, 0)),
                  pl.BlockSpec((None, D_MODEL, FF_BLOCK), lambda m, j: (j + 4, 0, 0)),
                  pl.BlockSpec((FF_BLOCK, D_MODEL), lambda m, j: (j, 0)), vec],
        out_specs=[gu_spec, tok, tok, vec],
        scratch_shapes=[pltpu.VMEM((tm, D_MODEL), F32)],
        compiler_params=_params([((tm, D_MODEL), BF16), ((tm, D_MODEL), F32), ((2, tm, 768), F32), ((tm, D_MODEL), F32),
                                 ((D_MODEL, 768), BF16), ((D_MODEL, 768), BF16), ((FF_BLOCK, D_MODEL), BF16),
                                 ((2, tm, 768), BF16), ((tm, D_MODEL), F32), ((tm, D_MODEL), BF16)],
                                scratch=[((tm, D_MODEL), F32)], temps=8 << 20, sem=("arbitrary", "arbitrary")),
    )(dx2b, dx2, gu, x1, w_gu, w_gu, w_down, ffn_g)


def _branch_out_backward(dx1b, ya, yb, proj, w_a, w_b, w_out):
    t = ya.shape[0]
    tm = _tile(t, 256)

    def body(dx_ref, ya_ref, yb_ref, ga_ref, gb_ref, wa_ref, wb_ref, wo_ref, dya_ref, dyb_ref, dgate_ref, da_ref, dog_ref):
        dm = _dot_nt(dx_ref[...], wo_ref[...])
        sa, sb = _sigmoid(ga_ref[...]), _sigmoid(gb_ref[...])
        dya = (dm * sa).astype(BF16)
        dyb = (dm * sb).astype(BF16)
        dya_ref[...] = dya
        dyb_ref[...] = dyb
        dgate_ref[0] = (dm * ya_ref[...] * (sa * (1.0 - sa))).astype(BF16)
        dgate_ref[1] = (dm * yb_ref[...] * (sb * (1.0 - sb))).astype(BF16)
        da_ref[...] = _dot_nt(dya, wa_ref[...])
        dog_ref[...] = _dot_nt(dyb, wb_ref[...])

    tok = pl.BlockSpec((tm, D_MODEL), lambda m: (m, 0))
    wsp = pl.BlockSpec((D_MODEL, D_MODEL), lambda m: (0, 0))
    return pl.pallas_call(
        body, name="branch_out_bwd",
        out_shape=[jax.ShapeDtypeStruct((t, D_MODEL), BF16), jax.ShapeDtypeStruct((t, D_MODEL), BF16),
                   jax.ShapeDtypeStruct((N_DEV, t, D_MODEL), BF16), jax.ShapeDtypeStruct((t, D_MODEL), F32),
                   jax.ShapeDtypeStruct((t, D_MODEL), F32)],
        grid=(t // tm,),
        in_specs=[tok, tok, tok, pl.BlockSpec((None, tm, D_MODEL), lambda m: (GATE_POS, m, 0)),
                  pl.BlockSpec((None, tm, D_MODEL), lambda m: (GATE_POS + 1, m, 0)), wsp, wsp, wsp],
        out_specs=[tok, tok, pl.BlockSpec((2, tm, D_MODEL), lambda m: (GATE_POS // 2, m, 0)), tok, tok],
        compiler_params=_params([((tm, D_MODEL), BF16)] * 5 + [((tm, D_MODEL), F32)] * 6 + [((D_MODEL, D_MODEL), BF16)] * 3,
                                temps=8 << 20, sem=("arbitrary",)),
    )(dx1b, ya, yb, proj, proj, w_a, w_b, w_out)


def _hgrn_backward(dproj, dog, o_saved, states, proj, lb_table, norm_g):
    t = proj.shape[1]
    tb = _tile(t, 256)
    nc = tb // HGRN_CHUNK
    nb = t // tb

    def body(_, dog_ref, o_ref, st_ref, q_ref, f_ref, i_ref, g_ref, tab_ref, ng_ref, dp_ref, dng_ref, dtab_ref, gstate):
        @pl.when(pl.program_id(1) == 0)
        def _():
            gstate[...] = jnp.zeros_like(gstate)
            dng_ref[...] = jnp.zeros_like(dng_ref)
            dtab_ref[...] = jnp.zeros_like(dtab_ref)

        lb = _lower_bound(tab_ref)
        ng = ng_ref[...]
        lower, upper = _tri_masks()
        gt = _hgrn_gates(q_ref[...], f_ref[...], lb, nc)
        qi, ki, kd, qe = (gt[n].astype(BF16) for n in ("qi", "ki", "kd", "qe"))
        vb = i_ref[...].astype(BF16)
        o, gz, d_og = o_ref[...], g_ref[...], dog_ref[...]
        r, oh = _rms_stats(o)
        sg = _sigmoid(gz)
        d_on = d_og * (gz * sg)
        dgz = d_og * (oh * ng) * (sg * (1.0 + gz * (1.0 - sg)))
        dng_ref[...] += jnp.sum(d_on * oh, axis=0, keepdims=True)
        doh = d_on * ng
        dob = (r * (doh - oh * jnp.mean(doh * oh, axis=-1, keepdims=True))).astype(BF16)
        dv_intra, dqi, dki, dqe, g_upd = [], [], [], [], []
        for c in range(nc):
            rows = _chunk_rows(c)
            p = jnp.where(lower, _dot_nt(qi[rows], ki[rows]), 0.0).astype(BF16)
            dv_intra.append(_dot_tn(p, dob[rows]))
            dp = jnp.where(lower, _dot_nt(dob[rows], vb[rows]), 0.0).astype(BF16)
            dqi.append(_dot(dp, ki[rows]))
            dki.append(_dot_tn(dp, qi[rows]))
            dqe.append(_dot(dob[rows], st_ref[c].astype(BF16)))
            g_upd.append(_dot_tn(dob[rows], qe[rows]))
        g_after = [None] * nc
        g = gstate[...]
        for c in reversed(range(nc)):
            g_after[c] = g
            g = g * gt["decay"][c] + g_upd[c]
        gstate[...] = g
        dkd, dv, da_last = [], [], []
        for c in range(nc):
            rows = _chunk_rows(c)
            gb = g_after[c].astype(BF16)
            dkd.append(_dot(vb[rows], gb))
            dv.append(dv_intra[c] + _dot_nt(kd[rows], gb))
            da_last.append(jnp.sum(g_after[c] * st_ref[c], axis=0, keepdims=True) * gt["decay"][c])
        dqi, dki, dqe, dkd, dv = (jnp.concatenate(z, axis=0) for z in (dqi, dki, dqe, dkd, dv))
        dqs = dqi * gt["e_in"] + dqe * gt["e_all"]
        dk = dki * gt["e_out"] + dkd * gt["e_end"]
        t_in, t_out, t_end = dqi * gt["qi"], dki * gt["ki"], dkd * gt["kd"]
        da = t_in - t_out + dqe * gt["qe"] - t_end
        row = lax.broadcasted_iota(jnp.int32, (HGRN_CHUNK, HEAD_DIM), 0)
        d_mid = t_out - t_in
        pieces = []
        for c in range(nc):
            rows = _chunk_rows(c)
            da_mid = jnp.sum(d_mid[rows], axis=0, keepdims=True)
            da_end = jnp.sum(t_end[rows], axis=0, keepdims=True) + da_last[c]
            da_c = da[rows] + jnp.where(row == HGRN_CHUNK // 2 - 1, da_mid, 0.0) + jnp.where(row == HGRN_CHUNK - 1, da_end, 0.0)
            pieces.append(_mask_mm(upper.astype(BF16), da_c))
        df = jnp.concatenate(pieces, axis=0) / gt["f"] - dk
        s = gt["s"]
        dlb = jnp.sum(df * (1.0 - s), axis=0, keepdims=True)
        dp_ref[0] = (dqs * HGRN_SCALE).astype(BF16)
        dp_ref[1] = (df * (1.0 - lb) * (s * (1.0 - s))).astype(BF16)
        dp_ref[2] = dv.astype(BF16)
        dp_ref[3] = dgz.astype(BF16)
        dt0 = dlb * (lb * (1.0 - lb))
        dtab_ref[0:1, :] += dt0
        dtab_ref[1:2, :] -= dt0

    def blk(p):
        return pl.BlockSpec((None, tb, HEAD_DIM), lambda h, n: (p, nb - 1 - n, h))

    tok = pl.BlockSpec((tb, HEAD_DIM), lambda h, n: (nb - 1 - n, h))
    return pl.pallas_call(
        body, name="hgrn_bwd",
        out_shape=[jax.ShapeDtypeStruct((N_DEV, t, D_MODEL), BF16), jax.ShapeDtypeStruct((1, D_MODEL), F32),
                   jax.ShapeDtypeStruct((2, D_MODEL), F32)],
        grid=(HEADS, nb),
        in_specs=[ANY, tok, tok, pl.BlockSpec((None, nc, HEAD_DIM, HEAD_DIM), lambda h, n: (h, nb - 1 - n, 0, 0)),
                  blk(Q_POS), blk(Q_POS + 1), blk(Q_POS + 2), blk(Q_POS + 3),
                  pl.BlockSpec((2, HEAD_DIM), lambda h, n: (0, h)), pl.BlockSpec((1, HEAD_DIM), lambda h, n: (0, h))],
        out_specs=[pl.BlockSpec((4, tb, HEAD_DIM), lambda h, n: (0, nb - 1 - n, h)),
                   pl.BlockSpec((1, HEAD_DIM), lambda h, n: (0, h)), pl.BlockSpec((2, HEAD_DIM), lambda h, n: (0, h))],
        scratch_shapes=[pltpu.VMEM((HEAD_DIM, HEAD_DIM), F32)],
        input_output_aliases={0: 0},
        compiler_params=_params([((tb, HEAD_DIM), F32)] * 6 + [((nc, HEAD_DIM, HEAD_DIM), F32)] + [((4, tb, HEAD_DIM), BF16)],
                                temps=8 << 20, sem=("arbitrary", "arbitrary")),
    )(dproj, dog, o_saved, states, proj, proj, proj, proj, lb_table, norm_g)


def _gmlp_backward(dproj, da, proj, ln_g, ln_b, w_s, bias_b):
    t = proj.shape[1]
    tm = _tile(t, 256)
    chunks = tm // GMLP_CHUNK

    def body(_, da_ref, u_ref, v_ref, lng_ref, lnb_ref, ws_ref, bias_ref, dp_ref, dlng_ref, dlnb_ref, dws_ref, dbs_ref,
             vn_scr, dvn_scr):
        @pl.when(pl.program_id(0) == 0)
        def _():
            dlng_ref[...] = jnp.zeros_like(dlng_ref)
            dlnb_ref[...] = jnp.zeros_like(dlnb_ref)
            dws_ref[...] = jnp.zeros_like(dws_ref)
            dbs_ref[...] = jnp.zeros_like(dbs_ref)

        v = v_ref[...]
        vv = _gelu(v)
        mu = jnp.mean(vv, axis=-1, keepdims=True)
        cen = vv - mu
        rstd = lax.rsqrt(jnp.mean(cen * cen, axis=-1, keepdims=True) + NORM_EPS)
        vhat = cen * rstd
        lng = lng_ref[...]
        vn_scr[...] = (vhat * lng + lnb_ref[...]).astype(BF16)
        row = lax.broadcasted_iota(jnp.int32, (GMLP_CHUNK, GMLP_CHUNK), 0)
        col = lax.broadcasted_iota(jnp.int32, (GMLP_CHUNK, GMLP_CHUNK), 1)
        for g in range(GROUPS):
            wm = _masked_ws(ws_ref, g)
            cols = slice(g * HEAD_DIM, (g + 1) * HEAD_DIM)
            dws = jnp.zeros((GMLP_CHUNK, GMLP_CHUNK), F32)
            dbs = jnp.zeros((GMLP_CHUNK, GMLP_CHUNK), F32)
            for c in range(chunks):
                rows = slice(c * GMLP_CHUNK, (c + 1) * GMLP_CHUNK)
                vn = vn_scr[rows, cols]
                mixed = _dot(wm, vn) + bias_ref[g]
                u = u_ref[rows, cols]
                d_a = da_ref[rows, cols]
                dp_ref[0, rows, cols] = (d_a * mixed * _gelu_grad(u)).astype(BF16)
                dmix = d_a * _gelu(u)
                dmb = dmix.astype(BF16)
                dbs = dbs + dmix
                dws = dws + _dot_nt(dmb, vn)
                dvn_scr[rows, cols] = _dot_tn(wm, dmb)
            dws_ref[g] += jnp.where(row >= col, dws, 0.0)
            dbs_ref[g] += jnp.broadcast_to(jnp.sum(dbs, axis=-1, keepdims=True), (GMLP_CHUNK, GMLP_CHUNK))
        dvn = dvn_scr[...]
        dlng_ref[...] += jnp.sum(dvn * vhat, axis=0, keepdims=True)
        dlnb_ref[...] += jnp.sum(dvn, axis=0, keepdims=True)
        dvh = dvn * lng
        dvv = rstd * (dvh - jnp.mean(dvh, axis=-1, keepdims=True) - vhat * jnp.mean(dvh * vhat, axis=-1, keepdims=True))
        dp_ref[1] = (dvv * _gelu_grad(v)).astype(BF16)

    tok = pl.BlockSpec((tm, D_MODEL), lambda m: (m, 0))
    small = pl.BlockSpec((GROUPS, GMLP_CHUNK, GMLP_CHUNK), lambda m: (0, 0, 0))
    vec = pl.BlockSpec((1, D_MODEL), lambda m: (0, 0))
    return pl.pallas_call(
        body, name="gmlp_bwd",
        out_shape=[jax.ShapeDtypeStruct(dproj.shape, BF16), jax.ShapeDtypeStruct((1, D_MODEL), F32),
                   jax.ShapeDtypeStruct((1, D_MODEL), F32), jax.ShapeDtypeStruct((GROUPS, GMLP_CHUNK, GMLP_CHUNK), F32),
                   jax.ShapeDtypeStruct((GROUPS, GMLP_CHUNK, GMLP_CHUNK), F32)],
        grid=(t // tm,),
        in_specs=[ANY, tok, pl.BlockSpec((None, tm, D_MODEL), lambda m: (U_POS, m, 0)),
                  pl.BlockSpec((None, tm, D_MODEL), lambda m: (U_POS + 1, m, 0)), vec, vec, small, small],
        out_specs=[pl.BlockSpec((2, tm, D_MODEL), lambda m: (U_POS // 2, m, 0)), vec, vec, small, small],
        scratch_shapes=[pltpu.VMEM((tm, D_MODEL), BF16), pltpu.VMEM((tm, D_MODEL), F32)],
        input_output_aliases={0: 0},
        compiler_params=_params([((tm, D_MODEL), F32)] * 3 + [((2, tm, D_MODEL), BF16)] + [((8, 128, 128), F32)] * 4,
                                scratch=[((tm, D_MODEL), BF16), ((tm, D_MODEL), F32)], temps=12 << 20, sem=("arbitrary",)),
    )(dproj, da, proj, proj, ln_g, ln_b, w_s, bias_b)


def _input_backward(dproj, w_in_g, x, dx1, mix_g):
    t = x.shape[0]
    tm = _tile(t, 256)

    def body(dp_ref, w_ref, x_ref, dx1_ref, g_ref, dx_ref, dg_ref):
        @pl.when(pl.program_id(0) == 0)
        def _():
            dg_ref[...] = jnp.zeros_like(dg_ref)

        dh = _dot_nt(dp_ref[0], w_ref[0])
        for p in range(1, N_DEV):
            dh = dh + _dot_nt(dp_ref[p], w_ref[p])
        dx, dg = _rms_bwd(dh, x_ref[...], g_ref[...])
        dx_ref[...] = dx1_ref[...] + dx
        dg_ref[...] += dg

    tok = pl.BlockSpec((tm, D_MODEL), lambda m: (m, 0))
    vec = pl.BlockSpec((1, D_MODEL), lambda m: (0, 0))
    return pl.pallas_call(
        body, name="input_bwd",
        out_shape=[jax.ShapeDtypeStruct((t, D_MODEL), F32), jax.ShapeDtypeStruct((1, D_MODEL), F32)],
        grid=(t // tm,),
        in_specs=[pl.BlockSpec((N_DEV, tm, D_MODEL), lambda m: (0, m, 0)),
                  pl.BlockSpec((N_DEV, D_MODEL, D_MODEL), lambda m: (0, 0, 0)), tok, tok, vec],
        out_specs=[tok, vec],
        compiler_params=_params([((N_DEV, tm, D_MODEL), BF16), ((N_DEV, D_MODEL, D_MODEL), BF16)] + [((tm, D_MODEL), F32)] * 3,
                                temps=4 << 20, sem=("arbitrary",)),
    )(dproj, w_in_g, x, dx1, mix_g)


def _weight_grad(name, a, b, a_spec, b_spec, out_shape, out_spec, grid, blocks):
    def body(a_ref, b_ref, o_ref):
        part = _dot_tn(a_ref[...], b_ref[...])
        m = pl.program_id(len(grid) - 1)

        @pl.when(m == 0)
        def _():
            o_ref[...] = part

        @pl.when(m > 0)
        def _():
            o_ref[...] += part

    return pl.pallas_call(
        body, name=name, out_shape=jax.ShapeDtypeStruct(out_shape, F32), grid=grid, in_specs=[a_spec, b_spec],
        out_specs=out_spec, compiler_params=_params(blocks, temps=8 << 20, sem=("arbitrary",) * len(grid)),
    )(a, b)


def _weight_grad_whole(name, a_t, b, b_spec, out_shape, out_spec, steps, blocks):
    def body(a_ref, b_ref, o_ref):
        o_ref[...] = _dot(a_ref[...], b_ref[...])

    return pl.pallas_call(
        body, name=name, out_shape=jax.ShapeDtypeStruct(out_shape, F32), grid=(steps,),
        in_specs=[pl.BlockSpec(a_t.shape, lambda j: (0, 0)), b_spec], out_specs=out_spec,
        compiler_params=_params([(a_t.shape, BF16)] + blocks, sem=("arbitrary",)),
    )(a_t, b)


def _pack_small(mix_g, ln_g, ln_b, w_s, b_s, lb_table, hg_norm, ffn_g, final_g):
    def part(a):
        a = a.reshape(-1, D_MODEL)
        return jnp.pad(a, ((0, 8 - a.shape[0]), (0, 0)))

    return jnp.concatenate([part(mix_g), part(ln_g), part(ln_b), part(hg_norm), part(ffn_g), part(final_g),
                            part(lb_table), part(b_s), w_s.reshape(GMLP_CHUNK, D_MODEL)], axis=0)


def _unpack_small(pack):
    return dict(norm_mix_g=pack[0:1], gmlp_ln_g=pack[8:9], gmlp_ln_b=pack[16:17], hgrn_norm_g=pack[24:25],
                norm_ffn_g=pack[32:33], norm_final_g=pack[40], hgrn_lb_table=pack[48:50],
                gmlp_b_s=pack[56:57].reshape(1, GROUPS, GMLP_CHUNK),
                gmlp_w_s=pack[64:192].reshape(1, GROUPS, GMLP_CHUNK, GMLP_CHUNK))


def _adamw_small(name, gathered, w, m, v):
    rows = w.shape[0]

    def body(p_ref, w_ref, m_ref, v_ref, g_out, d_out, m_out, v_out):
        g = p_ref[0]
        for j in range(1, N_DEV):
            g = g + p_ref[j]
        delta, m_new, v_new = _adamw_math(w_ref[...], g, m_ref[...], v_ref[...])
        g_out[...] = g
        d_out[...] = delta
        m_out[...] = m_new
        v_out[...] = v_new

    tr = _tile(rows, 64)
    spec = pl.BlockSpec((tr, D_MODEL), lambda r: (r, 0))
    return pl.pallas_call(
        body, name=name, out_shape=[jax.ShapeDtypeStruct((rows, D_MODEL), F32)] * 4, grid=(rows // tr,),
        in_specs=[pl.BlockSpec((N_DEV, tr, D_MODEL), lambda r: (0, r, 0)), spec, spec, spec], out_specs=[spec] * 4,
        compiler_params=_params([((N_DEV, tr, D_MODEL), F32)] + [((tr, D_MODEL), F32)] * 7, sem=("arbitrary",)),
    )(gathered, w, m, v)


def kernel(x, norm_mix_g, w_in, gmlp_ln_g, gmlp_ln_b, gmlp_w_s, gmlp_b_s, hgrn_lb_table, hgrn_norm_g, w_branch_a, w_branch_b, w_out, norm_ffn_g, w_gate_up, w_down, norm_final_g, loss_target, m_norm_mix_g, m_w_in, m_gmlp_ln_g, m_gmlp_ln_b, m_gmlp_w_s, m_gmlp_b_s, m_hgrn_lb_table, m_hgrn_norm_g, m_w_branch_a, m_w_branch_b, m_w_out, m_norm_ffn_g, m_w_gate_up, m_w_down, m_norm_final_g, v_norm_mix_g, v_w_in, v_gmlp_ln_g, v_gmlp_ln_b, v_gmlp_w_s, v_gmlp_b_s, v_hgrn_lb_table, v_hgrn_norm_g, v_w_branch_a, v_w_branch_b, v_w_out, v_norm_ffn_g, v_w_gate_up, v_w_down, v_norm_final_g):
    t = x.shape[1]
    x2d = x.reshape(t, D_MODEL)
    target = loss_target.reshape(t, D_MODEL)
    final_g = norm_final_g.reshape(1, D_MODEL)

    shards = [w_in[0].astype(BF16), w_branch_a[0].astype(BF16), w_branch_b[0].astype(BF16), w_out[0].astype(BF16),
              w_gate_up[0].astype(BF16), w_down[0].astype(BF16)]

    def rows_of(n):
        return lambda ref, j: ref.at[pl.ds(pl.multiple_of(j * n, 8), n)]

    gathered = [((N_DEV, D_MODEL, D_MODEL), BF16), ((D_MODEL, D_MODEL), BF16), ((D_MODEL, D_MODEL), BF16),
                ((D_MODEL, D_MODEL), BF16), ((N_DEV, D_MODEL, FF_BLOCK), BF16), ((D_FF, D_MODEL), BF16)]
    places = [lambda ref, j: ref.at[_pos_of_dev(j)], rows_of(BRANCH_ROWS), rows_of(BRANCH_ROWS), rows_of(BRANCH_ROWS),
              lambda ref, j: ref.at[j], rows_of(DOWN_ROWS)]
    (w_in_g,) = _all_gather("w_in_all_gather", shards[:1], gathered[:1], places[:1])
    _, later = lax.optimization_barrier((w_in_g, shards[1:]))
    w_a, w_b, w_o, w_gu, w_dn = _all_gather_async("weights_all_gather", 0, later, gathered[1:], places[1:])

    h, h_t = _rms_forward(x2d, norm_mix_g)
    proj = _proj_forward(h, w_in_g)
    bias_b = jnp.broadcast_to(gmlp_b_s[0][:, :, None], (GROUPS, GMLP_CHUNK, GMLP_CHUNK))
    a = _gmlp_forward(proj, gmlp_ln_g, gmlp_ln_b, gmlp_w_s[0], bias_b)
    og, o_saved, states = _hgrn_forward(proj, hgrn_lb_table, hgrn_norm_g)
    ya, yb, merged, x1, h2, h2_t = _branch_out_forward(a, og, proj, x2d, w_a, w_b, w_o, norm_ffn_g)
    gu, act, x2 = _ffn_forward(h2, x1, w_gu, w_dn)
    loss_tile, d_final_g, dx2, dx2b = _loss_and_final_backward(x2, target, final_g)

    core = lax.axis_index("c").astype(jnp.int32).reshape(1)
    chip = (2 * lax.axis_index("x") + lax.axis_index("y")).astype(jnp.int32).reshape(1)
    branch_rows, branch_shape = rows_of(BRANCH_ROWS), (BRANCH_ROWS, D_MODEL)
    branch_block = ((BRANCH_ROWS, D_MODEL), lambda q, r, c: (2 * q + c, 0))

    def chip_partials(names, grads, land, own_blocks):
        return [_chip_partial("chip_partial_" + nme, core, g_, blk, idx, l_)
                for nme, g_, (blk, idx), l_ in zip(names, grads, own_blocks, land)]

    tm = _tile(t, 512)
    nm = t // tm
    tok_a = pl.BlockSpec((tm, D_MODEL), lambda m: (m, 0))
    full_o = pl.BlockSpec((D_MODEL, D_MODEL), lambda m: (0, 0))
    sq_blocks = [((tm, D_MODEL), BF16)] * 2 + [((D_MODEL, D_MODEL), F32)]

    dgu, dx1, dx1b, d_ffn_g = _ffn_backward(dx2b, dx2, gu, x1, w_gu, w_dn, norm_ffn_g)
    g_gu = _weight_grad_whole(
        "grad_w_gate_up", h2_t, dgu, pl.BlockSpec((None, None, t, FF_BLOCK), lambda j: (j % 4, j // 4, 0, 0)),
        (N_DEV, D_MODEL, FF_BLOCK), pl.BlockSpec((None, D_MODEL, FF_BLOCK), lambda j: (j, 0, 0)), N_DEV,
        [((t, 768), BF16), ((D_MODEL, 768), F32)])
    g_dn = _weight_grad(
        "grad_w_down", act, dx2b, pl.BlockSpec((None, tm, FF_BLOCK), lambda j, m: (j, m, 0)),
        pl.BlockSpec((tm, D_MODEL), lambda j, m: (m, 0)), (D_FF, D_MODEL),
        pl.BlockSpec((FF_BLOCK, D_MODEL), lambda j, m: (j, 0)), (4, nm),
        [((tm, 768), BF16), ((tm, D_MODEL), BF16), ((FF_BLOCK, D_MODEL), F32)])
    names_f, grads_f = ["w_gate_up", "w_down"], [g_gu, g_dn]
    land_f = _exchange_sibling("ffn_grads_to_sibling", 2, grads_f, [lambda ref, j: ref.at[j], rows_of(DOWN_ROWS)],
                               [(D_MODEL, FF_BLOCK), (DOWN_ROWS, D_MODEL)])

    dya, dyb, dproj, da, dog = _branch_out_backward(dx1b, ya, yb, proj, w_a, w_b, w_o)
    g_a = _weight_grad("grad_w_a", a, dya, tok_a, tok_a, (D_MODEL, D_MODEL), full_o, (nm,), sq_blocks)
    g_b = _weight_grad("grad_w_b", og, dyb, tok_a, tok_a, (D_MODEL, D_MODEL), full_o, (nm,), sq_blocks)
    g_o = _weight_grad("grad_w_out", merged, dx1b, tok_a, tok_a, (D_MODEL, D_MODEL), full_o, (nm,), sq_blocks)
    names_b, grads_b = ["w_branch_a", "w_branch_b", "w_out"], [g_a, g_b, g_o]
    land_b = _exchange_sibling("branch_grads_to_sibling", 3, grads_b, [branch_rows] * 3, [branch_shape] * 3)

    part_f = chip_partials(names_f, grads_f, land_f,
                           [((None, 256, FF_BLOCK), lambda q, r, c: (2 * q + c, r, 0)),
                            ((DOWN_ROWS // 2, D_MODEL), lambda q, r, c: (2 * (2 * q + c) + r, 0))])
    landed_f = _exchange_chips("ffn_grads_to_chips", 5, part_f)

    dog, _ = lax.optimization_barrier((dog, part_f))
    dproj, d_hg_norm, d_lb = _hgrn_backward(dproj, dog, o_saved, states, proj, hgrn_lb_table, hgrn_norm_g)

    part_b = chip_partials(names_b, grads_b, land_b, [branch_block] * 3)
    landed_b = _exchange_chips("branch_grads_to_chips", 6, part_b)

    da, _ = lax.optimization_barrier((da, part_b))
    dproj, d_ln_g, d_ln_b, d_ws, d_bs = _gmlp_backward(dproj, da, proj, gmlp_ln_g, gmlp_ln_b, gmlp_w_s[0], bias_b)

    def packed(vals):
        return _pack_small(*vals)

    w_pack = packed([norm_mix_g, gmlp_ln_g, gmlp_ln_b, gmlp_w_s, gmlp_b_s, hgrn_lb_table, hgrn_norm_g, norm_ffn_g, norm_final_g])
    m_pack = packed([m_norm_mix_g, m_gmlp_ln_g, m_gmlp_ln_b, m_gmlp_w_s, m_gmlp_b_s, m_hgrn_lb_table, m_hgrn_norm_g, m_norm_ffn_g, m_norm_final_g])
    v_pack = packed([v_norm_mix_g, v_gmlp_ln_g, v_gmlp_ln_b, v_gmlp_w_s, v_gmlp_b_s, v_hgrn_lb_table, v_hgrn_norm_g, v_norm_ffn_g, v_norm_final_g])
    small_partial = _pack_small(jnp.zeros((1, D_MODEL), F32), d_ln_g, d_ln_b, d_ws, d_bs[:, :, 0], d_lb, d_hg_norm, d_ffn_g,
                                d_final_g)
    (small_all,) = _all_gather_async("small_grads_all_gather", 1, [small_partial],
                                     [((N_DEV, SMALL_ROWS, D_MODEL), F32)], [lambda ref, j: ref.at[j]])

    g_in = _weight_grad_whole(
        "grad_w_in", h_t, dproj, pl.BlockSpec((None, t, D_MODEL), lambda p: (p, 0, 0)), (N_DEV, D_MODEL, D_MODEL),
        pl.BlockSpec((None, D_MODEL, D_MODEL), lambda p: (p, 0, 0)), N_DEV, [((t, D_MODEL), BF16), ((D_MODEL, D_MODEL), F32)])
    land_i = _exchange_sibling("w_in_grads_to_sibling", 4, [g_in], [lambda ref, j: ref.at[_pos_of_dev(j)]],
                               [(D_MODEL, D_MODEL)])

    big = {}
    for nme, own, lnd, w, m, v in zip(
            names_f + names_b, part_f + part_b, landed_f + landed_b,
            [w_gate_up, w_down, w_branch_a, w_branch_b, w_out], [m_w_gate_up, m_w_down, m_w_branch_a, m_w_branch_b, m_w_out],
            [v_w_gate_up, v_w_down, v_w_branch_a, v_w_branch_b, v_w_out]):
        big[nme] = [o_[None] for o_ in _adamw("adamw_" + nme, chip, own, lnd, w[0], m[0], v[0])]
    small_outs = _adamw_small("adamw_small", small_all, w_pack, m_pack, v_pack)
    land_i, _ = lax.optimization_barrier((land_i, (big, small_outs)))
    part_i = chip_partials(["w_in"], [g_in], land_i,
                           [((None, 256, D_MODEL), lambda q, r, c: (_pos_of_dev(2 * q + c), r, 0))])
    landed_i = _exchange_chips("w_in_grads_to_chips", 7, part_i)

    dx1, _ = lax.optimization_barrier((dx1, part_i))
    grad_x, d_mix_g = _input_backward(dproj, w_in_g, x2d, dx1, norm_mix_g)
    big["w_in"] = [o_[None] for o_ in _adamw("adamw_w_in", chip, part_i[0], landed_i[0], w_in[0], m_w_in[0], v_w_in[0])]

    def row8(a):
        return jnp.pad(a, ((0, 7), (0, 0)))

    d_mix_g, _ = lax.optimization_barrier((d_mix_g, landed_i))
    (mix_all,) = _all_gather_async("mix_gain_grad_all_gather", 8, [row8(d_mix_g)], [((N_DEV, 8, D_MODEL), F32)],
                                   [lambda ref, j: ref.at[j]])
    mix_outs = _adamw_small("adamw_mix_gain", mix_all, row8(norm_mix_g), row8(m_norm_mix_g), row8(v_norm_mix_g))
    small = [dict(_unpack_small(p), norm_mix_g=q[0:1]) for p, q in zip(small_outs, mix_outs)]

    loss = lax.psum(loss_tile[0, 0], ("x", "y", "c"))
    order = ["norm_mix_g", "w_in", "gmlp_ln_g", "gmlp_ln_b", "gmlp_w_s", "gmlp_b_s", "hgrn_lb_table", "hgrn_norm_g",
             "w_branch_a", "w_branch_b", "w_out", "norm_ffn_g", "w_gate_up", "w_down", "norm_final_g"]
    outs = [loss, grad_x.reshape(1, t, D_MODEL)]
    for kind in range(4):
        for nme in order:
            outs.append(big[nme][kind] if nme in big else small[kind][nme])
    return tuple(outs)
```

```python
import functools

import jax
import jax.numpy as jnp
from jax import lax
from jax.experimental import pallas as pl
from jax.experimental.pallas import tpu as pltpu
from jax.experimental.pallas import tpu_sc as plsc

F32, BF16 = jnp.float32, jnp.bfloat16
D_MODEL = 1024
N_DEV = 8
HEADS = 8
HEAD_DIM = 128
GROUPS = 8
GMLP_CHUNK = 128
HGRN_CHUNK = 64
HGRN_SCALE = HEAD_DIM ** -0.5
D_FF = 2816
FF_BLOCK = D_FF // 4
DOWN_ROWS = D_FF // N_DEV
BRANCH_ROWS = D_MODEL // N_DEV
NORM_EPS = 1e-6
ADAM_LR, ADAM_B1, ADAM_B2, ADAM_EPS, ADAM_WD, ADAM_STEP = 0.001, 0.9, 0.999, 1e-08, 0.01, 10
SMALL_ROWS = 192
V7X_VMEM_BYTES = 64 * 1024 * 1024
VMEM_CAP = V7X_VMEM_BYTES - 6 * 1024 * 1024
MESH_ID = pl.DeviceIdType.MESH
ANY = pl.BlockSpec(memory_space=pl.ANY)
Q_POS, U_POS, GATE_POS = 0, 4, 6


def _pos_of_dev(j):
    return jnp.where(j < 2, j + 4, jnp.where(j < 6, j - 2, j))


def _dev_of_pos(p):
    return jnp.where(p < 4, p + 2, jnp.where(p < 6, p - 4, p))


def _nbytes(shape, dtype):
    n = 1
    for s in shape:
        n *= s
    return n * jnp.dtype(dtype).itemsize


def _params(blocks, scratch=(), temps=0, sem=None):
    need = 2 * sum(_nbytes(s, d) for s, d in blocks) + sum(_nbytes(s, d) for s, d in scratch) + temps
    assert need + (4 << 20) <= VMEM_CAP, need
    return pltpu.CompilerParams(dimension_semantics=sem, vmem_limit_bytes=VMEM_CAP)


def _tile(n, pref):
    return pref if n % pref == 0 else n


def _dot(a, b):
    return jnp.dot(a, b, preferred_element_type=F32)


def _dot_nt(a, b):
    return lax.dot_general(a, b, (((1,), (1,)), ((), ())), preferred_element_type=F32)


def _dot_tn(a, b):
    return lax.dot_general(a, b, (((0,), (0,)), ((), ())), preferred_element_type=F32)


def _sigmoid(x):
    return 1.0 / (1.0 + jnp.exp(-x))


_GELU_C = 0.7978845608028654


def _gelu(x):
    return x * (0.5 * (1.0 + jnp.tanh(_GELU_C * (x + 0.044715 * (x * x * x)))))


def _gelu_grad(x):
    t = jnp.tanh(_GELU_C * (x + 0.044715 * (x * x * x)))
    return 0.5 * (1.0 + t) + 0.5 * x * (1.0 - t * t) * (_GELU_C * (1.0 + 3.0 * 0.044715 * x * x))


def _rms_stats(x):
    r = lax.rsqrt(jnp.mean(x * x, axis=-1, keepdims=True) + NORM_EPS)
    return r, x * r


def _rms_bwd(dy, x, g):
    r, xh = _rms_stats(x)
    dg = jnp.sum(dy * xh, axis=0, keepdims=True)
    dxh = dy * g
    dx = r * (dxh - xh * jnp.mean(dxh * xh, axis=-1, keepdims=True))
    return dx, dg


def _split3(x):
    hi = x.astype(BF16)
    r = x - hi.astype(F32)
    mid = r.astype(BF16)
    lo = (r - mid.astype(F32)).astype(BF16)
    return hi, mid, lo


def _mask_mm(mask_bf16, x):
    hi, mid, lo = _split3(x)
    return _dot(mask_bf16, hi) + _dot(mask_bf16, mid) + _dot(mask_bf16, lo)


def _place():
    return lax.axis_index("x"), lax.axis_index("y"), lax.axis_index("c")


def _gather_copies(src, out, send, recv, loc, slicers):
    n = len(src)
    x, y, c = _place()
    me, sib = (x, y, c), (x, y, 1 - c)
    chips = [(1 - x, y), (x, 1 - y), (1 - x, 1 - y)]

    def dev(p):
        return 4 * p[0] + 2 * p[1] + p[2]

    def rc(i, k, block, to, from_src=False):
        dst = slicers[i](out[i], dev(block))
        return pltpu.make_async_remote_copy(
            src_ref=src[i] if from_src else dst, dst_ref=dst, send_sem=send.at[7 * i + k],
            recv_sem=recv.at[7 * i + k], device_id=to, device_id_type=MESH_ID)

    mine = [pltpu.make_async_copy(src[i], slicers[i](out[i], dev(me)), loc.at[i]) for i in range(n)]
    for cp in mine:
        cp.start()
    first = []
    for i in range(n):
        first.append(rc(i, 0, me, sib, True))
        for j, chip in enumerate(chips):
            first.append(rc(i, 1 + j, me, (*chip, c), True))
    for cp in first:
        cp.start()
    passed = []
    for j, chip in enumerate(chips):
        for i in range(n):
            rc(i, 1 + j, (*chip, c), me).wait_recv()
            cp = rc(i, 4 + j, (*chip, c), sib)
            cp.start()
            passed.append(cp)
    for i in range(n):
        rc(i, 0, sib, me).wait_recv()
        for j, chip in enumerate(chips):
            rc(i, 4 + j, (*chip, 1 - c), me).wait_recv()
    for cp in first + passed:
        cp.wait_send()
    for cp in mine:
        cp.wait()


def _gather_scratch(n):
    return [pltpu.SemaphoreType.DMA((7 * n,)), pltpu.SemaphoreType.DMA((7 * n,)), pltpu.SemaphoreType.DMA((n,))]


def _all_gather(name, srcs, out_shapes, slicers):
    n = len(srcs)

    def body(*refs):
        _gather_copies(refs[:n], refs[n:2 * n], *refs[2 * n:], slicers)

    return pl.pallas_call(
        body, name=name, out_shape=[jax.ShapeDtypeStruct(s, d) for s, d in out_shapes],
        in_specs=[ANY] * n, out_specs=[ANY] * n, scratch_shapes=_gather_scratch(n),
    )(*srcs)


def _handshake(peers):
    barrier = pltpu.get_barrier_semaphore()
    for peer in peers:
        pl.semaphore_signal(barrier, inc=1, device_id=peer, device_id_type=MESH_ID)
    pl.semaphore_wait(barrier, len(peers))


def _all_gather_async(name, collective_id, srcs, out_shapes, slicers):
    n = len(srcs)

    def body(*refs):
        x, y, c = _place()
        _handshake([(1 - x if dx else x, 1 - y if dy else y, 1 - c if dc else c)
                    for dx in (0, 1) for dy in (0, 1) for dc in (0, 1) if dx or dy or dc])
        _gather_copies(refs[:n], refs[n:2 * n], *refs[2 * n:], slicers)

    return _sequencer_call(name, collective_id, body, srcs, [jax.ShapeDtypeStruct(s, d) for s, d in out_shapes],
                           _gather_scratch(n))


def _sequencer_call(name, collective_id, body, operands, out_types, scratch):
    return pl.kernel(
        body, out_type=out_types, mesh=plsc.ScalarSubcoreMesh(axis_name="sequencer", num_cores=1), name=name,
        scratch_types=scratch, compiler_params=pltpu.CompilerParams(collective_id=collective_id),
    )(*operands)


def _exchange_sibling(name, collective_id, grads, shard_fns, shard_shapes):
    n = len(grads)

    def body(*refs):
        g, land = refs[:n], refs[n:2 * n]
        send, recv = refs[2 * n:]
        x, y, c = _place()
        _handshake([(x, y, 1 - c)])
        remote = []
        for i in range(n):
            for q in range(4):
                cp = pltpu.make_async_remote_copy(
                    src_ref=shard_fns[i](g[i], 2 * q + (1 - c)), dst_ref=land[i].at[q], send_sem=send.at[4 * i + q],
                    recv_sem=recv.at[4 * i + q], device_id=(x, y, 1 - c), device_id_type=MESH_ID)
                cp.start()
                remote.append(cp)
        for cp in remote:
            cp.wait()

    return _sequencer_call(name, collective_id, body, grads, [jax.ShapeDtypeStruct((4, *s), F32) for s in shard_shapes],
                           [pltpu.SemaphoreType.DMA((4 * n,)), pltpu.SemaphoreType.DMA((4 * n,))])


def _exchange_chips(name, collective_id, parts):
    n = len(parts)

    def body(*refs):
        part, out = refs[:n], refs[n:2 * n]
        send, recv = refs[2 * n:]
        x, y, c = _place()
        _handshake([(1 - x, y, c), (x, 1 - y, c), (1 - x, 1 - y, c)])
        remote = []
        for i in range(n):
            for s in range(3):
                qx = 1 - x if (s + 1) // 2 else x
                qy = 1 - y if (s + 1) % 2 else y
                cp = pltpu.make_async_remote_copy(
                    src_ref=part[i].at[2 * qx + qy], dst_ref=out[i].at[s], send_sem=send.at[3 * i + s],
                    recv_sem=recv.at[3 * i + s], device_id=(qx, qy, c), device_id_type=MESH_ID)
                cp.start()
                remote.append(cp)
        for cp in remote:
            cp.wait()

    return _sequencer_call(name, collective_id, body, parts,
                           [jax.ShapeDtypeStruct((3, *p.shape[1:]), p.dtype) for p in parts],
                           [pltpu.SemaphoreType.DMA((3 * n,)), pltpu.SemaphoreType.DMA((3 * n,))])


def _chip_partial(name, core, grad, own_block, own_index, land):
    _, rows, cols = land.shape
    tr = own_block[-2]

    def body(core_ref, a_ref, b_ref, o_ref):
        o_ref[...] = (a_ref[...] + b_ref[...]).astype(BF16)

    spec = pl.BlockSpec((None, tr, cols), lambda q, r, c: (q, r, 0))
    return pl.pallas_call(
        body, name=name, out_shape=jax.ShapeDtypeStruct(land.shape, BF16),
        grid_spec=pltpu.PrefetchScalarGridSpec(
            num_scalar_prefetch=1, grid=(4, rows // tr),
            in_specs=[pl.BlockSpec(own_block, lambda q, r, c: own_index(q, r, c[0])), spec], out_specs=spec),
        compiler_params=_params([((tr, cols), F32)] * 2 + [((tr, cols), BF16)], sem=("arbitrary", "arbitrary")),
    )(core, grad, land)


def _adamw_math(w, g, m, v):
    m = ADAM_B1 * m + (1.0 - ADAM_B1) * g
    v = ADAM_B2 * v + (1.0 - ADAM_B2) * (g * g)
    m_hat = m / (1.0 - ADAM_B1 ** ADAM_STEP)
    v_hat = v / (1.0 - ADAM_B2 ** ADAM_STEP)
    delta = -ADAM_LR * (m_hat / (jnp.sqrt(v_hat) + ADAM_EPS) + ADAM_WD * w)
    return delta, m, v


def _adamw(name, chip, own, landed, w, m, v):
    _, rows, cols = own.shape
    tr = _tile(rows, 256) if rows % 256 == 0 else _tile(rows, 176)

    def body(chip_ref, own_ref, l_ref, w_ref, m_ref, v_ref, g_out, d_out, m_out, v_out):
        g = own_ref[...].astype(F32)
        for s in range(3):
            g = g + l_ref[s].astype(F32)
        delta, m_new, v_new = _adamw_math(w_ref[...], g, m_ref[...], v_ref[...])
        g_out[...] = g
        d_out[...] = delta
        m_out[...] = m_new
        v_out[...] = v_new

    spec = pl.BlockSpec((tr, cols), lambda r, c: (r, 0))
    return pl.pallas_call(
        body, name=name, out_shape=[jax.ShapeDtypeStruct((rows, cols), F32)] * 4,
        grid_spec=pltpu.PrefetchScalarGridSpec(
            num_scalar_prefetch=1, grid=(rows // tr,),
            in_specs=[pl.BlockSpec((None, tr, cols), lambda r, c: (c[0], r, 0)),
                      pl.BlockSpec((3, tr, cols), lambda r, c: (0, r, 0)), spec, spec, spec],
            out_specs=[spec] * 4),
        compiler_params=_params([((4, tr, cols), own.dtype)] + [((tr, cols), F32)] * 7, sem=("arbitrary",)),
    )(chip, own, landed, w, m, v)


def _proj_forward(x, gain, w_in_g):
    t = x.shape[0]
    tm = _tile(t, 1024)

    def body(x_ref, g_ref, w_ref, o_ref, h_ref, ht_ref):
        @pl.when(pl.program_id(1) == 0)
        def _():
            _, xh = _rms_stats(x_ref[...])
            h = (xh * g_ref[...]).astype(BF16)
            h_ref[...] = h
            ht_ref[...] = h.T

        o_ref[...] = _dot(h_ref[...], w_ref[...])

    tok = pl.BlockSpec((tm, D_MODEL), lambda m, p: (m, 0))
    return pl.pallas_call(
        body, name="proj_fwd",
        out_shape=[jax.ShapeDtypeStruct((N_DEV, t, D_MODEL), F32), jax.ShapeDtypeStruct((t, D_MODEL), BF16),
                   jax.ShapeDtypeStruct((D_MODEL, t), BF16)],
        grid=(t // tm, N_DEV),
        in_specs=[tok, pl.BlockSpec((1, D_MODEL), lambda m, p: (0, 0)),
                  pl.BlockSpec((None, D_MODEL, D_MODEL), lambda m, p: (p, 0, 0))],
        out_specs=[pl.BlockSpec((None, tm, D_MODEL), lambda m, p: (p, m, 0)), tok,
                   pl.BlockSpec((D_MODEL, tm), lambda m, p: (0, m))],
        compiler_params=_params([((tm, D_MODEL), F32)] * 2 + [((D_MODEL, D_MODEL), BF16)] + [((tm, D_MODEL), BF16)] * 2,
                                temps=6 << 20, sem=("arbitrary", "arbitrary")),
    )(x, gain, w_in_g)


def _masked_ws(ws_ref, g):
    row = lax.broadcasted_iota(jnp.int32, (GMLP_CHUNK, GMLP_CHUNK), 0)
    col = lax.broadcasted_iota(jnp.int32, (GMLP_CHUNK, GMLP_CHUNK), 1)
    return jnp.where(row >= col, ws_ref[g], 0.0).astype(BF16)


def _gmlp_forward(proj, ln_g, ln_b, w_s, bias_b):
    t = proj.shape[1]
    tm = _tile(t, 256)
    chunks = tm // GMLP_CHUNK

    def body(u_ref, v_ref, lng_ref, lnb_ref, ws_ref, bias_ref, a_ref, vn_scr):
        vv = _gelu(v_ref[...])
        mu = jnp.mean(vv, axis=-1, keepdims=True)
        cen = vv - mu
        var = jnp.mean(cen * cen, axis=-1, keepdims=True)
        vn_scr[...] = ((cen * lax.rsqrt(var + NORM_EPS)) * lng_ref[...] + lnb_ref[...]).astype(BF16)
        for g in range(GROUPS):
            wm = _masked_ws(ws_ref, g)
            cols = slice(g * HEAD_DIM, (g + 1) * HEAD_DIM)
            for c in range(chunks):
                rows = slice(c * GMLP_CHUNK, (c + 1) * GMLP_CHUNK)
                mixed = _dot(wm, vn_scr[rows, cols]) + bias_ref[g]
                a_ref[rows, cols] = (_gelu(u_ref[rows, cols]) * mixed).astype(BF16)

    small = pl.BlockSpec((GROUPS, GMLP_CHUNK, GMLP_CHUNK), lambda m: (0, 0, 0))
    vec = pl.BlockSpec((1, D_MODEL), lambda m: (0, 0))
    return pl.pallas_call(
        body, name="gmlp_fwd", out_shape=jax.ShapeDtypeStruct((t, D_MODEL), BF16), grid=(t // tm,),
        in_specs=[pl.BlockSpec((None, tm, D_MODEL), lambda m: (U_POS, m, 0)),
                  pl.BlockSpec((None, tm, D_MODEL), lambda m: (U_POS + 1, m, 0)), vec, vec, small, small],
        out_specs=pl.BlockSpec((tm, D_MODEL), lambda m: (m, 0)),
        scratch_shapes=[pltpu.VMEM((tm, D_MODEL), BF16)],
        compiler_params=_params([((tm, D_MODEL), F32)] * 2 + [((tm, D_MODEL), BF16)] + [((8, 128, 128), F32)] * 2,
                                scratch=[((tm, D_MODEL), BF16)], temps=8 << 20, sem=("arbitrary",)),
    )(proj, proj, ln_g, ln_b, w_s, bias_b)


def _lower_bound(tab_ref):
    t0, t1 = tab_ref[0:1, :], tab_ref[1:2, :]
    mx = jnp.maximum(t0, t1)
    e0, e1 = jnp.exp(t0 - mx), jnp.exp(t1 - mx)
    return e0 / (e0 + e1)


def _tri_masks():
    row = lax.broadcasted_iota(jnp.int32, (HGRN_CHUNK, HGRN_CHUNK), 0)
    col = lax.broadcasted_iota(jnp.int32, (HGRN_CHUNK, HGRN_CHUNK), 1)
    return row >= col, row <= col


def _chunk_rows(c):
    return slice(c * HGRN_CHUNK, (c + 1) * HGRN_CHUNK)


def _per_chunk(x, nc, fn):
    return jnp.concatenate([fn(x[_chunk_rows(c)]) for c in range(nc)], axis=0)


def _chunk_row_bcast(x, nc, i):
    return _per_chunk(x, nc, lambda xc: jnp.broadcast_to(xc[i:i + 1, :], (HGRN_CHUNK, HEAD_DIM)))


def _hgrn_gates(q, fl, lb, nc):
    lower, _ = _tri_masks()
    lower = lower.astype(BF16)
    s = _sigmoid(fl)
    f = lb + (1.0 - lb) * s
    k = 1.0 - f
    hi, mid, lo = _split3(jnp.log(f))
    a = jnp.concatenate([_dot(lower, hi[_chunk_rows(c)]) + _dot(lower, mid[_chunk_rows(c)]) + _dot(lower, lo[_chunk_rows(c)])
                         for c in range(nc)], axis=0)
    a_mid = _chunk_row_bcast(a, nc, HGRN_CHUNK // 2 - 1)
    a_last = _chunk_row_bcast(a, nc, HGRN_CHUNK - 1)
    qs = q * HGRN_SCALE
    e_in, e_out, e_end, e_all = jnp.exp(a - a_mid), jnp.exp(a_mid - a), jnp.exp(a_last - a), jnp.exp(a)
    decay = [jnp.exp(a[c * HGRN_CHUNK + HGRN_CHUNK - 1:(c + 1) * HGRN_CHUNK, :]) for c in range(nc)]
    return dict(s=s, f=f, k=k, decay=decay, e_in=e_in, e_out=e_out, e_end=e_end, e_all=e_all,
                qi=qs * e_in, ki=k * e_out, kd=k * e_end, qe=qs * e_all)


def _hgrn_forward(proj, lb_table, norm_g):
    t = proj.shape[1]
    tb = _tile(t, 1024)
    nc = tb // HGRN_CHUNK
    n_chunks = t // HGRN_CHUNK

    def body(q_ref, f_ref, i_ref, g_ref, tab_ref, ng_ref, og_ref, o_ref, st_ref, state):
        @pl.when(pl.program_id(1) == 0)
        def _():
            state[...] = jnp.zeros_like(state)

        lower, _ = _tri_masks()
        gt = _hgrn_gates(q_ref[...], f_ref[...], _lower_bound(tab_ref), nc)
        qi, ki, kd, qe = (gt[n].astype(BF16) for n in ("qi", "ki", "kd", "qe"))
        vb = i_ref[...].astype(BF16)
        o_intra, d_state = [], []
        for c in range(nc):
            rows = _chunk_rows(c)
            p = jnp.where(lower, _dot_nt(qi[rows], ki[rows]), 0.0).astype(BF16)
            o_intra.append(_dot(p, vb[rows]))
            d_state.append(_dot_tn(vb[rows], kd[rows]))
        st = state[...]
        outs = []
        for c in range(nc):
            st_ref[c] = st
            outs.append(o_intra[c] + _dot_nt(qe[_chunk_rows(c)], st.astype(BF16)))
            st = st * gt["decay"][c] + d_state[c]
        state[...] = st
        o = jnp.concatenate(outs, axis=0)
        o_ref[...] = o
        _, oh = _rms_stats(o)
        gz = g_ref[...]
        og_ref[...] = ((oh * ng_ref[...]) * (gz * _sigmoid(gz))).astype(BF16)

    def blk(p):
        return pl.BlockSpec((None, tb, HEAD_DIM), lambda h, n: (p, n, h))

    out_blk = pl.BlockSpec((tb, HEAD_DIM), lambda h, n: (n, h))
    return pl.pallas_call(
        body, name="hgrn_fwd",
        out_shape=[jax.ShapeDtypeStruct((t, D_MODEL), BF16), jax.ShapeDtypeStruct((t, D_MODEL), F32),
                   jax.ShapeDtypeStruct((HEADS, n_chunks, HEAD_DIM, HEAD_DIM), F32)],
        grid=(HEADS, t // tb),
        in_specs=[blk(Q_POS), blk(Q_POS + 1), blk(Q_POS + 2), blk(Q_POS + 3),
                  pl.BlockSpec((2, HEAD_DIM), lambda h, n: (0, h)), pl.BlockSpec((1, HEAD_DIM), lambda h, n: (0, h))],
        out_specs=[out_blk, out_blk, pl.BlockSpec((None, nc, HEAD_DIM, HEAD_DIM), lambda h, n: (h, n, 0, 0))],
        scratch_shapes=[pltpu.VMEM((HEAD_DIM, HEAD_DIM), F32)],
        compiler_params=_params([((tb, HEAD_DIM), F32)] * 6 + [((nc, HEAD_DIM, HEAD_DIM), F32)], temps=8 << 20,
                                sem=("arbitrary", "arbitrary")),
    )(proj, proj, proj, proj, lb_table, norm_g)


def _branch_out_forward(a, og, proj, x, w_a, w_b, w_out, ffn_g):
    t = x.shape[0]
    tm = _tile(t, 256)

    def body(a_ref, og_ref, ga_ref, gb_ref, x_ref, wa_ref, wb_ref, wo_ref, g_ref, ya_ref, yb_ref, mg_ref, x1_ref, h2_ref,
             h2t_ref):
        ya = _dot(a_ref[...], wa_ref[...])
        yb = _dot(og_ref[...], wb_ref[...])
        ya_ref[...] = ya
        yb_ref[...] = yb
        merged = (_sigmoid(ga_ref[...]) * ya + _sigmoid(gb_ref[...]) * yb).astype(BF16)
        mg_ref[...] = merged
        x1 = x_ref[...] + _dot(merged, wo_ref[...])
        x1_ref[...] = x1
        _, xh = _rms_stats(x1)
        h2 = (xh * g_ref[...]).astype(BF16)
        h2_ref[...] = h2
        h2t_ref[...] = h2.T

    tok = pl.BlockSpec((tm, D_MODEL), lambda m: (m, 0))
    wsp = pl.BlockSpec((D_MODEL, D_MODEL), lambda m: (0, 0))
    return pl.pallas_call(
        body, name="branch_out_fwd",
        out_shape=[jax.ShapeDtypeStruct((t, D_MODEL), F32), jax.ShapeDtypeStruct((t, D_MODEL), F32),
                   jax.ShapeDtypeStruct((t, D_MODEL), BF16), jax.ShapeDtypeStruct((t, D_MODEL), F32),
                   jax.ShapeDtypeStruct((t, D_MODEL), BF16), jax.ShapeDtypeStruct((D_MODEL, t), BF16)],
        grid=(t // tm,),
        in_specs=[tok, tok, pl.BlockSpec((None, tm, D_MODEL), lambda m: (GATE_POS, m, 0)),
                  pl.BlockSpec((None, tm, D_MODEL), lambda m: (GATE_POS + 1, m, 0)), tok, wsp, wsp, wsp,
                  pl.BlockSpec((1, D_MODEL), lambda m: (0, 0))],
        out_specs=[tok] * 5 + [pl.BlockSpec((D_MODEL, tm), lambda m: (0, m))],
        compiler_params=_params([((tm, D_MODEL), BF16)] * 5 + [((tm, D_MODEL), F32)] * 6 + [((D_MODEL, D_MODEL), BF16)] * 3,
                                temps=8 << 20, sem=("arbitrary",)),
    )(a, og, proj, proj, x, w_a, w_b, w_out, ffn_g)


def _ffn_forward(h2, x1, w_gu, w_down, target, final_g):
    t = x1.shape[0]
    tm = _tile(t, 512)

    def body(h_ref, wg_ref, wu_ref, wd_ref, x1_ref, t_ref, g_ref, gu_ref, act_ref, loss_ref, dg_ref, dx_ref, dxb_ref, acc):
        m, j = pl.program_id(0), pl.program_id(1)

        @pl.when((m == 0) & (j == 0))
        def _():
            loss_ref[...] = jnp.zeros_like(loss_ref)
            dg_ref[...] = jnp.zeros_like(dg_ref)

        h = h_ref[...]
        gate = _dot(h, wg_ref[...])
        up = _dot(h, wu_ref[...])
        gu_ref[0] = gate
        gu_ref[1] = up
        act = ((gate * _sigmoid(gate)) * up).astype(BF16)
        act_ref[...] = act
        part = _dot(act, wd_ref[...])

        @pl.when(j == 0)
        def _():
            acc[...] = part

        @pl.when((j > 0) & (j < 3))
        def _():
            acc[...] += part

        @pl.when(j == 3)
        def _():
            x2 = x1_ref[...] + (acc[...] + part)
            g = g_ref[...]
            r, xh = _rms_stats(x2)
            err = xh * g - t_ref[...]
            loss_ref[...] += 0.5 * jnp.sum(jnp.mean(err * err, axis=-1, keepdims=True), axis=0, keepdims=True)
            dy = err * (1.0 / D_MODEL)
            dg_ref[...] += jnp.sum(dy * xh, axis=0, keepdims=True)
            dxh = dy * g
            dx = r * (dxh - xh * jnp.mean(dxh * xh, axis=-1, keepdims=True))
            dx_ref[...] = dx
            dxb_ref[...] = dx.astype(BF16)

    tok = pl.BlockSpec((tm, D_MODEL), lambda m, j: (m, 0))
    vec = pl.BlockSpec((1, D_MODEL), lambda m, j: (0, 0))
    return pl.pallas_call(
        body, name="ffn_fwd",
        out_shape=[jax.ShapeDtypeStruct((4, 2, t, FF_BLOCK), F32), jax.ShapeDtypeStruct((4, t, FF_BLOCK), BF16),
                   jax.ShapeDtypeStruct((8, 128), F32), jax.ShapeDtypeStruct((1, D_MODEL), F32),
                   jax.ShapeDtypeStruct((t, D_MODEL), F32), jax.ShapeDtypeStruct((t, D_MODEL), BF16)],
        grid=(t // tm, 4),
        in_specs=[tok, pl.BlockSpec((None, D_MODEL, FF_BLOCK), lambda m, j: (j, 0, 0)),
                  pl.BlockSpec((None, D_MODEL, FF_BLOCK), lambda m, j: (j + 4, 0, 0)),
                  pl.BlockSpec((FF_BLOCK, D_MODEL), lambda m, j: (j, 0)), tok, tok, vec],
        out_specs=[pl.BlockSpec((None, 2, tm, FF_BLOCK), lambda m, j: (j, 0, m, 0)),
                   pl.BlockSpec((None, tm, FF_BLOCK), lambda m, j: (j, m, 0)),
                   pl.BlockSpec((8, 128), lambda m, j: (0, 0)), vec, tok, tok],
        scratch_shapes=[pltpu.VMEM((tm, D_MODEL), F32)],
        compiler_params=_params([((tm, D_MODEL), BF16), ((D_MODEL, 768), BF16), ((D_MODEL, 768), BF16),
                                 ((FF_BLOCK, D_MODEL), BF16), ((tm, D_MODEL), F32), ((tm, D_MODEL), F32), ((2, tm, 768), F32),
                                 ((tm, 768), BF16), ((tm, D_MODEL), F32), ((tm, D_MODEL), BF16)],
                                scratch=[((tm, D_MODEL), F32)], temps=8 << 20, sem=("arbitrary", "arbitrary")),
    )(h2, w_gu, w_gu, w_down, x1, target, final_g)


def _ffn_backward(dx2b, dx2, gu, x1, w_gu, w_down, ffn_g):
    t = x1.shape[0]
    tm = _tile(t, 512)

    def body(dxb_ref, dx2_ref, gu_ref, x1_ref, wg_ref, wu_ref, wd_ref, g_ref, dgu_ref, dx1_ref, dx1b_ref, dg_ref, acc):
        m, j = pl.program_id(0), pl.program_id(1)

        @pl.when((m == 0) & (j == 0))
        def _():
            dg_ref[...] = jnp.zeros_like(dg_ref)

        dact = _dot_nt(dxb_ref[...], wd_ref[...])
        gate, up = gu_ref[0], gu_ref[1]
        sg = _sigmoid(gate)
        dgate = (dact * up * (sg * (1.0 + gate * (1.0 - sg)))).astype(BF16)
        dup = (dact * (gate * sg)).astype(BF16)
        dgu_ref[0] = dgate
        dgu_ref[1] = dup
        part = _dot_nt(dgate, wg_ref[...]) + _dot_nt(dup, wu_ref[...])

        @pl.when(j == 0)
        def _():
            acc[...] = part

        @pl.when(j > 0)
        def _():
            acc[...] += part

        @pl.when(j == 3)
        def _():
            dx, dg = _rms_bwd(acc[...], x1_ref[...], g_ref[...])
            dx1 = dx2_ref[...] + dx
            dx1_ref[...] = dx1
            dx1b_ref[...] = dx1.astype(BF16)
            dg_ref[...] += dg

    tok = pl.BlockSpec((tm, D_MODEL), lambda m, j: (m, 0))
    vec = pl.BlockSpec((1, D_MODEL), lambda m, j: (0, 0))
    gu_spec = pl.BlockSpec((None, 2, tm, FF_BLOCK), lambda m, j: (j, 0, m, 0))
    return pl.pallas_call(
        body, name="ffn_bwd",
        out_shape=[jax.ShapeDtypeStruct((4, 2, t, FF_BLOCK), BF16), jax.ShapeDtypeStruct((t, D_MODEL), F32),
                   jax.ShapeDtypeStruct((t, D_MODEL), BF16), jax.ShapeDtypeStruct((1, D_MODEL), F32)],
        grid=(t // tm, 4),
        in_specs=[tok, tok, gu_spec, tok, pl.BlockSpec((None, D_MODEL, FF_BLOCK), lambda m, j: (j, 0, 0)),
                  pl.BlockSpec((None, D_MODEL, FF_BLOCK), lambda m, j: (j + 4, 0, 0)),
                  pl.BlockSpec((FF_BLOCK, D_MODEL), lambda m, j: (j, 0)), vec],
        out_specs=[gu_spec, tok, tok, vec],
        scratch_shapes=[pltpu.VMEM((tm, D_MODEL), F32)],
        compiler_params=_params([((tm, D_MODEL), BF16), ((tm, D_MODEL), F32), ((2, tm, 768), F32), ((tm, D_MODEL), F32),
                                 ((D_MODEL, 768), BF16), ((D_MODEL, 768), BF16), ((FF_BLOCK, D_MODEL), BF16),
                                 ((2, tm, 768), BF16), ((tm, D_MODEL), F32), ((tm, D_MODEL), BF16)],
                                scratch=[((tm, D_MODEL), F32)], temps=8 << 20, sem=("arbitrary", "arbitrary")),
    )(dx2b, dx2, gu, x1, w_gu, w_gu, w_down, ffn_g)


def _branch_out_backward(dx1b, ya, yb, proj, w_a, w_b, w_out):
    t = ya.shape[0]
    tm = _tile(t, 256)

    def body(dx_ref, ya_ref, yb_ref, ga_ref, gb_ref, wa_ref, wb_ref, wo_ref, dya_ref, dyb_ref, dgate_ref, da_ref, dog_ref):
        dm = _dot_nt(dx_ref[...], wo_ref[...])
        sa, sb = _sigmoid(ga_ref[...]), _sigmoid(gb_ref[...])
        dya = (dm * sa).astype(BF16)
        dyb = (dm * sb).astype(BF16)
        dya_ref[...] = dya
        dyb_ref[...] = dyb
        dgate_ref[0] = (dm * ya_ref[...] * (sa * (1.0 - sa))).astype(BF16)
        dgate_ref[1] = (dm * yb_ref[...] * (sb * (1.0 - sb))).astype(BF16)
        da_ref[...] = _dot_nt(dya, wa_ref[...])
        dog_ref[...] = _dot_nt(dyb, wb_ref[...])

    tok = pl.BlockSpec((tm, D_MODEL), lambda m: (m, 0))
    wsp = pl.BlockSpec((D_MODEL, D_MODEL), lambda m: (0, 0))
    return pl.pallas_call(
        body, name="branch_out_bwd",
        out_shape=[jax.ShapeDtypeStruct((t, D_MODEL), BF16), jax.ShapeDtypeStruct((t, D_MODEL), BF16),
                   jax.ShapeDtypeStruct((N_DEV, t, D_MODEL), BF16), jax.ShapeDtypeStruct((t, D_MODEL), F32),
                   jax.ShapeDtypeStruct((t, D_MODEL), F32)],
        grid=(t // tm,),
        in_specs=[tok, tok, tok, pl.BlockSpec((None, tm, D_MODEL), lambda m: (GATE_POS, m, 0)),
                  pl.BlockSpec((None, tm, D_MODEL), lambda m: (GATE_POS + 1, m, 0)), wsp, wsp, wsp],
        out_specs=[tok, tok, pl.BlockSpec((2, tm, D_MODEL), lambda m: (GATE_POS // 2, m, 0)), tok, tok],
        compiler_params=_params([((tm, D_MODEL), BF16)] * 5 + [((tm, D_MODEL), F32)] * 6 + [((D_MODEL, D_MODEL), BF16)] * 3,
                                temps=8 << 20, sem=("arbitrary",)),
    )(dx1b, ya, yb, proj, proj, w_a, w_b, w_out)


def _hgrn_backward(dproj, dog, o_saved, states, proj, lb_table, norm_g):
    t = proj.shape[1]
    tb = _tile(t, 1024)
    nc = tb // HGRN_CHUNK
    nb = t // tb

    def body(_, dog_ref, o_ref, st_ref, q_ref, f_ref, i_ref, g_ref, tab_ref, ng_ref, dp_ref, dng_ref, dtab_ref, gstate):
        @pl.when(pl.program_id(1) == 0)
        def _():
            gstate[...] = jnp.zeros_like(gstate)
            dng_ref[...] = jnp.zeros_like(dng_ref)
            dtab_ref[...] = jnp.zeros_like(dtab_ref)

        lb = _lower_bound(tab_ref)
        ng = ng_ref[...]
        lower, upper = _tri_masks()
        gt = _hgrn_gates(q_ref[...], f_ref[...], lb, nc)
        qi, ki, kd, qe = (gt[n].astype(BF16) for n in ("qi", "ki", "kd", "qe"))
        vb = i_ref[...].astype(BF16)
        o, gz, d_og = o_ref[...], g_ref[...], dog_ref[...]
        r, oh = _rms_stats(o)
        sg = _sigmoid(gz)
        d_on = d_og * (gz * sg)
        dgz = d_og * (oh * ng) * (sg * (1.0 + gz * (1.0 - sg)))
        dng_ref[...] += jnp.sum(d_on * oh, axis=0, keepdims=True)
        doh = d_on * ng
        dob = (r * (doh - oh * jnp.mean(doh * oh, axis=-1, keepdims=True))).astype(BF16)
        dv_intra, dqi, dki, dqe, g_upd = [], [], [], [], []
        for c in range(nc):
            rows = _chunk_rows(c)
            p = jnp.where(lower, _dot_nt(qi[rows], ki[rows]), 0.0).astype(BF16)
            dv_intra.append(_dot_tn(p, dob[rows]))
            dp = jnp.where(lower, _dot_nt(dob[rows], vb[rows]), 0.0).astype(BF16)
            dqi.append(_dot(dp, ki[rows]))
            dki.append(_dot_tn(dp, qi[rows]))
            dqe.append(_dot(dob[rows], st_ref[c].astype(BF16)))
            g_upd.append(_dot_tn(dob[rows], qe[rows]))
        g_after = [None] * nc
        g = gstate[...]
        for c in reversed(range(nc)):
            g_after[c] = g
            g = g * gt["decay"][c] + g_upd[c]
        gstate[...] = g
        dkd, dv, da_last = [], [], []
        for c in range(nc):
            rows = _chunk_rows(c)
            gb = g_after[c].astype(BF16)
            dkd.append(_dot(vb[rows], gb))
            dv.append(dv_intra[c] + _dot_nt(kd[rows], gb))
            da_last.append(jnp.sum(g_after[c] * st_ref[c], axis=0, keepdims=True) * gt["decay"][c])
        dqi, dki, dqe, dkd, dv = (jnp.concatenate(z, axis=0) for z in (dqi, dki, dqe, dkd, dv))
        dqs = dqi * gt["e_in"] + dqe * gt["e_all"]
        dk = dki * gt["e_out"] + dkd * gt["e_end"]
        t_in, t_out, t_end = dqi * gt["qi"], dki * gt["ki"], dkd * gt["kd"]
        da = t_in - t_out + dqe * gt["qe"] - t_end
        row = lax.broadcasted_iota(jnp.int32, (HGRN_CHUNK, HEAD_DIM), 0)
        d_mid = t_out - t_in
        pieces = []
        for c in range(nc):
            rows = _chunk_rows(c)
            da_mid = jnp.sum(d_mid[rows], axis=0, keepdims=True)
            da_end = jnp.sum(t_end[rows], axis=0, keepdims=True) + da_last[c]
            da_c = da[rows] + jnp.where(row == HGRN_CHUNK // 2 - 1, da_mid, 0.0) + jnp.where(row == HGRN_CHUNK - 1, da_end, 0.0)
            pieces.append(_mask_mm(upper.astype(BF16), da_c))
        df = jnp.concatenate(pieces, axis=0) / gt["f"] - dk
        s = gt["s"]
        dlb = jnp.sum(df * (1.0 - s), axis=0, keepdims=True)
        dp_ref[0] = (dqs * HGRN_SCALE).astype(BF16)
        dp_ref[1] = (df * (1.0 - lb) * (s * (1.0 - s))).astype(BF16)
        dp_ref[2] = dv.astype(BF16)
        dp_ref[3] = dgz.astype(BF16)
        dt0 = dlb * (lb * (1.0 - lb))
        dtab_ref[0:1, :] += dt0
        dtab_ref[1:2, :] -= dt0

    def blk(p):
        return pl.BlockSpec((None, tb, HEAD_DIM), lambda h, n: (p, nb - 1 - n, h))

    tok = pl.BlockSpec((tb, HEAD_DIM), lambda h, n: (nb - 1 - n, h))
    return pl.pallas_call(
        body, name="hgrn_bwd",
        out_shape=[jax.ShapeDtypeStruct((N_DEV, t, D_MODEL), BF16), jax.ShapeDtypeStruct((1, D_MODEL), F32),
                   jax.ShapeDtypeStruct((2, D_MODEL), F32)],
        grid=(HEADS, nb),
        in_specs=[ANY, tok, tok, pl.BlockSpec((None, nc, HEAD_DIM, HEAD_DIM), lambda h, n: (h, nb - 1 - n, 0, 0)),
                  blk(Q_POS), blk(Q_POS + 1), blk(Q_POS + 2), blk(Q_POS + 3),
                  pl.BlockSpec((2, HEAD_DIM), lambda h, n: (0, h)), pl.BlockSpec((1, HEAD_DIM), lambda h, n: (0, h))],
        out_specs=[pl.BlockSpec((4, tb, HEAD_DIM), lambda h, n: (0, nb - 1 - n, h)),
                   pl.BlockSpec((1, HEAD_DIM), lambda h, n: (0, h)), pl.BlockSpec((2, HEAD_DIM), lambda h, n: (0, h))],
        scratch_shapes=[pltpu.VMEM((HEAD_DIM, HEAD_DIM), F32)],
        input_output_aliases={0: 0},
        compiler_params=_params([((tb, HEAD_DIM), F32)] * 6 + [((nc, HEAD_DIM, HEAD_DIM), F32)] + [((4, tb, HEAD_DIM), BF16)],
                                temps=8 << 20, sem=("arbitrary", "arbitrary")),
    )(dproj, dog, o_saved, states, proj, proj, proj, proj, lb_table, norm_g)


def _gmlp_backward(dproj, da, proj, ln_g, ln_b, w_s, bias_b):
    t = proj.shape[1]
    tm = _tile(t, 256)
    chunks = tm // GMLP_CHUNK

    def body(_, da_ref, u_ref, v_ref, lng_ref, lnb_ref, ws_ref, bias_ref, dp_ref, dlng_ref, dlnb_ref, dws_ref, dbs_ref,
             vn_scr, dvn_scr):
        @pl.when(pl.program_id(0) == 0)
        def _():
            dlng_ref[...] = jnp.zeros_like(dlng_ref)
            dlnb_ref[...] = jnp.zeros_like(dlnb_ref)
            dws_ref[...] = jnp.zeros_like(dws_ref)
            dbs_ref[...] = jnp.zeros_like(dbs_ref)

        v = v_ref[...]
        vv = _gelu(v)
        mu = jnp.mean(vv, axis=-1, keepdims=True)
        cen = vv - mu
        rstd = lax.rsqrt(jnp.mean(cen * cen, axis=-1, keepdims=True) + NORM_EPS)
        vhat = cen * rstd
        lng = lng_ref[...]
        vn_scr[...] = (vhat * lng + lnb_ref[...]).astype(BF16)
        row = lax.broadcasted_iota(jnp.int32, (GMLP_CHUNK, GMLP_CHUNK), 0)
        col = lax.broadcasted_iota(jnp.int32, (GMLP_CHUNK, GMLP_CHUNK), 1)
        for g in range(GROUPS):
            wm = _masked_ws(ws_ref, g)
            cols = slice(g * HEAD_DIM, (g + 1) * HEAD_DIM)
            dws = jnp.zeros((GMLP_CHUNK, GMLP_CHUNK), F32)
            dbs = jnp.zeros((GMLP_CHUNK, GMLP_CHUNK), F32)
            for c in range(chunks):
                rows = slice(c * GMLP_CHUNK, (c + 1) * GMLP_CHUNK)
                vn = vn_scr[rows, cols]
                mixed = _dot(wm, vn) + bias_ref[g]
                u = u_ref[rows, cols]
                d_a = da_ref[rows, cols]
                dp_ref[0, rows, cols] = (d_a * mixed * _gelu_grad(u)).astype(BF16)
                dmix = d_a * _gelu(u)
                dmb = dmix.astype(BF16)
                dbs = dbs + dmix
                dws = dws + _dot_nt(dmb, vn)
                dvn_scr[rows, cols] = _dot_tn(wm, dmb)
            dws_ref[g] += jnp.where(row >= col, dws, 0.0)
            dbs_ref[g] += jnp.broadcast_to(jnp.sum(dbs, axis=-1, keepdims=True), (GMLP_CHUNK, GMLP_CHUNK))
        dvn = dvn_scr[...]
        dlng_ref[...] += jnp.sum(dvn * vhat, axis=0, keepdims=True)
        dlnb_ref[...] += jnp.sum(dvn, axis=0, keepdims=True)
        dvh = dvn * lng
        dvv = rstd * (dvh - jnp.mean(dvh, axis=-1, keepdims=True) - vhat * jnp.mean(dvh * vhat, axis=-1, keepdims=True))
        dp_ref[1] = (dvv * _gelu_grad(v)).astype(BF16)

    tok = pl.BlockSpec((tm, D_MODEL), lambda m: (m, 0))
    small = pl.BlockSpec((GROUPS, GMLP_CHUNK, GMLP_CHUNK), lambda m: (0, 0, 0))
    vec = pl.BlockSpec((1, D_MODEL), lambda m: (0, 0))
    return pl.pallas_call(
        body, name="gmlp_bwd",
        out_shape=[jax.ShapeDtypeStruct(dproj.shape, BF16), jax.ShapeDtypeStruct((1, D_MODEL), F32),
                   jax.ShapeDtypeStruct((1, D_MODEL), F32), jax.ShapeDtypeStruct((GROUPS, GMLP_CHUNK, GMLP_CHUNK), F32),
                   jax.ShapeDtypeStruct((GROUPS, GMLP_CHUNK, GMLP_CHUNK), F32)],
        grid=(t // tm,),
        in_specs=[ANY, tok, pl.BlockSpec((None, tm, D_MODEL), lambda m: (U_POS, m, 0)),
                  pl.BlockSpec((None, tm, D_MODEL), lambda m: (U_POS + 1, m, 0)), vec, vec, small, small],
        out_specs=[pl.BlockSpec((2, tm, D_MODEL), lambda m: (U_POS // 2, m, 0)), vec, vec, small, small],
        scratch_shapes=[pltpu.VMEM((tm, D_MODEL), BF16), pltpu.VMEM((tm, D_MODEL), F32)],
        input_output_aliases={0: 0},
        compiler_params=_params([((tm, D_MODEL), F32)] * 3 + [((2, tm, D_MODEL), BF16)] + [((8, 128, 128), F32)] * 4,
                                scratch=[((tm, D_MODEL), BF16), ((tm, D_MODEL), F32)], temps=12 << 20, sem=("arbitrary",)),
    )(dproj, da, proj, proj, ln_g, ln_b, w_s, bias_b)


def _input_backward(dproj, w_in_g, x, dx1, mix_g):
    t = x.shape[0]
    tm = _tile(t, 256)

    def body(dp_ref, w_ref, x_ref, dx1_ref, g_ref, dx_ref, dg_ref):
        @pl.when(pl.program_id(0) == 0)
        def _():
            dg_ref[...] = jnp.zeros_like(dg_ref)

        dh = _dot_nt(dp_ref[0], w_ref[0])
        for p in range(1, N_DEV):
            dh = dh + _dot_nt(dp_ref[p], w_ref[p])
        dx, dg = _rms_bwd(dh, x_ref[...], g_ref[...])
        dx_ref[...] = dx1_ref[...] + dx
        dg_ref[...] += dg

    tok = pl.BlockSpec((tm, D_MODEL), lambda m: (m, 0))
    vec = pl.BlockSpec((1, D_MODEL), lambda m: (0, 0))
    return pl.pallas_call(
        body, name="input_bwd",
        out_shape=[jax.ShapeDtypeStruct((t, D_MODEL), F32), jax.ShapeDtypeStruct((1, D_MODEL), F32)],
        grid=(t // tm,),
        in_specs=[pl.BlockSpec((N_DEV, tm, D_MODEL), lambda m: (0, m, 0)),
                  pl.BlockSpec((N_DEV, D_MODEL, D_MODEL), lambda m: (0, 0, 0)), tok, tok, vec],
        out_specs=[tok, vec],
        compiler_params=_params([((N_DEV, tm, D_MODEL), BF16), ((N_DEV, D_MODEL, D_MODEL), BF16)] + [((tm, D_MODEL), F32)] * 3,
                                temps=4 << 20, sem=("arbitrary",)),
    )(dproj, w_in_g, x, dx1, mix_g)


def _weight_grad(name, a, b, a_spec, b_spec, out_shape, out_spec, grid, blocks):
    def body(a_ref, b_ref, o_ref):
        part = _dot_tn(a_ref[...], b_ref[...])
        m = pl.program_id(len(grid) - 1)

        @pl.when(m == 0)
        def _():
            o_ref[...] = part

        @pl.when(m > 0)
        def _():
            o_ref[...] += part

    return pl.pallas_call(
        body, name=name, out_shape=jax.ShapeDtypeStruct(out_shape, F32), grid=grid, in_specs=[a_spec, b_spec],
        out_specs=out_spec, compiler_params=_params(blocks, temps=8 << 20, sem=("arbitrary",) * len(grid)),
    )(a, b)


def _weight_grad_whole(name, a_t, b, b_spec, out_shape, out_spec, steps, blocks):
    def body(a_ref, b_ref, o_ref):
        o_ref[...] = _dot(a_ref[...], b_ref[...])

    return pl.pallas_call(
        body, name=name, out_shape=jax.ShapeDtypeStruct(out_shape, F32), grid=(steps,),
        in_specs=[pl.BlockSpec(a_t.shape, lambda j: (0, 0)), b_spec], out_specs=out_spec,
        compiler_params=_params([(a_t.shape, BF16)] + blocks, sem=("arbitrary",)),
    )(a_t, b)


def _pack_small(mix_g, ln_g, ln_b, w_s, b_s, lb_table, hg_norm, ffn_g, final_g):
    def part(a):
        a = a.reshape(-1, D_MODEL)
        return jnp.pad(a, ((0, 8 - a.shape[0]), (0, 0)))

    return jnp.concatenate([part(mix_g), part(ln_g), part(ln_b), part(hg_norm), part(ffn_g), part(final_g),
                            part(lb_table), part(b_s), w_s.reshape(GMLP_CHUNK, D_MODEL)], axis=0)


def _unpack_small(pack):
    return dict(norm_mix_g=pack[0:1], gmlp_ln_g=pack[8:9], gmlp_ln_b=pack[16:17], hgrn_norm_g=pack[24:25],
                norm_ffn_g=pack[32:33], norm_final_g=pack[40], hgrn_lb_table=pack[48:50],
                gmlp_b_s=pack[56:57].reshape(1, GROUPS, GMLP_CHUNK),
                gmlp_w_s=pack[64:192].reshape(1, GROUPS, GMLP_CHUNK, GMLP_CHUNK))


def _adamw_small(name, gathered, w, m, v):
    rows = w.shape[0]

    def body(p_ref, w_ref, m_ref, v_ref, g_out, d_out, m_out, v_out):
        g = p_ref[0]
        for j in range(1, N_DEV):
            g = g + p_ref[j]
        delta, m_new, v_new = _adamw_math(w_ref[...], g, m_ref[...], v_ref[...])
        g_out[...] = g
        d_out[...] = delta
        m_out[...] = m_new
        v_out[...] = v_new

    tr = _tile(rows, 64)
    spec = pl.BlockSpec((tr, D_MODEL), lambda r: (r, 0))
    return pl.pallas_call(
        body, name=name, out_shape=[jax.ShapeDtypeStruct((rows, D_MODEL), F32)] * 4, grid=(rows // tr,),
        in_specs=[pl.BlockSpec((N_DEV, tr, D_MODEL), lambda r: (0, r, 0)), spec, spec, spec], out_specs=[spec] * 4,
        compiler_params=_params([((N_DEV, tr, D_MODEL), F32)] + [((tr, D_MODEL), F32)] * 7, sem=("arbitrary",)),
    )(gathered, w, m, v)


def kernel(x, norm_mix_g, w_in, gmlp_ln_g, gmlp_ln_b, gmlp_w_s, gmlp_b_s, hgrn_lb_table, hgrn_norm_g, w_branch_a, w_branch_b, w_out, norm_ffn_g, w_gate_up, w_down, norm_final_g, loss_target, m_norm_mix_g, m_w_in, m_gmlp_ln_g, m_gmlp_ln_b, m_gmlp_w_s, m_gmlp_b_s, m_hgrn_lb_table, m_hgrn_norm_g, m_w_branch_a, m_w_branch_b, m_w_out, m_norm_ffn_g, m_w_gate_up, m_w_down, m_norm_final_g, v_norm_mix_g, v_w_in, v_gmlp_ln_g, v_gmlp_ln_b, v_gmlp_w_s, v_gmlp_b_s, v_hgrn_lb_table, v_hgrn_norm_g, v_w_branch_a, v_w_branch_b, v_w_out, v_norm_ffn_g, v_w_gate_up, v_w_down, v_norm_final_g):
    t = x.shape[1]
    x2d = x.reshape(t, D_MODEL)
    target = loss_target.reshape(t, D_MODEL)
    final_g = norm_final_g.reshape(1, D_MODEL)

    shards = [w_in[0].astype(BF16), w_branch_a[0].astype(BF16), w_branch_b[0].astype(BF16), w_out[0].astype(BF16),
              w_gate_up[0].astype(BF16), w_down[0].astype(BF16)]

    def rows_of(n):
        return lambda ref, j: ref.at[pl.ds(pl.multiple_of(j * n, 8), n)]

    gathered = [((N_DEV, D_MODEL, D_MODEL), BF16), ((D_MODEL, D_MODEL), BF16), ((D_MODEL, D_MODEL), BF16),
                ((D_MODEL, D_MODEL), BF16), ((N_DEV, D_MODEL, FF_BLOCK), BF16), ((D_FF, D_MODEL), BF16)]
    places = [lambda ref, j: ref.at[_pos_of_dev(j)], rows_of(BRANCH_ROWS), rows_of(BRANCH_ROWS), rows_of(BRANCH_ROWS),
              lambda ref, j: ref.at[j], rows_of(DOWN_ROWS)]
    (w_in_g,) = _all_gather("w_in_all_gather", shards[:1], gathered[:1], places[:1])
    _, later = lax.optimization_barrier((w_in_g, shards[1:]))
    w_a, w_b, w_o, w_gu, w_dn = _all_gather_async("weights_all_gather", 0, later, gathered[1:], places[1:])

    proj, h, h_t = _proj_forward(x2d, norm_mix_g, w_in_g)
    bias_b = jnp.broadcast_to(gmlp_b_s[0][:, :, None], (GROUPS, GMLP_CHUNK, GMLP_CHUNK))
    a = _gmlp_forward(proj, gmlp_ln_g, gmlp_ln_b, gmlp_w_s[0], bias_b)
    og, o_saved, states = _hgrn_forward(proj, hgrn_lb_table, hgrn_norm_g)
    ya, yb, merged, x1, h2, h2_t = _branch_out_forward(a, og, proj, x2d, w_a, w_b, w_o, norm_ffn_g)
    gu, act, loss_tile, d_final_g, dx2, dx2b = _ffn_forward(h2, x1, w_gu, w_dn, target, final_g)

    core = lax.axis_index("c").astype(jnp.int32).reshape(1)
    chip = (2 * lax.axis_index("x") + lax.axis_index("y")).astype(jnp.int32).reshape(1)
    branch_rows, branch_shape = rows_of(BRANCH_ROWS), (BRANCH_ROWS, D_MODEL)
    branch_block = ((BRANCH_ROWS, D_MODEL), lambda q, r, c: (2 * q + c, 0))

    def chip_partials(names, grads, land, own_blocks):
        return [_chip_partial("chip_partial_" + nme, core, g_, blk, idx, l_)
                for nme, g_, (blk, idx), l_ in zip(names, grads, own_blocks, land)]

    tm = _tile(t, 512)
    nm = t // tm
    tok_a = pl.BlockSpec((tm, D_MODEL), lambda m: (m, 0))
    full_o = pl.BlockSpec((D_MODEL, D_MODEL), lambda m: (0, 0))
    sq_blocks = [((tm, D_MODEL), BF16)] * 2 + [((D_MODEL, D_MODEL), F32)]

    dgu, dx1, dx1b, d_ffn_g = _ffn_backward(dx2b, dx2, gu, x1, w_gu, w_dn, norm_ffn_g)
    g_gu = _weight_grad_whole(
        "grad_w_gate_up", h2_t, dgu, pl.BlockSpec((None, None, t, FF_BLOCK), lambda j: (j % 4, j // 4, 0, 0)),
        (N_DEV, D_MODEL, FF_BLOCK), pl.BlockSpec((None, D_MODEL, FF_BLOCK), lambda j: (j, 0, 0)), N_DEV,
        [((t, 768), BF16), ((D_MODEL, 768), F32)])
    g_dn = _weight_grad(
        "grad_w_down", act, dx2b, pl.BlockSpec((None, tm, FF_BLOCK), lambda j, m: (j, m, 0)),
        pl.BlockSpec((tm, D_MODEL), lambda j, m: (m, 0)), (D_FF, D_MODEL),
        pl.BlockSpec((FF_BLOCK, D_MODEL), lambda j, m: (j, 0)), (4, nm),
        [((tm, 768), BF16), ((tm, D_MODEL), BF16), ((FF_BLOCK, D_MODEL), F32)])
    names_f, grads_f = ["w_gate_up", "w_down"], [g_gu, g_dn]
    land_f = _exchange_sibling("ffn_grads_to_sibling", 2, grads_f, [lambda ref, j: ref.at[j], rows_of(DOWN_ROWS)],
                               [(D_MODEL, FF_BLOCK), (DOWN_ROWS, D_MODEL)])

    dya, dyb, dproj, da, dog = _branch_out_backward(dx1b, ya, yb, proj, w_a, w_b, w_o)
    g_a = _weight_grad("grad_w_a", a, dya, tok_a, tok_a, (D_MODEL, D_MODEL), full_o, (nm,), sq_blocks)
    g_b = _weight_grad("grad_w_b", og, dyb, tok_a, tok_a, (D_MODEL, D_MODEL), full_o, (nm,), sq_blocks)
    g_o = _weight_grad("grad_w_out", merged, dx1b, tok_a, tok_a, (D_MODEL, D_MODEL), full_o, (nm,), sq_blocks)
    names_b, grads_b = ["w_branch_a", "w_branch_b", "w_out"], [g_a, g_b, g_o]
    land_b = _exchange_sibling("branch_grads_to_sibling", 3, grads_b, [branch_rows] * 3, [branch_shape] * 3)

    part_f = chip_partials(names_f, grads_f, land_f,
                           [((None, 256, FF_BLOCK), lambda q, r, c: (2 * q + c, r, 0)),
                            ((DOWN_ROWS // 2, D_MODEL), lambda q, r, c: (2 * (2 * q + c) + r, 0))])
    landed_f = _exchange_chips("ffn_grads_to_chips", 5, part_f)

    dog, _ = lax.optimization_barrier((dog, part_f))
    dproj, d_hg_norm, d_lb = _hgrn_backward(dproj, dog, o_saved, states, proj, hgrn_lb_table, hgrn_norm_g)

    part_b = chip_partials(names_b, grads_b, land_b, [branch_block] * 3)
    landed_b = _exchange_chips("branch_grads_to_chips", 6, part_b)

    da, _ = lax.optimization_barrier((da, part_b))
    dproj, d_ln_g, d_ln_b, d_ws, d_bs = _gmlp_backward(dproj, da, proj, gmlp_ln_g, gmlp_ln_b, gmlp_w_s[0], bias_b)

    def packed(vals):
        return _pack_small(*vals)

    w_pack = packed([norm_mix_g, gmlp_ln_g, gmlp_ln_b, gmlp_w_s, gmlp_b_s, hgrn_lb_table, hgrn_norm_g, norm_ffn_g, norm_final_g])
    m_pack = packed([m_norm_mix_g, m_gmlp_ln_g, m_gmlp_ln_b, m_gmlp_w_s, m_gmlp_b_s, m_hgrn_lb_table, m_hgrn_norm_g, m_norm_ffn_g, m_norm_final_g])
    v_pack = packed([v_norm_mix_g, v_gmlp_ln_g, v_gmlp_ln_b, v_gmlp_w_s, v_gmlp_b_s, v_hgrn_lb_table, v_hgrn_norm_g, v_norm_ffn_g, v_norm_final_g])
    small_partial = _pack_small(jnp.zeros((1, D_MODEL), F32), d_ln_g, d_ln_b, d_ws, d_bs[:, :, 0], d_lb, d_hg_norm, d_ffn_g,
                                d_final_g)
    (small_all,) = _all_gather_async("small_grads_all_gather", 1, [small_partial],
                                     [((N_DEV, SMALL_ROWS, D_MODEL), F32)], [lambda ref, j: ref.at[j]])

    g_in = _weight_grad_whole(
        "grad_w_in", h_t, dproj, pl.BlockSpec((None, t, D_MODEL), lambda p: (p, 0, 0)), (N_DEV, D_MODEL, D_MODEL),
        pl.BlockSpec((None, D_MODEL, D_MODEL), lambda p: (p, 0, 0)), N_DEV, [((t, D_MODEL), BF16), ((D_MODEL, D_MODEL), F32)])
    land_i = _exchange_sibling("w_in_grads_to_sibling", 4, [g_in], [lambda ref, j: ref.at[_pos_of_dev(j)]],
                               [(D_MODEL, D_MODEL)])

    big = {}
    for nme, own, lnd, w, m, v in zip(
            names_f + names_b, part_f + part_b, landed_f + landed_b,
            [w_gate_up, w_down, w_branch_a, w_branch_b, w_out], [m_w_gate_up, m_w_down, m_w_branch_a, m_w_branch_b, m_w_out],
            [v_w_gate_up, v_w_down, v_w_branch_a, v_w_branch_b, v_w_out]):
        big[nme] = [o_[None] for o_ in _adamw("adamw_" + nme, chip, own, lnd, w[0], m[0], v[0])]
    small_outs = _adamw_small("adamw_small", small_all, w_pack, m_pack, v_pack)
    land_i, _ = lax.optimization_barrier((land_i, (big, small_outs)))
    part_i = chip_partials(["w_in"], [g_in], land_i,
                           [((None, 256, D_MODEL), lambda q, r, c: (_pos_of_dev(2 * q + c), r, 0))])
    landed_i = _exchange_chips("w_in_grads_to_chips", 7, part_i)

    dx1, _ = lax.optimization_barrier((dx1, part_i))
    grad_x, d_mix_g = _input_backward(dproj, w_in_g, x2d, dx1, norm_mix_g)
    big["w_in"] = [o_[None] for o_ in _adamw("adamw_w_in", chip, part_i[0], landed_i[0], w_in[0], m_w_in[0], v_w_in[0])]

    def row8(a):
        return jnp.pad(a, ((0, 7), (0, 0)))

    d_mix_g, _ = lax.optimization_barrier((d_mix_g, landed_i))
    (mix_all,) = _all_gather_async("mix_gain_grad_all_gather", 8, [row8(d_mix_g)], [((N_DEV, 8, D_MODEL), F32)],
                                   [lambda ref, j: ref.at[j]])
    mix_outs = _adamw_small("adamw_mix_gain", mix_all, row8(norm_mix_g), row8(m_norm_mix_g), row8(v_norm_mix_g))
    small = [dict(_unpack_small(p), norm_mix_g=q[0:1]) for p, q in zip(small_outs, mix_outs)]

    loss = lax.psum(loss_tile[0, 0], ("x", "y", "c"))
    order = ["norm_mix_g", "w_in", "gmlp_ln_g", "gmlp_ln_b", "gmlp_w_s", "gmlp_b_s", "hgrn_lb_table", "hgrn_norm_g",
             "w_branch_a", "w_branch_b", "w_out", "norm_ffn_g", "w_gate_up", "w_down", "norm_final_g"]
    outs = [loss, grad_x.reshape(1, t, D_MODEL)]
    for kind in range(4):
        for nme in order:
            outs.append(big[nme][kind] if nme in big else small[kind][nme])
    return tuple(outs)
```

```python
import functools

import jax
import jax.numpy as jnp
from jax import lax
from jax.experimental import pallas as pl
from jax.experimental.pallas import tpu as pltpu
from jax.experimental.pallas import tpu_sc as plsc

F32, BF16 = jnp.float32, jnp.bfloat16
D_MODEL = 1024
N_DEV = 8
HEADS = 8
HEAD_DIM = 128
GROUPS = 8
GMLP_CHUNK = 128
HGRN_CHUNK = 64
HGRN_SCALE = HEAD_DIM ** -0.5
D_FF = 2816
FF_BLOCK = D_FF // 4
DOWN_ROWS = D_FF // N_DEV
BRANCH_ROWS = D_MODEL // N_DEV
NORM_EPS = 1e-6
ADAM_LR, ADAM_B1, ADAM_B2, ADAM_EPS, ADAM_WD, ADAM_STEP = 0.001, 0.9, 0.999, 1e-08, 0.01, 10
SMALL_ROWS = 64
V7X_VMEM_BYTES = 64 * 1024 * 1024
VMEM_CAP = V7X_VMEM_BYTES - 6 * 1024 * 1024
MESH_ID = pl.DeviceIdType.MESH
ANY = pl.BlockSpec(memory_space=pl.ANY)
Q_POS, U_POS, GATE_POS = 0, 4, 6


def _pos_of_dev(j):
    return jnp.where(j < 2, j + 4, jnp.where(j < 6, j - 2, j))


def _dev_of_pos(p):
    return jnp.where(p < 4, p + 2, jnp.where(p < 6, p - 4, p))


def _nbytes(shape, dtype):
    n = 1
    for s in shape:
        n *= s
    return n * jnp.dtype(dtype).itemsize


def _params(blocks, scratch=(), temps=0, sem=None):
    need = 2 * sum(_nbytes(s, d) for s, d in blocks) + sum(_nbytes(s, d) for s, d in scratch) + temps
    assert need + (4 << 20) <= VMEM_CAP, need
    return pltpu.CompilerParams(dimension_semantics=sem, vmem_limit_bytes=VMEM_CAP)


def _tile(n, pref):
    return pref if n % pref == 0 else n


def _dot(a, b):
    return jnp.dot(a, b, preferred_element_type=F32)


def _dot_nt(a, b):
    return lax.dot_general(a, b, (((1,), (1,)), ((), ())), preferred_element_type=F32)


def _dot_tn(a, b):
    return lax.dot_general(a, b, (((0,), (0,)), ((), ())), preferred_element_type=F32)


def _sigmoid(x):
    return 1.0 / (1.0 + jnp.exp(-x))


_GELU_C = 0.7978845608028654


def _gelu(x):
    return x * (0.5 * (1.0 + jnp.tanh(_GELU_C * (x + 0.044715 * (x * x * x)))))


def _gelu_grad(x):
    t = jnp.tanh(_GELU_C * (x + 0.044715 * (x * x * x)))
    return 0.5 * (1.0 + t) + 0.5 * x * (1.0 - t * t) * (_GELU_C * (1.0 + 3.0 * 0.044715 * x * x))


def _rms_stats(x):
    r = lax.rsqrt(jnp.mean(x * x, axis=-1, keepdims=True) + NORM_EPS)
    return r, x * r


def _rms_bwd(dy, x, g):
    r, xh = _rms_stats(x)
    dg = jnp.sum(dy * xh, axis=0, keepdims=True)
    dxh = dy * g
    dx = r * (dxh - xh * jnp.mean(dxh * xh, axis=-1, keepdims=True))
    return dx, dg


def _split3(x):
    hi = x.astype(BF16)
    r = x - hi.astype(F32)
    mid = r.astype(BF16)
    lo = (r - mid.astype(F32)).astype(BF16)
    return hi, mid, lo


def _mask_mm(mask_bf16, x):
    hi, mid, lo = _split3(x)
    return _dot(mask_bf16, hi) + _dot(mask_bf16, mid) + _dot(mask_bf16, lo)


def _place():
    return lax.axis_index("x"), lax.axis_index("y"), lax.axis_index("c")


def _gather_copies(src, out, send, recv, loc, slicers):
    n = len(src)
    x, y, c = _place()
    me, sib = (x, y, c), (x, y, 1 - c)
    chips = [(1 - x, y), (x, 1 - y), (1 - x, 1 - y)]

    def dev(p):
        return 4 * p[0] + 2 * p[1] + p[2]

    def rc(i, k, block, to, from_src=False):
        dst = slicers[i](out[i], dev(block))
        return pltpu.make_async_remote_copy(
            src_ref=src[i] if from_src else dst, dst_ref=dst, send_sem=send.at[7 * i + k],
            recv_sem=recv.at[7 * i + k], device_id=to, device_id_type=MESH_ID)

    mine = [pltpu.make_async_copy(src[i], slicers[i](out[i], dev(me)), loc.at[i]) for i in range(n)]
    for cp in mine:
        cp.start()
    first = []
    for i in range(n):
        first.append(rc(i, 0, me, sib, True))
        for j, chip in enumerate(chips):
            first.append(rc(i, 1 + j, me, (*chip, c), True))
    for cp in first:
        cp.start()
    passed = []
    for j, chip in enumerate(chips):
        for i in range(n):
            rc(i, 1 + j, (*chip, c), me).wait_recv()
            cp = rc(i, 4 + j, (*chip, c), sib)
            cp.start()
            passed.append(cp)
    for i in range(n):
        rc(i, 0, sib, me).wait_recv()
        for j, chip in enumerate(chips):
            rc(i, 4 + j, (*chip, 1 - c), me).wait_recv()
    for cp in first + passed:
        cp.wait_send()
    for cp in mine:
        cp.wait()


def _gather_scratch(n):
    return [pltpu.SemaphoreType.DMA((7 * n,)), pltpu.SemaphoreType.DMA((7 * n,)), pltpu.SemaphoreType.DMA((n,))]


def _all_gather(name, srcs, out_shapes, slicers):
    n = len(srcs)

    def body(*refs):
        _gather_copies(refs[:n], refs[n:2 * n], *refs[2 * n:], slicers)

    return pl.pallas_call(
        body, name=name, out_shape=[jax.ShapeDtypeStruct(s, d) for s, d in out_shapes],
        in_specs=[ANY] * n, out_specs=[ANY] * n, scratch_shapes=_gather_scratch(n),
    )(*srcs)


def _handshake(peers):
    barrier = pltpu.get_barrier_semaphore()
    for peer in peers:
        pl.semaphore_signal(barrier, inc=1, device_id=peer, device_id_type=MESH_ID)
    pl.semaphore_wait(barrier, len(peers))


def _all_gather_async(name, collective_id, srcs, out_shapes, slicers):
    n = len(srcs)

    def body(*refs):
        x, y, c = _place()
        _handshake([(1 - x if dx else x, 1 - y if dy else y, 1 - c if dc else c)
                    for dx in (0, 1) for dy in (0, 1) for dc in (0, 1) if dx or dy or dc])
        _gather_copies(refs[:n], refs[n:2 * n], *refs[2 * n:], slicers)

    return _sequencer_call(name, collective_id, body, srcs, [jax.ShapeDtypeStruct(s, d) for s, d in out_shapes],
                           _gather_scratch(n))


def _sequencer_call(name, collective_id, body, operands, out_types, scratch):
    return pl.kernel(
        body, out_type=out_types, mesh=plsc.ScalarSubcoreMesh(axis_name="sequencer", num_cores=1), name=name,
        scratch_types=scratch, compiler_params=pltpu.CompilerParams(collective_id=collective_id),
    )(*operands)


def _exchange_sibling(name, collective_id, grads, shard_fns, shard_shapes):
    n = len(grads)

    def body(*refs):
        g, land = refs[:n], refs[n:2 * n]
        send, recv = refs[2 * n:]
        x, y, c = _place()
        _handshake([(x, y, 1 - c)])
        remote = []
        for i in range(n):
            for q in range(4):
                cp = pltpu.make_async_remote_copy(
                    src_ref=shard_fns[i](g[i], 2 * q + (1 - c)), dst_ref=land[i].at[q], send_sem=send.at[4 * i + q],
                    recv_sem=recv.at[4 * i + q], device_id=(x, y, 1 - c), device_id_type=MESH_ID)
                cp.start()
                remote.append(cp)
        for cp in remote:
            cp.wait()

    return _sequencer_call(name, collective_id, body, grads, [jax.ShapeDtypeStruct((4, *s), F32) for s in shard_shapes],
                           [pltpu.SemaphoreType.DMA((4 * n,)), pltpu.SemaphoreType.DMA((4 * n,))])


def _exchange_chips(name, collective_id, parts):
    n = len(parts)

    def body(*refs):
        part, out = refs[:n], refs[n:2 * n]
        send, recv = refs[2 * n:]
        x, y, c = _place()
        _handshake([(1 - x, y, c), (x, 1 - y, c), (1 - x, 1 - y, c)])
        remote = []
        for i in range(n):
            for s in range(3):
                qx = 1 - x if (s + 1) // 2 else x
                qy = 1 - y if (s + 1) % 2 else y
                cp = pltpu.make_async_remote_copy(
                    src_ref=part[i].at[2 * qx + qy], dst_ref=out[i].at[s], send_sem=send.at[3 * i + s],
                    recv_sem=recv.at[3 * i + s], device_id=(qx, qy, c), device_id_type=MESH_ID)
                cp.start()
                remote.append(cp)
        for cp in remote:
            cp.wait()

    return _sequencer_call(name, collective_id, body, parts,
                           [jax.ShapeDtypeStruct((3, *p.shape[1:]), p.dtype) for p in parts],
                           [pltpu.SemaphoreType.DMA((3 * n,)), pltpu.SemaphoreType.DMA((3 * n,))])


def _chip_partial(name, core, grad, own_block, own_index, land):
    _, rows, cols = land.shape
    tr = own_block[-2]

    def body(core_ref, a_ref, b_ref, o_ref):
        o_ref[...] = (a_ref[...] + b_ref[...]).astype(BF16)

    spec = pl.BlockSpec((None, tr, cols), lambda q, r, c: (q, r, 0))
    return pl.pallas_call(
        body, name=name, out_shape=jax.ShapeDtypeStruct(land.shape, BF16),
        grid_spec=pltpu.PrefetchScalarGridSpec(
            num_scalar_prefetch=1, grid=(4, rows // tr),
            in_specs=[pl.BlockSpec(own_block, lambda q, r, c: own_index(q, r, c[0])), spec], out_specs=spec),
        compiler_params=_params([((tr, cols), F32)] * 2 + [((tr, cols), BF16)], sem=("arbitrary", "arbitrary")),
    )(core, grad, land)


def _adamw_math(w, g, m, v):
    m = ADAM_B1 * m + (1.0 - ADAM_B1) * g
    v = ADAM_B2 * v + (1.0 - ADAM_B2) * (g * g)
    m_hat = m / (1.0 - ADAM_B1 ** ADAM_STEP)
    v_hat = v / (1.0 - ADAM_B2 ** ADAM_STEP)
    delta = -ADAM_LR * (m_hat / (jnp.sqrt(v_hat) + ADAM_EPS) + ADAM_WD * w)
    return delta, m, v


def _adamw(name, chip, own, landed, w, m, v):
    _, rows, cols = own.shape
    tr = _tile(rows, 256) if rows % 256 == 0 else _tile(rows, 176)

    def body(chip_ref, own_ref, l_ref, w_ref, m_ref, v_ref, g_out, d_out, m_out, v_out):
        g = own_ref[...].astype(F32)
        for s in range(3):
            g = g + l_ref[s].astype(F32)
        delta, m_new, v_new = _adamw_math(w_ref[...], g, m_ref[...], v_ref[...])
        g_out[...] = g
        d_out[...] = delta
        m_out[...] = m_new
        v_out[...] = v_new

    spec = pl.BlockSpec((tr, cols), lambda r, c: (r, 0))
    return pl.pallas_call(
        body, name=name, out_shape=[jax.ShapeDtypeStruct((rows, cols), F32)] * 4,
        grid_spec=pltpu.PrefetchScalarGridSpec(
            num_scalar_prefetch=1, grid=(rows // tr,),
            in_specs=[pl.BlockSpec((None, tr, cols), lambda r, c: (c[0], r, 0)),
                      pl.BlockSpec((3, tr, cols), lambda r, c: (0, r, 0)), spec, spec, spec],
            out_specs=[spec] * 4),
        compiler_params=_params([((4, tr, cols), own.dtype)] + [((tr, cols), F32)] * 7, sem=("arbitrary",)),
    )(chip, own, landed, w, m, v)


def _proj_forward(x, gain, w_in_g):
    t = x.shape[0]
    tm = _tile(t, 1024)

    def body(x_ref, g_ref, w_ref, o_ref, h_ref, ht_ref):
        @pl.when(pl.program_id(1) == 0)
        def _():
            _, xh = _rms_stats(x_ref[...])
            h = (xh * g_ref[...]).astype(BF16)
            h_ref[...] = h
            ht_ref[...] = h.T

        o_ref[...] = _dot(h_ref[...], w_ref[...])

    tok = pl.BlockSpec((tm, D_MODEL), lambda m, p: (m, 0))
    return pl.pallas_call(
        body, name="proj_fwd",
        out_shape=[jax.ShapeDtypeStruct((N_DEV, t, D_MODEL), F32), jax.ShapeDtypeStruct((t, D_MODEL), BF16),
                   jax.ShapeDtypeStruct((D_MODEL, t), BF16)],
        grid=(t // tm, N_DEV),
        in_specs=[tok, pl.BlockSpec((1, D_MODEL), lambda m, p: (0, 0)),
                  pl.BlockSpec((None, D_MODEL, D_MODEL), lambda m, p: (p, 0, 0))],
        out_specs=[pl.BlockSpec((None, tm, D_MODEL), lambda m, p: (p, m, 0)), tok,
                   pl.BlockSpec((D_MODEL, tm), lambda m, p: (0, m))],
        compiler_params=_params([((tm, D_MODEL), F32)] * 2 + [((D_MODEL, D_MODEL), BF16)] + [((tm, D_MODEL), BF16)] * 2,
                                temps=6 << 20, sem=("arbitrary", "arbitrary")),
    )(x, gain, w_in_g)


def _masked_ws(ws_ref, g):
    row = lax.broadcasted_iota(jnp.int32, (GMLP_CHUNK, GMLP_CHUNK), 0)
    col = lax.broadcasted_iota(jnp.int32, (GMLP_CHUNK, GMLP_CHUNK), 1)
    return jnp.where(row >= col, ws_ref[g], 0.0).astype(BF16)


def _gmlp_forward(proj, ln_g, ln_b, w_s, bias_b):
    t = proj.shape[1]
    tm = _tile(t, 256)
    chunks = tm // GMLP_CHUNK

    def body(u_ref, v_ref, lng_ref, lnb_ref, ws_ref, bias_ref, a_ref, vn_scr):
        vv = _gelu(v_ref[...])
        mu = jnp.mean(vv, axis=-1, keepdims=True)
        cen = vv - mu
        var = jnp.mean(cen * cen, axis=-1, keepdims=True)
        vn_scr[...] = ((cen * lax.rsqrt(var + NORM_EPS)) * lng_ref[...] + lnb_ref[...]).astype(BF16)
        for g in range(GROUPS):
            wm = _masked_ws(ws_ref, g)
            cols = slice(g * HEAD_DIM, (g + 1) * HEAD_DIM)
            for c in range(chunks):
                rows = slice(c * GMLP_CHUNK, (c + 1) * GMLP_CHUNK)
                mixed = _dot(wm, vn_scr[rows, cols]) + bias_ref[g]
                a_ref[rows, cols] = (_gelu(u_ref[rows, cols]) * mixed).astype(BF16)

    small = pl.BlockSpec((GROUPS, GMLP_CHUNK, GMLP_CHUNK), lambda m: (0, 0, 0))
    vec = pl.BlockSpec((1, D_MODEL), lambda m: (0, 0))
    return pl.pallas_call(
        body, name="gmlp_fwd", out_shape=jax.ShapeDtypeStruct((t, D_MODEL), BF16), grid=(t // tm,),
        in_specs=[pl.BlockSpec((None, tm, D_MODEL), lambda m: (U_POS, m, 0)),
                  pl.BlockSpec((None, tm, D_MODEL), lambda m: (U_POS + 1, m, 0)), vec, vec, small, small],
        out_specs=pl.BlockSpec((tm, D_MODEL), lambda m: (m, 0)),
        scratch_shapes=[pltpu.VMEM((tm, D_MODEL), BF16)],
        compiler_params=_params([((tm, D_MODEL), F32)] * 2 + [((tm, D_MODEL), BF16)] + [((8, 128, 128), F32)] * 2,
                                scratch=[((tm, D_MODEL), BF16)], temps=8 << 20, sem=("arbitrary",)),
    )(proj, proj, ln_g, ln_b, w_s, bias_b)


def _lower_bound(tab_ref):
    t0, t1 = tab_ref[0:1, :], tab_ref[1:2, :]
    mx = jnp.maximum(t0, t1)
    e0, e1 = jnp.exp(t0 - mx), jnp.exp(t1 - mx)
    return e0 / (e0 + e1)


def _tri_masks():
    row = lax.broadcasted_iota(jnp.int32, (HGRN_CHUNK, HGRN_CHUNK), 0)
    col = lax.broadcasted_iota(jnp.int32, (HGRN_CHUNK, HGRN_CHUNK), 1)
    return row >= col, row <= col


def _chunk_rows(c):
    return slice(c * HGRN_CHUNK, (c + 1) * HGRN_CHUNK)


def _per_chunk(x, nc, fn):
    return jnp.concatenate([fn(x[_chunk_rows(c)]) for c in range(nc)], axis=0)


def _chunk_row_bcast(x, nc, i):
    return _per_chunk(x, nc, lambda xc: jnp.broadcast_to(xc[i:i + 1, :], (HGRN_CHUNK, HEAD_DIM)))


def _hgrn_gates(q, fl, lb, nc):
    lower, _ = _tri_masks()
    lower = lower.astype(BF16)
    s = _sigmoid(fl)
    f = lb + (1.0 - lb) * s
    k = 1.0 - f
    hi, mid, lo = _split3(jnp.log(f))
    a = jnp.concatenate([_dot(lower, hi[_chunk_rows(c)]) + _dot(lower, mid[_chunk_rows(c)]) + _dot(lower, lo[_chunk_rows(c)])
                         for c in range(nc)], axis=0)
    a_mid = _chunk_row_bcast(a, nc, HGRN_CHUNK // 2 - 1)
    a_last = _chunk_row_bcast(a, nc, HGRN_CHUNK - 1)
    qs = q * HGRN_SCALE
    e_in, e_out, e_end, e_all = jnp.exp(a - a_mid), jnp.exp(a_mid - a), jnp.exp(a_last - a), jnp.exp(a)
    decay = [jnp.exp(a[c * HGRN_CHUNK + HGRN_CHUNK - 1:(c + 1) * HGRN_CHUNK, :]) for c in range(nc)]
    return dict(s=s, f=f, k=k, decay=decay, e_in=e_in, e_out=e_out, e_end=e_end, e_all=e_all,
                qi=qs * e_in, ki=k * e_out, kd=k * e_end, qe=qs * e_all)


def _hgrn_forward(proj, lb_table, norm_g):
    t = proj.shape[1]
    tb = _tile(t, 1024)
    nc = tb // HGRN_CHUNK
    n_chunks = t // HGRN_CHUNK

    def body(q_ref, f_ref, i_ref, g_ref, tab_ref, ng_ref, og_ref, o_ref, st_ref, state):
        @pl.when(pl.program_id(1) == 0)
        def _():
            state[...] = jnp.zeros_like(state)

        lower, _ = _tri_masks()
        gt = _hgrn_gates(q_ref[...], f_ref[...], _lower_bound(tab_ref), nc)
        qi, ki, kd, qe = (gt[n].astype(BF16) for n in ("qi", "ki", "kd", "qe"))
        vb = i_ref[...].astype(BF16)
        o_intra, d_state = [], []
        for c in range(nc):
            rows = _chunk_rows(c)
            p = jnp.where(lower, _dot_nt(qi[rows], ki[rows]), 0.0).astype(BF16)
            o_intra.append(_dot(p, vb[rows]))
            d_state.append(_dot_tn(vb[rows], kd[rows]))
        st = state[...]
        outs = []
        for c in range(nc):
            st_ref[c] = st
            outs.append(o_intra[c] + _dot_nt(qe[_chunk_rows(c)], st.astype(BF16)))
            st = st * gt["decay"][c] + d_state[c]
        state[...] = st
        o = jnp.concatenate(outs, axis=0)
        o_ref[...] = o
        _, oh = _rms_stats(o)
        gz = g_ref[...]
        og_ref[...] = ((oh * ng_ref[...]) * (gz * _sigmoid(gz))).astype(BF16)

    def blk(p):
        return pl.BlockSpec((None, tb, HEAD_DIM), lambda h, n: (p, n, h))

    out_blk = pl.BlockSpec((tb, HEAD_DIM), lambda h, n: (n, h))
    return pl.pallas_call(
        body, name="hgrn_fwd",
        out_shape=[jax.ShapeDtypeStruct((t, D_MODEL), BF16), jax.ShapeDtypeStruct((t, D_MODEL), F32),
                   jax.ShapeDtypeStruct((HEADS, n_chunks, HEAD_DIM, HEAD_DIM), F32)],
        grid=(HEADS, t // tb),
        in_specs=[blk(Q_POS), blk(Q_POS + 1), blk(Q_POS + 2), blk(Q_POS + 3),
                  pl.BlockSpec((2, HEAD_DIM), lambda h, n: (0, h)), pl.BlockSpec((1, HEAD_DIM), lambda h, n: (0, h))],
        out_specs=[out_blk, out_blk, pl.BlockSpec((None, nc, HEAD_DIM, HEAD_DIM), lambda h, n: (h, n, 0, 0))],
        scratch_shapes=[pltpu.VMEM((HEAD_DIM, HEAD_DIM), F32)],
        compiler_params=_params([((tb, HEAD_DIM), F32)] * 6 + [((nc, HEAD_DIM, HEAD_DIM), F32)], temps=8 << 20,
                                sem=("arbitrary", "arbitrary")),
    )(proj, proj, proj, proj, lb_table, norm_g)


def _branch_out_forward(a, og, proj, x, w_a, w_b, w_out, ffn_g):
    t = x.shape[0]
    tm = _tile(t, 256)

    def body(a_ref, og_ref, ga_ref, gb_ref, x_ref, wa_ref, wb_ref, wo_ref, g_ref, ya_ref, yb_ref, mg_ref, x1_ref, h2_ref,
             h2t_ref):
        ya = _dot(a_ref[...], wa_ref[...])
        yb = _dot(og_ref[...], wb_ref[...])
        ya_ref[...] = ya
        yb_ref[...] = yb
        merged = (_sigmoid(ga_ref[...]) * ya + _sigmoid(gb_ref[...]) * yb).astype(BF16)
        mg_ref[...] = merged
        x1 = x_ref[...] + _dot(merged, wo_ref[...])
        x1_ref[...] = x1
        _, xh = _rms_stats(x1)
        h2 = (xh * g_ref[...]).astype(BF16)
        h2_ref[...] = h2
        h2t_ref[...] = h2.T

    tok = pl.BlockSpec((tm, D_MODEL), lambda m: (m, 0))
    wsp = pl.BlockSpec((D_MODEL, D_MODEL), lambda m: (0, 0))
    return pl.pallas_call(
        body, name="branch_out_fwd",
        out_shape=[jax.ShapeDtypeStruct((t, D_MODEL), F32), jax.ShapeDtypeStruct((t, D_MODEL), F32),
                   jax.ShapeDtypeStruct((t, D_MODEL), BF16), jax.ShapeDtypeStruct((t, D_MODEL), F32),
                   jax.ShapeDtypeStruct((t, D_MODEL), BF16), jax.ShapeDtypeStruct((D_MODEL, t), BF16)],
        grid=(t // tm,),
        in_specs=[tok, tok, pl.BlockSpec((None, tm, D_MODEL), lambda m: (GATE_POS, m, 0)),
                  pl.BlockSpec((None, tm, D_MODEL), lambda m: (GATE_POS + 1, m, 0)), tok, wsp, wsp, wsp,
                  pl.BlockSpec((1, D_MODEL), lambda m: (0, 0))],
        out_specs=[tok] * 5 + [pl.BlockSpec((D_MODEL, tm), lambda m: (0, m))],
        compiler_params=_params([((tm, D_MODEL), BF16)] * 5 + [((tm, D_MODEL), F32)] * 6 + [((D_MODEL, D_MODEL), BF16)] * 3,
                                temps=8 << 20, sem=("arbitrary",)),
    )(a, og, proj, proj, x, w_a, w_b, w_out, ffn_g)


def _ffn_forward(h2, x1, w_gu, w_down, target, final_g):
    t = x1.shape[0]
    tm = _tile(t, 512)

    def body(h_ref, wg_ref, wu_ref, wd_ref, x1_ref, t_ref, g_ref, gu_ref, act_ref, loss_ref, dg_ref, dx_ref, dxb_ref, acc):
        m, j = pl.program_id(0), pl.program_id(1)

        @pl.when((m == 0) & (j == 0))
        def _():
            loss_ref[...] = jnp.zeros_like(loss_ref)
            dg_ref[...] = jnp.zeros_like(dg_ref)

        h = h_ref[...]
        gate = _dot(h, wg_ref[...])
        up = _dot(h, wu_ref[...])
        gu_ref[0] = gate
        gu_ref[1] = up
        act = ((gate * _sigmoid(gate)) * up).astype(BF16)
        act_ref[...] = act
        part = _dot(act, wd_ref[...])

        @pl.when(j == 0)
        def _():
            acc[...] = part

        @pl.when((j > 0) & (j < 3))
        def _():
            acc[...] += part

        @pl.when(j == 3)
        def _():
            x2 = x1_ref[...] + (acc[...] + part)
            g = g_ref[...]
            r, xh = _rms_stats(x2)
            err = xh * g - t_ref[...]
            loss_ref[...] += 0.5 * jnp.sum(jnp.mean(err * err, axis=-1, keepdims=True), axis=0, keepdims=True)
            dy = err * (1.0 / D_MODEL)
            dg_ref[...] += jnp.sum(dy * xh, axis=0, keepdims=True)
            dxh = dy * g
            dx = r * (dxh - xh * jnp.mean(dxh * xh, axis=-1, keepdims=True))
            dx_ref[...] = dx
            dxb_ref[...] = dx.astype(BF16)

    tok = pl.BlockSpec((tm, D_MODEL), lambda m, j: (m, 0))
    vec = pl.BlockSpec((1, D_MODEL), lambda m, j: (0, 0))
    return pl.pallas_call(
        body, name="ffn_fwd",
        out_shape=[jax.ShapeDtypeStruct((4, 2, t, FF_BLOCK), F32), jax.ShapeDtypeStruct((4, t, FF_BLOCK), BF16),
                   jax.ShapeDtypeStruct((8, 128), F32), jax.ShapeDtypeStruct((1, D_MODEL), F32),
                   jax.ShapeDtypeStruct((t, D_MODEL), F32), jax.ShapeDtypeStruct((t, D_MODEL), BF16)],
        grid=(t // tm, 4),
        in_specs=[tok, pl.BlockSpec((None, D_MODEL, FF_BLOCK), lambda m, j: (j, 0, 0)),
                  pl.BlockSpec((None, D_MODEL, FF_BLOCK), lambda m, j: (j + 4, 0, 0)),
                  pl.BlockSpec((FF_BLOCK, D_MODEL), lambda m, j: (j, 0)), tok, tok, vec],
        out_specs=[pl.BlockSpec((None, 2, tm, FF_BLOCK), lambda m, j: (j, 0, m, 0)),
                   pl.BlockSpec((None, tm, FF_BLOCK), lambda m, j: (j, m, 0)),
                   pl.BlockSpec((8, 128), lambda m, j: (0, 0)), vec, tok, tok],
        scratch_shapes=[pltpu.VMEM((tm, D_MODEL), F32)],
        compiler_params=_params([((tm, D_MODEL), BF16), ((D_MODEL, 768), BF16), ((D_MODEL, 768), BF16),
                                 ((FF_BLOCK, D_MODEL), BF16), ((tm, D_MODEL), F32), ((tm, D_MODEL), F32), ((2, tm, 768), F32),
                                 ((tm, 768), BF16), ((tm, D_MODEL), F32), ((tm, D_MODEL), BF16)],
                                scratch=[((tm, D_MODEL), F32)], temps=8 << 20, sem=("arbitrary", "arbitrary")),
    )(h2, w_gu, w_gu, w_down, x1, target, final_g)


def _ffn_backward(dx2b, dx2, gu, x1, w_gu, w_down, ffn_g):
    t = x1.shape[0]
    tm = _tile(t, 512)

    def body(dxb_ref, dx2_ref, gu_ref, x1_ref, wg_ref, wu_ref, wd_ref, g_ref, dgu_ref, dx1_ref, dx1b_ref, dg_ref, acc):
        m, j = pl.program_id(0), pl.program_id(1)

        @pl.when((m == 0) & (j == 0))
        def _():
            dg_ref[...] = jnp.zeros_like(dg_ref)

        dact = _dot_nt(dxb_ref[...], wd_ref[...])
        gate, up = gu_ref[0], gu_ref[1]
        sg = _sigmoid(gate)
        dgate = (dact * up * (sg * (1.0 + gate * (1.0 - sg)))).astype(BF16)
        dup = (dact * (gate * sg)).astype(BF16)
        dgu_ref[0] = dgate
        dgu_ref[1] = dup
        part = _dot_nt(dgate, wg_ref[...]) + _dot_nt(dup, wu_ref[...])

        @pl.when(j == 0)
        def _():
            acc[...] = part

        @pl.when(j > 0)
        def _():
            acc[...] += part

        @pl.when(j == 3)
        def _():
            dx, dg = _rms_bwd(acc[...], x1_ref[...], g_ref[...])
            dx1 = dx2_ref[...] + dx
            dx1_ref[...] = dx1
            dx1b_ref[...] = dx1.astype(BF16)
            dg_ref[...] += dg

    tok = pl.BlockSpec((tm, D_MODEL), lambda m, j: (m, 0))
    vec = pl.BlockSpec((1, D_MODEL), lambda m, j: (0, 0))
    gu_spec = pl.BlockSpec((None, 2, tm, FF_BLOCK), lambda m, j: (j, 0, m, 0))
    return pl.pallas_call(
        body, name="ffn_bwd",
        out_shape=[jax.ShapeDtypeStruct((4, 2, t, FF_BLOCK), BF16), jax.ShapeDtypeStruct((t, D_MODEL), F32),
                   jax.ShapeDtypeStruct((t, D_MODEL), BF16), jax.ShapeDtypeStruct((1, D_MODEL), F32)],
        grid=(t // tm, 4),
        in_specs=[tok, tok, gu_spec, tok, pl.BlockSpec((None, D_MODEL, FF_BLOCK), lambda m, j: (j, 0, 0)),
                  pl.BlockSpec((None, D_MODEL, FF_BLOCK), lambda m, j: (j + 4, 0, 0)),
                  pl.BlockSpec((FF_BLOCK, D_MODEL), lambda m, j: (j, 0)), vec],
        out_specs=[gu_spec, tok, tok, vec],
        scratch_shapes=[pltpu.VMEM((tm, D_MODEL), F32)],
        compiler_params=_params([((tm, D_MODEL), BF16), ((tm, D_MODEL), F32), ((2, tm, 768), F32), ((tm, D_MODEL), F32),
                                 ((D_MODEL, 768), BF16), ((D_MODEL, 768), BF16), ((FF_BLOCK, D_MODEL), BF16),
                                 ((2, tm, 768), BF16), ((tm, D_MODEL), F32), ((tm, D_MODEL), BF16)],
                                scratch=[((tm, D_MODEL), F32)], temps=8 << 20, sem=("arbitrary", "arbitrary")),
    )(dx2b, dx2, gu, x1, w_gu, w_gu, w_down, ffn_g)


def _branch_out_backward(dx1b, ya, yb, proj, w_a, w_b, w_out):
    t = ya.shape[0]
    tm = _tile(t, 256)

    def body(dx_ref, ya_ref, yb_ref, ga_ref, gb_ref, wa_ref, wb_ref, wo_ref, dya_ref, dyb_ref, dgate_ref, da_ref, dog_ref):
        dm = _dot_nt(dx_ref[...], wo_ref[...])
        sa, sb = _sigmoid(ga_ref[...]), _sigmoid(gb_ref[...])
        dya = (dm * sa).astype(BF16)
        dyb = (dm * sb).astype(BF16)
        dya_ref[...] = dya
        dyb_ref[...] = dyb
        dgate_ref[0] = (dm * ya_ref[...] * (sa * (1.0 - sa))).astype(BF16)
        dgate_ref[1] = (dm * yb_ref[...] * (sb * (1.0 - sb))).astype(BF16)
        da_ref[...] = _dot_nt(dya, wa_ref[...])
        dog_ref[...] = _dot_nt(dyb, wb_ref[...])

    tok = pl.BlockSpec((tm, D_MODEL), lambda m: (m, 0))
    wsp = pl.BlockSpec((D_MODEL, D_MODEL), lambda m: (0, 0))
    return pl.pallas_call(
        body, name="branch_out_bwd",
        out_shape=[jax.ShapeDtypeStruct((t, D_MODEL), BF16), jax.ShapeDtypeStruct((t, D_MODEL), BF16),
                   jax.ShapeDtypeStruct((N_DEV, t, D_MODEL), BF16), jax.ShapeDtypeStruct((t, D_MODEL), F32),
                   jax.ShapeDtypeStruct((t, D_MODEL), F32)],
        grid=(t // tm,),
        in_specs=[tok, tok, tok, pl.BlockSpec((None, tm, D_MODEL), lambda m: (GATE_POS, m, 0)),
                  pl.BlockSpec((None, tm, D_MODEL), lambda m: (GATE_POS + 1, m, 0)), wsp, wsp, wsp],
        out_specs=[tok, tok, pl.BlockSpec((2, tm, D_MODEL), lambda m: (GATE_POS // 2, m, 0)), tok, tok],
        compiler_params=_params([((tm, D_MODEL), BF16)] * 5 + [((tm, D_MODEL), F32)] * 6 + [((D_MODEL, D_MODEL), BF16)] * 3,
                                temps=8 << 20, sem=("arbitrary",)),
    )(dx1b, ya, yb, proj, proj, w_a, w_b, w_out)


def _hgrn_backward(dproj, dog, o_saved, states, proj, lb_table, norm_g):
    t = proj.shape[1]
    tb = _tile(t, 1024)
    nc = tb // HGRN_CHUNK
    nb = t // tb

    def body(_, dog_ref, o_ref, st_ref, q_ref, f_ref, i_ref, g_ref, tab_ref, ng_ref, dp_ref, dng_ref, dtab_ref, gstate):
        @pl.when(pl.program_id(1) == 0)
        def _():
            gstate[...] = jnp.zeros_like(gstate)
            dng_ref[...] = jnp.zeros_like(dng_ref)
            dtab_ref[...] = jnp.zeros_like(dtab_ref)

        lb = _lower_bound(tab_ref)
        ng = ng_ref[...]
        lower, upper = _tri_masks()
        gt = _hgrn_gates(q_ref[...], f_ref[...], lb, nc)
        qi, ki, kd, qe = (gt[n].astype(BF16) for n in ("qi", "ki", "kd", "qe"))
        vb = i_ref[...].astype(BF16)
        o, gz, d_og = o_ref[...], g_ref[...], dog_ref[...]
        r, oh = _rms_stats(o)
        sg = _sigmoid(gz)
        d_on = d_og * (gz * sg)
        dgz = d_og * (oh * ng) * (sg * (1.0 + gz * (1.0 - sg)))
        dng_ref[...] += jnp.sum(d_on * oh, axis=0, keepdims=True)
        doh = d_on * ng
        dob = (r * (doh - oh * jnp.mean(doh * oh, axis=-1, keepdims=True))).astype(BF16)
        dv_intra, dqi, dki, dqe, g_upd = [], [], [], [], []
        for c in range(nc):
            rows = _chunk_rows(c)
            p = jnp.where(lower, _dot_nt(qi[rows], ki[rows]), 0.0).astype(BF16)
            dv_intra.append(_dot_tn(p, dob[rows]))
            dp = jnp.where(lower, _dot_nt(dob[rows], vb[rows]), 0.0).astype(BF16)
            dqi.append(_dot(dp, ki[rows]))
            dki.append(_dot_tn(dp, qi[rows]))
            dqe.append(_dot(dob[rows], st_ref[c].astype(BF16)))
            g_upd.append(_dot_tn(dob[rows], qe[rows]))
        g_after = [None] * nc
        g = gstate[...]
        for c in reversed(range(nc)):
            g_after[c] = g
            g = g * gt["decay"][c] + g_upd[c]
        gstate[...] = g
        dkd, dv, da_last = [], [], []
        for c in range(nc):
            rows = _chunk_rows(c)
            gb = g_after[c].astype(BF16)
            dkd.append(_dot(vb[rows], gb))
            dv.append(dv_intra[c] + _dot_nt(kd[rows], gb))
            da_last.append(jnp.sum(g_after[c] * st_ref[c], axis=0, keepdims=True) * gt["decay"][c])
        dqi, dki, dqe, dkd, dv = (jnp.concatenate(z, axis=0) for z in (dqi, dki, dqe, dkd, dv))
        dqs = dqi * gt["e_in"] + dqe * gt["e_all"]
        dk = dki * gt["e_out"] + dkd * gt["e_end"]
        t_in, t_out, t_end = dqi * gt["qi"], dki * gt["ki"], dkd * gt["kd"]
        da = t_in - t_out + dqe * gt["qe"] - t_end
        row = lax.broadcasted_iota(jnp.int32, (HGRN_CHUNK, HEAD_DIM), 0)
        d_mid = t_out - t_in
        pieces = []
        for c in range(nc):
            rows = _chunk_rows(c)
            da_mid = jnp.sum(d_mid[rows], axis=0, keepdims=True)
            da_end = jnp.sum(t_end[rows], axis=0, keepdims=True) + da_last[c]
            da_c = da[rows] + jnp.where(row == HGRN_CHUNK // 2 - 1, da_mid, 0.0) + jnp.where(row == HGRN_CHUNK - 1, da_end, 0.0)
            pieces.append(_mask_mm(upper.astype(BF16), da_c))
        df = jnp.concatenate(pieces, axis=0) / gt["f"] - dk
        s = gt["s"]
        dlb = jnp.sum(df * (1.0 - s), axis=0, keepdims=True)
        dp_ref[0] = (dqs * HGRN_SCALE).astype(BF16)
        dp_ref[1] = (df * (1.0 - lb) * (s * (1.0 - s))).astype(BF16)
        dp_ref[2] = dv.astype(BF16)
        dp_ref[3] = dgz.astype(BF16)
        dt0 = dlb * (lb * (1.0 - lb))
        dtab_ref[0:1, :] += dt0
        dtab_ref[1:2, :] -= dt0

    def blk(p):
        return pl.BlockSpec((None, tb, HEAD_DIM), lambda h, n: (p, nb - 1 - n, h))

    tok = pl.BlockSpec((tb, HEAD_DIM), lambda h, n: (nb - 1 - n, h))
    return pl.pallas_call(
        body, name="hgrn_bwd",
        out_shape=[jax.ShapeDtypeStruct((N_DEV, t, D_MODEL), BF16), jax.ShapeDtypeStruct((1, D_MODEL), F32),
                   jax.ShapeDtypeStruct((2, D_MODEL), F32)],
        grid=(HEADS, nb),
        in_specs=[ANY, tok, tok, pl.BlockSpec((None, nc, HEAD_DIM, HEAD_DIM), lambda h, n: (h, nb - 1 - n, 0, 0)),
                  blk(Q_POS), blk(Q_POS + 1), blk(Q_POS + 2), blk(Q_POS + 3),
                  pl.BlockSpec((2, HEAD_DIM), lambda h, n: (0, h)), pl.BlockSpec((1, HEAD_DIM), lambda h, n: (0, h))],
        out_specs=[pl.BlockSpec((4, tb, HEAD_DIM), lambda h, n: (0, nb - 1 - n, h)),
                   pl.BlockSpec((1, HEAD_DIM), lambda h, n: (0, h)), pl.BlockSpec((2, HEAD_DIM), lambda h, n: (0, h))],
        scratch_shapes=[pltpu.VMEM((HEAD_DIM, HEAD_DIM), F32)],
        input_output_aliases={0: 0},
        compiler_params=_params([((tb, HEAD_DIM), F32)] * 6 + [((nc, HEAD_DIM, HEAD_DIM), F32)] + [((4, tb, HEAD_DIM), BF16)],
                                temps=8 << 20, sem=("arbitrary", "arbitrary")),
    )(dproj, dog, o_saved, states, proj, proj, proj, proj, lb_table, norm_g)


def _gmlp_backward(dproj, da, proj, ln_g, ln_b, w_s, bias_b):
    t = proj.shape[1]
    tm = _tile(t, 256)
    chunks = tm // GMLP_CHUNK

    def body(_, da_ref, u_ref, v_ref, lng_ref, lnb_ref, ws_ref, bias_ref, dp_ref, dlng_ref, dlnb_ref, dws_ref, dbs_ref,
             vn_scr, dvn_scr):
        @pl.when(pl.program_id(0) == 0)
        def _():
            dlng_ref[...] = jnp.zeros_like(dlng_ref)
            dlnb_ref[...] = jnp.zeros_like(dlnb_ref)
            dws_ref[...] = jnp.zeros_like(dws_ref)
            dbs_ref[...] = jnp.zeros_like(dbs_ref)

        v = v_ref[...]
        vv = _gelu(v)
        mu = jnp.mean(vv, axis=-1, keepdims=True)
        cen = vv - mu
        rstd = lax.rsqrt(jnp.mean(cen * cen, axis=-1, keepdims=True) + NORM_EPS)
        vhat = cen * rstd
        lng = lng_ref[...]
        vn_scr[...] = (vhat * lng + lnb_ref[...]).astype(BF16)
        row = lax.broadcasted_iota(jnp.int32, (GMLP_CHUNK, GMLP_CHUNK), 0)
        col = lax.broadcasted_iota(jnp.int32, (GMLP_CHUNK, GMLP_CHUNK), 1)
        for g in range(GROUPS):
            wm = _masked_ws(ws_ref, g)
            cols = slice(g * HEAD_DIM, (g + 1) * HEAD_DIM)
            dws = jnp.zeros((GMLP_CHUNK, GMLP_CHUNK), F32)
            dbs = jnp.zeros((GMLP_CHUNK, GMLP_CHUNK), F32)
            for c in range(chunks):
                rows = slice(c * GMLP_CHUNK, (c + 1) * GMLP_CHUNK)
                vn = vn_scr[rows, cols]
                mixed = _dot(wm, vn) + bias_ref[g]
                u = u_ref[rows, cols]
                d_a = da_ref[rows, cols]
                dp_ref[0, rows, cols] = (d_a * mixed * _gelu_grad(u)).astype(BF16)
                dmix = d_a * _gelu(u)
                dmb = dmix.astype(BF16)
                dbs = dbs + dmix
                dws = dws + _dot_nt(dmb, vn)
                dvn_scr[rows, cols] = _dot_tn(wm, dmb)
            dws_ref[g] += jnp.where(row >= col, dws, 0.0)
            dbs_ref[g] += jnp.broadcast_to(jnp.sum(dbs, axis=-1, keepdims=True), (GMLP_CHUNK, GMLP_CHUNK))
        dvn = dvn_scr[...]
        dlng_ref[...] += jnp.sum(dvn * vhat, axis=0, keepdims=True)
        dlnb_ref[...] += jnp.sum(dvn, axis=0, keepdims=True)
        dvh = dvn * lng
        dvv = rstd * (dvh - jnp.mean(dvh, axis=-1, keepdims=True) - vhat * jnp.mean(dvh * vhat, axis=-1, keepdims=True))
        dp_ref[1] = (dvv * _gelu_grad(v)).astype(BF16)

    tok = pl.BlockSpec((tm, D_MODEL), lambda m: (m, 0))
    small = pl.BlockSpec((GROUPS, GMLP_CHUNK, GMLP_CHUNK), lambda m: (0, 0, 0))
    vec = pl.BlockSpec((1, D_MODEL), lambda m: (0, 0))
    return pl.pallas_call(
        body, name="gmlp_bwd",
        out_shape=[jax.ShapeDtypeStruct(dproj.shape, BF16), jax.ShapeDtypeStruct((1, D_MODEL), F32),
                   jax.ShapeDtypeStruct((1, D_MODEL), F32), jax.ShapeDtypeStruct((GROUPS, GMLP_CHUNK, GMLP_CHUNK), F32),
                   jax.ShapeDtypeStruct((GROUPS, GMLP_CHUNK, GMLP_CHUNK), F32)],
        grid=(t // tm,),
        in_specs=[ANY, tok, pl.BlockSpec((None, tm, D_MODEL), lambda m: (U_POS, m, 0)),
                  pl.BlockSpec((None, tm, D_MODEL), lambda m: (U_POS + 1, m, 0)), vec, vec, small, small],
        out_specs=[pl.BlockSpec((2, tm, D_MODEL), lambda m: (U_POS // 2, m, 0)), vec, vec, small, small],
        scratch_shapes=[pltpu.VMEM((tm, D_MODEL), BF16), pltpu.VMEM((tm, D_MODEL), F32)],
        input_output_aliases={0: 0},
        compiler_params=_params([((tm, D_MODEL), F32)] * 3 + [((2, tm, D_MODEL), BF16)] + [((8, 128, 128), F32)] * 4,
                                scratch=[((tm, D_MODEL), BF16), ((tm, D_MODEL), F32)], temps=12 << 20, sem=("arbitrary",)),
    )(dproj, da, proj, proj, ln_g, ln_b, w_s, bias_b)


def _input_backward(dproj, w_in_g, x, dx1, mix_g):
    t = x.shape[0]
    tm = _tile(t, 256)

    def body(dp_ref, w_ref, x_ref, dx1_ref, g_ref, dx_ref, dg_ref):
        @pl.when(pl.program_id(0) == 0)
        def _():
            dg_ref[...] = jnp.zeros_like(dg_ref)

        dh = _dot_nt(dp_ref[0], w_ref[0])
        for p in range(1, N_DEV):
            dh = dh + _dot_nt(dp_ref[p], w_ref[p])
        dx, dg = _rms_bwd(dh, x_ref[...], g_ref[...])
        dx_ref[...] = dx1_ref[...] + dx
        dg_ref[...] += dg

    tok = pl.BlockSpec((tm, D_MODEL), lambda m: (m, 0))
    vec = pl.BlockSpec((1, D_MODEL), lambda m: (0, 0))
    return pl.pallas_call(
        body, name="input_bwd",
        out_shape=[jax.ShapeDtypeStruct((t, D_MODEL), F32), jax.ShapeDtypeStruct((1, D_MODEL), F32)],
        grid=(t // tm,),
        in_specs=[pl.BlockSpec((N_DEV, tm, D_MODEL), lambda m: (0, m, 0)),
                  pl.BlockSpec((N_DEV, D_MODEL, D_MODEL), lambda m: (0, 0, 0)), tok, tok, vec],
        out_specs=[tok, vec],
        compiler_params=_params([((N_DEV, tm, D_MODEL), BF16), ((N_DEV, D_MODEL, D_MODEL), BF16)] + [((tm, D_MODEL), F32)] * 3,
                                temps=4 << 20, sem=("arbitrary",)),
    )(dproj, w_in_g, x, dx1, mix_g)


def _weight_grad(name, a, b, a_spec, b_spec, out_shape, out_spec, steps, blocks, a_is_transposed):
    def body(a_ref, b_ref, o_ref):
        o_ref[...] = _dot(a_ref[...], b_ref[...]) if a_is_transposed else _dot_tn(a_ref[...], b_ref[...])

    return pl.pallas_call(
        body, name=name, out_shape=jax.ShapeDtypeStruct(out_shape, F32), grid=(steps,), in_specs=[a_spec, b_spec],
        out_specs=out_spec, compiler_params=_params(blocks, temps=4 << 20, sem=("arbitrary",)),
    )(a, b)


def _pack_small(mix_g, ln_g, ln_b, b_s, lb_table, hg_norm, ffn_g, final_g):
    def part(a):
        a = a.reshape(-1, D_MODEL)
        return jnp.pad(a, ((0, 8 - a.shape[0]), (0, 0)))

    return jnp.concatenate([part(mix_g), part(ln_g), part(ln_b), part(hg_norm), part(ffn_g), part(final_g),
                            part(lb_table), part(b_s)], axis=0)


def _unpack_small(pack, w_s):
    return dict(norm_mix_g=pack[0:1], gmlp_ln_g=pack[8:9], gmlp_ln_b=pack[16:17], hgrn_norm_g=pack[24:25],
                norm_ffn_g=pack[32:33], norm_final_g=pack[40], hgrn_lb_table=pack[48:50],
                gmlp_b_s=pack[56:57].reshape(1, GROUPS, GMLP_CHUNK),
                gmlp_w_s=w_s.reshape(1, GROUPS, GMLP_CHUNK, GMLP_CHUNK))


def _adamw_small(name, gathered, w, m, v):
    rows, cols = w.shape

    def body(p_ref, w_ref, m_ref, v_ref, g_out, d_out, m_out, v_out):
        g = p_ref[0]
        for j in range(1, N_DEV):
            g = g + p_ref[j]
        delta, m_new, v_new = _adamw_math(w_ref[...], g, m_ref[...], v_ref[...])
        g_out[...] = g
        d_out[...] = delta
        m_out[...] = m_new
        v_out[...] = v_new

    tr = _tile(rows, 64)
    spec = pl.BlockSpec((tr, cols), lambda r: (r, 0))
    return pl.pallas_call(
        body, name=name, out_shape=[jax.ShapeDtypeStruct((rows, cols), F32)] * 4, grid=(rows // tr,),
        in_specs=[pl.BlockSpec((N_DEV, tr, cols), lambda r: (0, r, 0)), spec, spec, spec], out_specs=[spec] * 4,
        compiler_params=_params([((N_DEV, tr, cols), F32)] + [((tr, cols), F32)] * 7, sem=("arbitrary",)),
    )(gathered, w, m, v)


def kernel(x, norm_mix_g, w_in, gmlp_ln_g, gmlp_ln_b, gmlp_w_s, gmlp_b_s, hgrn_lb_table, hgrn_norm_g, w_branch_a, w_branch_b, w_out, norm_ffn_g, w_gate_up, w_down, norm_final_g, loss_target, m_norm_mix_g, m_w_in, m_gmlp_ln_g, m_gmlp_ln_b, m_gmlp_w_s, m_gmlp_b_s, m_hgrn_lb_table, m_hgrn_norm_g, m_w_branch_a, m_w_branch_b, m_w_out, m_norm_ffn_g, m_w_gate_up, m_w_down, m_norm_final_g, v_norm_mix_g, v_w_in, v_gmlp_ln_g, v_gmlp_ln_b, v_gmlp_w_s, v_gmlp_b_s, v_hgrn_lb_table, v_hgrn_norm_g, v_w_branch_a, v_w_branch_b, v_w_out, v_norm_ffn_g, v_w_gate_up, v_w_down, v_norm_final_g):
    t = x.shape[1]
    x2d = x.reshape(t, D_MODEL)
    target = loss_target.reshape(t, D_MODEL)
    final_g = norm_final_g.reshape(1, D_MODEL)

    shards = [w_in[0].astype(BF16), w_branch_a[0].astype(BF16), w_branch_b[0].astype(BF16), w_out[0].astype(BF16),
              w_gate_up[0].astype(BF16), w_down[0].astype(BF16)]

    def rows_of(n):
        return lambda ref, j: ref.at[pl.ds(pl.multiple_of(j * n, 8), n)]

    gathered = [((N_DEV, D_MODEL, D_MODEL), BF16), ((D_MODEL, D_MODEL), BF16), ((D_MODEL, D_MODEL), BF16),
                ((D_MODEL, D_MODEL), BF16), ((N_DEV, D_MODEL, FF_BLOCK), BF16), ((D_FF, D_MODEL), BF16)]
    places = [lambda ref, j: ref.at[_pos_of_dev(j)], rows_of(BRANCH_ROWS), rows_of(BRANCH_ROWS), rows_of(BRANCH_ROWS),
              lambda ref, j: ref.at[j], rows_of(DOWN_ROWS)]
    (w_in_g,) = _all_gather("w_in_all_gather", shards[:1], gathered[:1], places[:1])
    _, later = lax.optimization_barrier((w_in_g, shards[1:]))
    w_a, w_b, w_o, w_gu, w_dn = _all_gather_async("weights_all_gather", 0, later, gathered[1:], places[1:])

    proj, h, h_t = _proj_forward(x2d, norm_mix_g, w_in_g)
    bias_b = jnp.broadcast_to(gmlp_b_s[0][:, :, None], (GROUPS, GMLP_CHUNK, GMLP_CHUNK))
    a = _gmlp_forward(proj, gmlp_ln_g, gmlp_ln_b, gmlp_w_s[0], bias_b)
    og, o_saved, states = _hgrn_forward(proj, hgrn_lb_table, hgrn_norm_g)
    ya, yb, merged, x1, h2, h2_t = _branch_out_forward(a, og, proj, x2d, w_a, w_b, w_o, norm_ffn_g)
    gu, act, loss_tile, d_final_g, dx2, dx2b = _ffn_forward(h2, x1, w_gu, w_dn, target, final_g)

    core = lax.axis_index("c").astype(jnp.int32).reshape(1)
    chip = (2 * lax.axis_index("x") + lax.axis_index("y")).astype(jnp.int32).reshape(1)
    branch_rows, branch_shape = rows_of(BRANCH_ROWS), (BRANCH_ROWS, D_MODEL)
    branch_block = ((BRANCH_ROWS, D_MODEL), lambda q, r, c: (2 * q + c, 0))

    def chip_partials(names, grads, land, own_blocks):
        return [_chip_partial("chip_partial_" + nme, core, g_, blk, idx, l_)
                for nme, g_, (blk, idx), l_ in zip(names, grads, own_blocks, land)]

    whole = pl.BlockSpec((t, D_MODEL), lambda n: (0, 0))
    whole_t = pl.BlockSpec((D_MODEL, t), lambda n: (0, 0))
    col_blocks = [((t, D_MODEL), BF16), ((t, 256), BF16), ((D_MODEL, 256), F32)]

    def square_grad(name, a_, b_):
        return _weight_grad(name, a_, b_, whole, pl.BlockSpec((t, 256), lambda n: (0, n)), (D_MODEL, D_MODEL),
                            pl.BlockSpec((D_MODEL, 256), lambda n: (0, n)), D_MODEL // 256, col_blocks, False)

    dgu, dx1, dx1b, d_ffn_g = _ffn_backward(dx2b, dx2, gu, x1, w_gu, w_dn, norm_ffn_g)
    g_gu = _weight_grad(
        "grad_w_gate_up", h2_t, dgu, whole_t, pl.BlockSpec((None, None, t, FF_BLOCK), lambda j: (j % 4, j // 4, 0, 0)),
        (N_DEV, D_MODEL, FF_BLOCK), pl.BlockSpec((None, D_MODEL, FF_BLOCK), lambda j: (j, 0, 0)), N_DEV,
        [((D_MODEL, t), BF16), ((t, 768), BF16), ((D_MODEL, 768), F32)], True)
    g_dn = _weight_grad(
        "grad_w_down", act, dx2b, pl.BlockSpec((None, t, FF_BLOCK), lambda j: (j, 0, 0)), whole, (D_FF, D_MODEL),
        pl.BlockSpec((FF_BLOCK, D_MODEL), lambda j: (j, 0)), 4,
        [((t, 768), BF16), ((t, D_MODEL), BF16), ((FF_BLOCK, D_MODEL), F32)], False)
    names_f, grads_f = ["w_gate_up", "w_down"], [g_gu, g_dn]
    land_f = _exchange_sibling("ffn_grads_to_sibling", 2, grads_f, [lambda ref, j: ref.at[j], rows_of(DOWN_ROWS)],
                               [(D_MODEL, FF_BLOCK), (DOWN_ROWS, D_MODEL)])

    dya, dyb, dproj, da, dog = _branch_out_backward(dx1b, ya, yb, proj, w_a, w_b, w_o)
    g_a = square_grad("grad_w_a", a, dya)
    g_b = square_grad("grad_w_b", og, dyb)
    g_o = square_grad("grad_w_out", merged, dx1b)
    names_b, grads_b = ["w_branch_a", "w_branch_b", "w_out"], [g_a, g_b, g_o]
    land_b = _exchange_sibling("branch_grads_to_sibling", 3, grads_b, [branch_rows] * 3, [branch_shape] * 3)

    part_f = chip_partials(names_f, grads_f, land_f,
                           [((None, 256, FF_BLOCK), lambda q, r, c: (2 * q + c, r, 0)),
                            ((DOWN_ROWS // 2, D_MODEL), lambda q, r, c: (2 * (2 * q + c) + r, 0))])
    landed_f = _exchange_chips("ffn_grads_to_chips", 5, part_f)

    dog, _ = lax.optimization_barrier((dog, part_f))
    dproj, d_hg_norm, d_lb = _hgrn_backward(dproj, dog, o_saved, states, proj, hgrn_lb_table, hgrn_norm_g)

    part_b = chip_partials(names_b, grads_b, land_b, [branch_block] * 3)
    landed_b = _exchange_chips("branch_grads_to_chips", 6, part_b)

    da, _ = lax.optimization_barrier((da, part_b))
    dproj, d_ln_g, d_ln_b, d_ws, d_bs = _gmlp_backward(dproj, da, proj, gmlp_ln_g, gmlp_ln_b, gmlp_w_s[0], bias_b)

    def packed(vals):
        return _pack_small(*vals)

    def flat_ws(a):
        return a.reshape(GROUPS * GMLP_CHUNK, GMLP_CHUNK)

    w_pack = packed([norm_mix_g, gmlp_ln_g, gmlp_ln_b, gmlp_b_s, hgrn_lb_table, hgrn_norm_g, norm_ffn_g, norm_final_g])
    m_pack = packed([m_norm_mix_g, m_gmlp_ln_g, m_gmlp_ln_b, m_gmlp_b_s, m_hgrn_lb_table, m_hgrn_norm_g, m_norm_ffn_g, m_norm_final_g])
    v_pack = packed([v_norm_mix_g, v_gmlp_ln_g, v_gmlp_ln_b, v_gmlp_b_s, v_hgrn_lb_table, v_hgrn_norm_g, v_norm_ffn_g, v_norm_final_g])
    small_partial = _pack_small(jnp.zeros((1, D_MODEL), F32), d_ln_g, d_ln_b, d_bs[:, :, 0], d_lb, d_hg_norm, d_ffn_g, d_final_g)
    small_all, ws_all = _all_gather_async(
        "small_grads_all_gather", 1, [small_partial, flat_ws(d_ws)],
        [((N_DEV, SMALL_ROWS, D_MODEL), F32), ((N_DEV, GROUPS * GMLP_CHUNK, GMLP_CHUNK), F32)],
        [lambda ref, j: ref.at[j], lambda ref, j: ref.at[j]])

    g_in = _weight_grad(
        "grad_w_in", h_t, dproj, whole_t, pl.BlockSpec((None, t, D_MODEL), lambda p: (p, 0, 0)), (N_DEV, D_MODEL, D_MODEL),
        pl.BlockSpec((None, D_MODEL, D_MODEL), lambda p: (p, 0, 0)), N_DEV,
        [((D_MODEL, t), BF16), ((t, D_MODEL), BF16), ((D_MODEL, D_MODEL), F32)], True)
    land_i = _exchange_sibling("w_in_grads_to_sibling", 4, [g_in], [lambda ref, j: ref.at[_pos_of_dev(j)]],
                               [(D_MODEL, D_MODEL)])

    big = {}
    for nme, own, lnd, w, m, v in zip(
            names_f + names_b, part_f + part_b, landed_f + landed_b,
            [w_gate_up, w_down, w_branch_a, w_branch_b, w_out], [m_w_gate_up, m_w_down, m_w_branch_a, m_w_branch_b, m_w_out],
            [v_w_gate_up, v_w_down, v_w_branch_a, v_w_branch_b, v_w_out]):
        big[nme] = [o_[None] for o_ in _adamw("adamw_" + nme, chip, own, lnd, w[0], m[0], v[0])]
    small_outs = _adamw_small("adamw_small", small_all, w_pack, m_pack, v_pack)
    ws_outs = _adamw_small("adamw_w_s", ws_all, flat_ws(gmlp_w_s), flat_ws(m_gmlp_w_s), flat_ws(v_gmlp_w_s))
    land_i, _ = lax.optimization_barrier((land_i, (big, small_outs, ws_outs)))
    part_i = chip_partials(["w_in"], [g_in], land_i,
                           [((None, 256, D_MODEL), lambda q, r, c: (_pos_of_dev(2 * q + c), r, 0))])
    landed_i = _exchange_chips("w_in_grads_to_chips", 7, part_i)

    dx1, _ = lax.optimization_barrier((dx1, part_i))
    grad_x, d_mix_g = _input_backward(dproj, w_in_g, x2d, dx1, norm_mix_g)
    big["w_in"] = [o_[None] for o_ in _adamw("adamw_w_in", chip, part_i[0], landed_i[0], w_in[0], m_w_in[0], v_w_in[0])]

    def row8(a):
        return jnp.pad(a, ((0, 7), (0, 0)))

    d_mix_g, _ = lax.optimization_barrier((d_mix_g, landed_i))
    (mix_all,) = _all_gather_async("mix_gain_grad_all_gather", 8, [row8(d_mix_g)], [((N_DEV, 8, D_MODEL), F32)],
                                   [lambda ref, j: ref.at[j]])
    mix_outs = _adamw_small("adamw_mix_gain", mix_all, row8(norm_mix_g), row8(m_norm_mix_g), row8(v_norm_mix_g))
    small = [dict(_unpack_small(p, ws), norm_mix_g=q[0:1]) for p, ws, q in zip(small_outs, ws_outs, mix_outs)]

    loss = lax.psum(loss_tile[0, 0], ("x", "y", "c"))
    order = ["norm_mix_g", "w_in", "gmlp_ln_g", "gmlp_ln_b", "gmlp_w_s", "gmlp_b_s", "hgrn_lb_table", "hgrn_norm_g",
             "w_branch_a", "w_branch_b", "w_out", "norm_ffn_g", "w_gate_up", "w_down", "norm_final_g"]
    outs = [loss, grad_x.reshape(1, t, D_MODEL)]
    for kind in range(4):
        for nme in order:
            outs.append(big[nme][kind] if nme in big else small[kind][nme])
    return tuple(outs)
```

```python
import functools

import jax
import jax.numpy as jnp
from jax import lax
from jax.experimental import pallas as pl
from jax.experimental.pallas import tpu as pltpu
from jax.experimental.pallas import tpu_sc as plsc

F32, BF16 = jnp.float32, jnp.bfloat16
D_MODEL = 1024
N_DEV = 8
HEADS = 8
HEAD_DIM = 128
GROUPS = 8
GMLP_CHUNK = 128
HGRN_CHUNK = 64
HGRN_SCALE = HEAD_DIM ** -0.5
D_FF = 2816
FF_BLOCK = D_FF // 4
DOWN_ROWS = D_FF // N_DEV
BRANCH_ROWS = D_MODEL // N_DEV
NORM_EPS = 1e-6
ADAM_LR, ADAM_B1, ADAM_B2, ADAM_EPS, ADAM_WD, ADAM_STEP = 0.001, 0.9, 0.999, 1e-08, 0.01, 10
SMALL_ROWS = 64
V7X_VMEM_BYTES = 64 * 1024 * 1024
VMEM_CAP = V7X_VMEM_BYTES - 6 * 1024 * 1024
MESH_ID = pl.DeviceIdType.MESH
ANY = pl.BlockSpec(memory_space=pl.ANY)
RESIDENT = pl.BlockSpec(memory_space=pltpu.VMEM)
Q_POS, U_POS, GATE_POS = 0, 4, 6


def _pos_of_dev(j):
    return jnp.where(j < 2, j + 4, jnp.where(j < 6, j - 2, j))


def _dev_of_pos(p):
    return jnp.where(p < 4, p + 2, jnp.where(p < 6, p - 4, p))


def _nbytes(shape, dtype):
    n = 1
    for s in shape:
        n *= s
    return n * jnp.dtype(dtype).itemsize


def _params(blocks, scratch=(), temps=0, sem=None):
    need = 2 * sum(_nbytes(s, d) for s, d in blocks) + sum(_nbytes(s, d) for s, d in scratch) + temps
    assert need + (4 << 20) <= VMEM_CAP, need
    return pltpu.CompilerParams(dimension_semantics=sem, vmem_limit_bytes=VMEM_CAP)


def _tile(n, pref):
    return pref if n % pref == 0 else n


def _dot(a, b):
    return jnp.dot(a, b, preferred_element_type=F32)


def _dot_nt(a, b):
    return lax.dot_general(a, b, (((1,), (1,)), ((), ())), preferred_element_type=F32)


def _dot_tn(a, b):
    return lax.dot_general(a, b, (((0,), (0,)), ((), ())), preferred_element_type=F32)


def _sigmoid(x):
    return 1.0 / (1.0 + jnp.exp(-x))


_GELU_C = 0.7978845608028654


def _gelu(x):
    return x * (0.5 * (1.0 + jnp.tanh(_GELU_C * (x + 0.044715 * (x * x * x)))))


def _gelu_grad(x):
    t = jnp.tanh(_GELU_C * (x + 0.044715 * (x * x * x)))
    return 0.5 * (1.0 + t) + 0.5 * x * (1.0 - t * t) * (_GELU_C * (1.0 + 3.0 * 0.044715 * x * x))


def _rms_stats(x):
    r = lax.rsqrt(jnp.mean(x * x, axis=-1, keepdims=True) + NORM_EPS)
    return r, x * r


def _rms_bwd(dy, x, g):
    r, xh = _rms_stats(x)
    dg = jnp.sum(dy * xh, axis=0, keepdims=True)
    dxh = dy * g
    dx = r * (dxh - xh * jnp.mean(dxh * xh, axis=-1, keepdims=True))
    return dx, dg


def _split3(x):
    hi = x.astype(BF16)
    r = x - hi.astype(F32)
    mid = r.astype(BF16)
    lo = (r - mid.astype(F32)).astype(BF16)
    return hi, mid, lo


def _mask_mm(mask_bf16, x):
    hi, mid, lo = _split3(x)
    return _dot(mask_bf16, hi) + _dot(mask_bf16, mid) + _dot(mask_bf16, lo)


def _place():
    return lax.axis_index("x"), lax.axis_index("y"), lax.axis_index("c")


def _gather_copies(src, out, send, recv, loc, slicers):
    n = len(src)
    x, y, c = _place()
    me, sib = (x, y, c), (x, y, 1 - c)
    chips = [(1 - x, y), (x, 1 - y), (1 - x, 1 - y)]

    def dev(p):
        return 4 * p[0] + 2 * p[1] + p[2]

    def rc(i, k, block, to, from_src=False):
        dst = slicers[i](out[i], dev(block))
        return pltpu.make_async_remote_copy(
            src_ref=src[i] if from_src else dst, dst_ref=dst, send_sem=send.at[7 * i + k],
            recv_sem=recv.at[7 * i + k], device_id=to, device_id_type=MESH_ID)

    mine = [pltpu.make_async_copy(src[i], slicers[i](out[i], dev(me)), loc.at[i]) for i in range(n)]
    for cp in mine:
        cp.start()
    first = []
    for i in range(n):
        first.append(rc(i, 0, me, sib, True))
        for j, chip in enumerate(chips):
            first.append(rc(i, 1 + j, me, (*chip, c), True))
    for cp in first:
        cp.start()
    passed = []
    for j, chip in enumerate(chips):
        for i in range(n):
            rc(i, 1 + j, (*chip, c), me).wait_recv()
            cp = rc(i, 4 + j, (*chip, c), sib)
            cp.start()
            passed.append(cp)
    for i in range(n):
        rc(i, 0, sib, me).wait_recv()
        for j, chip in enumerate(chips):
            rc(i, 4 + j, (*chip, 1 - c), me).wait_recv()
    for cp in first + passed:
        cp.wait_send()
    for cp in mine:
        cp.wait()


def _gather_scratch(n):
    return [pltpu.SemaphoreType.DMA((7 * n,)), pltpu.SemaphoreType.DMA((7 * n,)), pltpu.SemaphoreType.DMA((n,))]


def _all_gather(name, srcs, out_shapes, slicers):
    n = len(srcs)

    def body(*refs):
        _gather_copies(refs[:n], refs[n:2 * n], *refs[2 * n:], slicers)

    return pl.pallas_call(
        body, name=name, out_shape=[jax.ShapeDtypeStruct(s, d) for s, d in out_shapes],
        in_specs=[ANY] * n, out_specs=[ANY] * n, scratch_shapes=_gather_scratch(n),
    )(*srcs)


def _handshake(peers):
    barrier = pltpu.get_barrier_semaphore()
    for peer in peers:
        pl.semaphore_signal(barrier, inc=1, device_id=peer, device_id_type=MESH_ID)
    pl.semaphore_wait(barrier, len(peers))


def _all_gather_async(name, collective_id, srcs, out_shapes, slicers):
    n = len(srcs)

    def body(*refs):
        x, y, c = _place()
        _handshake([(1 - x if dx else x, 1 - y if dy else y, 1 - c if dc else c)
                    for dx in (0, 1) for dy in (0, 1) for dc in (0, 1) if dx or dy or dc])
        _gather_copies(refs[:n], refs[n:2 * n], *refs[2 * n:], slicers)

    return _sequencer_call(name, collective_id, body, srcs, [jax.ShapeDtypeStruct(s, d) for s, d in out_shapes],
                           _gather_scratch(n))


def _sequencer_call(name, collective_id, body, operands, out_types, scratch):
    return pl.kernel(
        body, out_type=out_types, mesh=plsc.ScalarSubcoreMesh(axis_name="sequencer", num_cores=1), name=name,
        scratch_types=scratch, compiler_params=pltpu.CompilerParams(collective_id=collective_id),
    )(*operands)


def _exchange_sibling(name, collective_id, grads, shard_fns, shard_shapes):
    n = len(grads)

    def body(*refs):
        g, land = refs[:n], refs[n:2 * n]
        send, recv = refs[2 * n:]
        x, y, c = _place()
        _handshake([(x, y, 1 - c)])
        remote = []
        for i in range(n):
            for q in range(4):
                cp = pltpu.make_async_remote_copy(
                    src_ref=shard_fns[i](g[i], 2 * q + (1 - c)), dst_ref=land[i].at[q], send_sem=send.at[4 * i + q],
                    recv_sem=recv.at[4 * i + q], device_id=(x, y, 1 - c), device_id_type=MESH_ID)
                cp.start()
                remote.append(cp)
        for cp in remote:
            cp.wait()

    return _sequencer_call(name, collective_id, body, grads, [jax.ShapeDtypeStruct((4, *s), F32) for s in shard_shapes],
                           [pltpu.SemaphoreType.DMA((4 * n,)), pltpu.SemaphoreType.DMA((4 * n,))])


def _exchange_chips(name, collective_id, parts):
    n = len(parts)

    def body(*refs):
        part, out = refs[:n], refs[n:2 * n]
        send, recv = refs[2 * n:]
        x, y, c = _place()
        _handshake([(1 - x, y, c), (x, 1 - y, c), (1 - x, 1 - y, c)])
        remote = []
        for i in range(n):
            for s in range(3):
                qx = 1 - x if (s + 1) // 2 else x
                qy = 1 - y if (s + 1) % 2 else y
                cp = pltpu.make_async_remote_copy(
                    src_ref=part[i].at[2 * qx + qy], dst_ref=out[i].at[s], send_sem=send.at[3 * i + s],
                    recv_sem=recv.at[3 * i + s], device_id=(qx, qy, c), device_id_type=MESH_ID)
                cp.start()
                remote.append(cp)
        for cp in remote:
            cp.wait()

    return _sequencer_call(name, collective_id, body, parts,
                           [jax.ShapeDtypeStruct((3, *p.shape[1:]), p.dtype) for p in parts],
                           [pltpu.SemaphoreType.DMA((3 * n,)), pltpu.SemaphoreType.DMA((3 * n,))])


def _chip_partial(name, core, grad, own_block, own_index, land):
    _, rows, cols = land.shape
    tr = own_block[-2]

    def body(core_ref, a_ref, b_ref, o_ref):
        o_ref[...] = (a_ref[...] + b_ref[...]).astype(BF16)

    spec = pl.BlockSpec((None, tr, cols), lambda q, r, c: (q, r, 0))
    return pl.pallas_call(
        body, name=name, out_shape=jax.ShapeDtypeStruct(land.shape, BF16),
        grid_spec=pltpu.PrefetchScalarGridSpec(
            num_scalar_prefetch=1, grid=(4, rows // tr),
            in_specs=[pl.BlockSpec(own_block, lambda q, r, c: own_index(q, r, c[0])), spec], out_specs=spec),
        compiler_params=_params([((tr, cols), F32)] * 2 + [((tr, cols), BF16)], sem=("arbitrary", "arbitrary")),
    )(core, grad, land)


def _adamw_math(w, g, m, v):
    m = ADAM_B1 * m + (1.0 - ADAM_B1) * g
    v = ADAM_B2 * v + (1.0 - ADAM_B2) * (g * g)
    m_hat = m / (1.0 - ADAM_B1 ** ADAM_STEP)
    v_hat = v / (1.0 - ADAM_B2 ** ADAM_STEP)
    delta = -ADAM_LR * (m_hat / (jnp.sqrt(v_hat) + ADAM_EPS) + ADAM_WD * w)
    return delta, m, v


def _adamw(name, chip, own, landed, w, m, v):
    _, rows, cols = own.shape
    tr = _tile(rows, 256) if rows % 256 == 0 else _tile(rows, 176)

    def body(chip_ref, own_ref, l_ref, w_ref, m_ref, v_ref, g_out, d_out, m_out, v_out):
        g = own_ref[...].astype(F32)
        for s in range(3):
            g = g + l_ref[s].astype(F32)
        delta, m_new, v_new = _adamw_math(w_ref[...], g, m_ref[...], v_ref[...])
        g_out[...] = g
        d_out[...] = delta
        m_out[...] = m_new
        v_out[...] = v_new

    spec = pl.BlockSpec((tr, cols), lambda r, c: (r, 0))
    return pl.pallas_call(
        body, name=name, out_shape=[jax.ShapeDtypeStruct((rows, cols), F32)] * 4,
        grid_spec=pltpu.PrefetchScalarGridSpec(
            num_scalar_prefetch=1, grid=(rows // tr,),
            in_specs=[pl.BlockSpec((None, tr, cols), lambda r, c: (c[0], r, 0)),
                      pl.BlockSpec((3, tr, cols), lambda r, c: (0, r, 0)), spec, spec, spec],
            out_specs=[spec] * 4),
        compiler_params=_params([((4, tr, cols), own.dtype)] + [((tr, cols), F32)] * 7, sem=("arbitrary",)),
    )(chip, own, landed, w, m, v)


def _proj_forward(x, gain, w_in_g):
    t = x.shape[0]
    tm = _tile(t, 1024)

    def body(x_ref, g_ref, w_ref, o_ref, h_ref, ht_ref):
        @pl.when(pl.program_id(1) == 0)
        def _():
            _, xh = _rms_stats(x_ref[...])
            h = (xh * g_ref[...]).astype(BF16)
            h_ref[...] = h
            ht_ref[...] = h.T

        o_ref[...] = _dot(h_ref[...], w_ref[pl.program_id(1)])

    tok = pl.BlockSpec((tm, D_MODEL), lambda m, p: (m, 0))
    return pl.pallas_call(
        body, name="proj_fwd",
        out_shape=[jax.ShapeDtypeStruct((N_DEV, t, D_MODEL), F32), jax.ShapeDtypeStruct((t, D_MODEL), BF16),
                   jax.ShapeDtypeStruct((D_MODEL, t), BF16)],
        grid=(t // tm, N_DEV),
        in_specs=[tok, pl.BlockSpec((1, D_MODEL), lambda m, p: (0, 0)), RESIDENT],
        out_specs=[pl.BlockSpec((None, tm, D_MODEL), lambda m, p: (p, m, 0)), tok,
                   pl.BlockSpec((D_MODEL, tm), lambda m, p: (0, m))],
        compiler_params=_params([((tm, D_MODEL), F32)] * 2 + [((tm, D_MODEL), BF16)] * 2,
                                scratch=[((N_DEV, D_MODEL, D_MODEL), BF16)], temps=6 << 20, sem=("arbitrary", "arbitrary")),
    )(x, gain, w_in_g)


def _masked_ws(ws_ref, g):
    row = lax.broadcasted_iota(jnp.int32, (GMLP_CHUNK, GMLP_CHUNK), 0)
    col = lax.broadcasted_iota(jnp.int32, (GMLP_CHUNK, GMLP_CHUNK), 1)
    return jnp.where(row >= col, ws_ref[g], 0.0).astype(BF16)


def _gmlp_forward(proj, ln_g, ln_b, w_s, bias_b):
    t = proj.shape[1]
    tm = _tile(t, 256)
    chunks = tm // GMLP_CHUNK

    def body(u_ref, v_ref, lng_ref, lnb_ref, ws_ref, bias_ref, a_ref, vn_scr):
        vv = _gelu(v_ref[...])
        mu = jnp.mean(vv, axis=-1, keepdims=True)
        cen = vv - mu
        var = jnp.mean(cen * cen, axis=-1, keepdims=True)
        vn_scr[...] = ((cen * lax.rsqrt(var + NORM_EPS)) * lng_ref[...] + lnb_ref[...]).astype(BF16)
        for g in range(GROUPS):
            wm = _masked_ws(ws_ref, g)
            cols = slice(g * HEAD_DIM, (g + 1) * HEAD_DIM)
            for c in range(chunks):
                rows = slice(c * GMLP_CHUNK, (c + 1) * GMLP_CHUNK)
                mixed = _dot(wm, vn_scr[rows, cols]) + bias_ref[g]
                a_ref[rows, cols] = (_gelu(u_ref[rows, cols]) * mixed).astype(BF16)

    small = pl.BlockSpec((GROUPS, GMLP_CHUNK, GMLP_CHUNK), lambda m: (0, 0, 0))
    vec = pl.BlockSpec((1, D_MODEL), lambda m: (0, 0))
    return pl.pallas_call(
        body, name="gmlp_fwd", out_shape=jax.ShapeDtypeStruct((t, D_MODEL), BF16), grid=(t // tm,),
        in_specs=[pl.BlockSpec((None, tm, D_MODEL), lambda m: (U_POS, m, 0)),
                  pl.BlockSpec((None, tm, D_MODEL), lambda m: (U_POS + 1, m, 0)), vec, vec, small, small],
        out_specs=pl.BlockSpec((tm, D_MODEL), lambda m: (m, 0)),
        scratch_shapes=[pltpu.VMEM((tm, D_MODEL), BF16)],
        compiler_params=_params([((tm, D_MODEL), F32)] * 2 + [((tm, D_MODEL), BF16)] + [((8, 128, 128), F32)] * 2,
                                scratch=[((tm, D_MODEL), BF16)], temps=8 << 20, sem=("arbitrary",)),
    )(proj, proj, ln_g, ln_b, w_s, bias_b)


def _lower_bound(tab_ref):
    t0, t1 = tab_ref[0:1, :], tab_ref[1:2, :]
    mx = jnp.maximum(t0, t1)
    e0, e1 = jnp.exp(t0 - mx), jnp.exp(t1 - mx)
    return e0 / (e0 + e1)


def _tri_masks():
    row = lax.broadcasted_iota(jnp.int32, (HGRN_CHUNK, HGRN_CHUNK), 0)
    col = lax.broadcasted_iota(jnp.int32, (HGRN_CHUNK, HGRN_CHUNK), 1)
    return row >= col, row <= col


def _chunk_rows(c):
    return slice(c * HGRN_CHUNK, (c + 1) * HGRN_CHUNK)


def _per_chunk(x, nc, fn):
    return jnp.concatenate([fn(x[_chunk_rows(c)]) for c in range(nc)], axis=0)


def _chunk_row_bcast(x, nc, i):
    return _per_chunk(x, nc, lambda xc: jnp.broadcast_to(xc[i:i + 1, :], (HGRN_CHUNK, HEAD_DIM)))


def _hgrn_gates(q, fl, lb, nc):
    lower, _ = _tri_masks()
    lower = lower.astype(BF16)
    s = _sigmoid(fl)
    f = lb + (1.0 - lb) * s
    k = 1.0 - f
    hi, mid, lo = _split3(jnp.log(f))
    a = jnp.concatenate([_dot(lower, hi[_chunk_rows(c)]) + _dot(lower, mid[_chunk_rows(c)]) + _dot(lower, lo[_chunk_rows(c)])
                         for c in range(nc)], axis=0)
    a_mid = _chunk_row_bcast(a, nc, HGRN_CHUNK // 2 - 1)
    a_last = _chunk_row_bcast(a, nc, HGRN_CHUNK - 1)
    qs = q * HGRN_SCALE
    e_in, e_out, e_end, e_all = jnp.exp(a - a_mid), jnp.exp(a_mid - a), jnp.exp(a_last - a), jnp.exp(a)
    decay = [jnp.exp(a[c * HGRN_CHUNK + HGRN_CHUNK - 1:(c + 1) * HGRN_CHUNK, :]) for c in range(nc)]
    return dict(s=s, f=f, k=k, decay=decay, e_in=e_in, e_out=e_out, e_end=e_end, e_all=e_all,
                qi=qs * e_in, ki=k * e_out, kd=k * e_end, qe=qs * e_all)


def _hgrn_forward(proj, lb_table, norm_g):
    t = proj.shape[1]
    tb = _tile(t, 1024)
    nc = tb // HGRN_CHUNK
    n_chunks = t // HGRN_CHUNK

    def body(q_ref, f_ref, i_ref, g_ref, tab_ref, ng_ref, og_ref, o_ref, st_ref, state):
        @pl.when(pl.program_id(1) == 0)
        def _():
            state[...] = jnp.zeros_like(state)

        lower, _ = _tri_masks()
        gt = _hgrn_gates(q_ref[...], f_ref[...], _lower_bound(tab_ref), nc)
        qi, ki, kd, qe = (gt[n].astype(BF16) for n in ("qi", "ki", "kd", "qe"))
        vb = i_ref[...].astype(BF16)
        o_intra, d_state = [], []
        for c in range(nc):
            rows = _chunk_rows(c)
            p = jnp.where(lower, _dot_nt(qi[rows], ki[rows]), 0.0).astype(BF16)
            o_intra.append(_dot(p, vb[rows]))
            d_state.append(_dot_tn(vb[rows], kd[rows]))
        st = state[...]
        outs = []
        for c in range(nc):
            st_ref[c] = st
            outs.append(o_intra[c] + _dot_nt(qe[_chunk_rows(c)], st.astype(BF16)))
            st = st * gt["decay"][c] + d_state[c]
        state[...] = st
        o = jnp.concatenate(outs, axis=0)
        o_ref[...] = o
        _, oh = _rms_stats(o)
        gz = g_ref[...]
        og_ref[...] = ((oh * ng_ref[...]) * (gz * _sigmoid(gz))).astype(BF16)

    def blk(p):
        return pl.BlockSpec((None, tb, HEAD_DIM), lambda h, n: (p, n, h))

    out_blk = pl.BlockSpec((tb, HEAD_DIM), lambda h, n: (n, h))
    return pl.pallas_call(
        body, name="hgrn_fwd",
        out_shape=[jax.ShapeDtypeStruct((t, D_MODEL), BF16), jax.ShapeDtypeStruct((t, D_MODEL), F32),
                   jax.ShapeDtypeStruct((HEADS, n_chunks, HEAD_DIM, HEAD_DIM), F32)],
        grid=(HEADS, t // tb),
        in_specs=[blk(Q_POS), blk(Q_POS + 1), blk(Q_POS + 2), blk(Q_POS + 3),
                  pl.BlockSpec((2, HEAD_DIM), lambda h, n: (0, h)), pl.BlockSpec((1, HEAD_DIM), lambda h, n: (0, h))],
        out_specs=[out_blk, out_blk, pl.BlockSpec((None, nc, HEAD_DIM, HEAD_DIM), lambda h, n: (h, n, 0, 0))],
        scratch_shapes=[pltpu.VMEM((HEAD_DIM, HEAD_DIM), F32)],
        compiler_params=_params([((tb, HEAD_DIM), F32)] * 6 + [((nc, HEAD_DIM, HEAD_DIM), F32)], temps=8 << 20,
                                sem=("arbitrary", "arbitrary")),
    )(proj, proj, proj, proj, lb_table, norm_g)


def _branch_out_forward(a, og, proj, x, w_a, w_b, w_out, ffn_g):
    t = x.shape[0]
    tm = _tile(t, 256)

    def body(a_ref, og_ref, ga_ref, gb_ref, x_ref, wa_ref, wb_ref, wo_ref, g_ref, ya_ref, yb_ref, mg_ref, x1_ref, h2_ref,
             h2t_ref):
        ya = _dot(a_ref[...], wa_ref[...])
        yb = _dot(og_ref[...], wb_ref[...])
        ya_ref[...] = ya
        yb_ref[...] = yb
        merged = (_sigmoid(ga_ref[...]) * ya + _sigmoid(gb_ref[...]) * yb).astype(BF16)
        mg_ref[...] = merged
        x1 = x_ref[...] + _dot(merged, wo_ref[...])
        x1_ref[...] = x1
        _, xh = _rms_stats(x1)
        h2 = (xh * g_ref[...]).astype(BF16)
        h2_ref[...] = h2
        h2t_ref[...] = h2.T

    tok = pl.BlockSpec((tm, D_MODEL), lambda m: (m, 0))
    wsp = pl.BlockSpec((D_MODEL, D_MODEL), lambda m: (0, 0))
    return pl.pallas_call(
        body, name="branch_out_fwd",
        out_shape=[jax.ShapeDtypeStruct((t, D_MODEL), F32), jax.ShapeDtypeStruct((t, D_MODEL), F32),
                   jax.ShapeDtypeStruct((t, D_MODEL), BF16), jax.ShapeDtypeStruct((t, D_MODEL), F32),
                   jax.ShapeDtypeStruct((t, D_MODEL), BF16), jax.ShapeDtypeStruct((D_MODEL, t), BF16)],
        grid=(t // tm,),
        in_specs=[tok, tok, pl.BlockSpec((None, tm, D_MODEL), lambda m: (GATE_POS, m, 0)),
                  pl.BlockSpec((None, tm, D_MODEL), lambda m: (GATE_POS + 1, m, 0)), tok, wsp, wsp, wsp,
                  pl.BlockSpec((1, D_MODEL), lambda m: (0, 0))],
        out_specs=[tok] * 5 + [pl.BlockSpec((D_MODEL, tm), lambda m: (0, m))],
        compiler_params=_params([((tm, D_MODEL), BF16)] * 5 + [((tm, D_MODEL), F32)] * 6 + [((D_MODEL, D_MODEL), BF16)] * 3,
                                temps=8 << 20, sem=("arbitrary",)),
    )(a, og, proj, proj, x, w_a, w_b, w_out, ffn_g)


def _ffn_forward(h2, x1, w_gu, w_down, target, final_g):
    t = x1.shape[0]
    tm = _tile(t, 512)

    def body(h_ref, wgu_ref, wd_ref, x1_ref, t_ref, g_ref, gu_ref, act_ref, loss_ref, dg_ref, dx_ref, dxb_ref, acc):
        m, j = pl.program_id(0), pl.program_id(1)

        @pl.when((m == 0) & (j == 0))
        def _():
            loss_ref[...] = jnp.zeros_like(loss_ref)
            dg_ref[...] = jnp.zeros_like(dg_ref)

        h = h_ref[...]
        gate = _dot(h, wgu_ref[j])
        up = _dot(h, wgu_ref[j + 4])
        gu_ref[0] = gate
        gu_ref[1] = up
        act = ((gate * _sigmoid(gate)) * up).astype(BF16)
        act_ref[...] = act
        part = _dot(act, wd_ref[j])

        @pl.when(j == 0)
        def _():
            acc[...] = part

        @pl.when((j > 0) & (j < 3))
        def _():
            acc[...] += part

        @pl.when(j == 3)
        def _():
            x2 = x1_ref[...] + (acc[...] + part)
            g = g_ref[...]
            r, xh = _rms_stats(x2)
            err = xh * g - t_ref[...]
            loss_ref[...] += 0.5 * jnp.sum(jnp.mean(err * err, axis=-1, keepdims=True), axis=0, keepdims=True)
            dy = err * (1.0 / D_MODEL)
            dg_ref[...] += jnp.sum(dy * xh, axis=0, keepdims=True)
            dxh = dy * g
            dx = r * (dxh - xh * jnp.mean(dxh * xh, axis=-1, keepdims=True))
            dx_ref[...] = dx
            dxb_ref[...] = dx.astype(BF16)

    tok = pl.BlockSpec((tm, D_MODEL), lambda m, j: (m, 0))
    vec = pl.BlockSpec((1, D_MODEL), lambda m, j: (0, 0))
    return pl.pallas_call(
        body, name="ffn_fwd",
        out_shape=[jax.ShapeDtypeStruct((4, 2, t, FF_BLOCK), F32), jax.ShapeDtypeStruct((4, t, FF_BLOCK), BF16),
                   jax.ShapeDtypeStruct((8, 128), F32), jax.ShapeDtypeStruct((1, D_MODEL), F32),
                   jax.ShapeDtypeStruct((t, D_MODEL), F32), jax.ShapeDtypeStruct((t, D_MODEL), BF16)],
        grid=(t // tm, 4),
        in_specs=[tok, RESIDENT, RESIDENT, tok, tok, vec],
        out_specs=[pl.BlockSpec((None, 2, tm, FF_BLOCK), lambda m, j: (j, 0, m, 0)),
                   pl.BlockSpec((None, tm, FF_BLOCK), lambda m, j: (j, m, 0)),
                   pl.BlockSpec((8, 128), lambda m, j: (0, 0)), vec, tok, tok],
        scratch_shapes=[pltpu.VMEM((tm, D_MODEL), F32)],
        compiler_params=_params([((tm, D_MODEL), BF16), ((tm, D_MODEL), F32), ((tm, D_MODEL), F32), ((2, tm, 768), F32),
                                 ((tm, 768), BF16), ((tm, D_MODEL), F32), ((tm, D_MODEL), BF16)],
                                scratch=[((tm, D_MODEL), F32), ((N_DEV, D_MODEL, 768), BF16), ((D_FF, D_MODEL), BF16)],
                                temps=6 << 20, sem=("arbitrary", "arbitrary")),
    )(h2, w_gu, w_down.reshape(4, FF_BLOCK, D_MODEL), x1, target, final_g)


def _ffn_backward(dx2b, dx2, gu, x1, w_gu, w_down, ffn_g):
    t = x1.shape[0]
    tm = _tile(t, 512)

    def body(dxb_ref, dx2_ref, gu_ref, x1_ref, wgu_ref, wd_ref, g_ref, dgu_ref, dx1_ref, dx1b_ref, dg_ref, acc):
        m, j = pl.program_id(0), pl.program_id(1)

        @pl.when((m == 0) & (j == 0))
        def _():
            dg_ref[...] = jnp.zeros_like(dg_ref)

        dact = _dot_nt(dxb_ref[...], wd_ref[j])
        gate, up = gu_ref[0], gu_ref[1]
        sg = _sigmoid(gate)
        dgate = (dact * up * (sg * (1.0 + gate * (1.0 - sg)))).astype(BF16)
        dup = (dact * (gate * sg)).astype(BF16)
        dgu_ref[0] = dgate
        dgu_ref[1] = dup
        part = _dot_nt(dgate, wgu_ref[j]) + _dot_nt(dup, wgu_ref[j + 4])

        @pl.when(j == 0)
        def _():
            acc[...] = part

        @pl.when(j > 0)
        def _():
            acc[...] += part

        @pl.when(j == 3)
        def _():
            dx, dg = _rms_bwd(acc[...], x1_ref[...], g_ref[...])
            dx1 = dx2_ref[...] + dx
            dx1_ref[...] = dx1
            dx1b_ref[...] = dx1.astype(BF16)
            dg_ref[...] += dg

    tok = pl.BlockSpec((tm, D_MODEL), lambda m, j: (m, 0))
    vec = pl.BlockSpec((1, D_MODEL), lambda m, j: (0, 0))
    gu_spec = pl.BlockSpec((None, 2, tm, FF_BLOCK), lambda m, j: (j, 0, m, 0))
    return pl.pallas_call(
        body, name="ffn_bwd",
        out_shape=[jax.ShapeDtypeStruct((4, 2, t, FF_BLOCK), BF16), jax.ShapeDtypeStruct((t, D_MODEL), F32),
                   jax.ShapeDtypeStruct((t, D_MODEL), BF16), jax.ShapeDtypeStruct((1, D_MODEL), F32)],
        grid=(t // tm, 4),
        in_specs=[tok, tok, gu_spec, tok, RESIDENT, RESIDENT, vec],
        out_specs=[gu_spec, tok, tok, vec],
        scratch_shapes=[pltpu.VMEM((tm, D_MODEL), F32)],
        compiler_params=_params([((tm, D_MODEL), BF16), ((tm, D_MODEL), F32), ((2, tm, 768), F32), ((tm, D_MODEL), F32),
                                 ((2, tm, 768), BF16), ((tm, D_MODEL), F32), ((tm, D_MODEL), BF16)],
                                scratch=[((tm, D_MODEL), F32), ((N_DEV, D_MODEL, 768), BF16), ((D_FF, D_MODEL), BF16)],
                                temps=5 << 20, sem=("arbitrary", "arbitrary")),
    )(dx2b, dx2, gu, x1, w_gu, w_down.reshape(4, FF_BLOCK, D_MODEL), ffn_g)


def _branch_out_backward(dx1b, ya, yb, proj, w_a, w_b, w_out):
    t = ya.shape[0]
    tm = _tile(t, 256)

    def body(dx_ref, ya_ref, yb_ref, ga_ref, gb_ref, wa_ref, wb_ref, wo_ref, dya_ref, dyb_ref, dgate_ref, da_ref, dog_ref):
        dm = _dot_nt(dx_ref[...], wo_ref[...])
        sa, sb = _sigmoid(ga_ref[...]), _sigmoid(gb_ref[...])
        dya = (dm * sa).astype(BF16)
        dyb = (dm * sb).astype(BF16)
        dya_ref[...] = dya
        dyb_ref[...] = dyb
        dgate_ref[0] = (dm * ya_ref[...] * (sa * (1.0 - sa))).astype(BF16)
        dgate_ref[1] = (dm * yb_ref[...] * (sb * (1.0 - sb))).astype(BF16)
        da_ref[...] = _dot_nt(dya, wa_ref[...])
        dog_ref[...] = _dot_nt(dyb, wb_ref[...])

    tok = pl.BlockSpec((tm, D_MODEL), lambda m: (m, 0))
    wsp = pl.BlockSpec((D_MODEL, D_MODEL), lambda m: (0, 0))
    return pl.pallas_call(
        body, name="branch_out_bwd",
        out_shape=[jax.ShapeDtypeStruct((t, D_MODEL), BF16), jax.ShapeDtypeStruct((t, D_MODEL), BF16),
                   jax.ShapeDtypeStruct((N_DEV, t, D_MODEL), BF16), jax.ShapeDtypeStruct((t, D_MODEL), F32),
                   jax.ShapeDtypeStruct((t, D_MODEL), F32)],
        grid=(t // tm,),
        in_specs=[tok, tok, tok, pl.BlockSpec((None, tm, D_MODEL), lambda m: (GATE_POS, m, 0)),
                  pl.BlockSpec((None, tm, D_MODEL), lambda m: (GATE_POS + 1, m, 0)), wsp, wsp, wsp],
        out_specs=[tok, tok, pl.BlockSpec((2, tm, D_MODEL), lambda m: (GATE_POS // 2, m, 0)), tok, tok],
        compiler_params=_params([((tm, D_MODEL), BF16)] * 5 + [((tm, D_MODEL), F32)] * 6 + [((D_MODEL, D_MODEL), BF16)] * 3,
                                temps=8 << 20, sem=("arbitrary",)),
    )(dx1b, ya, yb, proj, proj, w_a, w_b, w_out)


def _hgrn_backward(dproj, dog, o_saved, states, proj, lb_table, norm_g):
    t = proj.shape[1]
    tb = _tile(t, 1024)
    nc = tb // HGRN_CHUNK
    nb = t // tb

    def body(_, dog_ref, o_ref, st_ref, q_ref, f_ref, i_ref, g_ref, tab_ref, ng_ref, dp_ref, dng_ref, dtab_ref, gstate):
        @pl.when(pl.program_id(1) == 0)
        def _():
            gstate[...] = jnp.zeros_like(gstate)
            dng_ref[...] = jnp.zeros_like(dng_ref)
            dtab_ref[...] = jnp.zeros_like(dtab_ref)

        lb = _lower_bound(tab_ref)
        ng = ng_ref[...]
        lower, upper = _tri_masks()
        gt = _hgrn_gates(q_ref[...], f_ref[...], lb, nc)
        qi, ki, kd, qe = (gt[n].astype(BF16) for n in ("qi", "ki", "kd", "qe"))
        vb = i_ref[...].astype(BF16)
        o, gz, d_og = o_ref[...], g_ref[...], dog_ref[...]
        r, oh = _rms_stats(o)
        sg = _sigmoid(gz)
        d_on = d_og * (gz * sg)
        dgz = d_og * (oh * ng) * (sg * (1.0 + gz * (1.0 - sg)))
        dng_ref[...] += jnp.sum(d_on * oh, axis=0, keepdims=True)
        doh = d_on * ng
        dob = (r * (doh - oh * jnp.mean(doh * oh, axis=-1, keepdims=True))).astype(BF16)
        dv_intra, dqi, dki, dqe, g_upd = [], [], [], [], []
        for c in range(nc):
            rows = _chunk_rows(c)
            p = jnp.where(lower, _dot_nt(qi[rows], ki[rows]), 0.0).astype(BF16)
            dv_intra.append(_dot_tn(p, dob[rows]))
            dp = jnp.where(lower, _dot_nt(dob[rows], vb[rows]), 0.0).astype(BF16)
            dqi.append(_dot(dp, ki[rows]))
            dki.append(_dot_tn(dp, qi[rows]))
            dqe.append(_dot(dob[rows], st_ref[c].astype(BF16)))
            g_upd.append(_dot_tn(dob[rows], qe[rows]))
        g_after = [None] * nc
        g = gstate[...]
        for c in reversed(range(nc)):
            g_after[c] = g
            g = g * gt["decay"][c] + g_upd[c]
        gstate[...] = g
        dkd, dv, da_last = [], [], []
        for c in range(nc):
            rows = _chunk_rows(c)
            gb = g_after[c].astype(BF16)
            dkd.append(_dot(vb[rows], gb))
            dv.append(dv_intra[c] + _dot_nt(kd[rows], gb))
            da_last.append(jnp.sum(g_after[c] * st_ref[c], axis=0, keepdims=True) * gt["decay"][c])
        dqi, dki, dqe, dkd, dv = (jnp.concatenate(z, axis=0) for z in (dqi, dki, dqe, dkd, dv))
        dqs = dqi * gt["e_in"] + dqe * gt["e_all"]
        dk = dki * gt["e_out"] + dkd * gt["e_end"]
        t_in, t_out, t_end = dqi * gt["qi"], dki * gt["ki"], dkd * gt["kd"]
        da = t_in - t_out + dqe * gt["qe"] - t_end
        row = lax.broadcasted_iota(jnp.int32, (HGRN_CHUNK, HEAD_DIM), 0)
        d_mid = t_out - t_in
        pieces = []
        for c in range(nc):
            rows = _chunk_rows(c)
            da_mid = jnp.sum(d_mid[rows], axis=0, keepdims=True)
            da_end = jnp.sum(t_end[rows], axis=0, keepdims=True) + da_last[c]
            da_c = da[rows] + jnp.where(row == HGRN_CHUNK // 2 - 1, da_mid, 0.0) + jnp.where(row == HGRN_CHUNK - 1, da_end, 0.0)
            pieces.append(_mask_mm(upper.astype(BF16), da_c))
        df = jnp.concatenate(pieces, axis=0) / gt["f"] - dk
        s = gt["s"]
        dlb = jnp.sum(df * (1.0 - s), axis=0, keepdims=True)
        dp_ref[0] = (dqs * HGRN_SCALE).astype(BF16)
        dp_ref[1] = (df * (1.0 - lb) * (s * (1.0 - s))).astype(BF16)
        dp_ref[2] = dv.astype(BF16)
        dp_ref[3] = dgz.astype(BF16)
        dt0 = dlb * (lb * (1.0 - lb))
        dtab_ref[0:1, :] += dt0
        dtab_ref[1:2, :] -= dt0

    def blk(p):
        return pl.BlockSpec((None, tb, HEAD_DIM), lambda h, n: (p, nb - 1 - n, h))

    tok = pl.BlockSpec((tb, HEAD_DIM), lambda h, n: (nb - 1 - n, h))
    return pl.pallas_call(
        body, name="hgrn_bwd",
        out_shape=[jax.ShapeDtypeStruct((N_DEV, t, D_MODEL), BF16), jax.ShapeDtypeStruct((1, D_MODEL), F32),
                   jax.ShapeDtypeStruct((2, D_MODEL), F32)],
        grid=(HEADS, nb),
        in_specs=[ANY, tok, tok, pl.BlockSpec((None, nc, HEAD_DIM, HEAD_DIM), lambda h, n: (h, nb - 1 - n, 0, 0)),
                  blk(Q_POS), blk(Q_POS + 1), blk(Q_POS + 2), blk(Q_POS + 3),
                  pl.BlockSpec((2, HEAD_DIM), lambda h, n: (0, h)), pl.BlockSpec((1, HEAD_DIM), lambda h, n: (0, h))],
        out_specs=[pl.BlockSpec((4, tb, HEAD_DIM), lambda h, n: (0, nb - 1 - n, h)),
                   pl.BlockSpec((1, HEAD_DIM), lambda h, n: (0, h)), pl.BlockSpec((2, HEAD_DIM), lambda h, n: (0, h))],
        scratch_shapes=[pltpu.VMEM((HEAD_DIM, HEAD_DIM), F32)],
        input_output_aliases={0: 0},
        compiler_params=_params([((tb, HEAD_DIM), F32)] * 6 + [((nc, HEAD_DIM, HEAD_DIM), F32)] + [((4, tb, HEAD_DIM), BF16)],
                                temps=8 << 20, sem=("arbitrary", "arbitrary")),
    )(dproj, dog, o_saved, states, proj, proj, proj, proj, lb_table, norm_g)


def _gmlp_backward(dproj, da, proj, ln_g, ln_b, w_s, bias_b):
    t = proj.shape[1]
    tm = _tile(t, 256)
    chunks = tm // GMLP_CHUNK

    def body(_, da_ref, u_ref, v_ref, lng_ref, lnb_ref, ws_ref, bias_ref, dp_ref, dlng_ref, dlnb_ref, dws_ref, dbs_ref,
             vn_scr, dvn_scr):
        @pl.when(pl.program_id(0) == 0)
        def _():
            dlng_ref[...] = jnp.zeros_like(dlng_ref)
            dlnb_ref[...] = jnp.zeros_like(dlnb_ref)
            dws_ref[...] = jnp.zeros_like(dws_ref)
            dbs_ref[...] = jnp.zeros_like(dbs_ref)

        v = v_ref[...]
        vv = _gelu(v)
        mu = jnp.mean(vv, axis=-1, keepdims=True)
        cen = vv - mu
        rstd = lax.rsqrt(jnp.mean(cen * cen, axis=-1, keepdims=True) + NORM_EPS)
        vhat = cen * rstd
        lng = lng_ref[...]
        vn_scr[...] = (vhat * lng + lnb_ref[...]).astype(BF16)
        row = lax.broadcasted_iota(jnp.int32, (GMLP_CHUNK, GMLP_CHUNK), 0)
        col = lax.broadcasted_iota(jnp.int32, (GMLP_CHUNK, GMLP_CHUNK), 1)
        for g in range(GROUPS):
            wm = _masked_ws(ws_ref, g)
            cols = slice(g * HEAD_DIM, (g + 1) * HEAD_DIM)
            dws = jnp.zeros((GMLP_CHUNK, GMLP_CHUNK), F32)
            dbs = jnp.zeros((GMLP_CHUNK, GMLP_CHUNK), F32)
            for c in range(chunks):
                rows = slice(c * GMLP_CHUNK, (c + 1) * GMLP_CHUNK)
                vn = vn_scr[rows, cols]
                mixed = _dot(wm, vn) + bias_ref[g]
                u = u_ref[rows, cols]
                d_a = da_ref[rows, cols]
                dp_ref[0, rows, cols] = (d_a * mixed * _gelu_grad(u)).astype(BF16)
                dmix = d_a * _gelu(u)
                dmb = dmix.astype(BF16)
                dbs = dbs + dmix
                dws = dws + _dot_nt(dmb, vn)
                dvn_scr[rows, cols] = _dot_tn(wm, dmb)
            dws_ref[g] += jnp.where(row >= col, dws, 0.0)
            dbs_ref[g] += jnp.broadcast_to(jnp.sum(dbs, axis=-1, keepdims=True), (GMLP_CHUNK, GMLP_CHUNK))
        dvn = dvn_scr[...]
        dlng_ref[...] += jnp.sum(dvn * vhat, axis=0, keepdims=True)
        dlnb_ref[...] += jnp.sum(dvn, axis=0, keepdims=True)
        dvh = dvn * lng
        dvv = rstd * (dvh - jnp.mean(dvh, axis=-1, keepdims=True) - vhat * jnp.mean(dvh * vhat, axis=-1, keepdims=True))
        dp_ref[1] = (dvv * _gelu_grad(v)).astype(BF16)

    tok = pl.BlockSpec((tm, D_MODEL), lambda m: (m, 0))
    small = pl.BlockSpec((GROUPS, GMLP_CHUNK, GMLP_CHUNK), lambda m: (0, 0, 0))
    vec = pl.BlockSpec((1, D_MODEL), lambda m: (0, 0))
    return pl.pallas_call(
        body, name="gmlp_bwd",
        out_shape=[jax.ShapeDtypeStruct(dproj.shape, BF16), jax.ShapeDtypeStruct((1, D_MODEL), F32),
                   jax.ShapeDtypeStruct((1, D_MODEL), F32), jax.ShapeDtypeStruct((GROUPS, GMLP_CHUNK, GMLP_CHUNK), F32),
                   jax.ShapeDtypeStruct((GROUPS, GMLP_CHUNK, GMLP_CHUNK), F32)],
        grid=(t // tm,),
        in_specs=[ANY, tok, pl.BlockSpec((None, tm, D_MODEL), lambda m: (U_POS, m, 0)),
                  pl.BlockSpec((None, tm, D_MODEL), lambda m: (U_POS + 1, m, 0)), vec, vec, small, small],
        out_specs=[pl.BlockSpec((2, tm, D_MODEL), lambda m: (U_POS // 2, m, 0)), vec, vec, small, small],
        scratch_shapes=[pltpu.VMEM((tm, D_MODEL), BF16), pltpu.VMEM((tm, D_MODEL), F32)],
        input_output_aliases={0: 0},
        compiler_params=_params([((tm, D_MODEL), F32)] * 3 + [((2, tm, D_MODEL), BF16)] + [((8, 128, 128), F32)] * 4,
                                scratch=[((tm, D_MODEL), BF16), ((tm, D_MODEL), F32)], temps=12 << 20, sem=("arbitrary",)),
    )(dproj, da, proj, proj, ln_g, ln_b, w_s, bias_b)


def _input_backward(dproj, w_in_g, x, dx1, mix_g):
    t = x.shape[0]
    tm = _tile(t, 512)

    def body(dp_ref, w_ref, x_ref, dx1_ref, g_ref, dx_ref, dg_ref):
        @pl.when(pl.program_id(0) == 0)
        def _():
            dg_ref[...] = jnp.zeros_like(dg_ref)

        dh = _dot_nt(dp_ref[0], w_ref[0])
        for p in range(1, N_DEV):
            dh = dh + _dot_nt(dp_ref[p], w_ref[p])
        dx, dg = _rms_bwd(dh, x_ref[...], g_ref[...])
        dx_ref[...] = dx1_ref[...] + dx
        dg_ref[...] += dg

    tok = pl.BlockSpec((tm, D_MODEL), lambda m: (m, 0))
    vec = pl.BlockSpec((1, D_MODEL), lambda m: (0, 0))
    return pl.pallas_call(
        body, name="input_bwd",
        out_shape=[jax.ShapeDtypeStruct((t, D_MODEL), F32), jax.ShapeDtypeStruct((1, D_MODEL), F32)],
        grid=(t // tm,),
        in_specs=[pl.BlockSpec((N_DEV, tm, D_MODEL), lambda m: (0, m, 0)), RESIDENT, tok, tok, vec],
        out_specs=[tok, vec],
        compiler_params=_params([((N_DEV, tm, D_MODEL), BF16)] + [((tm, D_MODEL), F32)] * 3,
                                scratch=[((N_DEV, D_MODEL, D_MODEL), BF16)], temps=6 << 20, sem=("arbitrary",)),
    )(dproj, w_in_g, x, dx1, mix_g)


def _weight_grad(name, a, b, a_spec, b_spec, out_shape, out_spec, steps, blocks, a_is_transposed):
    def body(a_ref, b_ref, o_ref):
        o_ref[...] = _dot(a_ref[...], b_ref[...]) if a_is_transposed else _dot_tn(a_ref[...], b_ref[...])

    return pl.pallas_call(
        body, name=name, out_shape=jax.ShapeDtypeStruct(out_shape, F32), grid=(steps,), in_specs=[a_spec, b_spec],
        out_specs=out_spec, compiler_params=_params(blocks, temps=4 << 20, sem=("arbitrary",)),
    )(a, b)


def _pack_small(mix_g, ln_g, ln_b, b_s, lb_table, hg_norm, ffn_g, final_g):
    def part(a):
        a = a.reshape(-1, D_MODEL)
        return jnp.pad(a, ((0, 8 - a.shape[0]), (0, 0)))

    return jnp.concatenate([part(mix_g), part(ln_g), part(ln_b), part(hg_norm), part(ffn_g), part(final_g),
                            part(lb_table), part(b_s)], axis=0)


def _unpack_small(pack, w_s):
    return dict(norm_mix_g=pack[0:1], gmlp_ln_g=pack[8:9], gmlp_ln_b=pack[16:17], hgrn_norm_g=pack[24:25],
                norm_ffn_g=pack[32:33], norm_final_g=pack[40], hgrn_lb_table=pack[48:50],
                gmlp_b_s=pack[56:57].reshape(1, GROUPS, GMLP_CHUNK),
                gmlp_w_s=w_s.reshape(1, GROUPS, GMLP_CHUNK, GMLP_CHUNK))


def _adamw_small(name, gathered, w, m, v):
    rows, cols = w.shape

    def body(p_ref, w_ref, m_ref, v_ref, g_out, d_out, m_out, v_out):
        g = p_ref[0]
        for j in range(1, N_DEV):
            g = g + p_ref[j]
        delta, m_new, v_new = _adamw_math(w_ref[...], g, m_ref[...], v_ref[...])
        g_out[...] = g
        d_out[...] = delta
        m_out[...] = m_new
        v_out[...] = v_new

    tr = _tile(rows, 512)
    spec = pl.BlockSpec((tr, cols), lambda r: (r, 0))
    return pl.pallas_call(
        body, name=name, out_shape=[jax.ShapeDtypeStruct((rows, cols), F32)] * 4, grid=(rows // tr,),
        in_specs=[pl.BlockSpec((N_DEV, tr, cols), lambda r: (0, r, 0)), spec, spec, spec], out_specs=[spec] * 4,
        compiler_params=_params([((N_DEV, tr, cols), F32)] + [((tr, cols), F32)] * 7, sem=("arbitrary",)),
    )(gathered, w, m, v)


def kernel(x, norm_mix_g, w_in, gmlp_ln_g, gmlp_ln_b, gmlp_w_s, gmlp_b_s, hgrn_lb_table, hgrn_norm_g, w_branch_a, w_branch_b, w_out, norm_ffn_g, w_gate_up, w_down, norm_final_g, loss_target, m_norm_mix_g, m_w_in, m_gmlp_ln_g, m_gmlp_ln_b, m_gmlp_w_s, m_gmlp_b_s, m_hgrn_lb_table, m_hgrn_norm_g, m_w_branch_a, m_w_branch_b, m_w_out, m_norm_ffn_g, m_w_gate_up, m_w_down, m_norm_final_g, v_norm_mix_g, v_w_in, v_gmlp_ln_g, v_gmlp_ln_b, v_gmlp_w_s, v_gmlp_b_s, v_hgrn_lb_table, v_hgrn_norm_g, v_w_branch_a, v_w_branch_b, v_w_out, v_norm_ffn_g, v_w_gate_up, v_w_down, v_norm_final_g):
    t = x.shape[1]
    x2d = x.reshape(t, D_MODEL)
    target = loss_target.reshape(t, D_MODEL)
    final_g = norm_final_g.reshape(1, D_MODEL)

    shards = [w_in[0].astype(BF16), w_branch_a[0].astype(BF16), w_branch_b[0].astype(BF16), w_out[0].astype(BF16),
              w_gate_up[0].astype(BF16), w_down[0].astype(BF16)]

    def rows_of(n):
        return lambda ref, j: ref.at[pl.ds(pl.multiple_of(j * n, 8), n)]

    gathered = [((N_DEV, D_MODEL, D_MODEL), BF16), ((D_MODEL, D_MODEL), BF16), ((D_MODEL, D_MODEL), BF16),
                ((D_MODEL, D_MODEL), BF16), ((N_DEV, D_MODEL, FF_BLOCK), BF16), ((D_FF, D_MODEL), BF16)]
    places = [lambda ref, j: ref.at[_pos_of_dev(j)], rows_of(BRANCH_ROWS), rows_of(BRANCH_ROWS), rows_of(BRANCH_ROWS),
              lambda ref, j: ref.at[j], rows_of(DOWN_ROWS)]
    (w_in_g,) = _all_gather("w_in_all_gather", shards[:1], gathered[:1], places[:1])
    _, later = lax.optimization_barrier((w_in_g, shards[1:]))
    w_a, w_b, w_o, w_gu, w_dn = _all_gather_async("weights_all_gather", 0, later, gathered[1:], places[1:])

    proj, h, h_t = _proj_forward(x2d, norm_mix_g, w_in_g)
    bias_b = jnp.broadcast_to(gmlp_b_s[0][:, :, None], (GROUPS, GMLP_CHUNK, GMLP_CHUNK))
    a = _gmlp_forward(proj, gmlp_ln_g, gmlp_ln_b, gmlp_w_s[0], bias_b)
    og, o_saved, states = _hgrn_forward(proj, hgrn_lb_table, hgrn_norm_g)
    ya, yb, merged, x1, h2, h2_t = _branch_out_forward(a, og, proj, x2d, w_a, w_b, w_o, norm_ffn_g)
    gu, act, loss_tile, d_final_g, dx2, dx2b = _ffn_forward(h2, x1, w_gu, w_dn, target, final_g)

    core = lax.axis_index("c").astype(jnp.int32).reshape(1)
    chip = (2 * lax.axis_index("x") + lax.axis_index("y")).astype(jnp.int32).reshape(1)
    branch_rows, branch_shape = rows_of(BRANCH_ROWS), (BRANCH_ROWS, D_MODEL)
    branch_block = ((BRANCH_ROWS, D_MODEL), lambda q, r, c: (2 * q + c, 0))

    def chip_partials(names, grads, land, own_blocks):
        return [_chip_partial("chip_partial_" + nme, core, g_, blk, idx, l_)
                for nme, g_, (blk, idx), l_ in zip(names, grads, own_blocks, land)]

    whole = pl.BlockSpec((t, D_MODEL), lambda n: (0, 0))
    whole_t = pl.BlockSpec((D_MODEL, t), lambda n: (0, 0))
    col_blocks = [((t, D_MODEL), BF16), ((t, 256), BF16), ((D_MODEL, 256), F32)]

    def square_grad(name, a_, b_):
        return _weight_grad(name, a_, b_, whole, pl.BlockSpec((t, 256), lambda n: (0, n)), (D_MODEL, D_MODEL),
                            pl.BlockSpec((D_MODEL, 256), lambda n: (0, n)), D_MODEL // 256, col_blocks, False)

    dgu, dx1, dx1b, d_ffn_g = _ffn_backward(dx2b, dx2, gu, x1, w_gu, w_dn, norm_ffn_g)
    g_gu = _weight_grad(
        "grad_w_gate_up", h2_t, dgu, whole_t, pl.BlockSpec((None, None, t, FF_BLOCK), lambda j: (j % 4, j // 4, 0, 0)),
        (N_DEV, D_MODEL, FF_BLOCK), pl.BlockSpec((None, D_MODEL, FF_BLOCK), lambda j: (j, 0, 0)), N_DEV,
        [((D_MODEL, t), BF16), ((t, 768), BF16), ((D_MODEL, 768), F32)], True)
    g_dn = _weight_grad(
        "grad_w_down", act, dx2b, pl.BlockSpec((None, t, FF_BLOCK), lambda j: (j, 0, 0)), whole, (D_FF, D_MODEL),
        pl.BlockSpec((FF_BLOCK, D_MODEL), lambda j: (j, 0)), 4,
        [((t, 768), BF16), ((t, D_MODEL), BF16), ((FF_BLOCK, D_MODEL), F32)], False)
    names_f, grads_f = ["w_gate_up", "w_down"], [g_gu, g_dn]
    land_f = _exchange_sibling("ffn_grads_to_sibling", 2, grads_f, [lambda ref, j: ref.at[j], rows_of(DOWN_ROWS)],
                               [(D_MODEL, FF_BLOCK), (DOWN_ROWS, D_MODEL)])

    dya, dyb, dproj, da, dog = _branch_out_backward(dx1b, ya, yb, proj, w_a, w_b, w_o)
    g_a = square_grad("grad_w_a", a, dya)
    g_b = square_grad("grad_w_b", og, dyb)
    g_o = square_grad("grad_w_out", merged, dx1b)
    names_b, grads_b = ["w_branch_a", "w_branch_b", "w_out"], [g_a, g_b, g_o]
    land_b = _exchange_sibling("branch_grads_to_sibling", 3, grads_b, [branch_rows] * 3, [branch_shape] * 3)

    part_f = chip_partials(names_f, grads_f, land_f,
                           [((None, 256, FF_BLOCK), lambda q, r, c: (2 * q + c, r, 0)),
                            ((DOWN_ROWS // 2, D_MODEL), lambda q, r, c: (2 * (2 * q + c) + r, 0))])
    landed_f = _exchange_chips("ffn_grads_to_chips", 5, part_f)

    dog, _ = lax.optimization_barrier((dog, part_f))
    dproj, d_hg_norm, d_lb = _hgrn_backward(dproj, dog, o_saved, states, proj, hgrn_lb_table, hgrn_norm_g)

    part_b = chip_partials(names_b, grads_b, land_b, [branch_block] * 3)
    landed_b = _exchange_chips("branch_grads_to_chips", 6, part_b)

    da, _ = lax.optimization_barrier((da, part_b))
    dproj, d_ln_g, d_ln_b, d_ws, d_bs = _gmlp_backward(dproj, da, proj, gmlp_ln_g, gmlp_ln_b, gmlp_w_s[0], bias_b)

    def packed(vals):
        return _pack_small(*vals)

    def flat_ws(a):
        return a.reshape(GROUPS * GMLP_CHUNK, GMLP_CHUNK)

    w_pack = packed([norm_mix_g, gmlp_ln_g, gmlp_ln_b, gmlp_b_s, hgrn_lb_table, hgrn_norm_g, norm_ffn_g, norm_final_g])
    m_pack = packed([m_norm_mix_g, m_gmlp_ln_g, m_gmlp_ln_b, m_gmlp_b_s, m_hgrn_lb_table, m_hgrn_norm_g, m_norm_ffn_g, m_norm_final_g])
    v_pack = packed([v_norm_mix_g, v_gmlp_ln_g, v_gmlp_ln_b, v_gmlp_b_s, v_hgrn_lb_table, v_hgrn_norm_g, v_norm_ffn_g, v_norm_final_g])
    small_partial = _pack_small(jnp.zeros((1, D_MODEL), F32), d_ln_g, d_ln_b, d_bs[:, :, 0], d_lb, d_hg_norm, d_ffn_g, d_final_g)
    small_all, ws_all = _all_gather_async(
        "small_grads_all_gather", 1, [small_partial, flat_ws(d_ws)],
        [((N_DEV, SMALL_ROWS, D_MODEL), F32), ((N_DEV, GROUPS * GMLP_CHUNK, GMLP_CHUNK), F32)],
        [lambda ref, j: ref.at[j], lambda ref, j: ref.at[j]])

    g_in = _weight_grad(
        "grad_w_in", h_t, dproj, whole_t, pl.BlockSpec((None, t, D_MODEL), lambda p: (p, 0, 0)), (N_DEV, D_MODEL, D_MODEL),
        pl.BlockSpec((None, D_MODEL, D_MODEL), lambda p: (p, 0, 0)), N_DEV,
        [((D_MODEL, t), BF16), ((t, D_MODEL), BF16), ((D_MODEL, D_MODEL), F32)], True)
    land_i = _exchange_sibling("w_in_grads_to_sibling", 4, [g_in], [lambda ref, j: ref.at[_pos_of_dev(j)]],
                               [(D_MODEL, D_MODEL)])

    big = {}
    for nme, own, lnd, w, m, v in zip(
            names_f + names_b, part_f + part_b, landed_f + landed_b,
            [w_gate_up, w_down, w_branch_a, w_branch_b, w_out], [m_w_gate_up, m_w_down, m_w_branch_a, m_w_branch_b, m_w_out],
            [v_w_gate_up, v_w_down, v_w_branch_a, v_w_branch_b, v_w_out]):
        big[nme] = [o_[None] for o_ in _adamw("adamw_" + nme, chip, own, lnd, w[0], m[0], v[0])]
    small_outs = _adamw_small("adamw_small", small_all, w_pack, m_pack, v_pack)
    ws_outs = _adamw_small("adamw_w_s", ws_all, flat_ws(gmlp_w_s), flat_ws(m_gmlp_w_s), flat_ws(v_gmlp_w_s))
    land_i, _ = lax.optimization_barrier((land_i, (big, small_outs, ws_outs)))
    part_i = chip_partials(["w_in"], [g_in], land_i,
                           [((None, 256, D_MODEL), lambda q, r, c: (_pos_of_dev(2 * q + c), r, 0))])
    landed_i = _exchange_chips("w_in_grads_to_chips", 7, part_i)

    dx1, _ = lax.optimization_barrier((dx1, part_i))
    grad_x, d_mix_g = _input_backward(dproj, w_in_g, x2d, dx1, norm_mix_g)
    big["w_in"] = [o_[None] for o_ in _adamw("adamw_w_in", chip, part_i[0], landed_i[0], w_in[0], m_w_in[0], v_w_in[0])]

    def row8(a):
        return jnp.pad(a, ((0, 7), (0, 0)))

    d_mix_g, _ = lax.optimization_barrier((d_mix_g, landed_i))
    (mix_all,) = _all_gather_async("mix_gain_grad_all_gather", 8, [row8(d_mix_g)], [((N_DEV, 8, D_MODEL), F32)],
                                   [lambda ref, j: ref.at[j]])
    mix_outs = _adamw_small("adamw_mix_gain", mix_all, row8(norm_mix_g), row8(m_norm_mix_g), row8(v_norm_mix_g))
    small = [dict(_unpack_small(p, ws), norm_mix_g=q[0:1]) for p, ws, q in zip(small_outs, ws_outs, mix_outs)]

    loss = lax.psum(loss_tile[0, 0], ("x", "y", "c"))
    order = ["norm_mix_g", "w_in", "gmlp_ln_g", "gmlp_ln_b", "gmlp_w_s", "gmlp_b_s", "hgrn_lb_table", "hgrn_norm_g",
             "w_branch_a", "w_branch_b", "w_out", "norm_ffn_g", "w_gate_up", "w_down", "norm_final_g"]
    outs = [loss, grad_x.reshape(1, t, D_MODEL)]
    for kind in range(4):
        for nme in order:
            outs.append(big[nme][kind] if nme in big else small[kind][nme])
    return tuple(outs)
```

```python
import functools

import jax
import jax.numpy as jnp
from jax import lax
from jax.experimental import pallas as pl
from jax.experimental.pallas import tpu as pltpu
from jax.experimental.pallas import tpu_sc as plsc

F32, BF16 = jnp.float32, jnp.bfloat16
D_MODEL = 1024
N_DEV = 8
HEADS = 8
HEAD_DIM = 128
GROUPS = 8
GMLP_CHUNK = 128
HGRN_CHUNK = 64
HGRN_SCALE = HEAD_DIM ** -0.5
D_FF = 2816
FF_BLOCK = D_FF // 4
DOWN_ROWS = D_FF // N_DEV
BRANCH_ROWS = D_MODEL // N_DEV
NORM_EPS = 1e-6
ADAM_LR, ADAM_B1, ADAM_B2, ADAM_EPS, ADAM_WD, ADAM_STEP = 0.001, 0.9, 0.999, 1e-08, 0.01, 10
SMALL_ROWS = 64
V7X_VMEM_BYTES = 64 * 1024 * 1024
VMEM_CAP = V7X_VMEM_BYTES - 6 * 1024 * 1024
MESH_ID = pl.DeviceIdType.MESH
ANY = pl.BlockSpec(memory_space=pl.ANY)
RESIDENT = pl.BlockSpec(memory_space=pltpu.VMEM)
Q_POS, U_POS, GATE_POS = 0, 4, 6


def _pos_of_dev(j):
    return jnp.where(j < 2, j + 4, jnp.where(j < 6, j - 2, j))


def _dev_of_pos(p):
    return jnp.where(p < 4, p + 2, jnp.where(p < 6, p - 4, p))


def _nbytes(shape, dtype):
    n = 1
    for s in shape:
        n *= s
    return n * jnp.dtype(dtype).itemsize


def _params(blocks, scratch=(), temps=0, sem=None):
    need = 2 * sum(_nbytes(s, d) for s, d in blocks) + sum(_nbytes(s, d) for s, d in scratch) + temps
    assert need + (4 << 20) <= VMEM_CAP, need
    return pltpu.CompilerParams(dimension_semantics=sem, vmem_limit_bytes=VMEM_CAP)


def _tile(n, pref):
    return pref if n % pref == 0 else n


def _dot(a, b):
    return jnp.dot(a, b, preferred_element_type=F32)


def _dot_nt(a, b):
    return lax.dot_general(a, b, (((1,), (1,)), ((), ())), preferred_element_type=F32)


def _dot_tn(a, b):
    return lax.dot_general(a, b, (((0,), (0,)), ((), ())), preferred_element_type=F32)


def _sigmoid(x):
    return 1.0 / (1.0 + jnp.exp(-x))


_GELU_C = 0.7978845608028654


def _gelu(x):
    return x * (0.5 * (1.0 + jnp.tanh(_GELU_C * (x + 0.044715 * (x * x * x)))))


def _gelu_grad(x):
    t = jnp.tanh(_GELU_C * (x + 0.044715 * (x * x * x)))
    return 0.5 * (1.0 + t) + 0.5 * x * (1.0 - t * t) * (_GELU_C * (1.0 + 3.0 * 0.044715 * x * x))


def _rms_stats(x):
    r = lax.rsqrt(jnp.mean(x * x, axis=-1, keepdims=True) + NORM_EPS)
    return r, x * r


def _rms_bwd(dy, x, g):
    r, xh = _rms_stats(x)
    dg = jnp.sum(dy * xh, axis=0, keepdims=True)
    dxh = dy * g
    dx = r * (dxh - xh * jnp.mean(dxh * xh, axis=-1, keepdims=True))
    return dx, dg


def _split3(x):
    hi = x.astype(BF16)
    r = x - hi.astype(F32)
    mid = r.astype(BF16)
    lo = (r - mid.astype(F32)).astype(BF16)
    return hi, mid, lo


def _mask_mm(mask_bf16, x):
    hi, mid, lo = _split3(x)
    return _dot(mask_bf16, hi) + _dot(mask_bf16, mid) + _dot(mask_bf16, lo)


def _place():
    return lax.axis_index("x"), lax.axis_index("y"), lax.axis_index("c")


def _gather_copies(src, out, send, recv, loc, slicers):
    n = len(src)
    x, y, c = _place()
    me, sib = (x, y, c), (x, y, 1 - c)
    chips = [(1 - x, y), (x, 1 - y), (1 - x, 1 - y)]

    def dev(p):
        return 4 * p[0] + 2 * p[1] + p[2]

    def rc(i, k, block, to, from_src=False):
        dst = slicers[i](out[i], dev(block))
        return pltpu.make_async_remote_copy(
            src_ref=src[i] if from_src else dst, dst_ref=dst, send_sem=send.at[7 * i + k],
            recv_sem=recv.at[7 * i + k], device_id=to, device_id_type=MESH_ID)

    mine = [pltpu.make_async_copy(src[i], slicers[i](out[i], dev(me)), loc.at[i]) for i in range(n)]
    for cp in mine:
        cp.start()
    first = []
    for i in range(n):
        first.append(rc(i, 0, me, sib, True))
        for j, chip in enumerate(chips):
            first.append(rc(i, 1 + j, me, (*chip, c), True))
    for cp in first:
        cp.start()
    passed = []
    for j, chip in enumerate(chips):
        for i in range(n):
            rc(i, 1 + j, (*chip, c), me).wait_recv()
            cp = rc(i, 4 + j, (*chip, c), sib)
            cp.start()
            passed.append(cp)
    for i in range(n):
        rc(i, 0, sib, me).wait_recv()
        for j, chip in enumerate(chips):
            rc(i, 4 + j, (*chip, 1 - c), me).wait_recv()
    for cp in first + passed:
        cp.wait_send()
    for cp in mine:
        cp.wait()


def _gather_scratch(n):
    return [pltpu.SemaphoreType.DMA((7 * n,)), pltpu.SemaphoreType.DMA((7 * n,)), pltpu.SemaphoreType.DMA((n,))]


def _all_gather(name, srcs, out_shapes, slicers):
    n = len(srcs)

    def body(*refs):
        _gather_copies(refs[:n], refs[n:2 * n], *refs[2 * n:], slicers)

    return pl.pallas_call(
        body, name=name, out_shape=[jax.ShapeDtypeStruct(s, d) for s, d in out_shapes],
        in_specs=[ANY] * n, out_specs=[ANY] * n, scratch_shapes=_gather_scratch(n),
    )(*srcs)


def _handshake(peers):
    barrier = pltpu.get_barrier_semaphore()
    for peer in peers:
        pl.semaphore_signal(barrier, inc=1, device_id=peer, device_id_type=MESH_ID)
    pl.semaphore_wait(barrier, len(peers))


def _all_gather_async(name, collective_id, srcs, out_shapes, slicers):
    n = len(srcs)

    def body(*refs):
        x, y, c = _place()
        _handshake([(1 - x if dx else x, 1 - y if dy else y, 1 - c if dc else c)
                    for dx in (0, 1) for dy in (0, 1) for dc in (0, 1) if dx or dy or dc])
        _gather_copies(refs[:n], refs[n:2 * n], *refs[2 * n:], slicers)

    return _sequencer_call(name, collective_id, body, srcs, [jax.ShapeDtypeStruct(s, d) for s, d in out_shapes],
                           _gather_scratch(n))


def _sequencer_call(name, collective_id, body, operands, out_types, scratch):
    return pl.kernel(
        body, out_type=out_types, mesh=plsc.ScalarSubcoreMesh(axis_name="sequencer", num_cores=1), name=name,
        scratch_types=scratch, compiler_params=pltpu.CompilerParams(collective_id=collective_id),
    )(*operands)


def _exchange_sibling(name, collective_id, grads, shard_fns, shard_shapes):
    n = len(grads)

    def body(*refs):
        g, land = refs[:n], refs[n:2 * n]
        send, recv = refs[2 * n:]
        x, y, c = _place()
        _handshake([(x, y, 1 - c)])
        remote = []
        for i in range(n):
            for q in range(4):
                cp = pltpu.make_async_remote_copy(
                    src_ref=shard_fns[i](g[i], 2 * q + (1 - c)), dst_ref=land[i].at[q], send_sem=send.at[4 * i + q],
                    recv_sem=recv.at[4 * i + q], device_id=(x, y, 1 - c), device_id_type=MESH_ID)
                cp.start()
                remote.append(cp)
        for cp in remote:
            cp.wait()

    return _sequencer_call(name, collective_id, body, grads, [jax.ShapeDtypeStruct((4, *s), F32) for s in shard_shapes],
                           [pltpu.SemaphoreType.DMA((4 * n,)), pltpu.SemaphoreType.DMA((4 * n,))])


def _exchange_chips(name, collective_id, parts):
    n = len(parts)

    def body(*refs):
        part, out = refs[:n], refs[n:2 * n]
        send, recv = refs[2 * n:]
        x, y, c = _place()
        _handshake([(1 - x, y, c), (x, 1 - y, c), (1 - x, 1 - y, c)])
        remote = []
        for i in range(n):
            for s in range(3):
                qx = 1 - x if (s + 1) // 2 else x
                qy = 1 - y if (s + 1) % 2 else y
                cp = pltpu.make_async_remote_copy(
                    src_ref=part[i].at[2 * qx + qy], dst_ref=out[i].at[s], send_sem=send.at[3 * i + s],
                    recv_sem=recv.at[3 * i + s], device_id=(qx, qy, c), device_id_type=MESH_ID)
                cp.start()
                remote.append(cp)
        for cp in remote:
            cp.wait()

    return _sequencer_call(name, collective_id, body, parts,
                           [jax.ShapeDtypeStruct((3, *p.shape[1:]), p.dtype) for p in parts],
                           [pltpu.SemaphoreType.DMA((3 * n,)), pltpu.SemaphoreType.DMA((3 * n,))])


def _chip_partial(name, core, grad, own_block, own_index, land):
    _, rows, cols = land.shape
    tr = own_block[-2]

    def body(core_ref, a_ref, b_ref, o_ref):
        o_ref[...] = (a_ref[...] + b_ref[...]).astype(BF16)

    spec = pl.BlockSpec((None, tr, cols), lambda q, r, c: (q, r, 0))
    return pl.pallas_call(
        body, name=name, out_shape=jax.ShapeDtypeStruct(land.shape, BF16),
        grid_spec=pltpu.PrefetchScalarGridSpec(
            num_scalar_prefetch=1, grid=(4, rows // tr),
            in_specs=[pl.BlockSpec(own_block, lambda q, r, c: own_index(q, r, c[0])), spec], out_specs=spec),
        compiler_params=_params([((tr, cols), F32)] * 2 + [((tr, cols), BF16)], sem=("arbitrary", "arbitrary")),
    )(core, grad, land)


def _adamw_math(w, g, m, v):
    m = ADAM_B1 * m + (1.0 - ADAM_B1) * g
    v = ADAM_B2 * v + (1.0 - ADAM_B2) * (g * g)
    m_hat = m / (1.0 - ADAM_B1 ** ADAM_STEP)
    v_hat = v / (1.0 - ADAM_B2 ** ADAM_STEP)
    delta = -ADAM_LR * (m_hat / (jnp.sqrt(v_hat) + ADAM_EPS) + ADAM_WD * w)
    return delta, m, v


def _adamw(name, chip, own, landed, w, m, v):
    _, rows, cols = own.shape
    tr = _tile(rows, 256) if rows % 256 == 0 else _tile(rows, 176)

    def body(chip_ref, own_ref, l_ref, w_ref, m_ref, v_ref, g_out, d_out, m_out, v_out):
        g = own_ref[...].astype(F32)
        for s in range(3):
            g = g + l_ref[s].astype(F32)
        delta, m_new, v_new = _adamw_math(w_ref[...], g, m_ref[...], v_ref[...])
        g_out[...] = g
        d_out[...] = delta
        m_out[...] = m_new
        v_out[...] = v_new

    spec = pl.BlockSpec((tr, cols), lambda r, c: (r, 0))
    return pl.pallas_call(
        body, name=name, out_shape=[jax.ShapeDtypeStruct((rows, cols), F32)] * 4,
        grid_spec=pltpu.PrefetchScalarGridSpec(
            num_scalar_prefetch=1, grid=(rows // tr,),
            in_specs=[pl.BlockSpec((None, tr, cols), lambda r, c: (c[0], r, 0)),
                      pl.BlockSpec((3, tr, cols), lambda r, c: (0, r, 0)), spec, spec, spec],
            out_specs=[spec] * 4),
        compiler_params=_params([((4, tr, cols), own.dtype)] + [((tr, cols), F32)] * 7, sem=("arbitrary",)),
    )(chip, own, landed, w, m, v)


def _proj_forward(x, gain, w_in_g):
    t = x.shape[0]
    tm = _tile(t, 1024)

    def body(x_ref, g_ref, w_ref, o_ref, h_ref, ht_ref):
        @pl.when(pl.program_id(1) == 0)
        def _():
            _, xh = _rms_stats(x_ref[...])
            h = (xh * g_ref[...]).astype(BF16)
            h_ref[...] = h
            ht_ref[...] = h.T

        o_ref[...] = _dot(h_ref[...], w_ref[pl.program_id(1)])

    tok = pl.BlockSpec((tm, D_MODEL), lambda m, p: (m, 0))
    return pl.pallas_call(
        body, name="proj_fwd",
        out_shape=[jax.ShapeDtypeStruct((N_DEV, t, D_MODEL), F32), jax.ShapeDtypeStruct((t, D_MODEL), BF16),
                   jax.ShapeDtypeStruct((D_MODEL, t), BF16)],
        grid=(t // tm, N_DEV),
        in_specs=[tok, pl.BlockSpec((1, D_MODEL), lambda m, p: (0, 0)), RESIDENT],
        out_specs=[pl.BlockSpec((None, tm, D_MODEL), lambda m, p: (p, m, 0)), tok,
                   pl.BlockSpec((D_MODEL, tm), lambda m, p: (0, m))],
        compiler_params=_params([((tm, D_MODEL), F32)] * 2 + [((tm, D_MODEL), BF16)] * 2,
                                scratch=[((N_DEV, D_MODEL, D_MODEL), BF16)], temps=6 << 20, sem=("arbitrary", "arbitrary")),
    )(x, gain, w_in_g)


def _masked_ws(ws_ref, g):
    row = lax.broadcasted_iota(jnp.int32, (GMLP_CHUNK, GMLP_CHUNK), 0)
    col = lax.broadcasted_iota(jnp.int32, (GMLP_CHUNK, GMLP_CHUNK), 1)
    return jnp.where(row >= col, ws_ref[g], 0.0).astype(BF16)


def _gmlp_forward(proj, ln_g, ln_b, w_s, bias_b):
    t = proj.shape[1]
    tm = _tile(t, 256)
    chunks = tm // GMLP_CHUNK

    def body(u_ref, v_ref, lng_ref, lnb_ref, ws_ref, bias_ref, a_ref, vn_scr):
        vv = _gelu(v_ref[...])
        mu = jnp.mean(vv, axis=-1, keepdims=True)
        cen = vv - mu
        var = jnp.mean(cen * cen, axis=-1, keepdims=True)
        vn_scr[...] = ((cen * lax.rsqrt(var + NORM_EPS)) * lng_ref[...] + lnb_ref[...]).astype(BF16)
        for g in range(GROUPS):
            wm = _masked_ws(ws_ref, g)
            cols = slice(g * HEAD_DIM, (g + 1) * HEAD_DIM)
            for c in range(chunks):
                rows = slice(c * GMLP_CHUNK, (c + 1) * GMLP_CHUNK)
                mixed = _dot(wm, vn_scr[rows, cols]) + bias_ref[g]
                a_ref[rows, cols] = (_gelu(u_ref[rows, cols]) * mixed).astype(BF16)

    small = pl.BlockSpec((GROUPS, GMLP_CHUNK, GMLP_CHUNK), lambda m: (0, 0, 0))
    vec = pl.BlockSpec((1, D_MODEL), lambda m: (0, 0))
    return pl.pallas_call(
        body, name="gmlp_fwd", out_shape=jax.ShapeDtypeStruct((t, D_MODEL), BF16), grid=(t // tm,),
        in_specs=[pl.BlockSpec((None, tm, D_MODEL), lambda m: (U_POS, m, 0)),
                  pl.BlockSpec((None, tm, D_MODEL), lambda m: (U_POS + 1, m, 0)), vec, vec, small, small],
        out_specs=pl.BlockSpec((tm, D_MODEL), lambda m: (m, 0)),
        scratch_shapes=[pltpu.VMEM((tm, D_MODEL), BF16)],
        compiler_params=_params([((tm, D_MODEL), F32)] * 2 + [((tm, D_MODEL), BF16)] + [((8, 128, 128), F32)] * 2,
                                scratch=[((tm, D_MODEL), BF16)], temps=8 << 20, sem=("arbitrary",)),
    )(proj, proj, ln_g, ln_b, w_s, bias_b)


def _lower_bound(tab_ref):
    t0, t1 = tab_ref[0:1, :], tab_ref[1:2, :]
    mx = jnp.maximum(t0, t1)
    e0, e1 = jnp.exp(t0 - mx), jnp.exp(t1 - mx)
    return e0 / (e0 + e1)


def _tri_masks():
    row = lax.broadcasted_iota(jnp.int32, (HGRN_CHUNK, HGRN_CHUNK), 0)
    col = lax.broadcasted_iota(jnp.int32, (HGRN_CHUNK, HGRN_CHUNK), 1)
    return row >= col, row <= col


def _chunk_rows(c):
    return slice(c * HGRN_CHUNK, (c + 1) * HGRN_CHUNK)


def _per_chunk(x, nc, fn):
    return jnp.concatenate([fn(x[_chunk_rows(c)]) for c in range(nc)], axis=0)


def _chunk_row_bcast(x, nc, i):
    return _per_chunk(x, nc, lambda xc: jnp.broadcast_to(xc[i:i + 1, :], (HGRN_CHUNK, HEAD_DIM)))


def _hgrn_gates(q, fl, lb, nc):
    lower, _ = _tri_masks()
    lower = lower.astype(BF16)
    s = _sigmoid(fl)
    f = lb + (1.0 - lb) * s
    k = 1.0 - f
    hi, mid, lo = _split3(jnp.log(f))
    a = jnp.concatenate([_dot(lower, hi[_chunk_rows(c)]) + _dot(lower, mid[_chunk_rows(c)]) + _dot(lower, lo[_chunk_rows(c)])
                         for c in range(nc)], axis=0)
    a_mid = _chunk_row_bcast(a, nc, HGRN_CHUNK // 2 - 1)
    a_last = _chunk_row_bcast(a, nc, HGRN_CHUNK - 1)
    qs = q * HGRN_SCALE
    e_in, e_out, e_end, e_all = jnp.exp(a - a_mid), jnp.exp(a_mid - a), jnp.exp(a_last - a), jnp.exp(a)
    decay = [jnp.exp(a[c * HGRN_CHUNK + HGRN_CHUNK - 1:(c + 1) * HGRN_CHUNK, :]) for c in range(nc)]
    return dict(s=s, f=f, k=k, decay=decay, e_in=e_in, e_out=e_out, e_end=e_end, e_all=e_all,
                qi=qs * e_in, ki=k * e_out, kd=k * e_end, qe=qs * e_all)


def _hgrn_forward(proj, lb_table, norm_g):
    t = proj.shape[1]
    tb = _tile(t, 1024)
    nc = tb // HGRN_CHUNK
    n_chunks = t // HGRN_CHUNK

    def body(q_ref, f_ref, i_ref, g_ref, tab_ref, ng_ref, og_ref, o_ref, st_ref, state):
        @pl.when(pl.program_id(1) == 0)
        def _():
            state[...] = jnp.zeros_like(state)

        lower, _ = _tri_masks()
        gt = _hgrn_gates(q_ref[...], f_ref[...], _lower_bound(tab_ref), nc)
        qi, ki, kd, qe = (gt[n].astype(BF16) for n in ("qi", "ki", "kd", "qe"))
        vb = i_ref[...].astype(BF16)
        o_intra, d_state = [], []
        for c in range(nc):
            rows = _chunk_rows(c)
            p = jnp.where(lower, _dot_nt(qi[rows], ki[rows]), 0.0).astype(BF16)
            o_intra.append(_dot(p, vb[rows]))
            d_state.append(_dot_tn(vb[rows], kd[rows]))
        st = state[...]
        outs = []
        for c in range(nc):
            st_ref[c] = st
            outs.append(o_intra[c] + _dot_nt(qe[_chunk_rows(c)], st.astype(BF16)))
            st = st * gt["decay"][c] + d_state[c]
        state[...] = st
        o = jnp.concatenate(outs, axis=0)
        o_ref[...] = o
        _, oh = _rms_stats(o)
        gz = g_ref[...]
        og_ref[...] = ((oh * ng_ref[...]) * (gz * _sigmoid(gz))).astype(BF16)

    def blk(p):
        return pl.BlockSpec((None, tb, HEAD_DIM), lambda h, n: (p, n, h))

    out_blk = pl.BlockSpec((tb, HEAD_DIM), lambda h, n: (n, h))
    return pl.pallas_call(
        body, name="hgrn_fwd",
        out_shape=[jax.ShapeDtypeStruct((t, D_MODEL), BF16), jax.ShapeDtypeStruct((t, D_MODEL), F32),
                   jax.ShapeDtypeStruct((HEADS, n_chunks, HEAD_DIM, HEAD_DIM), F32)],
        grid=(HEADS, t // tb),
        in_specs=[blk(Q_POS), blk(Q_POS + 1), blk(Q_POS + 2), blk(Q_POS + 3),
                  pl.BlockSpec((2, HEAD_DIM), lambda h, n: (0, h)), pl.BlockSpec((1, HEAD_DIM), lambda h, n: (0, h))],
        out_specs=[out_blk, out_blk, pl.BlockSpec((None, nc, HEAD_DIM, HEAD_DIM), lambda h, n: (h, n, 0, 0))],
        scratch_shapes=[pltpu.VMEM((HEAD_DIM, HEAD_DIM), F32)],
        compiler_params=_params([((tb, HEAD_DIM), F32)] * 6 + [((nc, HEAD_DIM, HEAD_DIM), F32)], temps=8 << 20,
                                sem=("arbitrary", "arbitrary")),
    )(proj, proj, proj, proj, lb_table, norm_g)


def _branch_out_forward(a, og, proj, x, w_a, w_b, w_out, ffn_g):
    t = x.shape[0]
    tm = _tile(t, 256)

    def body(a_ref, og_ref, ga_ref, gb_ref, x_ref, wa_ref, wb_ref, wo_ref, g_ref, ya_ref, yb_ref, mg_ref, x1_ref, h2_ref,
             h2t_ref):
        ya = _dot(a_ref[...], wa_ref[...])
        yb = _dot(og_ref[...], wb_ref[...])
        ya_ref[...] = ya
        yb_ref[...] = yb
        merged = (_sigmoid(ga_ref[...]) * ya + _sigmoid(gb_ref[...]) * yb).astype(BF16)
        mg_ref[...] = merged
        x1 = x_ref[...] + _dot(merged, wo_ref[...])
        x1_ref[...] = x1
        _, xh = _rms_stats(x1)
        h2 = (xh * g_ref[...]).astype(BF16)
        h2_ref[...] = h2
        h2t_ref[...] = h2.T

    tok = pl.BlockSpec((tm, D_MODEL), lambda m: (m, 0))
    wsp = pl.BlockSpec((D_MODEL, D_MODEL), lambda m: (0, 0))
    return pl.pallas_call(
        body, name="branch_out_fwd",
        out_shape=[jax.ShapeDtypeStruct((t, D_MODEL), F32), jax.ShapeDtypeStruct((t, D_MODEL), F32),
                   jax.ShapeDtypeStruct((t, D_MODEL), BF16), jax.ShapeDtypeStruct((t, D_MODEL), F32),
                   jax.ShapeDtypeStruct((t, D_MODEL), BF16), jax.ShapeDtypeStruct((D_MODEL, t), BF16)],
        grid=(t // tm,),
        in_specs=[tok, tok, pl.BlockSpec((None, tm, D_MODEL), lambda m: (GATE_POS, m, 0)),
                  pl.BlockSpec((None, tm, D_MODEL), lambda m: (GATE_POS + 1, m, 0)), tok, wsp, wsp, wsp,
                  pl.BlockSpec((1, D_MODEL), lambda m: (0, 0))],
        out_specs=[tok] * 5 + [pl.BlockSpec((D_MODEL, tm), lambda m: (0, m))],
        compiler_params=_params([((tm, D_MODEL), BF16)] * 5 + [((tm, D_MODEL), F32)] * 6 + [((D_MODEL, D_MODEL), BF16)] * 3,
                                temps=8 << 20, sem=("arbitrary",)),
    )(a, og, proj, proj, x, w_a, w_b, w_out, ffn_g)


def _ffn_forward(h2, x1, w_gu, w_down, target, final_g):
    t = x1.shape[0]
    tm = _tile(t, 512)

    def body(h_ref, wgu_ref, wd_ref, x1_ref, t_ref, g_ref, gu_ref, act_ref, loss_ref, dg_ref, dx_ref, dxb_ref, acc):
        m, j = pl.program_id(0), pl.program_id(1)

        @pl.when((m == 0) & (j == 0))
        def _():
            loss_ref[...] = jnp.zeros_like(loss_ref)
            dg_ref[...] = jnp.zeros_like(dg_ref)

        h = h_ref[...]
        gate = _dot(h, wgu_ref[j])
        up = _dot(h, wgu_ref[j + 4])
        gu_ref[0] = gate
        gu_ref[1] = up
        act = ((gate * _sigmoid(gate)) * up).astype(BF16)
        act_ref[...] = act
        part = _dot(act, wd_ref[j])

        @pl.when(j == 0)
        def _():
            acc[...] = part

        @pl.when((j > 0) & (j < 3))
        def _():
            acc[...] += part

        @pl.when(j == 3)
        def _():
            x2 = x1_ref[...] + (acc[...] + part)
            g = g_ref[...]
            r, xh = _rms_stats(x2)
            err = xh * g - t_ref[...]
            loss_ref[...] += 0.5 * jnp.sum(jnp.mean(err * err, axis=-1, keepdims=True), axis=0, keepdims=True)
            dy = err * (1.0 / D_MODEL)
            dg_ref[...] += jnp.sum(dy * xh, axis=0, keepdims=True)
            dxh = dy * g
            dx = r * (dxh - xh * jnp.mean(dxh * xh, axis=-1, keepdims=True))
            dx_ref[...] = dx
            dxb_ref[...] = dx.astype(BF16)

    tok = pl.BlockSpec((tm, D_MODEL), lambda m, j: (m, 0))
    vec = pl.BlockSpec((1, D_MODEL), lambda m, j: (0, 0))
    return pl.pallas_call(
        body, name="ffn_fwd",
        out_shape=[jax.ShapeDtypeStruct((4, 2, t, FF_BLOCK), F32), jax.ShapeDtypeStruct((4, t, FF_BLOCK), BF16),
                   jax.ShapeDtypeStruct((8, 128), F32), jax.ShapeDtypeStruct((1, D_MODEL), F32),
                   jax.ShapeDtypeStruct((t, D_MODEL), F32), jax.ShapeDtypeStruct((t, D_MODEL), BF16)],
        grid=(t // tm, 4),
        in_specs=[tok, RESIDENT, RESIDENT, tok, tok, vec],
        out_specs=[pl.BlockSpec((None, 2, tm, FF_BLOCK), lambda m, j: (j, 0, m, 0)),
                   pl.BlockSpec((None, tm, FF_BLOCK), lambda m, j: (j, m, 0)),
                   pl.BlockSpec((8, 128), lambda m, j: (0, 0)), vec, tok, tok],
        scratch_shapes=[pltpu.VMEM((tm, D_MODEL), F32)],
        compiler_params=_params([((tm, D_MODEL), BF16), ((tm, D_MODEL), F32), ((tm, D_MODEL), F32), ((2, tm, 768), F32),
                                 ((tm, 768), BF16), ((tm, D_MODEL), F32), ((tm, D_MODEL), BF16)],
                                scratch=[((tm, D_MODEL), F32), ((N_DEV, D_MODEL, 768), BF16), ((D_FF, D_MODEL), BF16)],
                                temps=6 << 20, sem=("arbitrary", "arbitrary")),
    )(h2, w_gu, w_down.reshape(4, FF_BLOCK, D_MODEL), x1, target, final_g)


def _ffn_backward(dx2b, dx2, gu, x1, w_gu, w_down, ffn_g):
    t = x1.shape[0]
    tm = _tile(t, 512)

    def body(dxb_ref, dx2_ref, gu_ref, x1_ref, wgu_ref, wd_ref, g_ref, dgu_ref, dx1_ref, dx1b_ref, dg_ref, acc, prev):
        m, j = pl.program_id(0), pl.program_id(1)

        @pl.when((m == 0) & (j == 0))
        def _():
            dg_ref[...] = jnp.zeros_like(dg_ref)

        @pl.when(j == 0)
        def _():
            prev[...] = jnp.zeros_like(prev)
            acc[...] = jnp.zeros_like(acc)

        jm1 = jnp.maximum(j - 1, 0)
        acc[...] += _dot_nt(prev[0], wgu_ref[jm1]) + _dot_nt(prev[1], wgu_ref[jm1 + 4])
        dact = _dot_nt(dxb_ref[...], wd_ref[j])
        gate, up = gu_ref[0], gu_ref[1]
        sg = _sigmoid(gate)
        dgate = (dact * up * (sg * (1.0 + gate * (1.0 - sg)))).astype(BF16)
        dup = (dact * (gate * sg)).astype(BF16)
        dgu_ref[0] = dgate
        dgu_ref[1] = dup
        prev[0] = dgate
        prev[1] = dup

        @pl.when(j == 3)
        def _():
            dh2 = acc[...] + (_dot_nt(prev[0], wgu_ref[3]) + _dot_nt(prev[1], wgu_ref[7]))
            dx, dg = _rms_bwd(dh2, x1_ref[...], g_ref[...])
            dx1 = dx2_ref[...] + dx
            dx1_ref[...] = dx1
            dx1b_ref[...] = dx1.astype(BF16)
            dg_ref[...] += dg

    tok = pl.BlockSpec((tm, D_MODEL), lambda m, j: (m, 0))
    vec = pl.BlockSpec((1, D_MODEL), lambda m, j: (0, 0))
    gu_spec = pl.BlockSpec((None, 2, tm, FF_BLOCK), lambda m, j: (j, 0, m, 0))
    return pl.pallas_call(
        body, name="ffn_bwd",
        out_shape=[jax.ShapeDtypeStruct((4, 2, t, FF_BLOCK), BF16), jax.ShapeDtypeStruct((t, D_MODEL), F32),
                   jax.ShapeDtypeStruct((t, D_MODEL), BF16), jax.ShapeDtypeStruct((1, D_MODEL), F32)],
        grid=(t // tm, 4),
        in_specs=[tok, tok, gu_spec, tok, RESIDENT, RESIDENT, vec],
        out_specs=[gu_spec, tok, tok, vec],
        scratch_shapes=[pltpu.VMEM((tm, D_MODEL), F32), pltpu.VMEM((2, tm, FF_BLOCK), BF16)],
        compiler_params=_params([((tm, D_MODEL), BF16), ((tm, D_MODEL), F32), ((2, tm, 768), F32), ((tm, D_MODEL), F32),
                                 ((2, tm, 768), BF16), ((tm, D_MODEL), F32), ((tm, D_MODEL), BF16)],
                                scratch=[((tm, D_MODEL), F32), ((2, tm, 768), BF16), ((N_DEV, D_MODEL, 768), BF16),
                                         ((D_FF, D_MODEL), BF16)],
                                temps=4 << 20, sem=("arbitrary", "arbitrary")),
    )(dx2b, dx2, gu, x1, w_gu, w_down.reshape(4, FF_BLOCK, D_MODEL), ffn_g)


def _branch_out_backward(dx1b, ya, yb, proj, w_a, w_b, w_out):
    t = ya.shape[0]
    tm = _tile(t, 256)

    def body(dx_ref, ya_ref, yb_ref, ga_ref, gb_ref, wa_ref, wb_ref, wo_ref, dya_ref, dyb_ref, dgate_ref, da_ref, dog_ref):
        dm = _dot_nt(dx_ref[...], wo_ref[...])
        sa, sb = _sigmoid(ga_ref[...]), _sigmoid(gb_ref[...])
        dya = (dm * sa).astype(BF16)
        dyb = (dm * sb).astype(BF16)
        dya_ref[...] = dya
        dyb_ref[...] = dyb
        dgate_ref[0] = (dm * ya_ref[...] * (sa * (1.0 - sa))).astype(BF16)
        dgate_ref[1] = (dm * yb_ref[...] * (sb * (1.0 - sb))).astype(BF16)
        da_ref[...] = _dot_nt(dya, wa_ref[...])
        dog_ref[...] = _dot_nt(dyb, wb_ref[...])

    tok = pl.BlockSpec((tm, D_MODEL), lambda m: (m, 0))
    wsp = pl.BlockSpec((D_MODEL, D_MODEL), lambda m: (0, 0))
    return pl.pallas_call(
        body, name="branch_out_bwd",
        out_shape=[jax.ShapeDtypeStruct((t, D_MODEL), BF16), jax.ShapeDtypeStruct((t, D_MODEL), BF16),
                   jax.ShapeDtypeStruct((N_DEV, t, D_MODEL), BF16), jax.ShapeDtypeStruct((t, D_MODEL), F32),
                   jax.ShapeDtypeStruct((t, D_MODEL), F32)],
        grid=(t // tm,),
        in_specs=[tok, tok, tok, pl.BlockSpec((None, tm, D_MODEL), lambda m: (GATE_POS, m, 0)),
                  pl.BlockSpec((None, tm, D_MODEL), lambda m: (GATE_POS + 1, m, 0)), wsp, wsp, wsp],
        out_specs=[tok, tok, pl.BlockSpec((2, tm, D_MODEL), lambda m: (GATE_POS // 2, m, 0)), tok, tok],
        compiler_params=_params([((tm, D_MODEL), BF16)] * 5 + [((tm, D_MODEL), F32)] * 6 + [((D_MODEL, D_MODEL), BF16)] * 3,
                                temps=8 << 20, sem=("arbitrary",)),
    )(dx1b, ya, yb, proj, proj, w_a, w_b, w_out)


def _hgrn_backward(dproj, dog, o_saved, states, proj, lb_table, norm_g):
    t = proj.shape[1]
    tb = _tile(t, 1024)
    nc = tb // HGRN_CHUNK
    nb = t // tb

    def body(_, dog_ref, o_ref, st_ref, q_ref, f_ref, i_ref, g_ref, tab_ref, ng_ref, dp_ref, dng_ref, dtab_ref, gstate):
        @pl.when(pl.program_id(1) == 0)
        def _():
            gstate[...] = jnp.zeros_like(gstate)
            dng_ref[...] = jnp.zeros_like(dng_ref)
            dtab_ref[...] = jnp.zeros_like(dtab_ref)

        lb = _lower_bound(tab_ref)
        ng = ng_ref[...]
        lower, upper = _tri_masks()
        gt = _hgrn_gates(q_ref[...], f_ref[...], lb, nc)
        qi, ki, kd, qe = (gt[n].astype(BF16) for n in ("qi", "ki", "kd", "qe"))
        vb = i_ref[...].astype(BF16)
        o, gz, d_og = o_ref[...], g_ref[...], dog_ref[...]
        r, oh = _rms_stats(o)
        sg = _sigmoid(gz)
        d_on = d_og * (gz * sg)
        dgz = d_og * (oh * ng) * (sg * (1.0 + gz * (1.0 - sg)))
        dng_ref[...] += jnp.sum(d_on * oh, axis=0, keepdims=True)
        doh = d_on * ng
        dob = (r * (doh - oh * jnp.mean(doh * oh, axis=-1, keepdims=True))).astype(BF16)
        dv_intra, dqi, dki, dqe, g_upd = [], [], [], [], []
        for c in range(nc):
            rows = _chunk_rows(c)
            p = jnp.where(lower, _dot_nt(qi[rows], ki[rows]), 0.0).astype(BF16)
            dv_intra.append(_dot_tn(p, dob[rows]))
            dp = jnp.where(lower, _dot_nt(dob[rows], vb[rows]), 0.0).astype(BF16)
            dqi.append(_dot(dp, ki[rows]))
            dki.append(_dot_tn(dp, qi[rows]))
            dqe.append(_dot(dob[rows], st_ref[c].astype(BF16)))
            g_upd.append(_dot_tn(dob[rows], qe[rows]))
        g_after = [None] * nc
        g = gstate[...]
        for c in reversed(range(nc)):
            g_after[c] = g
            g = g * gt["decay"][c] + g_upd[c]
        gstate[...] = g
        dkd, dv, da_last = [], [], []
        for c in range(nc):
            rows = _chunk_rows(c)
            gb = g_after[c].astype(BF16)
            dkd.append(_dot(vb[rows], gb))
            dv.append(dv_intra[c] + _dot_nt(kd[rows], gb))
            da_last.append(jnp.sum(g_after[c] * st_ref[c], axis=0, keepdims=True) * gt["decay"][c])
        dqi, dki, dqe, dkd, dv = (jnp.concatenate(z, axis=0) for z in (dqi, dki, dqe, dkd, dv))
        dqs = dqi * gt["e_in"] + dqe * gt["e_all"]
        dk = dki * gt["e_out"] + dkd * gt["e_end"]
        t_in, t_out, t_end = dqi * gt["qi"], dki * gt["ki"], dkd * gt["kd"]
        da = t_in - t_out + dqe * gt["qe"] - t_end
        row = lax.broadcasted_iota(jnp.int32, (HGRN_CHUNK, HEAD_DIM), 0)
        d_mid = t_out - t_in
        pieces = []
        for c in range(nc):
            rows = _chunk_rows(c)
            da_mid = jnp.sum(d_mid[rows], axis=0, keepdims=True)
            da_end = jnp.sum(t_end[rows], axis=0, keepdims=True) + da_last[c]
            da_c = da[rows] + jnp.where(row == HGRN_CHUNK // 2 - 1, da_mid, 0.0) + jnp.where(row == HGRN_CHUNK - 1, da_end, 0.0)
            pieces.append(_mask_mm(upper.astype(BF16), da_c))
        df = jnp.concatenate(pieces, axis=0) / gt["f"] - dk
        s = gt["s"]
        dlb = jnp.sum(df * (1.0 - s), axis=0, keepdims=True)
        dp_ref[0] = (dqs * HGRN_SCALE).astype(BF16)
        dp_ref[1] = (df * (1.0 - lb) * (s * (1.0 - s))).astype(BF16)
        dp_ref[2] = dv.astype(BF16)
        dp_ref[3] = dgz.astype(BF16)
        dt0 = dlb * (lb * (1.0 - lb))
        dtab_ref[0:1, :] += dt0
        dtab_ref[1:2, :] -= dt0

    def blk(p):
        return pl.BlockSpec((None, tb, HEAD_DIM), lambda h, n: (p, nb - 1 - n, h))

    tok = pl.BlockSpec((tb, HEAD_DIM), lambda h, n: (nb - 1 - n, h))
    return pl.pallas_call(
        body, name="hgrn_bwd",
        out_shape=[jax.ShapeDtypeStruct((N_DEV, t, D_MODEL), BF16), jax.ShapeDtypeStruct((1, D_MODEL), F32),
                   jax.ShapeDtypeStruct((2, D_MODEL), F32)],
        grid=(HEADS, nb),
        in_specs=[ANY, tok, tok, pl.BlockSpec((None, nc, HEAD_DIM, HEAD_DIM), lambda h, n: (h, nb - 1 - n, 0, 0)),
                  blk(Q_POS), blk(Q_POS + 1), blk(Q_POS + 2), blk(Q_POS + 3),
                  pl.BlockSpec((2, HEAD_DIM), lambda h, n: (0, h)), pl.BlockSpec((1, HEAD_DIM), lambda h, n: (0, h))],
        out_specs=[pl.BlockSpec((4, tb, HEAD_DIM), lambda h, n: (0, nb - 1 - n, h)),
                   pl.BlockSpec((1, HEAD_DIM), lambda h, n: (0, h)), pl.BlockSpec((2, HEAD_DIM), lambda h, n: (0, h))],
        scratch_shapes=[pltpu.VMEM((HEAD_DIM, HEAD_DIM), F32)],
        input_output_aliases={0: 0},
        compiler_params=_params([((tb, HEAD_DIM), F32)] * 6 + [((nc, HEAD_DIM, HEAD_DIM), F32)] + [((4, tb, HEAD_DIM), BF16)],
                                temps=8 << 20, sem=("arbitrary", "arbitrary")),
    )(dproj, dog, o_saved, states, proj, proj, proj, proj, lb_table, norm_g)


def _gmlp_backward(dproj, da, proj, ln_g, ln_b, w_s, bias_b):
    t = proj.shape[1]
    tm = _tile(t, 256)
    chunks = tm // GMLP_CHUNK

    def body(_, da_ref, u_ref, v_ref, lng_ref, lnb_ref, ws_ref, bias_ref, dp_ref, dlng_ref, dlnb_ref, dws_ref, dbs_ref,
             vn_scr, dvn_scr):
        @pl.when(pl.program_id(0) == 0)
        def _():
            dlng_ref[...] = jnp.zeros_like(dlng_ref)
            dlnb_ref[...] = jnp.zeros_like(dlnb_ref)
            dws_ref[...] = jnp.zeros_like(dws_ref)
            dbs_ref[...] = jnp.zeros_like(dbs_ref)

        v = v_ref[...]
        vv = _gelu(v)
        mu = jnp.mean(vv, axis=-1, keepdims=True)
        cen = vv - mu
        rstd = lax.rsqrt(jnp.mean(cen * cen, axis=-1, keepdims=True) + NORM_EPS)
        vhat = cen * rstd
        lng = lng_ref[...]
        vn_scr[...] = (vhat * lng + lnb_ref[...]).astype(BF16)
        row = lax.broadcasted_iota(jnp.int32, (GMLP_CHUNK, GMLP_CHUNK), 0)
        col = lax.broadcasted_iota(jnp.int32, (GMLP_CHUNK, GMLP_CHUNK), 1)
        for g in range(GROUPS):
            wm = _masked_ws(ws_ref, g)
            cols = slice(g * HEAD_DIM, (g + 1) * HEAD_DIM)
            dws = jnp.zeros((GMLP_CHUNK, GMLP_CHUNK), F32)
            dbs = jnp.zeros((GMLP_CHUNK, GMLP_CHUNK), F32)
            for c in range(chunks):
                rows = slice(c * GMLP_CHUNK, (c + 1) * GMLP_CHUNK)
                vn = vn_scr[rows, cols]
                mixed = _dot(wm, vn) + bias_ref[g]
                u = u_ref[rows, cols]
                d_a = da_ref[rows, cols]
                dp_ref[0, rows, cols] = (d_a * mixed * _gelu_grad(u)).astype(BF16)
                dmix = d_a * _gelu(u)
                dmb = dmix.astype(BF16)
                dbs = dbs + dmix
                dws = dws + _dot_nt(dmb, vn)
                dvn_scr[rows, cols] = _dot_tn(wm, dmb)
            dws_ref[g] += jnp.where(row >= col, dws, 0.0)
            dbs_ref[g] += jnp.broadcast_to(jnp.sum(dbs, axis=-1, keepdims=True), (GMLP_CHUNK, GMLP_CHUNK))
        dvn = dvn_scr[...]
        dlng_ref[...] += jnp.sum(dvn * vhat, axis=0, keepdims=True)
        dlnb_ref[...] += jnp.sum(dvn, axis=0, keepdims=True)
        dvh = dvn * lng
        dvv = rstd * (dvh - jnp.mean(dvh, axis=-1, keepdims=True) - vhat * jnp.mean(dvh * vhat, axis=-1, keepdims=True))
        dp_ref[1] = (dvv * _gelu_grad(v)).astype(BF16)

    tok = pl.BlockSpec((tm, D_MODEL), lambda m: (m, 0))
    small = pl.BlockSpec((GROUPS, GMLP_CHUNK, GMLP_CHUNK), lambda m: (0, 0, 0))
    vec = pl.BlockSpec((1, D_MODEL), lambda m: (0, 0))
    return pl.pallas_call(
        body, name="gmlp_bwd",
        out_shape=[jax.ShapeDtypeStruct(dproj.shape, BF16), jax.ShapeDtypeStruct((1, D_MODEL), F32),
                   jax.ShapeDtypeStruct((1, D_MODEL), F32), jax.ShapeDtypeStruct((GROUPS, GMLP_CHUNK, GMLP_CHUNK), F32),
                   jax.ShapeDtypeStruct((GROUPS, GMLP_CHUNK, GMLP_CHUNK), F32)],
        grid=(t // tm,),
        in_specs=[ANY, tok, pl.BlockSpec((None, tm, D_MODEL), lambda m: (U_POS, m, 0)),
                  pl.BlockSpec((None, tm, D_MODEL), lambda m: (U_POS + 1, m, 0)), vec, vec, small, small],
        out_specs=[pl.BlockSpec((2, tm, D_MODEL), lambda m: (U_POS // 2, m, 0)), vec, vec, small, small],
        scratch_shapes=[pltpu.VMEM((tm, D_MODEL), BF16), pltpu.VMEM((tm, D_MODEL), F32)],
        input_output_aliases={0: 0},
        compiler_params=_params([((tm, D_MODEL), F32)] * 3 + [((2, tm, D_MODEL), BF16)] + [((8, 128, 128), F32)] * 4,
                                scratch=[((tm, D_MODEL), BF16), ((tm, D_MODEL), F32)], temps=12 << 20, sem=("arbitrary",)),
    )(dproj, da, proj, proj, ln_g, ln_b, w_s, bias_b)


def _input_backward(dproj, w_in_g, x, dx1, mix_g):
    t = x.shape[0]
    tm = _tile(t, 512)

    def body(dp_ref, w_ref, x_ref, dx1_ref, g_ref, dx_ref, dg_ref):
        @pl.when(pl.program_id(0) == 0)
        def _():
            dg_ref[...] = jnp.zeros_like(dg_ref)

        dh = _dot_nt(dp_ref[0], w_ref[0])
        for p in range(1, N_DEV):
            dh = dh + _dot_nt(dp_ref[p], w_ref[p])
        dx, dg = _rms_bwd(dh, x_ref[...], g_ref[...])
        dx_ref[...] = dx1_ref[...] + dx
        dg_ref[...] += dg

    tok = pl.BlockSpec((tm, D_MODEL), lambda m: (m, 0))
    vec = pl.BlockSpec((1, D_MODEL), lambda m: (0, 0))
    return pl.pallas_call(
        body, name="input_bwd",
        out_shape=[jax.ShapeDtypeStruct((t, D_MODEL), F32), jax.ShapeDtypeStruct((1, D_MODEL), F32)],
        grid=(t // tm,),
        in_specs=[pl.BlockSpec((N_DEV, tm, D_MODEL), lambda m: (0, m, 0)), RESIDENT, tok, tok, vec],
        out_specs=[tok, vec],
        compiler_params=_params([((N_DEV, tm, D_MODEL), BF16)] + [((tm, D_MODEL), F32)] * 3,
                                scratch=[((N_DEV, D_MODEL, D_MODEL), BF16)], temps=6 << 20, sem=("arbitrary",)),
    )(dproj, w_in_g, x, dx1, mix_g)


def _weight_grad(name, a, b, a_spec, b_spec, out_shape, out_spec, steps, blocks, a_is_transposed):
    def body(a_ref, b_ref, o_ref):
        o_ref[...] = _dot(a_ref[...], b_ref[...]) if a_is_transposed else _dot_tn(a_ref[...], b_ref[...])

    return pl.pallas_call(
        body, name=name, out_shape=jax.ShapeDtypeStruct(out_shape, F32), grid=(steps,), in_specs=[a_spec, b_spec],
        out_specs=out_spec, compiler_params=_params(blocks, temps=4 << 20, sem=("arbitrary",)),
    )(a, b)


def _pack_small(mix_g, ln_g, ln_b, b_s, lb_table, hg_norm, ffn_g, final_g):
    def part(a):
        a = a.reshape(-1, D_MODEL)
        return jnp.pad(a, ((0, 8 - a.shape[0]), (0, 0)))

    return jnp.concatenate([part(mix_g), part(ln_g), part(ln_b), part(hg_norm), part(ffn_g), part(final_g),
                            part(lb_table), part(b_s)], axis=0)


def _unpack_small(pack, w_s):
    return dict(norm_mix_g=pack[0:1], gmlp_ln_g=pack[8:9], gmlp_ln_b=pack[16:17], hgrn_norm_g=pack[24:25],
                norm_ffn_g=pack[32:33], norm_final_g=pack[40], hgrn_lb_table=pack[48:50],
                gmlp_b_s=pack[56:57].reshape(1, GROUPS, GMLP_CHUNK),
                gmlp_w_s=w_s.reshape(1, GROUPS, GMLP_CHUNK, GMLP_CHUNK))


def _adamw_small(name, gathered, w, m, v):
    rows, cols = w.shape

    def body(p_ref, w_ref, m_ref, v_ref, g_out, d_out, m_out, v_out):
        g = p_ref[0]
        for j in range(1, N_DEV):
            g = g + p_ref[j]
        delta, m_new, v_new = _adamw_math(w_ref[...], g, m_ref[...], v_ref[...])
        g_out[...] = g
        d_out[...] = delta
        m_out[...] = m_new
        v_out[...] = v_new

    tr = _tile(rows, 512)
    spec = pl.BlockSpec((tr, cols), lambda r: (r, 0))
    return pl.pallas_call(
        body, name=name, out_shape=[jax.ShapeDtypeStruct((rows, cols), F32)] * 4, grid=(rows // tr,),
        in_specs=[pl.BlockSpec((N_DEV, tr, cols), lambda r: (0, r, 0)), spec, spec, spec], out_specs=[spec] * 4,
        compiler_params=_params([((N_DEV, tr, cols), F32)] + [((tr, cols), F32)] * 7, sem=("arbitrary",)),
    )(gathered, w, m, v)


def kernel(x, norm_mix_g, w_in, gmlp_ln_g, gmlp_ln_b, gmlp_w_s, gmlp_b_s, hgrn_lb_table, hgrn_norm_g, w_branch_a, w_branch_b, w_out, norm_ffn_g, w_gate_up, w_down, norm_final_g, loss_target, m_norm_mix_g, m_w_in, m_gmlp_ln_g, m_gmlp_ln_b, m_gmlp_w_s, m_gmlp_b_s, m_hgrn_lb_table, m_hgrn_norm_g, m_w_branch_a, m_w_branch_b, m_w_out, m_norm_ffn_g, m_w_gate_up, m_w_down, m_norm_final_g, v_norm_mix_g, v_w_in, v_gmlp_ln_g, v_gmlp_ln_b, v_gmlp_w_s, v_gmlp_b_s, v_hgrn_lb_table, v_hgrn_norm_g, v_w_branch_a, v_w_branch_b, v_w_out, v_norm_ffn_g, v_w_gate_up, v_w_down, v_norm_final_g):
    t = x.shape[1]
    x2d = x.reshape(t, D_MODEL)
    target = loss_target.reshape(t, D_MODEL)
    final_g = norm_final_g.reshape(1, D_MODEL)

    shards = [w_in[0].astype(BF16), w_branch_a[0].astype(BF16), w_branch_b[0].astype(BF16), w_out[0].astype(BF16),
              w_gate_up[0].astype(BF16), w_down[0].astype(BF16)]

    def rows_of(n):
        return lambda ref, j: ref.at[pl.ds(pl.multiple_of(j * n, 8), n)]

    gathered = [((N_DEV, D_MODEL, D_MODEL), BF16), ((D_MODEL, D_MODEL), BF16), ((D_MODEL, D_MODEL), BF16),
                ((D_MODEL, D_MODEL), BF16), ((N_DEV, D_MODEL, FF_BLOCK), BF16), ((D_FF, D_MODEL), BF16)]
    places = [lambda ref, j: ref.at[_pos_of_dev(j)], rows_of(BRANCH_ROWS), rows_of(BRANCH_ROWS), rows_of(BRANCH_ROWS),
              lambda ref, j: ref.at[j], rows_of(DOWN_ROWS)]
    (w_in_g,) = _all_gather("w_in_all_gather", shards[:1], gathered[:1], places[:1])
    _, later = lax.optimization_barrier((w_in_g, shards[1:]))
    w_a, w_b, w_o, w_gu, w_dn = _all_gather_async("weights_all_gather", 0, later, gathered[1:], places[1:])

    proj, h, h_t = _proj_forward(x2d, norm_mix_g, w_in_g)
    bias_b = jnp.broadcast_to(gmlp_b_s[0][:, :, None], (GROUPS, GMLP_CHUNK, GMLP_CHUNK))
    a = _gmlp_forward(proj, gmlp_ln_g, gmlp_ln_b, gmlp_w_s[0], bias_b)
    og, o_saved, states = _hgrn_forward(proj, hgrn_lb_table, hgrn_norm_g)
    ya, yb, merged, x1, h2, h2_t = _branch_out_forward(a, og, proj, x2d, w_a, w_b, w_o, norm_ffn_g)
    gu, act, loss_tile, d_final_g, dx2, dx2b = _ffn_forward(h2, x1, w_gu, w_dn, target, final_g)

    core = lax.axis_index("c").astype(jnp.int32).reshape(1)
    chip = (2 * lax.axis_index("x") + lax.axis_index("y")).astype(jnp.int32).reshape(1)
    branch_rows, branch_shape = rows_of(BRANCH_ROWS), (BRANCH_ROWS, D_MODEL)
    branch_block = ((BRANCH_ROWS, D_MODEL), lambda q, r, c: (2 * q + c, 0))

    def chip_partials(names, grads, land, own_blocks):
        return [_chip_partial("chip_partial_" + nme, core, g_, blk, idx, l_)
                for nme, g_, (blk, idx), l_ in zip(names, grads, own_blocks, land)]

    whole = pl.BlockSpec((t, D_MODEL), lambda n: (0, 0))
    whole_t = pl.BlockSpec((D_MODEL, t), lambda n: (0, 0))
    col_blocks = [((t, D_MODEL), BF16), ((t, 256), BF16), ((D_MODEL, 256), F32)]

    def square_grad(name, a_, b_):
        return _weight_grad(name, a_, b_, whole, pl.BlockSpec((t, 256), lambda n: (0, n)), (D_MODEL, D_MODEL),
                            pl.BlockSpec((D_MODEL, 256), lambda n: (0, n)), D_MODEL // 256, col_blocks, False)

    dgu, dx1, dx1b, d_ffn_g = _ffn_backward(dx2b, dx2, gu, x1, w_gu, w_dn, norm_ffn_g)
    g_gu = _weight_grad(
        "grad_w_gate_up", h2_t, dgu, whole_t, pl.BlockSpec((None, None, t, FF_BLOCK), lambda j: (j % 4, j // 4, 0, 0)),
        (N_DEV, D_MODEL, FF_BLOCK), pl.BlockSpec((None, D_MODEL, FF_BLOCK), lambda j: (j, 0, 0)), N_DEV,
        [((D_MODEL, t), BF16), ((t, 768), BF16), ((D_MODEL, 768), F32)], True)
    g_dn = _weight_grad(
        "grad_w_down", act, dx2b, pl.BlockSpec((None, t, FF_BLOCK), lambda j: (j, 0, 0)), whole, (D_FF, D_MODEL),
        pl.BlockSpec((FF_BLOCK, D_MODEL), lambda j: (j, 0)), 4,
        [((t, 768), BF16), ((t, D_MODEL), BF16), ((FF_BLOCK, D_MODEL), F32)], False)
    names_f, grads_f = ["w_gate_up", "w_down"], [g_gu, g_dn]
    land_f = _exchange_sibling("ffn_grads_to_sibling", 2, grads_f, [lambda ref, j: ref.at[j], rows_of(DOWN_ROWS)],
                               [(D_MODEL, FF_BLOCK), (DOWN_ROWS, D_MODEL)])

    dya, dyb, dproj, da, dog = _branch_out_backward(dx1b, ya, yb, proj, w_a, w_b, w_o)
    g_a = square_grad("grad_w_a", a, dya)
    g_b = square_grad("grad_w_b", og, dyb)
    g_o = square_grad("grad_w_out", merged, dx1b)
    names_b, grads_b = ["w_branch_a", "w_branch_b", "w_out"], [g_a, g_b, g_o]
    land_b = _exchange_sibling("branch_grads_to_sibling", 3, grads_b, [branch_rows] * 3, [branch_shape] * 3)

    part_f = chip_partials(names_f, grads_f, land_f,
                           [((None, 256, FF_BLOCK), lambda q, r, c: (2 * q + c, r, 0)),
                            ((DOWN_ROWS // 2, D_MODEL), lambda q, r, c: (2 * (2 * q + c) + r, 0))])
    landed_f = _exchange_chips("ffn_grads_to_chips", 5, part_f)

    dog, _ = lax.optimization_barrier((dog, part_f))
    dproj, d_hg_norm, d_lb = _hgrn_backward(dproj, dog, o_saved, states, proj, hgrn_lb_table, hgrn_norm_g)

    part_b = chip_partials(names_b, grads_b, land_b, [branch_block] * 3)
    landed_b = _exchange_chips("branch_grads_to_chips", 6, part_b)

    da, _ = lax.optimization_barrier((da, part_b))
    dproj, d_ln_g, d_ln_b, d_ws, d_bs = _gmlp_backward(dproj, da, proj, gmlp_ln_g, gmlp_ln_b, gmlp_w_s[0], bias_b)

    def packed(vals):
        return _pack_small(*vals)

    def flat_ws(a):
        return a.reshape(GROUPS * GMLP_CHUNK, GMLP_CHUNK)

    w_pack = packed([norm_mix_g, gmlp_ln_g, gmlp_ln_b, gmlp_b_s, hgrn_lb_table, hgrn_norm_g, norm_ffn_g, norm_final_g])
    m_pack = packed([m_norm_mix_g, m_gmlp_ln_g, m_gmlp_ln_b, m_gmlp_b_s, m_hgrn_lb_table, m_hgrn_norm_g, m_norm_ffn_g, m_norm_final_g])
    v_pack = packed([v_norm_mix_g, v_gmlp_ln_g, v_gmlp_ln_b, v_gmlp_b_s, v_hgrn_lb_table, v_hgrn_norm_g, v_norm_ffn_g, v_norm_final_g])
    small_partial = _pack_small(jnp.zeros((1, D_MODEL), F32), d_ln_g, d_ln_b, d_bs[:, :, 0], d_lb, d_hg_norm, d_ffn_g, d_final_g)
    small_all, ws_all = _all_gather_async(
        "small_grads_all_gather", 1, [small_partial, flat_ws(d_ws)],
        [((N_DEV, SMALL_ROWS, D_MODEL), F32), ((N_DEV, GROUPS * GMLP_CHUNK, GMLP_CHUNK), F32)],
        [lambda ref, j: ref.at[j], lambda ref, j: ref.at[j]])

    g_in = _weight_grad(
        "grad_w_in", h_t, dproj, whole_t, pl.BlockSpec((None, t, D_MODEL), lambda p: (p, 0, 0)), (N_DEV, D_MODEL, D_MODEL),
        pl.BlockSpec((None, D_MODEL, D_MODEL), lambda p: (p, 0, 0)), N_DEV,
        [((D_MODEL, t), BF16), ((t, D_MODEL), BF16), ((D_MODEL, D_MODEL), F32)], True)
    land_i = _exchange_sibling("w_in_grads_to_sibling", 4, [g_in], [lambda ref, j: ref.at[_pos_of_dev(j)]],
                               [(D_MODEL, D_MODEL)])

    big = {}
    for nme, own, lnd, w, m, v in zip(
            names_f + names_b, part_f + part_b, landed_f + landed_b,
            [w_gate_up, w_down, w_branch_a, w_branch_b, w_out], [m_w_gate_up, m_w_down, m_w_branch_a, m_w_branch_b, m_w_out],
            [v_w_gate_up, v_w_down, v_w_branch_a, v_w_branch_b, v_w_out]):
        big[nme] = [o_[None] for o_ in _adamw("adamw_" + nme, chip, own, lnd, w[0], m[0], v[0])]
    small_outs = _adamw_small("adamw_small", small_all, w_pack, m_pack, v_pack)
    ws_outs = _adamw_small("adamw_w_s", ws_all, flat_ws(gmlp_w_s), flat_ws(m_gmlp_w_s), flat_ws(v_gmlp_w_s))
    land_i, _ = lax.optimization_barrier((land_i, (big, small_outs, ws_outs)))
    part_i = chip_partials(["w_in"], [g_in], land_i,
                           [((None, 256, D_MODEL), lambda q, r, c: (_pos_of_dev(2 * q + c), r, 0))])
    landed_i = _exchange_chips("w_in_grads_to_chips", 7, part_i)

    dx1, _ = lax.optimization_barrier((dx1, part_i))
    grad_x, d_mix_g = _input_backward(dproj, w_in_g, x2d, dx1, norm_mix_g)
    big["w_in"] = [o_[None] for o_ in _adamw("adamw_w_in", chip, part_i[0], landed_i[0], w_in[0], m_w_in[0], v_w_in[0])]

    def row8(a):
        return jnp.pad(a, ((0, 7), (0, 0)))

    d_mix_g, _ = lax.optimization_barrier((d_mix_g, landed_i))
    (mix_all,) = _all_gather_async("mix_gain_grad_all_gather", 8, [row8(d_mix_g)], [((N_DEV, 8, D_MODEL), F32)],
                                   [lambda ref, j: ref.at[j]])
    mix_outs = _adamw_small("adamw_mix_gain", mix_all, row8(norm_mix_g), row8(m_norm_mix_g), row8(v_norm_mix_g))
    small = [dict(_unpack_small(p, ws), norm_mix_g=q[0:1]) for p, ws, q in zip(small_outs, ws_outs, mix_outs)]

    loss = lax.psum(loss_tile[0, 0], ("x", "y", "c"))
    order = ["norm_mix_g", "w_in", "gmlp_ln_g", "gmlp_ln_b", "gmlp_w_s", "gmlp_b_s", "hgrn_lb_table", "hgrn_norm_g",
             "w_branch_a", "w_branch_b", "w_out", "norm_ffn_g", "w_gate_up", "w_down", "norm_final_g"]
    outs = [loss, grad_x.reshape(1, t, D_MODEL)]
    for kind in range(4):
        for nme in order:
            outs.append(big[nme][kind] if nme in big else small[kind][nme])
    return tuple(outs)
```

```python
import functools

import jax
import jax.numpy as jnp
from jax import lax
from jax.experimental import pallas as pl
from jax.experimental.pallas import tpu as pltpu
from jax.experimental.pallas import tpu_sc as plsc

F32, BF16 = jnp.float32, jnp.bfloat16
D_MODEL = 1024
N_DEV = 8
HEADS = 8
HEAD_DIM = 128
GROUPS = 8
GMLP_CHUNK = 128
HGRN_CHUNK = 64
HGRN_SCALE = HEAD_DIM ** -0.5
D_FF = 2816
FF_BLOCK = D_FF // 4
DOWN_ROWS = D_FF // N_DEV
BRANCH_ROWS = D_MODEL // N_DEV
NORM_EPS = 1e-6
ADAM_LR, ADAM_B1, ADAM_B2, ADAM_EPS, ADAM_WD, ADAM_STEP = 0.001, 0.9, 0.999, 1e-08, 0.01, 10
SMALL_ROWS = 72
V7X_VMEM_BYTES = 64 * 1024 * 1024
VMEM_CAP = V7X_VMEM_BYTES - 6 * 1024 * 1024
MESH_ID = pl.DeviceIdType.MESH
ANY = pl.BlockSpec(memory_space=pl.ANY)
RESIDENT = pl.BlockSpec(memory_space=pltpu.VMEM)
Q_POS, U_POS, GATE_POS = 0, 4, 6


def _pos_of_dev(j):
    return jnp.where(j < 2, j + 4, jnp.where(j < 6, j - 2, j))


def _dev_of_pos(p):
    return jnp.where(p < 4, p + 2, jnp.where(p < 6, p - 4, p))


def _nbytes(shape, dtype):
    n = 1
    for s in shape:
        n *= s
    return n * jnp.dtype(dtype).itemsize


def _params(blocks, scratch=(), temps=0, sem=None):
    need = 2 * sum(_nbytes(s, d) for s, d in blocks) + sum(_nbytes(s, d) for s, d in scratch) + temps
    assert need + (4 << 20) <= VMEM_CAP, need
    return pltpu.CompilerParams(dimension_semantics=sem, vmem_limit_bytes=VMEM_CAP)


def _tile(n, pref):
    return pref if n % pref == 0 else n


def _dot(a, b):
    return jnp.dot(a, b, preferred_element_type=F32)


def _dot_nt(a, b):
    return lax.dot_general(a, b, (((1,), (1,)), ((), ())), preferred_element_type=F32)


def _dot_tn(a, b):
    return lax.dot_general(a, b, (((0,), (0,)), ((), ())), preferred_element_type=F32)


def _sigmoid(x):
    return 1.0 / (1.0 + jnp.exp(-x))


_GELU_C = 0.7978845608028654


def _gelu(x):
    return x * (0.5 * (1.0 + jnp.tanh(_GELU_C * (x + 0.044715 * (x * x * x)))))


def _gelu_grad(x):
    t = jnp.tanh(_GELU_C * (x + 0.044715 * (x * x * x)))
    return 0.5 * (1.0 + t) + 0.5 * x * (1.0 - t * t) * (_GELU_C * (1.0 + 3.0 * 0.044715 * x * x))


def _rms_stats(x):
    r = lax.rsqrt(jnp.mean(x * x, axis=-1, keepdims=True) + NORM_EPS)
    return r, x * r


def _rms_bwd(dy, x, g):
    r, xh = _rms_stats(x)
    dg = jnp.sum(dy * xh, axis=0, keepdims=True)
    dxh = dy * g
    dx = r * (dxh - xh * jnp.mean(dxh * xh, axis=-1, keepdims=True))
    return dx, dg


def _split3(x):
    hi = x.astype(BF16)
    r = x - hi.astype(F32)
    mid = r.astype(BF16)
    lo = (r - mid.astype(F32)).astype(BF16)
    return hi, mid, lo


def _mask_mm(mask_bf16, x):
    hi, mid, lo = _split3(x)
    return _dot(mask_bf16, hi) + _dot(mask_bf16, mid) + _dot(mask_bf16, lo)


def _place():
    return lax.axis_index("x"), lax.axis_index("y"), lax.axis_index("c")


def _gather_copies(src, out, send, recv, loc, slicers):
    n = len(src)
    x, y, c = _place()
    me, sib = (x, y, c), (x, y, 1 - c)
    chips = [(1 - x, y), (x, 1 - y), (1 - x, 1 - y)]

    def dev(p):
        return 4 * p[0] + 2 * p[1] + p[2]

    def rc(i, k, block, to, from_src=False):
        dst = slicers[i](out[i], dev(block))
        return pltpu.make_async_remote_copy(
            src_ref=src[i] if from_src else dst, dst_ref=dst, send_sem=send.at[7 * i + k],
            recv_sem=recv.at[7 * i + k], device_id=to, device_id_type=MESH_ID)

    mine = [pltpu.make_async_copy(src[i], slicers[i](out[i], dev(me)), loc.at[i]) for i in range(n)]
    for cp in mine:
        cp.start()
    first = []
    for i in range(n):
        first.append(rc(i, 0, me, sib, True))
        for j, chip in enumerate(chips):
            first.append(rc(i, 1 + j, me, (*chip, c), True))
    for cp in first:
        cp.start()
    passed = []
    for j, chip in enumerate(chips):
        for i in range(n):
            rc(i, 1 + j, (*chip, c), me).wait_recv()
            cp = rc(i, 4 + j, (*chip, c), sib)
            cp.start()
            passed.append(cp)
    for i in range(n):
        rc(i, 0, sib, me).wait_recv()
        for j, chip in enumerate(chips):
            rc(i, 4 + j, (*chip, 1 - c), me).wait_recv()
    for cp in first + passed:
        cp.wait_send()
    for cp in mine:
        cp.wait()


def _gather_scratch(n):
    return [pltpu.SemaphoreType.DMA((7 * n,)), pltpu.SemaphoreType.DMA((7 * n,)), pltpu.SemaphoreType.DMA((n,))]


def _handshake(peers):
    barrier = pltpu.get_barrier_semaphore()
    for peer in peers:
        pl.semaphore_signal(barrier, inc=1, device_id=peer, device_id_type=MESH_ID)
    pl.semaphore_wait(barrier, len(peers))


def _all_gather_async(name, collective_id, srcs, out_shapes, slicers):
    n = len(srcs)

    def body(*refs):
        x, y, c = _place()
        _handshake([(1 - x if dx else x, 1 - y if dy else y, 1 - c if dc else c)
                    for dx in (0, 1) for dy in (0, 1) for dc in (0, 1) if dx or dy or dc])
        _gather_copies(refs[:n], refs[n:2 * n], *refs[2 * n:], slicers)

    return _sequencer_call(name, collective_id, body, srcs, [jax.ShapeDtypeStruct(s, d) for s, d in out_shapes],
                           _gather_scratch(n))


def _sequencer_call(name, collective_id, body, operands, out_types, scratch):
    return pl.kernel(
        body, out_type=out_types, mesh=plsc.ScalarSubcoreMesh(axis_name="sequencer", num_cores=1), name=name,
        scratch_types=scratch, compiler_params=pltpu.CompilerParams(collective_id=collective_id),
    )(*operands)


def _exchange_sibling(name, collective_id, grads, shard_fns, shard_shapes):
    n = len(grads)

    def body(*refs):
        g, land = refs[:n], refs[n:2 * n]
        send, recv = refs[2 * n:]
        x, y, c = _place()
        _handshake([(x, y, 1 - c)])
        remote = []
        for i in range(n):
            for q in range(4):
                cp = pltpu.make_async_remote_copy(
                    src_ref=shard_fns[i](g[i], 2 * q + (1 - c)), dst_ref=land[i].at[q], send_sem=send.at[4 * i + q],
                    recv_sem=recv.at[4 * i + q], device_id=(x, y, 1 - c), device_id_type=MESH_ID)
                cp.start()
                remote.append(cp)
        for cp in remote:
            cp.wait()

    return _sequencer_call(name, collective_id, body, grads, [jax.ShapeDtypeStruct((4, *s), F32) for s in shard_shapes],
                           [pltpu.SemaphoreType.DMA((4 * n,)), pltpu.SemaphoreType.DMA((4 * n,))])


def _exchange_chips(name, collective_id, parts):
    n = len(parts)

    def body(*refs):
        part, out = refs[:n], refs[n:2 * n]
        send, recv = refs[2 * n:]
        x, y, c = _place()
        _handshake([(1 - x, y, c), (x, 1 - y, c), (1 - x, 1 - y, c)])
        remote = []
        for i in range(n):
            for s in range(3):
                qx = 1 - x if (s + 1) // 2 else x
                qy = 1 - y if (s + 1) % 2 else y
                cp = pltpu.make_async_remote_copy(
                    src_ref=part[i].at[2 * qx + qy], dst_ref=out[i].at[s], send_sem=send.at[3 * i + s],
                    recv_sem=recv.at[3 * i + s], device_id=(qx, qy, c), device_id_type=MESH_ID)
                cp.start()
                remote.append(cp)
        for cp in remote:
            cp.wait()

    return _sequencer_call(name, collective_id, body, parts,
                           [jax.ShapeDtypeStruct((3, *p.shape[1:]), p.dtype) for p in parts],
                           [pltpu.SemaphoreType.DMA((3 * n,)), pltpu.SemaphoreType.DMA((3 * n,))])


def _chip_partial(name, core, grad, own_block, own_index, land):
    _, rows, cols = land.shape
    tr = own_block[-2]

    def body(core_ref, a_ref, b_ref, o_ref):
        o_ref[...] = (a_ref[...] + b_ref[...]).astype(BF16)

    spec = pl.BlockSpec((None, tr, cols), lambda q, r, c: (q, r, 0))
    return pl.pallas_call(
        body, name=name, out_shape=jax.ShapeDtypeStruct(land.shape, BF16),
        grid_spec=pltpu.PrefetchScalarGridSpec(
            num_scalar_prefetch=1, grid=(4, rows // tr),
            in_specs=[pl.BlockSpec(own_block, lambda q, r, c: own_index(q, r, c[0])), spec], out_specs=spec),
        compiler_params=_params([((tr, cols), F32)] * 2 + [((tr, cols), BF16)], sem=("arbitrary", "arbitrary")),
    )(core, grad, land)


def _adamw_math(w, g, m, v):
    m = ADAM_B1 * m + (1.0 - ADAM_B1) * g
    v = ADAM_B2 * v + (1.0 - ADAM_B2) * (g * g)
    m_hat = m / (1.0 - ADAM_B1 ** ADAM_STEP)
    v_hat = v / (1.0 - ADAM_B2 ** ADAM_STEP)
    delta = -ADAM_LR * (m_hat / (jnp.sqrt(v_hat) + ADAM_EPS) + ADAM_WD * w)
    return delta, m, v


def _adamw(name, chip, own, landed, w, m, v):
    _, rows, cols = own.shape
    tr = _tile(rows, 256) if rows % 256 == 0 else _tile(rows, 176)

    def body(chip_ref, own_ref, l_ref, w_ref, m_ref, v_ref, g_out, d_out, m_out, v_out):
        g = own_ref[...].astype(F32)
        for s in range(3):
            g = g + l_ref[s].astype(F32)
        delta, m_new, v_new = _adamw_math(w_ref[...], g, m_ref[...], v_ref[...])
        g_out[...] = g
        d_out[...] = delta
        m_out[...] = m_new
        v_out[...] = v_new

    spec = pl.BlockSpec((tr, cols), lambda r, c: (r, 0))
    return pl.pallas_call(
        body, name=name, out_shape=[jax.ShapeDtypeStruct((rows, cols), F32)] * 4,
        grid_spec=pltpu.PrefetchScalarGridSpec(
            num_scalar_prefetch=1, grid=(rows // tr,),
            in_specs=[pl.BlockSpec((None, tr, cols), lambda r, c: (c[0], r, 0)),
                      pl.BlockSpec((3, tr, cols), lambda r, c: (0, r, 0)), spec, spec, spec],
            out_specs=[spec] * 4),
        compiler_params=_params([((4, tr, cols), own.dtype)] + [((tr, cols), F32)] * 7, sem=("arbitrary",)),
    )(chip, own, landed, w, m, v)


def _swap_with_sibling(name, x):
    def body(x_ref, o_ref, send, recv):
        px, py, c = _place()
        cp = pltpu.make_async_remote_copy(src_ref=x_ref, dst_ref=o_ref, send_sem=send, recv_sem=recv,
                                          device_id=(px, py, 1 - c), device_id_type=MESH_ID)
        cp.start()
        cp.wait()

    return pl.pallas_call(
        body, name=name, out_shape=jax.ShapeDtypeStruct(x.shape, x.dtype), in_specs=[ANY], out_specs=ANY,
        scratch_shapes=[pltpu.SemaphoreType.DMA, pltpu.SemaphoreType.DMA],
    )(x)


def _proj_forward_own_chip(positions, x, gain, w_own, w_sibling):
    t = x.shape[0]
    tm = _tile(t, 1024)

    def body(pos_ref, x_ref, g_ref, wo_ref, ws_ref, o_ref, h_ref, ht_ref):
        @pl.when(pl.program_id(1) == 0)
        def _():
            _, xh = _rms_stats(x_ref[...])
            h = (xh * g_ref[...]).astype(BF16)
            h_ref[...] = h
            ht_ref[...] = h.T
            o_ref[...] = _dot(h, wo_ref[...])

        @pl.when(pl.program_id(1) == 1)
        def _():
            o_ref[...] = _dot(h_ref[...], ws_ref[...])

    tok = pl.BlockSpec((tm, D_MODEL), lambda m, k, pos: (m, 0))
    return pl.pallas_call(
        body, name="proj_fwd_own_chip",
        out_shape=[jax.ShapeDtypeStruct((N_DEV, t, D_MODEL), F32), jax.ShapeDtypeStruct((t, D_MODEL), BF16),
                   jax.ShapeDtypeStruct((D_MODEL, t), BF16)],
        grid_spec=pltpu.PrefetchScalarGridSpec(
            num_scalar_prefetch=1, grid=(t // tm, 2),
            in_specs=[tok, pl.BlockSpec((1, D_MODEL), lambda m, k, pos: (0, 0)), RESIDENT, RESIDENT],
            out_specs=[pl.BlockSpec((None, tm, D_MODEL), lambda m, k, pos: (pos[k], m, 0)), tok,
                       pl.BlockSpec((D_MODEL, tm), lambda m, k, pos: (0, m))]),
        compiler_params=_params([((tm, D_MODEL), F32)] * 2 + [((tm, D_MODEL), BF16)] * 2,
                                scratch=[((2, D_MODEL, D_MODEL), BF16)], temps=6 << 20, sem=("arbitrary", "arbitrary")),
    )(positions, x, gain, w_own, w_sibling)


def _proj_forward_other_chips(positions, proj, h, w_in_g):
    t = h.shape[0]
    tm = _tile(t, 1024)

    def body(pos_ref, _, h_ref, w_ref, o_ref):
        o_ref[...] = _dot(h_ref[...], w_ref[pos_ref[pl.program_id(1)]])

    return pl.pallas_call(
        body, name="proj_fwd_other_chips", out_shape=jax.ShapeDtypeStruct(proj.shape, F32),
        grid_spec=pltpu.PrefetchScalarGridSpec(
            num_scalar_prefetch=1, grid=(t // tm, N_DEV - 2),
            in_specs=[ANY, pl.BlockSpec((tm, D_MODEL), lambda m, k, pos: (m, 0)), RESIDENT],
            out_specs=pl.BlockSpec((None, tm, D_MODEL), lambda m, k, pos: (pos[k], m, 0))),
        input_output_aliases={1: 0},
        compiler_params=_params([((tm, D_MODEL), F32), ((tm, D_MODEL), BF16)], scratch=[((N_DEV, D_MODEL, D_MODEL), BF16)],
                                temps=6 << 20, sem=("arbitrary", "arbitrary")),
    )(positions, proj, h, w_in_g)


def _masked_ws(ws_ref, g):
    row = lax.broadcasted_iota(jnp.int32, (GMLP_CHUNK, GMLP_CHUNK), 0)
    col = lax.broadcasted_iota(jnp.int32, (GMLP_CHUNK, GMLP_CHUNK), 1)
    return jnp.where(row >= col, ws_ref[g], 0.0).astype(BF16)


def _gmlp_forward(proj, ln_g, ln_b, w_s, bias_b):
    t = proj.shape[1]
    tm = _tile(t, 256)
    chunks = tm // GMLP_CHUNK

    def body(u_ref, v_ref, lng_ref, lnb_ref, ws_ref, bias_ref, a_ref, vn_scr):
        vv = _gelu(v_ref[...])
        mu = jnp.mean(vv, axis=-1, keepdims=True)
        cen = vv - mu
        var = jnp.mean(cen * cen, axis=-1, keepdims=True)
        vn_scr[...] = ((cen * lax.rsqrt(var + NORM_EPS)) * lng_ref[...] + lnb_ref[...]).astype(BF16)
        for g in range(GROUPS):
            wm = _masked_ws(ws_ref, g)
            cols = slice(g * HEAD_DIM, (g + 1) * HEAD_DIM)
            for c in range(chunks):
                rows = slice(c * GMLP_CHUNK, (c + 1) * GMLP_CHUNK)
                mixed = _dot(wm, vn_scr[rows, cols]) + bias_ref[g]
                a_ref[rows, cols] = (_gelu(u_ref[rows, cols]) * mixed).astype(BF16)

    small = pl.BlockSpec((GROUPS, GMLP_CHUNK, GMLP_CHUNK), lambda m: (0, 0, 0))
    vec = pl.BlockSpec((1, D_MODEL), lambda m: (0, 0))
    return pl.pallas_call(
        body, name="gmlp_fwd", out_shape=jax.ShapeDtypeStruct((t, D_MODEL), BF16), grid=(t // tm,),
        in_specs=[pl.BlockSpec((None, tm, D_MODEL), lambda m: (U_POS, m, 0)),
                  pl.BlockSpec((None, tm, D_MODEL), lambda m: (U_POS + 1, m, 0)), vec, vec, small, small],
        out_specs=pl.BlockSpec((tm, D_MODEL), lambda m: (m, 0)),
        scratch_shapes=[pltpu.VMEM((tm, D_MODEL), BF16)],
        compiler_params=_params([((tm, D_MODEL), F32)] * 2 + [((tm, D_MODEL), BF16)] + [((8, 128, 128), F32)] * 2,
                                scratch=[((tm, D_MODEL), BF16)], temps=8 << 20, sem=("arbitrary",)),
    )(proj, proj, ln_g, ln_b, w_s, bias_b)


def _lower_bound(tab_ref):
    t0, t1 = tab_ref[0:1, :], tab_ref[1:2, :]
    mx = jnp.maximum(t0, t1)
    e0, e1 = jnp.exp(t0 - mx), jnp.exp(t1 - mx)
    return e0 / (e0 + e1)


def _tri_masks():
    row = lax.broadcasted_iota(jnp.int32, (HGRN_CHUNK, HGRN_CHUNK), 0)
    col = lax.broadcasted_iota(jnp.int32, (HGRN_CHUNK, HGRN_CHUNK), 1)
    return row >= col, row <= col


def _chunk_rows(c):
    return slice(c * HGRN_CHUNK, (c + 1) * HGRN_CHUNK)


def _per_chunk(x, nc, fn):
    return jnp.concatenate([fn(x[_chunk_rows(c)]) for c in range(nc)], axis=0)


def _chunk_row_bcast(x, nc, i):
    return _per_chunk(x, nc, lambda xc: jnp.broadcast_to(xc[i:i + 1, :], (HGRN_CHUNK, HEAD_DIM)))


def _hgrn_gates(q, fl, lb, nc):
    lower, _ = _tri_masks()
    lower = lower.astype(BF16)
    s = _sigmoid(fl)
    f = lb + (1.0 - lb) * s
    k = 1.0 - f
    hi, mid, lo = _split3(jnp.log(f))
    a = jnp.concatenate([_dot(lower, hi[_chunk_rows(c)]) + _dot(lower, mid[_chunk_rows(c)]) + _dot(lower, lo[_chunk_rows(c)])
                         for c in range(nc)], axis=0)
    a_mid = _chunk_row_bcast(a, nc, HGRN_CHUNK // 2 - 1)
    a_last = _chunk_row_bcast(a, nc, HGRN_CHUNK - 1)
    qs = q * HGRN_SCALE
    e_in, e_out, e_end, e_all = jnp.exp(a - a_mid), jnp.exp(a_mid - a), jnp.exp(a_last - a), jnp.exp(a)
    decay = [jnp.exp(a[c * HGRN_CHUNK + HGRN_CHUNK - 1:(c + 1) * HGRN_CHUNK, :]) for c in range(nc)]
    return dict(s=s, f=f, k=k, decay=decay, e_in=e_in, e_out=e_out, e_end=e_end, e_all=e_all,
                qi=qs * e_in, ki=k * e_out, kd=k * e_end, qe=qs * e_all)


def _hgrn_forward(proj, lb_table, norm_g):
    t = proj.shape[1]
    tb = _tile(t, 1024)
    nc = tb // HGRN_CHUNK
    n_chunks = t // HGRN_CHUNK

    def body(q_ref, f_ref, i_ref, g_ref, tab_ref, ng_ref, og_ref, o_ref, st_ref, state):
        @pl.when(pl.program_id(1) == 0)
        def _():
            state[...] = jnp.zeros_like(state)

        lower, _ = _tri_masks()
        gt = _hgrn_gates(q_ref[...], f_ref[...], _lower_bound(tab_ref), nc)
        qi, ki, kd, qe = (gt[n].astype(BF16) for n in ("qi", "ki", "kd", "qe"))
        vb = i_ref[...].astype(BF16)
        o_intra, d_state = [], []
        for c in range(nc):
            rows = _chunk_rows(c)
            p = jnp.where(lower, _dot_nt(qi[rows], ki[rows]), 0.0).astype(BF16)
            o_intra.append(_dot(p, vb[rows]))
            d_state.append(_dot_tn(vb[rows], kd[rows]))
        st = state[...]
        outs = []
        for c in range(nc):
            st_ref[c] = st
            outs.append(o_intra[c] + _dot_nt(qe[_chunk_rows(c)], st.astype(BF16)))
            st = st * gt["decay"][c] + d_state[c]
        state[...] = st
        o = jnp.concatenate(outs, axis=0)
        o_ref[...] = o
        _, oh = _rms_stats(o)
        gz = g_ref[...]
        og_ref[...] = ((oh * ng_ref[...]) * (gz * _sigmoid(gz))).astype(BF16)

    def blk(p):
        return pl.BlockSpec((None, tb, HEAD_DIM), lambda h, n: (p, n, h))

    out_blk = pl.BlockSpec((tb, HEAD_DIM), lambda h, n: (n, h))
    return pl.pallas_call(
        body, name="hgrn_fwd",
        out_shape=[jax.ShapeDtypeStruct((t, D_MODEL), BF16), jax.ShapeDtypeStruct((t, D_MODEL), F32),
                   jax.ShapeDtypeStruct((HEADS, n_chunks, HEAD_DIM, HEAD_DIM), F32)],
        grid=(HEADS, t // tb),
        in_specs=[blk(Q_POS), blk(Q_POS + 1), blk(Q_POS + 2), blk(Q_POS + 3),
                  pl.BlockSpec((2, HEAD_DIM), lambda h, n: (0, h)), pl.BlockSpec((1, HEAD_DIM), lambda h, n: (0, h))],
        out_specs=[out_blk, out_blk, pl.BlockSpec((None, nc, HEAD_DIM, HEAD_DIM), lambda h, n: (h, n, 0, 0))],
        scratch_shapes=[pltpu.VMEM((HEAD_DIM, HEAD_DIM), F32)],
        compiler_params=_params([((tb, HEAD_DIM), F32)] * 6 + [((nc, HEAD_DIM, HEAD_DIM), F32)], temps=8 << 20,
                                sem=("arbitrary", "arbitrary")),
    )(proj, proj, proj, proj, lb_table, norm_g)


def _branch_out_forward(a, og, proj, x, w_a, w_b, w_out, ffn_g):
    t = x.shape[0]
    tm = _tile(t, 256)

    def body(a_ref, og_ref, ga_ref, gb_ref, x_ref, wa_ref, wb_ref, wo_ref, g_ref, ya_ref, yb_ref, mg_ref, x1_ref, h2_ref,
             h2t_ref):
        ya = _dot(a_ref[...], wa_ref[...])
        yb = _dot(og_ref[...], wb_ref[...])
        ya_ref[...] = ya
        yb_ref[...] = yb
        merged = (_sigmoid(ga_ref[...]) * ya + _sigmoid(gb_ref[...]) * yb).astype(BF16)
        mg_ref[...] = merged
        x1 = x_ref[...] + _dot(merged, wo_ref[...])
        x1_ref[...] = x1
        _, xh = _rms_stats(x1)
        h2 = (xh * g_ref[...]).astype(BF16)
        h2_ref[...] = h2
        h2t_ref[...] = h2.T

    tok = pl.BlockSpec((tm, D_MODEL), lambda m: (m, 0))
    wsp = pl.BlockSpec((D_MODEL, D_MODEL), lambda m: (0, 0))
    return pl.pallas_call(
        body, name="branch_out_fwd",
        out_shape=[jax.ShapeDtypeStruct((t, D_MODEL), F32), jax.ShapeDtypeStruct((t, D_MODEL), F32),
                   jax.ShapeDtypeStruct((t, D_MODEL), BF16), jax.ShapeDtypeStruct((t, D_MODEL), F32),
                   jax.ShapeDtypeStruct((t, D_MODEL), BF16), jax.ShapeDtypeStruct((D_MODEL, t), BF16)],
        grid=(t // tm,),
        in_specs=[tok, tok, pl.BlockSpec((None, tm, D_MODEL), lambda m: (GATE_POS, m, 0)),
                  pl.BlockSpec((None, tm, D_MODEL), lambda m: (GATE_POS + 1, m, 0)), tok, wsp, wsp, wsp,
                  pl.BlockSpec((1, D_MODEL), lambda m: (0, 0))],
        out_specs=[tok] * 5 + [pl.BlockSpec((D_MODEL, tm), lambda m: (0, m))],
        compiler_params=_params([((tm, D_MODEL), BF16)] * 5 + [((tm, D_MODEL), F32)] * 6 + [((D_MODEL, D_MODEL), BF16)] * 3,
                                temps=8 << 20, sem=("arbitrary",)),
    )(a, og, proj, proj, x, w_a, w_b, w_out, ffn_g)


def _ffn_forward(h2, x1, w_gu, w_down, target, final_g):
    t = x1.shape[0]
    tm = _tile(t, 512)

    def body(h_ref, wgu_ref, wd_ref, x1_ref, t_ref, g_ref, gu_ref, act_ref, loss_ref, dg_ref, dx_ref, dxb_ref, acc):
        m, j = pl.program_id(0), pl.program_id(1)

        @pl.when((m == 0) & (j == 0))
        def _():
            loss_ref[...] = jnp.zeros_like(loss_ref)
            dg_ref[...] = jnp.zeros_like(dg_ref)

        h = h_ref[...]
        gate = _dot(h, wgu_ref[j])
        up = _dot(h, wgu_ref[j + 4])
        gu_ref[0] = gate
        gu_ref[1] = up
        act = ((gate * _sigmoid(gate)) * up).astype(BF16)
        act_ref[...] = act
        part = _dot(act, wd_ref[j])

        @pl.when(j == 0)
        def _():
            acc[...] = part

        @pl.when((j > 0) & (j < 3))
        def _():
            acc[...] += part

        @pl.when(j == 3)
        def _():
            x2 = x1_ref[...] + (acc[...] + part)
            g = g_ref[...]
            r, xh = _rms_stats(x2)
            err = xh * g - t_ref[...]
            loss_ref[...] += 0.5 * jnp.sum(jnp.mean(err * err, axis=-1, keepdims=True), axis=0, keepdims=True)
            dy = err * (1.0 / D_MODEL)
            dg_ref[...] += jnp.sum(dy * xh, axis=0, keepdims=True)
            dxh = dy * g
            dx = r * (dxh - xh * jnp.mean(dxh * xh, axis=-1, keepdims=True))
            dx_ref[...] = dx
            dxb_ref[...] = dx.astype(BF16)

    tok = pl.BlockSpec((tm, D_MODEL), lambda m, j: (m, 0))
    vec = pl.BlockSpec((1, D_MODEL), lambda m, j: (0, 0))
    return pl.pallas_call(
        body, name="ffn_fwd",
        out_shape=[jax.ShapeDtypeStruct((4, 2, t, FF_BLOCK), F32), jax.ShapeDtypeStruct((4, t, FF_BLOCK), BF16),
                   jax.ShapeDtypeStruct((8, 128), F32), jax.ShapeDtypeStruct((1, D_MODEL), F32),
                   jax.ShapeDtypeStruct((t, D_MODEL), F32), jax.ShapeDtypeStruct((t, D_MODEL), BF16)],
        grid=(t // tm, 4),
        in_specs=[tok, RESIDENT, RESIDENT, tok, tok, vec],
        out_specs=[pl.BlockSpec((None, 2, tm, FF_BLOCK), lambda m, j: (j, 0, m, 0)),
                   pl.BlockSpec((None, tm, FF_BLOCK), lambda m, j: (j, m, 0)),
                   pl.BlockSpec((8, 128), lambda m, j: (0, 0)), vec, tok, tok],
        scratch_shapes=[pltpu.VMEM((tm, D_MODEL), F32)],
        compiler_params=_params([((tm, D_MODEL), BF16), ((tm, D_MODEL), F32), ((tm, D_MODEL), F32), ((2, tm, 768), F32),
                                 ((tm, 768), BF16), ((tm, D_MODEL), F32), ((tm, D_MODEL), BF16)],
                                scratch=[((tm, D_MODEL), F32), ((N_DEV, D_MODEL, 768), BF16), ((D_FF, D_MODEL), BF16)],
                                temps=6 << 20, sem=("arbitrary", "arbitrary")),
    )(h2, w_gu, w_down.reshape(4, FF_BLOCK, D_MODEL), x1, target, final_g)


def _ffn_backward(dx2b, dx2, gu, x1, w_gu, w_down, ffn_g):
    t = x1.shape[0]
    tm = _tile(t, 512)

    def body(dxb_ref, dx2_ref, gu_ref, x1_ref, wgu_ref, wd_ref, g_ref, dgu_ref, dx1_ref, dx1b_ref, dg_ref, acc, prev):
        m, j = pl.program_id(0), pl.program_id(1)

        @pl.when((m == 0) & (j == 0))
        def _():
            dg_ref[...] = jnp.zeros_like(dg_ref)

        @pl.when(j == 0)
        def _():
            prev[...] = jnp.zeros_like(prev)
            acc[...] = jnp.zeros_like(acc)

        jm1 = jnp.maximum(j - 1, 0)
        acc[...] += _dot_nt(prev[0], wgu_ref[jm1]) + _dot_nt(prev[1], wgu_ref[jm1 + 4])
        dact = _dot_nt(dxb_ref[...], wd_ref[j])
        gate, up = gu_ref[0], gu_ref[1]
        sg = _sigmoid(gate)
        dgate = (dact * up * (sg * (1.0 + gate * (1.0 - sg)))).astype(BF16)
        dup = (dact * (gate * sg)).astype(BF16)
        dgu_ref[0] = dgate
        dgu_ref[1] = dup
        prev[0] = dgate
        prev[1] = dup

        @pl.when(j == 3)
        def _():
            dh2 = acc[...] + (_dot_nt(prev[0], wgu_ref[3]) + _dot_nt(prev[1], wgu_ref[7]))
            dx, dg = _rms_bwd(dh2, x1_ref[...], g_ref[...])
            dx1 = dx2_ref[...] + dx
            dx1_ref[...] = dx1
            dx1b_ref[...] = dx1.astype(BF16)
            dg_ref[...] += dg

    tok = pl.BlockSpec((tm, D_MODEL), lambda m, j: (m, 0))
    vec = pl.BlockSpec((1, D_MODEL), lambda m, j: (0, 0))
    gu_spec = pl.BlockSpec((None, 2, tm, FF_BLOCK), lambda m, j: (j, 0, m, 0))
    return pl.pallas_call(
        body, name="ffn_bwd",
        out_shape=[jax.ShapeDtypeStruct((4, 2, t, FF_BLOCK), BF16), jax.ShapeDtypeStruct((t, D_MODEL), F32),
                   jax.ShapeDtypeStruct((t, D_MODEL), BF16), jax.ShapeDtypeStruct((1, D_MODEL), F32)],
        grid=(t // tm, 4),
        in_specs=[tok, tok, gu_spec, tok, RESIDENT, RESIDENT, vec],
        out_specs=[gu_spec, tok, tok, vec],
        scratch_shapes=[pltpu.VMEM((tm, D_MODEL), F32), pltpu.VMEM((2, tm, FF_BLOCK), BF16)],
        compiler_params=_params([((tm, D_MODEL), BF16), ((tm, D_MODEL), F32), ((2, tm, 768), F32), ((tm, D_MODEL), F32),
                                 ((2, tm, 768), BF16), ((tm, D_MODEL), F32), ((tm, D_MODEL), BF16)],
                                scratch=[((tm, D_MODEL), F32), ((2, tm, 768), BF16), ((N_DEV, D_MODEL, 768), BF16),
                                         ((D_FF, D_MODEL), BF16)],
                                temps=4 << 20, sem=("arbitrary", "arbitrary")),
    )(dx2b, dx2, gu, x1, w_gu, w_down.reshape(4, FF_BLOCK, D_MODEL), ffn_g)


def _branch_out_backward(dx1b, ya, yb, proj, w_a, w_b, w_out):
    t = ya.shape[0]
    tm = _tile(t, 256)

    def body(dx_ref, ya_ref, yb_ref, ga_ref, gb_ref, wa_ref, wb_ref, wo_ref, dya_ref, dyb_ref, dgate_ref, da_ref, dog_ref):
        dm = _dot_nt(dx_ref[...], wo_ref[...])
        sa, sb = _sigmoid(ga_ref[...]), _sigmoid(gb_ref[...])
        dya = (dm * sa).astype(BF16)
        dyb = (dm * sb).astype(BF16)
        dya_ref[...] = dya
        dyb_ref[...] = dyb
        dgate_ref[0] = (dm * ya_ref[...] * (sa * (1.0 - sa))).astype(BF16)
        dgate_ref[1] = (dm * yb_ref[...] * (sb * (1.0 - sb))).astype(BF16)
        da_ref[...] = _dot_nt(dya, wa_ref[...])
        dog_ref[...] = _dot_nt(dyb, wb_ref[...])

    tok = pl.BlockSpec((tm, D_MODEL), lambda m: (m, 0))
    wsp = pl.BlockSpec((D_MODEL, D_MODEL), lambda m: (0, 0))
    return pl.pallas_call(
        body, name="branch_out_bwd",
        out_shape=[jax.ShapeDtypeStruct((t, D_MODEL), BF16), jax.ShapeDtypeStruct((t, D_MODEL), BF16),
                   jax.ShapeDtypeStruct((N_DEV, t, D_MODEL), BF16), jax.ShapeDtypeStruct((t, D_MODEL), F32),
                   jax.ShapeDtypeStruct((t, D_MODEL), F32)],
        grid=(t // tm,),
        in_specs=[tok, tok, tok, pl.BlockSpec((None, tm, D_MODEL), lambda m: (GATE_POS, m, 0)),
                  pl.BlockSpec((None, tm, D_MODEL), lambda m: (GATE_POS + 1, m, 0)), wsp, wsp, wsp],
        out_specs=[tok, tok, pl.BlockSpec((2, tm, D_MODEL), lambda m: (GATE_POS // 2, m, 0)), tok, tok],
        compiler_params=_params([((tm, D_MODEL), BF16)] * 5 + [((tm, D_MODEL), F32)] * 6 + [((D_MODEL, D_MODEL), BF16)] * 3,
                                temps=8 << 20, sem=("arbitrary",)),
    )(dx1b, ya, yb, proj, proj, w_a, w_b, w_out)


def _hgrn_backward(dproj, dog, o_saved, states, proj, lb_table, norm_g):
    t = proj.shape[1]
    tb = _tile(t, 1024)
    nc = tb // HGRN_CHUNK
    nb = t // tb

    def body(_, dog_ref, o_ref, st_ref, q_ref, f_ref, i_ref, g_ref, tab_ref, ng_ref, dp_ref, dng_ref, dtab_ref, gstate):
        @pl.when(pl.program_id(1) == 0)
        def _():
            gstate[...] = jnp.zeros_like(gstate)
            dng_ref[...] = jnp.zeros_like(dng_ref)
            dtab_ref[...] = jnp.zeros_like(dtab_ref)

        lb = _lower_bound(tab_ref)
        ng = ng_ref[...]
        lower, upper = _tri_masks()
        gt = _hgrn_gates(q_ref[...], f_ref[...], lb, nc)
        qi, ki, kd, qe = (gt[n].astype(BF16) for n in ("qi", "ki", "kd", "qe"))
        vb = i_ref[...].astype(BF16)
        o, gz, d_og = o_ref[...], g_ref[...], dog_ref[...]
        r, oh = _rms_stats(o)
        sg = _sigmoid(gz)
        d_on = d_og * (gz * sg)
        dgz = d_og * (oh * ng) * (sg * (1.0 + gz * (1.0 - sg)))
        dng_ref[...] += jnp.sum(d_on * oh, axis=0, keepdims=True)
        doh = d_on * ng
        dob = (r * (doh - oh * jnp.mean(doh * oh, axis=-1, keepdims=True))).astype(BF16)
        dv_intra, dqi, dki, dqe, g_upd = [], [], [], [], []
        for c in range(nc):
            rows = _chunk_rows(c)
            p = jnp.where(lower, _dot_nt(qi[rows], ki[rows]), 0.0).astype(BF16)
            dv_intra.append(_dot_tn(p, dob[rows]))
            dp = jnp.where(lower, _dot_nt(dob[rows], vb[rows]), 0.0).astype(BF16)
            dqi.append(_dot(dp, ki[rows]))
            dki.append(_dot_tn(dp, qi[rows]))
            dqe.append(_dot(dob[rows], st_ref[c].astype(BF16)))
            g_upd.append(_dot_tn(dob[rows], qe[rows]))
        g_after = [None] * nc
        g = gstate[...]
        for c in reversed(range(nc)):
            g_after[c] = g
            g = g * gt["decay"][c] + g_upd[c]
        gstate[...] = g
        dkd, dv, da_last = [], [], []
        for c in range(nc):
            rows = _chunk_rows(c)
            gb = g_after[c].astype(BF16)
            dkd.append(_dot(vb[rows], gb))
            dv.append(dv_intra[c] + _dot_nt(kd[rows], gb))
            da_last.append(jnp.sum(g_after[c] * st_ref[c], axis=0, keepdims=True) * gt["decay"][c])
        dqi, dki, dqe, dkd, dv = (jnp.concatenate(z, axis=0) for z in (dqi, dki, dqe, dkd, dv))
        dqs = dqi * gt["e_in"] + dqe * gt["e_all"]
        dk = dki * gt["e_out"] + dkd * gt["e_end"]
        t_in, t_out, t_end = dqi * gt["qi"], dki * gt["ki"], dkd * gt["kd"]
        da = t_in - t_out + dqe * gt["qe"] - t_end
        row = lax.broadcasted_iota(jnp.int32, (HGRN_CHUNK, HEAD_DIM), 0)
        d_mid = t_out - t_in
        pieces = []
        for c in range(nc):
            rows = _chunk_rows(c)
            da_mid = jnp.sum(d_mid[rows], axis=0, keepdims=True)
            da_end = jnp.sum(t_end[rows], axis=0, keepdims=True) + da_last[c]
            da_c = da[rows] + jnp.where(row == HGRN_CHUNK // 2 - 1, da_mid, 0.0) + jnp.where(row == HGRN_CHUNK - 1, da_end, 0.0)
            pieces.append(_mask_mm(upper.astype(BF16), da_c))
        df = jnp.concatenate(pieces, axis=0) / gt["f"] - dk
        s = gt["s"]
        dlb = jnp.sum(df * (1.0 - s), axis=0, keepdims=True)
        dp_ref[0] = (dqs * HGRN_SCALE).astype(BF16)
        dp_ref[1] = (df * (1.0 - lb) * (s * (1.0 - s))).astype(BF16)
        dp_ref[2] = dv.astype(BF16)
        dp_ref[3] = dgz.astype(BF16)
        dt0 = dlb * (lb * (1.0 - lb))
        dtab_ref[0:1, :] += dt0
        dtab_ref[1:2, :] -= dt0

    def blk(p):
        return pl.BlockSpec((None, tb, HEAD_DIM), lambda h, n: (p, nb - 1 - n, h))

    tok = pl.BlockSpec((tb, HEAD_DIM), lambda h, n: (nb - 1 - n, h))
    return pl.pallas_call(
        body, name="hgrn_bwd",
        out_shape=[jax.ShapeDtypeStruct((N_DEV, t, D_MODEL), BF16), jax.ShapeDtypeStruct((1, D_MODEL), F32),
                   jax.ShapeDtypeStruct((2, D_MODEL), F32)],
        grid=(HEADS, nb),
        in_specs=[ANY, tok, tok, pl.BlockSpec((None, nc, HEAD_DIM, HEAD_DIM), lambda h, n: (h, nb - 1 - n, 0, 0)),
                  blk(Q_POS), blk(Q_POS + 1), blk(Q_POS + 2), blk(Q_POS + 3),
                  pl.BlockSpec((2, HEAD_DIM), lambda h, n: (0, h)), pl.BlockSpec((1, HEAD_DIM), lambda h, n: (0, h))],
        out_specs=[pl.BlockSpec((4, tb, HEAD_DIM), lambda h, n: (0, nb - 1 - n, h)),
                   pl.BlockSpec((1, HEAD_DIM), lambda h, n: (0, h)), pl.BlockSpec((2, HEAD_DIM), lambda h, n: (0, h))],
        scratch_shapes=[pltpu.VMEM((HEAD_DIM, HEAD_DIM), F32)],
        input_output_aliases={0: 0},
        compiler_params=_params([((tb, HEAD_DIM), F32)] * 6 + [((nc, HEAD_DIM, HEAD_DIM), F32)] + [((4, tb, HEAD_DIM), BF16)],
                                temps=8 << 20, sem=("arbitrary", "arbitrary")),
    )(dproj, dog, o_saved, states, proj, proj, proj, proj, lb_table, norm_g)


def _gmlp_backward(dproj, da, proj, ln_g, ln_b, w_s, bias_b):
    t = proj.shape[1]
    tm = _tile(t, 256)
    chunks = tm // GMLP_CHUNK

    def body(_, da_ref, u_ref, v_ref, lng_ref, lnb_ref, ws_ref, bias_ref, dp_ref, dlng_ref, dlnb_ref, dws_ref, dbs_ref,
             vn_scr, dvn_scr):
        @pl.when(pl.program_id(0) == 0)
        def _():
            dlng_ref[...] = jnp.zeros_like(dlng_ref)
            dlnb_ref[...] = jnp.zeros_like(dlnb_ref)
            dws_ref[...] = jnp.zeros_like(dws_ref)
            dbs_ref[...] = jnp.zeros_like(dbs_ref)

        v = v_ref[...]
        vv = _gelu(v)
        mu = jnp.mean(vv, axis=-1, keepdims=True)
        cen = vv - mu
        rstd = lax.rsqrt(jnp.mean(cen * cen, axis=-1, keepdims=True) + NORM_EPS)
        vhat = cen * rstd
        lng = lng_ref[...]
        vn_scr[...] = (vhat * lng + lnb_ref[...]).astype(BF16)
        row = lax.broadcasted_iota(jnp.int32, (GMLP_CHUNK, GMLP_CHUNK), 0)
        col = lax.broadcasted_iota(jnp.int32, (GMLP_CHUNK, GMLP_CHUNK), 1)
        for g in range(GROUPS):
            wm = _masked_ws(ws_ref, g)
            cols = slice(g * HEAD_DIM, (g + 1) * HEAD_DIM)
            dws = jnp.zeros((GMLP_CHUNK, GMLP_CHUNK), F32)
            dbs = jnp.zeros((GMLP_CHUNK, GMLP_CHUNK), F32)
            for c in range(chunks):
                rows = slice(c * GMLP_CHUNK, (c + 1) * GMLP_CHUNK)
                vn = vn_scr[rows, cols]
                mixed = _dot(wm, vn) + bias_ref[g]
                u = u_ref[rows, cols]
                d_a = da_ref[rows, cols]
                dp_ref[0, rows, cols] = (d_a * mixed * _gelu_grad(u)).astype(BF16)
                dmix = d_a * _gelu(u)
                dmb = dmix.astype(BF16)
                dbs = dbs + dmix
                dws = dws + _dot_nt(dmb, vn)
                dvn_scr[rows, cols] = _dot_tn(wm, dmb)
            dws_ref[g] += jnp.where(row >= col, dws, 0.0)
            dbs_ref[g] += jnp.broadcast_to(jnp.sum(dbs, axis=-1, keepdims=True), (GMLP_CHUNK, GMLP_CHUNK))
        dvn = dvn_scr[...]
        dlng_ref[...] += jnp.sum(dvn * vhat, axis=0, keepdims=True)
        dlnb_ref[...] += jnp.sum(dvn, axis=0, keepdims=True)
        dvh = dvn * lng
        dvv = rstd * (dvh - jnp.mean(dvh, axis=-1, keepdims=True) - vhat * jnp.mean(dvh * vhat, axis=-1, keepdims=True))
        dp_ref[1] = (dvv * _gelu_grad(v)).astype(BF16)

    tok = pl.BlockSpec((tm, D_MODEL), lambda m: (m, 0))
    small = pl.BlockSpec((GROUPS, GMLP_CHUNK, GMLP_CHUNK), lambda m: (0, 0, 0))
    vec = pl.BlockSpec((1, D_MODEL), lambda m: (0, 0))
    return pl.pallas_call(
        body, name="gmlp_bwd",
        out_shape=[jax.ShapeDtypeStruct(dproj.shape, BF16), jax.ShapeDtypeStruct((1, D_MODEL), F32),
                   jax.ShapeDtypeStruct((1, D_MODEL), F32), jax.ShapeDtypeStruct((GROUPS, GMLP_CHUNK, GMLP_CHUNK), F32),
                   jax.ShapeDtypeStruct((GROUPS, GMLP_CHUNK, GMLP_CHUNK), F32)],
        grid=(t // tm,),
        in_specs=[ANY, tok, pl.BlockSpec((None, tm, D_MODEL), lambda m: (U_POS, m, 0)),
                  pl.BlockSpec((None, tm, D_MODEL), lambda m: (U_POS + 1, m, 0)), vec, vec, small, small],
        out_specs=[pl.BlockSpec((2, tm, D_MODEL), lambda m: (U_POS // 2, m, 0)), vec, vec, small, small],
        scratch_shapes=[pltpu.VMEM((tm, D_MODEL), BF16), pltpu.VMEM((tm, D_MODEL), F32)],
        input_output_aliases={0: 0},
        compiler_params=_params([((tm, D_MODEL), F32)] * 3 + [((2, tm, D_MODEL), BF16)] + [((8, 128, 128), F32)] * 4,
                                scratch=[((tm, D_MODEL), BF16), ((tm, D_MODEL), F32)], temps=12 << 20, sem=("arbitrary",)),
    )(dproj, da, proj, proj, ln_g, ln_b, w_s, bias_b)


def _input_backward(dproj, w_in_g, x, dx1, mix_g):
    t = x.shape[0]
    tm = _tile(t, 512)

    def body(dp_ref, w_ref, x_ref, dx1_ref, g_ref, dx_ref, dg_ref):
        @pl.when(pl.program_id(0) == 0)
        def _():
            dg_ref[...] = jnp.zeros_like(dg_ref)

        dh = _dot_nt(dp_ref[0], w_ref[0])
        for p in range(1, N_DEV):
            dh = dh + _dot_nt(dp_ref[p], w_ref[p])
        dx, dg = _rms_bwd(dh, x_ref[...], g_ref[...])
        dx_ref[...] = dx1_ref[...] + dx
        dg_ref[...] += dg

    tok = pl.BlockSpec((tm, D_MODEL), lambda m: (m, 0))
    vec = pl.BlockSpec((1, D_MODEL), lambda m: (0, 0))
    return pl.pallas_call(
        body, name="input_bwd",
        out_shape=[jax.ShapeDtypeStruct((t, D_MODEL), F32), jax.ShapeDtypeStruct((1, D_MODEL), F32)],
        grid=(t // tm,),
        in_specs=[pl.BlockSpec((N_DEV, tm, D_MODEL), lambda m: (0, m, 0)), RESIDENT, tok, tok, vec],
        out_specs=[tok, vec],
        compiler_params=_params([((N_DEV, tm, D_MODEL), BF16)] + [((tm, D_MODEL), F32)] * 3,
                                scratch=[((N_DEV, D_MODEL, D_MODEL), BF16)], temps=6 << 20, sem=("arbitrary",)),
    )(dproj, w_in_g, x, dx1, mix_g)


def _weight_grad(name, a, b, a_spec, b_spec, out_shape, out_spec, steps, blocks, a_is_transposed):
    def body(a_ref, b_ref, o_ref):
        o_ref[...] = _dot(a_ref[...], b_ref[...]) if a_is_transposed else _dot_tn(a_ref[...], b_ref[...])

    return pl.pallas_call(
        body, name=name, out_shape=jax.ShapeDtypeStruct(out_shape, F32), grid=(steps,), in_specs=[a_spec, b_spec],
        out_specs=out_spec, compiler_params=_params(blocks, temps=4 << 20, sem=("arbitrary",)),
    )(a, b)


def _pack_small(mix_g, ln_g, ln_b, b_s, lb_table, hg_norm, ffn_g, final_g, loss_row):
    def part(a):
        a = a.reshape(-1, D_MODEL)
        return jnp.pad(a, ((0, 8 - a.shape[0]), (0, 0)))

    return jnp.concatenate([part(mix_g), part(ln_g), part(ln_b), part(hg_norm), part(ffn_g), part(final_g),
                            part(lb_table), part(b_s), part(loss_row)], axis=0)


def _unpack_small(pack, w_s):
    return dict(norm_mix_g=pack[0:1], gmlp_ln_g=pack[8:9], gmlp_ln_b=pack[16:17], hgrn_norm_g=pack[24:25],
                norm_ffn_g=pack[32:33], norm_final_g=pack[40], hgrn_lb_table=pack[48:50],
                gmlp_b_s=pack[56:57].reshape(1, GROUPS, GMLP_CHUNK),
                gmlp_w_s=w_s.reshape(1, GROUPS, GMLP_CHUNK, GMLP_CHUNK))


def _adamw_small(name, gathered, w, m, v):
    rows, cols = w.shape

    def body(p_ref, w_ref, m_ref, v_ref, g_out, d_out, m_out, v_out):
        g = p_ref[0]
        for j in range(1, N_DEV):
            g = g + p_ref[j]
        delta, m_new, v_new = _adamw_math(w_ref[...], g, m_ref[...], v_ref[...])
        g_out[...] = g
        d_out[...] = delta
        m_out[...] = m_new
        v_out[...] = v_new

    tr = _tile(rows, 512)
    spec = pl.BlockSpec((tr, cols), lambda r: (r, 0))
    return pl.pallas_call(
        body, name=name, out_shape=[jax.ShapeDtypeStruct((rows, cols), F32)] * 4, grid=(rows // tr,),
        in_specs=[pl.BlockSpec((N_DEV, tr, cols), lambda r: (0, r, 0)), spec, spec, spec], out_specs=[spec] * 4,
        compiler_params=_params([((N_DEV, tr, cols), F32)] + [((tr, cols), F32)] * 7, sem=("arbitrary",)),
    )(gathered, w, m, v)


def kernel(x, norm_mix_g, w_in, gmlp_ln_g, gmlp_ln_b, gmlp_w_s, gmlp_b_s, hgrn_lb_table, hgrn_norm_g, w_branch_a, w_branch_b, w_out, norm_ffn_g, w_gate_up, w_down, norm_final_g, loss_target, m_norm_mix_g, m_w_in, m_gmlp_ln_g, m_gmlp_ln_b, m_gmlp_w_s, m_gmlp_b_s, m_hgrn_lb_table, m_hgrn_norm_g, m_w_branch_a, m_w_branch_b, m_w_out, m_norm_ffn_g, m_w_gate_up, m_w_down, m_norm_final_g, v_norm_mix_g, v_w_in, v_gmlp_ln_g, v_gmlp_ln_b, v_gmlp_w_s, v_gmlp_b_s, v_hgrn_lb_table, v_hgrn_norm_g, v_w_branch_a, v_w_branch_b, v_w_out, v_norm_ffn_g, v_w_gate_up, v_w_down, v_norm_final_g):
    t = x.shape[1]
    x2d = x.reshape(t, D_MODEL)
    target = loss_target.reshape(t, D_MODEL)
    final_g = norm_final_g.reshape(1, D_MODEL)

    shards = [w_in[0].astype(BF16), w_branch_a[0].astype(BF16), w_branch_b[0].astype(BF16), w_out[0].astype(BF16),
              w_gate_up[0].astype(BF16), w_down[0].astype(BF16)]

    def rows_of(n):
        return lambda ref, j: ref.at[pl.ds(pl.multiple_of(j * n, 8), n)]

    gathered = [((N_DEV, D_MODEL, D_MODEL), BF16), ((D_MODEL, D_MODEL), BF16), ((D_MODEL, D_MODEL), BF16),
                ((D_MODEL, D_MODEL), BF16), ((N_DEV, D_MODEL, FF_BLOCK), BF16), ((D_FF, D_MODEL), BF16)]
    places = [lambda ref, j: ref.at[_pos_of_dev(j)], rows_of(BRANCH_ROWS), rows_of(BRANCH_ROWS), rows_of(BRANCH_ROWS),
              lambda ref, j: ref.at[j], rows_of(DOWN_ROWS)]
    (w_in_g,) = _all_gather_async("w_in_all_gather", 9, shards[:1], gathered[:1], places[:1])
    _, later = lax.optimization_barrier((w_in_g, shards[1:]))
    w_a, w_b, w_o, w_gu, w_dn = _all_gather_async("weights_all_gather", 0, later, gathered[1:], places[1:])

    core_i, chip_i = lax.axis_index("c"), 2 * lax.axis_index("x") + lax.axis_index("y")
    own_pos = jnp.stack([_pos_of_dev(2 * chip_i + core_i), _pos_of_dev(2 * chip_i + 1 - core_i)]).astype(jnp.int32)
    other_pos = jnp.stack([_pos_of_dev(2 * jnp.bitwise_xor(chip_i, q) + cc) for q in (1, 2, 3) for cc in (0, 1)]).astype(jnp.int32)
    w_in_sibling = _swap_with_sibling("w_in_from_sibling", shards[0])
    proj, h, h_t = _proj_forward_own_chip(own_pos, x2d, norm_mix_g, shards[0], w_in_sibling)
    proj = _proj_forward_other_chips(other_pos, proj, h, w_in_g)
    bias_b = jnp.broadcast_to(gmlp_b_s[0][:, :, None], (GROUPS, GMLP_CHUNK, GMLP_CHUNK))
    a = _gmlp_forward(proj, gmlp_ln_g, gmlp_ln_b, gmlp_w_s[0], bias_b)
    og, o_saved, states = _hgrn_forward(proj, hgrn_lb_table, hgrn_norm_g)
    ya, yb, merged, x1, h2, h2_t = _branch_out_forward(a, og, proj, x2d, w_a, w_b, w_o, norm_ffn_g)
    gu, act, loss_tile, d_final_g, dx2, dx2b = _ffn_forward(h2, x1, w_gu, w_dn, target, final_g)

    core = lax.axis_index("c").astype(jnp.int32).reshape(1)
    chip = (2 * lax.axis_index("x") + lax.axis_index("y")).astype(jnp.int32).reshape(1)
    branch_rows, branch_shape = rows_of(BRANCH_ROWS), (BRANCH_ROWS, D_MODEL)
    branch_block = ((BRANCH_ROWS, D_MODEL), lambda q, r, c: (2 * q + c, 0))

    def chip_partials(names, grads, land, own_blocks):
        return [_chip_partial("chip_partial_" + nme, core, g_, blk, idx, l_)
                for nme, g_, (blk, idx), l_ in zip(names, grads, own_blocks, land)]

    whole = pl.BlockSpec((t, D_MODEL), lambda n: (0, 0))
    whole_t = pl.BlockSpec((D_MODEL, t), lambda n: (0, 0))
    col_blocks = [((t, D_MODEL), BF16), ((t, 256), BF16), ((D_MODEL, 256), F32)]

    def square_grad(name, a_, b_):
        return _weight_grad(name, a_, b_, whole, pl.BlockSpec((t, 256), lambda n: (0, n)), (D_MODEL, D_MODEL),
                            pl.BlockSpec((D_MODEL, 256), lambda n: (0, n)), D_MODEL // 256, col_blocks, False)

    dgu, dx1, dx1b, d_ffn_g = _ffn_backward(dx2b, dx2, gu, x1, w_gu, w_dn, norm_ffn_g)
    g_gu = _weight_grad(
        "grad_w_gate_up", h2_t, dgu, whole_t, pl.BlockSpec((None, None, t, FF_BLOCK), lambda j: (j % 4, j // 4, 0, 0)),
        (N_DEV, D_MODEL, FF_BLOCK), pl.BlockSpec((None, D_MODEL, FF_BLOCK), lambda j: (j, 0, 0)), N_DEV,
        [((D_MODEL, t), BF16), ((t, 768), BF16), ((D_MODEL, 768), F32)], True)
    g_dn = _weight_grad(
        "grad_w_down", act, dx2b, pl.BlockSpec((None, t, FF_BLOCK), lambda j: (j, 0, 0)), whole, (D_FF, D_MODEL),
        pl.BlockSpec((FF_BLOCK, D_MODEL), lambda j: (j, 0)), 4,
        [((t, 768), BF16), ((t, D_MODEL), BF16), ((FF_BLOCK, D_MODEL), F32)], False)
    names_f, grads_f = ["w_gate_up", "w_down"], [g_gu, g_dn]
    land_f = _exchange_sibling("ffn_grads_to_sibling", 2, grads_f, [lambda ref, j: ref.at[j], rows_of(DOWN_ROWS)],
                               [(D_MODEL, FF_BLOCK), (DOWN_ROWS, D_MODEL)])

    dya, dyb, dproj, da, dog = _branch_out_backward(dx1b, ya, yb, proj, w_a, w_b, w_o)
    g_a = square_grad("grad_w_a", a, dya)
    g_b = square_grad("grad_w_b", og, dyb)
    g_o = square_grad("grad_w_out", merged, dx1b)
    names_b, grads_b = ["w_branch_a", "w_branch_b", "w_out"], [g_a, g_b, g_o]
    land_b = _exchange_sibling("branch_grads_to_sibling", 3, grads_b, [branch_rows] * 3, [branch_shape] * 3)

    part_f = chip_partials(names_f, grads_f, land_f,
                           [((None, 256, FF_BLOCK), lambda q, r, c: (2 * q + c, r, 0)),
                            ((DOWN_ROWS // 2, D_MODEL), lambda q, r, c: (2 * (2 * q + c) + r, 0))])
    landed_f = _exchange_chips("ffn_grads_to_chips", 5, part_f)

    dog, _ = lax.optimization_barrier((dog, part_f))
    dproj, d_hg_norm, d_lb = _hgrn_backward(dproj, dog, o_saved, states, proj, hgrn_lb_table, hgrn_norm_g)

    part_b = chip_partials(names_b, grads_b, land_b, [branch_block] * 3)
    landed_b = _exchange_chips("branch_grads_to_chips", 6, part_b)

    da, _ = lax.optimization_barrier((da, part_b))
    dproj, d_ln_g, d_ln_b, d_ws, d_bs = _gmlp_backward(dproj, da, proj, gmlp_ln_g, gmlp_ln_b, gmlp_w_s[0], bias_b)

    def packed(vals):
        return _pack_small(*vals)

    def flat_ws(a):
        return a.reshape(GROUPS * GMLP_CHUNK, GMLP_CHUNK)

    no_row = jnp.zeros((1, D_MODEL), F32)
    w_pack = packed([norm_mix_g, gmlp_ln_g, gmlp_ln_b, gmlp_b_s, hgrn_lb_table, hgrn_norm_g, norm_ffn_g, norm_final_g, no_row])
    m_pack = packed([m_norm_mix_g, m_gmlp_ln_g, m_gmlp_ln_b, m_gmlp_b_s, m_hgrn_lb_table, m_hgrn_norm_g, m_norm_ffn_g, m_norm_final_g, no_row])
    v_pack = packed([v_norm_mix_g, v_gmlp_ln_g, v_gmlp_ln_b, v_gmlp_b_s, v_hgrn_lb_table, v_hgrn_norm_g, v_norm_ffn_g, v_norm_final_g, no_row])
    small_partial = _pack_small(no_row, d_ln_g, d_ln_b, d_bs[:, :, 0], d_lb, d_hg_norm, d_ffn_g, d_final_g,
                                jnp.tile(loss_tile[0:1], (1, D_MODEL // 128)))
    small_all, ws_all = _all_gather_async(
        "small_grads_all_gather", 1, [small_partial, flat_ws(d_ws)],
        [((N_DEV, SMALL_ROWS, D_MODEL), F32), ((N_DEV, GROUPS * GMLP_CHUNK, GMLP_CHUNK), F32)],
        [lambda ref, j: ref.at[j], lambda ref, j: ref.at[j]])

    g_in = _weight_grad(
        "grad_w_in", h_t, dproj, whole_t, pl.BlockSpec((None, t, D_MODEL), lambda p: (p, 0, 0)), (N_DEV, D_MODEL, D_MODEL),
        pl.BlockSpec((None, D_MODEL, D_MODEL), lambda p: (p, 0, 0)), N_DEV,
        [((D_MODEL, t), BF16), ((t, D_MODEL), BF16), ((D_MODEL, D_MODEL), F32)], True)
    land_i = _exchange_sibling("w_in_grads_to_sibling", 4, [g_in], [lambda ref, j: ref.at[_pos_of_dev(j)]],
                               [(D_MODEL, D_MODEL)])

    big = {}
    for nme, own, lnd, w, m, v in zip(
            names_f + names_b, part_f + part_b, landed_f + landed_b,
            [w_gate_up, w_down, w_branch_a, w_branch_b, w_out], [m_w_gate_up, m_w_down, m_w_branch_a, m_w_branch_b, m_w_out],
            [v_w_gate_up, v_w_down, v_w_branch_a, v_w_branch_b, v_w_out]):
        big[nme] = [o_[None] for o_ in _adamw("adamw_" + nme, chip, own, lnd, w[0], m[0], v[0])]
    small_outs = _adamw_small("adamw_small", small_all, w_pack, m_pack, v_pack)
    ws_outs = _adamw_small("adamw_w_s", ws_all, flat_ws(gmlp_w_s), flat_ws(m_gmlp_w_s), flat_ws(v_gmlp_w_s))
    land_i, _ = lax.optimization_barrier((land_i, (big, small_outs, ws_outs)))
    part_i = chip_partials(["w_in"], [g_in], land_i,
                           [((None, 256, D_MODEL), lambda q, r, c: (_pos_of_dev(2 * q + c), r, 0))])
    landed_i = _exchange_chips("w_in_grads_to_chips", 7, part_i)

    dx1, _ = lax.optimization_barrier((dx1, part_i))
    grad_x, d_mix_g = _input_backward(dproj, w_in_g, x2d, dx1, norm_mix_g)
    big["w_in"] = [o_[None] for o_ in _adamw("adamw_w_in", chip, part_i[0], landed_i[0], w_in[0], m_w_in[0], v_w_in[0])]

    def row8(a):
        return jnp.pad(a, ((0, 7), (0, 0)))

    d_mix_g, _ = lax.optimization_barrier((d_mix_g, landed_i))
    (mix_all,) = _all_gather_async("mix_gain_grad_all_gather", 8, [row8(d_mix_g)], [((N_DEV, 8, D_MODEL), F32)],
                                   [lambda ref, j: ref.at[j]])
    mix_outs = _adamw_small("adamw_mix_gain", mix_all, row8(norm_mix_g), row8(m_norm_mix_g), row8(v_norm_mix_g))
    small = [dict(_unpack_small(p, ws), norm_mix_g=q[0:1]) for p, ws, q in zip(small_outs, ws_outs, mix_outs)]

    loss = small_outs[0][SMALL_ROWS - 8, 0]
    order = ["norm_mix_g", "w_in", "gmlp_ln_g", "gmlp_ln_b", "gmlp_w_s", "gmlp_b_s", "hgrn_lb_table", "hgrn_norm_g",
             "w_branch_a", "w_branch_b", "w_out", "norm_ffn_g", "w_gate_up", "w_down", "norm_final_g"]
    outs = [loss, grad_x.reshape(1, t, D_MODEL)]
    for kind in range(4):
        for nme in order:
            outs.append(big[nme][kind] if nme in big else small[kind][nme])
    return tuple(outs)
```

```python
import functools

import jax
import jax.numpy as jnp
from jax import lax
from jax.experimental import pallas as pl
from jax.experimental.pallas import tpu as pltpu
from jax.experimental.pallas import tpu_sc as plsc

F32, BF16 = jnp.float32, jnp.bfloat16
D_MODEL = 1024
N_DEV = 8
HEADS = 8
HEAD_DIM = 128
GROUPS = 8
GMLP_CHUNK = 128
HGRN_CHUNK = 64
HGRN_SCALE = HEAD_DIM ** -0.5
D_FF = 2816
FF_BLOCK = D_FF // 4
DOWN_ROWS = D_FF // N_DEV
BRANCH_ROWS = D_MODEL // N_DEV
NORM_EPS = 1e-6
ADAM_LR, ADAM_B1, ADAM_B2, ADAM_EPS, ADAM_WD, ADAM_STEP = 0.001, 0.9, 0.999, 1e-08, 0.01, 10
SMALL_ROWS = 72
V7X_VMEM_BYTES = 64 * 1024 * 1024
VMEM_CAP = V7X_VMEM_BYTES - 6 * 1024 * 1024
MESH_ID = pl.DeviceIdType.MESH
ANY = pl.BlockSpec(memory_space=pl.ANY)
RESIDENT = pl.BlockSpec(memory_space=pltpu.VMEM)
Q_POS, U_POS, GATE_POS = 0, 4, 6


def _pos_of_dev(j):
    return jnp.where(j < 2, j + 4, jnp.where(j < 6, j - 2, j))


def _dev_of_pos(p):
    return jnp.where(p < 4, p + 2, jnp.where(p < 6, p - 4, p))


def _nbytes(shape, dtype):
    n = 1
    for s in shape:
        n *= s
    return n * jnp.dtype(dtype).itemsize


def _params(blocks, scratch=(), temps=0, sem=None):
    need = 2 * sum(_nbytes(s, d) for s, d in blocks) + sum(_nbytes(s, d) for s, d in scratch) + temps
    assert need + (4 << 20) <= VMEM_CAP, need
    return pltpu.CompilerParams(dimension_semantics=sem, vmem_limit_bytes=VMEM_CAP)


def _tile(n, pref):
    return pref if n % pref == 0 else n


def _dot(a, b):
    return jnp.dot(a, b, preferred_element_type=F32)


def _dot_nt(a, b):
    return lax.dot_general(a, b, (((1,), (1,)), ((), ())), preferred_element_type=F32)


def _dot_tn(a, b):
    return lax.dot_general(a, b, (((0,), (0,)), ((), ())), preferred_element_type=F32)


def _sigmoid(x):
    return 1.0 / (1.0 + jnp.exp(-x))


_GELU_C = 0.7978845608028654


def _gelu(x):
    return x * (0.5 * (1.0 + jnp.tanh(_GELU_C * (x + 0.044715 * (x * x * x)))))


def _gelu_grad(x):
    t = jnp.tanh(_GELU_C * (x + 0.044715 * (x * x * x)))
    return 0.5 * (1.0 + t) + 0.5 * x * (1.0 - t * t) * (_GELU_C * (1.0 + 3.0 * 0.044715 * x * x))


def _rms_stats(x):
    r = lax.rsqrt(jnp.mean(x * x, axis=-1, keepdims=True) + NORM_EPS)
    return r, x * r


def _rms_bwd(dy, x, g):
    r, xh = _rms_stats(x)
    dg = jnp.sum(dy * xh, axis=0, keepdims=True)
    dxh = dy * g
    dx = r * (dxh - xh * jnp.mean(dxh * xh, axis=-1, keepdims=True))
    return dx, dg


def _split3(x):
    hi = x.astype(BF16)
    r = x - hi.astype(F32)
    mid = r.astype(BF16)
    lo = (r - mid.astype(F32)).astype(BF16)
    return hi, mid, lo


def _mask_mm(mask_bf16, x):
    hi, mid, lo = _split3(x)
    return _dot(mask_bf16, hi) + _dot(mask_bf16, mid) + _dot(mask_bf16, lo)


def _place():
    return lax.axis_index("x"), lax.axis_index("y"), lax.axis_index("c")


def _gather_copies(src, out, send, recv, loc, slicers):
    n = len(src)
    x, y, c = _place()
    me, sib = (x, y, c), (x, y, 1 - c)
    chips = [(1 - x, y), (x, 1 - y), (1 - x, 1 - y)]

    def dev(p):
        return 4 * p[0] + 2 * p[1] + p[2]

    def rc(i, k, block, to, from_src=False):
        dst = slicers[i](out[i], dev(block))
        return pltpu.make_async_remote_copy(
            src_ref=src[i] if from_src else dst, dst_ref=dst, send_sem=send.at[7 * i + k],
            recv_sem=recv.at[7 * i + k], device_id=to, device_id_type=MESH_ID)

    mine = [pltpu.make_async_copy(src[i], slicers[i](out[i], dev(me)), loc.at[i]) for i in range(n)]
    for cp in mine:
        cp.start()
    first = []
    for i in range(n):
        first.append(rc(i, 0, me, sib, True))
        for j, chip in enumerate(chips):
            first.append(rc(i, 1 + j, me, (*chip, c), True))
    for cp in first:
        cp.start()
    passed = []
    for j, chip in enumerate(chips):
        for i in range(n):
            rc(i, 1 + j, (*chip, c), me).wait_recv()
            cp = rc(i, 4 + j, (*chip, c), sib)
            cp.start()
            passed.append(cp)
    for i in range(n):
        rc(i, 0, sib, me).wait_recv()
        for j, chip in enumerate(chips):
            rc(i, 4 + j, (*chip, 1 - c), me).wait_recv()
    for cp in first + passed:
        cp.wait_send()
    for cp in mine:
        cp.wait()


def _gather_scratch(n):
    return [pltpu.SemaphoreType.DMA((7 * n,)), pltpu.SemaphoreType.DMA((7 * n,)), pltpu.SemaphoreType.DMA((n,))]


def _handshake(peers):
    barrier = pltpu.get_barrier_semaphore()
    for peer in peers:
        pl.semaphore_signal(barrier, inc=1, device_id=peer, device_id_type=MESH_ID)
    pl.semaphore_wait(barrier, len(peers))


def _all_gather_async(name, collective_id, srcs, out_shapes, slicers):
    n = len(srcs)

    def body(*refs):
        x, y, c = _place()
        _handshake([(1 - x if dx else x, 1 - y if dy else y, 1 - c if dc else c)
                    for dx in (0, 1) for dy in (0, 1) for dc in (0, 1) if dx or dy or dc])
        _gather_copies(refs[:n], refs[n:2 * n], *refs[2 * n:], slicers)

    return _sequencer_call(name, collective_id, body, srcs, [jax.ShapeDtypeStruct(s, d) for s, d in out_shapes],
                           _gather_scratch(n))


def _sequencer_call(name, collective_id, body, operands, out_types, scratch):
    return pl.kernel(
        body, out_type=out_types, mesh=plsc.ScalarSubcoreMesh(axis_name="sequencer", num_cores=1), name=name,
        scratch_types=scratch, compiler_params=pltpu.CompilerParams(collective_id=collective_id),
    )(*operands)


def _exchange_sibling(name, collective_id, grads, shard_fns, shard_shapes):
    n = len(grads)

    def body(*refs):
        g, land = refs[:n], refs[n:2 * n]
        send, recv = refs[2 * n:]
        x, y, c = _place()
        _handshake([(x, y, 1 - c)])
        remote = []
        for i in range(n):
            for q in range(4):
                cp = pltpu.make_async_remote_copy(
                    src_ref=shard_fns[i](g[i], 2 * q + (1 - c)), dst_ref=land[i].at[q], send_sem=send.at[4 * i + q],
                    recv_sem=recv.at[4 * i + q], device_id=(x, y, 1 - c), device_id_type=MESH_ID)
                cp.start()
                remote.append(cp)
        for cp in remote:
            cp.wait()

    return _sequencer_call(name, collective_id, body, grads, [jax.ShapeDtypeStruct((4, *s), F32) for s in shard_shapes],
                           [pltpu.SemaphoreType.DMA((4 * n,)), pltpu.SemaphoreType.DMA((4 * n,))])


def _exchange_chips(name, collective_id, parts):
    n = len(parts)

    def body(*refs):
        part, out = refs[:n], refs[n:2 * n]
        send, recv = refs[2 * n:]
        x, y, c = _place()
        _handshake([(1 - x, y, c), (x, 1 - y, c), (1 - x, 1 - y, c)])
        remote = []
        for i in range(n):
            for s in range(3):
                qx = 1 - x if (s + 1) // 2 else x
                qy = 1 - y if (s + 1) % 2 else y
                cp = pltpu.make_async_remote_copy(
                    src_ref=part[i].at[2 * qx + qy], dst_ref=out[i].at[s], send_sem=send.at[3 * i + s],
                    recv_sem=recv.at[3 * i + s], device_id=(qx, qy, c), device_id_type=MESH_ID)
                cp.start()
                remote.append(cp)
        for cp in remote:
            cp.wait()

    return _sequencer_call(name, collective_id, body, parts,
                           [jax.ShapeDtypeStruct((3, *p.shape[1:]), p.dtype) for p in parts],
                           [pltpu.SemaphoreType.DMA((3 * n,)), pltpu.SemaphoreType.DMA((3 * n,))])


def _chip_partial(name, core, grad, own_block, own_index, land):
    _, rows, cols = land.shape
    tr = own_block[-2]

    def body(core_ref, a_ref, b_ref, o_ref):
        o_ref[...] = (a_ref[...] + b_ref[...]).astype(BF16)

    spec = pl.BlockSpec((None, tr, cols), lambda q, r, c: (q, r, 0))
    return pl.pallas_call(
        body, name=name, out_shape=jax.ShapeDtypeStruct(land.shape, BF16),
        grid_spec=pltpu.PrefetchScalarGridSpec(
            num_scalar_prefetch=1, grid=(4, rows // tr),
            in_specs=[pl.BlockSpec(own_block, lambda q, r, c: own_index(q, r, c[0])), spec], out_specs=spec),
        compiler_params=_params([((tr, cols), F32)] * 2 + [((tr, cols), BF16)], sem=("arbitrary", "arbitrary")),
    )(core, grad, land)


def _adamw_math(w, g, m, v):
    m = ADAM_B1 * m + (1.0 - ADAM_B1) * g
    v = ADAM_B2 * v + (1.0 - ADAM_B2) * (g * g)
    m_hat = m / (1.0 - ADAM_B1 ** ADAM_STEP)
    v_hat = v / (1.0 - ADAM_B2 ** ADAM_STEP)
    delta = -ADAM_LR * (m_hat / (jnp.sqrt(v_hat) + ADAM_EPS) + ADAM_WD * w)
    return delta, m, v


def _adamw(name, chip, own, landed, w, m, v):
    _, rows, cols = own.shape
    tr = _tile(rows, 256) if rows % 256 == 0 else _tile(rows, 176)

    def body(chip_ref, own_ref, l_ref, w_ref, m_ref, v_ref, g_out, d_out, m_out, v_out):
        g = own_ref[...].astype(F32)
        for s in range(3):
            g = g + l_ref[s].astype(F32)
        delta, m_new, v_new = _adamw_math(w_ref[...], g, m_ref[...], v_ref[...])
        g_out[...] = g
        d_out[...] = delta
        m_out[...] = m_new
        v_out[...] = v_new

    spec = pl.BlockSpec((tr, cols), lambda r, c: (r, 0))
    return pl.pallas_call(
        body, name=name, out_shape=[jax.ShapeDtypeStruct((rows, cols), F32)] * 4,
        grid_spec=pltpu.PrefetchScalarGridSpec(
            num_scalar_prefetch=1, grid=(rows // tr,),
            in_specs=[pl.BlockSpec((None, tr, cols), lambda r, c: (c[0], r, 0)),
                      pl.BlockSpec((3, tr, cols), lambda r, c: (0, r, 0)), spec, spec, spec],
            out_specs=[spec] * 4),
        compiler_params=_params([((4, tr, cols), own.dtype)] + [((tr, cols), F32)] * 7, sem=("arbitrary",)),
    )(chip, own, landed, w, m, v)


def _swap_with_sibling(name, x):
    def body(x_ref, o_ref, send, recv):
        px, py, c = _place()
        cp = pltpu.make_async_remote_copy(src_ref=x_ref, dst_ref=o_ref, send_sem=send, recv_sem=recv,
                                          device_id=(px, py, 1 - c), device_id_type=MESH_ID)
        cp.start()
        cp.wait()

    return pl.pallas_call(
        body, name=name, out_shape=jax.ShapeDtypeStruct(x.shape, x.dtype), in_specs=[ANY], out_specs=ANY,
        scratch_shapes=[pltpu.SemaphoreType.DMA, pltpu.SemaphoreType.DMA],
    )(x)


def _proj_forward_own_chip(positions, x, gain, w_own, w_sibling):
    t = x.shape[0]
    tm = _tile(t, 1024)

    def body(pos_ref, x_ref, g_ref, wo_ref, ws_ref, o_ref, h_ref, ht_ref):
        @pl.when(pl.program_id(1) == 0)
        def _():
            _, xh = _rms_stats(x_ref[...])
            h = (xh * g_ref[...]).astype(BF16)
            h_ref[...] = h
            ht_ref[...] = h.T
            o_ref[...] = _dot(h, wo_ref[...])

        @pl.when(pl.program_id(1) == 1)
        def _():
            o_ref[...] = _dot(h_ref[...], ws_ref[...])

    tok = pl.BlockSpec((tm, D_MODEL), lambda m, k, pos: (m, 0))
    return pl.pallas_call(
        body, name="proj_fwd_own_chip",
        out_shape=[jax.ShapeDtypeStruct((N_DEV, t, D_MODEL), F32), jax.ShapeDtypeStruct((t, D_MODEL), BF16),
                   jax.ShapeDtypeStruct((D_MODEL, t), BF16)],
        grid_spec=pltpu.PrefetchScalarGridSpec(
            num_scalar_prefetch=1, grid=(t // tm, 2),
            in_specs=[tok, pl.BlockSpec((1, D_MODEL), lambda m, k, pos: (0, 0)), RESIDENT, RESIDENT],
            out_specs=[pl.BlockSpec((None, tm, D_MODEL), lambda m, k, pos: (pos[k], m, 0)), tok,
                       pl.BlockSpec((D_MODEL, tm), lambda m, k, pos: (0, m))]),
        compiler_params=_params([((tm, D_MODEL), F32)] * 2 + [((tm, D_MODEL), BF16)] * 2,
                                scratch=[((2, D_MODEL, D_MODEL), BF16)], temps=6 << 20, sem=("arbitrary", "arbitrary")),
    )(positions, x, gain, w_own, w_sibling)


def _proj_forward_other_chips(positions, proj, h, w_in_g):
    t = h.shape[0]
    tm = _tile(t, 1024)

    def body(pos_ref, _, h_ref, w_ref, o_ref):
        o_ref[...] = _dot(h_ref[...], w_ref[pos_ref[pl.program_id(1)]])

    return pl.pallas_call(
        body, name="proj_fwd_other_chips", out_shape=jax.ShapeDtypeStruct(proj.shape, F32),
        grid_spec=pltpu.PrefetchScalarGridSpec(
            num_scalar_prefetch=1, grid=(t // tm, N_DEV - 2),
            in_specs=[ANY, pl.BlockSpec((tm, D_MODEL), lambda m, k, pos: (m, 0)), RESIDENT],
            out_specs=pl.BlockSpec((None, tm, D_MODEL), lambda m, k, pos: (pos[k], m, 0))),
        input_output_aliases={1: 0},
        compiler_params=_params([((tm, D_MODEL), F32), ((tm, D_MODEL), BF16)], scratch=[((N_DEV, D_MODEL, D_MODEL), BF16)],
                                temps=6 << 20, sem=("arbitrary", "arbitrary")),
    )(positions, proj, h, w_in_g)


def _masked_ws(ws_ref, g):
    row = lax.broadcasted_iota(jnp.int32, (GMLP_CHUNK, GMLP_CHUNK), 0)
    col = lax.broadcasted_iota(jnp.int32, (GMLP_CHUNK, GMLP_CHUNK), 1)
    return jnp.where(row >= col, ws_ref[g], 0.0).astype(BF16)


def _gmlp_forward(proj, ln_g, ln_b, w_s, bias_b):
    t = proj.shape[1]
    tm = _tile(t, 256)
    chunks = tm // GMLP_CHUNK

    def body(u_ref, v_ref, lng_ref, lnb_ref, ws_ref, bias_ref, a_ref, vn_scr):
        vv = _gelu(v_ref[...])
        mu = jnp.mean(vv, axis=-1, keepdims=True)
        cen = vv - mu
        var = jnp.mean(cen * cen, axis=-1, keepdims=True)
        vn_scr[...] = ((cen * lax.rsqrt(var + NORM_EPS)) * lng_ref[...] + lnb_ref[...]).astype(BF16)
        for g in range(GROUPS):
            wm = _masked_ws(ws_ref, g)
            cols = slice(g * HEAD_DIM, (g + 1) * HEAD_DIM)
            for c in range(chunks):
                rows = slice(c * GMLP_CHUNK, (c + 1) * GMLP_CHUNK)
                mixed = _dot(wm, vn_scr[rows, cols]) + bias_ref[g]
                a_ref[rows, cols] = (_gelu(u_ref[rows, cols]) * mixed).astype(BF16)

    small = pl.BlockSpec((GROUPS, GMLP_CHUNK, GMLP_CHUNK), lambda m: (0, 0, 0))
    vec = pl.BlockSpec((1, D_MODEL), lambda m: (0, 0))
    return pl.pallas_call(
        body, name="gmlp_fwd", out_shape=jax.ShapeDtypeStruct((t, D_MODEL), BF16), grid=(t // tm,),
        in_specs=[pl.BlockSpec((None, tm, D_MODEL), lambda m: (U_POS, m, 0)),
                  pl.BlockSpec((None, tm, D_MODEL), lambda m: (U_POS + 1, m, 0)), vec, vec, small, small],
        out_specs=pl.BlockSpec((tm, D_MODEL), lambda m: (m, 0)),
        scratch_shapes=[pltpu.VMEM((tm, D_MODEL), BF16)],
        compiler_params=_params([((tm, D_MODEL), F32)] * 2 + [((tm, D_MODEL), BF16)] + [((8, 128, 128), F32)] * 2,
                                scratch=[((tm, D_MODEL), BF16)], temps=8 << 20, sem=("arbitrary",)),
    )(proj, proj, ln_g, ln_b, w_s, bias_b)


def _lower_bound(tab_ref):
    t0, t1 = tab_ref[0:1, :], tab_ref[1:2, :]
    mx = jnp.maximum(t0, t1)
    e0, e1 = jnp.exp(t0 - mx), jnp.exp(t1 - mx)
    return e0 / (e0 + e1)


def _tri_masks():
    row = lax.broadcasted_iota(jnp.int32, (HGRN_CHUNK, HGRN_CHUNK), 0)
    col = lax.broadcasted_iota(jnp.int32, (HGRN_CHUNK, HGRN_CHUNK), 1)
    return row >= col, row <= col


def _chunk_rows(c):
    return slice(c * HGRN_CHUNK, (c + 1) * HGRN_CHUNK)


def _per_chunk(x, nc, fn):
    return jnp.concatenate([fn(x[_chunk_rows(c)]) for c in range(nc)], axis=0)


def _chunk_row_bcast(x, nc, i):
    return _per_chunk(x, nc, lambda xc: jnp.broadcast_to(xc[i:i + 1, :], (HGRN_CHUNK, HEAD_DIM)))


def _hgrn_gates(q, fl, lb, nc):
    lower, _ = _tri_masks()
    lower = lower.astype(BF16)
    s = _sigmoid(fl)
    f = lb + (1.0 - lb) * s
    k = 1.0 - f
    hi, mid, lo = _split3(jnp.log(f))
    a = jnp.concatenate([_dot(lower, hi[_chunk_rows(c)]) + _dot(lower, mid[_chunk_rows(c)]) + _dot(lower, lo[_chunk_rows(c)])
                         for c in range(nc)], axis=0)
    a_mid = _chunk_row_bcast(a, nc, HGRN_CHUNK // 2 - 1)
    a_last = _chunk_row_bcast(a, nc, HGRN_CHUNK - 1)
    qs = q * HGRN_SCALE
    e_in, e_out, e_end, e_all = jnp.exp(a - a_mid), jnp.exp(a_mid - a), jnp.exp(a_last - a), jnp.exp(a)
    decay = [jnp.exp(a[c * HGRN_CHUNK + HGRN_CHUNK - 1:(c + 1) * HGRN_CHUNK, :]) for c in range(nc)]
    return dict(s=s, f=f, k=k, decay=decay, e_in=e_in, e_out=e_out, e_end=e_end, e_all=e_all,
                qi=qs * e_in, ki=k * e_out, kd=k * e_end, qe=qs * e_all)


def _hgrn_forward(proj, lb_table, norm_g):
    t = proj.shape[1]
    tb = _tile(t, 1024)
    nc = tb // HGRN_CHUNK
    n_chunks = t // HGRN_CHUNK

    def body(q_ref, f_ref, i_ref, g_ref, tab_ref, ng_ref, og_ref, o_ref, st_ref, state):
        @pl.when(pl.program_id(1) == 0)
        def _():
            state[...] = jnp.zeros_like(state)

        lower, _ = _tri_masks()
        gt = _hgrn_gates(q_ref[...], f_ref[...], _lower_bound(tab_ref), nc)
        qi, ki, kd, qe = (gt[n].astype(BF16) for n in ("qi", "ki", "kd", "qe"))
        vb = i_ref[...].astype(BF16)
        o_intra, d_state = [], []
        for c in range(nc):
            rows = _chunk_rows(c)
            p = jnp.where(lower, _dot_nt(qi[rows], ki[rows]), 0.0).astype(BF16)
            o_intra.append(_dot(p, vb[rows]))
            d_state.append(_dot_tn(vb[rows], kd[rows]))
        st = state[...]
        outs = []
        for c in range(nc):
            st_ref[c] = st
            outs.append(o_intra[c] + _dot_nt(qe[_chunk_rows(c)], st.astype(BF16)))
            st = st * gt["decay"][c] + d_state[c]
        state[...] = st
        o = jnp.concatenate(outs, axis=0)
        o_ref[...] = o
        _, oh = _rms_stats(o)
        gz = g_ref[...]
        og_ref[...] = ((oh * ng_ref[...]) * (gz * _sigmoid(gz))).astype(BF16)

    def blk(p):
        return pl.BlockSpec((None, tb, HEAD_DIM), lambda h, n: (p, n, h))

    out_blk = pl.BlockSpec((tb, HEAD_DIM), lambda h, n: (n, h))
    return pl.pallas_call(
        body, name="hgrn_fwd",
        out_shape=[jax.ShapeDtypeStruct((t, D_MODEL), BF16), jax.ShapeDtypeStruct((t, D_MODEL), F32),
                   jax.ShapeDtypeStruct((HEADS, n_chunks, HEAD_DIM, HEAD_DIM), F32)],
        grid=(HEADS, t // tb),
        in_specs=[blk(Q_POS), blk(Q_POS + 1), blk(Q_POS + 2), blk(Q_POS + 3),
                  pl.BlockSpec((2, HEAD_DIM), lambda h, n: (0, h)), pl.BlockSpec((1, HEAD_DIM), lambda h, n: (0, h))],
        out_specs=[out_blk, out_blk, pl.BlockSpec((None, nc, HEAD_DIM, HEAD_DIM), lambda h, n: (h, n, 0, 0))],
        scratch_shapes=[pltpu.VMEM((HEAD_DIM, HEAD_DIM), F32)],
        compiler_params=_params([((tb, HEAD_DIM), F32)] * 6 + [((nc, HEAD_DIM, HEAD_DIM), F32)], temps=8 << 20,
                                sem=("arbitrary", "arbitrary")),
    )(proj, proj, proj, proj, lb_table, norm_g)


def _branch_out_forward(a, og, proj, x, w_a, w_b, w_out, ffn_g):
    t = x.shape[0]
    tm = _tile(t, 256)

    def body(a_ref, og_ref, ga_ref, gb_ref, x_ref, wa_ref, wb_ref, wo_ref, g_ref, ya_ref, yb_ref, mg_ref, x1_ref, h2_ref,
             h2t_ref):
        ya = _dot(a_ref[...], wa_ref[...])
        yb = _dot(og_ref[...], wb_ref[...])
        ya_ref[...] = ya
        yb_ref[...] = yb
        merged = (_sigmoid(ga_ref[...]) * ya + _sigmoid(gb_ref[...]) * yb).astype(BF16)
        mg_ref[...] = merged
        x1 = x_ref[...] + _dot(merged, wo_ref[...])
        x1_ref[...] = x1
        _, xh = _rms_stats(x1)
        h2 = (xh * g_ref[...]).astype(BF16)
        h2_ref[...] = h2
        h2t_ref[...] = h2.T

    tok = pl.BlockSpec((tm, D_MODEL), lambda m: (m, 0))
    wsp = pl.BlockSpec((D_MODEL, D_MODEL), lambda m: (0, 0))
    return pl.pallas_call(
        body, name="branch_out_fwd",
        out_shape=[jax.ShapeDtypeStruct((t, D_MODEL), F32), jax.ShapeDtypeStruct((t, D_MODEL), F32),
                   jax.ShapeDtypeStruct((t, D_MODEL), BF16), jax.ShapeDtypeStruct((t, D_MODEL), F32),
                   jax.ShapeDtypeStruct((t, D_MODEL), BF16), jax.ShapeDtypeStruct((D_MODEL, t), BF16)],
        grid=(t // tm,),
        in_specs=[tok, tok, pl.BlockSpec((None, tm, D_MODEL), lambda m: (GATE_POS, m, 0)),
                  pl.BlockSpec((None, tm, D_MODEL), lambda m: (GATE_POS + 1, m, 0)), tok, wsp, wsp, wsp,
                  pl.BlockSpec((1, D_MODEL), lambda m: (0, 0))],
        out_specs=[tok] * 5 + [pl.BlockSpec((D_MODEL, tm), lambda m: (0, m))],
        compiler_params=_params([((tm, D_MODEL), BF16)] * 5 + [((tm, D_MODEL), F32)] * 6 + [((D_MODEL, D_MODEL), BF16)] * 3,
                                temps=8 << 20, sem=("arbitrary",)),
    )(a, og, proj, proj, x, w_a, w_b, w_out, ffn_g)


def _ffn_forward(h2, x1, w_gu, w_down, target, final_g):
    t = x1.shape[0]
    tm = _tile(t, 512)

    def body(h_ref, wgu_ref, wd_ref, x1_ref, t_ref, g_ref, gu_ref, act_ref, loss_ref, dg_ref, dx_ref, dxb_ref, acc):
        m, j = pl.program_id(0), pl.program_id(1)

        @pl.when((m == 0) & (j == 0))
        def _():
            loss_ref[...] = jnp.zeros_like(loss_ref)
            dg_ref[...] = jnp.zeros_like(dg_ref)

        h = h_ref[...]
        gate = _dot(h, wgu_ref[j])
        up = _dot(h, wgu_ref[j + 4])
        gu_ref[0] = gate
        gu_ref[1] = up
        act = ((gate * _sigmoid(gate)) * up).astype(BF16)
        act_ref[...] = act
        part = _dot(act, wd_ref[j])

        @pl.when(j == 0)
        def _():
            acc[...] = part

        @pl.when((j > 0) & (j < 3))
        def _():
            acc[...] += part

        @pl.when(j == 3)
        def _():
            x2 = x1_ref[...] + (acc[...] + part)
            g = g_ref[...]
            r, xh = _rms_stats(x2)
            err = xh * g - t_ref[...]
            loss_ref[...] += 0.5 * jnp.sum(jnp.mean(err * err, axis=-1, keepdims=True), axis=0, keepdims=True)
            dy = err * (1.0 / D_MODEL)
            dg_ref[...] += jnp.sum(dy * xh, axis=0, keepdims=True)
            dxh = dy * g
            dx = r * (dxh - xh * jnp.mean(dxh * xh, axis=-1, keepdims=True))
            dx_ref[...] = dx
            dxb_ref[...] = dx.astype(BF16)

    tok = pl.BlockSpec((tm, D_MODEL), lambda m, j: (m, 0))
    vec = pl.BlockSpec((1, D_MODEL), lambda m, j: (0, 0))
    return pl.pallas_call(
        body, name="ffn_fwd",
        out_shape=[jax.ShapeDtypeStruct((4, 2, t, FF_BLOCK), F32), jax.ShapeDtypeStruct((4, t, FF_BLOCK), BF16),
                   jax.ShapeDtypeStruct((8, 128), F32), jax.ShapeDtypeStruct((1, D_MODEL), F32),
                   jax.ShapeDtypeStruct((t, D_MODEL), F32), jax.ShapeDtypeStruct((t, D_MODEL), BF16)],
        grid=(t // tm, 4),
        in_specs=[tok, RESIDENT, RESIDENT, tok, tok, vec],
        out_specs=[pl.BlockSpec((None, 2, tm, FF_BLOCK), lambda m, j: (j, 0, m, 0)),
                   pl.BlockSpec((None, tm, FF_BLOCK), lambda m, j: (j, m, 0)),
                   pl.BlockSpec((8, 128), lambda m, j: (0, 0)), vec, tok, tok],
        scratch_shapes=[pltpu.VMEM((tm, D_MODEL), F32)],
        compiler_params=_params([((tm, D_MODEL), BF16), ((tm, D_MODEL), F32), ((tm, D_MODEL), F32), ((2, tm, 768), F32),
                                 ((tm, 768), BF16), ((tm, D_MODEL), F32), ((tm, D_MODEL), BF16)],
                                scratch=[((tm, D_MODEL), F32), ((N_DEV, D_MODEL, 768), BF16), ((D_FF, D_MODEL), BF16)],
                                temps=6 << 20, sem=("arbitrary", "arbitrary")),
    )(h2, w_gu, w_down.reshape(4, FF_BLOCK, D_MODEL), x1, target, final_g)


def _ffn_backward(dx2b, dx2, gu, x1, w_gu, w_down, ffn_g):
    t = x1.shape[0]
    tm = _tile(t, 512)

    def body(dxb_ref, dx2_ref, gu_ref, x1_ref, wgu_ref, wd_ref, g_ref, dgu_ref, dx1_ref, dx1b_ref, dg_ref, acc, prev):
        m, j = pl.program_id(0), pl.program_id(1)

        @pl.when((m == 0) & (j == 0))
        def _():
            dg_ref[...] = jnp.zeros_like(dg_ref)

        @pl.when(j == 0)
        def _():
            prev[...] = jnp.zeros_like(prev)
            acc[...] = jnp.zeros_like(acc)

        jm1 = jnp.maximum(j - 1, 0)
        acc[...] += _dot_nt(prev[0], wgu_ref[jm1]) + _dot_nt(prev[1], wgu_ref[jm1 + 4])
        dact = _dot_nt(dxb_ref[...], wd_ref[j])
        gate, up = gu_ref[0], gu_ref[1]
        sg = _sigmoid(gate)
        dgate = (dact * up * (sg * (1.0 + gate * (1.0 - sg)))).astype(BF16)
        dup = (dact * (gate * sg)).astype(BF16)
        dgu_ref[0] = dgate
        dgu_ref[1] = dup
        prev[0] = dgate
        prev[1] = dup

        @pl.when(j == 3)
        def _():
            dh2 = acc[...] + (_dot_nt(prev[0], wgu_ref[3]) + _dot_nt(prev[1], wgu_ref[7]))
            dx, dg = _rms_bwd(dh2, x1_ref[...], g_ref[...])
            dx1 = dx2_ref[...] + dx
            dx1_ref[...] = dx1
            dx1b_ref[...] = dx1.astype(BF16)
            dg_ref[...] += dg

    tok = pl.BlockSpec((tm, D_MODEL), lambda m, j: (m, 0))
    vec = pl.BlockSpec((1, D_MODEL), lambda m, j: (0, 0))
    gu_spec = pl.BlockSpec((None, 2, tm, FF_BLOCK), lambda m, j: (j, 0, m, 0))
    return pl.pallas_call(
        body, name="ffn_bwd",
        out_shape=[jax.ShapeDtypeStruct((4, 2, t, FF_BLOCK), BF16), jax.ShapeDtypeStruct((t, D_MODEL), F32),
                   jax.ShapeDtypeStruct((t, D_MODEL), BF16), jax.ShapeDtypeStruct((1, D_MODEL), F32)],
        grid=(t // tm, 4),
        in_specs=[tok, tok, gu_spec, tok, RESIDENT, RESIDENT, vec],
        out_specs=[gu_spec, tok, tok, vec],
        scratch_shapes=[pltpu.VMEM((tm, D_MODEL), F32), pltpu.VMEM((2, tm, FF_BLOCK), BF16)],
        compiler_params=_params([((tm, D_MODEL), BF16), ((tm, D_MODEL), F32), ((2, tm, 768), F32), ((tm, D_MODEL), F32),
                                 ((2, tm, 768), BF16), ((tm, D_MODEL), F32), ((tm, D_MODEL), BF16)],
                                scratch=[((tm, D_MODEL), F32), ((2, tm, 768), BF16), ((N_DEV, D_MODEL, 768), BF16),
                                         ((D_FF, D_MODEL), BF16)],
                                temps=4 << 20, sem=("arbitrary", "arbitrary")),
    )(dx2b, dx2, gu, x1, w_gu, w_down.reshape(4, FF_BLOCK, D_MODEL), ffn_g)


def _branch_out_backward(dx1b, ya, yb, proj, w_a, w_b, w_out):
    t = ya.shape[0]
    tm = _tile(t, 256)

    def body(dx_ref, ya_ref, yb_ref, ga_ref, gb_ref, wa_ref, wb_ref, wo_ref, dya_ref, dyb_ref, dgate_ref, da_ref, dog_ref):
        dm = _dot_nt(dx_ref[...], wo_ref[...])
        sa, sb = _sigmoid(ga_ref[...]), _sigmoid(gb_ref[...])
        dya = (dm * sa).astype(BF16)
        dyb = (dm * sb).astype(BF16)
        dya_ref[...] = dya
        dyb_ref[...] = dyb
        dgate_ref[0] = (dm * ya_ref[...] * (sa * (1.0 - sa))).astype(BF16)
        dgate_ref[1] = (dm * yb_ref[...] * (sb * (1.0 - sb))).astype(BF16)
        da_ref[...] = _dot_nt(dya, wa_ref[...])
        dog_ref[...] = _dot_nt(dyb, wb_ref[...])

    tok = pl.BlockSpec((tm, D_MODEL), lambda m: (m, 0))
    wsp = pl.BlockSpec((D_MODEL, D_MODEL), lambda m: (0, 0))
    return pl.pallas_call(
        body, name="branch_out_bwd",
        out_shape=[jax.ShapeDtypeStruct((t, D_MODEL), BF16), jax.ShapeDtypeStruct((t, D_MODEL), BF16),
                   jax.ShapeDtypeStruct((N_DEV, t, D_MODEL), BF16), jax.ShapeDtypeStruct((t, D_MODEL), F32),
                   jax.ShapeDtypeStruct((t, D_MODEL), F32)],
        grid=(t // tm,),
        in_specs=[tok, tok, tok, pl.BlockSpec((None, tm, D_MODEL), lambda m: (GATE_POS, m, 0)),
                  pl.BlockSpec((None, tm, D_MODEL), lambda m: (GATE_POS + 1, m, 0)), wsp, wsp, wsp],
        out_specs=[tok, tok, pl.BlockSpec((2, tm, D_MODEL), lambda m: (GATE_POS // 2, m, 0)), tok, tok],
        compiler_params=_params([((tm, D_MODEL), BF16)] * 5 + [((tm, D_MODEL), F32)] * 6 + [((D_MODEL, D_MODEL), BF16)] * 3,
                                temps=8 << 20, sem=("arbitrary",)),
    )(dx1b, ya, yb, proj, proj, w_a, w_b, w_out)


def _hgrn_backward(dproj, dog, o_saved, states, proj, lb_table, norm_g):
    t = proj.shape[1]
    tb = _tile(t, 1024)
    nc = tb // HGRN_CHUNK
    nb = t // tb

    def body(_, dog_ref, o_ref, st_ref, q_ref, f_ref, i_ref, g_ref, tab_ref, ng_ref, dp_ref, dng_ref, dtab_ref, gstate):
        @pl.when(pl.program_id(1) == 0)
        def _():
            gstate[...] = jnp.zeros_like(gstate)
            dng_ref[...] = jnp.zeros_like(dng_ref)
            dtab_ref[...] = jnp.zeros_like(dtab_ref)

        lb = _lower_bound(tab_ref)
        ng = ng_ref[...]
        lower, upper = _tri_masks()
        gt = _hgrn_gates(q_ref[...], f_ref[...], lb, nc)
        qi, ki, kd, qe = (gt[n].astype(BF16) for n in ("qi", "ki", "kd", "qe"))
        vb = i_ref[...].astype(BF16)
        o, gz, d_og = o_ref[...], g_ref[...], dog_ref[...]
        r, oh = _rms_stats(o)
        sg = _sigmoid(gz)
        d_on = d_og * (gz * sg)
        dgz = d_og * (oh * ng) * (sg * (1.0 + gz * (1.0 - sg)))
        dng_ref[...] += jnp.sum(d_on * oh, axis=0, keepdims=True)
        doh = d_on * ng
        dob = (r * (doh - oh * jnp.mean(doh * oh, axis=-1, keepdims=True))).astype(BF16)
        dv_intra, dqi, dki, dqe, g_upd = [], [], [], [], []
        for c in range(nc):
            rows = _chunk_rows(c)
            p = jnp.where(lower, _dot_nt(qi[rows], ki[rows]), 0.0).astype(BF16)
            dv_intra.append(_dot_tn(p, dob[rows]))
            dp = jnp.where(lower, _dot_nt(dob[rows], vb[rows]), 0.0).astype(BF16)
            dqi.append(_dot(dp, ki[rows]))
            dki.append(_dot_tn(dp, qi[rows]))
            dqe.append(_dot(dob[rows], st_ref[c].astype(BF16)))
            g_upd.append(_dot_tn(dob[rows], qe[rows]))
        g_after = [None] * nc
        g = gstate[...]
        for c in reversed(range(nc)):
            g_after[c] = g
            g = g * gt["decay"][c] + g_upd[c]
        gstate[...] = g
        dkd, dv, da_last = [], [], []
        for c in range(nc):
            rows = _chunk_rows(c)
            gb = g_after[c].astype(BF16)
            dkd.append(_dot(vb[rows], gb))
            dv.append(dv_intra[c] + _dot_nt(kd[rows], gb))
            da_last.append(jnp.sum(g_after[c] * st_ref[c], axis=0, keepdims=True) * gt["decay"][c])
        dqi, dki, dqe, dkd, dv = (jnp.concatenate(z, axis=0) for z in (dqi, dki, dqe, dkd, dv))
        dqs = dqi * gt["e_in"] + dqe * gt["e_all"]
        dk = dki * gt["e_out"] + dkd * gt["e_end"]
        t_in, t_out, t_end = dqi * gt["qi"], dki * gt["ki"], dkd * gt["kd"]
        da = t_in - t_out + dqe * gt["qe"] - t_end
        row = lax.broadcasted_iota(jnp.int32, (HGRN_CHUNK, HEAD_DIM), 0)
        d_mid = t_out - t_in
        pieces = []
        for c in range(nc):
            rows = _chunk_rows(c)
            da_mid = jnp.sum(d_mid[rows], axis=0, keepdims=True)
            da_end = jnp.sum(t_end[rows], axis=0, keepdims=True) + da_last[c]
            da_c = da[rows] + jnp.where(row == HGRN_CHUNK // 2 - 1, da_mid, 0.0) + jnp.where(row == HGRN_CHUNK - 1, da_end, 0.0)
            pieces.append(_mask_mm(upper.astype(BF16), da_c))
        df = jnp.concatenate(pieces, axis=0) / gt["f"] - dk
        s = gt["s"]
        dlb = jnp.sum(df * (1.0 - s), axis=0, keepdims=True)
        dp_ref[0] = (dqs * HGRN_SCALE).astype(BF16)
        dp_ref[1] = (df * (1.0 - lb) * (s * (1.0 - s))).astype(BF16)
        dp_ref[2] = dv.astype(BF16)
        dp_ref[3] = dgz.astype(BF16)
        dt0 = dlb * (lb * (1.0 - lb))
        dtab_ref[0:1, :] += dt0
        dtab_ref[1:2, :] -= dt0

    def blk(p):
        return pl.BlockSpec((None, tb, HEAD_DIM), lambda h, n: (p, nb - 1 - n, h))

    tok = pl.BlockSpec((tb, HEAD_DIM), lambda h, n: (nb - 1 - n, h))
    return pl.pallas_call(
        body, name="hgrn_bwd",
        out_shape=[jax.ShapeDtypeStruct((N_DEV, t, D_MODEL), BF16), jax.ShapeDtypeStruct((1, D_MODEL), F32),
                   jax.ShapeDtypeStruct((2, D_MODEL), F32)],
        grid=(HEADS, nb),
        in_specs=[ANY, tok, tok, pl.BlockSpec((None, nc, HEAD_DIM, HEAD_DIM), lambda h, n: (h, nb - 1 - n, 0, 0)),
                  blk(Q_POS), blk(Q_POS + 1), blk(Q_POS + 2), blk(Q_POS + 3),
                  pl.BlockSpec((2, HEAD_DIM), lambda h, n: (0, h)), pl.BlockSpec((1, HEAD_DIM), lambda h, n: (0, h))],
        out_specs=[pl.BlockSpec((4, tb, HEAD_DIM), lambda h, n: (0, nb - 1 - n, h)),
                   pl.BlockSpec((1, HEAD_DIM), lambda h, n: (0, h)), pl.BlockSpec((2, HEAD_DIM), lambda h, n: (0, h))],
        scratch_shapes=[pltpu.VMEM((HEAD_DIM, HEAD_DIM), F32)],
        input_output_aliases={0: 0},
        compiler_params=_params([((tb, HEAD_DIM), F32)] * 6 + [((nc, HEAD_DIM, HEAD_DIM), F32)] + [((4, tb, HEAD_DIM), BF16)],
                                temps=8 << 20, sem=("arbitrary", "arbitrary")),
    )(dproj, dog, o_saved, states, proj, proj, proj, proj, lb_table, norm_g)


def _gmlp_backward(dproj, da, proj, ln_g, ln_b, w_s, bias_b):
    t = proj.shape[1]
    tm = _tile(t, 256)
    chunks = tm // GMLP_CHUNK

    def body(_, da_ref, u_ref, v_ref, lng_ref, lnb_ref, ws_ref, bias_ref, dp_ref, dlng_ref, dlnb_ref, dws_ref, dbs_ref,
             vn_scr, dvn_scr):
        @pl.when(pl.program_id(0) == 0)
        def _():
            dlng_ref[...] = jnp.zeros_like(dlng_ref)
            dlnb_ref[...] = jnp.zeros_like(dlnb_ref)
            dws_ref[...] = jnp.zeros_like(dws_ref)
            dbs_ref[...] = jnp.zeros_like(dbs_ref)

        v = v_ref[...]
        vv = _gelu(v)
        mu = jnp.mean(vv, axis=-1, keepdims=True)
        cen = vv - mu
        rstd = lax.rsqrt(jnp.mean(cen * cen, axis=-1, keepdims=True) + NORM_EPS)
        vhat = cen * rstd
        lng = lng_ref[...]
        vn_scr[...] = (vhat * lng + lnb_ref[...]).astype(BF16)
        row = lax.broadcasted_iota(jnp.int32, (GMLP_CHUNK, GMLP_CHUNK), 0)
        col = lax.broadcasted_iota(jnp.int32, (GMLP_CHUNK, GMLP_CHUNK), 1)
        for g in range(GROUPS):
            wm = _masked_ws(ws_ref, g)
            cols = slice(g * HEAD_DIM, (g + 1) * HEAD_DIM)
            dws = jnp.zeros((GMLP_CHUNK, GMLP_CHUNK), F32)
            dbs = jnp.zeros((GMLP_CHUNK, GMLP_CHUNK), F32)
            for c in range(chunks):
                rows = slice(c * GMLP_CHUNK, (c + 1) * GMLP_CHUNK)
                vn = vn_scr[rows, cols]
                mixed = _dot(wm, vn) + bias_ref[g]
                u = u_ref[rows, cols]
                d_a = da_ref[rows, cols]
                dp_ref[0, rows, cols] = (d_a * mixed * _gelu_grad(u)).astype(BF16)
                dmix = d_a * _gelu(u)
                dmb = dmix.astype(BF16)
                dbs = dbs + dmix
                dws = dws + _dot_nt(dmb, vn)
                dvn_scr[rows, cols] = _dot_tn(wm, dmb)
            dws_ref[g] += jnp.where(row >= col, dws, 0.0)
            dbs_ref[g] += jnp.broadcast_to(jnp.sum(dbs, axis=-1, keepdims=True), (GMLP_CHUNK, GMLP_CHUNK))
        dvn = dvn_scr[...]
        dlng_ref[...] += jnp.sum(dvn * vhat, axis=0, keepdims=True)
        dlnb_ref[...] += jnp.sum(dvn, axis=0, keepdims=True)
        dvh = dvn * lng
        dvv = rstd * (dvh - jnp.mean(dvh, axis=-1, keepdims=True) - vhat * jnp.mean(dvh * vhat, axis=-1, keepdims=True))
        dp_ref[1] = (dvv * _gelu_grad(v)).astype(BF16)

    tok = pl.BlockSpec((tm, D_MODEL), lambda m: (m, 0))
    small = pl.BlockSpec((GROUPS, GMLP_CHUNK, GMLP_CHUNK), lambda m: (0, 0, 0))
    vec = pl.BlockSpec((1, D_MODEL), lambda m: (0, 0))
    return pl.pallas_call(
        body, name="gmlp_bwd",
        out_shape=[jax.ShapeDtypeStruct(dproj.shape, BF16), jax.ShapeDtypeStruct((1, D_MODEL), F32),
                   jax.ShapeDtypeStruct((1, D_MODEL), F32), jax.ShapeDtypeStruct((GROUPS, GMLP_CHUNK, GMLP_CHUNK), F32),
                   jax.ShapeDtypeStruct((GROUPS, GMLP_CHUNK, GMLP_CHUNK), F32)],
        grid=(t // tm,),
        in_specs=[ANY, tok, pl.BlockSpec((None, tm, D_MODEL), lambda m: (U_POS, m, 0)),
                  pl.BlockSpec((None, tm, D_MODEL), lambda m: (U_POS + 1, m, 0)), vec, vec, small, small],
        out_specs=[pl.BlockSpec((2, tm, D_MODEL), lambda m: (U_POS // 2, m, 0)), vec, vec, small, small],
        scratch_shapes=[pltpu.VMEM((tm, D_MODEL), BF16), pltpu.VMEM((tm, D_MODEL), F32)],
        input_output_aliases={0: 0},
        compiler_params=_params([((tm, D_MODEL), F32)] * 3 + [((2, tm, D_MODEL), BF16)] + [((8, 128, 128), F32)] * 4,
                                scratch=[((tm, D_MODEL), BF16), ((tm, D_MODEL), F32)], temps=12 << 20, sem=("arbitrary",)),
    )(dproj, da, proj, proj, ln_g, ln_b, w_s, bias_b)


def _input_backward(dproj, w_in_g, x, dx1, mix_g):
    t = x.shape[0]
    tm = _tile(t, 512)

    def body(dp_ref, w_ref, x_ref, dx1_ref, g_ref, dx_ref, dg_ref):
        @pl.when(pl.program_id(0) == 0)
        def _():
            dg_ref[...] = jnp.zeros_like(dg_ref)

        dh = _dot_nt(dp_ref[0], w_ref[0])
        for p in range(1, N_DEV):
            dh = dh + _dot_nt(dp_ref[p], w_ref[p])
        dx, dg = _rms_bwd(dh, x_ref[...], g_ref[...])
        dx_ref[...] = dx1_ref[...] + dx
        dg_ref[...] += dg

    tok = pl.BlockSpec((tm, D_MODEL), lambda m: (m, 0))
    vec = pl.BlockSpec((1, D_MODEL), lambda m: (0, 0))
    return pl.pallas_call(
        body, name="input_bwd",
        out_shape=[jax.ShapeDtypeStruct((t, D_MODEL), F32), jax.ShapeDtypeStruct((1, D_MODEL), F32)],
        grid=(t // tm,),
        in_specs=[pl.BlockSpec((N_DEV, tm, D_MODEL), lambda m: (0, m, 0)), RESIDENT, tok, tok, vec],
        out_specs=[tok, vec],
        compiler_params=_params([((N_DEV, tm, D_MODEL), BF16)] + [((tm, D_MODEL), F32)] * 3,
                                scratch=[((N_DEV, D_MODEL, D_MODEL), BF16)], temps=6 << 20, sem=("arbitrary",)),
    )(dproj, w_in_g, x, dx1, mix_g)


def _weight_grad(name, a, b, a_spec, b_spec, out_shape, out_spec, steps, blocks, a_is_transposed):
    def body(a_ref, b_ref, o_ref):
        o_ref[...] = _dot(a_ref[...], b_ref[...]) if a_is_transposed else _dot_tn(a_ref[...], b_ref[...])

    return pl.pallas_call(
        body, name=name, out_shape=jax.ShapeDtypeStruct(out_shape, F32), grid=(steps,), in_specs=[a_spec, b_spec],
        out_specs=out_spec, compiler_params=_params(blocks, temps=4 << 20, sem=("arbitrary",)),
    )(a, b)


def _pack_small(mix_g, ln_g, ln_b, b_s, lb_table, hg_norm, ffn_g, final_g, loss_row):
    def part(a):
        a = a.reshape(-1, D_MODEL)
        return jnp.pad(a, ((0, 8 - a.shape[0]), (0, 0)))

    return jnp.concatenate([part(mix_g), part(ln_g), part(ln_b), part(hg_norm), part(ffn_g), part(final_g),
                            part(lb_table), part(b_s), part(loss_row)], axis=0)


def _unpack_small(pack, w_s):
    return dict(norm_mix_g=pack[0:1], gmlp_ln_g=pack[8:9], gmlp_ln_b=pack[16:17], hgrn_norm_g=pack[24:25],
                norm_ffn_g=pack[32:33], norm_final_g=pack[40], hgrn_lb_table=pack[48:50],
                gmlp_b_s=pack[56:57].reshape(1, GROUPS, GMLP_CHUNK),
                gmlp_w_s=w_s.reshape(1, GROUPS, GMLP_CHUNK, GMLP_CHUNK))


def _adamw_small(name, gathered, w, m, v):
    rows, cols = w.shape

    def body(p_ref, w_ref, m_ref, v_ref, g_out, d_out, m_out, v_out):
        g = p_ref[0]
        for j in range(1, N_DEV):
            g = g + p_ref[j]
        delta, m_new, v_new = _adamw_math(w_ref[...], g, m_ref[...], v_ref[...])
        g_out[...] = g
        d_out[...] = delta
        m_out[...] = m_new
        v_out[...] = v_new

    tr = _tile(rows, 512)
    spec = pl.BlockSpec((tr, cols), lambda r: (r, 0))
    return pl.pallas_call(
        body, name=name, out_shape=[jax.ShapeDtypeStruct((rows, cols), F32)] * 4, grid=(rows // tr,),
        in_specs=[pl.BlockSpec((N_DEV, tr, cols), lambda r: (0, r, 0)), spec, spec, spec], out_specs=[spec] * 4,
        compiler_params=_params([((N_DEV, tr, cols), F32)] + [((tr, cols), F32)] * 7, sem=("arbitrary",)),
    )(gathered, w, m, v)


def kernel(x, norm_mix_g, w_in, gmlp_ln_g, gmlp_ln_b, gmlp_w_s, gmlp_b_s, hgrn_lb_table, hgrn_norm_g, w_branch_a, w_branch_b, w_out, norm_ffn_g, w_gate_up, w_down, norm_final_g, loss_target, m_norm_mix_g, m_w_in, m_gmlp_ln_g, m_gmlp_ln_b, m_gmlp_w_s, m_gmlp_b_s, m_hgrn_lb_table, m_hgrn_norm_g, m_w_branch_a, m_w_branch_b, m_w_out, m_norm_ffn_g, m_w_gate_up, m_w_down, m_norm_final_g, v_norm_mix_g, v_w_in, v_gmlp_ln_g, v_gmlp_ln_b, v_gmlp_w_s, v_gmlp_b_s, v_hgrn_lb_table, v_hgrn_norm_g, v_w_branch_a, v_w_branch_b, v_w_out, v_norm_ffn_g, v_w_gate_up, v_w_down, v_norm_final_g):
    t = x.shape[1]
    x2d = x.reshape(t, D_MODEL)
    target = loss_target.reshape(t, D_MODEL)
    final_g = norm_final_g.reshape(1, D_MODEL)

    shards = [w_in[0].astype(BF16), w_branch_a[0].astype(BF16), w_branch_b[0].astype(BF16), w_out[0].astype(BF16),
              w_gate_up[0].astype(BF16), w_down[0].astype(BF16)]

    def rows_of(n):
        return lambda ref, j: ref.at[pl.ds(pl.multiple_of(j * n, 8), n)]

    gathered = [((N_DEV, D_MODEL, D_MODEL), BF16), ((D_MODEL, D_MODEL), BF16), ((D_MODEL, D_MODEL), BF16),
                ((D_MODEL, D_MODEL), BF16), ((N_DEV, D_MODEL, FF_BLOCK), BF16), ((D_FF, D_MODEL), BF16)]
    places = [lambda ref, j: ref.at[_pos_of_dev(j)], rows_of(BRANCH_ROWS), rows_of(BRANCH_ROWS), rows_of(BRANCH_ROWS),
              lambda ref, j: ref.at[j], rows_of(DOWN_ROWS)]
    (w_in_g,) = _all_gather_async("w_in_all_gather", 9, shards[:1], gathered[:1], places[:1])
    w_in_sibling = _swap_with_sibling("w_in_from_sibling", shards[0])
    _, later = lax.optimization_barrier((w_in_sibling, shards[1:]))
    w_a, w_b, w_o, w_gu, w_dn = _all_gather_async("weights_all_gather", 0, later, gathered[1:], places[1:])

    core_i, chip_i = lax.axis_index("c"), 2 * lax.axis_index("x") + lax.axis_index("y")
    own_pos = jnp.stack([_pos_of_dev(2 * chip_i + core_i), _pos_of_dev(2 * chip_i + 1 - core_i)]).astype(jnp.int32)
    other_pos = jnp.stack([_pos_of_dev(2 * jnp.bitwise_xor(chip_i, q) + cc) for q in (1, 2, 3) for cc in (0, 1)]).astype(jnp.int32)
    proj, h, h_t = _proj_forward_own_chip(own_pos, x2d, norm_mix_g, shards[0], w_in_sibling)
    proj = _proj_forward_other_chips(other_pos, proj, h, w_in_g)
    bias_b = jnp.broadcast_to(gmlp_b_s[0][:, :, None], (GROUPS, GMLP_CHUNK, GMLP_CHUNK))
    a = _gmlp_forward(proj, gmlp_ln_g, gmlp_ln_b, gmlp_w_s[0], bias_b)
    og, o_saved, states = _hgrn_forward(proj, hgrn_lb_table, hgrn_norm_g)
    ya, yb, merged, x1, h2, h2_t = _branch_out_forward(a, og, proj, x2d, w_a, w_b, w_o, norm_ffn_g)
    gu, act, loss_tile, d_final_g, dx2, dx2b = _ffn_forward(h2, x1, w_gu, w_dn, target, final_g)

    core = lax.axis_index("c").astype(jnp.int32).reshape(1)
    chip = (2 * lax.axis_index("x") + lax.axis_index("y")).astype(jnp.int32).reshape(1)
    branch_rows, branch_shape = rows_of(BRANCH_ROWS), (BRANCH_ROWS, D_MODEL)
    branch_block = ((BRANCH_ROWS, D_MODEL), lambda q, r, c: (2 * q + c, 0))

    def chip_partials(names, grads, land, own_blocks):
        return [_chip_partial("chip_partial_" + nme, core, g_, blk, idx, l_)
                for nme, g_, (blk, idx), l_ in zip(names, grads, own_blocks, land)]

    whole = pl.BlockSpec((t, D_MODEL), lambda n: (0, 0))
    whole_t = pl.BlockSpec((D_MODEL, t), lambda n: (0, 0))
    col_blocks = [((t, D_MODEL), BF16), ((t, 256), BF16), ((D_MODEL, 256), F32)]

    def square_grad(name, a_, b_):
        return _weight_grad(name, a_, b_, whole, pl.BlockSpec((t, 256), lambda n: (0, n)), (D_MODEL, D_MODEL),
                            pl.BlockSpec((D_MODEL, 256), lambda n: (0, n)), D_MODEL // 256, col_blocks, False)

    dgu, dx1, dx1b, d_ffn_g = _ffn_backward(dx2b, dx2, gu, x1, w_gu, w_dn, norm_ffn_g)
    g_gu = _weight_grad(
        "grad_w_gate_up", h2_t, dgu, whole_t, pl.BlockSpec((None, None, t, FF_BLOCK), lambda j: (j % 4, j // 4, 0, 0)),
        (N_DEV, D_MODEL, FF_BLOCK), pl.BlockSpec((None, D_MODEL, FF_BLOCK), lambda j: (j, 0, 0)), N_DEV,
        [((D_MODEL, t), BF16), ((t, 768), BF16), ((D_MODEL, 768), F32)], True)
    g_dn = _weight_grad(
        "grad_w_down", act, dx2b, pl.BlockSpec((None, t, FF_BLOCK), lambda j: (j, 0, 0)), whole, (D_FF, D_MODEL),
        pl.BlockSpec((FF_BLOCK, D_MODEL), lambda j: (j, 0)), 4,
        [((t, 768), BF16), ((t, D_MODEL), BF16), ((FF_BLOCK, D_MODEL), F32)], False)
    names_f, grads_f = ["w_gate_up", "w_down"], [g_gu, g_dn]
    land_f = _exchange_sibling("ffn_grads_to_sibling", 2, grads_f, [lambda ref, j: ref.at[j], rows_of(DOWN_ROWS)],
                               [(D_MODEL, FF_BLOCK), (DOWN_ROWS, D_MODEL)])

    dya, dyb, dproj, da, dog = _branch_out_backward(dx1b, ya, yb, proj, w_a, w_b, w_o)
    g_a = square_grad("grad_w_a", a, dya)
    g_b = square_grad("grad_w_b", og, dyb)
    g_o = square_grad("grad_w_out", merged, dx1b)
    names_b, grads_b = ["w_branch_a", "w_branch_b", "w_out"], [g_a, g_b, g_o]
    land_b = _exchange_sibling("branch_grads_to_sibling", 3, grads_b, [branch_rows] * 3, [branch_shape] * 3)

    part_f = chip_partials(names_f, grads_f, land_f,
                           [((None, 256, FF_BLOCK), lambda q, r, c: (2 * q + c, r, 0)),
                            ((DOWN_ROWS // 2, D_MODEL), lambda q, r, c: (2 * (2 * q + c) + r, 0))])
    landed_f = _exchange_chips("ffn_grads_to_chips", 5, part_f)

    dog, _ = lax.optimization_barrier((dog, part_f))
    dproj, d_hg_norm, d_lb = _hgrn_backward(dproj, dog, o_saved, states, proj, hgrn_lb_table, hgrn_norm_g)

    part_b = chip_partials(names_b, grads_b, land_b, [branch_block] * 3)
    landed_b = _exchange_chips("branch_grads_to_chips", 6, part_b)

    da, _ = lax.optimization_barrier((da, part_b))
    dproj, d_ln_g, d_ln_b, d_ws, d_bs = _gmlp_backward(dproj, da, proj, gmlp_ln_g, gmlp_ln_b, gmlp_w_s[0], bias_b)

    def packed(vals):
        return _pack_small(*vals)

    def flat_ws(a):
        return a.reshape(GROUPS * GMLP_CHUNK, GMLP_CHUNK)

    no_row = jnp.zeros((1, D_MODEL), F32)
    w_pack = packed([norm_mix_g, gmlp_ln_g, gmlp_ln_b, gmlp_b_s, hgrn_lb_table, hgrn_norm_g, norm_ffn_g, norm_final_g, no_row])
    m_pack = packed([m_norm_mix_g, m_gmlp_ln_g, m_gmlp_ln_b, m_gmlp_b_s, m_hgrn_lb_table, m_hgrn_norm_g, m_norm_ffn_g, m_norm_final_g, no_row])
    v_pack = packed([v_norm_mix_g, v_gmlp_ln_g, v_gmlp_ln_b, v_gmlp_b_s, v_hgrn_lb_table, v_hgrn_norm_g, v_norm_ffn_g, v_norm_final_g, no_row])
    small_partial = _pack_small(no_row, d_ln_g, d_ln_b, d_bs[:, :, 0], d_lb, d_hg_norm, d_ffn_g, d_final_g,
                                jnp.tile(loss_tile[0:1], (1, D_MODEL // 128)))
    small_all, ws_all = _all_gather_async(
        "small_grads_all_gather", 1, [small_partial, flat_ws(d_ws)],
        [((N_DEV, SMALL_ROWS, D_MODEL), F32), ((N_DEV, GROUPS * GMLP_CHUNK, GMLP_CHUNK), F32)],
        [lambda ref, j: ref.at[j], lambda ref, j: ref.at[j]])

    g_in = _weight_grad(
        "grad_w_in", h_t, dproj, whole_t, pl.BlockSpec((None, t, D_MODEL), lambda p: (p, 0, 0)), (N_DEV, D_MODEL, D_MODEL),
        pl.BlockSpec((None, D_MODEL, D_MODEL), lambda p: (p, 0, 0)), N_DEV,
        [((D_MODEL, t), BF16), ((t, D_MODEL), BF16), ((D_MODEL, D_MODEL), F32)], True)
    land_i = _exchange_sibling("w_in_grads_to_sibling", 4, [g_in], [lambda ref, j: ref.at[_pos_of_dev(j)]],
                               [(D_MODEL, D_MODEL)])

    big = {}
    for nme, own, lnd, w, m, v in zip(
            names_f + names_b, part_f + part_b, landed_f + landed_b,
            [w_gate_up, w_down, w_branch_a, w_branch_b, w_out], [m_w_gate_up, m_w_down, m_w_branch_a, m_w_branch_b, m_w_out],
            [v_w_gate_up, v_w_down, v_w_branch_a, v_w_branch_b, v_w_out]):
        big[nme] = [o_[None] for o_ in _adamw("adamw_" + nme, chip, own, lnd, w[0], m[0], v[0])]
    small_outs = _adamw_small("adamw_small", small_all, w_pack, m_pack, v_pack)
    ws_outs = _adamw_small("adamw_w_s", ws_all, flat_ws(gmlp_w_s), flat_ws(m_gmlp_w_s), flat_ws(v_gmlp_w_s))
    land_i, _ = lax.optimization_barrier((land_i, (big, small_outs, ws_outs)))
    part_i = chip_partials(["w_in"], [g_in], land_i,
                           [((None, 256, D_MODEL), lambda q, r, c: (_pos_of_dev(2 * q + c), r, 0))])
    landed_i = _exchange_chips("w_in_grads_to_chips", 7, part_i)

    dx1, _ = lax.optimization_barrier((dx1, part_i))
    grad_x, d_mix_g = _input_backward(dproj, w_in_g, x2d, dx1, norm_mix_g)
    big["w_in"] = [o_[None] for o_ in _adamw("adamw_w_in", chip, part_i[0], landed_i[0], w_in[0], m_w_in[0], v_w_in[0])]

    def row8(a):
        return jnp.pad(a, ((0, 7), (0, 0)))

    d_mix_g, _ = lax.optimization_barrier((d_mix_g, landed_i))
    (mix_all,) = _all_gather_async("mix_gain_grad_all_gather", 8, [row8(d_mix_g)], [((N_DEV, 8, D_MODEL), F32)],
                                   [lambda ref, j: ref.at[j]])
    mix_outs = _adamw_small("adamw_mix_gain", mix_all, row8(norm_mix_g), row8(m_norm_mix_g), row8(v_norm_mix_g))
    small = [dict(_unpack_small(p, ws), norm_mix_g=q[0:1]) for p, ws, q in zip(small_outs, ws_outs, mix_outs)]

    loss = small_outs[0][SMALL_ROWS - 8, 0]
    order = ["norm_mix_g", "w_in", "gmlp_ln_g", "gmlp_ln_b", "gmlp_w_s", "gmlp_b_s", "hgrn_lb_table", "hgrn_norm_g",
             "w_branch_a", "w_branch_b", "w_out", "norm_ffn_g", "w_gate_up", "w_down", "norm_final_g"]
    outs = [loss, grad_x.reshape(1, t, D_MODEL)]
    for kind in range(4):
        for nme in order:
            outs.append(big[nme][kind] if nme in big else small[kind][nme])
    return tuple(outs)
```

```python
import functools

import jax
import jax.numpy as jnp
from jax import lax
from jax.experimental import pallas as pl
from jax.experimental.pallas import tpu as pltpu
from jax.experimental.pallas import tpu_sc as plsc

F32, BF16 = jnp.float32, jnp.bfloat16
D_MODEL = 1024
N_DEV = 8
HEADS = 8
HEAD_DIM = 128
GROUPS = 8
GMLP_CHUNK = 128
HGRN_CHUNK = 64
HGRN_SCALE = HEAD_DIM ** -0.5
D_FF = 2816
FF_BLOCK = D_FF // 4
DOWN_ROWS = D_FF // N_DEV
BRANCH_ROWS = D_MODEL // N_DEV
NORM_EPS = 1e-6
ADAM_LR, ADAM_B1, ADAM_B2, ADAM_EPS, ADAM_WD, ADAM_STEP = 0.001, 0.9, 0.999, 1e-08, 0.01, 10
SMALL_ROWS = 72
V7X_VMEM_BYTES = 64 * 1024 * 1024
VMEM_CAP = V7X_VMEM_BYTES - 6 * 1024 * 1024
MESH_ID = pl.DeviceIdType.MESH
ANY = pl.BlockSpec(memory_space=pl.ANY)
RESIDENT = pl.BlockSpec(memory_space=pltpu.VMEM)
Q_POS, U_POS, GATE_POS = 0, 4, 6


def _pos_of_dev(j):
    return jnp.where(j < 2, j + 4, jnp.where(j < 6, j - 2, j))


def _dev_of_pos(p):
    return jnp.where(p < 4, p + 2, jnp.where(p < 6, p - 4, p))


def _nbytes(shape, dtype):
    n = 1
    for s in shape:
        n *= s
    return n * jnp.dtype(dtype).itemsize


def _params(blocks, scratch=(), temps=0, sem=None):
    need = 2 * sum(_nbytes(s, d) for s, d in blocks) + sum(_nbytes(s, d) for s, d in scratch) + temps
    assert need + (4 << 20) <= VMEM_CAP, need
    return pltpu.CompilerParams(dimension_semantics=sem, vmem_limit_bytes=VMEM_CAP)


def _tile(n, pref):
    return pref if n % pref == 0 else n


def _dot(a, b):
    return jnp.dot(a, b, preferred_element_type=F32)


def _dot_nt(a, b):
    return lax.dot_general(a, b, (((1,), (1,)), ((), ())), preferred_element_type=F32)


def _dot_tn(a, b):
    return lax.dot_general(a, b, (((0,), (0,)), ((), ())), preferred_element_type=F32)


def _sigmoid(x):
    return 1.0 / (1.0 + jnp.exp(-x))


_GELU_C = 0.7978845608028654


def _gelu(x):
    return x * (0.5 * (1.0 + jnp.tanh(_GELU_C * (x + 0.044715 * (x * x * x)))))


def _gelu_grad(x):
    t = jnp.tanh(_GELU_C * (x + 0.044715 * (x * x * x)))
    return 0.5 * (1.0 + t) + 0.5 * x * (1.0 - t * t) * (_GELU_C * (1.0 + 3.0 * 0.044715 * x * x))


def _rms_stats(x):
    r = lax.rsqrt(jnp.mean(x * x, axis=-1, keepdims=True) + NORM_EPS)
    return r, x * r


def _rms_bwd(dy, x, g):
    r, xh = _rms_stats(x)
    dg = jnp.sum(dy * xh, axis=0, keepdims=True)
    dxh = dy * g
    dx = r * (dxh - xh * jnp.mean(dxh * xh, axis=-1, keepdims=True))
    return dx, dg


def _split3(x):
    hi = x.astype(BF16)
    r = x - hi.astype(F32)
    mid = r.astype(BF16)
    lo = (r - mid.astype(F32)).astype(BF16)
    return hi, mid, lo


def _mask_mm(mask_bf16, x):
    hi, mid, lo = _split3(x)
    return _dot(mask_bf16, hi) + _dot(mask_bf16, mid) + _dot(mask_bf16, lo)


def _place():
    return lax.axis_index("x"), lax.axis_index("y"), lax.axis_index("c")


def _gather_copies(src, out, send, recv, loc, slicers):
    n = len(src)
    x, y, c = _place()
    me, sib = (x, y, c), (x, y, 1 - c)
    chips = [(1 - x, y), (x, 1 - y), (1 - x, 1 - y)]

    def dev(p):
        return 4 * p[0] + 2 * p[1] + p[2]

    def rc(i, k, block, to, from_src=False):
        dst = slicers[i](out[i], dev(block))
        return pltpu.make_async_remote_copy(
            src_ref=src[i] if from_src else dst, dst_ref=dst, send_sem=send.at[7 * i + k],
            recv_sem=recv.at[7 * i + k], device_id=to, device_id_type=MESH_ID)

    mine = [pltpu.make_async_copy(src[i], slicers[i](out[i], dev(me)), loc.at[i]) for i in range(n)]
    for cp in mine:
        cp.start()
    first = []
    for i in range(n):
        first.append(rc(i, 0, me, sib, True))
        for j, chip in enumerate(chips):
            first.append(rc(i, 1 + j, me, (*chip, c), True))
    for cp in first:
        cp.start()
    passed = []
    for j, chip in enumerate(chips):
        for i in range(n):
            rc(i, 1 + j, (*chip, c), me).wait_recv()
            cp = rc(i, 4 + j, (*chip, c), sib)
            cp.start()
            passed.append(cp)
    for i in range(n):
        rc(i, 0, sib, me).wait_recv()
        for j, chip in enumerate(chips):
            rc(i, 4 + j, (*chip, 1 - c), me).wait_recv()
    for cp in first + passed:
        cp.wait_send()
    for cp in mine:
        cp.wait()


def _gather_scratch(n):
    return [pltpu.SemaphoreType.DMA((7 * n,)), pltpu.SemaphoreType.DMA((7 * n,)), pltpu.SemaphoreType.DMA((n,))]


def _handshake(peers):
    barrier = pltpu.get_barrier_semaphore()
    for peer in peers:
        pl.semaphore_signal(barrier, inc=1, device_id=peer, device_id_type=MESH_ID)
    pl.semaphore_wait(barrier, len(peers))


def _all_gather_async(name, collective_id, srcs, out_shapes, slicers):
    n = len(srcs)

    def body(*refs):
        x, y, c = _place()
        _handshake([(1 - x if dx else x, 1 - y if dy else y, 1 - c if dc else c)
                    for dx in (0, 1) for dy in (0, 1) for dc in (0, 1) if dx or dy or dc])
        _gather_copies(refs[:n], refs[n:2 * n], *refs[2 * n:], slicers)

    return _sequencer_call(name, collective_id, body, srcs, [jax.ShapeDtypeStruct(s, d) for s, d in out_shapes],
                           _gather_scratch(n))


def _sequencer_call(name, collective_id, body, operands, out_types, scratch):
    return pl.kernel(
        body, out_type=out_types, mesh=plsc.ScalarSubcoreMesh(axis_name="sequencer", num_cores=1), name=name,
        scratch_types=scratch, compiler_params=pltpu.CompilerParams(collective_id=collective_id),
    )(*operands)


def _exchange_sibling(name, collective_id, grads, shard_fns, shard_shapes):
    n = len(grads)

    def body(*refs):
        g, land = refs[:n], refs[n:2 * n]
        send, recv = refs[2 * n:]
        x, y, c = _place()
        _handshake([(x, y, 1 - c)])
        remote = []
        for i in range(n):
            for q in range(4):
                cp = pltpu.make_async_remote_copy(
                    src_ref=shard_fns[i](g[i], 2 * q + (1 - c)), dst_ref=land[i].at[q], send_sem=send.at[4 * i + q],
                    recv_sem=recv.at[4 * i + q], device_id=(x, y, 1 - c), device_id_type=MESH_ID)
                cp.start()
                remote.append(cp)
        for cp in remote:
            cp.wait()

    return _sequencer_call(name, collective_id, body, grads, [jax.ShapeDtypeStruct((4, *s), F32) for s in shard_shapes],
                           [pltpu.SemaphoreType.DMA((4 * n,)), pltpu.SemaphoreType.DMA((4 * n,))])


def _exchange_chips(name, collective_id, parts):
    n = len(parts)

    def body(*refs):
        part, out = refs[:n], refs[n:2 * n]
        send, recv = refs[2 * n:]
        x, y, c = _place()
        _handshake([(1 - x, y, c), (x, 1 - y, c), (1 - x, 1 - y, c)])
        remote = []
        for i in range(n):
            for s in range(3):
                qx = 1 - x if (s + 1) // 2 else x
                qy = 1 - y if (s + 1) % 2 else y
                cp = pltpu.make_async_remote_copy(
                    src_ref=part[i].at[2 * qx + qy], dst_ref=out[i].at[s], send_sem=send.at[3 * i + s],
                    recv_sem=recv.at[3 * i + s], device_id=(qx, qy, c), device_id_type=MESH_ID)
                cp.start()
                remote.append(cp)
        for cp in remote:
            cp.wait()

    return _sequencer_call(name, collective_id, body, parts,
                           [jax.ShapeDtypeStruct((3, *p.shape[1:]), p.dtype) for p in parts],
                           [pltpu.SemaphoreType.DMA((3 * n,)), pltpu.SemaphoreType.DMA((3 * n,))])


def _chip_partial(name, core, grad, own_block, own_index, land):
    _, rows, cols = land.shape
    tr = own_block[-2]

    def body(core_ref, a_ref, b_ref, o_ref):
        o_ref[...] = (a_ref[...] + b_ref[...]).astype(BF16)

    spec = pl.BlockSpec((None, tr, cols), lambda q, r, c: (q, r, 0))
    return pl.pallas_call(
        body, name=name, out_shape=jax.ShapeDtypeStruct(land.shape, BF16),
        grid_spec=pltpu.PrefetchScalarGridSpec(
            num_scalar_prefetch=1, grid=(4, rows // tr),
            in_specs=[pl.BlockSpec(own_block, lambda q, r, c: own_index(q, r, c[0])), spec], out_specs=spec),
        compiler_params=_params([((tr, cols), F32)] * 2 + [((tr, cols), BF16)], sem=("arbitrary", "arbitrary")),
    )(core, grad, land)


def _adamw_math(w, g, m, v):
    m = ADAM_B1 * m + (1.0 - ADAM_B1) * g
    v = ADAM_B2 * v + (1.0 - ADAM_B2) * (g * g)
    m_hat = m / (1.0 - ADAM_B1 ** ADAM_STEP)
    v_hat = v / (1.0 - ADAM_B2 ** ADAM_STEP)
    delta = -ADAM_LR * (m_hat / (jnp.sqrt(v_hat) + ADAM_EPS) + ADAM_WD * w)
    return delta, m, v


def _adamw(name, chip, own, landed, w, m, v):
    _, rows, cols = own.shape
    tr = _tile(rows, 256) if rows % 256 == 0 else _tile(rows, 176)

    def body(chip_ref, own_ref, l_ref, w_ref, m_ref, v_ref, g_out, d_out, m_out, v_out):
        g = own_ref[...].astype(F32)
        for s in range(3):
            g = g + l_ref[s].astype(F32)
        delta, m_new, v_new = _adamw_math(w_ref[...], g, m_ref[...], v_ref[...])
        g_out[...] = g
        d_out[...] = delta
        m_out[...] = m_new
        v_out[...] = v_new

    spec = pl.BlockSpec((tr, cols), lambda r, c: (r, 0))
    return pl.pallas_call(
        body, name=name, out_shape=[jax.ShapeDtypeStruct((rows, cols), F32)] * 4,
        grid_spec=pltpu.PrefetchScalarGridSpec(
            num_scalar_prefetch=1, grid=(rows // tr,),
            in_specs=[pl.BlockSpec((None, tr, cols), lambda r, c: (c[0], r, 0)),
                      pl.BlockSpec((3, tr, cols), lambda r, c: (0, r, 0)), spec, spec, spec],
            out_specs=[spec] * 4),
        compiler_params=_params([((4, tr, cols), own.dtype)] + [((tr, cols), F32)] * 7, sem=("arbitrary",)),
    )(chip, own, landed, w, m, v)


def _swap_with_sibling(name, x):
    def body(x_ref, o_ref, send, recv):
        px, py, c = _place()
        cp = pltpu.make_async_remote_copy(src_ref=x_ref, dst_ref=o_ref, send_sem=send, recv_sem=recv,
                                          device_id=(px, py, 1 - c), device_id_type=MESH_ID)
        cp.start()
        cp.wait()

    return pl.pallas_call(
        body, name=name, out_shape=jax.ShapeDtypeStruct(x.shape, x.dtype), in_specs=[ANY], out_specs=ANY,
        scratch_shapes=[pltpu.SemaphoreType.DMA, pltpu.SemaphoreType.DMA],
    )(x)


def _proj_forward_own_chip(positions, x, gain, w_own, w_sibling):
    t = x.shape[0]
    tm = _tile(t, 1024)

    def body(pos_ref, x_ref, g_ref, wo_ref, ws_ref, o_ref, h_ref, ht_ref):
        @pl.when(pl.program_id(1) == 0)
        def _():
            _, xh = _rms_stats(x_ref[...])
            h = (xh * g_ref[...]).astype(BF16)
            h_ref[...] = h
            ht_ref[...] = h.T
            o_ref[...] = _dot(h, wo_ref[...])

        @pl.when(pl.program_id(1) == 1)
        def _():
            o_ref[...] = _dot(h_ref[...], ws_ref[...])

    tok = pl.BlockSpec((tm, D_MODEL), lambda m, k, pos: (m, 0))
    return pl.pallas_call(
        body, name="proj_fwd_own_chip",
        out_shape=[jax.ShapeDtypeStruct((N_DEV, t, D_MODEL), F32), jax.ShapeDtypeStruct((t, D_MODEL), BF16),
                   jax.ShapeDtypeStruct((D_MODEL, t), BF16)],
        grid_spec=pltpu.PrefetchScalarGridSpec(
            num_scalar_prefetch=1, grid=(t // tm, 2),
            in_specs=[tok, pl.BlockSpec((1, D_MODEL), lambda m, k, pos: (0, 0)), RESIDENT, RESIDENT],
            out_specs=[pl.BlockSpec((None, tm, D_MODEL), lambda m, k, pos: (pos[k], m, 0)), tok,
                       pl.BlockSpec((D_MODEL, tm), lambda m, k, pos: (0, m))]),
        compiler_params=_params([((tm, D_MODEL), F32)] * 2 + [((tm, D_MODEL), BF16)] * 2,
                                scratch=[((2, D_MODEL, D_MODEL), BF16)], temps=6 << 20, sem=("arbitrary", "arbitrary")),
    )(positions, x, gain, w_own, w_sibling)


def _proj_forward_other_chips(positions, proj, h, w_in_g):
    t = h.shape[0]
    tm = _tile(t, 1024)

    def body(pos_ref, _, h_ref, w_ref, o_ref):
        o_ref[...] = _dot(h_ref[...], w_ref[pos_ref[pl.program_id(1)]])

    return pl.pallas_call(
        body, name="proj_fwd_other_chips", out_shape=jax.ShapeDtypeStruct(proj.shape, F32),
        grid_spec=pltpu.PrefetchScalarGridSpec(
            num_scalar_prefetch=1, grid=(t // tm, N_DEV - 2),
            in_specs=[ANY, pl.BlockSpec((tm, D_MODEL), lambda m, k, pos: (m, 0)), RESIDENT],
            out_specs=pl.BlockSpec((None, tm, D_MODEL), lambda m, k, pos: (pos[k], m, 0))),
        input_output_aliases={1: 0},
        compiler_params=_params([((tm, D_MODEL), F32), ((tm, D_MODEL), BF16)], scratch=[((N_DEV, D_MODEL, D_MODEL), BF16)],
                                temps=6 << 20, sem=("arbitrary", "arbitrary")),
    )(positions, proj, h, w_in_g)


def _masked_ws(ws_ref, g):
    row = lax.broadcasted_iota(jnp.int32, (GMLP_CHUNK, GMLP_CHUNK), 0)
    col = lax.broadcasted_iota(jnp.int32, (GMLP_CHUNK, GMLP_CHUNK), 1)
    return jnp.where(row >= col, ws_ref[g], 0.0).astype(BF16)


def _gmlp_forward(proj, ln_g, ln_b, w_s, bias_b):
    t = proj.shape[1]
    tm = _tile(t, 256)
    chunks = tm // GMLP_CHUNK

    def body(u_ref, v_ref, lng_ref, lnb_ref, ws_ref, bias_ref, a_ref, vn_scr):
        vv = _gelu(v_ref[...])
        mu = jnp.mean(vv, axis=-1, keepdims=True)
        cen = vv - mu
        var = jnp.mean(cen * cen, axis=-1, keepdims=True)
        vn_scr[...] = ((cen * lax.rsqrt(var + NORM_EPS)) * lng_ref[...] + lnb_ref[...]).astype(BF16)
        for g in range(GROUPS):
            wm = _masked_ws(ws_ref, g)
            cols = slice(g * HEAD_DIM, (g + 1) * HEAD_DIM)
            for c in range(chunks):
                rows = slice(c * GMLP_CHUNK, (c + 1) * GMLP_CHUNK)
                mixed = _dot(wm, vn_scr[rows, cols]) + bias_ref[g]
                a_ref[rows, cols] = (_gelu(u_ref[rows, cols]) * mixed).astype(BF16)

    small = pl.BlockSpec((GROUPS, GMLP_CHUNK, GMLP_CHUNK), lambda m: (0, 0, 0))
    vec = pl.BlockSpec((1, D_MODEL), lambda m: (0, 0))
    return pl.pallas_call(
        body, name="gmlp_fwd", out_shape=jax.ShapeDtypeStruct((t, D_MODEL), BF16), grid=(t // tm,),
        in_specs=[pl.BlockSpec((None, tm, D_MODEL), lambda m: (U_POS, m, 0)),
                  pl.BlockSpec((None, tm, D_MODEL), lambda m: (U_POS + 1, m, 0)), vec, vec, small, small],
        out_specs=pl.BlockSpec((tm, D_MODEL), lambda m: (m, 0)),
        scratch_shapes=[pltpu.VMEM((tm, D_MODEL), BF16)],
        compiler_params=_params([((tm, D_MODEL), F32)] * 2 + [((tm, D_MODEL), BF16)] + [((8, 128, 128), F32)] * 2,
                                scratch=[((tm, D_MODEL), BF16)], temps=8 << 20, sem=("arbitrary",)),
    )(proj, proj, ln_g, ln_b, w_s, bias_b)


def _lower_bound(tab_ref):
    t0, t1 = tab_ref[0:1, :], tab_ref[1:2, :]
    mx = jnp.maximum(t0, t1)
    e0, e1 = jnp.exp(t0 - mx), jnp.exp(t1 - mx)
    return e0 / (e0 + e1)


def _tri_masks():
    row = lax.broadcasted_iota(jnp.int32, (HGRN_CHUNK, HGRN_CHUNK), 0)
    col = lax.broadcasted_iota(jnp.int32, (HGRN_CHUNK, HGRN_CHUNK), 1)
    return row >= col, row <= col


def _chunk_rows(c):
    return slice(c * HGRN_CHUNK, (c + 1) * HGRN_CHUNK)


def _per_chunk(x, nc, fn):
    return jnp.concatenate([fn(x[_chunk_rows(c)]) for c in range(nc)], axis=0)


def _chunk_row_bcast(x, nc, i):
    return _per_chunk(x, nc, lambda xc: jnp.broadcast_to(xc[i:i + 1, :], (HGRN_CHUNK, HEAD_DIM)))


def _hgrn_gates(q, fl, lb, nc):
    lower, _ = _tri_masks()
    lower = lower.astype(BF16)
    s = _sigmoid(fl)
    f = lb + (1.0 - lb) * s
    k = 1.0 - f
    hi, mid, lo = _split3(jnp.log(f))
    a = jnp.concatenate([_dot(lower, hi[_chunk_rows(c)]) + _dot(lower, mid[_chunk_rows(c)]) + _dot(lower, lo[_chunk_rows(c)])
                         for c in range(nc)], axis=0)
    a_mid = _chunk_row_bcast(a, nc, HGRN_CHUNK // 2 - 1)
    a_last = _chunk_row_bcast(a, nc, HGRN_CHUNK - 1)
    qs = q * HGRN_SCALE
    e_in, e_out, e_end, e_all = jnp.exp(a - a_mid), jnp.exp(a_mid - a), jnp.exp(a_last - a), jnp.exp(a)
    decay = [jnp.exp(a[c * HGRN_CHUNK + HGRN_CHUNK - 1:(c + 1) * HGRN_CHUNK, :]) for c in range(nc)]
    return dict(s=s, f=f, k=k, decay=decay, e_in=e_in, e_out=e_out, e_end=e_end, e_all=e_all,
                qi=qs * e_in, ki=k * e_out, kd=k * e_end, qe=qs * e_all)


def _hgrn_forward(proj, lb_table, norm_g):
    t = proj.shape[1]
    tb = _tile(t, 1024)
    nc = tb // HGRN_CHUNK
    n_chunks = t // HGRN_CHUNK

    def body(q_ref, f_ref, i_ref, g_ref, tab_ref, ng_ref, og_ref, o_ref, st_ref, state):
        @pl.when(pl.program_id(1) == 0)
        def _():
            state[...] = jnp.zeros_like(state)

        lower, _ = _tri_masks()
        gt = _hgrn_gates(q_ref[...], f_ref[...], _lower_bound(tab_ref), nc)
        qi, ki, kd, qe = (gt[n].astype(BF16) for n in ("qi", "ki", "kd", "qe"))
        vb = i_ref[...].astype(BF16)
        o_intra, d_state = [], []
        for c in range(nc):
            rows = _chunk_rows(c)
            p = jnp.where(lower, _dot_nt(qi[rows], ki[rows]), 0.0).astype(BF16)
            o_intra.append(_dot(p, vb[rows]))
            d_state.append(_dot_tn(vb[rows], kd[rows]))
        st = state[...]
        outs = []
        for c in range(nc):
            st_ref[c] = st
            outs.append(o_intra[c] + _dot_nt(qe[_chunk_rows(c)], st.astype(BF16)))
            st = st * gt["decay"][c] + d_state[c]
        state[...] = st
        o = jnp.concatenate(outs, axis=0)
        o_ref[...] = o
        _, oh = _rms_stats(o)
        gz = g_ref[...]
        og_ref[...] = ((oh * ng_ref[...]) * (gz * _sigmoid(gz))).astype(BF16)

    def blk(p):
        return pl.BlockSpec((None, tb, HEAD_DIM), lambda h, n: (p, n, h))

    out_blk = pl.BlockSpec((tb, HEAD_DIM), lambda h, n: (n, h))
    return pl.pallas_call(
        body, name="hgrn_fwd",
        out_shape=[jax.ShapeDtypeStruct((t, D_MODEL), BF16), jax.ShapeDtypeStruct((t, D_MODEL), F32),
                   jax.ShapeDtypeStruct((HEADS, n_chunks, HEAD_DIM, HEAD_DIM), F32)],
        grid=(HEADS, t // tb),
        in_specs=[blk(Q_POS), blk(Q_POS + 1), blk(Q_POS + 2), blk(Q_POS + 3),
                  pl.BlockSpec((2, HEAD_DIM), lambda h, n: (0, h)), pl.BlockSpec((1, HEAD_DIM), lambda h, n: (0, h))],
        out_specs=[out_blk, out_blk, pl.BlockSpec((None, nc, HEAD_DIM, HEAD_DIM), lambda h, n: (h, n, 0, 0))],
        scratch_shapes=[pltpu.VMEM((HEAD_DIM, HEAD_DIM), F32)],
        compiler_params=_params([((tb, HEAD_DIM), F32)] * 6 + [((nc, HEAD_DIM, HEAD_DIM), F32)], temps=8 << 20,
                                sem=("arbitrary", "arbitrary")),
    )(proj, proj, proj, proj, lb_table, norm_g)


def _branch_out_forward(a, og, proj, x, w_a, w_b, w_out, ffn_g):
    t = x.shape[0]
    tm = _tile(t, 256)

    def body(a_ref, og_ref, ga_ref, gb_ref, x_ref, wa_ref, wb_ref, wo_ref, g_ref, ya_ref, yb_ref, mg_ref, x1_ref, h2_ref):
        ya = _dot(a_ref[...], wa_ref[...])
        yb = _dot(og_ref[...], wb_ref[...])
        ya_ref[...] = ya
        yb_ref[...] = yb
        merged = (_sigmoid(ga_ref[...]) * ya + _sigmoid(gb_ref[...]) * yb).astype(BF16)
        mg_ref[...] = merged
        x1 = x_ref[...] + _dot(merged, wo_ref[...])
        x1_ref[...] = x1
        _, xh = _rms_stats(x1)
        h2_ref[...] = (xh * g_ref[...]).astype(BF16)

    tok = pl.BlockSpec((tm, D_MODEL), lambda m: (m, 0))
    wsp = pl.BlockSpec((D_MODEL, D_MODEL), lambda m: (0, 0))
    return pl.pallas_call(
        body, name="branch_out_fwd",
        out_shape=[jax.ShapeDtypeStruct((t, D_MODEL), F32), jax.ShapeDtypeStruct((t, D_MODEL), F32),
                   jax.ShapeDtypeStruct((t, D_MODEL), BF16), jax.ShapeDtypeStruct((t, D_MODEL), F32),
                   jax.ShapeDtypeStruct((t, D_MODEL), BF16)],
        grid=(t // tm,),
        in_specs=[tok, tok, pl.BlockSpec((None, tm, D_MODEL), lambda m: (GATE_POS, m, 0)),
                  pl.BlockSpec((None, tm, D_MODEL), lambda m: (GATE_POS + 1, m, 0)), tok, wsp, wsp, wsp,
                  pl.BlockSpec((1, D_MODEL), lambda m: (0, 0))],
        out_specs=[tok] * 5,
        compiler_params=_params([((tm, D_MODEL), BF16)] * 4 + [((tm, D_MODEL), F32)] * 6 + [((D_MODEL, D_MODEL), BF16)] * 3,
                                temps=8 << 20, sem=("arbitrary",)),
    )(a, og, proj, proj, x, w_a, w_b, w_out, ffn_g)


def _ffn_forward(h2, x1, w_gu, w_down, target, final_g):
    t = x1.shape[0]
    tm = _tile(t, 512)

    def body(h_ref, wgu_ref, wd_ref, x1_ref, t_ref, g_ref, gu_ref, act_ref, loss_ref, dg_ref, dx_ref, dxb_ref, acc):
        m, j = pl.program_id(0), pl.program_id(1)

        @pl.when((m == 0) & (j == 0))
        def _():
            loss_ref[...] = jnp.zeros_like(loss_ref)
            dg_ref[...] = jnp.zeros_like(dg_ref)

        h = h_ref[...]
        gate = _dot_nt(h, wgu_ref[j])
        up = _dot_nt(h, wgu_ref[j + 4])
        gu_ref[0] = gate
        gu_ref[1] = up
        act = ((gate * _sigmoid(gate)) * up).astype(BF16)
        act_ref[...] = act
        part = _dot(act, wd_ref[j])

        @pl.when(j == 0)
        def _():
            acc[...] = part

        @pl.when((j > 0) & (j < 3))
        def _():
            acc[...] += part

        @pl.when(j == 3)
        def _():
            x2 = x1_ref[...] + (acc[...] + part)
            g = g_ref[...]
            r, xh = _rms_stats(x2)
            err = xh * g - t_ref[...]
            loss_ref[...] += 0.5 * jnp.sum(jnp.mean(err * err, axis=-1, keepdims=True), axis=0, keepdims=True)
            dy = err * (1.0 / D_MODEL)
            dg_ref[...] += jnp.sum(dy * xh, axis=0, keepdims=True)
            dxh = dy * g
            dx = r * (dxh - xh * jnp.mean(dxh * xh, axis=-1, keepdims=True))
            dx_ref[...] = dx
            dxb_ref[...] = dx.astype(BF16)

    tok = pl.BlockSpec((tm, D_MODEL), lambda m, j: (m, 0))
    vec = pl.BlockSpec((1, D_MODEL), lambda m, j: (0, 0))
    return pl.pallas_call(
        body, name="ffn_fwd",
        out_shape=[jax.ShapeDtypeStruct((4, 2, t, FF_BLOCK), F32), jax.ShapeDtypeStruct((4, t, FF_BLOCK), BF16),
                   jax.ShapeDtypeStruct((8, 128), F32), jax.ShapeDtypeStruct((1, D_MODEL), F32),
                   jax.ShapeDtypeStruct((t, D_MODEL), F32), jax.ShapeDtypeStruct((t, D_MODEL), BF16)],
        grid=(t // tm, 4),
        in_specs=[tok, RESIDENT, RESIDENT, tok, tok, vec],
        out_specs=[pl.BlockSpec((None, 2, tm, FF_BLOCK), lambda m, j: (j, 0, m, 0)),
                   pl.BlockSpec((None, tm, FF_BLOCK), lambda m, j: (j, m, 0)),
                   pl.BlockSpec((8, 128), lambda m, j: (0, 0)), vec, tok, tok],
        scratch_shapes=[pltpu.VMEM((tm, D_MODEL), F32)],
        compiler_params=_params([((tm, D_MODEL), BF16), ((tm, D_MODEL), F32), ((tm, D_MODEL), F32), ((2, tm, 768), F32),
                                 ((tm, 768), BF16), ((tm, D_MODEL), F32), ((tm, D_MODEL), BF16)],
                                scratch=[((tm, D_MODEL), F32), ((N_DEV, FF_BLOCK, D_MODEL), BF16), ((D_FF, D_MODEL), BF16)],
                                temps=6 << 20, sem=("arbitrary", "arbitrary")),
    )(h2, w_gu, w_down.reshape(4, FF_BLOCK, D_MODEL), x1, target, final_g)


def _ffn_backward(dx2b, dx2, gu, x1, w_gu, w_down, ffn_g):
    t = x1.shape[0]
    tm = _tile(t, 512)

    def body(dxb_ref, dx2_ref, gu_ref, x1_ref, wgu_ref, wd_ref, g_ref, dgu_ref, dx1_ref, dx1b_ref, dg_ref, acc, prev):
        m, j = pl.program_id(0), pl.program_id(1)

        @pl.when((m == 0) & (j == 0))
        def _():
            dg_ref[...] = jnp.zeros_like(dg_ref)

        @pl.when(j == 0)
        def _():
            prev[...] = jnp.zeros_like(prev)
            acc[...] = jnp.zeros_like(acc)

        jm1 = jnp.maximum(j - 1, 0)
        acc[...] += _dot(prev[0], wgu_ref[jm1]) + _dot(prev[1], wgu_ref[jm1 + 4])
        dact = _dot_nt(dxb_ref[...], wd_ref[j])
        gate, up = gu_ref[0], gu_ref[1]
        sg = _sigmoid(gate)
        dgate = (dact * up * (sg * (1.0 + gate * (1.0 - sg)))).astype(BF16)
        dup = (dact * (gate * sg)).astype(BF16)
        dgu_ref[0] = dgate
        dgu_ref[1] = dup
        prev[0] = dgate
        prev[1] = dup

        @pl.when(j == 3)
        def _():
            dh2 = acc[...] + (_dot(prev[0], wgu_ref[3]) + _dot(prev[1], wgu_ref[7]))
            dx, dg = _rms_bwd(dh2, x1_ref[...], g_ref[...])
            dx1 = dx2_ref[...] + dx
            dx1_ref[...] = dx1
            dx1b_ref[...] = dx1.astype(BF16)
            dg_ref[...] += dg

    tok = pl.BlockSpec((tm, D_MODEL), lambda m, j: (m, 0))
    vec = pl.BlockSpec((1, D_MODEL), lambda m, j: (0, 0))
    gu_spec = pl.BlockSpec((None, 2, tm, FF_BLOCK), lambda m, j: (j, 0, m, 0))
    return pl.pallas_call(
        body, name="ffn_bwd",
        out_shape=[jax.ShapeDtypeStruct((4, 2, t, FF_BLOCK), BF16), jax.ShapeDtypeStruct((t, D_MODEL), F32),
                   jax.ShapeDtypeStruct((t, D_MODEL), BF16), jax.ShapeDtypeStruct((1, D_MODEL), F32)],
        grid=(t // tm, 4),
        in_specs=[tok, tok, gu_spec, tok, RESIDENT, RESIDENT, vec],
        out_specs=[gu_spec, tok, tok, vec],
        scratch_shapes=[pltpu.VMEM((tm, D_MODEL), F32), pltpu.VMEM((2, tm, FF_BLOCK), BF16)],
        compiler_params=_params([((tm, D_MODEL), BF16), ((tm, D_MODEL), F32), ((2, tm, 768), F32), ((tm, D_MODEL), F32),
                                 ((2, tm, 768), BF16), ((tm, D_MODEL), F32), ((tm, D_MODEL), BF16)],
                                scratch=[((tm, D_MODEL), F32), ((2, tm, 768), BF16), ((N_DEV, FF_BLOCK, D_MODEL), BF16),
                                         ((D_FF, D_MODEL), BF16)],
                                temps=4 << 20, sem=("arbitrary", "arbitrary")),
    )(dx2b, dx2, gu, x1, w_gu, w_down.reshape(4, FF_BLOCK, D_MODEL), ffn_g)


def _branch_out_backward(dx1b, ya, yb, proj, w_a, w_b, w_out):
    t = ya.shape[0]
    tm = _tile(t, 256)

    def body(dx_ref, ya_ref, yb_ref, ga_ref, gb_ref, wa_ref, wb_ref, wo_ref, dya_ref, dyb_ref, dgate_ref, da_ref, dog_ref):
        dm = _dot_nt(dx_ref[...], wo_ref[...])
        sa, sb = _sigmoid(ga_ref[...]), _sigmoid(gb_ref[...])
        dya = (dm * sa).astype(BF16)
        dyb = (dm * sb).astype(BF16)
        dya_ref[...] = dya
        dyb_ref[...] = dyb
        dgate_ref[0] = (dm * ya_ref[...] * (sa * (1.0 - sa))).astype(BF16)
        dgate_ref[1] = (dm * yb_ref[...] * (sb * (1.0 - sb))).astype(BF16)
        da_ref[...] = _dot_nt(dya, wa_ref[...])
        dog_ref[...] = _dot_nt(dyb, wb_ref[...])

    tok = pl.BlockSpec((tm, D_MODEL), lambda m: (m, 0))
    wsp = pl.BlockSpec((D_MODEL, D_MODEL), lambda m: (0, 0))
    return pl.pallas_call(
        body, name="branch_out_bwd",
        out_shape=[jax.ShapeDtypeStruct((t, D_MODEL), BF16), jax.ShapeDtypeStruct((t, D_MODEL), BF16),
                   jax.ShapeDtypeStruct((N_DEV, t, D_MODEL), BF16), jax.ShapeDtypeStruct((t, D_MODEL), F32),
                   jax.ShapeDtypeStruct((t, D_MODEL), F32)],
        grid=(t // tm,),
        in_specs=[tok, tok, tok, pl.BlockSpec((None, tm, D_MODEL), lambda m: (GATE_POS, m, 0)),
                  pl.BlockSpec((None, tm, D_MODEL), lambda m: (GATE_POS + 1, m, 0)), wsp, wsp, wsp],
        out_specs=[tok, tok, pl.BlockSpec((2, tm, D_MODEL), lambda m: (GATE_POS // 2, m, 0)), tok, tok],
        compiler_params=_params([((tm, D_MODEL), BF16)] * 5 + [((tm, D_MODEL), F32)] * 6 + [((D_MODEL, D_MODEL), BF16)] * 3,
                                temps=8 << 20, sem=("arbitrary",)),
    )(dx1b, ya, yb, proj, proj, w_a, w_b, w_out)


def _hgrn_backward(dproj, dog, o_saved, states, proj, lb_table, norm_g):
    t = proj.shape[1]
    tb = _tile(t, 1024)
    nc = tb // HGRN_CHUNK
    nb = t // tb

    def body(_, dog_ref, o_ref, st_ref, q_ref, f_ref, i_ref, g_ref, tab_ref, ng_ref, dp_ref, dng_ref, dtab_ref, gstate):
        @pl.when(pl.program_id(1) == 0)
        def _():
            gstate[...] = jnp.zeros_like(gstate)
            dng_ref[...] = jnp.zeros_like(dng_ref)
            dtab_ref[...] = jnp.zeros_like(dtab_ref)

        lb = _lower_bound(tab_ref)
        ng = ng_ref[...]
        lower, upper = _tri_masks()
        gt = _hgrn_gates(q_ref[...], f_ref[...], lb, nc)
        qi, ki, kd, qe = (gt[n].astype(BF16) for n in ("qi", "ki", "kd", "qe"))
        vb = i_ref[...].astype(BF16)
        o, gz, d_og = o_ref[...], g_ref[...], dog_ref[...]
        r, oh = _rms_stats(o)
        sg = _sigmoid(gz)
        d_on = d_og * (gz * sg)
        dgz = d_og * (oh * ng) * (sg * (1.0 + gz * (1.0 - sg)))
        dng_ref[...] += jnp.sum(d_on * oh, axis=0, keepdims=True)
        doh = d_on * ng
        dob = (r * (doh - oh * jnp.mean(doh * oh, axis=-1, keepdims=True))).astype(BF16)
        dv_intra, dqi, dki, dqe, g_upd = [], [], [], [], []
        for c in range(nc):
            rows = _chunk_rows(c)
            p = jnp.where(lower, _dot_nt(qi[rows], ki[rows]), 0.0).astype(BF16)
            dv_intra.append(_dot_tn(p, dob[rows]))
            dp = jnp.where(lower, _dot_nt(dob[rows], vb[rows]), 0.0).astype(BF16)
            dqi.append(_dot(dp, ki[rows]))
            dki.append(_dot_tn(dp, qi[rows]))
            dqe.append(_dot(dob[rows], st_ref[c].astype(BF16)))
            g_upd.append(_dot_tn(dob[rows], qe[rows]))
        g_after = [None] * nc
        g = gstate[...]
        for c in reversed(range(nc)):
            g_after[c] = g
            g = g * gt["decay"][c] + g_upd[c]
        gstate[...] = g
        dkd, dv, da_last = [], [], []
        for c in range(nc):
            rows = _chunk_rows(c)
            gb = g_after[c].astype(BF16)
            dkd.append(_dot(vb[rows], gb))
            dv.append(dv_intra[c] + _dot_nt(kd[rows], gb))
            da_last.append(jnp.sum(g_after[c] * st_ref[c], axis=0, keepdims=True) * gt["decay"][c])
        dqi, dki, dqe, dkd, dv = (jnp.concatenate(z, axis=0) for z in (dqi, dki, dqe, dkd, dv))
        dqs = dqi * gt["e_in"] + dqe * gt["e_all"]
        dk = dki * gt["e_out"] + dkd * gt["e_end"]
        t_in, t_out, t_end = dqi * gt["qi"], dki * gt["ki"], dkd * gt["kd"]
        da = t_in - t_out + dqe * gt["qe"] - t_end
        row = lax.broadcasted_iota(jnp.int32, (HGRN_CHUNK, HEAD_DIM), 0)
        d_mid = t_out - t_in
        pieces = []
        for c in range(nc):
            rows = _chunk_rows(c)
            da_mid = jnp.sum(d_mid[rows], axis=0, keepdims=True)
            da_end = jnp.sum(t_end[rows], axis=0, keepdims=True) + da_last[c]
            da_c = da[rows] + jnp.where(row == HGRN_CHUNK // 2 - 1, da_mid, 0.0) + jnp.where(row == HGRN_CHUNK - 1, da_end, 0.0)
            pieces.append(_mask_mm(upper.astype(BF16), da_c))
        df = jnp.concatenate(pieces, axis=0) / gt["f"] - dk
        s = gt["s"]
        dlb = jnp.sum(df * (1.0 - s), axis=0, keepdims=True)
        dp_ref[0] = (dqs * HGRN_SCALE).astype(BF16)
        dp_ref[1] = (df * (1.0 - lb) * (s * (1.0 - s))).astype(BF16)
        dp_ref[2] = dv.astype(BF16)
        dp_ref[3] = dgz.astype(BF16)
        dt0 = dlb * (lb * (1.0 - lb))
        dtab_ref[0:1, :] += dt0
        dtab_ref[1:2, :] -= dt0

    def blk(p):
        return pl.BlockSpec((None, tb, HEAD_DIM), lambda h, n: (p, nb - 1 - n, h))

    tok = pl.BlockSpec((tb, HEAD_DIM), lambda h, n: (nb - 1 - n, h))
    return pl.pallas_call(
        body, name="hgrn_bwd",
        out_shape=[jax.ShapeDtypeStruct((N_DEV, t, D_MODEL), BF16), jax.ShapeDtypeStruct((1, D_MODEL), F32),
                   jax.ShapeDtypeStruct((2, D_MODEL), F32)],
        grid=(HEADS, nb),
        in_specs=[ANY, tok, tok, pl.BlockSpec((None, nc, HEAD_DIM, HEAD_DIM), lambda h, n: (h, nb - 1 - n, 0, 0)),
                  blk(Q_POS), blk(Q_POS + 1), blk(Q_POS + 2), blk(Q_POS + 3),
                  pl.BlockSpec((2, HEAD_DIM), lambda h, n: (0, h)), pl.BlockSpec((1, HEAD_DIM), lambda h, n: (0, h))],
        out_specs=[pl.BlockSpec((4, tb, HEAD_DIM), lambda h, n: (0, nb - 1 - n, h)),
                   pl.BlockSpec((1, HEAD_DIM), lambda h, n: (0, h)), pl.BlockSpec((2, HEAD_DIM), lambda h, n: (0, h))],
        scratch_shapes=[pltpu.VMEM((HEAD_DIM, HEAD_DIM), F32)],
        input_output_aliases={0: 0},
        compiler_params=_params([((tb, HEAD_DIM), F32)] * 6 + [((nc, HEAD_DIM, HEAD_DIM), F32)] + [((4, tb, HEAD_DIM), BF16)],
                                temps=8 << 20, sem=("arbitrary", "arbitrary")),
    )(dproj, dog, o_saved, states, proj, proj, proj, proj, lb_table, norm_g)


def _gmlp_backward(dproj, da, proj, ln_g, ln_b, w_s, bias_b):
    t = proj.shape[1]
    tm = _tile(t, 256)
    chunks = tm // GMLP_CHUNK

    def body(_, da_ref, u_ref, v_ref, lng_ref, lnb_ref, ws_ref, bias_ref, dp_ref, dlng_ref, dlnb_ref, dws_ref, dbs_ref,
             vn_scr, dvn_scr):
        @pl.when(pl.program_id(0) == 0)
        def _():
            dlng_ref[...] = jnp.zeros_like(dlng_ref)
            dlnb_ref[...] = jnp.zeros_like(dlnb_ref)
            dws_ref[...] = jnp.zeros_like(dws_ref)
            dbs_ref[...] = jnp.zeros_like(dbs_ref)

        v = v_ref[...]
        vv = _gelu(v)
        mu = jnp.mean(vv, axis=-1, keepdims=True)
        cen = vv - mu
        rstd = lax.rsqrt(jnp.mean(cen * cen, axis=-1, keepdims=True) + NORM_EPS)
        vhat = cen * rstd
        lng = lng_ref[...]
        vn_scr[...] = (vhat * lng + lnb_ref[...]).astype(BF16)
        row = lax.broadcasted_iota(jnp.int32, (GMLP_CHUNK, GMLP_CHUNK), 0)
        col = lax.broadcasted_iota(jnp.int32, (GMLP_CHUNK, GMLP_CHUNK), 1)
        for g in range(GROUPS):
            wm = _masked_ws(ws_ref, g)
            cols = slice(g * HEAD_DIM, (g + 1) * HEAD_DIM)
            dws = jnp.zeros((GMLP_CHUNK, GMLP_CHUNK), F32)
            dbs = jnp.zeros((GMLP_CHUNK, GMLP_CHUNK), F32)
            for c in range(chunks):
                rows = slice(c * GMLP_CHUNK, (c + 1) * GMLP_CHUNK)
                vn = vn_scr[rows, cols]
                mixed = _dot(wm, vn) + bias_ref[g]
                u = u_ref[rows, cols]
                d_a = da_ref[rows, cols]
                dp_ref[0, rows, cols] = (d_a * mixed * _gelu_grad(u)).astype(BF16)
                dmix = d_a * _gelu(u)
                dmb = dmix.astype(BF16)
                dbs = dbs + dmix
                dws = dws + _dot_nt(dmb, vn)
                dvn_scr[rows, cols] = _dot_tn(wm, dmb)
            dws_ref[g] += jnp.where(row >= col, dws, 0.0)
            dbs_ref[g] += jnp.broadcast_to(jnp.sum(dbs, axis=-1, keepdims=True), (GMLP_CHUNK, GMLP_CHUNK))
        dvn = dvn_scr[...]
        dlng_ref[...] += jnp.sum(dvn * vhat, axis=0, keepdims=True)
        dlnb_ref[...] += jnp.sum(dvn, axis=0, keepdims=True)
        dvh = dvn * lng
        dvv = rstd * (dvh - jnp.mean(dvh, axis=-1, keepdims=True) - vhat * jnp.mean(dvh * vhat, axis=-1, keepdims=True))
        dp_ref[1] = (dvv * _gelu_grad(v)).astype(BF16)

    tok = pl.BlockSpec((tm, D_MODEL), lambda m: (m, 0))
    small = pl.BlockSpec((GROUPS, GMLP_CHUNK, GMLP_CHUNK), lambda m: (0, 0, 0))
    vec = pl.BlockSpec((1, D_MODEL), lambda m: (0, 0))
    return pl.pallas_call(
        body, name="gmlp_bwd",
        out_shape=[jax.ShapeDtypeStruct(dproj.shape, BF16), jax.ShapeDtypeStruct((1, D_MODEL), F32),
                   jax.ShapeDtypeStruct((1, D_MODEL), F32), jax.ShapeDtypeStruct((GROUPS, GMLP_CHUNK, GMLP_CHUNK), F32),
                   jax.ShapeDtypeStruct((GROUPS, GMLP_CHUNK, GMLP_CHUNK), F32)],
        grid=(t // tm,),
        in_specs=[ANY, tok, pl.BlockSpec((None, tm, D_MODEL), lambda m: (U_POS, m, 0)),
                  pl.BlockSpec((None, tm, D_MODEL), lambda m: (U_POS + 1, m, 0)), vec, vec, small, small],
        out_specs=[pl.BlockSpec((2, tm, D_MODEL), lambda m: (U_POS // 2, m, 0)), vec, vec, small, small],
        scratch_shapes=[pltpu.VMEM((tm, D_MODEL), BF16), pltpu.VMEM((tm, D_MODEL), F32)],
        input_output_aliases={0: 0},
        compiler_params=_params([((tm, D_MODEL), F32)] * 3 + [((2, tm, D_MODEL), BF16)] + [((8, 128, 128), F32)] * 4,
                                scratch=[((tm, D_MODEL), BF16), ((tm, D_MODEL), F32)], temps=12 << 20, sem=("arbitrary",)),
    )(dproj, da, proj, proj, ln_g, ln_b, w_s, bias_b)


def _input_backward(dproj, w_in_g, x, dx1, mix_g):
    t = x.shape[0]
    tm = _tile(t, 512)

    def body(dp_ref, w_ref, x_ref, dx1_ref, g_ref, dx_ref, dg_ref):
        @pl.when(pl.program_id(0) == 0)
        def _():
            dg_ref[...] = jnp.zeros_like(dg_ref)

        dh = _dot_nt(dp_ref[0], w_ref[0])
        for p in range(1, N_DEV):
            dh = dh + _dot_nt(dp_ref[p], w_ref[p])
        dx, dg = _rms_bwd(dh, x_ref[...], g_ref[...])
        dx_ref[...] = dx1_ref[...] + dx
        dg_ref[...] += dg

    tok = pl.BlockSpec((tm, D_MODEL), lambda m: (m, 0))
    vec = pl.BlockSpec((1, D_MODEL), lambda m: (0, 0))
    return pl.pallas_call(
        body, name="input_bwd",
        out_shape=[jax.ShapeDtypeStruct((t, D_MODEL), F32), jax.ShapeDtypeStruct((1, D_MODEL), F32)],
        grid=(t // tm,),
        in_specs=[pl.BlockSpec((N_DEV, tm, D_MODEL), lambda m: (0, m, 0)), RESIDENT, tok, tok, vec],
        out_specs=[tok, vec],
        compiler_params=_params([((N_DEV, tm, D_MODEL), BF16)] + [((tm, D_MODEL), F32)] * 3,
                                scratch=[((N_DEV, D_MODEL, D_MODEL), BF16)], temps=6 << 20, sem=("arbitrary",)),
    )(dproj, w_in_g, x, dx1, mix_g)


def _weight_grad(name, a, b, a_spec, b_spec, out_shape, out_spec, steps, blocks, a_is_transposed):
    def body(a_ref, b_ref, o_ref):
        o_ref[...] = _dot(a_ref[...], b_ref[...]) if a_is_transposed else _dot_tn(a_ref[...], b_ref[...])

    return pl.pallas_call(
        body, name=name, out_shape=jax.ShapeDtypeStruct(out_shape, F32), grid=(steps,), in_specs=[a_spec, b_spec],
        out_specs=out_spec, compiler_params=_params(blocks, temps=4 << 20, sem=("arbitrary",)),
    )(a, b)


def _pack_small(mix_g, ln_g, ln_b, b_s, lb_table, hg_norm, ffn_g, final_g, loss_row):
    def part(a):
        a = a.reshape(-1, D_MODEL)
        return jnp.pad(a, ((0, 8 - a.shape[0]), (0, 0)))

    return jnp.concatenate([part(mix_g), part(ln_g), part(ln_b), part(hg_norm), part(ffn_g), part(final_g),
                            part(lb_table), part(b_s), part(loss_row)], axis=0)


def _unpack_small(pack, w_s):
    return dict(norm_mix_g=pack[0:1], gmlp_ln_g=pack[8:9], gmlp_ln_b=pack[16:17], hgrn_norm_g=pack[24:25],
                norm_ffn_g=pack[32:33], norm_final_g=pack[40], hgrn_lb_table=pack[48:50],
                gmlp_b_s=pack[56:57].reshape(1, GROUPS, GMLP_CHUNK),
                gmlp_w_s=w_s.reshape(1, GROUPS, GMLP_CHUNK, GMLP_CHUNK))


def _adamw_small(name, gathered, w, m, v):
    rows, cols = w.shape

    def body(p_ref, w_ref, m_ref, v_ref, g_out, d_out, m_out, v_out):
        g = p_ref[0]
        for j in range(1, N_DEV):
            g = g + p_ref[j]
        delta, m_new, v_new = _adamw_math(w_ref[...], g, m_ref[...], v_ref[...])
        g_out[...] = g
        d_out[...] = delta
        m_out[...] = m_new
        v_out[...] = v_new

    tr = _tile(rows, 512)
    spec = pl.BlockSpec((tr, cols), lambda r: (r, 0))
    return pl.pallas_call(
        body, name=name, out_shape=[jax.ShapeDtypeStruct((rows, cols), F32)] * 4, grid=(rows // tr,),
        in_specs=[pl.BlockSpec((N_DEV, tr, cols), lambda r: (0, r, 0)), spec, spec, spec], out_specs=[spec] * 4,
        compiler_params=_params([((N_DEV, tr, cols), F32)] + [((tr, cols), F32)] * 7, sem=("arbitrary",)),
    )(gathered, w, m, v)


def kernel(x, norm_mix_g, w_in, gmlp_ln_g, gmlp_ln_b, gmlp_w_s, gmlp_b_s, hgrn_lb_table, hgrn_norm_g, w_branch_a, w_branch_b, w_out, norm_ffn_g, w_gate_up, w_down, norm_final_g, loss_target, m_norm_mix_g, m_w_in, m_gmlp_ln_g, m_gmlp_ln_b, m_gmlp_w_s, m_gmlp_b_s, m_hgrn_lb_table, m_hgrn_norm_g, m_w_branch_a, m_w_branch_b, m_w_out, m_norm_ffn_g, m_w_gate_up, m_w_down, m_norm_final_g, v_norm_mix_g, v_w_in, v_gmlp_ln_g, v_gmlp_ln_b, v_gmlp_w_s, v_gmlp_b_s, v_hgrn_lb_table, v_hgrn_norm_g, v_w_branch_a, v_w_branch_b, v_w_out, v_norm_ffn_g, v_w_gate_up, v_w_down, v_norm_final_g):
    t = x.shape[1]
    x2d = x.reshape(t, D_MODEL)
    target = loss_target.reshape(t, D_MODEL)
    final_g = norm_final_g.reshape(1, D_MODEL)

    shards = [w_in[0].astype(BF16), w_branch_a[0].astype(BF16), w_branch_b[0].astype(BF16), w_out[0].astype(BF16),
              w_gate_up[0].T.astype(BF16), w_down[0].astype(BF16)]

    def rows_of(n):
        return lambda ref, j: ref.at[pl.ds(pl.multiple_of(j * n, 8), n)]

    gathered = [((N_DEV, D_MODEL, D_MODEL), BF16), ((D_MODEL, D_MODEL), BF16), ((D_MODEL, D_MODEL), BF16),
                ((D_MODEL, D_MODEL), BF16), ((N_DEV, FF_BLOCK, D_MODEL), BF16), ((D_FF, D_MODEL), BF16)]
    places = [lambda ref, j: ref.at[_pos_of_dev(j)], rows_of(BRANCH_ROWS), rows_of(BRANCH_ROWS), rows_of(BRANCH_ROWS),
              lambda ref, j: ref.at[j], rows_of(DOWN_ROWS)]
    (w_in_g,) = _all_gather_async("w_in_all_gather", 9, shards[:1], gathered[:1], places[:1])
    w_in_sibling = _swap_with_sibling("w_in_from_sibling", shards[0])
    _, later = lax.optimization_barrier((w_in_sibling, shards[1:]))
    w_a, w_b, w_o, w_gu, w_dn = _all_gather_async("weights_all_gather", 0, later, gathered[1:], places[1:])

    core_i, chip_i = lax.axis_index("c"), 2 * lax.axis_index("x") + lax.axis_index("y")
    own_pos = jnp.stack([_pos_of_dev(2 * chip_i + core_i), _pos_of_dev(2 * chip_i + 1 - core_i)]).astype(jnp.int32)
    other_pos = jnp.stack([_pos_of_dev(2 * jnp.bitwise_xor(chip_i, q) + cc) for q in (1, 2, 3) for cc in (0, 1)]).astype(jnp.int32)
    proj, h, h_t = _proj_forward_own_chip(own_pos, x2d, norm_mix_g, shards[0], w_in_sibling)
    proj = _proj_forward_other_chips(other_pos, proj, h, w_in_g)
    bias_b = jnp.broadcast_to(gmlp_b_s[0][:, :, None], (GROUPS, GMLP_CHUNK, GMLP_CHUNK))
    a = _gmlp_forward(proj, gmlp_ln_g, gmlp_ln_b, gmlp_w_s[0], bias_b)
    og, o_saved, states = _hgrn_forward(proj, hgrn_lb_table, hgrn_norm_g)
    ya, yb, merged, x1, h2 = _branch_out_forward(a, og, proj, x2d, w_a, w_b, w_o, norm_ffn_g)
    gu, act, loss_tile, d_final_g, dx2, dx2b = _ffn_forward(h2, x1, w_gu, w_dn, target, final_g)

    core = lax.axis_index("c").astype(jnp.int32).reshape(1)
    chip = (2 * lax.axis_index("x") + lax.axis_index("y")).astype(jnp.int32).reshape(1)
    branch_rows, branch_shape = rows_of(BRANCH_ROWS), (BRANCH_ROWS, D_MODEL)
    branch_block = ((BRANCH_ROWS, D_MODEL), lambda q, r, c: (2 * q + c, 0))

    def chip_partials(names, grads, land, own_blocks):
        return [_chip_partial("chip_partial_" + nme, core, g_, blk, idx, l_)
                for nme, g_, (blk, idx), l_ in zip(names, grads, own_blocks, land)]

    whole = pl.BlockSpec((t, D_MODEL), lambda n: (0, 0))
    whole_t = pl.BlockSpec((D_MODEL, t), lambda n: (0, 0))
    col_blocks = [((t, D_MODEL), BF16), ((t, 256), BF16), ((D_MODEL, 256), F32)]

    def square_grad(name, a_, b_):
        return _weight_grad(name, a_, b_, whole, pl.BlockSpec((t, 256), lambda n: (0, n)), (D_MODEL, D_MODEL),
                            pl.BlockSpec((D_MODEL, 256), lambda n: (0, n)), D_MODEL // 256, col_blocks, False)

    dgu, dx1, dx1b, d_ffn_g = _ffn_backward(dx2b, dx2, gu, x1, w_gu, w_dn, norm_ffn_g)
    g_gu = _weight_grad(
        "grad_w_gate_up", dgu, h2, pl.BlockSpec((None, None, t, FF_BLOCK), lambda j: (j % 4, j // 4, 0, 0)), whole,
        (N_DEV, FF_BLOCK, D_MODEL), pl.BlockSpec((None, FF_BLOCK, D_MODEL), lambda j: (j, 0, 0)), N_DEV,
        [((t, 768), BF16), ((t, D_MODEL), BF16), ((FF_BLOCK, D_MODEL), F32)], False)
    g_dn = _weight_grad(
        "grad_w_down", act, dx2b, pl.BlockSpec((None, t, FF_BLOCK), lambda j: (j, 0, 0)), whole, (D_FF, D_MODEL),
        pl.BlockSpec((FF_BLOCK, D_MODEL), lambda j: (j, 0)), 4,
        [((t, 768), BF16), ((t, D_MODEL), BF16), ((FF_BLOCK, D_MODEL), F32)], False)
    names_f, grads_f = ["w_gate_up", "w_down"], [g_gu, g_dn]
    land_f = _exchange_sibling("ffn_grads_to_sibling", 2, grads_f, [lambda ref, j: ref.at[j], rows_of(DOWN_ROWS)],
                               [(FF_BLOCK, D_MODEL), (DOWN_ROWS, D_MODEL)])

    dx1b_later, _ = lax.optimization_barrier((dx1b, grads_f))
    dya, dyb, dproj, da, dog = _branch_out_backward(dx1b_later, ya, yb, proj, w_a, w_b, w_o)
    g_a = square_grad("grad_w_a", a, dya)
    g_b = square_grad("grad_w_b", og, dyb)
    g_o = square_grad("grad_w_out", merged, dx1b)
    names_b, grads_b = ["w_branch_a", "w_branch_b", "w_out"], [g_a, g_b, g_o]
    land_b = _exchange_sibling("branch_grads_to_sibling", 3, grads_b, [branch_rows] * 3, [branch_shape] * 3)

    part_f = chip_partials(names_f, grads_f, land_f,
                           [((None, DOWN_ROWS // 2, D_MODEL), lambda q, r, c: (2 * q + c, r, 0)),
                            ((DOWN_ROWS // 2, D_MODEL), lambda q, r, c: (2 * (2 * q + c) + r, 0))])
    landed_f = _exchange_chips("ffn_grads_to_chips", 5, part_f)

    dog, _ = lax.optimization_barrier((dog, part_f))
    dproj, d_hg_norm, d_lb = _hgrn_backward(dproj, dog, o_saved, states, proj, hgrn_lb_table, hgrn_norm_g)

    part_b = chip_partials(names_b, grads_b, land_b, [branch_block] * 3)
    landed_b = _exchange_chips("branch_grads_to_chips", 6, part_b)

    da, _ = lax.optimization_barrier((da, part_b))
    dproj, d_ln_g, d_ln_b, d_ws, d_bs = _gmlp_backward(dproj, da, proj, gmlp_ln_g, gmlp_ln_b, gmlp_w_s[0], bias_b)

    def packed(vals):
        return _pack_small(*vals)

    def flat_ws(a):
        return a.reshape(GROUPS * GMLP_CHUNK, GMLP_CHUNK)

    no_row = jnp.zeros((1, D_MODEL), F32)
    w_pack = packed([norm_mix_g, gmlp_ln_g, gmlp_ln_b, gmlp_b_s, hgrn_lb_table, hgrn_norm_g, norm_ffn_g, norm_final_g, no_row])
    m_pack = packed([m_norm_mix_g, m_gmlp_ln_g, m_gmlp_ln_b, m_gmlp_b_s, m_hgrn_lb_table, m_hgrn_norm_g, m_norm_ffn_g, m_norm_final_g, no_row])
    v_pack = packed([v_norm_mix_g, v_gmlp_ln_g, v_gmlp_ln_b, v_gmlp_b_s, v_hgrn_lb_table, v_hgrn_norm_g, v_norm_ffn_g, v_norm_final_g, no_row])
    small_partial = _pack_small(no_row, d_ln_g, d_ln_b, d_bs[:, :, 0], d_lb, d_hg_norm, d_ffn_g, d_final_g,
                                jnp.tile(loss_tile[0:1], (1, D_MODEL // 128)))
    small_all, ws_all = _all_gather_async(
        "small_grads_all_gather", 1, [small_partial, flat_ws(d_ws)],
        [((N_DEV, SMALL_ROWS, D_MODEL), F32), ((N_DEV, GROUPS * GMLP_CHUNK, GMLP_CHUNK), F32)],
        [lambda ref, j: ref.at[j], lambda ref, j: ref.at[j]])

    g_in = _weight_grad(
        "grad_w_in", h_t, dproj, whole_t, pl.BlockSpec((None, t, D_MODEL), lambda p: (p, 0, 0)), (N_DEV, D_MODEL, D_MODEL),
        pl.BlockSpec((None, D_MODEL, D_MODEL), lambda p: (p, 0, 0)), N_DEV,
        [((D_MODEL, t), BF16), ((t, D_MODEL), BF16), ((D_MODEL, D_MODEL), F32)], True)
    land_i = _exchange_sibling("w_in_grads_to_sibling", 4, [g_in], [lambda ref, j: ref.at[_pos_of_dev(j)]],
                               [(D_MODEL, D_MODEL)])

    big = {}
    for nme, own, lnd, w, m, v in zip(
            names_f + names_b, part_f + part_b, landed_f + landed_b,
            [w_gate_up, w_down, w_branch_a, w_branch_b, w_out], [m_w_gate_up, m_w_down, m_w_branch_a, m_w_branch_b, m_w_out],
            [v_w_gate_up, v_w_down, v_w_branch_a, v_w_branch_b, v_w_out]):
        flip = (lambda z: z.T) if nme == "w_gate_up" else (lambda z: z)
        big[nme] = [flip(o_)[None] for o_ in _adamw("adamw_" + nme, chip, own, lnd, flip(w[0]), flip(m[0]), flip(v[0]))]
    small_outs = _adamw_small("adamw_small", small_all, w_pack, m_pack, v_pack)
    ws_outs = _adamw_small("adamw_w_s", ws_all, flat_ws(gmlp_w_s), flat_ws(m_gmlp_w_s), flat_ws(v_gmlp_w_s))
    land_i, _ = lax.optimization_barrier((land_i, (big, small_outs, ws_outs)))
    part_i = chip_partials(["w_in"], [g_in], land_i,
                           [((None, 256, D_MODEL), lambda q, r, c: (_pos_of_dev(2 * q + c), r, 0))])
    landed_i = _exchange_chips("w_in_grads_to_chips", 7, part_i)

    dx1, _ = lax.optimization_barrier((dx1, part_i))
    grad_x, d_mix_g = _input_backward(dproj, w_in_g, x2d, dx1, norm_mix_g)
    big["w_in"] = [o_[None] for o_ in _adamw("adamw_w_in", chip, part_i[0], landed_i[0], w_in[0], m_w_in[0], v_w_in[0])]

    def row8(a):
        return jnp.pad(a, ((0, 7), (0, 0)))

    d_mix_g, _ = lax.optimization_barrier((d_mix_g, landed_i))
    (mix_all,) = _all_gather_async("mix_gain_grad_all_gather", 8, [row8(d_mix_g)], [((N_DEV, 8, D_MODEL), F32)],
                                   [lambda ref, j: ref.at[j]])
    mix_outs = _adamw_small("adamw_mix_gain", mix_all, row8(norm_mix_g), row8(m_norm_mix_g), row8(v_norm_mix_g))
    small = [dict(_unpack_small(p, ws), norm_mix_g=q[0:1]) for p, ws, q in zip(small_outs, ws_outs, mix_outs)]

    loss = small_outs[0][SMALL_ROWS - 8, 0]
    order = ["norm_mix_g", "w_in", "gmlp_ln_g", "gmlp_ln_b", "gmlp_w_s", "gmlp_b_s", "hgrn_lb_table", "hgrn_norm_g",
             "w_branch_a", "w_branch_b", "w_out", "norm_ffn_g", "w_gate_up", "w_down", "norm_final_g"]
    outs = [loss, grad_x.reshape(1, t, D_MODEL)]
    for kind in range(4):
        for nme in order:
            outs.append(big[nme][kind] if nme in big else small[kind][nme])
    return tuple(outs)
```

```python
import functools

import jax
import jax.numpy as jnp
from jax import lax
from jax.experimental import pallas as pl
from jax.experimental.pallas import tpu as pltpu
from jax.experimental.pallas import tpu_sc as plsc

F32, BF16 = jnp.float32, jnp.bfloat16
D_MODEL = 1024
N_DEV = 8
HEADS = 8
HEAD_DIM = 128
GROUPS = 8
GMLP_CHUNK = 128
HGRN_CHUNK = 64
HGRN_SCALE = HEAD_DIM ** -0.5
D_FF = 2816
FF_BLOCK = D_FF // 4
DOWN_ROWS = D_FF // N_DEV
BRANCH_ROWS = D_MODEL // N_DEV
NORM_EPS = 1e-6
ADAM_LR, ADAM_B1, ADAM_B2, ADAM_EPS, ADAM_WD, ADAM_STEP = 0.001, 0.9, 0.999, 1e-08, 0.01, 10
SMALL_ROWS = 72
V7X_VMEM_BYTES = 64 * 1024 * 1024
VMEM_CAP = V7X_VMEM_BYTES - 6 * 1024 * 1024
MESH_ID = pl.DeviceIdType.MESH
ANY = pl.BlockSpec(memory_space=pl.ANY)
RESIDENT = pl.BlockSpec(memory_space=pltpu.VMEM)
Q_POS, U_POS, GATE_POS = 0, 4, 6


def _pos_of_dev(j):
    return jnp.where(j < 2, j + 4, jnp.where(j < 6, j - 2, j))


def _dev_of_pos(p):
    return jnp.where(p < 4, p + 2, jnp.where(p < 6, p - 4, p))


def _nbytes(shape, dtype):
    n = 1
    for s in shape:
        n *= s
    return n * jnp.dtype(dtype).itemsize


def _params(blocks, scratch=(), temps=0, sem=None):
    need = 2 * sum(_nbytes(s, d) for s, d in blocks) + sum(_nbytes(s, d) for s, d in scratch) + temps
    assert need + (4 << 20) <= VMEM_CAP, need
    return pltpu.CompilerParams(dimension_semantics=sem, vmem_limit_bytes=VMEM_CAP)


def _tile(n, pref):
    return pref if n % pref == 0 else n


def _dot(a, b):
    return jnp.dot(a, b, preferred_element_type=F32)


def _dot_nt(a, b):
    return lax.dot_general(a, b, (((1,), (1,)), ((), ())), preferred_element_type=F32)


def _dot_tn(a, b):
    return lax.dot_general(a, b, (((0,), (0,)), ((), ())), preferred_element_type=F32)


def _sigmoid(x):
    return 1.0 / (1.0 + jnp.exp(-x))


_GELU_C = 0.7978845608028654


def _gelu(x):
    return x * (0.5 * (1.0 + jnp.tanh(_GELU_C * (x + 0.044715 * (x * x * x)))))


def _gelu_grad(x):
    t = jnp.tanh(_GELU_C * (x + 0.044715 * (x * x * x)))
    return 0.5 * (1.0 + t) + 0.5 * x * (1.0 - t * t) * (_GELU_C * (1.0 + 3.0 * 0.044715 * x * x))


def _rms_stats(x):
    r = lax.rsqrt(jnp.mean(x * x, axis=-1, keepdims=True) + NORM_EPS)
    return r, x * r


def _rms_bwd(dy, x, g):
    r, xh = _rms_stats(x)
    dg = jnp.sum(dy * xh, axis=0, keepdims=True)
    dxh = dy * g
    dx = r * (dxh - xh * jnp.mean(dxh * xh, axis=-1, keepdims=True))
    return dx, dg


def _split3(x):
    hi = x.astype(BF16)
    r = x - hi.astype(F32)
    mid = r.astype(BF16)
    lo = (r - mid.astype(F32)).astype(BF16)
    return hi, mid, lo


def _mask_mm(mask_bf16, x):
    hi, mid, lo = _split3(x)
    return _dot(mask_bf16, hi) + _dot(mask_bf16, mid) + _dot(mask_bf16, lo)


def _place():
    return lax.axis_index("x"), lax.axis_index("y"), lax.axis_index("c")


def _gather_copies(src, out, send, recv, loc, slicers):
    n = len(src)
    x, y, c = _place()
    me, sib = (x, y, c), (x, y, 1 - c)
    chips = [(1 - x, y), (x, 1 - y), (1 - x, 1 - y)]

    def dev(p):
        return 4 * p[0] + 2 * p[1] + p[2]

    def rc(i, k, block, to, from_src=False):
        dst = slicers[i](out[i], dev(block))
        return pltpu.make_async_remote_copy(
            src_ref=src[i] if from_src else dst, dst_ref=dst, send_sem=send.at[7 * i + k],
            recv_sem=recv.at[7 * i + k], device_id=to, device_id_type=MESH_ID)

    mine = [pltpu.make_async_copy(src[i], slicers[i](out[i], dev(me)), loc.at[i]) for i in range(n)]
    for cp in mine:
        cp.start()
    first = []
    for i in range(n):
        first.append(rc(i, 0, me, sib, True))
        for j, chip in enumerate(chips):
            first.append(rc(i, 1 + j, me, (*chip, c), True))
    for cp in first:
        cp.start()
    passed = []
    for j, chip in enumerate(chips):
        for i in range(n):
            rc(i, 1 + j, (*chip, c), me).wait_recv()
            cp = rc(i, 4 + j, (*chip, c), sib)
            cp.start()
            passed.append(cp)
    for i in range(n):
        rc(i, 0, sib, me).wait_recv()
        for j, chip in enumerate(chips):
            rc(i, 4 + j, (*chip, 1 - c), me).wait_recv()
    for cp in first + passed:
        cp.wait_send()
    for cp in mine:
        cp.wait()


def _gather_copies_balanced(src, out, send, recv, loc, slicer, rows):
    x, y, c = _place()
    me, sib = (x, y, c), (x, y, 1 - c)
    xn, yn, dg = (1 - x, y), (x, 1 - y), (1 - x, 1 - y)
    half_rows = rows // 2

    def block(p):
        return slicer(out, 4 * p[0] + 2 * p[1] + p[2])

    def half(ref, h):
        return ref.at[pl.ds(h * half_rows, half_rows)]

    def rc(k, dst, to, from_src=False):
        return pltpu.make_async_remote_copy(src_ref=src if from_src else dst, dst_ref=dst, send_sem=send.at[k],
                                            recv_sem=recv.at[k], device_id=to, device_id_type=MESH_ID)

    mine = pltpu.make_async_copy(src, block(me), loc.at[0])
    mine.start()
    sends = [rc(0, block(me), sib, True), rc(1, block(me), (*xn, c), True), rc(2, block(me), (*yn, c), True)]
    for cp in sends:
        cp.start()

    def then(cp):
        cp.start()
        sends.append(cp)

    rc(1, block((*xn, c)), me).wait_recv()
    then(rc(3, half(block((*xn, c)), 0), (*yn, c)))
    then(rc(5, block((*xn, c)), sib))
    rc(2, block((*yn, c)), me).wait_recv()
    then(rc(4, half(block((*yn, c)), 1), (*xn, c)))
    then(rc(6, block((*yn, c)), sib))
    rc(3, half(block((*dg, c)), 0), me).wait_recv()
    then(rc(7, half(block((*dg, c)), 0), sib))
    rc(4, half(block((*dg, c)), 1), me).wait_recv()
    then(rc(8, half(block((*dg, c)), 1), sib))
    rc(0, block(sib), me).wait_recv()
    rc(5, block((*xn, 1 - c)), me).wait_recv()
    rc(6, block((*yn, 1 - c)), me).wait_recv()
    rc(7, half(block((*dg, 1 - c)), 0), me).wait_recv()
    rc(8, half(block((*dg, 1 - c)), 1), me).wait_recv()
    for cp in sends:
        cp.wait_send()
    mine.wait()


def _gather_scratch(n):
    return [pltpu.SemaphoreType.DMA((7 * n,)), pltpu.SemaphoreType.DMA((7 * n,)), pltpu.SemaphoreType.DMA((n,))]


def _handshake(peers):
    barrier = pltpu.get_barrier_semaphore()
    for peer in peers:
        pl.semaphore_signal(barrier, inc=1, device_id=peer, device_id_type=MESH_ID)
    pl.semaphore_wait(barrier, len(peers))


def _all_gather_async(name, collective_id, srcs, out_shapes, slicers):
    n = len(srcs)

    def body(*refs):
        x, y, c = _place()
        _handshake([(1 - x if dx else x, 1 - y if dy else y, 1 - c if dc else c)
                    for dx in (0, 1) for dy in (0, 1) for dc in (0, 1) if dx or dy or dc])
        _gather_copies(refs[:n], refs[n:2 * n], *refs[2 * n:], slicers)

    return _sequencer_call(name, collective_id, body, srcs, [jax.ShapeDtypeStruct(s, d) for s, d in out_shapes],
                           _gather_scratch(n))


def _all_gather_balanced_async(name, collective_id, src, out_shape, slicer, rows):
    def body(src_ref, out_ref, send, recv, loc):
        x, y, c = _place()
        _handshake([(1 - x if dx else x, 1 - y if dy else y, 1 - c if dc else c)
                    for dx in (0, 1) for dy in (0, 1) for dc in (0, 1) if dx or dy or dc])
        _gather_copies_balanced(src_ref, out_ref, send, recv, loc, slicer, rows)

    return _sequencer_call(name, collective_id, body, [src], [jax.ShapeDtypeStruct(*out_shape)],
                           [pltpu.SemaphoreType.DMA((9,)), pltpu.SemaphoreType.DMA((9,)), pltpu.SemaphoreType.DMA((1,))])[0]


def _sequencer_call(name, collective_id, body, operands, out_types, scratch):
    return pl.kernel(
        body, out_type=out_types, mesh=plsc.ScalarSubcoreMesh(axis_name="sequencer", num_cores=1), name=name,
        scratch_types=scratch, compiler_params=pltpu.CompilerParams(collective_id=collective_id),
    )(*operands)


def _exchange_sibling(name, collective_id, grads, shard_fns, shard_shapes):
    n = len(grads)

    def body(*refs):
        g, land = refs[:n], refs[n:2 * n]
        send, recv = refs[2 * n:]
        x, y, c = _place()
        _handshake([(x, y, 1 - c)])
        remote = []
        for i in range(n):
            for q in range(4):
                cp = pltpu.make_async_remote_copy(
                    src_ref=shard_fns[i](g[i], 2 * q + (1 - c)), dst_ref=land[i].at[q], send_sem=send.at[4 * i + q],
                    recv_sem=recv.at[4 * i + q], device_id=(x, y, 1 - c), device_id_type=MESH_ID)
                cp.start()
                remote.append(cp)
        for cp in remote:
            cp.wait()

    return _sequencer_call(name, collective_id, body, grads, [jax.ShapeDtypeStruct((4, *s), F32) for s in shard_shapes],
                           [pltpu.SemaphoreType.DMA((4 * n,)), pltpu.SemaphoreType.DMA((4 * n,))])


def _exchange_chips(name, collective_id, parts):
    n = len(parts)

    def body(*refs):
        part, out = refs[:n], refs[n:2 * n]
        send, recv = refs[2 * n:]
        x, y, c = _place()
        _handshake([(1 - x, y, c), (x, 1 - y, c), (1 - x, 1 - y, c)])
        remote = []
        for i in range(n):
            for s in range(3):
                qx = 1 - x if (s + 1) // 2 else x
                qy = 1 - y if (s + 1) % 2 else y
                cp = pltpu.make_async_remote_copy(
                    src_ref=part[i].at[2 * qx + qy], dst_ref=out[i].at[s], send_sem=send.at[3 * i + s],
                    recv_sem=recv.at[3 * i + s], device_id=(qx, qy, c), device_id_type=MESH_ID)
                cp.start()
                remote.append(cp)
        for cp in remote:
            cp.wait()

    return _sequencer_call(name, collective_id, body, parts,
                           [jax.ShapeDtypeStruct((3, *p.shape[1:]), p.dtype) for p in parts],
                           [pltpu.SemaphoreType.DMA((3 * n,)), pltpu.SemaphoreType.DMA((3 * n,))])


def _chip_partial(name, core, grad, own_block, own_index, land):
    _, rows, cols = land.shape
    tr = own_block[-2]

    def body(core_ref, a_ref, b_ref, o_ref):
        o_ref[...] = (a_ref[...] + b_ref[...]).astype(BF16)

    spec = pl.BlockSpec((None, tr, cols), lambda q, r, c: (q, r, 0))
    return pl.pallas_call(
        body, name=name, out_shape=jax.ShapeDtypeStruct(land.shape, BF16),
        grid_spec=pltpu.PrefetchScalarGridSpec(
            num_scalar_prefetch=1, grid=(4, rows // tr),
            in_specs=[pl.BlockSpec(own_block, lambda q, r, c: own_index(q, r, c[0])), spec], out_specs=spec),
        compiler_params=_params([((tr, cols), F32)] * 2 + [((tr, cols), BF16)], sem=("arbitrary", "arbitrary")),
    )(core, grad, land)


def _adamw_math(w, g, m, v):
    m = ADAM_B1 * m + (1.0 - ADAM_B1) * g
    v = ADAM_B2 * v + (1.0 - ADAM_B2) * (g * g)
    m_hat = m / (1.0 - ADAM_B1 ** ADAM_STEP)
    v_hat = v / (1.0 - ADAM_B2 ** ADAM_STEP)
    delta = -ADAM_LR * (m_hat / (jnp.sqrt(v_hat) + ADAM_EPS) + ADAM_WD * w)
    return delta, m, v


def _adamw(name, chip, own, landed, w, m, v):
    _, rows, cols = own.shape
    tr = _tile(rows, 256) if rows % 256 == 0 else _tile(rows, 176)

    def body(chip_ref, own_ref, l_ref, w_ref, m_ref, v_ref, g_out, d_out, m_out, v_out):
        g = own_ref[...].astype(F32)
        for s in range(3):
            g = g + l_ref[s].astype(F32)
        delta, m_new, v_new = _adamw_math(w_ref[...], g, m_ref[...], v_ref[...])
        g_out[...] = g
        d_out[...] = delta
        m_out[...] = m_new
        v_out[...] = v_new

    spec = pl.BlockSpec((tr, cols), lambda r, c: (r, 0))
    return pl.pallas_call(
        body, name=name, out_shape=[jax.ShapeDtypeStruct((rows, cols), F32)] * 4,
        grid_spec=pltpu.PrefetchScalarGridSpec(
            num_scalar_prefetch=1, grid=(rows // tr,),
            in_specs=[pl.BlockSpec((None, tr, cols), lambda r, c: (c[0], r, 0)),
                      pl.BlockSpec((3, tr, cols), lambda r, c: (0, r, 0)), spec, spec, spec],
            out_specs=[spec] * 4),
        compiler_params=_params([((4, tr, cols), own.dtype)] + [((tr, cols), F32)] * 7, sem=("arbitrary",)),
    )(chip, own, landed, w, m, v)


def _swap_with_sibling(name, x):
    def body(x_ref, o_ref, send, recv):
        px, py, c = _place()
        cp = pltpu.make_async_remote_copy(src_ref=x_ref, dst_ref=o_ref, send_sem=send, recv_sem=recv,
                                          device_id=(px, py, 1 - c), device_id_type=MESH_ID)
        cp.start()
        cp.wait()

    return pl.pallas_call(
        body, name=name, out_shape=jax.ShapeDtypeStruct(x.shape, x.dtype), in_specs=[ANY], out_specs=ANY,
        scratch_shapes=[pltpu.SemaphoreType.DMA, pltpu.SemaphoreType.DMA],
    )(x)


def _proj_forward_own_chip(positions, x, gain, w_own, w_sibling):
    t = x.shape[0]
    tm = _tile(t, 1024)

    def body(pos_ref, x_ref, g_ref, wo_ref, ws_ref, o_ref, h_ref, ht_ref):
        @pl.when(pl.program_id(1) == 0)
        def _():
            _, xh = _rms_stats(x_ref[...])
            h = (xh * g_ref[...]).astype(BF16)
            h_ref[...] = h
            ht_ref[...] = h.T
            o_ref[...] = _dot(h, wo_ref[...])

        @pl.when(pl.program_id(1) == 1)
        def _():
            o_ref[...] = _dot(h_ref[...], ws_ref[...])

    tok = pl.BlockSpec((tm, D_MODEL), lambda m, k, pos: (m, 0))
    return pl.pallas_call(
        body, name="proj_fwd_own_chip",
        out_shape=[jax.ShapeDtypeStruct((N_DEV, t, D_MODEL), F32), jax.ShapeDtypeStruct((t, D_MODEL), BF16),
                   jax.ShapeDtypeStruct((D_MODEL, t), BF16)],
        grid_spec=pltpu.PrefetchScalarGridSpec(
            num_scalar_prefetch=1, grid=(t // tm, 2),
            in_specs=[tok, pl.BlockSpec((1, D_MODEL), lambda m, k, pos: (0, 0)), RESIDENT, RESIDENT],
            out_specs=[pl.BlockSpec((None, tm, D_MODEL), lambda m, k, pos: (pos[k], m, 0)), tok,
                       pl.BlockSpec((D_MODEL, tm), lambda m, k, pos: (0, m))]),
        compiler_params=_params([((tm, D_MODEL), F32)] * 2 + [((tm, D_MODEL), BF16)] * 2,
                                scratch=[((2, D_MODEL, D_MODEL), BF16)], temps=6 << 20, sem=("arbitrary", "arbitrary")),
    )(positions, x, gain, w_own, w_sibling)


def _proj_forward_other_chips(positions, proj, h, w_in_g):
    t = h.shape[0]
    tm = _tile(t, 1024)

    def body(pos_ref, _, h_ref, w_ref, o_ref):
        o_ref[...] = _dot(h_ref[...], w_ref[pos_ref[pl.program_id(1)]])

    return pl.pallas_call(
        body, name="proj_fwd_other_chips", out_shape=jax.ShapeDtypeStruct(proj.shape, F32),
        grid_spec=pltpu.PrefetchScalarGridSpec(
            num_scalar_prefetch=1, grid=(t // tm, N_DEV - 2),
            in_specs=[ANY, pl.BlockSpec((tm, D_MODEL), lambda m, k, pos: (m, 0)), RESIDENT],
            out_specs=pl.BlockSpec((None, tm, D_MODEL), lambda m, k, pos: (pos[k], m, 0))),
        input_output_aliases={1: 0},
        compiler_params=_params([((tm, D_MODEL), F32), ((tm, D_MODEL), BF16)], scratch=[((N_DEV, D_MODEL, D_MODEL), BF16)],
                                temps=6 << 20, sem=("arbitrary", "arbitrary")),
    )(positions, proj, h, w_in_g)


def _masked_ws(ws_ref, g):
    row = lax.broadcasted_iota(jnp.int32, (GMLP_CHUNK, GMLP_CHUNK), 0)
    col = lax.broadcasted_iota(jnp.int32, (GMLP_CHUNK, GMLP_CHUNK), 1)
    return jnp.where(row >= col, ws_ref[g], 0.0).astype(BF16)


def _gmlp_forward(proj, ln_g, ln_b, w_s, bias_b):
    t = proj.shape[1]
    tm = _tile(t, 256)
    chunks = tm // GMLP_CHUNK

    def body(u_ref, v_ref, lng_ref, lnb_ref, ws_ref, bias_ref, a_ref, vn_scr):
        vv = _gelu(v_ref[...])
        mu = jnp.mean(vv, axis=-1, keepdims=True)
        cen = vv - mu
        var = jnp.mean(cen * cen, axis=-1, keepdims=True)
        vn_scr[...] = ((cen * lax.rsqrt(var + NORM_EPS)) * lng_ref[...] + lnb_ref[...]).astype(BF16)
        for g in range(GROUPS):
            wm = _masked_ws(ws_ref, g)
            cols = slice(g * HEAD_DIM, (g + 1) * HEAD_DIM)
            for c in range(chunks):
                rows = slice(c * GMLP_CHUNK, (c + 1) * GMLP_CHUNK)
                mixed = _dot(wm, vn_scr[rows, cols]) + bias_ref[g]
                a_ref[rows, cols] = (_gelu(u_ref[rows, cols]) * mixed).astype(BF16)

    small = pl.BlockSpec((GROUPS, GMLP_CHUNK, GMLP_CHUNK), lambda m: (0, 0, 0))
    vec = pl.BlockSpec((1, D_MODEL), lambda m: (0, 0))
    return pl.pallas_call(
        body, name="gmlp_fwd", out_shape=jax.ShapeDtypeStruct((t, D_MODEL), BF16), grid=(t // tm,),
        in_specs=[pl.BlockSpec((None, tm, D_MODEL), lambda m: (U_POS, m, 0)),
                  pl.BlockSpec((None, tm, D_MODEL), lambda m: (U_POS + 1, m, 0)), vec, vec, small, small],
        out_specs=pl.BlockSpec((tm, D_MODEL), lambda m: (m, 0)),
        scratch_shapes=[pltpu.VMEM((tm, D_MODEL), BF16)],
        compiler_params=_params([((tm, D_MODEL), F32)] * 2 + [((tm, D_MODEL), BF16)] + [((8, 128, 128), F32)] * 2,
                                scratch=[((tm, D_MODEL), BF16)], temps=8 << 20, sem=("arbitrary",)),
    )(proj, proj, ln_g, ln_b, w_s, bias_b)


def _lower_bound(tab_ref):
    t0, t1 = tab_ref[0:1, :], tab_ref[1:2, :]
    mx = jnp.maximum(t0, t1)
    e0, e1 = jnp.exp(t0 - mx), jnp.exp(t1 - mx)
    return e0 / (e0 + e1)


def _tri_masks():
    row = lax.broadcasted_iota(jnp.int32, (HGRN_CHUNK, HGRN_CHUNK), 0)
    col = lax.broadcasted_iota(jnp.int32, (HGRN_CHUNK, HGRN_CHUNK), 1)
    return row >= col, row <= col


def _chunk_rows(c):
    return slice(c * HGRN_CHUNK, (c + 1) * HGRN_CHUNK)


def _per_chunk(x, nc, fn):
    return jnp.concatenate([fn(x[_chunk_rows(c)]) for c in range(nc)], axis=0)


def _chunk_row_bcast(x, nc, i):
    return _per_chunk(x, nc, lambda xc: jnp.broadcast_to(xc[i:i + 1, :], (HGRN_CHUNK, HEAD_DIM)))


def _hgrn_gates(q, fl, lb, nc):
    lower, _ = _tri_masks()
    lower = lower.astype(BF16)
    s = _sigmoid(fl)
    f = lb + (1.0 - lb) * s
    k = 1.0 - f
    hi, mid, lo = _split3(jnp.log(f))
    a = jnp.concatenate([_dot(lower, hi[_chunk_rows(c)]) + _dot(lower, mid[_chunk_rows(c)]) + _dot(lower, lo[_chunk_rows(c)])
                         for c in range(nc)], axis=0)
    a_mid = _chunk_row_bcast(a, nc, HGRN_CHUNK // 2 - 1)
    a_last = _chunk_row_bcast(a, nc, HGRN_CHUNK - 1)
    qs = q * HGRN_SCALE
    e_in, e_out, e_end, e_all = jnp.exp(a - a_mid), jnp.exp(a_mid - a), jnp.exp(a_last - a), jnp.exp(a)
    decay = [jnp.exp(a[c * HGRN_CHUNK + HGRN_CHUNK - 1:(c + 1) * HGRN_CHUNK, :]) for c in range(nc)]
    return dict(s=s, f=f, k=k, decay=decay, e_in=e_in, e_out=e_out, e_end=e_end, e_all=e_all,
                qi=qs * e_in, ki=k * e_out, kd=k * e_end, qe=qs * e_all)


def _hgrn_forward(proj, lb_table, norm_g):
    t = proj.shape[1]
    tb = _tile(t, 1024)
    nc = tb // HGRN_CHUNK
    n_chunks = t // HGRN_CHUNK

    def body(q_ref, f_ref, i_ref, g_ref, tab_ref, ng_ref, og_ref, o_ref, st_ref, state):
        @pl.when(pl.program_id(1) == 0)
        def _():
            state[...] = jnp.zeros_like(state)

        lower, _ = _tri_masks()
        gt = _hgrn_gates(q_ref[...], f_ref[...], _lower_bound(tab_ref), nc)
        qi, ki, kd, qe = (gt[n].astype(BF16) for n in ("qi", "ki", "kd", "qe"))
        vb = i_ref[...].astype(BF16)
        o_intra, d_state = [], []
        for c in range(nc):
            rows = _chunk_rows(c)
            p = jnp.where(lower, _dot_nt(qi[rows], ki[rows]), 0.0).astype(BF16)
            o_intra.append(_dot(p, vb[rows]))
            d_state.append(_dot_tn(vb[rows], kd[rows]))
        st = state[...]
        outs = []
        for c in range(nc):
            st_ref[c] = st
            outs.append(o_intra[c] + _dot_nt(qe[_chunk_rows(c)], st.astype(BF16)))
            st = st * gt["decay"][c] + d_state[c]
        state[...] = st
        o = jnp.concatenate(outs, axis=0)
        o_ref[...] = o
        _, oh = _rms_stats(o)
        gz = g_ref[...]
        og_ref[...] = ((oh * ng_ref[...]) * (gz * _sigmoid(gz))).astype(BF16)

    def blk(p):
        return pl.BlockSpec((None, tb, HEAD_DIM), lambda h, n: (p, n, h))

    out_blk = pl.BlockSpec((tb, HEAD_DIM), lambda h, n: (n, h))
    return pl.pallas_call(
        body, name="hgrn_fwd",
        out_shape=[jax.ShapeDtypeStruct((t, D_MODEL), BF16), jax.ShapeDtypeStruct((t, D_MODEL), F32),
                   jax.ShapeDtypeStruct((HEADS, n_chunks, HEAD_DIM, HEAD_DIM), F32)],
        grid=(HEADS, t // tb),
        in_specs=[blk(Q_POS), blk(Q_POS + 1), blk(Q_POS + 2), blk(Q_POS + 3),
                  pl.BlockSpec((2, HEAD_DIM), lambda h, n: (0, h)), pl.BlockSpec((1, HEAD_DIM), lambda h, n: (0, h))],
        out_specs=[out_blk, out_blk, pl.BlockSpec((None, nc, HEAD_DIM, HEAD_DIM), lambda h, n: (h, n, 0, 0))],
        scratch_shapes=[pltpu.VMEM((HEAD_DIM, HEAD_DIM), F32)],
        compiler_params=_params([((tb, HEAD_DIM), F32)] * 6 + [((nc, HEAD_DIM, HEAD_DIM), F32)], temps=8 << 20,
                                sem=("arbitrary", "arbitrary")),
    )(proj, proj, proj, proj, lb_table, norm_g)


def _branch_out_forward(a, og, proj, x, w_a, w_b, w_out, ffn_g):
    t = x.shape[0]
    tm = _tile(t, 256)

    def body(a_ref, og_ref, ga_ref, gb_ref, x_ref, wa_ref, wb_ref, wo_ref, g_ref, ya_ref, yb_ref, mg_ref, x1_ref, h2_ref):
        ya = _dot(a_ref[...], wa_ref[...])
        yb = _dot(og_ref[...], wb_ref[...])
        ya_ref[...] = ya
        yb_ref[...] = yb
        merged = (_sigmoid(ga_ref[...]) * ya + _sigmoid(gb_ref[...]) * yb).astype(BF16)
        mg_ref[...] = merged
        x1 = x_ref[...] + _dot(merged, wo_ref[...])
        x1_ref[...] = x1
        _, xh = _rms_stats(x1)
        h2_ref[...] = (xh * g_ref[...]).astype(BF16)

    tok = pl.BlockSpec((tm, D_MODEL), lambda m: (m, 0))
    wsp = pl.BlockSpec((D_MODEL, D_MODEL), lambda m: (0, 0))
    return pl.pallas_call(
        body, name="branch_out_fwd",
        out_shape=[jax.ShapeDtypeStruct((t, D_MODEL), F32), jax.ShapeDtypeStruct((t, D_MODEL), F32),
                   jax.ShapeDtypeStruct((t, D_MODEL), BF16), jax.ShapeDtypeStruct((t, D_MODEL), F32),
                   jax.ShapeDtypeStruct((t, D_MODEL), BF16)],
        grid=(t // tm,),
        in_specs=[tok, tok, pl.BlockSpec((None, tm, D_MODEL), lambda m: (GATE_POS, m, 0)),
                  pl.BlockSpec((None, tm, D_MODEL), lambda m: (GATE_POS + 1, m, 0)), tok, wsp, wsp, wsp,
                  pl.BlockSpec((1, D_MODEL), lambda m: (0, 0))],
        out_specs=[tok] * 5,
        compiler_params=_params([((tm, D_MODEL), BF16)] * 4 + [((tm, D_MODEL), F32)] * 6 + [((D_MODEL, D_MODEL), BF16)] * 3,
                                temps=8 << 20, sem=("arbitrary",)),
    )(a, og, proj, proj, x, w_a, w_b, w_out, ffn_g)


def _ffn_forward(h2, x1, w_gu, w_down, target, final_g):
    t = x1.shape[0]
    tm = _tile(t, 512)

    def body(h_ref, wgu_ref, wd_ref, x1_ref, t_ref, g_ref, gu_ref, act_ref, loss_ref, dg_ref, dx_ref, dxb_ref, acc):
        m, j = pl.program_id(0), pl.program_id(1)

        @pl.when((m == 0) & (j == 0))
        def _():
            loss_ref[...] = jnp.zeros_like(loss_ref)
            dg_ref[...] = jnp.zeros_like(dg_ref)

        h = h_ref[...]
        gate = _dot_nt(h, wgu_ref[j])
        up = _dot_nt(h, wgu_ref[j + 4])
        gu_ref[0] = gate
        gu_ref[1] = up
        act = ((gate * _sigmoid(gate)) * up).astype(BF16)
        act_ref[...] = act
        part = _dot(act, wd_ref[j])

        @pl.when(j == 0)
        def _():
            acc[...] = part

        @pl.when((j > 0) & (j < 3))
        def _():
            acc[...] += part

        @pl.when(j == 3)
        def _():
            x2 = x1_ref[...] + (acc[...] + part)
            g = g_ref[...]
            r, xh = _rms_stats(x2)
            err = xh * g - t_ref[...]
            loss_ref[...] += 0.5 * jnp.sum(jnp.mean(err * err, axis=-1, keepdims=True), axis=0, keepdims=True)
            dy = err * (1.0 / D_MODEL)
            dg_ref[...] += jnp.sum(dy * xh, axis=0, keepdims=True)
            dxh = dy * g
            dx = r * (dxh - xh * jnp.mean(dxh * xh, axis=-1, keepdims=True))
            dx_ref[...] = dx
            dxb_ref[...] = dx.astype(BF16)

    tok = pl.BlockSpec((tm, D_MODEL), lambda m, j: (m, 0))
    vec = pl.BlockSpec((1, D_MODEL), lambda m, j: (0, 0))
    return pl.pallas_call(
        body, name="ffn_fwd",
        out_shape=[jax.ShapeDtypeStruct((4, 2, t, FF_BLOCK), F32), jax.ShapeDtypeStruct((4, t, FF_BLOCK), BF16),
                   jax.ShapeDtypeStruct((8, 128), F32), jax.ShapeDtypeStruct((1, D_MODEL), F32),
                   jax.ShapeDtypeStruct((t, D_MODEL), F32), jax.ShapeDtypeStruct((t, D_MODEL), BF16)],
        grid=(t // tm, 4),
        in_specs=[tok, RESIDENT, RESIDENT, tok, tok, vec],
        out_specs=[pl.BlockSpec((None, 2, tm, FF_BLOCK), lambda m, j: (j, 0, m, 0)),
                   pl.BlockSpec((None, tm, FF_BLOCK), lambda m, j: (j, m, 0)),
                   pl.BlockSpec((8, 128), lambda m, j: (0, 0)), vec, tok, tok],
        scratch_shapes=[pltpu.VMEM((tm, D_MODEL), F32)],
        compiler_params=_params([((tm, D_MODEL), BF16), ((tm, D_MODEL), F32), ((tm, D_MODEL), F32), ((2, tm, 768), F32),
                                 ((tm, 768), BF16), ((tm, D_MODEL), F32), ((tm, D_MODEL), BF16)],
                                scratch=[((tm, D_MODEL), F32), ((N_DEV, FF_BLOCK, D_MODEL), BF16), ((D_FF, D_MODEL), BF16)],
                                temps=6 << 20, sem=("arbitrary", "arbitrary")),
    )(h2, w_gu, w_down.reshape(4, FF_BLOCK, D_MODEL), x1, target, final_g)


def _ffn_backward(dx2b, dx2, gu, x1, w_gu, w_down, ffn_g):
    t = x1.shape[0]
    tm = _tile(t, 512)

    def body(dxb_ref, dx2_ref, gu_ref, x1_ref, wgu_ref, wd_ref, g_ref, dgu_ref, dx1_ref, dx1b_ref, dg_ref, acc, prev):
        m, j = pl.program_id(0), pl.program_id(1)

        @pl.when((m == 0) & (j == 0))
        def _():
            dg_ref[...] = jnp.zeros_like(dg_ref)

        @pl.when(j == 0)
        def _():
            prev[...] = jnp.zeros_like(prev)
            acc[...] = jnp.zeros_like(acc)

        jm1 = jnp.maximum(j - 1, 0)
        acc[...] += _dot(prev[0], wgu_ref[jm1]) + _dot(prev[1], wgu_ref[jm1 + 4])
        dact = _dot_nt(dxb_ref[...], wd_ref[j])
        gate, up = gu_ref[0], gu_ref[1]
        sg = _sigmoid(gate)
        dgate = (dact * up * (sg * (1.0 + gate * (1.0 - sg)))).astype(BF16)
        dup = (dact * (gate * sg)).astype(BF16)
        dgu_ref[0] = dgate
        dgu_ref[1] = dup
        prev[0] = dgate
        prev[1] = dup

        @pl.when(j == 3)
        def _():
            dh2 = acc[...] + (_dot(prev[0], wgu_ref[3]) + _dot(prev[1], wgu_ref[7]))
            dx, dg = _rms_bwd(dh2, x1_ref[...], g_ref[...])
            dx1 = dx2_ref[...] + dx
            dx1_ref[...] = dx1
            dx1b_ref[...] = dx1.astype(BF16)
            dg_ref[...] += dg

    tok = pl.BlockSpec((tm, D_MODEL), lambda m, j: (m, 0))
    vec = pl.BlockSpec((1, D_MODEL), lambda m, j: (0, 0))
    gu_spec = pl.BlockSpec((None, 2, tm, FF_BLOCK), lambda m, j: (j, 0, m, 0))
    return pl.pallas_call(
        body, name="ffn_bwd",
        out_shape=[jax.ShapeDtypeStruct((4, 2, t, FF_BLOCK), BF16), jax.ShapeDtypeStruct((t, D_MODEL), F32),
                   jax.ShapeDtypeStruct((t, D_MODEL), BF16), jax.ShapeDtypeStruct((1, D_MODEL), F32)],
        grid=(t // tm, 4),
        in_specs=[tok, tok, gu_spec, tok, RESIDENT, RESIDENT, vec],
        out_specs=[gu_spec, tok, tok, vec],
        scratch_shapes=[pltpu.VMEM((tm, D_MODEL), F32), pltpu.VMEM((2, tm, FF_BLOCK), BF16)],
        compiler_params=_params([((tm, D_MODEL), BF16), ((tm, D_MODEL), F32), ((2, tm, 768), F32), ((tm, D_MODEL), F32),
                                 ((2, tm, 768), BF16), ((tm, D_MODEL), F32), ((tm, D_MODEL), BF16)],
                                scratch=[((tm, D_MODEL), F32), ((2, tm, 768), BF16), ((N_DEV, FF_BLOCK, D_MODEL), BF16),
                                         ((D_FF, D_MODEL), BF16)],
                                temps=4 << 20, sem=("arbitrary", "arbitrary")),
    )(dx2b, dx2, gu, x1, w_gu, w_down.reshape(4, FF_BLOCK, D_MODEL), ffn_g)


def _branch_out_backward(dx1b, ya, yb, proj, w_a, w_b, w_out):
    t = ya.shape[0]
    tm = _tile(t, 256)

    def body(dx_ref, ya_ref, yb_ref, ga_ref, gb_ref, wa_ref, wb_ref, wo_ref, dya_ref, dyb_ref, dgate_ref, da_ref, dog_ref):
        dm = _dot_nt(dx_ref[...], wo_ref[...])
        sa, sb = _sigmoid(ga_ref[...]), _sigmoid(gb_ref[...])
        dya = (dm * sa).astype(BF16)
        dyb = (dm * sb).astype(BF16)
        dya_ref[...] = dya
        dyb_ref[...] = dyb
        dgate_ref[0] = (dm * ya_ref[...] * (sa * (1.0 - sa))).astype(BF16)
        dgate_ref[1] = (dm * yb_ref[...] * (sb * (1.0 - sb))).astype(BF16)
        da_ref[...] = _dot_nt(dya, wa_ref[...])
        dog_ref[...] = _dot_nt(dyb, wb_ref[...])

    tok = pl.BlockSpec((tm, D_MODEL), lambda m: (m, 0))
    wsp = pl.BlockSpec((D_MODEL, D_MODEL), lambda m: (0, 0))
    return pl.pallas_call(
        body, name="branch_out_bwd",
        out_shape=[jax.ShapeDtypeStruct((t, D_MODEL), BF16), jax.ShapeDtypeStruct((t, D_MODEL), BF16),
                   jax.ShapeDtypeStruct((N_DEV, t, D_MODEL), BF16), jax.ShapeDtypeStruct((t, D_MODEL), F32),
                   jax.ShapeDtypeStruct((t, D_MODEL), F32)],
        grid=(t // tm,),
        in_specs=[tok, tok, tok, pl.BlockSpec((None, tm, D_MODEL), lambda m: (GATE_POS, m, 0)),
                  pl.BlockSpec((None, tm, D_MODEL), lambda m: (GATE_POS + 1, m, 0)), wsp, wsp, wsp],
        out_specs=[tok, tok, pl.BlockSpec((2, tm, D_MODEL), lambda m: (GATE_POS // 2, m, 0)), tok, tok],
        compiler_params=_params([((tm, D_MODEL), BF16)] * 5 + [((tm, D_MODEL), F32)] * 6 + [((D_MODEL, D_MODEL), BF16)] * 3,
                                temps=8 << 20, sem=("arbitrary",)),
    )(dx1b, ya, yb, proj, proj, w_a, w_b, w_out)


def _hgrn_backward(dproj, dog, o_saved, states, proj, lb_table, norm_g):
    t = proj.shape[1]
    tb = _tile(t, 1024)
    nc = tb // HGRN_CHUNK
    nb = t // tb

    def body(_, dog_ref, o_ref, st_ref, q_ref, f_ref, i_ref, g_ref, tab_ref, ng_ref, dp_ref, dng_ref, dtab_ref, gstate):
        @pl.when(pl.program_id(1) == 0)
        def _():
            gstate[...] = jnp.zeros_like(gstate)
            dng_ref[...] = jnp.zeros_like(dng_ref)
            dtab_ref[...] = jnp.zeros_like(dtab_ref)

        lb = _lower_bound(tab_ref)
        ng = ng_ref[...]
        lower, upper = _tri_masks()
        gt = _hgrn_gates(q_ref[...], f_ref[...], lb, nc)
        qi, ki, kd, qe = (gt[n].astype(BF16) for n in ("qi", "ki", "kd", "qe"))
        vb = i_ref[...].astype(BF16)
        o, gz, d_og = o_ref[...], g_ref[...], dog_ref[...]
        r, oh = _rms_stats(o)
        sg = _sigmoid(gz)
        d_on = d_og * (gz * sg)
        dgz = d_og * (oh * ng) * (sg * (1.0 + gz * (1.0 - sg)))
        dng_ref[...] += jnp.sum(d_on * oh, axis=0, keepdims=True)
        doh = d_on * ng
        dob = (r * (doh - oh * jnp.mean(doh * oh, axis=-1, keepdims=True))).astype(BF16)
        dv_intra, dqi, dki, dqe, g_upd = [], [], [], [], []
        for c in range(nc):
            rows = _chunk_rows(c)
            p = jnp.where(lower, _dot_nt(qi[rows], ki[rows]), 0.0).astype(BF16)
            dv_intra.append(_dot_tn(p, dob[rows]))
            dp = jnp.where(lower, _dot_nt(dob[rows], vb[rows]), 0.0).astype(BF16)
            dqi.append(_dot(dp, ki[rows]))
            dki.append(_dot_tn(dp, qi[rows]))
            dqe.append(_dot(dob[rows], st_ref[c].astype(BF16)))
            g_upd.append(_dot_tn(dob[rows], qe[rows]))
        g_after = [None] * nc
        g = gstate[...]
        for c in reversed(range(nc)):
            g_after[c] = g
            g = g * gt["decay"][c] + g_upd[c]
        gstate[...] = g
        dkd, dv, da_last = [], [], []
        for c in range(nc):
            rows = _chunk_rows(c)
            gb = g_after[c].astype(BF16)
            dkd.append(_dot(vb[rows], gb))
            dv.append(dv_intra[c] + _dot_nt(kd[rows], gb))
            da_last.append(jnp.sum(g_after[c] * st_ref[c], axis=0, keepdims=True) * gt["decay"][c])
        dqi, dki, dqe, dkd, dv = (jnp.concatenate(z, axis=0) for z in (dqi, dki, dqe, dkd, dv))
        dqs = dqi * gt["e_in"] + dqe * gt["e_all"]
        dk = dki * gt["e_out"] + dkd * gt["e_end"]
        t_in, t_out, t_end = dqi * gt["qi"], dki * gt["ki"], dkd * gt["kd"]
        da = t_in - t_out + dqe * gt["qe"] - t_end
        row = lax.broadcasted_iota(jnp.int32, (HGRN_CHUNK, HEAD_DIM), 0)
        d_mid = t_out - t_in
        pieces = []
        for c in range(nc):
            rows = _chunk_rows(c)
            da_mid = jnp.sum(d_mid[rows], axis=0, keepdims=True)
            da_end = jnp.sum(t_end[rows], axis=0, keepdims=True) + da_last[c]
            da_c = da[rows] + jnp.where(row == HGRN_CHUNK // 2 - 1, da_mid, 0.0) + jnp.where(row == HGRN_CHUNK - 1, da_end, 0.0)
            pieces.append(_mask_mm(upper.astype(BF16), da_c))
        df = jnp.concatenate(pieces, axis=0) / gt["f"] - dk
        s = gt["s"]
        dlb = jnp.sum(df * (1.0 - s), axis=0, keepdims=True)
        dp_ref[0] = (dqs * HGRN_SCALE).astype(BF16)
        dp_ref[1] = (df * (1.0 - lb) * (s * (1.0 - s))).astype(BF16)
        dp_ref[2] = dv.astype(BF16)
        dp_ref[3] = dgz.astype(BF16)
        dt0 = dlb * (lb * (1.0 - lb))
        dtab_ref[0:1, :] += dt0
        dtab_ref[1:2, :] -= dt0

    def blk(p):
        return pl.BlockSpec((None, tb, HEAD_DIM), lambda h, n: (p, nb - 1 - n, h))

    tok = pl.BlockSpec((tb, HEAD_DIM), lambda h, n: (nb - 1 - n, h))
    return pl.pallas_call(
        body, name="hgrn_bwd",
        out_shape=[jax.ShapeDtypeStruct((N_DEV, t, D_MODEL), BF16), jax.ShapeDtypeStruct((1, D_MODEL), F32),
                   jax.ShapeDtypeStruct((2, D_MODEL), F32)],
        grid=(HEADS, nb),
        in_specs=[ANY, tok, tok, pl.BlockSpec((None, nc, HEAD_DIM, HEAD_DIM), lambda h, n: (h, nb - 1 - n, 0, 0)),
                  blk(Q_POS), blk(Q_POS + 1), blk(Q_POS + 2), blk(Q_POS + 3),
                  pl.BlockSpec((2, HEAD_DIM), lambda h, n: (0, h)), pl.BlockSpec((1, HEAD_DIM), lambda h, n: (0, h))],
        out_specs=[pl.BlockSpec((4, tb, HEAD_DIM), lambda h, n: (0, nb - 1 - n, h)),
                   pl.BlockSpec((1, HEAD_DIM), lambda h, n: (0, h)), pl.BlockSpec((2, HEAD_DIM), lambda h, n: (0, h))],
        scratch_shapes=[pltpu.VMEM((HEAD_DIM, HEAD_DIM), F32)],
        input_output_aliases={0: 0},
        compiler_params=_params([((tb, HEAD_DIM), F32)] * 6 + [((nc, HEAD_DIM, HEAD_DIM), F32)] + [((4, tb, HEAD_DIM), BF16)],
                                temps=8 << 20, sem=("arbitrary", "arbitrary")),
    )(dproj, dog, o_saved, states, proj, proj, proj, proj, lb_table, norm_g)


def _gmlp_backward(dproj, da, proj, ln_g, ln_b, w_s, bias_b):
    t = proj.shape[1]
    tm = _tile(t, 256)
    chunks = tm // GMLP_CHUNK

    def body(_, da_ref, u_ref, v_ref, lng_ref, lnb_ref, ws_ref, bias_ref, dp_ref, dlng_ref, dlnb_ref, dws_ref, dbs_ref,
             vn_scr, dvn_scr):
        @pl.when(pl.program_id(0) == 0)
        def _():
            dlng_ref[...] = jnp.zeros_like(dlng_ref)
            dlnb_ref[...] = jnp.zeros_like(dlnb_ref)
            dws_ref[...] = jnp.zeros_like(dws_ref)
            dbs_ref[...] = jnp.zeros_like(dbs_ref)

        v = v_ref[...]
        vv = _gelu(v)
        mu = jnp.mean(vv, axis=-1, keepdims=True)
        cen = vv - mu
        rstd = lax.rsqrt(jnp.mean(cen * cen, axis=-1, keepdims=True) + NORM_EPS)
        vhat = cen * rstd
        lng = lng_ref[...]
        vn_scr[...] = (vhat * lng + lnb_ref[...]).astype(BF16)
        row = lax.broadcasted_iota(jnp.int32, (GMLP_CHUNK, GMLP_CHUNK), 0)
        col = lax.broadcasted_iota(jnp.int32, (GMLP_CHUNK, GMLP_CHUNK), 1)
        for g in range(GROUPS):
            wm = _masked_ws(ws_ref, g)
            cols = slice(g * HEAD_DIM, (g + 1) * HEAD_DIM)
            dws = jnp.zeros((GMLP_CHUNK, GMLP_CHUNK), F32)
            dbs = jnp.zeros((GMLP_CHUNK, GMLP_CHUNK), F32)
            for c in range(chunks):
                rows = slice(c * GMLP_CHUNK, (c + 1) * GMLP_CHUNK)
                vn = vn_scr[rows, cols]
                mixed = _dot(wm, vn) + bias_ref[g]
                u = u_ref[rows, cols]
                d_a = da_ref[rows, cols]
                dp_ref[0, rows, cols] = (d_a * mixed * _gelu_grad(u)).astype(BF16)
                dmix = d_a * _gelu(u)
                dmb = dmix.astype(BF16)
                dbs = dbs + dmix
                dws = dws + _dot_nt(dmb, vn)
                dvn_scr[rows, cols] = _dot_tn(wm, dmb)
            dws_ref[g] += jnp.where(row >= col, dws, 0.0)
            dbs_ref[g] += jnp.broadcast_to(jnp.sum(dbs, axis=-1, keepdims=True), (GMLP_CHUNK, GMLP_CHUNK))
        dvn = dvn_scr[...]
        dlng_ref[...] += jnp.sum(dvn * vhat, axis=0, keepdims=True)
        dlnb_ref[...] += jnp.sum(dvn, axis=0, keepdims=True)
        dvh = dvn * lng
        dvv = rstd * (dvh - jnp.mean(dvh, axis=-1, keepdims=True) - vhat * jnp.mean(dvh * vhat, axis=-1, keepdims=True))
        dp_ref[1] = (dvv * _gelu_grad(v)).astype(BF16)

    tok = pl.BlockSpec((tm, D_MODEL), lambda m: (m, 0))
    small = pl.BlockSpec((GROUPS, GMLP_CHUNK, GMLP_CHUNK), lambda m: (0, 0, 0))
    vec = pl.BlockSpec((1, D_MODEL), lambda m: (0, 0))
    return pl.pallas_call(
        body, name="gmlp_bwd",
        out_shape=[jax.ShapeDtypeStruct(dproj.shape, BF16), jax.ShapeDtypeStruct((1, D_MODEL), F32),
                   jax.ShapeDtypeStruct((1, D_MODEL), F32), jax.ShapeDtypeStruct((GROUPS, GMLP_CHUNK, GMLP_CHUNK), F32),
                   jax.ShapeDtypeStruct((GROUPS, GMLP_CHUNK, GMLP_CHUNK), F32)],
        grid=(t // tm,),
        in_specs=[ANY, tok, pl.BlockSpec((None, tm, D_MODEL), lambda m: (U_POS, m, 0)),
                  pl.BlockSpec((None, tm, D_MODEL), lambda m: (U_POS + 1, m, 0)), vec, vec, small, small],
        out_specs=[pl.BlockSpec((2, tm, D_MODEL), lambda m: (U_POS // 2, m, 0)), vec, vec, small, small],
        scratch_shapes=[pltpu.VMEM((tm, D_MODEL), BF16), pltpu.VMEM((tm, D_MODEL), F32)],
        input_output_aliases={0: 0},
        compiler_params=_params([((tm, D_MODEL), F32)] * 3 + [((2, tm, D_MODEL), BF16)] + [((8, 128, 128), F32)] * 4,
                                scratch=[((tm, D_MODEL), BF16), ((tm, D_MODEL), F32)], temps=12 << 20, sem=("arbitrary",)),
    )(dproj, da, proj, proj, ln_g, ln_b, w_s, bias_b)


def _input_backward(dproj, w_in_g, x, dx1, mix_g):
    t = x.shape[0]
    tm = _tile(t, 512)

    def body(dp_ref, w_ref, x_ref, dx1_ref, g_ref, dx_ref, dg_ref):
        @pl.when(pl.program_id(0) == 0)
        def _():
            dg_ref[...] = jnp.zeros_like(dg_ref)

        dh = _dot_nt(dp_ref[0], w_ref[0])
        for p in range(1, N_DEV):
            dh = dh + _dot_nt(dp_ref[p], w_ref[p])
        dx, dg = _rms_bwd(dh, x_ref[...], g_ref[...])
        dx_ref[...] = dx1_ref[...] + dx
        dg_ref[...] += dg

    tok = pl.BlockSpec((tm, D_MODEL), lambda m: (m, 0))
    vec = pl.BlockSpec((1, D_MODEL), lambda m: (0, 0))
    return pl.pallas_call(
        body, name="input_bwd",
        out_shape=[jax.ShapeDtypeStruct((t, D_MODEL), F32), jax.ShapeDtypeStruct((1, D_MODEL), F32)],
        grid=(t // tm,),
        in_specs=[pl.BlockSpec((N_DEV, tm, D_MODEL), lambda m: (0, m, 0)), RESIDENT, tok, tok, vec],
        out_specs=[tok, vec],
        compiler_params=_params([((N_DEV, tm, D_MODEL), BF16)] + [((tm, D_MODEL), F32)] * 3,
                                scratch=[((N_DEV, D_MODEL, D_MODEL), BF16)], temps=6 << 20, sem=("arbitrary",)),
    )(dproj, w_in_g, x, dx1, mix_g)


def _weight_grad(name, a, b, a_spec, b_spec, out_shape, out_spec, steps, blocks, a_is_transposed):
    def body(a_ref, b_ref, o_ref):
        o_ref[...] = _dot(a_ref[...], b_ref[...]) if a_is_transposed else _dot_tn(a_ref[...], b_ref[...])

    return pl.pallas_call(
        body, name=name, out_shape=jax.ShapeDtypeStruct(out_shape, F32), grid=(steps,), in_specs=[a_spec, b_spec],
        out_specs=out_spec, compiler_params=_params(blocks, temps=4 << 20, sem=("arbitrary",)),
    )(a, b)


def _pack_small(mix_g, ln_g, ln_b, b_s, lb_table, hg_norm, ffn_g, final_g, loss_row):
    def part(a):
        a = a.reshape(-1, D_MODEL)
        return jnp.pad(a, ((0, 8 - a.shape[0]), (0, 0)))

    return jnp.concatenate([part(mix_g), part(ln_g), part(ln_b), part(hg_norm), part(ffn_g), part(final_g),
                            part(lb_table), part(b_s), part(loss_row)], axis=0)


def _unpack_small(pack, w_s):
    return dict(norm_mix_g=pack[0:1], gmlp_ln_g=pack[8:9], gmlp_ln_b=pack[16:17], hgrn_norm_g=pack[24:25],
                norm_ffn_g=pack[32:33], norm_final_g=pack[40], hgrn_lb_table=pack[48:50],
                gmlp_b_s=pack[56:57].reshape(1, GROUPS, GMLP_CHUNK),
                gmlp_w_s=w_s.reshape(1, GROUPS, GMLP_CHUNK, GMLP_CHUNK))


def _adamw_small(name, gathered, w, m, v):
    rows, cols = w.shape

    def body(p_ref, w_ref, m_ref, v_ref, g_out, d_out, m_out, v_out):
        g = p_ref[0]
        for j in range(1, N_DEV):
            g = g + p_ref[j]
        delta, m_new, v_new = _adamw_math(w_ref[...], g, m_ref[...], v_ref[...])
        g_out[...] = g
        d_out[...] = delta
        m_out[...] = m_new
        v_out[...] = v_new

    tr = _tile(rows, 512)
    spec = pl.BlockSpec((tr, cols), lambda r: (r, 0))
    return pl.pallas_call(
        body, name=name, out_shape=[jax.ShapeDtypeStruct((rows, cols), F32)] * 4, grid=(rows // tr,),
        in_specs=[pl.BlockSpec((N_DEV, tr, cols), lambda r: (0, r, 0)), spec, spec, spec], out_specs=[spec] * 4,
        compiler_params=_params([((N_DEV, tr, cols), F32)] + [((tr, cols), F32)] * 7, sem=("arbitrary",)),
    )(gathered, w, m, v)


def kernel(x, norm_mix_g, w_in, gmlp_ln_g, gmlp_ln_b, gmlp_w_s, gmlp_b_s, hgrn_lb_table, hgrn_norm_g, w_branch_a, w_branch_b, w_out, norm_ffn_g, w_gate_up, w_down, norm_final_g, loss_target, m_norm_mix_g, m_w_in, m_gmlp_ln_g, m_gmlp_ln_b, m_gmlp_w_s, m_gmlp_b_s, m_hgrn_lb_table, m_hgrn_norm_g, m_w_branch_a, m_w_branch_b, m_w_out, m_norm_ffn_g, m_w_gate_up, m_w_down, m_norm_final_g, v_norm_mix_g, v_w_in, v_gmlp_ln_g, v_gmlp_ln_b, v_gmlp_w_s, v_gmlp_b_s, v_hgrn_lb_table, v_hgrn_norm_g, v_w_branch_a, v_w_branch_b, v_w_out, v_norm_ffn_g, v_w_gate_up, v_w_down, v_norm_final_g):
    t = x.shape[1]
    x2d = x.reshape(t, D_MODEL)
    target = loss_target.reshape(t, D_MODEL)
    final_g = norm_final_g.reshape(1, D_MODEL)

    shards = [w_in[0].astype(BF16), w_branch_a[0].astype(BF16), w_branch_b[0].astype(BF16), w_out[0].astype(BF16),
              w_gate_up[0].T.astype(BF16), w_down[0].astype(BF16)]

    def rows_of(n):
        return lambda ref, j: ref.at[pl.ds(pl.multiple_of(j * n, 8), n)]

    gathered = [((N_DEV, D_MODEL, D_MODEL), BF16), ((D_MODEL, D_MODEL), BF16), ((D_MODEL, D_MODEL), BF16),
                ((D_MODEL, D_MODEL), BF16), ((N_DEV, FF_BLOCK, D_MODEL), BF16), ((D_FF, D_MODEL), BF16)]
    places = [lambda ref, j: ref.at[_pos_of_dev(j)], rows_of(BRANCH_ROWS), rows_of(BRANCH_ROWS), rows_of(BRANCH_ROWS),
              lambda ref, j: ref.at[j], rows_of(DOWN_ROWS)]
    w_in_g = _all_gather_balanced_async("w_in_all_gather", 9, shards[0], gathered[0], places[0], D_MODEL)
    w_in_sibling = _swap_with_sibling("w_in_from_sibling", shards[0])
    _, later = lax.optimization_barrier((w_in_sibling, shards[1:]))
    w_a, w_b, w_o, w_gu, w_dn = _all_gather_async("weights_all_gather", 0, later, gathered[1:], places[1:])

    core_i, chip_i = lax.axis_index("c"), 2 * lax.axis_index("x") + lax.axis_index("y")
    own_pos = jnp.stack([_pos_of_dev(2 * chip_i + core_i), _pos_of_dev(2 * chip_i + 1 - core_i)]).astype(jnp.int32)
    other_pos = jnp.stack([_pos_of_dev(2 * jnp.bitwise_xor(chip_i, q) + cc) for q in (1, 2, 3) for cc in (0, 1)]).astype(jnp.int32)
    proj, h, h_t = _proj_forward_own_chip(own_pos, x2d, norm_mix_g, shards[0], w_in_sibling)
    proj = _proj_forward_other_chips(other_pos, proj, h, w_in_g)
    bias_b = jnp.broadcast_to(gmlp_b_s[0][:, :, None], (GROUPS, GMLP_CHUNK, GMLP_CHUNK))
    a = _gmlp_forward(proj, gmlp_ln_g, gmlp_ln_b, gmlp_w_s[0], bias_b)
    og, o_saved, states = _hgrn_forward(proj, hgrn_lb_table, hgrn_norm_g)
    ya, yb, merged, x1, h2 = _branch_out_forward(a, og, proj, x2d, w_a, w_b, w_o, norm_ffn_g)
    gu, act, loss_tile, d_final_g, dx2, dx2b = _ffn_forward(h2, x1, w_gu, w_dn, target, final_g)

    core = lax.axis_index("c").astype(jnp.int32).reshape(1)
    chip = (2 * lax.axis_index("x") + lax.axis_index("y")).astype(jnp.int32).reshape(1)
    branch_rows, branch_shape = rows_of(BRANCH_ROWS), (BRANCH_ROWS, D_MODEL)
    branch_block = ((BRANCH_ROWS, D_MODEL), lambda q, r, c: (2 * q + c, 0))

    def chip_partials(names, grads, land, own_blocks):
        return [_chip_partial("chip_partial_" + nme, core, g_, blk, idx, l_)
                for nme, g_, (blk, idx), l_ in zip(names, grads, own_blocks, land)]

    whole = pl.BlockSpec((t, D_MODEL), lambda n: (0, 0))
    whole_t = pl.BlockSpec((D_MODEL, t), lambda n: (0, 0))
    col_blocks = [((t, D_MODEL), BF16), ((t, 256), BF16), ((D_MODEL, 256), F32)]

    def square_grad(name, a_, b_):
        return _weight_grad(name, a_, b_, whole, pl.BlockSpec((t, 256), lambda n: (0, n)), (D_MODEL, D_MODEL),
                            pl.BlockSpec((D_MODEL, 256), lambda n: (0, n)), D_MODEL // 256, col_blocks, False)

    dgu, dx1, dx1b, d_ffn_g = _ffn_backward(dx2b, dx2, gu, x1, w_gu, w_dn, norm_ffn_g)
    g_gu = _weight_grad(
        "grad_w_gate_up", dgu, h2, pl.BlockSpec((None, None, t, FF_BLOCK), lambda j: (j % 4, j // 4, 0, 0)), whole,
        (N_DEV, FF_BLOCK, D_MODEL), pl.BlockSpec((None, FF_BLOCK, D_MODEL), lambda j: (j, 0, 0)), N_DEV,
        [((t, 768), BF16), ((t, D_MODEL), BF16), ((FF_BLOCK, D_MODEL), F32)], False)
    g_dn = _weight_grad(
        "grad_w_down", act, dx2b, pl.BlockSpec((None, t, FF_BLOCK), lambda j: (j, 0, 0)), whole, (D_FF, D_MODEL),
        pl.BlockSpec((FF_BLOCK, D_MODEL), lambda j: (j, 0)), 4,
        [((t, 768), BF16), ((t, D_MODEL), BF16), ((FF_BLOCK, D_MODEL), F32)], False)
    names_f, grads_f = ["w_gate_up", "w_down"], [g_gu, g_dn]
    land_f = _exchange_sibling("ffn_grads_to_sibling", 2, grads_f, [lambda ref, j: ref.at[j], rows_of(DOWN_ROWS)],
                               [(FF_BLOCK, D_MODEL), (DOWN_ROWS, D_MODEL)])

    dx1b_later, _ = lax.optimization_barrier((dx1b, grads_f))
    dya, dyb, dproj, da, dog = _branch_out_backward(dx1b_later, ya, yb, proj, w_a, w_b, w_o)
    g_a = square_grad("grad_w_a", a, dya)
    g_b = square_grad("grad_w_b", og, dyb)
    g_o = square_grad("grad_w_out", merged, dx1b)
    names_b, grads_b = ["w_branch_a", "w_branch_b", "w_out"], [g_a, g_b, g_o]
    land_b = _exchange_sibling("branch_grads_to_sibling", 3, grads_b, [branch_rows] * 3, [branch_shape] * 3)

    part_f = chip_partials(names_f, grads_f, land_f,
                           [((None, DOWN_ROWS // 2, D_MODEL), lambda q, r, c: (2 * q + c, r, 0)),
                            ((DOWN_ROWS // 2, D_MODEL), lambda q, r, c: (2 * (2 * q + c) + r, 0))])
    landed_f = _exchange_chips("ffn_grads_to_chips", 5, part_f)

    dog, _ = lax.optimization_barrier((dog, part_f))
    dproj, d_hg_norm, d_lb = _hgrn_backward(dproj, dog, o_saved, states, proj, hgrn_lb_table, hgrn_norm_g)

    land_b, _ = lax.optimization_barrier((land_b, part_f))
    part_b = chip_partials(names_b, grads_b, land_b, [branch_block] * 3)
    landed_b = _exchange_chips("branch_grads_to_chips", 6, part_b)

    da, _ = lax.optimization_barrier((da, part_b))
    dproj, d_ln_g, d_ln_b, d_ws, d_bs = _gmlp_backward(dproj, da, proj, gmlp_ln_g, gmlp_ln_b, gmlp_w_s[0], bias_b)

    def packed(vals):
        return _pack_small(*vals)

    def flat_ws(a):
        return a.reshape(GROUPS * GMLP_CHUNK, GMLP_CHUNK)

    no_row = jnp.zeros((1, D_MODEL), F32)
    w_pack = packed([norm_mix_g, gmlp_ln_g, gmlp_ln_b, gmlp_b_s, hgrn_lb_table, hgrn_norm_g, norm_ffn_g, norm_final_g, no_row])
    m_pack = packed([m_norm_mix_g, m_gmlp_ln_g, m_gmlp_ln_b, m_gmlp_b_s, m_hgrn_lb_table, m_hgrn_norm_g, m_norm_ffn_g, m_norm_final_g, no_row])
    v_pack = packed([v_norm_mix_g, v_gmlp_ln_g, v_gmlp_ln_b, v_gmlp_b_s, v_hgrn_lb_table, v_hgrn_norm_g, v_norm_ffn_g, v_norm_final_g, no_row])
    small_partial = _pack_small(no_row, d_ln_g, d_ln_b, d_bs[:, :, 0], d_lb, d_hg_norm, d_ffn_g, d_final_g,
                                jnp.tile(loss_tile[0:1], (1, D_MODEL // 128)))
    small_all, ws_all = _all_gather_async(
        "small_grads_all_gather", 1, [small_partial, flat_ws(d_ws)],
        [((N_DEV, SMALL_ROWS, D_MODEL), F32), ((N_DEV, GROUPS * GMLP_CHUNK, GMLP_CHUNK), F32)],
        [lambda ref, j: ref.at[j], lambda ref, j: ref.at[j]])

    g_in = _weight_grad(
        "grad_w_in", h_t, dproj, whole_t, pl.BlockSpec((None, t, D_MODEL), lambda p: (p, 0, 0)), (N_DEV, D_MODEL, D_MODEL),
        pl.BlockSpec((None, D_MODEL, D_MODEL), lambda p: (p, 0, 0)), N_DEV,
        [((D_MODEL, t), BF16), ((t, D_MODEL), BF16), ((D_MODEL, D_MODEL), F32)], True)
    land_i = _exchange_sibling("w_in_grads_to_sibling", 4, [g_in], [lambda ref, j: ref.at[_pos_of_dev(j)]],
                               [(D_MODEL, D_MODEL)])

    big = {}
    for nme, own, lnd, w, m, v in zip(
            names_f + names_b, part_f + part_b, landed_f + landed_b,
            [w_gate_up, w_down, w_branch_a, w_branch_b, w_out], [m_w_gate_up, m_w_down, m_w_branch_a, m_w_branch_b, m_w_out],
            [v_w_gate_up, v_w_down, v_w_branch_a, v_w_branch_b, v_w_out]):
        flip = (lambda z: z.T) if nme == "w_gate_up" else (lambda z: z)
        big[nme] = [flip(o_)[None] for o_ in _adamw("adamw_" + nme, chip, own, lnd, flip(w[0]), flip(m[0]), flip(v[0]))]
    small_outs = _adamw_small("adamw_small", small_all, w_pack, m_pack, v_pack)
    ws_outs = _adamw_small("adamw_w_s", ws_all, flat_ws(gmlp_w_s), flat_ws(m_gmlp_w_s), flat_ws(v_gmlp_w_s))
    land_i, _ = lax.optimization_barrier((land_i, (big, small_outs, ws_outs)))
    part_i = chip_partials(["w_in"], [g_in], land_i,
                           [((None, 256, D_MODEL), lambda q, r, c: (_pos_of_dev(2 * q + c), r, 0))])
    landed_i = _exchange_chips("w_in_grads_to_chips", 7, part_i)

    dx1, _ = lax.optimization_barrier((dx1, part_i))
    grad_x, d_mix_g = _input_backward(dproj, w_in_g, x2d, dx1, norm_mix_g)
    big["w_in"] = [o_[None] for o_ in _adamw("adamw_w_in", chip, part_i[0], landed_i[0], w_in[0], m_w_in[0], v_w_in[0])]

    def row8(a):
        return jnp.pad(a, ((0, 7), (0, 0)))

    d_mix_g, _ = lax.optimization_barrier((d_mix_g, landed_i))
    (mix_all,) = _all_gather_async("mix_gain_grad_all_gather", 8, [row8(d_mix_g)], [((N_DEV, 8, D_MODEL), F32)],
                                   [lambda ref, j: ref.at[j]])
    mix_outs = _adamw_small("adamw_mix_gain", mix_all, row8(norm_mix_g), row8(m_norm_mix_g), row8(v_norm_mix_g))
    small = [dict(_unpack_small(p, ws), norm_mix_g=q[0:1]) for p, ws, q in zip(small_outs, ws_outs, mix_outs)]

    loss = small_outs[0][SMALL_ROWS - 8, 0]
    order = ["norm_mix_g", "w_in", "gmlp_ln_g", "gmlp_ln_b", "gmlp_w_s", "gmlp_b_s", "hgrn_lb_table", "hgrn_norm_g",
             "w_branch_a", "w_branch_b", "w_out", "norm_ffn_g", "w_gate_up", "w_down", "norm_final_g"]
    outs = [loss, grad_x.reshape(1, t, D_MODEL)]
    for kind in range(4):
        for nme in order:
            outs.append(big[nme][kind] if nme in big else small[kind][nme])
    return tuple(outs)
```

```python
import functools

import jax
import jax.numpy as jnp
from jax import lax
from jax.experimental import pallas as pl
from jax.experimental.pallas import tpu as pltpu
from jax.experimental.pallas import tpu_sc as plsc

F32, BF16 = jnp.float32, jnp.bfloat16
D_MODEL = 1024
N_DEV = 8
HEADS = 8
HEAD_DIM = 128
GROUPS = 8
GMLP_CHUNK = 128
HGRN_CHUNK = 64
HGRN_SCALE = HEAD_DIM ** -0.5
D_FF = 2816
FF_BLOCK = D_FF // 4
DOWN_ROWS = D_FF // N_DEV
BRANCH_ROWS = D_MODEL // N_DEV
NORM_EPS = 1e-6
ADAM_LR, ADAM_B1, ADAM_B2, ADAM_EPS, ADAM_WD, ADAM_STEP = 0.001, 0.9, 0.999, 1e-08, 0.01, 10
SMALL_ROWS = 72
V7X_VMEM_BYTES = 64 * 1024 * 1024
VMEM_CAP = V7X_VMEM_BYTES - 6 * 1024 * 1024
MESH_ID = pl.DeviceIdType.MESH
ANY = pl.BlockSpec(memory_space=pl.ANY)
RESIDENT = pl.BlockSpec(memory_space=pltpu.VMEM)
Q_POS, U_POS, GATE_POS = 0, 4, 6


def _pos_of_dev(j):
    return jnp.where(j < 2, j + 4, jnp.where(j < 6, j - 2, j))


def _dev_of_pos(p):
    return jnp.where(p < 4, p + 2, jnp.where(p < 6, p - 4, p))


def _nbytes(shape, dtype):
    n = 1
    for s in shape:
        n *= s
    return n * jnp.dtype(dtype).itemsize


def _params(blocks, scratch=(), temps=0, sem=None):
    need = 2 * sum(_nbytes(s, d) for s, d in blocks) + sum(_nbytes(s, d) for s, d in scratch) + temps
    assert need + (4 << 20) <= VMEM_CAP, need
    return pltpu.CompilerParams(dimension_semantics=sem, vmem_limit_bytes=VMEM_CAP)


def _tile(n, pref):
    return pref if n % pref == 0 else n


def _dot(a, b):
    return jnp.dot(a, b, preferred_element_type=F32)


def _dot_nt(a, b):
    return lax.dot_general(a, b, (((1,), (1,)), ((), ())), preferred_element_type=F32)


def _dot_tn(a, b):
    return lax.dot_general(a, b, (((0,), (0,)), ((), ())), preferred_element_type=F32)


def _sigmoid(x):
    return 1.0 / (1.0 + jnp.exp(-x))


_GELU_C = 0.7978845608028654


def _gelu(x):
    return x * (0.5 * (1.0 + jnp.tanh(_GELU_C * (x + 0.044715 * (x * x * x)))))


def _gelu_grad(x):
    t = jnp.tanh(_GELU_C * (x + 0.044715 * (x * x * x)))
    return 0.5 * (1.0 + t) + 0.5 * x * (1.0 - t * t) * (_GELU_C * (1.0 + 3.0 * 0.044715 * x * x))


def _rms_stats(x):
    r = lax.rsqrt(jnp.mean(x * x, axis=-1, keepdims=True) + NORM_EPS)
    return r, x * r


def _rms_bwd(dy, x, g):
    r, xh = _rms_stats(x)
    dg = jnp.sum(dy * xh, axis=0, keepdims=True)
    dxh = dy * g
    dx = r * (dxh - xh * jnp.mean(dxh * xh, axis=-1, keepdims=True))
    return dx, dg


def _split3(x):
    hi = x.astype(BF16)
    r = x - hi.astype(F32)
    mid = r.astype(BF16)
    lo = (r - mid.astype(F32)).astype(BF16)
    return hi, mid, lo


def _mask_mm(mask_bf16, x):
    hi, mid, lo = _split3(x)
    return _dot(mask_bf16, hi) + _dot(mask_bf16, mid) + _dot(mask_bf16, lo)


def _place():
    return lax.axis_index("x"), lax.axis_index("y"), lax.axis_index("c")


def _gather_copies(src, out, send, recv, loc, slicers):
    n = len(src)
    x, y, c = _place()
    me, sib = (x, y, c), (x, y, 1 - c)
    chips = [(1 - x, y), (x, 1 - y), (1 - x, 1 - y)]

    def dev(p):
        return 4 * p[0] + 2 * p[1] + p[2]

    def rc(i, k, block, to, from_src=False):
        dst = slicers[i](out[i], dev(block))
        return pltpu.make_async_remote_copy(
            src_ref=src[i] if from_src else dst, dst_ref=dst, send_sem=send.at[7 * i + k],
            recv_sem=recv.at[7 * i + k], device_id=to, device_id_type=MESH_ID)

    mine = [pltpu.make_async_copy(src[i], slicers[i](out[i], dev(me)), loc.at[i]) for i in range(n)]
    for cp in mine:
        cp.start()
    first = []
    for i in range(n):
        first.append(rc(i, 0, me, sib, True))
        for j, chip in enumerate(chips):
            first.append(rc(i, 1 + j, me, (*chip, c), True))
    for cp in first:
        cp.start()
    passed = []
    for j, chip in enumerate(chips):
        for i in range(n):
            rc(i, 1 + j, (*chip, c), me).wait_recv()
            cp = rc(i, 4 + j, (*chip, c), sib)
            cp.start()
            passed.append(cp)
    for i in range(n):
        rc(i, 0, sib, me).wait_recv()
        for j, chip in enumerate(chips):
            rc(i, 4 + j, (*chip, 1 - c), me).wait_recv()
    for cp in first + passed:
        cp.wait_send()
    for cp in mine:
        cp.wait()


def _gather_copies_balanced(src, out, send, recv, loc, slicer, rows):
    x, y, c = _place()
    me, sib = (x, y, c), (x, y, 1 - c)
    xn, yn, dg = (1 - x, y), (x, 1 - y), (1 - x, 1 - y)
    half_rows = rows // 2

    def block(p):
        return slicer(out, 4 * p[0] + 2 * p[1] + p[2])

    def half(ref, h):
        return ref.at[pl.ds(h * half_rows, half_rows)]

    def rc(k, dst, to, from_src=False):
        return pltpu.make_async_remote_copy(src_ref=src if from_src else dst, dst_ref=dst, send_sem=send.at[k],
                                            recv_sem=recv.at[k], device_id=to, device_id_type=MESH_ID)

    mine = pltpu.make_async_copy(src, block(me), loc.at[0])
    mine.start()
    sends = [rc(0, block(me), sib, True), rc(1, block(me), (*xn, c), True), rc(2, block(me), (*yn, c), True)]
    for cp in sends:
        cp.start()

    def then(cp):
        cp.start()
        sends.append(cp)

    rc(1, block((*xn, c)), me).wait_recv()
    then(rc(3, half(block((*xn, c)), 0), (*yn, c)))
    then(rc(5, block((*xn, c)), sib))
    rc(2, block((*yn, c)), me).wait_recv()
    then(rc(4, half(block((*yn, c)), 1), (*xn, c)))
    then(rc(6, block((*yn, c)), sib))
    rc(3, half(block((*dg, c)), 0), me).wait_recv()
    then(rc(7, half(block((*dg, c)), 0), sib))
    rc(4, half(block((*dg, c)), 1), me).wait_recv()
    then(rc(8, half(block((*dg, c)), 1), sib))
    rc(0, block(sib), me).wait_recv()
    rc(5, block((*xn, 1 - c)), me).wait_recv()
    rc(6, block((*yn, 1 - c)), me).wait_recv()
    rc(7, half(block((*dg, 1 - c)), 0), me).wait_recv()
    rc(8, half(block((*dg, 1 - c)), 1), me).wait_recv()
    for cp in sends:
        cp.wait_send()
    mine.wait()


def _gather_scratch(n):
    return [pltpu.SemaphoreType.DMA((7 * n,)), pltpu.SemaphoreType.DMA((7 * n,)), pltpu.SemaphoreType.DMA((n,))]


def _handshake(peers):
    barrier = pltpu.get_barrier_semaphore()
    for peer in peers:
        pl.semaphore_signal(barrier, inc=1, device_id=peer, device_id_type=MESH_ID)
    pl.semaphore_wait(barrier, len(peers))


def _all_gather_async(name, collective_id, srcs, out_shapes, slicers):
    n = len(srcs)

    def body(*refs):
        x, y, c = _place()
        _handshake([(1 - x if dx else x, 1 - y if dy else y, 1 - c if dc else c)
                    for dx in (0, 1) for dy in (0, 1) for dc in (0, 1) if dx or dy or dc])
        _gather_copies(refs[:n], refs[n:2 * n], *refs[2 * n:], slicers)

    return _sequencer_call(name, collective_id, body, srcs, [jax.ShapeDtypeStruct(s, d) for s, d in out_shapes],
                           _gather_scratch(n))


def _all_gather_balanced_async(name, collective_id, src, out_shape, slicer, rows):
    def body(src_ref, out_ref, send, recv, loc):
        x, y, c = _place()
        _handshake([(1 - x if dx else x, 1 - y if dy else y, 1 - c if dc else c)
                    for dx in (0, 1) for dy in (0, 1) for dc in (0, 1) if dx or dy or dc])
        _gather_copies_balanced(src_ref, out_ref, send, recv, loc, slicer, rows)

    return _sequencer_call(name, collective_id, body, [src], [jax.ShapeDtypeStruct(*out_shape)],
                           [pltpu.SemaphoreType.DMA((9,)), pltpu.SemaphoreType.DMA((9,)), pltpu.SemaphoreType.DMA((1,))])[0]


def _sequencer_call(name, collective_id, body, operands, out_types, scratch):
    return pl.kernel(
        body, out_type=out_types, mesh=plsc.ScalarSubcoreMesh(axis_name="sequencer", num_cores=1), name=name,
        scratch_types=scratch, compiler_params=pltpu.CompilerParams(collective_id=collective_id),
    )(*operands)


def _exchange_sibling(name, collective_id, grads, shard_fns, shard_shapes):
    n = len(grads)

    def body(*refs):
        g, land = refs[:n], refs[n:2 * n]
        send, recv = refs[2 * n:]
        x, y, c = _place()
        _handshake([(x, y, 1 - c)])
        remote = []
        for i in range(n):
            for q in range(4):
                cp = pltpu.make_async_remote_copy(
                    src_ref=shard_fns[i](g[i], 2 * q + (1 - c)), dst_ref=land[i].at[q], send_sem=send.at[4 * i + q],
                    recv_sem=recv.at[4 * i + q], device_id=(x, y, 1 - c), device_id_type=MESH_ID)
                cp.start()
                remote.append(cp)
        for cp in remote:
            cp.wait()

    return _sequencer_call(name, collective_id, body, grads, [jax.ShapeDtypeStruct((4, *s), F32) for s in shard_shapes],
                           [pltpu.SemaphoreType.DMA((4 * n,)), pltpu.SemaphoreType.DMA((4 * n,))])


def _exchange_chips(name, collective_id, parts):
    n = len(parts)

    def body(*refs):
        part, out = refs[:n], refs[n:2 * n]
        send, recv = refs[2 * n:]
        x, y, c = _place()
        _handshake([(1 - x, y, c), (x, 1 - y, c), (1 - x, 1 - y, c)])
        remote = []
        for i in range(n):
            for s in range(3):
                qx = 1 - x if (s + 1) // 2 else x
                qy = 1 - y if (s + 1) % 2 else y
                cp = pltpu.make_async_remote_copy(
                    src_ref=part[i].at[2 * qx + qy], dst_ref=out[i].at[s], send_sem=send.at[3 * i + s],
                    recv_sem=recv.at[3 * i + s], device_id=(qx, qy, c), device_id_type=MESH_ID)
                cp.start()
                remote.append(cp)
        for cp in remote:
            cp.wait()

    return _sequencer_call(name, collective_id, body, parts,
                           [jax.ShapeDtypeStruct((3, *p.shape[1:]), p.dtype) for p in parts],
                           [pltpu.SemaphoreType.DMA((3 * n,)), pltpu.SemaphoreType.DMA((3 * n,))])


def _chip_partial(name, core, grad, own_block, own_index, land):
    _, rows, cols = land.shape
    tr = own_block[-2]

    def body(core_ref, a_ref, b_ref, o_ref):
        o_ref[...] = (a_ref[...] + b_ref[...]).astype(BF16)

    spec = pl.BlockSpec((None, tr, cols), lambda q, r, c: (q, r, 0))
    return pl.pallas_call(
        body, name=name, out_shape=jax.ShapeDtypeStruct(land.shape, BF16),
        grid_spec=pltpu.PrefetchScalarGridSpec(
            num_scalar_prefetch=1, grid=(4, rows // tr),
            in_specs=[pl.BlockSpec(own_block, lambda q, r, c: own_index(q, r, c[0])), spec], out_specs=spec),
        compiler_params=_params([((tr, cols), F32)] * 2 + [((tr, cols), BF16)], sem=("arbitrary", "arbitrary")),
    )(core, grad, land)


def _adamw_math(w, g, m, v):
    m = ADAM_B1 * m + (1.0 - ADAM_B1) * g
    v = ADAM_B2 * v + (1.0 - ADAM_B2) * (g * g)
    m_hat = m / (1.0 - ADAM_B1 ** ADAM_STEP)
    v_hat = v / (1.0 - ADAM_B2 ** ADAM_STEP)
    delta = -ADAM_LR * (m_hat / (jnp.sqrt(v_hat) + ADAM_EPS) + ADAM_WD * w)
    return delta, m, v


def _adamw(name, chip, own, landed, w, m, v):
    _, rows, cols = own.shape
    tr = _tile(rows, 256) if rows % 256 == 0 else _tile(rows, 176)

    def body(chip_ref, own_ref, l_ref, w_ref, m_ref, v_ref, g_out, d_out, m_out, v_out):
        g = own_ref[...].astype(F32)
        for s in range(3):
            g = g + l_ref[s].astype(F32)
        delta, m_new, v_new = _adamw_math(w_ref[...], g, m_ref[...], v_ref[...])
        g_out[...] = g
        d_out[...] = delta
        m_out[...] = m_new
        v_out[...] = v_new

    spec = pl.BlockSpec((tr, cols), lambda r, c: (r, 0))
    return pl.pallas_call(
        body, name=name, out_shape=[jax.ShapeDtypeStruct((rows, cols), F32)] * 4,
        grid_spec=pltpu.PrefetchScalarGridSpec(
            num_scalar_prefetch=1, grid=(rows // tr,),
            in_specs=[pl.BlockSpec((None, tr, cols), lambda r, c: (c[0], r, 0)),
                      pl.BlockSpec((3, tr, cols), lambda r, c: (0, r, 0)), spec, spec, spec],
            out_specs=[spec] * 4),
        compiler_params=_params([((4, tr, cols), own.dtype)] + [((tr, cols), F32)] * 7, sem=("arbitrary",)),
    )(chip, own, landed, w, m, v)


def _swap_with_sibling(name, x):
    def body(x_ref, o_ref, send, recv):
        px, py, c = _place()
        cp = pltpu.make_async_remote_copy(src_ref=x_ref, dst_ref=o_ref, send_sem=send, recv_sem=recv,
                                          device_id=(px, py, 1 - c), device_id_type=MESH_ID)
        cp.start()
        cp.wait()

    return pl.pallas_call(
        body, name=name, out_shape=jax.ShapeDtypeStruct(x.shape, x.dtype), in_specs=[ANY], out_specs=ANY,
        scratch_shapes=[pltpu.SemaphoreType.DMA, pltpu.SemaphoreType.DMA],
    )(x)


def _proj_forward_own_chip(positions, x, gain, w_own, w_sibling):
    t = x.shape[0]
    tm = _tile(t, 1024)

    def body(pos_ref, x_ref, g_ref, wo_ref, ws_ref, o_ref, h_ref, ht_ref):
        @pl.when(pl.program_id(1) == 0)
        def _():
            _, xh = _rms_stats(x_ref[...])
            h = (xh * g_ref[...]).astype(BF16)
            h_ref[...] = h
            ht_ref[...] = h.T
            o_ref[...] = _dot(h, wo_ref[...])

        @pl.when(pl.program_id(1) == 1)
        def _():
            o_ref[...] = _dot(h_ref[...], ws_ref[...])

    tok = pl.BlockSpec((tm, D_MODEL), lambda m, k, pos: (m, 0))
    return pl.pallas_call(
        body, name="proj_fwd_own_chip",
        out_shape=[jax.ShapeDtypeStruct((N_DEV, t, D_MODEL), F32), jax.ShapeDtypeStruct((t, D_MODEL), BF16),
                   jax.ShapeDtypeStruct((D_MODEL, t), BF16)],
        grid_spec=pltpu.PrefetchScalarGridSpec(
            num_scalar_prefetch=1, grid=(t // tm, 2),
            in_specs=[tok, pl.BlockSpec((1, D_MODEL), lambda m, k, pos: (0, 0)), RESIDENT, RESIDENT],
            out_specs=[pl.BlockSpec((None, tm, D_MODEL), lambda m, k, pos: (pos[k], m, 0)), tok,
                       pl.BlockSpec((D_MODEL, tm), lambda m, k, pos: (0, m))]),
        compiler_params=_params([((tm, D_MODEL), F32)] * 2 + [((tm, D_MODEL), BF16)] * 2,
                                scratch=[((2, D_MODEL, D_MODEL), BF16)], temps=6 << 20, sem=("arbitrary", "arbitrary")),
    )(positions, x, gain, w_own, w_sibling)


def _proj_forward_other_chips(positions, proj, h, w_in_g):
    t = h.shape[0]
    tm = _tile(t, 1024)

    def body(pos_ref, _, h_ref, w_ref, o_ref):
        o_ref[...] = _dot(h_ref[...], w_ref[pos_ref[pl.program_id(1)]])

    return pl.pallas_call(
        body, name="proj_fwd_other_chips", out_shape=jax.ShapeDtypeStruct(proj.shape, F32),
        grid_spec=pltpu.PrefetchScalarGridSpec(
            num_scalar_prefetch=1, grid=(t // tm, N_DEV - 2),
            in_specs=[ANY, pl.BlockSpec((tm, D_MODEL), lambda m, k, pos: (m, 0)), RESIDENT],
            out_specs=pl.BlockSpec((None, tm, D_MODEL), lambda m, k, pos: (pos[k], m, 0))),
        input_output_aliases={1: 0},
        compiler_params=_params([((tm, D_MODEL), F32), ((tm, D_MODEL), BF16)], scratch=[((N_DEV, D_MODEL, D_MODEL), BF16)],
                                temps=6 << 20, sem=("arbitrary", "arbitrary")),
    )(positions, proj, h, w_in_g)


def _masked_ws(ws_ref, g):
    row = lax.broadcasted_iota(jnp.int32, (GMLP_CHUNK, GMLP_CHUNK), 0)
    col = lax.broadcasted_iota(jnp.int32, (GMLP_CHUNK, GMLP_CHUNK), 1)
    return jnp.where(row >= col, ws_ref[g], 0.0).astype(BF16)


def _gmlp_forward(proj, ln_g, ln_b, w_s, bias_b):
    t = proj.shape[1]
    tm = _tile(t, 256)
    chunks = tm // GMLP_CHUNK

    def body(u_ref, v_ref, lng_ref, lnb_ref, ws_ref, bias_ref, a_ref, vn_scr):
        vv = _gelu(v_ref[...])
        mu = jnp.mean(vv, axis=-1, keepdims=True)
        cen = vv - mu
        var = jnp.mean(cen * cen, axis=-1, keepdims=True)
        vn_scr[...] = ((cen * lax.rsqrt(var + NORM_EPS)) * lng_ref[...] + lnb_ref[...]).astype(BF16)
        for g in range(GROUPS):
            wm = _masked_ws(ws_ref, g)
            cols = slice(g * HEAD_DIM, (g + 1) * HEAD_DIM)
            for c in range(chunks):
                rows = slice(c * GMLP_CHUNK, (c + 1) * GMLP_CHUNK)
                mixed = _dot(wm, vn_scr[rows, cols]) + bias_ref[g]
                a_ref[rows, cols] = (_gelu(u_ref[rows, cols]) * mixed).astype(BF16)

    small = pl.BlockSpec((GROUPS, GMLP_CHUNK, GMLP_CHUNK), lambda m: (0, 0, 0))
    vec = pl.BlockSpec((1, D_MODEL), lambda m: (0, 0))
    return pl.pallas_call(
        body, name="gmlp_fwd", out_shape=jax.ShapeDtypeStruct((t, D_MODEL), BF16), grid=(t // tm,),
        in_specs=[pl.BlockSpec((None, tm, D_MODEL), lambda m: (U_POS, m, 0)),
                  pl.BlockSpec((None, tm, D_MODEL), lambda m: (U_POS + 1, m, 0)), vec, vec, small, small],
        out_specs=pl.BlockSpec((tm, D_MODEL), lambda m: (m, 0)),
        scratch_shapes=[pltpu.VMEM((tm, D_MODEL), BF16)],
        compiler_params=_params([((tm, D_MODEL), F32)] * 2 + [((tm, D_MODEL), BF16)] + [((8, 128, 128), F32)] * 2,
                                scratch=[((tm, D_MODEL), BF16)], temps=8 << 20, sem=("arbitrary",)),
    )(proj, proj, ln_g, ln_b, w_s, bias_b)


def _lower_bound(tab_ref):
    t0, t1 = tab_ref[0:1, :], tab_ref[1:2, :]
    mx = jnp.maximum(t0, t1)
    e0, e1 = jnp.exp(t0 - mx), jnp.exp(t1 - mx)
    return e0 / (e0 + e1)


def _tri_masks():
    row = lax.broadcasted_iota(jnp.int32, (HGRN_CHUNK, HGRN_CHUNK), 0)
    col = lax.broadcasted_iota(jnp.int32, (HGRN_CHUNK, HGRN_CHUNK), 1)
    return row >= col, row <= col


def _chunk_rows(c):
    return slice(c * HGRN_CHUNK, (c + 1) * HGRN_CHUNK)


def _per_chunk(x, nc, fn):
    return jnp.concatenate([fn(x[_chunk_rows(c)]) for c in range(nc)], axis=0)


def _chunk_row_bcast(x, nc, i):
    return _per_chunk(x, nc, lambda xc: jnp.broadcast_to(xc[i:i + 1, :], (HGRN_CHUNK, HEAD_DIM)))


def _hgrn_gates(q, fl, lb, nc):
    lower, _ = _tri_masks()
    lower = lower.astype(BF16)
    s = _sigmoid(fl)
    f = lb + (1.0 - lb) * s
    k = 1.0 - f
    hi, mid, lo = _split3(jnp.log(f))
    a = jnp.concatenate([_dot(lower, hi[_chunk_rows(c)]) + _dot(lower, mid[_chunk_rows(c)]) + _dot(lower, lo[_chunk_rows(c)])
                         for c in range(nc)], axis=0)
    a_mid = _chunk_row_bcast(a, nc, HGRN_CHUNK // 2 - 1)
    a_last = _chunk_row_bcast(a, nc, HGRN_CHUNK - 1)
    qs = q * HGRN_SCALE
    e_in, e_out, e_end, e_all = jnp.exp(a - a_mid), jnp.exp(a_mid - a), jnp.exp(a_last - a), jnp.exp(a)
    decay = [jnp.exp(a[c * HGRN_CHUNK + HGRN_CHUNK - 1:(c + 1) * HGRN_CHUNK, :]) for c in range(nc)]
    return dict(s=s, f=f, k=k, decay=decay, e_in=e_in, e_out=e_out, e_end=e_end, e_all=e_all,
                qi=qs * e_in, ki=k * e_out, kd=k * e_end, qe=qs * e_all)


def _hgrn_forward(proj, lb_table, norm_g):
    t = proj.shape[1]
    tb = _tile(t, 1024)
    nc = tb // HGRN_CHUNK
    n_chunks = t // HGRN_CHUNK

    def body(q_ref, f_ref, i_ref, g_ref, tab_ref, ng_ref, og_ref, o_ref, st_ref, state):
        @pl.when(pl.program_id(1) == 0)
        def _():
            state[...] = jnp.zeros_like(state)

        lower, _ = _tri_masks()
        gt = _hgrn_gates(q_ref[...], f_ref[...], _lower_bound(tab_ref), nc)
        qi, ki, kd, qe = (gt[n].astype(BF16) for n in ("qi", "ki", "kd", "qe"))
        vb = i_ref[...].astype(BF16)
        o_intra, d_state = [], []
        for c in range(nc):
            rows = _chunk_rows(c)
            p = jnp.where(lower, _dot_nt(qi[rows], ki[rows]), 0.0).astype(BF16)
            o_intra.append(_dot(p, vb[rows]))
            d_state.append(_dot_tn(vb[rows], kd[rows]))
        st = state[...]
        outs = []
        for c in range(nc):
            st_ref[c] = st
            outs.append(o_intra[c] + _dot_nt(qe[_chunk_rows(c)], st.astype(BF16)))
            st = st * gt["decay"][c] + d_state[c]
        state[...] = st
        o = jnp.concatenate(outs, axis=0)
        o_ref[...] = o
        _, oh = _rms_stats(o)
        gz = g_ref[...]
        og_ref[...] = ((oh * ng_ref[...]) * (gz * _sigmoid(gz))).astype(BF16)

    def blk(p):
        return pl.BlockSpec((None, tb, HEAD_DIM), lambda h, n: (p, n, h))

    out_blk = pl.BlockSpec((tb, HEAD_DIM), lambda h, n: (n, h))
    return pl.pallas_call(
        body, name="hgrn_fwd",
        out_shape=[jax.ShapeDtypeStruct((t, D_MODEL), BF16), jax.ShapeDtypeStruct((t, D_MODEL), F32),
                   jax.ShapeDtypeStruct((HEADS, n_chunks, HEAD_DIM, HEAD_DIM), F32)],
        grid=(HEADS, t // tb),
        in_specs=[blk(Q_POS), blk(Q_POS + 1), blk(Q_POS + 2), blk(Q_POS + 3),
                  pl.BlockSpec((2, HEAD_DIM), lambda h, n: (0, h)), pl.BlockSpec((1, HEAD_DIM), lambda h, n: (0, h))],
        out_specs=[out_blk, out_blk, pl.BlockSpec((None, nc, HEAD_DIM, HEAD_DIM), lambda h, n: (h, n, 0, 0))],
        scratch_shapes=[pltpu.VMEM((HEAD_DIM, HEAD_DIM), F32)],
        compiler_params=_params([((tb, HEAD_DIM), F32)] * 6 + [((nc, HEAD_DIM, HEAD_DIM), F32)], temps=8 << 20,
                                sem=("arbitrary", "arbitrary")),
    )(proj, proj, proj, proj, lb_table, norm_g)


def _branch_out_forward(a, og, proj, x, w_a, w_b, w_out, ffn_g):
    t = x.shape[0]
    tm = _tile(t, 256)

    def body(a_ref, og_ref, ga_ref, gb_ref, x_ref, wa_ref, wb_ref, wo_ref, g_ref, ya_ref, yb_ref, mg_ref, x1_ref, h2_ref):
        ya = _dot(a_ref[...], wa_ref[...])
        yb = _dot(og_ref[...], wb_ref[...])
        ya_ref[...] = ya
        yb_ref[...] = yb
        merged = (_sigmoid(ga_ref[...]) * ya + _sigmoid(gb_ref[...]) * yb).astype(BF16)
        mg_ref[...] = merged
        x1 = x_ref[...] + _dot(merged, wo_ref[...])
        x1_ref[...] = x1
        _, xh = _rms_stats(x1)
        h2_ref[...] = (xh * g_ref[...]).astype(BF16)

    tok = pl.BlockSpec((tm, D_MODEL), lambda m: (m, 0))
    wsp = pl.BlockSpec((D_MODEL, D_MODEL), lambda m: (0, 0))
    return pl.pallas_call(
        body, name="branch_out_fwd",
        out_shape=[jax.ShapeDtypeStruct((t, D_MODEL), F32), jax.ShapeDtypeStruct((t, D_MODEL), F32),
                   jax.ShapeDtypeStruct((t, D_MODEL), BF16), jax.ShapeDtypeStruct((t, D_MODEL), F32),
                   jax.ShapeDtypeStruct((t, D_MODEL), BF16)],
        grid=(t // tm,),
        in_specs=[tok, tok, pl.BlockSpec((None, tm, D_MODEL), lambda m: (GATE_POS, m, 0)),
                  pl.BlockSpec((None, tm, D_MODEL), lambda m: (GATE_POS + 1, m, 0)), tok, wsp, wsp, wsp,
                  pl.BlockSpec((1, D_MODEL), lambda m: (0, 0))],
        out_specs=[tok] * 5,
        compiler_params=_params([((tm, D_MODEL), BF16)] * 4 + [((tm, D_MODEL), F32)] * 6 + [((D_MODEL, D_MODEL), BF16)] * 3,
                                temps=8 << 20, sem=("arbitrary",)),
    )(a, og, proj, proj, x, w_a, w_b, w_out, ffn_g)


def _ffn_forward(h2, x1, w_gu, w_down, target, final_g):
    t = x1.shape[0]
    tm = _tile(t, 512)

    def body(h_ref, wgu_ref, wd_ref, x1_ref, t_ref, g_ref, gu_ref, act_ref, loss_ref, dg_ref, dx_ref, dxb_ref, acc):
        m, j = pl.program_id(0), pl.program_id(1)

        @pl.when((m == 0) & (j == 0))
        def _():
            loss_ref[...] = jnp.zeros_like(loss_ref)
            dg_ref[...] = jnp.zeros_like(dg_ref)

        h = h_ref[...]
        gate = _dot_nt(h, wgu_ref[j])
        up = _dot_nt(h, wgu_ref[j + 4])
        gu_ref[0] = gate
        gu_ref[1] = up
        act = ((gate * _sigmoid(gate)) * up).astype(BF16)
        act_ref[...] = act
        part = _dot(act, wd_ref[j])

        @pl.when(j == 0)
        def _():
            acc[...] = part

        @pl.when((j > 0) & (j < 3))
        def _():
            acc[...] += part

        @pl.when(j == 3)
        def _():
            x2 = x1_ref[...] + (acc[...] + part)
            g = g_ref[...]
            r, xh = _rms_stats(x2)
            err = xh * g - t_ref[...]
            loss_ref[...] += 0.5 * jnp.sum(jnp.mean(err * err, axis=-1, keepdims=True), axis=0, keepdims=True)
            dy = err * (1.0 / D_MODEL)
            dg_ref[...] += jnp.sum(dy * xh, axis=0, keepdims=True)
            dxh = dy * g
            dx = r * (dxh - xh * jnp.mean(dxh * xh, axis=-1, keepdims=True))
            dx_ref[...] = dx
            dxb_ref[...] = dx.astype(BF16)

    tok = pl.BlockSpec((tm, D_MODEL), lambda m, j: (m, 0))
    vec = pl.BlockSpec((1, D_MODEL), lambda m, j: (0, 0))
    return pl.pallas_call(
        body, name="ffn_fwd",
        out_shape=[jax.ShapeDtypeStruct((4, 2, t, FF_BLOCK), F32), jax.ShapeDtypeStruct((4, t, FF_BLOCK), BF16),
                   jax.ShapeDtypeStruct((8, 128), F32), jax.ShapeDtypeStruct((1, D_MODEL), F32),
                   jax.ShapeDtypeStruct((t, D_MODEL), F32), jax.ShapeDtypeStruct((t, D_MODEL), BF16)],
        grid=(t // tm, 4),
        in_specs=[tok, RESIDENT, RESIDENT, tok, tok, vec],
        out_specs=[pl.BlockSpec((None, 2, tm, FF_BLOCK), lambda m, j: (j, 0, m, 0)),
                   pl.BlockSpec((None, tm, FF_BLOCK), lambda m, j: (j, m, 0)),
                   pl.BlockSpec((8, 128), lambda m, j: (0, 0)), vec, tok, tok],
        scratch_shapes=[pltpu.VMEM((tm, D_MODEL), F32)],
        compiler_params=_params([((tm, D_MODEL), BF16), ((tm, D_MODEL), F32), ((tm, D_MODEL), F32), ((2, tm, 768), F32),
                                 ((tm, 768), BF16), ((tm, D_MODEL), F32), ((tm, D_MODEL), BF16)],
                                scratch=[((tm, D_MODEL), F32), ((N_DEV, FF_BLOCK, D_MODEL), BF16), ((D_FF, D_MODEL), BF16)],
                                temps=6 << 20, sem=("arbitrary", "arbitrary")),
    )(h2, w_gu, w_down.reshape(4, FF_BLOCK, D_MODEL), x1, target, final_g)


def _ffn_backward(dx2b, dx2, gu, x1, w_gu, w_down, ffn_g):
    t = x1.shape[0]
    tm = _tile(t, 512)

    def body(dxb_ref, dx2_ref, gu_ref, x1_ref, wgu_ref, wd_ref, g_ref, dgu_ref, dx1_ref, dx1b_ref, dg_ref, acc, prev):
        m, j = pl.program_id(0), pl.program_id(1)

        @pl.when((m == 0) & (j == 0))
        def _():
            dg_ref[...] = jnp.zeros_like(dg_ref)

        @pl.when(j == 0)
        def _():
            prev[...] = jnp.zeros_like(prev)
            acc[...] = jnp.zeros_like(acc)

        jm1 = jnp.maximum(j - 1, 0)
        acc[...] += _dot(prev[0], wgu_ref[jm1]) + _dot(prev[1], wgu_ref[jm1 + 4])
        dact = _dot_nt(dxb_ref[...], wd_ref[j])
        gate, up = gu_ref[0], gu_ref[1]
        sg = _sigmoid(gate)
        dgate = (dact * up * (sg * (1.0 + gate * (1.0 - sg)))).astype(BF16)
        dup = (dact * (gate * sg)).astype(BF16)
        dgu_ref[0] = dgate
        dgu_ref[1] = dup
        prev[0] = dgate
        prev[1] = dup

        @pl.when(j == 3)
        def _():
            dh2 = acc[...] + (_dot(prev[0], wgu_ref[3]) + _dot(prev[1], wgu_ref[7]))
            dx, dg = _rms_bwd(dh2, x1_ref[...], g_ref[...])
            dx1 = dx2_ref[...] + dx
            dx1_ref[...] = dx1
            dx1b_ref[...] = dx1.astype(BF16)
            dg_ref[...] += dg

    tok = pl.BlockSpec((tm, D_MODEL), lambda m, j: (m, 0))
    vec = pl.BlockSpec((1, D_MODEL), lambda m, j: (0, 0))
    gu_spec = pl.BlockSpec((None, 2, tm, FF_BLOCK), lambda m, j: (j, 0, m, 0))
    return pl.pallas_call(
        body, name="ffn_bwd",
        out_shape=[jax.ShapeDtypeStruct((4, 2, t, FF_BLOCK), BF16), jax.ShapeDtypeStruct((t, D_MODEL), F32),
                   jax.ShapeDtypeStruct((t, D_MODEL), BF16), jax.ShapeDtypeStruct((1, D_MODEL), F32)],
        grid=(t // tm, 4),
        in_specs=[tok, tok, gu_spec, tok, RESIDENT, RESIDENT, vec],
        out_specs=[gu_spec, tok, tok, vec],
        scratch_shapes=[pltpu.VMEM((tm, D_MODEL), F32), pltpu.VMEM((2, tm, FF_BLOCK), BF16)],
        compiler_params=_params([((tm, D_MODEL), BF16), ((tm, D_MODEL), F32), ((2, tm, 768), F32), ((tm, D_MODEL), F32),
                                 ((2, tm, 768), BF16), ((tm, D_MODEL), F32), ((tm, D_MODEL), BF16)],
                                scratch=[((tm, D_MODEL), F32), ((2, tm, 768), BF16), ((N_DEV, FF_BLOCK, D_MODEL), BF16),
                                         ((D_FF, D_MODEL), BF16)],
                                temps=4 << 20, sem=("arbitrary", "arbitrary")),
    )(dx2b, dx2, gu, x1, w_gu, w_down.reshape(4, FF_BLOCK, D_MODEL), ffn_g)


def _branch_out_backward(dx1b, ya, yb, proj, w_a, w_b, w_out):
    t = ya.shape[0]
    tm = _tile(t, 256)

    def body(dx_ref, ya_ref, yb_ref, ga_ref, gb_ref, wa_ref, wb_ref, wo_ref, dya_ref, dyb_ref, dgate_ref, da_ref, dog_ref):
        dm = _dot_nt(dx_ref[...], wo_ref[...])
        sa, sb = _sigmoid(ga_ref[...]), _sigmoid(gb_ref[...])
        dya = (dm * sa).astype(BF16)
        dyb = (dm * sb).astype(BF16)
        dya_ref[...] = dya
        dyb_ref[...] = dyb
        dgate_ref[0] = (dm * ya_ref[...] * (sa * (1.0 - sa))).astype(BF16)
        dgate_ref[1] = (dm * yb_ref[...] * (sb * (1.0 - sb))).astype(BF16)
        da_ref[...] = _dot_nt(dya, wa_ref[...])
        dog_ref[...] = _dot_nt(dyb, wb_ref[...])

    tok = pl.BlockSpec((tm, D_MODEL), lambda m: (m, 0))
    wsp = pl.BlockSpec((D_MODEL, D_MODEL), lambda m: (0, 0))
    return pl.pallas_call(
        body, name="branch_out_bwd",
        out_shape=[jax.ShapeDtypeStruct((t, D_MODEL), BF16), jax.ShapeDtypeStruct((t, D_MODEL), BF16),
                   jax.ShapeDtypeStruct((N_DEV, t, D_MODEL), BF16), jax.ShapeDtypeStruct((t, D_MODEL), F32),
                   jax.ShapeDtypeStruct((t, D_MODEL), F32)],
        grid=(t // tm,),
        in_specs=[tok, tok, tok, pl.BlockSpec((None, tm, D_MODEL), lambda m: (GATE_POS, m, 0)),
                  pl.BlockSpec((None, tm, D_MODEL), lambda m: (GATE_POS + 1, m, 0)), wsp, wsp, wsp],
        out_specs=[tok, tok, pl.BlockSpec((2, tm, D_MODEL), lambda m: (GATE_POS // 2, m, 0)), tok, tok],
        compiler_params=_params([((tm, D_MODEL), BF16)] * 5 + [((tm, D_MODEL), F32)] * 6 + [((D_MODEL, D_MODEL), BF16)] * 3,
                                temps=8 << 20, sem=("arbitrary",)),
    )(dx1b, ya, yb, proj, proj, w_a, w_b, w_out)


def _hgrn_backward(dproj, dog, o_saved, states, proj, lb_table, norm_g):
    t = proj.shape[1]
    tb = _tile(t, 1024)
    nc = tb // HGRN_CHUNK
    nb = t // tb

    def body(_, dog_ref, o_ref, st_ref, q_ref, f_ref, i_ref, g_ref, tab_ref, ng_ref, dp_ref, dng_ref, dtab_ref, gstate):
        @pl.when(pl.program_id(1) == 0)
        def _():
            gstate[...] = jnp.zeros_like(gstate)
            dng_ref[...] = jnp.zeros_like(dng_ref)
            dtab_ref[...] = jnp.zeros_like(dtab_ref)

        lb = _lower_bound(tab_ref)
        ng = ng_ref[...]
        lower, upper = _tri_masks()
        gt = _hgrn_gates(q_ref[...], f_ref[...], lb, nc)
        qi, ki, kd, qe = (gt[n].astype(BF16) for n in ("qi", "ki", "kd", "qe"))
        vb = i_ref[...].astype(BF16)
        o, gz, d_og = o_ref[...], g_ref[...], dog_ref[...]
        r, oh = _rms_stats(o)
        sg = _sigmoid(gz)
        d_on = d_og * (gz * sg)
        dgz = d_og * (oh * ng) * (sg * (1.0 + gz * (1.0 - sg)))
        dng_ref[...] += jnp.sum(d_on * oh, axis=0, keepdims=True)
        doh = d_on * ng
        dob = (r * (doh - oh * jnp.mean(doh * oh, axis=-1, keepdims=True))).astype(BF16)
        dv_intra, dqi, dki, dqe, g_upd = [], [], [], [], []
        for c in range(nc):
            rows = _chunk_rows(c)
            p = jnp.where(lower, _dot_nt(qi[rows], ki[rows]), 0.0).astype(BF16)
            dv_intra.append(_dot_tn(p, dob[rows]))
            dp = jnp.where(lower, _dot_nt(dob[rows], vb[rows]), 0.0).astype(BF16)
            dqi.append(_dot(dp, ki[rows]))
            dki.append(_dot_tn(dp, qi[rows]))
            dqe.append(_dot(dob[rows], st_ref[c].astype(BF16)))
            g_upd.append(_dot_tn(dob[rows], qe[rows]))
        g_after = [None] * nc
        g = gstate[...]
        for c in reversed(range(nc)):
            g_after[c] = g
            g = g * gt["decay"][c] + g_upd[c]
        gstate[...] = g
        dkd, dv, da_last = [], [], []
        for c in range(nc):
            rows = _chunk_rows(c)
            gb = g_after[c].astype(BF16)
            dkd.append(_dot(vb[rows], gb))
            dv.append(dv_intra[c] + _dot_nt(kd[rows], gb))
            da_last.append(jnp.sum(g_after[c] * st_ref[c], axis=0, keepdims=True) * gt["decay"][c])
        dqi, dki, dqe, dkd, dv = (jnp.concatenate(z, axis=0) for z in (dqi, dki, dqe, dkd, dv))
        dqs = dqi * gt["e_in"] + dqe * gt["e_all"]
        dk = dki * gt["e_out"] + dkd * gt["e_end"]
        t_in, t_out, t_end = dqi * gt["qi"], dki * gt["ki"], dkd * gt["kd"]
        da = t_in - t_out + dqe * gt["qe"] - t_end
        row = lax.broadcasted_iota(jnp.int32, (HGRN_CHUNK, HEAD_DIM), 0)
        d_mid = t_out - t_in
        pieces = []
        for c in range(nc):
            rows = _chunk_rows(c)
            da_mid = jnp.sum(d_mid[rows], axis=0, keepdims=True)
            da_end = jnp.sum(t_end[rows], axis=0, keepdims=True) + da_last[c]
            da_c = da[rows] + jnp.where(row == HGRN_CHUNK // 2 - 1, da_mid, 0.0) + jnp.where(row == HGRN_CHUNK - 1, da_end, 0.0)
            pieces.append(_mask_mm(upper.astype(BF16), da_c))
        df = jnp.concatenate(pieces, axis=0) / gt["f"] - dk
        s = gt["s"]
        dlb = jnp.sum(df * (1.0 - s), axis=0, keepdims=True)
        dp_ref[0] = (dqs * HGRN_SCALE).astype(BF16)
        dp_ref[1] = (df * (1.0 - lb) * (s * (1.0 - s))).astype(BF16)
        dp_ref[2] = dv.astype(BF16)
        dp_ref[3] = dgz.astype(BF16)
        dt0 = dlb * (lb * (1.0 - lb))
        dtab_ref[0:1, :] += dt0
        dtab_ref[1:2, :] -= dt0

    def blk(p):
        return pl.BlockSpec((None, tb, HEAD_DIM), lambda h, n: (p, nb - 1 - n, h))

    tok = pl.BlockSpec((tb, HEAD_DIM), lambda h, n: (nb - 1 - n, h))
    return pl.pallas_call(
        body, name="hgrn_bwd",
        out_shape=[jax.ShapeDtypeStruct((N_DEV, t, D_MODEL), BF16), jax.ShapeDtypeStruct((1, D_MODEL), F32),
                   jax.ShapeDtypeStruct((2, D_MODEL), F32)],
        grid=(HEADS, nb),
        in_specs=[ANY, tok, tok, pl.BlockSpec((None, nc, HEAD_DIM, HEAD_DIM), lambda h, n: (h, nb - 1 - n, 0, 0)),
                  blk(Q_POS), blk(Q_POS + 1), blk(Q_POS + 2), blk(Q_POS + 3),
                  pl.BlockSpec((2, HEAD_DIM), lambda h, n: (0, h)), pl.BlockSpec((1, HEAD_DIM), lambda h, n: (0, h))],
        out_specs=[pl.BlockSpec((4, tb, HEAD_DIM), lambda h, n: (0, nb - 1 - n, h)),
                   pl.BlockSpec((1, HEAD_DIM), lambda h, n: (0, h)), pl.BlockSpec((2, HEAD_DIM), lambda h, n: (0, h))],
        scratch_shapes=[pltpu.VMEM((HEAD_DIM, HEAD_DIM), F32)],
        input_output_aliases={0: 0},
        compiler_params=_params([((tb, HEAD_DIM), F32)] * 6 + [((nc, HEAD_DIM, HEAD_DIM), F32)] + [((4, tb, HEAD_DIM), BF16)],
                                temps=8 << 20, sem=("arbitrary", "arbitrary")),
    )(dproj, dog, o_saved, states, proj, proj, proj, proj, lb_table, norm_g)


def _gmlp_backward(dproj, da, proj, ln_g, ln_b, w_s, bias_b):
    t = proj.shape[1]
    tm = _tile(t, 256)
    chunks = tm // GMLP_CHUNK

    def body(_, da_ref, u_ref, v_ref, lng_ref, lnb_ref, ws_ref, bias_ref, dp_ref, dlng_ref, dlnb_ref, dws_ref, dbs_ref,
             vn_scr, dvn_scr):
        @pl.when(pl.program_id(0) == 0)
        def _():
            dlng_ref[...] = jnp.zeros_like(dlng_ref)
            dlnb_ref[...] = jnp.zeros_like(dlnb_ref)
            dws_ref[...] = jnp.zeros_like(dws_ref)
            dbs_ref[...] = jnp.zeros_like(dbs_ref)

        v = v_ref[...]
        vv = _gelu(v)
        mu = jnp.mean(vv, axis=-1, keepdims=True)
        cen = vv - mu
        rstd = lax.rsqrt(jnp.mean(cen * cen, axis=-1, keepdims=True) + NORM_EPS)
        vhat = cen * rstd
        lng = lng_ref[...]
        vn_scr[...] = (vhat * lng + lnb_ref[...]).astype(BF16)
        row = lax.broadcasted_iota(jnp.int32, (GMLP_CHUNK, GMLP_CHUNK), 0)
        col = lax.broadcasted_iota(jnp.int32, (GMLP_CHUNK, GMLP_CHUNK), 1)
        for g in range(GROUPS):
            wm = _masked_ws(ws_ref, g)
            cols = slice(g * HEAD_DIM, (g + 1) * HEAD_DIM)
            dws = jnp.zeros((GMLP_CHUNK, GMLP_CHUNK), F32)
            dbs = jnp.zeros((GMLP_CHUNK, GMLP_CHUNK), F32)
            for c in range(chunks):
                rows = slice(c * GMLP_CHUNK, (c + 1) * GMLP_CHUNK)
                vn = vn_scr[rows, cols]
                mixed = _dot(wm, vn) + bias_ref[g]
                u = u_ref[rows, cols]
                d_a = da_ref[rows, cols]
                dp_ref[0, rows, cols] = (d_a * mixed * _gelu_grad(u)).astype(BF16)
                dmix = d_a * _gelu(u)
                dmb = dmix.astype(BF16)
                dbs = dbs + dmix
                dws = dws + _dot_nt(dmb, vn)
                dvn_scr[rows, cols] = _dot_tn(wm, dmb)
            dws_ref[g] += jnp.where(row >= col, dws, 0.0)
            dbs_ref[g] += jnp.broadcast_to(jnp.sum(dbs, axis=-1, keepdims=True), (GMLP_CHUNK, GMLP_CHUNK))
        dvn = dvn_scr[...]
        dlng_ref[...] += jnp.sum(dvn * vhat, axis=0, keepdims=True)
        dlnb_ref[...] += jnp.sum(dvn, axis=0, keepdims=True)
        dvh = dvn * lng
        dvv = rstd * (dvh - jnp.mean(dvh, axis=-1, keepdims=True) - vhat * jnp.mean(dvh * vhat, axis=-1, keepdims=True))
        dp_ref[1] = (dvv * _gelu_grad(v)).astype(BF16)

    tok = pl.BlockSpec((tm, D_MODEL), lambda m: (m, 0))
    small = pl.BlockSpec((GROUPS, GMLP_CHUNK, GMLP_CHUNK), lambda m: (0, 0, 0))
    vec = pl.BlockSpec((1, D_MODEL), lambda m: (0, 0))
    return pl.pallas_call(
        body, name="gmlp_bwd",
        out_shape=[jax.ShapeDtypeStruct(dproj.shape, BF16), jax.ShapeDtypeStruct((1, D_MODEL), F32),
                   jax.ShapeDtypeStruct((1, D_MODEL), F32), jax.ShapeDtypeStruct((GROUPS, GMLP_CHUNK, GMLP_CHUNK), F32),
                   jax.ShapeDtypeStruct((GROUPS, GMLP_CHUNK, GMLP_CHUNK), F32)],
        grid=(t // tm,),
        in_specs=[ANY, tok, pl.BlockSpec((None, tm, D_MODEL), lambda m: (U_POS, m, 0)),
                  pl.BlockSpec((None, tm, D_MODEL), lambda m: (U_POS + 1, m, 0)), vec, vec, small, small],
        out_specs=[pl.BlockSpec((2, tm, D_MODEL), lambda m: (U_POS // 2, m, 0)), vec, vec, small, small],
        scratch_shapes=[pltpu.VMEM((tm, D_MODEL), BF16), pltpu.VMEM((tm, D_MODEL), F32)],
        input_output_aliases={0: 0},
        compiler_params=_params([((tm, D_MODEL), F32)] * 3 + [((2, tm, D_MODEL), BF16)] + [((8, 128, 128), F32)] * 4,
                                scratch=[((tm, D_MODEL), BF16), ((tm, D_MODEL), F32)], temps=12 << 20, sem=("arbitrary",)),
    )(dproj, da, proj, proj, ln_g, ln_b, w_s, bias_b)


def _input_backward(dproj, w_in_g, x, dx1, mix_g):
    t = x.shape[0]
    tm = _tile(t, 512)

    def body(dp_ref, w_ref, x_ref, dx1_ref, g_ref, dx_ref, dg_ref):
        @pl.when(pl.program_id(0) == 0)
        def _():
            dg_ref[...] = jnp.zeros_like(dg_ref)

        dh = _dot_nt(dp_ref[0], w_ref[0])
        for p in range(1, N_DEV):
            dh = dh + _dot_nt(dp_ref[p], w_ref[p])
        dx, dg = _rms_bwd(dh, x_ref[...], g_ref[...])
        dx_ref[...] = dx1_ref[...] + dx
        dg_ref[...] += dg

    tok = pl.BlockSpec((tm, D_MODEL), lambda m: (m, 0))
    vec = pl.BlockSpec((1, D_MODEL), lambda m: (0, 0))
    return pl.pallas_call(
        body, name="input_bwd",
        out_shape=[jax.ShapeDtypeStruct((t, D_MODEL), F32), jax.ShapeDtypeStruct((1, D_MODEL), F32)],
        grid=(t // tm,),
        in_specs=[pl.BlockSpec((N_DEV, tm, D_MODEL), lambda m: (0, m, 0)), RESIDENT, tok, tok, vec],
        out_specs=[tok, vec],
        compiler_params=_params([((N_DEV, tm, D_MODEL), BF16)] + [((tm, D_MODEL), F32)] * 3,
                                scratch=[((N_DEV, D_MODEL, D_MODEL), BF16)], temps=6 << 20, sem=("arbitrary",)),
    )(dproj, w_in_g, x, dx1, mix_g)


def _weight_grad(name, a, b, a_spec, b_spec, out_shape, out_spec, steps, blocks, a_is_transposed):
    def body(a_ref, b_ref, o_ref):
        o_ref[...] = _dot(a_ref[...], b_ref[...]) if a_is_transposed else _dot_tn(a_ref[...], b_ref[...])

    return pl.pallas_call(
        body, name=name, out_shape=jax.ShapeDtypeStruct(out_shape, F32), grid=(steps,), in_specs=[a_spec, b_spec],
        out_specs=out_spec, compiler_params=_params(blocks, temps=4 << 20, sem=("arbitrary",)),
    )(a, b)


def _pack_small(mix_g, ln_g, ln_b, b_s, lb_table, hg_norm, ffn_g, final_g, loss_row):
    def part(a):
        a = a.reshape(-1, D_MODEL)
        return jnp.pad(a, ((0, 8 - a.shape[0]), (0, 0)))

    return jnp.concatenate([part(mix_g), part(ln_g), part(ln_b), part(hg_norm), part(ffn_g), part(final_g),
                            part(lb_table), part(b_s), part(loss_row)], axis=0)


SMALL_PARTS = (("gmlp_ln_g", 8, 1), ("gmlp_ln_b", 16, 1), ("hgrn_norm_g", 24, 1), ("norm_ffn_g", 32, 1), ("norm_final_g", 40, 1),
               ("hgrn_lb_table", 48, 2))


def _adamw_small_unpacked(gathered, w, m, v):
    rows = w.shape[0]
    n_out = len(SMALL_PARTS) + 1

    def body(p_ref, w_ref, m_ref, v_ref, *outs):
        g = p_ref[0]
        for j in range(1, N_DEV):
            g = g + p_ref[j]
        delta, m_new, v_new = _adamw_math(w_ref[...], g, m_ref[...], v_ref[...])
        for kind, val in enumerate((g, delta, m_new, v_new)):
            refs = outs[kind * n_out:(kind + 1) * n_out]
            for (_, first, count), ref in zip(SMALL_PARTS, refs):
                ref[...] = val[first:first + count]
            for grp in range(GROUPS):
                refs[-1][0, grp:grp + 1, :] = val[56:57, grp * GMLP_CHUNK:(grp + 1) * GMLP_CHUNK]
        outs[-1][...] = g[SMALL_ROWS - 8:SMALL_ROWS - 7]

    shapes = [jax.ShapeDtypeStruct((count, D_MODEL), F32) for _, _, count in SMALL_PARTS]
    shapes.append(jax.ShapeDtypeStruct((1, GROUPS, GMLP_CHUNK), F32))
    whole = pl.BlockSpec((rows, D_MODEL), lambda: (0, 0))
    res = pl.pallas_call(
        body, name="adamw_small", out_shape=shapes * 4 + [jax.ShapeDtypeStruct((1, D_MODEL), F32)],
        in_specs=[pl.BlockSpec((N_DEV, rows, D_MODEL), lambda: (0, 0, 0)), whole, whole, whole],
        compiler_params=_params([((N_DEV, rows, D_MODEL), F32)] + [((rows, D_MODEL), F32)] * 7),
    )(gathered, w, m, v)
    names = [nme for nme, _, _ in SMALL_PARTS] + ["gmlp_b_s"]
    return [dict(zip(names, res[kind * n_out:(kind + 1) * n_out])) for kind in range(4)], res[-1]


def _adamw_row(name, gathered, w, m, v):
    def body(p_ref, w_ref, m_ref, v_ref, g_out, d_out, m_out, v_out):
        g = p_ref[0, 0:1, :]
        for j in range(1, N_DEV):
            g = g + p_ref[j, 0:1, :]
        delta, m_new, v_new = _adamw_math(w_ref[...], g, m_ref[...], v_ref[...])
        g_out[...] = g
        d_out[...] = delta
        m_out[...] = m_new
        v_out[...] = v_new

    return pl.pallas_call(
        body, name=name, out_shape=[jax.ShapeDtypeStruct((1, D_MODEL), F32)] * 4,
        compiler_params=_params([((N_DEV, 8, D_MODEL), F32)] + [((8, D_MODEL), F32)] * 7),
    )(gathered, w, m, v)


def _adamw_small(name, gathered, w, m, v):
    rows, cols = w.shape

    def body(p_ref, w_ref, m_ref, v_ref, g_out, d_out, m_out, v_out):
        g = p_ref[0]
        for j in range(1, N_DEV):
            g = g + p_ref[j]
        delta, m_new, v_new = _adamw_math(w_ref[...], g, m_ref[...], v_ref[...])
        g_out[...] = g
        d_out[...] = delta
        m_out[...] = m_new
        v_out[...] = v_new

    tr = _tile(rows, 512)
    spec = pl.BlockSpec((tr, cols), lambda r: (r, 0))
    return pl.pallas_call(
        body, name=name, out_shape=[jax.ShapeDtypeStruct((rows, cols), F32)] * 4, grid=(rows // tr,),
        in_specs=[pl.BlockSpec((N_DEV, tr, cols), lambda r: (0, r, 0)), spec, spec, spec], out_specs=[spec] * 4,
        compiler_params=_params([((N_DEV, tr, cols), F32)] + [((tr, cols), F32)] * 7, sem=("arbitrary",)),
    )(gathered, w, m, v)


def kernel(x, norm_mix_g, w_in, gmlp_ln_g, gmlp_ln_b, gmlp_w_s, gmlp_b_s, hgrn_lb_table, hgrn_norm_g, w_branch_a, w_branch_b, w_out, norm_ffn_g, w_gate_up, w_down, norm_final_g, loss_target, m_norm_mix_g, m_w_in, m_gmlp_ln_g, m_gmlp_ln_b, m_gmlp_w_s, m_gmlp_b_s, m_hgrn_lb_table, m_hgrn_norm_g, m_w_branch_a, m_w_branch_b, m_w_out, m_norm_ffn_g, m_w_gate_up, m_w_down, m_norm_final_g, v_norm_mix_g, v_w_in, v_gmlp_ln_g, v_gmlp_ln_b, v_gmlp_w_s, v_gmlp_b_s, v_hgrn_lb_table, v_hgrn_norm_g, v_w_branch_a, v_w_branch_b, v_w_out, v_norm_ffn_g, v_w_gate_up, v_w_down, v_norm_final_g):
    t = x.shape[1]
    x2d = x.reshape(t, D_MODEL)
    target = loss_target.reshape(t, D_MODEL)
    final_g = norm_final_g.reshape(1, D_MODEL)

    shards = [w_in[0].astype(BF16), w_branch_a[0].astype(BF16), w_branch_b[0].astype(BF16), w_out[0].astype(BF16),
              w_gate_up[0].T.astype(BF16), w_down[0].astype(BF16)]

    def rows_of(n):
        return lambda ref, j: ref.at[pl.ds(pl.multiple_of(j * n, 8), n)]

    gathered = [((N_DEV, D_MODEL, D_MODEL), BF16), ((D_MODEL, D_MODEL), BF16), ((D_MODEL, D_MODEL), BF16),
                ((D_MODEL, D_MODEL), BF16), ((N_DEV, FF_BLOCK, D_MODEL), BF16), ((D_FF, D_MODEL), BF16)]
    places = [lambda ref, j: ref.at[_pos_of_dev(j)], rows_of(BRANCH_ROWS), rows_of(BRANCH_ROWS), rows_of(BRANCH_ROWS),
              lambda ref, j: ref.at[j], rows_of(DOWN_ROWS)]
    w_in_g = _all_gather_balanced_async("w_in_all_gather", 9, shards[0], gathered[0], places[0], D_MODEL)
    w_in_sibling = _swap_with_sibling("w_in_from_sibling", shards[0])
    _, later = lax.optimization_barrier((w_in_sibling, shards[1:]))
    w_a, w_b, w_o, w_gu, w_dn = _all_gather_async("weights_all_gather", 0, later, gathered[1:], places[1:])

    core_i, chip_i = lax.axis_index("c"), 2 * lax.axis_index("x") + lax.axis_index("y")
    own_pos = jnp.stack([_pos_of_dev(2 * chip_i + core_i), _pos_of_dev(2 * chip_i + 1 - core_i)]).astype(jnp.int32)
    other_pos = jnp.stack([_pos_of_dev(2 * jnp.bitwise_xor(chip_i, q) + cc) for q in (1, 2, 3) for cc in (0, 1)]).astype(jnp.int32)
    proj, h, h_t = _proj_forward_own_chip(own_pos, x2d, norm_mix_g, shards[0], w_in_sibling)
    proj = _proj_forward_other_chips(other_pos, proj, h, w_in_g)
    bias_b = jnp.broadcast_to(gmlp_b_s[0][:, :, None], (GROUPS, GMLP_CHUNK, GMLP_CHUNK))
    a = _gmlp_forward(proj, gmlp_ln_g, gmlp_ln_b, gmlp_w_s[0], bias_b)
    og, o_saved, states = _hgrn_forward(proj, hgrn_lb_table, hgrn_norm_g)
    ya, yb, merged, x1, h2 = _branch_out_forward(a, og, proj, x2d, w_a, w_b, w_o, norm_ffn_g)
    gu, act, loss_tile, d_final_g, dx2, dx2b = _ffn_forward(h2, x1, w_gu, w_dn, target, final_g)

    core = lax.axis_index("c").astype(jnp.int32).reshape(1)
    chip = (2 * lax.axis_index("x") + lax.axis_index("y")).astype(jnp.int32).reshape(1)
    branch_rows, branch_shape = rows_of(BRANCH_ROWS), (BRANCH_ROWS, D_MODEL)
    branch_block = ((BRANCH_ROWS, D_MODEL), lambda q, r, c: (2 * q + c, 0))

    def chip_partials(names, grads, land, own_blocks):
        return [_chip_partial("chip_partial_" + nme, core, g_, blk, idx, l_)
                for nme, g_, (blk, idx), l_ in zip(names, grads, own_blocks, land)]

    whole = pl.BlockSpec((t, D_MODEL), lambda n: (0, 0))
    whole_t = pl.BlockSpec((D_MODEL, t), lambda n: (0, 0))
    col_blocks = [((t, D_MODEL), BF16), ((t, 256), BF16), ((D_MODEL, 256), F32)]

    def square_grad(name, a_, b_):
        return _weight_grad(name, a_, b_, whole, pl.BlockSpec((t, 256), lambda n: (0, n)), (D_MODEL, D_MODEL),
                            pl.BlockSpec((D_MODEL, 256), lambda n: (0, n)), D_MODEL // 256, col_blocks, False)

    dgu, dx1, dx1b, d_ffn_g = _ffn_backward(dx2b, dx2, gu, x1, w_gu, w_dn, norm_ffn_g)
    g_gu = _weight_grad(
        "grad_w_gate_up", dgu, h2, pl.BlockSpec((None, None, t, FF_BLOCK), lambda j: (j % 4, j // 4, 0, 0)), whole,
        (N_DEV, FF_BLOCK, D_MODEL), pl.BlockSpec((None, FF_BLOCK, D_MODEL), lambda j: (j, 0, 0)), N_DEV,
        [((t, 768), BF16), ((t, D_MODEL), BF16), ((FF_BLOCK, D_MODEL), F32)], False)
    g_dn = _weight_grad(
        "grad_w_down", act, dx2b, pl.BlockSpec((None, t, FF_BLOCK), lambda j: (j, 0, 0)), whole, (D_FF, D_MODEL),
        pl.BlockSpec((FF_BLOCK, D_MODEL), lambda j: (j, 0)), 4,
        [((t, 768), BF16), ((t, D_MODEL), BF16), ((FF_BLOCK, D_MODEL), F32)], False)
    names_f, grads_f = ["w_gate_up", "w_down"], [g_gu, g_dn]
    land_f = _exchange_sibling("ffn_grads_to_sibling", 2, grads_f, [lambda ref, j: ref.at[j], rows_of(DOWN_ROWS)],
                               [(FF_BLOCK, D_MODEL), (DOWN_ROWS, D_MODEL)])

    dx1b_later, _ = lax.optimization_barrier((dx1b, grads_f))
    dya, dyb, dproj, da, dog = _branch_out_backward(dx1b_later, ya, yb, proj, w_a, w_b, w_o)
    g_a = square_grad("grad_w_a", a, dya)
    g_b = square_grad("grad_w_b", og, dyb)
    g_o = square_grad("grad_w_out", merged, dx1b)
    names_b, grads_b = ["w_branch_a", "w_branch_b", "w_out"], [g_a, g_b, g_o]
    land_b = _exchange_sibling("branch_grads_to_sibling", 3, grads_b, [branch_rows] * 3, [branch_shape] * 3)

    part_f = chip_partials(names_f, grads_f, land_f,
                           [((None, DOWN_ROWS // 2, D_MODEL), lambda q, r, c: (2 * q + c, r, 0)),
                            ((DOWN_ROWS // 2, D_MODEL), lambda q, r, c: (2 * (2 * q + c) + r, 0))])
    landed_f = _exchange_chips("ffn_grads_to_chips", 5, part_f)

    dog, _ = lax.optimization_barrier((dog, part_f))
    dproj, d_hg_norm, d_lb = _hgrn_backward(dproj, dog, o_saved, states, proj, hgrn_lb_table, hgrn_norm_g)

    land_b, _ = lax.optimization_barrier((land_b, part_f))
    part_b = chip_partials(names_b, grads_b, land_b, [branch_block] * 3)
    landed_b = _exchange_chips("branch_grads_to_chips", 6, part_b)

    da, _ = lax.optimization_barrier((da, part_b))
    dproj, d_ln_g, d_ln_b, d_ws, d_bs = _gmlp_backward(dproj, da, proj, gmlp_ln_g, gmlp_ln_b, gmlp_w_s[0], bias_b)

    def packed(vals):
        return _pack_small(*vals)

    def flat_ws(a):
        return a.reshape(GROUPS * GMLP_CHUNK, GMLP_CHUNK)

    no_row = jnp.zeros((1, D_MODEL), F32)
    w_pack = packed([norm_mix_g, gmlp_ln_g, gmlp_ln_b, gmlp_b_s, hgrn_lb_table, hgrn_norm_g, norm_ffn_g, norm_final_g, no_row])
    m_pack = packed([m_norm_mix_g, m_gmlp_ln_g, m_gmlp_ln_b, m_gmlp_b_s, m_hgrn_lb_table, m_hgrn_norm_g, m_norm_ffn_g, m_norm_final_g, no_row])
    v_pack = packed([v_norm_mix_g, v_gmlp_ln_g, v_gmlp_ln_b, v_gmlp_b_s, v_hgrn_lb_table, v_hgrn_norm_g, v_norm_ffn_g, v_norm_final_g, no_row])
    small_partial = _pack_small(no_row, d_ln_g, d_ln_b, d_bs[:, :, 0], d_lb, d_hg_norm, d_ffn_g, d_final_g,
                                jnp.tile(loss_tile[0:1], (1, D_MODEL // 128)))
    small_all, ws_all = _all_gather_async(
        "small_grads_all_gather", 1, [small_partial, flat_ws(d_ws)],
        [((N_DEV, SMALL_ROWS, D_MODEL), F32), ((N_DEV, GROUPS * GMLP_CHUNK, GMLP_CHUNK), F32)],
        [lambda ref, j: ref.at[j], lambda ref, j: ref.at[j]])

    g_in = _weight_grad(
        "grad_w_in", h_t, dproj, whole_t, pl.BlockSpec((None, t, D_MODEL), lambda p: (p, 0, 0)), (N_DEV, D_MODEL, D_MODEL),
        pl.BlockSpec((None, D_MODEL, D_MODEL), lambda p: (p, 0, 0)), N_DEV,
        [((D_MODEL, t), BF16), ((t, D_MODEL), BF16), ((D_MODEL, D_MODEL), F32)], True)
    land_i = _exchange_sibling("w_in_grads_to_sibling", 4, [g_in], [lambda ref, j: ref.at[_pos_of_dev(j)]],
                               [(D_MODEL, D_MODEL)])

    big = {}
    for nme, own, lnd, w, m, v in zip(
            names_f + names_b, part_f + part_b, landed_f + landed_b,
            [w_gate_up, w_down, w_branch_a, w_branch_b, w_out], [m_w_gate_up, m_w_down, m_w_branch_a, m_w_branch_b, m_w_out],
            [v_w_gate_up, v_w_down, v_w_branch_a, v_w_branch_b, v_w_out]):
        flip = (lambda z: z.T) if nme == "w_gate_up" else (lambda z: z)
        big[nme] = [flip(o_)[None] for o_ in _adamw("adamw_" + nme, chip, own, lnd, flip(w[0]), flip(m[0]), flip(v[0]))]
    small, loss_row = _adamw_small_unpacked(small_all, w_pack, m_pack, v_pack)
    ws_outs = _adamw_small("adamw_w_s", ws_all, flat_ws(gmlp_w_s), flat_ws(m_gmlp_w_s), flat_ws(v_gmlp_w_s))
    land_i, _ = lax.optimization_barrier((land_i, (big, small, ws_outs)))
    part_i = chip_partials(["w_in"], [g_in], land_i,
                           [((None, 256, D_MODEL), lambda q, r, c: (_pos_of_dev(2 * q + c), r, 0))])
    landed_i = _exchange_chips("w_in_grads_to_chips", 7, part_i)

    dx1, _ = lax.optimization_barrier((dx1, part_i))
    grad_x, d_mix_g = _input_backward(dproj, w_in_g, x2d, dx1, norm_mix_g)
    big["w_in"] = [o_[None] for o_ in _adamw("adamw_w_in", chip, part_i[0], landed_i[0], w_in[0], m_w_in[0], v_w_in[0])]

    def row8(a):
        return jnp.pad(a, ((0, 7), (0, 0)))

    d_mix_g, _ = lax.optimization_barrier((d_mix_g, landed_i))
    (mix_all,) = _all_gather_async("mix_gain_grad_all_gather", 8, [row8(d_mix_g)], [((N_DEV, 8, D_MODEL), F32)],
                                   [lambda ref, j: ref.at[j]])
    mix_outs = _adamw_row("adamw_mix_gain", mix_all, norm_mix_g, m_norm_mix_g, v_norm_mix_g)
    small = [dict(p, norm_final_g=p["norm_final_g"][0], gmlp_w_s=ws.reshape(1, GROUPS, GMLP_CHUNK, GMLP_CHUNK), norm_mix_g=q)
             for p, ws, q in zip(small, ws_outs, mix_outs)]

    loss = loss_row[0, 0]
    order = ["norm_mix_g", "w_in", "gmlp_ln_g", "gmlp_ln_b", "gmlp_w_s", "gmlp_b_s", "hgrn_lb_table", "hgrn_norm_g",
             "w_branch_a", "w_branch_b", "w_out", "norm_ffn_g", "w_gate_up", "w_down", "norm_final_g"]
    outs = [loss, grad_x.reshape(1, t, D_MODEL)]
    for kind in range(4):
        for nme in order:
            outs.append(big[nme][kind] if nme in big else small[kind][nme])
    return tuple(outs)
```

```python
import functools

import jax
import jax.numpy as jnp
from jax import lax
from jax.experimental import pallas as pl
from jax.experimental.pallas import tpu as pltpu
from jax.experimental.pallas import tpu_sc as plsc

F32, BF16 = jnp.float32, jnp.bfloat16
D_MODEL = 1024
N_DEV = 8
HEADS = 8
HEAD_DIM = 128
GROUPS = 8
GMLP_CHUNK = 128
HGRN_CHUNK = 64
HGRN_SCALE = HEAD_DIM ** -0.5
D_FF = 2816
FF_BLOCK = D_FF // 4
DOWN_ROWS = D_FF // N_DEV
BRANCH_ROWS = D_MODEL // N_DEV
NORM_EPS = 1e-6
ADAM_LR, ADAM_B1, ADAM_B2, ADAM_EPS, ADAM_WD, ADAM_STEP = 0.001, 0.9, 0.999, 1e-08, 0.01, 10
SMALL_ROWS = 72
V7X_VMEM_BYTES = 64 * 1024 * 1024
VMEM_CAP = V7X_VMEM_BYTES - 6 * 1024 * 1024
MESH_ID = pl.DeviceIdType.MESH
ANY = pl.BlockSpec(memory_space=pl.ANY)
RESIDENT = pl.BlockSpec(memory_space=pltpu.VMEM)
Q_POS, U_POS, GATE_POS = 0, 4, 6


def _pos_of_dev(j):
    return jnp.where(j < 2, j + 4, jnp.where(j < 6, j - 2, j))


def _dev_of_pos(p):
    return jnp.where(p < 4, p + 2, jnp.where(p < 6, p - 4, p))


def _nbytes(shape, dtype):
    n = 1
    for s in shape:
        n *= s
    return n * jnp.dtype(dtype).itemsize


def _params(blocks, scratch=(), temps=0, sem=None):
    need = 2 * sum(_nbytes(s, d) for s, d in blocks) + sum(_nbytes(s, d) for s, d in scratch) + temps
    assert need + (4 << 20) <= VMEM_CAP, need
    return pltpu.CompilerParams(dimension_semantics=sem, vmem_limit_bytes=VMEM_CAP)


def _tile(n, pref):
    return pref if n % pref == 0 else n


def _dot(a, b):
    return jnp.dot(a, b, preferred_element_type=F32)


def _dot_nt(a, b):
    return lax.dot_general(a, b, (((1,), (1,)), ((), ())), preferred_element_type=F32)


def _dot_tn(a, b):
    return lax.dot_general(a, b, (((0,), (0,)), ((), ())), preferred_element_type=F32)


def _sigmoid(x):
    return 1.0 / (1.0 + jnp.exp(-x))


_GELU_C = 0.7978845608028654


def _gelu(x):
    return x * (0.5 * (1.0 + jnp.tanh(_GELU_C * (x + 0.044715 * (x * x * x)))))


def _gelu_grad(x):
    t = jnp.tanh(_GELU_C * (x + 0.044715 * (x * x * x)))
    return 0.5 * (1.0 + t) + 0.5 * x * (1.0 - t * t) * (_GELU_C * (1.0 + 3.0 * 0.044715 * x * x))


def _rms_stats(x):
    r = lax.rsqrt(jnp.mean(x * x, axis=-1, keepdims=True) + NORM_EPS)
    return r, x * r


def _rms_bwd(dy, x, g):
    r, xh = _rms_stats(x)
    dg = jnp.sum(dy * xh, axis=0, keepdims=True)
    dxh = dy * g
    dx = r * (dxh - xh * jnp.mean(dxh * xh, axis=-1, keepdims=True))
    return dx, dg


def _split3(x):
    hi = x.astype(BF16)
    r = x - hi.astype(F32)
    mid = r.astype(BF16)
    lo = (r - mid.astype(F32)).astype(BF16)
    return hi, mid, lo


def _mask_mm(mask_bf16, x):
    hi, mid, lo = _split3(x)
    return _dot(mask_bf16, hi) + _dot(mask_bf16, mid) + _dot(mask_bf16, lo)


def _place():
    return lax.axis_index("x"), lax.axis_index("y"), lax.axis_index("c")


def _gather_copies(src, out, send, recv, loc, slicers):
    n = len(src)
    x, y, c = _place()
    me, sib = (x, y, c), (x, y, 1 - c)
    chips = [(1 - x, y), (x, 1 - y), (1 - x, 1 - y)]

    def dev(p):
        return 4 * p[0] + 2 * p[1] + p[2]

    def rc(i, k, block, to, from_src=False):
        dst = slicers[i](out[i], dev(block))
        return pltpu.make_async_remote_copy(
            src_ref=src[i] if from_src else dst, dst_ref=dst, send_sem=send.at[7 * i + k],
            recv_sem=recv.at[7 * i + k], device_id=to, device_id_type=MESH_ID)

    mine = [pltpu.make_async_copy(src[i], slicers[i](out[i], dev(me)), loc.at[i]) for i in range(n)]
    for cp in mine:
        cp.start()
    first = []
    for i in range(n):
        first.append(rc(i, 0, me, sib, True))
        for j, chip in enumerate(chips):
            first.append(rc(i, 1 + j, me, (*chip, c), True))
    for cp in first:
        cp.start()
    passed = []
    for j, chip in enumerate(chips):
        for i in range(n):
            rc(i, 1 + j, (*chip, c), me).wait_recv()
            cp = rc(i, 4 + j, (*chip, c), sib)
            cp.start()
            passed.append(cp)
    for i in range(n):
        rc(i, 0, sib, me).wait_recv()
        for j, chip in enumerate(chips):
            rc(i, 4 + j, (*chip, 1 - c), me).wait_recv()
    for cp in first + passed:
        cp.wait_send()
    for cp in mine:
        cp.wait()


def _gather_copies_balanced(src, out, send, recv, loc, slicer, rows):
    x, y, c = _place()
    me, sib = (x, y, c), (x, y, 1 - c)
    xn, yn, dg = (1 - x, y), (x, 1 - y), (1 - x, 1 - y)
    half_rows = rows // 2

    def block(p):
        return slicer(out, 4 * p[0] + 2 * p[1] + p[2])

    def half(ref, h):
        return ref.at[pl.ds(h * half_rows, half_rows)]

    def rc(k, dst, to, from_src=False):
        return pltpu.make_async_remote_copy(src_ref=src if from_src else dst, dst_ref=dst, send_sem=send.at[k],
                                            recv_sem=recv.at[k], device_id=to, device_id_type=MESH_ID)

    mine = pltpu.make_async_copy(src, block(me), loc.at[0])
    mine.start()
    sends = [rc(0, block(me), sib, True), rc(1, block(me), (*xn, c), True), rc(2, block(me), (*yn, c), True)]
    for cp in sends:
        cp.start()

    def then(cp):
        cp.start()
        sends.append(cp)

    rc(1, block((*xn, c)), me).wait_recv()
    then(rc(3, half(block((*xn, c)), 0), (*yn, c)))
    then(rc(5, block((*xn, c)), sib))
    rc(2, block((*yn, c)), me).wait_recv()
    then(rc(4, half(block((*yn, c)), 1), (*xn, c)))
    then(rc(6, block((*yn, c)), sib))
    rc(3, half(block((*dg, c)), 0), me).wait_recv()
    then(rc(7, half(block((*dg, c)), 0), sib))
    rc(4, half(block((*dg, c)), 1), me).wait_recv()
    then(rc(8, half(block((*dg, c)), 1), sib))
    rc(0, block(sib), me).wait_recv()
    rc(5, block((*xn, 1 - c)), me).wait_recv()
    rc(6, block((*yn, 1 - c)), me).wait_recv()
    rc(7, half(block((*dg, 1 - c)), 0), me).wait_recv()
    rc(8, half(block((*dg, 1 - c)), 1), me).wait_recv()
    for cp in sends:
        cp.wait_send()
    mine.wait()


def _gather_scratch(n):
    return [pltpu.SemaphoreType.DMA((7 * n,)), pltpu.SemaphoreType.DMA((7 * n,)), pltpu.SemaphoreType.DMA((n,))]


def _handshake(peers):
    barrier = pltpu.get_barrier_semaphore()
    for peer in peers:
        pl.semaphore_signal(barrier, inc=1, device_id=peer, device_id_type=MESH_ID)
    pl.semaphore_wait(barrier, len(peers))


def _all_gather_async(name, collective_id, srcs, out_shapes, slicers):
    n = len(srcs)

    def body(*refs):
        x, y, c = _place()
        _handshake([(1 - x if dx else x, 1 - y if dy else y, 1 - c if dc else c)
                    for dx in (0, 1) for dy in (0, 1) for dc in (0, 1) if dx or dy or dc])
        _gather_copies(refs[:n], refs[n:2 * n], *refs[2 * n:], slicers)

    return _sequencer_call(name, collective_id, body, srcs, [jax.ShapeDtypeStruct(s, d) for s, d in out_shapes],
                           _gather_scratch(n))


def _all_gather_balanced_async(name, collective_id, src, out_shape, slicer, rows):
    def body(src_ref, out_ref, send, recv, loc):
        x, y, c = _place()
        _handshake([(1 - x if dx else x, 1 - y if dy else y, 1 - c if dc else c)
                    for dx in (0, 1) for dy in (0, 1) for dc in (0, 1) if dx or dy or dc])
        _gather_copies_balanced(src_ref, out_ref, send, recv, loc, slicer, rows)

    return _sequencer_call(name, collective_id, body, [src], [jax.ShapeDtypeStruct(*out_shape)],
                           [pltpu.SemaphoreType.DMA((9,)), pltpu.SemaphoreType.DMA((9,)), pltpu.SemaphoreType.DMA((1,))])[0]


def _sequencer_call(name, collective_id, body, operands, out_types, scratch):
    return pl.kernel(
        body, out_type=out_types, mesh=plsc.ScalarSubcoreMesh(axis_name="sequencer", num_cores=1), name=name,
        scratch_types=scratch, compiler_params=pltpu.CompilerParams(collective_id=collective_id),
    )(*operands)


def _exchange_sibling(name, collective_id, grads, shard_fns, shard_shapes):
    n = len(grads)

    def body(*refs):
        g, land = refs[:n], refs[n:2 * n]
        send, recv = refs[2 * n:]
        x, y, c = _place()
        _handshake([(x, y, 1 - c)])
        remote = []
        for i in range(n):
            for q in range(4):
                cp = pltpu.make_async_remote_copy(
                    src_ref=shard_fns[i](g[i], 2 * q + (1 - c)), dst_ref=land[i].at[q], send_sem=send.at[4 * i + q],
                    recv_sem=recv.at[4 * i + q], device_id=(x, y, 1 - c), device_id_type=MESH_ID)
                cp.start()
                remote.append(cp)
        for cp in remote:
            cp.wait()

    return _sequencer_call(name, collective_id, body, grads, [jax.ShapeDtypeStruct((4, *s), F32) for s in shard_shapes],
                           [pltpu.SemaphoreType.DMA((4 * n,)), pltpu.SemaphoreType.DMA((4 * n,))])


def _exchange_chips(name, collective_id, parts):
    n = len(parts)

    def body(*refs):
        part, out = refs[:n], refs[n:2 * n]
        send, recv = refs[2 * n:]
        x, y, c = _place()
        _handshake([(1 - x, y, c), (x, 1 - y, c), (1 - x, 1 - y, c)])
        remote = []
        for i in range(n):
            for s in range(3):
                qx = 1 - x if (s + 1) // 2 else x
                qy = 1 - y if (s + 1) % 2 else y
                cp = pltpu.make_async_remote_copy(
                    src_ref=part[i].at[2 * qx + qy], dst_ref=out[i].at[s], send_sem=send.at[3 * i + s],
                    recv_sem=recv.at[3 * i + s], device_id=(qx, qy, c), device_id_type=MESH_ID)
                cp.start()
                remote.append(cp)
        for cp in remote:
            cp.wait()

    return _sequencer_call(name, collective_id, body, parts,
                           [jax.ShapeDtypeStruct((3, *p.shape[1:]), p.dtype) for p in parts],
                           [pltpu.SemaphoreType.DMA((3 * n,)), pltpu.SemaphoreType.DMA((3 * n,))])


def _chip_partial(name, core, grad, own_block, own_index, land):
    _, rows, cols = land.shape
    tr = own_block[-2]

    def body(core_ref, a_ref, b_ref, o_ref):
        o_ref[...] = (a_ref[...] + b_ref[...]).astype(BF16)

    spec = pl.BlockSpec((None, tr, cols), lambda q, r, c: (q, r, 0))
    return pl.pallas_call(
        body, name=name, out_shape=jax.ShapeDtypeStruct(land.shape, BF16),
        grid_spec=pltpu.PrefetchScalarGridSpec(
            num_scalar_prefetch=1, grid=(4, rows // tr),
            in_specs=[pl.BlockSpec(own_block, lambda q, r, c: own_index(q, r, c[0])), spec], out_specs=spec),
        compiler_params=_params([((tr, cols), F32)] * 2 + [((tr, cols), BF16)], sem=("arbitrary", "arbitrary")),
    )(core, grad, land)


def _adamw_math(w, g, m, v):
    m = ADAM_B1 * m + (1.0 - ADAM_B1) * g
    v = ADAM_B2 * v + (1.0 - ADAM_B2) * (g * g)
    m_hat = m / (1.0 - ADAM_B1 ** ADAM_STEP)
    v_hat = v / (1.0 - ADAM_B2 ** ADAM_STEP)
    delta = -ADAM_LR * (m_hat / (jnp.sqrt(v_hat) + ADAM_EPS) + ADAM_WD * w)
    return delta, m, v


def _adamw(name, chip, own, landed, w, m, v):
    _, rows, cols = own.shape
    tr = _tile(rows, 256) if rows % 256 == 0 else _tile(rows, 176)

    def body(chip_ref, own_ref, l_ref, w_ref, m_ref, v_ref, g_out, d_out, m_out, v_out):
        g = own_ref[...].astype(F32)
        for s in range(3):
            g = g + l_ref[s].astype(F32)
        delta, m_new, v_new = _adamw_math(w_ref[...], g, m_ref[...], v_ref[...])
        g_out[...] = g
        d_out[...] = delta
        m_out[...] = m_new
        v_out[...] = v_new

    spec = pl.BlockSpec((tr, cols), lambda r, c: (r, 0))
    return pl.pallas_call(
        body, name=name, out_shape=[jax.ShapeDtypeStruct((rows, cols), F32)] * 4,
        grid_spec=pltpu.PrefetchScalarGridSpec(
            num_scalar_prefetch=1, grid=(rows // tr,),
            in_specs=[pl.BlockSpec((None, tr, cols), lambda r, c: (c[0], r, 0)),
                      pl.BlockSpec((3, tr, cols), lambda r, c: (0, r, 0)), spec, spec, spec],
            out_specs=[spec] * 4),
        compiler_params=_params([((4, tr, cols), own.dtype)] + [((tr, cols), F32)] * 7, sem=("arbitrary",)),
    )(chip, own, landed, w, m, v)


def _swap_with_sibling(name, x):
    def body(x_ref, o_ref, send, recv):
        px, py, c = _place()
        cp = pltpu.make_async_remote_copy(src_ref=x_ref, dst_ref=o_ref, send_sem=send, recv_sem=recv,
                                          device_id=(px, py, 1 - c), device_id_type=MESH_ID)
        cp.start()
        cp.wait()

    return pl.pallas_call(
        body, name=name, out_shape=jax.ShapeDtypeStruct(x.shape, x.dtype), in_specs=[ANY], out_specs=ANY,
        scratch_shapes=[pltpu.SemaphoreType.DMA, pltpu.SemaphoreType.DMA],
    )(x)


def _proj_forward_own_chip(positions, x, gain, w_own, w_sibling):
    t = x.shape[0]
    tm = _tile(t, 1024)

    def body(pos_ref, x_ref, g_ref, wo_ref, ws_ref, o_ref, h_ref, ht_ref):
        @pl.when(pl.program_id(1) == 0)
        def _():
            _, xh = _rms_stats(x_ref[...])
            h = (xh * g_ref[...]).astype(BF16)
            h_ref[...] = h
            ht_ref[...] = h.T
            o_ref[...] = _dot(h, wo_ref[...])

        @pl.when(pl.program_id(1) == 1)
        def _():
            o_ref[...] = _dot(h_ref[...], ws_ref[...])

    tok = pl.BlockSpec((tm, D_MODEL), lambda m, k, pos: (m, 0))
    return pl.pallas_call(
        body, name="proj_fwd_own_chip",
        out_shape=[jax.ShapeDtypeStruct((N_DEV, t, D_MODEL), F32), jax.ShapeDtypeStruct((t, D_MODEL), BF16),
                   jax.ShapeDtypeStruct((D_MODEL, t), BF16)],
        grid_spec=pltpu.PrefetchScalarGridSpec(
            num_scalar_prefetch=1, grid=(t // tm, 2),
            in_specs=[tok, pl.BlockSpec((1, D_MODEL), lambda m, k, pos: (0, 0)), RESIDENT, RESIDENT],
            out_specs=[pl.BlockSpec((None, tm, D_MODEL), lambda m, k, pos: (pos[k], m, 0)), tok,
                       pl.BlockSpec((D_MODEL, tm), lambda m, k, pos: (0, m))]),
        compiler_params=_params([((tm, D_MODEL), F32)] * 2 + [((tm, D_MODEL), BF16)] * 2,
                                scratch=[((2, D_MODEL, D_MODEL), BF16)], temps=6 << 20, sem=("arbitrary", "arbitrary")),
    )(positions, x, gain, w_own, w_sibling)


def _proj_forward_other_chips(positions, proj, h, w_in_g):
    t = h.shape[0]
    tm = _tile(t, 1024)

    def body(pos_ref, _, h_ref, w_ref, o_ref):
        o_ref[...] = _dot(h_ref[...], w_ref[pos_ref[pl.program_id(1)]])

    return pl.pallas_call(
        body, name="proj_fwd_other_chips", out_shape=jax.ShapeDtypeStruct(proj.shape, F32),
        grid_spec=pltpu.PrefetchScalarGridSpec(
            num_scalar_prefetch=1, grid=(t // tm, N_DEV - 2),
            in_specs=[ANY, pl.BlockSpec((tm, D_MODEL), lambda m, k, pos: (m, 0)), RESIDENT],
            out_specs=pl.BlockSpec((None, tm, D_MODEL), lambda m, k, pos: (pos[k], m, 0))),
        input_output_aliases={1: 0},
        compiler_params=_params([((tm, D_MODEL), F32), ((tm, D_MODEL), BF16)], scratch=[((N_DEV, D_MODEL, D_MODEL), BF16)],
                                temps=6 << 20, sem=("arbitrary", "arbitrary")),
    )(positions, proj, h, w_in_g)


def _masked_ws(ws_ref, g):
    row = lax.broadcasted_iota(jnp.int32, (GMLP_CHUNK, GMLP_CHUNK), 0)
    col = lax.broadcasted_iota(jnp.int32, (GMLP_CHUNK, GMLP_CHUNK), 1)
    return jnp.where(row >= col, ws_ref[g], 0.0).astype(BF16)


def _gmlp_forward(proj, ln_g, ln_b, w_s, bias_b):
    t = proj.shape[1]
    tm = _tile(t, 256)
    chunks = tm // GMLP_CHUNK

    def body(u_ref, v_ref, lng_ref, lnb_ref, ws_ref, bias_ref, a_ref, vn_scr):
        vv = _gelu(v_ref[...])
        mu = jnp.mean(vv, axis=-1, keepdims=True)
        cen = vv - mu
        var = jnp.mean(cen * cen, axis=-1, keepdims=True)
        vn_scr[...] = ((cen * lax.rsqrt(var + NORM_EPS)) * lng_ref[...] + lnb_ref[...]).astype(BF16)
        for g in range(GROUPS):
            wm = _masked_ws(ws_ref, g)
            cols = slice(g * HEAD_DIM, (g + 1) * HEAD_DIM)
            for c in range(chunks):
                rows = slice(c * GMLP_CHUNK, (c + 1) * GMLP_CHUNK)
                mixed = _dot(wm, vn_scr[rows, cols]) + bias_ref[g]
                a_ref[rows, cols] = (_gelu(u_ref[rows, cols]) * mixed).astype(BF16)

    small = pl.BlockSpec((GROUPS, GMLP_CHUNK, GMLP_CHUNK), lambda m: (0, 0, 0))
    vec = pl.BlockSpec((1, D_MODEL), lambda m: (0, 0))
    return pl.pallas_call(
        body, name="gmlp_fwd", out_shape=jax.ShapeDtypeStruct((t, D_MODEL), BF16), grid=(t // tm,),
        in_specs=[pl.BlockSpec((None, tm, D_MODEL), lambda m: (U_POS, m, 0)),
                  pl.BlockSpec((None, tm, D_MODEL), lambda m: (U_POS + 1, m, 0)), vec, vec, small, small],
        out_specs=pl.BlockSpec((tm, D_MODEL), lambda m: (m, 0)),
        scratch_shapes=[pltpu.VMEM((tm, D_MODEL), BF16)],
        compiler_params=_params([((tm, D_MODEL), F32)] * 2 + [((tm, D_MODEL), BF16)] + [((8, 128, 128), F32)] * 2,
                                scratch=[((tm, D_MODEL), BF16)], temps=8 << 20, sem=("arbitrary",)),
    )(proj, proj, ln_g, ln_b, w_s, bias_b)


def _lower_bound(tab_ref):
    t0, t1 = tab_ref[0:1, :], tab_ref[1:2, :]
    mx = jnp.maximum(t0, t1)
    e0, e1 = jnp.exp(t0 - mx), jnp.exp(t1 - mx)
    return e0 / (e0 + e1)


def _tri_masks():
    row = lax.broadcasted_iota(jnp.int32, (HGRN_CHUNK, HGRN_CHUNK), 0)
    col = lax.broadcasted_iota(jnp.int32, (HGRN_CHUNK, HGRN_CHUNK), 1)
    return row >= col, row <= col


def _chunk_rows(c):
    return slice(c * HGRN_CHUNK, (c + 1) * HGRN_CHUNK)


def _per_chunk(x, nc, fn):
    return jnp.concatenate([fn(x[_chunk_rows(c)]) for c in range(nc)], axis=0)


def _chunk_row_bcast(x, nc, i):
    return _per_chunk(x, nc, lambda xc: jnp.broadcast_to(xc[i:i + 1, :], (HGRN_CHUNK, HEAD_DIM)))


def _hgrn_gates(q, fl, lb, nc):
    lower, _ = _tri_masks()
    lower = lower.astype(BF16)
    s = _sigmoid(fl)
    f = lb + (1.0 - lb) * s
    k = 1.0 - f
    hi, mid, lo = _split3(jnp.log(f))
    a = jnp.concatenate([_dot(lower, hi[_chunk_rows(c)]) + _dot(lower, mid[_chunk_rows(c)]) + _dot(lower, lo[_chunk_rows(c)])
                         for c in range(nc)], axis=0)
    a_mid = _chunk_row_bcast(a, nc, HGRN_CHUNK // 2 - 1)
    a_last = _chunk_row_bcast(a, nc, HGRN_CHUNK - 1)
    qs = q * HGRN_SCALE
    e_in, e_out, e_end, e_all = jnp.exp(a - a_mid), jnp.exp(a_mid - a), jnp.exp(a_last - a), jnp.exp(a)
    decay = [jnp.exp(a[c * HGRN_CHUNK + HGRN_CHUNK - 1:(c + 1) * HGRN_CHUNK, :]) for c in range(nc)]
    return dict(s=s, f=f, k=k, decay=decay, e_in=e_in, e_out=e_out, e_end=e_end, e_all=e_all,
                qi=qs * e_in, ki=k * e_out, kd=k * e_end, qe=qs * e_all)


def _hgrn_forward(proj, lb_table, norm_g):
    t = proj.shape[1]
    tb = _tile(t, 1024)
    nc = tb // HGRN_CHUNK
    n_chunks = t // HGRN_CHUNK

    def body(q_ref, f_ref, i_ref, g_ref, tab_ref, ng_ref, og_ref, o_ref, st_ref, state):
        @pl.when(pl.program_id(1) == 0)
        def _():
            state[...] = jnp.zeros_like(state)

        lower, _ = _tri_masks()
        gt = _hgrn_gates(q_ref[...], f_ref[...], _lower_bound(tab_ref), nc)
        qi, ki, kd, qe = (gt[n].astype(BF16) for n in ("qi", "ki", "kd", "qe"))
        vb = i_ref[...].astype(BF16)
        o_intra, d_state = [], []
        for c in range(nc):
            rows = _chunk_rows(c)
            p = jnp.where(lower, _dot_nt(qi[rows], ki[rows]), 0.0).astype(BF16)
            o_intra.append(_dot(p, vb[rows]))
            d_state.append(_dot_tn(vb[rows], kd[rows]))
        st = state[...]
        outs = []
        for c in range(nc):
            st_ref[c] = st
            outs.append(o_intra[c] + _dot_nt(qe[_chunk_rows(c)], st.astype(BF16)))
            st = st * gt["decay"][c] + d_state[c]
        state[...] = st
        o = jnp.concatenate(outs, axis=0)
        o_ref[...] = o
        _, oh = _rms_stats(o)
        gz = g_ref[...]
        og_ref[...] = ((oh * ng_ref[...]) * (gz * _sigmoid(gz))).astype(BF16)

    def blk(p):
        return pl.BlockSpec((None, tb, HEAD_DIM), lambda h, n: (p, n, h))

    out_blk = pl.BlockSpec((tb, HEAD_DIM), lambda h, n: (n, h))
    return pl.pallas_call(
        body, name="hgrn_fwd",
        out_shape=[jax.ShapeDtypeStruct((t, D_MODEL), BF16), jax.ShapeDtypeStruct((t, D_MODEL), F32),
                   jax.ShapeDtypeStruct((HEADS, n_chunks, HEAD_DIM, HEAD_DIM), F32)],
        grid=(HEADS, t // tb),
        in_specs=[blk(Q_POS), blk(Q_POS + 1), blk(Q_POS + 2), blk(Q_POS + 3),
                  pl.BlockSpec((2, HEAD_DIM), lambda h, n: (0, h)), pl.BlockSpec((1, HEAD_DIM), lambda h, n: (0, h))],
        out_specs=[out_blk, out_blk, pl.BlockSpec((None, nc, HEAD_DIM, HEAD_DIM), lambda h, n: (h, n, 0, 0))],
        scratch_shapes=[pltpu.VMEM((HEAD_DIM, HEAD_DIM), F32)],
        compiler_params=_params([((tb, HEAD_DIM), F32)] * 6 + [((nc, HEAD_DIM, HEAD_DIM), F32)], temps=8 << 20,
                                sem=("arbitrary", "arbitrary")),
    )(proj, proj, proj, proj, lb_table, norm_g)


def _branch_out_forward(a, og, proj, x, w_a, w_b, w_out, ffn_g):
    t = x.shape[0]
    tm = _tile(t, 512)

    def body(a_ref, og_ref, ga_ref, gb_ref, x_ref, wa_ref, wb_ref, wo_ref, g_ref, ya_ref, yb_ref, mg_ref, x1_ref, h2_ref):
        ya = _dot(a_ref[...], wa_ref[...])
        yb = _dot(og_ref[...], wb_ref[...])
        ya_ref[...] = ya
        yb_ref[...] = yb
        merged = (_sigmoid(ga_ref[...]) * ya + _sigmoid(gb_ref[...]) * yb).astype(BF16)
        mg_ref[...] = merged
        x1 = x_ref[...] + _dot(merged, wo_ref[...])
        x1_ref[...] = x1
        _, xh = _rms_stats(x1)
        h2_ref[...] = (xh * g_ref[...]).astype(BF16)

    tok = pl.BlockSpec((tm, D_MODEL), lambda m: (m, 0))
    return pl.pallas_call(
        body, name="branch_out_fwd",
        out_shape=[jax.ShapeDtypeStruct((t, D_MODEL), F32), jax.ShapeDtypeStruct((t, D_MODEL), F32),
                   jax.ShapeDtypeStruct((t, D_MODEL), BF16), jax.ShapeDtypeStruct((t, D_MODEL), F32),
                   jax.ShapeDtypeStruct((t, D_MODEL), BF16)],
        grid=(t // tm,),
        in_specs=[tok, tok, pl.BlockSpec((None, tm, D_MODEL), lambda m: (GATE_POS, m, 0)),
                  pl.BlockSpec((None, tm, D_MODEL), lambda m: (GATE_POS + 1, m, 0)), tok, RESIDENT, RESIDENT, RESIDENT,
                  pl.BlockSpec((1, D_MODEL), lambda m: (0, 0))],
        out_specs=[tok] * 5,
        compiler_params=_params([((tm, D_MODEL), BF16)] * 4 + [((tm, D_MODEL), F32)] * 6, scratch=[((D_MODEL, D_MODEL), BF16)] * 3,
                                temps=8 << 20, sem=("arbitrary",)),
    )(a, og, proj, proj, x, w_a, w_b, w_out, ffn_g)


def _ffn_forward(h2, x1, w_gu, w_down, target, final_g):
    t = x1.shape[0]
    tm = _tile(t, 512)

    def body(h_ref, wgu_ref, wd_ref, x1_ref, t_ref, g_ref, gu_ref, act_ref, loss_ref, dg_ref, dx_ref, dxb_ref, acc):
        m, j = pl.program_id(0), pl.program_id(1)

        @pl.when((m == 0) & (j == 0))
        def _():
            loss_ref[...] = jnp.zeros_like(loss_ref)
            dg_ref[...] = jnp.zeros_like(dg_ref)

        h = h_ref[...]
        gate = _dot_nt(h, wgu_ref[j])
        up = _dot_nt(h, wgu_ref[j + 4])
        gu_ref[0] = gate
        gu_ref[1] = up
        act = ((gate * _sigmoid(gate)) * up).astype(BF16)
        act_ref[...] = act
        part = _dot(act, wd_ref[j])

        @pl.when(j == 0)
        def _():
            acc[...] = part

        @pl.when((j > 0) & (j < 3))
        def _():
            acc[...] += part

        @pl.when(j == 3)
        def _():
            x2 = x1_ref[...] + (acc[...] + part)
            g = g_ref[...]
            r, xh = _rms_stats(x2)
            err = xh * g - t_ref[...]
            loss_ref[...] += 0.5 * jnp.sum(jnp.mean(err * err, axis=-1, keepdims=True), axis=0, keepdims=True)
            dy = err * (1.0 / D_MODEL)
            dg_ref[...] += jnp.sum(dy * xh, axis=0, keepdims=True)
            dxh = dy * g
            dx = r * (dxh - xh * jnp.mean(dxh * xh, axis=-1, keepdims=True))
            dx_ref[...] = dx
            dxb_ref[...] = dx.astype(BF16)

    tok = pl.BlockSpec((tm, D_MODEL), lambda m, j: (m, 0))
    vec = pl.BlockSpec((1, D_MODEL), lambda m, j: (0, 0))
    return pl.pallas_call(
        body, name="ffn_fwd",
        out_shape=[jax.ShapeDtypeStruct((4, 2, t, FF_BLOCK), F32), jax.ShapeDtypeStruct((4, t, FF_BLOCK), BF16),
                   jax.ShapeDtypeStruct((8, 128), F32), jax.ShapeDtypeStruct((1, D_MODEL), F32),
                   jax.ShapeDtypeStruct((t, D_MODEL), F32), jax.ShapeDtypeStruct((t, D_MODEL), BF16)],
        grid=(t // tm, 4),
        in_specs=[tok, RESIDENT, RESIDENT, tok, tok, vec],
        out_specs=[pl.BlockSpec((None, 2, tm, FF_BLOCK), lambda m, j: (j, 0, m, 0)),
                   pl.BlockSpec((None, tm, FF_BLOCK), lambda m, j: (j, m, 0)),
                   pl.BlockSpec((8, 128), lambda m, j: (0, 0)), vec, tok, tok],
        scratch_shapes=[pltpu.VMEM((tm, D_MODEL), F32)],
        compiler_params=_params([((tm, D_MODEL), BF16), ((tm, D_MODEL), F32), ((tm, D_MODEL), F32), ((2, tm, 768), F32),
                                 ((tm, 768), BF16), ((tm, D_MODEL), F32), ((tm, D_MODEL), BF16)],
                                scratch=[((tm, D_MODEL), F32), ((N_DEV, FF_BLOCK, D_MODEL), BF16), ((D_FF, D_MODEL), BF16)],
                                temps=6 << 20, sem=("arbitrary", "arbitrary")),
    )(h2, w_gu, w_down.reshape(4, FF_BLOCK, D_MODEL), x1, target, final_g)


def _ffn_backward(dx2b, dx2, gu, x1, w_gu, w_down, ffn_g):
    t = x1.shape[0]
    tm = _tile(t, 512)

    def body(dxb_ref, dx2_ref, gu_ref, x1_ref, wgu_ref, wd_ref, g_ref, dgu_ref, dx1_ref, dx1b_ref, dg_ref, acc, prev):
        m, j = pl.program_id(0), pl.program_id(1)

        @pl.when((m == 0) & (j == 0))
        def _():
            dg_ref[...] = jnp.zeros_like(dg_ref)

        @pl.when(j == 0)
        def _():
            prev[...] = jnp.zeros_like(prev)
            acc[...] = jnp.zeros_like(acc)

        jm1 = jnp.maximum(j - 1, 0)
        acc[...] += _dot(prev[0], wgu_ref[jm1]) + _dot(prev[1], wgu_ref[jm1 + 4])
        dact = _dot_nt(dxb_ref[...], wd_ref[j])
        gate, up = gu_ref[0], gu_ref[1]
        sg = _sigmoid(gate)
        dgate = (dact * up * (sg * (1.0 + gate * (1.0 - sg)))).astype(BF16)
        dup = (dact * (gate * sg)).astype(BF16)
        dgu_ref[0] = dgate
        dgu_ref[1] = dup
        prev[0] = dgate
        prev[1] = dup

        @pl.when(j == 3)
        def _():
            dh2 = acc[...] + (_dot(prev[0], wgu_ref[3]) + _dot(prev[1], wgu_ref[7]))
            dx, dg = _rms_bwd(dh2, x1_ref[...], g_ref[...])
            dx1 = dx2_ref[...] + dx
            dx1_ref[...] = dx1
            dx1b_ref[...] = dx1.astype(BF16)
            dg_ref[...] += dg

    tok = pl.BlockSpec((tm, D_MODEL), lambda m, j: (m, 0))
    vec = pl.BlockSpec((1, D_MODEL), lambda m, j: (0, 0))
    gu_spec = pl.BlockSpec((None, 2, tm, FF_BLOCK), lambda m, j: (j, 0, m, 0))
    return pl.pallas_call(
        body, name="ffn_bwd",
        out_shape=[jax.ShapeDtypeStruct((4, 2, t, FF_BLOCK), BF16), jax.ShapeDtypeStruct((t, D_MODEL), F32),
                   jax.ShapeDtypeStruct((t, D_MODEL), BF16), jax.ShapeDtypeStruct((1, D_MODEL), F32)],
        grid=(t // tm, 4),
        in_specs=[tok, tok, gu_spec, tok, RESIDENT, RESIDENT, vec],
        out_specs=[gu_spec, tok, tok, vec],
        scratch_shapes=[pltpu.VMEM((tm, D_MODEL), F32), pltpu.VMEM((2, tm, FF_BLOCK), BF16)],
        compiler_params=_params([((tm, D_MODEL), BF16), ((tm, D_MODEL), F32), ((2, tm, 768), F32), ((tm, D_MODEL), F32),
                                 ((2, tm, 768), BF16), ((tm, D_MODEL), F32), ((tm, D_MODEL), BF16)],
                                scratch=[((tm, D_MODEL), F32), ((2, tm, 768), BF16), ((N_DEV, FF_BLOCK, D_MODEL), BF16),
                                         ((D_FF, D_MODEL), BF16)],
                                temps=4 << 20, sem=("arbitrary", "arbitrary")),
    )(dx2b, dx2, gu, x1, w_gu, w_down.reshape(4, FF_BLOCK, D_MODEL), ffn_g)


def _branch_out_backward(dx1b, ya, yb, proj, w_a, w_b, w_out):
    t = ya.shape[0]
    tm = _tile(t, 512)

    def body(dx_ref, ya_ref, yb_ref, ga_ref, gb_ref, wa_ref, wb_ref, wo_ref, dya_ref, dyb_ref, dgate_ref, da_ref, dog_ref):
        dm = _dot_nt(dx_ref[...], wo_ref[...])
        sa, sb = _sigmoid(ga_ref[...]), _sigmoid(gb_ref[...])
        dya = (dm * sa).astype(BF16)
        dyb = (dm * sb).astype(BF16)
        dya_ref[...] = dya
        dyb_ref[...] = dyb
        dgate_ref[0] = (dm * ya_ref[...] * (sa * (1.0 - sa))).astype(BF16)
        dgate_ref[1] = (dm * yb_ref[...] * (sb * (1.0 - sb))).astype(BF16)
        da_ref[...] = _dot_nt(dya, wa_ref[...])
        dog_ref[...] = _dot_nt(dyb, wb_ref[...])

    tok = pl.BlockSpec((tm, D_MODEL), lambda m: (m, 0))
    return pl.pallas_call(
        body, name="branch_out_bwd",
        out_shape=[jax.ShapeDtypeStruct((t, D_MODEL), BF16), jax.ShapeDtypeStruct((t, D_MODEL), BF16),
                   jax.ShapeDtypeStruct((N_DEV, t, D_MODEL), BF16), jax.ShapeDtypeStruct((t, D_MODEL), F32),
                   jax.ShapeDtypeStruct((t, D_MODEL), F32)],
        grid=(t // tm,),
        in_specs=[tok, tok, tok, pl.BlockSpec((None, tm, D_MODEL), lambda m: (GATE_POS, m, 0)),
                  pl.BlockSpec((None, tm, D_MODEL), lambda m: (GATE_POS + 1, m, 0)), RESIDENT, RESIDENT, RESIDENT],
        out_specs=[tok, tok, pl.BlockSpec((2, tm, D_MODEL), lambda m: (GATE_POS // 2, m, 0)), tok, tok],
        compiler_params=_params([((tm, D_MODEL), BF16)] * 5 + [((tm, D_MODEL), F32)] * 6, scratch=[((D_MODEL, D_MODEL), BF16)] * 3,
                                temps=8 << 20, sem=("arbitrary",)),
    )(dx1b, ya, yb, proj, proj, w_a, w_b, w_out)


def _hgrn_backward(dproj, dog, o_saved, states, proj, lb_table, norm_g):
    t = proj.shape[1]
    tb = _tile(t, 1024)
    nc = tb // HGRN_CHUNK
    nb = t // tb

    def body(_, dog_ref, o_ref, st_ref, q_ref, f_ref, i_ref, g_ref, tab_ref, ng_ref, dp_ref, dng_ref, dtab_ref, gstate):
        @pl.when(pl.program_id(1) == 0)
        def _():
            gstate[...] = jnp.zeros_like(gstate)
            dng_ref[...] = jnp.zeros_like(dng_ref)
            dtab_ref[...] = jnp.zeros_like(dtab_ref)

        lb = _lower_bound(tab_ref)
        ng = ng_ref[...]
        lower, upper = _tri_masks()
        gt = _hgrn_gates(q_ref[...], f_ref[...], lb, nc)
        qi, ki, kd, qe = (gt[n].astype(BF16) for n in ("qi", "ki", "kd", "qe"))
        vb = i_ref[...].astype(BF16)
        o, gz, d_og = o_ref[...], g_ref[...], dog_ref[...]
        r, oh = _rms_stats(o)
        sg = _sigmoid(gz)
        d_on = d_og * (gz * sg)
        dgz = d_og * (oh * ng) * (sg * (1.0 + gz * (1.0 - sg)))
        dng_ref[...] += jnp.sum(d_on * oh, axis=0, keepdims=True)
        doh = d_on * ng
        dob = (r * (doh - oh * jnp.mean(doh * oh, axis=-1, keepdims=True))).astype(BF16)
        dv_intra, dqi, dki, dqe, g_upd = [], [], [], [], []
        for c in range(nc):
            rows = _chunk_rows(c)
            p = jnp.where(lower, _dot_nt(qi[rows], ki[rows]), 0.0).astype(BF16)
            dv_intra.append(_dot_tn(p, dob[rows]))
            dp = jnp.where(lower, _dot_nt(dob[rows], vb[rows]), 0.0).astype(BF16)
            dqi.append(_dot(dp, ki[rows]))
            dki.append(_dot_tn(dp, qi[rows]))
            dqe.append(_dot(dob[rows], st_ref[c].astype(BF16)))
            g_upd.append(_dot_tn(dob[rows], qe[rows]))
        g_after = [None] * nc
        g = gstate[...]
        for c in reversed(range(nc)):
            g_after[c] = g
            g = g * gt["decay"][c] + g_upd[c]
        gstate[...] = g
        dkd, dv, da_last = [], [], []
        for c in range(nc):
            rows = _chunk_rows(c)
            gb = g_after[c].astype(BF16)
            dkd.append(_dot(vb[rows], gb))
            dv.append(dv_intra[c] + _dot_nt(kd[rows], gb))
            da_last.append(jnp.sum(g_after[c] * st_ref[c], axis=0, keepdims=True) * gt["decay"][c])
        dqi, dki, dqe, dkd, dv = (jnp.concatenate(z, axis=0) for z in (dqi, dki, dqe, dkd, dv))
        dqs = dqi * gt["e_in"] + dqe * gt["e_all"]
        dk = dki * gt["e_out"] + dkd * gt["e_end"]
        t_in, t_out, t_end = dqi * gt["qi"], dki * gt["ki"], dkd * gt["kd"]
        da = t_in - t_out + dqe * gt["qe"] - t_end
        row = lax.broadcasted_iota(jnp.int32, (HGRN_CHUNK, HEAD_DIM), 0)
        d_mid = t_out - t_in
        pieces = []
        for c in range(nc):
            rows = _chunk_rows(c)
            da_mid = jnp.sum(d_mid[rows], axis=0, keepdims=True)
            da_end = jnp.sum(t_end[rows], axis=0, keepdims=True) + da_last[c]
            da_c = da[rows] + jnp.where(row == HGRN_CHUNK // 2 - 1, da_mid, 0.0) + jnp.where(row == HGRN_CHUNK - 1, da_end, 0.0)
            pieces.append(_mask_mm(upper.astype(BF16), da_c))
        df = jnp.concatenate(pieces, axis=0) / gt["f"] - dk
        s = gt["s"]
        dlb = jnp.sum(df * (1.0 - s), axis=0, keepdims=True)
        dp_ref[0] = (dqs * HGRN_SCALE).astype(BF16)
        dp_ref[1] = (df * (1.0 - lb) * (s * (1.0 - s))).astype(BF16)
        dp_ref[2] = dv.astype(BF16)
        dp_ref[3] = dgz.astype(BF16)
        dt0 = dlb * (lb * (1.0 - lb))
        dtab_ref[0:1, :] += dt0
        dtab_ref[1:2, :] -= dt0

    def blk(p):
        return pl.BlockSpec((None, tb, HEAD_DIM), lambda h, n: (p, nb - 1 - n, h))

    tok = pl.BlockSpec((tb, HEAD_DIM), lambda h, n: (nb - 1 - n, h))
    return pl.pallas_call(
        body, name="hgrn_bwd",
        out_shape=[jax.ShapeDtypeStruct((N_DEV, t, D_MODEL), BF16), jax.ShapeDtypeStruct((1, D_MODEL), F32),
                   jax.ShapeDtypeStruct((2, D_MODEL), F32)],
        grid=(HEADS, nb),
        in_specs=[ANY, tok, tok, pl.BlockSpec((None, nc, HEAD_DIM, HEAD_DIM), lambda h, n: (h, nb - 1 - n, 0, 0)),
                  blk(Q_POS), blk(Q_POS + 1), blk(Q_POS + 2), blk(Q_POS + 3),
                  pl.BlockSpec((2, HEAD_DIM), lambda h, n: (0, h)), pl.BlockSpec((1, HEAD_DIM), lambda h, n: (0, h))],
        out_specs=[pl.BlockSpec((4, tb, HEAD_DIM), lambda h, n: (0, nb - 1 - n, h)),
                   pl.BlockSpec((1, HEAD_DIM), lambda h, n: (0, h)), pl.BlockSpec((2, HEAD_DIM), lambda h, n: (0, h))],
        scratch_shapes=[pltpu.VMEM((HEAD_DIM, HEAD_DIM), F32)],
        input_output_aliases={0: 0},
        compiler_params=_params([((tb, HEAD_DIM), F32)] * 6 + [((nc, HEAD_DIM, HEAD_DIM), F32)] + [((4, tb, HEAD_DIM), BF16)],
                                temps=8 << 20, sem=("arbitrary", "arbitrary")),
    )(dproj, dog, o_saved, states, proj, proj, proj, proj, lb_table, norm_g)


def _gmlp_backward(dproj, da, proj, ln_g, ln_b, w_s, bias_b):
    t = proj.shape[1]
    tm = _tile(t, 256)
    chunks = tm // GMLP_CHUNK

    def body(_, da_ref, u_ref, v_ref, lng_ref, lnb_ref, ws_ref, bias_ref, dp_ref, dlng_ref, dlnb_ref, dws_ref, dbs_ref,
             vn_scr, dvn_scr):
        @pl.when(pl.program_id(0) == 0)
        def _():
            dlng_ref[...] = jnp.zeros_like(dlng_ref)
            dlnb_ref[...] = jnp.zeros_like(dlnb_ref)
            dws_ref[...] = jnp.zeros_like(dws_ref)
            dbs_ref[...] = jnp.zeros_like(dbs_ref)

        v = v_ref[...]
        vv = _gelu(v)
        mu = jnp.mean(vv, axis=-1, keepdims=True)
        cen = vv - mu
        rstd = lax.rsqrt(jnp.mean(cen * cen, axis=-1, keepdims=True) + NORM_EPS)
        vhat = cen * rstd
        lng = lng_ref[...]
        vn_scr[...] = (vhat * lng + lnb_ref[...]).astype(BF16)
        row = lax.broadcasted_iota(jnp.int32, (GMLP_CHUNK, GMLP_CHUNK), 0)
        col = lax.broadcasted_iota(jnp.int32, (GMLP_CHUNK, GMLP_CHUNK), 1)
        for g in range(GROUPS):
            wm = _masked_ws(ws_ref, g)
            cols = slice(g * HEAD_DIM, (g + 1) * HEAD_DIM)
            dws = jnp.zeros((GMLP_CHUNK, GMLP_CHUNK), F32)
            dbs = jnp.zeros((GMLP_CHUNK, GMLP_CHUNK), F32)
            for c in range(chunks):
                rows = slice(c * GMLP_CHUNK, (c + 1) * GMLP_CHUNK)
                vn = vn_scr[rows, cols]
                mixed = _dot(wm, vn) + bias_ref[g]
                u = u_ref[rows, cols]
                d_a = da_ref[rows, cols]
                dp_ref[0, rows, cols] = (d_a * mixed * _gelu_grad(u)).astype(BF16)
                dmix = d_a * _gelu(u)
                dmb = dmix.astype(BF16)
                dbs = dbs + dmix
                dws = dws + _dot_nt(dmb, vn)
                dvn_scr[rows, cols] = _dot_tn(wm, dmb)
            dws_ref[g] += jnp.where(row >= col, dws, 0.0)
            dbs_ref[g] += jnp.broadcast_to(jnp.sum(dbs, axis=-1, keepdims=True), (GMLP_CHUNK, GMLP_CHUNK))
        dvn = dvn_scr[...]
        dlng_ref[...] += jnp.sum(dvn * vhat, axis=0, keepdims=True)
        dlnb_ref[...] += jnp.sum(dvn, axis=0, keepdims=True)
        dvh = dvn * lng
        dvv = rstd * (dvh - jnp.mean(dvh, axis=-1, keepdims=True) - vhat * jnp.mean(dvh * vhat, axis=-1, keepdims=True))
        dp_ref[1] = (dvv * _gelu_grad(v)).astype(BF16)

    tok = pl.BlockSpec((tm, D_MODEL), lambda m: (m, 0))
    small = pl.BlockSpec((GROUPS, GMLP_CHUNK, GMLP_CHUNK), lambda m: (0, 0, 0))
    vec = pl.BlockSpec((1, D_MODEL), lambda m: (0, 0))
    return pl.pallas_call(
        body, name="gmlp_bwd",
        out_shape=[jax.ShapeDtypeStruct(dproj.shape, BF16), jax.ShapeDtypeStruct((1, D_MODEL), F32),
                   jax.ShapeDtypeStruct((1, D_MODEL), F32), jax.ShapeDtypeStruct((GROUPS, GMLP_CHUNK, GMLP_CHUNK), F32),
                   jax.ShapeDtypeStruct((GROUPS, GMLP_CHUNK, GMLP_CHUNK), F32)],
        grid=(t // tm,),
        in_specs=[ANY, tok, pl.BlockSpec((None, tm, D_MODEL), lambda m: (U_POS, m, 0)),
                  pl.BlockSpec((None, tm, D_MODEL), lambda m: (U_POS + 1, m, 0)), vec, vec, small, small],
        out_specs=[pl.BlockSpec((2, tm, D_MODEL), lambda m: (U_POS // 2, m, 0)), vec, vec, small, small],
        scratch_shapes=[pltpu.VMEM((tm, D_MODEL), BF16), pltpu.VMEM((tm, D_MODEL), F32)],
        input_output_aliases={0: 0},
        compiler_params=_params([((tm, D_MODEL), F32)] * 3 + [((2, tm, D_MODEL), BF16)] + [((8, 128, 128), F32)] * 4,
                                scratch=[((tm, D_MODEL), BF16), ((tm, D_MODEL), F32)], temps=12 << 20, sem=("arbitrary",)),
    )(dproj, da, proj, proj, ln_g, ln_b, w_s, bias_b)


def _input_backward(dproj, w_in_g, x, dx1, mix_g):
    t = x.shape[0]
    tm = _tile(t, 512)

    def body(dp_ref, w_ref, x_ref, dx1_ref, g_ref, dx_ref, dg_ref):
        @pl.when(pl.program_id(0) == 0)
        def _():
            dg_ref[...] = jnp.zeros_like(dg_ref)

        dh = _dot_nt(dp_ref[0], w_ref[0])
        for p in range(1, N_DEV):
            dh = dh + _dot_nt(dp_ref[p], w_ref[p])
        dx, dg = _rms_bwd(dh, x_ref[...], g_ref[...])
        dx_ref[...] = dx1_ref[...] + dx
        dg_ref[...] += dg

    tok = pl.BlockSpec((tm, D_MODEL), lambda m: (m, 0))
    vec = pl.BlockSpec((1, D_MODEL), lambda m: (0, 0))
    return pl.pallas_call(
        body, name="input_bwd",
        out_shape=[jax.ShapeDtypeStruct((t, D_MODEL), F32), jax.ShapeDtypeStruct((1, D_MODEL), F32)],
        grid=(t // tm,),
        in_specs=[pl.BlockSpec((N_DEV, tm, D_MODEL), lambda m: (0, m, 0)), RESIDENT, tok, tok, vec],
        out_specs=[tok, vec],
        compiler_params=_params([((N_DEV, tm, D_MODEL), BF16)] + [((tm, D_MODEL), F32)] * 3,
                                scratch=[((N_DEV, D_MODEL, D_MODEL), BF16)], temps=6 << 20, sem=("arbitrary",)),
    )(dproj, w_in_g, x, dx1, mix_g)


def _weight_grad(name, a, b, a_spec, b_spec, out_shape, out_spec, steps, blocks, a_is_transposed):
    def body(a_ref, b_ref, o_ref):
        o_ref[...] = _dot(a_ref[...], b_ref[...]) if a_is_transposed else _dot_tn(a_ref[...], b_ref[...])

    return pl.pallas_call(
        body, name=name, out_shape=jax.ShapeDtypeStruct(out_shape, F32), grid=(steps,), in_specs=[a_spec, b_spec],
        out_specs=out_spec, compiler_params=_params(blocks, temps=4 << 20, sem=("arbitrary",)),
    )(a, b)


def _pack_small(mix_g, ln_g, ln_b, b_s, lb_table, hg_norm, ffn_g, final_g, loss_row):
    def part(a):
        a = a.reshape(-1, D_MODEL)
        return jnp.pad(a, ((0, 8 - a.shape[0]), (0, 0)))

    return jnp.concatenate([part(mix_g), part(ln_g), part(ln_b), part(hg_norm), part(ffn_g), part(final_g),
                            part(lb_table), part(b_s), part(loss_row)], axis=0)


SMALL_PARTS = (("gmlp_ln_g", 8, 1), ("gmlp_ln_b", 16, 1), ("hgrn_norm_g", 24, 1), ("norm_ffn_g", 32, 1), ("norm_final_g", 40, 1),
               ("hgrn_lb_table", 48, 2))


def _adamw_small_unpacked(gathered, w, m, v):
    rows = w.shape[0]
    n_out = len(SMALL_PARTS) + 1

    def body(p_ref, w_ref, m_ref, v_ref, *outs):
        g = p_ref[0]
        for j in range(1, N_DEV):
            g = g + p_ref[j]
        delta, m_new, v_new = _adamw_math(w_ref[...], g, m_ref[...], v_ref[...])
        for kind, val in enumerate((g, delta, m_new, v_new)):
            refs = outs[kind * n_out:(kind + 1) * n_out]
            for (_, first, count), ref in zip(SMALL_PARTS, refs):
                ref[...] = val[first:first + count]
            for grp in range(GROUPS):
                refs[-1][0, grp:grp + 1, :] = val[56:57, grp * GMLP_CHUNK:(grp + 1) * GMLP_CHUNK]
        outs[-1][...] = g[SMALL_ROWS - 8:SMALL_ROWS - 7]

    shapes = [jax.ShapeDtypeStruct((count, D_MODEL), F32) for _, _, count in SMALL_PARTS]
    shapes.append(jax.ShapeDtypeStruct((1, GROUPS, GMLP_CHUNK), F32))
    whole = pl.BlockSpec((rows, D_MODEL), lambda: (0, 0))
    res = pl.pallas_call(
        body, name="adamw_small", out_shape=shapes * 4 + [jax.ShapeDtypeStruct((1, D_MODEL), F32)],
        in_specs=[pl.BlockSpec((N_DEV, rows, D_MODEL), lambda: (0, 0, 0)), whole, whole, whole],
        compiler_params=_params([((N_DEV, rows, D_MODEL), F32)] + [((rows, D_MODEL), F32)] * 7),
    )(gathered, w, m, v)
    names = [nme for nme, _, _ in SMALL_PARTS] + ["gmlp_b_s"]
    return [dict(zip(names, res[kind * n_out:(kind + 1) * n_out])) for kind in range(4)], res[-1]


def _adamw_row(name, gathered, w, m, v):
    def body(p_ref, w_ref, m_ref, v_ref, g_out, d_out, m_out, v_out):
        g = p_ref[0, 0:1, :]
        for j in range(1, N_DEV):
            g = g + p_ref[j, 0:1, :]
        delta, m_new, v_new = _adamw_math(w_ref[...], g, m_ref[...], v_ref[...])
        g_out[...] = g
        d_out[...] = delta
        m_out[...] = m_new
        v_out[...] = v_new

    return pl.pallas_call(
        body, name=name, out_shape=[jax.ShapeDtypeStruct((1, D_MODEL), F32)] * 4,
        compiler_params=_params([((N_DEV, 8, D_MODEL), F32)] + [((8, D_MODEL), F32)] * 7),
    )(gathered, w, m, v)


def _adamw_small(name, gathered, w, m, v):
    rows, cols = w.shape

    def body(p_ref, w_ref, m_ref, v_ref, g_out, d_out, m_out, v_out):
        g = p_ref[0]
        for j in range(1, N_DEV):
            g = g + p_ref[j]
        delta, m_new, v_new = _adamw_math(w_ref[...], g, m_ref[...], v_ref[...])
        g_out[...] = g
        d_out[...] = delta
        m_out[...] = m_new
        v_out[...] = v_new

    tr = _tile(rows, 512)
    spec = pl.BlockSpec((tr, cols), lambda r: (r, 0))
    return pl.pallas_call(
        body, name=name, out_shape=[jax.ShapeDtypeStruct((rows, cols), F32)] * 4, grid=(rows // tr,),
        in_specs=[pl.BlockSpec((N_DEV, tr, cols), lambda r: (0, r, 0)), spec, spec, spec], out_specs=[spec] * 4,
        compiler_params=_params([((N_DEV, tr, cols), F32)] + [((tr, cols), F32)] * 7, sem=("arbitrary",)),
    )(gathered, w, m, v)


def kernel(x, norm_mix_g, w_in, gmlp_ln_g, gmlp_ln_b, gmlp_w_s, gmlp_b_s, hgrn_lb_table, hgrn_norm_g, w_branch_a, w_branch_b, w_out, norm_ffn_g, w_gate_up, w_down, norm_final_g, loss_target, m_norm_mix_g, m_w_in, m_gmlp_ln_g, m_gmlp_ln_b, m_gmlp_w_s, m_gmlp_b_s, m_hgrn_lb_table, m_hgrn_norm_g, m_w_branch_a, m_w_branch_b, m_w_out, m_norm_ffn_g, m_w_gate_up, m_w_down, m_norm_final_g, v_norm_mix_g, v_w_in, v_gmlp_ln_g, v_gmlp_ln_b, v_gmlp_w_s, v_gmlp_b_s, v_hgrn_lb_table, v_hgrn_norm_g, v_w_branch_a, v_w_branch_b, v_w_out, v_norm_ffn_g, v_w_gate_up, v_w_down, v_norm_final_g):
    t = x.shape[1]
    x2d = x.reshape(t, D_MODEL)
    target = loss_target.reshape(t, D_MODEL)
    final_g = norm_final_g.reshape(1, D_MODEL)

    shards = [w_in[0].astype(BF16), w_branch_a[0].astype(BF16), w_branch_b[0].astype(BF16), w_out[0].astype(BF16),
              w_gate_up[0].T.astype(BF16), w_down[0].astype(BF16)]

    def rows_of(n):
        return lambda ref, j: ref.at[pl.ds(pl.multiple_of(j * n, 8), n)]

    gathered = [((N_DEV, D_MODEL, D_MODEL), BF16), ((D_MODEL, D_MODEL), BF16), ((D_MODEL, D_MODEL), BF16),
                ((D_MODEL, D_MODEL), BF16), ((N_DEV, FF_BLOCK, D_MODEL), BF16), ((D_FF, D_MODEL), BF16)]
    places = [lambda ref, j: ref.at[_pos_of_dev(j)], rows_of(BRANCH_ROWS), rows_of(BRANCH_ROWS), rows_of(BRANCH_ROWS),
              lambda ref, j: ref.at[j], rows_of(DOWN_ROWS)]
    w_in_g = _all_gather_balanced_async("w_in_all_gather", 9, shards[0], gathered[0], places[0], D_MODEL)
    w_in_sibling = _swap_with_sibling("w_in_from_sibling", shards[0])
    _, later = lax.optimization_barrier((w_in_sibling, shards[1:]))
    w_a, w_b, w_o, w_gu, w_dn = _all_gather_async("weights_all_gather", 0, later, gathered[1:], places[1:])

    core_i, chip_i = lax.axis_index("c"), 2 * lax.axis_index("x") + lax.axis_index("y")
    own_pos = jnp.stack([_pos_of_dev(2 * chip_i + core_i), _pos_of_dev(2 * chip_i + 1 - core_i)]).astype(jnp.int32)
    other_pos = jnp.stack([_pos_of_dev(2 * jnp.bitwise_xor(chip_i, q) + cc) for q in (1, 2, 3) for cc in (0, 1)]).astype(jnp.int32)
    proj, h, h_t = _proj_forward_own_chip(own_pos, x2d, norm_mix_g, shards[0], w_in_sibling)
    proj = _proj_forward_other_chips(other_pos, proj, h, w_in_g)
    bias_b = jnp.broadcast_to(gmlp_b_s[0][:, :, None], (GROUPS, GMLP_CHUNK, GMLP_CHUNK))
    a = _gmlp_forward(proj, gmlp_ln_g, gmlp_ln_b, gmlp_w_s[0], bias_b)
    og, o_saved, states = _hgrn_forward(proj, hgrn_lb_table, hgrn_norm_g)
    ya, yb, merged, x1, h2 = _branch_out_forward(a, og, proj, x2d, w_a, w_b, w_o, norm_ffn_g)
    gu, act, loss_tile, d_final_g, dx2, dx2b = _ffn_forward(h2, x1, w_gu, w_dn, target, final_g)

    core = lax.axis_index("c").astype(jnp.int32).reshape(1)
    chip = (2 * lax.axis_index("x") + lax.axis_index("y")).astype(jnp.int32).reshape(1)
    branch_rows, branch_shape = rows_of(BRANCH_ROWS), (BRANCH_ROWS, D_MODEL)
    branch_block = ((BRANCH_ROWS, D_MODEL), lambda q, r, c: (2 * q + c, 0))

    def chip_partials(names, grads, land, own_blocks):
        return [_chip_partial("chip_partial_" + nme, core, g_, blk, idx, l_)
                for nme, g_, (blk, idx), l_ in zip(names, grads, own_blocks, land)]

    whole = pl.BlockSpec((t, D_MODEL), lambda n: (0, 0))
    whole_t = pl.BlockSpec((D_MODEL, t), lambda n: (0, 0))
    col_blocks = [((t, D_MODEL), BF16), ((t, 256), BF16), ((D_MODEL, 256), F32)]

    def square_grad(name, a_, b_):
        return _weight_grad(name, a_, b_, whole, pl.BlockSpec((t, 256), lambda n: (0, n)), (D_MODEL, D_MODEL),
                            pl.BlockSpec((D_MODEL, 256), lambda n: (0, n)), D_MODEL // 256, col_blocks, False)

    dgu, dx1, dx1b, d_ffn_g = _ffn_backward(dx2b, dx2, gu, x1, w_gu, w_dn, norm_ffn_g)
    g_gu = _weight_grad(
        "grad_w_gate_up", dgu, h2, pl.BlockSpec((None, None, t, FF_BLOCK), lambda j: (j % 4, j // 4, 0, 0)), whole,
        (N_DEV, FF_BLOCK, D_MODEL), pl.BlockSpec((None, FF_BLOCK, D_MODEL), lambda j: (j, 0, 0)), N_DEV,
        [((t, 768), BF16), ((t, D_MODEL), BF16), ((FF_BLOCK, D_MODEL), F32)], False)
    g_dn = _weight_grad(
        "grad_w_down", act, dx2b, pl.BlockSpec((None, t, FF_BLOCK), lambda j: (j, 0, 0)), whole, (D_FF, D_MODEL),
        pl.BlockSpec((FF_BLOCK, D_MODEL), lambda j: (j, 0)), 4,
        [((t, 768), BF16), ((t, D_MODEL), BF16), ((FF_BLOCK, D_MODEL), F32)], False)
    names_f, grads_f = ["w_gate_up", "w_down"], [g_gu, g_dn]
    land_f = _exchange_sibling("ffn_grads_to_sibling", 2, grads_f, [lambda ref, j: ref.at[j], rows_of(DOWN_ROWS)],
                               [(FF_BLOCK, D_MODEL), (DOWN_ROWS, D_MODEL)])

    dx1b_later, _ = lax.optimization_barrier((dx1b, grads_f))
    dya, dyb, dproj, da, dog = _branch_out_backward(dx1b_later, ya, yb, proj, w_a, w_b, w_o)
    g_a = square_grad("grad_w_a", a, dya)
    g_b = square_grad("grad_w_b", og, dyb)
    g_o = square_grad("grad_w_out", merged, dx1b)
    names_b, grads_b = ["w_branch_a", "w_branch_b", "w_out"], [g_a, g_b, g_o]
    land_b = _exchange_sibling("branch_grads_to_sibling", 3, grads_b, [branch_rows] * 3, [branch_shape] * 3)

    part_f = chip_partials(names_f, grads_f, land_f,
                           [((None, DOWN_ROWS // 2, D_MODEL), lambda q, r, c: (2 * q + c, r, 0)),
                            ((DOWN_ROWS // 2, D_MODEL), lambda q, r, c: (2 * (2 * q + c) + r, 0))])
    landed_f = _exchange_chips("ffn_grads_to_chips", 5, part_f)

    dog, _ = lax.optimization_barrier((dog, part_f))
    dproj, d_hg_norm, d_lb = _hgrn_backward(dproj, dog, o_saved, states, proj, hgrn_lb_table, hgrn_norm_g)

    land_b, _ = lax.optimization_barrier((land_b, part_f))
    part_b = chip_partials(names_b, grads_b, land_b, [branch_block] * 3)
    landed_b = _exchange_chips("branch_grads_to_chips", 6, part_b)

    da, _ = lax.optimization_barrier((da, part_b))
    dproj, d_ln_g, d_ln_b, d_ws, d_bs = _gmlp_backward(dproj, da, proj, gmlp_ln_g, gmlp_ln_b, gmlp_w_s[0], bias_b)

    def packed(vals):
        return _pack_small(*vals)

    def flat_ws(a):
        return a.reshape(GROUPS * GMLP_CHUNK, GMLP_CHUNK)

    no_row = jnp.zeros((1, D_MODEL), F32)
    w_pack = packed([norm_mix_g, gmlp_ln_g, gmlp_ln_b, gmlp_b_s, hgrn_lb_table, hgrn_norm_g, norm_ffn_g, norm_final_g, no_row])
    m_pack = packed([m_norm_mix_g, m_gmlp_ln_g, m_gmlp_ln_b, m_gmlp_b_s, m_hgrn_lb_table, m_hgrn_norm_g, m_norm_ffn_g, m_norm_final_g, no_row])
    v_pack = packed([v_norm_mix_g, v_gmlp_ln_g, v_gmlp_ln_b, v_gmlp_b_s, v_hgrn_lb_table, v_hgrn_norm_g, v_norm_ffn_g, v_norm_final_g, no_row])
    small_partial = _pack_small(no_row, d_ln_g, d_ln_b, d_bs[:, :, 0], d_lb, d_hg_norm, d_ffn_g, d_final_g,
                                jnp.tile(loss_tile[0:1], (1, D_MODEL // 128)))
    small_all, ws_all = _all_gather_async(
        "small_grads_all_gather", 1, [small_partial, flat_ws(d_ws)],
        [((N_DEV, SMALL_ROWS, D_MODEL), F32), ((N_DEV, GROUPS * GMLP_CHUNK, GMLP_CHUNK), F32)],
        [lambda ref, j: ref.at[j], lambda ref, j: ref.at[j]])

    g_in = _weight_grad(
        "grad_w_in", h_t, dproj, whole_t, pl.BlockSpec((None, t, D_MODEL), lambda p: (p, 0, 0)), (N_DEV, D_MODEL, D_MODEL),
        pl.BlockSpec((None, D_MODEL, D_MODEL), lambda p: (p, 0, 0)), N_DEV,
        [((D_MODEL, t), BF16), ((t, D_MODEL), BF16), ((D_MODEL, D_MODEL), F32)], True)
    land_i = _exchange_sibling("w_in_grads_to_sibling", 4, [g_in], [lambda ref, j: ref.at[_pos_of_dev(j)]],
                               [(D_MODEL, D_MODEL)])

    big = {}
    for nme, own, lnd, w, m, v in zip(
            names_f + names_b, part_f + part_b, landed_f + landed_b,
            [w_gate_up, w_down, w_branch_a, w_branch_b, w_out], [m_w_gate_up, m_w_down, m_w_branch_a, m_w_branch_b, m_w_out],
            [v_w_gate_up, v_w_down, v_w_branch_a, v_w_branch_b, v_w_out]):
        flip = (lambda z: z.T) if nme == "w_gate_up" else (lambda z: z)
        big[nme] = [flip(o_)[None] for o_ in _adamw("adamw_" + nme, chip, own, lnd, flip(w[0]), flip(m[0]), flip(v[0]))]
    small, loss_row = _adamw_small_unpacked(small_all, w_pack, m_pack, v_pack)
    ws_outs = _adamw_small("adamw_w_s", ws_all, flat_ws(gmlp_w_s), flat_ws(m_gmlp_w_s), flat_ws(v_gmlp_w_s))
    land_i, _ = lax.optimization_barrier((land_i, (big, small, ws_outs)))
    part_i = chip_partials(["w_in"], [g_in], land_i,
                           [((None, 256, D_MODEL), lambda q, r, c: (_pos_of_dev(2 * q + c), r, 0))])
    landed_i = _exchange_chips("w_in_grads_to_chips", 7, part_i)

    dx1, _ = lax.optimization_barrier((dx1, part_i))
    grad_x, d_mix_g = _input_backward(dproj, w_in_g, x2d, dx1, norm_mix_g)
    big["w_in"] = [o_[None] for o_ in _adamw("adamw_w_in", chip, part_i[0], landed_i[0], w_in[0], m_w_in[0], v_w_in[0])]

    def row8(a):
        return jnp.pad(a, ((0, 7), (0, 0)))

    d_mix_g, _ = lax.optimization_barrier((d_mix_g, landed_i))
    (mix_all,) = _all_gather_async("mix_gain_grad_all_gather", 8, [row8(d_mix_g)], [((N_DEV, 8, D_MODEL), F32)],
                                   [lambda ref, j: ref.at[j]])
    mix_outs = _adamw_row("adamw_mix_gain", mix_all, norm_mix_g, m_norm_mix_g, v_norm_mix_g)
    small = [dict(p, norm_final_g=p["norm_final_g"][0], gmlp_w_s=ws.reshape(1, GROUPS, GMLP_CHUNK, GMLP_CHUNK), norm_mix_g=q)
             for p, ws, q in zip(small, ws_outs, mix_outs)]

    loss = loss_row[0, 0]
    order = ["norm_mix_g", "w_in", "gmlp_ln_g", "gmlp_ln_b", "gmlp_w_s", "gmlp_b_s", "hgrn_lb_table", "hgrn_norm_g",
             "w_branch_a", "w_branch_b", "w_out", "norm_ffn_g", "w_gate_up", "w_down", "norm_final_g"]
    outs = [loss, grad_x.reshape(1, t, D_MODEL)]
    for kind in range(4):
        for nme in order:
            outs.append(big[nme][kind] if nme in big else small[kind][nme])
    return tuple(outs)
```

```python
import functools

import jax
import jax.numpy as jnp
from jax import lax
from jax.experimental import pallas as pl
from jax.experimental.pallas import tpu as pltpu
from jax.experimental.pallas import tpu_sc as plsc

F32, BF16 = jnp.float32, jnp.bfloat16
D_MODEL = 1024
N_DEV = 8
HEADS = 8
HEAD_DIM = 128
GROUPS = 8
GMLP_CHUNK = 128
HGRN_CHUNK = 64
HGRN_SCALE = HEAD_DIM ** -0.5
D_FF = 2816
FF_BLOCK = D_FF // 4
DOWN_ROWS = D_FF // N_DEV
BRANCH_ROWS = D_MODEL // N_DEV
NORM_EPS = 1e-6
ADAM_LR, ADAM_B1, ADAM_B2, ADAM_EPS, ADAM_WD, ADAM_STEP = 0.001, 0.9, 0.999, 1e-08, 0.01, 10
SMALL_ROWS = 72
V7X_VMEM_BYTES = 64 * 1024 * 1024
VMEM_CAP = V7X_VMEM_BYTES - 6 * 1024 * 1024
MESH_ID = pl.DeviceIdType.MESH
ANY = pl.BlockSpec(memory_space=pl.ANY)
RESIDENT = pl.BlockSpec(memory_space=pltpu.VMEM)
Q_POS, U_POS, GATE_POS = 0, 4, 6


def _pos_of_dev(j):
    return jnp.where(j < 2, j + 4, jnp.where(j < 6, j - 2, j))


def _dev_of_pos(p):
    return jnp.where(p < 4, p + 2, jnp.where(p < 6, p - 4, p))


def _nbytes(shape, dtype):
    n = 1
    for s in shape:
        n *= s
    return n * jnp.dtype(dtype).itemsize


def _params(blocks, scratch=(), temps=0, sem=None):
    need = 2 * sum(_nbytes(s, d) for s, d in blocks) + sum(_nbytes(s, d) for s, d in scratch) + temps
    assert need + (4 << 20) <= VMEM_CAP, need
    return pltpu.CompilerParams(dimension_semantics=sem, vmem_limit_bytes=VMEM_CAP)


def _tile(n, pref):
    return pref if n % pref == 0 else n


def _dot(a, b):
    return jnp.dot(a, b, preferred_element_type=F32)


def _dot_nt(a, b):
    return lax.dot_general(a, b, (((1,), (1,)), ((), ())), preferred_element_type=F32)


def _dot_tn(a, b):
    return lax.dot_general(a, b, (((0,), (0,)), ((), ())), preferred_element_type=F32)


def _sigmoid(x):
    return 1.0 / (1.0 + jnp.exp(-x))


_GELU_C = 0.7978845608028654


def _gelu(x):
    return x * (0.5 * (1.0 + jnp.tanh(_GELU_C * (x + 0.044715 * (x * x * x)))))


def _gelu_and_grad(x):
    t = jnp.tanh(_GELU_C * (x + 0.044715 * (x * x * x)))
    half = 0.5 * (1.0 + t)
    return x * half, half + 0.5 * x * (1.0 - t * t) * (_GELU_C * (1.0 + 3.0 * 0.044715 * x * x))


def _rms_stats(x):
    r = lax.rsqrt(jnp.mean(x * x, axis=-1, keepdims=True) + NORM_EPS)
    return r, x * r


def _rms_bwd(dy, x, g):
    r, xh = _rms_stats(x)
    dg = jnp.sum(dy * xh, axis=0, keepdims=True)
    dxh = dy * g
    dx = r * (dxh - xh * jnp.mean(dxh * xh, axis=-1, keepdims=True))
    return dx, dg


def _split3(x):
    hi = x.astype(BF16)
    r = x - hi.astype(F32)
    mid = r.astype(BF16)
    lo = (r - mid.astype(F32)).astype(BF16)
    return hi, mid, lo


def _mask_mm(mask_bf16, x):
    hi, mid, lo = _split3(x)
    return _dot(mask_bf16, hi) + _dot(mask_bf16, mid) + _dot(mask_bf16, lo)


def _place():
    return lax.axis_index("x"), lax.axis_index("y"), lax.axis_index("c")


def _gather_copies(src, out, send, recv, loc, slicers):
    n = len(src)
    x, y, c = _place()
    me, sib = (x, y, c), (x, y, 1 - c)
    chips = [(1 - x, y), (x, 1 - y), (1 - x, 1 - y)]

    def dev(p):
        return 4 * p[0] + 2 * p[1] + p[2]

    def rc(i, k, block, to, from_src=False):
        dst = slicers[i](out[i], dev(block))
        return pltpu.make_async_remote_copy(
            src_ref=src[i] if from_src else dst, dst_ref=dst, send_sem=send.at[7 * i + k],
            recv_sem=recv.at[7 * i + k], device_id=to, device_id_type=MESH_ID)

    mine = [pltpu.make_async_copy(src[i], slicers[i](out[i], dev(me)), loc.at[i]) for i in range(n)]
    for cp in mine:
        cp.start()
    first = []
    for i in range(n):
        first.append(rc(i, 0, me, sib, True))
        for j, chip in enumerate(chips):
            first.append(rc(i, 1 + j, me, (*chip, c), True))
    for cp in first:
        cp.start()
    passed = []
    for j, chip in enumerate(chips):
        for i in range(n):
            rc(i, 1 + j, (*chip, c), me).wait_recv()
            cp = rc(i, 4 + j, (*chip, c), sib)
            cp.start()
            passed.append(cp)
    for i in range(n):
        rc(i, 0, sib, me).wait_recv()
        for j, chip in enumerate(chips):
            rc(i, 4 + j, (*chip, 1 - c), me).wait_recv()
    for cp in first + passed:
        cp.wait_send()
    for cp in mine:
        cp.wait()


def _gather_copies_balanced(src, out, send, recv, loc, slicer, rows):
    x, y, c = _place()
    me, sib = (x, y, c), (x, y, 1 - c)
    xn, yn, dg = (1 - x, y), (x, 1 - y), (1 - x, 1 - y)
    half_rows = rows // 2

    def block(p):
        return slicer(out, 4 * p[0] + 2 * p[1] + p[2])

    def half(ref, h):
        return ref.at[pl.ds(h * half_rows, half_rows)]

    def rc(k, dst, to, from_src=False):
        return pltpu.make_async_remote_copy(src_ref=src if from_src else dst, dst_ref=dst, send_sem=send.at[k],
                                            recv_sem=recv.at[k], device_id=to, device_id_type=MESH_ID)

    mine = pltpu.make_async_copy(src, block(me), loc.at[0])
    mine.start()
    sends = [rc(0, block(me), sib, True), rc(1, block(me), (*xn, c), True), rc(2, block(me), (*yn, c), True)]
    for cp in sends:
        cp.start()

    def then(cp):
        cp.start()
        sends.append(cp)

    rc(1, block((*xn, c)), me).wait_recv()
    then(rc(3, half(block((*xn, c)), 0), (*yn, c)))
    then(rc(5, block((*xn, c)), sib))
    rc(2, block((*yn, c)), me).wait_recv()
    then(rc(4, half(block((*yn, c)), 1), (*xn, c)))
    then(rc(6, block((*yn, c)), sib))
    rc(3, half(block((*dg, c)), 0), me).wait_recv()
    then(rc(7, half(block((*dg, c)), 0), sib))
    rc(4, half(block((*dg, c)), 1), me).wait_recv()
    then(rc(8, half(block((*dg, c)), 1), sib))
    rc(0, block(sib), me).wait_recv()
    rc(5, block((*xn, 1 - c)), me).wait_recv()
    rc(6, block((*yn, 1 - c)), me).wait_recv()
    rc(7, half(block((*dg, 1 - c)), 0), me).wait_recv()
    rc(8, half(block((*dg, 1 - c)), 1), me).wait_recv()
    for cp in sends:
        cp.wait_send()
    mine.wait()


def _gather_scratch(n):
    return [pltpu.SemaphoreType.DMA((7 * n,)), pltpu.SemaphoreType.DMA((7 * n,)), pltpu.SemaphoreType.DMA((n,))]


def _handshake(peers):
    barrier = pltpu.get_barrier_semaphore()
    for peer in peers:
        pl.semaphore_signal(barrier, inc=1, device_id=peer, device_id_type=MESH_ID)
    pl.semaphore_wait(barrier, len(peers))


def _all_gather_async(name, collective_id, srcs, out_shapes, slicers):
    n = len(srcs)

    def body(*refs):
        x, y, c = _place()
        _handshake([(1 - x if dx else x, 1 - y if dy else y, 1 - c if dc else c)
                    for dx in (0, 1) for dy in (0, 1) for dc in (0, 1) if dx or dy or dc])
        _gather_copies(refs[:n], refs[n:2 * n], *refs[2 * n:], slicers)

    return _sequencer_call(name, collective_id, body, srcs, [jax.ShapeDtypeStruct(s, d) for s, d in out_shapes],
                           _gather_scratch(n))


def _all_gather_balanced_async(name, collective_id, src, out_shape, slicer, rows):
    def body(src_ref, out_ref, send, recv, loc):
        x, y, c = _place()
        _handshake([(1 - x if dx else x, 1 - y if dy else y, 1 - c if dc else c)
                    for dx in (0, 1) for dy in (0, 1) for dc in (0, 1) if dx or dy or dc])
        _gather_copies_balanced(src_ref, out_ref, send, recv, loc, slicer, rows)

    return _sequencer_call(name, collective_id, body, [src], [jax.ShapeDtypeStruct(*out_shape)],
                           [pltpu.SemaphoreType.DMA((9,)), pltpu.SemaphoreType.DMA((9,)), pltpu.SemaphoreType.DMA((1,))])[0]


def _sequencer_call(name, collective_id, body, operands, out_types, scratch):
    return pl.kernel(
        body, out_type=out_types, mesh=plsc.ScalarSubcoreMesh(axis_name="sequencer", num_cores=1), name=name,
        scratch_types=scratch, compiler_params=pltpu.CompilerParams(collective_id=collective_id),
    )(*operands)


def _exchange_sibling(name, collective_id, grads, shard_fns, shard_shapes):
    n = len(grads)

    def body(*refs):
        g, land = refs[:n], refs[n:2 * n]
        send, recv = refs[2 * n:]
        x, y, c = _place()
        _handshake([(x, y, 1 - c)])
        remote = []
        for i in range(n):
            for q in range(4):
                cp = pltpu.make_async_remote_copy(
                    src_ref=shard_fns[i](g[i], 2 * q + (1 - c)), dst_ref=land[i].at[q], send_sem=send.at[4 * i + q],
                    recv_sem=recv.at[4 * i + q], device_id=(x, y, 1 - c), device_id_type=MESH_ID)
                cp.start()
                remote.append(cp)
        for cp in remote:
            cp.wait()

    return _sequencer_call(name, collective_id, body, grads, [jax.ShapeDtypeStruct((4, *s), F32) for s in shard_shapes],
                           [pltpu.SemaphoreType.DMA((4 * n,)), pltpu.SemaphoreType.DMA((4 * n,))])


def _exchange_chips(name, collective_id, parts):
    n = len(parts)

    def body(*refs):
        part, out = refs[:n], refs[n:2 * n]
        send, recv = refs[2 * n:]
        x, y, c = _place()
        _handshake([(1 - x, y, c), (x, 1 - y, c), (1 - x, 1 - y, c)])
        remote = []
        for i in range(n):
            for s in range(3):
                qx = 1 - x if (s + 1) // 2 else x
                qy = 1 - y if (s + 1) % 2 else y
                cp = pltpu.make_async_remote_copy(
                    src_ref=part[i].at[2 * qx + qy], dst_ref=out[i].at[s], send_sem=send.at[3 * i + s],
                    recv_sem=recv.at[3 * i + s], device_id=(qx, qy, c), device_id_type=MESH_ID)
                cp.start()
                remote.append(cp)
        for cp in remote:
            cp.wait()

    return _sequencer_call(name, collective_id, body, parts,
                           [jax.ShapeDtypeStruct((3, *p.shape[1:]), p.dtype) for p in parts],
                           [pltpu.SemaphoreType.DMA((3 * n,)), pltpu.SemaphoreType.DMA((3 * n,))])


def _chip_partial(name, core, grad, own_block, own_index, land):
    _, rows, cols = land.shape
    tr = own_block[-2]

    def body(core_ref, a_ref, b_ref, o_ref):
        o_ref[...] = (a_ref[...] + b_ref[...]).astype(BF16)

    spec = pl.BlockSpec((None, tr, cols), lambda q, r, c: (q, r, 0))
    return pl.pallas_call(
        body, name=name, out_shape=jax.ShapeDtypeStruct(land.shape, BF16),
        grid_spec=pltpu.PrefetchScalarGridSpec(
            num_scalar_prefetch=1, grid=(4, rows // tr),
            in_specs=[pl.BlockSpec(own_block, lambda q, r, c: own_index(q, r, c[0])), spec], out_specs=spec),
        compiler_params=_params([((tr, cols), F32)] * 2 + [((tr, cols), BF16)], sem=("arbitrary", "arbitrary")),
    )(core, grad, land)


def _adamw_math(w, g, m, v):
    m = ADAM_B1 * m + (1.0 - ADAM_B1) * g
    v = ADAM_B2 * v + (1.0 - ADAM_B2) * (g * g)
    m_hat = m / (1.0 - ADAM_B1 ** ADAM_STEP)
    v_hat = v / (1.0 - ADAM_B2 ** ADAM_STEP)
    delta = -ADAM_LR * (m_hat / (jnp.sqrt(v_hat) + ADAM_EPS) + ADAM_WD * w)
    return delta, m, v


def _adamw(name, chip, own, landed, w, m, v):
    _, rows, cols = own.shape
    tr = _tile(rows, 256) if rows % 256 == 0 else _tile(rows, 176)

    def body(chip_ref, own_ref, l_ref, w_ref, m_ref, v_ref, g_out, d_out, m_out, v_out):
        g = own_ref[...].astype(F32)
        for s in range(3):
            g = g + l_ref[s].astype(F32)
        delta, m_new, v_new = _adamw_math(w_ref[...], g, m_ref[...], v_ref[...])
        g_out[...] = g
        d_out[...] = delta
        m_out[...] = m_new
        v_out[...] = v_new

    spec = pl.BlockSpec((tr, cols), lambda r, c: (r, 0))
    return pl.pallas_call(
        body, name=name, out_shape=[jax.ShapeDtypeStruct((rows, cols), F32)] * 4,
        grid_spec=pltpu.PrefetchScalarGridSpec(
            num_scalar_prefetch=1, grid=(rows // tr,),
            in_specs=[pl.BlockSpec((None, tr, cols), lambda r, c: (c[0], r, 0)),
                      pl.BlockSpec((3, tr, cols), lambda r, c: (0, r, 0)), spec, spec, spec],
            out_specs=[spec] * 4),
        compiler_params=_params([((4, tr, cols), own.dtype)] + [((tr, cols), F32)] * 7, sem=("arbitrary",)),
    )(chip, own, landed, w, m, v)


def _swap_with_sibling(name, x):
    def body(x_ref, o_ref, send, recv):
        px, py, c = _place()
        cp = pltpu.make_async_remote_copy(src_ref=x_ref, dst_ref=o_ref, send_sem=send, recv_sem=recv,
                                          device_id=(px, py, 1 - c), device_id_type=MESH_ID)
        cp.start()
        cp.wait()

    return pl.pallas_call(
        body, name=name, out_shape=jax.ShapeDtypeStruct(x.shape, x.dtype), in_specs=[ANY], out_specs=ANY,
        scratch_shapes=[pltpu.SemaphoreType.DMA, pltpu.SemaphoreType.DMA],
    )(x)


def _proj_forward_own_chip(positions, x, gain, w_own, w_sibling):
    t = x.shape[0]
    tm = _tile(t, 1024)

    def body(pos_ref, x_ref, g_ref, wo_ref, ws_ref, o_ref, h_ref, ht_ref):
        @pl.when(pl.program_id(1) == 0)
        def _():
            _, xh = _rms_stats(x_ref[...])
            h = (xh * g_ref[...]).astype(BF16)
            h_ref[...] = h
            ht_ref[...] = h.T
            o_ref[...] = _dot(h, wo_ref[...])

        @pl.when(pl.program_id(1) == 1)
        def _():
            o_ref[...] = _dot(h_ref[...], ws_ref[...])

    tok = pl.BlockSpec((tm, D_MODEL), lambda m, k, pos: (m, 0))
    return pl.pallas_call(
        body, name="proj_fwd_own_chip",
        out_shape=[jax.ShapeDtypeStruct((N_DEV, t, D_MODEL), F32), jax.ShapeDtypeStruct((t, D_MODEL), BF16),
                   jax.ShapeDtypeStruct((D_MODEL, t), BF16)],
        grid_spec=pltpu.PrefetchScalarGridSpec(
            num_scalar_prefetch=1, grid=(t // tm, 2),
            in_specs=[tok, pl.BlockSpec((1, D_MODEL), lambda m, k, pos: (0, 0)), RESIDENT, RESIDENT],
            out_specs=[pl.BlockSpec((None, tm, D_MODEL), lambda m, k, pos: (pos[k], m, 0)), tok,
                       pl.BlockSpec((D_MODEL, tm), lambda m, k, pos: (0, m))]),
        compiler_params=_params([((tm, D_MODEL), F32)] * 2 + [((tm, D_MODEL), BF16)] * 2,
                                scratch=[((2, D_MODEL, D_MODEL), BF16)], temps=6 << 20, sem=("arbitrary", "arbitrary")),
    )(positions, x, gain, w_own, w_sibling)


def _proj_forward_other_chips(positions, proj, h, w_in_g):
    t = h.shape[0]
    tm = _tile(t, 1024)

    def body(pos_ref, _, h_ref, w_ref, o_ref):
        o_ref[...] = _dot(h_ref[...], w_ref[pos_ref[pl.program_id(1)]])

    return pl.pallas_call(
        body, name="proj_fwd_other_chips", out_shape=jax.ShapeDtypeStruct(proj.shape, F32),
        grid_spec=pltpu.PrefetchScalarGridSpec(
            num_scalar_prefetch=1, grid=(t // tm, N_DEV - 2),
            in_specs=[ANY, pl.BlockSpec((tm, D_MODEL), lambda m, k, pos: (m, 0)), RESIDENT],
            out_specs=pl.BlockSpec((None, tm, D_MODEL), lambda m, k, pos: (pos[k], m, 0))),
        input_output_aliases={1: 0},
        compiler_params=_params([((tm, D_MODEL), F32), ((tm, D_MODEL), BF16)], scratch=[((N_DEV, D_MODEL, D_MODEL), BF16)],
                                temps=6 << 20, sem=("arbitrary", "arbitrary")),
    )(positions, proj, h, w_in_g)


def _masked_ws(ws_ref, g):
    row = lax.broadcasted_iota(jnp.int32, (GMLP_CHUNK, GMLP_CHUNK), 0)
    col = lax.broadcasted_iota(jnp.int32, (GMLP_CHUNK, GMLP_CHUNK), 1)
    return jnp.where(row >= col, ws_ref[g], 0.0).astype(BF16)


def _gmlp_forward(proj, ln_g, ln_b, w_s, bias_b):
    t = proj.shape[1]
    tm = _tile(t, 256)
    chunks = tm // GMLP_CHUNK

    def body(u_ref, v_ref, lng_ref, lnb_ref, ws_ref, bias_ref, a_ref, vn_scr):
        vv = _gelu(v_ref[...])
        mu = jnp.mean(vv, axis=-1, keepdims=True)
        cen = vv - mu
        var = jnp.mean(cen * cen, axis=-1, keepdims=True)
        vn_scr[...] = ((cen * lax.rsqrt(var + NORM_EPS)) * lng_ref[...] + lnb_ref[...]).astype(BF16)
        for g in range(GROUPS):
            wm = _masked_ws(ws_ref, g)
            cols = slice(g * HEAD_DIM, (g + 1) * HEAD_DIM)
            for c in range(chunks):
                rows = slice(c * GMLP_CHUNK, (c + 1) * GMLP_CHUNK)
                mixed = _dot(wm, vn_scr[rows, cols]) + bias_ref[g]
                a_ref[rows, cols] = (_gelu(u_ref[rows, cols]) * mixed).astype(BF16)

    small = pl.BlockSpec((GROUPS, GMLP_CHUNK, GMLP_CHUNK), lambda m: (0, 0, 0))
    vec = pl.BlockSpec((1, D_MODEL), lambda m: (0, 0))
    return pl.pallas_call(
        body, name="gmlp_fwd", out_shape=jax.ShapeDtypeStruct((t, D_MODEL), BF16), grid=(t // tm,),
        in_specs=[pl.BlockSpec((None, tm, D_MODEL), lambda m: (U_POS, m, 0)),
                  pl.BlockSpec((None, tm, D_MODEL), lambda m: (U_POS + 1, m, 0)), vec, vec, small, small],
        out_specs=pl.BlockSpec((tm, D_MODEL), lambda m: (m, 0)),
        scratch_shapes=[pltpu.VMEM((tm, D_MODEL), BF16)],
        compiler_params=_params([((tm, D_MODEL), F32)] * 2 + [((tm, D_MODEL), BF16)] + [((8, 128, 128), F32)] * 2,
                                scratch=[((tm, D_MODEL), BF16)], temps=8 << 20, sem=("arbitrary",)),
    )(proj, proj, ln_g, ln_b, w_s, bias_b)


def _lower_bound(tab_ref):
    t0, t1 = tab_ref[0:1, :], tab_ref[1:2, :]
    mx = jnp.maximum(t0, t1)
    e0, e1 = jnp.exp(t0 - mx), jnp.exp(t1 - mx)
    return e0 / (e0 + e1)


def _tri_masks():
    row = lax.broadcasted_iota(jnp.int32, (HGRN_CHUNK, HGRN_CHUNK), 0)
    col = lax.broadcasted_iota(jnp.int32, (HGRN_CHUNK, HGRN_CHUNK), 1)
    return row >= col, row <= col


def _chunk_rows(c):
    return slice(c * HGRN_CHUNK, (c + 1) * HGRN_CHUNK)


def _per_chunk(x, nc, fn):
    return jnp.concatenate([fn(x[_chunk_rows(c)]) for c in range(nc)], axis=0)


def _chunk_row_bcast(x, nc, i):
    return _per_chunk(x, nc, lambda xc: jnp.broadcast_to(xc[i:i + 1, :], (HGRN_CHUNK, HEAD_DIM)))


def _hgrn_gates(q, fl, lb, nc):
    lower, _ = _tri_masks()
    lower = lower.astype(BF16)
    s = _sigmoid(fl)
    f = lb + (1.0 - lb) * s
    k = 1.0 - f
    hi, mid, lo = _split3(jnp.log(f))
    a = jnp.concatenate([_dot(lower, hi[_chunk_rows(c)]) + _dot(lower, mid[_chunk_rows(c)]) + _dot(lower, lo[_chunk_rows(c)])
                         for c in range(nc)], axis=0)
    a_mid = _chunk_row_bcast(a, nc, HGRN_CHUNK // 2 - 1)
    a_last = _chunk_row_bcast(a, nc, HGRN_CHUNK - 1)
    qs = q * HGRN_SCALE
    e_in, e_out, e_end, e_all = jnp.exp(a - a_mid), jnp.exp(a_mid - a), jnp.exp(a_last - a), jnp.exp(a)
    decay = [jnp.exp(a[c * HGRN_CHUNK + HGRN_CHUNK - 1:(c + 1) * HGRN_CHUNK, :]) for c in range(nc)]
    return dict(s=s, f=f, k=k, decay=decay, e_in=e_in, e_out=e_out, e_end=e_end, e_all=e_all,
                qi=qs * e_in, ki=k * e_out, kd=k * e_end, qe=qs * e_all)


def _hgrn_forward(proj, lb_table, norm_g):
    t = proj.shape[1]
    tb = _tile(t, 1024)
    nc = tb // HGRN_CHUNK
    n_chunks = t // HGRN_CHUNK

    def body(q_ref, f_ref, i_ref, g_ref, tab_ref, ng_ref, og_ref, o_ref, st_ref, state):
        @pl.when(pl.program_id(1) == 0)
        def _():
            state[...] = jnp.zeros_like(state)

        lower, _ = _tri_masks()
        gt = _hgrn_gates(q_ref[...], f_ref[...], _lower_bound(tab_ref), nc)
        qi, ki, kd, qe = (gt[n].astype(BF16) for n in ("qi", "ki", "kd", "qe"))
        vb = i_ref[...].astype(BF16)
        o_intra, d_state = [], []
        for c in range(nc):
            rows = _chunk_rows(c)
            p = jnp.where(lower, _dot_nt(qi[rows], ki[rows]), 0.0).astype(BF16)
            o_intra.append(_dot(p, vb[rows]))
            d_state.append(_dot_tn(vb[rows], kd[rows]))
        st = state[...]
        outs = []
        for c in range(nc):
            st_ref[c] = st
            outs.append(o_intra[c] + _dot_nt(qe[_chunk_rows(c)], st.astype(BF16)))
            st = st * gt["decay"][c] + d_state[c]
        state[...] = st
        o = jnp.concatenate(outs, axis=0)
        o_ref[...] = o
        _, oh = _rms_stats(o)
        gz = g_ref[...]
        og_ref[...] = ((oh * ng_ref[...]) * (gz * _sigmoid(gz))).astype(BF16)

    def blk(p):
        return pl.BlockSpec((None, tb, HEAD_DIM), lambda h, n: (p, n, h))

    out_blk = pl.BlockSpec((tb, HEAD_DIM), lambda h, n: (n, h))
    return pl.pallas_call(
        body, name="hgrn_fwd",
        out_shape=[jax.ShapeDtypeStruct((t, D_MODEL), BF16), jax.ShapeDtypeStruct((t, D_MODEL), F32),
                   jax.ShapeDtypeStruct((HEADS, n_chunks, HEAD_DIM, HEAD_DIM), F32)],
        grid=(HEADS, t // tb),
        in_specs=[blk(Q_POS), blk(Q_POS + 1), blk(Q_POS + 2), blk(Q_POS + 3),
                  pl.BlockSpec((2, HEAD_DIM), lambda h, n: (0, h)), pl.BlockSpec((1, HEAD_DIM), lambda h, n: (0, h))],
        out_specs=[out_blk, out_blk, pl.BlockSpec((None, nc, HEAD_DIM, HEAD_DIM), lambda h, n: (h, n, 0, 0))],
        scratch_shapes=[pltpu.VMEM((HEAD_DIM, HEAD_DIM), F32)],
        compiler_params=_params([((tb, HEAD_DIM), F32)] * 6 + [((nc, HEAD_DIM, HEAD_DIM), F32)], temps=8 << 20,
                                sem=("arbitrary", "arbitrary")),
    )(proj, proj, proj, proj, lb_table, norm_g)


def _branch_out_forward(a, og, proj, x, w_a, w_b, w_out, ffn_g):
    t = x.shape[0]
    tm = _tile(t, 512)

    def body(a_ref, og_ref, ga_ref, gb_ref, x_ref, wa_ref, wb_ref, wo_ref, g_ref, ya_ref, yb_ref, mg_ref, x1_ref, h2_ref):
        ya = _dot(a_ref[...], wa_ref[...])
        yb = _dot(og_ref[...], wb_ref[...])
        ya_ref[...] = ya
        yb_ref[...] = yb
        merged = (_sigmoid(ga_ref[...]) * ya + _sigmoid(gb_ref[...]) * yb).astype(BF16)
        mg_ref[...] = merged
        x1 = x_ref[...] + _dot(merged, wo_ref[...])
        x1_ref[...] = x1
        _, xh = _rms_stats(x1)
        h2_ref[...] = (xh * g_ref[...]).astype(BF16)

    tok = pl.BlockSpec((tm, D_MODEL), lambda m: (m, 0))
    return pl.pallas_call(
        body, name="branch_out_fwd",
        out_shape=[jax.ShapeDtypeStruct((t, D_MODEL), F32), jax.ShapeDtypeStruct((t, D_MODEL), F32),
                   jax.ShapeDtypeStruct((t, D_MODEL), BF16), jax.ShapeDtypeStruct((t, D_MODEL), F32),
                   jax.ShapeDtypeStruct((t, D_MODEL), BF16)],
        grid=(t // tm,),
        in_specs=[tok, tok, pl.BlockSpec((None, tm, D_MODEL), lambda m: (GATE_POS, m, 0)),
                  pl.BlockSpec((None, tm, D_MODEL), lambda m: (GATE_POS + 1, m, 0)), tok, RESIDENT, RESIDENT, RESIDENT,
                  pl.BlockSpec((1, D_MODEL), lambda m: (0, 0))],
        out_specs=[tok] * 5,
        compiler_params=_params([((tm, D_MODEL), BF16)] * 4 + [((tm, D_MODEL), F32)] * 6, scratch=[((D_MODEL, D_MODEL), BF16)] * 3,
                                temps=8 << 20, sem=("arbitrary",)),
    )(a, og, proj, proj, x, w_a, w_b, w_out, ffn_g)


def _ffn_forward(h2, x1, w_gu, w_down, target, final_g):
    t = x1.shape[0]
    tm = _tile(t, 512)

    def body(h_ref, wgu_ref, wd_ref, x1_ref, t_ref, g_ref, gu_ref, act_ref, loss_ref, dg_ref, dx_ref, dxb_ref, acc):
        m, j = pl.program_id(0), pl.program_id(1)

        @pl.when((m == 0) & (j == 0))
        def _():
            loss_ref[...] = jnp.zeros_like(loss_ref)
            dg_ref[...] = jnp.zeros_like(dg_ref)

        h = h_ref[...]
        gate = _dot_nt(h, wgu_ref[j])
        up = _dot_nt(h, wgu_ref[j + 4])
        gu_ref[0] = gate
        gu_ref[1] = up
        act = ((gate * _sigmoid(gate)) * up).astype(BF16)
        act_ref[...] = act
        part = _dot(act, wd_ref[j])

        @pl.when(j == 0)
        def _():
            acc[...] = part

        @pl.when((j > 0) & (j < 3))
        def _():
            acc[...] += part

        @pl.when(j == 3)
        def _():
            x2 = x1_ref[...] + (acc[...] + part)
            g = g_ref[...]
            r, xh = _rms_stats(x2)
            err = xh * g - t_ref[...]
            loss_ref[...] += 0.5 * jnp.sum(jnp.mean(err * err, axis=-1, keepdims=True), axis=0, keepdims=True)
            dy = err * (1.0 / D_MODEL)
            dg_ref[...] += jnp.sum(dy * xh, axis=0, keepdims=True)
            dxh = dy * g
            dx = r * (dxh - xh * jnp.mean(dxh * xh, axis=-1, keepdims=True))
            dx_ref[...] = dx
            dxb_ref[...] = dx.astype(BF16)

    tok = pl.BlockSpec((tm, D_MODEL), lambda m, j: (m, 0))
    vec = pl.BlockSpec((1, D_MODEL), lambda m, j: (0, 0))
    return pl.pallas_call(
        body, name="ffn_fwd",
        out_shape=[jax.ShapeDtypeStruct((4, 2, t, FF_BLOCK), F32), jax.ShapeDtypeStruct((4, t, FF_BLOCK), BF16),
                   jax.ShapeDtypeStruct((8, 128), F32), jax.ShapeDtypeStruct((1, D_MODEL), F32),
                   jax.ShapeDtypeStruct((t, D_MODEL), F32), jax.ShapeDtypeStruct((t, D_MODEL), BF16)],
        grid=(t // tm, 4),
        in_specs=[tok, RESIDENT, RESIDENT, tok, tok, vec],
        out_specs=[pl.BlockSpec((None, 2, tm, FF_BLOCK), lambda m, j: (j, 0, m, 0)),
                   pl.BlockSpec((None, tm, FF_BLOCK), lambda m, j: (j, m, 0)),
                   pl.BlockSpec((8, 128), lambda m, j: (0, 0)), vec, tok, tok],
        scratch_shapes=[pltpu.VMEM((tm, D_MODEL), F32)],
        compiler_params=_params([((tm, D_MODEL), BF16), ((tm, D_MODEL), F32), ((tm, D_MODEL), F32), ((2, tm, 768), F32),
                                 ((tm, 768), BF16), ((tm, D_MODEL), F32), ((tm, D_MODEL), BF16)],
                                scratch=[((tm, D_MODEL), F32), ((N_DEV, FF_BLOCK, D_MODEL), BF16), ((D_FF, D_MODEL), BF16)],
                                temps=6 << 20, sem=("arbitrary", "arbitrary")),
    )(h2, w_gu, w_down.reshape(4, FF_BLOCK, D_MODEL), x1, target, final_g)


def _ffn_backward(dx2b, dx2, gu, x1, w_gu, w_down, ffn_g):
    t = x1.shape[0]
    tm = _tile(t, 512)

    def body(dxb_ref, dx2_ref, gu_ref, x1_ref, wgu_ref, wd_ref, g_ref, dgu_ref, dx1_ref, dx1b_ref, dg_ref, acc, prev):
        m, j = pl.program_id(0), pl.program_id(1)

        @pl.when((m == 0) & (j == 0))
        def _():
            dg_ref[...] = jnp.zeros_like(dg_ref)

        @pl.when(j == 0)
        def _():
            prev[...] = jnp.zeros_like(prev)
            acc[...] = jnp.zeros_like(acc)

        jm1 = jnp.maximum(j - 1, 0)
        acc[...] += _dot(prev[0], wgu_ref[jm1]) + _dot(prev[1], wgu_ref[jm1 + 4])
        dact = _dot_nt(dxb_ref[...], wd_ref[j])
        gate, up = gu_ref[0], gu_ref[1]
        sg = _sigmoid(gate)
        dgate = (dact * up * (sg * (1.0 + gate * (1.0 - sg)))).astype(BF16)
        dup = (dact * (gate * sg)).astype(BF16)
        dgu_ref[0] = dgate
        dgu_ref[1] = dup
        prev[0] = dgate
        prev[1] = dup

        @pl.when(j == 3)
        def _():
            dh2 = acc[...] + (_dot(prev[0], wgu_ref[3]) + _dot(prev[1], wgu_ref[7]))
            dx, dg = _rms_bwd(dh2, x1_ref[...], g_ref[...])
            dx1 = dx2_ref[...] + dx
            dx1_ref[...] = dx1
            dx1b_ref[...] = dx1.astype(BF16)
            dg_ref[...] += dg

    tok = pl.BlockSpec((tm, D_MODEL), lambda m, j: (m, 0))
    vec = pl.BlockSpec((1, D_MODEL), lambda m, j: (0, 0))
    gu_spec = pl.BlockSpec((None, 2, tm, FF_BLOCK), lambda m, j: (j, 0, m, 0))
    return pl.pallas_call(
        body, name="ffn_bwd",
        out_shape=[jax.ShapeDtypeStruct((4, 2, t, FF_BLOCK), BF16), jax.ShapeDtypeStruct((t, D_MODEL), F32),
                   jax.ShapeDtypeStruct((t, D_MODEL), BF16), jax.ShapeDtypeStruct((1, D_MODEL), F32)],
        grid=(t // tm, 4),
        in_specs=[tok, tok, gu_spec, tok, RESIDENT, RESIDENT, vec],
        out_specs=[gu_spec, tok, tok, vec],
        scratch_shapes=[pltpu.VMEM((tm, D_MODEL), F32), pltpu.VMEM((2, tm, FF_BLOCK), BF16)],
        compiler_params=_params([((tm, D_MODEL), BF16), ((tm, D_MODEL), F32), ((2, tm, 768), F32), ((tm, D_MODEL), F32),
                                 ((2, tm, 768), BF16), ((tm, D_MODEL), F32), ((tm, D_MODEL), BF16)],
                                scratch=[((tm, D_MODEL), F32), ((2, tm, 768), BF16), ((N_DEV, FF_BLOCK, D_MODEL), BF16),
                                         ((D_FF, D_MODEL), BF16)],
                                temps=4 << 20, sem=("arbitrary", "arbitrary")),
    )(dx2b, dx2, gu, x1, w_gu, w_down.reshape(4, FF_BLOCK, D_MODEL), ffn_g)


def _branch_out_backward(dx1b, ya, yb, proj, w_a, w_b, w_out):
    t = ya.shape[0]
    tm = _tile(t, 512)

    def body(dx_ref, ya_ref, yb_ref, ga_ref, gb_ref, wa_ref, wb_ref, wo_ref, dya_ref, dyb_ref, dgate_ref, da_ref, dog_ref):
        dm = _dot_nt(dx_ref[...], wo_ref[...])
        sa, sb = _sigmoid(ga_ref[...]), _sigmoid(gb_ref[...])
        dya = (dm * sa).astype(BF16)
        dyb = (dm * sb).astype(BF16)
        dya_ref[...] = dya
        dyb_ref[...] = dyb
        dgate_ref[0] = (dm * ya_ref[...] * (sa * (1.0 - sa))).astype(BF16)
        dgate_ref[1] = (dm * yb_ref[...] * (sb * (1.0 - sb))).astype(BF16)
        da_ref[...] = _dot_nt(dya, wa_ref[...])
        dog_ref[...] = _dot_nt(dyb, wb_ref[...])

    tok = pl.BlockSpec((tm, D_MODEL), lambda m: (m, 0))
    return pl.pallas_call(
        body, name="branch_out_bwd",
        out_shape=[jax.ShapeDtypeStruct((t, D_MODEL), BF16), jax.ShapeDtypeStruct((t, D_MODEL), BF16),
                   jax.ShapeDtypeStruct((N_DEV, t, D_MODEL), BF16), jax.ShapeDtypeStruct((t, D_MODEL), F32),
                   jax.ShapeDtypeStruct((t, D_MODEL), F32)],
        grid=(t // tm,),
        in_specs=[tok, tok, tok, pl.BlockSpec((None, tm, D_MODEL), lambda m: (GATE_POS, m, 0)),
                  pl.BlockSpec((None, tm, D_MODEL), lambda m: (GATE_POS + 1, m, 0)), RESIDENT, RESIDENT, RESIDENT],
        out_specs=[tok, tok, pl.BlockSpec((2, tm, D_MODEL), lambda m: (GATE_POS // 2, m, 0)), tok, tok],
        compiler_params=_params([((tm, D_MODEL), BF16)] * 5 + [((tm, D_MODEL), F32)] * 6, scratch=[((D_MODEL, D_MODEL), BF16)] * 3,
                                temps=8 << 20, sem=("arbitrary",)),
    )(dx1b, ya, yb, proj, proj, w_a, w_b, w_out)


def _hgrn_backward(dproj, dog, o_saved, states, proj, lb_table, norm_g):
    t = proj.shape[1]
    tb = _tile(t, 1024)
    nc = tb // HGRN_CHUNK
    nb = t // tb

    def body(_, dog_ref, o_ref, st_ref, q_ref, f_ref, i_ref, g_ref, tab_ref, ng_ref, dp_ref, dng_ref, dtab_ref, gstate):
        @pl.when(pl.program_id(1) == 0)
        def _():
            gstate[...] = jnp.zeros_like(gstate)
            dng_ref[...] = jnp.zeros_like(dng_ref)
            dtab_ref[...] = jnp.zeros_like(dtab_ref)

        lb = _lower_bound(tab_ref)
        ng = ng_ref[...]
        lower, upper = _tri_masks()
        gt = _hgrn_gates(q_ref[...], f_ref[...], lb, nc)
        qi, ki, kd, qe = (gt[n].astype(BF16) for n in ("qi", "ki", "kd", "qe"))
        vb = i_ref[...].astype(BF16)
        o, gz, d_og = o_ref[...], g_ref[...], dog_ref[...]
        r, oh = _rms_stats(o)
        sg = _sigmoid(gz)
        d_on = d_og * (gz * sg)
        dgz = d_og * (oh * ng) * (sg * (1.0 + gz * (1.0 - sg)))
        dng_ref[...] += jnp.sum(d_on * oh, axis=0, keepdims=True)
        doh = d_on * ng
        dob = (r * (doh - oh * jnp.mean(doh * oh, axis=-1, keepdims=True))).astype(BF16)
        dv_intra, dqi, dki, dqe, g_upd = [], [], [], [], []
        for c in range(nc):
            rows = _chunk_rows(c)
            p = jnp.where(lower, _dot_nt(qi[rows], ki[rows]), 0.0).astype(BF16)
            dv_intra.append(_dot_tn(p, dob[rows]))
            dp = jnp.where(lower, _dot_nt(dob[rows], vb[rows]), 0.0).astype(BF16)
            dqi.append(_dot(dp, ki[rows]))
            dki.append(_dot_tn(dp, qi[rows]))
            dqe.append(_dot(dob[rows], st_ref[c].astype(BF16)))
            g_upd.append(_dot_tn(dob[rows], qe[rows]))
        g_after = [None] * nc
        g = gstate[...]
        for c in reversed(range(nc)):
            g_after[c] = g
            g = g * gt["decay"][c] + g_upd[c]
        gstate[...] = g
        dkd, dv, da_last = [], [], []
        for c in range(nc):
            rows = _chunk_rows(c)
            gb = g_after[c].astype(BF16)
            dkd.append(_dot(vb[rows], gb))
            dv.append(dv_intra[c] + _dot_nt(kd[rows], gb))
            da_last.append(jnp.sum(g_after[c] * st_ref[c], axis=0, keepdims=True) * gt["decay"][c])
        dqi, dki, dqe, dkd, dv = (jnp.concatenate(z, axis=0) for z in (dqi, dki, dqe, dkd, dv))
        dqs = dqi * gt["e_in"] + dqe * gt["e_all"]
        dk = dki * gt["e_out"] + dkd * gt["e_end"]
        t_in, t_out, t_end = dqi * gt["qi"], dki * gt["ki"], dkd * gt["kd"]
        da = t_in - t_out + dqe * gt["qe"] - t_end
        row = lax.broadcasted_iota(jnp.int32, (HGRN_CHUNK, HEAD_DIM), 0)
        d_mid = t_out - t_in
        pieces = []
        for c in range(nc):
            rows = _chunk_rows(c)
            da_mid = jnp.sum(d_mid[rows], axis=0, keepdims=True)
            da_end = jnp.sum(t_end[rows], axis=0, keepdims=True) + da_last[c]
            da_c = da[rows] + jnp.where(row == HGRN_CHUNK // 2 - 1, da_mid, 0.0) + jnp.where(row == HGRN_CHUNK - 1, da_end, 0.0)
            pieces.append(_mask_mm(upper.astype(BF16), da_c))
        df = jnp.concatenate(pieces, axis=0) / gt["f"] - dk
        s = gt["s"]
        dlb = jnp.sum(df * (1.0 - s), axis=0, keepdims=True)
        dp_ref[0] = (dqs * HGRN_SCALE).astype(BF16)
        dp_ref[1] = (df * (1.0 - lb) * (s * (1.0 - s))).astype(BF16)
        dp_ref[2] = dv.astype(BF16)
        dp_ref[3] = dgz.astype(BF16)
        dt0 = dlb * (lb * (1.0 - lb))
        dtab_ref[0:1, :] += dt0
        dtab_ref[1:2, :] -= dt0

    def blk(p):
        return pl.BlockSpec((None, tb, HEAD_DIM), lambda h, n: (p, nb - 1 - n, h))

    tok = pl.BlockSpec((tb, HEAD_DIM), lambda h, n: (nb - 1 - n, h))
    return pl.pallas_call(
        body, name="hgrn_bwd",
        out_shape=[jax.ShapeDtypeStruct((N_DEV, t, D_MODEL), BF16), jax.ShapeDtypeStruct((1, D_MODEL), F32),
                   jax.ShapeDtypeStruct((2, D_MODEL), F32)],
        grid=(HEADS, nb),
        in_specs=[ANY, tok, tok, pl.BlockSpec((None, nc, HEAD_DIM, HEAD_DIM), lambda h, n: (h, nb - 1 - n, 0, 0)),
                  blk(Q_POS), blk(Q_POS + 1), blk(Q_POS + 2), blk(Q_POS + 3),
                  pl.BlockSpec((2, HEAD_DIM), lambda h, n: (0, h)), pl.BlockSpec((1, HEAD_DIM), lambda h, n: (0, h))],
        out_specs=[pl.BlockSpec((4, tb, HEAD_DIM), lambda h, n: (0, nb - 1 - n, h)),
                   pl.BlockSpec((1, HEAD_DIM), lambda h, n: (0, h)), pl.BlockSpec((2, HEAD_DIM), lambda h, n: (0, h))],
        scratch_shapes=[pltpu.VMEM((HEAD_DIM, HEAD_DIM), F32)],
        input_output_aliases={0: 0},
        compiler_params=_params([((tb, HEAD_DIM), F32)] * 6 + [((nc, HEAD_DIM, HEAD_DIM), F32)] + [((4, tb, HEAD_DIM), BF16)],
                                temps=8 << 20, sem=("arbitrary", "arbitrary")),
    )(dproj, dog, o_saved, states, proj, proj, proj, proj, lb_table, norm_g)


def _gmlp_backward(dproj, da, proj, ln_g, ln_b, w_s, bias_b):
    t = proj.shape[1]
    tm = _tile(t, 256)
    chunks = tm // GMLP_CHUNK

    def body(_, da_ref, u_ref, v_ref, lng_ref, lnb_ref, ws_ref, bias_ref, dp_ref, dlng_ref, dlnb_ref, dws_ref, dbs_ref,
             vn_scr, dvn_scr):
        @pl.when(pl.program_id(0) == 0)
        def _():
            dlng_ref[...] = jnp.zeros_like(dlng_ref)
            dlnb_ref[...] = jnp.zeros_like(dlnb_ref)
            dws_ref[...] = jnp.zeros_like(dws_ref)
            dbs_ref[...] = jnp.zeros_like(dbs_ref)

        v = v_ref[...]
        vv, dvv_dv = _gelu_and_grad(v)
        mu = jnp.mean(vv, axis=-1, keepdims=True)
        cen = vv - mu
        rstd = lax.rsqrt(jnp.mean(cen * cen, axis=-1, keepdims=True) + NORM_EPS)
        vhat = cen * rstd
        lng = lng_ref[...]
        vn_scr[...] = (vhat * lng + lnb_ref[...]).astype(BF16)
        row = lax.broadcasted_iota(jnp.int32, (GMLP_CHUNK, GMLP_CHUNK), 0)
        col = lax.broadcasted_iota(jnp.int32, (GMLP_CHUNK, GMLP_CHUNK), 1)
        for g in range(GROUPS):
            wm = _masked_ws(ws_ref, g)
            cols = slice(g * HEAD_DIM, (g + 1) * HEAD_DIM)
            dws = jnp.zeros((GMLP_CHUNK, GMLP_CHUNK), F32)
            dbs = jnp.zeros((GMLP_CHUNK, GMLP_CHUNK), F32)
            for c in range(chunks):
                rows = slice(c * GMLP_CHUNK, (c + 1) * GMLP_CHUNK)
                vn = vn_scr[rows, cols]
                mixed = _dot(wm, vn) + bias_ref[g]
                u = u_ref[rows, cols]
                d_a = da_ref[rows, cols]
                gelu_u, dgelu_u = _gelu_and_grad(u)
                dp_ref[0, rows, cols] = (d_a * mixed * dgelu_u).astype(BF16)
                dmix = d_a * gelu_u
                dmb = dmix.astype(BF16)
                dbs = dbs + dmix
                dws = dws + _dot_nt(dmb, vn)
                dvn_scr[rows, cols] = _dot_tn(wm, dmb)
            dws_ref[g] += jnp.where(row >= col, dws, 0.0)
            dbs_ref[g] += jnp.broadcast_to(jnp.sum(dbs, axis=-1, keepdims=True), (GMLP_CHUNK, GMLP_CHUNK))
        dvn = dvn_scr[...]
        dlng_ref[...] += jnp.sum(dvn * vhat, axis=0, keepdims=True)
        dlnb_ref[...] += jnp.sum(dvn, axis=0, keepdims=True)
        dvh = dvn * lng
        dvv = rstd * (dvh - jnp.mean(dvh, axis=-1, keepdims=True) - vhat * jnp.mean(dvh * vhat, axis=-1, keepdims=True))
        dp_ref[1] = (dvv * dvv_dv).astype(BF16)

    tok = pl.BlockSpec((tm, D_MODEL), lambda m: (m, 0))
    small = pl.BlockSpec((GROUPS, GMLP_CHUNK, GMLP_CHUNK), lambda m: (0, 0, 0))
    vec = pl.BlockSpec((1, D_MODEL), lambda m: (0, 0))
    return pl.pallas_call(
        body, name="gmlp_bwd",
        out_shape=[jax.ShapeDtypeStruct(dproj.shape, BF16), jax.ShapeDtypeStruct((1, D_MODEL), F32),
                   jax.ShapeDtypeStruct((1, D_MODEL), F32), jax.ShapeDtypeStruct((GROUPS, GMLP_CHUNK, GMLP_CHUNK), F32),
                   jax.ShapeDtypeStruct((GROUPS, GMLP_CHUNK, GMLP_CHUNK), F32)],
        grid=(t // tm,),
        in_specs=[ANY, tok, pl.BlockSpec((None, tm, D_MODEL), lambda m: (U_POS, m, 0)),
                  pl.BlockSpec((None, tm, D_MODEL), lambda m: (U_POS + 1, m, 0)), vec, vec, small, small],
        out_specs=[pl.BlockSpec((2, tm, D_MODEL), lambda m: (U_POS // 2, m, 0)), vec, vec, small, small],
        scratch_shapes=[pltpu.VMEM((tm, D_MODEL), BF16), pltpu.VMEM((tm, D_MODEL), F32)],
        input_output_aliases={0: 0},
        compiler_params=_params([((tm, D_MODEL), F32)] * 3 + [((2, tm, D_MODEL), BF16)] + [((8, 128, 128), F32)] * 4,
                                scratch=[((tm, D_MODEL), BF16), ((tm, D_MODEL), F32)], temps=12 << 20, sem=("arbitrary",)),
    )(dproj, da, proj, proj, ln_g, ln_b, w_s, bias_b)


def _input_backward(dproj, w_in_g, x, dx1, mix_g):
    t = x.shape[0]
    tm = _tile(t, 512)

    def body(dp_ref, w_ref, x_ref, dx1_ref, g_ref, dx_ref, dg_ref):
        @pl.when(pl.program_id(0) == 0)
        def _():
            dg_ref[...] = jnp.zeros_like(dg_ref)

        dh = _dot_nt(dp_ref[0], w_ref[0])
        for p in range(1, N_DEV):
            dh = dh + _dot_nt(dp_ref[p], w_ref[p])
        dx, dg = _rms_bwd(dh, x_ref[...], g_ref[...])
        dx_ref[...] = dx1_ref[...] + dx
        dg_ref[...] += dg

    tok = pl.BlockSpec((tm, D_MODEL), lambda m: (m, 0))
    vec = pl.BlockSpec((1, D_MODEL), lambda m: (0, 0))
    return pl.pallas_call(
        body, name="input_bwd",
        out_shape=[jax.ShapeDtypeStruct((t, D_MODEL), F32), jax.ShapeDtypeStruct((1, D_MODEL), F32)],
        grid=(t // tm,),
        in_specs=[pl.BlockSpec((N_DEV, tm, D_MODEL), lambda m: (0, m, 0)), RESIDENT, tok, tok, vec],
        out_specs=[tok, vec],
        compiler_params=_params([((N_DEV, tm, D_MODEL), BF16)] + [((tm, D_MODEL), F32)] * 3,
                                scratch=[((N_DEV, D_MODEL, D_MODEL), BF16)], temps=6 << 20, sem=("arbitrary",)),
    )(dproj, w_in_g, x, dx1, mix_g)


def _weight_grad(name, a, b, a_spec, b_spec, out_shape, out_spec, steps, blocks, a_is_transposed):
    def body(a_ref, b_ref, o_ref):
        o_ref[...] = _dot(a_ref[...], b_ref[...]) if a_is_transposed else _dot_tn(a_ref[...], b_ref[...])

    return pl.pallas_call(
        body, name=name, out_shape=jax.ShapeDtypeStruct(out_shape, F32), grid=(steps,), in_specs=[a_spec, b_spec],
        out_specs=out_spec, compiler_params=_params(blocks, temps=4 << 20, sem=("arbitrary",)),
    )(a, b)


def _pack_small(mix_g, ln_g, ln_b, b_s, lb_table, hg_norm, ffn_g, final_g, loss_row):
    def part(a):
        a = a.reshape(-1, D_MODEL)
        return jnp.pad(a, ((0, 8 - a.shape[0]), (0, 0)))

    return jnp.concatenate([part(mix_g), part(ln_g), part(ln_b), part(hg_norm), part(ffn_g), part(final_g),
                            part(lb_table), part(b_s), part(loss_row)], axis=0)


SMALL_PARTS = (("gmlp_ln_g", 8, 1), ("gmlp_ln_b", 16, 1), ("hgrn_norm_g", 24, 1), ("norm_ffn_g", 32, 1), ("norm_final_g", 40, 1),
               ("hgrn_lb_table", 48, 2))


def _adamw_small_unpacked(gathered, w, m, v):
    rows = w.shape[0]
    n_out = len(SMALL_PARTS) + 1

    def body(p_ref, w_ref, m_ref, v_ref, *outs):
        g = p_ref[0]
        for j in range(1, N_DEV):
            g = g + p_ref[j]
        delta, m_new, v_new = _adamw_math(w_ref[...], g, m_ref[...], v_ref[...])
        for kind, val in enumerate((g, delta, m_new, v_new)):
            refs = outs[kind * n_out:(kind + 1) * n_out]
            for (_, first, count), ref in zip(SMALL_PARTS, refs):
                ref[...] = val[first:first + count]
            for grp in range(GROUPS):
                refs[-1][0, grp:grp + 1, :] = val[56:57, grp * GMLP_CHUNK:(grp + 1) * GMLP_CHUNK]
        outs[-1][...] = g[SMALL_ROWS - 8:SMALL_ROWS - 7]

    shapes = [jax.ShapeDtypeStruct((count, D_MODEL), F32) for _, _, count in SMALL_PARTS]
    shapes.append(jax.ShapeDtypeStruct((1, GROUPS, GMLP_CHUNK), F32))
    whole = pl.BlockSpec((rows, D_MODEL), lambda: (0, 0))
    res = pl.pallas_call(
        body, name="adamw_small", out_shape=shapes * 4 + [jax.ShapeDtypeStruct((1, D_MODEL), F32)],
        in_specs=[pl.BlockSpec((N_DEV, rows, D_MODEL), lambda: (0, 0, 0)), whole, whole, whole],
        compiler_params=_params([((N_DEV, rows, D_MODEL), F32)] + [((rows, D_MODEL), F32)] * 7),
    )(gathered, w, m, v)
    names = [nme for nme, _, _ in SMALL_PARTS] + ["gmlp_b_s"]
    return [dict(zip(names, res[kind * n_out:(kind + 1) * n_out])) for kind in range(4)], res[-1]


def _adamw_row(name, gathered, w, m, v):
    def body(p_ref, w_ref, m_ref, v_ref, g_out, d_out, m_out, v_out):
        g = p_ref[0, 0:1, :]
        for j in range(1, N_DEV):
            g = g + p_ref[j, 0:1, :]
        delta, m_new, v_new = _adamw_math(w_ref[...], g, m_ref[...], v_ref[...])
        g_out[...] = g
        d_out[...] = delta
        m_out[...] = m_new
        v_out[...] = v_new

    return pl.pallas_call(
        body, name=name, out_shape=[jax.ShapeDtypeStruct((1, D_MODEL), F32)] * 4,
        compiler_params=_params([((N_DEV, 8, D_MODEL), F32)] + [((8, D_MODEL), F32)] * 7),
    )(gathered, w, m, v)


def _adamw_small(name, gathered, w, m, v):
    rows, cols = w.shape

    def body(p_ref, w_ref, m_ref, v_ref, g_out, d_out, m_out, v_out):
        g = p_ref[0]
        for j in range(1, N_DEV):
            g = g + p_ref[j]
        delta, m_new, v_new = _adamw_math(w_ref[...], g, m_ref[...], v_ref[...])
        g_out[...] = g
        d_out[...] = delta
        m_out[...] = m_new
        v_out[...] = v_new

    tr = _tile(rows, 512)
    spec = pl.BlockSpec((tr, cols), lambda r: (r, 0))
    return pl.pallas_call(
        body, name=name, out_shape=[jax.ShapeDtypeStruct((rows, cols), F32)] * 4, grid=(rows // tr,),
        in_specs=[pl.BlockSpec((N_DEV, tr, cols), lambda r: (0, r, 0)), spec, spec, spec], out_specs=[spec] * 4,
        compiler_params=_params([((N_DEV, tr, cols), F32)] + [((tr, cols), F32)] * 7, sem=("arbitrary",)),
    )(gathered, w, m, v)


def kernel(x, norm_mix_g, w_in, gmlp_ln_g, gmlp_ln_b, gmlp_w_s, gmlp_b_s, hgrn_lb_table, hgrn_norm_g, w_branch_a, w_branch_b, w_out, norm_ffn_g, w_gate_up, w_down, norm_final_g, loss_target, m_norm_mix_g, m_w_in, m_gmlp_ln_g, m_gmlp_ln_b, m_gmlp_w_s, m_gmlp_b_s, m_hgrn_lb_table, m_hgrn_norm_g, m_w_branch_a, m_w_branch_b, m_w_out, m_norm_ffn_g, m_w_gate_up, m_w_down, m_norm_final_g, v_norm_mix_g, v_w_in, v_gmlp_ln_g, v_gmlp_ln_b, v_gmlp_w_s, v_gmlp_b_s, v_hgrn_lb_table, v_hgrn_norm_g, v_w_branch_a, v_w_branch_b, v_w_out, v_norm_ffn_g, v_w_gate_up, v_w_down, v_norm_final_g):
    t = x.shape[1]
    x2d = x.reshape(t, D_MODEL)
    target = loss_target.reshape(t, D_MODEL)
    final_g = norm_final_g.reshape(1, D_MODEL)

    shards = [w_in[0].astype(BF16), w_branch_a[0].astype(BF16), w_branch_b[0].astype(BF16), w_out[0].astype(BF16),
              w_gate_up[0].T.astype(BF16), w_down[0].astype(BF16)]

    def rows_of(n):
        return lambda ref, j: ref.at[pl.ds(pl.multiple_of(j * n, 8), n)]

    gathered = [((N_DEV, D_MODEL, D_MODEL), BF16), ((D_MODEL, D_MODEL), BF16), ((D_MODEL, D_MODEL), BF16),
                ((D_MODEL, D_MODEL), BF16), ((N_DEV, FF_BLOCK, D_MODEL), BF16), ((D_FF, D_MODEL), BF16)]
    places = [lambda ref, j: ref.at[_pos_of_dev(j)], rows_of(BRANCH_ROWS), rows_of(BRANCH_ROWS), rows_of(BRANCH_ROWS),
              lambda ref, j: ref.at[j], rows_of(DOWN_ROWS)]
    w_in_g = _all_gather_balanced_async("w_in_all_gather", 9, shards[0], gathered[0], places[0], D_MODEL)
    w_in_sibling = _swap_with_sibling("w_in_from_sibling", shards[0])
    _, later = lax.optimization_barrier((w_in_sibling, shards[1:]))
    w_a, w_b, w_o, w_gu, w_dn = _all_gather_async("weights_all_gather", 0, later, gathered[1:], places[1:])

    core_i, chip_i = lax.axis_index("c"), 2 * lax.axis_index("x") + lax.axis_index("y")
    own_pos = jnp.stack([_pos_of_dev(2 * chip_i + core_i), _pos_of_dev(2 * chip_i + 1 - core_i)]).astype(jnp.int32)
    other_pos = jnp.stack([_pos_of_dev(2 * jnp.bitwise_xor(chip_i, q) + cc) for q in (1, 2, 3) for cc in (0, 1)]).astype(jnp.int32)
    proj, h, h_t = _proj_forward_own_chip(own_pos, x2d, norm_mix_g, shards[0], w_in_sibling)
    proj = _proj_forward_other_chips(other_pos, proj, h, w_in_g)
    bias_b = jnp.broadcast_to(gmlp_b_s[0][:, :, None], (GROUPS, GMLP_CHUNK, GMLP_CHUNK))
    a = _gmlp_forward(proj, gmlp_ln_g, gmlp_ln_b, gmlp_w_s[0], bias_b)
    og, o_saved, states = _hgrn_forward(proj, hgrn_lb_table, hgrn_norm_g)
    ya, yb, merged, x1, h2 = _branch_out_forward(a, og, proj, x2d, w_a, w_b, w_o, norm_ffn_g)
    gu, act, loss_tile, d_final_g, dx2, dx2b = _ffn_forward(h2, x1, w_gu, w_dn, target, final_g)

    core = lax.axis_index("c").astype(jnp.int32).reshape(1)
    chip = (2 * lax.axis_index("x") + lax.axis_index("y")).astype(jnp.int32).reshape(1)
    branch_rows, branch_shape = rows_of(BRANCH_ROWS), (BRANCH_ROWS, D_MODEL)
    branch_block = ((BRANCH_ROWS, D_MODEL), lambda q, r, c: (2 * q + c, 0))

    def chip_partials(names, grads, land, own_blocks):
        return [_chip_partial("chip_partial_" + nme, core, g_, blk, idx, l_)
                for nme, g_, (blk, idx), l_ in zip(names, grads, own_blocks, land)]

    whole = pl.BlockSpec((t, D_MODEL), lambda n: (0, 0))
    whole_t = pl.BlockSpec((D_MODEL, t), lambda n: (0, 0))
    col_blocks = [((t, D_MODEL), BF16), ((t, 256), BF16), ((D_MODEL, 256), F32)]

    def square_grad(name, a_, b_):
        return _weight_grad(name, a_, b_, whole, pl.BlockSpec((t, 256), lambda n: (0, n)), (D_MODEL, D_MODEL),
                            pl.BlockSpec((D_MODEL, 256), lambda n: (0, n)), D_MODEL // 256, col_blocks, False)

    dgu, dx1, dx1b, d_ffn_g = _ffn_backward(dx2b, dx2, gu, x1, w_gu, w_dn, norm_ffn_g)
    g_gu = _weight_grad(
        "grad_w_gate_up", dgu, h2, pl.BlockSpec((None, None, t, FF_BLOCK), lambda j: (j % 4, j // 4, 0, 0)), whole,
        (N_DEV, FF_BLOCK, D_MODEL), pl.BlockSpec((None, FF_BLOCK, D_MODEL), lambda j: (j, 0, 0)), N_DEV,
        [((t, 768), BF16), ((t, D_MODEL), BF16), ((FF_BLOCK, D_MODEL), F32)], False)
    g_dn = _weight_grad(
        "grad_w_down", act, dx2b, pl.BlockSpec((None, t, FF_BLOCK), lambda j: (j, 0, 0)), whole, (D_FF, D_MODEL),
        pl.BlockSpec((FF_BLOCK, D_MODEL), lambda j: (j, 0)), 4,
        [((t, 768), BF16), ((t, D_MODEL), BF16), ((FF_BLOCK, D_MODEL), F32)], False)
    names_f, grads_f = ["w_gate_up", "w_down"], [g_gu, g_dn]
    land_f = _exchange_sibling("ffn_grads_to_sibling", 2, grads_f, [lambda ref, j: ref.at[j], rows_of(DOWN_ROWS)],
                               [(FF_BLOCK, D_MODEL), (DOWN_ROWS, D_MODEL)])

    dx1b_later, _ = lax.optimization_barrier((dx1b, grads_f))
    dya, dyb, dproj, da, dog = _branch_out_backward(dx1b_later, ya, yb, proj, w_a, w_b, w_o)
    g_a = square_grad("grad_w_a", a, dya)
    g_b = square_grad("grad_w_b", og, dyb)
    g_o = square_grad("grad_w_out", merged, dx1b)
    names_b, grads_b = ["w_branch_a", "w_branch_b", "w_out"], [g_a, g_b, g_o]
    land_b = _exchange_sibling("branch_grads_to_sibling", 3, grads_b, [branch_rows] * 3, [branch_shape] * 3)

    part_f = chip_partials(names_f, grads_f, land_f,
                           [((None, FF_BLOCK, D_MODEL), lambda q, r, c: (2 * q + c, 0, 0)),
                            ((DOWN_ROWS, D_MODEL), lambda q, r, c: (2 * q + c, 0))])
    landed_f = _exchange_chips("ffn_grads_to_chips", 5, part_f)

    dog, _ = lax.optimization_barrier((dog, part_f))
    dproj, d_hg_norm, d_lb = _hgrn_backward(dproj, dog, o_saved, states, proj, hgrn_lb_table, hgrn_norm_g)

    land_b, _ = lax.optimization_barrier((land_b, part_f))
    part_b = chip_partials(names_b, grads_b, land_b, [branch_block] * 3)
    landed_b = _exchange_chips("branch_grads_to_chips", 6, part_b)

    da, _ = lax.optimization_barrier((da, part_b))
    dproj, d_ln_g, d_ln_b, d_ws, d_bs = _gmlp_backward(dproj, da, proj, gmlp_ln_g, gmlp_ln_b, gmlp_w_s[0], bias_b)

    def packed(vals):
        return _pack_small(*vals)

    def flat_ws(a):
        return a.reshape(GROUPS * GMLP_CHUNK, GMLP_CHUNK)

    no_row = jnp.zeros((1, D_MODEL), F32)
    w_pack = packed([norm_mix_g, gmlp_ln_g, gmlp_ln_b, gmlp_b_s, hgrn_lb_table, hgrn_norm_g, norm_ffn_g, norm_final_g, no_row])
    m_pack = packed([m_norm_mix_g, m_gmlp_ln_g, m_gmlp_ln_b, m_gmlp_b_s, m_hgrn_lb_table, m_hgrn_norm_g, m_norm_ffn_g, m_norm_final_g, no_row])
    v_pack = packed([v_norm_mix_g, v_gmlp_ln_g, v_gmlp_ln_b, v_gmlp_b_s, v_hgrn_lb_table, v_hgrn_norm_g, v_norm_ffn_g, v_norm_final_g, no_row])
    small_partial = _pack_small(no_row, d_ln_g, d_ln_b, d_bs[:, :, 0], d_lb, d_hg_norm, d_ffn_g, d_final_g,
                                jnp.tile(loss_tile[0:1], (1, D_MODEL // 128)))
    small_all, ws_all = _all_gather_async(
        "small_grads_all_gather", 1, [small_partial, flat_ws(d_ws)],
        [((N_DEV, SMALL_ROWS, D_MODEL), F32), ((N_DEV, GROUPS * GMLP_CHUNK, GMLP_CHUNK), F32)],
        [lambda ref, j: ref.at[j], lambda ref, j: ref.at[j]])

    g_in = _weight_grad(
        "grad_w_in", h_t, dproj, whole_t, pl.BlockSpec((None, t, D_MODEL), lambda p: (p, 0, 0)), (N_DEV, D_MODEL, D_MODEL),
        pl.BlockSpec((None, D_MODEL, D_MODEL), lambda p: (p, 0, 0)), N_DEV,
        [((D_MODEL, t), BF16), ((t, D_MODEL), BF16), ((D_MODEL, D_MODEL), F32)], True)
    land_i = _exchange_sibling("w_in_grads_to_sibling", 4, [g_in], [lambda ref, j: ref.at[_pos_of_dev(j)]],
                               [(D_MODEL, D_MODEL)])

    big = {}
    for nme, own, lnd, w, m, v in zip(
            names_f + names_b, part_f + part_b, landed_f + landed_b,
            [w_gate_up, w_down, w_branch_a, w_branch_b, w_out], [m_w_gate_up, m_w_down, m_w_branch_a, m_w_branch_b, m_w_out],
            [v_w_gate_up, v_w_down, v_w_branch_a, v_w_branch_b, v_w_out]):
        flip = (lambda z: z.T) if nme == "w_gate_up" else (lambda z: z)
        big[nme] = [flip(o_)[None] for o_ in _adamw("adamw_" + nme, chip, own, lnd, flip(w[0]), flip(m[0]), flip(v[0]))]
    small, loss_row = _adamw_small_unpacked(small_all, w_pack, m_pack, v_pack)
    ws_outs = _adamw_small("adamw_w_s", ws_all, flat_ws(gmlp_w_s), flat_ws(m_gmlp_w_s), flat_ws(v_gmlp_w_s))
    land_i, _ = lax.optimization_barrier((land_i, (big, small, ws_outs)))
    part_i = chip_partials(["w_in"], [g_in], land_i,
                           [((None, D_MODEL, D_MODEL), lambda q, r, c: (_pos_of_dev(2 * q + c), 0, 0))])
    landed_i = _exchange_chips("w_in_grads_to_chips", 7, part_i)

    dx1, _ = lax.optimization_barrier((dx1, part_i))
    grad_x, d_mix_g = _input_backward(dproj, w_in_g, x2d, dx1, norm_mix_g)
    big["w_in"] = [o_[None] for o_ in _adamw("adamw_w_in", chip, part_i[0], landed_i[0], w_in[0], m_w_in[0], v_w_in[0])]

    def row8(a):
        return jnp.pad(a, ((0, 7), (0, 0)))

    d_mix_g, _ = lax.optimization_barrier((d_mix_g, landed_i))
    (mix_all,) = _all_gather_async("mix_gain_grad_all_gather", 8, [row8(d_mix_g)], [((N_DEV, 8, D_MODEL), F32)],
                                   [lambda ref, j: ref.at[j]])
    mix_outs = _adamw_row("adamw_mix_gain", mix_all, norm_mix_g, m_norm_mix_g, v_norm_mix_g)
    small = [dict(p, norm_final_g=p["norm_final_g"][0], gmlp_w_s=ws.reshape(1, GROUPS, GMLP_CHUNK, GMLP_CHUNK), norm_mix_g=q)
             for p, ws, q in zip(small, ws_outs, mix_outs)]

    loss = loss_row[0, 0]
    order = ["norm_mix_g", "w_in", "gmlp_ln_g", "gmlp_ln_b", "gmlp_w_s", "gmlp_b_s", "hgrn_lb_table", "hgrn_norm_g",
             "w_branch_a", "w_branch_b", "w_out", "norm_ffn_g", "w_gate_up", "w_down", "norm_final_g"]
    outs = [loss, grad_x.reshape(1, t, D_MODEL)]
    for kind in range(4):
        for nme in order:
            outs.append(big[nme][kind] if nme in big else small[kind][nme])
    return tuple(outs)
```

```python
import jax
import jax.numpy as jnp
from jax import lax
from jax.experimental import pallas as pl
from jax.experimental.pallas import tpu as pltpu
from jax.experimental.pallas import tpu_sc as plsc

F32, BF16 = jnp.float32, jnp.bfloat16
D_MODEL = 1024
N_DEV = 8
HEADS = 8
HEAD_DIM = 128
GROUPS = 8
GMLP_CHUNK = 128
HGRN_CHUNK = 64
HGRN_SCALE = HEAD_DIM ** -0.5
D_FF = 2816
FF_BLOCK = D_FF // 4
DOWN_ROWS = D_FF // N_DEV
BRANCH_ROWS = D_MODEL // N_DEV
NORM_EPS = 1e-6
ADAM_LR, ADAM_B1, ADAM_B2, ADAM_EPS, ADAM_WD, ADAM_STEP = 0.001, 0.9, 0.999, 1e-08, 0.01, 10
SMALL_ROWS = 72
V7X_VMEM_BYTES = 64 * 1024 * 1024
VMEM_CAP = V7X_VMEM_BYTES - 6 * 1024 * 1024
MESH_ID = pl.DeviceIdType.MESH
ANY = pl.BlockSpec(memory_space=pl.ANY)
RESIDENT = pl.BlockSpec(memory_space=pltpu.VMEM)
Q_POS, U_POS, GATE_POS = 0, 4, 6


def _pos_of_dev(j):
    return jnp.where(j < 2, j + 4, jnp.where(j < 6, j - 2, j))


def _nbytes(shape, dtype):
    n = 1
    for s in shape:
        n *= s
    return n * jnp.dtype(dtype).itemsize


def _params(blocks, scratch=(), temps=0, sem=None):
    need = 2 * sum(_nbytes(s, d) for s, d in blocks) + sum(_nbytes(s, d) for s, d in scratch) + temps
    assert need + (4 << 20) <= VMEM_CAP, need
    return pltpu.CompilerParams(dimension_semantics=sem, vmem_limit_bytes=VMEM_CAP)


def _tile(n, pref):
    return pref if n % pref == 0 else n


def _dot(a, b):
    return jnp.dot(a, b, preferred_element_type=F32)


def _dot_nt(a, b):
    return lax.dot_general(a, b, (((1,), (1,)), ((), ())), preferred_element_type=F32)


def _dot_tn(a, b):
    return lax.dot_general(a, b, (((0,), (0,)), ((), ())), preferred_element_type=F32)


def _sigmoid(x):
    return 1.0 / (1.0 + jnp.exp(-x))


_GELU_C = 0.7978845608028654


def _gelu(x):
    return x * (0.5 * (1.0 + jnp.tanh(_GELU_C * (x + 0.044715 * (x * x * x)))))


def _gelu_and_grad(x):
    t = jnp.tanh(_GELU_C * (x + 0.044715 * (x * x * x)))
    half = 0.5 * (1.0 + t)
    return x * half, half + 0.5 * x * (1.0 - t * t) * (_GELU_C * (1.0 + 3.0 * 0.044715 * x * x))


def _rms_stats(x):
    r = lax.rsqrt(jnp.mean(x * x, axis=-1, keepdims=True) + NORM_EPS)
    return r, x * r


def _rms_bwd(dy, x, g):
    r, xh = _rms_stats(x)
    dg = jnp.sum(dy * xh, axis=0, keepdims=True)
    dxh = dy * g
    dx = r * (dxh - xh * jnp.mean(dxh * xh, axis=-1, keepdims=True))
    return dx, dg


def _split3(x):
    hi = x.astype(BF16)
    r = x - hi.astype(F32)
    mid = r.astype(BF16)
    lo = (r - mid.astype(F32)).astype(BF16)
    return hi, mid, lo


def _mask_mm(mask_bf16, x):
    hi, mid, lo = _split3(x)
    return _dot(mask_bf16, hi) + _dot(mask_bf16, mid) + _dot(mask_bf16, lo)


def _place():
    return lax.axis_index("x"), lax.axis_index("y"), lax.axis_index("c")


def _gather_copies(src, out, send, recv, loc, slicers):
    n = len(src)
    x, y, c = _place()
    me, sib = (x, y, c), (x, y, 1 - c)
    chips = [(1 - x, y), (x, 1 - y), (1 - x, 1 - y)]

    def dev(p):
        return 4 * p[0] + 2 * p[1] + p[2]

    def rc(i, k, block, to, from_src=False):
        dst = slicers[i](out[i], dev(block))
        return pltpu.make_async_remote_copy(
            src_ref=src[i] if from_src else dst, dst_ref=dst, send_sem=send.at[7 * i + k],
            recv_sem=recv.at[7 * i + k], device_id=to, device_id_type=MESH_ID)

    mine = [pltpu.make_async_copy(src[i], slicers[i](out[i], dev(me)), loc.at[i]) for i in range(n)]
    for cp in mine:
        cp.start()
    first = []
    for i in range(n):
        first.append(rc(i, 0, me, sib, True))
        for j, chip in enumerate(chips):
            first.append(rc(i, 1 + j, me, (*chip, c), True))
    for cp in first:
        cp.start()
    passed = []
    for j, chip in enumerate(chips):
        for i in range(n):
            rc(i, 1 + j, (*chip, c), me).wait_recv()
            cp = rc(i, 4 + j, (*chip, c), sib)
            cp.start()
            passed.append(cp)
    for i in range(n):
        rc(i, 0, sib, me).wait_recv()
        for j, chip in enumerate(chips):
            rc(i, 4 + j, (*chip, 1 - c), me).wait_recv()
    for cp in first + passed:
        cp.wait_send()
    for cp in mine:
        cp.wait()


def _gather_copies_balanced(src, out, send, recv, loc, slicer, rows):
    x, y, c = _place()
    me, sib = (x, y, c), (x, y, 1 - c)
    xn, yn, dg = (1 - x, y), (x, 1 - y), (1 - x, 1 - y)
    half_rows = rows // 2

    def block(p):
        return slicer(out, 4 * p[0] + 2 * p[1] + p[2])

    def half(ref, h):
        return ref.at[pl.ds(h * half_rows, half_rows)]

    def rc(k, dst, to, from_src=False):
        return pltpu.make_async_remote_copy(src_ref=src if from_src else dst, dst_ref=dst, send_sem=send.at[k],
                                            recv_sem=recv.at[k], device_id=to, device_id_type=MESH_ID)

    mine = pltpu.make_async_copy(src, block(me), loc.at[0])
    mine.start()
    sends = [rc(0, block(me), sib, True), rc(1, block(me), (*xn, c), True), rc(2, block(me), (*yn, c), True)]
    for cp in sends:
        cp.start()

    def then(cp):
        cp.start()
        sends.append(cp)

    rc(1, block((*xn, c)), me).wait_recv()
    then(rc(3, half(block((*xn, c)), 0), (*yn, c)))
    then(rc(5, block((*xn, c)), sib))
    rc(2, block((*yn, c)), me).wait_recv()
    then(rc(4, half(block((*yn, c)), 1), (*xn, c)))
    then(rc(6, block((*yn, c)), sib))
    rc(3, half(block((*dg, c)), 0), me).wait_recv()
    then(rc(7, half(block((*dg, c)), 0), sib))
    rc(4, half(block((*dg, c)), 1), me).wait_recv()
    then(rc(8, half(block((*dg, c)), 1), sib))
    rc(0, block(sib), me).wait_recv()
    rc(5, block((*xn, 1 - c)), me).wait_recv()
    rc(6, block((*yn, 1 - c)), me).wait_recv()
    rc(7, half(block((*dg, 1 - c)), 0), me).wait_recv()
    rc(8, half(block((*dg, 1 - c)), 1), me).wait_recv()
    for cp in sends:
        cp.wait_send()
    mine.wait()


def _gather_scratch(n):
    return [pltpu.SemaphoreType.DMA((7 * n,)), pltpu.SemaphoreType.DMA((7 * n,)), pltpu.SemaphoreType.DMA((n,))]


def _handshake(peers):
    barrier = pltpu.get_barrier_semaphore()
    for peer in peers:
        pl.semaphore_signal(barrier, inc=1, device_id=peer, device_id_type=MESH_ID)
    pl.semaphore_wait(barrier, len(peers))


def _all_gather_async(name, collective_id, srcs, out_shapes, slicers):
    n = len(srcs)

    def body(*refs):
        x, y, c = _place()
        _handshake([(1 - x if dx else x, 1 - y if dy else y, 1 - c if dc else c)
                    for dx in (0, 1) for dy in (0, 1) for dc in (0, 1) if dx or dy or dc])
        _gather_copies(refs[:n], refs[n:2 * n], *refs[2 * n:], slicers)

    return _sequencer_call(name, collective_id, body, srcs, [jax.ShapeDtypeStruct(s, d) for s, d in out_shapes],
                           _gather_scratch(n))


def _all_gather_balanced_async(name, collective_id, src, out_shape, slicer, rows):
    def body(src_ref, out_ref, send, recv, loc):
        x, y, c = _place()
        _handshake([(1 - x if dx else x, 1 - y if dy else y, 1 - c if dc else c)
                    for dx in (0, 1) for dy in (0, 1) for dc in (0, 1) if dx or dy or dc])
        _gather_copies_balanced(src_ref, out_ref, send, recv, loc, slicer, rows)

    return _sequencer_call(name, collective_id, body, [src], [jax.ShapeDtypeStruct(*out_shape)],
                           [pltpu.SemaphoreType.DMA((9,)), pltpu.SemaphoreType.DMA((9,)), pltpu.SemaphoreType.DMA((1,))])[0]


def _sequencer_call(name, collective_id, body, operands, out_types, scratch):
    return pl.kernel(
        body, out_type=out_types, mesh=plsc.ScalarSubcoreMesh(axis_name="sequencer", num_cores=1), name=name,
        scratch_types=scratch, compiler_params=pltpu.CompilerParams(collective_id=collective_id),
    )(*operands)


def _exchange_sibling(name, collective_id, grads, shard_fns, shard_shapes):
    n = len(grads)

    def body(*refs):
        g, land = refs[:n], refs[n:2 * n]
        send, recv = refs[2 * n:]
        x, y, c = _place()
        _handshake([(x, y, 1 - c)])
        remote = []
        for i in range(n):
            for q in range(4):
                cp = pltpu.make_async_remote_copy(
                    src_ref=shard_fns[i](g[i], 2 * q + (1 - c)), dst_ref=land[i].at[q], send_sem=send.at[4 * i + q],
                    recv_sem=recv.at[4 * i + q], device_id=(x, y, 1 - c), device_id_type=MESH_ID)
                cp.start()
                remote.append(cp)
        for cp in remote:
            cp.wait()

    return _sequencer_call(name, collective_id, body, grads, [jax.ShapeDtypeStruct((4, *s), F32) for s in shard_shapes],
                           [pltpu.SemaphoreType.DMA((4 * n,)), pltpu.SemaphoreType.DMA((4 * n,))])


def _exchange_chips(name, collective_id, parts):
    n = len(parts)

    def body(*refs):
        part, out = refs[:n], refs[n:2 * n]
        send, recv = refs[2 * n:]
        x, y, c = _place()
        _handshake([(1 - x, y, c), (x, 1 - y, c), (1 - x, 1 - y, c)])
        remote = []
        for i in range(n):
            for s in range(3):
                qx = 1 - x if (s + 1) // 2 else x
                qy = 1 - y if (s + 1) % 2 else y
                cp = pltpu.make_async_remote_copy(
                    src_ref=part[i].at[2 * qx + qy], dst_ref=out[i].at[s], send_sem=send.at[3 * i + s],
                    recv_sem=recv.at[3 * i + s], device_id=(qx, qy, c), device_id_type=MESH_ID)
                cp.start()
                remote.append(cp)
        for cp in remote:
            cp.wait()

    return _sequencer_call(name, collective_id, body, parts,
                           [jax.ShapeDtypeStruct((3, *p.shape[1:]), p.dtype) for p in parts],
                           [pltpu.SemaphoreType.DMA((3 * n,)), pltpu.SemaphoreType.DMA((3 * n,))])


def _chip_partial(name, core, grad, own_block, own_index, land):
    _, rows, cols = land.shape
    tr = own_block[-2]

    def body(core_ref, a_ref, b_ref, o_ref):
        o_ref[...] = (a_ref[...] + b_ref[...]).astype(BF16)

    spec = pl.BlockSpec((None, tr, cols), lambda q, r, c: (q, r, 0))
    return pl.pallas_call(
        body, name=name, out_shape=jax.ShapeDtypeStruct(land.shape, BF16),
        grid_spec=pltpu.PrefetchScalarGridSpec(
            num_scalar_prefetch=1, grid=(4, rows // tr),
            in_specs=[pl.BlockSpec(own_block, lambda q, r, c: own_index(q, r, c[0])), spec], out_specs=spec),
        compiler_params=_params([((tr, cols), F32)] * 2 + [((tr, cols), BF16)], sem=("arbitrary", "arbitrary")),
    )(core, grad, land)


def _adamw_math(w, g, m, v):
    m = ADAM_B1 * m + (1.0 - ADAM_B1) * g
    v = ADAM_B2 * v + (1.0 - ADAM_B2) * (g * g)
    m_hat = m / (1.0 - ADAM_B1 ** ADAM_STEP)
    v_hat = v / (1.0 - ADAM_B2 ** ADAM_STEP)
    delta = -ADAM_LR * (m_hat / (jnp.sqrt(v_hat) + ADAM_EPS) + ADAM_WD * w)
    return delta, m, v


def _adamw(name, chip, own, landed, w, m, v):
    _, rows, cols = own.shape
    tr = _tile(rows, 256) if rows % 256 == 0 else _tile(rows, 176)

    def body(chip_ref, own_ref, l_ref, w_ref, m_ref, v_ref, g_out, d_out, m_out, v_out):
        g = own_ref[...].astype(F32)
        for s in range(3):
            g = g + l_ref[s].astype(F32)
        delta, m_new, v_new = _adamw_math(w_ref[...], g, m_ref[...], v_ref[...])
        g_out[...] = g
        d_out[...] = delta
        m_out[...] = m_new
        v_out[...] = v_new

    spec = pl.BlockSpec((tr, cols), lambda r, c: (r, 0))
    return pl.pallas_call(
        body, name=name, out_shape=[jax.ShapeDtypeStruct((rows, cols), F32)] * 4,
        grid_spec=pltpu.PrefetchScalarGridSpec(
            num_scalar_prefetch=1, grid=(rows // tr,),
            in_specs=[pl.BlockSpec((None, tr, cols), lambda r, c: (c[0], r, 0)),
                      pl.BlockSpec((3, tr, cols), lambda r, c: (0, r, 0)), spec, spec, spec],
            out_specs=[spec] * 4),
        compiler_params=_params([((4, tr, cols), own.dtype)] + [((tr, cols), F32)] * 7, sem=("arbitrary",)),
    )(chip, own, landed, w, m, v)


def _swap_with_sibling(name, x):
    def body(x_ref, o_ref, send, recv):
        px, py, c = _place()
        cp = pltpu.make_async_remote_copy(src_ref=x_ref, dst_ref=o_ref, send_sem=send, recv_sem=recv,
                                          device_id=(px, py, 1 - c), device_id_type=MESH_ID)
        cp.start()
        cp.wait()

    return pl.pallas_call(
        body, name=name, out_shape=jax.ShapeDtypeStruct(x.shape, x.dtype), in_specs=[ANY], out_specs=ANY,
        scratch_shapes=[pltpu.SemaphoreType.DMA, pltpu.SemaphoreType.DMA],
    )(x)


def _proj_forward_own_chip(positions, x, gain, w_own, w_sibling):
    t = x.shape[0]
    tm = _tile(t, 1024)

    def body(pos_ref, x_ref, g_ref, wo_ref, ws_ref, o_ref, h_ref, ht_ref):
        @pl.when(pl.program_id(1) == 0)
        def _():
            _, xh = _rms_stats(x_ref[...])
            h = (xh * g_ref[...]).astype(BF16)
            h_ref[...] = h
            ht_ref[...] = h.T
            o_ref[...] = _dot(h, wo_ref[...])

        @pl.when(pl.program_id(1) == 1)
        def _():
            o_ref[...] = _dot(h_ref[...], ws_ref[...])

    tok = pl.BlockSpec((tm, D_MODEL), lambda m, k, pos: (m, 0))
    return pl.pallas_call(
        body, name="proj_fwd_own_chip",
        out_shape=[jax.ShapeDtypeStruct((N_DEV, t, D_MODEL), F32), jax.ShapeDtypeStruct((t, D_MODEL), BF16),
                   jax.ShapeDtypeStruct((D_MODEL, t), BF16)],
        grid_spec=pltpu.PrefetchScalarGridSpec(
            num_scalar_prefetch=1, grid=(t // tm, 2),
            in_specs=[tok, pl.BlockSpec((1, D_MODEL), lambda m, k, pos: (0, 0)), RESIDENT, RESIDENT],
            out_specs=[pl.BlockSpec((None, tm, D_MODEL), lambda m, k, pos: (pos[k], m, 0)), tok,
                       pl.BlockSpec((D_MODEL, tm), lambda m, k, pos: (0, m))]),
        compiler_params=_params([((tm, D_MODEL), F32)] * 2 + [((tm, D_MODEL), BF16)] * 2,
                                scratch=[((2, D_MODEL, D_MODEL), BF16)], temps=6 << 20, sem=("arbitrary", "arbitrary")),
    )(positions, x, gain, w_own, w_sibling)


def _proj_forward_other_chips(positions, proj, h, w_in_g):
    t = h.shape[0]
    tm = _tile(t, 1024)

    def body(pos_ref, _, h_ref, w_ref, o_ref):
        o_ref[...] = _dot(h_ref[...], w_ref[pos_ref[pl.program_id(1)]])

    return pl.pallas_call(
        body, name="proj_fwd_other_chips", out_shape=jax.ShapeDtypeStruct(proj.shape, F32),
        grid_spec=pltpu.PrefetchScalarGridSpec(
            num_scalar_prefetch=1, grid=(t // tm, N_DEV - 2),
            in_specs=[ANY, pl.BlockSpec((tm, D_MODEL), lambda m, k, pos: (m, 0)), RESIDENT],
            out_specs=pl.BlockSpec((None, tm, D_MODEL), lambda m, k, pos: (pos[k], m, 0))),
        input_output_aliases={1: 0},
        compiler_params=_params([((tm, D_MODEL), F32), ((tm, D_MODEL), BF16)], scratch=[((N_DEV, D_MODEL, D_MODEL), BF16)],
                                temps=6 << 20, sem=("arbitrary", "arbitrary")),
    )(positions, proj, h, w_in_g)


def _masked_ws(ws_ref, g):
    row = lax.broadcasted_iota(jnp.int32, (GMLP_CHUNK, GMLP_CHUNK), 0)
    col = lax.broadcasted_iota(jnp.int32, (GMLP_CHUNK, GMLP_CHUNK), 1)
    return jnp.where(row >= col, ws_ref[g], 0.0).astype(BF16)


def _gmlp_forward(proj, ln_g, ln_b, w_s, bias_b):
    t = proj.shape[1]
    tm = _tile(t, 256)
    chunks = tm // GMLP_CHUNK

    def body(u_ref, v_ref, lng_ref, lnb_ref, ws_ref, bias_ref, a_ref, vn_scr):
        vv = _gelu(v_ref[...])
        mu = jnp.mean(vv, axis=-1, keepdims=True)
        cen = vv - mu
        var = jnp.mean(cen * cen, axis=-1, keepdims=True)
        vn_scr[...] = ((cen * lax.rsqrt(var + NORM_EPS)) * lng_ref[...] + lnb_ref[...]).astype(BF16)
        for g in range(GROUPS):
            wm = _masked_ws(ws_ref, g)
            cols = slice(g * HEAD_DIM, (g + 1) * HEAD_DIM)
            for c in range(chunks):
                rows = slice(c * GMLP_CHUNK, (c + 1) * GMLP_CHUNK)
                mixed = _dot(wm, vn_scr[rows, cols]) + bias_ref[g]
                a_ref[rows, cols] = (_gelu(u_ref[rows, cols]) * mixed).astype(BF16)

    small = pl.BlockSpec((GROUPS, GMLP_CHUNK, GMLP_CHUNK), lambda m: (0, 0, 0))
    vec = pl.BlockSpec((1, D_MODEL), lambda m: (0, 0))
    return pl.pallas_call(
        body, name="gmlp_fwd", out_shape=jax.ShapeDtypeStruct((t, D_MODEL), BF16), grid=(t // tm,),
        in_specs=[pl.BlockSpec((None, tm, D_MODEL), lambda m: (U_POS, m, 0)),
                  pl.BlockSpec((None, tm, D_MODEL), lambda m: (U_POS + 1, m, 0)), vec, vec, small, small],
        out_specs=pl.BlockSpec((tm, D_MODEL), lambda m: (m, 0)),
        scratch_shapes=[pltpu.VMEM((tm, D_MODEL), BF16)],
        compiler_params=_params([((tm, D_MODEL), F32)] * 2 + [((tm, D_MODEL), BF16)] + [((8, 128, 128), F32)] * 2,
                                scratch=[((tm, D_MODEL), BF16)], temps=8 << 20, sem=("arbitrary",)),
    )(proj, proj, ln_g, ln_b, w_s, bias_b)


def _lower_bound(tab_ref):
    t0, t1 = tab_ref[0:1, :], tab_ref[1:2, :]
    mx = jnp.maximum(t0, t1)
    e0, e1 = jnp.exp(t0 - mx), jnp.exp(t1 - mx)
    return e0 / (e0 + e1)


def _tri_masks():
    row = lax.broadcasted_iota(jnp.int32, (HGRN_CHUNK, HGRN_CHUNK), 0)
    col = lax.broadcasted_iota(jnp.int32, (HGRN_CHUNK, HGRN_CHUNK), 1)
    return row >= col, row <= col


def _chunk_rows(c):
    return slice(c * HGRN_CHUNK, (c + 1) * HGRN_CHUNK)


def _per_chunk(x, nc, fn):
    return jnp.concatenate([fn(x[_chunk_rows(c)]) for c in range(nc)], axis=0)


def _chunk_row_bcast(x, nc, i):
    return _per_chunk(x, nc, lambda xc: jnp.broadcast_to(xc[i:i + 1, :], (HGRN_CHUNK, HEAD_DIM)))


def _hgrn_gates(q, fl, lb, nc):
    lower, _ = _tri_masks()
    lower = lower.astype(BF16)
    s = _sigmoid(fl)
    f = lb + (1.0 - lb) * s
    k = 1.0 - f
    hi, mid, lo = _split3(jnp.log(f))
    a = jnp.concatenate([_dot(lower, hi[_chunk_rows(c)]) + _dot(lower, mid[_chunk_rows(c)]) + _dot(lower, lo[_chunk_rows(c)])
                         for c in range(nc)], axis=0)
    a_mid = _chunk_row_bcast(a, nc, HGRN_CHUNK // 2 - 1)
    a_last = _chunk_row_bcast(a, nc, HGRN_CHUNK - 1)
    qs = q * HGRN_SCALE
    e_in, e_out, e_end, e_all = jnp.exp(a - a_mid), jnp.exp(a_mid - a), jnp.exp(a_last - a), jnp.exp(a)
    decay = [jnp.exp(a[c * HGRN_CHUNK + HGRN_CHUNK - 1:(c + 1) * HGRN_CHUNK, :]) for c in range(nc)]
    return dict(s=s, f=f, k=k, decay=decay, e_in=e_in, e_out=e_out, e_end=e_end, e_all=e_all,
                qi=qs * e_in, ki=k * e_out, kd=k * e_end, qe=qs * e_all)


def _hgrn_forward(proj, lb_table, norm_g):
    t = proj.shape[1]
    tb = _tile(t, 1024)
    nc = tb // HGRN_CHUNK
    n_chunks = t // HGRN_CHUNK

    def body(q_ref, f_ref, i_ref, g_ref, tab_ref, ng_ref, og_ref, st_ref, state):
        @pl.when(pl.program_id(1) == 0)
        def _():
            state[...] = jnp.zeros_like(state)

        lower, _ = _tri_masks()
        gt = _hgrn_gates(q_ref[...], f_ref[...], _lower_bound(tab_ref), nc)
        qi, ki, kd, qe = (gt[n].astype(BF16) for n in ("qi", "ki", "kd", "qe"))
        vb = i_ref[...].astype(BF16)
        o_intra, d_state = [], []
        for c in range(nc):
            rows = _chunk_rows(c)
            p = jnp.where(lower, _dot_nt(qi[rows], ki[rows]), 0.0).astype(BF16)
            o_intra.append(_dot(p, vb[rows]))
            d_state.append(_dot_tn(vb[rows], kd[rows]))
        st = state[...]
        outs = []
        for c in range(nc):
            st_ref[c] = st
            outs.append(o_intra[c] + _dot_nt(qe[_chunk_rows(c)], st.astype(BF16)))
            st = st * gt["decay"][c] + d_state[c]
        state[...] = st
        _, oh = _rms_stats(jnp.concatenate(outs, axis=0))
        gz = g_ref[...]
        og_ref[...] = ((oh * ng_ref[...]) * (gz * _sigmoid(gz))).astype(BF16)

    def blk(p):
        return pl.BlockSpec((None, tb, HEAD_DIM), lambda h, n: (p, n, h))

    out_blk = pl.BlockSpec((tb, HEAD_DIM), lambda h, n: (n, h))
    return pl.pallas_call(
        body, name="hgrn_fwd",
        out_shape=[jax.ShapeDtypeStruct((t, D_MODEL), BF16), jax.ShapeDtypeStruct((HEADS, n_chunks, HEAD_DIM, HEAD_DIM), F32)],
        grid=(HEADS, t // tb),
        in_specs=[blk(Q_POS), blk(Q_POS + 1), blk(Q_POS + 2), blk(Q_POS + 3),
                  pl.BlockSpec((2, HEAD_DIM), lambda h, n: (0, h)), pl.BlockSpec((1, HEAD_DIM), lambda h, n: (0, h))],
        out_specs=[out_blk, pl.BlockSpec((None, nc, HEAD_DIM, HEAD_DIM), lambda h, n: (h, n, 0, 0))],
        scratch_shapes=[pltpu.VMEM((HEAD_DIM, HEAD_DIM), F32)],
        compiler_params=_params([((tb, HEAD_DIM), F32)] * 6 + [((nc, HEAD_DIM, HEAD_DIM), F32)], temps=8 << 20,
                                sem=("arbitrary", "arbitrary")),
    )(proj, proj, proj, proj, lb_table, norm_g)


def _branch_out_forward(a, og, proj, x, w_a, w_b, w_out, ffn_g):
    t = x.shape[0]
    tm = _tile(t, 512)

    def body(a_ref, og_ref, ga_ref, gb_ref, x_ref, wa_ref, wb_ref, wo_ref, g_ref, ya_ref, yb_ref, mg_ref, x1_ref, h2_ref):
        ya = _dot(a_ref[...], wa_ref[...])
        yb = _dot(og_ref[...], wb_ref[...])
        ya_ref[...] = ya
        yb_ref[...] = yb
        merged = (_sigmoid(ga_ref[...]) * ya + _sigmoid(gb_ref[...]) * yb).astype(BF16)
        mg_ref[...] = merged
        x1 = x_ref[...] + _dot(merged, wo_ref[...])
        x1_ref[...] = x1
        _, xh = _rms_stats(x1)
        h2_ref[...] = (xh * g_ref[...]).astype(BF16)

    tok = pl.BlockSpec((tm, D_MODEL), lambda m: (m, 0))
    return pl.pallas_call(
        body, name="branch_out_fwd",
        out_shape=[jax.ShapeDtypeStruct((t, D_MODEL), F32), jax.ShapeDtypeStruct((t, D_MODEL), F32),
                   jax.ShapeDtypeStruct((t, D_MODEL), BF16), jax.ShapeDtypeStruct((t, D_MODEL), F32),
                   jax.ShapeDtypeStruct((t, D_MODEL), BF16)],
        grid=(t // tm,),
        in_specs=[tok, tok, pl.BlockSpec((None, tm, D_MODEL), lambda m: (GATE_POS, m, 0)),
                  pl.BlockSpec((None, tm, D_MODEL), lambda m: (GATE_POS + 1, m, 0)), tok, RESIDENT, RESIDENT, RESIDENT,
                  pl.BlockSpec((1, D_MODEL), lambda m: (0, 0))],
        out_specs=[tok] * 5,
        compiler_params=_params([((tm, D_MODEL), BF16)] * 4 + [((tm, D_MODEL), F32)] * 6, scratch=[((D_MODEL, D_MODEL), BF16)] * 3,
                                temps=8 << 20, sem=("arbitrary",)),
    )(a, og, proj, proj, x, w_a, w_b, w_out, ffn_g)


def _ffn_forward(h2, x1, w_gu, w_down, target, final_g):
    t = x1.shape[0]
    tm = _tile(t, 512)

    def body(h_ref, wgu_ref, wd_ref, x1_ref, t_ref, g_ref, gu_ref, act_ref, loss_ref, dg_ref, dx_ref, dxb_ref, acc):
        m, j = pl.program_id(0), pl.program_id(1)

        @pl.when((m == 0) & (j == 0))
        def _():
            loss_ref[...] = jnp.zeros_like(loss_ref)
            dg_ref[...] = jnp.zeros_like(dg_ref)

        h = h_ref[...]
        gate = _dot_nt(h, wgu_ref[j])
        up = _dot_nt(h, wgu_ref[j + 4])
        gu_ref[0] = gate
        gu_ref[1] = up
        act = ((gate * _sigmoid(gate)) * up).astype(BF16)
        act_ref[...] = act
        part = _dot(act, wd_ref[j])

        @pl.when(j == 0)
        def _():
            acc[...] = part

        @pl.when((j > 0) & (j < 3))
        def _():
            acc[...] += part

        @pl.when(j == 3)
        def _():
            x2 = x1_ref[...] + (acc[...] + part)
            g = g_ref[...]
            r, xh = _rms_stats(x2)
            err = xh * g - t_ref[...]
            loss_ref[...] += 0.5 * jnp.sum(jnp.mean(err * err, axis=-1, keepdims=True), axis=0, keepdims=True)
            dy = err * (1.0 / D_MODEL)
            dg_ref[...] += jnp.sum(dy * xh, axis=0, keepdims=True)
            dxh = dy * g
            dx = r * (dxh - xh * jnp.mean(dxh * xh, axis=-1, keepdims=True))
            dx_ref[...] = dx
            dxb_ref[...] = dx.astype(BF16)

    tok = pl.BlockSpec((tm, D_MODEL), lambda m, j: (m, 0))
    vec = pl.BlockSpec((1, D_MODEL), lambda m, j: (0, 0))
    return pl.pallas_call(
        body, name="ffn_fwd",
        out_shape=[jax.ShapeDtypeStruct((4, 2, t, FF_BLOCK), F32), jax.ShapeDtypeStruct((4, t, FF_BLOCK), BF16),
                   jax.ShapeDtypeStruct((8, 128), F32), jax.ShapeDtypeStruct((1, D_MODEL), F32),
                   jax.ShapeDtypeStruct((t, D_MODEL), F32), jax.ShapeDtypeStruct((t, D_MODEL), BF16)],
        grid=(t // tm, 4),
        in_specs=[tok, RESIDENT, RESIDENT, tok, tok, vec],
        out_specs=[pl.BlockSpec((None, 2, tm, FF_BLOCK), lambda m, j: (j, 0, m, 0)),
                   pl.BlockSpec((None, tm, FF_BLOCK), lambda m, j: (j, m, 0)),
                   pl.BlockSpec((8, 128), lambda m, j: (0, 0)), vec, tok, tok],
        scratch_shapes=[pltpu.VMEM((tm, D_MODEL), F32)],
        compiler_params=_params([((tm, D_MODEL), BF16), ((tm, D_MODEL), F32), ((tm, D_MODEL), F32), ((2, tm, 768), F32),
                                 ((tm, 768), BF16), ((tm, D_MODEL), F32), ((tm, D_MODEL), BF16)],
                                scratch=[((tm, D_MODEL), F32), ((N_DEV, FF_BLOCK, D_MODEL), BF16), ((D_FF, D_MODEL), BF16)],
                                temps=6 << 20, sem=("arbitrary", "arbitrary")),
    )(h2, w_gu, w_down.reshape(4, FF_BLOCK, D_MODEL), x1, target, final_g)


def _ffn_backward(dx2b, dx2, gu, x1, w_gu, w_down, ffn_g):
    t = x1.shape[0]
    tm = _tile(t, 512)

    def body(dxb_ref, dx2_ref, gu_ref, x1_ref, wgu_ref, wd_ref, g_ref, dgu_ref, dx1_ref, dx1b_ref, dg_ref, acc, prev):
        m, j = pl.program_id(0), pl.program_id(1)

        @pl.when((m == 0) & (j == 0))
        def _():
            dg_ref[...] = jnp.zeros_like(dg_ref)

        @pl.when(j == 0)
        def _():
            prev[...] = jnp.zeros_like(prev)
            acc[...] = jnp.zeros_like(acc)

        jm1 = jnp.maximum(j - 1, 0)
        acc[...] += _dot(prev[0], wgu_ref[jm1]) + _dot(prev[1], wgu_ref[jm1 + 4])
        dact = _dot_nt(dxb_ref[...], wd_ref[j])
        gate, up = gu_ref[0], gu_ref[1]
        sg = _sigmoid(gate)
        dgate = (dact * up * (sg * (1.0 + gate * (1.0 - sg)))).astype(BF16)
        dup = (dact * (gate * sg)).astype(BF16)
        dgu_ref[0] = dgate
        dgu_ref[1] = dup
        prev[0] = dgate
        prev[1] = dup

        @pl.when(j == 3)
        def _():
            dh2 = acc[...] + (_dot(prev[0], wgu_ref[3]) + _dot(prev[1], wgu_ref[7]))
            dx, dg = _rms_bwd(dh2, x1_ref[...], g_ref[...])
            dx1 = dx2_ref[...] + dx
            dx1_ref[...] = dx1
            dx1b_ref[...] = dx1.astype(BF16)
            dg_ref[...] += dg

    tok = pl.BlockSpec((tm, D_MODEL), lambda m, j: (m, 0))
    vec = pl.BlockSpec((1, D_MODEL), lambda m, j: (0, 0))
    gu_spec = pl.BlockSpec((None, 2, tm, FF_BLOCK), lambda m, j: (j, 0, m, 0))
    return pl.pallas_call(
        body, name="ffn_bwd",
        out_shape=[jax.ShapeDtypeStruct((4, 2, t, FF_BLOCK), BF16), jax.ShapeDtypeStruct((t, D_MODEL), F32),
                   jax.ShapeDtypeStruct((t, D_MODEL), BF16), jax.ShapeDtypeStruct((1, D_MODEL), F32)],
        grid=(t // tm, 4),
        in_specs=[tok, tok, gu_spec, tok, RESIDENT, RESIDENT, vec],
        out_specs=[gu_spec, tok, tok, vec],
        scratch_shapes=[pltpu.VMEM((tm, D_MODEL), F32), pltpu.VMEM((2, tm, FF_BLOCK), BF16)],
        compiler_params=_params([((tm, D_MODEL), BF16), ((tm, D_MODEL), F32), ((2, tm, 768), F32), ((tm, D_MODEL), F32),
                                 ((2, tm, 768), BF16), ((tm, D_MODEL), F32), ((tm, D_MODEL), BF16)],
                                scratch=[((tm, D_MODEL), F32), ((2, tm, 768), BF16), ((N_DEV, FF_BLOCK, D_MODEL), BF16),
                                         ((D_FF, D_MODEL), BF16)],
                                temps=4 << 20, sem=("arbitrary", "arbitrary")),
    )(dx2b, dx2, gu, x1, w_gu, w_down.reshape(4, FF_BLOCK, D_MODEL), ffn_g)


def _branch_out_backward(dx1b, ya, yb, proj, w_a, w_b, w_out):
    t = ya.shape[0]
    tm = _tile(t, 512)

    def body(dx_ref, ya_ref, yb_ref, ga_ref, gb_ref, wa_ref, wb_ref, wo_ref, dya_ref, dyb_ref, dgate_ref, da_ref, dog_ref):
        dm = _dot_nt(dx_ref[...], wo_ref[...])
        sa, sb = _sigmoid(ga_ref[...]), _sigmoid(gb_ref[...])
        dya = (dm * sa).astype(BF16)
        dyb = (dm * sb).astype(BF16)
        dya_ref[...] = dya
        dyb_ref[...] = dyb
        dgate_ref[0] = (dm * ya_ref[...] * (sa * (1.0 - sa))).astype(BF16)
        dgate_ref[1] = (dm * yb_ref[...] * (sb * (1.0 - sb))).astype(BF16)
        da_ref[...] = _dot_nt(dya, wa_ref[...])
        dog_ref[...] = _dot_nt(dyb, wb_ref[...])

    tok = pl.BlockSpec((tm, D_MODEL), lambda m: (m, 0))
    return pl.pallas_call(
        body, name="branch_out_bwd",
        out_shape=[jax.ShapeDtypeStruct((t, D_MODEL), BF16), jax.ShapeDtypeStruct((t, D_MODEL), BF16),
                   jax.ShapeDtypeStruct((N_DEV, t, D_MODEL), BF16), jax.ShapeDtypeStruct((t, D_MODEL), F32),
                   jax.ShapeDtypeStruct((t, D_MODEL), F32)],
        grid=(t // tm,),
        in_specs=[tok, tok, tok, pl.BlockSpec((None, tm, D_MODEL), lambda m: (GATE_POS, m, 0)),
                  pl.BlockSpec((None, tm, D_MODEL), lambda m: (GATE_POS + 1, m, 0)), RESIDENT, RESIDENT, RESIDENT],
        out_specs=[tok, tok, pl.BlockSpec((2, tm, D_MODEL), lambda m: (GATE_POS // 2, m, 0)), tok, tok],
        compiler_params=_params([((tm, D_MODEL), BF16)] * 5 + [((tm, D_MODEL), F32)] * 6, scratch=[((D_MODEL, D_MODEL), BF16)] * 3,
                                temps=8 << 20, sem=("arbitrary",)),
    )(dx1b, ya, yb, proj, proj, w_a, w_b, w_out)


def _hgrn_backward(dproj, dog, states, proj, lb_table, norm_g):
    t = proj.shape[1]
    tb = _tile(t, 1024)
    nc = tb // HGRN_CHUNK
    nb = t // tb

    def body(_, dog_ref, st_ref, q_ref, f_ref, i_ref, g_ref, tab_ref, ng_ref, dp_ref, dng_ref, dtab_ref, gstate):
        @pl.when(pl.program_id(1) == 0)
        def _():
            gstate[...] = jnp.zeros_like(gstate)
            dng_ref[...] = jnp.zeros_like(dng_ref)
            dtab_ref[...] = jnp.zeros_like(dtab_ref)

        lb = _lower_bound(tab_ref)
        ng = ng_ref[...]
        lower, upper = _tri_masks()
        gt = _hgrn_gates(q_ref[...], f_ref[...], lb, nc)
        qi, ki, kd, qe = (gt[n].astype(BF16) for n in ("qi", "ki", "kd", "qe"))
        vb = i_ref[...].astype(BF16)
        scores = [jnp.where(lower, _dot_nt(qi[_chunk_rows(c)], ki[_chunk_rows(c)]), 0.0).astype(BF16) for c in range(nc)]
        o = jnp.concatenate([_dot(scores[c], vb[_chunk_rows(c)]) + _dot_nt(qe[_chunk_rows(c)], st_ref[c].astype(BF16))
                             for c in range(nc)], axis=0)
        gz, d_og = g_ref[...], dog_ref[...]
        r, oh = _rms_stats(o)
        sg = _sigmoid(gz)
        d_on = d_og * (gz * sg)
        dgz = d_og * (oh * ng) * (sg * (1.0 + gz * (1.0 - sg)))
        dng_ref[...] += jnp.sum(d_on * oh, axis=0, keepdims=True)
        doh = d_on * ng
        dob = (r * (doh - oh * jnp.mean(doh * oh, axis=-1, keepdims=True))).astype(BF16)
        dv_intra, dqi, dki, dqe, g_upd = [], [], [], [], []
        for c in range(nc):
            rows = _chunk_rows(c)
            dv_intra.append(_dot_tn(scores[c], dob[rows]))
            dp = jnp.where(lower, _dot_nt(dob[rows], vb[rows]), 0.0).astype(BF16)
            dqi.append(_dot(dp, ki[rows]))
            dki.append(_dot_tn(dp, qi[rows]))
            dqe.append(_dot(dob[rows], st_ref[c].astype(BF16)))
            g_upd.append(_dot_tn(dob[rows], qe[rows]))
        g_after = [None] * nc
        g = gstate[...]
        for c in reversed(range(nc)):
            g_after[c] = g
            g = g * gt["decay"][c] + g_upd[c]
        gstate[...] = g
        dkd, dv, da_last = [], [], []
        for c in range(nc):
            rows = _chunk_rows(c)
            gb = g_after[c].astype(BF16)
            dkd.append(_dot(vb[rows], gb))
            dv.append(dv_intra[c] + _dot_nt(kd[rows], gb))
            da_last.append(jnp.sum(g_after[c] * st_ref[c], axis=0, keepdims=True) * gt["decay"][c])
        dqi, dki, dqe, dkd, dv = (jnp.concatenate(z, axis=0) for z in (dqi, dki, dqe, dkd, dv))
        dqs = dqi * gt["e_in"] + dqe * gt["e_all"]
        dk = dki * gt["e_out"] + dkd * gt["e_end"]
        t_in, t_out, t_end = dqi * gt["qi"], dki * gt["ki"], dkd * gt["kd"]
        da = t_in - t_out + dqe * gt["qe"] - t_end
        row = lax.broadcasted_iota(jnp.int32, (HGRN_CHUNK, HEAD_DIM), 0)
        d_mid = t_out - t_in
        pieces = []
        for c in range(nc):
            rows = _chunk_rows(c)
            da_mid = jnp.sum(d_mid[rows], axis=0, keepdims=True)
            da_end = jnp.sum(t_end[rows], axis=0, keepdims=True) + da_last[c]
            da_c = da[rows] + jnp.where(row == HGRN_CHUNK // 2 - 1, da_mid, 0.0) + jnp.where(row == HGRN_CHUNK - 1, da_end, 0.0)
            pieces.append(_mask_mm(upper.astype(BF16), da_c))
        df = jnp.concatenate(pieces, axis=0) / gt["f"] - dk
        s = gt["s"]
        dlb = jnp.sum(df * (1.0 - s), axis=0, keepdims=True)
        dp_ref[0] = (dqs * HGRN_SCALE).astype(BF16)
        dp_ref[1] = (df * (1.0 - lb) * (s * (1.0 - s))).astype(BF16)
        dp_ref[2] = dv.astype(BF16)
        dp_ref[3] = dgz.astype(BF16)
        dt0 = dlb * (lb * (1.0 - lb))
        dtab_ref[0:1, :] += dt0
        dtab_ref[1:2, :] -= dt0

    def blk(p):
        return pl.BlockSpec((None, tb, HEAD_DIM), lambda h, n: (p, nb - 1 - n, h))

    tok = pl.BlockSpec((tb, HEAD_DIM), lambda h, n: (nb - 1 - n, h))
    return pl.pallas_call(
        body, name="hgrn_bwd",
        out_shape=[jax.ShapeDtypeStruct((N_DEV, t, D_MODEL), BF16), jax.ShapeDtypeStruct((1, D_MODEL), F32),
                   jax.ShapeDtypeStruct((2, D_MODEL), F32)],
        grid=(HEADS, nb),
        in_specs=[ANY, tok, pl.BlockSpec((None, nc, HEAD_DIM, HEAD_DIM), lambda h, n: (h, nb - 1 - n, 0, 0)),
                  blk(Q_POS), blk(Q_POS + 1), blk(Q_POS + 2), blk(Q_POS + 3),
                  pl.BlockSpec((2, HEAD_DIM), lambda h, n: (0, h)), pl.BlockSpec((1, HEAD_DIM), lambda h, n: (0, h))],
        out_specs=[pl.BlockSpec((4, tb, HEAD_DIM), lambda h, n: (0, nb - 1 - n, h)),
                   pl.BlockSpec((1, HEAD_DIM), lambda h, n: (0, h)), pl.BlockSpec((2, HEAD_DIM), lambda h, n: (0, h))],
        scratch_shapes=[pltpu.VMEM((HEAD_DIM, HEAD_DIM), F32)],
        input_output_aliases={0: 0},
        compiler_params=_params([((tb, HEAD_DIM), F32)] * 6 + [((nc, HEAD_DIM, HEAD_DIM), F32)] + [((4, tb, HEAD_DIM), BF16)],
                                temps=8 << 20, sem=("arbitrary", "arbitrary")),
    )(dproj, dog, states, proj, proj, proj, proj, lb_table, norm_g)


def _gmlp_backward(dproj, da, proj, ln_g, ln_b, w_s, bias_b):
    t = proj.shape[1]
    tm = _tile(t, 256)
    chunks = tm // GMLP_CHUNK

    def body(_, da_ref, u_ref, v_ref, lng_ref, lnb_ref, ws_ref, bias_ref, dp_ref, dlng_ref, dlnb_ref, dws_ref, dbs_ref,
             vn_scr, dvn_scr):
        @pl.when(pl.program_id(0) == 0)
        def _():
            dlng_ref[...] = jnp.zeros_like(dlng_ref)
            dlnb_ref[...] = jnp.zeros_like(dlnb_ref)
            dws_ref[...] = jnp.zeros_like(dws_ref)
            dbs_ref[...] = jnp.zeros_like(dbs_ref)

        v = v_ref[...]
        vv, dvv_dv = _gelu_and_grad(v)
        mu = jnp.mean(vv, axis=-1, keepdims=True)
        cen = vv - mu
        rstd = lax.rsqrt(jnp.mean(cen * cen, axis=-1, keepdims=True) + NORM_EPS)
        vhat = cen * rstd
        lng = lng_ref[...]
        vn_scr[...] = (vhat * lng + lnb_ref[...]).astype(BF16)
        row = lax.broadcasted_iota(jnp.int32, (GMLP_CHUNK, GMLP_CHUNK), 0)
        col = lax.broadcasted_iota(jnp.int32, (GMLP_CHUNK, GMLP_CHUNK), 1)
        for g in range(GROUPS):
            wm = _masked_ws(ws_ref, g)
            cols = slice(g * HEAD_DIM, (g + 1) * HEAD_DIM)
            dws = jnp.zeros((GMLP_CHUNK, GMLP_CHUNK), F32)
            dbs = jnp.zeros((GMLP_CHUNK, GMLP_CHUNK), F32)
            for c in range(chunks):
                rows = slice(c * GMLP_CHUNK, (c + 1) * GMLP_CHUNK)
                vn = vn_scr[rows, cols]
                mixed = _dot(wm, vn) + bias_ref[g]
                u = u_ref[rows, cols]
                d_a = da_ref[rows, cols]
                gelu_u, dgelu_u = _gelu_and_grad(u)
                dp_ref[0, rows, cols] = (d_a * mixed * dgelu_u).astype(BF16)
                dmix = d_a * gelu_u
                dmb = dmix.astype(BF16)
                dbs = dbs + dmix
                dws = dws + _dot_nt(dmb, vn)
                dvn_scr[rows, cols] = _dot_tn(wm, dmb)
            dws_ref[g] += jnp.where(row >= col, dws, 0.0)
            dbs_ref[g] += jnp.broadcast_to(jnp.sum(dbs, axis=-1, keepdims=True), (GMLP_CHUNK, GMLP_CHUNK))
        dvn = dvn_scr[...]
        dlng_ref[...] += jnp.sum(dvn * vhat, axis=0, keepdims=True)
        dlnb_ref[...] += jnp.sum(dvn, axis=0, keepdims=True)
        dvh = dvn * lng
        dvv = rstd * (dvh - jnp.mean(dvh, axis=-1, keepdims=True) - vhat * jnp.mean(dvh * vhat, axis=-1, keepdims=True))
        dp_ref[1] = (dvv * dvv_dv).astype(BF16)

    tok = pl.BlockSpec((tm, D_MODEL), lambda m: (m, 0))
    small = pl.BlockSpec((GROUPS, GMLP_CHUNK, GMLP_CHUNK), lambda m: (0, 0, 0))
    vec = pl.BlockSpec((1, D_MODEL), lambda m: (0, 0))
    return pl.pallas_call(
        body, name="gmlp_bwd",
        out_shape=[jax.ShapeDtypeStruct(dproj.shape, BF16), jax.ShapeDtypeStruct((1, D_MODEL), F32),
                   jax.ShapeDtypeStruct((1, D_MODEL), F32), jax.ShapeDtypeStruct((GROUPS, GMLP_CHUNK, GMLP_CHUNK), F32),
                   jax.ShapeDtypeStruct((GROUPS, GMLP_CHUNK, GMLP_CHUNK), F32)],
        grid=(t // tm,),
        in_specs=[ANY, tok, pl.BlockSpec((None, tm, D_MODEL), lambda m: (U_POS, m, 0)),
                  pl.BlockSpec((None, tm, D_MODEL), lambda m: (U_POS + 1, m, 0)), vec, vec, small, small],
        out_specs=[pl.BlockSpec((2, tm, D_MODEL), lambda m: (U_POS // 2, m, 0)), vec, vec, small, small],
        scratch_shapes=[pltpu.VMEM((tm, D_MODEL), BF16), pltpu.VMEM((tm, D_MODEL), F32)],
        input_output_aliases={0: 0},
        compiler_params=_params([((tm, D_MODEL), F32)] * 3 + [((2, tm, D_MODEL), BF16)] + [((8, 128, 128), F32)] * 4,
                                scratch=[((tm, D_MODEL), BF16), ((tm, D_MODEL), F32)], temps=12 << 20, sem=("arbitrary",)),
    )(dproj, da, proj, proj, ln_g, ln_b, w_s, bias_b)


def _input_backward(dproj, w_in_g, x, dx1, mix_g):
    t = x.shape[0]
    tm = _tile(t, 512)

    def body(dp_ref, w_ref, x_ref, dx1_ref, g_ref, dx_ref, dg_ref):
        @pl.when(pl.program_id(0) == 0)
        def _():
            dg_ref[...] = jnp.zeros_like(dg_ref)

        dh = _dot_nt(dp_ref[0], w_ref[0])
        for p in range(1, N_DEV):
            dh = dh + _dot_nt(dp_ref[p], w_ref[p])
        dx, dg = _rms_bwd(dh, x_ref[...], g_ref[...])
        dx_ref[...] = dx1_ref[...] + dx
        dg_ref[...] += dg

    tok = pl.BlockSpec((tm, D_MODEL), lambda m: (m, 0))
    vec = pl.BlockSpec((1, D_MODEL), lambda m: (0, 0))
    return pl.pallas_call(
        body, name="input_bwd",
        out_shape=[jax.ShapeDtypeStruct((t, D_MODEL), F32), jax.ShapeDtypeStruct((1, D_MODEL), F32)],
        grid=(t // tm,),
        in_specs=[pl.BlockSpec((N_DEV, tm, D_MODEL), lambda m: (0, m, 0)), RESIDENT, tok, tok, vec],
        out_specs=[tok, vec],
        compiler_params=_params([((N_DEV, tm, D_MODEL), BF16)] + [((tm, D_MODEL), F32)] * 3,
                                scratch=[((N_DEV, D_MODEL, D_MODEL), BF16)], temps=6 << 20, sem=("arbitrary",)),
    )(dproj, w_in_g, x, dx1, mix_g)


def _weight_grad(name, a, b, a_spec, b_spec, out_shape, out_spec, steps, blocks, a_is_transposed):
    def body(a_ref, b_ref, o_ref):
        o_ref[...] = _dot(a_ref[...], b_ref[...]) if a_is_transposed else _dot_tn(a_ref[...], b_ref[...])

    return pl.pallas_call(
        body, name=name, out_shape=jax.ShapeDtypeStruct(out_shape, F32), grid=(steps,), in_specs=[a_spec, b_spec],
        out_specs=out_spec, compiler_params=_params(blocks, temps=4 << 20, sem=("arbitrary",)),
    )(a, b)


def _pack_small(mix_g, ln_g, ln_b, b_s, lb_table, hg_norm, ffn_g, final_g, loss_row):
    def part(a):
        a = a.reshape(-1, D_MODEL)
        return jnp.pad(a, ((0, 8 - a.shape[0]), (0, 0)))

    return jnp.concatenate([part(mix_g), part(ln_g), part(ln_b), part(hg_norm), part(ffn_g), part(final_g),
                            part(lb_table), part(b_s), part(loss_row)], axis=0)


SMALL_PARTS = (("gmlp_ln_g", 8, 1), ("gmlp_ln_b", 16, 1), ("hgrn_norm_g", 24, 1), ("norm_ffn_g", 32, 1), ("norm_final_g", 40, 1),
               ("hgrn_lb_table", 48, 2))


def _adamw_small_unpacked(gathered, w, m, v):
    rows = w.shape[0]
    n_out = len(SMALL_PARTS) + 1

    def body(p_ref, w_ref, m_ref, v_ref, *outs):
        g = p_ref[0]
        for j in range(1, N_DEV):
            g = g + p_ref[j]
        delta, m_new, v_new = _adamw_math(w_ref[...], g, m_ref[...], v_ref[...])
        for kind, val in enumerate((g, delta, m_new, v_new)):
            refs = outs[kind * n_out:(kind + 1) * n_out]
            for (_, first, count), ref in zip(SMALL_PARTS, refs):
                ref[...] = val[first:first + count]
            for grp in range(GROUPS):
                refs[-1][0, grp:grp + 1, :] = val[56:57, grp * GMLP_CHUNK:(grp + 1) * GMLP_CHUNK]
        outs[-1][...] = g[SMALL_ROWS - 8:SMALL_ROWS - 7]

    shapes = [jax.ShapeDtypeStruct((count, D_MODEL), F32) for _, _, count in SMALL_PARTS]
    shapes.append(jax.ShapeDtypeStruct((1, GROUPS, GMLP_CHUNK), F32))
    whole = pl.BlockSpec((rows, D_MODEL), lambda: (0, 0))
    res = pl.pallas_call(
        body, name="adamw_small", out_shape=shapes * 4 + [jax.ShapeDtypeStruct((1, D_MODEL), F32)],
        in_specs=[pl.BlockSpec((N_DEV, rows, D_MODEL), lambda: (0, 0, 0)), whole, whole, whole],
        compiler_params=_params([((N_DEV, rows, D_MODEL), F32)] + [((rows, D_MODEL), F32)] * 7),
    )(gathered, w, m, v)
    names = [nme for nme, _, _ in SMALL_PARTS] + ["gmlp_b_s"]
    return [dict(zip(names, res[kind * n_out:(kind + 1) * n_out])) for kind in range(4)], res[-1]


def _adamw_row(name, gathered, w, m, v):
    def body(p_ref, w_ref, m_ref, v_ref, g_out, d_out, m_out, v_out):
        g = p_ref[0, 0:1, :]
        for j in range(1, N_DEV):
            g = g + p_ref[j, 0:1, :]
        delta, m_new, v_new = _adamw_math(w_ref[...], g, m_ref[...], v_ref[...])
        g_out[...] = g
        d_out[...] = delta
        m_out[...] = m_new
        v_out[...] = v_new

    return pl.pallas_call(
        body, name=name, out_shape=[jax.ShapeDtypeStruct((1, D_MODEL), F32)] * 4,
        compiler_params=_params([((N_DEV, 8, D_MODEL), F32)] + [((8, D_MODEL), F32)] * 7),
    )(gathered, w, m, v)


def _adamw_small(name, gathered, w, m, v):
    rows, cols = w.shape

    def body(p_ref, w_ref, m_ref, v_ref, g_out, d_out, m_out, v_out):
        g = p_ref[0]
        for j in range(1, N_DEV):
            g = g + p_ref[j]
        delta, m_new, v_new = _adamw_math(w_ref[...], g, m_ref[...], v_ref[...])
        g_out[...] = g
        d_out[...] = delta
        m_out[...] = m_new
        v_out[...] = v_new

    tr = _tile(rows, 512)
    spec = pl.BlockSpec((tr, cols), lambda r: (r, 0))
    return pl.pallas_call(
        body, name=name, out_shape=[jax.ShapeDtypeStruct((rows, cols), F32)] * 4, grid=(rows // tr,),
        in_specs=[pl.BlockSpec((N_DEV, tr, cols), lambda r: (0, r, 0)), spec, spec, spec], out_specs=[spec] * 4,
        compiler_params=_params([((N_DEV, tr, cols), F32)] + [((tr, cols), F32)] * 7, sem=("arbitrary",)),
    )(gathered, w, m, v)


def kernel(x, norm_mix_g, w_in, gmlp_ln_g, gmlp_ln_b, gmlp_w_s, gmlp_b_s, hgrn_lb_table, hgrn_norm_g, w_branch_a, w_branch_b, w_out, norm_ffn_g, w_gate_up, w_down, norm_final_g, loss_target, m_norm_mix_g, m_w_in, m_gmlp_ln_g, m_gmlp_ln_b, m_gmlp_w_s, m_gmlp_b_s, m_hgrn_lb_table, m_hgrn_norm_g, m_w_branch_a, m_w_branch_b, m_w_out, m_norm_ffn_g, m_w_gate_up, m_w_down, m_norm_final_g, v_norm_mix_g, v_w_in, v_gmlp_ln_g, v_gmlp_ln_b, v_gmlp_w_s, v_gmlp_b_s, v_hgrn_lb_table, v_hgrn_norm_g, v_w_branch_a, v_w_branch_b, v_w_out, v_norm_ffn_g, v_w_gate_up, v_w_down, v_norm_final_g):
    t = x.shape[1]
    x2d = x.reshape(t, D_MODEL)
    target = loss_target.reshape(t, D_MODEL)
    final_g = norm_final_g.reshape(1, D_MODEL)

    shards = [w_in[0].astype(BF16), w_branch_a[0].astype(BF16), w_branch_b[0].astype(BF16), w_out[0].astype(BF16),
              w_gate_up[0].T.astype(BF16), w_down[0].astype(BF16)]

    def rows_of(n):
        return lambda ref, j: ref.at[pl.ds(pl.multiple_of(j * n, 8), n)]

    gathered = [((N_DEV, D_MODEL, D_MODEL), BF16), ((D_MODEL, D_MODEL), BF16), ((D_MODEL, D_MODEL), BF16),
                ((D_MODEL, D_MODEL), BF16), ((N_DEV, FF_BLOCK, D_MODEL), BF16), ((D_FF, D_MODEL), BF16)]
    places = [lambda ref, j: ref.at[_pos_of_dev(j)], rows_of(BRANCH_ROWS), rows_of(BRANCH_ROWS), rows_of(BRANCH_ROWS),
              lambda ref, j: ref.at[j], rows_of(DOWN_ROWS)]
    w_in_g = _all_gather_balanced_async("w_in_all_gather", 9, shards[0], gathered[0], places[0], D_MODEL)
    w_in_sibling = _swap_with_sibling("w_in_from_sibling", shards[0])
    _, later = lax.optimization_barrier((w_in_sibling, shards[1:]))
    w_a, w_b, w_o, w_gu, w_dn = _all_gather_async("weights_all_gather", 0, later, gathered[1:], places[1:])

    core_i, chip_i = lax.axis_index("c"), 2 * lax.axis_index("x") + lax.axis_index("y")
    own_pos = jnp.stack([_pos_of_dev(2 * chip_i + core_i), _pos_of_dev(2 * chip_i + 1 - core_i)]).astype(jnp.int32)
    other_pos = jnp.stack([_pos_of_dev(2 * jnp.bitwise_xor(chip_i, q) + cc) for q in (1, 2, 3) for cc in (0, 1)]).astype(jnp.int32)
    proj, h, h_t = _proj_forward_own_chip(own_pos, x2d, norm_mix_g, shards[0], w_in_sibling)
    proj = _proj_forward_other_chips(other_pos, proj, h, w_in_g)
    bias_b = jnp.broadcast_to(gmlp_b_s[0][:, :, None], (GROUPS, GMLP_CHUNK, GMLP_CHUNK))
    a = _gmlp_forward(proj, gmlp_ln_g, gmlp_ln_b, gmlp_w_s[0], bias_b)
    og, states = _hgrn_forward(proj, hgrn_lb_table, hgrn_norm_g)
    ya, yb, merged, x1, h2 = _branch_out_forward(a, og, proj, x2d, w_a, w_b, w_o, norm_ffn_g)
    gu, act, loss_tile, d_final_g, dx2, dx2b = _ffn_forward(h2, x1, w_gu, w_dn, target, final_g)

    core = lax.axis_index("c").astype(jnp.int32).reshape(1)
    chip = (2 * lax.axis_index("x") + lax.axis_index("y")).astype(jnp.int32).reshape(1)
    branch_rows, branch_shape = rows_of(BRANCH_ROWS), (BRANCH_ROWS, D_MODEL)
    branch_block = ((BRANCH_ROWS, D_MODEL), lambda q, r, c: (2 * q + c, 0))

    def chip_partials(names, grads, land, own_blocks):
        return [_chip_partial("chip_partial_" + nme, core, g_, blk, idx, l_)
                for nme, g_, (blk, idx), l_ in zip(names, grads, own_blocks, land)]

    whole = pl.BlockSpec((t, D_MODEL), lambda n: (0, 0))
    whole_t = pl.BlockSpec((D_MODEL, t), lambda n: (0, 0))
    col_blocks = [((t, D_MODEL), BF16), ((t, 256), BF16), ((D_MODEL, 256), F32)]

    def square_grad(name, a_, b_):
        return _weight_grad(name, a_, b_, whole, pl.BlockSpec((t, 256), lambda n: (0, n)), (D_MODEL, D_MODEL),
                            pl.BlockSpec((D_MODEL, 256), lambda n: (0, n)), D_MODEL // 256, col_blocks, False)

    dgu, dx1, dx1b, d_ffn_g = _ffn_backward(dx2b, dx2, gu, x1, w_gu, w_dn, norm_ffn_g)
    g_gu = _weight_grad(
        "grad_w_gate_up", dgu, h2, pl.BlockSpec((None, None, t, FF_BLOCK), lambda j: (j % 4, j // 4, 0, 0)), whole,
        (N_DEV, FF_BLOCK, D_MODEL), pl.BlockSpec((None, FF_BLOCK, D_MODEL), lambda j: (j, 0, 0)), N_DEV,
        [((t, 768), BF16), ((t, D_MODEL), BF16), ((FF_BLOCK, D_MODEL), F32)], False)
    g_dn = _weight_grad(
        "grad_w_down", act, dx2b, pl.BlockSpec((None, t, FF_BLOCK), lambda j: (j, 0, 0)), whole, (D_FF, D_MODEL),
        pl.BlockSpec((FF_BLOCK, D_MODEL), lambda j: (j, 0)), 4,
        [((t, 768), BF16), ((t, D_MODEL), BF16), ((FF_BLOCK, D_MODEL), F32)], False)
    names_f, grads_f = ["w_gate_up", "w_down"], [g_gu, g_dn]
    land_f = _exchange_sibling("ffn_grads_to_sibling", 2, grads_f, [lambda ref, j: ref.at[j], rows_of(DOWN_ROWS)],
                               [(FF_BLOCK, D_MODEL), (DOWN_ROWS, D_MODEL)])

    dx1b_later, _ = lax.optimization_barrier((dx1b, grads_f))
    dya, dyb, dproj, da, dog = _branch_out_backward(dx1b_later, ya, yb, proj, w_a, w_b, w_o)
    g_a = square_grad("grad_w_a", a, dya)
    g_b = square_grad("grad_w_b", og, dyb)
    g_o = square_grad("grad_w_out", merged, dx1b)
    names_b, grads_b = ["w_branch_a", "w_branch_b", "w_out"], [g_a, g_b, g_o]
    land_b = _exchange_sibling("branch_grads_to_sibling", 3, grads_b, [branch_rows] * 3, [branch_shape] * 3)

    part_f = chip_partials(names_f, grads_f, land_f,
                           [((None, FF_BLOCK, D_MODEL), lambda q, r, c: (2 * q + c, 0, 0)),
                            ((DOWN_ROWS, D_MODEL), lambda q, r, c: (2 * q + c, 0))])
    landed_f = _exchange_chips("ffn_grads_to_chips", 5, part_f)

    dog, _ = lax.optimization_barrier((dog, part_f))
    dproj, d_hg_norm, d_lb = _hgrn_backward(dproj, dog, states, proj, hgrn_lb_table, hgrn_norm_g)

    land_b, _ = lax.optimization_barrier((land_b, part_f))
    part_b = chip_partials(names_b, grads_b, land_b, [branch_block] * 3)
    landed_b = _exchange_chips("branch_grads_to_chips", 6, part_b)

    da, _ = lax.optimization_barrier((da, part_b))
    dproj, d_ln_g, d_ln_b, d_ws, d_bs = _gmlp_backward(dproj, da, proj, gmlp_ln_g, gmlp_ln_b, gmlp_w_s[0], bias_b)

    def packed(vals):
        return _pack_small(*vals)

    def flat_ws(a):
        return a.reshape(GROUPS * GMLP_CHUNK, GMLP_CHUNK)

    no_row = jnp.zeros((1, D_MODEL), F32)
    w_pack = packed([norm_mix_g, gmlp_ln_g, gmlp_ln_b, gmlp_b_s, hgrn_lb_table, hgrn_norm_g, norm_ffn_g, norm_final_g, no_row])
    m_pack = packed([m_norm_mix_g, m_gmlp_ln_g, m_gmlp_ln_b, m_gmlp_b_s, m_hgrn_lb_table, m_hgrn_norm_g, m_norm_ffn_g, m_norm_final_g, no_row])
    v_pack = packed([v_norm_mix_g, v_gmlp_ln_g, v_gmlp_ln_b, v_gmlp_b_s, v_hgrn_lb_table, v_hgrn_norm_g, v_norm_ffn_g, v_norm_final_g, no_row])
    small_partial = _pack_small(no_row, d_ln_g, d_ln_b, d_bs[:, :, 0], d_lb, d_hg_norm, d_ffn_g, d_final_g,
                                jnp.tile(loss_tile[0:1], (1, D_MODEL // 128)))
    small_all, ws_all = _all_gather_async(
        "small_grads_all_gather", 1, [small_partial, flat_ws(d_ws)],
        [((N_DEV, SMALL_ROWS, D_MODEL), F32), ((N_DEV, GROUPS * GMLP_CHUNK, GMLP_CHUNK), F32)],
        [lambda ref, j: ref.at[j], lambda ref, j: ref.at[j]])

    g_in = _weight_grad(
        "grad_w_in", h_t, dproj, whole_t, pl.BlockSpec((None, t, D_MODEL), lambda p: (p, 0, 0)), (N_DEV, D_MODEL, D_MODEL),
        pl.BlockSpec((None, D_MODEL, D_MODEL), lambda p: (p, 0, 0)), N_DEV,
        [((D_MODEL, t), BF16), ((t, D_MODEL), BF16), ((D_MODEL, D_MODEL), F32)], True)
    land_i = _exchange_sibling("w_in_grads_to_sibling", 4, [g_in], [lambda ref, j: ref.at[_pos_of_dev(j)]],
                               [(D_MODEL, D_MODEL)])

    big = {}
    for nme, own, lnd, w, m, v in zip(
            names_f + names_b, part_f + part_b, landed_f + landed_b,
            [w_gate_up, w_down, w_branch_a, w_branch_b, w_out], [m_w_gate_up, m_w_down, m_w_branch_a, m_w_branch_b, m_w_out],
            [v_w_gate_up, v_w_down, v_w_branch_a, v_w_branch_b, v_w_out]):
        flip = (lambda z: z.T) if nme == "w_gate_up" else (lambda z: z)
        big[nme] = [flip(o_)[None] for o_ in _adamw("adamw_" + nme, chip, own, lnd, flip(w[0]), flip(m[0]), flip(v[0]))]
    small, loss_row = _adamw_small_unpacked(small_all, w_pack, m_pack, v_pack)
    ws_outs = _adamw_small("adamw_w_s", ws_all, flat_ws(gmlp_w_s), flat_ws(m_gmlp_w_s), flat_ws(v_gmlp_w_s))
    land_i, _ = lax.optimization_barrier((land_i, (big, small, ws_outs)))
    part_i = chip_partials(["w_in"], [g_in], land_i,
                           [((None, D_MODEL, D_MODEL), lambda q, r, c: (_pos_of_dev(2 * q + c), 0, 0))])
    landed_i = _exchange_chips("w_in_grads_to_chips", 7, part_i)

    dx1, _ = lax.optimization_barrier((dx1, part_i))
    grad_x, d_mix_g = _input_backward(dproj, w_in_g, x2d, dx1, norm_mix_g)
    big["w_in"] = [o_[None] for o_ in _adamw("adamw_w_in", chip, part_i[0], landed_i[0], w_in[0], m_w_in[0], v_w_in[0])]

    def row8(a):
        return jnp.pad(a, ((0, 7), (0, 0)))

    d_mix_g, _ = lax.optimization_barrier((d_mix_g, landed_i))
    (mix_all,) = _all_gather_async("mix_gain_grad_all_gather", 8, [row8(d_mix_g)], [((N_DEV, 8, D_MODEL), F32)],
                                   [lambda ref, j: ref.at[j]])
    mix_outs = _adamw_row("adamw_mix_gain", mix_all, norm_mix_g, m_norm_mix_g, v_norm_mix_g)
    small = [dict(p, norm_final_g=p["norm_final_g"][0], gmlp_w_s=ws.reshape(1, GROUPS, GMLP_CHUNK, GMLP_CHUNK), norm_mix_g=q)
             for p, ws, q in zip(small, ws_outs, mix_outs)]

    loss = loss_row[0, 0]
    order = ["norm_mix_g", "w_in", "gmlp_ln_g", "gmlp_ln_b", "gmlp_w_s", "gmlp_b_s", "hgrn_lb_table", "hgrn_norm_g",
             "w_branch_a", "w_branch_b", "w_out", "norm_ffn_g", "w_gate_up", "w_down", "norm_final_g"]
    outs = [loss, grad_x.reshape(1, t, D_MODEL)]
    for kind in range(4):
        for nme in order:
            outs.append(big[nme][kind] if nme in big else small[kind][nme])
    return tuple(outs)
```

```python
import functools

import jax
import jax.numpy as jnp
from jax import lax
from jax.experimental import pallas as pl
from jax.experimental.pallas import tpu as pltpu
from jax.experimental.pallas import tpu_sc as plsc

F32, BF16 = jnp.float32, jnp.bfloat16
D_MODEL = 1024
N_DEV = 8
HEADS = 8
HEAD_DIM = 128
GROUPS = 8
GMLP_CHUNK = 128
HGRN_CHUNK = 64
HGRN_SCALE = HEAD_DIM ** -0.5
D_FF = 2816
FF_BLOCK = D_FF // 4
DOWN_ROWS = D_FF // N_DEV
BRANCH_ROWS = D_MODEL // N_DEV
NORM_EPS = 1e-6
ADAM_LR, ADAM_B1, ADAM_B2, ADAM_EPS, ADAM_WD, ADAM_STEP = 0.001, 0.9, 0.999, 1e-08, 0.01, 10
SMALL_ROWS = 72
V7X_VMEM_BYTES = 64 * 1024 * 1024
VMEM_CAP = V7X_VMEM_BYTES - 6 * 1024 * 1024
MESH_ID = pl.DeviceIdType.MESH
ANY = pl.BlockSpec(memory_space=pl.ANY)
RESIDENT = pl.BlockSpec(memory_space=pltpu.VMEM)
Q_POS, U_POS, GATE_POS = 0, 4, 6


def _pos_of_dev(j):
    return jnp.where(j < 2, j + 4, jnp.where(j < 6, j - 2, j))


def _dev_of_pos(p):
    return jnp.where(p < 4, p + 2, jnp.where(p < 6, p - 4, p))


def _nbytes(shape, dtype):
    n = 1
    for s in shape:
        n *= s
    return n * jnp.dtype(dtype).itemsize


def _params(blocks, scratch=(), temps=0, sem=None):
    need = 2 * sum(_nbytes(s, d) for s, d in blocks) + sum(_nbytes(s, d) for s, d in scratch) + temps
    assert need + (4 << 20) <= VMEM_CAP, need
    return pltpu.CompilerParams(dimension_semantics=sem, vmem_limit_bytes=VMEM_CAP)


def _tile(n, pref):
    return pref if n % pref == 0 else n


def _dot(a, b):
    return jnp.dot(a, b, preferred_element_type=F32)


def _dot_nt(a, b):
    return lax.dot_general(a, b, (((1,), (1,)), ((), ())), preferred_element_type=F32)


def _dot_tn(a, b):
    return lax.dot_general(a, b, (((0,), (0,)), ((), ())), preferred_element_type=F32)


def _sigmoid(x):
    return 1.0 / (1.0 + jnp.exp(-x))


_GELU_C = 0.7978845608028654


def _gelu(x):
    return x * (0.5 * (1.0 + jnp.tanh(_GELU_C * (x + 0.044715 * (x * x * x)))))


def _gelu_and_grad(x):
    t = jnp.tanh(_GELU_C * (x + 0.044715 * (x * x * x)))
    half = 0.5 * (1.0 + t)
    return x * half, half + 0.5 * x * (1.0 - t * t) * (_GELU_C * (1.0 + 3.0 * 0.044715 * x * x))


def _rms_stats(x):
    r = lax.rsqrt(jnp.mean(x * x, axis=-1, keepdims=True) + NORM_EPS)
    return r, x * r


def _rms_bwd(dy, x, g):
    r, xh = _rms_stats(x)
    dg = jnp.sum(dy * xh, axis=0, keepdims=True)
    dxh = dy * g
    dx = r * (dxh - xh * jnp.mean(dxh * xh, axis=-1, keepdims=True))
    return dx, dg


def _split3(x):
    hi = x.astype(BF16)
    r = x - hi.astype(F32)
    mid = r.astype(BF16)
    lo = (r - mid.astype(F32)).astype(BF16)
    return hi, mid, lo


def _mask_mm(mask_bf16, x):
    hi, mid, lo = _split3(x)
    return _dot(mask_bf16, hi) + _dot(mask_bf16, mid) + _dot(mask_bf16, lo)


def _place():
    return lax.axis_index("x"), lax.axis_index("y"), lax.axis_index("c")


def _gather_copies(src, out, send, recv, loc, slicers):
    n = len(src)
    x, y, c = _place()
    me, sib = (x, y, c), (x, y, 1 - c)
    chips = [(1 - x, y), (x, 1 - y), (1 - x, 1 - y)]

    def dev(p):
        return 4 * p[0] + 2 * p[1] + p[2]

    def rc(i, k, block, to, from_src=False):
        dst = slicers[i](out[i], dev(block))
        return pltpu.make_async_remote_copy(
            src_ref=src[i] if from_src else dst, dst_ref=dst, send_sem=send.at[7 * i + k],
            recv_sem=recv.at[7 * i + k], device_id=to, device_id_type=MESH_ID)

    mine = [pltpu.make_async_copy(src[i], slicers[i](out[i], dev(me)), loc.at[i]) for i in range(n)]
    for cp in mine:
        cp.start()
    first = []
    for i in range(n):
        first.append(rc(i, 0, me, sib, True))
        for j, chip in enumerate(chips):
            first.append(rc(i, 1 + j, me, (*chip, c), True))
    for cp in first:
        cp.start()
    passed = []
    for j, chip in enumerate(chips):
        for i in range(n):
            rc(i, 1 + j, (*chip, c), me).wait_recv()
            cp = rc(i, 4 + j, (*chip, c), sib)
            cp.start()
            passed.append(cp)
    for i in range(n):
        rc(i, 0, sib, me).wait_recv()
        for j, chip in enumerate(chips):
            rc(i, 4 + j, (*chip, 1 - c), me).wait_recv()
    for cp in first + passed:
        cp.wait_send()
    for cp in mine:
        cp.wait()


def _gather_copies_balanced(src, out, send, recv, loc, slicer, rows):
    x, y, c = _place()
    me, sib = (x, y, c), (x, y, 1 - c)
    xn, yn, dg = (1 - x, y), (x, 1 - y), (1 - x, 1 - y)
    half_rows = rows // 2

    def block(p):
        return slicer(out, 4 * p[0] + 2 * p[1] + p[2])

    def half(ref, h):
        return ref.at[pl.ds(h * half_rows, half_rows)]

    def rc(k, dst, to, from_src=False):
        return pltpu.make_async_remote_copy(src_ref=src if from_src else dst, dst_ref=dst, send_sem=send.at[k],
                                            recv_sem=recv.at[k], device_id=to, device_id_type=MESH_ID)

    mine = pltpu.make_async_copy(src, block(me), loc.at[0])
    mine.start()
    sends = [rc(0, block(me), sib, True), rc(1, block(me), (*xn, c), True), rc(2, block(me), (*yn, c), True)]
    for cp in sends:
        cp.start()

    def then(cp):
        cp.start()
        sends.append(cp)

    rc(1, block((*xn, c)), me).wait_recv()
    then(rc(3, half(block((*xn, c)), 0), (*yn, c)))
    then(rc(5, block((*xn, c)), sib))
    rc(2, block((*yn, c)), me).wait_recv()
    then(rc(4, half(block((*yn, c)), 1), (*xn, c)))
    then(rc(6, block((*yn, c)), sib))
    rc(3, half(block((*dg, c)), 0), me).wait_recv()
    then(rc(7, half(block((*dg, c)), 0), sib))
    rc(4, half(block((*dg, c)), 1), me).wait_recv()
    then(rc(8, half(block((*dg, c)), 1), sib))
    rc(0, block(sib), me).wait_recv()
    rc(5, block((*xn, 1 - c)), me).wait_recv()
    rc(6, block((*yn, 1 - c)), me).wait_recv()
    rc(7, half(block((*dg, 1 - c)), 0), me).wait_recv()
    rc(8, half(block((*dg, 1 - c)), 1), me).wait_recv()
    for cp in sends:
        cp.wait_send()
    mine.wait()


def _gather_scratch(n):
    return [pltpu.SemaphoreType.DMA((7 * n,)), pltpu.SemaphoreType.DMA((7 * n,)), pltpu.SemaphoreType.DMA((n,))]


def _handshake(peers):
    barrier = pltpu.get_barrier_semaphore()
    for peer in peers:
        pl.semaphore_signal(barrier, inc=1, device_id=peer, device_id_type=MESH_ID)
    pl.semaphore_wait(barrier, len(peers))


def _all_gather_async(name, collective_id, srcs, out_shapes, slicers):
    n = len(srcs)

    def body(*refs):
        x, y, c = _place()
        _handshake([(1 - x if dx else x, 1 - y if dy else y, 1 - c if dc else c)
                    for dx in (0, 1) for dy in (0, 1) for dc in (0, 1) if dx or dy or dc])
        _gather_copies(refs[:n], refs[n:2 * n], *refs[2 * n:], slicers)

    return _sequencer_call(name, collective_id, body, srcs, [jax.ShapeDtypeStruct(s, d) for s, d in out_shapes],
                           _gather_scratch(n))


def _all_gather_balanced_async(name, collective_id, src, out_shape, slicer, rows):
    def body(src_ref, out_ref, send, recv, loc):
        x, y, c = _place()
        _handshake([(1 - x if dx else x, 1 - y if dy else y, 1 - c if dc else c)
                    for dx in (0, 1) for dy in (0, 1) for dc in (0, 1) if dx or dy or dc])
        _gather_copies_balanced(src_ref, out_ref, send, recv, loc, slicer, rows)

    return _sequencer_call(name, collective_id, body, [src], [jax.ShapeDtypeStruct(*out_shape)],
                           [pltpu.SemaphoreType.DMA((9,)), pltpu.SemaphoreType.DMA((9,)), pltpu.SemaphoreType.DMA((1,))])[0]


def _sequencer_call(name, collective_id, body, operands, out_types, scratch):
    return pl.kernel(
        body, out_type=out_types, mesh=plsc.ScalarSubcoreMesh(axis_name="sequencer", num_cores=1), name=name,
        scratch_types=scratch, compiler_params=pltpu.CompilerParams(collective_id=collective_id),
    )(*operands)


def _exchange_sibling(name, collective_id, grads, shard_fns, shard_shapes):
    n = len(grads)

    def body(*refs):
        g, land = refs[:n], refs[n:2 * n]
        send, recv = refs[2 * n:]
        x, y, c = _place()
        _handshake([(x, y, 1 - c)])
        remote = []
        for i in range(n):
            for q in range(4):
                cp = pltpu.make_async_remote_copy(
                    src_ref=shard_fns[i](g[i], 2 * q + (1 - c)), dst_ref=land[i].at[q], send_sem=send.at[4 * i + q],
                    recv_sem=recv.at[4 * i + q], device_id=(x, y, 1 - c), device_id_type=MESH_ID)
                cp.start()
                remote.append(cp)
        for cp in remote:
            cp.wait()

    return _sequencer_call(name, collective_id, body, grads, [jax.ShapeDtypeStruct((4, *s), F32) for s in shard_shapes],
                           [pltpu.SemaphoreType.DMA((4 * n,)), pltpu.SemaphoreType.DMA((4 * n,))])


def _exchange_chips(name, collective_id, parts):
    n = len(parts)

    def body(*refs):
        part, out = refs[:n], refs[n:2 * n]
        send, recv = refs[2 * n:]
        x, y, c = _place()
        _handshake([(1 - x, y, c), (x, 1 - y, c), (1 - x, 1 - y, c)])
        remote = []
        for i in range(n):
            for s in range(3):
                qx = 1 - x if (s + 1) // 2 else x
                qy = 1 - y if (s + 1) % 2 else y
                cp = pltpu.make_async_remote_copy(
                    src_ref=part[i].at[2 * qx + qy], dst_ref=out[i].at[s], send_sem=send.at[3 * i + s],
                    recv_sem=recv.at[3 * i + s], device_id=(qx, qy, c), device_id_type=MESH_ID)
                cp.start()
                remote.append(cp)
        for cp in remote:
            cp.wait()

    return _sequencer_call(name, collective_id, body, parts,
                           [jax.ShapeDtypeStruct((3, *p.shape[1:]), p.dtype) for p in parts],
                           [pltpu.SemaphoreType.DMA((3 * n,)), pltpu.SemaphoreType.DMA((3 * n,))])


def _chip_partial(name, core, grad, own_block, own_index, land):
    _, rows, cols = land.shape
    tr = own_block[-2]

    def body(core_ref, a_ref, b_ref, o_ref):
        o_ref[...] = (a_ref[...] + b_ref[...]).astype(BF16)

    spec = pl.BlockSpec((None, tr, cols), lambda q, r, c: (q, r, 0))
    return pl.pallas_call(
        body, name=name, out_shape=jax.ShapeDtypeStruct(land.shape, BF16),
        grid_spec=pltpu.PrefetchScalarGridSpec(
            num_scalar_prefetch=1, grid=(4, rows // tr),
            in_specs=[pl.BlockSpec(own_block, lambda q, r, c: own_index(q, r, c[0])), spec], out_specs=spec),
        compiler_params=_params([((tr, cols), F32)] * 2 + [((tr, cols), BF16)], sem=("arbitrary", "arbitrary")),
    )(core, grad, land)


def _adamw_math(w, g, m, v):
    m = ADAM_B1 * m + (1.0 - ADAM_B1) * g
    v = ADAM_B2 * v + (1.0 - ADAM_B2) * (g * g)
    m_hat = m / (1.0 - ADAM_B1 ** ADAM_STEP)
    v_hat = v / (1.0 - ADAM_B2 ** ADAM_STEP)
    delta = -ADAM_LR * (m_hat / (jnp.sqrt(v_hat) + ADAM_EPS) + ADAM_WD * w)
    return delta, m, v


def _adamw(name, chip, own, landed, w, m, v):
    _, rows, cols = own.shape
    tr = _tile(rows, 512) if rows % 512 == 0 else _tile(rows, 176)

    def body(chip_ref, own_ref, l_ref, w_ref, m_ref, v_ref, g_out, d_out, m_out, v_out):
        g = own_ref[...].astype(F32)
        for s in range(3):
            g = g + l_ref[s].astype(F32)
        delta, m_new, v_new = _adamw_math(w_ref[...], g, m_ref[...], v_ref[...])
        g_out[...] = g
        d_out[...] = delta
        m_out[...] = m_new
        v_out[...] = v_new

    spec = pl.BlockSpec((tr, cols), lambda r, c: (r, 0))
    return pl.pallas_call(
        body, name=name, out_shape=[jax.ShapeDtypeStruct((rows, cols), F32)] * 4,
        grid_spec=pltpu.PrefetchScalarGridSpec(
            num_scalar_prefetch=1, grid=(rows // tr,),
            in_specs=[pl.BlockSpec((None, tr, cols), lambda r, c: (c[0], r, 0)),
                      pl.BlockSpec((3, tr, cols), lambda r, c: (0, r, 0)), spec, spec, spec],
            out_specs=[spec] * 4),
        compiler_params=_params([((4, tr, cols), own.dtype)] + [((tr, cols), F32)] * 7, sem=("arbitrary",)),
    )(chip, own, landed, w, m, v)


def _swap_with_sibling(name, x):
    def body(x_ref, o_ref, send, recv):
        px, py, c = _place()
        cp = pltpu.make_async_remote_copy(src_ref=x_ref, dst_ref=o_ref, send_sem=send, recv_sem=recv,
                                          device_id=(px, py, 1 - c), device_id_type=MESH_ID)
        cp.start()
        cp.wait()

    return pl.pallas_call(
        body, name=name, out_shape=jax.ShapeDtypeStruct(x.shape, x.dtype), in_specs=[ANY], out_specs=ANY,
        scratch_shapes=[pltpu.SemaphoreType.DMA, pltpu.SemaphoreType.DMA],
    )(x)


def _proj_forward_own_chip(positions, x, gain, w_own, w_sibling):
    t = x.shape[0]
    tm = _tile(t, 1024)

    def body(pos_ref, x_ref, g_ref, wo_ref, ws_ref, o_ref, h_ref, ht_ref):
        @pl.when(pl.program_id(1) == 0)
        def _():
            _, xh = _rms_stats(x_ref[...])
            h = (xh * g_ref[...]).astype(BF16)
            h_ref[...] = h
            ht_ref[...] = h.T
            o_ref[...] = _dot(h, wo_ref[...])

        @pl.when(pl.program_id(1) == 1)
        def _():
            o_ref[...] = _dot(h_ref[...], ws_ref[...])

    tok = pl.BlockSpec((tm, D_MODEL), lambda m, k, pos: (m, 0))
    return pl.pallas_call(
        body, name="proj_fwd_own_chip",
        out_shape=[jax.ShapeDtypeStruct((N_DEV, t, D_MODEL), F32), jax.ShapeDtypeStruct((t, D_MODEL), BF16),
                   jax.ShapeDtypeStruct((D_MODEL, t), BF16)],
        grid_spec=pltpu.PrefetchScalarGridSpec(
            num_scalar_prefetch=1, grid=(t // tm, 2),
            in_specs=[tok, pl.BlockSpec((1, D_MODEL), lambda m, k, pos: (0, 0)), RESIDENT, RESIDENT],
            out_specs=[pl.BlockSpec((None, tm, D_MODEL), lambda m, k, pos: (pos[k], m, 0)), tok,
                       pl.BlockSpec((D_MODEL, tm), lambda m, k, pos: (0, m))]),
        compiler_params=_params([((tm, D_MODEL), F32)] * 2 + [((tm, D_MODEL), BF16)] * 2,
                                scratch=[((2, D_MODEL, D_MODEL), BF16)], temps=6 << 20, sem=("arbitrary", "arbitrary")),
    )(positions, x, gain, w_own, w_sibling)


def _proj_forward_other_chips(positions, proj, h, w_in_g):
    t = h.shape[0]
    tm = _tile(t, 2048)

    def body(pos_ref, _, h_ref, w_ref, o_ref):
        o_ref[...] = _dot(h_ref[...], w_ref[pos_ref[pl.program_id(1)]])

    return pl.pallas_call(
        body, name="proj_fwd_other_chips", out_shape=jax.ShapeDtypeStruct(proj.shape, F32),
        grid_spec=pltpu.PrefetchScalarGridSpec(
            num_scalar_prefetch=1, grid=(t // tm, N_DEV - 2),
            in_specs=[ANY, pl.BlockSpec((tm, D_MODEL), lambda m, k, pos: (m, 0)), RESIDENT],
            out_specs=pl.BlockSpec((None, tm, D_MODEL), lambda m, k, pos: (pos[k], m, 0))),
        input_output_aliases={1: 0},
        compiler_params=_params([((tm, D_MODEL), F32), ((tm, D_MODEL), BF16)], scratch=[((N_DEV, D_MODEL, D_MODEL), BF16)],
                                temps=6 << 20, sem=("arbitrary", "arbitrary")),
    )(positions, proj, h, w_in_g)


def _masked_ws(ws_ref, g):
    row = lax.broadcasted_iota(jnp.int32, (GMLP_CHUNK, GMLP_CHUNK), 0)
    col = lax.broadcasted_iota(jnp.int32, (GMLP_CHUNK, GMLP_CHUNK), 1)
    return jnp.where(row >= col, ws_ref[g], 0.0).astype(BF16)


def _gmlp_forward(proj, ln_g, ln_b, w_s, bias_b):
    t = proj.shape[1]
    tm = _tile(t, 512)
    chunks = tm // GMLP_CHUNK

    def body(u_ref, v_ref, lng_ref, lnb_ref, ws_ref, bias_ref, a_ref, vn_scr):
        vv = _gelu(v_ref[...])
        mu = jnp.mean(vv, axis=-1, keepdims=True)
        cen = vv - mu
        var = jnp.mean(cen * cen, axis=-1, keepdims=True)
        vn_scr[...] = ((cen * lax.rsqrt(var + NORM_EPS)) * lng_ref[...] + lnb_ref[...]).astype(BF16)
        for g in range(GROUPS):
            wm = _masked_ws(ws_ref, g)
            cols = slice(g * HEAD_DIM, (g + 1) * HEAD_DIM)
            for c in range(chunks):
                rows = slice(c * GMLP_CHUNK, (c + 1) * GMLP_CHUNK)
                mixed = _dot(wm, vn_scr[rows, cols]) + bias_ref[g]
                a_ref[rows, cols] = (_gelu(u_ref[rows, cols]) * mixed).astype(BF16)

    small = pl.BlockSpec((GROUPS, GMLP_CHUNK, GMLP_CHUNK), lambda m: (0, 0, 0))
    vec = pl.BlockSpec((1, D_MODEL), lambda m: (0, 0))
    return pl.pallas_call(
        body, name="gmlp_fwd", out_shape=jax.ShapeDtypeStruct((t, D_MODEL), BF16), grid=(t // tm,),
        in_specs=[pl.BlockSpec((None, tm, D_MODEL), lambda m: (U_POS, m, 0)),
                  pl.BlockSpec((None, tm, D_MODEL), lambda m: (U_POS + 1, m, 0)), vec, vec, small, small],
        out_specs=pl.BlockSpec((tm, D_MODEL), lambda m: (m, 0)),
        scratch_shapes=[pltpu.VMEM((tm, D_MODEL), BF16)],
        compiler_params=_params([((tm, D_MODEL), F32)] * 2 + [((tm, D_MODEL), BF16)] + [((8, 128, 128), F32)] * 2,
                                scratch=[((tm, D_MODEL), BF16)], temps=8 << 20, sem=("arbitrary",)),
    )(proj, proj, ln_g, ln_b, w_s, bias_b)


def _lower_bound(tab_ref):
    t0, t1 = tab_ref[0:1, :], tab_ref[1:2, :]
    mx = jnp.maximum(t0, t1)
    e0, e1 = jnp.exp(t0 - mx), jnp.exp(t1 - mx)
    return e0 / (e0 + e1)


def _tri_masks():
    row = lax.broadcasted_iota(jnp.int32, (HGRN_CHUNK, HGRN_CHUNK), 0)
    col = lax.broadcasted_iota(jnp.int32, (HGRN_CHUNK, HGRN_CHUNK), 1)
    return row >= col, row <= col


def _chunk_rows(c):
    return slice(c * HGRN_CHUNK, (c + 1) * HGRN_CHUNK)


def _per_chunk(x, nc, fn):
    return jnp.concatenate([fn(x[_chunk_rows(c)]) for c in range(nc)], axis=0)


def _chunk_row_bcast(x, nc, i):
    return _per_chunk(x, nc, lambda xc: jnp.broadcast_to(xc[i:i + 1, :], (HGRN_CHUNK, HEAD_DIM)))


def _hgrn_gates(q, fl, lb, nc):
    lower, _ = _tri_masks()
    lower = lower.astype(BF16)
    s = _sigmoid(fl)
    f = lb + (1.0 - lb) * s
    k = 1.0 - f
    hi, mid, lo = _split3(jnp.log(f))
    a = jnp.concatenate([_dot(lower, hi[_chunk_rows(c)]) + _dot(lower, mid[_chunk_rows(c)]) + _dot(lower, lo[_chunk_rows(c)])
                         for c in range(nc)], axis=0)
    a_mid = _chunk_row_bcast(a, nc, HGRN_CHUNK // 2 - 1)
    a_last = _chunk_row_bcast(a, nc, HGRN_CHUNK - 1)
    qs = q * HGRN_SCALE
    e_in, e_out, e_end, e_all = jnp.exp(a - a_mid), jnp.exp(a_mid - a), jnp.exp(a_last - a), jnp.exp(a)
    decay = [jnp.exp(a[c * HGRN_CHUNK + HGRN_CHUNK - 1:(c + 1) * HGRN_CHUNK, :]) for c in range(nc)]
    return dict(s=s, f=f, k=k, decay=decay, e_in=e_in, e_out=e_out, e_end=e_end, e_all=e_all,
                qi=qs * e_in, ki=k * e_out, kd=k * e_end, qe=qs * e_all)


def _hgrn_forward(proj, lb_table, norm_g):
    t = proj.shape[1]
    tb = _tile(t, 1024)
    nc = tb // HGRN_CHUNK
    n_chunks = t // HGRN_CHUNK

    def body(q_ref, f_ref, i_ref, g_ref, tab_ref, ng_ref, og_ref, o_ref, st_ref, state):
        @pl.when(pl.program_id(1) == 0)
        def _():
            state[...] = jnp.zeros_like(state)

        lower, _ = _tri_masks()
        gt = _hgrn_gates(q_ref[...], f_ref[...], _lower_bound(tab_ref), nc)
        qi, ki, kd, qe = (gt[n].astype(BF16) for n in ("qi", "ki", "kd", "qe"))
        vb = i_ref[...].astype(BF16)
        o_intra, d_state = [], []
        for c in range(nc):
            rows = _chunk_rows(c)
            p = jnp.where(lower, _dot_nt(qi[rows], ki[rows]), 0.0).astype(BF16)
            o_intra.append(_dot(p, vb[rows]))
            d_state.append(_dot_tn(vb[rows], kd[rows]))
        st = state[...]
        outs = []
        for c in range(nc):
            st_ref[c] = st
            outs.append(o_intra[c] + _dot_nt(qe[_chunk_rows(c)], st.astype(BF16)))
            st = st * gt["decay"][c] + d_state[c]
        state[...] = st
        o = jnp.concatenate(outs, axis=0)
        o_ref[...] = o
        _, oh = _rms_stats(o)
        gz = g_ref[...]
        og_ref[...] = ((oh * ng_ref[...]) * (gz * _sigmoid(gz))).astype(BF16)

    def blk(p):
        return pl.BlockSpec((None, tb, HEAD_DIM), lambda h, n: (p, n, h))

    out_blk = pl.BlockSpec((tb, HEAD_DIM), lambda h, n: (n, h))
    return pl.pallas_call(
        body, name="hgrn_fwd",
        out_shape=[jax.ShapeDtypeStruct((t, D_MODEL), BF16), jax.ShapeDtypeStruct((t, D_MODEL), F32),
                   jax.ShapeDtypeStruct((HEADS, n_chunks, HEAD_DIM, HEAD_DIM), F32)],
        grid=(HEADS, t // tb),
        in_specs=[blk(Q_POS), blk(Q_POS + 1), blk(Q_POS + 2), blk(Q_POS + 3),
                  pl.BlockSpec((2, HEAD_DIM), lambda h, n: (0, h)), pl.BlockSpec((1, HEAD_DIM), lambda h, n: (0, h))],
        out_specs=[out_blk, out_blk, pl.BlockSpec((None, nc, HEAD_DIM, HEAD_DIM), lambda h, n: (h, n, 0, 0))],
        scratch_shapes=[pltpu.VMEM((HEAD_DIM, HEAD_DIM), F32)],
        compiler_params=_params([((tb, HEAD_DIM), F32)] * 6 + [((nc, HEAD_DIM, HEAD_DIM), F32)], temps=8 << 20,
                                sem=("arbitrary", "arbitrary")),
    )(proj, proj, proj, proj, lb_table, norm_g)


def _branch_out_forward(a, og, proj, x, w_a, w_b, w_out, ffn_g):
    t = x.shape[0]
    tm = _tile(t, 512)

    def body(a_ref, og_ref, ga_ref, gb_ref, x_ref, wa_ref, wb_ref, wo_ref, g_ref, ya_ref, yb_ref, mg_ref, x1_ref, h2_ref):
        ya = _dot(a_ref[...], wa_ref[...])
        yb = _dot(og_ref[...], wb_ref[...])
        ya_ref[...] = ya
        yb_ref[...] = yb
        merged = (_sigmoid(ga_ref[...]) * ya + _sigmoid(gb_ref[...]) * yb).astype(BF16)
        mg_ref[...] = merged
        x1 = x_ref[...] + _dot(merged, wo_ref[...])
        x1_ref[...] = x1
        _, xh = _rms_stats(x1)
        h2_ref[...] = (xh * g_ref[...]).astype(BF16)

    tok = pl.BlockSpec((tm, D_MODEL), lambda m: (m, 0))
    return pl.pallas_call(
        body, name="branch_out_fwd",
        out_shape=[jax.ShapeDtypeStruct((t, D_MODEL), F32), jax.ShapeDtypeStruct((t, D_MODEL), F32),
                   jax.ShapeDtypeStruct((t, D_MODEL), BF16), jax.ShapeDtypeStruct((t, D_MODEL), F32),
                   jax.ShapeDtypeStruct((t, D_MODEL), BF16)],
        grid=(t // tm,),
        in_specs=[tok, tok, pl.BlockSpec((None, tm, D_MODEL), lambda m: (GATE_POS, m, 0)),
                  pl.BlockSpec((None, tm, D_MODEL), lambda m: (GATE_POS + 1, m, 0)), tok, RESIDENT, RESIDENT, RESIDENT,
                  pl.BlockSpec((1, D_MODEL), lambda m: (0, 0))],
        out_specs=[tok] * 5,
        compiler_params=_params([((tm, D_MODEL), BF16)] * 4 + [((tm, D_MODEL), F32)] * 6, scratch=[((D_MODEL, D_MODEL), BF16)] * 3,
                                temps=8 << 20, sem=("arbitrary",)),
    )(a, og, proj, proj, x, w_a, w_b, w_out, ffn_g)


def _ffn_forward(h2, x1, w_gu, w_down, target, final_g):
    t = x1.shape[0]
    tm = _tile(t, 512)

    def body(h_ref, wgu_ref, wd_ref, x1_ref, t_ref, g_ref, gu_ref, act_ref, loss_ref, dg_ref, dx_ref, dxb_ref, acc):
        m, j = pl.program_id(0), pl.program_id(1)

        @pl.when((m == 0) & (j == 0))
        def _():
            loss_ref[...] = jnp.zeros_like(loss_ref)
            dg_ref[...] = jnp.zeros_like(dg_ref)

        h = h_ref[...]
        gate = _dot_nt(h, wgu_ref[j])
        up = _dot_nt(h, wgu_ref[j + 4])
        gu_ref[0] = gate
        gu_ref[1] = up
        act = ((gate * _sigmoid(gate)) * up).astype(BF16)
        act_ref[...] = act
        part = _dot(act, wd_ref[j])

        @pl.when(j == 0)
        def _():
            acc[...] = part

        @pl.when((j > 0) & (j < 3))
        def _():
            acc[...] += part

        @pl.when(j == 3)
        def _():
            x2 = x1_ref[...] + (acc[...] + part)
            g = g_ref[...]
            r, xh = _rms_stats(x2)
            err = xh * g - t_ref[...]
            loss_ref[...] += 0.5 * jnp.sum(jnp.mean(err * err, axis=-1, keepdims=True), axis=0, keepdims=True)
            dy = err * (1.0 / D_MODEL)
            dg_ref[...] += jnp.sum(dy * xh, axis=0, keepdims=True)
            dxh = dy * g
            dx = r * (dxh - xh * jnp.mean(dxh * xh, axis=-1, keepdims=True))
            dx_ref[...] = dx
            dxb_ref[...] = dx.astype(BF16)

    tok = pl.BlockSpec((tm, D_MODEL), lambda m, j: (m, 0))
    vec = pl.BlockSpec((1, D_MODEL), lambda m, j: (0, 0))
    return pl.pallas_call(
        body, name="ffn_fwd",
        out_shape=[jax.ShapeDtypeStruct((4, 2, t, FF_BLOCK), F32), jax.ShapeDtypeStruct((4, t, FF_BLOCK), BF16),
                   jax.ShapeDtypeStruct((8, 128), F32), jax.ShapeDtypeStruct((1, D_MODEL), F32),
                   jax.ShapeDtypeStruct((t, D_MODEL), F32), jax.ShapeDtypeStruct((t, D_MODEL), BF16)],
        grid=(t // tm, 4),
        in_specs=[tok, RESIDENT, RESIDENT, tok, tok, vec],
        out_specs=[pl.BlockSpec((None, 2, tm, FF_BLOCK), lambda m, j: (j, 0, m, 0)),
                   pl.BlockSpec((None, tm, FF_BLOCK), lambda m, j: (j, m, 0)),
                   pl.BlockSpec((8, 128), lambda m, j: (0, 0)), vec, tok, tok],
        scratch_shapes=[pltpu.VMEM((tm, D_MODEL), F32)],
        compiler_params=_params([((tm, D_MODEL), BF16), ((tm, D_MODEL), F32), ((tm, D_MODEL), F32), ((2, tm, 768), F32),
                                 ((tm, 768), BF16), ((tm, D_MODEL), F32), ((tm, D_MODEL), BF16)],
                                scratch=[((tm, D_MODEL), F32), ((N_DEV, FF_BLOCK, D_MODEL), BF16), ((D_FF, D_MODEL), BF16)],
                                temps=6 << 20, sem=("arbitrary", "arbitrary")),
    )(h2, w_gu, w_down.reshape(4, FF_BLOCK, D_MODEL), x1, target, final_g)


def _ffn_backward(dx2b, dx2, gu, x1, w_gu, w_down, ffn_g):
    t = x1.shape[0]
    tm = _tile(t, 512)

    def body(dxb_ref, dx2_ref, gu_ref, x1_ref, wgu_ref, wd_ref, g_ref, dgu_ref, dx1_ref, dx1b_ref, dg_ref, acc, prev):
        m, j = pl.program_id(0), pl.program_id(1)

        @pl.when((m == 0) & (j == 0))
        def _():
            dg_ref[...] = jnp.zeros_like(dg_ref)

        @pl.when(j == 0)
        def _():
            prev[...] = jnp.zeros_like(prev)
            acc[...] = jnp.zeros_like(acc)

        jm1 = jnp.maximum(j - 1, 0)
        acc[...] += _dot(prev[0], wgu_ref[jm1]) + _dot(prev[1], wgu_ref[jm1 + 4])
        dact = _dot_nt(dxb_ref[...], wd_ref[j])
        gate, up = gu_ref[0], gu_ref[1]
        sg = _sigmoid(gate)
        dgate = (dact * up * (sg * (1.0 + gate * (1.0 - sg)))).astype(BF16)
        dup = (dact * (gate * sg)).astype(BF16)
        dgu_ref[0] = dgate
        dgu_ref[1] = dup
        prev[0] = dgate
        prev[1] = dup

        @pl.when(j == 3)
        def _():
            dh2 = acc[...] + (_dot(prev[0], wgu_ref[3]) + _dot(prev[1], wgu_ref[7]))
            dx, dg = _rms_bwd(dh2, x1_ref[...], g_ref[...])
            dx1 = dx2_ref[...] + dx
            dx1_ref[...] = dx1
            dx1b_ref[...] = dx1.astype(BF16)
            dg_ref[...] += dg

    tok = pl.BlockSpec((tm, D_MODEL), lambda m, j: (m, 0))
    vec = pl.BlockSpec((1, D_MODEL), lambda m, j: (0, 0))
    gu_spec = pl.BlockSpec((None, 2, tm, FF_BLOCK), lambda m, j: (j, 0, m, 0))
    return pl.pallas_call(
        body, name="ffn_bwd",
        out_shape=[jax.ShapeDtypeStruct((4, 2, t, FF_BLOCK), BF16), jax.ShapeDtypeStruct((t, D_MODEL), F32),
                   jax.ShapeDtypeStruct((t, D_MODEL), BF16), jax.ShapeDtypeStruct((1, D_MODEL), F32)],
        grid=(t // tm, 4),
        in_specs=[tok, tok, gu_spec, tok, RESIDENT, RESIDENT, vec],
        out_specs=[gu_spec, tok, tok, vec],
        scratch_shapes=[pltpu.VMEM((tm, D_MODEL), F32), pltpu.VMEM((2, tm, FF_BLOCK), BF16)],
        compiler_params=_params([((tm, D_MODEL), BF16), ((tm, D_MODEL), F32), ((2, tm, 768), F32), ((tm, D_MODEL), F32),
                                 ((2, tm, 768), BF16), ((tm, D_MODEL), F32), ((tm, D_MODEL), BF16)],
                                scratch=[((tm, D_MODEL), F32), ((2, tm, 768), BF16), ((N_DEV, FF_BLOCK, D_MODEL), BF16),
                                         ((D_FF, D_MODEL), BF16)],
                                temps=4 << 20, sem=("arbitrary", "arbitrary")),
    )(dx2b, dx2, gu, x1, w_gu, w_down.reshape(4, FF_BLOCK, D_MODEL), ffn_g)


def _branch_out_backward(dx1b, ya, yb, proj, w_a, w_b, w_out):
    t = ya.shape[0]
    tm = _tile(t, 512)

    def body(dx_ref, ya_ref, yb_ref, ga_ref, gb_ref, wa_ref, wb_ref, wo_ref, dya_ref, dyb_ref, dgate_ref, da_ref, dog_ref):
        dm = _dot_nt(dx_ref[...], wo_ref[...])
        sa, sb = _sigmoid(ga_ref[...]), _sigmoid(gb_ref[...])
        dya = (dm * sa).astype(BF16)
        dyb = (dm * sb).astype(BF16)
        dya_ref[...] = dya
        dyb_ref[...] = dyb
        dgate_ref[0] = (dm * ya_ref[...] * (sa * (1.0 - sa))).astype(BF16)
        dgate_ref[1] = (dm * yb_ref[...] * (sb * (1.0 - sb))).astype(BF16)
        da_ref[...] = _dot_nt(dya, wa_ref[...])
        dog_ref[...] = _dot_nt(dyb, wb_ref[...])

    tok = pl.BlockSpec((tm, D_MODEL), lambda m: (m, 0))
    return pl.pallas_call(
        body, name="branch_out_bwd",
        out_shape=[jax.ShapeDtypeStruct((t, D_MODEL), BF16), jax.ShapeDtypeStruct((t, D_MODEL), BF16),
                   jax.ShapeDtypeStruct((N_DEV, t, D_MODEL), BF16), jax.ShapeDtypeStruct((t, D_MODEL), F32),
                   jax.ShapeDtypeStruct((t, D_MODEL), F32)],
        grid=(t // tm,),
        in_specs=[tok, tok, tok, pl.BlockSpec((None, tm, D_MODEL), lambda m: (GATE_POS, m, 0)),
                  pl.BlockSpec((None, tm, D_MODEL), lambda m: (GATE_POS + 1, m, 0)), RESIDENT, RESIDENT, RESIDENT],
        out_specs=[tok, tok, pl.BlockSpec((2, tm, D_MODEL), lambda m: (GATE_POS // 2, m, 0)), tok, tok],
        compiler_params=_params([((tm, D_MODEL), BF16)] * 5 + [((tm, D_MODEL), F32)] * 6, scratch=[((D_MODEL, D_MODEL), BF16)] * 3,
                                temps=8 << 20, sem=("arbitrary",)),
    )(dx1b, ya, yb, proj, proj, w_a, w_b, w_out)


def _hgrn_backward(dproj, dog, o_saved, states, proj, lb_table, norm_g):
    t = proj.shape[1]
    tb = _tile(t, 1024)
    nc = tb // HGRN_CHUNK
    nb = t // tb

    def body(_, dog_ref, o_ref, st_ref, q_ref, f_ref, i_ref, g_ref, tab_ref, ng_ref, dp_ref, dng_ref, dtab_ref, gstate):
        @pl.when(pl.program_id(1) == 0)
        def _():
            gstate[...] = jnp.zeros_like(gstate)
            dng_ref[...] = jnp.zeros_like(dng_ref)
            dtab_ref[...] = jnp.zeros_like(dtab_ref)

        lb = _lower_bound(tab_ref)
        ng = ng_ref[...]
        lower, upper = _tri_masks()
        gt = _hgrn_gates(q_ref[...], f_ref[...], lb, nc)
        qi, ki, kd, qe = (gt[n].astype(BF16) for n in ("qi", "ki", "kd", "qe"))
        vb = i_ref[...].astype(BF16)
        o, gz, d_og = o_ref[...], g_ref[...], dog_ref[...]
        r, oh = _rms_stats(o)
        sg = _sigmoid(gz)
        d_on = d_og * (gz * sg)
        dgz = d_og * (oh * ng) * (sg * (1.0 + gz * (1.0 - sg)))
        dng_ref[...] += jnp.sum(d_on * oh, axis=0, keepdims=True)
        doh = d_on * ng
        dob = (r * (doh - oh * jnp.mean(doh * oh, axis=-1, keepdims=True))).astype(BF16)
        dv_intra, dqi, dki, dqe, g_upd = [], [], [], [], []
        for c in range(nc):
            rows = _chunk_rows(c)
            p = jnp.where(lower, _dot_nt(qi[rows], ki[rows]), 0.0).astype(BF16)
            dv_intra.append(_dot_tn(p, dob[rows]))
            dp = jnp.where(lower, _dot_nt(dob[rows], vb[rows]), 0.0).astype(BF16)
            dqi.append(_dot(dp, ki[rows]))
            dki.append(_dot_tn(dp, qi[rows]))
            dqe.append(_dot(dob[rows], st_ref[c].astype(BF16)))
            g_upd.append(_dot_tn(dob[rows], qe[rows]))
        g_after = [None] * nc
        g = gstate[...]
        for c in reversed(range(nc)):
            g_after[c] = g
            g = g * gt["decay"][c] + g_upd[c]
        gstate[...] = g
        dkd, dv, da_last = [], [], []
        for c in range(nc):
            rows = _chunk_rows(c)
            gb = g_after[c].astype(BF16)
            dkd.append(_dot(vb[rows], gb))
            dv.append(dv_intra[c] + _dot_nt(kd[rows], gb))
            da_last.append(jnp.sum(g_after[c] * st_ref[c], axis=0, keepdims=True) * gt["decay"][c])
        dqi, dki, dqe, dkd, dv = (jnp.concatenate(z, axis=0) for z in (dqi, dki, dqe, dkd, dv))
        dqs = dqi * gt["e_in"] + dqe * gt["e_all"]
        dk = dki * gt["e_out"] + dkd * gt["e_end"]
        t_in, t_out, t_end = dqi * gt["qi"], dki * gt["ki"], dkd * gt["kd"]
        da = t_in - t_out + dqe * gt["qe"] - t_end
        row = lax.broadcasted_iota(jnp.int32, (HGRN_CHUNK, HEAD_DIM), 0)
        d_mid = t_out - t_in
        pieces = []
        for c in range(nc):
            rows = _chunk_rows(c)
            da_mid = jnp.sum(d_mid[rows], axis=0, keepdims=True)
            da_end = jnp.sum(t_end[rows], axis=0, keepdims=True) + da_last[c]
            da_c = da[rows] + jnp.where(row == HGRN_CHUNK // 2 - 1, da_mid, 0.0) + jnp.where(row == HGRN_CHUNK - 1, da_end, 0.0)
            pieces.append(_mask_mm(upper.astype(BF16), da_c))
        df = jnp.concatenate(pieces, axis=0) / gt["f"] - dk
        s = gt["s"]
        dlb = jnp.sum(df * (1.0 - s), axis=0, keepdims=True)
        dp_ref[0] = (dqs * HGRN_SCALE).astype(BF16)
        dp_ref[1] = (df * (1.0 - lb) * (s * (1.0 - s))).astype(BF16)
        dp_ref[2] = dv.astype(BF16)
        dp_ref[3] = dgz.astype(BF16)
        dt0 = dlb * (lb * (1.0 - lb))
        dtab_ref[0:1, :] += dt0
        dtab_ref[1:2, :] -= dt0

    def blk(p):
        return pl.BlockSpec((None, tb, HEAD_DIM), lambda h, n: (p, nb - 1 - n, h))

    tok = pl.BlockSpec((tb, HEAD_DIM), lambda h, n: (nb - 1 - n, h))
    return pl.pallas_call(
        body, name="hgrn_bwd",
        out_shape=[jax.ShapeDtypeStruct((N_DEV, t, D_MODEL), BF16), jax.ShapeDtypeStruct((1, D_MODEL), F32),
                   jax.ShapeDtypeStruct((2, D_MODEL), F32)],
        grid=(HEADS, nb),
        in_specs=[ANY, tok, tok, pl.BlockSpec((None, nc, HEAD_DIM, HEAD_DIM), lambda h, n: (h, nb - 1 - n, 0, 0)),
                  blk(Q_POS), blk(Q_POS + 1), blk(Q_POS + 2), blk(Q_POS + 3),
                  pl.BlockSpec((2, HEAD_DIM), lambda h, n: (0, h)), pl.BlockSpec((1, HEAD_DIM), lambda h, n: (0, h))],
        out_specs=[pl.BlockSpec((4, tb, HEAD_DIM), lambda h, n: (0, nb - 1 - n, h)),
                   pl.BlockSpec((1, HEAD_DIM), lambda h, n: (0, h)), pl.BlockSpec((2, HEAD_DIM), lambda h, n: (0, h))],
        scratch_shapes=[pltpu.VMEM((HEAD_DIM, HEAD_DIM), F32)],
        input_output_aliases={0: 0},
        compiler_params=_params([((tb, HEAD_DIM), F32)] * 6 + [((nc, HEAD_DIM, HEAD_DIM), F32)] + [((4, tb, HEAD_DIM), BF16)],
                                temps=8 << 20, sem=("arbitrary", "arbitrary")),
    )(dproj, dog, o_saved, states, proj, proj, proj, proj, lb_table, norm_g)


def _gmlp_backward(dproj, da, proj, ln_g, ln_b, w_s, bias_b):
    t = proj.shape[1]
    tm = _tile(t, 256)
    chunks = tm // GMLP_CHUNK

    def body(_, da_ref, u_ref, v_ref, lng_ref, lnb_ref, ws_ref, bias_ref, dp_ref, dlng_ref, dlnb_ref, dws_ref, dbs_ref,
             vn_scr, dvn_scr):
        @pl.when(pl.program_id(0) == 0)
        def _():
            dlng_ref[...] = jnp.zeros_like(dlng_ref)
            dlnb_ref[...] = jnp.zeros_like(dlnb_ref)
            dws_ref[...] = jnp.zeros_like(dws_ref)
            dbs_ref[...] = jnp.zeros_like(dbs_ref)

        v = v_ref[...]
        vv, dvv_dv = _gelu_and_grad(v)
        mu = jnp.mean(vv, axis=-1, keepdims=True)
        cen = vv - mu
        rstd = lax.rsqrt(jnp.mean(cen * cen, axis=-1, keepdims=True) + NORM_EPS)
        vhat = cen * rstd
        lng = lng_ref[...]
        vn_scr[...] = (vhat * lng + lnb_ref[...]).astype(BF16)
        row = lax.broadcasted_iota(jnp.int32, (GMLP_CHUNK, GMLP_CHUNK), 0)
        col = lax.broadcasted_iota(jnp.int32, (GMLP_CHUNK, GMLP_CHUNK), 1)
        for g in range(GROUPS):
            wm = _masked_ws(ws_ref, g)
            cols = slice(g * HEAD_DIM, (g + 1) * HEAD_DIM)
            dws = jnp.zeros((GMLP_CHUNK, GMLP_CHUNK), F32)
            dbs = jnp.zeros((GMLP_CHUNK, GMLP_CHUNK), F32)
            for c in range(chunks):
                rows = slice(c * GMLP_CHUNK, (c + 1) * GMLP_CHUNK)
                vn = vn_scr[rows, cols]
                mixed = _dot(wm, vn) + bias_ref[g]
                u = u_ref[rows, cols]
                d_a = da_ref[rows, cols]
                gelu_u, dgelu_u = _gelu_and_grad(u)
                dp_ref[0, rows, cols] = (d_a * mixed * dgelu_u).astype(BF16)
                dmix = d_a * gelu_u
                dmb = dmix.astype(BF16)
                dbs = dbs + dmix
                dws = dws + _dot_nt(dmb, vn)
                dvn_scr[rows, cols] = _dot_tn(wm, dmb)
            dws_ref[g] += jnp.where(row >= col, dws, 0.0)
            dbs_ref[g] += jnp.broadcast_to(jnp.sum(dbs, axis=-1, keepdims=True), (GMLP_CHUNK, GMLP_CHUNK))
        dvn = dvn_scr[...]
        dlng_ref[...] += jnp.sum(dvn * vhat, axis=0, keepdims=True)
        dlnb_ref[...] += jnp.sum(dvn, axis=0, keepdims=True)
        dvh = dvn * lng
        dvv = rstd * (dvh - jnp.mean(dvh, axis=-1, keepdims=True) - vhat * jnp.mean(dvh * vhat, axis=-1, keepdims=True))
        dp_ref[1] = (dvv * dvv_dv).astype(BF16)

    tok = pl.BlockSpec((tm, D_MODEL), lambda m: (m, 0))
    small = pl.BlockSpec((GROUPS, GMLP_CHUNK, GMLP_CHUNK), lambda m: (0, 0, 0))
    vec = pl.BlockSpec((1, D_MODEL), lambda m: (0, 0))
    return pl.pallas_call(
        body, name="gmlp_bwd",
        out_shape=[jax.ShapeDtypeStruct(dproj.shape, BF16), jax.ShapeDtypeStruct((1, D_MODEL), F32),
                   jax.ShapeDtypeStruct((1, D_MODEL), F32), jax.ShapeDtypeStruct((GROUPS, GMLP_CHUNK, GMLP_CHUNK), F32),
                   jax.ShapeDtypeStruct((GROUPS, GMLP_CHUNK, GMLP_CHUNK), F32)],
        grid=(t // tm,),
        in_specs=[ANY, tok, pl.BlockSpec((None, tm, D_MODEL), lambda m: (U_POS, m, 0)),
                  pl.BlockSpec((None, tm, D_MODEL), lambda m: (U_POS + 1, m, 0)), vec, vec, small, small],
        out_specs=[pl.BlockSpec((2, tm, D_MODEL), lambda m: (U_POS // 2, m, 0)), vec, vec, small, small],
        scratch_shapes=[pltpu.VMEM((tm, D_MODEL), BF16), pltpu.VMEM((tm, D_MODEL), F32)],
        input_output_aliases={0: 0},
        compiler_params=_params([((tm, D_MODEL), F32)] * 3 + [((2, tm, D_MODEL), BF16)] + [((8, 128, 128), F32)] * 4,
                                scratch=[((tm, D_MODEL), BF16), ((tm, D_MODEL), F32)], temps=12 << 20, sem=("arbitrary",)),
    )(dproj, da, proj, proj, ln_g, ln_b, w_s, bias_b)


def _input_backward(dproj, w_in_g, x, dx1, mix_g):
    t = x.shape[0]
    tm = _tile(t, 512)

    def body(dp_ref, w_ref, x_ref, dx1_ref, g_ref, dx_ref, dg_ref):
        @pl.when(pl.program_id(0) == 0)
        def _():
            dg_ref[...] = jnp.zeros_like(dg_ref)

        dh = _dot_nt(dp_ref[0], w_ref[0])
        for p in range(1, N_DEV):
            dh = dh + _dot_nt(dp_ref[p], w_ref[p])
        dx, dg = _rms_bwd(dh, x_ref[...], g_ref[...])
        dx_ref[...] = dx1_ref[...] + dx
        dg_ref[...] += dg

    tok = pl.BlockSpec((tm, D_MODEL), lambda m: (m, 0))
    vec = pl.BlockSpec((1, D_MODEL), lambda m: (0, 0))
    return pl.pallas_call(
        body, name="input_bwd",
        out_shape=[jax.ShapeDtypeStruct((t, D_MODEL), F32), jax.ShapeDtypeStruct((1, D_MODEL), F32)],
        grid=(t // tm,),
        in_specs=[pl.BlockSpec((N_DEV, tm, D_MODEL), lambda m: (0, m, 0)), RESIDENT, tok, tok, vec],
        out_specs=[tok, vec],
        compiler_params=_params([((N_DEV, tm, D_MODEL), BF16)] + [((tm, D_MODEL), F32)] * 3,
                                scratch=[((N_DEV, D_MODEL, D_MODEL), BF16)], temps=6 << 20, sem=("arbitrary",)),
    )(dproj, w_in_g, x, dx1, mix_g)


def _weight_grad(name, a, b, a_spec, b_spec, out_shape, out_spec, steps, blocks, a_is_transposed):
    def body(a_ref, b_ref, o_ref):
        o_ref[...] = _dot(a_ref[...], b_ref[...]) if a_is_transposed else _dot_tn(a_ref[...], b_ref[...])

    return pl.pallas_call(
        body, name=name, out_shape=jax.ShapeDtypeStruct(out_shape, F32), grid=(steps,), in_specs=[a_spec, b_spec],
        out_specs=out_spec, compiler_params=_params(blocks, temps=4 << 20, sem=("arbitrary",)),
    )(a, b)


def _pack_small(mix_g, ln_g, ln_b, b_s, lb_table, hg_norm, ffn_g, final_g, loss_row):
    def part(a):
        a = a.reshape(-1, D_MODEL)
        return jnp.pad(a, ((0, 8 - a.shape[0]), (0, 0)))

    return jnp.concatenate([part(mix_g), part(ln_g), part(ln_b), part(hg_norm), part(ffn_g), part(final_g),
                            part(lb_table), part(b_s), part(loss_row)], axis=0)


SMALL_PARTS = (("gmlp_ln_g", 8, 1), ("gmlp_ln_b", 16, 1), ("hgrn_norm_g", 24, 1), ("norm_ffn_g", 32, 1), ("norm_final_g", 40, 1),
               ("hgrn_lb_table", 48, 2))


def _adamw_small_unpacked(gathered, w, m, v):
    rows = w.shape[0]
    n_out = len(SMALL_PARTS) + 1

    def body(p_ref, w_ref, m_ref, v_ref, *outs):
        g = p_ref[0]
        for j in range(1, N_DEV):
            g = g + p_ref[j]
        delta, m_new, v_new = _adamw_math(w_ref[...], g, m_ref[...], v_ref[...])
        for kind, val in enumerate((g, delta, m_new, v_new)):
            refs = outs[kind * n_out:(kind + 1) * n_out]
            for (_, first, count), ref in zip(SMALL_PARTS, refs):
                ref[...] = val[first:first + count]
            for grp in range(GROUPS):
                refs[-1][0, grp:grp + 1, :] = val[56:57, grp * GMLP_CHUNK:(grp + 1) * GMLP_CHUNK]
        outs[-1][...] = g[SMALL_ROWS - 8:SMALL_ROWS - 7]

    shapes = [jax.ShapeDtypeStruct((count, D_MODEL), F32) for _, _, count in SMALL_PARTS]
    shapes.append(jax.ShapeDtypeStruct((1, GROUPS, GMLP_CHUNK), F32))
    whole = pl.BlockSpec((rows, D_MODEL), lambda: (0, 0))
    res = pl.pallas_call(
        body, name="adamw_small", out_shape=shapes * 4 + [jax.ShapeDtypeStruct((1, D_MODEL), F32)],
        in_specs=[pl.BlockSpec((N_DEV, rows, D_MODEL), lambda: (0, 0, 0)), whole, whole, whole],
        compiler_params=_params([((N_DEV, rows, D_MODEL), F32)] + [((rows, D_MODEL), F32)] * 7),
    )(gathered, w, m, v)
    names = [nme for nme, _, _ in SMALL_PARTS] + ["gmlp_b_s"]
    return [dict(zip(names, res[kind * n_out:(kind + 1) * n_out])) for kind in range(4)], res[-1]


def _adamw_row(name, gathered, w, m, v):
    def body(p_ref, w_ref, m_ref, v_ref, g_out, d_out, m_out, v_out):
        g = p_ref[0, 0:1, :]
        for j in range(1, N_DEV):
            g = g + p_ref[j, 0:1, :]
        delta, m_new, v_new = _adamw_math(w_ref[...], g, m_ref[...], v_ref[...])
        g_out[...] = g
        d_out[...] = delta
        m_out[...] = m_new
        v_out[...] = v_new

    return pl.pallas_call(
        body, name=name, out_shape=[jax.ShapeDtypeStruct((1, D_MODEL), F32)] * 4,
        compiler_params=_params([((N_DEV, 8, D_MODEL), F32)] + [((8, D_MODEL), F32)] * 7),
    )(gathered, w, m, v)


def _adamw_small(name, gathered, w, m, v):
    rows, cols = w.shape

    def body(p_ref, w_ref, m_ref, v_ref, g_out, d_out, m_out, v_out):
        g = p_ref[0]
        for j in range(1, N_DEV):
            g = g + p_ref[j]
        delta, m_new, v_new = _adamw_math(w_ref[...], g, m_ref[...], v_ref[...])
        g_out[...] = g
        d_out[...] = delta
        m_out[...] = m_new
        v_out[...] = v_new

    tr = _tile(rows, 512)
    spec = pl.BlockSpec((tr, cols), lambda r: (r, 0))
    return pl.pallas_call(
        body, name=name, out_shape=[jax.ShapeDtypeStruct((rows, cols), F32)] * 4, grid=(rows // tr,),
        in_specs=[pl.BlockSpec((N_DEV, tr, cols), lambda r: (0, r, 0)), spec, spec, spec], out_specs=[spec] * 4,
        compiler_params=_params([((N_DEV, tr, cols), F32)] + [((tr, cols), F32)] * 7, sem=("arbitrary",)),
    )(gathered, w, m, v)


def kernel(x, norm_mix_g, w_in, gmlp_ln_g, gmlp_ln_b, gmlp_w_s, gmlp_b_s, hgrn_lb_table, hgrn_norm_g, w_branch_a, w_branch_b, w_out, norm_ffn_g, w_gate_up, w_down, norm_final_g, loss_target, m_norm_mix_g, m_w_in, m_gmlp_ln_g, m_gmlp_ln_b, m_gmlp_w_s, m_gmlp_b_s, m_hgrn_lb_table, m_hgrn_norm_g, m_w_branch_a, m_w_branch_b, m_w_out, m_norm_ffn_g, m_w_gate_up, m_w_down, m_norm_final_g, v_norm_mix_g, v_w_in, v_gmlp_ln_g, v_gmlp_ln_b, v_gmlp_w_s, v_gmlp_b_s, v_hgrn_lb_table, v_hgrn_norm_g, v_w_branch_a, v_w_branch_b, v_w_out, v_norm_ffn_g, v_w_gate_up, v_w_down, v_norm_final_g):
    t = x.shape[1]
    x2d = x.reshape(t, D_MODEL)
    target = loss_target.reshape(t, D_MODEL)
    final_g = norm_final_g.reshape(1, D_MODEL)

    shards = [w_in[0].astype(BF16), w_branch_a[0].astype(BF16), w_branch_b[0].astype(BF16), w_out[0].astype(BF16),
              w_gate_up[0].T.astype(BF16), w_down[0].astype(BF16)]

    def rows_of(n):
        return lambda ref, j: ref.at[pl.ds(pl.multiple_of(j * n, 8), n)]

    gathered = [((N_DEV, D_MODEL, D_MODEL), BF16), ((D_MODEL, D_MODEL), BF16), ((D_MODEL, D_MODEL), BF16),
                ((D_MODEL, D_MODEL), BF16), ((N_DEV, FF_BLOCK, D_MODEL), BF16), ((D_FF, D_MODEL), BF16)]
    places = [lambda ref, j: ref.at[_pos_of_dev(j)], rows_of(BRANCH_ROWS), rows_of(BRANCH_ROWS), rows_of(BRANCH_ROWS),
              lambda ref, j: ref.at[j], rows_of(DOWN_ROWS)]
    w_in_g = _all_gather_balanced_async("w_in_all_gather", 9, shards[0], gathered[0], places[0], D_MODEL)
    w_in_sibling = _swap_with_sibling("w_in_from_sibling", shards[0])
    _, later = lax.optimization_barrier((w_in_sibling, shards[1:]))
    w_a, w_b, w_o, w_gu, w_dn = _all_gather_async("weights_all_gather", 0, later, gathered[1:], places[1:])

    core_i, chip_i = lax.axis_index("c"), 2 * lax.axis_index("x") + lax.axis_index("y")
    own_pos = jnp.stack([_pos_of_dev(2 * chip_i + core_i), _pos_of_dev(2 * chip_i + 1 - core_i)]).astype(jnp.int32)
    other_pos = jnp.stack([_pos_of_dev(2 * jnp.bitwise_xor(chip_i, q) + cc) for q in (1, 2, 3) for cc in (0, 1)]).astype(jnp.int32)
    proj, h, h_t = _proj_forward_own_chip(own_pos, x2d, norm_mix_g, shards[0], w_in_sibling)
    proj = _proj_forward_other_chips(other_pos, proj, h, w_in_g)
    bias_b = jnp.broadcast_to(gmlp_b_s[0][:, :, None], (GROUPS, GMLP_CHUNK, GMLP_CHUNK))
    a = _gmlp_forward(proj, gmlp_ln_g, gmlp_ln_b, gmlp_w_s[0], bias_b)
    og, o_saved, states = _hgrn_forward(proj, hgrn_lb_table, hgrn_norm_g)
    ya, yb, merged, x1, h2 = _branch_out_forward(a, og, proj, x2d, w_a, w_b, w_o, norm_ffn_g)
    gu, act, loss_tile, d_final_g, dx2, dx2b = _ffn_forward(h2, x1, w_gu, w_dn, target, final_g)

    core = lax.axis_index("c").astype(jnp.int32).reshape(1)
    chip = (2 * lax.axis_index("x") + lax.axis_index("y")).astype(jnp.int32).reshape(1)
    branch_rows, branch_shape = rows_of(BRANCH_ROWS), (BRANCH_ROWS, D_MODEL)
    branch_block = ((BRANCH_ROWS, D_MODEL), lambda q, r, c: (2 * q + c, 0))

    def chip_partials(names, grads, land, own_blocks):
        return [_chip_partial("chip_partial_" + nme, core, g_, blk, idx, l_)
                for nme, g_, (blk, idx), l_ in zip(names, grads, own_blocks, land)]

    whole = pl.BlockSpec((t, D_MODEL), lambda n: (0, 0))
    whole_t = pl.BlockSpec((D_MODEL, t), lambda n: (0, 0))
    col_blocks = [((t, D_MODEL), BF16), ((t, 256), BF16), ((D_MODEL, 256), F32)]

    def square_grad(name, a_, b_):
        return _weight_grad(name, a_, b_, whole, pl.BlockSpec((t, 256), lambda n: (0, n)), (D_MODEL, D_MODEL),
                            pl.BlockSpec((D_MODEL, 256), lambda n: (0, n)), D_MODEL // 256, col_blocks, False)

    dgu, dx1, dx1b, d_ffn_g = _ffn_backward(dx2b, dx2, gu, x1, w_gu, w_dn, norm_ffn_g)
    g_gu = _weight_grad(
        "grad_w_gate_up", dgu, h2, pl.BlockSpec((None, None, t, FF_BLOCK), lambda j: (j % 4, j // 4, 0, 0)), whole,
        (N_DEV, FF_BLOCK, D_MODEL), pl.BlockSpec((None, FF_BLOCK, D_MODEL), lambda j: (j, 0, 0)), N_DEV,
        [((t, 768), BF16), ((t, D_MODEL), BF16), ((FF_BLOCK, D_MODEL), F32)], False)
    g_dn = _weight_grad(
        "grad_w_down", act, dx2b, pl.BlockSpec((None, t, FF_BLOCK), lambda j: (j, 0, 0)), whole, (D_FF, D_MODEL),
        pl.BlockSpec((FF_BLOCK, D_MODEL), lambda j: (j, 0)), 4,
        [((t, 768), BF16), ((t, D_MODEL), BF16), ((FF_BLOCK, D_MODEL), F32)], False)
    names_f, grads_f = ["w_gate_up", "w_down"], [g_gu, g_dn]
    land_f = _exchange_sibling("ffn_grads_to_sibling", 2, grads_f, [lambda ref, j: ref.at[j], rows_of(DOWN_ROWS)],
                               [(FF_BLOCK, D_MODEL), (DOWN_ROWS, D_MODEL)])

    dx1b_later, _ = lax.optimization_barrier((dx1b, grads_f))
    dya, dyb, dproj, da, dog = _branch_out_backward(dx1b_later, ya, yb, proj, w_a, w_b, w_o)
    g_a = square_grad("grad_w_a", a, dya)
    g_b = square_grad("grad_w_b", og, dyb)
    g_o = square_grad("grad_w_out", merged, dx1b)
    names_b, grads_b = ["w_branch_a", "w_branch_b", "w_out"], [g_a, g_b, g_o]
    land_b = _exchange_sibling("branch_grads_to_sibling", 3, grads_b, [branch_rows] * 3, [branch_shape] * 3)

    part_f = chip_partials(names_f, grads_f, land_f,
                           [((None, FF_BLOCK, D_MODEL), lambda q, r, c: (2 * q + c, 0, 0)),
                            ((DOWN_ROWS, D_MODEL), lambda q, r, c: (2 * q + c, 0))])
    landed_f = _exchange_chips("ffn_grads_to_chips", 5, part_f)

    dog, _ = lax.optimization_barrier((dog, part_f))
    dproj, d_hg_norm, d_lb = _hgrn_backward(dproj, dog, o_saved, states, proj, hgrn_lb_table, hgrn_norm_g)

    land_b, _ = lax.optimization_barrier((land_b, part_f))
    part_b = chip_partials(names_b, grads_b, land_b, [branch_block] * 3)
    landed_b = _exchange_chips("branch_grads_to_chips", 6, part_b)

    da, _ = lax.optimization_barrier((da, part_b))
    dproj, d_ln_g, d_ln_b, d_ws, d_bs = _gmlp_backward(dproj, da, proj, gmlp_ln_g, gmlp_ln_b, gmlp_w_s[0], bias_b)

    def packed(vals):
        return _pack_small(*vals)

    def flat_ws(a):
        return a.reshape(GROUPS * GMLP_CHUNK, GMLP_CHUNK)

    no_row = jnp.zeros((1, D_MODEL), F32)
    w_pack = packed([norm_mix_g, gmlp_ln_g, gmlp_ln_b, gmlp_b_s, hgrn_lb_table, hgrn_norm_g, norm_ffn_g, norm_final_g, no_row])
    m_pack = packed([m_norm_mix_g, m_gmlp_ln_g, m_gmlp_ln_b, m_gmlp_b_s, m_hgrn_lb_table, m_hgrn_norm_g, m_norm_ffn_g, m_norm_final_g, no_row])
    v_pack = packed([v_norm_mix_g, v_gmlp_ln_g, v_gmlp_ln_b, v_gmlp_b_s, v_hgrn_lb_table, v_hgrn_norm_g, v_norm_ffn_g, v_norm_final_g, no_row])
    small_partial = _pack_small(no_row, d_ln_g, d_ln_b, d_bs[:, :, 0], d_lb, d_hg_norm, d_ffn_g, d_final_g,
                                jnp.tile(loss_tile[0:1], (1, D_MODEL // 128)))
    small_all, ws_all = _all_gather_async(
        "small_grads_all_gather", 1, [small_partial, flat_ws(d_ws)],
        [((N_DEV, SMALL_ROWS, D_MODEL), F32), ((N_DEV, GROUPS * GMLP_CHUNK, GMLP_CHUNK), F32)],
        [lambda ref, j: ref.at[j], lambda ref, j: ref.at[j]])

    g_in = _weight_grad(
        "grad_w_in", h_t, dproj, whole_t, pl.BlockSpec((None, t, D_MODEL), lambda p: (p, 0, 0)), (N_DEV, D_MODEL, D_MODEL),
        pl.BlockSpec((None, D_MODEL, D_MODEL), lambda p: (p, 0, 0)), N_DEV,
        [((D_MODEL, t), BF16), ((t, D_MODEL), BF16), ((D_MODEL, D_MODEL), F32)], True)
    land_i = _exchange_sibling("w_in_grads_to_sibling", 4, [g_in], [lambda ref, j: ref.at[_pos_of_dev(j)]],
                               [(D_MODEL, D_MODEL)])

    big = {}
    for nme, own, lnd, w, m, v in zip(
            names_f + names_b, part_f + part_b, landed_f + landed_b,
            [w_gate_up, w_down, w_branch_a, w_branch_b, w_out], [m_w_gate_up, m_w_down, m_w_branch_a, m_w_branch_b, m_w_out],
            [v_w_gate_up, v_w_down, v_w_branch_a, v_w_branch_b, v_w_out]):
        flip = (lambda z: z.T) if nme == "w_gate_up" else (lambda z: z)
        big[nme] = [flip(o_)[None] for o_ in _adamw("adamw_" + nme, chip, own, lnd, flip(w[0]), flip(m[0]), flip(v[0]))]
    small, loss_row = _adamw_small_unpacked(small_all, w_pack, m_pack, v_pack)
    ws_outs = _adamw_small("adamw_w_s", ws_all, flat_ws(gmlp_w_s), flat_ws(m_gmlp_w_s), flat_ws(v_gmlp_w_s))
    land_i, _ = lax.optimization_barrier((land_i, (big, small, ws_outs)))
    part_i = chip_partials(["w_in"], [g_in], land_i,
                           [((None, D_MODEL, D_MODEL), lambda q, r, c: (_pos_of_dev(2 * q + c), 0, 0))])
    landed_i = _exchange_chips("w_in_grads_to_chips", 7, part_i)

    dx1, _ = lax.optimization_barrier((dx1, part_i))
    grad_x, d_mix_g = _input_backward(dproj, w_in_g, x2d, dx1, norm_mix_g)
    big["w_in"] = [o_[None] for o_ in _adamw("adamw_w_in", chip, part_i[0], landed_i[0], w_in[0], m_w_in[0], v_w_in[0])]

    def row8(a):
        return jnp.pad(a, ((0, 7), (0, 0)))

    d_mix_g, _ = lax.optimization_barrier((d_mix_g, landed_i))
    (mix_all,) = _all_gather_async("mix_gain_grad_all_gather", 8, [row8(d_mix_g)], [((N_DEV, 8, D_MODEL), F32)],
                                   [lambda ref, j: ref.at[j]])
    mix_outs = _adamw_row("adamw_mix_gain", mix_all, norm_mix_g, m_norm_mix_g, v_norm_mix_g)
    small = [dict(p, norm_final_g=p["norm_final_g"][0], gmlp_w_s=ws.reshape(1, GROUPS, GMLP_CHUNK, GMLP_CHUNK), norm_mix_g=q)
             for p, ws, q in zip(small, ws_outs, mix_outs)]

    loss = loss_row[0, 0]
    order = ["norm_mix_g", "w_in", "gmlp_ln_g", "gmlp_ln_b", "gmlp_w_s", "gmlp_b_s", "hgrn_lb_table", "hgrn_norm_g",
             "w_branch_a", "w_branch_b", "w_out", "norm_ffn_g", "w_gate_up", "w_down", "norm_final_g"]
    outs = [loss, grad_x.reshape(1, t, D_MODEL)]
    for kind in range(4):
        for nme in order:
            outs.append(big[nme][kind] if nme in big else small[kind][nme])
    return tuple(outs)
```

```python
import functools

import jax
import jax.numpy as jnp
from jax import lax
from jax.experimental import pallas as pl
from jax.experimental.pallas import tpu as pltpu
from jax.experimental.pallas import tpu_sc as plsc

F32, BF16 = jnp.float32, jnp.bfloat16
D_MODEL = 1024
N_DEV = 8
HEADS = 8
HEAD_DIM = 128
GROUPS = 8
GMLP_CHUNK = 128
HGRN_CHUNK = 64
HGRN_SCALE = HEAD_DIM ** -0.5
D_FF = 2816
FF_BLOCK = D_FF // 4
DOWN_ROWS = D_FF // N_DEV
BRANCH_ROWS = D_MODEL // N_DEV
NORM_EPS = 1e-6
ADAM_LR, ADAM_B1, ADAM_B2, ADAM_EPS, ADAM_WD, ADAM_STEP = 0.001, 0.9, 0.999, 1e-08, 0.01, 10
SMALL_ROWS = 72
V7X_VMEM_BYTES = 64 * 1024 * 1024
VMEM_CAP = V7X_VMEM_BYTES - 6 * 1024 * 1024
MESH_ID = pl.DeviceIdType.MESH
ANY = pl.BlockSpec(memory_space=pl.ANY)
RESIDENT = pl.BlockSpec(memory_space=pltpu.VMEM)
Q_POS, U_POS, GATE_POS = 0, 4, 6


def _pos_of_dev(j):
    return jnp.where(j < 2, j + 4, jnp.where(j < 6, j - 2, j))


def _dev_of_pos(p):
    return jnp.where(p < 4, p + 2, jnp.where(p < 6, p - 4, p))


def _nbytes(shape, dtype):
    n = 1
    for s in shape:
        n *= s
    return n * jnp.dtype(dtype).itemsize


def _params(blocks, scratch=(), temps=0, sem=None):
    need = 2 * sum(_nbytes(s, d) for s, d in blocks) + sum(_nbytes(s, d) for s, d in scratch) + temps
    assert need + (4 << 20) <= VMEM_CAP, need
    return pltpu.CompilerParams(dimension_semantics=sem, vmem_limit_bytes=VMEM_CAP)


def _tile(n, pref):
    return pref if n % pref == 0 else n


def _dot(a, b):
    return jnp.dot(a, b, preferred_element_type=F32)


def _dot_nt(a, b):
    return lax.dot_general(a, b, (((1,), (1,)), ((), ())), preferred_element_type=F32)


def _dot_tn(a, b):
    return lax.dot_general(a, b, (((0,), (0,)), ((), ())), preferred_element_type=F32)


def _sigmoid(x):
    return 1.0 / (1.0 + jnp.exp(-x))


_GELU_C = 0.7978845608028654


def _gelu(x):
    return x * (0.5 * (1.0 + jnp.tanh(_GELU_C * (x + 0.044715 * (x * x * x)))))


def _gelu_and_grad(x):
    t = jnp.tanh(_GELU_C * (x + 0.044715 * (x * x * x)))
    half = 0.5 * (1.0 + t)
    return x * half, half + 0.5 * x * (1.0 - t * t) * (_GELU_C * (1.0 + 3.0 * 0.044715 * x * x))


def _rms_stats(x):
    r = lax.rsqrt(jnp.mean(x * x, axis=-1, keepdims=True) + NORM_EPS)
    return r, x * r


def _rms_bwd(dy, x, g):
    r, xh = _rms_stats(x)
    dg = jnp.sum(dy * xh, axis=0, keepdims=True)
    dxh = dy * g
    dx = r * (dxh - xh * jnp.mean(dxh * xh, axis=-1, keepdims=True))
    return dx, dg


def _split3(x):
    hi = x.astype(BF16)
    r = x - hi.astype(F32)
    mid = r.astype(BF16)
    lo = (r - mid.astype(F32)).astype(BF16)
    return hi, mid, lo


def _mask_mm(mask_bf16, x):
    hi, mid, lo = _split3(x)
    return _dot(mask_bf16, hi) + _dot(mask_bf16, mid) + _dot(mask_bf16, lo)


def _place():
    return lax.axis_index("x"), lax.axis_index("y"), lax.axis_index("c")


def _gather_copies(src, out, send, recv, loc, slicers):
    n = len(src)
    x, y, c = _place()
    me, sib = (x, y, c), (x, y, 1 - c)
    chips = [(1 - x, y), (x, 1 - y), (1 - x, 1 - y)]

    def dev(p):
        return 4 * p[0] + 2 * p[1] + p[2]

    def rc(i, k, block, to, from_src=False):
        dst = slicers[i](out[i], dev(block))
        return pltpu.make_async_remote_copy(
            src_ref=src[i] if from_src else dst, dst_ref=dst, send_sem=send.at[7 * i + k],
            recv_sem=recv.at[7 * i + k], device_id=to, device_id_type=MESH_ID)

    mine = [pltpu.make_async_copy(src[i], slicers[i](out[i], dev(me)), loc.at[i]) for i in range(n)]
    for cp in mine:
        cp.start()
    first = []
    for i in range(n):
        first.append(rc(i, 0, me, sib, True))
        for j, chip in enumerate(chips):
            first.append(rc(i, 1 + j, me, (*chip, c), True))
    for cp in first:
        cp.start()
    passed = []
    for j, chip in enumerate(chips):
        for i in range(n):
            rc(i, 1 + j, (*chip, c), me).wait_recv()
            cp = rc(i, 4 + j, (*chip, c), sib)
            cp.start()
            passed.append(cp)
    for i in range(n):
        rc(i, 0, sib, me).wait_recv()
        for j, chip in enumerate(chips):
            rc(i, 4 + j, (*chip, 1 - c), me).wait_recv()
    for cp in first + passed:
        cp.wait_send()
    for cp in mine:
        cp.wait()


def _gather_copies_balanced(src, out, send, recv, loc, slicer, rows):
    x, y, c = _place()
    me, sib = (x, y, c), (x, y, 1 - c)
    xn, yn, dg = (1 - x, y), (x, 1 - y), (1 - x, 1 - y)
    half_rows = rows // 2

    def block(p):
        return slicer(out, 4 * p[0] + 2 * p[1] + p[2])

    def half(ref, h):
        return ref.at[pl.ds(h * half_rows, half_rows)]

    def rc(k, dst, to, from_src=False):
        return pltpu.make_async_remote_copy(src_ref=src if from_src else dst, dst_ref=dst, send_sem=send.at[k],
                                            recv_sem=recv.at[k], device_id=to, device_id_type=MESH_ID)

    mine = pltpu.make_async_copy(src, block(me), loc.at[0])
    mine.start()
    sends = [rc(0, block(me), sib, True), rc(1, block(me), (*xn, c), True), rc(2, block(me), (*yn, c), True)]
    for cp in sends:
        cp.start()

    def then(cp):
        cp.start()
        sends.append(cp)

    rc(1, block((*xn, c)), me).wait_recv()
    then(rc(3, half(block((*xn, c)), 0), (*yn, c)))
    then(rc(5, block((*xn, c)), sib))
    rc(2, block((*yn, c)), me).wait_recv()
    then(rc(4, half(block((*yn, c)), 1), (*xn, c)))
    then(rc(6, block((*yn, c)), sib))
    rc(3, half(block((*dg, c)), 0), me).wait_recv()
    then(rc(7, half(block((*dg, c)), 0), sib))
    rc(4, half(block((*dg, c)), 1), me).wait_recv()
    then(rc(8, half(block((*dg, c)), 1), sib))
    rc(0, block(sib), me).wait_recv()
    rc(5, block((*xn, 1 - c)), me).wait_recv()
    rc(6, block((*yn, 1 - c)), me).wait_recv()
    rc(7, half(block((*dg, 1 - c)), 0), me).wait_recv()
    rc(8, half(block((*dg, 1 - c)), 1), me).wait_recv()
    for cp in sends:
        cp.wait_send()
    mine.wait()


def _gather_scratch(n):
    return [pltpu.SemaphoreType.DMA((7 * n,)), pltpu.SemaphoreType.DMA((7 * n,)), pltpu.SemaphoreType.DMA((n,))]


def _handshake(peers):
    barrier = pltpu.get_barrier_semaphore()
    for peer in peers:
        pl.semaphore_signal(barrier, inc=1, device_id=peer, device_id_type=MESH_ID)
    pl.semaphore_wait(barrier, len(peers))


def _all_gather_async(name, collective_id, srcs, out_shapes, slicers):
    n = len(srcs)

    def body(*refs):
        x, y, c = _place()
        _handshake([(1 - x if dx else x, 1 - y if dy else y, 1 - c if dc else c)
                    for dx in (0, 1) for dy in (0, 1) for dc in (0, 1) if dx or dy or dc])
        _gather_copies(refs[:n], refs[n:2 * n], *refs[2 * n:], slicers)

    return _sequencer_call(name, collective_id, body, srcs, [jax.ShapeDtypeStruct(s, d) for s, d in out_shapes],
                           _gather_scratch(n))


def _all_gather_balanced_async(name, collective_id, src, out_shape, slicer, rows):
    def body(src_ref, out_ref, send, recv, loc):
        x, y, c = _place()
        _handshake([(1 - x if dx else x, 1 - y if dy else y, 1 - c if dc else c)
                    for dx in (0, 1) for dy in (0, 1) for dc in (0, 1) if dx or dy or dc])
        _gather_copies_balanced(src_ref, out_ref, send, recv, loc, slicer, rows)

    return _sequencer_call(name, collective_id, body, [src], [jax.ShapeDtypeStruct(*out_shape)],
                           [pltpu.SemaphoreType.DMA((9,)), pltpu.SemaphoreType.DMA((9,)), pltpu.SemaphoreType.DMA((1,))])[0]


def _sequencer_call(name, collective_id, body, operands, out_types, scratch):
    return pl.kernel(
        body, out_type=out_types, mesh=plsc.ScalarSubcoreMesh(axis_name="sequencer", num_cores=1), name=name,
        scratch_types=scratch, compiler_params=pltpu.CompilerParams(collective_id=collective_id),
    )(*operands)


def _exchange_sibling(name, collective_id, grads, shard_fns, shard_shapes):
    n = len(grads)

    def body(*refs):
        g, land = refs[:n], refs[n:2 * n]
        send, recv = refs[2 * n:]
        x, y, c = _place()
        _handshake([(x, y, 1 - c)])
        remote = []
        for i in range(n):
            for q in range(4):
                cp = pltpu.make_async_remote_copy(
                    src_ref=shard_fns[i](g[i], 2 * q + (1 - c)), dst_ref=land[i].at[q], send_sem=send.at[4 * i + q],
                    recv_sem=recv.at[4 * i + q], device_id=(x, y, 1 - c), device_id_type=MESH_ID)
                cp.start()
                remote.append(cp)
        for cp in remote:
            cp.wait()

    return _sequencer_call(name, collective_id, body, grads, [jax.ShapeDtypeStruct((4, *s), F32) for s in shard_shapes],
                           [pltpu.SemaphoreType.DMA((4 * n,)), pltpu.SemaphoreType.DMA((4 * n,))])


def _exchange_chips(name, collective_id, parts):
    n = len(parts)

    def body(*refs):
        part, out = refs[:n], refs[n:2 * n]
        send, recv = refs[2 * n:]
        x, y, c = _place()
        _handshake([(1 - x, y, c), (x, 1 - y, c), (1 - x, 1 - y, c)])
        remote = []
        for i in range(n):
            for s in range(3):
                qx = 1 - x if (s + 1) // 2 else x
                qy = 1 - y if (s + 1) % 2 else y
                cp = pltpu.make_async_remote_copy(
                    src_ref=part[i].at[2 * qx + qy], dst_ref=out[i].at[s], send_sem=send.at[3 * i + s],
                    recv_sem=recv.at[3 * i + s], device_id=(qx, qy, c), device_id_type=MESH_ID)
                cp.start()
                remote.append(cp)
        for cp in remote:
            cp.wait()

    return _sequencer_call(name, collective_id, body, parts,
                           [jax.ShapeDtypeStruct((3, *p.shape[1:]), p.dtype) for p in parts],
                           [pltpu.SemaphoreType.DMA((3 * n,)), pltpu.SemaphoreType.DMA((3 * n,))])


def _chip_partial(name, core, grad, own_block, own_index, land):
    _, rows, cols = land.shape
    tr = own_block[-2]

    def body(core_ref, a_ref, b_ref, o_ref):
        o_ref[...] = (a_ref[...] + b_ref[...]).astype(BF16)

    spec = pl.BlockSpec((None, tr, cols), lambda q, r, c: (q, r, 0))
    return pl.pallas_call(
        body, name=name, out_shape=jax.ShapeDtypeStruct(land.shape, BF16),
        grid_spec=pltpu.PrefetchScalarGridSpec(
            num_scalar_prefetch=1, grid=(4, rows // tr),
            in_specs=[pl.BlockSpec(own_block, lambda q, r, c: own_index(q, r, c[0])), spec], out_specs=spec),
        compiler_params=_params([((tr, cols), F32)] * 2 + [((tr, cols), BF16)], sem=("arbitrary", "arbitrary")),
    )(core, grad, land)


def _adamw_math(w, g, m, v):
    m = ADAM_B1 * m + (1.0 - ADAM_B1) * g
    v = ADAM_B2 * v + (1.0 - ADAM_B2) * (g * g)
    m_hat = m / (1.0 - ADAM_B1 ** ADAM_STEP)
    v_hat = v / (1.0 - ADAM_B2 ** ADAM_STEP)
    delta = -ADAM_LR * (m_hat / (jnp.sqrt(v_hat) + ADAM_EPS) + ADAM_WD * w)
    return delta, m, v


def _adamw(name, chip, own, landed, w, m, v):
    _, rows, cols = own.shape
    tr = _tile(rows, 512) if rows % 512 == 0 else _tile(rows, 176)

    def body(chip_ref, own_ref, l_ref, w_ref, m_ref, v_ref, g_out, d_out, m_out, v_out):
        g = own_ref[...].astype(F32)
        for s in range(3):
            g = g + l_ref[s].astype(F32)
        delta, m_new, v_new = _adamw_math(w_ref[...], g, m_ref[...], v_ref[...])
        g_out[...] = g
        d_out[...] = delta
        m_out[...] = m_new
        v_out[...] = v_new

    spec = pl.BlockSpec((tr, cols), lambda r, c: (r, 0))
    return pl.pallas_call(
        body, name=name, out_shape=[jax.ShapeDtypeStruct((rows, cols), F32)] * 4,
        grid_spec=pltpu.PrefetchScalarGridSpec(
            num_scalar_prefetch=1, grid=(rows // tr,),
            in_specs=[pl.BlockSpec((None, tr, cols), lambda r, c: (c[0], r, 0)),
                      pl.BlockSpec((3, tr, cols), lambda r, c: (0, r, 0)), spec, spec, spec],
            out_specs=[spec] * 4),
        compiler_params=_params([((4, tr, cols), own.dtype)] + [((tr, cols), F32)] * 7, sem=("arbitrary",)),
    )(chip, own, landed, w, m, v)


def _swap_with_sibling(name, x):
    def body(x_ref, o_ref, send, recv):
        px, py, c = _place()
        cp = pltpu.make_async_remote_copy(src_ref=x_ref, dst_ref=o_ref, send_sem=send, recv_sem=recv,
                                          device_id=(px, py, 1 - c), device_id_type=MESH_ID)
        cp.start()
        cp.wait()

    return pl.pallas_call(
        body, name=name, out_shape=jax.ShapeDtypeStruct(x.shape, x.dtype), in_specs=[ANY], out_specs=ANY,
        scratch_shapes=[pltpu.SemaphoreType.DMA, pltpu.SemaphoreType.DMA],
    )(x)


def _proj_forward_own_chip(positions, x, gain, w_own, w_sibling):
    t = x.shape[0]
    tm = _tile(t, 1024)

    def body(pos_ref, x_ref, g_ref, wo_ref, ws_ref, o_ref, h_ref, ht_ref):
        @pl.when(pl.program_id(1) == 0)
        def _():
            _, xh = _rms_stats(x_ref[...])
            h = (xh * g_ref[...]).astype(BF16)
            h_ref[...] = h
            ht_ref[...] = h.T
            o_ref[...] = _dot(h, wo_ref[...])

        @pl.when(pl.program_id(1) == 1)
        def _():
            o_ref[...] = _dot(h_ref[...], ws_ref[...])

    tok = pl.BlockSpec((tm, D_MODEL), lambda m, k, pos: (m, 0))
    return pl.pallas_call(
        body, name="proj_fwd_own_chip",
        out_shape=[jax.ShapeDtypeStruct((N_DEV, t, D_MODEL), F32), jax.ShapeDtypeStruct((t, D_MODEL), BF16),
                   jax.ShapeDtypeStruct((D_MODEL, t), BF16)],
        grid_spec=pltpu.PrefetchScalarGridSpec(
            num_scalar_prefetch=1, grid=(t // tm, 2),
            in_specs=[tok, pl.BlockSpec((1, D_MODEL), lambda m, k, pos: (0, 0)), RESIDENT, RESIDENT],
            out_specs=[pl.BlockSpec((None, tm, D_MODEL), lambda m, k, pos: (pos[k], m, 0)), tok,
                       pl.BlockSpec((D_MODEL, tm), lambda m, k, pos: (0, m))]),
        compiler_params=_params([((tm, D_MODEL), F32)] * 2 + [((tm, D_MODEL), BF16)] * 2,
                                scratch=[((2, D_MODEL, D_MODEL), BF16)], temps=6 << 20, sem=("arbitrary", "arbitrary")),
    )(positions, x, gain, w_own, w_sibling)


def _proj_forward_other_chips(positions, proj, h, w_in_g):
    t = h.shape[0]
    tm = _tile(t, 2048)

    def body(pos_ref, _, h_ref, w_ref, o_ref):
        o_ref[...] = _dot(h_ref[...], w_ref[pos_ref[pl.program_id(1)]])

    return pl.pallas_call(
        body, name="proj_fwd_other_chips", out_shape=jax.ShapeDtypeStruct(proj.shape, F32),
        grid_spec=pltpu.PrefetchScalarGridSpec(
            num_scalar_prefetch=1, grid=(t // tm, N_DEV - 2),
            in_specs=[ANY, pl.BlockSpec((tm, D_MODEL), lambda m, k, pos: (m, 0)), RESIDENT],
            out_specs=pl.BlockSpec((None, tm, D_MODEL), lambda m, k, pos: (pos[k], m, 0))),
        input_output_aliases={1: 0},
        compiler_params=_params([((tm, D_MODEL), F32), ((tm, D_MODEL), BF16)], scratch=[((N_DEV, D_MODEL, D_MODEL), BF16)],
                                temps=6 << 20, sem=("arbitrary", "arbitrary")),
    )(positions, proj, h, w_in_g)


def _masked_ws(ws_ref, g):
    row = lax.broadcasted_iota(jnp.int32, (GMLP_CHUNK, GMLP_CHUNK), 0)
    col = lax.broadcasted_iota(jnp.int32, (GMLP_CHUNK, GMLP_CHUNK), 1)
    return jnp.where(row >= col, ws_ref[g], 0.0).astype(BF16)


def _gmlp_forward(proj, ln_g, ln_b, w_s, bias_b):
    t = proj.shape[1]
    tm = _tile(t, 512)
    chunks = tm // GMLP_CHUNK

    def body(u_ref, v_ref, lng_ref, lnb_ref, ws_ref, bias_ref, a_ref, vn_scr):
        vv = _gelu(v_ref[...])
        mu = jnp.mean(vv, axis=-1, keepdims=True)
        cen = vv - mu
        var = jnp.mean(cen * cen, axis=-1, keepdims=True)
        vn_scr[...] = ((cen * lax.rsqrt(var + NORM_EPS)) * lng_ref[...] + lnb_ref[...]).astype(BF16)
        for g in range(GROUPS):
            wm = _masked_ws(ws_ref, g)
            cols = slice(g * HEAD_DIM, (g + 1) * HEAD_DIM)
            for c in range(chunks):
                rows = slice(c * GMLP_CHUNK, (c + 1) * GMLP_CHUNK)
                mixed = _dot(wm, vn_scr[rows, cols]) + bias_ref[g]
                a_ref[rows, cols] = (_gelu(u_ref[rows, cols]) * mixed).astype(BF16)

    small = pl.BlockSpec((GROUPS, GMLP_CHUNK, GMLP_CHUNK), lambda m: (0, 0, 0))
    vec = pl.BlockSpec((1, D_MODEL), lambda m: (0, 0))
    return pl.pallas_call(
        body, name="gmlp_fwd", out_shape=jax.ShapeDtypeStruct((t, D_MODEL), BF16), grid=(t // tm,),
        in_specs=[pl.BlockSpec((None, tm, D_MODEL), lambda m: (U_POS, m, 0)),
                  pl.BlockSpec((None, tm, D_MODEL), lambda m: (U_POS + 1, m, 0)), vec, vec, small, small],
        out_specs=pl.BlockSpec((tm, D_MODEL), lambda m: (m, 0)),
        scratch_shapes=[pltpu.VMEM((tm, D_MODEL), BF16)],
        compiler_params=_params([((tm, D_MODEL), F32)] * 2 + [((tm, D_MODEL), BF16)] + [((8, 128, 128), F32)] * 2,
                                scratch=[((tm, D_MODEL), BF16)], temps=8 << 20, sem=("arbitrary",)),
    )(proj, proj, ln_g, ln_b, w_s, bias_b)


def _lower_bound(tab_ref):
    t0, t1 = tab_ref[0:1, :], tab_ref[1:2, :]
    mx = jnp.maximum(t0, t1)
    e0, e1 = jnp.exp(t0 - mx), jnp.exp(t1 - mx)
    return e0 / (e0 + e1)


def _tri_masks():
    row = lax.broadcasted_iota(jnp.int32, (HGRN_CHUNK, HGRN_CHUNK), 0)
    col = lax.broadcasted_iota(jnp.int32, (HGRN_CHUNK, HGRN_CHUNK), 1)
    return row >= col, row <= col


def _chunk_rows(c):
    return slice(c * HGRN_CHUNK, (c + 1) * HGRN_CHUNK)


def _per_chunk(x, nc, fn):
    return jnp.concatenate([fn(x[_chunk_rows(c)]) for c in range(nc)], axis=0)


def _chunk_row_bcast(x, nc, i):
    return _per_chunk(x, nc, lambda xc: jnp.broadcast_to(xc[i:i + 1, :], (HGRN_CHUNK, HEAD_DIM)))


def _hgrn_gates(q, fl, lb, nc):
    lower, _ = _tri_masks()
    lower = lower.astype(BF16)
    s = _sigmoid(fl)
    f = lb + (1.0 - lb) * s
    k = 1.0 - f
    hi, mid, lo = _split3(jnp.log(f))
    a = jnp.concatenate([_dot(lower, hi[_chunk_rows(c)]) + _dot(lower, mid[_chunk_rows(c)]) + _dot(lower, lo[_chunk_rows(c)])
                         for c in range(nc)], axis=0)
    a_mid = _chunk_row_bcast(a, nc, HGRN_CHUNK // 2 - 1)
    a_last = _chunk_row_bcast(a, nc, HGRN_CHUNK - 1)
    qs = q * HGRN_SCALE
    e_in, e_out, e_end, e_all = jnp.exp(a - a_mid), jnp.exp(a_mid - a), jnp.exp(a_last - a), jnp.exp(a)
    decay = [jnp.exp(a[c * HGRN_CHUNK + HGRN_CHUNK - 1:(c + 1) * HGRN_CHUNK, :]) for c in range(nc)]
    return dict(s=s, f=f, k=k, decay=decay, e_in=e_in, e_out=e_out, e_end=e_end, e_all=e_all,
                qi=qs * e_in, ki=k * e_out, kd=k * e_end, qe=qs * e_all)


def _hgrn_forward(proj, lb_table, norm_g):
    t = proj.shape[1]
    tb = _tile(t, 1024)
    nc = tb // HGRN_CHUNK
    n_chunks = t // HGRN_CHUNK

    def body(q_ref, f_ref, i_ref, g_ref, tab_ref, ng_ref, og_ref, o_ref, st_ref, state):
        @pl.when(pl.program_id(1) == 0)
        def _():
            state[...] = jnp.zeros_like(state)

        lower, _ = _tri_masks()
        gt = _hgrn_gates(q_ref[...], f_ref[...], _lower_bound(tab_ref), nc)
        qi, ki, kd, qe = (gt[n].astype(BF16) for n in ("qi", "ki", "kd", "qe"))
        vb = i_ref[...].astype(BF16)
        o_intra, d_state = [], []
        for c in range(nc):
            rows = _chunk_rows(c)
            p = jnp.where(lower, _dot_nt(qi[rows], ki[rows]), 0.0).astype(BF16)
            o_intra.append(_dot(p, vb[rows]))
            d_state.append(_dot_tn(vb[rows], kd[rows]))
        st = state[...]
        outs = []
        for c in range(nc):
            st_ref[c] = st
            outs.append(o_intra[c] + _dot_nt(qe[_chunk_rows(c)], st.astype(BF16)))
            st = st * gt["decay"][c] + d_state[c]
        state[...] = st
        o = jnp.concatenate(outs, axis=0)
        o_ref[...] = o
        _, oh = _rms_stats(o)
        gz = g_ref[...]
        og_ref[...] = ((oh * ng_ref[...]) * (gz * _sigmoid(gz))).astype(BF16)

    def blk(p):
        return pl.BlockSpec((None, tb, HEAD_DIM), lambda h, n: (p, n, h))

    out_blk = pl.BlockSpec((tb, HEAD_DIM), lambda h, n: (n, h))
    return pl.pallas_call(
        body, name="hgrn_fwd",
        out_shape=[jax.ShapeDtypeStruct((t, D_MODEL), BF16), jax.ShapeDtypeStruct((t, D_MODEL), F32),
                   jax.ShapeDtypeStruct((HEADS, n_chunks, HEAD_DIM, HEAD_DIM), F32)],
        grid=(HEADS, t // tb),
        in_specs=[blk(Q_POS), blk(Q_POS + 1), blk(Q_POS + 2), blk(Q_POS + 3),
                  pl.BlockSpec((2, HEAD_DIM), lambda h, n: (0, h)), pl.BlockSpec((1, HEAD_DIM), lambda h, n: (0, h))],
        out_specs=[out_blk, out_blk, pl.BlockSpec((None, nc, HEAD_DIM, HEAD_DIM), lambda h, n: (h, n, 0, 0))],
        scratch_shapes=[pltpu.VMEM((HEAD_DIM, HEAD_DIM), F32)],
        compiler_params=_params([((tb, HEAD_DIM), F32)] * 6 + [((nc, HEAD_DIM, HEAD_DIM), F32)], temps=8 << 20,
                                sem=("arbitrary", "arbitrary")),
    )(proj, proj, proj, proj, lb_table, norm_g)


def _branch_out_forward(a, og, proj, x, w_a, w_b, w_out, ffn_g):
    t = x.shape[0]
    tm = _tile(t, 512)

    def body(a_ref, og_ref, ga_ref, gb_ref, x_ref, wa_ref, wb_ref, wo_ref, g_ref, ya_ref, yb_ref, mg_ref, x1_ref, h2_ref):
        ya = _dot(a_ref[...], wa_ref[...])
        yb = _dot(og_ref[...], wb_ref[...])
        ya_ref[...] = ya
        yb_ref[...] = yb
        merged = (_sigmoid(ga_ref[...]) * ya + _sigmoid(gb_ref[...]) * yb).astype(BF16)
        mg_ref[...] = merged
        x1 = x_ref[...] + _dot(merged, wo_ref[...])
        x1_ref[...] = x1
        _, xh = _rms_stats(x1)
        h2_ref[...] = (xh * g_ref[...]).astype(BF16)

    tok = pl.BlockSpec((tm, D_MODEL), lambda m: (m, 0))
    return pl.pallas_call(
        body, name="branch_out_fwd",
        out_shape=[jax.ShapeDtypeStruct((t, D_MODEL), F32), jax.ShapeDtypeStruct((t, D_MODEL), F32),
                   jax.ShapeDtypeStruct((t, D_MODEL), BF16), jax.ShapeDtypeStruct((t, D_MODEL), F32),
                   jax.ShapeDtypeStruct((t, D_MODEL), BF16)],
        grid=(t // tm,),
        in_specs=[tok, tok, pl.BlockSpec((None, tm, D_MODEL), lambda m: (GATE_POS, m, 0)),
                  pl.BlockSpec((None, tm, D_MODEL), lambda m: (GATE_POS + 1, m, 0)), tok, RESIDENT, RESIDENT, RESIDENT,
                  pl.BlockSpec((1, D_MODEL), lambda m: (0, 0))],
        out_specs=[tok] * 5,
        compiler_params=_params([((tm, D_MODEL), BF16)] * 4 + [((tm, D_MODEL), F32)] * 6, scratch=[((D_MODEL, D_MODEL), BF16)] * 3,
                                temps=8 << 20, sem=("arbitrary",)),
    )(a, og, proj, proj, x, w_a, w_b, w_out, ffn_g)


def _ffn_forward(h2, x1, w_gu, w_down, target, final_g):
    t = x1.shape[0]
    tm = _tile(t, 512)

    def body(h_ref, wgu_ref, wd_ref, x1_ref, t_ref, g_ref, gu_ref, act_ref, loss_ref, dg_ref, dx_ref, dxb_ref, acc):
        m, j = pl.program_id(0), pl.program_id(1)

        @pl.when((m == 0) & (j == 0))
        def _():
            loss_ref[...] = jnp.zeros_like(loss_ref)
            dg_ref[...] = jnp.zeros_like(dg_ref)

        h = h_ref[...]
        gate = _dot_nt(h, wgu_ref[j])
        up = _dot_nt(h, wgu_ref[j + 4])
        gu_ref[0] = gate
        gu_ref[1] = up
        act = ((gate * _sigmoid(gate)) * up).astype(BF16)
        act_ref[...] = act
        part = _dot(act, wd_ref[j])

        @pl.when(j == 0)
        def _():
            acc[...] = part

        @pl.when((j > 0) & (j < 3))
        def _():
            acc[...] += part

        @pl.when(j == 3)
        def _():
            x2 = x1_ref[...] + (acc[...] + part)
            g = g_ref[...]
            r, xh = _rms_stats(x2)
            err = xh * g - t_ref[...]
            loss_ref[...] += 0.5 * jnp.sum(jnp.mean(err * err, axis=-1, keepdims=True), axis=0, keepdims=True)
            dy = err * (1.0 / D_MODEL)
            dg_ref[...] += jnp.sum(dy * xh, axis=0, keepdims=True)
            dxh = dy * g
            dx = r * (dxh - xh * jnp.mean(dxh * xh, axis=-1, keepdims=True))
            dx_ref[...] = dx
            dxb_ref[...] = dx.astype(BF16)

    tok = pl.BlockSpec((tm, D_MODEL), lambda m, j: (m, 0))
    vec = pl.BlockSpec((1, D_MODEL), lambda m, j: (0, 0))
    return pl.pallas_call(
        body, name="ffn_fwd",
        out_shape=[jax.ShapeDtypeStruct((4, 2, t, FF_BLOCK), F32), jax.ShapeDtypeStruct((4, t, FF_BLOCK), BF16),
                   jax.ShapeDtypeStruct((8, 128), F32), jax.ShapeDtypeStruct((1, D_MODEL), F32),
                   jax.ShapeDtypeStruct((t, D_MODEL), F32), jax.ShapeDtypeStruct((t, D_MODEL), BF16)],
        grid=(t // tm, 4),
        in_specs=[tok, RESIDENT, RESIDENT, tok, tok, vec],
        out_specs=[pl.BlockSpec((None, 2, tm, FF_BLOCK), lambda m, j: (j, 0, m, 0)),
                   pl.BlockSpec((None, tm, FF_BLOCK), lambda m, j: (j, m, 0)),
                   pl.BlockSpec((8, 128), lambda m, j: (0, 0)), vec, tok, tok],
        scratch_shapes=[pltpu.VMEM((tm, D_MODEL), F32)],
        compiler_params=_params([((tm, D_MODEL), BF16), ((tm, D_MODEL), F32), ((tm, D_MODEL), F32), ((2, tm, 768), F32),
                                 ((tm, 768), BF16), ((tm, D_MODEL), F32), ((tm, D_MODEL), BF16)],
                                scratch=[((tm, D_MODEL), F32), ((N_DEV, FF_BLOCK, D_MODEL), BF16), ((D_FF, D_MODEL), BF16)],
                                temps=6 << 20, sem=("arbitrary", "arbitrary")),
    )(h2, w_gu, w_down.reshape(4, FF_BLOCK, D_MODEL), x1, target, final_g)


def _ffn_backward(dx2b, dx2, gu, x1, w_gu, w_down, ffn_g):
    t = x1.shape[0]
    tm = _tile(t, 512)

    def body(dxb_ref, dx2_ref, gu_ref, x1_ref, wgu_ref, wd_ref, g_ref, dgu_ref, dx1_ref, dx1b_ref, dg_ref, acc, prev):
        m, j = pl.program_id(0), pl.program_id(1)

        @pl.when((m == 0) & (j == 0))
        def _():
            dg_ref[...] = jnp.zeros_like(dg_ref)

        @pl.when(j == 0)
        def _():
            prev[...] = jnp.zeros_like(prev)
            acc[...] = jnp.zeros_like(acc)

        jm1 = jnp.maximum(j - 1, 0)
        acc[...] += _dot(prev[0], wgu_ref[jm1]) + _dot(prev[1], wgu_ref[jm1 + 4])
        dact = _dot_nt(dxb_ref[...], wd_ref[j])
        gate, up = gu_ref[0], gu_ref[1]
        sg = _sigmoid(gate)
        dgate = (dact * up * (sg * (1.0 + gate * (1.0 - sg)))).astype(BF16)
        dup = (dact * (gate * sg)).astype(BF16)
        dgu_ref[0] = dgate
        dgu_ref[1] = dup
        prev[0] = dgate
        prev[1] = dup

        @pl.when(j == 3)
        def _():
            dh2 = acc[...] + (_dot(prev[0], wgu_ref[3]) + _dot(prev[1], wgu_ref[7]))
            dx, dg = _rms_bwd(dh2, x1_ref[...], g_ref[...])
            dx1 = dx2_ref[...] + dx
            dx1_ref[...] = dx1
            dx1b_ref[...] = dx1.astype(BF16)
            dg_ref[...] += dg

    tok = pl.BlockSpec((tm, D_MODEL), lambda m, j: (m, 0))
    vec = pl.BlockSpec((1, D_MODEL), lambda m, j: (0, 0))
    gu_spec = pl.BlockSpec((None, 2, tm, FF_BLOCK), lambda m, j: (j, 0, m, 0))
    return pl.pallas_call(
        body, name="ffn_bwd",
        out_shape=[jax.ShapeDtypeStruct((4, 2, t, FF_BLOCK), BF16), jax.ShapeDtypeStruct((t, D_MODEL), F32),
                   jax.ShapeDtypeStruct((t, D_MODEL), BF16), jax.ShapeDtypeStruct((1, D_MODEL), F32)],
        grid=(t // tm, 4),
        in_specs=[tok, tok, gu_spec, tok, RESIDENT, RESIDENT, vec],
        out_specs=[gu_spec, tok, tok, vec],
        scratch_shapes=[pltpu.VMEM((tm, D_MODEL), F32), pltpu.VMEM((2, tm, FF_BLOCK), BF16)],
        compiler_params=_params([((tm, D_MODEL), BF16), ((tm, D_MODEL), F32), ((2, tm, 768), F32), ((tm, D_MODEL), F32),
                                 ((2, tm, 768), BF16), ((tm, D_MODEL), F32), ((tm, D_MODEL), BF16)],
                                scratch=[((tm, D_MODEL), F32), ((2, tm, 768), BF16), ((N_DEV, FF_BLOCK, D_MODEL), BF16),
                                         ((D_FF, D_MODEL), BF16)],
                                temps=4 << 20, sem=("arbitrary", "arbitrary")),
    )(dx2b, dx2, gu, x1, w_gu, w_down.reshape(4, FF_BLOCK, D_MODEL), ffn_g)


def _branch_out_backward(dx1b, ya, yb, proj, w_a, w_b, w_out):
    t = ya.shape[0]
    tm = _tile(t, 512)

    def body(dx_ref, ya_ref, yb_ref, ga_ref, gb_ref, wa_ref, wb_ref, wo_ref, dya_ref, dyb_ref, dgate_ref, da_ref, dog_ref):
        dm = _dot_nt(dx_ref[...], wo_ref[...])
        sa, sb = _sigmoid(ga_ref[...]), _sigmoid(gb_ref[...])
        dya = (dm * sa).astype(BF16)
        dyb = (dm * sb).astype(BF16)
        dya_ref[...] = dya
        dyb_ref[...] = dyb
        dgate_ref[0] = (dm * ya_ref[...] * (sa * (1.0 - sa))).astype(BF16)
        dgate_ref[1] = (dm * yb_ref[...] * (sb * (1.0 - sb))).astype(BF16)
        da_ref[...] = _dot_nt(dya, wa_ref[...])
        dog_ref[...] = _dot_nt(dyb, wb_ref[...])

    tok = pl.BlockSpec((tm, D_MODEL), lambda m: (m, 0))
    return pl.pallas_call(
        body, name="branch_out_bwd",
        out_shape=[jax.ShapeDtypeStruct((t, D_MODEL), BF16), jax.ShapeDtypeStruct((t, D_MODEL), BF16),
                   jax.ShapeDtypeStruct((N_DEV, t, D_MODEL), BF16), jax.ShapeDtypeStruct((t, D_MODEL), F32),
                   jax.ShapeDtypeStruct((t, D_MODEL), F32)],
        grid=(t // tm,),
        in_specs=[tok, tok, tok, pl.BlockSpec((None, tm, D_MODEL), lambda m: (GATE_POS, m, 0)),
                  pl.BlockSpec((None, tm, D_MODEL), lambda m: (GATE_POS + 1, m, 0)), RESIDENT, RESIDENT, RESIDENT],
        out_specs=[tok, tok, pl.BlockSpec((2, tm, D_MODEL), lambda m: (GATE_POS // 2, m, 0)), tok, tok],
        compiler_params=_params([((tm, D_MODEL), BF16)] * 5 + [((tm, D_MODEL), F32)] * 6, scratch=[((D_MODEL, D_MODEL), BF16)] * 3,
                                temps=8 << 20, sem=("arbitrary",)),
    )(dx1b, ya, yb, proj, proj, w_a, w_b, w_out)


def _hgrn_backward(dproj, dog, o_saved, states, proj, lb_table, norm_g):
    t = proj.shape[1]
    tb = _tile(t, 1024)
    nc = tb // HGRN_CHUNK
    nb = t // tb

    def body(_, dog_ref, o_ref, st_ref, q_ref, f_ref, i_ref, g_ref, tab_ref, ng_ref, dp_ref, dng_ref, dtab_ref, gstate):
        @pl.when(pl.program_id(1) == 0)
        def _():
            gstate[...] = jnp.zeros_like(gstate)
            dng_ref[...] = jnp.zeros_like(dng_ref)
            dtab_ref[...] = jnp.zeros_like(dtab_ref)

        lb = _lower_bound(tab_ref)
        ng = ng_ref[...]
        lower, upper = _tri_masks()
        gt = _hgrn_gates(q_ref[...], f_ref[...], lb, nc)
        qi, ki, kd, qe = (gt[n].astype(BF16) for n in ("qi", "ki", "kd", "qe"))
        vb = i_ref[...].astype(BF16)
        o, gz, d_og = o_ref[...], g_ref[...], dog_ref[...]
        r, oh = _rms_stats(o)
        sg = _sigmoid(gz)
        d_on = d_og * (gz * sg)
        dgz = d_og * (oh * ng) * (sg * (1.0 + gz * (1.0 - sg)))
        dng_ref[...] += jnp.sum(d_on * oh, axis=0, keepdims=True)
        doh = d_on * ng
        dob = (r * (doh - oh * jnp.mean(doh * oh, axis=-1, keepdims=True))).astype(BF16)
        dv_intra, dqi, dki, dqe, g_upd = [], [], [], [], []
        for c in range(nc):
            rows = _chunk_rows(c)
            p = jnp.where(lower, _dot_nt(qi[rows], ki[rows]), 0.0).astype(BF16)
            dv_intra.append(_dot_tn(p, dob[rows]))
            dp = jnp.where(lower, _dot_nt(dob[rows], vb[rows]), 0.0).astype(BF16)
            dqi.append(_dot(dp, ki[rows]))
            dki.append(_dot_tn(dp, qi[rows]))
            dqe.append(_dot(dob[rows], st_ref[c].astype(BF16)))
            g_upd.append(_dot_tn(dob[rows], qe[rows]))
        g_after = [None] * nc
        g = gstate[...]
        for c in reversed(range(nc)):
            g_after[c] = g
            g = g * gt["decay"][c] + g_upd[c]
        gstate[...] = g
        dkd, dv, da_last = [], [], []
        for c in range(nc):
            rows = _chunk_rows(c)
            gb = g_after[c].astype(BF16)
            dkd.append(_dot(vb[rows], gb))
            dv.append(dv_intra[c] + _dot_nt(kd[rows], gb))
            da_last.append(jnp.sum(g_after[c] * st_ref[c], axis=0, keepdims=True) * gt["decay"][c])
        dqi, dki, dqe, dkd, dv = (jnp.concatenate(z, axis=0) for z in (dqi, dki, dqe, dkd, dv))
        dqs = dqi * gt["e_in"] + dqe * gt["e_all"]
        dk = dki * gt["e_out"] + dkd * gt["e_end"]
        t_in, t_out, t_end = dqi * gt["qi"], dki * gt["ki"], dkd * gt["kd"]
        da = t_in - t_out + dqe * gt["qe"] - t_end
        row = lax.broadcasted_iota(jnp.int32, (HGRN_CHUNK, HEAD_DIM), 0)
        d_mid = t_out - t_in
        pieces = []
        for c in range(nc):
            rows = _chunk_rows(c)
            da_mid = jnp.sum(d_mid[rows], axis=0, keepdims=True)
            da_end = jnp.sum(t_end[rows], axis=0, keepdims=True) + da_last[c]
            da_c = da[rows] + jnp.where(row == HGRN_CHUNK // 2 - 1, da_mid, 0.0) + jnp.where(row == HGRN_CHUNK - 1, da_end, 0.0)
            pieces.append(_mask_mm(upper.astype(BF16), da_c))
        df = jnp.concatenate(pieces, axis=0) / gt["f"] - dk
        s = gt["s"]
        dlb = jnp.sum(df * (1.0 - s), axis=0, keepdims=True)
        dp_ref[0] = (dqs * HGRN_SCALE).astype(BF16)
        dp_ref[1] = (df * (1.0 - lb) * (s * (1.0 - s))).astype(BF16)
        dp_ref[2] = dv.astype(BF16)
        dp_ref[3] = dgz.astype(BF16)
        dt0 = dlb * (lb * (1.0 - lb))
        dtab_ref[0:1, :] += dt0
        dtab_ref[1:2, :] -= dt0

    def blk(p):
        return pl.BlockSpec((None, tb, HEAD_DIM), lambda h, n: (p, nb - 1 - n, h))

    tok = pl.BlockSpec((tb, HEAD_DIM), lambda h, n: (nb - 1 - n, h))
    return pl.pallas_call(
        body, name="hgrn_bwd",
        out_shape=[jax.ShapeDtypeStruct((N_DEV, t, D_MODEL), BF16), jax.ShapeDtypeStruct((1, D_MODEL), F32),
                   jax.ShapeDtypeStruct((2, D_MODEL), F32)],
        grid=(HEADS, nb),
        in_specs=[ANY, tok, tok, pl.BlockSpec((None, nc, HEAD_DIM, HEAD_DIM), lambda h, n: (h, nb - 1 - n, 0, 0)),
                  blk(Q_POS), blk(Q_POS + 1), blk(Q_POS + 2), blk(Q_POS + 3),
                  pl.BlockSpec((2, HEAD_DIM), lambda h, n: (0, h)), pl.BlockSpec((1, HEAD_DIM), lambda h, n: (0, h))],
        out_specs=[pl.BlockSpec((4, tb, HEAD_DIM), lambda h, n: (0, nb - 1 - n, h)),
                   pl.BlockSpec((1, HEAD_DIM), lambda h, n: (0, h)), pl.BlockSpec((2, HEAD_DIM), lambda h, n: (0, h))],
        scratch_shapes=[pltpu.VMEM((HEAD_DIM, HEAD_DIM), F32)],
        input_output_aliases={0: 0},
        compiler_params=_params([((tb, HEAD_DIM), F32)] * 6 + [((nc, HEAD_DIM, HEAD_DIM), F32)] + [((4, tb, HEAD_DIM), BF16)],
                                temps=8 << 20, sem=("arbitrary", "arbitrary")),
    )(dproj, dog, o_saved, states, proj, proj, proj, proj, lb_table, norm_g)


def _gmlp_backward(dproj, da, proj, ln_g, ln_b, w_s, bias_b):
    t = proj.shape[1]
    tm = _tile(t, 256)
    chunks = tm // GMLP_CHUNK

    def body(_, da_ref, u_ref, v_ref, lng_ref, lnb_ref, ws_ref, bias_ref, dp_ref, dlng_ref, dlnb_ref, dws_ref, dbs_ref,
             vn_scr, dvn_scr):
        @pl.when(pl.program_id(0) == 0)
        def _():
            dlng_ref[...] = jnp.zeros_like(dlng_ref)
            dlnb_ref[...] = jnp.zeros_like(dlnb_ref)
            dws_ref[...] = jnp.zeros_like(dws_ref)
            dbs_ref[...] = jnp.zeros_like(dbs_ref)

        v = v_ref[...]
        vv, dvv_dv = _gelu_and_grad(v)
        mu = jnp.mean(vv, axis=-1, keepdims=True)
        cen = vv - mu
        rstd = lax.rsqrt(jnp.mean(cen * cen, axis=-1, keepdims=True) + NORM_EPS)
        vhat = cen * rstd
        lng = lng_ref[...]
        vn_scr[...] = (vhat * lng + lnb_ref[...]).astype(BF16)
        row = lax.broadcasted_iota(jnp.int32, (GMLP_CHUNK, GMLP_CHUNK), 0)
        col = lax.broadcasted_iota(jnp.int32, (GMLP_CHUNK, GMLP_CHUNK), 1)
        for g in range(GROUPS):
            wm = _masked_ws(ws_ref, g)
            cols = slice(g * HEAD_DIM, (g + 1) * HEAD_DIM)
            dws = jnp.zeros((GMLP_CHUNK, GMLP_CHUNK), F32)
            dbs = jnp.zeros((GMLP_CHUNK, GMLP_CHUNK), F32)
            for c in range(chunks):
                rows = slice(c * GMLP_CHUNK, (c + 1) * GMLP_CHUNK)
                vn = vn_scr[rows, cols]
                mixed = _dot(wm, vn) + bias_ref[g]
                u = u_ref[rows, cols]
                d_a = da_ref[rows, cols]
                gelu_u, dgelu_u = _gelu_and_grad(u)
                dp_ref[0, rows, cols] = (d_a * mixed * dgelu_u).astype(BF16)
                dmix = d_a * gelu_u
                dmb = dmix.astype(BF16)
                dbs = dbs + dmix
                dws = dws + _dot_nt(dmb, vn)
                dvn_scr[rows, cols] = _dot_tn(wm, dmb)
            dws_ref[g] += jnp.where(row >= col, dws, 0.0)
            dbs_ref[g] += jnp.broadcast_to(jnp.sum(dbs, axis=-1, keepdims=True), (GMLP_CHUNK, GMLP_CHUNK))
        dvn = dvn_scr[...]
        dlng_ref[...] += jnp.sum(dvn * vhat, axis=0, keepdims=True)
        dlnb_ref[...] += jnp.sum(dvn, axis=0, keepdims=True)
        dvh = dvn * lng
        dvv = rstd * (dvh - jnp.mean(dvh, axis=-1, keepdims=True) - vhat * jnp.mean(dvh * vhat, axis=-1, keepdims=True))
        dp_ref[1] = (dvv * dvv_dv).astype(BF16)

    tok = pl.BlockSpec((tm, D_MODEL), lambda m: (m, 0))
    small = pl.BlockSpec((GROUPS, GMLP_CHUNK, GMLP_CHUNK), lambda m: (0, 0, 0))
    vec = pl.BlockSpec((1, D_MODEL), lambda m: (0, 0))
    return pl.pallas_call(
        body, name="gmlp_bwd",
        out_shape=[jax.ShapeDtypeStruct(dproj.shape, BF16), jax.ShapeDtypeStruct((1, D_MODEL), F32),
                   jax.ShapeDtypeStruct((1, D_MODEL), F32), jax.ShapeDtypeStruct((GROUPS, GMLP_CHUNK, GMLP_CHUNK), F32),
                   jax.ShapeDtypeStruct((GROUPS, GMLP_CHUNK, GMLP_CHUNK), F32)],
        grid=(t // tm,),
        in_specs=[ANY, tok, pl.BlockSpec((None, tm, D_MODEL), lambda m: (U_POS, m, 0)),
                  pl.BlockSpec((None, tm, D_MODEL), lambda m: (U_POS + 1, m, 0)), vec, vec, small, small],
        out_specs=[pl.BlockSpec((2, tm, D_MODEL), lambda m: (U_POS // 2, m, 0)), vec, vec, small, small],
        scratch_shapes=[pltpu.VMEM((tm, D_MODEL), BF16), pltpu.VMEM((tm, D_MODEL), F32)],
        input_output_aliases={0: 0},
        compiler_params=_params([((tm, D_MODEL), F32)] * 3 + [((2, tm, D_MODEL), BF16)] + [((8, 128, 128), F32)] * 4,
                                scratch=[((tm, D_MODEL), BF16), ((tm, D_MODEL), F32)], temps=12 << 20, sem=("arbitrary",)),
    )(dproj, da, proj, proj, ln_g, ln_b, w_s, bias_b)


def _input_backward(dproj, w_in_g, x, dx1, mix_g):
    t = x.shape[0]
    tm = _tile(t, 512)

    def body(dp_ref, w_ref, x_ref, dx1_ref, g_ref, dx_ref, dg_ref):
        @pl.when(pl.program_id(0) == 0)
        def _():
            dg_ref[...] = jnp.zeros_like(dg_ref)

        dh = _dot_nt(dp_ref[0], w_ref[0])
        for p in range(1, N_DEV):
            dh = dh + _dot_nt(dp_ref[p], w_ref[p])
        dx, dg = _rms_bwd(dh, x_ref[...], g_ref[...])
        dx_ref[...] = dx1_ref[...] + dx
        dg_ref[...] += dg

    tok = pl.BlockSpec((tm, D_MODEL), lambda m: (m, 0))
    vec = pl.BlockSpec((1, D_MODEL), lambda m: (0, 0))
    return pl.pallas_call(
        body, name="input_bwd",
        out_shape=[jax.ShapeDtypeStruct((t, D_MODEL), F32), jax.ShapeDtypeStruct((1, D_MODEL), F32)],
        grid=(t // tm,),
        in_specs=[pl.BlockSpec((N_DEV, tm, D_MODEL), lambda m: (0, m, 0)), RESIDENT, tok, tok, vec],
        out_specs=[tok, vec],
        compiler_params=_params([((N_DEV, tm, D_MODEL), BF16)] + [((tm, D_MODEL), F32)] * 3,
                                scratch=[((N_DEV, D_MODEL, D_MODEL), BF16)], temps=6 << 20, sem=("arbitrary",)),
    )(dproj, w_in_g, x, dx1, mix_g)


def _weight_grad(name, a, b, a_spec, b_spec, out_shape, out_spec, steps, blocks, a_is_transposed):
    def body(a_ref, b_ref, o_ref):
        o_ref[...] = _dot(a_ref[...], b_ref[...]) if a_is_transposed else _dot_tn(a_ref[...], b_ref[...])

    return pl.pallas_call(
        body, name=name, out_shape=jax.ShapeDtypeStruct(out_shape, F32), grid=(steps,), in_specs=[a_spec, b_spec],
        out_specs=out_spec, compiler_params=_params(blocks, temps=4 << 20, sem=("arbitrary",)),
    )(a, b)


def _pack_small(mix_g, ln_g, ln_b, b_s, lb_table, hg_norm, ffn_g, final_g, loss_row):
    def part(a):
        a = a.reshape(-1, D_MODEL)
        return jnp.pad(a, ((0, 8 - a.shape[0]), (0, 0)))

    return jnp.concatenate([part(mix_g), part(ln_g), part(ln_b), part(hg_norm), part(ffn_g), part(final_g),
                            part(lb_table), part(b_s), part(loss_row)], axis=0)


SMALL_PARTS = (("gmlp_ln_g", 8, 1), ("gmlp_ln_b", 16, 1), ("hgrn_norm_g", 24, 1), ("norm_ffn_g", 32, 1), ("norm_final_g", 40, 1),
               ("hgrn_lb_table", 48, 2))


def _adamw_small_unpacked(gathered, w, m, v):
    rows = w.shape[0]
    n_out = len(SMALL_PARTS) + 1

    def body(p_ref, w_ref, m_ref, v_ref, *outs):
        g = p_ref[0]
        for j in range(1, N_DEV):
            g = g + p_ref[j]
        delta, m_new, v_new = _adamw_math(w_ref[...], g, m_ref[...], v_ref[...])
        for kind, val in enumerate((g, delta, m_new, v_new)):
            refs = outs[kind * n_out:(kind + 1) * n_out]
            for (_, first, count), ref in zip(SMALL_PARTS, refs):
                ref[...] = val[first:first + count]
            for grp in range(GROUPS):
                refs[-1][0, grp:grp + 1, :] = val[56:57, grp * GMLP_CHUNK:(grp + 1) * GMLP_CHUNK]
        outs[-1][...] = g[SMALL_ROWS - 8:SMALL_ROWS - 7]

    shapes = [jax.ShapeDtypeStruct((count, D_MODEL), F32) for _, _, count in SMALL_PARTS]
    shapes.append(jax.ShapeDtypeStruct((1, GROUPS, GMLP_CHUNK), F32))
    whole = pl.BlockSpec((rows, D_MODEL), lambda: (0, 0))
    res = pl.pallas_call(
        body, name="adamw_small", out_shape=shapes * 4 + [jax.ShapeDtypeStruct((1, D_MODEL), F32)],
        in_specs=[pl.BlockSpec((N_DEV, rows, D_MODEL), lambda: (0, 0, 0)), whole, whole, whole],
        compiler_params=_params([((N_DEV, rows, D_MODEL), F32)] + [((rows, D_MODEL), F32)] * 7),
    )(gathered, w, m, v)
    names = [nme for nme, _, _ in SMALL_PARTS] + ["gmlp_b_s"]
    return [dict(zip(names, res[kind * n_out:(kind + 1) * n_out])) for kind in range(4)], res[-1]


def _adamw_row(name, gathered, w, m, v):
    def body(p_ref, w_ref, m_ref, v_ref, g_out, d_out, m_out, v_out):
        g = p_ref[0, 0:1, :]
        for j in range(1, N_DEV):
            g = g + p_ref[j, 0:1, :]
        delta, m_new, v_new = _adamw_math(w_ref[...], g, m_ref[...], v_ref[...])
        g_out[...] = g
        d_out[...] = delta
        m_out[...] = m_new
        v_out[...] = v_new

    return pl.pallas_call(
        body, name=name, out_shape=[jax.ShapeDtypeStruct((1, D_MODEL), F32)] * 4,
        compiler_params=_params([((N_DEV, 8, D_MODEL), F32)] + [((8, D_MODEL), F32)] * 7),
    )(gathered, w, m, v)


def _adamw_small(name, gathered, w, m, v):
    rows, cols = w.shape

    def body(p_ref, w_ref, m_ref, v_ref, g_out, d_out, m_out, v_out):
        g = p_ref[0]
        for j in range(1, N_DEV):
            g = g + p_ref[j]
        delta, m_new, v_new = _adamw_math(w_ref[...], g, m_ref[...], v_ref[...])
        g_out[...] = g
        d_out[...] = delta
        m_out[...] = m_new
        v_out[...] = v_new

    tr = _tile(rows, 512)
    spec = pl.BlockSpec((tr, cols), lambda r: (r, 0))
    return pl.pallas_call(
        body, name=name, out_shape=[jax.ShapeDtypeStruct((rows, cols), F32)] * 4, grid=(rows // tr,),
        in_specs=[pl.BlockSpec((N_DEV, tr, cols), lambda r: (0, r, 0)), spec, spec, spec], out_specs=[spec] * 4,
        compiler_params=_params([((N_DEV, tr, cols), F32)] + [((tr, cols), F32)] * 7, sem=("arbitrary",)),
    )(gathered, w, m, v)


def kernel(x, norm_mix_g, w_in, gmlp_ln_g, gmlp_ln_b, gmlp_w_s, gmlp_b_s, hgrn_lb_table, hgrn_norm_g, w_branch_a, w_branch_b, w_out, norm_ffn_g, w_gate_up, w_down, norm_final_g, loss_target, m_norm_mix_g, m_w_in, m_gmlp_ln_g, m_gmlp_ln_b, m_gmlp_w_s, m_gmlp_b_s, m_hgrn_lb_table, m_hgrn_norm_g, m_w_branch_a, m_w_branch_b, m_w_out, m_norm_ffn_g, m_w_gate_up, m_w_down, m_norm_final_g, v_norm_mix_g, v_w_in, v_gmlp_ln_g, v_gmlp_ln_b, v_gmlp_w_s, v_gmlp_b_s, v_hgrn_lb_table, v_hgrn_norm_g, v_w_branch_a, v_w_branch_b, v_w_out, v_norm_ffn_g, v_w_gate_up, v_w_down, v_norm_final_g):
    t = x.shape[1]
    x2d = x.reshape(t, D_MODEL)
    target = loss_target.reshape(t, D_MODEL)
    final_g = norm_final_g.reshape(1, D_MODEL)

    w_in_shard = w_in[0].astype(BF16)

    def rows_of(n):
        return lambda ref, j: ref.at[pl.ds(pl.multiple_of(j * n, 8), n)]

    gathered = [((N_DEV, D_MODEL, D_MODEL), BF16), ((D_MODEL, D_MODEL), BF16), ((D_MODEL, D_MODEL), BF16),
                ((D_MODEL, D_MODEL), BF16), ((N_DEV, FF_BLOCK, D_MODEL), BF16), ((D_FF, D_MODEL), BF16)]
    places = [lambda ref, j: ref.at[_pos_of_dev(j)], rows_of(BRANCH_ROWS), rows_of(BRANCH_ROWS), rows_of(BRANCH_ROWS),
              lambda ref, j: ref.at[j], rows_of(DOWN_ROWS)]
    w_in_g = _all_gather_balanced_async("w_in_all_gather", 9, w_in_shard, gathered[0], places[0], D_MODEL)
    w_in_sibling = _swap_with_sibling("w_in_from_sibling", w_in_shard)

    core_i, chip_i = lax.axis_index("c"), 2 * lax.axis_index("x") + lax.axis_index("y")
    own_pos = jnp.stack([_pos_of_dev(2 * chip_i + core_i), _pos_of_dev(2 * chip_i + 1 - core_i)]).astype(jnp.int32)
    other_pos = jnp.stack([_pos_of_dev(2 * jnp.bitwise_xor(chip_i, q) + cc) for q in (1, 2, 3) for cc in (0, 1)]).astype(jnp.int32)
    proj, h, h_t = _proj_forward_own_chip(own_pos, x2d, norm_mix_g, w_in_shard, w_in_sibling)

    small_w = [norm_mix_g, gmlp_ln_g, gmlp_ln_b, gmlp_b_s, hgrn_lb_table, hgrn_norm_g, norm_ffn_g, norm_final_g]
    small_m = [m_norm_mix_g, m_gmlp_ln_g, m_gmlp_ln_b, m_gmlp_b_s, m_hgrn_lb_table, m_hgrn_norm_g, m_norm_ffn_g, m_norm_final_g]
    small_v = [v_norm_mix_g, v_gmlp_ln_g, v_gmlp_ln_b, v_gmlp_b_s, v_hgrn_lb_table, v_hgrn_norm_g, v_norm_ffn_g, v_norm_final_g]
    _, (raw, small_w, small_m, small_v) = lax.optimization_barrier(
        (h, ([w_branch_a[0], w_branch_b[0], w_out[0], w_gate_up[0], w_down[0]], small_w, small_m, small_v)))
    later = [raw[0].astype(BF16), raw[1].astype(BF16), raw[2].astype(BF16),
             raw[3].T.astype(BF16), raw[4].astype(BF16)]
    w_a, w_b, w_o, w_gu, w_dn = _all_gather_async("weights_all_gather", 0, later, gathered[1:], places[1:])
    no_row = jnp.zeros((1, D_MODEL), F32)
    w_pack, m_pack, v_pack = (_pack_small(*vals, no_row) for vals in (small_w, small_m, small_v))
    bias_b = jnp.broadcast_to(small_w[3][0][:, :, None], (GROUPS, GMLP_CHUNK, GMLP_CHUNK))
    h_later, _ = lax.optimization_barrier((h, (later, w_pack, m_pack, v_pack, bias_b)))
    proj = _proj_forward_other_chips(other_pos, proj, h_later, w_in_g)
    a = _gmlp_forward(proj, gmlp_ln_g, gmlp_ln_b, gmlp_w_s[0], bias_b)
    og, o_saved, states = _hgrn_forward(proj, hgrn_lb_table, hgrn_norm_g)
    ya, yb, merged, x1, h2 = _branch_out_forward(a, og, proj, x2d, w_a, w_b, w_o, norm_ffn_g)
    gu, act, loss_tile, d_final_g, dx2, dx2b = _ffn_forward(h2, x1, w_gu, w_dn, target, final_g)

    core = lax.axis_index("c").astype(jnp.int32).reshape(1)
    chip = (2 * lax.axis_index("x") + lax.axis_index("y")).astype(jnp.int32).reshape(1)
    branch_rows, branch_shape = rows_of(BRANCH_ROWS), (BRANCH_ROWS, D_MODEL)
    branch_block = ((BRANCH_ROWS, D_MODEL), lambda q, r, c: (2 * q + c, 0))

    def chip_partials(names, grads, land, own_blocks):
        return [_chip_partial("chip_partial_" + nme, core, g_, blk, idx, l_)
                for nme, g_, (blk, idx), l_ in zip(names, grads, own_blocks, land)]

    whole = pl.BlockSpec((t, D_MODEL), lambda n: (0, 0))
    whole_t = pl.BlockSpec((D_MODEL, t), lambda n: (0, 0))
    col_blocks = [((t, D_MODEL), BF16), ((t, 256), BF16), ((D_MODEL, 256), F32)]

    def square_grad(name, a_, b_):
        return _weight_grad(name, a_, b_, whole, pl.BlockSpec((t, 256), lambda n: (0, n)), (D_MODEL, D_MODEL),
                            pl.BlockSpec((D_MODEL, 256), lambda n: (0, n)), D_MODEL // 256, col_blocks, False)

    dgu, dx1, dx1b, d_ffn_g = _ffn_backward(dx2b, dx2, gu, x1, w_gu, w_dn, norm_ffn_g)
    g_gu = _weight_grad(
        "grad_w_gate_up", dgu, h2, pl.BlockSpec((None, None, t, FF_BLOCK), lambda j: (j % 4, j // 4, 0, 0)), whole,
        (N_DEV, FF_BLOCK, D_MODEL), pl.BlockSpec((None, FF_BLOCK, D_MODEL), lambda j: (j, 0, 0)), N_DEV,
        [((t, 768), BF16), ((t, D_MODEL), BF16), ((FF_BLOCK, D_MODEL), F32)], False)
    g_dn = _weight_grad(
        "grad_w_down", act, dx2b, pl.BlockSpec((None, t, FF_BLOCK), lambda j: (j, 0, 0)), whole, (D_FF, D_MODEL),
        pl.BlockSpec((FF_BLOCK, D_MODEL), lambda j: (j, 0)), 4,
        [((t, 768), BF16), ((t, D_MODEL), BF16), ((FF_BLOCK, D_MODEL), F32)], False)
    names_f, grads_f = ["w_gate_up", "w_down"], [g_gu, g_dn]
    land_f = _exchange_sibling("ffn_grads_to_sibling", 2, grads_f, [lambda ref, j: ref.at[j], rows_of(DOWN_ROWS)],
                               [(FF_BLOCK, D_MODEL), (DOWN_ROWS, D_MODEL)])

    dx1b_later, _ = lax.optimization_barrier((dx1b, grads_f))
    dya, dyb, dproj, da, dog = _branch_out_backward(dx1b_later, ya, yb, proj, w_a, w_b, w_o)
    g_a = square_grad("grad_w_a", a, dya)
    g_b = square_grad("grad_w_b", og, dyb)
    g_o = square_grad("grad_w_out", merged, dx1b)
    names_b, grads_b = ["w_branch_a", "w_branch_b", "w_out"], [g_a, g_b, g_o]
    land_b = _exchange_sibling("branch_grads_to_sibling", 3, grads_b, [branch_rows] * 3, [branch_shape] * 3)

    part_f = chip_partials(names_f, grads_f, land_f,
                           [((None, FF_BLOCK, D_MODEL), lambda q, r, c: (2 * q + c, 0, 0)),
                            ((DOWN_ROWS, D_MODEL), lambda q, r, c: (2 * q + c, 0))])
    landed_f = _exchange_chips("ffn_grads_to_chips", 5, part_f)

    dog, _ = lax.optimization_barrier((dog, part_f))
    dproj, d_hg_norm, d_lb = _hgrn_backward(dproj, dog, o_saved, states, proj, hgrn_lb_table, hgrn_norm_g)

    land_b, _ = lax.optimization_barrier((land_b, part_f))
    part_b = chip_partials(names_b, grads_b, land_b, [branch_block] * 3)
    landed_b = _exchange_chips("branch_grads_to_chips", 6, part_b)

    da, _ = lax.optimization_barrier((da, part_b))
    dproj, d_ln_g, d_ln_b, d_ws, d_bs = _gmlp_backward(dproj, da, proj, gmlp_ln_g, gmlp_ln_b, gmlp_w_s[0], bias_b)

    def flat_ws(a):
        return a.reshape(GROUPS * GMLP_CHUNK, GMLP_CHUNK)

    small_partial = _pack_small(no_row, d_ln_g, d_ln_b, d_bs[:, :, 0], d_lb, d_hg_norm, d_ffn_g, d_final_g,
                                jnp.tile(loss_tile[0:1], (1, D_MODEL // 128)))
    small_all, ws_all = _all_gather_async(
        "small_grads_all_gather", 1, [small_partial, flat_ws(d_ws)],
        [((N_DEV, SMALL_ROWS, D_MODEL), F32), ((N_DEV, GROUPS * GMLP_CHUNK, GMLP_CHUNK), F32)],
        [lambda ref, j: ref.at[j], lambda ref, j: ref.at[j]])

    g_in = _weight_grad(
        "grad_w_in", h_t, dproj, whole_t, pl.BlockSpec((None, t, D_MODEL), lambda p: (p, 0, 0)), (N_DEV, D_MODEL, D_MODEL),
        pl.BlockSpec((None, D_MODEL, D_MODEL), lambda p: (p, 0, 0)), N_DEV,
        [((D_MODEL, t), BF16), ((t, D_MODEL), BF16), ((D_MODEL, D_MODEL), F32)], True)
    land_i = _exchange_sibling("w_in_grads_to_sibling", 4, [g_in], [lambda ref, j: ref.at[_pos_of_dev(j)]],
                               [(D_MODEL, D_MODEL)])

    big = {}
    for nme, own, lnd, w, m, v in zip(
            names_f + names_b, part_f + part_b, landed_f + landed_b,
            [w_gate_up, w_down, w_branch_a, w_branch_b, w_out], [m_w_gate_up, m_w_down, m_w_branch_a, m_w_branch_b, m_w_out],
            [v_w_gate_up, v_w_down, v_w_branch_a, v_w_branch_b, v_w_out]):
        flip = (lambda z: z.T) if nme == "w_gate_up" else (lambda z: z)
        big[nme] = [flip(o_)[None] for o_ in _adamw("adamw_" + nme, chip, own, lnd, flip(w[0]), flip(m[0]), flip(v[0]))]
    small, loss_row = _adamw_small_unpacked(small_all, w_pack, m_pack, v_pack)
    ws_outs = _adamw_small("adamw_w_s", ws_all, flat_ws(gmlp_w_s), flat_ws(m_gmlp_w_s), flat_ws(v_gmlp_w_s))
    land_i, _ = lax.optimization_barrier((land_i, (big, small, ws_outs)))
    part_i = chip_partials(["w_in"], [g_in], land_i,
                           [((None, D_MODEL, D_MODEL), lambda q, r, c: (_pos_of_dev(2 * q + c), 0, 0))])
    landed_i = _exchange_chips("w_in_grads_to_chips", 7, part_i)

    dx1, _ = lax.optimization_barrier((dx1, part_i))
    grad_x, d_mix_g = _input_backward(dproj, w_in_g, x2d, dx1, norm_mix_g)
    big["w_in"] = [o_[None] for o_ in _adamw("adamw_w_in", chip, part_i[0], landed_i[0], w_in[0], m_w_in[0], v_w_in[0])]

    def row8(a):
        return jnp.pad(a, ((0, 7), (0, 0)))

    d_mix_g, _ = lax.optimization_barrier((d_mix_g, landed_i))
    (mix_all,) = _all_gather_async("mix_gain_grad_all_gather", 8, [row8(d_mix_g)], [((N_DEV, 8, D_MODEL), F32)],
                                   [lambda ref, j: ref.at[j]])
    mix_outs = _adamw_row("adamw_mix_gain", mix_all, norm_mix_g, m_norm_mix_g, v_norm_mix_g)
    small = [dict(p, norm_final_g=p["norm_final_g"][0], gmlp_w_s=ws.reshape(1, GROUPS, GMLP_CHUNK, GMLP_CHUNK), norm_mix_g=q)
             for p, ws, q in zip(small, ws_outs, mix_outs)]

    loss = loss_row[0, 0]
    order = ["norm_mix_g", "w_in", "gmlp_ln_g", "gmlp_ln_b", "gmlp_w_s", "gmlp_b_s", "hgrn_lb_table", "hgrn_norm_g",
             "w_branch_a", "w_branch_b", "w_out", "norm_ffn_g", "w_gate_up", "w_down", "norm_final_g"]
    outs = [loss, grad_x.reshape(1, t, D_MODEL)]
    for kind in range(4):
        for nme in order:
            outs.append(big[nme][kind] if nme in big else small[kind][nme])
    return tuple(outs)
```

```python
import functools

import jax
import jax.numpy as jnp
from jax import lax
from jax.experimental import pallas as pl
from jax.experimental.pallas import tpu as pltpu
from jax.experimental.pallas import tpu_sc as plsc

F32, BF16 = jnp.float32, jnp.bfloat16
D_MODEL = 1024
N_DEV = 8
HEADS = 8
HEAD_DIM = 128
GROUPS = 8
GMLP_CHUNK = 128
HGRN_CHUNK = 64
HGRN_SCALE = HEAD_DIM ** -0.5
D_FF = 2816
FF_BLOCK = D_FF // 4
DOWN_ROWS = D_FF // N_DEV
BRANCH_ROWS = D_MODEL // N_DEV
NORM_EPS = 1e-6
ADAM_LR, ADAM_B1, ADAM_B2, ADAM_EPS, ADAM_WD, ADAM_STEP = 0.001, 0.9, 0.999, 1e-08, 0.01, 10
SMALL_ROWS = 72
V7X_VMEM_BYTES = 64 * 1024 * 1024
VMEM_CAP = V7X_VMEM_BYTES - 6 * 1024 * 1024
MESH_ID = pl.DeviceIdType.MESH
ANY = pl.BlockSpec(memory_space=pl.ANY)
RESIDENT = pl.BlockSpec(memory_space=pltpu.VMEM)
Q_POS, U_POS, GATE_POS = 0, 4, 6


def _pos_of_dev(j):
    return jnp.where(j < 2, j + 4, jnp.where(j < 6, j - 2, j))


def _dev_of_pos(p):
    return jnp.where(p < 4, p + 2, jnp.where(p < 6, p - 4, p))


def _nbytes(shape, dtype):
    n = 1
    for s in shape:
        n *= s
    return n * jnp.dtype(dtype).itemsize


def _params(blocks, scratch=(), temps=0, sem=None):
    need = 2 * sum(_nbytes(s, d) for s, d in blocks) + sum(_nbytes(s, d) for s, d in scratch) + temps
    assert need + (4 << 20) <= VMEM_CAP, need
    return pltpu.CompilerParams(dimension_semantics=sem, vmem_limit_bytes=VMEM_CAP)


def _tile(n, pref):
    return pref if n % pref == 0 else n


def _dot(a, b):
    return jnp.dot(a, b, preferred_element_type=F32)


def _dot_nt(a, b):
    return lax.dot_general(a, b, (((1,), (1,)), ((), ())), preferred_element_type=F32)


def _dot_tn(a, b):
    return lax.dot_general(a, b, (((0,), (0,)), ((), ())), preferred_element_type=F32)


def _sigmoid(x):
    return 1.0 / (1.0 + jnp.exp(-x))


_GELU_C = 0.7978845608028654


def _gelu(x):
    return x * (0.5 * (1.0 + jnp.tanh(_GELU_C * (x + 0.044715 * (x * x * x)))))


def _gelu_and_grad(x):
    t = jnp.tanh(_GELU_C * (x + 0.044715 * (x * x * x)))
    half = 0.5 * (1.0 + t)
    return x * half, half + 0.5 * x * (1.0 - t * t) * (_GELU_C * (1.0 + 3.0 * 0.044715 * x * x))


def _rms_stats(x):
    r = lax.rsqrt(jnp.mean(x * x, axis=-1, keepdims=True) + NORM_EPS)
    return r, x * r


def _rms_bwd(dy, x, g):
    r, xh = _rms_stats(x)
    dg = jnp.sum(dy * xh, axis=0, keepdims=True)
    dxh = dy * g
    dx = r * (dxh - xh * jnp.mean(dxh * xh, axis=-1, keepdims=True))
    return dx, dg


def _split3(x):
    hi = x.astype(BF16)
    r = x - hi.astype(F32)
    mid = r.astype(BF16)
    lo = (r - mid.astype(F32)).astype(BF16)
    return hi, mid, lo


def _mask_mm(mask_bf16, x):
    hi, mid, lo = _split3(x)
    return _dot(mask_bf16, hi) + _dot(mask_bf16, mid) + _dot(mask_bf16, lo)


def _place():
    return lax.axis_index("x"), lax.axis_index("y"), lax.axis_index("c")


def _gather_copies(src, out, send, recv, loc, slicers):
    n = len(src)
    x, y, c = _place()
    me, sib = (x, y, c), (x, y, 1 - c)
    chips = [(1 - x, y), (x, 1 - y), (1 - x, 1 - y)]

    def dev(p):
        return 4 * p[0] + 2 * p[1] + p[2]

    def rc(i, k, block, to, from_src=False):
        dst = slicers[i](out[i], dev(block))
        return pltpu.make_async_remote_copy(
            src_ref=src[i] if from_src else dst, dst_ref=dst, send_sem=send.at[7 * i + k],
            recv_sem=recv.at[7 * i + k], device_id=to, device_id_type=MESH_ID)

    mine = [pltpu.make_async_copy(src[i], slicers[i](out[i], dev(me)), loc.at[i]) for i in range(n)]
    for cp in mine:
        cp.start()
    first = []
    for i in range(n):
        first.append(rc(i, 0, me, sib, True))
        for j, chip in enumerate(chips):
            first.append(rc(i, 1 + j, me, (*chip, c), True))
    for cp in first:
        cp.start()
    passed = []
    for j, chip in enumerate(chips):
        for i in range(n):
            rc(i, 1 + j, (*chip, c), me).wait_recv()
            cp = rc(i, 4 + j, (*chip, c), sib)
            cp.start()
            passed.append(cp)
    for i in range(n):
        rc(i, 0, sib, me).wait_recv()
        for j, chip in enumerate(chips):
            rc(i, 4 + j, (*chip, 1 - c), me).wait_recv()
    for cp in first + passed:
        cp.wait_send()
    for cp in mine:
        cp.wait()


def _gather_copies_balanced(src, out, send, recv, loc, slicer, rows):
    x, y, c = _place()
    me, sib = (x, y, c), (x, y, 1 - c)
    xn, yn, dg = (1 - x, y), (x, 1 - y), (1 - x, 1 - y)
    half_rows = rows // 2

    def block(p):
        return slicer(out, 4 * p[0] + 2 * p[1] + p[2])

    def half(ref, h):
        return ref.at[pl.ds(h * half_rows, half_rows)]

    def rc(k, dst, to, from_src=False):
        return pltpu.make_async_remote_copy(src_ref=src if from_src else dst, dst_ref=dst, send_sem=send.at[k],
                                            recv_sem=recv.at[k], device_id=to, device_id_type=MESH_ID)

    mine = pltpu.make_async_copy(src, block(me), loc.at[0])
    mine.start()
    sends = [rc(0, block(me), sib, True), rc(1, block(me), (*xn, c), True), rc(2, block(me), (*yn, c), True)]
    for cp in sends:
        cp.start()

    def then(cp):
        cp.start()
        sends.append(cp)

    rc(1, block((*xn, c)), me).wait_recv()
    then(rc(3, half(block((*xn, c)), 0), (*yn, c)))
    then(rc(5, block((*xn, c)), sib))
    rc(2, block((*yn, c)), me).wait_recv()
    then(rc(4, half(block((*yn, c)), 1), (*xn, c)))
    then(rc(6, block((*yn, c)), sib))
    rc(3, half(block((*dg, c)), 0), me).wait_recv()
    then(rc(7, half(block((*dg, c)), 0), sib))
    rc(4, half(block((*dg, c)), 1), me).wait_recv()
    then(rc(8, half(block((*dg, c)), 1), sib))
    rc(0, block(sib), me).wait_recv()
    rc(5, block((*xn, 1 - c)), me).wait_recv()
    rc(6, block((*yn, 1 - c)), me).wait_recv()
    rc(7, half(block((*dg, 1 - c)), 0), me).wait_recv()
    rc(8, half(block((*dg, 1 - c)), 1), me).wait_recv()
    for cp in sends:
        cp.wait_send()
    mine.wait()


def _gather_scratch(n):
    return [pltpu.SemaphoreType.DMA((7 * n,)), pltpu.SemaphoreType.DMA((7 * n,)), pltpu.SemaphoreType.DMA((n,))]


def _handshake(peers):
    barrier = pltpu.get_barrier_semaphore()
    for peer in peers:
        pl.semaphore_signal(barrier, inc=1, device_id=peer, device_id_type=MESH_ID)
    pl.semaphore_wait(barrier, len(peers))


def _all_gather_async(name, collective_id, srcs, out_shapes, slicers):
    n = len(srcs)

    def body(*refs):
        x, y, c = _place()
        _handshake([(1 - x if dx else x, 1 - y if dy else y, 1 - c if dc else c)
                    for dx in (0, 1) for dy in (0, 1) for dc in (0, 1) if dx or dy or dc])
        _gather_copies(refs[:n], refs[n:2 * n], *refs[2 * n:], slicers)

    return _sequencer_call(name, collective_id, body, srcs, [jax.ShapeDtypeStruct(s, d) for s, d in out_shapes],
                           _gather_scratch(n))


def _all_gather_balanced_async(name, collective_id, src, out_shape, slicer, rows):
    def body(src_ref, out_ref, send, recv, loc):
        x, y, c = _place()
        _handshake([(1 - x if dx else x, 1 - y if dy else y, 1 - c if dc else c)
                    for dx in (0, 1) for dy in (0, 1) for dc in (0, 1) if dx or dy or dc])
        _gather_copies_balanced(src_ref, out_ref, send, recv, loc, slicer, rows)

    return _sequencer_call(name, collective_id, body, [src], [jax.ShapeDtypeStruct(*out_shape)],
                           [pltpu.SemaphoreType.DMA((9,)), pltpu.SemaphoreType.DMA((9,)), pltpu.SemaphoreType.DMA((1,))])[0]


def _sequencer_call(name, collective_id, body, operands, out_types, scratch):
    return pl.kernel(
        body, out_type=out_types, mesh=plsc.ScalarSubcoreMesh(axis_name="sequencer", num_cores=1), name=name,
        scratch_types=scratch, compiler_params=pltpu.CompilerParams(collective_id=collective_id),
    )(*operands)


def _exchange_sibling(name, collective_id, grads, shard_fns, shard_shapes):
    n = len(grads)

    def body(*refs):
        g, land = refs[:n], refs[n:2 * n]
        send, recv = refs[2 * n:]
        x, y, c = _place()
        _handshake([(x, y, 1 - c)])
        remote = []
        for i in range(n):
            for q in range(4):
                cp = pltpu.make_async_remote_copy(
                    src_ref=shard_fns[i](g[i], 2 * q + (1 - c)), dst_ref=land[i].at[q], send_sem=send.at[4 * i + q],
                    recv_sem=recv.at[4 * i + q], device_id=(x, y, 1 - c), device_id_type=MESH_ID)
                cp.start()
                remote.append(cp)
        for cp in remote:
            cp.wait()

    return _sequencer_call(name, collective_id, body, grads, [jax.ShapeDtypeStruct((4, *s), F32) for s in shard_shapes],
                           [pltpu.SemaphoreType.DMA((4 * n,)), pltpu.SemaphoreType.DMA((4 * n,))])


def _exchange_chips(name, collective_id, parts):
    n = len(parts)

    def body(*refs):
        part, out = refs[:n], refs[n:2 * n]
        send, recv = refs[2 * n:]
        x, y, c = _place()
        _handshake([(1 - x, y, c), (x, 1 - y, c), (1 - x, 1 - y, c)])
        remote = []
        for i in range(n):
            for s in range(3):
                qx = 1 - x if (s + 1) // 2 else x
                qy = 1 - y if (s + 1) % 2 else y
                cp = pltpu.make_async_remote_copy(
                    src_ref=part[i].at[2 * qx + qy], dst_ref=out[i].at[s], send_sem=send.at[3 * i + s],
                    recv_sem=recv.at[3 * i + s], device_id=(qx, qy, c), device_id_type=MESH_ID)
                cp.start()
                remote.append(cp)
        for cp in remote:
            cp.wait()

    return _sequencer_call(name, collective_id, body, parts,
                           [jax.ShapeDtypeStruct((3, *p.shape[1:]), p.dtype) for p in parts],
                           [pltpu.SemaphoreType.DMA((3 * n,)), pltpu.SemaphoreType.DMA((3 * n,))])


def _chip_partial(name, core, grad, own_block, own_index, land):
    _, rows, cols = land.shape
    tr = own_block[-2]

    def body(core_ref, a_ref, b_ref, o_ref):
        o_ref[...] = (a_ref[...] + b_ref[...]).astype(BF16)

    spec = pl.BlockSpec((None, tr, cols), lambda q, r, c: (q, r, 0))
    return pl.pallas_call(
        body, name=name, out_shape=jax.ShapeDtypeStruct(land.shape, BF16),
        grid_spec=pltpu.PrefetchScalarGridSpec(
            num_scalar_prefetch=1, grid=(4, rows // tr),
            in_specs=[pl.BlockSpec(own_block, lambda q, r, c: own_index(q, r, c[0])), spec], out_specs=spec),
        compiler_params=_params([((tr, cols), F32)] * 2 + [((tr, cols), BF16)], sem=("arbitrary", "arbitrary")),
    )(core, grad, land)


def _adamw_math(w, g, m, v):
    m = ADAM_B1 * m + (1.0 - ADAM_B1) * g
    v = ADAM_B2 * v + (1.0 - ADAM_B2) * (g * g)
    m_hat = m / (1.0 - ADAM_B1 ** ADAM_STEP)
    v_hat = v / (1.0 - ADAM_B2 ** ADAM_STEP)
    delta = -ADAM_LR * (m_hat / (jnp.sqrt(v_hat) + ADAM_EPS) + ADAM_WD * w)
    return delta, m, v


def _adamw(name, chip, own, landed, w, m, v):
    _, rows, cols = own.shape
    tr = _tile(rows, 512) if rows % 512 == 0 else _tile(rows, 176)

    def body(chip_ref, own_ref, l_ref, w_ref, m_ref, v_ref, g_out, d_out, m_out, v_out):
        g = own_ref[...].astype(F32)
        for s in range(3):
            g = g + l_ref[s].astype(F32)
        delta, m_new, v_new = _adamw_math(w_ref[...], g, m_ref[...], v_ref[...])
        g_out[...] = g
        d_out[...] = delta
        m_out[...] = m_new
        v_out[...] = v_new

    spec = pl.BlockSpec((tr, cols), lambda r, c: (r, 0))
    return pl.pallas_call(
        body, name=name, out_shape=[jax.ShapeDtypeStruct((rows, cols), F32)] * 4,
        grid_spec=pltpu.PrefetchScalarGridSpec(
            num_scalar_prefetch=1, grid=(rows // tr,),
            in_specs=[pl.BlockSpec((None, tr, cols), lambda r, c: (c[0], r, 0)),
                      pl.BlockSpec((3, tr, cols), lambda r, c: (0, r, 0)), spec, spec, spec],
            out_specs=[spec] * 4),
        compiler_params=_params([((4, tr, cols), own.dtype)] + [((tr, cols), F32)] * 7, sem=("arbitrary",)),
    )(chip, own, landed, w, m, v)


def _swap_with_sibling(name, x):
    def body(x_ref, o_ref, send, recv):
        px, py, c = _place()
        cp = pltpu.make_async_remote_copy(src_ref=x_ref, dst_ref=o_ref, send_sem=send, recv_sem=recv,
                                          device_id=(px, py, 1 - c), device_id_type=MESH_ID)
        cp.start()
        cp.wait()

    return pl.pallas_call(
        body, name=name, out_shape=jax.ShapeDtypeStruct(x.shape, x.dtype), in_specs=[ANY], out_specs=ANY,
        scratch_shapes=[pltpu.SemaphoreType.DMA, pltpu.SemaphoreType.DMA],
    )(x)


def _proj_forward_own_chip(positions, x, gain, w_own, w_sibling):
    t = x.shape[0]
    tm = _tile(t, 1024)

    def body(pos_ref, x_ref, g_ref, wo_ref, ws_ref, o_ref, h_ref, ht_ref):
        @pl.when(pl.program_id(1) == 0)
        def _():
            _, xh = _rms_stats(x_ref[...])
            h = (xh * g_ref[...]).astype(BF16)
            h_ref[...] = h
            ht_ref[...] = h.T
            o_ref[...] = _dot(h, wo_ref[...])

        @pl.when(pl.program_id(1) == 1)
        def _():
            o_ref[...] = _dot(h_ref[...], ws_ref[...])

    tok = pl.BlockSpec((tm, D_MODEL), lambda m, k, pos: (m, 0))
    return pl.pallas_call(
        body, name="proj_fwd_own_chip",
        out_shape=[jax.ShapeDtypeStruct((N_DEV, t, D_MODEL), F32), jax.ShapeDtypeStruct((t, D_MODEL), BF16),
                   jax.ShapeDtypeStruct((D_MODEL, t), BF16)],
        grid_spec=pltpu.PrefetchScalarGridSpec(
            num_scalar_prefetch=1, grid=(t // tm, 2),
            in_specs=[tok, pl.BlockSpec((1, D_MODEL), lambda m, k, pos: (0, 0)), RESIDENT, RESIDENT],
            out_specs=[pl.BlockSpec((None, tm, D_MODEL), lambda m, k, pos: (pos[k], m, 0)), tok,
                       pl.BlockSpec((D_MODEL, tm), lambda m, k, pos: (0, m))]),
        compiler_params=_params([((tm, D_MODEL), F32)] * 2 + [((tm, D_MODEL), BF16)] * 2,
                                scratch=[((2, D_MODEL, D_MODEL), BF16)], temps=6 << 20, sem=("arbitrary", "arbitrary")),
    )(positions, x, gain, w_own, w_sibling)


def _proj_forward_other_chips(positions, proj, h, w_in_g):
    t = h.shape[0]
    tm = _tile(t, 2048)

    def body(pos_ref, _, h_ref, w_ref, o_ref):
        o_ref[...] = _dot(h_ref[...], w_ref[pos_ref[pl.program_id(1)]])

    return pl.pallas_call(
        body, name="proj_fwd_other_chips", out_shape=jax.ShapeDtypeStruct(proj.shape, F32),
        grid_spec=pltpu.PrefetchScalarGridSpec(
            num_scalar_prefetch=1, grid=(t // tm, N_DEV - 2),
            in_specs=[ANY, pl.BlockSpec((tm, D_MODEL), lambda m, k, pos: (m, 0)), RESIDENT],
            out_specs=pl.BlockSpec((None, tm, D_MODEL), lambda m, k, pos: (pos[k], m, 0))),
        input_output_aliases={1: 0},
        compiler_params=_params([((tm, D_MODEL), F32), ((tm, D_MODEL), BF16)], scratch=[((N_DEV, D_MODEL, D_MODEL), BF16)],
                                temps=6 << 20, sem=("arbitrary", "arbitrary")),
    )(positions, proj, h, w_in_g)


def _masked_ws(ws_ref, g):
    row = lax.broadcasted_iota(jnp.int32, (GMLP_CHUNK, GMLP_CHUNK), 0)
    col = lax.broadcasted_iota(jnp.int32, (GMLP_CHUNK, GMLP_CHUNK), 1)
    return jnp.where(row >= col, ws_ref[g], 0.0).astype(BF16)


def _gmlp_forward(proj, ln_g, ln_b, w_s, bias_b):
    t = proj.shape[1]
    tm = _tile(t, 512)
    chunks = tm // GMLP_CHUNK

    def body(u_ref, v_ref, lng_ref, lnb_ref, ws_ref, bias_ref, a_ref, vn_scr):
        vv = _gelu(v_ref[...])
        mu = jnp.mean(vv, axis=-1, keepdims=True)
        cen = vv - mu
        var = jnp.mean(cen * cen, axis=-1, keepdims=True)
        vn_scr[...] = ((cen * lax.rsqrt(var + NORM_EPS)) * lng_ref[...] + lnb_ref[...]).astype(BF16)
        for g in range(GROUPS):
            wm = _masked_ws(ws_ref, g)
            cols = slice(g * HEAD_DIM, (g + 1) * HEAD_DIM)
            for c in range(chunks):
                rows = slice(c * GMLP_CHUNK, (c + 1) * GMLP_CHUNK)
                mixed = _dot(wm, vn_scr[rows, cols]) + bias_ref[g]
                a_ref[rows, cols] = (_gelu(u_ref[rows, cols]) * mixed).astype(BF16)

    small = pl.BlockSpec((GROUPS, GMLP_CHUNK, GMLP_CHUNK), lambda m: (0, 0, 0))
    vec = pl.BlockSpec((1, D_MODEL), lambda m: (0, 0))
    return pl.pallas_call(
        body, name="gmlp_fwd", out_shape=jax.ShapeDtypeStruct((t, D_MODEL), BF16), grid=(t // tm,),
        in_specs=[pl.BlockSpec((None, tm, D_MODEL), lambda m: (U_POS, m, 0)),
                  pl.BlockSpec((None, tm, D_MODEL), lambda m: (U_POS + 1, m, 0)), vec, vec, small, small],
        out_specs=pl.BlockSpec((tm, D_MODEL), lambda m: (m, 0)),
        scratch_shapes=[pltpu.VMEM((tm, D_MODEL), BF16)],
        compiler_params=_params([((tm, D_MODEL), F32)] * 2 + [((tm, D_MODEL), BF16)] + [((8, 128, 128), F32)] * 2,
                                scratch=[((tm, D_MODEL), BF16)], temps=8 << 20, sem=("arbitrary",)),
    )(proj, proj, ln_g, ln_b, w_s, bias_b)


def _lower_bound(tab_ref):
    t0, t1 = tab_ref[0:1, :], tab_ref[1:2, :]
    mx = jnp.maximum(t0, t1)
    e0, e1 = jnp.exp(t0 - mx), jnp.exp(t1 - mx)
    return e0 / (e0 + e1)


def _tri_masks():
    row = lax.broadcasted_iota(jnp.int32, (HGRN_CHUNK, HGRN_CHUNK), 0)
    col = lax.broadcasted_iota(jnp.int32, (HGRN_CHUNK, HGRN_CHUNK), 1)
    return row >= col, row <= col


def _chunk_rows(c):
    return slice(c * HGRN_CHUNK, (c + 1) * HGRN_CHUNK)


def _per_chunk(x, nc, fn):
    return jnp.concatenate([fn(x[_chunk_rows(c)]) for c in range(nc)], axis=0)


def _chunk_row_bcast(x, nc, i):
    return _per_chunk(x, nc, lambda xc: jnp.broadcast_to(xc[i:i + 1, :], (HGRN_CHUNK, HEAD_DIM)))


def _hgrn_gates(q, fl, lb, nc):
    lower, _ = _tri_masks()
    lower = lower.astype(BF16)
    s = _sigmoid(fl)
    f = lb + (1.0 - lb) * s
    k = 1.0 - f
    hi, mid, lo = _split3(jnp.log(f))
    a = jnp.concatenate([_dot(lower, hi[_chunk_rows(c)]) + _dot(lower, mid[_chunk_rows(c)]) + _dot(lower, lo[_chunk_rows(c)])
                         for c in range(nc)], axis=0)
    a_mid = _chunk_row_bcast(a, nc, HGRN_CHUNK // 2 - 1)
    a_last = _chunk_row_bcast(a, nc, HGRN_CHUNK - 1)
    qs = q * HGRN_SCALE
    e_in, e_out, e_end, e_all = jnp.exp(a - a_mid), jnp.exp(a_mid - a), jnp.exp(a_last - a), jnp.exp(a)
    decay = [jnp.exp(a[c * HGRN_CHUNK + HGRN_CHUNK - 1:(c + 1) * HGRN_CHUNK, :]) for c in range(nc)]
    return dict(s=s, f=f, k=k, decay=decay, e_in=e_in, e_out=e_out, e_end=e_end, e_all=e_all,
                qi=qs * e_in, ki=k * e_out, kd=k * e_end, qe=qs * e_all)


def _hgrn_forward(proj, lb_table, norm_g):
    t = proj.shape[1]
    tb = _tile(t, 1024)
    nc = tb // HGRN_CHUNK
    n_chunks = t // HGRN_CHUNK

    def body(q_ref, f_ref, i_ref, g_ref, tab_ref, ng_ref, og_ref, o_ref, st_ref, state):
        @pl.when(pl.program_id(1) == 0)
        def _():
            state[...] = jnp.zeros_like(state)

        lower, _ = _tri_masks()
        gt = _hgrn_gates(q_ref[...], f_ref[...], _lower_bound(tab_ref), nc)
        qi, ki, kd, qe = (gt[n].astype(BF16) for n in ("qi", "ki", "kd", "qe"))
        vb = i_ref[...].astype(BF16)
        o_intra, d_state = [], []
        for c in range(nc):
            rows = _chunk_rows(c)
            p = jnp.where(lower, _dot_nt(qi[rows], ki[rows]), 0.0).astype(BF16)
            o_intra.append(_dot(p, vb[rows]))
            d_state.append(_dot_tn(vb[rows], kd[rows]))
        st = state[...]
        outs = []
        for c in range(nc):
            st_ref[c] = st
            outs.append(o_intra[c] + _dot_nt(qe[_chunk_rows(c)], st.astype(BF16)))
            st = st * gt["decay"][c] + d_state[c]
        state[...] = st
        o = jnp.concatenate(outs, axis=0)
        o_ref[...] = o
        _, oh = _rms_stats(o)
        gz = g_ref[...]
        og_ref[...] = ((oh * ng_ref[...]) * (gz * _sigmoid(gz))).astype(BF16)

    def blk(p):
        return pl.BlockSpec((None, tb, HEAD_DIM), lambda h, n: (p, n, h))

    out_blk = pl.BlockSpec((tb, HEAD_DIM), lambda h, n: (n, h))
    return pl.pallas_call(
        body, name="hgrn_fwd",
        out_shape=[jax.ShapeDtypeStruct((t, D_MODEL), BF16), jax.ShapeDtypeStruct((t, D_MODEL), F32),
                   jax.ShapeDtypeStruct((HEADS, n_chunks, HEAD_DIM, HEAD_DIM), F32)],
        grid=(HEADS, t // tb),
        in_specs=[blk(Q_POS), blk(Q_POS + 1), blk(Q_POS + 2), blk(Q_POS + 3),
                  pl.BlockSpec((2, HEAD_DIM), lambda h, n: (0, h)), pl.BlockSpec((1, HEAD_DIM), lambda h, n: (0, h))],
        out_specs=[out_blk, out_blk, pl.BlockSpec((None, nc, HEAD_DIM, HEAD_DIM), lambda h, n: (h, n, 0, 0))],
        scratch_shapes=[pltpu.VMEM((HEAD_DIM, HEAD_DIM), F32)],
        compiler_params=_params([((tb, HEAD_DIM), F32)] * 6 + [((nc, HEAD_DIM, HEAD_DIM), F32)], temps=8 << 20,
                                sem=("arbitrary", "arbitrary")),
    )(proj, proj, proj, proj, lb_table, norm_g)


def _branch_out_forward(a, og, proj, x, w_a, w_b, w_out, ffn_g):
    t = x.shape[0]
    tm = _tile(t, 512)

    def body(a_ref, og_ref, ga_ref, gb_ref, x_ref, wa_ref, wb_ref, wo_ref, g_ref, ya_ref, yb_ref, mg_ref, x1_ref, h2_ref):
        ya = _dot(a_ref[...], wa_ref[...])
        yb = _dot(og_ref[...], wb_ref[...])
        ya_ref[...] = ya
        yb_ref[...] = yb
        merged = (_sigmoid(ga_ref[...]) * ya + _sigmoid(gb_ref[...]) * yb).astype(BF16)
        mg_ref[...] = merged
        x1 = x_ref[...] + _dot(merged, wo_ref[...])
        x1_ref[...] = x1
        _, xh = _rms_stats(x1)
        h2_ref[...] = (xh * g_ref[...]).astype(BF16)

    tok = pl.BlockSpec((tm, D_MODEL), lambda m: (m, 0))
    return pl.pallas_call(
        body, name="branch_out_fwd",
        out_shape=[jax.ShapeDtypeStruct((t, D_MODEL), F32), jax.ShapeDtypeStruct((t, D_MODEL), F32),
                   jax.ShapeDtypeStruct((t, D_MODEL), BF16), jax.ShapeDtypeStruct((t, D_MODEL), F32),
                   jax.ShapeDtypeStruct((t, D_MODEL), BF16)],
        grid=(t // tm,),
        in_specs=[tok, tok, pl.BlockSpec((None, tm, D_MODEL), lambda m: (GATE_POS, m, 0)),
                  pl.BlockSpec((None, tm, D_MODEL), lambda m: (GATE_POS + 1, m, 0)), tok, RESIDENT, RESIDENT, RESIDENT,
                  pl.BlockSpec((1, D_MODEL), lambda m: (0, 0))],
        out_specs=[tok] * 5,
        compiler_params=_params([((tm, D_MODEL), BF16)] * 4 + [((tm, D_MODEL), F32)] * 6, scratch=[((D_MODEL, D_MODEL), BF16)] * 3,
                                temps=8 << 20, sem=("arbitrary",)),
    )(a, og, proj, proj, x, w_a, w_b, w_out, ffn_g)


def _ffn_forward(h2, x1, w_gu, w_down, target, final_g):
    t = x1.shape[0]
    tm = _tile(t, 512)

    def body(h_ref, wgu_ref, wd_ref, x1_ref, t_ref, g_ref, gu_ref, act_ref, loss_ref, dg_ref, dx_ref, dxb_ref, acc):
        m, j = pl.program_id(0), pl.program_id(1)

        @pl.when((m == 0) & (j == 0))
        def _():
            loss_ref[...] = jnp.zeros_like(loss_ref)
            dg_ref[...] = jnp.zeros_like(dg_ref)

        h = h_ref[...]
        gate = _dot_nt(h, wgu_ref[j])
        up = _dot_nt(h, wgu_ref[j + 4])
        gu_ref[0] = gate
        gu_ref[1] = up
        act = ((gate * _sigmoid(gate)) * up).astype(BF16)
        act_ref[...] = act
        part = _dot(act, wd_ref[j])

        @pl.when(j == 0)
        def _():
            acc[...] = part

        @pl.when((j > 0) & (j < 3))
        def _():
            acc[...] += part

        @pl.when(j == 3)
        def _():
            x2 = x1_ref[...] + (acc[...] + part)
            g = g_ref[...]
            r, xh = _rms_stats(x2)
            err = xh * g - t_ref[...]
            loss_ref[...] += 0.5 * jnp.sum(jnp.mean(err * err, axis=-1, keepdims=True), axis=0, keepdims=True)
            dy = err * (1.0 / D_MODEL)
            dg_ref[...] += jnp.sum(dy * xh, axis=0, keepdims=True)
            dxh = dy * g
            dx = r * (dxh - xh * jnp.mean(dxh * xh, axis=-1, keepdims=True))
            dx_ref[...] = dx
            dxb_ref[...] = dx.astype(BF16)

    tok = pl.BlockSpec((tm, D_MODEL), lambda m, j: (m, 0))
    vec = pl.BlockSpec((1, D_MODEL), lambda m, j: (0, 0))
    return pl.pallas_call(
        body, name="ffn_fwd",
        out_shape=[jax.ShapeDtypeStruct((4, 2, t, FF_BLOCK), F32), jax.ShapeDtypeStruct((4, t, FF_BLOCK), BF16),
                   jax.ShapeDtypeStruct((8, 128), F32), jax.ShapeDtypeStruct((1, D_MODEL), F32),
                   jax.ShapeDtypeStruct((t, D_MODEL), F32), jax.ShapeDtypeStruct((t, D_MODEL), BF16)],
        grid=(t // tm, 4),
        in_specs=[tok, RESIDENT, RESIDENT, tok, tok, vec],
        out_specs=[pl.BlockSpec((None, 2, tm, FF_BLOCK), lambda m, j: (j, 0, m, 0)),
                   pl.BlockSpec((None, tm, FF_BLOCK), lambda m, j: (j, m, 0)),
                   pl.BlockSpec((8, 128), lambda m, j: (0, 0)), vec, tok, tok],
        scratch_shapes=[pltpu.VMEM((tm, D_MODEL), F32)],
        compiler_params=_params([((tm, D_MODEL), BF16), ((tm, D_MODEL), F32), ((tm, D_MODEL), F32), ((2, tm, 768), F32),
                                 ((tm, 768), BF16), ((tm, D_MODEL), F32), ((tm, D_MODEL), BF16)],
                                scratch=[((tm, D_MODEL), F32), ((N_DEV, FF_BLOCK, D_MODEL), BF16), ((D_FF, D_MODEL), BF16)],
                                temps=6 << 20, sem=("arbitrary", "arbitrary")),
    )(h2, w_gu, w_down.reshape(4, FF_BLOCK, D_MODEL), x1, target, final_g)


def _ffn_backward(dx2b, dx2, gu, x1, w_gu, w_down, ffn_g):
    t = x1.shape[0]
    tm = _tile(t, 512)

    def body(dxb_ref, dx2_ref, gu_ref, x1_ref, wgu_ref, wd_ref, g_ref, dgu_ref, dx1_ref, dx1b_ref, dg_ref, acc, prev):
        m, j = pl.program_id(0), pl.program_id(1)

        @pl.when((m == 0) & (j == 0))
        def _():
            dg_ref[...] = jnp.zeros_like(dg_ref)

        @pl.when(j == 0)
        def _():
            prev[...] = jnp.zeros_like(prev)
            acc[...] = jnp.zeros_like(acc)

        jm1 = jnp.maximum(j - 1, 0)
        acc[...] += _dot(prev[0], wgu_ref[jm1]) + _dot(prev[1], wgu_ref[jm1 + 4])
        dact = _dot_nt(dxb_ref[...], wd_ref[j])
        gate, up = gu_ref[0], gu_ref[1]
        sg = _sigmoid(gate)
        dgate = (dact * up * (sg * (1.0 + gate * (1.0 - sg)))).astype(BF16)
        dup = (dact * (gate * sg)).astype(BF16)
        dgu_ref[0] = dgate
        dgu_ref[1] = dup
        prev[0] = dgate
        prev[1] = dup

        @pl.when(j == 3)
        def _():
            dh2 = acc[...] + (_dot(prev[0], wgu_ref[3]) + _dot(prev[1], wgu_ref[7]))
            dx, dg = _rms_bwd(dh2, x1_ref[...], g_ref[...])
            dx1 = dx2_ref[...] + dx
            dx1_ref[...] = dx1
            dx1b_ref[...] = dx1.astype(BF16)
            dg_ref[...] += dg

    tok = pl.BlockSpec((tm, D_MODEL), lambda m, j: (m, 0))
    vec = pl.BlockSpec((1, D_MODEL), lambda m, j: (0, 0))
    gu_spec = pl.BlockSpec((None, 2, tm, FF_BLOCK), lambda m, j: (j, 0, m, 0))
    return pl.pallas_call(
        body, name="ffn_bwd",
        out_shape=[jax.ShapeDtypeStruct((4, 2, t, FF_BLOCK), BF16), jax.ShapeDtypeStruct((t, D_MODEL), F32),
                   jax.ShapeDtypeStruct((t, D_MODEL), BF16), jax.ShapeDtypeStruct((1, D_MODEL), F32)],
        grid=(t // tm, 4),
        in_specs=[tok, tok, gu_spec, tok, RESIDENT, RESIDENT, vec],
        out_specs=[gu_spec, tok, tok, vec],
        scratch_shapes=[pltpu.VMEM((tm, D_MODEL), F32), pltpu.VMEM((2, tm, FF_BLOCK), BF16)],
        compiler_params=_params([((tm, D_MODEL), BF16), ((tm, D_MODEL), F32), ((2, tm, 768), F32), ((tm, D_MODEL), F32),
                                 ((2, tm, 768), BF16), ((tm, D_MODEL), F32), ((tm, D_MODEL), BF16)],
                                scratch=[((tm, D_MODEL), F32), ((2, tm, 768), BF16), ((N_DEV, FF_BLOCK, D_MODEL), BF16),
                                         ((D_FF, D_MODEL), BF16)],
                                temps=4 << 20, sem=("arbitrary", "arbitrary")),
    )(dx2b, dx2, gu, x1, w_gu, w_down.reshape(4, FF_BLOCK, D_MODEL), ffn_g)


def _branch_out_backward(dx1b, ya, yb, proj, w_a, w_b, w_out):
    t = ya.shape[0]
    tm = _tile(t, 512)

    def body(dx_ref, ya_ref, yb_ref, ga_ref, gb_ref, wa_ref, wb_ref, wo_ref, dya_ref, dyb_ref, dgate_ref, da_ref, dog_ref):
        dm = _dot_nt(dx_ref[...], wo_ref[...])
        sa, sb = _sigmoid(ga_ref[...]), _sigmoid(gb_ref[...])
        dya = (dm * sa).astype(BF16)
        dyb = (dm * sb).astype(BF16)
        dya_ref[...] = dya
        dyb_ref[...] = dyb
        dgate_ref[0] = (dm * ya_ref[...] * (sa * (1.0 - sa))).astype(BF16)
        dgate_ref[1] = (dm * yb_ref[...] * (sb * (1.0 - sb))).astype(BF16)
        da_ref[...] = _dot_nt(dya, wa_ref[...])
        dog_ref[...] = _dot_nt(dyb, wb_ref[...])

    tok = pl.BlockSpec((tm, D_MODEL), lambda m: (m, 0))
    return pl.pallas_call(
        body, name="branch_out_bwd",
        out_shape=[jax.ShapeDtypeStruct((t, D_MODEL), BF16), jax.ShapeDtypeStruct((t, D_MODEL), BF16),
                   jax.ShapeDtypeStruct((N_DEV, t, D_MODEL), BF16), jax.ShapeDtypeStruct((t, D_MODEL), F32),
                   jax.ShapeDtypeStruct((t, D_MODEL), F32)],
        grid=(t // tm,),
        in_specs=[tok, tok, tok, pl.BlockSpec((None, tm, D_MODEL), lambda m: (GATE_POS, m, 0)),
                  pl.BlockSpec((None, tm, D_MODEL), lambda m: (GATE_POS + 1, m, 0)), RESIDENT, RESIDENT, RESIDENT],
        out_specs=[tok, tok, pl.BlockSpec((2, tm, D_MODEL), lambda m: (GATE_POS // 2, m, 0)), tok, tok],
        compiler_params=_params([((tm, D_MODEL), BF16)] * 5 + [((tm, D_MODEL), F32)] * 6, scratch=[((D_MODEL, D_MODEL), BF16)] * 3,
                                temps=8 << 20, sem=("arbitrary",)),
    )(dx1b, ya, yb, proj, proj, w_a, w_b, w_out)


def _hgrn_backward(dproj, dog, o_saved, states, proj, lb_table, norm_g):
    t = proj.shape[1]
    tb = _tile(t, 1024)
    nc = tb // HGRN_CHUNK
    nb = t // tb

    def body(_, dog_ref, o_ref, st_ref, q_ref, f_ref, i_ref, g_ref, tab_ref, ng_ref, dp_ref, dng_ref, dtab_ref, gstate):
        @pl.when(pl.program_id(1) == 0)
        def _():
            gstate[...] = jnp.zeros_like(gstate)
            dng_ref[...] = jnp.zeros_like(dng_ref)
            dtab_ref[...] = jnp.zeros_like(dtab_ref)

        lb = _lower_bound(tab_ref)
        ng = ng_ref[...]
        lower, upper = _tri_masks()
        gt = _hgrn_gates(q_ref[...], f_ref[...], lb, nc)
        qi, ki, kd, qe = (gt[n].astype(BF16) for n in ("qi", "ki", "kd", "qe"))
        vb = i_ref[...].astype(BF16)
        o, gz, d_og = o_ref[...], g_ref[...], dog_ref[...]
        r, oh = _rms_stats(o)
        sg = _sigmoid(gz)
        d_on = d_og * (gz * sg)
        dgz = d_og * (oh * ng) * (sg * (1.0 + gz * (1.0 - sg)))
        dng_ref[...] += jnp.sum(d_on * oh, axis=0, keepdims=True)
        doh = d_on * ng
        dob = (r * (doh - oh * jnp.mean(doh * oh, axis=-1, keepdims=True))).astype(BF16)
        dv_intra, dqi, dki, dqe, g_upd = [], [], [], [], []
        for c in range(nc):
            rows = _chunk_rows(c)
            p = jnp.where(lower, _dot_nt(qi[rows], ki[rows]), 0.0).astype(BF16)
            dv_intra.append(_dot_tn(p, dob[rows]))
            dp = jnp.where(lower, _dot_nt(dob[rows], vb[rows]), 0.0).astype(BF16)
            dqi.append(_dot(dp, ki[rows]))
            dki.append(_dot_tn(dp, qi[rows]))
            dqe.append(_dot(dob[rows], st_ref[c].astype(BF16)))
            g_upd.append(_dot_tn(dob[rows], qe[rows]))
        g_after = [None] * nc
        g = gstate[...]
        for c in reversed(range(nc)):
            g_after[c] = g
            g = g * gt["decay"][c] + g_upd[c]
        gstate[...] = g
        dkd, dv, da_last = [], [], []
        for c in range(nc):
            rows = _chunk_rows(c)
            gb = g_after[c].astype(BF16)
            dkd.append(_dot(vb[rows], gb))
            dv.append(dv_intra[c] + _dot_nt(kd[rows], gb))
            da_last.append(jnp.sum(g_after[c] * st_ref[c], axis=0, keepdims=True) * gt["decay"][c])
        dqi, dki, dqe, dkd, dv = (jnp.concatenate(z, axis=0) for z in (dqi, dki, dqe, dkd, dv))
        dqs = dqi * gt["e_in"] + dqe * gt["e_all"]
        dk = dki * gt["e_out"] + dkd * gt["e_end"]
        t_in, t_out, t_end = dqi * gt["qi"], dki * gt["ki"], dkd * gt["kd"]
        da = t_in - t_out + dqe * gt["qe"] - t_end
        row = lax.broadcasted_iota(jnp.int32, (HGRN_CHUNK, HEAD_DIM), 0)
        d_mid = t_out - t_in
        pieces = []
        for c in range(nc):
            rows = _chunk_rows(c)
            da_mid = jnp.sum(d_mid[rows], axis=0, keepdims=True)
            da_end = jnp.sum(t_end[rows], axis=0, keepdims=True) + da_last[c]
            da_c = da[rows] + jnp.where(row == HGRN_CHUNK // 2 - 1, da_mid, 0.0) + jnp.where(row == HGRN_CHUNK - 1, da_end, 0.0)
            pieces.append(_mask_mm(upper.astype(BF16), da_c))
        df = jnp.concatenate(pieces, axis=0) / gt["f"] - dk
        s = gt["s"]
        dlb = jnp.sum(df * (1.0 - s), axis=0, keepdims=True)
        dp_ref[0] = (dqs * HGRN_SCALE).astype(BF16)
        dp_ref[1] = (df * (1.0 - lb) * (s * (1.0 - s))).astype(BF16)
        dp_ref[2] = dv.astype(BF16)
        dp_ref[3] = dgz.astype(BF16)
        dt0 = dlb * (lb * (1.0 - lb))
        dtab_ref[0:1, :] += dt0
        dtab_ref[1:2, :] -= dt0

    def blk(p):
        return pl.BlockSpec((None, tb, HEAD_DIM), lambda h, n: (p, nb - 1 - n, h))

    tok = pl.BlockSpec((tb, HEAD_DIM), lambda h, n: (nb - 1 - n, h))
    return pl.pallas_call(
        body, name="hgrn_bwd",
        out_shape=[jax.ShapeDtypeStruct((N_DEV, t, D_MODEL), BF16), jax.ShapeDtypeStruct((1, D_MODEL), F32),
                   jax.ShapeDtypeStruct((2, D_MODEL), F32)],
        grid=(HEADS, nb),
        in_specs=[ANY, tok, tok, pl.BlockSpec((None, nc, HEAD_DIM, HEAD_DIM), lambda h, n: (h, nb - 1 - n, 0, 0)),
                  blk(Q_POS), blk(Q_POS + 1), blk(Q_POS + 2), blk(Q_POS + 3),
                  pl.BlockSpec((2, HEAD_DIM), lambda h, n: (0, h)), pl.BlockSpec((1, HEAD_DIM), lambda h, n: (0, h))],
        out_specs=[pl.BlockSpec((4, tb, HEAD_DIM), lambda h, n: (0, nb - 1 - n, h)),
                   pl.BlockSpec((1, HEAD_DIM), lambda h, n: (0, h)), pl.BlockSpec((2, HEAD_DIM), lambda h, n: (0, h))],
        scratch_shapes=[pltpu.VMEM((HEAD_DIM, HEAD_DIM), F32)],
        input_output_aliases={0: 0},
        compiler_params=_params([((tb, HEAD_DIM), F32)] * 6 + [((nc, HEAD_DIM, HEAD_DIM), F32)] + [((4, tb, HEAD_DIM), BF16)],
                                temps=8 << 20, sem=("arbitrary", "arbitrary")),
    )(dproj, dog, o_saved, states, proj, proj, proj, proj, lb_table, norm_g)


def _gmlp_backward(dproj, da, proj, ln_g, ln_b, w_s, bias_b):
    t = proj.shape[1]
    tm = _tile(t, 256)
    chunks = tm // GMLP_CHUNK

    def body(_, da_ref, u_ref, v_ref, lng_ref, lnb_ref, ws_ref, bias_ref, dp_ref, dlng_ref, dlnb_ref, dws_ref, dbs_ref,
             vn_scr, dvn_scr):
        @pl.when(pl.program_id(0) == 0)
        def _():
            dlng_ref[...] = jnp.zeros_like(dlng_ref)
            dlnb_ref[...] = jnp.zeros_like(dlnb_ref)
            dws_ref[...] = jnp.zeros_like(dws_ref)
            dbs_ref[...] = jnp.zeros_like(dbs_ref)

        v = v_ref[...]
        vv, dvv_dv = _gelu_and_grad(v)
        mu = jnp.mean(vv, axis=-1, keepdims=True)
        cen = vv - mu
        rstd = lax.rsqrt(jnp.mean(cen * cen, axis=-1, keepdims=True) + NORM_EPS)
        vhat = cen * rstd
        lng = lng_ref[...]
        vn_scr[...] = (vhat * lng + lnb_ref[...]).astype(BF16)
        row = lax.broadcasted_iota(jnp.int32, (GMLP_CHUNK, GMLP_CHUNK), 0)
        col = lax.broadcasted_iota(jnp.int32, (GMLP_CHUNK, GMLP_CHUNK), 1)
        for g in range(GROUPS):
            wm = _masked_ws(ws_ref, g)
            cols = slice(g * HEAD_DIM, (g + 1) * HEAD_DIM)
            dws = jnp.zeros((GMLP_CHUNK, GMLP_CHUNK), F32)
            dbs = jnp.zeros((GMLP_CHUNK, GMLP_CHUNK), F32)
            for c in range(chunks):
                rows = slice(c * GMLP_CHUNK, (c + 1) * GMLP_CHUNK)
                vn = vn_scr[rows, cols]
                mixed = _dot(wm, vn) + bias_ref[g]
                u = u_ref[rows, cols]
                d_a = da_ref[rows, cols]
                gelu_u, dgelu_u = _gelu_and_grad(u)
                dp_ref[0, rows, cols] = (d_a * mixed * dgelu_u).astype(BF16)
                dmix = d_a * gelu_u
                dmb = dmix.astype(BF16)
                dbs = dbs + dmix
                dws = dws + _dot_nt(dmb, vn)
                dvn_scr[rows, cols] = _dot_tn(wm, dmb)
            dws_ref[g] += jnp.where(row >= col, dws, 0.0)
            dbs_ref[g] += jnp.broadcast_to(jnp.sum(dbs, axis=-1, keepdims=True), (GMLP_CHUNK, GMLP_CHUNK))
        dvn = dvn_scr[...]
        dlng_ref[...] += jnp.sum(dvn * vhat, axis=0, keepdims=True)
        dlnb_ref[...] += jnp.sum(dvn, axis=0, keepdims=True)
        dvh = dvn * lng
        dvv = rstd * (dvh - jnp.mean(dvh, axis=-1, keepdims=True) - vhat * jnp.mean(dvh * vhat, axis=-1, keepdims=True))
        dp_ref[1] = (dvv * dvv_dv).astype(BF16)

    tok = pl.BlockSpec((tm, D_MODEL), lambda m: (m, 0))
    small = pl.BlockSpec((GROUPS, GMLP_CHUNK, GMLP_CHUNK), lambda m: (0, 0, 0))
    vec = pl.BlockSpec((1, D_MODEL), lambda m: (0, 0))
    return pl.pallas_call(
        body, name="gmlp_bwd",
        out_shape=[jax.ShapeDtypeStruct(dproj.shape, BF16), jax.ShapeDtypeStruct((1, D_MODEL), F32),
                   jax.ShapeDtypeStruct((1, D_MODEL), F32), jax.ShapeDtypeStruct((GROUPS, GMLP_CHUNK, GMLP_CHUNK), F32),
                   jax.ShapeDtypeStruct((GROUPS, GMLP_CHUNK, GMLP_CHUNK), F32)],
        grid=(t // tm,),
        in_specs=[ANY, tok, pl.BlockSpec((None, tm, D_MODEL), lambda m: (U_POS, m, 0)),
                  pl.BlockSpec((None, tm, D_MODEL), lambda m: (U_POS + 1, m, 0)), vec, vec, small, small],
        out_specs=[pl.BlockSpec((2, tm, D_MODEL), lambda m: (U_POS // 2, m, 0)), vec, vec, small, small],
        scratch_shapes=[pltpu.VMEM((tm, D_MODEL), BF16), pltpu.VMEM((tm, D_MODEL), F32)],
        input_output_aliases={0: 0},
        compiler_params=_params([((tm, D_MODEL), F32)] * 3 + [((2, tm, D_MODEL), BF16)] + [((8, 128, 128), F32)] * 4,
                                scratch=[((tm, D_MODEL), BF16), ((tm, D_MODEL), F32)], temps=12 << 20, sem=("arbitrary",)),
    )(dproj, da, proj, proj, ln_g, ln_b, w_s, bias_b)


def _input_backward(dproj, w_in_g, x, dx1, mix_g):
    t = x.shape[0]
    tm = _tile(t, 512)

    def body(dp_ref, w_ref, x_ref, dx1_ref, g_ref, dx_ref, dg_ref):
        @pl.when(pl.program_id(0) == 0)
        def _():
            dg_ref[...] = jnp.zeros_like(dg_ref)

        dh = _dot_nt(dp_ref[0], w_ref[0])
        for p in range(1, N_DEV):
            dh = dh + _dot_nt(dp_ref[p], w_ref[p])
        dx, dg = _rms_bwd(dh, x_ref[...], g_ref[...])
        dx_ref[...] = dx1_ref[...] + dx
        dg_ref[...] += dg

    tok = pl.BlockSpec((tm, D_MODEL), lambda m: (m, 0))
    vec = pl.BlockSpec((1, D_MODEL), lambda m: (0, 0))
    return pl.pallas_call(
        body, name="input_bwd",
        out_shape=[jax.ShapeDtypeStruct((t, D_MODEL), F32), jax.ShapeDtypeStruct((1, D_MODEL), F32)],
        grid=(t // tm,),
        in_specs=[pl.BlockSpec((N_DEV, tm, D_MODEL), lambda m: (0, m, 0)), RESIDENT, tok, tok, vec],
        out_specs=[tok, vec],
        compiler_params=_params([((N_DEV, tm, D_MODEL), BF16)] + [((tm, D_MODEL), F32)] * 3,
                                scratch=[((N_DEV, D_MODEL, D_MODEL), BF16)], temps=6 << 20, sem=("arbitrary",)),
    )(dproj, w_in_g, x, dx1, mix_g)


def _weight_grad(name, a, b, a_spec, b_spec, out_shape, out_spec, steps, blocks, a_is_transposed):
    def body(a_ref, b_ref, o_ref):
        o_ref[...] = _dot(a_ref[...], b_ref[...]) if a_is_transposed else _dot_tn(a_ref[...], b_ref[...])

    return pl.pallas_call(
        body, name=name, out_shape=jax.ShapeDtypeStruct(out_shape, F32), grid=(steps,), in_specs=[a_spec, b_spec],
        out_specs=out_spec, compiler_params=_params(blocks, temps=4 << 20, sem=("arbitrary",)),
    )(a, b)


def _pack_small(mix_g, ln_g, ln_b, b_s, lb_table, hg_norm, ffn_g, final_g, loss_row):
    def part(a):
        a = a.reshape(-1, D_MODEL)
        return jnp.pad(a, ((0, 8 - a.shape[0]), (0, 0)))

    return jnp.concatenate([part(mix_g), part(ln_g), part(ln_b), part(hg_norm), part(ffn_g), part(final_g),
                            part(lb_table), part(b_s), part(loss_row)], axis=0)


SMALL_PARTS = (("gmlp_ln_g", 8, 1), ("gmlp_ln_b", 16, 1), ("hgrn_norm_g", 24, 1), ("norm_ffn_g", 32, 1), ("norm_final_g", 40, 1),
               ("hgrn_lb_table", 48, 2))


def _adamw_small_unpacked(gathered, w, m, v):
    rows = w.shape[0]
    n_out = len(SMALL_PARTS) + 1

    def body(p_ref, w_ref, m_ref, v_ref, *outs):
        g = p_ref[0]
        for j in range(1, N_DEV):
            g = g + p_ref[j]
        delta, m_new, v_new = _adamw_math(w_ref[...], g, m_ref[...], v_ref[...])
        for kind, val in enumerate((g, delta, m_new, v_new)):
            refs = outs[kind * n_out:(kind + 1) * n_out]
            for (_, first, count), ref in zip(SMALL_PARTS, refs):
                ref[...] = val[first:first + count]
            for grp in range(GROUPS):
                refs[-1][0, grp:grp + 1, :] = val[56:57, grp * GMLP_CHUNK:(grp + 1) * GMLP_CHUNK]
        outs[-1][...] = g[SMALL_ROWS - 8:SMALL_ROWS - 7]

    shapes = [jax.ShapeDtypeStruct((count, D_MODEL), F32) for _, _, count in SMALL_PARTS]
    shapes.append(jax.ShapeDtypeStruct((1, GROUPS, GMLP_CHUNK), F32))
    whole = pl.BlockSpec((rows, D_MODEL), lambda: (0, 0))
    res = pl.pallas_call(
        body, name="adamw_small", out_shape=shapes * 4 + [jax.ShapeDtypeStruct((1, D_MODEL), F32)],
        in_specs=[pl.BlockSpec((N_DEV, rows, D_MODEL), lambda: (0, 0, 0)), whole, whole, whole],
        compiler_params=_params([((N_DEV, rows, D_MODEL), F32)] + [((rows, D_MODEL), F32)] * 7),
    )(gathered, w, m, v)
    names = [nme for nme, _, _ in SMALL_PARTS] + ["gmlp_b_s"]
    return [dict(zip(names, res[kind * n_out:(kind + 1) * n_out])) for kind in range(4)], res[-1]


def _adamw_row(name, gathered, w, m, v):
    def body(p_ref, w_ref, m_ref, v_ref, g_out, d_out, m_out, v_out):
        g = p_ref[0, 0:1, :]
        for j in range(1, N_DEV):
            g = g + p_ref[j, 0:1, :]
        delta, m_new, v_new = _adamw_math(w_ref[...], g, m_ref[...], v_ref[...])
        g_out[...] = g
        d_out[...] = delta
        m_out[...] = m_new
        v_out[...] = v_new

    return pl.pallas_call(
        body, name=name, out_shape=[jax.ShapeDtypeStruct((1, D_MODEL), F32)] * 4,
        compiler_params=_params([((N_DEV, 8, D_MODEL), F32)] + [((8, D_MODEL), F32)] * 7),
    )(gathered, w, m, v)


def _adamw_small(name, gathered, w, m, v):
    rows, cols = w.shape

    def body(p_ref, w_ref, m_ref, v_ref, g_out, d_out, m_out, v_out):
        g = p_ref[0]
        for j in range(1, N_DEV):
            g = g + p_ref[j]
        delta, m_new, v_new = _adamw_math(w_ref[...], g, m_ref[...], v_ref[...])
        g_out[...] = g
        d_out[...] = delta
        m_out[...] = m_new
        v_out[...] = v_new

    tr = _tile(rows, 512)
    spec = pl.BlockSpec((tr, cols), lambda r: (r, 0))
    return pl.pallas_call(
        body, name=name, out_shape=[jax.ShapeDtypeStruct((rows, cols), F32)] * 4, grid=(rows // tr,),
        in_specs=[pl.BlockSpec((N_DEV, tr, cols), lambda r: (0, r, 0)), spec, spec, spec], out_specs=[spec] * 4,
        compiler_params=_params([((N_DEV, tr, cols), F32)] + [((tr, cols), F32)] * 7, sem=("arbitrary",)),
    )(gathered, w, m, v)


def kernel(x, norm_mix_g, w_in, gmlp_ln_g, gmlp_ln_b, gmlp_w_s, gmlp_b_s, hgrn_lb_table, hgrn_norm_g, w_branch_a, w_branch_b, w_out, norm_ffn_g, w_gate_up, w_down, norm_final_g, loss_target, m_norm_mix_g, m_w_in, m_gmlp_ln_g, m_gmlp_ln_b, m_gmlp_w_s, m_gmlp_b_s, m_hgrn_lb_table, m_hgrn_norm_g, m_w_branch_a, m_w_branch_b, m_w_out, m_norm_ffn_g, m_w_gate_up, m_w_down, m_norm_final_g, v_norm_mix_g, v_w_in, v_gmlp_ln_g, v_gmlp_ln_b, v_gmlp_w_s, v_gmlp_b_s, v_hgrn_lb_table, v_hgrn_norm_g, v_w_branch_a, v_w_branch_b, v_w_out, v_norm_ffn_g, v_w_gate_up, v_w_down, v_norm_final_g):
    t = x.shape[1]
    x2d = x.reshape(t, D_MODEL)
    target = loss_target.reshape(t, D_MODEL)
    final_g = norm_final_g.reshape(1, D_MODEL)

    w_in_shard = w_in[0].astype(BF16)

    def rows_of(n):
        return lambda ref, j: ref.at[pl.ds(pl.multiple_of(j * n, 8), n)]

    gathered = [((N_DEV, D_MODEL, D_MODEL), BF16), ((D_MODEL, D_MODEL), BF16), ((D_MODEL, D_MODEL), BF16),
                ((D_MODEL, D_MODEL), BF16), ((N_DEV, FF_BLOCK, D_MODEL), BF16), ((D_FF, D_MODEL), BF16)]
    places = [lambda ref, j: ref.at[_pos_of_dev(j)], rows_of(BRANCH_ROWS), rows_of(BRANCH_ROWS), rows_of(BRANCH_ROWS),
              lambda ref, j: ref.at[j], rows_of(DOWN_ROWS)]
    w_in_g = _all_gather_balanced_async("w_in_all_gather", 9, w_in_shard, gathered[0], places[0], D_MODEL)
    w_in_sibling = _swap_with_sibling("w_in_from_sibling", w_in_shard)

    core_i, chip_i = lax.axis_index("c"), 2 * lax.axis_index("x") + lax.axis_index("y")
    own_pos = jnp.stack([_pos_of_dev(2 * chip_i + core_i), _pos_of_dev(2 * chip_i + 1 - core_i)]).astype(jnp.int32)
    other_pos = jnp.stack([_pos_of_dev(2 * jnp.bitwise_xor(chip_i, q) + cc) for q in (1, 2, 3) for cc in (0, 1)]).astype(jnp.int32)
    proj, h, h_t = _proj_forward_own_chip(own_pos, x2d, norm_mix_g, w_in_shard, w_in_sibling)

    small_w = [norm_mix_g, gmlp_ln_g, gmlp_ln_b, gmlp_b_s, hgrn_lb_table, hgrn_norm_g, norm_ffn_g, norm_final_g]
    small_m = [m_norm_mix_g, m_gmlp_ln_g, m_gmlp_ln_b, m_gmlp_b_s, m_hgrn_lb_table, m_hgrn_norm_g, m_norm_ffn_g, m_norm_final_g]
    small_v = [v_norm_mix_g, v_gmlp_ln_g, v_gmlp_ln_b, v_gmlp_b_s, v_hgrn_lb_table, v_hgrn_norm_g, v_norm_ffn_g, v_norm_final_g]
    _, (raw, small_w, small_m, small_v) = lax.optimization_barrier(
        (h, ([w_branch_a[0], w_branch_b[0], w_out[0], w_gate_up[0], w_down[0]], small_w, small_m, small_v)))
    later = [raw[0].astype(BF16), raw[1].astype(BF16), raw[2].astype(BF16),
             raw[3].T.astype(BF16), raw[4].astype(BF16)]
    w_a, w_b, w_o, w_gu, w_dn = _all_gather_async("weights_all_gather", 0, later, gathered[1:], places[1:])
    no_row = jnp.zeros((1, D_MODEL), F32)
    w_pack, m_pack, v_pack = (_pack_small(*vals, no_row) for vals in (small_w, small_m, small_v))
    bias_b = jnp.broadcast_to(small_w[3][0][:, :, None], (GROUPS, GMLP_CHUNK, GMLP_CHUNK))
    h_later, _ = lax.optimization_barrier((h, (later, w_pack, m_pack, v_pack, bias_b)))
    proj = _proj_forward_other_chips(other_pos, proj, h_later, w_in_g)
    a = _gmlp_forward(proj, gmlp_ln_g, gmlp_ln_b, gmlp_w_s[0], bias_b)
    og, o_saved, states = _hgrn_forward(proj, hgrn_lb_table, hgrn_norm_g)
    ya, yb, merged, x1, h2 = _branch_out_forward(a, og, proj, x2d, w_a, w_b, w_o, norm_ffn_g)
    gu, act, loss_tile, d_final_g, dx2, dx2b = _ffn_forward(h2, x1, w_gu, w_dn, target, final_g)

    core = lax.axis_index("c").astype(jnp.int32).reshape(1)
    chip = (2 * lax.axis_index("x") + lax.axis_index("y")).astype(jnp.int32).reshape(1)
    branch_rows, branch_shape = rows_of(BRANCH_ROWS), (BRANCH_ROWS, D_MODEL)
    branch_block = ((BRANCH_ROWS, D_MODEL), lambda q, r, c: (2 * q + c, 0))

    def chip_partials(names, grads, land, own_blocks):
        return [_chip_partial("chip_partial_" + nme, core, g_, blk, idx, l_)
                for nme, g_, (blk, idx), l_ in zip(names, grads, own_blocks, land)]

    whole = pl.BlockSpec((t, D_MODEL), lambda n: (0, 0))
    whole_t = pl.BlockSpec((D_MODEL, t), lambda n: (0, 0))
    col_blocks = [((t, D_MODEL), BF16), ((t, 256), BF16), ((D_MODEL, 256), F32)]

    def square_grad(name, a_, b_):
        return _weight_grad(name, a_, b_, whole, pl.BlockSpec((t, 256), lambda n: (0, n)), (D_MODEL, D_MODEL),
                            pl.BlockSpec((D_MODEL, 256), lambda n: (0, n)), D_MODEL // 256, col_blocks, False)

    dgu, dx1, dx1b, d_ffn_g = _ffn_backward(dx2b, dx2, gu, x1, w_gu, w_dn, norm_ffn_g)
    g_gu = _weight_grad(
        "grad_w_gate_up", dgu, h2, pl.BlockSpec((None, None, t, FF_BLOCK), lambda j: (j % 4, j // 4, 0, 0)), whole,
        (N_DEV, FF_BLOCK, D_MODEL), pl.BlockSpec((None, FF_BLOCK, D_MODEL), lambda j: (j, 0, 0)), N_DEV,
        [((t, 768), BF16), ((t, D_MODEL), BF16), ((FF_BLOCK, D_MODEL), F32)], False)
    g_dn = _weight_grad(
        "grad_w_down", act, dx2b, pl.BlockSpec((None, t, FF_BLOCK), lambda j: (j, 0, 0)), whole, (D_FF, D_MODEL),
        pl.BlockSpec((FF_BLOCK, D_MODEL), lambda j: (j, 0)), 4,
        [((t, 768), BF16), ((t, D_MODEL), BF16), ((FF_BLOCK, D_MODEL), F32)], False)
    names_f, grads_f = ["w_gate_up", "w_down"], [g_gu, g_dn]
    land_f = _exchange_sibling("ffn_grads_to_sibling", 2, grads_f, [lambda ref, j: ref.at[j], rows_of(DOWN_ROWS)],
                               [(FF_BLOCK, D_MODEL), (DOWN_ROWS, D_MODEL)])

    dx1b_later, _ = lax.optimization_barrier((dx1b, grads_f))
    dya, dyb, dproj, da, dog = _branch_out_backward(dx1b_later, ya, yb, proj, w_a, w_b, w_o)
    g_a = square_grad("grad_w_a", a, dya)
    g_b = square_grad("grad_w_b", og, dyb)
    g_o = square_grad("grad_w_out", merged, dx1b)
    names_b, grads_b = ["w_branch_a", "w_branch_b", "w_out"], [g_a, g_b, g_o]
    land_b = _exchange_sibling("branch_grads_to_sibling", 3, grads_b, [branch_rows] * 3, [branch_shape] * 3)

    part_f = chip_partials(names_f, grads_f, land_f,
                           [((None, FF_BLOCK, D_MODEL), lambda q, r, c: (2 * q + c, 0, 0)),
                            ((DOWN_ROWS, D_MODEL), lambda q, r, c: (2 * q + c, 0))])
    landed_f = _exchange_chips("ffn_grads_to_chips", 5, part_f)

    dog, _ = lax.optimization_barrier((dog, part_f))
    dproj, d_hg_norm, d_lb = _hgrn_backward(dproj, dog, o_saved, states, proj, hgrn_lb_table, hgrn_norm_g)

    land_b, _ = lax.optimization_barrier((land_b, part_f))
    part_b = chip_partials(names_b, grads_b, land_b, [branch_block] * 3)
    landed_b = _exchange_chips("branch_grads_to_chips", 6, part_b)

    da, _ = lax.optimization_barrier((da, part_b))
    dproj, d_ln_g, d_ln_b, d_ws, d_bs = _gmlp_backward(dproj, da, proj, gmlp_ln_g, gmlp_ln_b, gmlp_w_s[0], bias_b)

    def flat_ws(a):
        return a.reshape(GROUPS * GMLP_CHUNK, GMLP_CHUNK)

    small_partial = _pack_small(no_row, d_ln_g, d_ln_b, d_bs[:, :, 0], d_lb, d_hg_norm, d_ffn_g, d_final_g,
                                jnp.tile(loss_tile[0:1], (1, D_MODEL // 128)))
    small_all, ws_all = _all_gather_async(
        "small_grads_all_gather", 1, [small_partial, flat_ws(d_ws)],
        [((N_DEV, SMALL_ROWS, D_MODEL), F32), ((N_DEV, GROUPS * GMLP_CHUNK, GMLP_CHUNK), F32)],
        [lambda ref, j: ref.at[j], lambda ref, j: ref.at[j]])

    g_in = _weight_grad(
        "grad_w_in", h_t, dproj, whole_t, pl.BlockSpec((None, t, D_MODEL), lambda p: (p, 0, 0)), (N_DEV, D_MODEL, D_MODEL),
        pl.BlockSpec((None, D_MODEL, D_MODEL), lambda p: (p, 0, 0)), N_DEV,
        [((D_MODEL, t), BF16), ((t, D_MODEL), BF16), ((D_MODEL, D_MODEL), F32)], True)
    land_i = _exchange_sibling("w_in_grads_to_sibling", 4, [g_in], [lambda ref, j: ref.at[_pos_of_dev(j)]],
                               [(D_MODEL, D_MODEL)])

    (landed_f, landed_b), _ = lax.optimization_barrier(((landed_f, landed_b), g_in))
    big = {}
    for nme, own, lnd, w, m, v in zip(
            names_f + names_b, part_f + part_b, landed_f + landed_b,
            [w_gate_up, w_down, w_branch_a, w_branch_b, w_out], [m_w_gate_up, m_w_down, m_w_branch_a, m_w_branch_b, m_w_out],
            [v_w_gate_up, v_w_down, v_w_branch_a, v_w_branch_b, v_w_out]):
        flip = (lambda z: z.T) if nme == "w_gate_up" else (lambda z: z)
        big[nme] = [flip(o_)[None] for o_ in _adamw("adamw_" + nme, chip, own, lnd, flip(w[0]), flip(m[0]), flip(v[0]))]
    small, loss_row = _adamw_small_unpacked(small_all, w_pack, m_pack, v_pack)
    ws_outs = _adamw_small("adamw_w_s", ws_all, flat_ws(gmlp_w_s), flat_ws(m_gmlp_w_s), flat_ws(v_gmlp_w_s))
    land_i, _ = lax.optimization_barrier((land_i, (big, small, ws_outs)))
    part_i = chip_partials(["w_in"], [g_in], land_i,
                           [((None, D_MODEL, D_MODEL), lambda q, r, c: (_pos_of_dev(2 * q + c), 0, 0))])
    landed_i = _exchange_chips("w_in_grads_to_chips", 7, part_i)

    dx1, _ = lax.optimization_barrier((dx1, part_i))
    grad_x, d_mix_g = _input_backward(dproj, w_in_g, x2d, dx1, norm_mix_g)
    big["w_in"] = [o_[None] for o_ in _adamw("adamw_w_in", chip, part_i[0], landed_i[0], w_in[0], m_w_in[0], v_w_in[0])]

    def row8(a):
        return jnp.pad(a, ((0, 7), (0, 0)))

    d_mix_g, _ = lax.optimization_barrier((d_mix_g, landed_i))
    (mix_all,) = _all_gather_async("mix_gain_grad_all_gather", 8, [row8(d_mix_g)], [((N_DEV, 8, D_MODEL), F32)],
                                   [lambda ref, j: ref.at[j]])
    mix_outs = _adamw_row("adamw_mix_gain", mix_all, norm_mix_g, m_norm_mix_g, v_norm_mix_g)
    small = [dict(p, norm_final_g=p["norm_final_g"][0], gmlp_w_s=ws.reshape(1, GROUPS, GMLP_CHUNK, GMLP_CHUNK), norm_mix_g=q)
             for p, ws, q in zip(small, ws_outs, mix_outs)]

    loss = loss_row[0, 0]
    order = ["norm_mix_g", "w_in", "gmlp_ln_g", "gmlp_ln_b", "gmlp_w_s", "gmlp_b_s", "hgrn_lb_table", "hgrn_norm_g",
             "w_branch_a", "w_branch_b", "w_out", "norm_ffn_g", "w_gate_up", "w_down", "norm_final_g"]
    outs = [loss, grad_x.reshape(1, t, D_MODEL)]
    for kind in range(4):
        for nme in order:
            outs.append(big[nme][kind] if nme in big else small[kind][nme])
    return tuple(outs)
```

```python
import jax
import jax.numpy as jnp
from jax import lax
from jax.experimental import pallas as pl
from jax.experimental.pallas import tpu as pltpu
from jax.experimental.pallas import tpu_sc as plsc

F32, BF16 = jnp.float32, jnp.bfloat16
D_MODEL = 1024
N_DEV = 8
HEADS = 8
HEAD_DIM = 128
GROUPS = 8
GMLP_CHUNK = 128
HGRN_CHUNK = 64
HGRN_SCALE = HEAD_DIM ** -0.5
D_FF = 2816
FF_BLOCK = D_FF // 4
DOWN_ROWS = D_FF // N_DEV
BRANCH_ROWS = D_MODEL // N_DEV
NORM_EPS = 1e-6
ADAM_LR, ADAM_B1, ADAM_B2, ADAM_EPS, ADAM_WD, ADAM_STEP = 0.001, 0.9, 0.999, 1e-08, 0.01, 10
SMALL_ROWS = 72
V7X_VMEM_BYTES = 64 * 1024 * 1024
VMEM_CAP = V7X_VMEM_BYTES - 6 * 1024 * 1024
MESH_ID = pl.DeviceIdType.MESH
ANY = pl.BlockSpec(memory_space=pl.ANY)
RESIDENT = pl.BlockSpec(memory_space=pltpu.VMEM)
Q_POS, U_POS, GATE_POS = 0, 4, 6


def _pos_of_dev(j):
    return jnp.where(j < 2, j + 4, jnp.where(j < 6, j - 2, j))


def _nbytes(shape, dtype):
    n = 1
    for s in shape:
        n *= s
    return n * jnp.dtype(dtype).itemsize


def _params(blocks, scratch=(), temps=0, sem=None):
    need = 2 * sum(_nbytes(s, d) for s, d in blocks) + sum(_nbytes(s, d) for s, d in scratch) + temps
    assert need + (4 << 20) <= VMEM_CAP, need
    return pltpu.CompilerParams(dimension_semantics=sem, vmem_limit_bytes=VMEM_CAP)


def _tile(n, pref):
    return pref if n % pref == 0 else n


def _dot(a, b):
    return jnp.dot(a, b, preferred_element_type=F32)


def _dot_nt(a, b):
    return lax.dot_general(a, b, (((1,), (1,)), ((), ())), preferred_element_type=F32)


def _dot_tn(a, b):
    return lax.dot_general(a, b, (((0,), (0,)), ((), ())), preferred_element_type=F32)


def _sigmoid(x):
    return 1.0 / (1.0 + jnp.exp(-x))


_GELU_C = 0.7978845608028654


def _gelu(x):
    return x * (0.5 * (1.0 + jnp.tanh(_GELU_C * (x + 0.044715 * (x * x * x)))))


def _gelu_and_grad(x):
    t = jnp.tanh(_GELU_C * (x + 0.044715 * (x * x * x)))
    half = 0.5 * (1.0 + t)
    return x * half, half + 0.5 * x * (1.0 - t * t) * (_GELU_C * (1.0 + 3.0 * 0.044715 * x * x))


def _rms_stats(x):
    r = lax.rsqrt(jnp.mean(x * x, axis=-1, keepdims=True) + NORM_EPS)
    return r, x * r


def _rms_bwd(dy, x, g):
    r, xh = _rms_stats(x)
    dg = jnp.sum(dy * xh, axis=0, keepdims=True)
    dxh = dy * g
    dx = r * (dxh - xh * jnp.mean(dxh * xh, axis=-1, keepdims=True))
    return dx, dg


def _split3(x):
    hi = x.astype(BF16)
    r = x - hi.astype(F32)
    mid = r.astype(BF16)
    lo = (r - mid.astype(F32)).astype(BF16)
    return hi, mid, lo


def _mask_mm(mask_bf16, x):
    hi, mid, lo = _split3(x)
    return _dot(mask_bf16, hi) + _dot(mask_bf16, mid) + _dot(mask_bf16, lo)


def _place():
    return lax.axis_index("x"), lax.axis_index("y"), lax.axis_index("c")


def _gather_copies(src, out, send, recv, loc, slicers):
    n = len(src)
    x, y, c = _place()
    me, sib = (x, y, c), (x, y, 1 - c)
    chips = [(1 - x, y), (x, 1 - y), (1 - x, 1 - y)]

    def dev(p):
        return 4 * p[0] + 2 * p[1] + p[2]

    def rc(i, k, block, to, from_src=False):
        dst = slicers[i](out[i], dev(block))
        return pltpu.make_async_remote_copy(
            src_ref=src[i] if from_src else dst, dst_ref=dst, send_sem=send.at[7 * i + k],
            recv_sem=recv.at[7 * i + k], device_id=to, device_id_type=MESH_ID)

    mine = [pltpu.make_async_copy(src[i], slicers[i](out[i], dev(me)), loc.at[i]) for i in range(n)]
    for cp in mine:
        cp.start()
    first = []
    for i in range(n):
        first.append(rc(i, 0, me, sib, True))
        for j, chip in enumerate(chips):
            first.append(rc(i, 1 + j, me, (*chip, c), True))
    for cp in first:
        cp.start()
    passed = []
    for j, chip in enumerate(chips):
        for i in range(n):
            rc(i, 1 + j, (*chip, c), me).wait_recv()
            cp = rc(i, 4 + j, (*chip, c), sib)
            cp.start()
            passed.append(cp)
    for i in range(n):
        rc(i, 0, sib, me).wait_recv()
        for j, chip in enumerate(chips):
            rc(i, 4 + j, (*chip, 1 - c), me).wait_recv()
    for cp in first + passed:
        cp.wait_send()
    for cp in mine:
        cp.wait()


def _gather_copies_balanced(src, out, send, recv, loc, slicer, rows):
    x, y, c = _place()
    me, sib = (x, y, c), (x, y, 1 - c)
    xn, yn, dg = (1 - x, y), (x, 1 - y), (1 - x, 1 - y)
    half_rows = rows // 2

    def block(p):
        return slicer(out, 4 * p[0] + 2 * p[1] + p[2])

    def half(ref, h):
        return ref.at[pl.ds(h * half_rows, half_rows)]

    def rc(k, dst, to, from_src=False):
        return pltpu.make_async_remote_copy(src_ref=src if from_src else dst, dst_ref=dst, send_sem=send.at[k],
                                            recv_sem=recv.at[k], device_id=to, device_id_type=MESH_ID)

    mine = pltpu.make_async_copy(src, block(me), loc.at[0])
    mine.start()
    sends = [rc(0, block(me), sib, True), rc(1, block(me), (*xn, c), True), rc(2, block(me), (*yn, c), True)]
    for cp in sends:
        cp.start()

    def then(cp):
        cp.start()
        sends.append(cp)

    rc(1, block((*xn, c)), me).wait_recv()
    then(rc(3, half(block((*xn, c)), 0), (*yn, c)))
    then(rc(5, block((*xn, c)), sib))
    rc(2, block((*yn, c)), me).wait_recv()
    then(rc(4, half(block((*yn, c)), 1), (*xn, c)))
    then(rc(6, block((*yn, c)), sib))
    rc(3, half(block((*dg, c)), 0), me).wait_recv()
    then(rc(7, half(block((*dg, c)), 0), sib))
    rc(4, half(block((*dg, c)), 1), me).wait_recv()
    then(rc(8, half(block((*dg, c)), 1), sib))
    rc(0, block(sib), me).wait_recv()
    rc(5, block((*xn, 1 - c)), me).wait_recv()
    rc(6, block((*yn, 1 - c)), me).wait_recv()
    rc(7, half(block((*dg, 1 - c)), 0), me).wait_recv()
    rc(8, half(block((*dg, 1 - c)), 1), me).wait_recv()
    for cp in sends:
        cp.wait_send()
    mine.wait()


def _gather_scratch(n):
    return [pltpu.SemaphoreType.DMA((7 * n,)), pltpu.SemaphoreType.DMA((7 * n,)), pltpu.SemaphoreType.DMA((n,))]


def _handshake(peers):
    barrier = pltpu.get_barrier_semaphore()
    for peer in peers:
        pl.semaphore_signal(barrier, inc=1, device_id=peer, device_id_type=MESH_ID)
    pl.semaphore_wait(barrier, len(peers))


def _all_gather_async(name, collective_id, srcs, out_shapes, slicers):
    n = len(srcs)

    def body(*refs):
        x, y, c = _place()
        _handshake([(1 - x if dx else x, 1 - y if dy else y, 1 - c if dc else c)
                    for dx in (0, 1) for dy in (0, 1) for dc in (0, 1) if dx or dy or dc])
        _gather_copies(refs[:n], refs[n:2 * n], *refs[2 * n:], slicers)

    return _sequencer_call(name, collective_id, body, srcs, [jax.ShapeDtypeStruct(s, d) for s, d in out_shapes],
                           _gather_scratch(n))


def _all_gather_balanced_async(name, collective_id, src, out_shape, slicer, rows):
    def body(src_ref, out_ref, send, recv, loc):
        x, y, c = _place()
        _handshake([(1 - x if dx else x, 1 - y if dy else y, 1 - c if dc else c)
                    for dx in (0, 1) for dy in (0, 1) for dc in (0, 1) if dx or dy or dc])
        _gather_copies_balanced(src_ref, out_ref, send, recv, loc, slicer, rows)

    return _sequencer_call(name, collective_id, body, [src], [jax.ShapeDtypeStruct(*out_shape)],
                           [pltpu.SemaphoreType.DMA((9,)), pltpu.SemaphoreType.DMA((9,)), pltpu.SemaphoreType.DMA((1,))])[0]


def _sequencer_call(name, collective_id, body, operands, out_types, scratch):
    return pl.kernel(
        body, out_type=out_types, mesh=plsc.ScalarSubcoreMesh(axis_name="sequencer", num_cores=1), name=name,
        scratch_types=scratch, compiler_params=pltpu.CompilerParams(collective_id=collective_id),
    )(*operands)


def _exchange_sibling(name, collective_id, grads, shard_fns, shard_shapes):
    n = len(grads)

    def body(*refs):
        g, land = refs[:n], refs[n:2 * n]
        send, recv = refs[2 * n:]
        x, y, c = _place()
        _handshake([(x, y, 1 - c)])
        remote = []
        for i in range(n):
            for q in range(4):
                cp = pltpu.make_async_remote_copy(
                    src_ref=shard_fns[i](g[i], 2 * q + (1 - c)), dst_ref=land[i].at[q], send_sem=send.at[4 * i + q],
                    recv_sem=recv.at[4 * i + q], device_id=(x, y, 1 - c), device_id_type=MESH_ID)
                cp.start()
                remote.append(cp)
        for cp in remote:
            cp.wait()

    return _sequencer_call(name, collective_id, body, grads, [jax.ShapeDtypeStruct((4, *s), F32) for s in shard_shapes],
                           [pltpu.SemaphoreType.DMA((4 * n,)), pltpu.SemaphoreType.DMA((4 * n,))])


def _exchange_chips(name, collective_id, parts):
    n = len(parts)

    def body(*refs):
        part, out = refs[:n], refs[n:2 * n]
        send, recv = refs[2 * n:]
        x, y, c = _place()
        _handshake([(1 - x, y, c), (x, 1 - y, c), (1 - x, 1 - y, c)])
        remote = []
        for i in range(n):
            for s in range(3):
                qx = 1 - x if (s + 1) // 2 else x
                qy = 1 - y if (s + 1) % 2 else y
                cp = pltpu.make_async_remote_copy(
                    src_ref=part[i].at[2 * qx + qy], dst_ref=out[i].at[s], send_sem=send.at[3 * i + s],
                    recv_sem=recv.at[3 * i + s], device_id=(qx, qy, c), device_id_type=MESH_ID)
                cp.start()
                remote.append(cp)
        for cp in remote:
            cp.wait()

    return _sequencer_call(name, collective_id, body, parts,
                           [jax.ShapeDtypeStruct((3, *p.shape[1:]), p.dtype) for p in parts],
                           [pltpu.SemaphoreType.DMA((3 * n,)), pltpu.SemaphoreType.DMA((3 * n,))])


def _chip_partial(name, core, grad, own_block, own_index, land):
    _, rows, cols = land.shape
    tr = own_block[-2]

    def body(core_ref, a_ref, b_ref, o_ref):
        o_ref[...] = (a_ref[...] + b_ref[...]).astype(BF16)

    spec = pl.BlockSpec((None, tr, cols), lambda q, r, c: (q, r, 0))
    return pl.pallas_call(
        body, name=name, out_shape=jax.ShapeDtypeStruct(land.shape, BF16),
        grid_spec=pltpu.PrefetchScalarGridSpec(
            num_scalar_prefetch=1, grid=(4, rows // tr),
            in_specs=[pl.BlockSpec(own_block, lambda q, r, c: own_index(q, r, c[0])), spec], out_specs=spec),
        compiler_params=_params([((tr, cols), F32)] * 2 + [((tr, cols), BF16)], sem=("arbitrary", "arbitrary")),
    )(core, grad, land)


def _adamw_math(w, g, m, v):
    m = ADAM_B1 * m + (1.0 - ADAM_B1) * g
    v = ADAM_B2 * v + (1.0 - ADAM_B2) * (g * g)
    m_hat = m / (1.0 - ADAM_B1 ** ADAM_STEP)
    v_hat = v / (1.0 - ADAM_B2 ** ADAM_STEP)
    delta = -ADAM_LR * (m_hat / (jnp.sqrt(v_hat) + ADAM_EPS) + ADAM_WD * w)
    return delta, m, v


def _adamw(name, chip, own, landed, w, m, v):
    _, rows, cols = own.shape
    tr = _tile(rows, 512) if rows % 512 == 0 else _tile(rows, 176)

    def body(chip_ref, own_ref, l_ref, w_ref, m_ref, v_ref, g_out, d_out, m_out, v_out):
        g = own_ref[...].astype(F32)
        for s in range(3):
            g = g + l_ref[s].astype(F32)
        delta, m_new, v_new = _adamw_math(w_ref[...], g, m_ref[...], v_ref[...])
        g_out[...] = g
        d_out[...] = delta
        m_out[...] = m_new
        v_out[...] = v_new

    spec = pl.BlockSpec((tr, cols), lambda r, c: (r, 0))
    return pl.pallas_call(
        body, name=name, out_shape=[jax.ShapeDtypeStruct((rows, cols), F32)] * 4,
        grid_spec=pltpu.PrefetchScalarGridSpec(
            num_scalar_prefetch=1, grid=(rows // tr,),
            in_specs=[pl.BlockSpec((None, tr, cols), lambda r, c: (c[0], r, 0)),
                      pl.BlockSpec((3, tr, cols), lambda r, c: (0, r, 0)), spec, spec, spec],
            out_specs=[spec] * 4),
        compiler_params=_params([((4, tr, cols), own.dtype)] + [((tr, cols), F32)] * 7, sem=("arbitrary",)),
    )(chip, own, landed, w, m, v)


def _swap_with_sibling(name, x):
    def body(x_ref, o_ref, send, recv):
        px, py, c = _place()
        cp = pltpu.make_async_remote_copy(src_ref=x_ref, dst_ref=o_ref, send_sem=send, recv_sem=recv,
                                          device_id=(px, py, 1 - c), device_id_type=MESH_ID)
        cp.start()
        cp.wait()

    return pl.pallas_call(
        body, name=name, out_shape=jax.ShapeDtypeStruct(x.shape, x.dtype), in_specs=[ANY], out_specs=ANY,
        scratch_shapes=[pltpu.SemaphoreType.DMA, pltpu.SemaphoreType.DMA],
    )(x)


def _proj_forward_own_chip(positions, x, gain, w_own, w_sibling):
    t = x.shape[0]
    tm = _tile(t, 1024)

    def body(pos_ref, x_ref, g_ref, wo_ref, ws_ref, o_ref, h_ref, ht_ref):
        @pl.when(pl.program_id(1) == 0)
        def _():
            _, xh = _rms_stats(x_ref[...])
            h = (xh * g_ref[...]).astype(BF16)
            h_ref[...] = h
            ht_ref[...] = h.T
            o_ref[...] = _dot(h, wo_ref[...])

        @pl.when(pl.program_id(1) == 1)
        def _():
            o_ref[...] = _dot(h_ref[...], ws_ref[...])

    tok = pl.BlockSpec((tm, D_MODEL), lambda m, k, pos: (m, 0))
    return pl.pallas_call(
        body, name="proj_fwd_own_chip",
        out_shape=[jax.ShapeDtypeStruct((N_DEV, t, D_MODEL), F32), jax.ShapeDtypeStruct((t, D_MODEL), BF16),
                   jax.ShapeDtypeStruct((D_MODEL, t), BF16)],
        grid_spec=pltpu.PrefetchScalarGridSpec(
            num_scalar_prefetch=1, grid=(t // tm, 2),
            in_specs=[tok, pl.BlockSpec((1, D_MODEL), lambda m, k, pos: (0, 0)), RESIDENT, RESIDENT],
            out_specs=[pl.BlockSpec((None, tm, D_MODEL), lambda m, k, pos: (pos[k], m, 0)), tok,
                       pl.BlockSpec((D_MODEL, tm), lambda m, k, pos: (0, m))]),
        compiler_params=_params([((tm, D_MODEL), F32)] * 2 + [((tm, D_MODEL), BF16)] * 2,
                                scratch=[((2, D_MODEL, D_MODEL), BF16)], temps=6 << 20, sem=("arbitrary", "arbitrary")),
    )(positions, x, gain, w_own, w_sibling)


def _proj_forward_other_chips(positions, proj, h, w_in_g):
    t = h.shape[0]
    tm = _tile(t, 2048)

    def body(pos_ref, _, h_ref, w_ref, o_ref):
        o_ref[...] = _dot(h_ref[...], w_ref[pos_ref[pl.program_id(1)]])

    return pl.pallas_call(
        body, name="proj_fwd_other_chips", out_shape=jax.ShapeDtypeStruct(proj.shape, F32),
        grid_spec=pltpu.PrefetchScalarGridSpec(
            num_scalar_prefetch=1, grid=(t // tm, N_DEV - 2),
            in_specs=[ANY, pl.BlockSpec((tm, D_MODEL), lambda m, k, pos: (m, 0)), RESIDENT],
            out_specs=pl.BlockSpec((None, tm, D_MODEL), lambda m, k, pos: (pos[k], m, 0))),
        input_output_aliases={1: 0},
        compiler_params=_params([((tm, D_MODEL), F32), ((tm, D_MODEL), BF16)], scratch=[((N_DEV, D_MODEL, D_MODEL), BF16)],
                                temps=6 << 20, sem=("arbitrary", "arbitrary")),
    )(positions, proj, h, w_in_g)


def _masked_ws(ws_ref, g):
    row = lax.broadcasted_iota(jnp.int32, (GMLP_CHUNK, GMLP_CHUNK), 0)
    col = lax.broadcasted_iota(jnp.int32, (GMLP_CHUNK, GMLP_CHUNK), 1)
    return jnp.where(row >= col, ws_ref[g], 0.0).astype(BF16)


def _gmlp_forward(proj, ln_g, ln_b, w_s, bias_b):
    t = proj.shape[1]
    tm = _tile(t, 512)
    chunks = tm // GMLP_CHUNK

    def body(u_ref, v_ref, lng_ref, lnb_ref, ws_ref, bias_ref, a_ref, vn_scr):
        vv = _gelu(v_ref[...])
        mu = jnp.mean(vv, axis=-1, keepdims=True)
        cen = vv - mu
        var = jnp.mean(cen * cen, axis=-1, keepdims=True)
        vn_scr[...] = ((cen * lax.rsqrt(var + NORM_EPS)) * lng_ref[...] + lnb_ref[...]).astype(BF16)
        for g in range(GROUPS):
            wm = _masked_ws(ws_ref, g)
            cols = slice(g * HEAD_DIM, (g + 1) * HEAD_DIM)
            for c in range(chunks):
                rows = slice(c * GMLP_CHUNK, (c + 1) * GMLP_CHUNK)
                mixed = _dot(wm, vn_scr[rows, cols]) + bias_ref[g]
                a_ref[rows, cols] = (_gelu(u_ref[rows, cols]) * mixed).astype(BF16)

    small = pl.BlockSpec((GROUPS, GMLP_CHUNK, GMLP_CHUNK), lambda m: (0, 0, 0))
    vec = pl.BlockSpec((1, D_MODEL), lambda m: (0, 0))
    return pl.pallas_call(
        body, name="gmlp_fwd", out_shape=jax.ShapeDtypeStruct((t, D_MODEL), BF16), grid=(t // tm,),
        in_specs=[pl.BlockSpec((None, tm, D_MODEL), lambda m: (U_POS, m, 0)),
                  pl.BlockSpec((None, tm, D_MODEL), lambda m: (U_POS + 1, m, 0)), vec, vec, small, small],
        out_specs=pl.BlockSpec((tm, D_MODEL), lambda m: (m, 0)),
        scratch_shapes=[pltpu.VMEM((tm, D_MODEL), BF16)],
        compiler_params=_params([((tm, D_MODEL), F32)] * 2 + [((tm, D_MODEL), BF16)] + [((8, 128, 128), F32)] * 2,
                                scratch=[((tm, D_MODEL), BF16)], temps=8 << 20, sem=("arbitrary",)),
    )(proj, proj, ln_g, ln_b, w_s, bias_b)


def _lower_bound(tab_ref):
    t0, t1 = tab_ref[0:1, :], tab_ref[1:2, :]
    mx = jnp.maximum(t0, t1)
    e0, e1 = jnp.exp(t0 - mx), jnp.exp(t1 - mx)
    return e0 / (e0 + e1)


def _tri_masks():
    row = lax.broadcasted_iota(jnp.int32, (HGRN_CHUNK, HGRN_CHUNK), 0)
    col = lax.broadcasted_iota(jnp.int32, (HGRN_CHUNK, HGRN_CHUNK), 1)
    return row >= col, row <= col


def _chunk_rows(c):
    return slice(c * HGRN_CHUNK, (c + 1) * HGRN_CHUNK)


def _per_chunk(x, nc, fn):
    return jnp.concatenate([fn(x[_chunk_rows(c)]) for c in range(nc)], axis=0)


def _chunk_row_bcast(x, nc, i):
    return _per_chunk(x, nc, lambda xc: jnp.broadcast_to(xc[i:i + 1, :], (HGRN_CHUNK, HEAD_DIM)))


def _hgrn_gates(q, fl, lb, nc):
    lower, _ = _tri_masks()
    lower = lower.astype(BF16)
    s = _sigmoid(fl)
    f = lb + (1.0 - lb) * s
    k = 1.0 - f
    hi, mid, lo = _split3(jnp.log(f))
    a = jnp.concatenate([_dot(lower, hi[_chunk_rows(c)]) + _dot(lower, mid[_chunk_rows(c)]) + _dot(lower, lo[_chunk_rows(c)])
                         for c in range(nc)], axis=0)
    a_mid = _chunk_row_bcast(a, nc, HGRN_CHUNK // 2 - 1)
    a_last = _chunk_row_bcast(a, nc, HGRN_CHUNK - 1)
    qs = q * HGRN_SCALE
    e_in, e_out, e_end, e_all = jnp.exp(a - a_mid), jnp.exp(a_mid - a), jnp.exp(a_last - a), jnp.exp(a)
    decay = [jnp.exp(a[c * HGRN_CHUNK + HGRN_CHUNK - 1:(c + 1) * HGRN_CHUNK, :]) for c in range(nc)]
    return dict(s=s, f=f, k=k, decay=decay, e_in=e_in, e_out=e_out, e_end=e_end, e_all=e_all,
                qi=qs * e_in, ki=k * e_out, kd=k * e_end, qe=qs * e_all)


def _hgrn_forward(proj, lb_table, norm_g):
    t = proj.shape[1]
    tb = _tile(t, 1024)
    nc = tb // HGRN_CHUNK
    n_chunks = t // HGRN_CHUNK

    def body(q_ref, f_ref, i_ref, g_ref, tab_ref, ng_ref, og_ref, o_ref, st_ref, state):
        @pl.when(pl.program_id(1) == 0)
        def _():
            state[...] = jnp.zeros_like(state)

        lower, _ = _tri_masks()
        gt = _hgrn_gates(q_ref[...], f_ref[...], _lower_bound(tab_ref), nc)
        qi, ki, kd, qe = (gt[n].astype(BF16) for n in ("qi", "ki", "kd", "qe"))
        vb = i_ref[...].astype(BF16)
        o_intra, d_state = [], []
        for c in range(nc):
            rows = _chunk_rows(c)
            p = jnp.where(lower, _dot_nt(qi[rows], ki[rows]), 0.0).astype(BF16)
            o_intra.append(_dot(p, vb[rows]))
            d_state.append(_dot_tn(vb[rows], kd[rows]))
        st = state[...]
        outs = []
        for c in range(nc):
            st_ref[c] = st
            outs.append(o_intra[c] + _dot_nt(qe[_chunk_rows(c)], st.astype(BF16)))
            st = st * gt["decay"][c] + d_state[c]
        state[...] = st
        o = jnp.concatenate(outs, axis=0)
        o_ref[...] = o
        _, oh = _rms_stats(o)
        gz = g_ref[...]
        og_ref[...] = ((oh * ng_ref[...]) * (gz * _sigmoid(gz))).astype(BF16)

    def blk(p):
        return pl.BlockSpec((None, tb, HEAD_DIM), lambda h, n: (p, n, h))

    out_blk = pl.BlockSpec((tb, HEAD_DIM), lambda h, n: (n, h))
    return pl.pallas_call(
        body, name="hgrn_fwd",
        out_shape=[jax.ShapeDtypeStruct((t, D_MODEL), BF16), jax.ShapeDtypeStruct((t, D_MODEL), F32),
                   jax.ShapeDtypeStruct((HEADS, n_chunks, HEAD_DIM, HEAD_DIM), F32)],
        grid=(HEADS, t // tb),
        in_specs=[blk(Q_POS), blk(Q_POS + 1), blk(Q_POS + 2), blk(Q_POS + 3),
                  pl.BlockSpec((2, HEAD_DIM), lambda h, n: (0, h)), pl.BlockSpec((1, HEAD_DIM), lambda h, n: (0, h))],
        out_specs=[out_blk, out_blk, pl.BlockSpec((None, nc, HEAD_DIM, HEAD_DIM), lambda h, n: (h, n, 0, 0))],
        scratch_shapes=[pltpu.VMEM((HEAD_DIM, HEAD_DIM), F32)],
        compiler_params=_params([((tb, HEAD_DIM), F32)] * 6 + [((nc, HEAD_DIM, HEAD_DIM), F32)], temps=8 << 20,
                                sem=("arbitrary", "arbitrary")),
    )(proj, proj, proj, proj, lb_table, norm_g)


def _branch_out_forward(a, og, proj, x, w_a, w_b, w_out, ffn_g):
    t = x.shape[0]
    tm = _tile(t, 512)

    def body(a_ref, og_ref, ga_ref, gb_ref, x_ref, wa_ref, wb_ref, wo_ref, g_ref, ya_ref, yb_ref, mg_ref, x1_ref, h2_ref):
        ya = _dot(a_ref[...], wa_ref[...])
        yb = _dot(og_ref[...], wb_ref[...])
        ya_ref[...] = ya
        yb_ref[...] = yb
        merged = (_sigmoid(ga_ref[...]) * ya + _sigmoid(gb_ref[...]) * yb).astype(BF16)
        mg_ref[...] = merged
        x1 = x_ref[...] + _dot(merged, wo_ref[...])
        x1_ref[...] = x1
        _, xh = _rms_stats(x1)
        h2_ref[...] = (xh * g_ref[...]).astype(BF16)

    tok = pl.BlockSpec((tm, D_MODEL), lambda m: (m, 0))
    return pl.pallas_call(
        body, name="branch_out_fwd",
        out_shape=[jax.ShapeDtypeStruct((t, D_MODEL), F32), jax.ShapeDtypeStruct((t, D_MODEL), F32),
                   jax.ShapeDtypeStruct((t, D_MODEL), BF16), jax.ShapeDtypeStruct((t, D_MODEL), F32),
                   jax.ShapeDtypeStruct((t, D_MODEL), BF16)],
        grid=(t // tm,),
        in_specs=[tok, tok, pl.BlockSpec((None, tm, D_MODEL), lambda m: (GATE_POS, m, 0)),
                  pl.BlockSpec((None, tm, D_MODEL), lambda m: (GATE_POS + 1, m, 0)), tok, RESIDENT, RESIDENT, RESIDENT,
                  pl.BlockSpec((1, D_MODEL), lambda m: (0, 0))],
        out_specs=[tok] * 5,
        compiler_params=_params([((tm, D_MODEL), BF16)] * 4 + [((tm, D_MODEL), F32)] * 6, scratch=[((D_MODEL, D_MODEL), BF16)] * 3,
                                temps=8 << 20, sem=("arbitrary",)),
    )(a, og, proj, proj, x, w_a, w_b, w_out, ffn_g)


def _ffn_forward(h2, x1, w_gu, w_down, target, final_g):
    t = x1.shape[0]
    tm = _tile(t, 512)

    def body(h_ref, wgu_ref, wd_ref, x1_ref, t_ref, g_ref, gu_ref, act_ref, loss_ref, dg_ref, dx_ref, dxb_ref, acc):
        m, j = pl.program_id(0), pl.program_id(1)

        @pl.when((m == 0) & (j == 0))
        def _():
            loss_ref[...] = jnp.zeros_like(loss_ref)
            dg_ref[...] = jnp.zeros_like(dg_ref)

        h = h_ref[...]
        gate = _dot_nt(h, wgu_ref[j])
        up = _dot_nt(h, wgu_ref[j + 4])
        gu_ref[0] = gate
        gu_ref[1] = up
        act = ((gate * _sigmoid(gate)) * up).astype(BF16)
        act_ref[...] = act
        part = _dot(act, wd_ref[j])

        @pl.when(j == 0)
        def _():
            acc[...] = part

        @pl.when((j > 0) & (j < 3))
        def _():
            acc[...] += part

        @pl.when(j == 3)
        def _():
            x2 = x1_ref[...] + (acc[...] + part)
            g = g_ref[...]
            r, xh = _rms_stats(x2)
            err = xh * g - t_ref[...]
            loss_ref[...] += 0.5 * jnp.sum(jnp.mean(err * err, axis=-1, keepdims=True), axis=0, keepdims=True)
            dy = err * (1.0 / D_MODEL)
            dg_ref[...] += jnp.sum(dy * xh, axis=0, keepdims=True)
            dxh = dy * g
            dx = r * (dxh - xh * jnp.mean(dxh * xh, axis=-1, keepdims=True))
            dx_ref[...] = dx
            dxb_ref[...] = dx.astype(BF16)

    tok = pl.BlockSpec((tm, D_MODEL), lambda m, j: (m, 0))
    vec = pl.BlockSpec((1, D_MODEL), lambda m, j: (0, 0))
    return pl.pallas_call(
        body, name="ffn_fwd",
        out_shape=[jax.ShapeDtypeStruct((4, 2, t, FF_BLOCK), F32), jax.ShapeDtypeStruct((4, t, FF_BLOCK), BF16),
                   jax.ShapeDtypeStruct((8, 128), F32), jax.ShapeDtypeStruct((1, D_MODEL), F32),
                   jax.ShapeDtypeStruct((t, D_MODEL), F32), jax.ShapeDtypeStruct((t, D_MODEL), BF16)],
        grid=(t // tm, 4),
        in_specs=[tok, RESIDENT, RESIDENT, tok, tok, vec],
        out_specs=[pl.BlockSpec((None, 2, tm, FF_BLOCK), lambda m, j: (j, 0, m, 0)),
                   pl.BlockSpec((None, tm, FF_BLOCK), lambda m, j: (j, m, 0)),
                   pl.BlockSpec((8, 128), lambda m, j: (0, 0)), vec, tok, tok],
        scratch_shapes=[pltpu.VMEM((tm, D_MODEL), F32)],
        compiler_params=_params([((tm, D_MODEL), BF16), ((tm, D_MODEL), F32), ((tm, D_MODEL), F32), ((2, tm, 768), F32),
                                 ((tm, 768), BF16), ((tm, D_MODEL), F32), ((tm, D_MODEL), BF16)],
                                scratch=[((tm, D_MODEL), F32), ((N_DEV, FF_BLOCK, D_MODEL), BF16), ((D_FF, D_MODEL), BF16)],
                                temps=6 << 20, sem=("arbitrary", "arbitrary")),
    )(h2, w_gu, w_down.reshape(4, FF_BLOCK, D_MODEL), x1, target, final_g)


def _ffn_backward(dx2b, dx2, gu, x1, w_gu, w_down, ffn_g):
    t = x1.shape[0]
    tm = _tile(t, 512)

    def body(dxb_ref, dx2_ref, gu_ref, x1_ref, wgu_ref, wd_ref, g_ref, dgu_ref, dx1_ref, dx1b_ref, dg_ref, acc, prev):
        m, j = pl.program_id(0), pl.program_id(1)

        @pl.when((m == 0) & (j == 0))
        def _():
            dg_ref[...] = jnp.zeros_like(dg_ref)

        @pl.when(j == 0)
        def _():
            prev[...] = jnp.zeros_like(prev)
            acc[...] = jnp.zeros_like(acc)

        jm1 = jnp.maximum(j - 1, 0)
        acc[...] += _dot(prev[0], wgu_ref[jm1]) + _dot(prev[1], wgu_ref[jm1 + 4])
        dact = _dot_nt(dxb_ref[...], wd_ref[j])
        gate, up = gu_ref[0], gu_ref[1]
        sg = _sigmoid(gate)
        dgate = (dact * up * (sg * (1.0 + gate * (1.0 - sg)))).astype(BF16)
        dup = (dact * (gate * sg)).astype(BF16)
        dgu_ref[0] = dgate
        dgu_ref[1] = dup
        prev[0] = dgate
        prev[1] = dup

        @pl.when(j == 3)
        def _():
            dh2 = acc[...] + (_dot(prev[0], wgu_ref[3]) + _dot(prev[1], wgu_ref[7]))
            dx, dg = _rms_bwd(dh2, x1_ref[...], g_ref[...])
            dx1 = dx2_ref[...] + dx
            dx1_ref[...] = dx1
            dx1b_ref[...] = dx1.astype(BF16)
            dg_ref[...] += dg

    tok = pl.BlockSpec((tm, D_MODEL), lambda m, j: (m, 0))
    vec = pl.BlockSpec((1, D_MODEL), lambda m, j: (0, 0))
    gu_spec = pl.BlockSpec((None, 2, tm, FF_BLOCK), lambda m, j: (j, 0, m, 0))
    return pl.pallas_call(
        body, name="ffn_bwd",
        out_shape=[jax.ShapeDtypeStruct((4, 2, t, FF_BLOCK), BF16), jax.ShapeDtypeStruct((t, D_MODEL), F32),
                   jax.ShapeDtypeStruct((t, D_MODEL), BF16), jax.ShapeDtypeStruct((1, D_MODEL), F32)],
        grid=(t // tm, 4),
        in_specs=[tok, tok, gu_spec, tok, RESIDENT, RESIDENT, vec],
        out_specs=[gu_spec, tok, tok, vec],
        scratch_shapes=[pltpu.VMEM((tm, D_MODEL), F32), pltpu.VMEM((2, tm, FF_BLOCK), BF16)],
        compiler_params=_params([((tm, D_MODEL), BF16), ((tm, D_MODEL), F32), ((2, tm, 768), F32), ((tm, D_MODEL), F32),
                                 ((2, tm, 768), BF16), ((tm, D_MODEL), F32), ((tm, D_MODEL), BF16)],
                                scratch=[((tm, D_MODEL), F32), ((2, tm, 768), BF16), ((N_DEV, FF_BLOCK, D_MODEL), BF16),
                                         ((D_FF, D_MODEL), BF16)],
                                temps=4 << 20, sem=("arbitrary", "arbitrary")),
    )(dx2b, dx2, gu, x1, w_gu, w_down.reshape(4, FF_BLOCK, D_MODEL), ffn_g)


def _branch_out_backward(dx1b, ya, yb, proj, w_a, w_b, w_out):
    t = ya.shape[0]
    tm = _tile(t, 512)

    def body(dx_ref, ya_ref, yb_ref, ga_ref, gb_ref, wa_ref, wb_ref, wo_ref, dya_ref, dyb_ref, dgate_ref, da_ref, dog_ref):
        dm = _dot_nt(dx_ref[...], wo_ref[...])
        sa, sb = _sigmoid(ga_ref[...]), _sigmoid(gb_ref[...])
        dya = (dm * sa).astype(BF16)
        dyb = (dm * sb).astype(BF16)
        dya_ref[...] = dya
        dyb_ref[...] = dyb
        dgate_ref[0] = (dm * ya_ref[...] * (sa * (1.0 - sa))).astype(BF16)
        dgate_ref[1] = (dm * yb_ref[...] * (sb * (1.0 - sb))).astype(BF16)
        da_ref[...] = _dot_nt(dya, wa_ref[...])
        dog_ref[...] = _dot_nt(dyb, wb_ref[...])

    tok = pl.BlockSpec((tm, D_MODEL), lambda m: (m, 0))
    return pl.pallas_call(
        body, name="branch_out_bwd",
        out_shape=[jax.ShapeDtypeStruct((t, D_MODEL), BF16), jax.ShapeDtypeStruct((t, D_MODEL), BF16),
                   jax.ShapeDtypeStruct((N_DEV, t, D_MODEL), BF16), jax.ShapeDtypeStruct((t, D_MODEL), F32),
                   jax.ShapeDtypeStruct((t, D_MODEL), F32)],
        grid=(t // tm,),
        in_specs=[tok, tok, tok, pl.BlockSpec((None, tm, D_MODEL), lambda m: (GATE_POS, m, 0)),
                  pl.BlockSpec((None, tm, D_MODEL), lambda m: (GATE_POS + 1, m, 0)), RESIDENT, RESIDENT, RESIDENT],
        out_specs=[tok, tok, pl.BlockSpec((2, tm, D_MODEL), lambda m: (GATE_POS // 2, m, 0)), tok, tok],
        compiler_params=_params([((tm, D_MODEL), BF16)] * 5 + [((tm, D_MODEL), F32)] * 6, scratch=[((D_MODEL, D_MODEL), BF16)] * 3,
                                temps=8 << 20, sem=("arbitrary",)),
    )(dx1b, ya, yb, proj, proj, w_a, w_b, w_out)


def _hgrn_backward(dproj, dog, o_saved, states, proj, lb_table, norm_g):
    t = proj.shape[1]
    tb = _tile(t, 1024)
    nc = tb // HGRN_CHUNK
    nb = t // tb

    def body(_, dog_ref, o_ref, st_ref, q_ref, f_ref, i_ref, g_ref, tab_ref, ng_ref, dp_ref, dng_ref, dtab_ref, gstate):
        @pl.when(pl.program_id(1) == 0)
        def _():
            gstate[...] = jnp.zeros_like(gstate)
            dng_ref[...] = jnp.zeros_like(dng_ref)
            dtab_ref[...] = jnp.zeros_like(dtab_ref)

        lb = _lower_bound(tab_ref)
        ng = ng_ref[...]
        lower, upper = _tri_masks()
        gt = _hgrn_gates(q_ref[...], f_ref[...], lb, nc)
        qi, ki, kd, qe = (gt[n].astype(BF16) for n in ("qi", "ki", "kd", "qe"))
        vb = i_ref[...].astype(BF16)
        o, gz, d_og = o_ref[...], g_ref[...], dog_ref[...]
        r, oh = _rms_stats(o)
        sg = _sigmoid(gz)
        d_on = d_og * (gz * sg)
        dgz = d_og * (oh * ng) * (sg * (1.0 + gz * (1.0 - sg)))
        dng_ref[...] += jnp.sum(d_on * oh, axis=0, keepdims=True)
        doh = d_on * ng
        dob = (r * (doh - oh * jnp.mean(doh * oh, axis=-1, keepdims=True))).astype(BF16)
        dv_intra, dqi, dki, dqe, g_upd = [], [], [], [], []
        for c in range(nc):
            rows = _chunk_rows(c)
            p = jnp.where(lower, _dot_nt(qi[rows], ki[rows]), 0.0).astype(BF16)
            dv_intra.append(_dot_tn(p, dob[rows]))
            dp = jnp.where(lower, _dot_nt(dob[rows], vb[rows]), 0.0).astype(BF16)
            dqi.append(_dot(dp, ki[rows]))
            dki.append(_dot_tn(dp, qi[rows]))
            dqe.append(_dot(dob[rows], st_ref[c].astype(BF16)))
            g_upd.append(_dot_tn(dob[rows], qe[rows]))
        g_after = [None] * nc
        g = gstate[...]
        for c in reversed(range(nc)):
            g_after[c] = g
            g = g * gt["decay"][c] + g_upd[c]
        gstate[...] = g
        dkd, dv, da_last = [], [], []
        for c in range(nc):
            rows = _chunk_rows(c)
            gb = g_after[c].astype(BF16)
            dkd.append(_dot(vb[rows], gb))
            dv.append(dv_intra[c] + _dot_nt(kd[rows], gb))
            da_last.append(jnp.sum(g_after[c] * st_ref[c], axis=0, keepdims=True) * gt["decay"][c])
        dqi, dki, dqe, dkd, dv = (jnp.concatenate(z, axis=0) for z in (dqi, dki, dqe, dkd, dv))
        dqs = dqi * gt["e_in"] + dqe * gt["e_all"]
        dk = dki * gt["e_out"] + dkd * gt["e_end"]
        t_in, t_out, t_end = dqi * gt["qi"], dki * gt["ki"], dkd * gt["kd"]
        da = t_in - t_out + dqe * gt["qe"] - t_end
        row = lax.broadcasted_iota(jnp.int32, (HGRN_CHUNK, HEAD_DIM), 0)
        d_mid = t_out - t_in
        pieces = []
        for c in range(nc):
            rows = _chunk_rows(c)
            da_mid = jnp.sum(d_mid[rows], axis=0, keepdims=True)
            da_end = jnp.sum(t_end[rows], axis=0, keepdims=True) + da_last[c]
            da_c = da[rows] + jnp.where(row == HGRN_CHUNK // 2 - 1, da_mid, 0.0) + jnp.where(row == HGRN_CHUNK - 1, da_end, 0.0)
            pieces.append(_mask_mm(upper.astype(BF16), da_c))
        df = jnp.concatenate(pieces, axis=0) / gt["f"] - dk
        s = gt["s"]
        dlb = jnp.sum(df * (1.0 - s), axis=0, keepdims=True)
        dp_ref[0] = (dqs * HGRN_SCALE).astype(BF16)
        dp_ref[1] = (df * (1.0 - lb) * (s * (1.0 - s))).astype(BF16)
        dp_ref[2] = dv.astype(BF16)
        dp_ref[3] = dgz.astype(BF16)
        dt0 = dlb * (lb * (1.0 - lb))
        dtab_ref[0:1, :] += dt0
        dtab_ref[1:2, :] -= dt0

    def blk(p):
        return pl.BlockSpec((None, tb, HEAD_DIM), lambda h, n: (p, nb - 1 - n, h))

    tok = pl.BlockSpec((tb, HEAD_DIM), lambda h, n: (nb - 1 - n, h))
    return pl.pallas_call(
        body, name="hgrn_bwd",
        out_shape=[jax.ShapeDtypeStruct((N_DEV, t, D_MODEL), BF16), jax.ShapeDtypeStruct((1, D_MODEL), F32),
                   jax.ShapeDtypeStruct((2, D_MODEL), F32)],
        grid=(HEADS, nb),
        in_specs=[ANY, tok, tok, pl.BlockSpec((None, nc, HEAD_DIM, HEAD_DIM), lambda h, n: (h, nb - 1 - n, 0, 0)),
                  blk(Q_POS), blk(Q_POS + 1), blk(Q_POS + 2), blk(Q_POS + 3),
                  pl.BlockSpec((2, HEAD_DIM), lambda h, n: (0, h)), pl.BlockSpec((1, HEAD_DIM), lambda h, n: (0, h))],
        out_specs=[pl.BlockSpec((4, tb, HEAD_DIM), lambda h, n: (0, nb - 1 - n, h)),
                   pl.BlockSpec((1, HEAD_DIM), lambda h, n: (0, h)), pl.BlockSpec((2, HEAD_DIM), lambda h, n: (0, h))],
        scratch_shapes=[pltpu.VMEM((HEAD_DIM, HEAD_DIM), F32)],
        input_output_aliases={0: 0},
        compiler_params=_params([((tb, HEAD_DIM), F32)] * 6 + [((nc, HEAD_DIM, HEAD_DIM), F32)] + [((4, tb, HEAD_DIM), BF16)],
                                temps=8 << 20, sem=("arbitrary", "arbitrary")),
    )(dproj, dog, o_saved, states, proj, proj, proj, proj, lb_table, norm_g)


def _gmlp_backward(dproj, da, proj, ln_g, ln_b, w_s, bias_b):
    t = proj.shape[1]
    tm = _tile(t, 256)
    chunks = tm // GMLP_CHUNK

    def body(_, da_ref, u_ref, v_ref, lng_ref, lnb_ref, ws_ref, bias_ref, dp_ref, dlng_ref, dlnb_ref, dws_ref, dbs_ref,
             vn_scr, dvn_scr):
        @pl.when(pl.program_id(0) == 0)
        def _():
            dlng_ref[...] = jnp.zeros_like(dlng_ref)
            dlnb_ref[...] = jnp.zeros_like(dlnb_ref)
            dws_ref[...] = jnp.zeros_like(dws_ref)
            dbs_ref[...] = jnp.zeros_like(dbs_ref)

        v = v_ref[...]
        vv, dvv_dv = _gelu_and_grad(v)
        mu = jnp.mean(vv, axis=-1, keepdims=True)
        cen = vv - mu
        rstd = lax.rsqrt(jnp.mean(cen * cen, axis=-1, keepdims=True) + NORM_EPS)
        vhat = cen * rstd
        lng = lng_ref[...]
        vn_scr[...] = (vhat * lng + lnb_ref[...]).astype(BF16)
        row = lax.broadcasted_iota(jnp.int32, (GMLP_CHUNK, GMLP_CHUNK), 0)
        col = lax.broadcasted_iota(jnp.int32, (GMLP_CHUNK, GMLP_CHUNK), 1)
        for g in range(GROUPS):
            wm = _masked_ws(ws_ref, g)
            cols = slice(g * HEAD_DIM, (g + 1) * HEAD_DIM)
            dws = jnp.zeros((GMLP_CHUNK, GMLP_CHUNK), F32)
            dbs = jnp.zeros((GMLP_CHUNK, GMLP_CHUNK), F32)
            for c in range(chunks):
                rows = slice(c * GMLP_CHUNK, (c + 1) * GMLP_CHUNK)
                vn = vn_scr[rows, cols]
                mixed = _dot(wm, vn) + bias_ref[g]
                u = u_ref[rows, cols]
                d_a = da_ref[rows, cols]
                gelu_u, dgelu_u = _gelu_and_grad(u)
                dp_ref[0, rows, cols] = (d_a * mixed * dgelu_u).astype(BF16)
                dmix = d_a * gelu_u
                dmb = dmix.astype(BF16)
                dbs = dbs + dmix
                dws = dws + _dot_nt(dmb, vn)
                dvn_scr[rows, cols] = _dot_tn(wm, dmb)
            dws_ref[g] += jnp.where(row >= col, dws, 0.0)
            dbs_ref[g] += jnp.broadcast_to(jnp.sum(dbs, axis=-1, keepdims=True), (GMLP_CHUNK, GMLP_CHUNK))
        dvn = dvn_scr[...]
        dlng_ref[...] += jnp.sum(dvn * vhat, axis=0, keepdims=True)
        dlnb_ref[...] += jnp.sum(dvn, axis=0, keepdims=True)
        dvh = dvn * lng
        dvv = rstd * (dvh - jnp.mean(dvh, axis=-1, keepdims=True) - vhat * jnp.mean(dvh * vhat, axis=-1, keepdims=True))
        dp_ref[1] = (dvv * dvv_dv).astype(BF16)

    tok = pl.BlockSpec((tm, D_MODEL), lambda m: (m, 0))
    small = pl.BlockSpec((GROUPS, GMLP_CHUNK, GMLP_CHUNK), lambda m: (0, 0, 0))
    vec = pl.BlockSpec((1, D_MODEL), lambda m: (0, 0))
    return pl.pallas_call(
        body, name="gmlp_bwd",
        out_shape=[jax.ShapeDtypeStruct(dproj.shape, BF16), jax.ShapeDtypeStruct((1, D_MODEL), F32),
                   jax.ShapeDtypeStruct((1, D_MODEL), F32), jax.ShapeDtypeStruct((GROUPS, GMLP_CHUNK, GMLP_CHUNK), F32),
                   jax.ShapeDtypeStruct((GROUPS, GMLP_CHUNK, GMLP_CHUNK), F32)],
        grid=(t // tm,),
        in_specs=[ANY, tok, pl.BlockSpec((None, tm, D_MODEL), lambda m: (U_POS, m, 0)),
                  pl.BlockSpec((None, tm, D_MODEL), lambda m: (U_POS + 1, m, 0)), vec, vec, small, small],
        out_specs=[pl.BlockSpec((2, tm, D_MODEL), lambda m: (U_POS // 2, m, 0)), vec, vec, small, small],
        scratch_shapes=[pltpu.VMEM((tm, D_MODEL), BF16), pltpu.VMEM((tm, D_MODEL), F32)],
        input_output_aliases={0: 0},
        compiler_params=_params([((tm, D_MODEL), F32)] * 3 + [((2, tm, D_MODEL), BF16)] + [((8, 128, 128), F32)] * 4,
                                scratch=[((tm, D_MODEL), BF16), ((tm, D_MODEL), F32)], temps=12 << 20, sem=("arbitrary",)),
    )(dproj, da, proj, proj, ln_g, ln_b, w_s, bias_b)


def _input_backward(dproj, w_in_g, x, dx1, mix_g):
    t = x.shape[0]
    tm = _tile(t, 512)

    def body(dp_ref, w_ref, x_ref, dx1_ref, g_ref, dx_ref, dg_ref):
        @pl.when(pl.program_id(0) == 0)
        def _():
            dg_ref[...] = jnp.zeros_like(dg_ref)

        dh = _dot_nt(dp_ref[0], w_ref[0])
        for p in range(1, N_DEV):
            dh = dh + _dot_nt(dp_ref[p], w_ref[p])
        dx, dg = _rms_bwd(dh, x_ref[...], g_ref[...])
        dx_ref[...] = dx1_ref[...] + dx
        dg_ref[...] += dg

    tok = pl.BlockSpec((tm, D_MODEL), lambda m: (m, 0))
    vec = pl.BlockSpec((1, D_MODEL), lambda m: (0, 0))
    return pl.pallas_call(
        body, name="input_bwd",
        out_shape=[jax.ShapeDtypeStruct((t, D_MODEL), F32), jax.ShapeDtypeStruct((1, D_MODEL), F32)],
        grid=(t // tm,),
        in_specs=[pl.BlockSpec((N_DEV, tm, D_MODEL), lambda m: (0, m, 0)), RESIDENT, tok, tok, vec],
        out_specs=[tok, vec],
        compiler_params=_params([((N_DEV, tm, D_MODEL), BF16)] + [((tm, D_MODEL), F32)] * 3,
                                scratch=[((N_DEV, D_MODEL, D_MODEL), BF16)], temps=6 << 20, sem=("arbitrary",)),
    )(dproj, w_in_g, x, dx1, mix_g)


def _weight_grad(name, a, b, a_spec, b_spec, out_shape, out_spec, steps, blocks, a_is_transposed):
    def body(a_ref, b_ref, o_ref):
        o_ref[...] = _dot(a_ref[...], b_ref[...]) if a_is_transposed else _dot_tn(a_ref[...], b_ref[...])

    return pl.pallas_call(
        body, name=name, out_shape=jax.ShapeDtypeStruct(out_shape, F32), grid=(steps,), in_specs=[a_spec, b_spec],
        out_specs=out_spec, compiler_params=_params(blocks, temps=4 << 20, sem=("arbitrary",)),
    )(a, b)


def _pack_small(mix_g, ln_g, ln_b, b_s, lb_table, hg_norm, ffn_g, final_g, loss_row):
    def part(a):
        a = a.reshape(-1, D_MODEL)
        return jnp.pad(a, ((0, 8 - a.shape[0]), (0, 0)))

    return jnp.concatenate([part(mix_g), part(ln_g), part(ln_b), part(hg_norm), part(ffn_g), part(final_g),
                            part(lb_table), part(b_s), part(loss_row)], axis=0)


SMALL_PARTS = (("gmlp_ln_g", 8, 1), ("gmlp_ln_b", 16, 1), ("hgrn_norm_g", 24, 1), ("norm_ffn_g", 32, 1), ("norm_final_g", 40, 1),
               ("hgrn_lb_table", 48, 2))


def _adamw_small_unpacked(gathered, w, m, v):
    rows = w.shape[0]
    n_out = len(SMALL_PARTS) + 1

    def body(p_ref, w_ref, m_ref, v_ref, *outs):
        g = p_ref[0]
        for j in range(1, N_DEV):
            g = g + p_ref[j]
        delta, m_new, v_new = _adamw_math(w_ref[...], g, m_ref[...], v_ref[...])
        for kind, val in enumerate((g, delta, m_new, v_new)):
            refs = outs[kind * n_out:(kind + 1) * n_out]
            for (_, first, count), ref in zip(SMALL_PARTS, refs):
                ref[...] = val[first:first + count]
            for grp in range(GROUPS):
                refs[-1][0, grp:grp + 1, :] = val[56:57, grp * GMLP_CHUNK:(grp + 1) * GMLP_CHUNK]
        outs[-1][...] = g[SMALL_ROWS - 8:SMALL_ROWS - 7]

    shapes = [jax.ShapeDtypeStruct((count, D_MODEL), F32) for _, _, count in SMALL_PARTS]
    shapes.append(jax.ShapeDtypeStruct((1, GROUPS, GMLP_CHUNK), F32))
    whole = pl.BlockSpec((rows, D_MODEL), lambda: (0, 0))
    res = pl.pallas_call(
        body, name="adamw_small", out_shape=shapes * 4 + [jax.ShapeDtypeStruct((1, D_MODEL), F32)],
        in_specs=[pl.BlockSpec((N_DEV, rows, D_MODEL), lambda: (0, 0, 0)), whole, whole, whole],
        compiler_params=_params([((N_DEV, rows, D_MODEL), F32)] + [((rows, D_MODEL), F32)] * 7),
    )(gathered, w, m, v)
    names = [nme for nme, _, _ in SMALL_PARTS] + ["gmlp_b_s"]
    return [dict(zip(names, res[kind * n_out:(kind + 1) * n_out])) for kind in range(4)], res[-1]


def _adamw_row(name, gathered, w, m, v):
    def body(p_ref, w_ref, m_ref, v_ref, g_out, d_out, m_out, v_out):
        g = p_ref[0, 0:1, :]
        for j in range(1, N_DEV):
            g = g + p_ref[j, 0:1, :]
        delta, m_new, v_new = _adamw_math(w_ref[...], g, m_ref[...], v_ref[...])
        g_out[...] = g
        d_out[...] = delta
        m_out[...] = m_new
        v_out[...] = v_new

    return pl.pallas_call(
        body, name=name, out_shape=[jax.ShapeDtypeStruct((1, D_MODEL), F32)] * 4,
        compiler_params=_params([((N_DEV, 8, D_MODEL), F32)] + [((8, D_MODEL), F32)] * 7),
    )(gathered, w, m, v)


def _adamw_small(name, gathered, w, m, v):
    rows, cols = w.shape

    def body(p_ref, w_ref, m_ref, v_ref, g_out, d_out, m_out, v_out):
        g = p_ref[0]
        for j in range(1, N_DEV):
            g = g + p_ref[j]
        delta, m_new, v_new = _adamw_math(w_ref[...], g, m_ref[...], v_ref[...])
        g_out[...] = g
        d_out[...] = delta
        m_out[...] = m_new
        v_out[...] = v_new

    tr = _tile(rows, 512)
    spec = pl.BlockSpec((tr, cols), lambda r: (r, 0))
    return pl.pallas_call(
        body, name=name, out_shape=[jax.ShapeDtypeStruct((rows, cols), F32)] * 4, grid=(rows // tr,),
        in_specs=[pl.BlockSpec((N_DEV, tr, cols), lambda r: (0, r, 0)), spec, spec, spec], out_specs=[spec] * 4,
        compiler_params=_params([((N_DEV, tr, cols), F32)] + [((tr, cols), F32)] * 7, sem=("arbitrary",)),
    )(gathered, w, m, v)


def kernel(x, norm_mix_g, w_in, gmlp_ln_g, gmlp_ln_b, gmlp_w_s, gmlp_b_s, hgrn_lb_table, hgrn_norm_g, w_branch_a, w_branch_b, w_out, norm_ffn_g, w_gate_up, w_down, norm_final_g, loss_target, m_norm_mix_g, m_w_in, m_gmlp_ln_g, m_gmlp_ln_b, m_gmlp_w_s, m_gmlp_b_s, m_hgrn_lb_table, m_hgrn_norm_g, m_w_branch_a, m_w_branch_b, m_w_out, m_norm_ffn_g, m_w_gate_up, m_w_down, m_norm_final_g, v_norm_mix_g, v_w_in, v_gmlp_ln_g, v_gmlp_ln_b, v_gmlp_w_s, v_gmlp_b_s, v_hgrn_lb_table, v_hgrn_norm_g, v_w_branch_a, v_w_branch_b, v_w_out, v_norm_ffn_g, v_w_gate_up, v_w_down, v_norm_final_g):
    t = x.shape[1]
    x2d = x.reshape(t, D_MODEL)
    target = loss_target.reshape(t, D_MODEL)
    final_g = norm_final_g.reshape(1, D_MODEL)

    w_in_shard = w_in[0].astype(BF16)

    def rows_of(n):
        return lambda ref, j: ref.at[pl.ds(pl.multiple_of(j * n, 8), n)]

    gathered = [((N_DEV, D_MODEL, D_MODEL), BF16), ((D_MODEL, D_MODEL), BF16), ((D_MODEL, D_MODEL), BF16),
                ((D_MODEL, D_MODEL), BF16), ((N_DEV, FF_BLOCK, D_MODEL), BF16), ((D_FF, D_MODEL), BF16)]
    places = [lambda ref, j: ref.at[_pos_of_dev(j)], rows_of(BRANCH_ROWS), rows_of(BRANCH_ROWS), rows_of(BRANCH_ROWS),
              lambda ref, j: ref.at[j], rows_of(DOWN_ROWS)]
    w_in_g = _all_gather_balanced_async("w_in_all_gather", 9, w_in_shard, gathered[0], places[0], D_MODEL)
    w_in_sibling = _swap_with_sibling("w_in_from_sibling", w_in_shard)

    core_i, chip_i = lax.axis_index("c"), 2 * lax.axis_index("x") + lax.axis_index("y")
    own_pos = jnp.stack([_pos_of_dev(2 * chip_i + core_i), _pos_of_dev(2 * chip_i + 1 - core_i)]).astype(jnp.int32)
    other_pos = jnp.stack([_pos_of_dev(2 * jnp.bitwise_xor(chip_i, q) + cc) for q in (1, 2, 3) for cc in (0, 1)]).astype(jnp.int32)
    proj, h, h_t = _proj_forward_own_chip(own_pos, x2d, norm_mix_g, w_in_shard, w_in_sibling)

    small_w = [norm_mix_g, gmlp_ln_g, gmlp_ln_b, gmlp_b_s, hgrn_lb_table, hgrn_norm_g, norm_ffn_g, norm_final_g]
    small_m = [m_norm_mix_g, m_gmlp_ln_g, m_gmlp_ln_b, m_gmlp_b_s, m_hgrn_lb_table, m_hgrn_norm_g, m_norm_ffn_g, m_norm_final_g]
    small_v = [v_norm_mix_g, v_gmlp_ln_g, v_gmlp_ln_b, v_gmlp_b_s, v_hgrn_lb_table, v_hgrn_norm_g, v_norm_ffn_g, v_norm_final_g]
    _, (raw, small_w, small_m, small_v) = lax.optimization_barrier(
        (h, ([w_branch_a[0], w_branch_b[0], w_out[0], w_gate_up[0], w_down[0]], small_w, small_m, small_v)))
    later = [raw[0].astype(BF16), raw[1].astype(BF16), raw[2].astype(BF16),
             raw[3].T.astype(BF16), raw[4].astype(BF16)]
    w_a, w_b, w_o, w_gu, w_dn = _all_gather_async("weights_all_gather", 0, later, gathered[1:], places[1:])
    no_row = jnp.zeros((1, D_MODEL), F32)
    w_pack, m_pack, v_pack = (_pack_small(*vals, no_row) for vals in (small_w, small_m, small_v))
    bias_b = jnp.broadcast_to(small_w[3][0][:, :, None], (GROUPS, GMLP_CHUNK, GMLP_CHUNK))
    h_later, _ = lax.optimization_barrier((h, (later, w_pack, m_pack, v_pack, bias_b)))
    proj = _proj_forward_other_chips(other_pos, proj, h_later, w_in_g)
    a = _gmlp_forward(proj, gmlp_ln_g, gmlp_ln_b, gmlp_w_s[0], bias_b)
    og, o_saved, states = _hgrn_forward(proj, hgrn_lb_table, hgrn_norm_g)
    ya, yb, merged, x1, h2 = _branch_out_forward(a, og, proj, x2d, w_a, w_b, w_o, norm_ffn_g)
    gu, act, loss_tile, d_final_g, dx2, dx2b = _ffn_forward(h2, x1, w_gu, w_dn, target, final_g)

    core = lax.axis_index("c").astype(jnp.int32).reshape(1)
    chip = (2 * lax.axis_index("x") + lax.axis_index("y")).astype(jnp.int32).reshape(1)
    branch_rows, branch_shape = rows_of(BRANCH_ROWS), (BRANCH_ROWS, D_MODEL)
    branch_block = ((BRANCH_ROWS, D_MODEL), lambda q, r, c: (2 * q + c, 0))

    def chip_partials(names, grads, land, own_blocks):
        return [_chip_partial("chip_partial_" + nme, core, g_, blk, idx, l_)
                for nme, g_, (blk, idx), l_ in zip(names, grads, own_blocks, land)]

    whole = pl.BlockSpec((t, D_MODEL), lambda n: (0, 0))
    whole_t = pl.BlockSpec((D_MODEL, t), lambda n: (0, 0))
    col_blocks = [((t, D_MODEL), BF16), ((t, 256), BF16), ((D_MODEL, 256), F32)]

    def square_grad(name, a_, b_):
        return _weight_grad(name, a_, b_, whole, pl.BlockSpec((t, 256), lambda n: (0, n)), (D_MODEL, D_MODEL),
                            pl.BlockSpec((D_MODEL, 256), lambda n: (0, n)), D_MODEL // 256, col_blocks, False)

    dgu, dx1, dx1b, d_ffn_g = _ffn_backward(dx2b, dx2, gu, x1, w_gu, w_dn, norm_ffn_g)
    g_gu = _weight_grad(
        "grad_w_gate_up", dgu, h2, pl.BlockSpec((None, None, t, FF_BLOCK), lambda j: (j % 4, j // 4, 0, 0)), whole,
        (N_DEV, FF_BLOCK, D_MODEL), pl.BlockSpec((None, FF_BLOCK, D_MODEL), lambda j: (j, 0, 0)), N_DEV,
        [((t, 768), BF16), ((t, D_MODEL), BF16), ((FF_BLOCK, D_MODEL), F32)], False)
    g_dn = _weight_grad(
        "grad_w_down", act, dx2b, pl.BlockSpec((None, t, FF_BLOCK), lambda j: (j, 0, 0)), whole, (D_FF, D_MODEL),
        pl.BlockSpec((FF_BLOCK, D_MODEL), lambda j: (j, 0)), 4,
        [((t, 768), BF16), ((t, D_MODEL), BF16), ((FF_BLOCK, D_MODEL), F32)], False)
    names_f, grads_f = ["w_gate_up", "w_down"], [g_gu, g_dn]
    land_f = _exchange_sibling("ffn_grads_to_sibling", 2, grads_f, [lambda ref, j: ref.at[j], rows_of(DOWN_ROWS)],
                               [(FF_BLOCK, D_MODEL), (DOWN_ROWS, D_MODEL)])

    dx1b_later, _ = lax.optimization_barrier((dx1b, grads_f))
    dya, dyb, dproj, da, dog = _branch_out_backward(dx1b_later, ya, yb, proj, w_a, w_b, w_o)
    g_a = square_grad("grad_w_a", a, dya)
    g_b = square_grad("grad_w_b", og, dyb)
    g_o = square_grad("grad_w_out", merged, dx1b)
    names_b, grads_b = ["w_branch_a", "w_branch_b", "w_out"], [g_a, g_b, g_o]
    land_b = _exchange_sibling("branch_grads_to_sibling", 3, grads_b, [branch_rows] * 3, [branch_shape] * 3)

    part_f = chip_partials(names_f, grads_f, land_f,
                           [((None, FF_BLOCK, D_MODEL), lambda q, r, c: (2 * q + c, 0, 0)),
                            ((DOWN_ROWS, D_MODEL), lambda q, r, c: (2 * q + c, 0))])
    landed_f = _exchange_chips("ffn_grads_to_chips", 5, part_f)

    dog, _ = lax.optimization_barrier((dog, part_f))
    dproj, d_hg_norm, d_lb = _hgrn_backward(dproj, dog, o_saved, states, proj, hgrn_lb_table, hgrn_norm_g)

    land_b, _ = lax.optimization_barrier((land_b, part_f))
    part_b = chip_partials(names_b, grads_b, land_b, [branch_block] * 3)
    landed_b = _exchange_chips("branch_grads_to_chips", 6, part_b)

    da, _ = lax.optimization_barrier((da, part_b))
    dproj, d_ln_g, d_ln_b, d_ws, d_bs = _gmlp_backward(dproj, da, proj, gmlp_ln_g, gmlp_ln_b, gmlp_w_s[0], bias_b)

    def flat_ws(a):
        return a.reshape(GROUPS * GMLP_CHUNK, GMLP_CHUNK)

    small_partial = _pack_small(no_row, d_ln_g, d_ln_b, d_bs[:, :, 0], d_lb, d_hg_norm, d_ffn_g, d_final_g,
                                jnp.tile(loss_tile[0:1], (1, D_MODEL // 128)))
    small_all, ws_all = _all_gather_async(
        "small_grads_all_gather", 1, [small_partial, flat_ws(d_ws)],
        [((N_DEV, SMALL_ROWS, D_MODEL), F32), ((N_DEV, GROUPS * GMLP_CHUNK, GMLP_CHUNK), F32)],
        [lambda ref, j: ref.at[j], lambda ref, j: ref.at[j]])

    g_in = _weight_grad(
        "grad_w_in", h_t, dproj, whole_t, pl.BlockSpec((None, t, D_MODEL // 2), lambda n: (n // 2, 0, n % 2)),
        (N_DEV, D_MODEL, D_MODEL), pl.BlockSpec((None, D_MODEL, D_MODEL // 2), lambda n: (n // 2, 0, n % 2)), 2 * N_DEV,
        [((D_MODEL, t), BF16), ((t, D_MODEL // 2), BF16), ((D_MODEL, D_MODEL // 2), F32)], True)
    land_i = _exchange_sibling("w_in_grads_to_sibling", 4, [g_in], [lambda ref, j: ref.at[_pos_of_dev(j)]],
                               [(D_MODEL, D_MODEL)])

    (landed_f, landed_b), _ = lax.optimization_barrier(((landed_f, landed_b), g_in))
    big = {}
    for nme, own, lnd, w, m, v in zip(
            names_f + names_b, part_f + part_b, landed_f + landed_b,
            [w_gate_up, w_down, w_branch_a, w_branch_b, w_out], [m_w_gate_up, m_w_down, m_w_branch_a, m_w_branch_b, m_w_out],
            [v_w_gate_up, v_w_down, v_w_branch_a, v_w_branch_b, v_w_out]):
        flip = (lambda z: z.T) if nme == "w_gate_up" else (lambda z: z)
        big[nme] = [flip(o_)[None] for o_ in _adamw("adamw_" + nme, chip, own, lnd, flip(w[0]), flip(m[0]), flip(v[0]))]
    small, loss_row = _adamw_small_unpacked(small_all, w_pack, m_pack, v_pack)
    ws_outs = _adamw_small("adamw_w_s", ws_all, flat_ws(gmlp_w_s), flat_ws(m_gmlp_w_s), flat_ws(v_gmlp_w_s))
    land_i, _ = lax.optimization_barrier((land_i, (big, small, ws_outs)))
    part_i = chip_partials(["w_in"], [g_in], land_i,
                           [((None, D_MODEL, D_MODEL), lambda q, r, c: (_pos_of_dev(2 * q + c), 0, 0))])
    landed_i = _exchange_chips("w_in_grads_to_chips", 7, part_i)

    dx1, _ = lax.optimization_barrier((dx1, part_i))
    grad_x, d_mix_g = _input_backward(dproj, w_in_g, x2d, dx1, norm_mix_g)
    big["w_in"] = [o_[None] for o_ in _adamw("adamw_w_in", chip, part_i[0], landed_i[0], w_in[0], m_w_in[0], v_w_in[0])]

    def row8(a):
        return jnp.pad(a, ((0, 7), (0, 0)))

    d_mix_g, _ = lax.optimization_barrier((d_mix_g, landed_i))
    (mix_all,) = _all_gather_async("mix_gain_grad_all_gather", 8, [row8(d_mix_g)], [((N_DEV, 8, D_MODEL), F32)],
                                   [lambda ref, j: ref.at[j]])
    mix_outs = _adamw_row("adamw_mix_gain", mix_all, norm_mix_g, m_norm_mix_g, v_norm_mix_g)
    small = [dict(p, norm_final_g=p["norm_final_g"][0], gmlp_w_s=ws.reshape(1, GROUPS, GMLP_CHUNK, GMLP_CHUNK), norm_mix_g=q)
             for p, ws, q in zip(small, ws_outs, mix_outs)]

    loss = loss_row[0, 0]
    order = ["norm_mix_g", "w_in", "gmlp_ln_g", "gmlp_ln_b", "gmlp_w_s", "gmlp_b_s", "hgrn_lb_table", "hgrn_norm_g",
             "w_branch_a", "w_branch_b", "w_out", "norm_ffn_g", "w_gate_up", "w_down", "norm_final_g"]
    outs = [loss, grad_x.reshape(1, t, D_MODEL)]
    for kind in range(4):
        for nme in order:
            outs.append(big[nme][kind] if nme in big else small[kind][nme])
    return tuple(outs)
```

```python
import jax
import jax.numpy as jnp
from jax import lax
from jax.experimental import pallas as pl
from jax.experimental.pallas import tpu as pltpu
from jax.experimental.pallas import tpu_sc as plsc

F32, BF16 = jnp.float32, jnp.bfloat16
D_MODEL = 1024
N_DEV = 8
HEADS = 8
HEAD_DIM = 128
GROUPS = 8
GMLP_CHUNK = 128
HGRN_CHUNK = 64
HGRN_SCALE = HEAD_DIM ** -0.5
D_FF = 2816
FF_BLOCK = D_FF // 4
DOWN_ROWS = D_FF // N_DEV
BRANCH_ROWS = D_MODEL // N_DEV
NORM_EPS = 1e-6
ADAM_LR, ADAM_B1, ADAM_B2, ADAM_EPS, ADAM_WD, ADAM_STEP = 0.001, 0.9, 0.999, 1e-08, 0.01, 10
SMALL_ROWS = 72
V7X_VMEM_BYTES = 64 * 1024 * 1024
VMEM_CAP = V7X_VMEM_BYTES - 6 * 1024 * 1024
MESH_ID = pl.DeviceIdType.MESH
ANY = pl.BlockSpec(memory_space=pl.ANY)
RESIDENT = pl.BlockSpec(memory_space=pltpu.VMEM)
Q_POS, U_POS, GATE_POS = 0, 4, 6


def _pos_of_dev(j):
    return jnp.where(j < 2, j + 4, jnp.where(j < 6, j - 2, j))


def _nbytes(shape, dtype):
    n = 1
    for s in shape:
        n *= s
    return n * jnp.dtype(dtype).itemsize


def _params(blocks, scratch=(), temps=0, sem=None):
    need = 2 * sum(_nbytes(s, d) for s, d in blocks) + sum(_nbytes(s, d) for s, d in scratch) + temps
    assert need + (4 << 20) <= VMEM_CAP, need
    return pltpu.CompilerParams(dimension_semantics=sem, vmem_limit_bytes=VMEM_CAP)


def _tile(n, pref):
    return pref if n % pref == 0 else n


def _dot(a, b):
    return jnp.dot(a, b, preferred_element_type=F32)


def _dot_nt(a, b):
    return lax.dot_general(a, b, (((1,), (1,)), ((), ())), preferred_element_type=F32)


def _dot_tn(a, b):
    return lax.dot_general(a, b, (((0,), (0,)), ((), ())), preferred_element_type=F32)


def _sigmoid(x):
    return 1.0 / (1.0 + jnp.exp(-x))


_GELU_C = 0.7978845608028654


def _gelu(x):
    return x * (0.5 * (1.0 + jnp.tanh(_GELU_C * (x + 0.044715 * (x * x * x)))))


def _gelu_and_grad(x):
    t = jnp.tanh(_GELU_C * (x + 0.044715 * (x * x * x)))
    half = 0.5 * (1.0 + t)
    return x * half, half + 0.5 * x * (1.0 - t * t) * (_GELU_C * (1.0 + 3.0 * 0.044715 * x * x))


def _rms_stats(x):
    r = lax.rsqrt(jnp.mean(x * x, axis=-1, keepdims=True) + NORM_EPS)
    return r, x * r


def _rms_bwd(dy, x, g):
    r, xh = _rms_stats(x)
    dg = jnp.sum(dy * xh, axis=0, keepdims=True)
    dxh = dy * g
    dx = r * (dxh - xh * jnp.mean(dxh * xh, axis=-1, keepdims=True))
    return dx, dg


def _split3(x):
    hi = x.astype(BF16)
    r = x - hi.astype(F32)
    mid = r.astype(BF16)
    lo = (r - mid.astype(F32)).astype(BF16)
    return hi, mid, lo


def _mask_mm(mask_bf16, x):
    hi, mid, lo = _split3(x)
    return _dot(mask_bf16, hi) + _dot(mask_bf16, mid) + _dot(mask_bf16, lo)


def _place():
    return lax.axis_index("x"), lax.axis_index("y"), lax.axis_index("c")


def _gather_copies(src, out, send, recv, loc, slicers):
    n = len(src)
    x, y, c = _place()
    me, sib = (x, y, c), (x, y, 1 - c)
    chips = [(1 - x, y), (x, 1 - y), (1 - x, 1 - y)]

    def dev(p):
        return 4 * p[0] + 2 * p[1] + p[2]

    def rc(i, k, block, to, from_src=False):
        dst = slicers[i](out[i], dev(block))
        return pltpu.make_async_remote_copy(
            src_ref=src[i] if from_src else dst, dst_ref=dst, send_sem=send.at[7 * i + k],
            recv_sem=recv.at[7 * i + k], device_id=to, device_id_type=MESH_ID)

    mine = [pltpu.make_async_copy(src[i], slicers[i](out[i], dev(me)), loc.at[i]) for i in range(n)]
    for cp in mine:
        cp.start()
    first = []
    for i in range(n):
        first.append(rc(i, 0, me, sib, True))
        for j, chip in enumerate(chips):
            first.append(rc(i, 1 + j, me, (*chip, c), True))
    for cp in first:
        cp.start()
    passed = []
    for j, chip in enumerate(chips):
        for i in range(n):
            rc(i, 1 + j, (*chip, c), me).wait_recv()
            cp = rc(i, 4 + j, (*chip, c), sib)
            cp.start()
            passed.append(cp)
    for i in range(n):
        rc(i, 0, sib, me).wait_recv()
        for j, chip in enumerate(chips):
            rc(i, 4 + j, (*chip, 1 - c), me).wait_recv()
    for cp in first + passed:
        cp.wait_send()
    for cp in mine:
        cp.wait()


def _gather_copies_balanced(src, out, send, recv, loc, slicer, rows):
    x, y, c = _place()
    me, sib = (x, y, c), (x, y, 1 - c)
    xn, yn, dg = (1 - x, y), (x, 1 - y), (1 - x, 1 - y)
    half_rows = rows // 2

    def block(p):
        return slicer(out, 4 * p[0] + 2 * p[1] + p[2])

    def half(ref, h):
        return ref.at[pl.ds(h * half_rows, half_rows)]

    def rc(k, dst, to, from_src=False):
        return pltpu.make_async_remote_copy(src_ref=src if from_src else dst, dst_ref=dst, send_sem=send.at[k],
                                            recv_sem=recv.at[k], device_id=to, device_id_type=MESH_ID)

    mine = pltpu.make_async_copy(src, block(me), loc.at[0])
    mine.start()
    sends = [rc(0, block(me), sib, True), rc(1, block(me), (*xn, c), True), rc(2, block(me), (*yn, c), True)]
    for cp in sends:
        cp.start()

    def then(cp):
        cp.start()
        sends.append(cp)

    rc(1, block((*xn, c)), me).wait_recv()
    then(rc(3, half(block((*xn, c)), 0), (*yn, c)))
    then(rc(5, block((*xn, c)), sib))
    rc(2, block((*yn, c)), me).wait_recv()
    then(rc(4, half(block((*yn, c)), 1), (*xn, c)))
    then(rc(6, block((*yn, c)), sib))
    rc(3, half(block((*dg, c)), 0), me).wait_recv()
    then(rc(7, half(block((*dg, c)), 0), sib))
    rc(4, half(block((*dg, c)), 1), me).wait_recv()
    then(rc(8, half(block((*dg, c)), 1), sib))
    rc(0, block(sib), me).wait_recv()
    rc(5, block((*xn, 1 - c)), me).wait_recv()
    rc(6, block((*yn, 1 - c)), me).wait_recv()
    rc(7, half(block((*dg, 1 - c)), 0), me).wait_recv()
    rc(8, half(block((*dg, 1 - c)), 1), me).wait_recv()
    for cp in sends:
        cp.wait_send()
    mine.wait()


def _gather_scratch(n):
    return [pltpu.SemaphoreType.DMA((7 * n,)), pltpu.SemaphoreType.DMA((7 * n,)), pltpu.SemaphoreType.DMA((n,))]


def _handshake(peers):
    barrier = pltpu.get_barrier_semaphore()
    for peer in peers:
        pl.semaphore_signal(barrier, inc=1, device_id=peer, device_id_type=MESH_ID)
    pl.semaphore_wait(barrier, len(peers))


def _all_gather_async(name, collective_id, srcs, out_shapes, slicers):
    n = len(srcs)

    def body(*refs):
        x, y, c = _place()
        _handshake([(1 - x if dx else x, 1 - y if dy else y, 1 - c if dc else c)
                    for dx in (0, 1) for dy in (0, 1) for dc in (0, 1) if dx or dy or dc])
        _gather_copies(refs[:n], refs[n:2 * n], *refs[2 * n:], slicers)

    return _sequencer_call(name, collective_id, body, srcs, [jax.ShapeDtypeStruct(s, d) for s, d in out_shapes],
                           _gather_scratch(n))


def _all_gather_balanced_async(name, collective_id, src, out_shape, slicer, rows):
    def body(src_ref, out_ref, send, recv, loc):
        x, y, c = _place()
        _handshake([(1 - x if dx else x, 1 - y if dy else y, 1 - c if dc else c)
                    for dx in (0, 1) for dy in (0, 1) for dc in (0, 1) if dx or dy or dc])
        _gather_copies_balanced(src_ref, out_ref, send, recv, loc, slicer, rows)

    return _sequencer_call(name, collective_id, body, [src], [jax.ShapeDtypeStruct(*out_shape)],
                           [pltpu.SemaphoreType.DMA((9,)), pltpu.SemaphoreType.DMA((9,)), pltpu.SemaphoreType.DMA((1,))])[0]


def _sequencer_call(name, collective_id, body, operands, out_types, scratch):
    return pl.kernel(
        body, out_type=out_types, mesh=plsc.ScalarSubcoreMesh(axis_name="sequencer", num_cores=1), name=name,
        scratch_types=scratch, compiler_params=pltpu.CompilerParams(collective_id=collective_id),
    )(*operands)


def _exchange_sibling(name, collective_id, grads, shard_fns, shard_shapes):
    n = len(grads)

    def body(*refs):
        g, land = refs[:n], refs[n:2 * n]
        send, recv = refs[2 * n:]
        x, y, c = _place()
        _handshake([(x, y, 1 - c)])
        remote = []
        for i in range(n):
            for q in range(4):
                cp = pltpu.make_async_remote_copy(
                    src_ref=shard_fns[i](g[i], 2 * q + (1 - c)), dst_ref=land[i].at[q], send_sem=send.at[4 * i + q],
                    recv_sem=recv.at[4 * i + q], device_id=(x, y, 1 - c), device_id_type=MESH_ID)
                cp.start()
                remote.append(cp)
        for cp in remote:
            cp.wait()

    return _sequencer_call(name, collective_id, body, grads, [jax.ShapeDtypeStruct((4, *s), F32) for s in shard_shapes],
                           [pltpu.SemaphoreType.DMA((4 * n,)), pltpu.SemaphoreType.DMA((4 * n,))])


def _exchange_chips(name, collective_id, parts):
    n = len(parts)

    def body(*refs):
        part, out = refs[:n], refs[n:2 * n]
        send, recv = refs[2 * n:]
        x, y, c = _place()
        _handshake([(1 - x, y, c), (x, 1 - y, c), (1 - x, 1 - y, c)])
        remote = []
        for i in range(n):
            for s in range(3):
                qx = 1 - x if (s + 1) // 2 else x
                qy = 1 - y if (s + 1) % 2 else y
                cp = pltpu.make_async_remote_copy(
                    src_ref=part[i].at[2 * qx + qy], dst_ref=out[i].at[s], send_sem=send.at[3 * i + s],
                    recv_sem=recv.at[3 * i + s], device_id=(qx, qy, c), device_id_type=MESH_ID)
                cp.start()
                remote.append(cp)
        for cp in remote:
            cp.wait()

    return _sequencer_call(name, collective_id, body, parts,
                           [jax.ShapeDtypeStruct((3, *p.shape[1:]), p.dtype) for p in parts],
                           [pltpu.SemaphoreType.DMA((3 * n,)), pltpu.SemaphoreType.DMA((3 * n,))])


def _chip_partial(name, core, grad, own_block, own_index, land):
    _, rows, cols = land.shape
    tr = own_block[-2]

    def body(core_ref, a_ref, b_ref, o_ref):
        o_ref[...] = (a_ref[...] + b_ref[...]).astype(BF16)

    spec = pl.BlockSpec((None, tr, cols), lambda q, r, c: (q, r, 0))
    return pl.pallas_call(
        body, name=name, out_shape=jax.ShapeDtypeStruct(land.shape, BF16),
        grid_spec=pltpu.PrefetchScalarGridSpec(
            num_scalar_prefetch=1, grid=(4, rows // tr),
            in_specs=[pl.BlockSpec(own_block, lambda q, r, c: own_index(q, r, c[0])), spec], out_specs=spec),
        compiler_params=_params([((tr, cols), F32)] * 2 + [((tr, cols), BF16)], sem=("arbitrary", "arbitrary")),
    )(core, grad, land)


def _adamw_math(w, g, m, v):
    m = ADAM_B1 * m + (1.0 - ADAM_B1) * g
    v = ADAM_B2 * v + (1.0 - ADAM_B2) * (g * g)
    m_hat = m / (1.0 - ADAM_B1 ** ADAM_STEP)
    v_hat = v / (1.0 - ADAM_B2 ** ADAM_STEP)
    delta = -ADAM_LR * (m_hat / (jnp.sqrt(v_hat) + ADAM_EPS) + ADAM_WD * w)
    return delta, m, v


def _adamw(name, chip, own, landed, w, m, v):
    _, rows, cols = own.shape
    tr = _tile(rows, 512) if rows % 512 == 0 else _tile(rows, 176)

    def body(chip_ref, own_ref, l_ref, w_ref, m_ref, v_ref, g_out, d_out, m_out, v_out):
        g = own_ref[...].astype(F32)
        for s in range(3):
            g = g + l_ref[s].astype(F32)
        delta, m_new, v_new = _adamw_math(w_ref[...], g, m_ref[...], v_ref[...])
        g_out[...] = g
        d_out[...] = delta
        m_out[...] = m_new
        v_out[...] = v_new

    spec = pl.BlockSpec((tr, cols), lambda r, c: (r, 0))
    return pl.pallas_call(
        body, name=name, out_shape=[jax.ShapeDtypeStruct((rows, cols), F32)] * 4,
        grid_spec=pltpu.PrefetchScalarGridSpec(
            num_scalar_prefetch=1, grid=(rows // tr,),
            in_specs=[pl.BlockSpec((None, tr, cols), lambda r, c: (c[0], r, 0)),
                      pl.BlockSpec((3, tr, cols), lambda r, c: (0, r, 0)), spec, spec, spec],
            out_specs=[spec] * 4),
        compiler_params=_params([((4, tr, cols), own.dtype)] + [((tr, cols), F32)] * 7, sem=("arbitrary",)),
    )(chip, own, landed, w, m, v)


def _swap_with_sibling(name, x):
    def body(x_ref, o_ref, send, recv):
        px, py, c = _place()
        cp = pltpu.make_async_remote_copy(src_ref=x_ref, dst_ref=o_ref, send_sem=send, recv_sem=recv,
                                          device_id=(px, py, 1 - c), device_id_type=MESH_ID)
        cp.start()
        cp.wait()

    return pl.pallas_call(
        body, name=name, out_shape=jax.ShapeDtypeStruct(x.shape, x.dtype), in_specs=[ANY], out_specs=ANY,
        scratch_shapes=[pltpu.SemaphoreType.DMA, pltpu.SemaphoreType.DMA],
    )(x)


def _proj_forward_own_chip(positions, x, gain, w_own, w_sibling):
    t = x.shape[0]
    tm = _tile(t, 1024)

    def body(pos_ref, x_ref, g_ref, wo_ref, ws_ref, o_ref, h_ref, ht_ref):
        @pl.when(pl.program_id(1) == 0)
        def _():
            _, xh = _rms_stats(x_ref[...])
            h = (xh * g_ref[...]).astype(BF16)
            h_ref[...] = h
            ht_ref[...] = h.T
            o_ref[...] = _dot(h, wo_ref[...])

        @pl.when(pl.program_id(1) == 1)
        def _():
            o_ref[...] = _dot(h_ref[...], ws_ref[...])

    tok = pl.BlockSpec((tm, D_MODEL), lambda m, k, pos: (m, 0))
    return pl.pallas_call(
        body, name="proj_fwd_own_chip",
        out_shape=[jax.ShapeDtypeStruct((N_DEV, t, D_MODEL), F32), jax.ShapeDtypeStruct((t, D_MODEL), BF16),
                   jax.ShapeDtypeStruct((D_MODEL, t), BF16)],
        grid_spec=pltpu.PrefetchScalarGridSpec(
            num_scalar_prefetch=1, grid=(t // tm, 2),
            in_specs=[tok, pl.BlockSpec((1, D_MODEL), lambda m, k, pos: (0, 0)), RESIDENT, RESIDENT],
            out_specs=[pl.BlockSpec((None, tm, D_MODEL), lambda m, k, pos: (pos[k], m, 0)), tok,
                       pl.BlockSpec((D_MODEL, tm), lambda m, k, pos: (0, m))]),
        compiler_params=_params([((tm, D_MODEL), F32)] * 2 + [((tm, D_MODEL), BF16)] * 2,
                                scratch=[((2, D_MODEL, D_MODEL), BF16)], temps=6 << 20, sem=("arbitrary", "arbitrary")),
    )(positions, x, gain, w_own, w_sibling)


def _proj_forward_other_chips(positions, proj, h, w_in_g):
    t = h.shape[0]
    tm = _tile(t, 2048)

    def body(pos_ref, _, h_ref, w_ref, o_ref):
        o_ref[...] = _dot(h_ref[...], w_ref[pos_ref[pl.program_id(1)]])

    return pl.pallas_call(
        body, name="proj_fwd_other_chips", out_shape=jax.ShapeDtypeStruct(proj.shape, F32),
        grid_spec=pltpu.PrefetchScalarGridSpec(
            num_scalar_prefetch=1, grid=(t // tm, N_DEV - 2),
            in_specs=[ANY, pl.BlockSpec((tm, D_MODEL), lambda m, k, pos: (m, 0)), RESIDENT],
            out_specs=pl.BlockSpec((None, tm, D_MODEL), lambda m, k, pos: (pos[k], m, 0))),
        input_output_aliases={1: 0},
        compiler_params=_params([((tm, D_MODEL), F32), ((tm, D_MODEL), BF16)], scratch=[((N_DEV, D_MODEL, D_MODEL), BF16)],
                                temps=6 << 20, sem=("arbitrary", "arbitrary")),
    )(positions, proj, h, w_in_g)


def _masked_ws(ws_ref, g):
    row = lax.broadcasted_iota(jnp.int32, (GMLP_CHUNK, GMLP_CHUNK), 0)
    col = lax.broadcasted_iota(jnp.int32, (GMLP_CHUNK, GMLP_CHUNK), 1)
    return jnp.where(row >= col, ws_ref[g], 0.0).astype(BF16)


def _gmlp_forward(proj, ln_g, ln_b, w_s, bias_b):
    t = proj.shape[1]
    tm = _tile(t, 512)
    chunks = tm // GMLP_CHUNK

    def body(u_ref, v_ref, lng_ref, lnb_ref, ws_ref, bias_ref, a_ref, vn_scr):
        vv = _gelu(v_ref[...])
        mu = jnp.mean(vv, axis=-1, keepdims=True)
        cen = vv - mu
        var = jnp.mean(cen * cen, axis=-1, keepdims=True)
        vn_scr[...] = ((cen * lax.rsqrt(var + NORM_EPS)) * lng_ref[...] + lnb_ref[...]).astype(BF16)
        for g in range(GROUPS):
            wm = _masked_ws(ws_ref, g)
            cols = slice(g * HEAD_DIM, (g + 1) * HEAD_DIM)
            for c in range(chunks):
                rows = slice(c * GMLP_CHUNK, (c + 1) * GMLP_CHUNK)
                mixed = _dot(wm, vn_scr[rows, cols]) + bias_ref[g]
                a_ref[rows, cols] = (_gelu(u_ref[rows, cols]) * mixed).astype(BF16)

    small = pl.BlockSpec((GROUPS, GMLP_CHUNK, GMLP_CHUNK), lambda m: (0, 0, 0))
    vec = pl.BlockSpec((1, D_MODEL), lambda m: (0, 0))
    return pl.pallas_call(
        body, name="gmlp_fwd", out_shape=jax.ShapeDtypeStruct((t, D_MODEL), BF16), grid=(t // tm,),
        in_specs=[pl.BlockSpec((None, tm, D_MODEL), lambda m: (U_POS, m, 0)),
                  pl.BlockSpec((None, tm, D_MODEL), lambda m: (U_POS + 1, m, 0)), vec, vec, small, small],
        out_specs=pl.BlockSpec((tm, D_MODEL), lambda m: (m, 0)),
        scratch_shapes=[pltpu.VMEM((tm, D_MODEL), BF16)],
        compiler_params=_params([((tm, D_MODEL), F32)] * 2 + [((tm, D_MODEL), BF16)] + [((8, 128, 128), F32)] * 2,
                                scratch=[((tm, D_MODEL), BF16)], temps=8 << 20, sem=("arbitrary",)),
    )(proj, proj, ln_g, ln_b, w_s, bias_b)


def _lower_bound(tab_ref):
    t0, t1 = tab_ref[0:1, :], tab_ref[1:2, :]
    mx = jnp.maximum(t0, t1)
    e0, e1 = jnp.exp(t0 - mx), jnp.exp(t1 - mx)
    return e0 / (e0 + e1)


def _tri_masks():
    row = lax.broadcasted_iota(jnp.int32, (HGRN_CHUNK, HGRN_CHUNK), 0)
    col = lax.broadcasted_iota(jnp.int32, (HGRN_CHUNK, HGRN_CHUNK), 1)
    return row >= col, row <= col


def _chunk_rows(c):
    return slice(c * HGRN_CHUNK, (c + 1) * HGRN_CHUNK)


def _per_chunk(x, nc, fn):
    return jnp.concatenate([fn(x[_chunk_rows(c)]) for c in range(nc)], axis=0)


def _chunk_row_bcast(x, nc, i):
    return _per_chunk(x, nc, lambda xc: jnp.broadcast_to(xc[i:i + 1, :], (HGRN_CHUNK, HEAD_DIM)))


def _hgrn_gates(q, fl, lb, nc):
    lower, _ = _tri_masks()
    lower = lower.astype(BF16)
    s = _sigmoid(fl)
    f = lb + (1.0 - lb) * s
    k = 1.0 - f
    hi, mid, lo = _split3(jnp.log(f))
    a = jnp.concatenate([_dot(lower, hi[_chunk_rows(c)]) + _dot(lower, mid[_chunk_rows(c)]) + _dot(lower, lo[_chunk_rows(c)])
                         for c in range(nc)], axis=0)
    a_mid = _chunk_row_bcast(a, nc, HGRN_CHUNK // 2 - 1)
    a_last = _chunk_row_bcast(a, nc, HGRN_CHUNK - 1)
    qs = q * HGRN_SCALE
    e_in, e_out, e_end, e_all = jnp.exp(a - a_mid), jnp.exp(a_mid - a), jnp.exp(a_last - a), jnp.exp(a)
    decay = [jnp.exp(a[c * HGRN_CHUNK + HGRN_CHUNK - 1:(c + 1) * HGRN_CHUNK, :]) for c in range(nc)]
    return dict(s=s, f=f, k=k, decay=decay, e_in=e_in, e_out=e_out, e_end=e_end, e_all=e_all,
                qi=qs * e_in, ki=k * e_out, kd=k * e_end, qe=qs * e_all)


def _hgrn_forward(proj, lb_table, norm_g):
    t = proj.shape[1]
    tb = _tile(t, 512)
    nc = tb // HGRN_CHUNK
    n_chunks = t // HGRN_CHUNK
    pair = 2 * HEAD_DIM

    def one_head(q, fl, v, gz, tab, ng, st):
        lower, _ = _tri_masks()
        gt = _hgrn_gates(q, fl, _lower_bound(tab), nc)
        qi, ki, kd, qe = (gt[n].astype(BF16) for n in ("qi", "ki", "kd", "qe"))
        vb = v.astype(BF16)
        o_intra, d_state = [], []
        for c in range(nc):
            rows = _chunk_rows(c)
            p = jnp.where(lower, _dot_nt(qi[rows], ki[rows]), 0.0).astype(BF16)
            o_intra.append(_dot(p, vb[rows]))
            d_state.append(_dot_tn(vb[rows], kd[rows]))
        outs, before = [], []
        for c in range(nc):
            before.append(st)
            outs.append(o_intra[c] + _dot_nt(qe[_chunk_rows(c)], st.astype(BF16)))
            st = st * gt["decay"][c] + d_state[c]
        o = jnp.concatenate(outs, axis=0)
        _, oh = _rms_stats(o)
        return o, ((oh * ng) * (gz * _sigmoid(gz))).astype(BF16), before, st

    def body(q_ref, f_ref, i_ref, g_ref, tab_ref, ng_ref, og_ref, o_ref, st_ref, state):
        @pl.when(pl.program_id(1) == 0)
        def _():
            state[...] = jnp.zeros_like(state)

        for k in range(2):
            cols = slice(k * HEAD_DIM, (k + 1) * HEAD_DIM)
            o, og, before, after = one_head(q_ref[:, cols], f_ref[:, cols], i_ref[:, cols], g_ref[:, cols],
                                            tab_ref[:, cols], ng_ref[:, cols], state[k])
            o_ref[:, cols] = o
            og_ref[:, cols] = og
            for c in range(nc):
                st_ref[k, c] = before[c]
            state[k] = after

    def blk(p):
        return pl.BlockSpec((None, tb, pair), lambda h, n: (p, n, h))

    out_blk = pl.BlockSpec((tb, pair), lambda h, n: (n, h))
    return pl.pallas_call(
        body, name="hgrn_fwd",
        out_shape=[jax.ShapeDtypeStruct((t, D_MODEL), BF16), jax.ShapeDtypeStruct((t, D_MODEL), F32),
                   jax.ShapeDtypeStruct((HEADS, n_chunks, HEAD_DIM, HEAD_DIM), F32)],
        grid=(HEADS // 2, t // tb),
        in_specs=[blk(Q_POS), blk(Q_POS + 1), blk(Q_POS + 2), blk(Q_POS + 3),
                  pl.BlockSpec((2, pair), lambda h, n: (0, h)), pl.BlockSpec((1, pair), lambda h, n: (0, h))],
        out_specs=[out_blk, out_blk, pl.BlockSpec((2, nc, HEAD_DIM, HEAD_DIM), lambda h, n: (h, n, 0, 0))],
        scratch_shapes=[pltpu.VMEM((2, HEAD_DIM, HEAD_DIM), F32)],
        compiler_params=_params([((tb, pair), F32)] * 6 + [((2, nc, HEAD_DIM, HEAD_DIM), F32)], temps=8 << 20,
                                sem=("arbitrary", "arbitrary")),
    )(proj, proj, proj, proj, lb_table, norm_g)


def _branch_out_forward(a, og, proj, x, w_a, w_b, w_out, ffn_g):
    t = x.shape[0]
    tm = _tile(t, 512)

    def body(a_ref, og_ref, ga_ref, gb_ref, x_ref, wa_ref, wb_ref, wo_ref, g_ref, ya_ref, yb_ref, mg_ref, x1_ref, h2_ref):
        ya = _dot(a_ref[...], wa_ref[...])
        yb = _dot(og_ref[...], wb_ref[...])
        ya_ref[...] = ya
        yb_ref[...] = yb
        merged = (_sigmoid(ga_ref[...]) * ya + _sigmoid(gb_ref[...]) * yb).astype(BF16)
        mg_ref[...] = merged
        x1 = x_ref[...] + _dot(merged, wo_ref[...])
        x1_ref[...] = x1
        _, xh = _rms_stats(x1)
        h2_ref[...] = (xh * g_ref[...]).astype(BF16)

    tok = pl.BlockSpec((tm, D_MODEL), lambda m: (m, 0))
    return pl.pallas_call(
        body, name="branch_out_fwd",
        out_shape=[jax.ShapeDtypeStruct((t, D_MODEL), F32), jax.ShapeDtypeStruct((t, D_MODEL), F32),
                   jax.ShapeDtypeStruct((t, D_MODEL), BF16), jax.ShapeDtypeStruct((t, D_MODEL), F32),
                   jax.ShapeDtypeStruct((t, D_MODEL), BF16)],
        grid=(t // tm,),
        in_specs=[tok, tok, pl.BlockSpec((None, tm, D_MODEL), lambda m: (GATE_POS, m, 0)),
                  pl.BlockSpec((None, tm, D_MODEL), lambda m: (GATE_POS + 1, m, 0)), tok, RESIDENT, RESIDENT, RESIDENT,
                  pl.BlockSpec((1, D_MODEL), lambda m: (0, 0))],
        out_specs=[tok] * 5,
        compiler_params=_params([((tm, D_MODEL), BF16)] * 4 + [((tm, D_MODEL), F32)] * 6, scratch=[((D_MODEL, D_MODEL), BF16)] * 3,
                                temps=8 << 20, sem=("arbitrary",)),
    )(a, og, proj, proj, x, w_a, w_b, w_out, ffn_g)


def _ffn_forward(h2, x1, w_gu, w_down, target, final_g):
    t = x1.shape[0]
    tm = _tile(t, 512)

    def body(h_ref, wgu_ref, wd_ref, x1_ref, t_ref, g_ref, gu_ref, act_ref, loss_ref, dg_ref, dx_ref, dxb_ref, acc):
        m, j = pl.program_id(0), pl.program_id(1)

        @pl.when((m == 0) & (j == 0))
        def _():
            loss_ref[...] = jnp.zeros_like(loss_ref)
            dg_ref[...] = jnp.zeros_like(dg_ref)

        h = h_ref[...]
        gate = _dot_nt(h, wgu_ref[j])
        up = _dot_nt(h, wgu_ref[j + 4])
        gu_ref[0] = gate
        gu_ref[1] = up
        act = ((gate * _sigmoid(gate)) * up).astype(BF16)
        act_ref[...] = act
        part = _dot(act, wd_ref[j])

        @pl.when(j == 0)
        def _():
            acc[...] = part

        @pl.when((j > 0) & (j < 3))
        def _():
            acc[...] += part

        @pl.when(j == 3)
        def _():
            x2 = x1_ref[...] + (acc[...] + part)
            g = g_ref[...]
            r, xh = _rms_stats(x2)
            err = xh * g - t_ref[...]
            loss_ref[...] += 0.5 * jnp.sum(jnp.mean(err * err, axis=-1, keepdims=True), axis=0, keepdims=True)
            dy = err * (1.0 / D_MODEL)
            dg_ref[...] += jnp.sum(dy * xh, axis=0, keepdims=True)
            dxh = dy * g
            dx = r * (dxh - xh * jnp.mean(dxh * xh, axis=-1, keepdims=True))
            dx_ref[...] = dx
            dxb_ref[...] = dx.astype(BF16)

    tok = pl.BlockSpec((tm, D_MODEL), lambda m, j: (m, 0))
    vec = pl.BlockSpec((1, D_MODEL), lambda m, j: (0, 0))
    return pl.pallas_call(
        body, name="ffn_fwd",
        out_shape=[jax.ShapeDtypeStruct((4, 2, t, FF_BLOCK), F32), jax.ShapeDtypeStruct((4, t, FF_BLOCK), BF16),
                   jax.ShapeDtypeStruct((8, 128), F32), jax.ShapeDtypeStruct((1, D_MODEL), F32),
                   jax.ShapeDtypeStruct((t, D_MODEL), F32), jax.ShapeDtypeStruct((t, D_MODEL), BF16)],
        grid=(t // tm, 4),
        in_specs=[tok, RESIDENT, RESIDENT, tok, tok, vec],
        out_specs=[pl.BlockSpec((None, 2, tm, FF_BLOCK), lambda m, j: (j, 0, m, 0)),
                   pl.BlockSpec((None, tm, FF_BLOCK), lambda m, j: (j, m, 0)),
                   pl.BlockSpec((8, 128), lambda m, j: (0, 0)), vec, tok, tok],
        scratch_shapes=[pltpu.VMEM((tm, D_MODEL), F32)],
        compiler_params=_params([((tm, D_MODEL), BF16), ((tm, D_MODEL), F32), ((tm, D_MODEL), F32), ((2, tm, 768), F32),
                                 ((tm, 768), BF16), ((tm, D_MODEL), F32), ((tm, D_MODEL), BF16)],
                                scratch=[((tm, D_MODEL), F32), ((N_DEV, FF_BLOCK, D_MODEL), BF16), ((D_FF, D_MODEL), BF16)],
                                temps=6 << 20, sem=("arbitrary", "arbitrary")),
    )(h2, w_gu, w_down.reshape(4, FF_BLOCK, D_MODEL), x1, target, final_g)


def _ffn_backward(dx2b, dx2, gu, x1, w_gu, w_down, ffn_g):
    t = x1.shape[0]
    tm = _tile(t, 512)

    def body(dxb_ref, dx2_ref, gu_ref, x1_ref, wgu_ref, wd_ref, g_ref, dgu_ref, dx1_ref, dx1b_ref, dg_ref, acc, prev):
        m, j = pl.program_id(0), pl.program_id(1)

        @pl.when((m == 0) & (j == 0))
        def _():
            dg_ref[...] = jnp.zeros_like(dg_ref)

        @pl.when(j == 0)
        def _():
            prev[...] = jnp.zeros_like(prev)
            acc[...] = jnp.zeros_like(acc)

        jm1 = jnp.maximum(j - 1, 0)
        acc[...] += _dot(prev[0], wgu_ref[jm1]) + _dot(prev[1], wgu_ref[jm1 + 4])
        dact = _dot_nt(dxb_ref[...], wd_ref[j])
        gate, up = gu_ref[0], gu_ref[1]
        sg = _sigmoid(gate)
        dgate = (dact * up * (sg * (1.0 + gate * (1.0 - sg)))).astype(BF16)
        dup = (dact * (gate * sg)).astype(BF16)
        dgu_ref[0] = dgate
        dgu_ref[1] = dup
        prev[0] = dgate
        prev[1] = dup

        @pl.when(j == 3)
        def _():
            dh2 = acc[...] + (_dot(prev[0], wgu_ref[3]) + _dot(prev[1], wgu_ref[7]))
            dx, dg = _rms_bwd(dh2, x1_ref[...], g_ref[...])
            dx1 = dx2_ref[...] + dx
            dx1_ref[...] = dx1
            dx1b_ref[...] = dx1.astype(BF16)
            dg_ref[...] += dg

    tok = pl.BlockSpec((tm, D_MODEL), lambda m, j: (m, 0))
    vec = pl.BlockSpec((1, D_MODEL), lambda m, j: (0, 0))
    gu_spec = pl.BlockSpec((None, 2, tm, FF_BLOCK), lambda m, j: (j, 0, m, 0))
    return pl.pallas_call(
        body, name="ffn_bwd",
        out_shape=[jax.ShapeDtypeStruct((4, 2, t, FF_BLOCK), BF16), jax.ShapeDtypeStruct((t, D_MODEL), F32),
                   jax.ShapeDtypeStruct((t, D_MODEL), BF16), jax.ShapeDtypeStruct((1, D_MODEL), F32)],
        grid=(t // tm, 4),
        in_specs=[tok, tok, gu_spec, tok, RESIDENT, RESIDENT, vec],
        out_specs=[gu_spec, tok, tok, vec],
        scratch_shapes=[pltpu.VMEM((tm, D_MODEL), F32), pltpu.VMEM((2, tm, FF_BLOCK), BF16)],
        compiler_params=_params([((tm, D_MODEL), BF16), ((tm, D_MODEL), F32), ((2, tm, 768), F32), ((tm, D_MODEL), F32),
                                 ((2, tm, 768), BF16), ((tm, D_MODEL), F32), ((tm, D_MODEL), BF16)],
                                scratch=[((tm, D_MODEL), F32), ((2, tm, 768), BF16), ((N_DEV, FF_BLOCK, D_MODEL), BF16),
                                         ((D_FF, D_MODEL), BF16)],
                                temps=4 << 20, sem=("arbitrary", "arbitrary")),
    )(dx2b, dx2, gu, x1, w_gu, w_down.reshape(4, FF_BLOCK, D_MODEL), ffn_g)


def _branch_out_backward(dx1b, ya, yb, proj, w_a, w_b, w_out):
    t = ya.shape[0]
    tm = _tile(t, 512)

    def body(dx_ref, ya_ref, yb_ref, ga_ref, gb_ref, wa_ref, wb_ref, wo_ref, dya_ref, dyb_ref, dgate_ref, da_ref, dog_ref):
        dm = _dot_nt(dx_ref[...], wo_ref[...])
        sa, sb = _sigmoid(ga_ref[...]), _sigmoid(gb_ref[...])
        dya = (dm * sa).astype(BF16)
        dyb = (dm * sb).astype(BF16)
        dya_ref[...] = dya
        dyb_ref[...] = dyb
        dgate_ref[0] = (dm * ya_ref[...] * (sa * (1.0 - sa))).astype(BF16)
        dgate_ref[1] = (dm * yb_ref[...] * (sb * (1.0 - sb))).astype(BF16)
        da_ref[...] = _dot_nt(dya, wa_ref[...])
        dog_ref[...] = _dot_nt(dyb, wb_ref[...])

    tok = pl.BlockSpec((tm, D_MODEL), lambda m: (m, 0))
    return pl.pallas_call(
        body, name="branch_out_bwd",
        out_shape=[jax.ShapeDtypeStruct((t, D_MODEL), BF16), jax.ShapeDtypeStruct((t, D_MODEL), BF16),
                   jax.ShapeDtypeStruct((N_DEV, t, D_MODEL), BF16), jax.ShapeDtypeStruct((t, D_MODEL), F32),
                   jax.ShapeDtypeStruct((t, D_MODEL), F32)],
        grid=(t // tm,),
        in_specs=[tok, tok, tok, pl.BlockSpec((None, tm, D_MODEL), lambda m: (GATE_POS, m, 0)),
                  pl.BlockSpec((None, tm, D_MODEL), lambda m: (GATE_POS + 1, m, 0)), RESIDENT, RESIDENT, RESIDENT],
        out_specs=[tok, tok, pl.BlockSpec((2, tm, D_MODEL), lambda m: (GATE_POS // 2, m, 0)), tok, tok],
        compiler_params=_params([((tm, D_MODEL), BF16)] * 5 + [((tm, D_MODEL), F32)] * 6, scratch=[((D_MODEL, D_MODEL), BF16)] * 3,
                                temps=8 << 20, sem=("arbitrary",)),
    )(dx1b, ya, yb, proj, proj, w_a, w_b, w_out)


def _hgrn_backward(dproj, dog, o_saved, states, proj, lb_table, norm_g):
    t = proj.shape[1]
    tb = _tile(t, 1024)
    nc = tb // HGRN_CHUNK
    nb = t // tb

    def body(_, dog_ref, o_ref, st_ref, q_ref, f_ref, i_ref, g_ref, tab_ref, ng_ref, dp_ref, dng_ref, dtab_ref, gstate):
        @pl.when(pl.program_id(1) == 0)
        def _():
            gstate[...] = jnp.zeros_like(gstate)
            dng_ref[...] = jnp.zeros_like(dng_ref)
            dtab_ref[...] = jnp.zeros_like(dtab_ref)

        lb = _lower_bound(tab_ref)
        ng = ng_ref[...]
        lower, upper = _tri_masks()
        gt = _hgrn_gates(q_ref[...], f_ref[...], lb, nc)
        qi, ki, kd, qe = (gt[n].astype(BF16) for n in ("qi", "ki", "kd", "qe"))
        vb = i_ref[...].astype(BF16)
        o, gz, d_og = o_ref[...], g_ref[...], dog_ref[...]
        r, oh = _rms_stats(o)
        sg = _sigmoid(gz)
        d_on = d_og * (gz * sg)
        dgz = d_og * (oh * ng) * (sg * (1.0 + gz * (1.0 - sg)))
        dng_ref[...] += jnp.sum(d_on * oh, axis=0, keepdims=True)
        doh = d_on * ng
        dob = (r * (doh - oh * jnp.mean(doh * oh, axis=-1, keepdims=True))).astype(BF16)
        dv_intra, dqi, dki, dqe, g_upd = [], [], [], [], []
        for c in range(nc):
            rows = _chunk_rows(c)
            p = jnp.where(lower, _dot_nt(qi[rows], ki[rows]), 0.0).astype(BF16)
            dv_intra.append(_dot_tn(p, dob[rows]))
            dp = jnp.where(lower, _dot_nt(dob[rows], vb[rows]), 0.0).astype(BF16)
            dqi.append(_dot(dp, ki[rows]))
            dki.append(_dot_tn(dp, qi[rows]))
            dqe.append(_dot(dob[rows], st_ref[c].astype(BF16)))
            g_upd.append(_dot_tn(dob[rows], qe[rows]))
        g_after = [None] * nc
        g = gstate[...]
        for c in reversed(range(nc)):
            g_after[c] = g
            g = g * gt["decay"][c] + g_upd[c]
        gstate[...] = g
        dkd, dv, da_last = [], [], []
        for c in range(nc):
            rows = _chunk_rows(c)
            gb = g_after[c].astype(BF16)
            dkd.append(_dot(vb[rows], gb))
            dv.append(dv_intra[c] + _dot_nt(kd[rows], gb))
            da_last.append(jnp.sum(g_after[c] * st_ref[c], axis=0, keepdims=True) * gt["decay"][c])
        dqi, dki, dqe, dkd, dv = (jnp.concatenate(z, axis=0) for z in (dqi, dki, dqe, dkd, dv))
        dqs = dqi * gt["e_in"] + dqe * gt["e_all"]
        dk = dki * gt["e_out"] + dkd * gt["e_end"]
        t_in, t_out, t_end = dqi * gt["qi"], dki * gt["ki"], dkd * gt["kd"]
        da = t_in - t_out + dqe * gt["qe"] - t_end
        row = lax.broadcasted_iota(jnp.int32, (HGRN_CHUNK, HEAD_DIM), 0)
        d_mid = t_out - t_in
        pieces = []
        for c in range(nc):
            rows = _chunk_rows(c)
            da_mid = jnp.sum(d_mid[rows], axis=0, keepdims=True)
            da_end = jnp.sum(t_end[rows], axis=0, keepdims=True) + da_last[c]
            da_c = da[rows] + jnp.where(row == HGRN_CHUNK // 2 - 1, da_mid, 0.0) + jnp.where(row == HGRN_CHUNK - 1, da_end, 0.0)
            pieces.append(_mask_mm(upper.astype(BF16), da_c))
        df = jnp.concatenate(pieces, axis=0) / gt["f"] - dk
        s = gt["s"]
        dlb = jnp.sum(df * (1.0 - s), axis=0, keepdims=True)
        dp_ref[0] = (dqs * HGRN_SCALE).astype(BF16)
        dp_ref[1] = (df * (1.0 - lb) * (s * (1.0 - s))).astype(BF16)
        dp_ref[2] = dv.astype(BF16)
        dp_ref[3] = dgz.astype(BF16)
        dt0 = dlb * (lb * (1.0 - lb))
        dtab_ref[0:1, :] += dt0
        dtab_ref[1:2, :] -= dt0

    def blk(p):
        return pl.BlockSpec((None, tb, HEAD_DIM), lambda h, n: (p, nb - 1 - n, h))

    tok = pl.BlockSpec((tb, HEAD_DIM), lambda h, n: (nb - 1 - n, h))
    return pl.pallas_call(
        body, name="hgrn_bwd",
        out_shape=[jax.ShapeDtypeStruct((N_DEV, t, D_MODEL), BF16), jax.ShapeDtypeStruct((1, D_MODEL), F32),
                   jax.ShapeDtypeStruct((2, D_MODEL), F32)],
        grid=(HEADS, nb),
        in_specs=[ANY, tok, tok, pl.BlockSpec((None, nc, HEAD_DIM, HEAD_DIM), lambda h, n: (h, nb - 1 - n, 0, 0)),
                  blk(Q_POS), blk(Q_POS + 1), blk(Q_POS + 2), blk(Q_POS + 3),
                  pl.BlockSpec((2, HEAD_DIM), lambda h, n: (0, h)), pl.BlockSpec((1, HEAD_DIM), lambda h, n: (0, h))],
        out_specs=[pl.BlockSpec((4, tb, HEAD_DIM), lambda h, n: (0, nb - 1 - n, h)),
                   pl.BlockSpec((1, HEAD_DIM), lambda h, n: (0, h)), pl.BlockSpec((2, HEAD_DIM), lambda h, n: (0, h))],
        scratch_shapes=[pltpu.VMEM((HEAD_DIM, HEAD_DIM), F32)],
        input_output_aliases={0: 0},
        compiler_params=_params([((tb, HEAD_DIM), F32)] * 6 + [((nc, HEAD_DIM, HEAD_DIM), F32)] + [((4, tb, HEAD_DIM), BF16)],
                                temps=8 << 20, sem=("arbitrary", "arbitrary")),
    )(dproj, dog, o_saved, states, proj, proj, proj, proj, lb_table, norm_g)


def _gmlp_backward(dproj, da, proj, ln_g, ln_b, w_s, bias_b):
    t = proj.shape[1]
    tm = _tile(t, 256)
    chunks = tm // GMLP_CHUNK

    def body(_, da_ref, u_ref, v_ref, lng_ref, lnb_ref, ws_ref, bias_ref, dp_ref, dlng_ref, dlnb_ref, dws_ref, dbs_ref,
             vn_scr, dvn_scr):
        @pl.when(pl.program_id(0) == 0)
        def _():
            dlng_ref[...] = jnp.zeros_like(dlng_ref)
            dlnb_ref[...] = jnp.zeros_like(dlnb_ref)
            dws_ref[...] = jnp.zeros_like(dws_ref)
            dbs_ref[...] = jnp.zeros_like(dbs_ref)

        v = v_ref[...]
        vv, dvv_dv = _gelu_and_grad(v)
        mu = jnp.mean(vv, axis=-1, keepdims=True)
        cen = vv - mu
        rstd = lax.rsqrt(jnp.mean(cen * cen, axis=-1, keepdims=True) + NORM_EPS)
        vhat = cen * rstd
        lng = lng_ref[...]
        vn_scr[...] = (vhat * lng + lnb_ref[...]).astype(BF16)
        row = lax.broadcasted_iota(jnp.int32, (GMLP_CHUNK, GMLP_CHUNK), 0)
        col = lax.broadcasted_iota(jnp.int32, (GMLP_CHUNK, GMLP_CHUNK), 1)
        for g in range(GROUPS):
            wm = _masked_ws(ws_ref, g)
            cols = slice(g * HEAD_DIM, (g + 1) * HEAD_DIM)
            dws = jnp.zeros((GMLP_CHUNK, GMLP_CHUNK), F32)
            dbs = jnp.zeros((GMLP_CHUNK, GMLP_CHUNK), F32)
            for c in range(chunks):
                rows = slice(c * GMLP_CHUNK, (c + 1) * GMLP_CHUNK)
                vn = vn_scr[rows, cols]
                mixed = _dot(wm, vn) + bias_ref[g]
                u = u_ref[rows, cols]
                d_a = da_ref[rows, cols]
                gelu_u, dgelu_u = _gelu_and_grad(u)
                dp_ref[0, rows, cols] = (d_a * mixed * dgelu_u).astype(BF16)
                dmix = d_a * gelu_u
                dmb = dmix.astype(BF16)
                dbs = dbs + dmix
                dws = dws + _dot_nt(dmb, vn)
                dvn_scr[rows, cols] = _dot_tn(wm, dmb)
            dws_ref[g] += jnp.where(row >= col, dws, 0.0)
            dbs_ref[g] += jnp.broadcast_to(jnp.sum(dbs, axis=-1, keepdims=True), (GMLP_CHUNK, GMLP_CHUNK))
        dvn = dvn_scr[...]
        dlng_ref[...] += jnp.sum(dvn * vhat, axis=0, keepdims=True)
        dlnb_ref[...] += jnp.sum(dvn, axis=0, keepdims=True)
        dvh = dvn * lng
        dvv = rstd * (dvh - jnp.mean(dvh, axis=-1, keepdims=True) - vhat * jnp.mean(dvh * vhat, axis=-1, keepdims=True))
        dp_ref[1] = (dvv * dvv_dv).astype(BF16)

    tok = pl.BlockSpec((tm, D_MODEL), lambda m: (m, 0))
    small = pl.BlockSpec((GROUPS, GMLP_CHUNK, GMLP_CHUNK), lambda m: (0, 0, 0))
    vec = pl.BlockSpec((1, D_MODEL), lambda m: (0, 0))
    return pl.pallas_call(
        body, name="gmlp_bwd",
        out_shape=[jax.ShapeDtypeStruct(dproj.shape, BF16), jax.ShapeDtypeStruct((1, D_MODEL), F32),
                   jax.ShapeDtypeStruct((1, D_MODEL), F32), jax.ShapeDtypeStruct((GROUPS, GMLP_CHUNK, GMLP_CHUNK), F32),
                   jax.ShapeDtypeStruct((GROUPS, GMLP_CHUNK, GMLP_CHUNK), F32)],
        grid=(t // tm,),
        in_specs=[ANY, tok, pl.BlockSpec((None, tm, D_MODEL), lambda m: (U_POS, m, 0)),
                  pl.BlockSpec((None, tm, D_MODEL), lambda m: (U_POS + 1, m, 0)), vec, vec, small, small],
        out_specs=[pl.BlockSpec((2, tm, D_MODEL), lambda m: (U_POS // 2, m, 0)), vec, vec, small, small],
        scratch_shapes=[pltpu.VMEM((tm, D_MODEL), BF16), pltpu.VMEM((tm, D_MODEL), F32)],
        input_output_aliases={0: 0},
        compiler_params=_params([((tm, D_MODEL), F32)] * 3 + [((2, tm, D_MODEL), BF16)] + [((8, 128, 128), F32)] * 4,
                                scratch=[((tm, D_MODEL), BF16), ((tm, D_MODEL), F32)], temps=12 << 20, sem=("arbitrary",)),
    )(dproj, da, proj, proj, ln_g, ln_b, w_s, bias_b)


def _input_backward(dproj, w_in_g, x, dx1, mix_g):
    t = x.shape[0]
    tm = _tile(t, 512)

    def body(dp_ref, w_ref, x_ref, dx1_ref, g_ref, dx_ref, dg_ref):
        @pl.when(pl.program_id(0) == 0)
        def _():
            dg_ref[...] = jnp.zeros_like(dg_ref)

        dh = _dot_nt(dp_ref[0], w_ref[0])
        for p in range(1, N_DEV):
            dh = dh + _dot_nt(dp_ref[p], w_ref[p])
        dx, dg = _rms_bwd(dh, x_ref[...], g_ref[...])
        dx_ref[...] = dx1_ref[...] + dx
        dg_ref[...] += dg

    tok = pl.BlockSpec((tm, D_MODEL), lambda m: (m, 0))
    vec = pl.BlockSpec((1, D_MODEL), lambda m: (0, 0))
    return pl.pallas_call(
        body, name="input_bwd",
        out_shape=[jax.ShapeDtypeStruct((t, D_MODEL), F32), jax.ShapeDtypeStruct((1, D_MODEL), F32)],
        grid=(t // tm,),
        in_specs=[pl.BlockSpec((N_DEV, tm, D_MODEL), lambda m: (0, m, 0)), RESIDENT, tok, tok, vec],
        out_specs=[tok, vec],
        compiler_params=_params([((N_DEV, tm, D_MODEL), BF16)] + [((tm, D_MODEL), F32)] * 3,
                                scratch=[((N_DEV, D_MODEL, D_MODEL), BF16)], temps=6 << 20, sem=("arbitrary",)),
    )(dproj, w_in_g, x, dx1, mix_g)


def _weight_grad(name, a, b, a_spec, b_spec, out_shape, out_spec, steps, blocks, a_is_transposed):
    def body(a_ref, b_ref, o_ref):
        o_ref[...] = _dot(a_ref[...], b_ref[...]) if a_is_transposed else _dot_tn(a_ref[...], b_ref[...])

    return pl.pallas_call(
        body, name=name, out_shape=jax.ShapeDtypeStruct(out_shape, F32), grid=(steps,), in_specs=[a_spec, b_spec],
        out_specs=out_spec, compiler_params=_params(blocks, temps=4 << 20, sem=("arbitrary",)),
    )(a, b)


def _pack_small(mix_g, ln_g, ln_b, b_s, lb_table, hg_norm, ffn_g, final_g, loss_row):
    def part(a):
        a = a.reshape(-1, D_MODEL)
        return jnp.pad(a, ((0, 8 - a.shape[0]), (0, 0)))

    return jnp.concatenate([part(mix_g), part(ln_g), part(ln_b), part(hg_norm), part(ffn_g), part(final_g),
                            part(lb_table), part(b_s), part(loss_row)], axis=0)


SMALL_PARTS = (("gmlp_ln_g", 8, 1), ("gmlp_ln_b", 16, 1), ("hgrn_norm_g", 24, 1), ("norm_ffn_g", 32, 1), ("norm_final_g", 40, 1),
               ("hgrn_lb_table", 48, 2))


def _adamw_small_unpacked(gathered, w, m, v):
    rows = w.shape[0]
    n_out = len(SMALL_PARTS) + 1

    def body(p_ref, w_ref, m_ref, v_ref, *outs):
        g = p_ref[0]
        for j in range(1, N_DEV):
            g = g + p_ref[j]
        delta, m_new, v_new = _adamw_math(w_ref[...], g, m_ref[...], v_ref[...])
        for kind, val in enumerate((g, delta, m_new, v_new)):
            refs = outs[kind * n_out:(kind + 1) * n_out]
            for (_, first, count), ref in zip(SMALL_PARTS, refs):
                ref[...] = val[first:first + count]
            for grp in range(GROUPS):
                refs[-1][0, grp:grp + 1, :] = val[56:57, grp * GMLP_CHUNK:(grp + 1) * GMLP_CHUNK]
        outs[-1][...] = g[SMALL_ROWS - 8:SMALL_ROWS - 7]

    shapes = [jax.ShapeDtypeStruct((count, D_MODEL), F32) for _, _, count in SMALL_PARTS]
    shapes.append(jax.ShapeDtypeStruct((1, GROUPS, GMLP_CHUNK), F32))
    whole = pl.BlockSpec((rows, D_MODEL), lambda: (0, 0))
    res = pl.pallas_call(
        body, name="adamw_small", out_shape=shapes * 4 + [jax.ShapeDtypeStruct((1, D_MODEL), F32)],
        in_specs=[pl.BlockSpec((N_DEV, rows, D_MODEL), lambda: (0, 0, 0)), whole, whole, whole],
        compiler_params=_params([((N_DEV, rows, D_MODEL), F32)] + [((rows, D_MODEL), F32)] * 7),
    )(gathered, w, m, v)
    names = [nme for nme, _, _ in SMALL_PARTS] + ["gmlp_b_s"]
    return [dict(zip(names, res[kind * n_out:(kind + 1) * n_out])) for kind in range(4)], res[-1]


def _adamw_row(name, gathered, w, m, v):
    def body(p_ref, w_ref, m_ref, v_ref, g_out, d_out, m_out, v_out):
        g = p_ref[0, 0:1, :]
        for j in range(1, N_DEV):
            g = g + p_ref[j, 0:1, :]
        delta, m_new, v_new = _adamw_math(w_ref[...], g, m_ref[...], v_ref[...])
        g_out[...] = g
        d_out[...] = delta
        m_out[...] = m_new
        v_out[...] = v_new

    return pl.pallas_call(
        body, name=name, out_shape=[jax.ShapeDtypeStruct((1, D_MODEL), F32)] * 4,
        compiler_params=_params([((N_DEV, 8, D_MODEL), F32)] + [((8, D_MODEL), F32)] * 7),
    )(gathered, w, m, v)


def _adamw_small(name, gathered, w, m, v):
    rows, cols = w.shape

    def body(p_ref, w_ref, m_ref, v_ref, g_out, d_out, m_out, v_out):
        g = p_ref[0]
        for j in range(1, N_DEV):
            g = g + p_ref[j]
        delta, m_new, v_new = _adamw_math(w_ref[...], g, m_ref[...], v_ref[...])
        g_out[...] = g
        d_out[...] = delta
        m_out[...] = m_new
        v_out[...] = v_new

    tr = _tile(rows, 512)
    spec = pl.BlockSpec((tr, cols), lambda r: (r, 0))
    return pl.pallas_call(
        body, name=name, out_shape=[jax.ShapeDtypeStruct((rows, cols), F32)] * 4, grid=(rows // tr,),
        in_specs=[pl.BlockSpec((N_DEV, tr, cols), lambda r: (0, r, 0)), spec, spec, spec], out_specs=[spec] * 4,
        compiler_params=_params([((N_DEV, tr, cols), F32)] + [((tr, cols), F32)] * 7, sem=("arbitrary",)),
    )(gathered, w, m, v)


def kernel(x, norm_mix_g, w_in, gmlp_ln_g, gmlp_ln_b, gmlp_w_s, gmlp_b_s, hgrn_lb_table, hgrn_norm_g, w_branch_a, w_branch_b, w_out, norm_ffn_g, w_gate_up, w_down, norm_final_g, loss_target, m_norm_mix_g, m_w_in, m_gmlp_ln_g, m_gmlp_ln_b, m_gmlp_w_s, m_gmlp_b_s, m_hgrn_lb_table, m_hgrn_norm_g, m_w_branch_a, m_w_branch_b, m_w_out, m_norm_ffn_g, m_w_gate_up, m_w_down, m_norm_final_g, v_norm_mix_g, v_w_in, v_gmlp_ln_g, v_gmlp_ln_b, v_gmlp_w_s, v_gmlp_b_s, v_hgrn_lb_table, v_hgrn_norm_g, v_w_branch_a, v_w_branch_b, v_w_out, v_norm_ffn_g, v_w_gate_up, v_w_down, v_norm_final_g):
    t = x.shape[1]
    x2d = x.reshape(t, D_MODEL)
    target = loss_target.reshape(t, D_MODEL)
    final_g = norm_final_g.reshape(1, D_MODEL)

    w_in_shard = w_in[0].astype(BF16)

    def rows_of(n):
        return lambda ref, j: ref.at[pl.ds(pl.multiple_of(j * n, 8), n)]

    gathered = [((N_DEV, D_MODEL, D_MODEL), BF16), ((D_MODEL, D_MODEL), BF16), ((D_MODEL, D_MODEL), BF16),
                ((D_MODEL, D_MODEL), BF16), ((N_DEV, FF_BLOCK, D_MODEL), BF16), ((D_FF, D_MODEL), BF16)]
    places = [lambda ref, j: ref.at[_pos_of_dev(j)], rows_of(BRANCH_ROWS), rows_of(BRANCH_ROWS), rows_of(BRANCH_ROWS),
              lambda ref, j: ref.at[j], rows_of(DOWN_ROWS)]
    w_in_g = _all_gather_balanced_async("w_in_all_gather", 9, w_in_shard, gathered[0], places[0], D_MODEL)
    w_in_sibling = _swap_with_sibling("w_in_from_sibling", w_in_shard)

    core_i, chip_i = lax.axis_index("c"), 2 * lax.axis_index("x") + lax.axis_index("y")
    own_pos = jnp.stack([_pos_of_dev(2 * chip_i + core_i), _pos_of_dev(2 * chip_i + 1 - core_i)]).astype(jnp.int32)
    other_pos = jnp.stack([_pos_of_dev(2 * jnp.bitwise_xor(chip_i, q) + cc) for q in (1, 2, 3) for cc in (0, 1)]).astype(jnp.int32)
    proj, h, h_t = _proj_forward_own_chip(own_pos, x2d, norm_mix_g, w_in_shard, w_in_sibling)

    small_w = [norm_mix_g, gmlp_ln_g, gmlp_ln_b, gmlp_b_s, hgrn_lb_table, hgrn_norm_g, norm_ffn_g, norm_final_g]
    small_m = [m_norm_mix_g, m_gmlp_ln_g, m_gmlp_ln_b, m_gmlp_b_s, m_hgrn_lb_table, m_hgrn_norm_g, m_norm_ffn_g, m_norm_final_g]
    small_v = [v_norm_mix_g, v_gmlp_ln_g, v_gmlp_ln_b, v_gmlp_b_s, v_hgrn_lb_table, v_hgrn_norm_g, v_norm_ffn_g, v_norm_final_g]
    _, (raw, small_w, small_m, small_v) = lax.optimization_barrier(
        (h, ([w_branch_a[0], w_branch_b[0], w_out[0], w_gate_up[0], w_down[0]], small_w, small_m, small_v)))
    later = [raw[0].astype(BF16), raw[1].astype(BF16), raw[2].astype(BF16),
             raw[3].T.astype(BF16), raw[4].astype(BF16)]
    w_a, w_b, w_o, w_gu, w_dn = _all_gather_async("weights_all_gather", 0, later, gathered[1:], places[1:])
    no_row = jnp.zeros((1, D_MODEL), F32)
    w_pack, m_pack, v_pack = (_pack_small(*vals, no_row) for vals in (small_w, small_m, small_v))
    bias_b = jnp.broadcast_to(small_w[3][0][:, :, None], (GROUPS, GMLP_CHUNK, GMLP_CHUNK))
    h_later, _ = lax.optimization_barrier((h, (later, w_pack, m_pack, v_pack, bias_b)))
    proj = _proj_forward_other_chips(other_pos, proj, h_later, w_in_g)
    a = _gmlp_forward(proj, gmlp_ln_g, gmlp_ln_b, gmlp_w_s[0], bias_b)
    og, o_saved, states = _hgrn_forward(proj, hgrn_lb_table, hgrn_norm_g)
    ya, yb, merged, x1, h2 = _branch_out_forward(a, og, proj, x2d, w_a, w_b, w_o, norm_ffn_g)
    gu, act, loss_tile, d_final_g, dx2, dx2b = _ffn_forward(h2, x1, w_gu, w_dn, target, final_g)

    core = lax.axis_index("c").astype(jnp.int32).reshape(1)
    chip = (2 * lax.axis_index("x") + lax.axis_index("y")).astype(jnp.int32).reshape(1)
    branch_rows, branch_shape = rows_of(BRANCH_ROWS), (BRANCH_ROWS, D_MODEL)
    branch_block = ((BRANCH_ROWS, D_MODEL), lambda q, r, c: (2 * q + c, 0))

    def chip_partials(names, grads, land, own_blocks):
        return [_chip_partial("chip_partial_" + nme, core, g_, blk, idx, l_)
                for nme, g_, (blk, idx), l_ in zip(names, grads, own_blocks, land)]

    whole = pl.BlockSpec((t, D_MODEL), lambda n: (0, 0))
    whole_t = pl.BlockSpec((D_MODEL, t), lambda n: (0, 0))
    col_blocks = [((t, D_MODEL), BF16), ((t, 256), BF16), ((D_MODEL, 256), F32)]

    def square_grad(name, a_, b_):
        return _weight_grad(name, a_, b_, whole, pl.BlockSpec((t, 256), lambda n: (0, n)), (D_MODEL, D_MODEL),
                            pl.BlockSpec((D_MODEL, 256), lambda n: (0, n)), D_MODEL // 256, col_blocks, False)

    dgu, dx1, dx1b, d_ffn_g = _ffn_backward(dx2b, dx2, gu, x1, w_gu, w_dn, norm_ffn_g)
    g_gu = _weight_grad(
        "grad_w_gate_up", dgu, h2, pl.BlockSpec((None, None, t, FF_BLOCK), lambda j: (j % 4, j // 4, 0, 0)), whole,
        (N_DEV, FF_BLOCK, D_MODEL), pl.BlockSpec((None, FF_BLOCK, D_MODEL), lambda j: (j, 0, 0)), N_DEV,
        [((t, 768), BF16), ((t, D_MODEL), BF16), ((FF_BLOCK, D_MODEL), F32)], False)
    g_dn = _weight_grad(
        "grad_w_down", act, dx2b, pl.BlockSpec((None, t, FF_BLOCK), lambda j: (j, 0, 0)), whole, (D_FF, D_MODEL),
        pl.BlockSpec((FF_BLOCK, D_MODEL), lambda j: (j, 0)), 4,
        [((t, 768), BF16), ((t, D_MODEL), BF16), ((FF_BLOCK, D_MODEL), F32)], False)
    names_f, grads_f = ["w_gate_up", "w_down"], [g_gu, g_dn]
    land_f = _exchange_sibling("ffn_grads_to_sibling", 2, grads_f, [lambda ref, j: ref.at[j], rows_of(DOWN_ROWS)],
                               [(FF_BLOCK, D_MODEL), (DOWN_ROWS, D_MODEL)])

    dx1b_later, _ = lax.optimization_barrier((dx1b, grads_f))
    dya, dyb, dproj, da, dog = _branch_out_backward(dx1b_later, ya, yb, proj, w_a, w_b, w_o)
    g_a = square_grad("grad_w_a", a, dya)
    g_b = square_grad("grad_w_b", og, dyb)
    g_o = square_grad("grad_w_out", merged, dx1b)
    names_b, grads_b = ["w_branch_a", "w_branch_b", "w_out"], [g_a, g_b, g_o]
    land_b = _exchange_sibling("branch_grads_to_sibling", 3, grads_b, [branch_rows] * 3, [branch_shape] * 3)

    part_f = chip_partials(names_f, grads_f, land_f,
                           [((None, FF_BLOCK, D_MODEL), lambda q, r, c: (2 * q + c, 0, 0)),
                            ((DOWN_ROWS, D_MODEL), lambda q, r, c: (2 * q + c, 0))])
    landed_f = _exchange_chips("ffn_grads_to_chips", 5, part_f)

    dog, _ = lax.optimization_barrier((dog, part_f))
    dproj, d_hg_norm, d_lb = _hgrn_backward(dproj, dog, o_saved, states, proj, hgrn_lb_table, hgrn_norm_g)

    land_b, _ = lax.optimization_barrier((land_b, part_f))
    part_b = chip_partials(names_b, grads_b, land_b, [branch_block] * 3)
    landed_b = _exchange_chips("branch_grads_to_chips", 6, part_b)

    da, _ = lax.optimization_barrier((da, part_b))
    dproj, d_ln_g, d_ln_b, d_ws, d_bs = _gmlp_backward(dproj, da, proj, gmlp_ln_g, gmlp_ln_b, gmlp_w_s[0], bias_b)

    def flat_ws(a):
        return a.reshape(GROUPS * GMLP_CHUNK, GMLP_CHUNK)

    small_partial = _pack_small(no_row, d_ln_g, d_ln_b, d_bs[:, :, 0], d_lb, d_hg_norm, d_ffn_g, d_final_g,
                                jnp.tile(loss_tile[0:1], (1, D_MODEL // 128)))
    small_all, ws_all = _all_gather_async(
        "small_grads_all_gather", 1, [small_partial, flat_ws(d_ws)],
        [((N_DEV, SMALL_ROWS, D_MODEL), F32), ((N_DEV, GROUPS * GMLP_CHUNK, GMLP_CHUNK), F32)],
        [lambda ref, j: ref.at[j], lambda ref, j: ref.at[j]])

    g_in = _weight_grad(
        "grad_w_in", h_t, dproj, whole_t, pl.BlockSpec((None, t, D_MODEL // 2), lambda n: (n // 2, 0, n % 2)),
        (N_DEV, D_MODEL, D_MODEL), pl.BlockSpec((None, D_MODEL, D_MODEL // 2), lambda n: (n // 2, 0, n % 2)), 2 * N_DEV,
        [((D_MODEL, t), BF16), ((t, D_MODEL // 2), BF16), ((D_MODEL, D_MODEL // 2), F32)], True)
    land_i = _exchange_sibling("w_in_grads_to_sibling", 4, [g_in], [lambda ref, j: ref.at[_pos_of_dev(j)]],
                               [(D_MODEL, D_MODEL)])

    (landed_f, landed_b), _ = lax.optimization_barrier(((landed_f, landed_b), g_in))
    big = {}
    for nme, own, lnd, w, m, v in zip(
            names_f + names_b, part_f + part_b, landed_f + landed_b,
            [w_gate_up, w_down, w_branch_a, w_branch_b, w_out], [m_w_gate_up, m_w_down, m_w_branch_a, m_w_branch_b, m_w_out],
            [v_w_gate_up, v_w_down, v_w_branch_a, v_w_branch_b, v_w_out]):
        flip = (lambda z: z.T) if nme == "w_gate_up" else (lambda z: z)
        big[nme] = [flip(o_)[None] for o_ in _adamw("adamw_" + nme, chip, own, lnd, flip(w[0]), flip(m[0]), flip(v[0]))]
    small, loss_row = _adamw_small_unpacked(small_all, w_pack, m_pack, v_pack)
    ws_outs = _adamw_small("adamw_w_s", ws_all, flat_ws(gmlp_w_s), flat_ws(m_gmlp_w_s), flat_ws(v_gmlp_w_s))
    land_i, _ = lax.optimization_barrier((land_i, (big, small, ws_outs)))
    part_i = chip_partials(["w_in"], [g_in], land_i,
                           [((None, D_MODEL, D_MODEL), lambda q, r, c: (_pos_of_dev(2 * q + c), 0, 0))])
    landed_i = _exchange_chips("w_in_grads_to_chips", 7, part_i)

    dx1, _ = lax.optimization_barrier((dx1, part_i))
    grad_x, d_mix_g = _input_backward(dproj, w_in_g, x2d, dx1, norm_mix_g)
    big["w_in"] = [o_[None] for o_ in _adamw("adamw_w_in", chip, part_i[0], landed_i[0], w_in[0], m_w_in[0], v_w_in[0])]

    def row8(a):
        return jnp.pad(a, ((0, 7), (0, 0)))

    d_mix_g, _ = lax.optimization_barrier((d_mix_g, landed_i))
    (mix_all,) = _all_gather_async("mix_gain_grad_all_gather", 8, [row8(d_mix_g)], [((N_DEV, 8, D_MODEL), F32)],
                                   [lambda ref, j: ref.at[j]])
    mix_outs = _adamw_row("adamw_mix_gain", mix_all, norm_mix_g, m_norm_mix_g, v_norm_mix_g)
    small = [dict(p, norm_final_g=p["norm_final_g"][0], gmlp_w_s=ws.reshape(1, GROUPS, GMLP_CHUNK, GMLP_CHUNK), norm_mix_g=q)
             for p, ws, q in zip(small, ws_outs, mix_outs)]

    loss = loss_row[0, 0]
    order = ["norm_mix_g", "w_in", "gmlp_ln_g", "gmlp_ln_b", "gmlp_w_s", "gmlp_b_s", "hgrn_lb_table", "hgrn_norm_g",
             "w_branch_a", "w_branch_b", "w_out", "norm_ffn_g", "w_gate_up", "w_down", "norm_final_g"]
    outs = [loss, grad_x.reshape(1, t, D_MODEL)]
    for kind in range(4):
        for nme in order:
            outs.append(big[nme][kind] if nme in big else small[kind][nme])
    return tuple(outs)
```

```python
import jax
import jax.numpy as jnp
from jax import lax
from jax.experimental import pallas as pl
from jax.experimental.pallas import tpu as pltpu
from jax.experimental.pallas import tpu_sc as plsc

F32, BF16 = jnp.float32, jnp.bfloat16
D_MODEL = 1024
N_DEV = 8
HEADS = 8
HEAD_DIM = 128
GROUPS = 8
GMLP_CHUNK = 128
HGRN_CHUNK = 64
HGRN_SCALE = HEAD_DIM ** -0.5
D_FF = 2816
FF_BLOCK = D_FF // 4
DOWN_ROWS = D_FF // N_DEV
BRANCH_ROWS = D_MODEL // N_DEV
NORM_EPS = 1e-6
ADAM_LR, ADAM_B1, ADAM_B2, ADAM_EPS, ADAM_WD, ADAM_STEP = 0.001, 0.9, 0.999, 1e-08, 0.01, 10
SMALL_ROWS = 72
V7X_VMEM_BYTES = 64 * 1024 * 1024
VMEM_CAP = V7X_VMEM_BYTES - 6 * 1024 * 1024
MESH_ID = pl.DeviceIdType.MESH
ANY = pl.BlockSpec(memory_space=pl.ANY)
RESIDENT = pl.BlockSpec(memory_space=pltpu.VMEM)
Q_POS, U_POS, GATE_POS = 0, 4, 6


def _pos_of_dev(j):
    return jnp.where(j < 2, j + 4, jnp.where(j < 6, j - 2, j))


def _nbytes(shape, dtype):
    n = 1
    for s in shape:
        n *= s
    return n * jnp.dtype(dtype).itemsize


def _params(blocks, scratch=(), temps=0, sem=None):
    need = 2 * sum(_nbytes(s, d) for s, d in blocks) + sum(_nbytes(s, d) for s, d in scratch) + temps
    assert need + (4 << 20) <= VMEM_CAP, need
    return pltpu.CompilerParams(dimension_semantics=sem, vmem_limit_bytes=VMEM_CAP)


def _tile(n, pref):
    return pref if n % pref == 0 else n


def _dot(a, b):
    return jnp.dot(a, b, preferred_element_type=F32)


def _dot_nt(a, b):
    return lax.dot_general(a, b, (((1,), (1,)), ((), ())), preferred_element_type=F32)


def _dot_tn(a, b):
    return lax.dot_general(a, b, (((0,), (0,)), ((), ())), preferred_element_type=F32)


def _sigmoid(x):
    return 1.0 / (1.0 + jnp.exp(-x))


_GELU_C = 0.7978845608028654


def _gelu(x):
    return x * (0.5 * (1.0 + jnp.tanh(_GELU_C * (x + 0.044715 * (x * x * x)))))


def _gelu_and_grad(x):
    t = jnp.tanh(_GELU_C * (x + 0.044715 * (x * x * x)))
    half = 0.5 * (1.0 + t)
    return x * half, half + 0.5 * x * (1.0 - t * t) * (_GELU_C * (1.0 + 3.0 * 0.044715 * x * x))


def _rms_stats(x):
    r = lax.rsqrt(jnp.mean(x * x, axis=-1, keepdims=True) + NORM_EPS)
    return r, x * r


def _rms_bwd(dy, x, g):
    r, xh = _rms_stats(x)
    dg = jnp.sum(dy * xh, axis=0, keepdims=True)
    dxh = dy * g
    dx = r * (dxh - xh * jnp.mean(dxh * xh, axis=-1, keepdims=True))
    return dx, dg


def _split3(x):
    hi = x.astype(BF16)
    r = x - hi.astype(F32)
    mid = r.astype(BF16)
    lo = (r - mid.astype(F32)).astype(BF16)
    return hi, mid, lo


def _mask_mm(mask_bf16, x):
    hi, mid, lo = _split3(x)
    return _dot(mask_bf16, hi) + _dot(mask_bf16, mid) + _dot(mask_bf16, lo)


def _place():
    return lax.axis_index("x"), lax.axis_index("y"), lax.axis_index("c")


def _gather_copies(src, out, send, recv, loc, slicers):
    n = len(src)
    x, y, c = _place()
    me, sib = (x, y, c), (x, y, 1 - c)
    chips = [(1 - x, y), (x, 1 - y), (1 - x, 1 - y)]

    def dev(p):
        return 4 * p[0] + 2 * p[1] + p[2]

    def rc(i, k, block, to, from_src=False):
        dst = slicers[i](out[i], dev(block))
        return pltpu.make_async_remote_copy(
            src_ref=src[i] if from_src else dst, dst_ref=dst, send_sem=send.at[7 * i + k],
            recv_sem=recv.at[7 * i + k], device_id=to, device_id_type=MESH_ID)

    mine = [pltpu.make_async_copy(src[i], slicers[i](out[i], dev(me)), loc.at[i]) for i in range(n)]
    for cp in mine:
        cp.start()
    first = []
    for i in range(n):
        first.append(rc(i, 0, me, sib, True))
        for j, chip in enumerate(chips):
            first.append(rc(i, 1 + j, me, (*chip, c), True))
    for cp in first:
        cp.start()
    passed = []
    for j, chip in enumerate(chips):
        for i in range(n):
            rc(i, 1 + j, (*chip, c), me).wait_recv()
            cp = rc(i, 4 + j, (*chip, c), sib)
            cp.start()
            passed.append(cp)
    for i in range(n):
        rc(i, 0, sib, me).wait_recv()
        for j, chip in enumerate(chips):
            rc(i, 4 + j, (*chip, 1 - c), me).wait_recv()
    for cp in first + passed:
        cp.wait_send()
    for cp in mine:
        cp.wait()


def _gather_copies_balanced(src, out, send, recv, loc, slicer, rows):
    x, y, c = _place()
    me, sib = (x, y, c), (x, y, 1 - c)
    xn, yn, dg = (1 - x, y), (x, 1 - y), (1 - x, 1 - y)
    half_rows = rows // 2

    def block(p):
        return slicer(out, 4 * p[0] + 2 * p[1] + p[2])

    def half(ref, h):
        return ref.at[pl.ds(h * half_rows, half_rows)]

    def rc(k, dst, to, from_src=False):
        return pltpu.make_async_remote_copy(src_ref=src if from_src else dst, dst_ref=dst, send_sem=send.at[k],
                                            recv_sem=recv.at[k], device_id=to, device_id_type=MESH_ID)

    mine = pltpu.make_async_copy(src, block(me), loc.at[0])
    mine.start()
    sends = [rc(0, block(me), sib, True), rc(1, block(me), (*xn, c), True), rc(2, block(me), (*yn, c), True)]
    for cp in sends:
        cp.start()

    def then(cp):
        cp.start()
        sends.append(cp)

    rc(1, block((*xn, c)), me).wait_recv()
    then(rc(3, half(block((*xn, c)), 0), (*yn, c)))
    then(rc(5, block((*xn, c)), sib))
    rc(2, block((*yn, c)), me).wait_recv()
    then(rc(4, half(block((*yn, c)), 1), (*xn, c)))
    then(rc(6, block((*yn, c)), sib))
    rc(3, half(block((*dg, c)), 0), me).wait_recv()
    then(rc(7, half(block((*dg, c)), 0), sib))
    rc(4, half(block((*dg, c)), 1), me).wait_recv()
    then(rc(8, half(block((*dg, c)), 1), sib))
    rc(0, block(sib), me).wait_recv()
    rc(5, block((*xn, 1 - c)), me).wait_recv()
    rc(6, block((*yn, 1 - c)), me).wait_recv()
    rc(7, half(block((*dg, 1 - c)), 0), me).wait_recv()
    rc(8, half(block((*dg, 1 - c)), 1), me).wait_recv()
    for cp in sends:
        cp.wait_send()
    mine.wait()


def _gather_scratch(n):
    return [pltpu.SemaphoreType.DMA((7 * n,)), pltpu.SemaphoreType.DMA((7 * n,)), pltpu.SemaphoreType.DMA((n,))]


def _handshake(peers):
    barrier = pltpu.get_barrier_semaphore()
    for peer in peers:
        pl.semaphore_signal(barrier, inc=1, device_id=peer, device_id_type=MESH_ID)
    pl.semaphore_wait(barrier, len(peers))


def _all_gather_async(name, collective_id, srcs, out_shapes, slicers):
    n = len(srcs)

    def body(*refs):
        x, y, c = _place()
        _handshake([(1 - x if dx else x, 1 - y if dy else y, 1 - c if dc else c)
                    for dx in (0, 1) for dy in (0, 1) for dc in (0, 1) if dx or dy or dc])
        _gather_copies(refs[:n], refs[n:2 * n], *refs[2 * n:], slicers)

    return _sequencer_call(name, collective_id, body, srcs, [jax.ShapeDtypeStruct(s, d) for s, d in out_shapes],
                           _gather_scratch(n))


def _all_gather_balanced_async(name, collective_id, src, out_shape, slicer, rows):
    def body(src_ref, out_ref, send, recv, loc):
        x, y, c = _place()
        _handshake([(1 - x if dx else x, 1 - y if dy else y, 1 - c if dc else c)
                    for dx in (0, 1) for dy in (0, 1) for dc in (0, 1) if dx or dy or dc])
        _gather_copies_balanced(src_ref, out_ref, send, recv, loc, slicer, rows)

    return _sequencer_call(name, collective_id, body, [src], [jax.ShapeDtypeStruct(*out_shape)],
                           [pltpu.SemaphoreType.DMA((9,)), pltpu.SemaphoreType.DMA((9,)), pltpu.SemaphoreType.DMA((1,))])[0]


def _sequencer_call(name, collective_id, body, operands, out_types, scratch):
    return pl.kernel(
        body, out_type=out_types, mesh=plsc.ScalarSubcoreMesh(axis_name="sequencer", num_cores=1), name=name,
        scratch_types=scratch, compiler_params=pltpu.CompilerParams(collective_id=collective_id),
    )(*operands)


def _exchange_sibling(name, collective_id, grads, shard_fns, shard_shapes):
    n = len(grads)

    def body(*refs):
        g, land = refs[:n], refs[n:2 * n]
        send, recv = refs[2 * n:]
        x, y, c = _place()
        _handshake([(x, y, 1 - c)])
        remote = []
        for i in range(n):
            for q in range(4):
                cp = pltpu.make_async_remote_copy(
                    src_ref=shard_fns[i](g[i], 2 * q + (1 - c)), dst_ref=land[i].at[q], send_sem=send.at[4 * i + q],
                    recv_sem=recv.at[4 * i + q], device_id=(x, y, 1 - c), device_id_type=MESH_ID)
                cp.start()
                remote.append(cp)
        for cp in remote:
            cp.wait()

    return _sequencer_call(name, collective_id, body, grads, [jax.ShapeDtypeStruct((4, *s), F32) for s in shard_shapes],
                           [pltpu.SemaphoreType.DMA((4 * n,)), pltpu.SemaphoreType.DMA((4 * n,))])


def _exchange_chips(name, collective_id, parts):
    n = len(parts)

    def body(*refs):
        part, out = refs[:n], refs[n:2 * n]
        send, recv = refs[2 * n:]
        x, y, c = _place()
        _handshake([(1 - x, y, c), (x, 1 - y, c), (1 - x, 1 - y, c)])
        remote = []
        for i in range(n):
            for s in range(3):
                qx = 1 - x if (s + 1) // 2 else x
                qy = 1 - y if (s + 1) % 2 else y
                cp = pltpu.make_async_remote_copy(
                    src_ref=part[i].at[2 * qx + qy], dst_ref=out[i].at[s], send_sem=send.at[3 * i + s],
                    recv_sem=recv.at[3 * i + s], device_id=(qx, qy, c), device_id_type=MESH_ID)
                cp.start()
                remote.append(cp)
        for cp in remote:
            cp.wait()

    return _sequencer_call(name, collective_id, body, parts,
                           [jax.ShapeDtypeStruct((3, *p.shape[1:]), p.dtype) for p in parts],
                           [pltpu.SemaphoreType.DMA((3 * n,)), pltpu.SemaphoreType.DMA((3 * n,))])


def _chip_partial(name, core, grad, own_block, own_index, land):
    _, rows, cols = land.shape
    tr = own_block[-2]

    def body(core_ref, a_ref, b_ref, o_ref):
        o_ref[...] = (a_ref[...] + b_ref[...]).astype(BF16)

    spec = pl.BlockSpec((None, tr, cols), lambda q, r, c: (q, r, 0))
    return pl.pallas_call(
        body, name=name, out_shape=jax.ShapeDtypeStruct(land.shape, BF16),
        grid_spec=pltpu.PrefetchScalarGridSpec(
            num_scalar_prefetch=1, grid=(4, rows // tr),
            in_specs=[pl.BlockSpec(own_block, lambda q, r, c: own_index(q, r, c[0])), spec], out_specs=spec),
        compiler_params=_params([((tr, cols), F32)] * 2 + [((tr, cols), BF16)], sem=("arbitrary", "arbitrary")),
    )(core, grad, land)


def _adamw_math(w, g, m, v):
    m = ADAM_B1 * m + (1.0 - ADAM_B1) * g
    v = ADAM_B2 * v + (1.0 - ADAM_B2) * (g * g)
    m_hat = m / (1.0 - ADAM_B1 ** ADAM_STEP)
    v_hat = v / (1.0 - ADAM_B2 ** ADAM_STEP)
    delta = -ADAM_LR * (m_hat / (jnp.sqrt(v_hat) + ADAM_EPS) + ADAM_WD * w)
    return delta, m, v


def _adamw(name, chip, own, landed, w, m, v):
    _, rows, cols = own.shape
    tr = _tile(rows, 512) if rows % 512 == 0 else _tile(rows, 176)

    def body(chip_ref, own_ref, l_ref, w_ref, m_ref, v_ref, g_out, d_out, m_out, v_out):
        g = own_ref[...].astype(F32)
        for s in range(3):
            g = g + l_ref[s].astype(F32)
        delta, m_new, v_new = _adamw_math(w_ref[...], g, m_ref[...], v_ref[...])
        g_out[...] = g
        d_out[...] = delta
        m_out[...] = m_new
        v_out[...] = v_new

    spec = pl.BlockSpec((tr, cols), lambda r, c: (r, 0))
    return pl.pallas_call(
        body, name=name, out_shape=[jax.ShapeDtypeStruct((rows, cols), F32)] * 4,
        grid_spec=pltpu.PrefetchScalarGridSpec(
            num_scalar_prefetch=1, grid=(rows // tr,),
            in_specs=[pl.BlockSpec((None, tr, cols), lambda r, c: (c[0], r, 0)),
                      pl.BlockSpec((3, tr, cols), lambda r, c: (0, r, 0)), spec, spec, spec],
            out_specs=[spec] * 4),
        compiler_params=_params([((4, tr, cols), own.dtype)] + [((tr, cols), F32)] * 7, sem=("arbitrary",)),
    )(chip, own, landed, w, m, v)


def _swap_with_sibling(name, x):
    def body(x_ref, o_ref, send, recv):
        px, py, c = _place()
        cp = pltpu.make_async_remote_copy(src_ref=x_ref, dst_ref=o_ref, send_sem=send, recv_sem=recv,
                                          device_id=(px, py, 1 - c), device_id_type=MESH_ID)
        cp.start()
        cp.wait()

    return pl.pallas_call(
        body, name=name, out_shape=jax.ShapeDtypeStruct(x.shape, x.dtype), in_specs=[ANY], out_specs=ANY,
        scratch_shapes=[pltpu.SemaphoreType.DMA, pltpu.SemaphoreType.DMA],
    )(x)


def _proj_forward_own_chip(positions, x, gain, w_own, w_sibling):
    t = x.shape[0]
    tm = _tile(t, 1024)

    def body(pos_ref, x_ref, g_ref, wo_ref, ws_ref, o_ref, h_ref, ht_ref):
        @pl.when(pl.program_id(1) == 0)
        def _():
            _, xh = _rms_stats(x_ref[...])
            h = (xh * g_ref[...]).astype(BF16)
            h_ref[...] = h
            ht_ref[...] = h.T
            o_ref[...] = _dot(h, wo_ref[...])

        @pl.when(pl.program_id(1) == 1)
        def _():
            o_ref[...] = _dot(h_ref[...], ws_ref[...])

    tok = pl.BlockSpec((tm, D_MODEL), lambda m, k, pos: (m, 0))
    return pl.pallas_call(
        body, name="proj_fwd_own_chip",
        out_shape=[jax.ShapeDtypeStruct((N_DEV, t, D_MODEL), F32), jax.ShapeDtypeStruct((t, D_MODEL), BF16),
                   jax.ShapeDtypeStruct((D_MODEL, t), BF16)],
        grid_spec=pltpu.PrefetchScalarGridSpec(
            num_scalar_prefetch=1, grid=(t // tm, 2),
            in_specs=[tok, pl.BlockSpec((1, D_MODEL), lambda m, k, pos: (0, 0)), RESIDENT, RESIDENT],
            out_specs=[pl.BlockSpec((None, tm, D_MODEL), lambda m, k, pos: (pos[k], m, 0)), tok,
                       pl.BlockSpec((D_MODEL, tm), lambda m, k, pos: (0, m))]),
        compiler_params=_params([((tm, D_MODEL), F32)] * 2 + [((tm, D_MODEL), BF16)] * 2,
                                scratch=[((2, D_MODEL, D_MODEL), BF16)], temps=6 << 20, sem=("arbitrary", "arbitrary")),
    )(positions, x, gain, w_own, w_sibling)


def _proj_forward_other_chips(positions, proj, h, w_in_g):
    t = h.shape[0]
    tm = _tile(t, 2048)

    def body(pos_ref, _, h_ref, w_ref, o_ref):
        o_ref[...] = _dot(h_ref[...], w_ref[pos_ref[pl.program_id(1)]])

    return pl.pallas_call(
        body, name="proj_fwd_other_chips", out_shape=jax.ShapeDtypeStruct(proj.shape, F32),
        grid_spec=pltpu.PrefetchScalarGridSpec(
            num_scalar_prefetch=1, grid=(t // tm, N_DEV - 2),
            in_specs=[ANY, pl.BlockSpec((tm, D_MODEL), lambda m, k, pos: (m, 0)), RESIDENT],
            out_specs=pl.BlockSpec((None, tm, D_MODEL), lambda m, k, pos: (pos[k], m, 0))),
        input_output_aliases={1: 0},
        compiler_params=_params([((tm, D_MODEL), F32), ((tm, D_MODEL), BF16)], scratch=[((N_DEV, D_MODEL, D_MODEL), BF16)],
                                temps=6 << 20, sem=("arbitrary", "arbitrary")),
    )(positions, proj, h, w_in_g)


def _masked_ws(ws_ref, g):
    row = lax.broadcasted_iota(jnp.int32, (GMLP_CHUNK, GMLP_CHUNK), 0)
    col = lax.broadcasted_iota(jnp.int32, (GMLP_CHUNK, GMLP_CHUNK), 1)
    return jnp.where(row >= col, ws_ref[g], 0.0).astype(BF16)


def _gmlp_forward(proj, ln_g, ln_b, w_s, bias_b):
    t = proj.shape[1]
    tm = _tile(t, 512)
    chunks = tm // GMLP_CHUNK

    def body(u_ref, v_ref, lng_ref, lnb_ref, ws_ref, bias_ref, a_ref, vn_scr):
        vv = _gelu(v_ref[...])
        mu = jnp.mean(vv, axis=-1, keepdims=True)
        cen = vv - mu
        var = jnp.mean(cen * cen, axis=-1, keepdims=True)
        vn_scr[...] = ((cen * lax.rsqrt(var + NORM_EPS)) * lng_ref[...] + lnb_ref[...]).astype(BF16)
        for g in range(GROUPS):
            wm = _masked_ws(ws_ref, g)
            cols = slice(g * HEAD_DIM, (g + 1) * HEAD_DIM)
            for c in range(chunks):
                rows = slice(c * GMLP_CHUNK, (c + 1) * GMLP_CHUNK)
                mixed = _dot(wm, vn_scr[rows, cols]) + bias_ref[g]
                a_ref[rows, cols] = (_gelu(u_ref[rows, cols]) * mixed).astype(BF16)

    small = pl.BlockSpec((GROUPS, GMLP_CHUNK, GMLP_CHUNK), lambda m: (0, 0, 0))
    vec = pl.BlockSpec((1, D_MODEL), lambda m: (0, 0))
    return pl.pallas_call(
        body, name="gmlp_fwd", out_shape=jax.ShapeDtypeStruct((t, D_MODEL), BF16), grid=(t // tm,),
        in_specs=[pl.BlockSpec((None, tm, D_MODEL), lambda m: (U_POS, m, 0)),
                  pl.BlockSpec((None, tm, D_MODEL), lambda m: (U_POS + 1, m, 0)), vec, vec, small, small],
        out_specs=pl.BlockSpec((tm, D_MODEL), lambda m: (m, 0)),
        scratch_shapes=[pltpu.VMEM((tm, D_MODEL), BF16)],
        compiler_params=_params([((tm, D_MODEL), F32)] * 2 + [((tm, D_MODEL), BF16)] + [((8, 128, 128), F32)] * 2,
                                scratch=[((tm, D_MODEL), BF16)], temps=8 << 20, sem=("arbitrary",)),
    )(proj, proj, ln_g, ln_b, w_s, bias_b)


def _lower_bound(tab_ref):
    t0, t1 = tab_ref[0:1, :], tab_ref[1:2, :]
    mx = jnp.maximum(t0, t1)
    e0, e1 = jnp.exp(t0 - mx), jnp.exp(t1 - mx)
    return e0 / (e0 + e1)


def _tri_masks():
    row = lax.broadcasted_iota(jnp.int32, (HGRN_CHUNK, HGRN_CHUNK), 0)
    col = lax.broadcasted_iota(jnp.int32, (HGRN_CHUNK, HGRN_CHUNK), 1)
    return row >= col, row <= col


def _chunk_rows(c):
    return slice(c * HGRN_CHUNK, (c + 1) * HGRN_CHUNK)


def _per_chunk(x, nc, fn):
    return jnp.concatenate([fn(x[_chunk_rows(c)]) for c in range(nc)], axis=0)


def _chunk_row_bcast(x, nc, i):
    return _per_chunk(x, nc, lambda xc: jnp.broadcast_to(xc[i:i + 1, :], (HGRN_CHUNK, HEAD_DIM)))


def _hgrn_gates(q, fl, lb, nc):
    lower, _ = _tri_masks()
    lower = lower.astype(BF16)
    s = _sigmoid(fl)
    f = lb + (1.0 - lb) * s
    k = 1.0 - f
    hi, mid, lo = _split3(jnp.log(f))
    a = jnp.concatenate([_dot(lower, hi[_chunk_rows(c)]) + _dot(lower, mid[_chunk_rows(c)]) + _dot(lower, lo[_chunk_rows(c)])
                         for c in range(nc)], axis=0)
    a_mid = _chunk_row_bcast(a, nc, HGRN_CHUNK // 2 - 1)
    a_last = _chunk_row_bcast(a, nc, HGRN_CHUNK - 1)
    qs = q * HGRN_SCALE
    e_in, e_out, e_end, e_all = jnp.exp(a - a_mid), jnp.exp(a_mid - a), jnp.exp(a_last - a), jnp.exp(a)
    decay = [jnp.exp(a[c * HGRN_CHUNK + HGRN_CHUNK - 1:(c + 1) * HGRN_CHUNK, :]) for c in range(nc)]
    return dict(s=s, f=f, k=k, decay=decay, e_in=e_in, e_out=e_out, e_end=e_end, e_all=e_all,
                qi=qs * e_in, ki=k * e_out, kd=k * e_end, qe=qs * e_all)


def _hgrn_forward(proj, lb_table, norm_g):
    t = proj.shape[1]
    tb = _tile(t, 1024)
    nc = tb // HGRN_CHUNK
    n_chunks = t // HGRN_CHUNK

    def body(q_ref, f_ref, i_ref, g_ref, tab_ref, ng_ref, og_ref, o_ref, st_ref, state):
        @pl.when(pl.program_id(1) == 0)
        def _():
            state[...] = jnp.zeros_like(state)

        lower, _ = _tri_masks()
        gt = _hgrn_gates(q_ref[...], f_ref[...], _lower_bound(tab_ref), nc)
        qi, ki, kd, qe = (gt[n].astype(BF16) for n in ("qi", "ki", "kd", "qe"))
        vb = i_ref[...].astype(BF16)
        o_intra, d_state = [], []
        for c in range(nc):
            rows = _chunk_rows(c)
            p = jnp.where(lower, _dot_nt(qi[rows], ki[rows]), 0.0).astype(BF16)
            o_intra.append(_dot(p, vb[rows]))
            d_state.append(_dot_tn(vb[rows], kd[rows]))
        st = state[...]
        outs = []
        for c in range(nc):
            st_ref[c] = st
            outs.append(o_intra[c] + _dot_nt(qe[_chunk_rows(c)], st.astype(BF16)))
            st = st * gt["decay"][c] + d_state[c]
        state[...] = st
        o = jnp.concatenate(outs, axis=0)
        o_ref[...] = o
        _, oh = _rms_stats(o)
        gz = g_ref[...]
        og_ref[...] = ((oh * ng_ref[...]) * (gz * _sigmoid(gz))).astype(BF16)

    def blk(p):
        return pl.BlockSpec((None, tb, HEAD_DIM), lambda h, n: (p, n, h))

    out_blk = pl.BlockSpec((tb, HEAD_DIM), lambda h, n: (n, h))
    return pl.pallas_call(
        body, name="hgrn_fwd",
        out_shape=[jax.ShapeDtypeStruct((t, D_MODEL), BF16), jax.ShapeDtypeStruct((t, D_MODEL), F32),
                   jax.ShapeDtypeStruct((HEADS, n_chunks, HEAD_DIM, HEAD_DIM), F32)],
        grid=(HEADS, t // tb),
        in_specs=[blk(Q_POS), blk(Q_POS + 1), blk(Q_POS + 2), blk(Q_POS + 3),
                  pl.BlockSpec((2, HEAD_DIM), lambda h, n: (0, h)), pl.BlockSpec((1, HEAD_DIM), lambda h, n: (0, h))],
        out_specs=[out_blk, out_blk, pl.BlockSpec((None, nc, HEAD_DIM, HEAD_DIM), lambda h, n: (h, n, 0, 0))],
        scratch_shapes=[pltpu.VMEM((HEAD_DIM, HEAD_DIM), F32)],
        compiler_params=_params([((tb, HEAD_DIM), F32)] * 6 + [((nc, HEAD_DIM, HEAD_DIM), F32)], temps=8 << 20,
                                sem=("arbitrary", "arbitrary")),
    )(proj, proj, proj, proj, lb_table, norm_g)


def _branch_out_forward(a, og, proj, x, w_a, w_b, w_out, ffn_g):
    t = x.shape[0]
    tm = _tile(t, 512)

    def body(a_ref, og_ref, ga_ref, gb_ref, x_ref, wa_ref, wb_ref, wo_ref, g_ref, ya_ref, yb_ref, mg_ref, x1_ref, h2_ref):
        ya = _dot(a_ref[...], wa_ref[...])
        yb = _dot(og_ref[...], wb_ref[...])
        ya_ref[...] = ya
        yb_ref[...] = yb
        merged = (_sigmoid(ga_ref[...]) * ya + _sigmoid(gb_ref[...]) * yb).astype(BF16)
        mg_ref[...] = merged
        x1 = x_ref[...] + _dot(merged, wo_ref[...])
        x1_ref[...] = x1
        _, xh = _rms_stats(x1)
        h2_ref[...] = (xh * g_ref[...]).astype(BF16)

    tok = pl.BlockSpec((tm, D_MODEL), lambda m: (m, 0))
    return pl.pallas_call(
        body, name="branch_out_fwd",
        out_shape=[jax.ShapeDtypeStruct((t, D_MODEL), F32), jax.ShapeDtypeStruct((t, D_MODEL), F32),
                   jax.ShapeDtypeStruct((t, D_MODEL), BF16), jax.ShapeDtypeStruct((t, D_MODEL), F32),
                   jax.ShapeDtypeStruct((t, D_MODEL), BF16)],
        grid=(t // tm,),
        in_specs=[tok, tok, pl.BlockSpec((None, tm, D_MODEL), lambda m: (GATE_POS, m, 0)),
                  pl.BlockSpec((None, tm, D_MODEL), lambda m: (GATE_POS + 1, m, 0)), tok, RESIDENT, RESIDENT, RESIDENT,
                  pl.BlockSpec((1, D_MODEL), lambda m: (0, 0))],
        out_specs=[tok] * 5,
        compiler_params=_params([((tm, D_MODEL), BF16)] * 4 + [((tm, D_MODEL), F32)] * 6, scratch=[((D_MODEL, D_MODEL), BF16)] * 3,
                                temps=8 << 20, sem=("arbitrary",)),
    )(a, og, proj, proj, x, w_a, w_b, w_out, ffn_g)


def _ffn_forward(h2, x1, w_gu, w_down, target, final_g):
    t = x1.shape[0]
    tm = _tile(t, 512)

    def body(h_ref, wgu_ref, wd_ref, x1_ref, t_ref, g_ref, gu_ref, act_ref, loss_ref, dg_ref, dx_ref, dxb_ref, acc):
        m, j = pl.program_id(0), pl.program_id(1)

        @pl.when((m == 0) & (j == 0))
        def _():
            loss_ref[...] = jnp.zeros_like(loss_ref)
            dg_ref[...] = jnp.zeros_like(dg_ref)

        h = h_ref[...]
        gate = _dot_nt(h, wgu_ref[j])
        up = _dot_nt(h, wgu_ref[j + 4])
        gu_ref[0] = gate.astype(BF16)
        gu_ref[1] = up.astype(BF16)
        act = ((gate * _sigmoid(gate)) * up).astype(BF16)
        act_ref[...] = act
        part = _dot(act, wd_ref[j])

        @pl.when(j == 0)
        def _():
            acc[...] = part

        @pl.when((j > 0) & (j < 3))
        def _():
            acc[...] += part

        @pl.when(j == 3)
        def _():
            x2 = x1_ref[...] + (acc[...] + part)
            g = g_ref[...]
            r, xh = _rms_stats(x2)
            err = xh * g - t_ref[...]
            loss_ref[...] += 0.5 * jnp.sum(jnp.mean(err * err, axis=-1, keepdims=True), axis=0, keepdims=True)
            dy = err * (1.0 / D_MODEL)
            dg_ref[...] += jnp.sum(dy * xh, axis=0, keepdims=True)
            dxh = dy * g
            dx = r * (dxh - xh * jnp.mean(dxh * xh, axis=-1, keepdims=True))
            dx_ref[...] = dx
            dxb_ref[...] = dx.astype(BF16)

    tok = pl.BlockSpec((tm, D_MODEL), lambda m, j: (m, 0))
    vec = pl.BlockSpec((1, D_MODEL), lambda m, j: (0, 0))
    return pl.pallas_call(
        body, name="ffn_fwd",
        out_shape=[jax.ShapeDtypeStruct((4, 2, t, FF_BLOCK), BF16), jax.ShapeDtypeStruct((4, t, FF_BLOCK), BF16),
                   jax.ShapeDtypeStruct((8, 128), F32), jax.ShapeDtypeStruct((1, D_MODEL), F32),
                   jax.ShapeDtypeStruct((t, D_MODEL), F32), jax.ShapeDtypeStruct((t, D_MODEL), BF16)],
        grid=(t // tm, 4),
        in_specs=[tok, RESIDENT, RESIDENT, tok, tok, vec],
        out_specs=[pl.BlockSpec((None, 2, tm, FF_BLOCK), lambda m, j: (j, 0, m, 0)),
                   pl.BlockSpec((None, tm, FF_BLOCK), lambda m, j: (j, m, 0)),
                   pl.BlockSpec((8, 128), lambda m, j: (0, 0)), vec, tok, tok],
        scratch_shapes=[pltpu.VMEM((tm, D_MODEL), F32)],
        compiler_params=_params([((tm, D_MODEL), BF16), ((tm, D_MODEL), F32), ((tm, D_MODEL), F32), ((2, tm, 768), BF16),
                                 ((tm, 768), BF16), ((tm, D_MODEL), F32), ((tm, D_MODEL), BF16)],
                                scratch=[((tm, D_MODEL), F32), ((N_DEV, FF_BLOCK, D_MODEL), BF16), ((D_FF, D_MODEL), BF16)],
                                temps=6 << 20, sem=("arbitrary", "arbitrary")),
    )(h2, w_gu, w_down.reshape(4, FF_BLOCK, D_MODEL), x1, target, final_g)


def _ffn_backward(dx2b, dx2, gu, x1, w_gu, w_down, ffn_g):
    t = x1.shape[0]
    tm = _tile(t, 512)

    def body(dxb_ref, dx2_ref, gu_ref, x1_ref, wgu_ref, wd_ref, g_ref, dgu_ref, dx1_ref, dx1b_ref, dg_ref, acc, prev):
        m, j = pl.program_id(0), pl.program_id(1)

        @pl.when((m == 0) & (j == 0))
        def _():
            dg_ref[...] = jnp.zeros_like(dg_ref)

        @pl.when(j == 0)
        def _():
            prev[...] = jnp.zeros_like(prev)
            acc[...] = jnp.zeros_like(acc)

        jm1 = jnp.maximum(j - 1, 0)
        acc[...] += _dot(prev[0], wgu_ref[jm1]) + _dot(prev[1], wgu_ref[jm1 + 4])
        dact = _dot_nt(dxb_ref[...], wd_ref[j])
        gate, up = gu_ref[0].astype(F32), gu_ref[1].astype(F32)
        sg = _sigmoid(gate)
        dgate = (dact * up * (sg * (1.0 + gate * (1.0 - sg)))).astype(BF16)
        dup = (dact * (gate * sg)).astype(BF16)
        dgu_ref[0] = dgate
        dgu_ref[1] = dup
        prev[0] = dgate
        prev[1] = dup

        @pl.when(j == 3)
        def _():
            dh2 = acc[...] + (_dot(prev[0], wgu_ref[3]) + _dot(prev[1], wgu_ref[7]))
            dx, dg = _rms_bwd(dh2, x1_ref[...], g_ref[...])
            dx1 = dx2_ref[...] + dx
            dx1_ref[...] = dx1
            dx1b_ref[...] = dx1.astype(BF16)
            dg_ref[...] += dg

    tok = pl.BlockSpec((tm, D_MODEL), lambda m, j: (m, 0))
    vec = pl.BlockSpec((1, D_MODEL), lambda m, j: (0, 0))
    gu_spec = pl.BlockSpec((None, 2, tm, FF_BLOCK), lambda m, j: (j, 0, m, 0))
    return pl.pallas_call(
        body, name="ffn_bwd",
        out_shape=[jax.ShapeDtypeStruct((4, 2, t, FF_BLOCK), BF16), jax.ShapeDtypeStruct((t, D_MODEL), F32),
                   jax.ShapeDtypeStruct((t, D_MODEL), BF16), jax.ShapeDtypeStruct((1, D_MODEL), F32)],
        grid=(t // tm, 4),
        in_specs=[tok, tok, gu_spec, tok, RESIDENT, RESIDENT, vec],
        out_specs=[gu_spec, tok, tok, vec],
        scratch_shapes=[pltpu.VMEM((tm, D_MODEL), F32), pltpu.VMEM((2, tm, FF_BLOCK), BF16)],
        compiler_params=_params([((tm, D_MODEL), BF16), ((tm, D_MODEL), F32), ((2, tm, 768), BF16), ((tm, D_MODEL), F32),
                                 ((2, tm, 768), BF16), ((tm, D_MODEL), F32), ((tm, D_MODEL), BF16)],
                                scratch=[((tm, D_MODEL), F32), ((2, tm, 768), BF16), ((N_DEV, FF_BLOCK, D_MODEL), BF16),
                                         ((D_FF, D_MODEL), BF16)],
                                temps=4 << 20, sem=("arbitrary", "arbitrary")),
    )(dx2b, dx2, gu, x1, w_gu, w_down.reshape(4, FF_BLOCK, D_MODEL), ffn_g)


def _branch_out_backward(dx1b, ya, yb, proj, w_a, w_b, w_out):
    t = ya.shape[0]
    tm = _tile(t, 512)

    def body(dx_ref, ya_ref, yb_ref, ga_ref, gb_ref, wa_ref, wb_ref, wo_ref, dya_ref, dyb_ref, dgate_ref, da_ref, dog_ref):
        dm = _dot_nt(dx_ref[...], wo_ref[...])
        sa, sb = _sigmoid(ga_ref[...]), _sigmoid(gb_ref[...])
        dya = (dm * sa).astype(BF16)
        dyb = (dm * sb).astype(BF16)
        dya_ref[...] = dya
        dyb_ref[...] = dyb
        dgate_ref[0] = (dm * ya_ref[...] * (sa * (1.0 - sa))).astype(BF16)
        dgate_ref[1] = (dm * yb_ref[...] * (sb * (1.0 - sb))).astype(BF16)
        da_ref[...] = _dot_nt(dya, wa_ref[...])
        dog_ref[...] = _dot_nt(dyb, wb_ref[...])

    tok = pl.BlockSpec((tm, D_MODEL), lambda m: (m, 0))
    return pl.pallas_call(
        body, name="branch_out_bwd",
        out_shape=[jax.ShapeDtypeStruct((t, D_MODEL), BF16), jax.ShapeDtypeStruct((t, D_MODEL), BF16),
                   jax.ShapeDtypeStruct((N_DEV, t, D_MODEL), BF16), jax.ShapeDtypeStruct((t, D_MODEL), F32),
                   jax.ShapeDtypeStruct((t, D_MODEL), F32)],
        grid=(t // tm,),
        in_specs=[tok, tok, tok, pl.BlockSpec((None, tm, D_MODEL), lambda m: (GATE_POS, m, 0)),
                  pl.BlockSpec((None, tm, D_MODEL), lambda m: (GATE_POS + 1, m, 0)), RESIDENT, RESIDENT, RESIDENT],
        out_specs=[tok, tok, pl.BlockSpec((2, tm, D_MODEL), lambda m: (GATE_POS // 2, m, 0)), tok, tok],
        compiler_params=_params([((tm, D_MODEL), BF16)] * 5 + [((tm, D_MODEL), F32)] * 6, scratch=[((D_MODEL, D_MODEL), BF16)] * 3,
                                temps=8 << 20, sem=("arbitrary",)),
    )(dx1b, ya, yb, proj, proj, w_a, w_b, w_out)


def _hgrn_backward(dproj, dog, o_saved, states, proj, lb_table, norm_g):
    t = proj.shape[1]
    tb = _tile(t, 1024)
    nc = tb // HGRN_CHUNK
    nb = t // tb

    def body(_, dog_ref, o_ref, st_ref, q_ref, f_ref, i_ref, g_ref, tab_ref, ng_ref, dp_ref, dng_ref, dtab_ref, gstate):
        @pl.when(pl.program_id(1) == 0)
        def _():
            gstate[...] = jnp.zeros_like(gstate)
            dng_ref[...] = jnp.zeros_like(dng_ref)
            dtab_ref[...] = jnp.zeros_like(dtab_ref)

        lb = _lower_bound(tab_ref)
        ng = ng_ref[...]
        lower, upper = _tri_masks()
        gt = _hgrn_gates(q_ref[...], f_ref[...], lb, nc)
        qi, ki, kd, qe = (gt[n].astype(BF16) for n in ("qi", "ki", "kd", "qe"))
        vb = i_ref[...].astype(BF16)
        o, gz, d_og = o_ref[...], g_ref[...], dog_ref[...]
        r, oh = _rms_stats(o)
        sg = _sigmoid(gz)
        d_on = d_og * (gz * sg)
        dgz = d_og * (oh * ng) * (sg * (1.0 + gz * (1.0 - sg)))
        dng_ref[...] += jnp.sum(d_on * oh, axis=0, keepdims=True)
        doh = d_on * ng
        dob = (r * (doh - oh * jnp.mean(doh * oh, axis=-1, keepdims=True))).astype(BF16)
        dv_intra, dqi, dki, dqe, g_upd = [], [], [], [], []
        for c in range(nc):
            rows = _chunk_rows(c)
            p = jnp.where(lower, _dot_nt(qi[rows], ki[rows]), 0.0).astype(BF16)
            dv_intra.append(_dot_tn(p, dob[rows]))
            dp = jnp.where(lower, _dot_nt(dob[rows], vb[rows]), 0.0).astype(BF16)
            dqi.append(_dot(dp, ki[rows]))
            dki.append(_dot_tn(dp, qi[rows]))
            dqe.append(_dot(dob[rows], st_ref[c].astype(BF16)))
            g_upd.append(_dot_tn(dob[rows], qe[rows]))
        g_after = [None] * nc
        g = gstate[...]
        for c in reversed(range(nc)):
            g_after[c] = g
            g = g * gt["decay"][c] + g_upd[c]
        gstate[...] = g
        dkd, dv, da_last = [], [], []
        for c in range(nc):
            rows = _chunk_rows(c)
            gb = g_after[c].astype(BF16)
            dkd.append(_dot(vb[rows], gb))
            dv.append(dv_intra[c] + _dot_nt(kd[rows], gb))
            da_last.append(jnp.sum(g_after[c] * st_ref[c], axis=0, keepdims=True) * gt["decay"][c])
        dqi, dki, dqe, dkd, dv = (jnp.concatenate(z, axis=0) for z in (dqi, dki, dqe, dkd, dv))
        dqs = dqi * gt["e_in"] + dqe * gt["e_all"]
        dk = dki * gt["e_out"] + dkd * gt["e_end"]
        t_in, t_out, t_end = dqi * gt["qi"], dki * gt["ki"], dkd * gt["kd"]
        da = t_in - t_out + dqe * gt["qe"] - t_end
        row = lax.broadcasted_iota(jnp.int32, (HGRN_CHUNK, HEAD_DIM), 0)
        d_mid = t_out - t_in
        pieces = []
        for c in range(nc):
            rows = _chunk_rows(c)
            da_mid = jnp.sum(d_mid[rows], axis=0, keepdims=True)
            da_end = jnp.sum(t_end[rows], axis=0, keepdims=True) + da_last[c]
            da_c = da[rows] + jnp.where(row == HGRN_CHUNK // 2 - 1, da_mid, 0.0) + jnp.where(row == HGRN_CHUNK - 1, da_end, 0.0)
            pieces.append(_mask_mm(upper.astype(BF16), da_c))
        df = jnp.concatenate(pieces, axis=0) / gt["f"] - dk
        s = gt["s"]
        dlb = jnp.sum(df * (1.0 - s), axis=0, keepdims=True)
        dp_ref[0] = (dqs * HGRN_SCALE).astype(BF16)
        dp_ref[1] = (df * (1.0 - lb) * (s * (1.0 - s))).astype(BF16)
        dp_ref[2] = dv.astype(BF16)
        dp_ref[3] = dgz.astype(BF16)
        dt0 = dlb * (lb * (1.0 - lb))
        dtab_ref[0:1, :] += dt0
        dtab_ref[1:2, :] -= dt0

    def blk(p):
        return pl.BlockSpec((None, tb, HEAD_DIM), lambda h, n: (p, nb - 1 - n, h))

    tok = pl.BlockSpec((tb, HEAD_DIM), lambda h, n: (nb - 1 - n, h))
    return pl.pallas_call(
        body, name="hgrn_bwd",
        out_shape=[jax.ShapeDtypeStruct((N_DEV, t, D_MODEL), BF16), jax.ShapeDtypeStruct((1, D_MODEL), F32),
                   jax.ShapeDtypeStruct((2, D_MODEL), F32)],
        grid=(HEADS, nb),
        in_specs=[ANY, tok, tok, pl.BlockSpec((None, nc, HEAD_DIM, HEAD_DIM), lambda h, n: (h, nb - 1 - n, 0, 0)),
                  blk(Q_POS), blk(Q_POS + 1), blk(Q_POS + 2), blk(Q_POS + 3),
                  pl.BlockSpec((2, HEAD_DIM), lambda h, n: (0, h)), pl.BlockSpec((1, HEAD_DIM), lambda h, n: (0, h))],
        out_specs=[pl.BlockSpec((4, tb, HEAD_DIM), lambda h, n: (0, nb - 1 - n, h)),
                   pl.BlockSpec((1, HEAD_DIM), lambda h, n: (0, h)), pl.BlockSpec((2, HEAD_DIM), lambda h, n: (0, h))],
        scratch_shapes=[pltpu.VMEM((HEAD_DIM, HEAD_DIM), F32)],
        input_output_aliases={0: 0},
        compiler_params=_params([((tb, HEAD_DIM), F32)] * 6 + [((nc, HEAD_DIM, HEAD_DIM), F32)] + [((4, tb, HEAD_DIM), BF16)],
                                temps=8 << 20, sem=("arbitrary", "arbitrary")),
    )(dproj, dog, o_saved, states, proj, proj, proj, proj, lb_table, norm_g)


def _gmlp_backward(dproj, da, proj, ln_g, ln_b, w_s, bias_b):
    t = proj.shape[1]
    tm = _tile(t, 256)
    chunks = tm // GMLP_CHUNK

    def body(_, da_ref, u_ref, v_ref, lng_ref, lnb_ref, ws_ref, bias_ref, dp_ref, dlng_ref, dlnb_ref, dws_ref, dbs_ref,
             vn_scr, dvn_scr):
        @pl.when(pl.program_id(0) == 0)
        def _():
            dlng_ref[...] = jnp.zeros_like(dlng_ref)
            dlnb_ref[...] = jnp.zeros_like(dlnb_ref)
            dws_ref[...] = jnp.zeros_like(dws_ref)
            dbs_ref[...] = jnp.zeros_like(dbs_ref)

        v = v_ref[...]
        vv, dvv_dv = _gelu_and_grad(v)
        mu = jnp.mean(vv, axis=-1, keepdims=True)
        cen = vv - mu
        rstd = lax.rsqrt(jnp.mean(cen * cen, axis=-1, keepdims=True) + NORM_EPS)
        vhat = cen * rstd
        lng = lng_ref[...]
        vn_scr[...] = (vhat * lng + lnb_ref[...]).astype(BF16)
        row = lax.broadcasted_iota(jnp.int32, (GMLP_CHUNK, GMLP_CHUNK), 0)
        col = lax.broadcasted_iota(jnp.int32, (GMLP_CHUNK, GMLP_CHUNK), 1)
        for g in range(GROUPS):
            wm = _masked_ws(ws_ref, g)
            cols = slice(g * HEAD_DIM, (g + 1) * HEAD_DIM)
            dws = jnp.zeros((GMLP_CHUNK, GMLP_CHUNK), F32)
            dbs = jnp.zeros((GMLP_CHUNK, GMLP_CHUNK), F32)
            for c in range(chunks):
                rows = slice(c * GMLP_CHUNK, (c + 1) * GMLP_CHUNK)
                vn = vn_scr[rows, cols]
                mixed = _dot(wm, vn) + bias_ref[g]
                u = u_ref[rows, cols]
                d_a = da_ref[rows, cols]
                gelu_u, dgelu_u = _gelu_and_grad(u)
                dp_ref[0, rows, cols] = (d_a * mixed * dgelu_u).astype(BF16)
                dmix = d_a * gelu_u
                dmb = dmix.astype(BF16)
                dbs = dbs + dmix
                dws = dws + _dot_nt(dmb, vn)
                dvn_scr[rows, cols] = _dot_tn(wm, dmb)
            dws_ref[g] += jnp.where(row >= col, dws, 0.0)
            dbs_ref[g] += jnp.broadcast_to(jnp.sum(dbs, axis=-1, keepdims=True), (GMLP_CHUNK, GMLP_CHUNK))
        dvn = dvn_scr[...]
        dlng_ref[...] += jnp.sum(dvn * vhat, axis=0, keepdims=True)
        dlnb_ref[...] += jnp.sum(dvn, axis=0, keepdims=True)
        dvh = dvn * lng
        dvv = rstd * (dvh - jnp.mean(dvh, axis=-1, keepdims=True) - vhat * jnp.mean(dvh * vhat, axis=-1, keepdims=True))
        dp_ref[1] = (dvv * dvv_dv).astype(BF16)

    tok = pl.BlockSpec((tm, D_MODEL), lambda m: (m, 0))
    small = pl.BlockSpec((GROUPS, GMLP_CHUNK, GMLP_CHUNK), lambda m: (0, 0, 0))
    vec = pl.BlockSpec((1, D_MODEL), lambda m: (0, 0))
    return pl.pallas_call(
        body, name="gmlp_bwd",
        out_shape=[jax.ShapeDtypeStruct(dproj.shape, BF16), jax.ShapeDtypeStruct((1, D_MODEL), F32),
                   jax.ShapeDtypeStruct((1, D_MODEL), F32), jax.ShapeDtypeStruct((GROUPS, GMLP_CHUNK, GMLP_CHUNK), F32),
                   jax.ShapeDtypeStruct((GROUPS, GMLP_CHUNK, GMLP_CHUNK), F32)],
        grid=(t // tm,),
        in_specs=[ANY, tok, pl.BlockSpec((None, tm, D_MODEL), lambda m: (U_POS, m, 0)),
                  pl.BlockSpec((None, tm, D_MODEL), lambda m: (U_POS + 1, m, 0)), vec, vec, small, small],
        out_specs=[pl.BlockSpec((2, tm, D_MODEL), lambda m: (U_POS // 2, m, 0)), vec, vec, small, small],
        scratch_shapes=[pltpu.VMEM((tm, D_MODEL), BF16), pltpu.VMEM((tm, D_MODEL), F32)],
        input_output_aliases={0: 0},
        compiler_params=_params([((tm, D_MODEL), F32)] * 3 + [((2, tm, D_MODEL), BF16)] + [((8, 128, 128), F32)] * 4,
                                scratch=[((tm, D_MODEL), BF16), ((tm, D_MODEL), F32)], temps=12 << 20, sem=("arbitrary",)),
    )(dproj, da, proj, proj, ln_g, ln_b, w_s, bias_b)


def _input_backward(dproj, w_in_g, x, dx1, mix_g):
    t = x.shape[0]
    tm = _tile(t, 512)

    def body(dp_ref, w_ref, x_ref, dx1_ref, g_ref, dx_ref, dg_ref):
        @pl.when(pl.program_id(0) == 0)
        def _():
            dg_ref[...] = jnp.zeros_like(dg_ref)

        dh = _dot_nt(dp_ref[0], w_ref[0])
        for p in range(1, N_DEV):
            dh = dh + _dot_nt(dp_ref[p], w_ref[p])
        dx, dg = _rms_bwd(dh, x_ref[...], g_ref[...])
        dx_ref[...] = dx1_ref[...] + dx
        dg_ref[...] += dg

    tok = pl.BlockSpec((tm, D_MODEL), lambda m: (m, 0))
    vec = pl.BlockSpec((1, D_MODEL), lambda m: (0, 0))
    return pl.pallas_call(
        body, name="input_bwd",
        out_shape=[jax.ShapeDtypeStruct((t, D_MODEL), F32), jax.ShapeDtypeStruct((1, D_MODEL), F32)],
        grid=(t // tm,),
        in_specs=[pl.BlockSpec((N_DEV, tm, D_MODEL), lambda m: (0, m, 0)), RESIDENT, tok, tok, vec],
        out_specs=[tok, vec],
        compiler_params=_params([((N_DEV, tm, D_MODEL), BF16)] + [((tm, D_MODEL), F32)] * 3,
                                scratch=[((N_DEV, D_MODEL, D_MODEL), BF16)], temps=6 << 20, sem=("arbitrary",)),
    )(dproj, w_in_g, x, dx1, mix_g)


def _weight_grad(name, a, b, a_spec, b_spec, out_shape, out_spec, steps, blocks, a_is_transposed):
    def body(a_ref, b_ref, o_ref):
        o_ref[...] = _dot(a_ref[...], b_ref[...]) if a_is_transposed else _dot_tn(a_ref[...], b_ref[...])

    return pl.pallas_call(
        body, name=name, out_shape=jax.ShapeDtypeStruct(out_shape, F32), grid=(steps,), in_specs=[a_spec, b_spec],
        out_specs=out_spec, compiler_params=_params(blocks, temps=4 << 20, sem=("arbitrary",)),
    )(a, b)


def _pack_small(mix_g, ln_g, ln_b, b_s, lb_table, hg_norm, ffn_g, final_g, loss_row):
    def part(a):
        a = a.reshape(-1, D_MODEL)
        return jnp.pad(a, ((0, 8 - a.shape[0]), (0, 0)))

    return jnp.concatenate([part(mix_g), part(ln_g), part(ln_b), part(hg_norm), part(ffn_g), part(final_g),
                            part(lb_table), part(b_s), part(loss_row)], axis=0)


SMALL_PARTS = (("gmlp_ln_g", 8, 1), ("gmlp_ln_b", 16, 1), ("hgrn_norm_g", 24, 1), ("norm_ffn_g", 32, 1), ("norm_final_g", 40, 1),
               ("hgrn_lb_table", 48, 2))


def _adamw_small_unpacked(gathered, w, m, v):
    rows = w.shape[0]
    n_out = len(SMALL_PARTS) + 1

    def body(p_ref, w_ref, m_ref, v_ref, *outs):
        g = p_ref[0]
        for j in range(1, N_DEV):
            g = g + p_ref[j]
        delta, m_new, v_new = _adamw_math(w_ref[...], g, m_ref[...], v_ref[...])
        for kind, val in enumerate((g, delta, m_new, v_new)):
            refs = outs[kind * n_out:(kind + 1) * n_out]
            for (_, first, count), ref in zip(SMALL_PARTS, refs):
                ref[...] = val[first:first + count]
            for grp in range(GROUPS):
                refs[-1][0, grp:grp + 1, :] = val[56:57, grp * GMLP_CHUNK:(grp + 1) * GMLP_CHUNK]
        outs[-1][...] = g[SMALL_ROWS - 8:SMALL_ROWS - 7]

    shapes = [jax.ShapeDtypeStruct((count, D_MODEL), F32) for _, _, count in SMALL_PARTS]
    shapes.append(jax.ShapeDtypeStruct((1, GROUPS, GMLP_CHUNK), F32))
    whole = pl.BlockSpec((rows, D_MODEL), lambda: (0, 0))
    res = pl.pallas_call(
        body, name="adamw_small", out_shape=shapes * 4 + [jax.ShapeDtypeStruct((1, D_MODEL), F32)],
        in_specs=[pl.BlockSpec((N_DEV, rows, D_MODEL), lambda: (0, 0, 0)), whole, whole, whole],
        compiler_params=_params([((N_DEV, rows, D_MODEL), F32)] + [((rows, D_MODEL), F32)] * 7),
    )(gathered, w, m, v)
    names = [nme for nme, _, _ in SMALL_PARTS] + ["gmlp_b_s"]
    return [dict(zip(names, res[kind * n_out:(kind + 1) * n_out])) for kind in range(4)], res[-1]


def _adamw_row(name, gathered, w, m, v):
    def body(p_ref, w_ref, m_ref, v_ref, g_out, d_out, m_out, v_out):
        g = p_ref[0, 0:1, :]
        for j in range(1, N_DEV):
            g = g + p_ref[j, 0:1, :]
        delta, m_new, v_new = _adamw_math(w_ref[...], g, m_ref[...], v_ref[...])
        g_out[...] = g
        d_out[...] = delta
        m_out[...] = m_new
        v_out[...] = v_new

    return pl.pallas_call(
        body, name=name, out_shape=[jax.ShapeDtypeStruct((1, D_MODEL), F32)] * 4,
        compiler_params=_params([((N_DEV, 8, D_MODEL), F32)] + [((8, D_MODEL), F32)] * 7),
    )(gathered, w, m, v)


def _adamw_small(name, gathered, w, m, v):
    rows, cols = w.shape

    def body(p_ref, w_ref, m_ref, v_ref, g_out, d_out, m_out, v_out):
        g = p_ref[0]
        for j in range(1, N_DEV):
            g = g + p_ref[j]
        delta, m_new, v_new = _adamw_math(w_ref[...], g, m_ref[...], v_ref[...])
        g_out[...] = g
        d_out[...] = delta
        m_out[...] = m_new
        v_out[...] = v_new

    tr = _tile(rows, 512)
    spec = pl.BlockSpec((tr, cols), lambda r: (r, 0))
    return pl.pallas_call(
        body, name=name, out_shape=[jax.ShapeDtypeStruct((rows, cols), F32)] * 4, grid=(rows // tr,),
        in_specs=[pl.BlockSpec((N_DEV, tr, cols), lambda r: (0, r, 0)), spec, spec, spec], out_specs=[spec] * 4,
        compiler_params=_params([((N_DEV, tr, cols), F32)] + [((tr, cols), F32)] * 7, sem=("arbitrary",)),
    )(gathered, w, m, v)


def kernel(x, norm_mix_g, w_in, gmlp_ln_g, gmlp_ln_b, gmlp_w_s, gmlp_b_s, hgrn_lb_table, hgrn_norm_g, w_branch_a, w_branch_b, w_out, norm_ffn_g, w_gate_up, w_down, norm_final_g, loss_target, m_norm_mix_g, m_w_in, m_gmlp_ln_g, m_gmlp_ln_b, m_gmlp_w_s, m_gmlp_b_s, m_hgrn_lb_table, m_hgrn_norm_g, m_w_branch_a, m_w_branch_b, m_w_out, m_norm_ffn_g, m_w_gate_up, m_w_down, m_norm_final_g, v_norm_mix_g, v_w_in, v_gmlp_ln_g, v_gmlp_ln_b, v_gmlp_w_s, v_gmlp_b_s, v_hgrn_lb_table, v_hgrn_norm_g, v_w_branch_a, v_w_branch_b, v_w_out, v_norm_ffn_g, v_w_gate_up, v_w_down, v_norm_final_g):
    t = x.shape[1]
    x2d = x.reshape(t, D_MODEL)
    target = loss_target.reshape(t, D_MODEL)
    final_g = norm_final_g.reshape(1, D_MODEL)

    w_in_shard = w_in[0].astype(BF16)

    def rows_of(n):
        return lambda ref, j: ref.at[pl.ds(pl.multiple_of(j * n, 8), n)]

    gathered = [((N_DEV, D_MODEL, D_MODEL), BF16), ((D_MODEL, D_MODEL), BF16), ((D_MODEL, D_MODEL), BF16),
                ((D_MODEL, D_MODEL), BF16), ((N_DEV, FF_BLOCK, D_MODEL), BF16), ((D_FF, D_MODEL), BF16)]
    places = [lambda ref, j: ref.at[_pos_of_dev(j)], rows_of(BRANCH_ROWS), rows_of(BRANCH_ROWS), rows_of(BRANCH_ROWS),
              lambda ref, j: ref.at[j], rows_of(DOWN_ROWS)]
    w_in_g = _all_gather_balanced_async("w_in_all_gather", 9, w_in_shard, gathered[0], places[0], D_MODEL)
    w_in_sibling = _swap_with_sibling("w_in_from_sibling", w_in_shard)

    core_i, chip_i = lax.axis_index("c"), 2 * lax.axis_index("x") + lax.axis_index("y")
    own_pos = jnp.stack([_pos_of_dev(2 * chip_i + core_i), _pos_of_dev(2 * chip_i + 1 - core_i)]).astype(jnp.int32)
    other_pos = jnp.stack([_pos_of_dev(2 * jnp.bitwise_xor(chip_i, q) + cc) for q in (1, 2, 3) for cc in (0, 1)]).astype(jnp.int32)
    proj, h, h_t = _proj_forward_own_chip(own_pos, x2d, norm_mix_g, w_in_shard, w_in_sibling)

    small_w = [norm_mix_g, gmlp_ln_g, gmlp_ln_b, gmlp_b_s, hgrn_lb_table, hgrn_norm_g, norm_ffn_g, norm_final_g]
    small_m = [m_norm_mix_g, m_gmlp_ln_g, m_gmlp_ln_b, m_gmlp_b_s, m_hgrn_lb_table, m_hgrn_norm_g, m_norm_ffn_g, m_norm_final_g]
    small_v = [v_norm_mix_g, v_gmlp_ln_g, v_gmlp_ln_b, v_gmlp_b_s, v_hgrn_lb_table, v_hgrn_norm_g, v_norm_ffn_g, v_norm_final_g]
    _, (raw, small_w, small_m, small_v) = lax.optimization_barrier(
        (h, ([w_branch_a[0], w_branch_b[0], w_out[0], w_gate_up[0], w_down[0]], small_w, small_m, small_v)))
    later = [raw[0].astype(BF16), raw[1].astype(BF16), raw[2].astype(BF16),
             raw[3].T.astype(BF16), raw[4].astype(BF16)]
    w_a, w_b, w_o, w_gu, w_dn = _all_gather_async("weights_all_gather", 0, later, gathered[1:], places[1:])
    no_row = jnp.zeros((1, D_MODEL), F32)
    w_pack, m_pack, v_pack = (_pack_small(*vals, no_row) for vals in (small_w, small_m, small_v))
    bias_b = jnp.broadcast_to(small_w[3][0][:, :, None], (GROUPS, GMLP_CHUNK, GMLP_CHUNK))
    h_later, _ = lax.optimization_barrier((h, (later, w_pack, m_pack, v_pack, bias_b)))
    proj = _proj_forward_other_chips(other_pos, proj, h_later, w_in_g)
    a = _gmlp_forward(proj, gmlp_ln_g, gmlp_ln_b, gmlp_w_s[0], bias_b)
    og, o_saved, states = _hgrn_forward(proj, hgrn_lb_table, hgrn_norm_g)
    ya, yb, merged, x1, h2 = _branch_out_forward(a, og, proj, x2d, w_a, w_b, w_o, norm_ffn_g)
    gu, act, loss_tile, d_final_g, dx2, dx2b = _ffn_forward(h2, x1, w_gu, w_dn, target, final_g)

    core = lax.axis_index("c").astype(jnp.int32).reshape(1)
    chip = (2 * lax.axis_index("x") + lax.axis_index("y")).astype(jnp.int32).reshape(1)
    branch_rows, branch_shape = rows_of(BRANCH_ROWS), (BRANCH_ROWS, D_MODEL)
    branch_block = ((BRANCH_ROWS, D_MODEL), lambda q, r, c: (2 * q + c, 0))

    def chip_partials(names, grads, land, own_blocks):
        return [_chip_partial("chip_partial_" + nme, core, g_, blk, idx, l_)
                for nme, g_, (blk, idx), l_ in zip(names, grads, own_blocks, land)]

    whole = pl.BlockSpec((t, D_MODEL), lambda n: (0, 0))
    whole_t = pl.BlockSpec((D_MODEL, t), lambda n: (0, 0))
    col_blocks = [((t, D_MODEL), BF16), ((t, 256), BF16), ((D_MODEL, 256), F32)]

    def square_grad(name, a_, b_):
        return _weight_grad(name, a_, b_, whole, pl.BlockSpec((t, 256), lambda n: (0, n)), (D_MODEL, D_MODEL),
                            pl.BlockSpec((D_MODEL, 256), lambda n: (0, n)), D_MODEL // 256, col_blocks, False)

    dgu, dx1, dx1b, d_ffn_g = _ffn_backward(dx2b, dx2, gu, x1, w_gu, w_dn, norm_ffn_g)
    g_gu = _weight_grad(
        "grad_w_gate_up", dgu, h2, pl.BlockSpec((None, None, t, FF_BLOCK), lambda j: (j % 4, j // 4, 0, 0)), whole,
        (N_DEV, FF_BLOCK, D_MODEL), pl.BlockSpec((None, FF_BLOCK, D_MODEL), lambda j: (j, 0, 0)), N_DEV,
        [((t, 768), BF16), ((t, D_MODEL), BF16), ((FF_BLOCK, D_MODEL), F32)], False)
    g_dn = _weight_grad(
        "grad_w_down", act, dx2b, pl.BlockSpec((None, t, FF_BLOCK), lambda j: (j, 0, 0)), whole, (D_FF, D_MODEL),
        pl.BlockSpec((FF_BLOCK, D_MODEL), lambda j: (j, 0)), 4,
        [((t, 768), BF16), ((t, D_MODEL), BF16), ((FF_BLOCK, D_MODEL), F32)], False)
    names_f, grads_f = ["w_gate_up", "w_down"], [g_gu, g_dn]
    land_f = _exchange_sibling("ffn_grads_to_sibling", 2, grads_f, [lambda ref, j: ref.at[j], rows_of(DOWN_ROWS)],
                               [(FF_BLOCK, D_MODEL), (DOWN_ROWS, D_MODEL)])

    dx1b_later, _ = lax.optimization_barrier((dx1b, grads_f))
    dya, dyb, dproj, da, dog = _branch_out_backward(dx1b_later, ya, yb, proj, w_a, w_b, w_o)
    g_a = square_grad("grad_w_a", a, dya)
    g_b = square_grad("grad_w_b", og, dyb)
    g_o = square_grad("grad_w_out", merged, dx1b)
    names_b, grads_b = ["w_branch_a", "w_branch_b", "w_out"], [g_a, g_b, g_o]
    land_b = _exchange_sibling("branch_grads_to_sibling", 3, grads_b, [branch_rows] * 3, [branch_shape] * 3)

    part_f = chip_partials(names_f, grads_f, land_f,
                           [((None, FF_BLOCK, D_MODEL), lambda q, r, c: (2 * q + c, 0, 0)),
                            ((DOWN_ROWS, D_MODEL), lambda q, r, c: (2 * q + c, 0))])
    landed_f = _exchange_chips("ffn_grads_to_chips", 5, part_f)

    dog, _ = lax.optimization_barrier((dog, part_f))
    dproj, d_hg_norm, d_lb = _hgrn_backward(dproj, dog, o_saved, states, proj, hgrn_lb_table, hgrn_norm_g)

    land_b, _ = lax.optimization_barrier((land_b, part_f))
    part_b = chip_partials(names_b, grads_b, land_b, [branch_block] * 3)
    landed_b = _exchange_chips("branch_grads_to_chips", 6, part_b)

    da, _ = lax.optimization_barrier((da, part_b))
    dproj, d_ln_g, d_ln_b, d_ws, d_bs = _gmlp_backward(dproj, da, proj, gmlp_ln_g, gmlp_ln_b, gmlp_w_s[0], bias_b)

    def flat_ws(a):
        return a.reshape(GROUPS * GMLP_CHUNK, GMLP_CHUNK)

    small_partial = _pack_small(no_row, d_ln_g, d_ln_b, d_bs[:, :, 0], d_lb, d_hg_norm, d_ffn_g, d_final_g,
                                jnp.tile(loss_tile[0:1], (1, D_MODEL // 128)))
    small_all, ws_all = _all_gather_async(
        "small_grads_all_gather", 1, [small_partial, flat_ws(d_ws)],
        [((N_DEV, SMALL_ROWS, D_MODEL), F32), ((N_DEV, GROUPS * GMLP_CHUNK, GMLP_CHUNK), F32)],
        [lambda ref, j: ref.at[j], lambda ref, j: ref.at[j]])

    g_in = _weight_grad(
        "grad_w_in", h_t, dproj, whole_t, pl.BlockSpec((None, t, D_MODEL // 2), lambda n: (n // 2, 0, n % 2)),
        (N_DEV, D_MODEL, D_MODEL), pl.BlockSpec((None, D_MODEL, D_MODEL // 2), lambda n: (n // 2, 0, n % 2)), 2 * N_DEV,
        [((D_MODEL, t), BF16), ((t, D_MODEL // 2), BF16), ((D_MODEL, D_MODEL // 2), F32)], True)
    land_i = _exchange_sibling("w_in_grads_to_sibling", 4, [g_in], [lambda ref, j: ref.at[_pos_of_dev(j)]],
                               [(D_MODEL, D_MODEL)])

    (landed_f, landed_b), _ = lax.optimization_barrier(((landed_f, landed_b), g_in))
    big = {}
    for nme, own, lnd, w, m, v in zip(
            names_f + names_b, part_f + part_b, landed_f + landed_b,
            [w_gate_up, w_down, w_branch_a, w_branch_b, w_out], [m_w_gate_up, m_w_down, m_w_branch_a, m_w_branch_b, m_w_out],
            [v_w_gate_up, v_w_down, v_w_branch_a, v_w_branch_b, v_w_out]):
        flip = (lambda z: z.T) if nme == "w_gate_up" else (lambda z: z)
        big[nme] = [flip(o_)[None] for o_ in _adamw("adamw_" + nme, chip, own, lnd, flip(w[0]), flip(m[0]), flip(v[0]))]
    small, loss_row = _adamw_small_unpacked(small_all, w_pack, m_pack, v_pack)
    ws_outs = _adamw_small("adamw_w_s", ws_all, flat_ws(gmlp_w_s), flat_ws(m_gmlp_w_s), flat_ws(v_gmlp_w_s))
    land_i, _ = lax.optimization_barrier((land_i, (big, small, ws_outs)))
    part_i = chip_partials(["w_in"], [g_in], land_i,
                           [((None, D_MODEL, D_MODEL), lambda q, r, c: (_pos_of_dev(2 * q + c), 0, 0))])
    landed_i = _exchange_chips("w_in_grads_to_chips", 7, part_i)

    dx1, _ = lax.optimization_barrier((dx1, part_i))
    grad_x, d_mix_g = _input_backward(dproj, w_in_g, x2d, dx1, norm_mix_g)
    big["w_in"] = [o_[None] for o_ in _adamw("adamw_w_in", chip, part_i[0], landed_i[0], w_in[0], m_w_in[0], v_w_in[0])]

    def row8(a):
        return jnp.pad(a, ((0, 7), (0, 0)))

    d_mix_g, _ = lax.optimization_barrier((d_mix_g, landed_i))
    (mix_all,) = _all_gather_async("mix_gain_grad_all_gather", 8, [row8(d_mix_g)], [((N_DEV, 8, D_MODEL), F32)],
                                   [lambda ref, j: ref.at[j]])
    mix_outs = _adamw_row("adamw_mix_gain", mix_all, norm_mix_g, m_norm_mix_g, v_norm_mix_g)
    small = [dict(p, norm_final_g=p["norm_final_g"][0], gmlp_w_s=ws.reshape(1, GROUPS, GMLP_CHUNK, GMLP_CHUNK), norm_mix_g=q)
             for p, ws, q in zip(small, ws_outs, mix_outs)]

    loss = loss_row[0, 0]
    order = ["norm_mix_g", "w_in", "gmlp_ln_g", "gmlp_ln_b", "gmlp_w_s", "gmlp_b_s", "hgrn_lb_table", "hgrn_norm_g",
             "w_branch_a", "w_branch_b", "w_out", "norm_ffn_g", "w_gate_up", "w_down", "norm_final_g"]
    outs = [loss, grad_x.reshape(1, t, D_MODEL)]
    for kind in range(4):
        for nme in order:
            outs.append(big[nme][kind] if nme in big else small[kind][nme])
    return tuple(outs)
```

```python
import jax
import jax.numpy as jnp
from jax import lax
from jax.experimental import pallas as pl
from jax.experimental.pallas import tpu as pltpu
from jax.experimental.pallas import tpu_sc as plsc

F32, BF16 = jnp.float32, jnp.bfloat16
D_MODEL = 1024
N_DEV = 8
HEADS = 8
HEAD_DIM = 128
GROUPS = 8
GMLP_CHUNK = 128
HGRN_CHUNK = 64
HGRN_SCALE = HEAD_DIM ** -0.5
D_FF = 2816
FF_BLOCK = D_FF // 4
DOWN_ROWS = D_FF // N_DEV
BRANCH_ROWS = D_MODEL // N_DEV
NORM_EPS = 1e-6
ADAM_LR, ADAM_B1, ADAM_B2, ADAM_EPS, ADAM_WD, ADAM_STEP = 0.001, 0.9, 0.999, 1e-08, 0.01, 10
SMALL_ROWS = 72
V7X_VMEM_BYTES = 64 * 1024 * 1024
VMEM_CAP = V7X_VMEM_BYTES - 6 * 1024 * 1024
MESH_ID = pl.DeviceIdType.MESH
ANY = pl.BlockSpec(memory_space=pl.ANY)
RESIDENT = pl.BlockSpec(memory_space=pltpu.VMEM)
Q_POS, U_POS, GATE_POS = 0, 4, 6


def _pos_of_dev(j):
    return jnp.where(j < 2, j + 4, jnp.where(j < 6, j - 2, j))


def _nbytes(shape, dtype):
    n = 1
    for s in shape:
        n *= s
    return n * jnp.dtype(dtype).itemsize


def _params(blocks, scratch=(), temps=0, sem=None):
    need = 2 * sum(_nbytes(s, d) for s, d in blocks) + sum(_nbytes(s, d) for s, d in scratch) + temps
    assert need + (4 << 20) <= VMEM_CAP, need
    return pltpu.CompilerParams(dimension_semantics=sem, vmem_limit_bytes=VMEM_CAP)


def _tile(n, pref):
    return pref if n % pref == 0 else n


def _dot(a, b):
    return jnp.dot(a, b, preferred_element_type=F32)


def _dot_nt(a, b):
    return lax.dot_general(a, b, (((1,), (1,)), ((), ())), preferred_element_type=F32)


def _dot_tn(a, b):
    return lax.dot_general(a, b, (((0,), (0,)), ((), ())), preferred_element_type=F32)


def _sigmoid(x):
    return 1.0 / (1.0 + jnp.exp(-x))


_GELU_C = 0.7978845608028654


def _gelu(x):
    return x * (0.5 * (1.0 + jnp.tanh(_GELU_C * (x + 0.044715 * (x * x * x)))))


def _gelu_and_grad(x):
    t = jnp.tanh(_GELU_C * (x + 0.044715 * (x * x * x)))
    half = 0.5 * (1.0 + t)
    return x * half, half + 0.5 * x * (1.0 - t * t) * (_GELU_C * (1.0 + 3.0 * 0.044715 * x * x))


def _rms_stats(x):
    r = lax.rsqrt(jnp.mean(x * x, axis=-1, keepdims=True) + NORM_EPS)
    return r, x * r


def _rms_bwd(dy, x, g):
    r, xh = _rms_stats(x)
    dg = jnp.sum(dy * xh, axis=0, keepdims=True)
    dxh = dy * g
    dx = r * (dxh - xh * jnp.mean(dxh * xh, axis=-1, keepdims=True))
    return dx, dg


def _split3(x):
    hi = x.astype(BF16)
    r = x - hi.astype(F32)
    mid = r.astype(BF16)
    lo = (r - mid.astype(F32)).astype(BF16)
    return hi, mid, lo


def _mask_mm(mask_bf16, x):
    hi, mid, lo = _split3(x)
    return _dot(mask_bf16, hi) + _dot(mask_bf16, mid) + _dot(mask_bf16, lo)


def _place():
    return lax.axis_index("x"), lax.axis_index("y"), lax.axis_index("c")


def _gather_copies(src, out, send, recv, loc, slicers):
    n = len(src)
    x, y, c = _place()
    me, sib = (x, y, c), (x, y, 1 - c)
    chips = [(1 - x, y), (x, 1 - y), (1 - x, 1 - y)]

    def dev(p):
        return 4 * p[0] + 2 * p[1] + p[2]

    def rc(i, k, block, to, from_src=False):
        dst = slicers[i](out[i], dev(block))
        return pltpu.make_async_remote_copy(
            src_ref=src[i] if from_src else dst, dst_ref=dst, send_sem=send.at[7 * i + k],
            recv_sem=recv.at[7 * i + k], device_id=to, device_id_type=MESH_ID)

    mine = [pltpu.make_async_copy(src[i], slicers[i](out[i], dev(me)), loc.at[i]) for i in range(n)]
    for cp in mine:
        cp.start()
    first = []
    for i in range(n):
        first.append(rc(i, 0, me, sib, True))
        for j, chip in enumerate(chips):
            first.append(rc(i, 1 + j, me, (*chip, c), True))
    for cp in first:
        cp.start()
    passed = []
    for j, chip in enumerate(chips):
        for i in range(n):
            rc(i, 1 + j, (*chip, c), me).wait_recv()
            cp = rc(i, 4 + j, (*chip, c), sib)
            cp.start()
            passed.append(cp)
    for i in range(n):
        rc(i, 0, sib, me).wait_recv()
        for j, chip in enumerate(chips):
            rc(i, 4 + j, (*chip, 1 - c), me).wait_recv()
    for cp in first + passed:
        cp.wait_send()
    for cp in mine:
        cp.wait()


def _gather_copies_balanced(src, out, send, recv, loc, slicer, rows):
    x, y, c = _place()
    me, sib = (x, y, c), (x, y, 1 - c)
    xn, yn, dg = (1 - x, y), (x, 1 - y), (1 - x, 1 - y)
    half_rows = rows // 2

    def block(p):
        return slicer(out, 4 * p[0] + 2 * p[1] + p[2])

    def half(ref, h):
        return ref.at[pl.ds(h * half_rows, half_rows)]

    def rc(k, dst, to, from_src=False):
        return pltpu.make_async_remote_copy(src_ref=src if from_src else dst, dst_ref=dst, send_sem=send.at[k],
                                            recv_sem=recv.at[k], device_id=to, device_id_type=MESH_ID)

    mine = pltpu.make_async_copy(src, block(me), loc.at[0])
    mine.start()
    sends = [rc(0, block(me), sib, True), rc(1, block(me), (*xn, c), True), rc(2, block(me), (*yn, c), True)]
    for cp in sends:
        cp.start()

    def then(cp):
        cp.start()
        sends.append(cp)

    rc(1, block((*xn, c)), me).wait_recv()
    then(rc(3, half(block((*xn, c)), 0), (*yn, c)))
    then(rc(5, block((*xn, c)), sib))
    rc(2, block((*yn, c)), me).wait_recv()
    then(rc(4, half(block((*yn, c)), 1), (*xn, c)))
    then(rc(6, block((*yn, c)), sib))
    rc(3, half(block((*dg, c)), 0), me).wait_recv()
    then(rc(7, half(block((*dg, c)), 0), sib))
    rc(4, half(block((*dg, c)), 1), me).wait_recv()
    then(rc(8, half(block((*dg, c)), 1), sib))
    rc(0, block(sib), me).wait_recv()
    rc(5, block((*xn, 1 - c)), me).wait_recv()
    rc(6, block((*yn, 1 - c)), me).wait_recv()
    rc(7, half(block((*dg, 1 - c)), 0), me).wait_recv()
    rc(8, half(block((*dg, 1 - c)), 1), me).wait_recv()
    for cp in sends:
        cp.wait_send()
    mine.wait()


def _gather_scratch(n):
    return [pltpu.SemaphoreType.DMA((7 * n,)), pltpu.SemaphoreType.DMA((7 * n,)), pltpu.SemaphoreType.DMA((n,))]


def _handshake(peers):
    barrier = pltpu.get_barrier_semaphore()
    for peer in peers:
        pl.semaphore_signal(barrier, inc=1, device_id=peer, device_id_type=MESH_ID)
    pl.semaphore_wait(barrier, len(peers))


def _all_gather_async(name, collective_id, srcs, out_shapes, slicers):
    n = len(srcs)

    def body(*refs):
        x, y, c = _place()
        _handshake([(1 - x if dx else x, 1 - y if dy else y, 1 - c if dc else c)
                    for dx in (0, 1) for dy in (0, 1) for dc in (0, 1) if dx or dy or dc])
        _gather_copies(refs[:n], refs[n:2 * n], *refs[2 * n:], slicers)

    return _sequencer_call(name, collective_id, body, srcs, [jax.ShapeDtypeStruct(s, d) for s, d in out_shapes],
                           _gather_scratch(n))


def _all_gather_balanced_async(name, collective_id, src, out_shape, slicer, rows):
    def body(src_ref, out_ref, send, recv, loc):
        x, y, c = _place()
        _handshake([(1 - x if dx else x, 1 - y if dy else y, 1 - c if dc else c)
                    for dx in (0, 1) for dy in (0, 1) for dc in (0, 1) if dx or dy or dc])
        _gather_copies_balanced(src_ref, out_ref, send, recv, loc, slicer, rows)

    return _sequencer_call(name, collective_id, body, [src], [jax.ShapeDtypeStruct(*out_shape)],
                           [pltpu.SemaphoreType.DMA((9,)), pltpu.SemaphoreType.DMA((9,)), pltpu.SemaphoreType.DMA((1,))])[0]


def _sequencer_call(name, collective_id, body, operands, out_types, scratch):
    return pl.kernel(
        body, out_type=out_types, mesh=plsc.ScalarSubcoreMesh(axis_name="sequencer", num_cores=1), name=name,
        scratch_types=scratch, compiler_params=pltpu.CompilerParams(collective_id=collective_id),
    )(*operands)


def _exchange_sibling(name, collective_id, grads, shard_fns, shard_shapes):
    n = len(grads)

    def body(*refs):
        g, land = refs[:n], refs[n:2 * n]
        send, recv = refs[2 * n:]
        x, y, c = _place()
        _handshake([(x, y, 1 - c)])
        remote = []
        for i in range(n):
            for q in range(4):
                cp = pltpu.make_async_remote_copy(
                    src_ref=shard_fns[i](g[i], 2 * q + (1 - c)), dst_ref=land[i].at[q], send_sem=send.at[4 * i + q],
                    recv_sem=recv.at[4 * i + q], device_id=(x, y, 1 - c), device_id_type=MESH_ID)
                cp.start()
                remote.append(cp)
        for cp in remote:
            cp.wait()

    return _sequencer_call(name, collective_id, body, grads, [jax.ShapeDtypeStruct((4, *s), F32) for s in shard_shapes],
                           [pltpu.SemaphoreType.DMA((4 * n,)), pltpu.SemaphoreType.DMA((4 * n,))])


def _exchange_chips(name, collective_id, parts):
    n = len(parts)

    def body(*refs):
        part, out = refs[:n], refs[n:2 * n]
        send, recv = refs[2 * n:]
        x, y, c = _place()
        _handshake([(1 - x, y, c), (x, 1 - y, c), (1 - x, 1 - y, c)])
        remote = []
        for i in range(n):
            for s in range(3):
                qx = 1 - x if (s + 1) // 2 else x
                qy = 1 - y if (s + 1) % 2 else y
                cp = pltpu.make_async_remote_copy(
                    src_ref=part[i].at[2 * qx + qy], dst_ref=out[i].at[s], send_sem=send.at[3 * i + s],
                    recv_sem=recv.at[3 * i + s], device_id=(qx, qy, c), device_id_type=MESH_ID)
                cp.start()
                remote.append(cp)
        for cp in remote:
            cp.wait()

    return _sequencer_call(name, collective_id, body, parts,
                           [jax.ShapeDtypeStruct((3, *p.shape[1:]), p.dtype) for p in parts],
                           [pltpu.SemaphoreType.DMA((3 * n,)), pltpu.SemaphoreType.DMA((3 * n,))])


def _chip_partial(name, core, grad, own_block, own_index, land):
    _, rows, cols = land.shape
    tr = own_block[-2]

    def body(core_ref, a_ref, b_ref, o_ref):
        o_ref[...] = (a_ref[...] + b_ref[...]).astype(BF16)

    spec = pl.BlockSpec((None, tr, cols), lambda q, r, c: (q, r, 0))
    return pl.pallas_call(
        body, name=name, out_shape=jax.ShapeDtypeStruct(land.shape, BF16),
        grid_spec=pltpu.PrefetchScalarGridSpec(
            num_scalar_prefetch=1, grid=(4, rows // tr),
            in_specs=[pl.BlockSpec(own_block, lambda q, r, c: own_index(q, r, c[0])), spec], out_specs=spec),
        compiler_params=_params([((tr, cols), F32)] * 2 + [((tr, cols), BF16)], sem=("arbitrary", "arbitrary")),
    )(core, grad, land)


def _adamw_math(w, g, m, v):
    m = ADAM_B1 * m + (1.0 - ADAM_B1) * g
    v = ADAM_B2 * v + (1.0 - ADAM_B2) * (g * g)
    m_hat = m / (1.0 - ADAM_B1 ** ADAM_STEP)
    v_hat = v / (1.0 - ADAM_B2 ** ADAM_STEP)
    delta = -ADAM_LR * (m_hat / (jnp.sqrt(v_hat) + ADAM_EPS) + ADAM_WD * w)
    return delta, m, v


def _adamw(name, chip, own, landed, w, m, v):
    _, rows, cols = own.shape
    tr = _tile(rows, 512) if rows % 512 == 0 else _tile(rows, 176)

    def body(chip_ref, own_ref, l_ref, w_ref, m_ref, v_ref, g_out, d_out, m_out, v_out):
        g = own_ref[...].astype(F32)
        for s in range(3):
            g = g + l_ref[s].astype(F32)
        delta, m_new, v_new = _adamw_math(w_ref[...], g, m_ref[...], v_ref[...])
        g_out[...] = g
        d_out[...] = delta
        m_out[...] = m_new
        v_out[...] = v_new

    spec = pl.BlockSpec((tr, cols), lambda r, c: (r, 0))
    return pl.pallas_call(
        body, name=name, out_shape=[jax.ShapeDtypeStruct((rows, cols), F32)] * 4,
        grid_spec=pltpu.PrefetchScalarGridSpec(
            num_scalar_prefetch=1, grid=(rows // tr,),
            in_specs=[pl.BlockSpec((None, tr, cols), lambda r, c: (c[0], r, 0)),
                      pl.BlockSpec((3, tr, cols), lambda r, c: (0, r, 0)), spec, spec, spec],
            out_specs=[spec] * 4),
        compiler_params=_params([((4, tr, cols), own.dtype)] + [((tr, cols), F32)] * 7, sem=("arbitrary",)),
    )(chip, own, landed, w, m, v)


def _swap_with_sibling(name, x):
    def body(x_ref, o_ref, send, recv):
        px, py, c = _place()
        cp = pltpu.make_async_remote_copy(src_ref=x_ref, dst_ref=o_ref, send_sem=send, recv_sem=recv,
                                          device_id=(px, py, 1 - c), device_id_type=MESH_ID)
        cp.start()
        cp.wait()

    return pl.pallas_call(
        body, name=name, out_shape=jax.ShapeDtypeStruct(x.shape, x.dtype), in_specs=[ANY], out_specs=ANY,
        scratch_shapes=[pltpu.SemaphoreType.DMA, pltpu.SemaphoreType.DMA],
    )(x)


def _proj_forward_own_chip(positions, x, gain, w_own, w_sibling):
    t = x.shape[0]
    tm = _tile(t, 1024)

    def body(pos_ref, x_ref, g_ref, wo_ref, ws_ref, o_ref, h_ref, ht_ref):
        @pl.when(pl.program_id(1) == 0)
        def _():
            _, xh = _rms_stats(x_ref[...])
            h = (xh * g_ref[...]).astype(BF16)
            h_ref[...] = h
            ht_ref[...] = h.T
            o_ref[...] = _dot(h, wo_ref[...])

        @pl.when(pl.program_id(1) == 1)
        def _():
            o_ref[...] = _dot(h_ref[...], ws_ref[...])

    tok = pl.BlockSpec((tm, D_MODEL), lambda m, k, pos: (m, 0))
    return pl.pallas_call(
        body, name="proj_fwd_own_chip",
        out_shape=[jax.ShapeDtypeStruct((N_DEV, t, D_MODEL), F32), jax.ShapeDtypeStruct((t, D_MODEL), BF16),
                   jax.ShapeDtypeStruct((D_MODEL, t), BF16)],
        grid_spec=pltpu.PrefetchScalarGridSpec(
            num_scalar_prefetch=1, grid=(t // tm, 2),
            in_specs=[tok, pl.BlockSpec((1, D_MODEL), lambda m, k, pos: (0, 0)), RESIDENT, RESIDENT],
            out_specs=[pl.BlockSpec((None, tm, D_MODEL), lambda m, k, pos: (pos[k], m, 0)), tok,
                       pl.BlockSpec((D_MODEL, tm), lambda m, k, pos: (0, m))]),
        compiler_params=_params([((tm, D_MODEL), F32)] * 2 + [((tm, D_MODEL), BF16)] * 2,
                                scratch=[((2, D_MODEL, D_MODEL), BF16)], temps=6 << 20, sem=("arbitrary", "arbitrary")),
    )(positions, x, gain, w_own, w_sibling)


def _proj_forward_other_chips(positions, proj, h, w_in_g):
    t = h.shape[0]
    tm = _tile(t, 2048)

    def body(pos_ref, _, h_ref, w_ref, o_ref):
        o_ref[...] = _dot(h_ref[...], w_ref[pos_ref[pl.program_id(1)]])

    return pl.pallas_call(
        body, name="proj_fwd_other_chips", out_shape=jax.ShapeDtypeStruct(proj.shape, F32),
        grid_spec=pltpu.PrefetchScalarGridSpec(
            num_scalar_prefetch=1, grid=(t // tm, N_DEV - 2),
            in_specs=[ANY, pl.BlockSpec((tm, D_MODEL), lambda m, k, pos: (m, 0)), RESIDENT],
            out_specs=pl.BlockSpec((None, tm, D_MODEL), lambda m, k, pos: (pos[k], m, 0))),
        input_output_aliases={1: 0},
        compiler_params=_params([((tm, D_MODEL), F32), ((tm, D_MODEL), BF16)], scratch=[((N_DEV, D_MODEL, D_MODEL), BF16)],
                                temps=6 << 20, sem=("arbitrary", "arbitrary")),
    )(positions, proj, h, w_in_g)


def _masked_ws(ws_ref, g):
    row = lax.broadcasted_iota(jnp.int32, (GMLP_CHUNK, GMLP_CHUNK), 0)
    col = lax.broadcasted_iota(jnp.int32, (GMLP_CHUNK, GMLP_CHUNK), 1)
    return jnp.where(row >= col, ws_ref[g], 0.0).astype(BF16)


def _gmlp_forward(proj, ln_g, ln_b, w_s, bias_b):
    t = proj.shape[1]
    tm = _tile(t, 512)
    chunks = tm // GMLP_CHUNK

    def body(u_ref, v_ref, lng_ref, lnb_ref, ws_ref, bias_ref, a_ref, vn_scr):
        vv = _gelu(v_ref[...])
        mu = jnp.mean(vv, axis=-1, keepdims=True)
        cen = vv - mu
        var = jnp.mean(cen * cen, axis=-1, keepdims=True)
        vn_scr[...] = ((cen * lax.rsqrt(var + NORM_EPS)) * lng_ref[...] + lnb_ref[...]).astype(BF16)
        for g in range(GROUPS):
            wm = _masked_ws(ws_ref, g)
            cols = slice(g * HEAD_DIM, (g + 1) * HEAD_DIM)
            for c in range(chunks):
                rows = slice(c * GMLP_CHUNK, (c + 1) * GMLP_CHUNK)
                mixed = _dot(wm, vn_scr[rows, cols]) + bias_ref[g]
                a_ref[rows, cols] = (_gelu(u_ref[rows, cols]) * mixed).astype(BF16)

    small = pl.BlockSpec((GROUPS, GMLP_CHUNK, GMLP_CHUNK), lambda m: (0, 0, 0))
    vec = pl.BlockSpec((1, D_MODEL), lambda m: (0, 0))
    return pl.pallas_call(
        body, name="gmlp_fwd", out_shape=jax.ShapeDtypeStruct((t, D_MODEL), BF16), grid=(t // tm,),
        in_specs=[pl.BlockSpec((None, tm, D_MODEL), lambda m: (U_POS, m, 0)),
                  pl.BlockSpec((None, tm, D_MODEL), lambda m: (U_POS + 1, m, 0)), vec, vec, small, small],
        out_specs=pl.BlockSpec((tm, D_MODEL), lambda m: (m, 0)),
        scratch_shapes=[pltpu.VMEM((tm, D_MODEL), BF16)],
        compiler_params=_params([((tm, D_MODEL), F32)] * 2 + [((tm, D_MODEL), BF16)] + [((8, 128, 128), F32)] * 2,
                                scratch=[((tm, D_MODEL), BF16)], temps=8 << 20, sem=("arbitrary",)),
    )(proj, proj, ln_g, ln_b, w_s, bias_b)


def _lower_bound(tab_ref):
    t0, t1 = tab_ref[0:1, :], tab_ref[1:2, :]
    mx = jnp.maximum(t0, t1)
    e0, e1 = jnp.exp(t0 - mx), jnp.exp(t1 - mx)
    return e0 / (e0 + e1)


def _tri_masks():
    row = lax.broadcasted_iota(jnp.int32, (HGRN_CHUNK, HGRN_CHUNK), 0)
    col = lax.broadcasted_iota(jnp.int32, (HGRN_CHUNK, HGRN_CHUNK), 1)
    return row >= col, row <= col


def _chunk_rows(c):
    return slice(c * HGRN_CHUNK, (c + 1) * HGRN_CHUNK)


def _per_chunk(x, nc, fn):
    return jnp.concatenate([fn(x[_chunk_rows(c)]) for c in range(nc)], axis=0)


def _chunk_row_bcast(x, nc, i):
    return _per_chunk(x, nc, lambda xc: jnp.broadcast_to(xc[i:i + 1, :], (HGRN_CHUNK, HEAD_DIM)))


def _hgrn_gates(q, fl, lb, nc):
    lower, _ = _tri_masks()
    lower = lower.astype(BF16)
    s = _sigmoid(fl)
    f = lb + (1.0 - lb) * s
    k = 1.0 - f
    hi, mid, lo = _split3(jnp.log(f))
    a = jnp.concatenate([_dot(lower, hi[_chunk_rows(c)]) + _dot(lower, mid[_chunk_rows(c)]) + _dot(lower, lo[_chunk_rows(c)])
                         for c in range(nc)], axis=0)
    a_mid = _chunk_row_bcast(a, nc, HGRN_CHUNK // 2 - 1)
    a_last = _chunk_row_bcast(a, nc, HGRN_CHUNK - 1)
    qs = q * HGRN_SCALE
    e_in, e_out, e_end, e_all = jnp.exp(a - a_mid), jnp.exp(a_mid - a), jnp.exp(a_last - a), jnp.exp(a)
    decay = [jnp.exp(a[c * HGRN_CHUNK + HGRN_CHUNK - 1:(c + 1) * HGRN_CHUNK, :]) for c in range(nc)]
    return dict(s=s, f=f, k=k, decay=decay, e_in=e_in, e_out=e_out, e_end=e_end, e_all=e_all,
                qi=qs * e_in, ki=k * e_out, kd=k * e_end, qe=qs * e_all)


def _hgrn_forward(proj, lb_table, norm_g):
    t = proj.shape[1]
    tb = _tile(t, 1024)
    nc = tb // HGRN_CHUNK
    n_chunks = t // HGRN_CHUNK

    def body(q_ref, f_ref, i_ref, g_ref, tab_ref, ng_ref, og_ref, o_ref, st_ref, state):
        @pl.when(pl.program_id(1) == 0)
        def _():
            state[...] = jnp.zeros_like(state)

        lower, _ = _tri_masks()
        gt = _hgrn_gates(q_ref[...], f_ref[...], _lower_bound(tab_ref), nc)
        qi, ki, kd, qe = (gt[n].astype(BF16) for n in ("qi", "ki", "kd", "qe"))
        vb = i_ref[...].astype(BF16)
        o_intra, d_state = [], []
        for c in range(nc):
            rows = _chunk_rows(c)
            p = jnp.where(lower, _dot_nt(qi[rows], ki[rows]), 0.0).astype(BF16)
            o_intra.append(_dot(p, vb[rows]))
            d_state.append(_dot_tn(vb[rows], kd[rows]))
        st = state[...]
        outs = []
        for c in range(nc):
            st_ref[c] = st
            outs.append(o_intra[c] + _dot_nt(qe[_chunk_rows(c)], st.astype(BF16)))
            st = st * gt["decay"][c] + d_state[c]
        state[...] = st
        o = jnp.concatenate(outs, axis=0)
        o_ref[...] = o
        _, oh = _rms_stats(o)
        gz = g_ref[...]
        og_ref[...] = ((oh * ng_ref[...]) * (gz * _sigmoid(gz))).astype(BF16)

    def blk(p):
        return pl.BlockSpec((None, tb, HEAD_DIM), lambda h, n: (p, n, h))

    out_blk = pl.BlockSpec((tb, HEAD_DIM), lambda h, n: (n, h))
    return pl.pallas_call(
        body, name="hgrn_fwd",
        out_shape=[jax.ShapeDtypeStruct((t, D_MODEL), BF16), jax.ShapeDtypeStruct((t, D_MODEL), F32),
                   jax.ShapeDtypeStruct((HEADS, n_chunks, HEAD_DIM, HEAD_DIM), F32)],
        grid=(HEADS, t // tb),
        in_specs=[blk(Q_POS), blk(Q_POS + 1), blk(Q_POS + 2), blk(Q_POS + 3),
                  pl.BlockSpec((2, HEAD_DIM), lambda h, n: (0, h)), pl.BlockSpec((1, HEAD_DIM), lambda h, n: (0, h))],
        out_specs=[out_blk, out_blk, pl.BlockSpec((None, nc, HEAD_DIM, HEAD_DIM), lambda h, n: (h, n, 0, 0))],
        scratch_shapes=[pltpu.VMEM((HEAD_DIM, HEAD_DIM), F32)],
        compiler_params=_params([((tb, HEAD_DIM), F32)] * 6 + [((nc, HEAD_DIM, HEAD_DIM), F32)], temps=8 << 20,
                                sem=("arbitrary", "arbitrary")),
    )(proj, proj, proj, proj, lb_table, norm_g)


def _branch_out_forward(a, og, proj, x, w_a, w_b, w_out, ffn_g):
    t = x.shape[0]
    tm = _tile(t, 512)

    def body(a_ref, og_ref, ga_ref, gb_ref, x_ref, wa_ref, wb_ref, wo_ref, g_ref, ya_ref, yb_ref, mg_ref, x1_ref, h2_ref):
        ya = _dot(a_ref[...], wa_ref[...])
        yb = _dot(og_ref[...], wb_ref[...])
        ya_ref[...] = ya
        yb_ref[...] = yb
        merged = (_sigmoid(ga_ref[...]) * ya + _sigmoid(gb_ref[...]) * yb).astype(BF16)
        mg_ref[...] = merged
        x1 = x_ref[...] + _dot(merged, wo_ref[...])
        x1_ref[...] = x1
        _, xh = _rms_stats(x1)
        h2_ref[...] = (xh * g_ref[...]).astype(BF16)

    tok = pl.BlockSpec((tm, D_MODEL), lambda m: (m, 0))
    return pl.pallas_call(
        body, name="branch_out_fwd",
        out_shape=[jax.ShapeDtypeStruct((t, D_MODEL), F32), jax.ShapeDtypeStruct((t, D_MODEL), F32),
                   jax.ShapeDtypeStruct((t, D_MODEL), BF16), jax.ShapeDtypeStruct((t, D_MODEL), F32),
                   jax.ShapeDtypeStruct((t, D_MODEL), BF16)],
        grid=(t // tm,),
        in_specs=[tok, tok, pl.BlockSpec((None, tm, D_MODEL), lambda m: (GATE_POS, m, 0)),
                  pl.BlockSpec((None, tm, D_MODEL), lambda m: (GATE_POS + 1, m, 0)), tok, RESIDENT, RESIDENT, RESIDENT,
                  pl.BlockSpec((1, D_MODEL), lambda m: (0, 0))],
        out_specs=[tok] * 5,
        compiler_params=_params([((tm, D_MODEL), BF16)] * 4 + [((tm, D_MODEL), F32)] * 6, scratch=[((D_MODEL, D_MODEL), BF16)] * 3,
                                temps=8 << 20, sem=("arbitrary",)),
    )(a, og, proj, proj, x, w_a, w_b, w_out, ffn_g)


def _ffn_forward(h2, x1, w_gu, w_down, target, final_g):
    t = x1.shape[0]
    tm = _tile(t, 512)

    def body(h_ref, wgu_ref, wd_ref, x1_ref, t_ref, g_ref, gu_ref, act_ref, loss_ref, dg_ref, dx_ref, dxb_ref, acts):
        m, j = pl.program_id(0), pl.program_id(1)

        @pl.when((m == 0) & (j == 0))
        def _():
            loss_ref[...] = jnp.zeros_like(loss_ref)
            dg_ref[...] = jnp.zeros_like(dg_ref)

        h = h_ref[...]
        gate = _dot_nt(h, wgu_ref[j])
        up = _dot_nt(h, wgu_ref[j + 4])
        gu_ref[0] = gate
        gu_ref[1] = up
        act = ((gate * _sigmoid(gate)) * up).astype(BF16)
        act_ref[...] = act

        @pl.when(j < 3)
        def _():
            acts[j] = act

        @pl.when(j == 3)
        def _():
            down = _dot(acts[0], wd_ref[0]) + _dot(acts[1], wd_ref[1]) + _dot(acts[2], wd_ref[2]) + _dot(act, wd_ref[3])
            x2 = x1_ref[...] + down
            g = g_ref[...]
            r, xh = _rms_stats(x2)
            err = xh * g - t_ref[...]
            loss_ref[...] += 0.5 * jnp.sum(jnp.mean(err * err, axis=-1, keepdims=True), axis=0, keepdims=True)
            dy = err * (1.0 / D_MODEL)
            dg_ref[...] += jnp.sum(dy * xh, axis=0, keepdims=True)
            dxh = dy * g
            dx = r * (dxh - xh * jnp.mean(dxh * xh, axis=-1, keepdims=True))
            dx_ref[...] = dx
            dxb_ref[...] = dx.astype(BF16)

    tok = pl.BlockSpec((tm, D_MODEL), lambda m, j: (m, 0))
    vec = pl.BlockSpec((1, D_MODEL), lambda m, j: (0, 0))
    return pl.pallas_call(
        body, name="ffn_fwd",
        out_shape=[jax.ShapeDtypeStruct((4, 2, t, FF_BLOCK), F32), jax.ShapeDtypeStruct((4, t, FF_BLOCK), BF16),
                   jax.ShapeDtypeStruct((8, 128), F32), jax.ShapeDtypeStruct((1, D_MODEL), F32),
                   jax.ShapeDtypeStruct((t, D_MODEL), F32), jax.ShapeDtypeStruct((t, D_MODEL), BF16)],
        grid=(t // tm, 4),
        in_specs=[tok, RESIDENT, RESIDENT, tok, tok, vec],
        out_specs=[pl.BlockSpec((None, 2, tm, FF_BLOCK), lambda m, j: (j, 0, m, 0)),
                   pl.BlockSpec((None, tm, FF_BLOCK), lambda m, j: (j, m, 0)),
                   pl.BlockSpec((8, 128), lambda m, j: (0, 0)), vec, tok, tok],
        scratch_shapes=[pltpu.VMEM((3, tm, FF_BLOCK), BF16)],
        compiler_params=_params([((tm, D_MODEL), BF16), ((tm, D_MODEL), F32), ((tm, D_MODEL), F32), ((2, tm, 768), F32),
                                 ((tm, 768), BF16), ((tm, D_MODEL), F32), ((tm, D_MODEL), BF16)],
                                scratch=[((3, tm, 768), BF16), ((N_DEV, FF_BLOCK, D_MODEL), BF16), ((D_FF, D_MODEL), BF16)],
                                temps=6 << 20, sem=("arbitrary", "arbitrary")),
    )(h2, w_gu, w_down.reshape(4, FF_BLOCK, D_MODEL), x1, target, final_g)


def _ffn_backward(dx2b, dx2, gu, x1, w_gu, w_down, ffn_g):
    t = x1.shape[0]
    tm = _tile(t, 512)

    def body(dxb_ref, dx2_ref, gu_ref, x1_ref, wgu_ref, wd_ref, g_ref, dgu_ref, dx1_ref, dx1b_ref, dg_ref, acc, prev):
        m, j = pl.program_id(0), pl.program_id(1)

        @pl.when((m == 0) & (j == 0))
        def _():
            dg_ref[...] = jnp.zeros_like(dg_ref)

        @pl.when(j == 0)
        def _():
            prev[...] = jnp.zeros_like(prev)
            acc[...] = jnp.zeros_like(acc)

        jm1 = jnp.maximum(j - 1, 0)
        acc[...] += _dot(prev[0], wgu_ref[jm1]) + _dot(prev[1], wgu_ref[jm1 + 4])
        dact = _dot_nt(dxb_ref[...], wd_ref[j])
        gate, up = gu_ref[0], gu_ref[1]
        sg = _sigmoid(gate)
        dgate = (dact * up * (sg * (1.0 + gate * (1.0 - sg)))).astype(BF16)
        dup = (dact * (gate * sg)).astype(BF16)
        dgu_ref[0] = dgate
        dgu_ref[1] = dup
        prev[0] = dgate
        prev[1] = dup

        @pl.when(j == 3)
        def _():
            dh2 = acc[...] + (_dot(prev[0], wgu_ref[3]) + _dot(prev[1], wgu_ref[7]))
            dx, dg = _rms_bwd(dh2, x1_ref[...], g_ref[...])
            dx1 = dx2_ref[...] + dx
            dx1_ref[...] = dx1
            dx1b_ref[...] = dx1.astype(BF16)
            dg_ref[...] += dg

    tok = pl.BlockSpec((tm, D_MODEL), lambda m, j: (m, 0))
    vec = pl.BlockSpec((1, D_MODEL), lambda m, j: (0, 0))
    gu_spec = pl.BlockSpec((None, 2, tm, FF_BLOCK), lambda m, j: (j, 0, m, 0))
    return pl.pallas_call(
        body, name="ffn_bwd",
        out_shape=[jax.ShapeDtypeStruct((4, 2, t, FF_BLOCK), BF16), jax.ShapeDtypeStruct((t, D_MODEL), F32),
                   jax.ShapeDtypeStruct((t, D_MODEL), BF16), jax.ShapeDtypeStruct((1, D_MODEL), F32)],
        grid=(t // tm, 4),
        in_specs=[tok, tok, gu_spec, tok, RESIDENT, RESIDENT, vec],
        out_specs=[gu_spec, tok, tok, vec],
        scratch_shapes=[pltpu.VMEM((tm, D_MODEL), F32), pltpu.VMEM((2, tm, FF_BLOCK), BF16)],
        compiler_params=_params([((tm, D_MODEL), BF16), ((tm, D_MODEL), F32), ((2, tm, 768), F32), ((tm, D_MODEL), F32),
                                 ((2, tm, 768), BF16), ((tm, D_MODEL), F32), ((tm, D_MODEL), BF16)],
                                scratch=[((tm, D_MODEL), F32), ((2, tm, 768), BF16), ((N_DEV, FF_BLOCK, D_MODEL), BF16),
                                         ((D_FF, D_MODEL), BF16)],
                                temps=4 << 20, sem=("arbitrary", "arbitrary")),
    )(dx2b, dx2, gu, x1, w_gu, w_down.reshape(4, FF_BLOCK, D_MODEL), ffn_g)


def _branch_out_backward(dx1b, ya, yb, proj, w_a, w_b, w_out):
    t = ya.shape[0]
    tm = _tile(t, 512)

    def body(dx_ref, ya_ref, yb_ref, ga_ref, gb_ref, wa_ref, wb_ref, wo_ref, dya_ref, dyb_ref, dgate_ref, da_ref, dog_ref):
        dm = _dot_nt(dx_ref[...], wo_ref[...])
        sa, sb = _sigmoid(ga_ref[...]), _sigmoid(gb_ref[...])
        dya = (dm * sa).astype(BF16)
        dyb = (dm * sb).astype(BF16)
        dya_ref[...] = dya
        dyb_ref[...] = dyb
        dgate_ref[0] = (dm * ya_ref[...] * (sa * (1.0 - sa))).astype(BF16)
        dgate_ref[1] = (dm * yb_ref[...] * (sb * (1.0 - sb))).astype(BF16)
        da_ref[...] = _dot_nt(dya, wa_ref[...])
        dog_ref[...] = _dot_nt(dyb, wb_ref[...])

    tok = pl.BlockSpec((tm, D_MODEL), lambda m: (m, 0))
    return pl.pallas_call(
        body, name="branch_out_bwd",
        out_shape=[jax.ShapeDtypeStruct((t, D_MODEL), BF16), jax.ShapeDtypeStruct((t, D_MODEL), BF16),
                   jax.ShapeDtypeStruct((N_DEV, t, D_MODEL), BF16), jax.ShapeDtypeStruct((t, D_MODEL), F32),
                   jax.ShapeDtypeStruct((t, D_MODEL), F32)],
        grid=(t // tm,),
        in_specs=[tok, tok, tok, pl.BlockSpec((None, tm, D_MODEL), lambda m: (GATE_POS, m, 0)),
                  pl.BlockSpec((None, tm, D_MODEL), lambda m: (GATE_POS + 1, m, 0)), RESIDENT, RESIDENT, RESIDENT],
        out_specs=[tok, tok, pl.BlockSpec((2, tm, D_MODEL), lambda m: (GATE_POS // 2, m, 0)), tok, tok],
        compiler_params=_params([((tm, D_MODEL), BF16)] * 5 + [((tm, D_MODEL), F32)] * 6, scratch=[((D_MODEL, D_MODEL), BF16)] * 3,
                                temps=8 << 20, sem=("arbitrary",)),
    )(dx1b, ya, yb, proj, proj, w_a, w_b, w_out)


def _hgrn_backward(dproj, dog, o_saved, states, proj, lb_table, norm_g):
    t = proj.shape[1]
    tb = _tile(t, 1024)
    nc = tb // HGRN_CHUNK
    nb = t // tb

    def body(_, dog_ref, o_ref, st_ref, q_ref, f_ref, i_ref, g_ref, tab_ref, ng_ref, dp_ref, dng_ref, dtab_ref, gstate):
        @pl.when(pl.program_id(1) == 0)
        def _():
            gstate[...] = jnp.zeros_like(gstate)
            dng_ref[...] = jnp.zeros_like(dng_ref)
            dtab_ref[...] = jnp.zeros_like(dtab_ref)

        lb = _lower_bound(tab_ref)
        ng = ng_ref[...]
        lower, upper = _tri_masks()
        gt = _hgrn_gates(q_ref[...], f_ref[...], lb, nc)
        qi, ki, kd, qe = (gt[n].astype(BF16) for n in ("qi", "ki", "kd", "qe"))
        vb = i_ref[...].astype(BF16)
        o, gz, d_og = o_ref[...], g_ref[...], dog_ref[...]
        r, oh = _rms_stats(o)
        sg = _sigmoid(gz)
        d_on = d_og * (gz * sg)
        dgz = d_og * (oh * ng) * (sg * (1.0 + gz * (1.0 - sg)))
        dng_ref[...] += jnp.sum(d_on * oh, axis=0, keepdims=True)
        doh = d_on * ng
        dob = (r * (doh - oh * jnp.mean(doh * oh, axis=-1, keepdims=True))).astype(BF16)
        dv_intra, dqi, dki, dqe, g_upd = [], [], [], [], []
        for c in range(nc):
            rows = _chunk_rows(c)
            p = jnp.where(lower, _dot_nt(qi[rows], ki[rows]), 0.0).astype(BF16)
            dv_intra.append(_dot_tn(p, dob[rows]))
            dp = jnp.where(lower, _dot_nt(dob[rows], vb[rows]), 0.0).astype(BF16)
            dqi.append(_dot(dp, ki[rows]))
            dki.append(_dot_tn(dp, qi[rows]))
            dqe.append(_dot(dob[rows], st_ref[c].astype(BF16)))
            g_upd.append(_dot_tn(dob[rows], qe[rows]))
        g_after = [None] * nc
        g = gstate[...]
        for c in reversed(range(nc)):
            g_after[c] = g
            g = g * gt["decay"][c] + g_upd[c]
        gstate[...] = g
        dkd, dv, da_last = [], [], []
        for c in range(nc):
            rows = _chunk_rows(c)
            gb = g_after[c].astype(BF16)
            dkd.append(_dot(vb[rows], gb))
            dv.append(dv_intra[c] + _dot_nt(kd[rows], gb))
            da_last.append(jnp.sum(g_after[c] * st_ref[c], axis=0, keepdims=True) * gt["decay"][c])
        dqi, dki, dqe, dkd, dv = (jnp.concatenate(z, axis=0) for z in (dqi, dki, dqe, dkd, dv))
        dqs = dqi * gt["e_in"] + dqe * gt["e_all"]
        dk = dki * gt["e_out"] + dkd * gt["e_end"]
        t_in, t_out, t_end = dqi * gt["qi"], dki * gt["ki"], dkd * gt["kd"]
        da = t_in - t_out + dqe * gt["qe"] - t_end
        row = lax.broadcasted_iota(jnp.int32, (HGRN_CHUNK, HEAD_DIM), 0)
        d_mid = t_out - t_in
        pieces = []
        for c in range(nc):
            rows = _chunk_rows(c)
            da_mid = jnp.sum(d_mid[rows], axis=0, keepdims=True)
            da_end = jnp.sum(t_end[rows], axis=0, keepdims=True) + da_last[c]
            da_c = da[rows] + jnp.where(row == HGRN_CHUNK // 2 - 1, da_mid, 0.0) + jnp.where(row == HGRN_CHUNK - 1, da_end, 0.0)
            pieces.append(_mask_mm(upper.astype(BF16), da_c))
        df = jnp.concatenate(pieces, axis=0) / gt["f"] - dk
        s = gt["s"]
        dlb = jnp.sum(df * (1.0 - s), axis=0, keepdims=True)
        dp_ref[0] = (dqs * HGRN_SCALE).astype(BF16)
        dp_ref[1] = (df * (1.0 - lb) * (s * (1.0 - s))).astype(BF16)
        dp_ref[2] = dv.astype(BF16)
        dp_ref[3] = dgz.astype(BF16)
        dt0 = dlb * (lb * (1.0 - lb))
        dtab_ref[0:1, :] += dt0
        dtab_ref[1:2, :] -= dt0

    def blk(p):
        return pl.BlockSpec((None, tb, HEAD_DIM), lambda h, n: (p, nb - 1 - n, h))

    tok = pl.BlockSpec((tb, HEAD_DIM), lambda h, n: (nb - 1 - n, h))
    return pl.pallas_call(
        body, name="hgrn_bwd",
        out_shape=[jax.ShapeDtypeStruct((N_DEV, t, D_MODEL), BF16), jax.ShapeDtypeStruct((1, D_MODEL), F32),
                   jax.ShapeDtypeStruct((2, D_MODEL), F32)],
        grid=(HEADS, nb),
        in_specs=[ANY, tok, tok, pl.BlockSpec((None, nc, HEAD_DIM, HEAD_DIM), lambda h, n: (h, nb - 1 - n, 0, 0)),
                  blk(Q_POS), blk(Q_POS + 1), blk(Q_POS + 2), blk(Q_POS + 3),
                  pl.BlockSpec((2, HEAD_DIM), lambda h, n: (0, h)), pl.BlockSpec((1, HEAD_DIM), lambda h, n: (0, h))],
        out_specs=[pl.BlockSpec((4, tb, HEAD_DIM), lambda h, n: (0, nb - 1 - n, h)),
                   pl.BlockSpec((1, HEAD_DIM), lambda h, n: (0, h)), pl.BlockSpec((2, HEAD_DIM), lambda h, n: (0, h))],
        scratch_shapes=[pltpu.VMEM((HEAD_DIM, HEAD_DIM), F32)],
        input_output_aliases={0: 0},
        compiler_params=_params([((tb, HEAD_DIM), F32)] * 6 + [((nc, HEAD_DIM, HEAD_DIM), F32)] + [((4, tb, HEAD_DIM), BF16)],
                                temps=8 << 20, sem=("arbitrary", "arbitrary")),
    )(dproj, dog, o_saved, states, proj, proj, proj, proj, lb_table, norm_g)


def _gmlp_backward(dproj, da, proj, ln_g, ln_b, w_s, bias_b):
    t = proj.shape[1]
    tm = _tile(t, 256)
    chunks = tm // GMLP_CHUNK

    def body(_, da_ref, u_ref, v_ref, lng_ref, lnb_ref, ws_ref, bias_ref, dp_ref, dlng_ref, dlnb_ref, dws_ref, dbs_ref,
             vn_scr, dvn_scr):
        @pl.when(pl.program_id(0) == 0)
        def _():
            dlng_ref[...] = jnp.zeros_like(dlng_ref)
            dlnb_ref[...] = jnp.zeros_like(dlnb_ref)
            dws_ref[...] = jnp.zeros_like(dws_ref)
            dbs_ref[...] = jnp.zeros_like(dbs_ref)

        v = v_ref[...]
        vv, dvv_dv = _gelu_and_grad(v)
        mu = jnp.mean(vv, axis=-1, keepdims=True)
        cen = vv - mu
        rstd = lax.rsqrt(jnp.mean(cen * cen, axis=-1, keepdims=True) + NORM_EPS)
        vhat = cen * rstd
        lng = lng_ref[...]
        vn_scr[...] = (vhat * lng + lnb_ref[...]).astype(BF16)
        row = lax.broadcasted_iota(jnp.int32, (GMLP_CHUNK, GMLP_CHUNK), 0)
        col = lax.broadcasted_iota(jnp.int32, (GMLP_CHUNK, GMLP_CHUNK), 1)
        for g in range(GROUPS):
            wm = _masked_ws(ws_ref, g)
            cols = slice(g * HEAD_DIM, (g + 1) * HEAD_DIM)
            dws = jnp.zeros((GMLP_CHUNK, GMLP_CHUNK), F32)
            dbs = jnp.zeros((GMLP_CHUNK, GMLP_CHUNK), F32)
            for c in range(chunks):
                rows = slice(c * GMLP_CHUNK, (c + 1) * GMLP_CHUNK)
                vn = vn_scr[rows, cols]
                mixed = _dot(wm, vn) + bias_ref[g]
                u = u_ref[rows, cols]
                d_a = da_ref[rows, cols]
                gelu_u, dgelu_u = _gelu_and_grad(u)
                dp_ref[0, rows, cols] = (d_a * mixed * dgelu_u).astype(BF16)
                dmix = d_a * gelu_u
                dmb = dmix.astype(BF16)
                dbs = dbs + dmix
                dws = dws + _dot_nt(dmb, vn)
                dvn_scr[rows, cols] = _dot_tn(wm, dmb)
            dws_ref[g] += jnp.where(row >= col, dws, 0.0)
            dbs_ref[g] += jnp.broadcast_to(jnp.sum(dbs, axis=-1, keepdims=True), (GMLP_CHUNK, GMLP_CHUNK))
        dvn = dvn_scr[...]
        dlng_ref[...] += jnp.sum(dvn * vhat, axis=0, keepdims=True)
        dlnb_ref[...] += jnp.sum(dvn, axis=0, keepdims=True)
        dvh = dvn * lng
        dvv = rstd * (dvh - jnp.mean(dvh, axis=-1, keepdims=True) - vhat * jnp.mean(dvh * vhat, axis=-1, keepdims=True))
        dp_ref[1] = (dvv * dvv_dv).astype(BF16)

    tok = pl.BlockSpec((tm, D_MODEL), lambda m: (m, 0))
    small = pl.BlockSpec((GROUPS, GMLP_CHUNK, GMLP_CHUNK), lambda m: (0, 0, 0))
    vec = pl.BlockSpec((1, D_MODEL), lambda m: (0, 0))
    return pl.pallas_call(
        body, name="gmlp_bwd",
        out_shape=[jax.ShapeDtypeStruct(dproj.shape, BF16), jax.ShapeDtypeStruct((1, D_MODEL), F32),
                   jax.ShapeDtypeStruct((1, D_MODEL), F32), jax.ShapeDtypeStruct((GROUPS, GMLP_CHUNK, GMLP_CHUNK), F32),
                   jax.ShapeDtypeStruct((GROUPS, GMLP_CHUNK, GMLP_CHUNK), F32)],
        grid=(t // tm,),
        in_specs=[ANY, tok, pl.BlockSpec((None, tm, D_MODEL), lambda m: (U_POS, m, 0)),
                  pl.BlockSpec((None, tm, D_MODEL), lambda m: (U_POS + 1, m, 0)), vec, vec, small, small],
        out_specs=[pl.BlockSpec((2, tm, D_MODEL), lambda m: (U_POS // 2, m, 0)), vec, vec, small, small],
        scratch_shapes=[pltpu.VMEM((tm, D_MODEL), BF16), pltpu.VMEM((tm, D_MODEL), F32)],
        input_output_aliases={0: 0},
        compiler_params=_params([((tm, D_MODEL), F32)] * 3 + [((2, tm, D_MODEL), BF16)] + [((8, 128, 128), F32)] * 4,
                                scratch=[((tm, D_MODEL), BF16), ((tm, D_MODEL), F32)], temps=12 << 20, sem=("arbitrary",)),
    )(dproj, da, proj, proj, ln_g, ln_b, w_s, bias_b)


def _input_backward(dproj, w_in_g, x, dx1, mix_g):
    t = x.shape[0]
    tm = _tile(t, 512)

    def body(dp_ref, w_ref, x_ref, dx1_ref, g_ref, dx_ref, dg_ref):
        @pl.when(pl.program_id(0) == 0)
        def _():
            dg_ref[...] = jnp.zeros_like(dg_ref)

        dh = _dot_nt(dp_ref[0], w_ref[0])
        for p in range(1, N_DEV):
            dh = dh + _dot_nt(dp_ref[p], w_ref[p])
        dx, dg = _rms_bwd(dh, x_ref[...], g_ref[...])
        dx_ref[...] = dx1_ref[...] + dx
        dg_ref[...] += dg

    tok = pl.BlockSpec((tm, D_MODEL), lambda m: (m, 0))
    vec = pl.BlockSpec((1, D_MODEL), lambda m: (0, 0))
    return pl.pallas_call(
        body, name="input_bwd",
        out_shape=[jax.ShapeDtypeStruct((t, D_MODEL), F32), jax.ShapeDtypeStruct((1, D_MODEL), F32)],
        grid=(t // tm,),
        in_specs=[pl.BlockSpec((N_DEV, tm, D_MODEL), lambda m: (0, m, 0)), RESIDENT, tok, tok, vec],
        out_specs=[tok, vec],
        compiler_params=_params([((N_DEV, tm, D_MODEL), BF16)] + [((tm, D_MODEL), F32)] * 3,
                                scratch=[((N_DEV, D_MODEL, D_MODEL), BF16)], temps=6 << 20, sem=("arbitrary",)),
    )(dproj, w_in_g, x, dx1, mix_g)


def _weight_grad(name, a, b, a_spec, b_spec, out_shape, out_spec, steps, blocks, a_is_transposed):
    def body(a_ref, b_ref, o_ref):
        o_ref[...] = _dot(a_ref[...], b_ref[...]) if a_is_transposed else _dot_tn(a_ref[...], b_ref[...])

    return pl.pallas_call(
        body, name=name, out_shape=jax.ShapeDtypeStruct(out_shape, F32), grid=(steps,), in_specs=[a_spec, b_spec],
        out_specs=out_spec, compiler_params=_params(blocks, temps=4 << 20, sem=("arbitrary",)),
    )(a, b)


def _pack_small(mix_g, ln_g, ln_b, b_s, lb_table, hg_norm, ffn_g, final_g, loss_row):
    def part(a):
        a = a.reshape(-1, D_MODEL)
        return jnp.pad(a, ((0, 8 - a.shape[0]), (0, 0)))

    return jnp.concatenate([part(mix_g), part(ln_g), part(ln_b), part(hg_norm), part(ffn_g), part(final_g),
                            part(lb_table), part(b_s), part(loss_row)], axis=0)


SMALL_PARTS = (("gmlp_ln_g", 8, 1), ("gmlp_ln_b", 16, 1), ("hgrn_norm_g", 24, 1), ("norm_ffn_g", 32, 1), ("norm_final_g", 40, 1),
               ("hgrn_lb_table", 48, 2))


def _adamw_small_unpacked(gathered, w, m, v):
    rows = w.shape[0]
    n_out = len(SMALL_PARTS) + 1

    def body(p_ref, w_ref, m_ref, v_ref, *outs):
        g = p_ref[0]
        for j in range(1, N_DEV):
            g = g + p_ref[j]
        delta, m_new, v_new = _adamw_math(w_ref[...], g, m_ref[...], v_ref[...])
        for kind, val in enumerate((g, delta, m_new, v_new)):
            refs = outs[kind * n_out:(kind + 1) * n_out]
            for (_, first, count), ref in zip(SMALL_PARTS, refs):
                ref[...] = val[first:first + count]
            for grp in range(GROUPS):
                refs[-1][0, grp:grp + 1, :] = val[56:57, grp * GMLP_CHUNK:(grp + 1) * GMLP_CHUNK]
        outs[-1][...] = g[SMALL_ROWS - 8:SMALL_ROWS - 7]

    shapes = [jax.ShapeDtypeStruct((count, D_MODEL), F32) for _, _, count in SMALL_PARTS]
    shapes.append(jax.ShapeDtypeStruct((1, GROUPS, GMLP_CHUNK), F32))
    whole = pl.BlockSpec((rows, D_MODEL), lambda: (0, 0))
    res = pl.pallas_call(
        body, name="adamw_small", out_shape=shapes * 4 + [jax.ShapeDtypeStruct((1, D_MODEL), F32)],
        in_specs=[pl.BlockSpec((N_DEV, rows, D_MODEL), lambda: (0, 0, 0)), whole, whole, whole],
        compiler_params=_params([((N_DEV, rows, D_MODEL), F32)] + [((rows, D_MODEL), F32)] * 7),
    )(gathered, w, m, v)
    names = [nme for nme, _, _ in SMALL_PARTS] + ["gmlp_b_s"]
    return [dict(zip(names, res[kind * n_out:(kind + 1) * n_out])) for kind in range(4)], res[-1]


def _adamw_row(name, gathered, w, m, v):
    def body(p_ref, w_ref, m_ref, v_ref, g_out, d_out, m_out, v_out):
        g = p_ref[0, 0:1, :]
        for j in range(1, N_DEV):
            g = g + p_ref[j, 0:1, :]
        delta, m_new, v_new = _adamw_math(w_ref[...], g, m_ref[...], v_ref[...])
        g_out[...] = g
        d_out[...] = delta
        m_out[...] = m_new
        v_out[...] = v_new

    return pl.pallas_call(
        body, name=name, out_shape=[jax.ShapeDtypeStruct((1, D_MODEL), F32)] * 4,
        compiler_params=_params([((N_DEV, 8, D_MODEL), F32)] + [((8, D_MODEL), F32)] * 7),
    )(gathered, w, m, v)


def _adamw_small(name, gathered, w, m, v):
    rows, cols = w.shape

    def body(p_ref, w_ref, m_ref, v_ref, g_out, d_out, m_out, v_out):
        g = p_ref[0]
        for j in range(1, N_DEV):
            g = g + p_ref[j]
        delta, m_new, v_new = _adamw_math(w_ref[...], g, m_ref[...], v_ref[...])
        g_out[...] = g
        d_out[...] = delta
        m_out[...] = m_new
        v_out[...] = v_new

    tr = _tile(rows, 512)
    spec = pl.BlockSpec((tr, cols), lambda r: (r, 0))
    return pl.pallas_call(
        body, name=name, out_shape=[jax.ShapeDtypeStruct((rows, cols), F32)] * 4, grid=(rows // tr,),
        in_specs=[pl.BlockSpec((N_DEV, tr, cols), lambda r: (0, r, 0)), spec, spec, spec], out_specs=[spec] * 4,
        compiler_params=_params([((N_DEV, tr, cols), F32)] + [((tr, cols), F32)] * 7, sem=("arbitrary",)),
    )(gathered, w, m, v)


def kernel(x, norm_mix_g, w_in, gmlp_ln_g, gmlp_ln_b, gmlp_w_s, gmlp_b_s, hgrn_lb_table, hgrn_norm_g, w_branch_a, w_branch_b, w_out, norm_ffn_g, w_gate_up, w_down, norm_final_g, loss_target, m_norm_mix_g, m_w_in, m_gmlp_ln_g, m_gmlp_ln_b, m_gmlp_w_s, m_gmlp_b_s, m_hgrn_lb_table, m_hgrn_norm_g, m_w_branch_a, m_w_branch_b, m_w_out, m_norm_ffn_g, m_w_gate_up, m_w_down, m_norm_final_g, v_norm_mix_g, v_w_in, v_gmlp_ln_g, v_gmlp_ln_b, v_gmlp_w_s, v_gmlp_b_s, v_hgrn_lb_table, v_hgrn_norm_g, v_w_branch_a, v_w_branch_b, v_w_out, v_norm_ffn_g, v_w_gate_up, v_w_down, v_norm_final_g):
    t = x.shape[1]
    x2d = x.reshape(t, D_MODEL)
    target = loss_target.reshape(t, D_MODEL)
    final_g = norm_final_g.reshape(1, D_MODEL)

    w_in_shard = w_in[0].astype(BF16)

    def rows_of(n):
        return lambda ref, j: ref.at[pl.ds(pl.multiple_of(j * n, 8), n)]

    gathered = [((N_DEV, D_MODEL, D_MODEL), BF16), ((D_MODEL, D_MODEL), BF16), ((D_MODEL, D_MODEL), BF16),
                ((D_MODEL, D_MODEL), BF16), ((N_DEV, FF_BLOCK, D_MODEL), BF16), ((D_FF, D_MODEL), BF16)]
    places = [lambda ref, j: ref.at[_pos_of_dev(j)], rows_of(BRANCH_ROWS), rows_of(BRANCH_ROWS), rows_of(BRANCH_ROWS),
              lambda ref, j: ref.at[j], rows_of(DOWN_ROWS)]
    w_in_g = _all_gather_balanced_async("w_in_all_gather", 9, w_in_shard, gathered[0], places[0], D_MODEL)
    w_in_sibling = _swap_with_sibling("w_in_from_sibling", w_in_shard)

    core_i, chip_i = lax.axis_index("c"), 2 * lax.axis_index("x") + lax.axis_index("y")
    own_pos = jnp.stack([_pos_of_dev(2 * chip_i + core_i), _pos_of_dev(2 * chip_i + 1 - core_i)]).astype(jnp.int32)
    other_pos = jnp.stack([_pos_of_dev(2 * jnp.bitwise_xor(chip_i, q) + cc) for q in (1, 2, 3) for cc in (0, 1)]).astype(jnp.int32)
    proj, h, h_t = _proj_forward_own_chip(own_pos, x2d, norm_mix_g, w_in_shard, w_in_sibling)

    small_w = [norm_mix_g, gmlp_ln_g, gmlp_ln_b, gmlp_b_s, hgrn_lb_table, hgrn_norm_g, norm_ffn_g, norm_final_g]
    small_m = [m_norm_mix_g, m_gmlp_ln_g, m_gmlp_ln_b, m_gmlp_b_s, m_hgrn_lb_table, m_hgrn_norm_g, m_norm_ffn_g, m_norm_final_g]
    small_v = [v_norm_mix_g, v_gmlp_ln_g, v_gmlp_ln_b, v_gmlp_b_s, v_hgrn_lb_table, v_hgrn_norm_g, v_norm_ffn_g, v_norm_final_g]
    _, (raw, small_w, small_m, small_v) = lax.optimization_barrier(
        (h, ([w_branch_a[0], w_branch_b[0], w_out[0], w_gate_up[0], w_down[0]], small_w, small_m, small_v)))
    later = [raw[0].astype(BF16), raw[1].astype(BF16), raw[2].astype(BF16),
             raw[3].T.astype(BF16), raw[4].astype(BF16)]
    w_a, w_b, w_o, w_gu, w_dn = _all_gather_async("weights_all_gather", 0, later, gathered[1:], places[1:])
    no_row = jnp.zeros((1, D_MODEL), F32)
    w_pack, m_pack, v_pack = (_pack_small(*vals, no_row) for vals in (small_w, small_m, small_v))
    bias_b = jnp.broadcast_to(small_w[3][0][:, :, None], (GROUPS, GMLP_CHUNK, GMLP_CHUNK))
    h_later, _ = lax.optimization_barrier((h, (later, w_pack, m_pack, v_pack, bias_b)))
    proj = _proj_forward_other_chips(other_pos, proj, h_later, w_in_g)
    a = _gmlp_forward(proj, gmlp_ln_g, gmlp_ln_b, gmlp_w_s[0], bias_b)
    og, o_saved, states = _hgrn_forward(proj, hgrn_lb_table, hgrn_norm_g)
    ya, yb, merged, x1, h2 = _branch_out_forward(a, og, proj, x2d, w_a, w_b, w_o, norm_ffn_g)
    gu, act, loss_tile, d_final_g, dx2, dx2b = _ffn_forward(h2, x1, w_gu, w_dn, target, final_g)

    core = lax.axis_index("c").astype(jnp.int32).reshape(1)
    chip = (2 * lax.axis_index("x") + lax.axis_index("y")).astype(jnp.int32).reshape(1)
    branch_rows, branch_shape = rows_of(BRANCH_ROWS), (BRANCH_ROWS, D_MODEL)
    branch_block = ((BRANCH_ROWS, D_MODEL), lambda q, r, c: (2 * q + c, 0))

    def chip_partials(names, grads, land, own_blocks):
        return [_chip_partial("chip_partial_" + nme, core, g_, blk, idx, l_)
                for nme, g_, (blk, idx), l_ in zip(names, grads, own_blocks, land)]

    whole = pl.BlockSpec((t, D_MODEL), lambda n: (0, 0))
    whole_t = pl.BlockSpec((D_MODEL, t), lambda n: (0, 0))
    col_blocks = [((t, D_MODEL), BF16), ((t, 256), BF16), ((D_MODEL, 256), F32)]

    def square_grad(name, a_, b_):
        return _weight_grad(name, a_, b_, whole, pl.BlockSpec((t, 256), lambda n: (0, n)), (D_MODEL, D_MODEL),
                            pl.BlockSpec((D_MODEL, 256), lambda n: (0, n)), D_MODEL // 256, col_blocks, False)

    dgu, dx1, dx1b, d_ffn_g = _ffn_backward(dx2b, dx2, gu, x1, w_gu, w_dn, norm_ffn_g)
    g_gu = _weight_grad(
        "grad_w_gate_up", dgu, h2, pl.BlockSpec((None, None, t, FF_BLOCK), lambda j: (j % 4, j // 4, 0, 0)), whole,
        (N_DEV, FF_BLOCK, D_MODEL), pl.BlockSpec((None, FF_BLOCK, D_MODEL), lambda j: (j, 0, 0)), N_DEV,
        [((t, 768), BF16), ((t, D_MODEL), BF16), ((FF_BLOCK, D_MODEL), F32)], False)
    g_dn = _weight_grad(
        "grad_w_down", act, dx2b, pl.BlockSpec((None, t, FF_BLOCK), lambda j: (j, 0, 0)), whole, (D_FF, D_MODEL),
        pl.BlockSpec((FF_BLOCK, D_MODEL), lambda j: (j, 0)), 4,
        [((t, 768), BF16), ((t, D_MODEL), BF16), ((FF_BLOCK, D_MODEL), F32)], False)
    names_f, grads_f = ["w_gate_up", "w_down"], [g_gu, g_dn]
    land_f = _exchange_sibling("ffn_grads_to_sibling", 2, grads_f, [lambda ref, j: ref.at[j], rows_of(DOWN_ROWS)],
                               [(FF_BLOCK, D_MODEL), (DOWN_ROWS, D_MODEL)])

    dx1b_later, _ = lax.optimization_barrier((dx1b, grads_f))
    dya, dyb, dproj, da, dog = _branch_out_backward(dx1b_later, ya, yb, proj, w_a, w_b, w_o)
    g_a = square_grad("grad_w_a", a, dya)
    g_b = square_grad("grad_w_b", og, dyb)
    g_o = square_grad("grad_w_out", merged, dx1b)
    names_b, grads_b = ["w_branch_a", "w_branch_b", "w_out"], [g_a, g_b, g_o]
    land_b = _exchange_sibling("branch_grads_to_sibling", 3, grads_b, [branch_rows] * 3, [branch_shape] * 3)

    part_f = chip_partials(names_f, grads_f, land_f,
                           [((None, FF_BLOCK, D_MODEL), lambda q, r, c: (2 * q + c, 0, 0)),
                            ((DOWN_ROWS, D_MODEL), lambda q, r, c: (2 * q + c, 0))])
    landed_f = _exchange_chips("ffn_grads_to_chips", 5, part_f)

    dog, _ = lax.optimization_barrier((dog, part_f))
    dproj, d_hg_norm, d_lb = _hgrn_backward(dproj, dog, o_saved, states, proj, hgrn_lb_table, hgrn_norm_g)

    land_b, _ = lax.optimization_barrier((land_b, part_f))
    part_b = chip_partials(names_b, grads_b, land_b, [branch_block] * 3)
    landed_b = _exchange_chips("branch_grads_to_chips", 6, part_b)

    da, _ = lax.optimization_barrier((da, part_b))
    dproj, d_ln_g, d_ln_b, d_ws, d_bs = _gmlp_backward(dproj, da, proj, gmlp_ln_g, gmlp_ln_b, gmlp_w_s[0], bias_b)

    def flat_ws(a):
        return a.reshape(GROUPS * GMLP_CHUNK, GMLP_CHUNK)

    small_partial = _pack_small(no_row, d_ln_g, d_ln_b, d_bs[:, :, 0], d_lb, d_hg_norm, d_ffn_g, d_final_g,
                                jnp.tile(loss_tile[0:1], (1, D_MODEL // 128)))
    small_all, ws_all = _all_gather_async(
        "small_grads_all_gather", 1, [small_partial, flat_ws(d_ws)],
        [((N_DEV, SMALL_ROWS, D_MODEL), F32), ((N_DEV, GROUPS * GMLP_CHUNK, GMLP_CHUNK), F32)],
        [lambda ref, j: ref.at[j], lambda ref, j: ref.at[j]])

    g_in = _weight_grad(
        "grad_w_in", h_t, dproj, whole_t, pl.BlockSpec((None, t, D_MODEL // 2), lambda n: (n // 2, 0, n % 2)),
        (N_DEV, D_MODEL, D_MODEL), pl.BlockSpec((None, D_MODEL, D_MODEL // 2), lambda n: (n // 2, 0, n % 2)), 2 * N_DEV,
        [((D_MODEL, t), BF16), ((t, D_MODEL // 2), BF16), ((D_MODEL, D_MODEL // 2), F32)], True)
    land_i = _exchange_sibling("w_in_grads_to_sibling", 4, [g_in], [lambda ref, j: ref.at[_pos_of_dev(j)]],
                               [(D_MODEL, D_MODEL)])

    (landed_f, landed_b), _ = lax.optimization_barrier(((landed_f, landed_b), g_in))
    big = {}
    for nme, own, lnd, w, m, v in zip(
            names_f + names_b, part_f + part_b, landed_f + landed_b,
            [w_gate_up, w_down, w_branch_a, w_branch_b, w_out], [m_w_gate_up, m_w_down, m_w_branch_a, m_w_branch_b, m_w_out],
            [v_w_gate_up, v_w_down, v_w_branch_a, v_w_branch_b, v_w_out]):
        flip = (lambda z: z.T) if nme == "w_gate_up" else (lambda z: z)
        big[nme] = [flip(o_)[None] for o_ in _adamw("adamw_" + nme, chip, own, lnd, flip(w[0]), flip(m[0]), flip(v[0]))]
    small, loss_row = _adamw_small_unpacked(small_all, w_pack, m_pack, v_pack)
    ws_outs = _adamw_small("adamw_w_s", ws_all, flat_ws(gmlp_w_s), flat_ws(m_gmlp_w_s), flat_ws(v_gmlp_w_s))
    land_i, _ = lax.optimization_barrier((land_i, (big, small, ws_outs)))
    part_i = chip_partials(["w_in"], [g_in], land_i,
                           [((None, D_MODEL, D_MODEL), lambda q, r, c: (_pos_of_dev(2 * q + c), 0, 0))])
    landed_i = _exchange_chips("w_in_grads_to_chips", 7, part_i)

    dx1, _ = lax.optimization_barrier((dx1, part_i))
    grad_x, d_mix_g = _input_backward(dproj, w_in_g, x2d, dx1, norm_mix_g)
    big["w_in"] = [o_[None] for o_ in _adamw("adamw_w_in", chip, part_i[0], landed_i[0], w_in[0], m_w_in[0], v_w_in[0])]

    def row8(a):
        return jnp.pad(a, ((0, 7), (0, 0)))

    d_mix_g, _ = lax.optimization_barrier((d_mix_g, landed_i))
    (mix_all,) = _all_gather_async("mix_gain_grad_all_gather", 8, [row8(d_mix_g)], [((N_DEV, 8, D_MODEL), F32)],
                                   [lambda ref, j: ref.at[j]])
    mix_outs = _adamw_row("adamw_mix_gain", mix_all, norm_mix_g, m_norm_mix_g, v_norm_mix_g)
    small = [dict(p, norm_final_g=p["norm_final_g"][0], gmlp_w_s=ws.reshape(1, GROUPS, GMLP_CHUNK, GMLP_CHUNK), norm_mix_g=q)
             for p, ws, q in zip(small, ws_outs, mix_outs)]

    loss = loss_row[0, 0]
    order = ["norm_mix_g", "w_in", "gmlp_ln_g", "gmlp_ln_b", "gmlp_w_s", "gmlp_b_s", "hgrn_lb_table", "hgrn_norm_g",
             "w_branch_a", "w_branch_b", "w_out", "norm_ffn_g", "w_gate_up", "w_down", "norm_final_g"]
    outs = [loss, grad_x.reshape(1, t, D_MODEL)]
    for kind in range(4):
        for nme in order:
            outs.append(big[nme][kind] if nme in big else small[kind][nme])
    return tuple(outs)
```
